```python
import jax, jax.numpy as jnp
from jax import lax
import numpy as np

D_MODEL = 1024
BATCH = 8
SEQ = 8192
DEPTH = 1

N_HEADS_A = 8
N_KV_A = 2
HEAD_DIM_A = 64
WINDOW = 128
BLOCK = 128
N_HEADS_B = 8
QK_NOPE = 64
QK_ROPE = 32
V_DIM_B = 64
Q_LORA = 256
KV_LORA = 128
ROPE_THETA = 10000.0
D_FF = 4 * D_MODEL
EPS = 1e-6

WIDTH_A = N_HEADS_A * HEAD_DIM_A
WIDTH_B = N_HEADS_B * V_DIM_B
KV_WIDTH_A = N_KV_A * HEAD_DIM_A
Q_HEAD_B = QK_NOPE + QK_ROPE
KV_HEAD_B = QK_NOPE + V_DIM_B
SPLITS = (D_MODEL, D_MODEL, WIDTH_A, KV_WIDTH_A, KV_WIDTH_A, Q_LORA, KV_LORA, QK_ROPE)
D_IN = int(sum(SPLITS))
SPLIT_IDX = tuple(int(i) for i in np.cumsum(SPLITS)[:-1])

kernel_name = "hybrid_swa_sink_alibi_mla_gated_sqrelu"


def rmsnorm(x, g):
    x32 = x.astype(jnp.float32)
    y = x32 * lax.rsqrt(jnp.mean(x32 * x32, axis=-1, keepdims=True) + EPS)
    return y.astype(x.dtype) * g


def alibi_slopes(n):
    return 2.0 ** (-8.0 * jnp.arange(1, n + 1, dtype=jnp.float32) / n)


def rope(x, pos):
    d = x.shape[-1]
    freqs = ROPE_THETA ** (-jnp.arange(0, d, 2, dtype=jnp.float32) / d)
    ang = pos.astype(jnp.float32)[..., None] * freqs
    cos, sin = jnp.cos(ang)[:, :, None, :], jnp.sin(ang)[:, :, None, :]
    x32 = x.astype(jnp.float32)
    x1, x2 = x32[..., : d // 2], x32[..., d // 2:]
    return jnp.concatenate([x1 * cos - x2 * sin, x2 * cos + x1 * sin], axis=-1).astype(x.dtype)


def swa_sink_alibi_attention(q, k, v, pos, sinks):
    B, S = q.shape[0], q.shape[1]
    nb = S // BLOCK
    G = N_HEADS_A // N_KV_A
    qb = q.reshape(B, nb, BLOCK, N_KV_A, G, HEAD_DIM_A)

    def band(t):
        padded = jnp.pad(t, [(0, 0), (BLOCK, 0)] + [(0, 0)] * (t.ndim - 2))
        prev = padded[:, :S].reshape((B, nb, BLOCK) + t.shape[2:])
        cur = t.reshape((B, nb, BLOCK) + t.shape[2:])
        return jnp.concatenate([prev, cur], axis=2)

    kb, vb, pb = band(k), band(v), band(pos)
    qpos = pos.reshape(B, nb, BLOCK)
    scale = HEAD_DIM_A ** -0.5
    s = jnp.einsum('bnqkgd,bnskd->bnkgqs', qb, kb).astype(jnp.float32) * scale
    dist = jnp.abs(qpos[:, :, :, None] - pb[:, :, None, :]).astype(jnp.float32)
    slopes = alibi_slopes(N_HEADS_A).reshape(N_KV_A, G)
    s = s - slopes[None, None, :, :, None, None] * dist[:, :, None, None]
    qi = jnp.arange(BLOCK)[:, None] + BLOCK
    si = jnp.arange(2 * BLOCK)[None, :]
    diff = qi - si
    valid = (diff >= 0) & (diff < WINDOW)
    not_pad = (jnp.arange(nb)[:, None, None] > 0) | (si[None] >= BLOCK)
    mask = valid[None] & not_pad
    s = jnp.where(mask[None, :, None, None], s, -jnp.inf)
    sink = sinks.astype(jnp.float32).reshape(1, 1, N_KV_A, G, 1, 1)
    m = jnp.maximum(jnp.max(s, axis=-1, keepdims=True), sink)
    e = jnp.exp(s - m)
    p = e / (jnp.sum(e, axis=-1, keepdims=True) + jnp.exp(sink - m))
    o = jnp.einsum('bnkgqs,bnskd->bnqkgd', p.astype(v.dtype), vb)
    return o.reshape(B, S, WIDTH_A)


def mla_attention(q_nope, q_rope, k_nope, k_rope, v):
    B, S = q_nope.shape[0], q_nope.shape[1]
    nb = S // BLOCK
    scale = Q_HEAD_B ** -0.5
    qn = q_nope.reshape(B, nb, BLOCK, N_HEADS_B, QK_NOPE).transpose(1, 0, 2, 3, 4)
    qr = q_rope.reshape(B, nb, BLOCK, N_HEADS_B, QK_ROPE).transpose(1, 0, 2, 3, 4)
    kidx = jnp.arange(S)

    def one_block(args):
        qn_b, qr_b, i = args
        s = (jnp.einsum('bqhd,bshd->bhqs', qn_b, k_nope)
             + jnp.einsum('bqhd,bsd->bhqs', qr_b, k_rope)).astype(jnp.float32) * scale
        qidx = i * BLOCK + jnp.arange(BLOCK)
        s = jnp.where(kidx[None, :] <= qidx[:, None], s, -jnp.inf)
        p = jax.nn.softmax(s, axis=-1)
        return jnp.einsum('bhqs,bshd->bqhd', p.astype(v.dtype), v)

    o = lax.map(one_block, (qn, qr, jnp.arange(nb)))
    return o.transpose(1, 0, 2, 3, 4).reshape(B, S, WIDTH_B)


def _fwd_setup_inputs(seed: int = 0) -> dict:
    key = jax.random.key(seed)
    ks = jax.random.split(key, 20)

    def w(k, shape, fan_in):
        return jax.random.normal(k, shape, jnp.float32) * fan_in ** -0.5

    def gain(k, n):
        return 1.0 + 0.02 * jax.random.normal(k, (DEPTH, n), jnp.float32)

    x = jax.random.normal(ks[0], (BATCH, SEQ, D_MODEL), jnp.float32)
    offset = jax.random.randint(ks[1], (BATCH, 1), 0, 1024, dtype=jnp.int32)
    positions = (offset + jnp.arange(SEQ, dtype=jnp.int32)[None, :]).astype(jnp.int32)
    return {
        "x": x,
        "positions": positions,
        "pre_norm_mix": gain(ks[2], D_MODEL),
        "w_in": w(ks[3], (DEPTH, D_MODEL, D_IN), D_MODEL),
        "q_a_norm": gain(ks[4], Q_LORA),
        "w_q_b": w(ks[5], (DEPTH, Q_LORA, N_HEADS_B * Q_HEAD_B), Q_LORA),
        "kv_a_norm": gain(ks[6], KV_LORA),
        "w_kv_b": w(ks[7], (DEPTH, KV_LORA, N_HEADS_B * KV_HEAD_B), KV_LORA),
        "sinks": jax.random.normal(ks[8], (DEPTH, N_HEADS_A), jnp.float32),
        "w_o_a": w(ks[9], (DEPTH, WIDTH_A, D_MODEL), WIDTH_A),
        "w_o_b": w(ks[10], (DEPTH, WIDTH_B, D_MODEL), WIDTH_B),
        "w_out": w(ks[11], (DEPTH, D_MODEL, D_MODEL), D_MODEL),
        "post_norm_mix": gain(ks[12], D_MODEL),
        "pre_norm_mlp": gain(ks[13], D_MODEL),
        "w_up": w(ks[14], (DEPTH, D_MODEL, D_FF), D_MODEL),
        "w_down": w(ks[15], (DEPTH, D_FF, D_MODEL), D_FF),
        "post_norm_mlp": gain(ks[16], D_MODEL),
    }


def _fwd_reference(x, positions, pre_norm_mix, w_in, q_a_norm, w_q_b, kv_a_norm, w_kv_b, sinks,
              w_o_a, w_o_b, w_out, post_norm_mix, pre_norm_mlp, w_up, w_down, post_norm_mlp):
    B, S = x.shape[0], x.shape[1]
    for l in range(DEPTH):
        h = rmsnorm(x, pre_norm_mix[l])
        proj = h @ w_in[l]
        g_a, g_b, qa, ka, va, cq, ckv, kr = jnp.split(proj, SPLIT_IDX, axis=-1)
        qa = qa.reshape(B, S, N_HEADS_A, HEAD_DIM_A)
        ka = ka.reshape(B, S, N_KV_A, HEAD_DIM_A)
        va = va.reshape(B, S, N_KV_A, HEAD_DIM_A)
        out_a = swa_sink_alibi_attention(qa, ka, va, positions, sinks[l])
        qb = (rmsnorm(cq, q_a_norm[l]) @ w_q_b[l]).reshape(B, S, N_HEADS_B, Q_HEAD_B)
        kvb = (rmsnorm(ckv, kv_a_norm[l]) @ w_kv_b[l]).reshape(B, S, N_HEADS_B, KV_HEAD_B)
        q_nope, q_rope = qb[..., :QK_NOPE], rope(qb[..., QK_NOPE:], positions)
        k_nope, v_b = kvb[..., :QK_NOPE], kvb[..., QK_NOPE:]
        k_rope = rope(kr[:, :, None, :], positions)[:, :, 0, :]
        out_b = mla_attention(q_nope, q_rope, k_nope, k_rope, v_b)
        merged = jax.nn.sigmoid(g_a) * (out_a @ w_o_a[l]) + jax.nn.sigmoid(g_b) * (out_b @ w_o_b[l])
        x = x + rmsnorm(merged @ w_out[l], post_norm_mix[l])
        h2 = rmsnorm(x, pre_norm_mlp[l])
        y = jnp.square(jax.nn.relu(h2 @ w_up[l])) @ w_down[l]
        x = x + rmsnorm(y, post_norm_mlp[l])
    return x


import jax as _jax
import jax.numpy as _jnp

TWIN_FORMAT = 'train_step'
FWD_PARAMS = ['x', 'positions', 'pre_norm_mix', 'w_in', 'q_a_norm', 'w_q_b', 'kv_a_norm', 'w_kv_b', 'sinks', 'w_o_a', 'w_o_b', 'w_out', 'post_norm_mix', 'pre_norm_mlp', 'w_up', 'w_down', 'post_norm_mlp']
TWIN_WEIGHTS = ['pre_norm_mix', 'w_in', 'q_a_norm', 'w_q_b', 'kv_a_norm', 'w_kv_b', 'sinks', 'w_o_a', 'w_o_b', 'w_out', 'post_norm_mix', 'pre_norm_mlp', 'w_up', 'w_down', 'post_norm_mlp']
TWIN_DIFF_INPUT = 'x'
TWIN_INPUTS = ['x', 'positions', 'pre_norm_mix', 'w_in', 'q_a_norm', 'w_q_b', 'kv_a_norm', 'w_kv_b', 'sinks', 'w_o_a', 'w_o_b', 'w_out', 'post_norm_mix', 'pre_norm_mlp', 'w_up', 'w_down', 'post_norm_mlp', 'loss_target', 'm_pre_norm_mix', 'm_w_in', 'm_q_a_norm', 'm_w_q_b', 'm_kv_a_norm', 'm_w_kv_b', 'm_sinks', 'm_w_o_a', 'm_w_o_b', 'm_w_out', 'm_post_norm_mix', 'm_pre_norm_mlp', 'm_w_up', 'm_w_down', 'm_post_norm_mlp', 'v_pre_norm_mix', 'v_w_in', 'v_q_a_norm', 'v_w_q_b', 'v_kv_a_norm', 'v_w_kv_b', 'v_sinks', 'v_w_o_a', 'v_w_o_b', 'v_w_out', 'v_post_norm_mix', 'v_pre_norm_mlp', 'v_w_up', 'v_w_down', 'v_post_norm_mlp']
TWIN_OUTPUTS = ['loss', 'grad_x', 'grad_pre_norm_mix', 'grad_w_in', 'grad_q_a_norm', 'grad_w_q_b', 'grad_kv_a_norm', 'grad_w_kv_b', 'grad_sinks', 'grad_w_o_a', 'grad_w_o_b', 'grad_w_out', 'grad_post_norm_mix', 'grad_pre_norm_mlp', 'grad_w_up', 'grad_w_down', 'grad_post_norm_mlp', 'delta_pre_norm_mix', 'delta_w_in', 'delta_q_a_norm', 'delta_w_q_b', 'delta_kv_a_norm', 'delta_w_kv_b', 'delta_sinks', 'delta_w_o_a', 'delta_w_o_b', 'delta_w_out', 'delta_post_norm_mix', 'delta_pre_norm_mlp', 'delta_w_up', 'delta_w_down', 'delta_post_norm_mlp', 'new_m_pre_norm_mix', 'new_m_w_in', 'new_m_q_a_norm', 'new_m_w_q_b', 'new_m_kv_a_norm', 'new_m_w_kv_b', 'new_m_sinks', 'new_m_w_o_a', 'new_m_w_o_b', 'new_m_w_out', 'new_m_post_norm_mix', 'new_m_pre_norm_mlp', 'new_m_w_up', 'new_m_w_down', 'new_m_post_norm_mlp', 'new_v_pre_norm_mix', 'new_v_w_in', 'new_v_q_a_norm', 'new_v_w_q_b', 'new_v_kv_a_norm', 'new_v_w_kv_b', 'new_v_sinks', 'new_v_w_o_a', 'new_v_w_o_b', 'new_v_w_out', 'new_v_post_norm_mix', 'new_v_pre_norm_mlp', 'new_v_w_up', 'new_v_w_down', 'new_v_post_norm_mlp']
TWIN_LEAF_KINDS = {'loss': 'loss', 'grad_x': 'grad_x', 'grad_pre_norm_mix': 'grad_w', 'grad_w_in': 'grad_w', 'grad_q_a_norm': 'grad_w', 'grad_w_q_b': 'grad_w', 'grad_kv_a_norm': 'grad_w', 'grad_w_kv_b': 'grad_w', 'grad_sinks': 'grad_w', 'grad_w_o_a': 'grad_w', 'grad_w_o_b': 'grad_w', 'grad_w_out': 'grad_w', 'grad_post_norm_mix': 'grad_w', 'grad_pre_norm_mlp': 'grad_w', 'grad_w_up': 'grad_w', 'grad_w_down': 'grad_w', 'grad_post_norm_mlp': 'grad_w', 'delta_pre_norm_mix': 'delta_w', 'delta_w_in': 'delta_w', 'delta_q_a_norm': 'delta_w', 'delta_w_q_b': 'delta_w', 'delta_kv_a_norm': 'delta_w', 'delta_w_kv_b': 'delta_w', 'delta_sinks': 'delta_w', 'delta_w_o_a': 'delta_w', 'delta_w_o_b': 'delta_w', 'delta_w_out': 'delta_w', 'delta_post_norm_mix': 'delta_w', 'delta_pre_norm_mlp': 'delta_w', 'delta_w_up': 'delta_w', 'delta_w_down': 'delta_w', 'delta_post_norm_mlp': 'delta_w', 'new_m_pre_norm_mix': 'new_m', 'new_m_w_in': 'new_m', 'new_m_q_a_norm': 'new_m', 'new_m_w_q_b': 'new_m', 'new_m_kv_a_norm': 'new_m', 'new_m_w_kv_b': 'new_m', 'new_m_sinks': 'new_m', 'new_m_w_o_a': 'new_m', 'new_m_w_o_b': 'new_m', 'new_m_w_out': 'new_m', 'new_m_post_norm_mix': 'new_m', 'new_m_pre_norm_mlp': 'new_m', 'new_m_w_up': 'new_m', 'new_m_w_down': 'new_m', 'new_m_post_norm_mlp': 'new_m', 'new_v_pre_norm_mix': 'new_v', 'new_v_w_in': 'new_v', 'new_v_q_a_norm': 'new_v', 'new_v_w_q_b': 'new_v', 'new_v_kv_a_norm': 'new_v', 'new_v_w_kv_b': 'new_v', 'new_v_sinks': 'new_v', 'new_v_w_o_a': 'new_v', 'new_v_w_o_b': 'new_v', 'new_v_w_out': 'new_v', 'new_v_post_norm_mix': 'new_v', 'new_v_pre_norm_mlp': 'new_v', 'new_v_w_up': 'new_v', 'new_v_w_down': 'new_v', 'new_v_post_norm_mlp': 'new_v'}


def _forward(args):
    return _fwd_reference(*[args[k] for k in FWD_PARAMS])


def _output_shape():
    def fwd():
        inp = _fwd_setup_inputs(0)
        return _fwd_reference(*[inp[k] for k in FWD_PARAMS])
    out = _jax.eval_shape(fwd)
    return out.shape, out.dtype

N_MICROBATCH = 1
ADAM_LR = 0.001
ADAM_B1 = 0.9
ADAM_B2 = 0.999
ADAM_EPS = 1e-08
ADAM_WD = 0.01
ADAM_STEP = 10
PER_EXAMPLE_BATCH_AXIS = {'x': 0, 'positions': 0, 'loss_target': 0}
SHARED_INPUTS = []
_WEIGHT_DTYPES = {'pre_norm_mix': _jnp.float32, 'w_in': _jnp.float32, 'q_a_norm': _jnp.float32, 'w_q_b': _jnp.float32, 'kv_a_norm': _jnp.float32, 'w_kv_b': _jnp.float32, 'sinks': _jnp.float32, 'w_o_a': _jnp.float32, 'w_o_b': _jnp.float32, 'w_out': _jnp.float32, 'post_norm_mix': _jnp.float32, 'pre_norm_mlp': _jnp.float32, 'w_up': _jnp.float32, 'w_down': _jnp.float32, 'post_norm_mlp': _jnp.float32}
MOMENT_SCALE = {'pre_norm_mix': 1.295658e+00, 'w_in': 7.099084e-01, 'q_a_norm': 6.143588e-01, 'w_q_b': 3.471636e-01, 'kv_a_norm': 2.348869e+00, 'w_kv_b': 7.251259e-01, 'sinks': 2.156003e+00, 'w_o_a': 2.090301e+00, 'w_o_b': 6.588847e-01, 'w_out': 2.307745e+00, 'post_norm_mix': 6.415041e+01, 'pre_norm_mlp': 2.177658e+00, 'w_up': 1.063655e+00, 'w_down': 2.222215e+00, 'post_norm_mlp': 6.562936e+01}


def _to_microbatches(a, axis):
    t = _jnp.moveaxis(a, axis, 0)
    t = t.reshape((N_MICROBATCH, t.shape[0] // N_MICROBATCH) + t.shape[1:])
    return _jnp.moveaxis(t, 1, axis + 1)


def setup_inputs(seed: int = 0) -> dict:
    inp = _fwd_setup_inputs(seed)
    key = _jax.random.fold_in(_jax.random.key(seed), 7919)
    shape, _ = _output_shape()
    out = dict(inp)
    out["loss_target"] = _jax.random.normal(_jax.random.fold_in(key, 0), shape, _jnp.float32)
    for i, name in enumerate(TWIN_WEIGHTS):
        w = inp[name].astype(_jnp.float32)
        if MOMENT_SCALE is None:
            s = _jnp.sqrt(_jnp.mean(_jnp.square(w)) + 1e-30)
        else:
            s = MOMENT_SCALE[name]
        km, kv = _jax.random.split(_jax.random.fold_in(key, i + 1))
        out[name] = w
        out["m_" + name] = s * _jax.random.normal(km, w.shape, _jnp.float32)
        out["v_" + name] = (s * s) * _jax.random.uniform(kv, w.shape, _jnp.float32, 0.5, 1.5)
    if N_MICROBATCH > 1:
        for name, axis in PER_EXAMPLE_BATCH_AXIS.items():
            out[name] = _to_microbatches(out[name], axis)
    return {'x': out['x'], 'positions': out['positions'], 'pre_norm_mix': out['pre_norm_mix'], 'w_in': out['w_in'], 'q_a_norm': out['q_a_norm'], 'w_q_b': out['w_q_b'], 'kv_a_norm': out['kv_a_norm'], 'w_kv_b': out['w_kv_b'], 'sinks': out['sinks'], 'w_o_a': out['w_o_a'], 'w_o_b': out['w_o_b'], 'w_out': out['w_out'], 'post_norm_mix': out['post_norm_mix'], 'pre_norm_mlp': out['pre_norm_mlp'], 'w_up': out['w_up'], 'w_down': out['w_down'], 'post_norm_mlp': out['post_norm_mlp'], 'loss_target': out['loss_target'], 'm_pre_norm_mix': out['m_pre_norm_mix'], 'm_w_in': out['m_w_in'], 'm_q_a_norm': out['m_q_a_norm'], 'm_w_q_b': out['m_w_q_b'], 'm_kv_a_norm': out['m_kv_a_norm'], 'm_w_kv_b': out['m_w_kv_b'], 'm_sinks': out['m_sinks'], 'm_w_o_a': out['m_w_o_a'], 'm_w_o_b': out['m_w_o_b'], 'm_w_out': out['m_w_out'], 'm_post_norm_mix': out['m_post_norm_mix'], 'm_pre_norm_mlp': out['m_pre_norm_mlp'], 'm_w_up': out['m_w_up'], 'm_w_down': out['m_w_down'], 'm_post_norm_mlp': out['m_post_norm_mlp'], 'v_pre_norm_mix': out['v_pre_norm_mix'], 'v_w_in': out['v_w_in'], 'v_q_a_norm': out['v_q_a_norm'], 'v_w_q_b': out['v_w_q_b'], 'v_kv_a_norm': out['v_kv_a_norm'], 'v_w_kv_b': out['v_w_kv_b'], 'v_sinks': out['v_sinks'], 'v_w_o_a': out['v_w_o_a'], 'v_w_o_b': out['v_w_o_b'], 'v_w_out': out['v_w_out'], 'v_post_norm_mix': out['v_post_norm_mix'], 'v_pre_norm_mlp': out['v_pre_norm_mlp'], 'v_w_up': out['v_w_up'], 'v_w_down': out['v_w_down'], 'v_post_norm_mlp': out['v_post_norm_mlp']}


def _loss(weights, diff, rest, loss_target):
    with _jax.named_scope("forward"):
        args = {**rest, TWIN_DIFF_INPUT: diff, **{k: w.astype(_WEIGHT_DTYPES[k]) for k, w in weights.items()}}
        y = _forward(args)
    with _jax.named_scope("loss_head"):
        err = _jnp.square(y.astype(_jnp.float32) - loss_target)
        return 0.5 * _jnp.sum(_jnp.mean(err, axis=-1)) if err.ndim else 0.5 * err


def _adamw(w, g, m, v):
    m = ADAM_B1 * m + (1.0 - ADAM_B1) * g
    v = ADAM_B2 * v + (1.0 - ADAM_B2) * _jnp.square(g)
    m_hat = m / (1.0 - ADAM_B1 ** ADAM_STEP)
    v_hat = v / (1.0 - ADAM_B2 ** ADAM_STEP)
    delta = -ADAM_LR * (m_hat / (_jnp.sqrt(v_hat) + ADAM_EPS) + ADAM_WD * w)
    return delta, m, v


def reference(x, positions, pre_norm_mix, w_in, q_a_norm, w_q_b, kv_a_norm, w_kv_b, sinks, w_o_a, w_o_b, w_out, post_norm_mix, pre_norm_mlp, w_up, w_down, post_norm_mlp, loss_target, m_pre_norm_mix, m_w_in, m_q_a_norm, m_w_q_b, m_kv_a_norm, m_w_kv_b, m_sinks, m_w_o_a, m_w_o_b, m_w_out, m_post_norm_mix, m_pre_norm_mlp, m_w_up, m_w_down, m_post_norm_mlp, v_pre_norm_mix, v_w_in, v_q_a_norm, v_w_q_b, v_kv_a_norm, v_w_kv_b, v_sinks, v_w_o_a, v_w_o_b, v_w_out, v_post_norm_mix, v_pre_norm_mlp, v_w_up, v_w_down, v_post_norm_mlp):
    given = dict(x=x, positions=positions, pre_norm_mix=pre_norm_mix, w_in=w_in, q_a_norm=q_a_norm, w_q_b=w_q_b, kv_a_norm=kv_a_norm, w_kv_b=w_kv_b, sinks=sinks, w_o_a=w_o_a, w_o_b=w_o_b, w_out=w_out, post_norm_mix=post_norm_mix, pre_norm_mlp=pre_norm_mlp, w_up=w_up, w_down=w_down, post_norm_mlp=post_norm_mlp, loss_target=loss_target, m_pre_norm_mix=m_pre_norm_mix, m_w_in=m_w_in, m_q_a_norm=m_q_a_norm, m_w_q_b=m_w_q_b, m_kv_a_norm=m_kv_a_norm, m_w_kv_b=m_w_kv_b, m_sinks=m_sinks, m_w_o_a=m_w_o_a, m_w_o_b=m_w_o_b, m_w_out=m_w_out, m_post_norm_mix=m_post_norm_mix, m_pre_norm_mlp=m_pre_norm_mlp, m_w_up=m_w_up, m_w_down=m_w_down, m_post_norm_mlp=m_post_norm_mlp, v_pre_norm_mix=v_pre_norm_mix, v_w_in=v_w_in, v_q_a_norm=v_q_a_norm, v_w_q_b=v_w_q_b, v_kv_a_norm=v_kv_a_norm, v_w_kv_b=v_w_kv_b, v_sinks=v_sinks, v_w_o_a=v_w_o_a, v_w_o_b=v_w_o_b, v_w_out=v_w_out, v_post_norm_mix=v_post_norm_mix, v_pre_norm_mlp=v_pre_norm_mlp, v_w_up=v_w_up, v_w_down=v_w_down, v_post_norm_mlp=v_post_norm_mlp)
    weights = {n: given[n] for n in TWIN_WEIGHTS}
    shared = {n: given[n] for n in SHARED_INPUTS}
    per_example = {n: given[n] for n in ['x', 'positions']}
    grad_fn = _jax.value_and_grad(_loss, argnums=(0, 1))

    def one_microbatch(ex, loss_target):
        ex = dict(ex)
        diff = ex.pop(TWIN_DIFF_INPUT)
        return grad_fn(weights, diff, {**shared, **ex}, loss_target)

    if N_MICROBATCH == 1:
        loss, (grad_w, grad_x) = one_microbatch(per_example, given["loss_target"])
    else:
        def body(carry, xs):
            loss_sum, grad_sum = carry
            l_k, (gw_k, gx_k) = one_microbatch(xs[0], xs[1])
            with _jax.named_scope("update"):
                return (loss_sum + l_k, _jax.tree.map(_jnp.add, grad_sum, gw_k)), gx_k

        init = (_jnp.zeros((), _jnp.float32), _jax.tree.map(_jnp.zeros_like, weights))
        (loss, grad_w), grad_x = _jax.lax.scan(body, init, (per_example, given["loss_target"]))
    with _jax.named_scope("update"):
        delta_w, new_m, new_v = {}, {}, {}
        for n in TWIN_WEIGHTS:
            delta_w[n], new_m[n], new_v[n] = _adamw(weights[n], grad_w[n], given["m_" + n], given["v_" + n])
    return (loss, grad_x, *[grad_w[n] for n in TWIN_WEIGHTS], *[delta_w[n] for n in TWIN_WEIGHTS],
            *[new_m[n] for n in TWIN_WEIGHTS], *[new_v[n] for n in TWIN_WEIGHTS])
```

```python
import functools

import numpy as np
import jax
import jax.numpy as jnp
from jax import lax
from jax.experimental import pallas as pl
from jax.experimental.pallas import tpu as pltpu

F32 = jnp.float32
BF16 = jnp.bfloat16

D_MODEL = 1024
D_FF = 4096
N_HEADS = 8
N_KV_A = 2
GROUP_A = N_HEADS // N_KV_A
HEAD_A = 64
QK_NOPE = 64
QK_ROPE = 32
V_DIM_B = 64
Q_LORA = 256
KV_LORA = 128
BLOCK = 128
SLAB = 128
ROPE_THETA = 10000.0
EPS = 1e-6
N_DEV = 8
NEG = -1e30

SCALE_A = HEAD_A ** -0.5
SCALE_B = (QK_NOPE + QK_ROPE) ** -0.5
SLOPES_A = tuple(2.0 ** (-8.0 * (h + 1) / N_HEADS) for h in range(N_HEADS))

ADAM_LR = 0.001
ADAM_B1 = 0.9
ADAM_B2 = 0.999
ADAM_EPS = 1e-08
ADAM_WD = 0.01
ADAM_STEP = 10

HM = N_HEADS * SLAB
C_GATES = 0
C_QA = 2 * D_MODEL
C_KA = C_QA + HM
C_VA = C_KA + N_KV_A * SLAB
C_CQ = C_VA + N_KV_A * SLAB
C_CKV = C_CQ + Q_LORA
C_KR = C_CKV + KV_LORA
D_IN_PAD = C_KR + SLAB

VMEM_LIMIT = 56 * 1024 * 1024

FLAT_ROWS = (("w_in", 404), ("w_q_b", 24), ("w_kv_b", 16), ("w_o_a", 64), ("w_o_b", 64),
             ("w_out", 128), ("w_up", 512), ("w_down", 512))
FLAT_USED = sum(r for _, r in FLAT_ROWS)
FLAT_R = 1728
SMALL_ROWS = 8


def _token_tile(t):
    return min(256, t)


def _attn_tile(t):
    return 512 if t >= 2048 else 128


def _params(sem, vmem=VMEM_LIMIT):
    return pltpu.CompilerParams(dimension_semantics=sem, vmem_limit_bytes=vmem)


def _dot(a, b):
    return jnp.dot(a, b, preferred_element_type=F32)


def _dot_nt(a, b):
    return lax.dot_general(a, b, (((1,), (1,)), ((), ())), preferred_element_type=F32)


def _dot_tn(a, b):
    return lax.dot_general(a, b, (((0,), (0,)), ((), ())), preferred_element_type=F32)


def _rms_r(x):
    return lax.rsqrt(jnp.mean(x * x, axis=-1, keepdims=True) + EPS)


def _rms_bwd(x, r, g, dy):
    t = dy * g
    return r * t - x * (r * r * r) * jnp.mean(x * t, axis=-1, keepdims=True)


def _sigmoid(x):
    return 1.0 / (1.0 + jnp.exp(-x))


def _rope(x, c, s1, s2):
    return x * c + pltpu.roll(x, SLAB - 16, 1) * s1 + pltpu.roll(x, 16, 1) * s2


def _rope_bwd(d, c, s1, s2):
    return d * c + pltpu.roll(d * s1, 16, 1) + pltpu.roll(d * s2, SLAB - 16, 1)


def _row_spec(tm, n):
    return pl.BlockSpec((tm, n), lambda i: (i, 0))


def _full_spec(shape):
    nd = len(shape)
    return pl.BlockSpec(shape, lambda i: (0,) * nd, pipeline_mode=pl.Buffered(1))


def _acc_rows(ref, val):
    @pl.when(pl.program_id(0) == 0)
    def _():
        ref[...] = jnp.zeros_like(ref)
    ref[...] += jnp.sum(val, axis=0, keepdims=True)


def _rope_tables(pos_col, freq_row):
    t = pos_col.shape[0]
    tm = _token_tile(t)

    def body(pos_ref, f_ref, c_ref, s1_ref, s2_ref):
        ang = pos_ref[...].astype(F32) * f_ref[...]
        lane = lax.broadcasted_iota(jnp.int32, ang.shape, 1)
        s = jnp.sin(ang)
        c_ref[...] = jnp.cos(ang)
        s1_ref[...] = jnp.where((lane >= 64) & (lane < 80), -s, 0.0)
        s2_ref[...] = jnp.where((lane >= 80) & (lane < 96), s, 0.0)

    tab = jax.ShapeDtypeStruct((t, SLAB), F32)
    return pl.pallas_call(
        body, name="rope_tables", grid=(t // tm,),
        in_specs=[_row_spec(tm, 1), _full_spec((1, SLAB))],
        out_specs=[_row_spec(tm, SLAB)] * 3, out_shape=[tab] * 3,
        compiler_params=_params(("parallel",)),
    )(pos_col, freq_row)


def _inproj_fwd(x, g1, w_in, g_q, w_qb, g_kv, w_kvb, rope_c, rope_s1, rope_s2):
    t = x.shape[0]
    tm = _token_tile(t)

    def body(x_ref, g1_ref, win_ref, gq_ref, wqb_ref, gkv_ref, wkvb_ref, c_ref, s1_ref, s2_ref,
             h_ref, gates_ref, qa_ref, ka_ref, va_ref, cq_ref, ckv_ref, cqn_ref, ckvn_ref,
             qb_ref, kb_ref, vb_ref):
        xv = x_ref[...]
        h = (xv * _rms_r(xv) * g1_ref[...]).astype(BF16)
        h_ref[...] = h
        proj = _dot(h, win_ref[...])
        gates_ref[...] = proj[:, C_GATES:C_QA]
        qa_ref[...] = proj[:, C_QA:C_KA].astype(BF16)
        ka_ref[...] = proj[:, C_KA:C_VA].astype(BF16)
        va_ref[...] = proj[:, C_VA:C_CQ].astype(BF16)
        cq = proj[:, C_CQ:C_CKV]
        ckv = proj[:, C_CKV:C_KR]
        kr = proj[:, C_KR:D_IN_PAD]
        cq_ref[...] = cq
        ckv_ref[...] = ckv
        cqn = (cq * _rms_r(cq) * gq_ref[...]).astype(BF16)
        ckvn = (ckv * _rms_r(ckv) * gkv_ref[...]).astype(BF16)
        cqn_ref[...] = cqn
        ckvn_ref[...] = ckvn
        c, s1, s2 = c_ref[...], s1_ref[...], s2_ref[...]
        qb = _dot(cqn, wqb_ref[...])
        kvb = _dot(ckvn, wkvb_ref[...])
        kr_rot = _rope(kr, c, s1, s2)
        for hd in range(N_HEADS):
            sl = slice(hd * SLAB, (hd + 1) * SLAB)
            qb_ref[:, sl] = _rope(qb[:, sl], c, s1, s2).astype(BF16)
            kb_ref[:, sl] = (kvb[:, sl] + kr_rot).astype(BF16)
        vb_ref[...] = kvb[:, HM:2 * HM].astype(BF16)

    def sds(n, dt):
        return jax.ShapeDtypeStruct((t, n), dt)

    outs = [(D_MODEL, BF16), (2 * D_MODEL, F32), (HM, BF16), (N_KV_A * SLAB, BF16), (N_KV_A * SLAB, BF16),
            (Q_LORA, F32), (KV_LORA, F32), (Q_LORA, BF16), (KV_LORA, BF16), (HM, BF16), (HM, BF16), (HM, BF16)]
    return pl.pallas_call(
        body, name="inproj_fwd", grid=(t // tm,),
        in_specs=[_row_spec(tm, D_MODEL), _full_spec((1, D_MODEL)), _full_spec((D_MODEL, D_IN_PAD)),
                  _full_spec((1, Q_LORA)), _full_spec((Q_LORA, HM)), _full_spec((1, KV_LORA)),
                  _full_spec((KV_LORA, 2 * HM)), _row_spec(tm, SLAB), _row_spec(tm, SLAB), _row_spec(tm, SLAB)],
        out_specs=[_row_spec(tm, n) for n, _ in outs],
        out_shape=[sds(n, dt) for n, dt in outs],
        compiler_params=_params(("parallel",)),
    )(x, g1, w_in, g_q, w_qb, g_kv, w_kvb, rope_c, rope_s1, rope_s2)


def _swa_masks():
    row = lax.broadcasted_iota(jnp.int32, (BLOCK, BLOCK), 0)
    col = lax.broadcasted_iota(jnp.int32, (BLOCK, BLOCK), 1)
    return row >= col, col > row


def _swa_fwd(qa, ka, va, pos_col, pos_row, sinks):
    t = qa.shape[0]
    nb = t // BLOCK

    def body(sinks_ref, q_ref, kc_ref, kp_ref, vc_ref, vp_ref, pq_ref, pkc_ref, pkp_ref, o_ref, l_ref):
        i = pl.program_id(0)
        pq = pq_ref[...]
        dist_c = jnp.abs(pq - pkc_ref[...]).astype(F32)
        dist_p = jnp.abs(pq - pkp_ref[...]).astype(F32)
        mask_c, upper = _swa_masks()
        mask_p = jnp.logical_and(upper, i > 0)
        for hd in range(N_HEADS):
            g = hd // GROUP_A
            q = q_ref[:, hd * SLAB:(hd + 1) * SLAB]
            kc = kc_ref[:, g * SLAB:(g + 1) * SLAB]
            kp = kp_ref[:, g * SLAB:(g + 1) * SLAB]
            vc = vc_ref[:, g * SLAB:(g + 1) * SLAB]
            vp = vp_ref[:, g * SLAB:(g + 1) * SLAB]
            sink = sinks_ref[hd]
            s_c = jnp.where(mask_c, _dot_nt(q, kc) * SCALE_A - SLOPES_A[hd] * dist_c, NEG)
            s_p = jnp.where(mask_p, _dot_nt(q, kp) * SCALE_A - SLOPES_A[hd] * dist_p, NEG)
            m = jnp.maximum(jnp.maximum(jnp.max(s_c, axis=-1, keepdims=True),
                                        jnp.max(s_p, axis=-1, keepdims=True)), sink)
            e_c = jnp.exp(s_c - m)
            e_p = jnp.exp(s_p - m)
            den = (jnp.sum(e_c, axis=-1, keepdims=True) + jnp.sum(e_p, axis=-1, keepdims=True)
                   + jnp.exp(sink - m))
            inv = 1.0 / den
            o = _dot((e_c * inv).astype(BF16), vc) + _dot((e_p * inv).astype(BF16), vp)
            o_ref[:, hd * SLAB:(hd + 1) * SLAB] = o.astype(BF16)
            l_ref[hd] = m + jnp.log(den)

    cur = lambda i: (i, 0)
    prev = lambda i: (jnp.maximum(i - 1, 0), 0)
    kvw = N_KV_A * SLAB
    return pl.pallas_call(
        body, name="swa_fwd", grid=(nb,),
        in_specs=[pl.BlockSpec(memory_space=pltpu.SMEM),
                  pl.BlockSpec((BLOCK, HM), cur),
                  pl.BlockSpec((BLOCK, kvw), cur), pl.BlockSpec((BLOCK, kvw), prev),
                  pl.BlockSpec((BLOCK, kvw), cur), pl.BlockSpec((BLOCK, kvw), prev),
                  pl.BlockSpec((BLOCK, 1), cur),
                  pl.BlockSpec((1, BLOCK), lambda i: (0, i)),
                  pl.BlockSpec((1, BLOCK), lambda i: (0, jnp.maximum(i - 1, 0)))],
        out_specs=[pl.BlockSpec((BLOCK, HM), cur), pl.BlockSpec((N_HEADS, BLOCK, 1), lambda i: (0, i, 0))],
        out_shape=[jax.ShapeDtypeStruct((t, HM), BF16), jax.ShapeDtypeStruct((N_HEADS, t, 1), F32)],
        compiler_params=_params(("parallel",)),
    )(sinks, qa, ka, ka, va, va, pos_col, pos_row, pos_row)


def _swa_bwd(qa, ka, va, d_oa, lse, delta, pos_col, pos_row, sinks):
    t = qa.shape[0]
    nb = t // BLOCK

    def body(sinks_ref, q_ref, qn_ref, do_ref, don_ref, l_ref, ln_ref, dl_ref, dln_ref,
             kp_ref, kc_ref, vp_ref, vc_ref, pq_ref, pqn_ref, pkp_ref, pkc_ref,
             dq_ref, dk_ref, dv_ref, dsink_ref):
        j = pl.program_id(0)
        pq, pqn = pq_ref[...], pqn_ref[...]
        pkp, pkc = pkp_ref[...], pkc_ref[...]
        dist_cc = jnp.abs(pq - pkc).astype(F32)
        dist_cp = jnp.abs(pq - pkp).astype(F32)
        dist_nc = jnp.abs(pqn - pkc).astype(F32)
        mask_cc, upper = _swa_masks()
        mask_cp = jnp.logical_and(upper, j > 0)
        mask_nc = jnp.logical_and(upper, j < nb - 1)

        @pl.when(j == 0)
        def _():
            dsink_ref[...] = jnp.zeros_like(dsink_ref)

        def pair(q, do, lq, dl, k, v, dist, mask, slope):
            s = jnp.where(mask, _dot_nt(q, k) * SCALE_A - slope * dist, NEG)
            p = jnp.exp(s - lq)
            ds = p * (_dot_nt(do, v) - dl)
            return p.astype(BF16), ds.astype(BF16)

        for g in range(N_KV_A):
            kc = kc_ref[:, g * SLAB:(g + 1) * SLAB]
            kp = kp_ref[:, g * SLAB:(g + 1) * SLAB]
            vc = vc_ref[:, g * SLAB:(g + 1) * SLAB]
            vp = vp_ref[:, g * SLAB:(g + 1) * SLAB]
            dk_acc = jnp.zeros((BLOCK, SLAB), F32)
            dv_acc = jnp.zeros((BLOCK, SLAB), F32)
            for hh in range(GROUP_A):
                hd = g * GROUP_A + hh
                sl = slice(hd * SLAB, (hd + 1) * SLAB)
                slope = SLOPES_A[hd]
                q, do, lq, dl = q_ref[:, sl], do_ref[:, sl], l_ref[hd], dl_ref[hd]
                qn, don, lqn, dln = qn_ref[:, sl], don_ref[:, sl], ln_ref[hd], dln_ref[hd]
                p_cc, ds_cc = pair(q, do, lq, dl, kc, vc, dist_cc, mask_cc, slope)
                _, ds_cp = pair(q, do, lq, dl, kp, vp, dist_cp, mask_cp, slope)
                p_nc, ds_nc = pair(qn, don, lqn, dln, kc, vc, dist_nc, mask_nc, slope)
                dq_ref[:, sl] = ((_dot(ds_cc, kc) + _dot(ds_cp, kp)) * SCALE_A).astype(BF16)
                dk_acc += (_dot_tn(ds_cc, q) + _dot_tn(ds_nc, qn)) * SCALE_A
                dv_acc += _dot_tn(p_cc, do) + _dot_tn(p_nc, don)
                p_sink = jnp.exp(sinks_ref[hd] - lq)
                dsink_ref[hd:hd + 1, :] += jnp.broadcast_to(-jnp.sum(p_sink * dl), (1, SLAB))
            dk_ref[:, g * SLAB:(g + 1) * SLAB] = dk_acc.astype(BF16)
            dv_ref[:, g * SLAB:(g + 1) * SLAB] = dv_acc.astype(BF16)

    cur = lambda j: (j, 0)
    prev = lambda j: (jnp.maximum(j - 1, 0), 0)
    nxt = lambda j: (jnp.minimum(j + 1, nb - 1), 0)
    cur3 = lambda j: (0, j, 0)
    nxt3 = lambda j: (0, jnp.minimum(j + 1, nb - 1), 0)
    kvw = N_KV_A * SLAB
    return pl.pallas_call(
        body, name="swa_bwd", grid=(nb,),
        in_specs=[pl.BlockSpec(memory_space=pltpu.SMEM),
                  pl.BlockSpec((BLOCK, HM), cur), pl.BlockSpec((BLOCK, HM), nxt),
                  pl.BlockSpec((BLOCK, HM), cur), pl.BlockSpec((BLOCK, HM), nxt),
                  pl.BlockSpec((N_HEADS, BLOCK, 1), cur3), pl.BlockSpec((N_HEADS, BLOCK, 1), nxt3),
                  pl.BlockSpec((N_HEADS, BLOCK, 1), cur3), pl.BlockSpec((N_HEADS, BLOCK, 1), nxt3),
                  pl.BlockSpec((BLOCK, kvw), prev), pl.BlockSpec((BLOCK, kvw), cur),
                  pl.BlockSpec((BLOCK, kvw), prev), pl.BlockSpec((BLOCK, kvw), cur),
                  pl.BlockSpec((BLOCK, 1), cur), pl.BlockSpec((BLOCK, 1), nxt),
                  pl.BlockSpec((1, BLOCK), lambda j: (0, jnp.maximum(j - 1, 0))),
                  pl.BlockSpec((1, BLOCK), lambda j: (0, j))],
        out_specs=[pl.BlockSpec((BLOCK, HM), cur), pl.BlockSpec((BLOCK, kvw), cur),
                   pl.BlockSpec((BLOCK, kvw), cur), pl.BlockSpec((N_HEADS, SLAB), lambda j: (0, 0))],
        out_shape=[jax.ShapeDtypeStruct((t, HM), BF16), jax.ShapeDtypeStruct((t, kvw), BF16),
                   jax.ShapeDtypeStruct((t, kvw), BF16), jax.ShapeDtypeStruct((N_HEADS, SLAB), F32)],
        compiler_params=_params(("arbitrary",)),
    )(sinks, qa, qa, d_oa, d_oa, lse, lse, delta, delta, ka, ka, va, va,
      pos_col, pos_col, pos_row, pos_row)


def _causal_mask(i, j, tq):
    row = lax.broadcasted_iota(jnp.int32, (tq, tq), 0) + i * tq
    col = lax.broadcasted_iota(jnp.int32, (tq, tq), 1) + j * tq
    return row >= col


def _mla_fwd(qb, kb, vb):
    t = qb.shape[0]
    tq = _attn_tile(t)
    nt = t // tq
    pairs = [(i, j) for i in range(nt) for j in range(i + 1)]
    i_tab = jnp.asarray(np.array([p[0] for p in pairs], np.int32))
    j_tab = jnp.asarray(np.array([p[1] for p in pairs], np.int32))

    def body(it_ref, jt_ref, q_ref, k_ref, v_ref, o_ref, l_ref, m_s, l_s, acc_s):
        n = pl.program_id(1)
        i, j = it_ref[n], jt_ref[n]

        @pl.when(j == 0)
        def _():
            m_s[...] = jnp.full_like(m_s, NEG)
            l_s[...] = jnp.zeros_like(l_s)
            acc_s[...] = jnp.zeros_like(acc_s)

        s = _dot_nt(q_ref[...], k_ref[...]) * SCALE_B
        s = jnp.where(_causal_mask(i, j, tq), s, NEG)
        m_old = m_s[...]
        m_new = jnp.maximum(m_old, jnp.max(s, axis=-1, keepdims=True))
        alpha = jnp.exp(m_old - m_new)
        p = jnp.exp(s - m_new)
        l_s[...] = alpha * l_s[...] + jnp.sum(p, axis=-1, keepdims=True)
        acc_s[...] = alpha * acc_s[...] + _dot(p.astype(BF16), v_ref[...])
        m_s[...] = m_new

        @pl.when(j == i)
        def _():
            o_ref[...] = (acc_s[...] / l_s[...]).astype(BF16)
            l_ref[0] = m_s[...] + jnp.log(l_s[...])

    grid_spec = pltpu.PrefetchScalarGridSpec(
        num_scalar_prefetch=2, grid=(N_HEADS, len(pairs)),
        in_specs=[pl.BlockSpec((tq, SLAB), lambda h, n, it, jt: (it[n], h)),
                  pl.BlockSpec((tq, SLAB), lambda h, n, it, jt: (jt[n], h)),
                  pl.BlockSpec((tq, SLAB), lambda h, n, it, jt: (jt[n], h))],
        out_specs=[pl.BlockSpec((tq, SLAB), lambda h, n, it, jt: (it[n], h)),
                   pl.BlockSpec((1, tq, 1), lambda h, n, it, jt: (h, it[n], 0))],
        scratch_shapes=[pltpu.VMEM((tq, 1), F32), pltpu.VMEM((tq, 1), F32), pltpu.VMEM((tq, SLAB), F32)])
    return pl.pallas_call(
        body, name="mla_fwd", grid_spec=grid_spec,
        out_shape=[jax.ShapeDtypeStruct((t, HM), BF16), jax.ShapeDtypeStruct((N_HEADS, t, 1), F32)],
        compiler_params=_params(("parallel", "arbitrary")),
    )(i_tab, j_tab, qb, kb, vb)


def _mla_bwd(qb, kb, vb, d_ob, lse, delta):
    t = qb.shape[0]
    tq = _attn_tile(t)
    nt = t // tq
    pairs = [(j, i) for j in range(nt) for i in range(j, nt)]
    j_tab = jnp.asarray(np.array([p[0] for p in pairs], np.int32))
    i_tab = jnp.asarray(np.array([p[1] for p in pairs], np.int32))

    def body(jt_ref, it_ref, q_ref, do_ref, l_ref, dl_ref, k_ref, v_ref, dq_ref, dk_ref, dv_ref, dk_s, dv_s):
        n = pl.program_id(1)
        j, i = jt_ref[n], it_ref[n]

        @pl.when(n == 0)
        def _():
            dq_ref[...] = jnp.zeros_like(dq_ref)

        @pl.when(i == j)
        def _():
            dk_s[...] = jnp.zeros_like(dk_s)
            dv_s[...] = jnp.zeros_like(dv_s)

        q, do, k, v = q_ref[...], do_ref[...], k_ref[...], v_ref[...]
        s = _dot_nt(q, k) * SCALE_B
        s = jnp.where(_causal_mask(i, j, tq), s, NEG)
        p = jnp.exp(s - l_ref[0])
        ds = (p * (_dot_nt(do, v) - dl_ref[0]) * SCALE_B).astype(BF16)
        dv_s[...] += _dot_tn(p.astype(BF16), do)
        dk_s[...] += _dot_tn(ds, q)
        rows = pl.ds(pl.multiple_of(i * tq, tq), tq)
        dq_ref[rows, :] += _dot(ds, k)

        @pl.when(i == nt - 1)
        def _():
            dk_ref[...] = dk_s[...].astype(BF16)
            dv_ref[...] = dv_s[...].astype(BF16)

    grid_spec = pltpu.PrefetchScalarGridSpec(
        num_scalar_prefetch=2, grid=(N_HEADS, len(pairs)),
        in_specs=[pl.BlockSpec((tq, SLAB), lambda h, n, jt, it: (it[n], h)),
                  pl.BlockSpec((tq, SLAB), lambda h, n, jt, it: (it[n], h)),
                  pl.BlockSpec((1, tq, 1), lambda h, n, jt, it: (h, it[n], 0)),
                  pl.BlockSpec((1, tq, 1), lambda h, n, jt, it: (h, it[n], 0)),
                  pl.BlockSpec((tq, SLAB), lambda h, n, jt, it: (jt[n], h)),
                  pl.BlockSpec((tq, SLAB), lambda h, n, jt, it: (jt[n], h))],
        out_specs=[pl.BlockSpec((t, SLAB), lambda h, n, jt, it: (0, h)),
                   pl.BlockSpec((tq, SLAB), lambda h, n, jt, it: (jt[n], h)),
                   pl.BlockSpec((tq, SLAB), lambda h, n, jt, it: (jt[n], h))],
        scratch_shapes=[pltpu.VMEM((tq, SLAB), F32), pltpu.VMEM((tq, SLAB), F32)])
    return pl.pallas_call(
        body, name="mla_bwd", grid_spec=grid_spec,
        out_shape=[jax.ShapeDtypeStruct((t, HM), F32), jax.ShapeDtypeStruct((t, HM), BF16),
                   jax.ShapeDtypeStruct((t, HM), BF16)],
        compiler_params=_params(("parallel", "arbitrary")),
    )(j_tab, i_tab, qb, d_ob, lse, delta, kb, vb)


def _merge_fwd(out_a, out_b, gates, x, w_oa, w_ob, w_out, g2, g3):
    t = x.shape[0]
    tm = _token_tile(t)

    def body(oa_ref, ob_ref, gates_ref, x_ref, woa_ref, wob_ref, wout_ref, g2_ref, g3_ref,
             oap_ref, obp_ref, merged_ref, y_ref, x1_ref, h2_ref):
        oa_p = _dot(oa_ref[...], woa_ref[...])
        ob_p = _dot(ob_ref[...], wob_ref[...])
        oap_ref[...] = oa_p.astype(BF16)
        obp_ref[...] = ob_p.astype(BF16)
        sa = _sigmoid(gates_ref[:, 0:D_MODEL])
        sb = _sigmoid(gates_ref[:, D_MODEL:2 * D_MODEL])
        merged = (sa * oa_p + sb * ob_p).astype(BF16)
        merged_ref[...] = merged
        y = _dot(merged, wout_ref[...])
        y_ref[...] = y
        x1 = x_ref[...] + y * _rms_r(y) * g2_ref[...]
        x1_ref[...] = x1
        h2_ref[...] = (x1 * _rms_r(x1) * g3_ref[...]).astype(BF16)

    def sds(dt):
        return jax.ShapeDtypeStruct((t, D_MODEL), dt)

    row = _row_spec(tm, D_MODEL)
    return pl.pallas_call(
        body, name="merge_fwd", grid=(t // tm,),
        in_specs=[_row_spec(tm, HM), _row_spec(tm, HM), _row_spec(tm, 2 * D_MODEL), row,
                  _full_spec((HM, D_MODEL)), _full_spec((HM, D_MODEL)), _full_spec((D_MODEL, D_MODEL)),
                  _full_spec((1, D_MODEL)), _full_spec((1, D_MODEL))],
        out_specs=[row] * 6,
        out_shape=[sds(BF16), sds(BF16), sds(BF16), sds(F32), sds(F32), sds(BF16)],
        compiler_params=_params(("parallel",)),
    )(out_a, out_b, gates, x, w_oa, w_ob, w_out, g2, g3)


def _merge_bwd(dx1, y, gates, oa_p, ob_p, out_a, out_b, w_oa, w_ob, w_out, g2):
    t = dx1.shape[0]
    tm = _token_tile(t)

    def body(dx1_ref, y_ref, gates_ref, oap_ref, obp_ref, oa_ref, ob_ref, woa_ref, wob_ref, wout_ref, g2_ref,
             dy_ref, doap_ref, dobp_ref, dgates_ref, doa_ref, dob_ref, dla_ref, dlb_ref, dg2_ref):
        dx1v = dx1_ref[...]
        yv = y_ref[...]
        r2 = _rms_r(yv)
        _acc_rows(dg2_ref, dx1v * yv * r2)
        dy = _rms_bwd(yv, r2, g2_ref[...], dx1v).astype(BF16)
        dy_ref[...] = dy
        dm = _dot_nt(dy, wout_ref[...])
        sa = _sigmoid(gates_ref[:, 0:D_MODEL])
        sb = _sigmoid(gates_ref[:, D_MODEL:2 * D_MODEL])
        d_oap = (dm * sa).astype(BF16)
        d_obp = (dm * sb).astype(BF16)
        doap_ref[...] = d_oap
        dobp_ref[...] = d_obp
        dgates_ref[:, 0:D_MODEL] = (dm * oap_ref[...].astype(F32) * sa * (1.0 - sa)).astype(BF16)
        dgates_ref[:, D_MODEL:2 * D_MODEL] = (dm * obp_ref[...].astype(F32) * sb * (1.0 - sb)).astype(BF16)
        d_oa = _dot_nt(d_oap, woa_ref[...])
        d_ob = _dot_nt(d_obp, wob_ref[...])
        doa_ref[...] = d_oa.astype(BF16)
        dob_ref[...] = d_ob.astype(BF16)
        for hd in range(N_HEADS):
            sl = slice(hd * SLAB, (hd + 1) * SLAB)
            dla_ref[hd] = jnp.sum(d_oa[:, sl] * oa_ref[:, sl].astype(F32), axis=-1, keepdims=True)
            dlb_ref[hd] = jnp.sum(d_ob[:, sl] * ob_ref[:, sl].astype(F32), axis=-1, keepdims=True)

    def sds(n, dt):
        return jax.ShapeDtypeStruct((t, n), dt)

    row = _row_spec(tm, D_MODEL)
    head3 = pl.BlockSpec((N_HEADS, tm, 1), lambda i: (0, i, 0))
    return pl.pallas_call(
        body, name="merge_bwd", grid=(t // tm,),
        in_specs=[row, row, _row_spec(tm, 2 * D_MODEL), row, row, _row_spec(tm, HM), _row_spec(tm, HM),
                  _full_spec((HM, D_MODEL)), _full_spec((HM, D_MODEL)), _full_spec((D_MODEL, D_MODEL)),
                  _full_spec((1, D_MODEL))],
        out_specs=[row, row, row, _row_spec(tm, 2 * D_MODEL), _row_spec(tm, HM), _row_spec(tm, HM),
                   head3, head3, _full_spec((1, D_MODEL))],
        out_shape=[sds(D_MODEL, BF16), sds(D_MODEL, BF16), sds(D_MODEL, BF16), sds(2 * D_MODEL, BF16),
                   sds(HM, BF16), sds(HM, BF16),
                   jax.ShapeDtypeStruct((N_HEADS, t, 1), F32), jax.ShapeDtypeStruct((N_HEADS, t, 1), F32),
                   jax.ShapeDtypeStruct((1, D_MODEL), F32)],
        compiler_params=_params(("arbitrary",)),
    )(dx1, y, gates, oa_p, ob_p, out_a, out_b, w_oa, w_ob, w_out, g2)


def _mlp_fwd_bwd(x1, h2, target, w_up, w_down, g3, g4):
    t = x1.shape[0]
    tm = _token_tile(t)

    def body(x1_ref, h2_ref, tgt_ref, wup_ref, wdown_ref, g3_ref, g4_ref,
             a_ref, du_ref, dy2_ref, dx1_ref, loss_ref, dg3_ref, dg4_ref):
        x1v = x1_ref[...]
        u = _dot(h2_ref[...], wup_ref[...])
        ru = jnp.maximum(u, 0.0)
        a = (ru * ru).astype(BF16)
        a_ref[...] = a
        y2 = _dot(a, wdown_ref[...])
        r4 = _rms_r(y2)
        diff = x1v + y2 * r4 * g4_ref[...] - tgt_ref[...]
        _acc_rows(loss_ref, jnp.sum(diff * diff, axis=-1, keepdims=True) * (0.5 / D_MODEL)
                  * jnp.ones((1, SLAB), F32))
        dx2 = diff * (1.0 / D_MODEL)
        _acc_rows(dg4_ref, dx2 * y2 * r4)
        dy2 = _rms_bwd(y2, r4, g4_ref[...], dx2).astype(BF16)
        dy2_ref[...] = dy2
        du = (_dot_nt(dy2, wdown_ref[...]) * (2.0 * ru)).astype(BF16)
        du_ref[...] = du
        dh2 = _dot_nt(du, wup_ref[...])
        r3 = _rms_r(x1v)
        _acc_rows(dg3_ref, dh2 * x1v * r3)
        dx1_ref[...] = dx2 + _rms_bwd(x1v, r3, g3_ref[...], dh2)

    row = _row_spec(tm, D_MODEL)
    frow = _row_spec(tm, D_FF)
    vec = _full_spec((1, D_MODEL))
    return pl.pallas_call(
        body, name="mlp_fwd_bwd", grid=(t // tm,),
        in_specs=[row, row, row, _full_spec((D_MODEL, D_FF)), _full_spec((D_FF, D_MODEL)), vec, vec],
        out_specs=[frow, frow, row, row, _full_spec((1, SLAB)), vec, vec],
        out_shape=[jax.ShapeDtypeStruct((t, D_FF), BF16), jax.ShapeDtypeStruct((t, D_FF), BF16),
                   jax.ShapeDtypeStruct((t, D_MODEL), BF16), jax.ShapeDtypeStruct((t, D_MODEL), F32),
                   jax.ShapeDtypeStruct((1, SLAB), F32), jax.ShapeDtypeStruct((1, D_MODEL), F32),
                   jax.ShapeDtypeStruct((1, D_MODEL), F32)],
        compiler_params=_params(("arbitrary",)),
    )(x1, h2, target, w_up, w_down, g3, g4)


def _inproj_bwd(dgates, dqa, dka, dva, dqb, dkb, dvb, cq, ckv, x, dx1, rope_c, rope_s1, rope_s2,
                g1, g_q, g_kv, w_in, w_qb, w_kvb):
    t = x.shape[0]
    tm = _token_tile(t)

    def body(dgates_ref, dqa_ref, dka_ref, dva_ref, dqb_ref, dkb_ref, dvb_ref, cq_ref, ckv_ref, x_ref, dx1_ref,
             c_ref, s1_ref, s2_ref, g1_ref, gq_ref, gkv_ref, win_ref, wqb_ref, wkvb_ref,
             dproj_ref, dqbr_ref, dkvb_ref, dx_ref, dg1_ref, dgq_ref, dgkv_ref):
        c, s1, s2 = c_ref[...], s1_ref[...], s2_ref[...]
        dk_sum = jnp.zeros((tm, SLAB), F32)
        for hd in range(N_HEADS):
            sl = slice(hd * SLAB, (hd + 1) * SLAB)
            dqbr_ref[:, sl] = _rope_bwd(dqb_ref[:, sl], c, s1, s2).astype(BF16)
            dk_sum += dkb_ref[:, sl].astype(F32)
        dkvb_ref[:, 0:HM] = dkb_ref[...]
        dkvb_ref[:, HM:2 * HM] = dvb_ref[...]
        dkr = _rope_bwd(dk_sum, c, s1, s2)
        dcqn = _dot_nt(dqbr_ref[...], wqb_ref[...])
        cq = cq_ref[...]
        rq = _rms_r(cq)
        _acc_rows(dgq_ref, dcqn * cq * rq)
        dcq = _rms_bwd(cq, rq, gq_ref[...], dcqn)
        dckvn = _dot_nt(dkvb_ref[...], wkvb_ref[...])
        ckv = ckv_ref[...]
        rkv = _rms_r(ckv)
        _acc_rows(dgkv_ref, dckvn * ckv * rkv)
        dckv = _rms_bwd(ckv, rkv, gkv_ref[...], dckvn)
        dproj_ref[:, C_GATES:C_QA] = dgates_ref[...]
        dproj_ref[:, C_QA:C_KA] = dqa_ref[...]
        dproj_ref[:, C_KA:C_VA] = dka_ref[...]
        dproj_ref[:, C_VA:C_CQ] = dva_ref[...]
        dproj_ref[:, C_CQ:C_CKV] = dcq.astype(BF16)
        dproj_ref[:, C_CKV:C_KR] = dckv.astype(BF16)
        dproj_ref[:, C_KR:D_IN_PAD] = dkr.astype(BF16)
        dh = _dot_nt(dproj_ref[...], win_ref[...])
        xv = x_ref[...]
        r1 = _rms_r(xv)
        _acc_rows(dg1_ref, dh * xv * r1)
        dx_ref[...] = dx1_ref[...] + _rms_bwd(xv, r1, g1_ref[...], dh)

    kvw = N_KV_A * SLAB
    row = _row_spec(tm, D_MODEL)
    hm = _row_spec(tm, HM)
    tab = _row_spec(tm, SLAB)
    return pl.pallas_call(
        body, name="inproj_bwd", grid=(t // tm,),
        in_specs=[_row_spec(tm, 2 * D_MODEL), hm, _row_spec(tm, kvw), _row_spec(tm, kvw), hm, hm, hm,
                  _row_spec(tm, Q_LORA), _row_spec(tm, KV_LORA), row, row, tab, tab, tab,
                  _full_spec((1, D_MODEL)), _full_spec((1, Q_LORA)), _full_spec((1, KV_LORA)),
                  _full_spec((D_MODEL, D_IN_PAD)), _full_spec((Q_LORA, HM)), _full_spec((KV_LORA, 2 * HM))],
        out_specs=[_row_spec(tm, D_IN_PAD), hm, _row_spec(tm, 2 * HM), row,
                   _full_spec((1, D_MODEL)), _full_spec((1, Q_LORA)), _full_spec((1, KV_LORA))],
        out_shape=[jax.ShapeDtypeStruct((t, D_IN_PAD), BF16), jax.ShapeDtypeStruct((t, HM), BF16),
                   jax.ShapeDtypeStruct((t, 2 * HM), BF16), jax.ShapeDtypeStruct((t, D_MODEL), F32),
                   jax.ShapeDtypeStruct((1, D_MODEL), F32), jax.ShapeDtypeStruct((1, Q_LORA), F32),
                   jax.ShapeDtypeStruct((1, KV_LORA), F32)],
        compiler_params=_params(("arbitrary",)),
    )(dgates, dqa, dka, dva, dqb, dkb, dvb, cq, ckv, x, dx1, rope_c, rope_s1, rope_s2,
      g1, g_q, g_kv, w_in, w_qb, w_kvb)


def _matmul_tn(a, b, name):
    t, k = a.shape
    n = b.shape[1]
    bt = min(t, 512)
    bk = min(k, 1024)
    bn = min(n, 1024)

    def body(a_ref, b_ref, o_ref):
        @pl.when(pl.program_id(2) == 0)
        def _():
            o_ref[...] = jnp.zeros_like(o_ref)
        o_ref[...] += _dot_tn(a_ref[...], b_ref[...])

    return pl.pallas_call(
        body, name=name, grid=(k // bk, n // bn, t // bt),
        in_specs=[pl.BlockSpec((bt, bk), lambda i, j, s: (s, i)), pl.BlockSpec((bt, bn), lambda i, j, s: (s, j))],
        out_specs=pl.BlockSpec((bk, bn), lambda i, j, s: (i, j)),
        out_shape=jax.ShapeDtypeStruct((k, n), F32),
        compiler_params=_params(("parallel", "parallel", "arbitrary")),
    )(a, b)


def _mesh_pos():
    return lax.axis_index("x"), lax.axis_index("y"), lax.axis_index("c")


def _flip(v, bit):
    return 1 - v if bit else v


def _all_gather_flat(flat):
    rows, cols = flat.shape

    def body(x_ref, out_ref, send_sems, recv_sems, local_sem):
        x, y, c = _mesh_pos()
        me, sibling = (x, y, c), (x, y, 1 - c)
        chips = [(1 - x, y), (x, 1 - y), (1 - x, 1 - y)]

        def slot(px, py, pc):
            return out_ref.at[4 * px + 2 * py + pc]

        def copy(k, block, to, src=None):
            return pltpu.make_async_remote_copy(
                src_ref=slot(*block) if src is None else src, dst_ref=slot(*block),
                send_sem=send_sems.at[k], recv_sem=recv_sems.at[k],
                device_id=to, device_id_type=pl.DeviceIdType.MESH)

        mine = pltpu.make_async_copy(x_ref, slot(*me), local_sem)
        mine.start()
        first = [copy(0, me, sibling, src=x_ref)]
        first += [copy(1 + j, me, (*chip, c), src=x_ref) for j, chip in enumerate(chips)]
        for cp in first:
            cp.start()
        passed = [copy(4 + j, (*chip, c), sibling) for j, chip in enumerate(chips)]
        for j, chip in enumerate(chips):
            copy(1 + j, (*chip, c), me).wait_recv()
            passed[j].start()
        copy(0, sibling, me).wait_recv()
        for j, chip in enumerate(chips):
            copy(4 + j, (*chip, 1 - c), me).wait_recv()
        for cp in first + passed:
            cp.wait_send()
        mine.wait()

    return pl.pallas_call(
        body, name="all_gather_weights",
        out_shape=jax.ShapeDtypeStruct((N_DEV, rows, cols), flat.dtype),
        in_specs=[pl.BlockSpec(memory_space=pl.ANY)],
        out_specs=pl.BlockSpec(memory_space=pl.ANY),
        scratch_shapes=[pltpu.SemaphoreType.DMA((7,)), pltpu.SemaphoreType.DMA((7,)), pltpu.SemaphoreType.DMA],
    )(flat)


def _exchange_grads(gflat, small):
    _, rows, cols = gflat.shape

    def body(g_ref, s_ref, grecv_ref, srecv_ref, send_sems, recv_sems, local_sems):
        x, y, c = _mesh_pos()
        me = 4 * x + 2 * y + c
        own_g = pltpu.make_async_copy(g_ref.at[me], grecv_ref.at[me], local_sems.at[0])
        own_s = pltpu.make_async_copy(s_ref, srecv_ref.at[me], local_sems.at[1])
        own_g.start()
        own_s.start()
        copies = []
        for r in range(1, N_DEV):
            px, py, pc = _flip(x, r & 4), _flip(y, r & 2), _flip(c, r & 1)
            peer = 4 * px + 2 * py + pc
            to = (px, py, pc)
            cg = pltpu.make_async_remote_copy(
                src_ref=g_ref.at[peer], dst_ref=grecv_ref.at[me],
                send_sem=send_sems.at[r - 1], recv_sem=recv_sems.at[r - 1],
                device_id=to, device_id_type=pl.DeviceIdType.MESH)
            cs = pltpu.make_async_remote_copy(
                src_ref=s_ref, dst_ref=srecv_ref.at[me],
                send_sem=send_sems.at[6 + r], recv_sem=recv_sems.at[6 + r],
                device_id=to, device_id_type=pl.DeviceIdType.MESH)
            cg.start()
            cs.start()
            copies += [cg, cs]
        for cp in copies:
            cp.wait_recv()
        for cp in copies:
            cp.wait_send()
        own_g.wait()
        own_s.wait()

    return pl.pallas_call(
        body, name="exchange_grads",
        out_shape=[jax.ShapeDtypeStruct((N_DEV, rows, cols), gflat.dtype),
                   jax.ShapeDtypeStruct((N_DEV, SMALL_ROWS, cols), small.dtype)],
        in_specs=[pl.BlockSpec(memory_space=pl.ANY), pl.BlockSpec(memory_space=pl.ANY)],
        out_specs=[pl.BlockSpec(memory_space=pl.ANY), pl.BlockSpec(memory_space=pl.ANY)],
        scratch_shapes=[pltpu.SemaphoreType.DMA((14,)), pltpu.SemaphoreType.DMA((14,)),
                        pltpu.SemaphoreType.DMA((2,))],
    )(gflat, small)


def _adamw(parts, w, m, v, name):
    _, rows, cols = parts.shape
    tr = 288 if rows % 288 == 0 else rows
    c1 = 1.0 - ADAM_B1 ** ADAM_STEP
    c2 = 1.0 - ADAM_B2 ** ADAM_STEP

    def body(p_ref, w_ref, m_ref, v_ref, g_ref, d_ref, mo_ref, vo_ref):
        g = p_ref[0].astype(F32)
        for s in range(1, N_DEV):
            g = g + p_ref[s].astype(F32)
        g_ref[...] = g
        m_new = ADAM_B1 * m_ref[...] + (1.0 - ADAM_B1) * g
        v_new = ADAM_B2 * v_ref[...] + (1.0 - ADAM_B2) * (g * g)
        mo_ref[...] = m_new
        vo_ref[...] = v_new
        m_hat = m_new / c1
        v_hat = v_new / c2
        d_ref[...] = -ADAM_LR * (m_hat / (jnp.sqrt(v_hat) + ADAM_EPS) + ADAM_WD * w_ref[...])

    row = pl.BlockSpec((tr, cols), lambda i: (i, 0))
    out = jax.ShapeDtypeStruct((rows, cols), F32)
    return pl.pallas_call(
        body, name=name, grid=(rows // tr,),
        in_specs=[pl.BlockSpec((N_DEV, tr, cols), lambda i: (0, i, 0)), row, row, row],
        out_specs=[row] * 4, out_shape=[out] * 4,
        compiler_params=_params(("parallel",)),
    )(parts, w, m, v)


def _pad_heads_cols(w, heads, width):
    k = w.shape[0]
    w = w.reshape(k, heads, width)
    return jnp.pad(w, ((0, 0), (0, 0), (0, SLAB - width))).reshape(k, heads * SLAB)


def _unpad_heads_cols(w, heads, width):
    k = w.shape[0]
    return w.reshape(k, heads, SLAB)[:, :, :width].reshape(k, heads * width)


def _pad_heads_rows(w, heads, width):
    n = w.shape[1]
    w = w.reshape(heads, width, n)
    return jnp.pad(w, ((0, 0), (0, SLAB - width), (0, 0))).reshape(heads * SLAB, n)


def _unpad_heads_rows(w, heads, width):
    n = w.shape[1]
    return w.reshape(heads, SLAB, n)[:, :width, :].reshape(heads * width, n)


def _pad_w_in(w_in):
    o = 2 * D_MODEL
    qa = _pad_heads_cols(w_in[:, o:o + 512], N_HEADS, HEAD_A)
    ka = _pad_heads_cols(w_in[:, o + 512:o + 640], N_KV_A, HEAD_A)
    va = _pad_heads_cols(w_in[:, o + 640:o + 768], N_KV_A, HEAD_A)
    kr = jnp.pad(w_in[:, o + 1152:o + 1184], ((0, 0), (QK_NOPE, SLAB - QK_NOPE - QK_ROPE)))
    return jnp.concatenate([w_in[:, :o], qa, ka, va, w_in[:, o + 768:o + 1152], kr], axis=1)


def _unpad_w_in(w):
    qa = _unpad_heads_cols(w[:, C_QA:C_KA], N_HEADS, HEAD_A)
    ka = _unpad_heads_cols(w[:, C_KA:C_VA], N_KV_A, HEAD_A)
    va = _unpad_heads_cols(w[:, C_VA:C_CQ], N_KV_A, HEAD_A)
    kr = w[:, C_KR + QK_NOPE:C_KR + QK_NOPE + QK_ROPE]
    return jnp.concatenate([w[:, :C_QA], qa, ka, va, w[:, C_CQ:C_KR], kr], axis=1)


def _pad_w_kvb(w_kvb):
    w = w_kvb.reshape(KV_LORA, N_HEADS, QK_NOPE + V_DIM_B)
    k = jnp.pad(w[:, :, :QK_NOPE], ((0, 0), (0, 0), (0, SLAB - QK_NOPE))).reshape(KV_LORA, HM)
    v = jnp.pad(w[:, :, QK_NOPE:], ((0, 0), (0, 0), (0, SLAB - V_DIM_B))).reshape(KV_LORA, HM)
    return jnp.concatenate([k, v], axis=1)


def _unpad_w_kvb(w):
    k = w[:, :HM].reshape(KV_LORA, N_HEADS, SLAB)[:, :, :QK_NOPE]
    v = w[:, HM:].reshape(KV_LORA, N_HEADS, SLAB)[:, :, :V_DIM_B]
    return jnp.concatenate([k, v], axis=2).reshape(KV_LORA, N_HEADS * (QK_NOPE + V_DIM_B))


def _col_shards(w):
    k, n = w.shape
    return w.reshape(k, N_DEV, n // N_DEV).transpose(1, 0, 2).reshape(N_DEV, -1, 1024)


def _row_shards(w):
    return w.reshape(N_DEV, -1, 1024)


def _from_col_shards(s, k, n):
    return s.reshape(N_DEV, k, n // N_DEV).transpose(1, 0, 2).reshape(k, n)


def _flatten_shards(parts):
    pieces = [parts[name] for name, _ in FLAT_ROWS]
    lead = pieces[0].shape[:-2]
    pad = jnp.zeros(lead + (FLAT_R - FLAT_USED, 1024), pieces[0].dtype)
    return jnp.concatenate(pieces + [pad], axis=-2)


def _split_flat(flat):
    out, o = {}, 0
    for name, r in FLAT_ROWS:
        out[name] = flat[..., o:o + r, :]
        o += r
    return out


def _freq_row():
    freqs = ROPE_THETA ** (-jnp.arange(0, QK_ROPE, 2, dtype=F32) / QK_ROPE)
    return jnp.concatenate([jnp.zeros((QK_NOPE,), F32), freqs, freqs,
                            jnp.zeros((SLAB - QK_NOPE - QK_ROPE,), F32)]).reshape(1, SLAB)


SMALL_D_ROWS = ("pre_norm_mix", "post_norm_mix", "pre_norm_mlp", "post_norm_mlp")
SMALL_Q_OFF, SMALL_KV_OFF, SMALL_SINK_OFF, SMALL_LOSS_OFF = 0, 256, 384, 392


def _pack_small(vals):
    row4 = jnp.concatenate([vals["q_a_norm"].reshape(-1), vals["kv_a_norm"].reshape(-1), vals["sinks"].reshape(-1),
                            vals["loss"].reshape(-1), jnp.zeros((1024 - 393,), F32)])
    rows = [vals[n].reshape(1024) for n in SMALL_D_ROWS] + [row4]
    return jnp.concatenate([jnp.stack(rows), jnp.zeros((SMALL_ROWS - 5, 1024), F32)], axis=0)


def _unpack_small(blk):
    out = {n: blk[i].reshape(1, 1024) for i, n in enumerate(SMALL_D_ROWS)}
    out["q_a_norm"] = blk[4, SMALL_Q_OFF:SMALL_Q_OFF + 256].reshape(1, 256)
    out["kv_a_norm"] = blk[4, SMALL_KV_OFF:SMALL_KV_OFF + 128].reshape(1, 128)
    out["sinks"] = blk[4, SMALL_SINK_OFF:SMALL_SINK_OFF + 8].reshape(1, 8)
    out["loss"] = blk[4, SMALL_LOSS_OFF]
    return out


def _local_step(x, pos, target, w, small):
    t = x.shape[0]
    pos_col = pos.reshape(t, 1)
    pos_row = pos.reshape(1, t)
    w_in = _pad_w_in(w["w_in"])
    w_qb = _pad_heads_cols(w["w_q_b"], N_HEADS, QK_NOPE + QK_ROPE)
    w_kvb = _pad_w_kvb(w["w_kv_b"])
    w_oa = _pad_heads_rows(w["w_o_a"], N_HEADS, HEAD_A)
    w_ob = _pad_heads_rows(w["w_o_b"], N_HEADS, V_DIM_B)
    g1, g2, g3, g4 = (small[n] for n in SMALL_D_ROWS)
    g_q, g_kv = small["q_a_norm"], small["kv_a_norm"]
    sinks = small["sinks"].reshape(N_HEADS)

    rc, rs1, rs2 = _rope_tables(pos_col, _freq_row())
    (h, gates, qa, ka, va, cq, ckv, cqn, ckvn, qb, kb, vb) = _inproj_fwd(
        x, g1, w_in, g_q, w_qb, g_kv, w_kvb, rc, rs1, rs2)
    out_a, lse_a = _swa_fwd(qa, ka, va, pos_col, pos_row, sinks)
    out_b, lse_b = _mla_fwd(qb, kb, vb)
    oa_p, ob_p, merged, y, x1, h2 = _merge_fwd(out_a, out_b, gates, x, w_oa, w_ob, w["w_out"], g2, g3)
    a, du, dy2, dx1, loss, dg3, dg4 = _mlp_fwd_bwd(x1, h2, target, w["w_up"], w["w_down"], g3, g4)
    (dy, d_oap, d_obp, dgates, d_oa, d_ob, delta_a, delta_b, dg2) = _merge_bwd(
        dx1, y, gates, oa_p, ob_p, out_a, out_b, w_oa, w_ob, w["w_out"], g2)
    dqa, dka, dva, dsink = _swa_bwd(qa, ka, va, d_oa, lse_a, delta_a, pos_col, pos_row, sinks)
    dqb, dkb, dvb = _mla_bwd(qb, kb, vb, d_ob, lse_b, delta_b)
    dproj, dqbr, dkvb, dx, dg1, dgq, dgkv = _inproj_bwd(
        dgates, dqa, dka, dva, dqb, dkb, dvb, cq, ckv, x, dx1, rc, rs1, rs2, g1, g_q, g_kv, w_in, w_qb, w_kvb)

    grads = {
        "w_in": _unpad_w_in(_matmul_tn(h, dproj, "dw_in")),
        "w_q_b": _unpad_heads_cols(_matmul_tn(cqn, dqbr, "dw_q_b"), N_HEADS, QK_NOPE + QK_ROPE),
        "w_kv_b": _unpad_w_kvb(_matmul_tn(ckvn, dkvb, "dw_kv_b")),
        "w_o_a": _unpad_heads_rows(_matmul_tn(out_a, d_oap, "dw_o_a"), N_HEADS, HEAD_A),
        "w_o_b": _unpad_heads_rows(_matmul_tn(out_b, d_obp, "dw_o_b"), N_HEADS, V_DIM_B),
        "w_out": _matmul_tn(merged, dy, "dw_out"),
        "w_up": _matmul_tn(h2, du, "dw_up"),
        "w_down": _matmul_tn(a, dy2, "dw_down"),
    }
    small_grads = {"pre_norm_mix": dg1, "post_norm_mix": dg2, "pre_norm_mlp": dg3, "post_norm_mlp": dg4,
                   "q_a_norm": dgq, "kv_a_norm": dgkv, "sinks": dsink[:, 0], "loss": loss[0, 0:1]}
    return dx, grads, small_grads


COL_SHARDED = {"w_in": (D_MODEL, 3232), "w_q_b": (Q_LORA, 768), "w_kv_b": (KV_LORA, 1024),
               "w_o_a": (512, D_MODEL), "w_o_b": (512, D_MODEL), "w_up": (D_MODEL, D_FF)}
ROW_SHARDED = {"w_out": (D_MODEL, D_MODEL), "w_down": (D_FF, D_MODEL)}
WEIGHT_ORDER = ("pre_norm_mix", "w_in", "q_a_norm", "w_q_b", "kv_a_norm", "w_kv_b", "sinks", "w_o_a", "w_o_b",
                "w_out", "post_norm_mix", "pre_norm_mlp", "w_up", "w_down", "post_norm_mlp")
SMALL_NAMES = ("pre_norm_mix", "q_a_norm", "kv_a_norm", "sinks", "post_norm_mix", "pre_norm_mlp", "post_norm_mlp")


def _flat_local(tensors):
    return _flatten_shards({name: tensors[name].reshape(-1, 1024) for name, _ in FLAT_ROWS})


def kernel(x, positions, pre_norm_mix, w_in, q_a_norm, w_q_b, kv_a_norm, w_kv_b, sinks, w_o_a, w_o_b, w_out, post_norm_mix, pre_norm_mlp, w_up, w_down, post_norm_mlp, loss_target, m_pre_norm_mix, m_w_in, m_q_a_norm, m_w_q_b, m_kv_a_norm, m_w_kv_b, m_sinks, m_w_o_a, m_w_o_b, m_w_out, m_post_norm_mix, m_pre_norm_mlp, m_w_up, m_w_down, m_post_norm_mlp, v_pre_norm_mix, v_w_in, v_q_a_norm, v_w_q_b, v_kv_a_norm, v_w_kv_b, v_sinks, v_w_o_a, v_w_o_b, v_w_out, v_post_norm_mix, v_pre_norm_mlp, v_w_up, v_w_down, v_post_norm_mlp):
    weights = dict(pre_norm_mix=pre_norm_mix, w_in=w_in, q_a_norm=q_a_norm, w_q_b=w_q_b, kv_a_norm=kv_a_norm,
                   w_kv_b=w_kv_b, sinks=sinks, w_o_a=w_o_a, w_o_b=w_o_b, w_out=w_out, post_norm_mix=post_norm_mix,
                   pre_norm_mlp=pre_norm_mlp, w_up=w_up, w_down=w_down, post_norm_mlp=post_norm_mlp)
    m_in = dict(pre_norm_mix=m_pre_norm_mix, w_in=m_w_in, q_a_norm=m_q_a_norm, w_q_b=m_w_q_b, kv_a_norm=m_kv_a_norm,
                w_kv_b=m_w_kv_b, sinks=m_sinks, w_o_a=m_w_o_a, w_o_b=m_w_o_b, w_out=m_w_out,
                post_norm_mix=m_post_norm_mix, pre_norm_mlp=m_pre_norm_mlp, w_up=m_w_up, w_down=m_w_down,
                post_norm_mlp=m_post_norm_mlp)
    v_in = dict(pre_norm_mix=v_pre_norm_mix, w_in=v_w_in, q_a_norm=v_q_a_norm, w_q_b=v_w_q_b, kv_a_norm=v_kv_a_norm,
                w_kv_b=v_w_kv_b, sinks=v_sinks, w_o_a=v_w_o_a, w_o_b=v_w_o_b, w_out=v_w_out,
                post_norm_mix=v_post_norm_mix, pre_norm_mlp=v_pre_norm_mlp, w_up=v_w_up, w_down=v_w_down,
                post_norm_mlp=v_post_norm_mlp)

    gathered = _split_flat(_all_gather_flat(_flat_local(weights).astype(BF16)))
    full = {}
    for name, (k, n) in COL_SHARDED.items():
        full[name] = _from_col_shards(gathered[name], k, n)
    for name, (k, n) in ROW_SHARDED.items():
        full[name] = gathered[name].reshape(k, n)
    small = {n: weights[n] for n in SMALL_NAMES}

    dx, grads, small_grads = _local_step(x[0], positions[0], loss_target[0], full, small)

    shards = {name: _col_shards(grads[name]) for name in COL_SHARDED}
    shards.update({name: _row_shards(grads[name]) for name in ROW_SHARDED})
    g_parts, s_parts = _exchange_grads(_flatten_shards(shards).astype(BF16), _pack_small(small_grads))

    g_flat, d_flat, m_flat, v_flat = _adamw(g_parts, _flat_local(weights), _flat_local(m_in), _flat_local(v_in),
                                            "adamw_shards")
    zero = jnp.zeros((), F32)
    pack = lambda src: _pack_small({**{n: src[n] for n in SMALL_NAMES}, "loss": zero})
    g_small, d_small, m_small, v_small = _adamw(s_parts, pack(weights), pack(m_in), pack(v_in), "adamw_small")

    results = []
    for flat, sm in ((g_flat, g_small), (d_flat, d_small), (m_flat, m_small), (v_flat, v_small)):
        big = _split_flat(flat)
        sml = _unpack_small(sm)
        for name in WEIGHT_ORDER:
            if name in sml:
                results.append(sml[name])
            else:
                results.append(big[name].reshape(weights[name].shape))
    loss = _unpack_small(g_small)["loss"]
    return (loss, dx[None], *results)
```

```python
import functools

import numpy as np
import jax
import jax.numpy as jnp
from jax import lax
from jax.experimental import pallas as pl
from jax.experimental.pallas import tpu as pltpu

F32 = jnp.float32
BF16 = jnp.bfloat16

D_MODEL = 1024
D_FF = 4096
N_HEADS = 8
N_KV_A = 2
GROUP_A = N_HEADS // N_KV_A
HEAD_A = 64
QK_NOPE = 64
QK_ROPE = 32
V_DIM_B = 64
Q_LORA = 256
KV_LORA = 128
BLOCK = 128
SLAB = 128
ROPE_THETA = 10000.0
EPS = 1e-6
N_DEV = 8
NEG = -1e30

SCALE_A = HEAD_A ** -0.5
SCALE_B = (QK_NOPE + QK_ROPE) ** -0.5
LOG2E = 1.4426950408889634
SCORE_B = SCALE_B * LOG2E
MLA_HEADS_PER_STEP = 2
SLOPES_A = tuple(2.0 ** (-8.0 * (h + 1) / N_HEADS) for h in range(N_HEADS))

ADAM_LR = 0.001
ADAM_B1 = 0.9
ADAM_B2 = 0.999
ADAM_EPS = 1e-08
ADAM_WD = 0.01
ADAM_STEP = 10

HM = N_HEADS * SLAB
C_GATES = 0
C_QA = 2 * D_MODEL
C_KA = C_QA + HM
C_VA = C_KA + N_KV_A * SLAB
C_CQ = C_VA + N_KV_A * SLAB
C_CKV = C_CQ + Q_LORA
C_KR = C_CKV + KV_LORA
D_IN_PAD = C_KR + SLAB

VMEM_LIMIT = 56 * 1024 * 1024

FLAT_ROWS = (("w_in", 404), ("w_q_b", 24), ("w_kv_b", 16), ("w_o_a", 64), ("w_o_b", 64),
             ("w_out", 128), ("w_up", 512), ("w_down", 512))
FLAT_ALIGN = 16
FLAT_TILE = 256
FLAT_USED = sum(-(-r // FLAT_ALIGN) * FLAT_ALIGN for _, r in FLAT_ROWS)
FLAT_R = -(-FLAT_USED // FLAT_TILE) * FLAT_TILE
SMALL_ROWS = 8


def _token_tile(t):
    return min(256, t)


def _attn_tile(t):
    return 512 if t >= 2048 else 128


def _params(sem, vmem=VMEM_LIMIT):
    return pltpu.CompilerParams(dimension_semantics=sem, vmem_limit_bytes=vmem)


def _dot(a, b):
    return jnp.dot(a, b, preferred_element_type=F32)


def _dot_nt(a, b):
    return lax.dot_general(a, b, (((1,), (1,)), ((), ())), preferred_element_type=F32)


def _dot_tn(a, b):
    return lax.dot_general(a, b, (((0,), (0,)), ((), ())), preferred_element_type=F32)


def _rms_r(x):
    return lax.rsqrt(jnp.mean(x * x, axis=-1, keepdims=True) + EPS)


def _rms_bwd(x, r, g, dy):
    t = dy * g
    return r * t - x * (r * r * r) * jnp.mean(x * t, axis=-1, keepdims=True)


def _sigmoid(x):
    return 1.0 / (1.0 + jnp.exp(-x))


def _rope(x, c, s1, s2):
    return x * c + pltpu.roll(x, SLAB - 16, 1) * s1 + pltpu.roll(x, 16, 1) * s2


def _rope_bwd(d, c, s1, s2):
    return d * c + pltpu.roll(d * s1, 16, 1) + pltpu.roll(d * s2, SLAB - 16, 1)


def _row_spec(tm, n):
    return pl.BlockSpec((tm, n), lambda i: (i, 0))


def _full_spec(shape):
    nd = len(shape)
    return pl.BlockSpec(shape, lambda i: (0,) * nd, pipeline_mode=pl.Buffered(1))


def _acc_rows(ref, val):
    @pl.when(pl.program_id(0) == 0)
    def _():
        ref[...] = jnp.zeros_like(ref)
    ref[...] += jnp.sum(val, axis=0, keepdims=True)


def _rope_tables(pos_col, freq_row):
    t = pos_col.shape[0]
    tm = _token_tile(t)

    def body(pos_ref, f_ref, c_ref, s1_ref, s2_ref):
        ang = pos_ref[...].astype(F32) * f_ref[...]
        lane = lax.broadcasted_iota(jnp.int32, ang.shape, 1)
        s = jnp.sin(ang)
        c_ref[...] = jnp.cos(ang)
        s1_ref[...] = jnp.where((lane >= 64) & (lane < 80), -s, 0.0)
        s2_ref[...] = jnp.where((lane >= 80) & (lane < 96), s, 0.0)

    tab = jax.ShapeDtypeStruct((t, SLAB), F32)
    return pl.pallas_call(
        body, name="rope_tables", grid=(t // tm,),
        in_specs=[_row_spec(tm, 1), _full_spec((1, SLAB))],
        out_specs=[_row_spec(tm, SLAB)] * 3, out_shape=[tab] * 3,
        compiler_params=_params(("parallel",)),
    )(pos_col, freq_row)


def _inproj_fwd(x, g1, w_in, g_q, w_qb, g_kv, w_kvb, rope_c, rope_s1, rope_s2):
    t = x.shape[0]
    tm = _token_tile(t)

    def body(x_ref, g1_ref, win_ref, gq_ref, wqb_ref, gkv_ref, wkvb_ref, c_ref, s1_ref, s2_ref,
             h_ref, gates_ref, qa_ref, ka_ref, va_ref, cq_ref, ckv_ref, cqn_ref, ckvn_ref,
             qb_ref, kb_ref, vb_ref):
        xv = x_ref[...]
        h = (xv * _rms_r(xv) * g1_ref[...]).astype(BF16)
        h_ref[...] = h
        proj = _dot(h, win_ref[...])
        gates_ref[...] = proj[:, C_GATES:C_QA]
        qa_ref[...] = proj[:, C_QA:C_KA].astype(BF16)
        ka_ref[...] = proj[:, C_KA:C_VA].astype(BF16)
        va_ref[...] = proj[:, C_VA:C_CQ].astype(BF16)
        cq = proj[:, C_CQ:C_CKV]
        ckv = proj[:, C_CKV:C_KR]
        kr = proj[:, C_KR:D_IN_PAD]
        cq_ref[...] = cq
        ckv_ref[...] = ckv
        cqn = (cq * _rms_r(cq) * gq_ref[...]).astype(BF16)
        ckvn = (ckv * _rms_r(ckv) * gkv_ref[...]).astype(BF16)
        cqn_ref[...] = cqn
        ckvn_ref[...] = ckvn
        c, s1, s2 = c_ref[...], s1_ref[...], s2_ref[...]
        qb = _dot(cqn, wqb_ref[...])
        kvb = _dot(ckvn, wkvb_ref[...])
        kr_rot = _rope(kr, c, s1, s2)
        for hd in range(N_HEADS):
            sl = slice(hd * SLAB, (hd + 1) * SLAB)
            qb_ref[:, sl] = (_rope(qb[:, sl], c, s1, s2) * SCORE_B).astype(BF16)
            kb_ref[:, sl] = (kvb[:, sl] + kr_rot).astype(BF16)
        vb_ref[...] = kvb[:, HM:2 * HM].astype(BF16)

    def sds(n, dt):
        return jax.ShapeDtypeStruct((t, n), dt)

    outs = [(D_MODEL, BF16), (2 * D_MODEL, F32), (HM, BF16), (N_KV_A * SLAB, BF16), (N_KV_A * SLAB, BF16),
            (Q_LORA, F32), (KV_LORA, F32), (Q_LORA, BF16), (KV_LORA, BF16), (HM, BF16), (HM, BF16), (HM, BF16)]
    return pl.pallas_call(
        body, name="inproj_fwd", grid=(t // tm,),
        in_specs=[_row_spec(tm, D_MODEL), _full_spec((1, D_MODEL)), _full_spec((D_MODEL, D_IN_PAD)),
                  _full_spec((1, Q_LORA)), _full_spec((Q_LORA, HM)), _full_spec((1, KV_LORA)),
                  _full_spec((KV_LORA, 2 * HM)), _row_spec(tm, SLAB), _row_spec(tm, SLAB), _row_spec(tm, SLAB)],
        out_specs=[_row_spec(tm, n) for n, _ in outs],
        out_shape=[sds(n, dt) for n, dt in outs],
        compiler_params=_params(("parallel",)),
    )(x, g1, w_in, g_q, w_qb, g_kv, w_kvb, rope_c, rope_s1, rope_s2)


def _swa_masks():
    row = lax.broadcasted_iota(jnp.int32, (BLOCK, BLOCK), 0)
    col = lax.broadcasted_iota(jnp.int32, (BLOCK, BLOCK), 1)
    return row >= col, col > row


def _swa_fwd(qa, ka, va, pos_col, pos_row, sinks):
    t = qa.shape[0]
    nb = t // BLOCK

    def body(sinks_ref, q_ref, kc_ref, kp_ref, vc_ref, vp_ref, pq_ref, pkc_ref, pkp_ref, o_ref, l_ref):
        i = pl.program_id(0)
        pq = pq_ref[...]
        dist_c = jnp.abs(pq - pkc_ref[...]).astype(F32)
        dist_p = jnp.abs(pq - pkp_ref[...]).astype(F32)
        mask_c, upper = _swa_masks()
        mask_p = jnp.logical_and(upper, i > 0)
        for hd in range(N_HEADS):
            g = hd // GROUP_A
            q = q_ref[:, hd * SLAB:(hd + 1) * SLAB]
            kc = kc_ref[:, g * SLAB:(g + 1) * SLAB]
            kp = kp_ref[:, g * SLAB:(g + 1) * SLAB]
            vc = vc_ref[:, g * SLAB:(g + 1) * SLAB]
            vp = vp_ref[:, g * SLAB:(g + 1) * SLAB]
            sink = sinks_ref[hd]
            s_c = jnp.where(mask_c, _dot_nt(q, kc) * SCALE_A - SLOPES_A[hd] * dist_c, NEG)
            s_p = jnp.where(mask_p, _dot_nt(q, kp) * SCALE_A - SLOPES_A[hd] * dist_p, NEG)
            m = jnp.maximum(jnp.maximum(jnp.max(s_c, axis=-1, keepdims=True),
                                        jnp.max(s_p, axis=-1, keepdims=True)), sink)
            e_c = jnp.exp(s_c - m)
            e_p = jnp.exp(s_p - m)
            den = (jnp.sum(e_c, axis=-1, keepdims=True) + jnp.sum(e_p, axis=-1, keepdims=True)
                   + jnp.exp(sink - m))
            inv = 1.0 / den
            o = _dot((e_c * inv).astype(BF16), vc) + _dot((e_p * inv).astype(BF16), vp)
            o_ref[:, hd * SLAB:(hd + 1) * SLAB] = o.astype(BF16)
            l_ref[hd] = m + jnp.log(den)

    cur = lambda i: (i, 0)
    prev = lambda i: (jnp.maximum(i - 1, 0), 0)
    kvw = N_KV_A * SLAB
    return pl.pallas_call(
        body, name="swa_fwd", grid=(nb,),
        in_specs=[pl.BlockSpec(memory_space=pltpu.SMEM),
                  pl.BlockSpec((BLOCK, HM), cur),
                  pl.BlockSpec((BLOCK, kvw), cur), pl.BlockSpec((BLOCK, kvw), prev),
                  pl.BlockSpec((BLOCK, kvw), cur), pl.BlockSpec((BLOCK, kvw), prev),
                  pl.BlockSpec((BLOCK, 1), cur),
                  pl.BlockSpec((1, BLOCK), lambda i: (0, i)),
                  pl.BlockSpec((1, BLOCK), lambda i: (0, jnp.maximum(i - 1, 0)))],
        out_specs=[pl.BlockSpec((BLOCK, HM), cur), pl.BlockSpec((N_HEADS, BLOCK, 1), lambda i: (0, i, 0))],
        out_shape=[jax.ShapeDtypeStruct((t, HM), BF16), jax.ShapeDtypeStruct((N_HEADS, t, 1), F32)],
        compiler_params=_params(("parallel",)),
    )(sinks, qa, ka, ka, va, va, pos_col, pos_row, pos_row)


def _swa_bwd(qa, ka, va, d_oa, lse, delta, pos_col, pos_row, sinks):
    t = qa.shape[0]
    nb = t // BLOCK

    def body(sinks_ref, q_ref, qn_ref, do_ref, don_ref, l_ref, ln_ref, dl_ref, dln_ref,
             kp_ref, kc_ref, vp_ref, vc_ref, pq_ref, pqn_ref, pkp_ref, pkc_ref,
             dq_ref, dk_ref, dv_ref, dsink_ref):
        j = pl.program_id(0)
        pq, pqn = pq_ref[...], pqn_ref[...]
        pkp, pkc = pkp_ref[...], pkc_ref[...]
        dist_cc = jnp.abs(pq - pkc).astype(F32)
        dist_cp = jnp.abs(pq - pkp).astype(F32)
        dist_nc = jnp.abs(pqn - pkc).astype(F32)
        mask_cc, upper = _swa_masks()
        mask_cp = jnp.logical_and(upper, j > 0)
        mask_nc = jnp.logical_and(upper, j < nb - 1)

        @pl.when(j == 0)
        def _():
            dsink_ref[...] = jnp.zeros_like(dsink_ref)

        def pair(q, do, lq, dl, k, v, dist, mask, slope):
            s = jnp.where(mask, _dot_nt(q, k) * SCALE_A - slope * dist, NEG)
            p = jnp.exp(s - lq)
            ds = p * (_dot_nt(do, v) - dl)
            return p.astype(BF16), ds.astype(BF16)

        for g in range(N_KV_A):
            kc = kc_ref[:, g * SLAB:(g + 1) * SLAB]
            kp = kp_ref[:, g * SLAB:(g + 1) * SLAB]
            vc = vc_ref[:, g * SLAB:(g + 1) * SLAB]
            vp = vp_ref[:, g * SLAB:(g + 1) * SLAB]
            dk_acc = jnp.zeros((BLOCK, SLAB), F32)
            dv_acc = jnp.zeros((BLOCK, SLAB), F32)
            for hh in range(GROUP_A):
                hd = g * GROUP_A + hh
                sl = slice(hd * SLAB, (hd + 1) * SLAB)
                slope = SLOPES_A[hd]
                q, do, lq, dl = q_ref[:, sl], do_ref[:, sl], l_ref[hd], dl_ref[hd]
                qn, don, lqn, dln = qn_ref[:, sl], don_ref[:, sl], ln_ref[hd], dln_ref[hd]
                p_cc, ds_cc = pair(q, do, lq, dl, kc, vc, dist_cc, mask_cc, slope)
                _, ds_cp = pair(q, do, lq, dl, kp, vp, dist_cp, mask_cp, slope)
                p_nc, ds_nc = pair(qn, don, lqn, dln, kc, vc, dist_nc, mask_nc, slope)
                dq_ref[:, sl] = ((_dot(ds_cc, kc) + _dot(ds_cp, kp)) * SCALE_A).astype(BF16)
                dk_acc += (_dot_tn(ds_cc, q) + _dot_tn(ds_nc, qn)) * SCALE_A
                dv_acc += _dot_tn(p_cc, do) + _dot_tn(p_nc, don)
                p_sink = jnp.exp(sinks_ref[hd] - lq)
                dsink_ref[hd:hd + 1, :] += jnp.broadcast_to(-jnp.sum(p_sink * dl), (1, SLAB))
            dk_ref[:, g * SLAB:(g + 1) * SLAB] = dk_acc.astype(BF16)
            dv_ref[:, g * SLAB:(g + 1) * SLAB] = dv_acc.astype(BF16)

    cur = lambda j: (j, 0)
    prev = lambda j: (jnp.maximum(j - 1, 0), 0)
    nxt = lambda j: (jnp.minimum(j + 1, nb - 1), 0)
    cur3 = lambda j: (0, j, 0)
    nxt3 = lambda j: (0, jnp.minimum(j + 1, nb - 1), 0)
    kvw = N_KV_A * SLAB
    return pl.pallas_call(
        body, name="swa_bwd", grid=(nb,),
        in_specs=[pl.BlockSpec(memory_space=pltpu.SMEM),
                  pl.BlockSpec((BLOCK, HM), cur), pl.BlockSpec((BLOCK, HM), nxt),
                  pl.BlockSpec((BLOCK, HM), cur), pl.BlockSpec((BLOCK, HM), nxt),
                  pl.BlockSpec((N_HEADS, BLOCK, 1), cur3), pl.BlockSpec((N_HEADS, BLOCK, 1), nxt3),
                  pl.BlockSpec((N_HEADS, BLOCK, 1), cur3), pl.BlockSpec((N_HEADS, BLOCK, 1), nxt3),
                  pl.BlockSpec((BLOCK, kvw), prev), pl.BlockSpec((BLOCK, kvw), cur),
                  pl.BlockSpec((BLOCK, kvw), prev), pl.BlockSpec((BLOCK, kvw), cur),
                  pl.BlockSpec((BLOCK, 1), cur), pl.BlockSpec((BLOCK, 1), nxt),
                  pl.BlockSpec((1, BLOCK), lambda j: (0, jnp.maximum(j - 1, 0))),
                  pl.BlockSpec((1, BLOCK), lambda j: (0, j))],
        out_specs=[pl.BlockSpec((BLOCK, HM), cur), pl.BlockSpec((BLOCK, kvw), cur),
                   pl.BlockSpec((BLOCK, kvw), cur), pl.BlockSpec((N_HEADS, SLAB), lambda j: (0, 0))],
        out_shape=[jax.ShapeDtypeStruct((t, HM), BF16), jax.ShapeDtypeStruct((t, kvw), BF16),
                   jax.ShapeDtypeStruct((t, kvw), BF16), jax.ShapeDtypeStruct((N_HEADS, SLAB), F32)],
        compiler_params=_params(("arbitrary",)),
    )(sinks, qa, qa, d_oa, d_oa, lse, lse, delta, delta, ka, ka, va, va,
      pos_col, pos_col, pos_row, pos_row)


def _lower_triangle(n):
    row = lax.broadcasted_iota(jnp.int32, (n, n), 0)
    col = lax.broadcasted_iota(jnp.int32, (n, n), 1)
    return row >= col


def _upper_triangle(n):
    row = lax.broadcasted_iota(jnp.int32, (n, n), 0)
    col = lax.broadcasted_iota(jnp.int32, (n, n), 1)
    return row <= col


def _mla_fwd(qb, kb, vb):
    t = qb.shape[0]
    tq = _attn_tile(t)
    nt = t // tq
    hps = MLA_HEADS_PER_STEP
    w = hps * SLAB
    pairs = [(i, j) for i in range(nt) for j in range(i + 1)]
    i_tab = jnp.asarray(np.array([p[0] for p in pairs], np.int32))
    j_tab = jnp.asarray(np.array([p[1] for p in pairs], np.int32))

    def body(it_ref, jt_ref, q_ref, k_ref, vt_ref, o_ref, l_ref, m_s, l_s, acc_s):
        n = pl.program_id(1)
        i, j = it_ref[n], jt_ref[n]

        @pl.when(j == 0)
        def _():
            m_s[...] = jnp.full_like(m_s, NEG)
            l_s[...] = jnp.zeros_like(l_s)
            acc_s[...] = jnp.zeros_like(acc_s)

        def update(masked):
            for hh in range(hps):
                sl = slice(hh * SLAB, (hh + 1) * SLAB)
                s = _dot_nt(k_ref[:, sl], q_ref[:, sl])
                if masked:
                    s = jnp.where(_upper_triangle(tq), s, NEG)
                m_old = m_s[hh]
                m_new = jnp.maximum(m_old, jnp.max(s, axis=0, keepdims=True))
                alpha = jnp.exp2(m_old - m_new)
                p = jnp.exp2(s - m_new)
                l_s[hh] = alpha * l_s[hh] + jnp.sum(p, axis=0, keepdims=True)
                acc_s[sl, :] = alpha * acc_s[sl, :] + _dot(vt_ref[sl, :], p.astype(BF16))
                m_s[hh] = m_new

        @pl.when(j < i)
        def _():
            update(False)

        @pl.when(j == i)
        def _():
            update(True)
            for hh in range(hps):
                sl = slice(hh * SLAB, (hh + 1) * SLAB)
                o_ref[:, sl] = (acc_s[sl, :] / l_s[hh]).T.astype(BF16)
                l_ref[hh] = m_s[hh] + jnp.log2(l_s[hh])

    grid_spec = pltpu.PrefetchScalarGridSpec(
        num_scalar_prefetch=2, grid=(N_HEADS // hps, len(pairs)),
        in_specs=[pl.BlockSpec((tq, w), lambda h, n, it, jt: (it[n], h)),
                  pl.BlockSpec((tq, w), lambda h, n, it, jt: (jt[n], h)),
                  pl.BlockSpec((w, tq), lambda h, n, it, jt: (h, jt[n]))],
        out_specs=[pl.BlockSpec((tq, w), lambda h, n, it, jt: (it[n], h)),
                   pl.BlockSpec((hps, 1, tq), lambda h, n, it, jt: (h, 0, it[n]))],
        scratch_shapes=[pltpu.VMEM((hps, 1, tq), F32), pltpu.VMEM((hps, 1, tq), F32), pltpu.VMEM((w, tq), F32)])
    out, lse = pl.pallas_call(
        body, name="mla_fwd", grid_spec=grid_spec,
        out_shape=[jax.ShapeDtypeStruct((t, HM), BF16), jax.ShapeDtypeStruct((N_HEADS, 1, t), F32)],
        compiler_params=_params(("parallel", "arbitrary")),
    )(i_tab, j_tab, qb, kb, vb.T)
    return out, lse


def _mla_bwd(qb, kb, vb, d_ob, lse, delta):
    t = qb.shape[0]
    tq = _attn_tile(t)
    nt = t // tq
    hps = MLA_HEADS_PER_STEP
    w = hps * SLAB
    pairs = [(j, i) for j in range(nt) for i in range(j, nt)]
    j_tab = jnp.asarray(np.array([p[0] for p in pairs], np.int32))
    i_tab = jnp.asarray(np.array([p[1] for p in pairs], np.int32))

    def body(jt_ref, it_ref, q_ref, qt_ref, do_ref, dot_ref, l_ref, dl_ref, k_ref, kt_ref, v_ref,
             dqt_ref, dkt_ref, dvt_ref, dk_s, dv_s):
        n = pl.program_id(1)
        j, i = jt_ref[n], it_ref[n]

        @pl.when(n == 0)
        def _():
            dqt_ref[...] = jnp.zeros_like(dqt_ref)

        def update(diagonal):
            cols = pl.ds(pl.multiple_of(i * tq, tq), tq)
            for hh in range(hps):
                sl = slice(hh * SLAB, (hh + 1) * SLAB)
                s = _dot_nt(k_ref[:, sl], q_ref[:, sl])
                if diagonal:
                    s = jnp.where(_upper_triangle(tq), s, NEG)
                p = jnp.exp2(s - l_ref[hh])
                ds = (p * (_dot_nt(v_ref[:, sl], do_ref[:, sl]) - dl_ref[hh])).astype(BF16)
                dv = _dot_nt(dot_ref[sl, :], p.astype(BF16))
                dk = _dot_nt(qt_ref[sl, :], ds)
                if diagonal:
                    dv_s[sl, :] = dv
                    dk_s[sl, :] = dk
                else:
                    dv_s[sl, :] += dv
                    dk_s[sl, :] += dk
                dqt_ref[sl, cols] += _dot(kt_ref[sl, :], ds)

        @pl.when(i == j)
        def _():
            update(True)

        @pl.when(i > j)
        def _():
            update(False)

        @pl.when(i == nt - 1)
        def _():
            dkt_ref[...] = (dk_s[...] * (1.0 / LOG2E)).astype(BF16)
            dvt_ref[...] = dv_s[...].astype(BF16)

    grid_spec = pltpu.PrefetchScalarGridSpec(
        num_scalar_prefetch=2, grid=(N_HEADS // hps, len(pairs)),
        in_specs=[pl.BlockSpec((tq, w), lambda h, n, jt, it: (it[n], h)),
                  pl.BlockSpec((w, tq), lambda h, n, jt, it: (h, it[n])),
                  pl.BlockSpec((tq, w), lambda h, n, jt, it: (it[n], h)),
                  pl.BlockSpec((w, tq), lambda h, n, jt, it: (h, it[n])),
                  pl.BlockSpec((hps, 1, tq), lambda h, n, jt, it: (h, 0, it[n])),
                  pl.BlockSpec((hps, 1, tq), lambda h, n, jt, it: (h, 0, it[n])),
                  pl.BlockSpec((tq, w), lambda h, n, jt, it: (jt[n], h)),
                  pl.BlockSpec((w, tq), lambda h, n, jt, it: (h, jt[n])),
                  pl.BlockSpec((tq, w), lambda h, n, jt, it: (jt[n], h))],
        out_specs=[pl.BlockSpec((w, t), lambda h, n, jt, it: (h, 0)),
                   pl.BlockSpec((w, tq), lambda h, n, jt, it: (h, jt[n])),
                   pl.BlockSpec((w, tq), lambda h, n, jt, it: (h, jt[n]))],
        scratch_shapes=[pltpu.VMEM((w, tq), F32), pltpu.VMEM((w, tq), F32)])
    dqt, dkt, dvt = pl.pallas_call(
        body, name="mla_bwd", grid_spec=grid_spec,
        out_shape=[jax.ShapeDtypeStruct((HM, t), F32), jax.ShapeDtypeStruct((HM, t), BF16),
                   jax.ShapeDtypeStruct((HM, t), BF16)],
        compiler_params=_params(("parallel", "arbitrary")),
    )(j_tab, i_tab, qb, qb.T, d_ob, d_ob.T, lse, delta.reshape(N_HEADS, 1, t), kb, kb.T, vb)
    return dqt.T, dkt.T, dvt.T


def _merge_fwd(out_a, out_b, gates, x, w_oa, w_ob, w_out, g2, g3):
    t = x.shape[0]
    tm = _token_tile(t)

    def body(oa_ref, ob_ref, gates_ref, x_ref, woa_ref, wob_ref, wout_ref, g2_ref, g3_ref,
             oap_ref, obp_ref, merged_ref, y_ref, x1_ref, h2_ref):
        oa_p = _dot(oa_ref[...], woa_ref[...])
        ob_p = _dot(ob_ref[...], wob_ref[...])
        oap_ref[...] = oa_p.astype(BF16)
        obp_ref[...] = ob_p.astype(BF16)
        sa = _sigmoid(gates_ref[:, 0:D_MODEL])
        sb = _sigmoid(gates_ref[:, D_MODEL:2 * D_MODEL])
        merged = (sa * oa_p + sb * ob_p).astype(BF16)
        merged_ref[...] = merged
        y = _dot(merged, wout_ref[...])
        y_ref[...] = y
        x1 = x_ref[...] + y * _rms_r(y) * g2_ref[...]
        x1_ref[...] = x1
        h2_ref[...] = (x1 * _rms_r(x1) * g3_ref[...]).astype(BF16)

    def sds(dt):
        return jax.ShapeDtypeStruct((t, D_MODEL), dt)

    row = _row_spec(tm, D_MODEL)
    return pl.pallas_call(
        body, name="merge_fwd", grid=(t // tm,),
        in_specs=[_row_spec(tm, HM), _row_spec(tm, HM), _row_spec(tm, 2 * D_MODEL), row,
                  _full_spec((HM, D_MODEL)), _full_spec((HM, D_MODEL)), _full_spec((D_MODEL, D_MODEL)),
                  _full_spec((1, D_MODEL)), _full_spec((1, D_MODEL))],
        out_specs=[row] * 6,
        out_shape=[sds(BF16), sds(BF16), sds(BF16), sds(F32), sds(F32), sds(BF16)],
        compiler_params=_params(("parallel",)),
    )(out_a, out_b, gates, x, w_oa, w_ob, w_out, g2, g3)


def _merge_bwd(dx1, y, gates, oa_p, ob_p, out_a, out_b, w_oa, w_ob, w_out, g2):
    t = dx1.shape[0]
    tm = _token_tile(t)

    def body(dx1_ref, y_ref, gates_ref, oap_ref, obp_ref, oa_ref, ob_ref, woa_ref, wob_ref, wout_ref, g2_ref,
             dy_ref, doap_ref, dobp_ref, dgates_ref, doa_ref, dob_ref, dla_ref, dlb_ref, dg2_ref):
        dx1v = dx1_ref[...]
        yv = y_ref[...]
        r2 = _rms_r(yv)
        _acc_rows(dg2_ref, dx1v * yv * r2)
        dy = _rms_bwd(yv, r2, g2_ref[...], dx1v).astype(BF16)
        dy_ref[...] = dy
        dm = _dot_nt(dy, wout_ref[...])
        sa = _sigmoid(gates_ref[:, 0:D_MODEL])
        sb = _sigmoid(gates_ref[:, D_MODEL:2 * D_MODEL])
        d_oap = (dm * sa).astype(BF16)
        d_obp = (dm * sb).astype(BF16)
        doap_ref[...] = d_oap
        dobp_ref[...] = d_obp
        dgates_ref[:, 0:D_MODEL] = (dm * oap_ref[...].astype(F32) * sa * (1.0 - sa)).astype(BF16)
        dgates_ref[:, D_MODEL:2 * D_MODEL] = (dm * obp_ref[...].astype(F32) * sb * (1.0 - sb)).astype(BF16)
        d_oa = _dot_nt(d_oap, woa_ref[...])
        d_ob = _dot_nt(d_obp, wob_ref[...])
        doa_ref[...] = d_oa.astype(BF16)
        dob_ref[...] = d_ob.astype(BF16)
        for hd in range(N_HEADS):
            sl = slice(hd * SLAB, (hd + 1) * SLAB)
            dla_ref[hd] = jnp.sum(d_oa[:, sl] * oa_ref[:, sl].astype(F32), axis=-1, keepdims=True)
            dlb_ref[hd] = jnp.sum(d_ob[:, sl] * ob_ref[:, sl].astype(F32), axis=-1, keepdims=True)

    def sds(n, dt):
        return jax.ShapeDtypeStruct((t, n), dt)

    row = _row_spec(tm, D_MODEL)
    head3 = pl.BlockSpec((N_HEADS, tm, 1), lambda i: (0, i, 0))
    return pl.pallas_call(
        body, name="merge_bwd", grid=(t // tm,),
        in_specs=[row, row, _row_spec(tm, 2 * D_MODEL), row, row, _row_spec(tm, HM), _row_spec(tm, HM),
                  _full_spec((HM, D_MODEL)), _full_spec((HM, D_MODEL)), _full_spec((D_MODEL, D_MODEL)),
                  _full_spec((1, D_MODEL))],
        out_specs=[row, row, row, _row_spec(tm, 2 * D_MODEL), _row_spec(tm, HM), _row_spec(tm, HM),
                   head3, head3, _full_spec((1, D_MODEL))],
        out_shape=[sds(D_MODEL, BF16), sds(D_MODEL, BF16), sds(D_MODEL, BF16), sds(2 * D_MODEL, BF16),
                   sds(HM, BF16), sds(HM, BF16),
                   jax.ShapeDtypeStruct((N_HEADS, t, 1), F32), jax.ShapeDtypeStruct((N_HEADS, t, 1), F32),
                   jax.ShapeDtypeStruct((1, D_MODEL), F32)],
        compiler_params=_params(("arbitrary",)),
    )(dx1, y, gates, oa_p, ob_p, out_a, out_b, w_oa, w_ob, w_out, g2)


def _mlp_fwd_bwd(x1, h2, target, w_up, w_down, g3, g4):
    t = x1.shape[0]
    tm = _token_tile(t)

    def body(x1_ref, h2_ref, tgt_ref, wup_ref, wdown_ref, g3_ref, g4_ref,
             a_ref, du_ref, dy2_ref, dx1_ref, loss_ref, dg3_ref, dg4_ref):
        x1v = x1_ref[...]
        u = _dot(h2_ref[...], wup_ref[...])
        ru = jnp.maximum(u, 0.0)
        a = (ru * ru).astype(BF16)
        a_ref[...] = a
        y2 = _dot(a, wdown_ref[...])
        r4 = _rms_r(y2)
        diff = x1v + y2 * r4 * g4_ref[...] - tgt_ref[...]
        _acc_rows(loss_ref, jnp.sum(diff * diff, axis=-1, keepdims=True) * (0.5 / D_MODEL)
                  * jnp.ones((1, SLAB), F32))
        dx2 = diff * (1.0 / D_MODEL)
        _acc_rows(dg4_ref, dx2 * y2 * r4)
        dy2 = _rms_bwd(y2, r4, g4_ref[...], dx2).astype(BF16)
        dy2_ref[...] = dy2
        du = (_dot_nt(dy2, wdown_ref[...]) * (2.0 * ru)).astype(BF16)
        du_ref[...] = du
        dh2 = _dot_nt(du, wup_ref[...])
        r3 = _rms_r(x1v)
        _acc_rows(dg3_ref, dh2 * x1v * r3)
        dx1_ref[...] = dx2 + _rms_bwd(x1v, r3, g3_ref[...], dh2)

    row = _row_spec(tm, D_MODEL)
    frow = _row_spec(tm, D_FF)
    vec = _full_spec((1, D_MODEL))
    return pl.pallas_call(
        body, name="mlp_fwd_bwd", grid=(t // tm,),
        in_specs=[row, row, row, _full_spec((D_MODEL, D_FF)), _full_spec((D_FF, D_MODEL)), vec, vec],
        out_specs=[frow, frow, row, row, _full_spec((1, SLAB)), vec, vec],
        out_shape=[jax.ShapeDtypeStruct((t, D_FF), BF16), jax.ShapeDtypeStruct((t, D_FF), BF16),
                   jax.ShapeDtypeStruct((t, D_MODEL), BF16), jax.ShapeDtypeStruct((t, D_MODEL), F32),
                   jax.ShapeDtypeStruct((1, SLAB), F32), jax.ShapeDtypeStruct((1, D_MODEL), F32),
                   jax.ShapeDtypeStruct((1, D_MODEL), F32)],
        compiler_params=_params(("arbitrary",)),
    )(x1, h2, target, w_up, w_down, g3, g4)


def _inproj_bwd(dgates, dqa, dka, dva, dqb, dkb, dvb, cq, ckv, x, dx1, rope_c, rope_s1, rope_s2,
                g1, g_q, g_kv, w_in, w_qb, w_kvb):
    t = x.shape[0]
    tm = _token_tile(t)

    def body(dgates_ref, dqa_ref, dka_ref, dva_ref, dqb_ref, dkb_ref, dvb_ref, cq_ref, ckv_ref, x_ref, dx1_ref,
             c_ref, s1_ref, s2_ref, g1_ref, gq_ref, gkv_ref, win_ref, wqb_ref, wkvb_ref,
             dproj_ref, dqbr_ref, dkvb_ref, dx_ref, dg1_ref, dgq_ref, dgkv_ref):
        c, s1, s2 = c_ref[...], s1_ref[...], s2_ref[...]
        dk_sum = jnp.zeros((tm, SLAB), F32)
        for hd in range(N_HEADS):
            sl = slice(hd * SLAB, (hd + 1) * SLAB)
            dqbr_ref[:, sl] = _rope_bwd(dqb_ref[:, sl] * SCALE_B, c, s1, s2).astype(BF16)
            dk_sum += dkb_ref[:, sl].astype(F32)
        dkvb_ref[:, 0:HM] = dkb_ref[...]
        dkvb_ref[:, HM:2 * HM] = dvb_ref[...]
        dkr = _rope_bwd(dk_sum, c, s1, s2)
        dcqn = _dot_nt(dqbr_ref[...], wqb_ref[...])
        cq = cq_ref[...]
        rq = _rms_r(cq)
        _acc_rows(dgq_ref, dcqn * cq * rq)
        dcq = _rms_bwd(cq, rq, gq_ref[...], dcqn)
        dckvn = _dot_nt(dkvb_ref[...], wkvb_ref[...])
        ckv = ckv_ref[...]
        rkv = _rms_r(ckv)
        _acc_rows(dgkv_ref, dckvn * ckv * rkv)
        dckv = _rms_bwd(ckv, rkv, gkv_ref[...], dckvn)
        dproj_ref[:, C_GATES:C_QA] = dgates_ref[...]
        dproj_ref[:, C_QA:C_KA] = dqa_ref[...]
        dproj_ref[:, C_KA:C_VA] = dka_ref[...]
        dproj_ref[:, C_VA:C_CQ] = dva_ref[...]
        dproj_ref[:, C_CQ:C_CKV] = dcq.astype(BF16)
        dproj_ref[:, C_CKV:C_KR] = dckv.astype(BF16)
        dproj_ref[:, C_KR:D_IN_PAD] = dkr.astype(BF16)
        dh = _dot_nt(dproj_ref[...], win_ref[...])
        xv = x_ref[...]
        r1 = _rms_r(xv)
        _acc_rows(dg1_ref, dh * xv * r1)
        dx_ref[...] = dx1_ref[...] + _rms_bwd(xv, r1, g1_ref[...], dh)

    kvw = N_KV_A * SLAB
    row = _row_spec(tm, D_MODEL)
    hm = _row_spec(tm, HM)
    tab = _row_spec(tm, SLAB)
    return pl.pallas_call(
        body, name="inproj_bwd", grid=(t // tm,),
        in_specs=[_row_spec(tm, 2 * D_MODEL), hm, _row_spec(tm, kvw), _row_spec(tm, kvw), hm, hm, hm,
                  _row_spec(tm, Q_LORA), _row_spec(tm, KV_LORA), row, row, tab, tab, tab,
                  _full_spec((1, D_MODEL)), _full_spec((1, Q_LORA)), _full_spec((1, KV_LORA)),
                  _full_spec((D_MODEL, D_IN_PAD)), _full_spec((Q_LORA, HM)), _full_spec((KV_LORA, 2 * HM))],
        out_specs=[_row_spec(tm, D_IN_PAD), hm, _row_spec(tm, 2 * HM), row,
                   _full_spec((1, D_MODEL)), _full_spec((1, Q_LORA)), _full_spec((1, KV_LORA))],
        out_shape=[jax.ShapeDtypeStruct((t, D_IN_PAD), BF16), jax.ShapeDtypeStruct((t, HM), BF16),
                   jax.ShapeDtypeStruct((t, 2 * HM), BF16), jax.ShapeDtypeStruct((t, D_MODEL), F32),
                   jax.ShapeDtypeStruct((1, D_MODEL), F32), jax.ShapeDtypeStruct((1, Q_LORA), F32),
                   jax.ShapeDtypeStruct((1, KV_LORA), F32)],
        compiler_params=_params(("arbitrary",)),
    )(dgates, dqa, dka, dva, dqb, dkb, dvb, cq, ckv, x, dx1, rope_c, rope_s1, rope_s2,
      g1, g_q, g_kv, w_in, w_qb, w_kvb)


def _matmul_tn(a, b, name):
    t, k = a.shape
    n = b.shape[1]
    bt = min(t, 512)
    bk = min(k, 1024)
    bn = min(n, 1024)

    def body(a_ref, b_ref, o_ref):
        @pl.when(pl.program_id(2) == 0)
        def _():
            o_ref[...] = jnp.zeros_like(o_ref)
        o_ref[...] += _dot_tn(a_ref[...], b_ref[...])

    return pl.pallas_call(
        body, name=name, grid=(k // bk, n // bn, t // bt),
        in_specs=[pl.BlockSpec((bt, bk), lambda i, j, s: (s, i)), pl.BlockSpec((bt, bn), lambda i, j, s: (s, j))],
        out_specs=pl.BlockSpec((bk, bn), lambda i, j, s: (i, j)),
        out_shape=jax.ShapeDtypeStruct((k, n), F32),
        compiler_params=_params(("parallel", "parallel", "arbitrary")),
    )(a, b)


def _mesh_pos():
    return lax.axis_index("x"), lax.axis_index("y"), lax.axis_index("c")


def _flip(v, bit):
    return 1 - v if bit else v


def _all_gather_flat(flat):
    rows, cols = flat.shape

    def body(x_ref, out_ref, send_sems, recv_sems, local_sem):
        x, y, c = _mesh_pos()
        me, sibling = (x, y, c), (x, y, 1 - c)
        chips = [(1 - x, y), (x, 1 - y), (1 - x, 1 - y)]

        def slot(px, py, pc):
            return out_ref.at[4 * px + 2 * py + pc]

        def copy(k, block, to, src=None):
            return pltpu.make_async_remote_copy(
                src_ref=slot(*block) if src is None else src, dst_ref=slot(*block),
                send_sem=send_sems.at[k], recv_sem=recv_sems.at[k],
                device_id=to, device_id_type=pl.DeviceIdType.MESH)

        mine = pltpu.make_async_copy(x_ref, slot(*me), local_sem)
        mine.start()
        first = [copy(0, me, sibling, src=x_ref)]
        first += [copy(1 + j, me, (*chip, c), src=x_ref) for j, chip in enumerate(chips)]
        for cp in first:
            cp.start()
        passed = [copy(4 + j, (*chip, c), sibling) for j, chip in enumerate(chips)]
        for j, chip in enumerate(chips):
            copy(1 + j, (*chip, c), me).wait_recv()
            passed[j].start()
        copy(0, sibling, me).wait_recv()
        for j, chip in enumerate(chips):
            copy(4 + j, (*chip, 1 - c), me).wait_recv()
        for cp in first + passed:
            cp.wait_send()
        mine.wait()

    return pl.pallas_call(
        body, name="all_gather_weights",
        out_shape=jax.ShapeDtypeStruct((N_DEV, rows, cols), flat.dtype),
        in_specs=[pl.BlockSpec(memory_space=pl.ANY)],
        out_specs=pl.BlockSpec(memory_space=pl.ANY),
        scratch_shapes=[pltpu.SemaphoreType.DMA((7,)), pltpu.SemaphoreType.DMA((7,)), pltpu.SemaphoreType.DMA],
    )(flat)


def _exchange_grads(gflat, small):
    _, rows, cols = gflat.shape

    def body(g_ref, s_ref, grecv_ref, srecv_ref, send_sems, recv_sems, local_sems):
        x, y, c = _mesh_pos()
        me = 4 * x + 2 * y + c
        own_g = pltpu.make_async_copy(g_ref.at[me], grecv_ref.at[me], local_sems.at[0])
        own_s = pltpu.make_async_copy(s_ref, srecv_ref.at[me], local_sems.at[1])
        own_g.start()
        own_s.start()
        copies = []
        for r in range(1, N_DEV):
            px, py, pc = _flip(x, r & 4), _flip(y, r & 2), _flip(c, r & 1)
            peer = 4 * px + 2 * py + pc
            to = (px, py, pc)
            cg = pltpu.make_async_remote_copy(
                src_ref=g_ref.at[peer], dst_ref=grecv_ref.at[me],
                send_sem=send_sems.at[r - 1], recv_sem=recv_sems.at[r - 1],
                device_id=to, device_id_type=pl.DeviceIdType.MESH)
            cs = pltpu.make_async_remote_copy(
                src_ref=s_ref, dst_ref=srecv_ref.at[me],
                send_sem=send_sems.at[6 + r], recv_sem=recv_sems.at[6 + r],
                device_id=to, device_id_type=pl.DeviceIdType.MESH)
            cg.start()
            cs.start()
            copies += [cg, cs]
        for cp in copies:
            cp.wait_recv()
        for cp in copies:
            cp.wait_send()
        own_g.wait()
        own_s.wait()

    return pl.pallas_call(
        body, name="exchange_grads",
        out_shape=[jax.ShapeDtypeStruct((N_DEV, rows, cols), gflat.dtype),
                   jax.ShapeDtypeStruct((N_DEV, SMALL_ROWS, cols), small.dtype)],
        in_specs=[pl.BlockSpec(memory_space=pl.ANY), pl.BlockSpec(memory_space=pl.ANY)],
        out_specs=[pl.BlockSpec(memory_space=pl.ANY), pl.BlockSpec(memory_space=pl.ANY)],
        scratch_shapes=[pltpu.SemaphoreType.DMA((14,)), pltpu.SemaphoreType.DMA((14,)),
                        pltpu.SemaphoreType.DMA((2,))],
    )(gflat, small)


def _adamw(parts, w, m, v, name):
    _, rows, cols = parts.shape
    tr = FLAT_TILE if rows % FLAT_TILE == 0 else rows
    c1 = 1.0 - ADAM_B1 ** ADAM_STEP
    c2 = 1.0 - ADAM_B2 ** ADAM_STEP

    def body(p_ref, w_ref, m_ref, v_ref, g_ref, d_ref, mo_ref, vo_ref):
        g = p_ref[0].astype(F32)
        for s in range(1, N_DEV):
            g = g + p_ref[s].astype(F32)
        g_ref[...] = g
        m_new = ADAM_B1 * m_ref[...] + (1.0 - ADAM_B1) * g
        v_new = ADAM_B2 * v_ref[...] + (1.0 - ADAM_B2) * (g * g)
        mo_ref[...] = m_new
        vo_ref[...] = v_new
        m_hat = m_new / c1
        v_hat = v_new / c2
        d_ref[...] = -ADAM_LR * (m_hat / (jnp.sqrt(v_hat) + ADAM_EPS) + ADAM_WD * w_ref[...])

    row = pl.BlockSpec((tr, cols), lambda i: (i, 0))
    out = jax.ShapeDtypeStruct((rows, cols), F32)
    return pl.pallas_call(
        body, name=name, grid=(rows // tr,),
        in_specs=[pl.BlockSpec((N_DEV, tr, cols), lambda i: (0, i, 0)), row, row, row],
        out_specs=[row] * 4, out_shape=[out] * 4,
        compiler_params=_params(("parallel",)),
    )(parts, w, m, v)


def _pad_heads_cols(w, heads, width):
    k = w.shape[0]
    w = w.reshape(k, heads, width)
    return jnp.pad(w, ((0, 0), (0, 0), (0, SLAB - width))).reshape(k, heads * SLAB)


def _unpad_heads_cols(w, heads, width):
    k = w.shape[0]
    return w.reshape(k, heads, SLAB)[:, :, :width].reshape(k, heads * width)


def _pad_heads_rows(w, heads, width):
    n = w.shape[1]
    w = w.reshape(heads, width, n)
    return jnp.pad(w, ((0, 0), (0, SLAB - width), (0, 0))).reshape(heads * SLAB, n)


def _unpad_heads_rows(w, heads, width):
    n = w.shape[1]
    return w.reshape(heads, SLAB, n)[:, :width, :].reshape(heads * width, n)


def _pad_w_in(w_in):
    o = 2 * D_MODEL
    qa = _pad_heads_cols(w_in[:, o:o + 512], N_HEADS, HEAD_A)
    ka = _pad_heads_cols(w_in[:, o + 512:o + 640], N_KV_A, HEAD_A)
    va = _pad_heads_cols(w_in[:, o + 640:o + 768], N_KV_A, HEAD_A)
    kr = jnp.pad(w_in[:, o + 1152:o + 1184], ((0, 0), (QK_NOPE, SLAB - QK_NOPE - QK_ROPE)))
    return jnp.concatenate([w_in[:, :o], qa, ka, va, w_in[:, o + 768:o + 1152], kr], axis=1)


def _unpad_w_in(w):
    qa = _unpad_heads_cols(w[:, C_QA:C_KA], N_HEADS, HEAD_A)
    ka = _unpad_heads_cols(w[:, C_KA:C_VA], N_KV_A, HEAD_A)
    va = _unpad_heads_cols(w[:, C_VA:C_CQ], N_KV_A, HEAD_A)
    kr = w[:, C_KR + QK_NOPE:C_KR + QK_NOPE + QK_ROPE]
    return jnp.concatenate([w[:, :C_QA], qa, ka, va, w[:, C_CQ:C_KR], kr], axis=1)


def _pad_w_kvb(w_kvb):
    w = w_kvb.reshape(KV_LORA, N_HEADS, QK_NOPE + V_DIM_B)
    k = jnp.pad(w[:, :, :QK_NOPE], ((0, 0), (0, 0), (0, SLAB - QK_NOPE))).reshape(KV_LORA, HM)
    v = jnp.pad(w[:, :, QK_NOPE:], ((0, 0), (0, 0), (0, SLAB - V_DIM_B))).reshape(KV_LORA, HM)
    return jnp.concatenate([k, v], axis=1)


def _unpad_w_kvb(w):
    k = w[:, :HM].reshape(KV_LORA, N_HEADS, SLAB)[:, :, :QK_NOPE]
    v = w[:, HM:].reshape(KV_LORA, N_HEADS, SLAB)[:, :, :V_DIM_B]
    return jnp.concatenate([k, v], axis=2).reshape(KV_LORA, N_HEADS * (QK_NOPE + V_DIM_B))


def _col_shards(w):
    k, n = w.shape
    return w.reshape(k, N_DEV, n // N_DEV).transpose(1, 0, 2).reshape(N_DEV, -1, 1024)


def _row_shards(w):
    return w.reshape(N_DEV, -1, 1024)


def _from_col_shards(s, k, n):
    return s.reshape(N_DEV, k, n // N_DEV).transpose(1, 0, 2).reshape(k, n)


def _flatten_shards(parts):
    pieces = []
    for name, r in FLAT_ROWS:
        p = parts[name]
        pad = [(0, 0)] * (p.ndim - 2) + [(0, -r % FLAT_ALIGN), (0, 0)]
        pieces.append(jnp.pad(p, pad))
    lead = pieces[0].shape[:-2]
    tail = jnp.zeros(lead + (FLAT_R - FLAT_USED, 1024), pieces[0].dtype)
    return jnp.concatenate(pieces + [tail], axis=-2)


def _split_flat(flat):
    out, o = {}, 0
    for name, r in FLAT_ROWS:
        out[name] = flat[..., o:o + r, :]
        o += r + (-r % FLAT_ALIGN)
    return out


def _freq_row():
    freqs = ROPE_THETA ** (-jnp.arange(0, QK_ROPE, 2, dtype=F32) / QK_ROPE)
    return jnp.concatenate([jnp.zeros((QK_NOPE,), F32), freqs, freqs,
                            jnp.zeros((SLAB - QK_NOPE - QK_ROPE,), F32)]).reshape(1, SLAB)


SMALL_D_ROWS = ("pre_norm_mix", "post_norm_mix", "pre_norm_mlp", "post_norm_mlp")
SMALL_Q_OFF, SMALL_KV_OFF, SMALL_SINK_OFF, SMALL_LOSS_OFF = 0, 256, 384, 392


def _pack_small(vals):
    row4 = jnp.concatenate([vals["q_a_norm"].reshape(-1), vals["kv_a_norm"].reshape(-1), vals["sinks"].reshape(-1),
                            vals["loss"].reshape(-1), jnp.zeros((1024 - 393,), F32)])
    rows = [vals[n].reshape(1024) for n in SMALL_D_ROWS] + [row4]
    return jnp.concatenate([jnp.stack(rows), jnp.zeros((SMALL_ROWS - 5, 1024), F32)], axis=0)


def _unpack_small(blk):
    out = {n: blk[i].reshape(1, 1024) for i, n in enumerate(SMALL_D_ROWS)}
    out["q_a_norm"] = blk[4, SMALL_Q_OFF:SMALL_Q_OFF + 256].reshape(1, 256)
    out["kv_a_norm"] = blk[4, SMALL_KV_OFF:SMALL_KV_OFF + 128].reshape(1, 128)
    out["sinks"] = blk[4, SMALL_SINK_OFF:SMALL_SINK_OFF + 8].reshape(1, 8)
    out["loss"] = blk[4, SMALL_LOSS_OFF]
    return out


def _local_step(x, pos, target, w, small):
    t = x.shape[0]
    pos_col = pos.reshape(t, 1)
    pos_row = pos.reshape(1, t)
    w_in = _pad_w_in(w["w_in"])
    w_qb = _pad_heads_cols(w["w_q_b"], N_HEADS, QK_NOPE + QK_ROPE)
    w_kvb = _pad_w_kvb(w["w_kv_b"])
    w_oa = _pad_heads_rows(w["w_o_a"], N_HEADS, HEAD_A)
    w_ob = _pad_heads_rows(w["w_o_b"], N_HEADS, V_DIM_B)
    g1, g2, g3, g4 = (small[n] for n in SMALL_D_ROWS)
    g_q, g_kv = small["q_a_norm"], small["kv_a_norm"]
    sinks = small["sinks"].reshape(N_HEADS)

    rc, rs1, rs2 = _rope_tables(pos_col, _freq_row())
    (h, gates, qa, ka, va, cq, ckv, cqn, ckvn, qb, kb, vb) = _inproj_fwd(
        x, g1, w_in, g_q, w_qb, g_kv, w_kvb, rc, rs1, rs2)
    out_a, lse_a = _swa_fwd(qa, ka, va, pos_col, pos_row, sinks)
    out_b, lse_b = _mla_fwd(qb, kb, vb)
    oa_p, ob_p, merged, y, x1, h2 = _merge_fwd(out_a, out_b, gates, x, w_oa, w_ob, w["w_out"], g2, g3)
    a, du, dy2, dx1, loss, dg3, dg4 = _mlp_fwd_bwd(x1, h2, target, w["w_up"], w["w_down"], g3, g4)
    (dy, d_oap, d_obp, dgates, d_oa, d_ob, delta_a, delta_b, dg2) = _merge_bwd(
        dx1, y, gates, oa_p, ob_p, out_a, out_b, w_oa, w_ob, w["w_out"], g2)
    dqa, dka, dva, dsink = _swa_bwd(qa, ka, va, d_oa, lse_a, delta_a, pos_col, pos_row, sinks)
    dqb, dkb, dvb = _mla_bwd(qb, kb, vb, d_ob, lse_b, delta_b)
    dproj, dqbr, dkvb, dx, dg1, dgq, dgkv = _inproj_bwd(
        dgates, dqa, dka, dva, dqb, dkb, dvb, cq, ckv, x, dx1, rc, rs1, rs2, g1, g_q, g_kv, w_in, w_qb, w_kvb)

    grads = {
        "w_in": _unpad_w_in(_matmul_tn(h, dproj, "dw_in")),
        "w_q_b": _unpad_heads_cols(_matmul_tn(cqn, dqbr, "dw_q_b"), N_HEADS, QK_NOPE + QK_ROPE),
        "w_kv_b": _unpad_w_kvb(_matmul_tn(ckvn, dkvb, "dw_kv_b")),
        "w_o_a": _unpad_heads_rows(_matmul_tn(out_a, d_oap, "dw_o_a"), N_HEADS, HEAD_A),
        "w_o_b": _unpad_heads_rows(_matmul_tn(out_b, d_obp, "dw_o_b"), N_HEADS, V_DIM_B),
        "w_out": _matmul_tn(merged, dy, "dw_out"),
        "w_up": _matmul_tn(h2, du, "dw_up"),
        "w_down": _matmul_tn(a, dy2, "dw_down"),
    }
    small_grads = {"pre_norm_mix": dg1, "post_norm_mix": dg2, "pre_norm_mlp": dg3, "post_norm_mlp": dg4,
                   "q_a_norm": dgq, "kv_a_norm": dgkv, "sinks": dsink[:, 0], "loss": loss[0, 0:1]}
    return dx, grads, small_grads


COL_SHARDED = {"w_in": (D_MODEL, 3232), "w_q_b": (Q_LORA, 768), "w_kv_b": (KV_LORA, 1024),
               "w_o_a": (512, D_MODEL), "w_o_b": (512, D_MODEL), "w_up": (D_MODEL, D_FF)}
ROW_SHARDED = {"w_out": (D_MODEL, D_MODEL), "w_down": (D_FF, D_MODEL)}
WEIGHT_ORDER = ("pre_norm_mix", "w_in", "q_a_norm", "w_q_b", "kv_a_norm", "w_kv_b", "sinks", "w_o_a", "w_o_b",
                "w_out", "post_norm_mix", "pre_norm_mlp", "w_up", "w_down", "post_norm_mlp")
SMALL_NAMES = ("pre_norm_mix", "q_a_norm", "kv_a_norm", "sinks", "post_norm_mix", "pre_norm_mlp", "post_norm_mlp")


def _flat_local(tensors):
    return _flatten_shards({name: tensors[name].reshape(-1, 1024) for name, _ in FLAT_ROWS})


def kernel(x, positions, pre_norm_mix, w_in, q_a_norm, w_q_b, kv_a_norm, w_kv_b, sinks, w_o_a, w_o_b, w_out, post_norm_mix, pre_norm_mlp, w_up, w_down, post_norm_mlp, loss_target, m_pre_norm_mix, m_w_in, m_q_a_norm, m_w_q_b, m_kv_a_norm, m_w_kv_b, m_sinks, m_w_o_a, m_w_o_b, m_w_out, m_post_norm_mix, m_pre_norm_mlp, m_w_up, m_w_down, m_post_norm_mlp, v_pre_norm_mix, v_w_in, v_q_a_norm, v_w_q_b, v_kv_a_norm, v_w_kv_b, v_sinks, v_w_o_a, v_w_o_b, v_w_out, v_post_norm_mix, v_pre_norm_mlp, v_w_up, v_w_down, v_post_norm_mlp):
    weights = dict(pre_norm_mix=pre_norm_mix, w_in=w_in, q_a_norm=q_a_norm, w_q_b=w_q_b, kv_a_norm=kv_a_norm,
                   w_kv_b=w_kv_b, sinks=sinks, w_o_a=w_o_a, w_o_b=w_o_b, w_out=w_out, post_norm_mix=post_norm_mix,
                   pre_norm_mlp=pre_norm_mlp, w_up=w_up, w_down=w_down, post_norm_mlp=post_norm_mlp)
    m_in = dict(pre_norm_mix=m_pre_norm_mix, w_in=m_w_in, q_a_norm=m_q_a_norm, w_q_b=m_w_q_b, kv_a_norm=m_kv_a_norm,
                w_kv_b=m_w_kv_b, sinks=m_sinks, w_o_a=m_w_o_a, w_o_b=m_w_o_b, w_out=m_w_out,
                post_norm_mix=m_post_norm_mix, pre_norm_mlp=m_pre_norm_mlp, w_up=m_w_up, w_down=m_w_down,
                post_norm_mlp=m_post_norm_mlp)
    v_in = dict(pre_norm_mix=v_pre_norm_mix, w_in=v_w_in, q_a_norm=v_q_a_norm, w_q_b=v_w_q_b, kv_a_norm=v_kv_a_norm,
                w_kv_b=v_w_kv_b, sinks=v_sinks, w_o_a=v_w_o_a, w_o_b=v_w_o_b, w_out=v_w_out,
                post_norm_mix=v_post_norm_mix, pre_norm_mlp=v_pre_norm_mlp, w_up=v_w_up, w_down=v_w_down,
                post_norm_mlp=v_post_norm_mlp)

    gathered = _split_flat(_all_gather_flat(_flat_local(weights).astype(BF16)))
    full = {}
    for name, (k, n) in COL_SHARDED.items():
        full[name] = _from_col_shards(gathered[name], k, n)
    for name, (k, n) in ROW_SHARDED.items():
        full[name] = gathered[name].reshape(k, n)
    small = {n: weights[n] for n in SMALL_NAMES}

    dx, grads, small_grads = _local_step(x[0], positions[0], loss_target[0], full, small)

    shards = {name: _col_shards(grads[name]) for name in COL_SHARDED}
    shards.update({name: _row_shards(grads[name]) for name in ROW_SHARDED})
    g_parts, s_parts = _exchange_grads(_flatten_shards(shards).astype(BF16), _pack_small(small_grads))

    g_flat, d_flat, m_flat, v_flat = _adamw(g_parts, _flat_local(weights), _flat_local(m_in), _flat_local(v_in),
                                            "adamw_shards")
    zero = jnp.zeros((), F32)
    pack = lambda src: _pack_small({**{n: src[n] for n in SMALL_NAMES}, "loss": zero})
    g_small, d_small, m_small, v_small = _adamw(s_parts, pack(weights), pack(m_in), pack(v_in), "adamw_small")

    results = []
    for flat, sm in ((g_flat, g_small), (d_flat, d_small), (m_flat, m_small), (v_flat, v_small)):
        big = _split_flat(flat)
        sml = _unpack_small(sm)
        for name in WEIGHT_ORDER:
            if name in sml:
                results.append(sml[name])
            else:
                results.append(big[name].reshape(weights[name].shape))
    loss = _unpack_small(g_small)["loss"]
    return (loss, dx[None], *results)
```

```python
import functools

import numpy as np
import jax
import jax.numpy as jnp
from jax import lax
from jax.experimental import pallas as pl
from jax.experimental.pallas import tpu as pltpu

F32 = jnp.float32
BF16 = jnp.bfloat16

D_MODEL = 1024
D_FF = 4096
N_HEADS = 8
N_KV_A = 2
GROUP_A = N_HEADS // N_KV_A
HEAD_A = 64
QK_NOPE = 64
QK_ROPE = 32
V_DIM_B = 64
Q_LORA = 256
KV_LORA = 128
BLOCK = 128
SLAB = 128
ROPE_THETA = 10000.0
EPS = 1e-6
N_DEV = 8
NEG = -1e30

SCALE_A = HEAD_A ** -0.5
SCALE_B = (QK_NOPE + QK_ROPE) ** -0.5
LOG2E = 1.4426950408889634
SCORE_B = SCALE_B * LOG2E
MLA_HEADS_PER_STEP = 4
SLOPES_A = tuple(2.0 ** (-8.0 * (h + 1) / N_HEADS) for h in range(N_HEADS))

ADAM_LR = 0.001
ADAM_B1 = 0.9
ADAM_B2 = 0.999
ADAM_EPS = 1e-08
ADAM_WD = 0.01
ADAM_STEP = 10

HM = N_HEADS * SLAB
C_GATES = 0
C_QA = 2 * D_MODEL
C_KA = C_QA + HM
C_VA = C_KA + N_KV_A * SLAB
C_CQ = C_VA + N_KV_A * SLAB
C_CKV = C_CQ + Q_LORA
C_KR = C_CKV + KV_LORA
D_IN_PAD = C_KR + SLAB

VMEM_LIMIT = 56 * 1024 * 1024

EARLY_ROWS = (("w_in", 404), ("w_q_b", 24), ("w_kv_b", 16))
LATE_ROWS = (("w_o_a", 64), ("w_o_b", 64), ("w_out", 128), ("w_up", 512), ("w_down", 512))
FLAT_ALIGN = 16
FLAT_TILE = 256
SMALL_ROWS = 8


def _flat_rows(layout):
    used = sum(-(-r // FLAT_ALIGN) * FLAT_ALIGN for _, r in layout)
    return -(-used // FLAT_TILE) * FLAT_TILE


def _token_tile(t):
    return min(256, t)


def _attn_tile(t):
    return 512 if t >= 2048 else 128


def _params(sem, vmem=VMEM_LIMIT):
    return pltpu.CompilerParams(dimension_semantics=sem, vmem_limit_bytes=vmem)


def _dot(a, b):
    return jnp.dot(a, b, preferred_element_type=F32)


def _dot_nt(a, b):
    return lax.dot_general(a, b, (((1,), (1,)), ((), ())), preferred_element_type=F32)


def _dot_tn(a, b):
    return lax.dot_general(a, b, (((0,), (0,)), ((), ())), preferred_element_type=F32)


def _rms_r(x):
    return lax.rsqrt(jnp.mean(x * x, axis=-1, keepdims=True) + EPS)


def _rms_bwd(x, r, g, dy):
    t = dy * g
    return r * t - x * (r * r * r) * jnp.mean(x * t, axis=-1, keepdims=True)


def _sigmoid(x):
    return 1.0 / (1.0 + jnp.exp(-x))


def _rope(x, c, s1, s2):
    return x * c + pltpu.roll(x, SLAB - 16, 1) * s1 + pltpu.roll(x, 16, 1) * s2


def _rope_bwd(d, c, s1, s2):
    return d * c + pltpu.roll(d * s1, 16, 1) + pltpu.roll(d * s2, SLAB - 16, 1)


def _row_spec(tm, n):
    return pl.BlockSpec((tm, n), lambda i: (i, 0))


def _full_spec(shape):
    nd = len(shape)
    return pl.BlockSpec(shape, lambda i: (0,) * nd, pipeline_mode=pl.Buffered(1))


def _acc_rows(ref, val):
    @pl.when(pl.program_id(0) == 0)
    def _():
        ref[...] = jnp.zeros_like(ref)
    ref[...] += jnp.sum(val, axis=0, keepdims=True)


def _rope_tables(pos_col, freq_row):
    t = pos_col.shape[0]
    tm = _token_tile(t)

    def body(pos_ref, f_ref, c_ref, s1_ref, s2_ref):
        ang = pos_ref[...].astype(F32) * f_ref[...]
        lane = lax.broadcasted_iota(jnp.int32, ang.shape, 1)
        s = jnp.sin(ang)
        c_ref[...] = jnp.cos(ang)
        s1_ref[...] = jnp.where((lane >= 64) & (lane < 80), -s, 0.0)
        s2_ref[...] = jnp.where((lane >= 80) & (lane < 96), s, 0.0)

    tab = jax.ShapeDtypeStruct((t, SLAB), F32)
    return pl.pallas_call(
        body, name="rope_tables", grid=(t // tm,),
        in_specs=[_row_spec(tm, 1), _full_spec((1, SLAB))],
        out_specs=[_row_spec(tm, SLAB)] * 3, out_shape=[tab] * 3,
        compiler_params=_params(("parallel",)),
    )(pos_col, freq_row)


def _inproj_fwd(x, g1, w_in, g_q, w_qb, g_kv, w_kvb, rope_c, rope_s1, rope_s2):
    t = x.shape[0]
    tm = _token_tile(t)

    def body(x_ref, g1_ref, win_ref, gq_ref, wqb_ref, gkv_ref, wkvb_ref, c_ref, s1_ref, s2_ref,
             h_ref, gates_ref, qa_ref, ka_ref, va_ref, cq_ref, ckv_ref, cqn_ref, ckvn_ref,
             qb_ref, kb_ref, vb_ref):
        xv = x_ref[...]
        h = (xv * _rms_r(xv) * g1_ref[...]).astype(BF16)
        h_ref[...] = h
        proj = _dot(h, win_ref[...])
        gates_ref[...] = proj[:, C_GATES:C_QA]
        qa_ref[...] = proj[:, C_QA:C_KA].astype(BF16)
        ka_ref[...] = proj[:, C_KA:C_VA].astype(BF16)
        va_ref[...] = proj[:, C_VA:C_CQ].astype(BF16)
        cq = proj[:, C_CQ:C_CKV]
        ckv = proj[:, C_CKV:C_KR]
        kr = proj[:, C_KR:D_IN_PAD]
        cq_ref[...] = cq
        ckv_ref[...] = ckv
        cqn = (cq * _rms_r(cq) * gq_ref[...]).astype(BF16)
        ckvn = (ckv * _rms_r(ckv) * gkv_ref[...]).astype(BF16)
        cqn_ref[...] = cqn
        ckvn_ref[...] = ckvn
        c, s1, s2 = c_ref[...], s1_ref[...], s2_ref[...]
        qb = _dot(cqn, wqb_ref[...])
        kvb = _dot(ckvn, wkvb_ref[...])
        kr_rot = _rope(kr, c, s1, s2)
        for hd in range(N_HEADS):
            sl = slice(hd * SLAB, (hd + 1) * SLAB)
            qb_ref[:, sl] = (_rope(qb[:, sl], c, s1, s2) * SCORE_B).astype(BF16)
            kb_ref[:, sl] = (kvb[:, sl] + kr_rot).astype(BF16)
        vb_ref[...] = kvb[:, HM:2 * HM].astype(BF16)

    def sds(n, dt):
        return jax.ShapeDtypeStruct((t, n), dt)

    outs = [(D_MODEL, BF16), (2 * D_MODEL, F32), (HM, BF16), (N_KV_A * SLAB, BF16), (N_KV_A * SLAB, BF16),
            (Q_LORA, F32), (KV_LORA, F32), (Q_LORA, BF16), (KV_LORA, BF16), (HM, BF16), (HM, BF16), (HM, BF16)]
    return pl.pallas_call(
        body, name="inproj_fwd", grid=(t // tm,),
        in_specs=[_row_spec(tm, D_MODEL), _full_spec((1, D_MODEL)), _full_spec((D_MODEL, D_IN_PAD)),
                  _full_spec((1, Q_LORA)), _full_spec((Q_LORA, HM)), _full_spec((1, KV_LORA)),
                  _full_spec((KV_LORA, 2 * HM)), _row_spec(tm, SLAB), _row_spec(tm, SLAB), _row_spec(tm, SLAB)],
        out_specs=[_row_spec(tm, n) for n, _ in outs],
        out_shape=[sds(n, dt) for n, dt in outs],
        compiler_params=_params(("parallel",)),
    )(x, g1, w_in, g_q, w_qb, g_kv, w_kvb, rope_c, rope_s1, rope_s2)


def _swa_masks():
    row = lax.broadcasted_iota(jnp.int32, (BLOCK, BLOCK), 0)
    col = lax.broadcasted_iota(jnp.int32, (BLOCK, BLOCK), 1)
    return row >= col, col > row


def _swa_fwd(qa, ka, va, pos_col, pos_row, sinks):
    t = qa.shape[0]
    nb = t // BLOCK

    def body(sinks_ref, q_ref, kc_ref, kp_ref, vc_ref, vp_ref, pq_ref, pkc_ref, pkp_ref, o_ref, l_ref):
        i = pl.program_id(0)
        pq = pq_ref[...]
        dist_c = jnp.abs(pq - pkc_ref[...]).astype(F32)
        dist_p = jnp.abs(pq - pkp_ref[...]).astype(F32)
        mask_c, upper = _swa_masks()
        mask_p = jnp.logical_and(upper, i > 0)
        for hd in range(N_HEADS):
            g = hd // GROUP_A
            q = q_ref[:, hd * SLAB:(hd + 1) * SLAB]
            kc = kc_ref[:, g * SLAB:(g + 1) * SLAB]
            kp = kp_ref[:, g * SLAB:(g + 1) * SLAB]
            vc = vc_ref[:, g * SLAB:(g + 1) * SLAB]
            vp = vp_ref[:, g * SLAB:(g + 1) * SLAB]
            sink = sinks_ref[hd]
            s_c = jnp.where(mask_c, _dot_nt(q, kc) * SCALE_A - SLOPES_A[hd] * dist_c, NEG)
            s_p = jnp.where(mask_p, _dot_nt(q, kp) * SCALE_A - SLOPES_A[hd] * dist_p, NEG)
            m = jnp.maximum(jnp.maximum(jnp.max(s_c, axis=-1, keepdims=True),
                                        jnp.max(s_p, axis=-1, keepdims=True)), sink)
            e_c = jnp.exp(s_c - m)
            e_p = jnp.exp(s_p - m)
            den = (jnp.sum(e_c, axis=-1, keepdims=True) + jnp.sum(e_p, axis=-1, keepdims=True)
                   + jnp.exp(sink - m))
            inv = 1.0 / den
            o = _dot((e_c * inv).astype(BF16), vc) + _dot((e_p * inv).astype(BF16), vp)
            o_ref[:, hd * SLAB:(hd + 1) * SLAB] = o.astype(BF16)
            l_ref[hd] = m + jnp.log(den)

    cur = lambda i: (i, 0)
    prev = lambda i: (jnp.maximum(i - 1, 0), 0)
    kvw = N_KV_A * SLAB
    return pl.pallas_call(
        body, name="swa_fwd", grid=(nb,),
        in_specs=[pl.BlockSpec(memory_space=pltpu.SMEM),
                  pl.BlockSpec((BLOCK, HM), cur),
                  pl.BlockSpec((BLOCK, kvw), cur), pl.BlockSpec((BLOCK, kvw), prev),
                  pl.BlockSpec((BLOCK, kvw), cur), pl.BlockSpec((BLOCK, kvw), prev),
                  pl.BlockSpec((BLOCK, 1), cur),
                  pl.BlockSpec((1, BLOCK), lambda i: (0, i)),
                  pl.BlockSpec((1, BLOCK), lambda i: (0, jnp.maximum(i - 1, 0)))],
        out_specs=[pl.BlockSpec((BLOCK, HM), cur), pl.BlockSpec((N_HEADS, BLOCK, 1), lambda i: (0, i, 0))],
        out_shape=[jax.ShapeDtypeStruct((t, HM), BF16), jax.ShapeDtypeStruct((N_HEADS, t, 1), F32)],
        compiler_params=_params(("parallel",)),
    )(sinks, qa, ka, ka, va, va, pos_col, pos_row, pos_row)


def _swa_bwd(qa, ka, va, d_oa, lse, delta, pos_col, pos_row, sinks):
    t = qa.shape[0]
    nb = t // BLOCK

    def body(sinks_ref, q_ref, qn_ref, do_ref, don_ref, l_ref, ln_ref, dl_ref, dln_ref,
             kp_ref, kc_ref, vp_ref, vc_ref, pq_ref, pqn_ref, pkp_ref, pkc_ref,
             dq_ref, dk_ref, dv_ref, dsink_ref):
        j = pl.program_id(0)
        pq, pqn = pq_ref[...], pqn_ref[...]
        pkp, pkc = pkp_ref[...], pkc_ref[...]
        dist_cc = jnp.abs(pq - pkc).astype(F32)
        dist_cp = jnp.abs(pq - pkp).astype(F32)
        dist_nc = jnp.abs(pqn - pkc).astype(F32)
        mask_cc, upper = _swa_masks()
        mask_cp = jnp.logical_and(upper, j > 0)
        mask_nc = jnp.logical_and(upper, j < nb - 1)

        @pl.when(j == 0)
        def _():
            dsink_ref[...] = jnp.zeros_like(dsink_ref)

        def pair(q, do, lq, dl, k, v, dist, mask, slope):
            s = jnp.where(mask, _dot_nt(q, k) * SCALE_A - slope * dist, NEG)
            p = jnp.exp(s - lq)
            ds = p * (_dot_nt(do, v) - dl)
            return p.astype(BF16), ds.astype(BF16)

        for g in range(N_KV_A):
            kc = kc_ref[:, g * SLAB:(g + 1) * SLAB]
            kp = kp_ref[:, g * SLAB:(g + 1) * SLAB]
            vc = vc_ref[:, g * SLAB:(g + 1) * SLAB]
            vp = vp_ref[:, g * SLAB:(g + 1) * SLAB]
            dk_acc = jnp.zeros((BLOCK, SLAB), F32)
            dv_acc = jnp.zeros((BLOCK, SLAB), F32)
            for hh in range(GROUP_A):
                hd = g * GROUP_A + hh
                sl = slice(hd * SLAB, (hd + 1) * SLAB)
                slope = SLOPES_A[hd]
                q, do, lq, dl = q_ref[:, sl], do_ref[:, sl], l_ref[hd], dl_ref[hd]
                qn, don, lqn, dln = qn_ref[:, sl], don_ref[:, sl], ln_ref[hd], dln_ref[hd]
                p_cc, ds_cc = pair(q, do, lq, dl, kc, vc, dist_cc, mask_cc, slope)
                _, ds_cp = pair(q, do, lq, dl, kp, vp, dist_cp, mask_cp, slope)
                p_nc, ds_nc = pair(qn, don, lqn, dln, kc, vc, dist_nc, mask_nc, slope)
                dq_ref[:, sl] = ((_dot(ds_cc, kc) + _dot(ds_cp, kp)) * SCALE_A).astype(BF16)
                dk_acc += (_dot_tn(ds_cc, q) + _dot_tn(ds_nc, qn)) * SCALE_A
                dv_acc += _dot_tn(p_cc, do) + _dot_tn(p_nc, don)
                p_sink = jnp.exp(sinks_ref[hd] - lq)
                dsink_ref[hd:hd + 1, :] += jnp.broadcast_to(-jnp.sum(p_sink * dl), (1, SLAB))
            dk_ref[:, g * SLAB:(g + 1) * SLAB] = dk_acc.astype(BF16)
            dv_ref[:, g * SLAB:(g + 1) * SLAB] = dv_acc.astype(BF16)

    cur = lambda j: (j, 0)
    prev = lambda j: (jnp.maximum(j - 1, 0), 0)
    nxt = lambda j: (jnp.minimum(j + 1, nb - 1), 0)
    cur3 = lambda j: (0, j, 0)
    nxt3 = lambda j: (0, jnp.minimum(j + 1, nb - 1), 0)
    kvw = N_KV_A * SLAB
    return pl.pallas_call(
        body, name="swa_bwd", grid=(nb,),
        in_specs=[pl.BlockSpec(memory_space=pltpu.SMEM),
                  pl.BlockSpec((BLOCK, HM), cur), pl.BlockSpec((BLOCK, HM), nxt),
                  pl.BlockSpec((BLOCK, HM), cur), pl.BlockSpec((BLOCK, HM), nxt),
                  pl.BlockSpec((N_HEADS, BLOCK, 1), cur3), pl.BlockSpec((N_HEADS, BLOCK, 1), nxt3),
                  pl.BlockSpec((N_HEADS, BLOCK, 1), cur3), pl.BlockSpec((N_HEADS, BLOCK, 1), nxt3),
                  pl.BlockSpec((BLOCK, kvw), prev), pl.BlockSpec((BLOCK, kvw), cur),
                  pl.BlockSpec((BLOCK, kvw), prev), pl.BlockSpec((BLOCK, kvw), cur),
                  pl.BlockSpec((BLOCK, 1), cur), pl.BlockSpec((BLOCK, 1), nxt),
                  pl.BlockSpec((1, BLOCK), lambda j: (0, jnp.maximum(j - 1, 0))),
                  pl.BlockSpec((1, BLOCK), lambda j: (0, j))],
        out_specs=[pl.BlockSpec((BLOCK, HM), cur), pl.BlockSpec((BLOCK, kvw), cur),
                   pl.BlockSpec((BLOCK, kvw), cur), pl.BlockSpec((N_HEADS, SLAB), lambda j: (0, 0))],
        out_shape=[jax.ShapeDtypeStruct((t, HM), BF16), jax.ShapeDtypeStruct((t, kvw), BF16),
                   jax.ShapeDtypeStruct((t, kvw), BF16), jax.ShapeDtypeStruct((N_HEADS, SLAB), F32)],
        compiler_params=_params(("arbitrary",)),
    )(sinks, qa, qa, d_oa, d_oa, lse, lse, delta, delta, ka, ka, va, va,
      pos_col, pos_col, pos_row, pos_row)


def _lower_triangle(n):
    row = lax.broadcasted_iota(jnp.int32, (n, n), 0)
    col = lax.broadcasted_iota(jnp.int32, (n, n), 1)
    return row >= col


def _upper_triangle(n):
    row = lax.broadcasted_iota(jnp.int32, (n, n), 0)
    col = lax.broadcasted_iota(jnp.int32, (n, n), 1)
    return row <= col


def _mesh_pos():
    return lax.axis_index("x"), lax.axis_index("y"), lax.axis_index("c")


def _flip(v, bit):
    return 1 - v if bit else v


def _direct_copies(src_ref, dst_ref, send_sems, recv_sems, local_sem, sem_base, gather):
    x, y, c = _mesh_pos()
    me = 4 * x + 2 * y + c
    local = pltpu.make_async_copy(src_ref if gather else src_ref.at[me], dst_ref.at[me], local_sem)
    remote = []
    for r in range(1, N_DEV):
        px, py, pc = _flip(x, r & 4), _flip(y, r & 2), _flip(c, r & 1)
        remote.append(pltpu.make_async_remote_copy(
            src_ref=src_ref if gather else src_ref.at[4 * px + 2 * py + pc], dst_ref=dst_ref.at[me],
            send_sem=send_sems.at[sem_base + r - 1], recv_sem=recv_sems.at[sem_base + r - 1],
            device_id=(px, py, pc), device_id_type=pl.DeviceIdType.MESH))
    return local, remote


def _start_copies(local, remote):
    local.start()
    for cp in remote:
        cp.start()


def _wait_copies(local, remote):
    for cp in remote:
        cp.wait_recv()
    for cp in remote:
        cp.wait_send()
    local.wait()


EXCHANGE_SCRATCH = [pltpu.SemaphoreType.DMA((N_DEV - 1,)), pltpu.SemaphoreType.DMA((N_DEV - 1,)),
                    pltpu.SemaphoreType.DMA]
ANY_SPEC = pl.BlockSpec(memory_space=pl.ANY)


def _mla_fwd(qb, kb, vb, late_flat):
    t = qb.shape[0]
    tq = _attn_tile(t)
    nt = t // tq
    hps = MLA_HEADS_PER_STEP
    w = hps * SLAB
    pairs = [(i, j) for i in range(nt) for j in range(i + 1)]
    i_tab = jnp.asarray(np.array([p[0] for p in pairs], np.int32))
    j_tab = jnp.asarray(np.array([p[1] for p in pairs], np.int32))

    def body(it_ref, jt_ref, q_ref, k_ref, vt_ref, late_ref, o_ref, l_ref, gathered_ref,
             m_s, l_s, acc_s, send_sems, recv_sems, local_sem):
        n = pl.program_id(1)
        i, j = it_ref[n], jt_ref[n]
        first_step = jnp.logical_and(pl.program_id(0) == 0, n == 0)
        last_step = jnp.logical_and(pl.program_id(0) == N_HEADS // hps - 1, n == len(pairs) - 1)

        @pl.when(first_step)
        def _():
            _start_copies(*_direct_copies(late_ref, gathered_ref, send_sems, recv_sems, local_sem, 0, True))

        @pl.when(j == 0)
        def _():
            m_s[...] = jnp.full_like(m_s, NEG)
            l_s[...] = jnp.zeros_like(l_s)
            acc_s[...] = jnp.zeros_like(acc_s)

        def update(masked):
            def scores(hh):
                sl = slice(hh * SLAB, (hh + 1) * SLAB)
                return _dot_nt(k_ref[:, sl], q_ref[:, sl])

            def softmax(hh, s):
                if masked:
                    s = jnp.where(_upper_triangle(tq), s, NEG)
                m_old = m_s[hh]
                m_new = jnp.maximum(m_old, jnp.max(s, axis=0, keepdims=True))
                alpha = jnp.exp2(m_old - m_new)
                p = jnp.exp2(s - m_new)
                l_s[hh] = alpha * l_s[hh] + jnp.sum(p, axis=0, keepdims=True)
                m_s[hh] = m_new
                return p.astype(BF16), alpha

            def accumulate(hh, p, alpha):
                sl = slice(hh * SLAB, (hh + 1) * SLAB)
                acc_s[sl, :] = alpha * acc_s[sl, :] + _dot(vt_ref[sl, :], p)

            s_next, pending = scores(0), None
            for hh in range(hps):
                s = s_next
                if hh + 1 < hps:
                    s_next = scores(hh + 1)
                p, alpha = softmax(hh, s)
                if pending is not None:
                    accumulate(*pending)
                pending = (hh, p, alpha)
            accumulate(*pending)

        @pl.when(j < i)
        def _():
            update(False)

        @pl.when(j == i)
        def _():
            update(True)
            for hh in range(hps):
                sl = slice(hh * SLAB, (hh + 1) * SLAB)
                o_ref[:, sl] = (acc_s[sl, :] / l_s[hh]).T.astype(BF16)
                l_ref[hh] = m_s[hh] + jnp.log2(l_s[hh])

        @pl.when(last_step)
        def _():
            _wait_copies(*_direct_copies(late_ref, gathered_ref, send_sems, recv_sems, local_sem, 0, True))

    grid_spec = pltpu.PrefetchScalarGridSpec(
        num_scalar_prefetch=2, grid=(N_HEADS // hps, len(pairs)),
        in_specs=[pl.BlockSpec((tq, w), lambda h, n, it, jt: (it[n], h)),
                  pl.BlockSpec((tq, w), lambda h, n, it, jt: (jt[n], h)),
                  pl.BlockSpec((w, tq), lambda h, n, it, jt: (h, jt[n])),
                  ANY_SPEC],
        out_specs=[pl.BlockSpec((tq, w), lambda h, n, it, jt: (it[n], h)),
                   pl.BlockSpec((hps, 1, tq), lambda h, n, it, jt: (h, 0, it[n])),
                   ANY_SPEC],
        scratch_shapes=[pltpu.VMEM((hps, 1, tq), F32), pltpu.VMEM((hps, 1, tq), F32), pltpu.VMEM((w, tq), F32)]
        + EXCHANGE_SCRATCH)
    return pl.pallas_call(
        body, name="mla_fwd", grid_spec=grid_spec,
        out_shape=[jax.ShapeDtypeStruct((t, HM), BF16), jax.ShapeDtypeStruct((N_HEADS, 1, t), F32),
                   jax.ShapeDtypeStruct((N_DEV,) + late_flat.shape, late_flat.dtype)],
        compiler_params=_params(("arbitrary", "arbitrary")),
    )(i_tab, j_tab, qb, kb, vb.T, late_flat)


def _mla_bwd(qb, kb, vb, d_ob, lse, delta, grad_slices):
    t = qb.shape[0]
    tq = _attn_tile(t)
    nt = t // tq
    hps = MLA_HEADS_PER_STEP
    w = hps * SLAB
    pairs = [(j, i) for j in range(nt) for i in range(j, nt)]
    j_tab = jnp.asarray(np.array([p[0] for p in pairs], np.int32))
    i_tab = jnp.asarray(np.array([p[1] for p in pairs], np.int32))

    def body(jt_ref, it_ref, q_ref, qt_ref, do_ref, dot_ref, l_ref, dl_ref, k_ref, kt_ref, v_ref, slices_ref,
             dqt_ref, dkt_ref, dvt_ref, parts_ref, dk_s, dv_s, send_sems, recv_sems, local_sem):
        n = pl.program_id(1)
        j, i = jt_ref[n], it_ref[n]
        first_step = jnp.logical_and(pl.program_id(0) == 0, n == 0)
        last_step = jnp.logical_and(pl.program_id(0) == N_HEADS // hps - 1, n == len(pairs) - 1)

        @pl.when(first_step)
        def _():
            _start_copies(*_direct_copies(slices_ref, parts_ref, send_sems, recv_sems, local_sem, 0, False))

        @pl.when(n == 0)
        def _():
            dqt_ref[...] = jnp.zeros_like(dqt_ref)

        def update(diagonal):
            cols = pl.ds(pl.multiple_of(i * tq, tq), tq)

            def products(hh):
                sl = slice(hh * SLAB, (hh + 1) * SLAB)
                return _dot_nt(k_ref[:, sl], q_ref[:, sl]), _dot_nt(v_ref[:, sl], do_ref[:, sl])

            def softmax_bwd(hh, s, dp):
                if diagonal:
                    s = jnp.where(_upper_triangle(tq), s, NEG)
                p = jnp.exp2(s - l_ref[hh])
                return p.astype(BF16), (p * (dp - dl_ref[hh])).astype(BF16)

            def gradients(hh, p, ds):
                sl = slice(hh * SLAB, (hh + 1) * SLAB)
                dv = _dot_nt(dot_ref[sl, :], p)
                dk = _dot_nt(qt_ref[sl, :], ds)
                if diagonal:
                    dv_s[sl, :] = dv
                    dk_s[sl, :] = dk
                else:
                    dv_s[sl, :] += dv
                    dk_s[sl, :] += dk
                dqt_ref[sl, cols] += _dot(kt_ref[sl, :], ds)

            for hh in range(hps):
                gradients(hh, *softmax_bwd(hh, *products(hh)))

        @pl.when(i == j)
        def _():
            update(True)

        @pl.when(i > j)
        def _():
            update(False)

        @pl.when(i == nt - 1)
        def _():
            dkt_ref[...] = (dk_s[...] * (1.0 / LOG2E)).astype(BF16)
            dvt_ref[...] = dv_s[...].astype(BF16)

        @pl.when(last_step)
        def _():
            _wait_copies(*_direct_copies(slices_ref, parts_ref, send_sems, recv_sems, local_sem, 0, False))

    grid_spec = pltpu.PrefetchScalarGridSpec(
        num_scalar_prefetch=2, grid=(N_HEADS // hps, len(pairs)),
        in_specs=[pl.BlockSpec((tq, w), lambda h, n, jt, it: (it[n], h)),
                  pl.BlockSpec((w, tq), lambda h, n, jt, it: (h, it[n])),
                  pl.BlockSpec((tq, w), lambda h, n, jt, it: (it[n], h)),
                  pl.BlockSpec((w, tq), lambda h, n, jt, it: (h, it[n])),
                  pl.BlockSpec((hps, 1, tq), lambda h, n, jt, it: (h, 0, it[n])),
                  pl.BlockSpec((hps, 1, tq), lambda h, n, jt, it: (h, 0, it[n])),
                  pl.BlockSpec((tq, w), lambda h, n, jt, it: (jt[n], h)),
                  pl.BlockSpec((w, tq), lambda h, n, jt, it: (h, jt[n])),
                  pl.BlockSpec((tq, w), lambda h, n, jt, it: (jt[n], h)),
                  ANY_SPEC],
        out_specs=[pl.BlockSpec((w, t), lambda h, n, jt, it: (h, 0)),
                   pl.BlockSpec((w, tq), lambda h, n, jt, it: (h, jt[n])),
                   pl.BlockSpec((w, tq), lambda h, n, jt, it: (h, jt[n])),
                   ANY_SPEC],
        scratch_shapes=[pltpu.VMEM((w, tq), F32), pltpu.VMEM((w, tq), F32)] + EXCHANGE_SCRATCH)
    dqt, dkt, dvt, parts = pl.pallas_call(
        body, name="mla_bwd", grid_spec=grid_spec,
        out_shape=[jax.ShapeDtypeStruct((HM, t), F32), jax.ShapeDtypeStruct((HM, t), BF16),
                   jax.ShapeDtypeStruct((HM, t), BF16),
                   jax.ShapeDtypeStruct(grad_slices.shape, grad_slices.dtype)],
        compiler_params=_params(("arbitrary", "arbitrary")),
    )(j_tab, i_tab, qb, qb.T, d_ob, d_ob.T, lse, delta.reshape(N_HEADS, 1, t), kb, kb.T, vb, grad_slices)
    return dqt.T, dkt.T, dvt.T, parts


def _merge_fwd(out_a, out_b, gates, x, w_oa, w_ob, w_out, g2, g3):
    t = x.shape[0]
    tm = _token_tile(t)

    def body(oa_ref, ob_ref, gates_ref, x_ref, woa_ref, wob_ref, wout_ref, g2_ref, g3_ref,
             oap_ref, obp_ref, merged_ref, y_ref, x1_ref, h2_ref):
        oa_p = _dot(oa_ref[...], woa_ref[...])
        ob_p = _dot(ob_ref[...], wob_ref[...])
        oap_ref[...] = oa_p.astype(BF16)
        obp_ref[...] = ob_p.astype(BF16)
        sa = _sigmoid(gates_ref[:, 0:D_MODEL])
        sb = _sigmoid(gates_ref[:, D_MODEL:2 * D_MODEL])
        merged = (sa * oa_p + sb * ob_p).astype(BF16)
        merged_ref[...] = merged
        y = _dot(merged, wout_ref[...])
        y_ref[...] = y
        x1 = x_ref[...] + y * _rms_r(y) * g2_ref[...]
        x1_ref[...] = x1
        h2_ref[...] = (x1 * _rms_r(x1) * g3_ref[...]).astype(BF16)

    def sds(dt):
        return jax.ShapeDtypeStruct((t, D_MODEL), dt)

    row = _row_spec(tm, D_MODEL)
    return pl.pallas_call(
        body, name="merge_fwd", grid=(t // tm,),
        in_specs=[_row_spec(tm, HM), _row_spec(tm, HM), _row_spec(tm, 2 * D_MODEL), row,
                  _full_spec((HM, D_MODEL)), _full_spec((HM, D_MODEL)), _full_spec((D_MODEL, D_MODEL)),
                  _full_spec((1, D_MODEL)), _full_spec((1, D_MODEL))],
        out_specs=[row] * 6,
        out_shape=[sds(BF16), sds(BF16), sds(BF16), sds(F32), sds(F32), sds(BF16)],
        compiler_params=_params(("parallel",)),
    )(out_a, out_b, gates, x, w_oa, w_ob, w_out, g2, g3)


def _merge_bwd(dx1, y, gates, oa_p, ob_p, out_a, out_b, w_oa, w_ob, w_out, g2):
    t = dx1.shape[0]
    tm = _token_tile(t)

    def body(dx1_ref, y_ref, gates_ref, oap_ref, obp_ref, oa_ref, ob_ref, woa_ref, wob_ref, wout_ref, g2_ref,
             dy_ref, doap_ref, dobp_ref, dgates_ref, doa_ref, dob_ref, dla_ref, dlb_ref, dg2_ref):
        dx1v = dx1_ref[...]
        yv = y_ref[...]
        r2 = _rms_r(yv)
        _acc_rows(dg2_ref, dx1v * yv * r2)
        dy = _rms_bwd(yv, r2, g2_ref[...], dx1v).astype(BF16)
        dy_ref[...] = dy
        dm = _dot_nt(dy, wout_ref[...])
        sa = _sigmoid(gates_ref[:, 0:D_MODEL])
        sb = _sigmoid(gates_ref[:, D_MODEL:2 * D_MODEL])
        d_oap = (dm * sa).astype(BF16)
        d_obp = (dm * sb).astype(BF16)
        doap_ref[...] = d_oap
        dobp_ref[...] = d_obp
        dgates_ref[:, 0:D_MODEL] = (dm * oap_ref[...].astype(F32) * sa * (1.0 - sa)).astype(BF16)
        dgates_ref[:, D_MODEL:2 * D_MODEL] = (dm * obp_ref[...].astype(F32) * sb * (1.0 - sb)).astype(BF16)
        d_oa = _dot_nt(d_oap, woa_ref[...])
        d_ob = _dot_nt(d_obp, wob_ref[...])
        doa_ref[...] = d_oa.astype(BF16)
        dob_ref[...] = d_ob.astype(BF16)
        for hd in range(N_HEADS):
            sl = slice(hd * SLAB, (hd + 1) * SLAB)
            dla_ref[hd] = jnp.sum(d_oa[:, sl] * oa_ref[:, sl].astype(F32), axis=-1, keepdims=True)
            dlb_ref[hd] = jnp.sum(d_ob[:, sl] * ob_ref[:, sl].astype(F32), axis=-1, keepdims=True)

    def sds(n, dt):
        return jax.ShapeDtypeStruct((t, n), dt)

    row = _row_spec(tm, D_MODEL)
    head3 = pl.BlockSpec((N_HEADS, tm, 1), lambda i: (0, i, 0))
    return pl.pallas_call(
        body, name="merge_bwd", grid=(t // tm,),
        in_specs=[row, row, _row_spec(tm, 2 * D_MODEL), row, row, _row_spec(tm, HM), _row_spec(tm, HM),
                  _full_spec((HM, D_MODEL)), _full_spec((HM, D_MODEL)), _full_spec((D_MODEL, D_MODEL)),
                  _full_spec((1, D_MODEL))],
        out_specs=[row, row, row, _row_spec(tm, 2 * D_MODEL), _row_spec(tm, HM), _row_spec(tm, HM),
                   head3, head3, _full_spec((1, D_MODEL))],
        out_shape=[sds(D_MODEL, BF16), sds(D_MODEL, BF16), sds(D_MODEL, BF16), sds(2 * D_MODEL, BF16),
                   sds(HM, BF16), sds(HM, BF16),
                   jax.ShapeDtypeStruct((N_HEADS, t, 1), F32), jax.ShapeDtypeStruct((N_HEADS, t, 1), F32),
                   jax.ShapeDtypeStruct((1, D_MODEL), F32)],
        compiler_params=_params(("arbitrary",)),
    )(dx1, y, gates, oa_p, ob_p, out_a, out_b, w_oa, w_ob, w_out, g2)


def _mlp_fwd_bwd(x1, h2, target, w_up, w_down, g3, g4):
    t = x1.shape[0]
    tm = _token_tile(t)

    def body(x1_ref, h2_ref, tgt_ref, wup_ref, wdown_ref, g3_ref, g4_ref,
             a_ref, du_ref, dy2_ref, dx1_ref, loss_ref, dg3_ref, dg4_ref):
        x1v = x1_ref[...]
        u = _dot(h2_ref[...], wup_ref[...])
        ru = jnp.maximum(u, 0.0)
        a = (ru * ru).astype(BF16)
        a_ref[...] = a
        y2 = _dot(a, wdown_ref[...])
        r4 = _rms_r(y2)
        diff = x1v + y2 * r4 * g4_ref[...] - tgt_ref[...]
        _acc_rows(loss_ref, jnp.sum(diff * diff, axis=-1, keepdims=True) * (0.5 / D_MODEL)
                  * jnp.ones((1, SLAB), F32))
        dx2 = diff * (1.0 / D_MODEL)
        _acc_rows(dg4_ref, dx2 * y2 * r4)
        dy2 = _rms_bwd(y2, r4, g4_ref[...], dx2).astype(BF16)
        dy2_ref[...] = dy2
        du = (_dot_nt(dy2, wdown_ref[...]) * (2.0 * ru)).astype(BF16)
        du_ref[...] = du
        dh2 = _dot_nt(du, wup_ref[...])
        r3 = _rms_r(x1v)
        _acc_rows(dg3_ref, dh2 * x1v * r3)
        dx1_ref[...] = dx2 + _rms_bwd(x1v, r3, g3_ref[...], dh2)

    row = _row_spec(tm, D_MODEL)
    frow = _row_spec(tm, D_FF)
    vec = _full_spec((1, D_MODEL))
    return pl.pallas_call(
        body, name="mlp_fwd_bwd", grid=(t // tm,),
        in_specs=[row, row, row, _full_spec((D_MODEL, D_FF)), _full_spec((D_FF, D_MODEL)), vec, vec],
        out_specs=[frow, frow, row, row, _full_spec((1, SLAB)), vec, vec],
        out_shape=[jax.ShapeDtypeStruct((t, D_FF), BF16), jax.ShapeDtypeStruct((t, D_FF), BF16),
                   jax.ShapeDtypeStruct((t, D_MODEL), BF16), jax.ShapeDtypeStruct((t, D_MODEL), F32),
                   jax.ShapeDtypeStruct((1, SLAB), F32), jax.ShapeDtypeStruct((1, D_MODEL), F32),
                   jax.ShapeDtypeStruct((1, D_MODEL), F32)],
        compiler_params=_params(("arbitrary",)),
    )(x1, h2, target, w_up, w_down, g3, g4)


def _inproj_bwd(dgates, dqa, dka, dva, dqb, dkb, dvb, cq, ckv, x, dx1, rope_c, rope_s1, rope_s2,
                g1, g_q, g_kv, w_in, w_qb, w_kvb):
    t = x.shape[0]
    tm = _token_tile(t)

    def body(dgates_ref, dqa_ref, dka_ref, dva_ref, dqb_ref, dkb_ref, dvb_ref, cq_ref, ckv_ref, x_ref, dx1_ref,
             c_ref, s1_ref, s2_ref, g1_ref, gq_ref, gkv_ref, win_ref, wqb_ref, wkvb_ref,
             dproj_ref, dqbr_ref, dkvb_ref, dx_ref, dg1_ref, dgq_ref, dgkv_ref):
        c, s1, s2 = c_ref[...], s1_ref[...], s2_ref[...]
        dk_sum = jnp.zeros((tm, SLAB), F32)
        for hd in range(N_HEADS):
            sl = slice(hd * SLAB, (hd + 1) * SLAB)
            dqbr_ref[:, sl] = _rope_bwd(dqb_ref[:, sl] * SCALE_B, c, s1, s2).astype(BF16)
            dk_sum += dkb_ref[:, sl].astype(F32)
        dkvb_ref[:, 0:HM] = dkb_ref[...]
        dkvb_ref[:, HM:2 * HM] = dvb_ref[...]
        dkr = _rope_bwd(dk_sum, c, s1, s2)
        dcqn = _dot_nt(dqbr_ref[...], wqb_ref[...])
        cq = cq_ref[...]
        rq = _rms_r(cq)
        _acc_rows(dgq_ref, dcqn * cq * rq)
        dcq = _rms_bwd(cq, rq, gq_ref[...], dcqn)
        dckvn = _dot_nt(dkvb_ref[...], wkvb_ref[...])
        ckv = ckv_ref[...]
        rkv = _rms_r(ckv)
        _acc_rows(dgkv_ref, dckvn * ckv * rkv)
        dckv = _rms_bwd(ckv, rkv, gkv_ref[...], dckvn)
        dproj_ref[:, C_GATES:C_QA] = dgates_ref[...]
        dproj_ref[:, C_QA:C_KA] = dqa_ref[...]
        dproj_ref[:, C_KA:C_VA] = dka_ref[...]
        dproj_ref[:, C_VA:C_CQ] = dva_ref[...]
        dproj_ref[:, C_CQ:C_CKV] = dcq.astype(BF16)
        dproj_ref[:, C_CKV:C_KR] = dckv.astype(BF16)
        dproj_ref[:, C_KR:D_IN_PAD] = dkr.astype(BF16)
        dh = _dot_nt(dproj_ref[...], win_ref[...])
        xv = x_ref[...]
        r1 = _rms_r(xv)
        _acc_rows(dg1_ref, dh * xv * r1)
        dx_ref[...] = dx1_ref[...] + _rms_bwd(xv, r1, g1_ref[...], dh)

    kvw = N_KV_A * SLAB
    row = _row_spec(tm, D_MODEL)
    hm = _row_spec(tm, HM)
    tab = _row_spec(tm, SLAB)
    return pl.pallas_call(
        body, name="inproj_bwd", grid=(t // tm,),
        in_specs=[_row_spec(tm, 2 * D_MODEL), hm, _row_spec(tm, kvw), _row_spec(tm, kvw), hm, hm, hm,
                  _row_spec(tm, Q_LORA), _row_spec(tm, KV_LORA), row, row, tab, tab, tab,
                  _full_spec((1, D_MODEL)), _full_spec((1, Q_LORA)), _full_spec((1, KV_LORA)),
                  _full_spec((D_MODEL, D_IN_PAD)), _full_spec((Q_LORA, HM)), _full_spec((KV_LORA, 2 * HM))],
        out_specs=[_row_spec(tm, D_IN_PAD), hm, _row_spec(tm, 2 * HM), row,
                   _full_spec((1, D_MODEL)), _full_spec((1, Q_LORA)), _full_spec((1, KV_LORA))],
        out_shape=[jax.ShapeDtypeStruct((t, D_IN_PAD), BF16), jax.ShapeDtypeStruct((t, HM), BF16),
                   jax.ShapeDtypeStruct((t, 2 * HM), BF16), jax.ShapeDtypeStruct((t, D_MODEL), F32),
                   jax.ShapeDtypeStruct((1, D_MODEL), F32), jax.ShapeDtypeStruct((1, Q_LORA), F32),
                   jax.ShapeDtypeStruct((1, KV_LORA), F32)],
        compiler_params=_params(("arbitrary",)),
    )(dgates, dqa, dka, dva, dqb, dkb, dvb, cq, ckv, x, dx1, rope_c, rope_s1, rope_s2,
      g1, g_q, g_kv, w_in, w_qb, w_kvb)


def _matmul_tn(a, b, name):
    t, k = a.shape
    n = b.shape[1]
    bt = min(t, 512)
    bk = min(k, 1024)
    bn = min(n, 1024)

    def body(a_ref, b_ref, o_ref):
        @pl.when(pl.program_id(2) == 0)
        def _():
            o_ref[...] = jnp.zeros_like(o_ref)
        o_ref[...] += _dot_tn(a_ref[...], b_ref[...])

    return pl.pallas_call(
        body, name=name, grid=(k // bk, n // bn, t // bt),
        in_specs=[pl.BlockSpec((bt, bk), lambda i, j, s: (s, i)), pl.BlockSpec((bt, bn), lambda i, j, s: (s, j))],
        out_specs=pl.BlockSpec((bk, bn), lambda i, j, s: (i, j)),
        out_shape=jax.ShapeDtypeStruct((k, n), F32),
        compiler_params=_params(("parallel", "parallel", "arbitrary")),
    )(a, b)


def _all_gather_flat(flat):
    rows, cols = flat.shape

    def body(x_ref, out_ref, send_sems, recv_sems, local_sem):
        x, y, c = _mesh_pos()
        me, sibling = (x, y, c), (x, y, 1 - c)
        chips = [(1 - x, y), (x, 1 - y), (1 - x, 1 - y)]

        def slot(px, py, pc):
            return out_ref.at[4 * px + 2 * py + pc]

        def copy(k, block, to, src=None):
            return pltpu.make_async_remote_copy(
                src_ref=slot(*block) if src is None else src, dst_ref=slot(*block),
                send_sem=send_sems.at[k], recv_sem=recv_sems.at[k],
                device_id=to, device_id_type=pl.DeviceIdType.MESH)

        mine = pltpu.make_async_copy(x_ref, slot(*me), local_sem)
        mine.start()
        first = [copy(0, me, sibling, src=x_ref)]
        first += [copy(1 + j, me, (*chip, c), src=x_ref) for j, chip in enumerate(chips)]
        for cp in first:
            cp.start()
        passed = [copy(4 + j, (*chip, c), sibling) for j, chip in enumerate(chips)]
        for j, chip in enumerate(chips):
            copy(1 + j, (*chip, c), me).wait_recv()
            passed[j].start()
        copy(0, sibling, me).wait_recv()
        for j, chip in enumerate(chips):
            copy(4 + j, (*chip, 1 - c), me).wait_recv()
        for cp in first + passed:
            cp.wait_send()
        mine.wait()

    return pl.pallas_call(
        body, name="all_gather_weights",
        out_shape=jax.ShapeDtypeStruct((N_DEV, rows, cols), flat.dtype),
        in_specs=[pl.BlockSpec(memory_space=pl.ANY)],
        out_specs=pl.BlockSpec(memory_space=pl.ANY),
        scratch_shapes=[pltpu.SemaphoreType.DMA((7,)), pltpu.SemaphoreType.DMA((7,)), pltpu.SemaphoreType.DMA],
    )(flat)


def _exchange_grads(gflat, small):
    _, rows, cols = gflat.shape

    def body(g_ref, s_ref, grecv_ref, srecv_ref, send_sems, recv_sems, local_sems):
        slices = _direct_copies(g_ref, grecv_ref, send_sems, recv_sems, local_sems.at[0], 0, False)
        smalls = _direct_copies(s_ref, srecv_ref, send_sems, recv_sems, local_sems.at[1], N_DEV - 1, True)
        _start_copies(*slices)
        _start_copies(*smalls)
        _wait_copies(*slices)
        _wait_copies(*smalls)

    return pl.pallas_call(
        body, name="exchange_grads",
        out_shape=[jax.ShapeDtypeStruct((N_DEV, rows, cols), gflat.dtype),
                   jax.ShapeDtypeStruct((N_DEV, SMALL_ROWS, cols), small.dtype)],
        in_specs=[pl.BlockSpec(memory_space=pl.ANY), pl.BlockSpec(memory_space=pl.ANY)],
        out_specs=[pl.BlockSpec(memory_space=pl.ANY), pl.BlockSpec(memory_space=pl.ANY)],
        scratch_shapes=[pltpu.SemaphoreType.DMA((14,)), pltpu.SemaphoreType.DMA((14,)),
                        pltpu.SemaphoreType.DMA((2,))],
    )(gflat, small)


def _adamw(parts, w, m, v, name):
    _, rows, cols = parts.shape
    tr = FLAT_TILE if rows % FLAT_TILE == 0 else rows
    c1 = 1.0 - ADAM_B1 ** ADAM_STEP
    c2 = 1.0 - ADAM_B2 ** ADAM_STEP

    def body(p_ref, w_ref, m_ref, v_ref, g_ref, d_ref, mo_ref, vo_ref):
        g = p_ref[0].astype(F32)
        for s in range(1, N_DEV):
            g = g + p_ref[s].astype(F32)
        g_ref[...] = g
        m_new = ADAM_B1 * m_ref[...] + (1.0 - ADAM_B1) * g
        v_new = ADAM_B2 * v_ref[...] + (1.0 - ADAM_B2) * (g * g)
        mo_ref[...] = m_new
        vo_ref[...] = v_new
        m_hat = m_new / c1
        v_hat = v_new / c2
        d_ref[...] = -ADAM_LR * (m_hat / (jnp.sqrt(v_hat) + ADAM_EPS) + ADAM_WD * w_ref[...])

    row = pl.BlockSpec((tr, cols), lambda i: (i, 0))
    out = jax.ShapeDtypeStruct((rows, cols), F32)
    return pl.pallas_call(
        body, name=name, grid=(rows // tr,),
        in_specs=[pl.BlockSpec((N_DEV, tr, cols), lambda i: (0, i, 0)), row, row, row],
        out_specs=[row] * 4, out_shape=[out] * 4,
        compiler_params=_params(("parallel",)),
    )(parts, w, m, v)


def _pad_heads_cols(w, heads, width):
    k = w.shape[0]
    w = w.reshape(k, heads, width)
    return jnp.pad(w, ((0, 0), (0, 0), (0, SLAB - width))).reshape(k, heads * SLAB)


def _unpad_heads_cols(w, heads, width):
    k = w.shape[0]
    return w.reshape(k, heads, SLAB)[:, :, :width].reshape(k, heads * width)


def _pad_heads_rows(w, heads, width):
    n = w.shape[1]
    w = w.reshape(heads, width, n)
    return jnp.pad(w, ((0, 0), (0, SLAB - width), (0, 0))).reshape(heads * SLAB, n)


def _unpad_heads_rows(w, heads, width):
    n = w.shape[1]
    return w.reshape(heads, SLAB, n)[:, :width, :].reshape(heads * width, n)


def _pad_w_in(w_in):
    o = 2 * D_MODEL
    qa = _pad_heads_cols(w_in[:, o:o + 512], N_HEADS, HEAD_A)
    ka = _pad_heads_cols(w_in[:, o + 512:o + 640], N_KV_A, HEAD_A)
    va = _pad_heads_cols(w_in[:, o + 640:o + 768], N_KV_A, HEAD_A)
    kr = jnp.pad(w_in[:, o + 1152:o + 1184], ((0, 0), (QK_NOPE, SLAB - QK_NOPE - QK_ROPE)))
    return jnp.concatenate([w_in[:, :o], qa, ka, va, w_in[:, o + 768:o + 1152], kr], axis=1)


def _unpad_w_in(w):
    qa = _unpad_heads_cols(w[:, C_QA:C_KA], N_HEADS, HEAD_A)
    ka = _unpad_heads_cols(w[:, C_KA:C_VA], N_KV_A, HEAD_A)
    va = _unpad_heads_cols(w[:, C_VA:C_CQ], N_KV_A, HEAD_A)
    kr = w[:, C_KR + QK_NOPE:C_KR + QK_NOPE + QK_ROPE]
    return jnp.concatenate([w[:, :C_QA], qa, ka, va, w[:, C_CQ:C_KR], kr], axis=1)


def _pad_w_kvb(w_kvb):
    w = w_kvb.reshape(KV_LORA, N_HEADS, QK_NOPE + V_DIM_B)
    k = jnp.pad(w[:, :, :QK_NOPE], ((0, 0), (0, 0), (0, SLAB - QK_NOPE))).reshape(KV_LORA, HM)
    v = jnp.pad(w[:, :, QK_NOPE:], ((0, 0), (0, 0), (0, SLAB - V_DIM_B))).reshape(KV_LORA, HM)
    return jnp.concatenate([k, v], axis=1)


def _unpad_w_kvb(w):
    k = w[:, :HM].reshape(KV_LORA, N_HEADS, SLAB)[:, :, :QK_NOPE]
    v = w[:, HM:].reshape(KV_LORA, N_HEADS, SLAB)[:, :, :V_DIM_B]
    return jnp.concatenate([k, v], axis=2).reshape(KV_LORA, N_HEADS * (QK_NOPE + V_DIM_B))


def _col_shards(w):
    k, n = w.shape
    return w.reshape(k, N_DEV, n // N_DEV).transpose(1, 0, 2).reshape(N_DEV, -1, 1024)


def _row_shards(w):
    return w.reshape(N_DEV, -1, 1024)


def _from_col_shards(s, k, n):
    return s.reshape(N_DEV, k, n // N_DEV).transpose(1, 0, 2).reshape(k, n)


def _flatten_shards(parts, layout):
    pieces, used = [], 0
    for name, r in layout:
        p = parts[name]
        pad = [(0, 0)] * (p.ndim - 2) + [(0, -r % FLAT_ALIGN), (0, 0)]
        pieces.append(jnp.pad(p, pad))
        used += r + (-r % FLAT_ALIGN)
    tail = _flat_rows(layout) - used
    if tail:
        pieces.append(jnp.zeros(pieces[0].shape[:-2] + (tail, 1024), pieces[0].dtype))
    return jnp.concatenate(pieces, axis=-2)


def _split_flat(flat, layout):
    out, o = {}, 0
    for name, r in layout:
        out[name] = flat[..., o:o + r, :]
        o += r + (-r % FLAT_ALIGN)
    return out


def _freq_row():
    freqs = ROPE_THETA ** (-jnp.arange(0, QK_ROPE, 2, dtype=F32) / QK_ROPE)
    return jnp.concatenate([jnp.zeros((QK_NOPE,), F32), freqs, freqs,
                            jnp.zeros((SLAB - QK_NOPE - QK_ROPE,), F32)]).reshape(1, SLAB)


SMALL_D_ROWS = ("pre_norm_mix", "post_norm_mix", "pre_norm_mlp", "post_norm_mlp")
SMALL_Q_OFF, SMALL_KV_OFF, SMALL_SINK_OFF, SMALL_LOSS_OFF = 0, 256, 384, 392


def _pack_small(vals):
    row4 = jnp.concatenate([vals["q_a_norm"].reshape(-1), vals["kv_a_norm"].reshape(-1), vals["sinks"].reshape(-1),
                            vals["loss"].reshape(-1), jnp.zeros((1024 - 393,), F32)])
    rows = [vals[n].reshape(1024) for n in SMALL_D_ROWS] + [row4]
    return jnp.concatenate([jnp.stack(rows), jnp.zeros((SMALL_ROWS - 5, 1024), F32)], axis=0)


def _unpack_small(blk):
    out = {n: blk[i].reshape(1, 1024) for i, n in enumerate(SMALL_D_ROWS)}
    out["q_a_norm"] = blk[4, SMALL_Q_OFF:SMALL_Q_OFF + 256].reshape(1, 256)
    out["kv_a_norm"] = blk[4, SMALL_KV_OFF:SMALL_KV_OFF + 128].reshape(1, 128)
    out["sinks"] = blk[4, SMALL_SINK_OFF:SMALL_SINK_OFF + 8].reshape(1, 8)
    out["loss"] = blk[4, SMALL_LOSS_OFF]
    return out


COL_SHARDED = {"w_in": (D_MODEL, 3232), "w_q_b": (Q_LORA, 768), "w_kv_b": (KV_LORA, 1024),
               "w_o_a": (512, D_MODEL), "w_o_b": (512, D_MODEL), "w_up": (D_MODEL, D_FF)}
ROW_SHARDED = {"w_out": (D_MODEL, D_MODEL), "w_down": (D_FF, D_MODEL)}
WEIGHT_ORDER = ("pre_norm_mix", "w_in", "q_a_norm", "w_q_b", "kv_a_norm", "w_kv_b", "sinks", "w_o_a", "w_o_b",
                "w_out", "post_norm_mix", "pre_norm_mlp", "w_up", "w_down", "post_norm_mlp")
SMALL_NAMES = ("pre_norm_mix", "q_a_norm", "kv_a_norm", "sinks", "post_norm_mix", "pre_norm_mlp", "post_norm_mlp")


def _flat_local(tensors, layout):
    return _flatten_shards({name: tensors[name].reshape(-1, 1024) for name, _ in layout}, layout)


def _full_weights(gathered, layout):
    out = {}
    for name, s in _split_flat(gathered, layout).items():
        k, n = COL_SHARDED.get(name) or ROW_SHARDED[name]
        out[name] = _from_col_shards(s, k, n) if name in COL_SHARDED else s.reshape(k, n)
    return out


def _grad_slices(grads, layout):
    shards = {name: (_col_shards(grads[name]) if name in COL_SHARDED else _row_shards(grads[name]))
              for name, _ in layout}
    return _flatten_shards(shards, layout).astype(BF16)


def kernel(x, positions, pre_norm_mix, w_in, q_a_norm, w_q_b, kv_a_norm, w_kv_b, sinks, w_o_a, w_o_b, w_out, post_norm_mix, pre_norm_mlp, w_up, w_down, post_norm_mlp, loss_target, m_pre_norm_mix, m_w_in, m_q_a_norm, m_w_q_b, m_kv_a_norm, m_w_kv_b, m_sinks, m_w_o_a, m_w_o_b, m_w_out, m_post_norm_mix, m_pre_norm_mlp, m_w_up, m_w_down, m_post_norm_mlp, v_pre_norm_mix, v_w_in, v_q_a_norm, v_w_q_b, v_kv_a_norm, v_w_kv_b, v_sinks, v_w_o_a, v_w_o_b, v_w_out, v_post_norm_mix, v_pre_norm_mlp, v_w_up, v_w_down, v_post_norm_mlp):
    weights = dict(pre_norm_mix=pre_norm_mix, w_in=w_in, q_a_norm=q_a_norm, w_q_b=w_q_b, kv_a_norm=kv_a_norm,
                   w_kv_b=w_kv_b, sinks=sinks, w_o_a=w_o_a, w_o_b=w_o_b, w_out=w_out, post_norm_mix=post_norm_mix,
                   pre_norm_mlp=pre_norm_mlp, w_up=w_up, w_down=w_down, post_norm_mlp=post_norm_mlp)
    m_in = dict(pre_norm_mix=m_pre_norm_mix, w_in=m_w_in, q_a_norm=m_q_a_norm, w_q_b=m_w_q_b, kv_a_norm=m_kv_a_norm,
                w_kv_b=m_w_kv_b, sinks=m_sinks, w_o_a=m_w_o_a, w_o_b=m_w_o_b, w_out=m_w_out,
                post_norm_mix=m_post_norm_mix, pre_norm_mlp=m_pre_norm_mlp, w_up=m_w_up, w_down=m_w_down,
                post_norm_mlp=m_post_norm_mlp)
    v_in = dict(pre_norm_mix=v_pre_norm_mix, w_in=v_w_in, q_a_norm=v_q_a_norm, w_q_b=v_w_q_b, kv_a_norm=v_kv_a_norm,
                w_kv_b=v_w_kv_b, sinks=v_sinks, w_o_a=v_w_o_a, w_o_b=v_w_o_b, w_out=v_w_out,
                post_norm_mix=v_post_norm_mix, pre_norm_mlp=v_pre_norm_mlp, w_up=v_w_up, w_down=v_w_down,
                post_norm_mlp=v_post_norm_mlp)

    xs, pos, target = x[0], positions[0], loss_target[0]
    t = xs.shape[0]
    pos_col = pos.reshape(t, 1)
    pos_row = pos.reshape(1, t)
    g1, g2, g3, g4 = (weights[n] for n in SMALL_D_ROWS)
    g_q, g_kv = q_a_norm, kv_a_norm
    sink_vec = sinks.reshape(N_HEADS)

    early = _full_weights(_all_gather_flat(_flat_local(weights, EARLY_ROWS).astype(BF16)), EARLY_ROWS)
    w_in_p = _pad_w_in(early["w_in"])
    w_qb = _pad_heads_cols(early["w_q_b"], N_HEADS, QK_NOPE + QK_ROPE)
    w_kvb = _pad_w_kvb(early["w_kv_b"])

    rc, rs1, rs2 = _rope_tables(pos_col, _freq_row())
    (h, gates, qa, ka, va, cq, ckv, cqn, ckvn, qb, kb, vb) = _inproj_fwd(
        xs, g1, w_in_p, g_q, w_qb, g_kv, w_kvb, rc, rs1, rs2)
    out_a, lse_a = _swa_fwd(qa, ka, va, pos_col, pos_row, sink_vec)
    out_b, lse_b, late_all = _mla_fwd(qb, kb, vb, _flat_local(weights, LATE_ROWS).astype(BF16))
    late = _full_weights(late_all, LATE_ROWS)
    w_oa = _pad_heads_rows(late["w_o_a"], N_HEADS, HEAD_A)
    w_ob = _pad_heads_rows(late["w_o_b"], N_HEADS, V_DIM_B)

    oa_p, ob_p, merged, y, x1, h2 = _merge_fwd(out_a, out_b, gates, xs, w_oa, w_ob, late["w_out"], g2, g3)
    a, du, dy2, dx1, loss, dg3, dg4 = _mlp_fwd_bwd(x1, h2, target, late["w_up"], late["w_down"], g3, g4)
    (dy, d_oap, d_obp, dgates, d_oa, d_ob, delta_a, delta_b, dg2) = _merge_bwd(
        dx1, y, gates, oa_p, ob_p, out_a, out_b, w_oa, w_ob, late["w_out"], g2)
    late_grads = {
        "w_o_a": _unpad_heads_rows(_matmul_tn(out_a, d_oap, "dw_o_a"), N_HEADS, HEAD_A),
        "w_o_b": _unpad_heads_rows(_matmul_tn(out_b, d_obp, "dw_o_b"), N_HEADS, V_DIM_B),
        "w_out": _matmul_tn(merged, dy, "dw_out"),
        "w_up": _matmul_tn(h2, du, "dw_up"),
        "w_down": _matmul_tn(a, dy2, "dw_down"),
    }
    dqa, dka, dva, dsink = _swa_bwd(qa, ka, va, d_oa, lse_a, delta_a, pos_col, pos_row, sink_vec)
    dqb, dkb, dvb, late_parts = _mla_bwd(qb, kb, vb, d_ob, lse_b, delta_b, _grad_slices(late_grads, LATE_ROWS))
    dproj, dqbr, dkvb, dx, dg1, dgq, dgkv = _inproj_bwd(
        dgates, dqa, dka, dva, dqb, dkb, dvb, cq, ckv, xs, dx1, rc, rs1, rs2, g1, g_q, g_kv, w_in_p, w_qb, w_kvb)
    early_grads = {
        "w_in": _unpad_w_in(_matmul_tn(h, dproj, "dw_in")),
        "w_q_b": _unpad_heads_cols(_matmul_tn(cqn, dqbr, "dw_q_b"), N_HEADS, QK_NOPE + QK_ROPE),
        "w_kv_b": _unpad_w_kvb(_matmul_tn(ckvn, dkvb, "dw_kv_b")),
    }
    small_grads = {"pre_norm_mix": dg1, "post_norm_mix": dg2, "pre_norm_mlp": dg3, "post_norm_mlp": dg4,
                   "q_a_norm": dgq, "kv_a_norm": dgkv, "sinks": dsink[:, 0], "loss": loss[0, 0:1]}
    early_parts, s_parts = _exchange_grads(_grad_slices(early_grads, EARLY_ROWS), _pack_small(small_grads))

    zero = jnp.zeros((), F32)
    pack = lambda src: _pack_small({**{n: src[n] for n in SMALL_NAMES}, "loss": zero})
    updates = {}
    for parts, layout, name in ((late_parts, LATE_ROWS, "adamw_late"), (early_parts, EARLY_ROWS, "adamw_early")):
        flats = _adamw(parts, _flat_local(weights, layout), _flat_local(m_in, layout), _flat_local(v_in, layout), name)
        for kind, flat in zip(("g", "d", "m", "v"), flats):
            for wname, piece in _split_flat(flat, layout).items():
                updates[kind, wname] = piece.reshape(weights[wname].shape)
    smalls = _adamw(s_parts, pack(weights), pack(m_in), pack(v_in), "adamw_small")
    for kind, blk in zip(("g", "d", "m", "v"), smalls):
        for wname, piece in _unpack_small(blk).items():
            updates[kind, wname] = piece
    results = [updates[kind, name] for kind in ("g", "d", "m", "v") for name in WEIGHT_ORDER]
    return (updates["g", "loss"], dx[None], *results)
```

```python
import functools

import numpy as np
import jax
import jax.numpy as jnp
from jax import lax
from jax.experimental import pallas as pl
from jax.experimental.pallas import tpu as pltpu

F32 = jnp.float32
BF16 = jnp.bfloat16

D_MODEL = 1024
D_FF = 4096
N_HEADS = 8
N_KV_A = 2
GROUP_A = N_HEADS // N_KV_A
HEAD_A = 64
QK_NOPE = 64
QK_ROPE = 32
V_DIM_B = 64
Q_LORA = 256
KV_LORA = 128
BLOCK = 128
SLAB = 128
ROPE_THETA = 10000.0
EPS = 1e-6
N_DEV = 8
NEG = -1e30

SCALE_A = HEAD_A ** -0.5
SCALE_B = (QK_NOPE + QK_ROPE) ** -0.5
LOG2E = 1.4426950408889634
SCORE_B = SCALE_B * LOG2E
MLA_HEADS_PER_STEP = 4
SLOPES_A = tuple(2.0 ** (-8.0 * (h + 1) / N_HEADS) for h in range(N_HEADS))

ADAM_LR = 0.001
ADAM_B1 = 0.9
ADAM_B2 = 0.999
ADAM_EPS = 1e-08
ADAM_WD = 0.01
ADAM_STEP = 10

HM = N_HEADS * SLAB
C_GATES = 0
C_QA = 2 * D_MODEL
C_KA = C_QA + HM
C_VA = C_KA + N_KV_A * SLAB
C_CQ = C_VA + N_KV_A * SLAB
C_CKV = C_CQ + Q_LORA
C_KR = C_CKV + KV_LORA
D_IN_PAD = C_KR + SLAB

VMEM_LIMIT = 56 * 1024 * 1024

EARLY_ROWS = (("w_in", 404), ("w_q_b", 24), ("w_kv_b", 16))
LATE_ROWS = (("w_o_a", 64), ("w_o_b", 64), ("w_out", 128), ("w_up", 512), ("w_down", 512))
FLAT_ALIGN = 16
FLAT_TILE = 256
SMALL_ROWS = 8


def _flat_rows(layout):
    used = sum(-(-r // FLAT_ALIGN) * FLAT_ALIGN for _, r in layout)
    return -(-used // FLAT_TILE) * FLAT_TILE


def _token_tile(t):
    return min(256, t)


def _attn_tile(t):
    return 512 if t >= 2048 else 128


def _params(sem, vmem=VMEM_LIMIT):
    return pltpu.CompilerParams(dimension_semantics=sem, vmem_limit_bytes=vmem)


def _dot(a, b):
    return jnp.dot(a, b, preferred_element_type=F32)


def _dot_nt(a, b):
    return lax.dot_general(a, b, (((1,), (1,)), ((), ())), preferred_element_type=F32)


def _dot_tn(a, b):
    return lax.dot_general(a, b, (((0,), (0,)), ((), ())), preferred_element_type=F32)


def _rms_r(x):
    return lax.rsqrt(jnp.mean(x * x, axis=-1, keepdims=True) + EPS)


def _rms_bwd(x, r, g, dy):
    t = dy * g
    return r * t - x * (r * r * r) * jnp.mean(x * t, axis=-1, keepdims=True)


def _sigmoid(x):
    return 1.0 / (1.0 + jnp.exp(-x))


def _rope(x, c, s1, s2):
    return x * c + pltpu.roll(x, SLAB - 16, 1) * s1 + pltpu.roll(x, 16, 1) * s2


def _rope_bwd(d, c, s1, s2):
    return d * c + pltpu.roll(d * s1, 16, 1) + pltpu.roll(d * s2, SLAB - 16, 1)


def _row_spec(tm, n):
    return pl.BlockSpec((tm, n), lambda i: (i, 0))


def _full_spec(shape):
    nd = len(shape)
    return pl.BlockSpec(shape, lambda i: (0,) * nd, pipeline_mode=pl.Buffered(1))


def _acc_rows(ref, val):
    @pl.when(pl.program_id(0) == 0)
    def _():
        ref[...] = jnp.zeros_like(ref)
    ref[...] += jnp.sum(val, axis=0, keepdims=True)


def _rope_tables(pos_col, freq_row):
    t = pos_col.shape[0]
    tm = _token_tile(t)

    def body(pos_ref, f_ref, c_ref, s1_ref, s2_ref):
        ang = pos_ref[...].astype(F32) * f_ref[...]
        lane = lax.broadcasted_iota(jnp.int32, ang.shape, 1)
        s = jnp.sin(ang)
        c_ref[...] = jnp.cos(ang)
        s1_ref[...] = jnp.where((lane >= 64) & (lane < 80), -s, 0.0)
        s2_ref[...] = jnp.where((lane >= 80) & (lane < 96), s, 0.0)

    tab = jax.ShapeDtypeStruct((t, SLAB), F32)
    return pl.pallas_call(
        body, name="rope_tables", grid=(t // tm,),
        in_specs=[_row_spec(tm, 1), _full_spec((1, SLAB))],
        out_specs=[_row_spec(tm, SLAB)] * 3, out_shape=[tab] * 3,
        compiler_params=_params(("parallel",)),
    )(pos_col, freq_row)


def _inproj_fwd(x, g1, w_in, g_q, w_qb, g_kv, w_kvb, rope_c, rope_s1, rope_s2):
    t = x.shape[0]
    tm = _token_tile(t)

    def body(x_ref, g1_ref, win_ref, gq_ref, wqb_ref, gkv_ref, wkvb_ref, c_ref, s1_ref, s2_ref,
             h_ref, gates_ref, qa_ref, ka_ref, va_ref, cq_ref, ckv_ref, cqn_ref, ckvn_ref,
             qb_ref, kb_ref, vb_ref):
        xv = x_ref[...]
        h = (xv * _rms_r(xv) * g1_ref[...]).astype(BF16)
        h_ref[...] = h
        proj = _dot(h, win_ref[...])
        gates_ref[...] = proj[:, C_GATES:C_QA]
        qa_ref[...] = proj[:, C_QA:C_KA].astype(BF16)
        ka_ref[...] = proj[:, C_KA:C_VA].astype(BF16)
        va_ref[...] = proj[:, C_VA:C_CQ].astype(BF16)
        cq = proj[:, C_CQ:C_CKV]
        ckv = proj[:, C_CKV:C_KR]
        kr = proj[:, C_KR:D_IN_PAD]
        cq_ref[...] = cq
        ckv_ref[...] = ckv
        cqn = (cq * _rms_r(cq) * gq_ref[...]).astype(BF16)
        ckvn = (ckv * _rms_r(ckv) * gkv_ref[...]).astype(BF16)
        cqn_ref[...] = cqn
        ckvn_ref[...] = ckvn
        c, s1, s2 = c_ref[...], s1_ref[...], s2_ref[...]
        qb = _dot(cqn, wqb_ref[...])
        kvb = _dot(ckvn, wkvb_ref[...])
        kr_rot = _rope(kr, c, s1, s2)
        for hd in range(N_HEADS):
            sl = slice(hd * SLAB, (hd + 1) * SLAB)
            qb_ref[:, sl] = (_rope(qb[:, sl], c, s1, s2) * SCORE_B).astype(BF16)
            kb_ref[:, sl] = (kvb[:, sl] + kr_rot).astype(BF16)
        vb_ref[...] = kvb[:, HM:2 * HM].astype(BF16)

    def sds(n, dt):
        return jax.ShapeDtypeStruct((t, n), dt)

    outs = [(D_MODEL, BF16), (2 * D_MODEL, F32), (HM, BF16), (N_KV_A * SLAB, BF16), (N_KV_A * SLAB, BF16),
            (Q_LORA, F32), (KV_LORA, F32), (Q_LORA, BF16), (KV_LORA, BF16), (HM, BF16), (HM, BF16), (HM, BF16)]
    return pl.pallas_call(
        body, name="inproj_fwd", grid=(t // tm,),
        in_specs=[_row_spec(tm, D_MODEL), _full_spec((1, D_MODEL)), _full_spec((D_MODEL, D_IN_PAD)),
                  _full_spec((1, Q_LORA)), _full_spec((Q_LORA, HM)), _full_spec((1, KV_LORA)),
                  _full_spec((KV_LORA, 2 * HM)), _row_spec(tm, SLAB), _row_spec(tm, SLAB), _row_spec(tm, SLAB)],
        out_specs=[_row_spec(tm, n) for n, _ in outs],
        out_shape=[sds(n, dt) for n, dt in outs],
        compiler_params=_params(("parallel",)),
    )(x, g1, w_in, g_q, w_qb, g_kv, w_kvb, rope_c, rope_s1, rope_s2)


def _tile_group(a):
    return jnp.concatenate([a] * GROUP_A, axis=1)


def _swa_masks():
    row = lax.broadcasted_iota(jnp.int32, (BLOCK, GROUP_A * BLOCK), 0)
    col = lax.broadcasted_iota(jnp.int32, (BLOCK, GROUP_A * BLOCK), 1) & (BLOCK - 1)
    return row <= col, row > col


def _heads_beside(ref, g):
    return jnp.concatenate([ref[:, (g * GROUP_A + hh) * SLAB:(g * GROUP_A + hh + 1) * SLAB].T
                            for hh in range(GROUP_A)], axis=1)


def _rows_beside(ref, g):
    return jnp.concatenate([ref[g * GROUP_A + hh] for hh in range(GROUP_A)], axis=1)


def _swa_rows(sinks):
    slopes = jnp.repeat(jnp.asarray(SLOPES_A, F32).reshape(N_KV_A, GROUP_A, 1), BLOCK, axis=2)
    sink_rows = jnp.repeat(sinks.reshape(N_KV_A, GROUP_A, 1), BLOCK, axis=2)
    return slopes.reshape(N_KV_A, 1, GROUP_A * BLOCK), sink_rows.reshape(N_KV_A, 1, GROUP_A * BLOCK)


def _swa_fwd(qa, ka, va, pos_col, pos_row, sinks):
    t = qa.shape[0]
    nb = t // BLOCK
    gw = GROUP_A * BLOCK
    slope_rows, sink_rows = _swa_rows(sinks)

    def body(q_ref, kc_ref, kp_ref, vc_ref, vp_ref, pkc_ref, pkp_ref, pq_ref, slope_ref, sink_ref, o_ref, l_ref):
        i = pl.program_id(0)
        pq = pq_ref[...]
        dist_c = _tile_group(jnp.abs(pkc_ref[...] - pq).astype(F32))
        dist_p = _tile_group(jnp.abs(pkp_ref[...] - pq).astype(F32))
        mask_c, older = _swa_masks()
        mask_p = jnp.logical_and(older, i > 0)
        for g in range(N_KV_A):
            gs = slice(g * SLAB, (g + 1) * SLAB)
            x = _heads_beside(q_ref, g)
            slope, sink = slope_ref[g], sink_ref[g]
            s_c = jnp.where(mask_c, _dot(kc_ref[:, gs], x) * SCALE_A - slope * dist_c, NEG)
            s_p = jnp.where(mask_p, _dot(kp_ref[:, gs], x) * SCALE_A - slope * dist_p, NEG)
            m = jnp.maximum(jnp.maximum(jnp.max(s_c, axis=0, keepdims=True),
                                        jnp.max(s_p, axis=0, keepdims=True)), sink)
            e_c = jnp.exp(s_c - m)
            e_p = jnp.exp(s_p - m)
            den = jnp.sum(e_c, axis=0, keepdims=True) + jnp.sum(e_p, axis=0, keepdims=True) + jnp.exp(sink - m)
            inv = 1.0 / den
            ot = (_dot_tn(vc_ref[:, gs], (e_c * inv).astype(BF16))
                  + _dot_tn(vp_ref[:, gs], (e_p * inv).astype(BF16)))
            lse = m + jnp.log(den)
            for hh in range(GROUP_A):
                hd = g * GROUP_A + hh
                seg = slice(hh * BLOCK, (hh + 1) * BLOCK)
                o_ref[:, hd * SLAB:(hd + 1) * SLAB] = ot[:, seg].T.astype(BF16)
                l_ref[hd] = lse[:, seg]

    cur = lambda i: (i, 0)
    prev = lambda i: (jnp.maximum(i - 1, 0), 0)
    kvw = N_KV_A * SLAB
    rows = pl.BlockSpec((N_KV_A, 1, gw), lambda i: (0, 0, 0))
    return pl.pallas_call(
        body, name="swa_fwd", grid=(nb,),
        in_specs=[pl.BlockSpec((BLOCK, HM), cur),
                  pl.BlockSpec((BLOCK, kvw), cur), pl.BlockSpec((BLOCK, kvw), prev),
                  pl.BlockSpec((BLOCK, kvw), cur), pl.BlockSpec((BLOCK, kvw), prev),
                  pl.BlockSpec((BLOCK, 1), cur), pl.BlockSpec((BLOCK, 1), prev),
                  pl.BlockSpec((1, BLOCK), lambda i: (0, i)), rows, rows],
        out_specs=[pl.BlockSpec((BLOCK, HM), cur), pl.BlockSpec((N_HEADS, 1, BLOCK), lambda i: (0, 0, i))],
        out_shape=[jax.ShapeDtypeStruct((t, HM), BF16), jax.ShapeDtypeStruct((N_HEADS, 1, t), F32)],
        compiler_params=_params(("parallel",)),
    )(qa, ka, ka, va, va, pos_col, pos_col, pos_row, slope_rows, sink_rows)


def _swa_bwd(qa, ka, va, d_oa, lse, delta, pos_col, pos_row, sinks):
    t = qa.shape[0]
    nb = t // BLOCK
    gw = GROUP_A * BLOCK
    slope_rows, sink_rows = _swa_rows(sinks)

    def body(q_ref, qn_ref, do_ref, don_ref, l_ref, ln_ref, dl_ref, dln_ref, kp_ref, kc_ref, vp_ref, vc_ref,
             pkp_ref, pkc_ref, pq_ref, pqn_ref, slope_ref, sink_ref, dq_ref, dk_ref, dv_ref, dsink_ref):
        j = pl.program_id(0)
        pkc, pkp = pkc_ref[...], pkp_ref[...]
        dist_cc = _tile_group(jnp.abs(pkc - pq_ref[...]).astype(F32))
        dist_cp = _tile_group(jnp.abs(pkp - pq_ref[...]).astype(F32))
        dist_nc = _tile_group(jnp.abs(pkc - pqn_ref[...]).astype(F32))
        mask_cc, older = _swa_masks()
        mask_cp = jnp.logical_and(older, j > 0)
        mask_nc = jnp.logical_and(older, j < nb - 1)

        @pl.when(j == 0)
        def _():
            dsink_ref[...] = jnp.zeros_like(dsink_ref)

        def tile(k, v, x, dox, lrow, drow, dist, mask, slope):
            s = jnp.where(mask, _dot(k, x) * SCALE_A - slope * dist, NEG)
            p = jnp.exp(s - lrow)
            ds = p * (_dot(v, dox) - drow)
            return p.astype(BF16), ds.astype(BF16)

        for g in range(N_KV_A):
            gs = slice(g * SLAB, (g + 1) * SLAB)
            kc, kp, vc, vp = kc_ref[:, gs], kp_ref[:, gs], vc_ref[:, gs], vp_ref[:, gs]
            slope, sink = slope_ref[g], sink_ref[g]
            x, xn = _heads_beside(q_ref, g), _heads_beside(qn_ref, g)
            dox, doxn = _heads_beside(do_ref, g), _heads_beside(don_ref, g)
            lrow, drow = _rows_beside(l_ref, g), _rows_beside(dl_ref, g)
            lrown, drown = _rows_beside(ln_ref, g), _rows_beside(dln_ref, g)
            p_cc, ds_cc = tile(kc, vc, x, dox, lrow, drow, dist_cc, mask_cc, slope)
            _, ds_cp = tile(kp, vp, x, dox, lrow, drow, dist_cp, mask_cp, slope)
            p_nc, ds_nc = tile(kc, vc, xn, doxn, lrown, drown, dist_nc, mask_nc, slope)
            dqt = (_dot_tn(kc, ds_cc) + _dot_tn(kp, ds_cp)) * SCALE_A
            for hh in range(GROUP_A):
                hd = g * GROUP_A + hh
                dq_ref[:, hd * SLAB:(hd + 1) * SLAB] = dqt[:, hh * BLOCK:(hh + 1) * BLOCK].T.astype(BF16)
            dk_ref[:, gs] = ((_dot_nt(ds_cc, x) + _dot_nt(ds_nc, xn)) * SCALE_A).astype(BF16)
            dv_ref[:, gs] = (_dot_nt(p_cc, dox) + _dot_nt(p_nc, doxn)).astype(BF16)
            dsink_ref[g] -= jnp.exp(sink - lrow) * drow

    cur = lambda j: (j, 0)
    prev = lambda j: (jnp.maximum(j - 1, 0), 0)
    nxt = lambda j: (jnp.minimum(j + 1, nb - 1), 0)
    cur3 = lambda j: (0, 0, j)
    nxt3 = lambda j: (0, 0, jnp.minimum(j + 1, nb - 1))
    kvw = N_KV_A * SLAB
    rows = pl.BlockSpec((N_KV_A, 1, gw), lambda j: (0, 0, 0))
    stat = lambda im: pl.BlockSpec((N_HEADS, 1, BLOCK), im)
    return pl.pallas_call(
        body, name="swa_bwd", grid=(nb,),
        in_specs=[pl.BlockSpec((BLOCK, HM), cur), pl.BlockSpec((BLOCK, HM), nxt),
                  pl.BlockSpec((BLOCK, HM), cur), pl.BlockSpec((BLOCK, HM), nxt),
                  stat(cur3), stat(nxt3), stat(cur3), stat(nxt3),
                  pl.BlockSpec((BLOCK, kvw), prev), pl.BlockSpec((BLOCK, kvw), cur),
                  pl.BlockSpec((BLOCK, kvw), prev), pl.BlockSpec((BLOCK, kvw), cur),
                  pl.BlockSpec((BLOCK, 1), prev), pl.BlockSpec((BLOCK, 1), cur),
                  pl.BlockSpec((1, BLOCK), lambda j: (0, j)),
                  pl.BlockSpec((1, BLOCK), lambda j: (0, jnp.minimum(j + 1, nb - 1))), rows, rows],
        out_specs=[pl.BlockSpec((BLOCK, HM), cur), pl.BlockSpec((BLOCK, kvw), cur),
                   pl.BlockSpec((BLOCK, kvw), cur), rows],
        out_shape=[jax.ShapeDtypeStruct((t, HM), BF16), jax.ShapeDtypeStruct((t, kvw), BF16),
                   jax.ShapeDtypeStruct((t, kvw), BF16), jax.ShapeDtypeStruct((N_KV_A, 1, gw), F32)],
        compiler_params=_params(("arbitrary",)),
    )(qa, qa, d_oa, d_oa, lse, lse, delta, delta, ka, ka, va, va,
      pos_col, pos_col, pos_row, pos_row, slope_rows, sink_rows)


def _lower_triangle(n):
    row = lax.broadcasted_iota(jnp.int32, (n, n), 0)
    col = lax.broadcasted_iota(jnp.int32, (n, n), 1)
    return row >= col


def _upper_triangle(n):
    row = lax.broadcasted_iota(jnp.int32, (n, n), 0)
    col = lax.broadcasted_iota(jnp.int32, (n, n), 1)
    return row <= col


def _mesh_pos():
    return lax.axis_index("x"), lax.axis_index("y"), lax.axis_index("c")


def _flip(v, bit):
    return 1 - v if bit else v


def _direct_copies(src_ref, dst_ref, send_sems, recv_sems, local_sem, sem_base, gather):
    x, y, c = _mesh_pos()
    me = 4 * x + 2 * y + c
    local = pltpu.make_async_copy(src_ref if gather else src_ref.at[me], dst_ref.at[me], local_sem)
    remote = []
    for r in range(1, N_DEV):
        px, py, pc = _flip(x, r & 4), _flip(y, r & 2), _flip(c, r & 1)
        remote.append(pltpu.make_async_remote_copy(
            src_ref=src_ref if gather else src_ref.at[4 * px + 2 * py + pc], dst_ref=dst_ref.at[me],
            send_sem=send_sems.at[sem_base + r - 1], recv_sem=recv_sems.at[sem_base + r - 1],
            device_id=(px, py, pc), device_id_type=pl.DeviceIdType.MESH))
    return local, remote


def _start_copies(local, remote):
    local.start()
    for cp in remote:
        cp.start()


def _wait_copies(local, remote):
    for cp in remote:
        cp.wait_recv()
    for cp in remote:
        cp.wait_send()
    local.wait()


EXCHANGE_SCRATCH = [pltpu.SemaphoreType.DMA((N_DEV - 1,)), pltpu.SemaphoreType.DMA((N_DEV - 1,)),
                    pltpu.SemaphoreType.DMA]
ANY_SPEC = pl.BlockSpec(memory_space=pl.ANY)


def _mla_fwd(qb, kb, vb, late_flat):
    t = qb.shape[0]
    tq = _attn_tile(t)
    nt = t // tq
    hps = MLA_HEADS_PER_STEP
    w = hps * SLAB
    pairs = [(i, j) for i in range(nt) for j in range(i + 1)]
    i_tab = jnp.asarray(np.array([p[0] for p in pairs], np.int32))
    j_tab = jnp.asarray(np.array([p[1] for p in pairs], np.int32))

    def body(it_ref, jt_ref, q_ref, k_ref, vt_ref, late_ref, o_ref, l_ref, gathered_ref,
             m_s, l_s, acc_s, send_sems, recv_sems, local_sem):
        n = pl.program_id(1)
        i, j = it_ref[n], jt_ref[n]
        first_step = jnp.logical_and(pl.program_id(0) == 0, n == 0)
        last_step = jnp.logical_and(pl.program_id(0) == N_HEADS // hps - 1, n == len(pairs) - 1)

        @pl.when(first_step)
        def _():
            _start_copies(*_direct_copies(late_ref, gathered_ref, send_sems, recv_sems, local_sem, 0, True))

        @pl.when(j == 0)
        def _():
            m_s[...] = jnp.full_like(m_s, NEG)
            l_s[...] = jnp.zeros_like(l_s)
            acc_s[...] = jnp.zeros_like(acc_s)

        def update(masked):
            def scores(hh):
                sl = slice(hh * SLAB, (hh + 1) * SLAB)
                return _dot_nt(k_ref[:, sl], q_ref[:, sl])

            def softmax(hh, s):
                if masked:
                    s = jnp.where(_upper_triangle(tq), s, NEG)
                m_old = m_s[hh]
                m_new = jnp.maximum(m_old, jnp.max(s, axis=0, keepdims=True))
                alpha = jnp.exp2(m_old - m_new)
                p = jnp.exp2(s - m_new)
                l_s[hh] = alpha * l_s[hh] + jnp.sum(p, axis=0, keepdims=True)
                m_s[hh] = m_new
                return p.astype(BF16), alpha

            def accumulate(hh, p, alpha):
                sl = slice(hh * SLAB, hh * SLAB + V_DIM_B)
                acc_s[sl, :] = alpha * acc_s[sl, :] + _dot(vt_ref[sl, :], p)

            s_next, pending = scores(0), None
            for hh in range(hps):
                s = s_next
                if hh + 1 < hps:
                    s_next = scores(hh + 1)
                p, alpha = softmax(hh, s)
                if pending is not None:
                    accumulate(*pending)
                pending = (hh, p, alpha)
            accumulate(*pending)

        @pl.when(j < i)
        def _():
            update(False)

        @pl.when(j == i)
        def _():
            update(True)
            for hh in range(hps):
                sl = slice(hh * SLAB, (hh + 1) * SLAB)
                o_ref[:, sl] = (acc_s[sl, :] / l_s[hh]).T.astype(BF16)
                l_ref[hh] = m_s[hh] + jnp.log2(l_s[hh])

        @pl.when(last_step)
        def _():
            _wait_copies(*_direct_copies(late_ref, gathered_ref, send_sems, recv_sems, local_sem, 0, True))

    grid_spec = pltpu.PrefetchScalarGridSpec(
        num_scalar_prefetch=2, grid=(N_HEADS // hps, len(pairs)),
        in_specs=[pl.BlockSpec((tq, w), lambda h, n, it, jt: (it[n], h)),
                  pl.BlockSpec((tq, w), lambda h, n, it, jt: (jt[n], h)),
                  pl.BlockSpec((w, tq), lambda h, n, it, jt: (h, jt[n])),
                  ANY_SPEC],
        out_specs=[pl.BlockSpec((tq, w), lambda h, n, it, jt: (it[n], h)),
                   pl.BlockSpec((hps, 1, tq), lambda h, n, it, jt: (h, 0, it[n])),
                   ANY_SPEC],
        scratch_shapes=[pltpu.VMEM((hps, 1, tq), F32), pltpu.VMEM((hps, 1, tq), F32), pltpu.VMEM((w, tq), F32)]
        + EXCHANGE_SCRATCH)
    return pl.pallas_call(
        body, name="mla_fwd", grid_spec=grid_spec,
        out_shape=[jax.ShapeDtypeStruct((t, HM), BF16), jax.ShapeDtypeStruct((N_HEADS, 1, t), F32),
                   jax.ShapeDtypeStruct((N_DEV,) + late_flat.shape, late_flat.dtype)],
        compiler_params=_params(("arbitrary", "arbitrary")),
    )(i_tab, j_tab, qb, kb, vb.T, late_flat)


def _mla_bwd(qb, kb, vb, d_ob, lse, delta, grad_slices):
    t = qb.shape[0]
    tq = _attn_tile(t)
    nt = t // tq
    hps = MLA_HEADS_PER_STEP
    w = hps * SLAB
    pairs = [(j, i) for j in range(nt) for i in range(j, nt)]
    j_tab = jnp.asarray(np.array([p[0] for p in pairs], np.int32))
    i_tab = jnp.asarray(np.array([p[1] for p in pairs], np.int32))

    def body(jt_ref, it_ref, q_ref, qt_ref, do_ref, dot_ref, l_ref, dl_ref, k_ref, kt_ref, v_ref, slices_ref,
             dqt_ref, dkt_ref, dvt_ref, parts_ref, dk_s, dv_s, send_sems, recv_sems, local_sem):
        n = pl.program_id(1)
        j, i = jt_ref[n], it_ref[n]
        first_step = jnp.logical_and(pl.program_id(0) == 0, n == 0)
        last_step = jnp.logical_and(pl.program_id(0) == N_HEADS // hps - 1, n == len(pairs) - 1)

        @pl.when(first_step)
        def _():
            _start_copies(*_direct_copies(slices_ref, parts_ref, send_sems, recv_sems, local_sem, 0, False))

        @pl.when(n == 0)
        def _():
            dqt_ref[...] = jnp.zeros_like(dqt_ref)

        def update(diagonal):
            cols = pl.ds(pl.multiple_of(i * tq, tq), tq)

            def products(hh):
                sl = slice(hh * SLAB, (hh + 1) * SLAB)
                return _dot_nt(k_ref[:, sl], q_ref[:, sl]), _dot_nt(v_ref[:, sl], do_ref[:, sl])

            def softmax_bwd(hh, s, dp):
                if diagonal:
                    s = jnp.where(_upper_triangle(tq), s, NEG)
                p = jnp.exp2(s - l_ref[hh])
                return p.astype(BF16), (p * (dp - dl_ref[hh])).astype(BF16)

            def gradients(hh, p, ds):
                base = hh * SLAB
                vrows = slice(base, base + V_DIM_B)
                qrows = slice(base, base + QK_NOPE + QK_ROPE)
                dv = _dot_nt(dot_ref[vrows, :], p)
                dk = _dot_nt(qt_ref[qrows, :], ds)
                if diagonal:
                    dv_s[base:base + SLAB, :] = jnp.concatenate([dv, jnp.zeros((SLAB - V_DIM_B, tq), F32)], axis=0)
                    dk_s[base:base + SLAB, :] = jnp.concatenate(
                        [dk, jnp.zeros((SLAB - QK_NOPE - QK_ROPE, tq), F32)], axis=0)
                else:
                    dv_s[vrows, :] += dv
                    dk_s[qrows, :] += dk
                dqt_ref[qrows, cols] += _dot(kt_ref[qrows, :], ds)

            for hh in range(hps):
                gradients(hh, *softmax_bwd(hh, *products(hh)))

        @pl.when(i == j)
        def _():
            update(True)

        @pl.when(i > j)
        def _():
            update(False)

        @pl.when(i == nt - 1)
        def _():
            dkt_ref[...] = (dk_s[...] * (1.0 / LOG2E)).astype(BF16)
            dvt_ref[...] = dv_s[...].astype(BF16)

        @pl.when(last_step)
        def _():
            _wait_copies(*_direct_copies(slices_ref, parts_ref, send_sems, recv_sems, local_sem, 0, False))

    grid_spec = pltpu.PrefetchScalarGridSpec(
        num_scalar_prefetch=2, grid=(N_HEADS // hps, len(pairs)),
        in_specs=[pl.BlockSpec((tq, w), lambda h, n, jt, it: (it[n], h)),
                  pl.BlockSpec((w, tq), lambda h, n, jt, it: (h, it[n])),
                  pl.BlockSpec((tq, w), lambda h, n, jt, it: (it[n], h)),
                  pl.BlockSpec((w, tq), lambda h, n, jt, it: (h, it[n])),
                  pl.BlockSpec((hps, 1, tq), lambda h, n, jt, it: (h, 0, it[n])),
                  pl.BlockSpec((hps, 1, tq), lambda h, n, jt, it: (h, 0, it[n])),
                  pl.BlockSpec((tq, w), lambda h, n, jt, it: (jt[n], h)),
                  pl.BlockSpec((w, tq), lambda h, n, jt, it: (h, jt[n])),
                  pl.BlockSpec((tq, w), lambda h, n, jt, it: (jt[n], h)),
                  ANY_SPEC],
        out_specs=[pl.BlockSpec((w, t), lambda h, n, jt, it: (h, 0)),
                   pl.BlockSpec((w, tq), lambda h, n, jt, it: (h, jt[n])),
                   pl.BlockSpec((w, tq), lambda h, n, jt, it: (h, jt[n])),
                   ANY_SPEC],
        scratch_shapes=[pltpu.VMEM((w, tq), F32), pltpu.VMEM((w, tq), F32)] + EXCHANGE_SCRATCH)
    dqt, dkt, dvt, parts = pl.pallas_call(
        body, name="mla_bwd", grid_spec=grid_spec,
        out_shape=[jax.ShapeDtypeStruct((HM, t), F32), jax.ShapeDtypeStruct((HM, t), BF16),
                   jax.ShapeDtypeStruct((HM, t), BF16),
                   jax.ShapeDtypeStruct(grad_slices.shape, grad_slices.dtype)],
        compiler_params=_params(("arbitrary", "arbitrary")),
    )(j_tab, i_tab, qb, qb.T, d_ob, d_ob.T, lse, delta.reshape(N_HEADS, 1, t), kb, kb.T, vb, grad_slices)
    return dqt.T, dkt.T, dvt.T, parts


def _merge_fwd(out_a, out_b, gates, x, w_oa, w_ob, w_out, g2, g3):
    t = x.shape[0]
    tm = _token_tile(t)

    def body(oa_ref, ob_ref, gates_ref, x_ref, woa_ref, wob_ref, wout_ref, g2_ref, g3_ref,
             oap_ref, obp_ref, merged_ref, y_ref, x1_ref, h2_ref):
        oa_p = _dot(oa_ref[...], woa_ref[...])
        ob_p = _dot(ob_ref[...], wob_ref[...])
        oap_ref[...] = oa_p.astype(BF16)
        obp_ref[...] = ob_p.astype(BF16)
        sa = _sigmoid(gates_ref[:, 0:D_MODEL])
        sb = _sigmoid(gates_ref[:, D_MODEL:2 * D_MODEL])
        merged = (sa * oa_p + sb * ob_p).astype(BF16)
        merged_ref[...] = merged
        y = _dot(merged, wout_ref[...])
        y_ref[...] = y
        x1 = x_ref[...] + y * _rms_r(y) * g2_ref[...]
        x1_ref[...] = x1
        h2_ref[...] = (x1 * _rms_r(x1) * g3_ref[...]).astype(BF16)

    def sds(dt):
        return jax.ShapeDtypeStruct((t, D_MODEL), dt)

    row = _row_spec(tm, D_MODEL)
    return pl.pallas_call(
        body, name="merge_fwd", grid=(t // tm,),
        in_specs=[_row_spec(tm, HM), _row_spec(tm, HM), _row_spec(tm, 2 * D_MODEL), row,
                  _full_spec((HM, D_MODEL)), _full_spec((HM, D_MODEL)), _full_spec((D_MODEL, D_MODEL)),
                  _full_spec((1, D_MODEL)), _full_spec((1, D_MODEL))],
        out_specs=[row] * 6,
        out_shape=[sds(BF16), sds(BF16), sds(BF16), sds(F32), sds(F32), sds(BF16)],
        compiler_params=_params(("parallel",)),
    )(out_a, out_b, gates, x, w_oa, w_ob, w_out, g2, g3)


def _merge_bwd(dx1, y, gates, oa_p, ob_p, out_a, out_b, w_oa, w_ob, w_out, g2):
    t = dx1.shape[0]
    tm = _token_tile(t)

    def body(dx1_ref, y_ref, gates_ref, oap_ref, obp_ref, oa_ref, ob_ref, woa_ref, wob_ref, wout_ref, g2_ref,
             dy_ref, doap_ref, dobp_ref, dgates_ref, doa_ref, dob_ref, dla_ref, dlb_ref, dg2_ref):
        dx1v = dx1_ref[...]
        yv = y_ref[...]
        r2 = _rms_r(yv)
        _acc_rows(dg2_ref, dx1v * yv * r2)
        dy = _rms_bwd(yv, r2, g2_ref[...], dx1v).astype(BF16)
        dy_ref[...] = dy
        dm = _dot_nt(dy, wout_ref[...])
        sa = _sigmoid(gates_ref[:, 0:D_MODEL])
        sb = _sigmoid(gates_ref[:, D_MODEL:2 * D_MODEL])
        d_oap = (dm * sa).astype(BF16)
        d_obp = (dm * sb).astype(BF16)
        doap_ref[...] = d_oap
        dobp_ref[...] = d_obp
        dgates_ref[:, 0:D_MODEL] = (dm * oap_ref[...].astype(F32) * sa * (1.0 - sa)).astype(BF16)
        dgates_ref[:, D_MODEL:2 * D_MODEL] = (dm * obp_ref[...].astype(F32) * sb * (1.0 - sb)).astype(BF16)
        d_oa = _dot_nt(d_oap, woa_ref[...])
        d_ob = _dot_nt(d_obp, wob_ref[...])
        doa_ref[...] = d_oa.astype(BF16)
        dob_ref[...] = d_ob.astype(BF16)
        for hd in range(N_HEADS):
            sl = slice(hd * SLAB, (hd + 1) * SLAB)
            dla_ref[hd] = jnp.sum(d_oa[:, sl] * oa_ref[:, sl].astype(F32), axis=-1, keepdims=True)
            dlb_ref[hd] = jnp.sum(d_ob[:, sl] * ob_ref[:, sl].astype(F32), axis=-1, keepdims=True)

    def sds(n, dt):
        return jax.ShapeDtypeStruct((t, n), dt)

    row = _row_spec(tm, D_MODEL)
    head3 = pl.BlockSpec((N_HEADS, tm, 1), lambda i: (0, i, 0))
    return pl.pallas_call(
        body, name="merge_bwd", grid=(t // tm,),
        in_specs=[row, row, _row_spec(tm, 2 * D_MODEL), row, row, _row_spec(tm, HM), _row_spec(tm, HM),
                  _full_spec((HM, D_MODEL)), _full_spec((HM, D_MODEL)), _full_spec((D_MODEL, D_MODEL)),
                  _full_spec((1, D_MODEL))],
        out_specs=[row, row, row, _row_spec(tm, 2 * D_MODEL), _row_spec(tm, HM), _row_spec(tm, HM),
                   head3, head3, _full_spec((1, D_MODEL))],
        out_shape=[sds(D_MODEL, BF16), sds(D_MODEL, BF16), sds(D_MODEL, BF16), sds(2 * D_MODEL, BF16),
                   sds(HM, BF16), sds(HM, BF16),
                   jax.ShapeDtypeStruct((N_HEADS, t, 1), F32), jax.ShapeDtypeStruct((N_HEADS, t, 1), F32),
                   jax.ShapeDtypeStruct((1, D_MODEL), F32)],
        compiler_params=_params(("arbitrary",)),
    )(dx1, y, gates, oa_p, ob_p, out_a, out_b, w_oa, w_ob, w_out, g2)


def _mlp_fwd_bwd(x1, h2, target, w_up, w_down, g3, g4):
    t = x1.shape[0]
    tm = _token_tile(t)

    def body(x1_ref, h2_ref, tgt_ref, wup_ref, wdown_ref, g3_ref, g4_ref,
             a_ref, du_ref, dy2_ref, dx1_ref, loss_ref, dg3_ref, dg4_ref):
        x1v = x1_ref[...]
        u = _dot(h2_ref[...], wup_ref[...])
        ru = jnp.maximum(u, 0.0)
        a = (ru * ru).astype(BF16)
        a_ref[...] = a
        y2 = _dot(a, wdown_ref[...])
        r4 = _rms_r(y2)
        diff = x1v + y2 * r4 * g4_ref[...] - tgt_ref[...]
        _acc_rows(loss_ref, jnp.sum(diff * diff, axis=-1, keepdims=True) * (0.5 / D_MODEL)
                  * jnp.ones((1, SLAB), F32))
        dx2 = diff * (1.0 / D_MODEL)
        _acc_rows(dg4_ref, dx2 * y2 * r4)
        dy2 = _rms_bwd(y2, r4, g4_ref[...], dx2).astype(BF16)
        dy2_ref[...] = dy2
        du = (_dot_nt(dy2, wdown_ref[...]) * (2.0 * ru)).astype(BF16)
        du_ref[...] = du
        dh2 = _dot_nt(du, wup_ref[...])
        r3 = _rms_r(x1v)
        _acc_rows(dg3_ref, dh2 * x1v * r3)
        dx1_ref[...] = dx2 + _rms_bwd(x1v, r3, g3_ref[...], dh2)

    row = _row_spec(tm, D_MODEL)
    frow = _row_spec(tm, D_FF)
    vec = _full_spec((1, D_MODEL))
    return pl.pallas_call(
        body, name="mlp_fwd_bwd", grid=(t // tm,),
        in_specs=[row, row, row, _full_spec((D_MODEL, D_FF)), _full_spec((D_FF, D_MODEL)), vec, vec],
        out_specs=[frow, frow, row, row, _full_spec((1, SLAB)), vec, vec],
        out_shape=[jax.ShapeDtypeStruct((t, D_FF), BF16), jax.ShapeDtypeStruct((t, D_FF), BF16),
                   jax.ShapeDtypeStruct((t, D_MODEL), BF16), jax.ShapeDtypeStruct((t, D_MODEL), F32),
                   jax.ShapeDtypeStruct((1, SLAB), F32), jax.ShapeDtypeStruct((1, D_MODEL), F32),
                   jax.ShapeDtypeStruct((1, D_MODEL), F32)],
        compiler_params=_params(("arbitrary",)),
    )(x1, h2, target, w_up, w_down, g3, g4)


def _inproj_bwd(dgates, dqa, dka, dva, dqb, dkb, dvb, cq, ckv, x, dx1, rope_c, rope_s1, rope_s2,
                g1, g_q, g_kv, w_in, w_qb, w_kvb):
    t = x.shape[0]
    tm = _token_tile(t)

    def body(dgates_ref, dqa_ref, dka_ref, dva_ref, dqb_ref, dkb_ref, dvb_ref, cq_ref, ckv_ref, x_ref, dx1_ref,
             c_ref, s1_ref, s2_ref, g1_ref, gq_ref, gkv_ref, win_ref, wqb_ref, wkvb_ref,
             dproj_ref, dqbr_ref, dkvb_ref, dx_ref, dg1_ref, dgq_ref, dgkv_ref):
        c, s1, s2 = c_ref[...], s1_ref[...], s2_ref[...]
        dk_sum = jnp.zeros((tm, SLAB), F32)
        for hd in range(N_HEADS):
            sl = slice(hd * SLAB, (hd + 1) * SLAB)
            dqbr_ref[:, sl] = _rope_bwd(dqb_ref[:, sl] * SCALE_B, c, s1, s2).astype(BF16)
            dk_sum += dkb_ref[:, sl].astype(F32)
        dkvb_ref[:, 0:HM] = dkb_ref[...]
        dkvb_ref[:, HM:2 * HM] = dvb_ref[...]
        dkr = _rope_bwd(dk_sum, c, s1, s2)
        dcqn = _dot_nt(dqbr_ref[...], wqb_ref[...])
        cq = cq_ref[...]
        rq = _rms_r(cq)
        _acc_rows(dgq_ref, dcqn * cq * rq)
        dcq = _rms_bwd(cq, rq, gq_ref[...], dcqn)
        dckvn = _dot_nt(dkvb_ref[...], wkvb_ref[...])
        ckv = ckv_ref[...]
        rkv = _rms_r(ckv)
        _acc_rows(dgkv_ref, dckvn * ckv * rkv)
        dckv = _rms_bwd(ckv, rkv, gkv_ref[...], dckvn)
        dproj_ref[:, C_GATES:C_QA] = dgates_ref[...]
        dproj_ref[:, C_QA:C_KA] = dqa_ref[...]
        dproj_ref[:, C_KA:C_VA] = dka_ref[...]
        dproj_ref[:, C_VA:C_CQ] = dva_ref[...]
        dproj_ref[:, C_CQ:C_CKV] = dcq.astype(BF16)
        dproj_ref[:, C_CKV:C_KR] = dckv.astype(BF16)
        dproj_ref[:, C_KR:D_IN_PAD] = dkr.astype(BF16)
        dh = _dot_nt(dproj_ref[...], win_ref[...])
        xv = x_ref[...]
        r1 = _rms_r(xv)
        _acc_rows(dg1_ref, dh * xv * r1)
        dx_ref[...] = dx1_ref[...] + _rms_bwd(xv, r1, g1_ref[...], dh)

    kvw = N_KV_A * SLAB
    row = _row_spec(tm, D_MODEL)
    hm = _row_spec(tm, HM)
    tab = _row_spec(tm, SLAB)
    return pl.pallas_call(
        body, name="inproj_bwd", grid=(t // tm,),
        in_specs=[_row_spec(tm, 2 * D_MODEL), hm, _row_spec(tm, kvw), _row_spec(tm, kvw), hm, hm, hm,
                  _row_spec(tm, Q_LORA), _row_spec(tm, KV_LORA), row, row, tab, tab, tab,
                  _full_spec((1, D_MODEL)), _full_spec((1, Q_LORA)), _full_spec((1, KV_LORA)),
                  _full_spec((D_MODEL, D_IN_PAD)), _full_spec((Q_LORA, HM)), _full_spec((KV_LORA, 2 * HM))],
        out_specs=[_row_spec(tm, D_IN_PAD), hm, _row_spec(tm, 2 * HM), row,
                   _full_spec((1, D_MODEL)), _full_spec((1, Q_LORA)), _full_spec((1, KV_LORA))],
        out_shape=[jax.ShapeDtypeStruct((t, D_IN_PAD), BF16), jax.ShapeDtypeStruct((t, HM), BF16),
                   jax.ShapeDtypeStruct((t, 2 * HM), BF16), jax.ShapeDtypeStruct((t, D_MODEL), F32),
                   jax.ShapeDtypeStruct((1, D_MODEL), F32), jax.ShapeDtypeStruct((1, Q_LORA), F32),
                   jax.ShapeDtypeStruct((1, KV_LORA), F32)],
        compiler_params=_params(("arbitrary",)),
    )(dgates, dqa, dka, dva, dqb, dkb, dvb, cq, ckv, x, dx1, rope_c, rope_s1, rope_s2,
      g1, g_q, g_kv, w_in, w_qb, w_kvb)


def _matmul_tn(a, b, name):
    t, k = a.shape
    n = b.shape[1]
    bt = min(t, 512)
    bk = min(k, 1024)
    bn = min(n, 1024)

    def body(a_ref, b_ref, o_ref):
        @pl.when(pl.program_id(2) == 0)
        def _():
            o_ref[...] = jnp.zeros_like(o_ref)
        o_ref[...] += _dot_tn(a_ref[...], b_ref[...])

    return pl.pallas_call(
        body, name=name, grid=(k // bk, n // bn, t // bt),
        in_specs=[pl.BlockSpec((bt, bk), lambda i, j, s: (s, i)), pl.BlockSpec((bt, bn), lambda i, j, s: (s, j))],
        out_specs=pl.BlockSpec((bk, bn), lambda i, j, s: (i, j)),
        out_shape=jax.ShapeDtypeStruct((k, n), F32),
        compiler_params=_params(("parallel", "parallel", "arbitrary")),
    )(a, b)


def _all_gather_flat(flat):
    rows, cols = flat.shape

    def body(x_ref, out_ref, send_sems, recv_sems, local_sem):
        x, y, c = _mesh_pos()
        me, sibling = (x, y, c), (x, y, 1 - c)
        chips = [(1 - x, y), (x, 1 - y), (1 - x, 1 - y)]

        def slot(px, py, pc):
            return out_ref.at[4 * px + 2 * py + pc]

        def copy(k, block, to, src=None):
            return pltpu.make_async_remote_copy(
                src_ref=slot(*block) if src is None else src, dst_ref=slot(*block),
                send_sem=send_sems.at[k], recv_sem=recv_sems.at[k],
                device_id=to, device_id_type=pl.DeviceIdType.MESH)

        mine = pltpu.make_async_copy(x_ref, slot(*me), local_sem)
        mine.start()
        first = [copy(0, me, sibling, src=x_ref)]
        first += [copy(1 + j, me, (*chip, c), src=x_ref) for j, chip in enumerate(chips)]
        for cp in first:
            cp.start()
        passed = [copy(4 + j, (*chip, c), sibling) for j, chip in enumerate(chips)]
        for j, chip in enumerate(chips):
            copy(1 + j, (*chip, c), me).wait_recv()
            passed[j].start()
        copy(0, sibling, me).wait_recv()
        for j, chip in enumerate(chips):
            copy(4 + j, (*chip, 1 - c), me).wait_recv()
        for cp in first + passed:
            cp.wait_send()
        mine.wait()

    return pl.pallas_call(
        body, name="all_gather_weights",
        out_shape=jax.ShapeDtypeStruct((N_DEV, rows, cols), flat.dtype),
        in_specs=[pl.BlockSpec(memory_space=pl.ANY)],
        out_specs=pl.BlockSpec(memory_space=pl.ANY),
        scratch_shapes=[pltpu.SemaphoreType.DMA((7,)), pltpu.SemaphoreType.DMA((7,)), pltpu.SemaphoreType.DMA],
    )(flat)


def _exchange_grads(gflat, small):
    _, rows, cols = gflat.shape

    def body(g_ref, s_ref, grecv_ref, srecv_ref, send_sems, recv_sems, local_sems):
        slices = _direct_copies(g_ref, grecv_ref, send_sems, recv_sems, local_sems.at[0], 0, False)
        smalls = _direct_copies(s_ref, srecv_ref, send_sems, recv_sems, local_sems.at[1], N_DEV - 1, True)
        _start_copies(*slices)
        _start_copies(*smalls)
        _wait_copies(*slices)
        _wait_copies(*smalls)

    return pl.pallas_call(
        body, name="exchange_grads",
        out_shape=[jax.ShapeDtypeStruct((N_DEV, rows, cols), gflat.dtype),
                   jax.ShapeDtypeStruct((N_DEV, SMALL_ROWS, cols), small.dtype)],
        in_specs=[pl.BlockSpec(memory_space=pl.ANY), pl.BlockSpec(memory_space=pl.ANY)],
        out_specs=[pl.BlockSpec(memory_space=pl.ANY), pl.BlockSpec(memory_space=pl.ANY)],
        scratch_shapes=[pltpu.SemaphoreType.DMA((14,)), pltpu.SemaphoreType.DMA((14,)),
                        pltpu.SemaphoreType.DMA((2,))],
    )(gflat, small)


def _adamw(parts, w, m, v, name):
    _, rows, cols = parts.shape
    tr = FLAT_TILE if rows % FLAT_TILE == 0 else rows
    c1 = 1.0 - ADAM_B1 ** ADAM_STEP
    c2 = 1.0 - ADAM_B2 ** ADAM_STEP

    def body(p_ref, w_ref, m_ref, v_ref, g_ref, d_ref, mo_ref, vo_ref):
        g = p_ref[0].astype(F32)
        for s in range(1, N_DEV):
            g = g + p_ref[s].astype(F32)
        g_ref[...] = g
        m_new = ADAM_B1 * m_ref[...] + (1.0 - ADAM_B1) * g
        v_new = ADAM_B2 * v_ref[...] + (1.0 - ADAM_B2) * (g * g)
        mo_ref[...] = m_new
        vo_ref[...] = v_new
        m_hat = m_new / c1
        v_hat = v_new / c2
        d_ref[...] = -ADAM_LR * (m_hat / (jnp.sqrt(v_hat) + ADAM_EPS) + ADAM_WD * w_ref[...])

    row = pl.BlockSpec((tr, cols), lambda i: (i, 0))
    out = jax.ShapeDtypeStruct((rows, cols), F32)
    return pl.pallas_call(
        body, name=name, grid=(rows // tr,),
        in_specs=[pl.BlockSpec((N_DEV, tr, cols), lambda i: (0, i, 0)), row, row, row],
        out_specs=[row] * 4, out_shape=[out] * 4,
        compiler_params=_params(("parallel",)),
    )(parts, w, m, v)


def _pad_heads_cols(w, heads, width):
    k = w.shape[0]
    w = w.reshape(k, heads, width)
    return jnp.pad(w, ((0, 0), (0, 0), (0, SLAB - width))).reshape(k, heads * SLAB)


def _unpad_heads_cols(w, heads, width):
    k = w.shape[0]
    return w.reshape(k, heads, SLAB)[:, :, :width].reshape(k, heads * width)


def _pad_heads_rows(w, heads, width):
    n = w.shape[1]
    w = w.reshape(heads, width, n)
    return jnp.pad(w, ((0, 0), (0, SLAB - width), (0, 0))).reshape(heads * SLAB, n)


def _unpad_heads_rows(w, heads, width):
    n = w.shape[1]
    return w.reshape(heads, SLAB, n)[:, :width, :].reshape(heads * width, n)


def _pad_w_in(w_in):
    o = 2 * D_MODEL
    qa = _pad_heads_cols(w_in[:, o:o + 512], N_HEADS, HEAD_A)
    ka = _pad_heads_cols(w_in[:, o + 512:o + 640], N_KV_A, HEAD_A)
    va = _pad_heads_cols(w_in[:, o + 640:o + 768], N_KV_A, HEAD_A)
    kr = jnp.pad(w_in[:, o + 1152:o + 1184], ((0, 0), (QK_NOPE, SLAB - QK_NOPE - QK_ROPE)))
    return jnp.concatenate([w_in[:, :o], qa, ka, va, w_in[:, o + 768:o + 1152], kr], axis=1)


def _unpad_w_in(w):
    qa = _unpad_heads_cols(w[:, C_QA:C_KA], N_HEADS, HEAD_A)
    ka = _unpad_heads_cols(w[:, C_KA:C_VA], N_KV_A, HEAD_A)
    va = _unpad_heads_cols(w[:, C_VA:C_CQ], N_KV_A, HEAD_A)
    kr = w[:, C_KR + QK_NOPE:C_KR + QK_NOPE + QK_ROPE]
    return jnp.concatenate([w[:, :C_QA], qa, ka, va, w[:, C_CQ:C_KR], kr], axis=1)


def _pad_w_kvb(w_kvb):
    w = w_kvb.reshape(KV_LORA, N_HEADS, QK_NOPE + V_DIM_B)
    k = jnp.pad(w[:, :, :QK_NOPE], ((0, 0), (0, 0), (0, SLAB - QK_NOPE))).reshape(KV_LORA, HM)
    v = jnp.pad(w[:, :, QK_NOPE:], ((0, 0), (0, 0), (0, SLAB - V_DIM_B))).reshape(KV_LORA, HM)
    return jnp.concatenate([k, v], axis=1)


def _unpad_w_kvb(w):
    k = w[:, :HM].reshape(KV_LORA, N_HEADS, SLAB)[:, :, :QK_NOPE]
    v = w[:, HM:].reshape(KV_LORA, N_HEADS, SLAB)[:, :, :V_DIM_B]
    return jnp.concatenate([k, v], axis=2).reshape(KV_LORA, N_HEADS * (QK_NOPE + V_DIM_B))


def _col_shards(w):
    k, n = w.shape
    return w.reshape(k, N_DEV, n // N_DEV).transpose(1, 0, 2).reshape(N_DEV, -1, 1024)


def _row_shards(w):
    return w.reshape(N_DEV, -1, 1024)


def _from_col_shards(s, k, n):
    return s.reshape(N_DEV, k, n // N_DEV).transpose(1, 0, 2).reshape(k, n)


def _flatten_shards(parts, layout):
    pieces, used = [], 0
    for name, r in layout:
        p = parts[name]
        pad = [(0, 0)] * (p.ndim - 2) + [(0, -r % FLAT_ALIGN), (0, 0)]
        pieces.append(jnp.pad(p, pad))
        used += r + (-r % FLAT_ALIGN)
    tail = _flat_rows(layout) - used
    if tail:
        pieces.append(jnp.zeros(pieces[0].shape[:-2] + (tail, 1024), pieces[0].dtype))
    return jnp.concatenate(pieces, axis=-2)


def _split_flat(flat, layout):
    out, o = {}, 0
    for name, r in layout:
        out[name] = flat[..., o:o + r, :]
        o += r + (-r % FLAT_ALIGN)
    return out


def _freq_row():
    freqs = ROPE_THETA ** (-jnp.arange(0, QK_ROPE, 2, dtype=F32) / QK_ROPE)
    return jnp.concatenate([jnp.zeros((QK_NOPE,), F32), freqs, freqs,
                            jnp.zeros((SLAB - QK_NOPE - QK_ROPE,), F32)]).reshape(1, SLAB)


SMALL_D_ROWS = ("pre_norm_mix", "post_norm_mix", "pre_norm_mlp", "post_norm_mlp")
SMALL_Q_OFF, SMALL_KV_OFF, SMALL_SINK_OFF, SMALL_LOSS_OFF = 0, 256, 384, 392


def _pack_small(vals):
    row4 = jnp.concatenate([vals["q_a_norm"].reshape(-1), vals["kv_a_norm"].reshape(-1), vals["sinks"].reshape(-1),
                            vals["loss"].reshape(-1), jnp.zeros((1024 - 393,), F32)])
    rows = [vals[n].reshape(1024) for n in SMALL_D_ROWS] + [row4]
    return jnp.concatenate([jnp.stack(rows), jnp.zeros((SMALL_ROWS - 5, 1024), F32)], axis=0)


def _unpack_small(blk):
    out = {n: blk[i].reshape(1, 1024) for i, n in enumerate(SMALL_D_ROWS)}
    out["q_a_norm"] = blk[4, SMALL_Q_OFF:SMALL_Q_OFF + 256].reshape(1, 256)
    out["kv_a_norm"] = blk[4, SMALL_KV_OFF:SMALL_KV_OFF + 128].reshape(1, 128)
    out["sinks"] = blk[4, SMALL_SINK_OFF:SMALL_SINK_OFF + 8].reshape(1, 8)
    out["loss"] = blk[4, SMALL_LOSS_OFF]
    return out


COL_SHARDED = {"w_in": (D_MODEL, 3232), "w_q_b": (Q_LORA, 768), "w_kv_b": (KV_LORA, 1024),
               "w_o_a": (512, D_MODEL), "w_o_b": (512, D_MODEL), "w_up": (D_MODEL, D_FF)}
ROW_SHARDED = {"w_out": (D_MODEL, D_MODEL), "w_down": (D_FF, D_MODEL)}
WEIGHT_ORDER = ("pre_norm_mix", "w_in", "q_a_norm", "w_q_b", "kv_a_norm", "w_kv_b", "sinks", "w_o_a", "w_o_b",
                "w_out", "post_norm_mix", "pre_norm_mlp", "w_up", "w_down", "post_norm_mlp")
SMALL_NAMES = ("pre_norm_mix", "q_a_norm", "kv_a_norm", "sinks", "post_norm_mix", "pre_norm_mlp", "post_norm_mlp")


def _flat_local(tensors, layout):
    return _flatten_shards({name: tensors[name].reshape(-1, 1024) for name, _ in layout}, layout)


def _full_weights(gathered, layout):
    out = {}
    for name, s in _split_flat(gathered, layout).items():
        k, n = COL_SHARDED.get(name) or ROW_SHARDED[name]
        out[name] = _from_col_shards(s, k, n) if name in COL_SHARDED else s.reshape(k, n)
    return out


def _grad_slices(grads, layout):
    shards = {name: (_col_shards(grads[name]) if name in COL_SHARDED else _row_shards(grads[name]))
              for name, _ in layout}
    return _flatten_shards(shards, layout).astype(BF16)


def kernel(x, positions, pre_norm_mix, w_in, q_a_norm, w_q_b, kv_a_norm, w_kv_b, sinks, w_o_a, w_o_b, w_out, post_norm_mix, pre_norm_mlp, w_up, w_down, post_norm_mlp, loss_target, m_pre_norm_mix, m_w_in, m_q_a_norm, m_w_q_b, m_kv_a_norm, m_w_kv_b, m_sinks, m_w_o_a, m_w_o_b, m_w_out, m_post_norm_mix, m_pre_norm_mlp, m_w_up, m_w_down, m_post_norm_mlp, v_pre_norm_mix, v_w_in, v_q_a_norm, v_w_q_b, v_kv_a_norm, v_w_kv_b, v_sinks, v_w_o_a, v_w_o_b, v_w_out, v_post_norm_mix, v_pre_norm_mlp, v_w_up, v_w_down, v_post_norm_mlp):
    weights = dict(pre_norm_mix=pre_norm_mix, w_in=w_in, q_a_norm=q_a_norm, w_q_b=w_q_b, kv_a_norm=kv_a_norm,
                   w_kv_b=w_kv_b, sinks=sinks, w_o_a=w_o_a, w_o_b=w_o_b, w_out=w_out, post_norm_mix=post_norm_mix,
                   pre_norm_mlp=pre_norm_mlp, w_up=w_up, w_down=w_down, post_norm_mlp=post_norm_mlp)
    m_in = dict(pre_norm_mix=m_pre_norm_mix, w_in=m_w_in, q_a_norm=m_q_a_norm, w_q_b=m_w_q_b, kv_a_norm=m_kv_a_norm,
                w_kv_b=m_w_kv_b, sinks=m_sinks, w_o_a=m_w_o_a, w_o_b=m_w_o_b, w_out=m_w_out,
                post_norm_mix=m_post_norm_mix, pre_norm_mlp=m_pre_norm_mlp, w_up=m_w_up, w_down=m_w_down,
                post_norm_mlp=m_post_norm_mlp)
    v_in = dict(pre_norm_mix=v_pre_norm_mix, w_in=v_w_in, q_a_norm=v_q_a_norm, w_q_b=v_w_q_b, kv_a_norm=v_kv_a_norm,
                w_kv_b=v_w_kv_b, sinks=v_sinks, w_o_a=v_w_o_a, w_o_b=v_w_o_b, w_out=v_w_out,
                post_norm_mix=v_post_norm_mix, pre_norm_mlp=v_pre_norm_mlp, w_up=v_w_up, w_down=v_w_down,
                post_norm_mlp=v_post_norm_mlp)

    xs, pos, target = x[0], positions[0], loss_target[0]
    t = xs.shape[0]
    pos_col = pos.reshape(t, 1)
    pos_row = pos.reshape(1, t)
    g1, g2, g3, g4 = (weights[n] for n in SMALL_D_ROWS)
    g_q, g_kv = q_a_norm, kv_a_norm
    sink_vec = sinks.reshape(N_HEADS)

    early = _full_weights(_all_gather_flat(_flat_local(weights, EARLY_ROWS).astype(BF16)), EARLY_ROWS)
    w_in_p = _pad_w_in(early["w_in"])
    w_qb = _pad_heads_cols(early["w_q_b"], N_HEADS, QK_NOPE + QK_ROPE)
    w_kvb = _pad_w_kvb(early["w_kv_b"])

    rc, rs1, rs2 = _rope_tables(pos_col, _freq_row())
    (h, gates, qa, ka, va, cq, ckv, cqn, ckvn, qb, kb, vb) = _inproj_fwd(
        xs, g1, w_in_p, g_q, w_qb, g_kv, w_kvb, rc, rs1, rs2)
    out_a, lse_a = _swa_fwd(qa, ka, va, pos_col, pos_row, sink_vec)
    out_b, lse_b, late_all = _mla_fwd(qb, kb, vb, _flat_local(weights, LATE_ROWS).astype(BF16))
    late = _full_weights(late_all, LATE_ROWS)
    w_oa = _pad_heads_rows(late["w_o_a"], N_HEADS, HEAD_A)
    w_ob = _pad_heads_rows(late["w_o_b"], N_HEADS, V_DIM_B)

    oa_p, ob_p, merged, y, x1, h2 = _merge_fwd(out_a, out_b, gates, xs, w_oa, w_ob, late["w_out"], g2, g3)
    a, du, dy2, dx1, loss, dg3, dg4 = _mlp_fwd_bwd(x1, h2, target, late["w_up"], late["w_down"], g3, g4)
    (dy, d_oap, d_obp, dgates, d_oa, d_ob, delta_a, delta_b, dg2) = _merge_bwd(
        dx1, y, gates, oa_p, ob_p, out_a, out_b, w_oa, w_ob, late["w_out"], g2)
    late_grads = {
        "w_o_a": _unpad_heads_rows(_matmul_tn(out_a, d_oap, "dw_o_a"), N_HEADS, HEAD_A),
        "w_o_b": _unpad_heads_rows(_matmul_tn(out_b, d_obp, "dw_o_b"), N_HEADS, V_DIM_B),
        "w_out": _matmul_tn(merged, dy, "dw_out"),
        "w_up": _matmul_tn(h2, du, "dw_up"),
        "w_down": _matmul_tn(a, dy2, "dw_down"),
    }
    dqa, dka, dva, dsink = _swa_bwd(qa, ka, va, d_oa, lse_a, delta_a.reshape(N_HEADS, 1, t), pos_col, pos_row,
                                     sink_vec)
    dqb, dkb, dvb, late_parts = _mla_bwd(qb, kb, vb, d_ob, lse_b, delta_b, _grad_slices(late_grads, LATE_ROWS))
    dproj, dqbr, dkvb, dx, dg1, dgq, dgkv = _inproj_bwd(
        dgates, dqa, dka, dva, dqb, dkb, dvb, cq, ckv, xs, dx1, rc, rs1, rs2, g1, g_q, g_kv, w_in_p, w_qb, w_kvb)
    early_grads = {
        "w_in": _unpad_w_in(_matmul_tn(h, dproj, "dw_in")),
        "w_q_b": _unpad_heads_cols(_matmul_tn(cqn, dqbr, "dw_q_b"), N_HEADS, QK_NOPE + QK_ROPE),
        "w_kv_b": _unpad_w_kvb(_matmul_tn(ckvn, dkvb, "dw_kv_b")),
    }
    small_grads = {"pre_norm_mix": dg1, "post_norm_mix": dg2, "pre_norm_mlp": dg3, "post_norm_mlp": dg4,
                   "q_a_norm": dgq, "kv_a_norm": dgkv, "sinks": dsink.reshape(N_HEADS, BLOCK).sum(axis=1), "loss": loss[0, 0:1]}
    early_parts, s_parts = _exchange_grads(_grad_slices(early_grads, EARLY_ROWS), _pack_small(small_grads))

    zero = jnp.zeros((), F32)
    pack = lambda src: _pack_small({**{n: src[n] for n in SMALL_NAMES}, "loss": zero})
    updates = {}
    for parts, layout, name in ((late_parts, LATE_ROWS, "adamw_late"), (early_parts, EARLY_ROWS, "adamw_early")):
        flats = _adamw(parts, _flat_local(weights, layout), _flat_local(m_in, layout), _flat_local(v_in, layout), name)
        for kind, flat in zip(("g", "d", "m", "v"), flats):
            for wname, piece in _split_flat(flat, layout).items():
                updates[kind, wname] = piece.reshape(weights[wname].shape)
    smalls = _adamw(s_parts, pack(weights), pack(m_in), pack(v_in), "adamw_small")
    for kind, blk in zip(("g", "d", "m", "v"), smalls):
        for wname, piece in _unpack_small(blk).items():
            updates[kind, wname] = piece
    results = [updates[kind, name] for kind in ("g", "d", "m", "v") for name in WEIGHT_ORDER]
    return (updates["g", "loss"], dx[None], *results)
```

```python
import functools

import numpy as np
import jax
import jax.numpy as jnp
from jax import lax
from jax.experimental import pallas as pl
from jax.experimental.pallas import tpu as pltpu

F32 = jnp.float32
BF16 = jnp.bfloat16

D_MODEL = 1024
D_FF = 4096
N_HEADS = 8
N_KV_A = 2
GROUP_A = N_HEADS // N_KV_A
HEAD_A = 64
QK_NOPE = 64
QK_ROPE = 32
V_DIM_B = 64
Q_LORA = 256
KV_LORA = 128
BLOCK = 128
SLAB = 128
ROPE_THETA = 10000.0
EPS = 1e-6
N_DEV = 8
NEG = -1e30

SCALE_A = HEAD_A ** -0.5
SCALE_B = (QK_NOPE + QK_ROPE) ** -0.5
LOG2E = 1.4426950408889634
SCORE_B = SCALE_B * LOG2E
MLA_HEADS_PER_STEP = 4
SLOPES_A = tuple(2.0 ** (-8.0 * (h + 1) / N_HEADS) for h in range(N_HEADS))

ADAM_LR = 0.001
ADAM_B1 = 0.9
ADAM_B2 = 0.999
ADAM_EPS = 1e-08
ADAM_WD = 0.01
ADAM_STEP = 10

HM = N_HEADS * SLAB
C_GATES = 0
C_QA = 2 * D_MODEL
C_KA = C_QA + HM
C_VA = C_KA + N_KV_A * SLAB
C_CQ = C_VA + N_KV_A * SLAB
C_CKV = C_CQ + Q_LORA
C_KR = C_CKV + KV_LORA
D_IN_PAD = C_KR + SLAB

VMEM_LIMIT = 56 * 1024 * 1024

EARLY = ("w_in", "w_q_b", "w_kv_b")
LATE = ("w_o_a", "w_o_b", "w_out", "w_up", "w_down")
ADAM_ROWS = 256
SMALL_ROWS = 8


def _token_tile(t):
    return min(256, t)


def _attn_tile(t):
    return 512 if t >= 2048 else 128


def _params(sem, vmem=VMEM_LIMIT):
    return pltpu.CompilerParams(dimension_semantics=sem, vmem_limit_bytes=vmem)


def _dot(a, b):
    return jnp.dot(a, b, preferred_element_type=F32)


def _dot_nt(a, b):
    return lax.dot_general(a, b, (((1,), (1,)), ((), ())), preferred_element_type=F32)


def _dot_tn(a, b):
    return lax.dot_general(a, b, (((0,), (0,)), ((), ())), preferred_element_type=F32)


def _rms_r(x):
    return lax.rsqrt(jnp.mean(x * x, axis=-1, keepdims=True) + EPS)


def _rms_bwd(x, r, g, dy):
    t = dy * g
    return r * t - x * (r * r * r) * jnp.mean(x * t, axis=-1, keepdims=True)


def _sigmoid(x):
    return 1.0 / (1.0 + jnp.exp(-x))


def _rope(x, c, s1, s2):
    return x * c + pltpu.roll(x, SLAB - 16, 1) * s1 + pltpu.roll(x, 16, 1) * s2


def _rope_bwd(d, c, s1, s2):
    return d * c + pltpu.roll(d * s1, 16, 1) + pltpu.roll(d * s2, SLAB - 16, 1)


def _row_spec(tm, n):
    return pl.BlockSpec((tm, n), lambda i: (i, 0))


def _full_spec(shape):
    nd = len(shape)
    return pl.BlockSpec(shape, lambda i: (0,) * nd, pipeline_mode=pl.Buffered(1))


def _acc_rows(ref, val):
    @pl.when(pl.program_id(0) == 0)
    def _():
        ref[...] = jnp.zeros_like(ref)
    ref[...] += jnp.sum(val, axis=0, keepdims=True)


def _rope_tables(pos_col, freq_row):
    t = pos_col.shape[0]
    tm = _token_tile(t)

    def body(pos_ref, f_ref, c_ref, s1_ref, s2_ref):
        ang = pos_ref[...].astype(F32) * f_ref[...]
        lane = lax.broadcasted_iota(jnp.int32, ang.shape, 1)
        s = jnp.sin(ang)
        c_ref[...] = jnp.cos(ang)
        s1_ref[...] = jnp.where((lane >= 64) & (lane < 80), -s, 0.0)
        s2_ref[...] = jnp.where((lane >= 80) & (lane < 96), s, 0.0)

    tab = jax.ShapeDtypeStruct((t, SLAB), F32)
    return pl.pallas_call(
        body, name="rope_tables", grid=(t // tm,),
        in_specs=[_row_spec(tm, 1), _full_spec((1, SLAB))],
        out_specs=[_row_spec(tm, SLAB)] * 3, out_shape=[tab] * 3,
        compiler_params=_params(("parallel",)),
    )(pos_col, freq_row)


def _inproj_fwd(x, g1, w_in, g_q, w_qb, g_kv, w_kvb, rope_c, rope_s1, rope_s2):
    t = x.shape[0]
    tm = _token_tile(t)

    def body(x_ref, g1_ref, win_ref, gq_ref, wqb_ref, gkv_ref, wkvb_ref, c_ref, s1_ref, s2_ref,
             h_ref, gates_ref, qa_ref, ka_ref, va_ref, cq_ref, ckv_ref, cqn_ref, ckvn_ref,
             qb_ref, kb_ref, vb_ref):
        xv = x_ref[...]
        h = (xv * _rms_r(xv) * g1_ref[...]).astype(BF16)
        h_ref[...] = h
        proj = _dot(h, win_ref[...])
        gates_ref[...] = proj[:, C_GATES:C_QA]
        qa_ref[...] = proj[:, C_QA:C_KA].astype(BF16)
        ka_ref[...] = proj[:, C_KA:C_VA].astype(BF16)
        va_ref[...] = proj[:, C_VA:C_CQ].astype(BF16)
        cq = proj[:, C_CQ:C_CKV]
        ckv = proj[:, C_CKV:C_KR]
        kr = proj[:, C_KR:D_IN_PAD]
        cq_ref[...] = cq
        ckv_ref[...] = ckv
        cqn = (cq * _rms_r(cq) * gq_ref[...]).astype(BF16)
        ckvn = (ckv * _rms_r(ckv) * gkv_ref[...]).astype(BF16)
        cqn_ref[...] = cqn
        ckvn_ref[...] = ckvn
        c, s1, s2 = c_ref[...], s1_ref[...], s2_ref[...]
        qb = _dot(cqn, wqb_ref[...])
        kvb = _dot(ckvn, wkvb_ref[...])
        kr_rot = _rope(kr, c, s1, s2)
        for hd in range(N_HEADS):
            sl = slice(hd * SLAB, (hd + 1) * SLAB)
            qb_ref[:, sl] = (_rope(qb[:, sl], c, s1, s2) * SCORE_B).astype(BF16)
            kb_ref[:, sl] = (kvb[:, sl] + kr_rot).astype(BF16)
        vb_ref[...] = kvb[:, HM:2 * HM].astype(BF16)

    def sds(n, dt):
        return jax.ShapeDtypeStruct((t, n), dt)

    outs = [(D_MODEL, BF16), (2 * D_MODEL, F32), (HM, BF16), (N_KV_A * SLAB, BF16), (N_KV_A * SLAB, BF16),
            (Q_LORA, F32), (KV_LORA, F32), (Q_LORA, BF16), (KV_LORA, BF16), (HM, BF16), (HM, BF16), (HM, BF16)]
    return pl.pallas_call(
        body, name="inproj_fwd", grid=(t // tm,),
        in_specs=[_row_spec(tm, D_MODEL), _full_spec((1, D_MODEL)), _full_spec((D_MODEL, D_IN_PAD)),
                  _full_spec((1, Q_LORA)), _full_spec((Q_LORA, HM)), _full_spec((1, KV_LORA)),
                  _full_spec((KV_LORA, 2 * HM)), _row_spec(tm, SLAB), _row_spec(tm, SLAB), _row_spec(tm, SLAB)],
        out_specs=[_row_spec(tm, n) for n, _ in outs],
        out_shape=[sds(n, dt) for n, dt in outs],
        compiler_params=_params(("parallel",)),
    )(x, g1, w_in, g_q, w_qb, g_kv, w_kvb, rope_c, rope_s1, rope_s2)


def _tile_group(a):
    return jnp.concatenate([a] * GROUP_A, axis=1)


def _swa_masks():
    row = lax.broadcasted_iota(jnp.int32, (BLOCK, GROUP_A * BLOCK), 0)
    col = lax.broadcasted_iota(jnp.int32, (BLOCK, GROUP_A * BLOCK), 1) & (BLOCK - 1)
    return row <= col, row > col


def _heads_beside(ref, g):
    return jnp.concatenate([ref[:, (g * GROUP_A + hh) * SLAB:(g * GROUP_A + hh + 1) * SLAB].T
                            for hh in range(GROUP_A)], axis=1)


def _rows_beside(ref, g):
    return jnp.concatenate([ref[g * GROUP_A + hh] for hh in range(GROUP_A)], axis=1)


def _swa_rows(sinks):
    slopes = jnp.repeat(jnp.asarray(SLOPES_A, F32).reshape(N_KV_A, GROUP_A, 1), BLOCK, axis=2)
    sink_rows = jnp.repeat(sinks.reshape(N_KV_A, GROUP_A, 1), BLOCK, axis=2)
    return slopes.reshape(N_KV_A, 1, GROUP_A * BLOCK), sink_rows.reshape(N_KV_A, 1, GROUP_A * BLOCK)


def _swa_fwd(qa, ka, va, pos_col, pos_row, sinks):
    t = qa.shape[0]
    nb = t // BLOCK
    gw = GROUP_A * BLOCK
    slope_rows, sink_rows = _swa_rows(sinks)

    def body(q_ref, kc_ref, kp_ref, vc_ref, vp_ref, pkc_ref, pkp_ref, pq_ref, slope_ref, sink_ref, o_ref, l_ref):
        i = pl.program_id(0)
        pq = pq_ref[...]
        dist_c = _tile_group(jnp.abs(pkc_ref[...] - pq).astype(F32))
        dist_p = _tile_group(jnp.abs(pkp_ref[...] - pq).astype(F32))
        mask_c, older = _swa_masks()
        mask_p = jnp.logical_and(older, i > 0)
        for g in range(N_KV_A):
            gs = slice(g * SLAB, (g + 1) * SLAB)
            x = _heads_beside(q_ref, g)
            slope, sink = slope_ref[g], sink_ref[g]
            s_c = jnp.where(mask_c, _dot(kc_ref[:, gs], x) * SCALE_A - slope * dist_c, NEG)
            s_p = jnp.where(mask_p, _dot(kp_ref[:, gs], x) * SCALE_A - slope * dist_p, NEG)
            m = jnp.maximum(jnp.maximum(jnp.max(s_c, axis=0, keepdims=True),
                                        jnp.max(s_p, axis=0, keepdims=True)), sink)
            e_c = jnp.exp(s_c - m)
            e_p = jnp.exp(s_p - m)
            den = jnp.sum(e_c, axis=0, keepdims=True) + jnp.sum(e_p, axis=0, keepdims=True) + jnp.exp(sink - m)
            inv = 1.0 / den
            ot = (_dot_tn(vc_ref[:, gs], (e_c * inv).astype(BF16))
                  + _dot_tn(vp_ref[:, gs], (e_p * inv).astype(BF16)))
            lse = m + jnp.log(den)
            for hh in range(GROUP_A):
                hd = g * GROUP_A + hh
                seg = slice(hh * BLOCK, (hh + 1) * BLOCK)
                o_ref[:, hd * SLAB:(hd + 1) * SLAB] = ot[:, seg].T.astype(BF16)
                l_ref[hd] = lse[:, seg]

    cur = lambda i: (i, 0)
    prev = lambda i: (jnp.maximum(i - 1, 0), 0)
    kvw = N_KV_A * SLAB
    rows = pl.BlockSpec((N_KV_A, 1, gw), lambda i: (0, 0, 0))
    return pl.pallas_call(
        body, name="swa_fwd", grid=(nb,),
        in_specs=[pl.BlockSpec((BLOCK, HM), cur),
                  pl.BlockSpec((BLOCK, kvw), cur), pl.BlockSpec((BLOCK, kvw), prev),
                  pl.BlockSpec((BLOCK, kvw), cur), pl.BlockSpec((BLOCK, kvw), prev),
                  pl.BlockSpec((BLOCK, 1), cur), pl.BlockSpec((BLOCK, 1), prev),
                  pl.BlockSpec((1, BLOCK), lambda i: (0, i)), rows, rows],
        out_specs=[pl.BlockSpec((BLOCK, HM), cur), pl.BlockSpec((N_HEADS, 1, BLOCK), lambda i: (0, 0, i))],
        out_shape=[jax.ShapeDtypeStruct((t, HM), BF16), jax.ShapeDtypeStruct((N_HEADS, 1, t), F32)],
        compiler_params=_params(("parallel",)),
    )(qa, ka, ka, va, va, pos_col, pos_col, pos_row, slope_rows, sink_rows)


def _swa_bwd(qa, ka, va, d_oa, lse, delta, pos_col, pos_row, sinks):
    t = qa.shape[0]
    nb = t // BLOCK
    gw = GROUP_A * BLOCK
    slope_rows, sink_rows = _swa_rows(sinks)

    def body(q_ref, qn_ref, do_ref, don_ref, l_ref, ln_ref, dl_ref, dln_ref, kp_ref, kc_ref, vp_ref, vc_ref,
             pkp_ref, pkc_ref, pq_ref, pqn_ref, slope_ref, sink_ref, dq_ref, dk_ref, dv_ref, dsink_ref):
        j = pl.program_id(0)
        pkc, pkp = pkc_ref[...], pkp_ref[...]
        dist_cc = _tile_group(jnp.abs(pkc - pq_ref[...]).astype(F32))
        dist_cp = _tile_group(jnp.abs(pkp - pq_ref[...]).astype(F32))
        dist_nc = _tile_group(jnp.abs(pkc - pqn_ref[...]).astype(F32))
        mask_cc, older = _swa_masks()
        mask_cp = jnp.logical_and(older, j > 0)
        mask_nc = jnp.logical_and(older, j < nb - 1)

        @pl.when(j == 0)
        def _():
            dsink_ref[...] = jnp.zeros_like(dsink_ref)

        def tile(k, v, x, dox, lrow, drow, dist, mask, slope):
            s = jnp.where(mask, _dot(k, x) * SCALE_A - slope * dist, NEG)
            p = jnp.exp(s - lrow)
            ds = p * (_dot(v, dox) - drow)
            return p.astype(BF16), ds.astype(BF16)

        for g in range(N_KV_A):
            gs = slice(g * SLAB, (g + 1) * SLAB)
            kc, kp, vc, vp = kc_ref[:, gs], kp_ref[:, gs], vc_ref[:, gs], vp_ref[:, gs]
            slope, sink = slope_ref[g], sink_ref[g]
            x, xn = _heads_beside(q_ref, g), _heads_beside(qn_ref, g)
            dox, doxn = _heads_beside(do_ref, g), _heads_beside(don_ref, g)
            lrow, drow = _rows_beside(l_ref, g), _rows_beside(dl_ref, g)
            lrown, drown = _rows_beside(ln_ref, g), _rows_beside(dln_ref, g)
            p_cc, ds_cc = tile(kc, vc, x, dox, lrow, drow, dist_cc, mask_cc, slope)
            _, ds_cp = tile(kp, vp, x, dox, lrow, drow, dist_cp, mask_cp, slope)
            p_nc, ds_nc = tile(kc, vc, xn, doxn, lrown, drown, dist_nc, mask_nc, slope)
            dqt = (_dot_tn(kc, ds_cc) + _dot_tn(kp, ds_cp)) * SCALE_A
            for hh in range(GROUP_A):
                hd = g * GROUP_A + hh
                dq_ref[:, hd * SLAB:(hd + 1) * SLAB] = dqt[:, hh * BLOCK:(hh + 1) * BLOCK].T.astype(BF16)
            dk_ref[:, gs] = ((_dot_nt(ds_cc, x) + _dot_nt(ds_nc, xn)) * SCALE_A).astype(BF16)
            dv_ref[:, gs] = (_dot_nt(p_cc, dox) + _dot_nt(p_nc, doxn)).astype(BF16)
            dsink_ref[g] -= jnp.exp(sink - lrow) * drow

    cur = lambda j: (j, 0)
    prev = lambda j: (jnp.maximum(j - 1, 0), 0)
    nxt = lambda j: (jnp.minimum(j + 1, nb - 1), 0)
    cur3 = lambda j: (0, 0, j)
    nxt3 = lambda j: (0, 0, jnp.minimum(j + 1, nb - 1))
    kvw = N_KV_A * SLAB
    rows = pl.BlockSpec((N_KV_A, 1, gw), lambda j: (0, 0, 0))
    stat = lambda im: pl.BlockSpec((N_HEADS, 1, BLOCK), im)
    return pl.pallas_call(
        body, name="swa_bwd", grid=(nb,),
        in_specs=[pl.BlockSpec((BLOCK, HM), cur), pl.BlockSpec((BLOCK, HM), nxt),
                  pl.BlockSpec((BLOCK, HM), cur), pl.BlockSpec((BLOCK, HM), nxt),
                  stat(cur3), stat(nxt3), stat(cur3), stat(nxt3),
                  pl.BlockSpec((BLOCK, kvw), prev), pl.BlockSpec((BLOCK, kvw), cur),
                  pl.BlockSpec((BLOCK, kvw), prev), pl.BlockSpec((BLOCK, kvw), cur),
                  pl.BlockSpec((BLOCK, 1), prev), pl.BlockSpec((BLOCK, 1), cur),
                  pl.BlockSpec((1, BLOCK), lambda j: (0, j)),
                  pl.BlockSpec((1, BLOCK), lambda j: (0, jnp.minimum(j + 1, nb - 1))), rows, rows],
        out_specs=[pl.BlockSpec((BLOCK, HM), cur), pl.BlockSpec((BLOCK, kvw), cur),
                   pl.BlockSpec((BLOCK, kvw), cur), rows],
        out_shape=[jax.ShapeDtypeStruct((t, HM), BF16), jax.ShapeDtypeStruct((t, kvw), BF16),
                   jax.ShapeDtypeStruct((t, kvw), BF16), jax.ShapeDtypeStruct((N_KV_A, 1, gw), F32)],
        compiler_params=_params(("arbitrary",)),
    )(qa, qa, d_oa, d_oa, lse, lse, delta, delta, ka, ka, va, va,
      pos_col, pos_col, pos_row, pos_row, slope_rows, sink_rows)


def _lower_triangle(n):
    row = lax.broadcasted_iota(jnp.int32, (n, n), 0)
    col = lax.broadcasted_iota(jnp.int32, (n, n), 1)
    return row >= col


def _upper_triangle(n):
    row = lax.broadcasted_iota(jnp.int32, (n, n), 0)
    col = lax.broadcasted_iota(jnp.int32, (n, n), 1)
    return row <= col


def _mesh_pos():
    return lax.axis_index("x"), lax.axis_index("y"), lax.axis_index("c")


def _flip(v, bit):
    return 1 - v if bit else v


def _direct_copies(srcs, dsts, send_sems, recv_sems, local_sems, gather, sem_base=0):
    x, y, c = _mesh_pos()
    me = 4 * x + 2 * y + c
    local, remote = [], []
    for a, (src, dst) in enumerate(zip(srcs, dsts)):
        local.append(pltpu.make_async_copy(src if gather else src.at[me], dst.at[me], local_sems.at[sem_base + a]))
        for r in range(1, N_DEV):
            px, py, pc = _flip(x, r & 4), _flip(y, r & 2), _flip(c, r & 1)
            sem = (N_DEV - 1) * (sem_base + a) + r - 1
            remote.append(pltpu.make_async_remote_copy(
                src_ref=src if gather else src.at[4 * px + 2 * py + pc], dst_ref=dst.at[me],
                send_sem=send_sems.at[sem], recv_sem=recv_sems.at[sem],
                device_id=(px, py, pc), device_id_type=pl.DeviceIdType.MESH))
    return local, remote


def _start_copies(local, remote):
    for cp in local + remote:
        cp.start()


def _wait_copies(local, remote):
    for cp in remote:
        cp.wait_recv()
    for cp in remote:
        cp.wait_send()
    for cp in local:
        cp.wait()


def _exchange_scratch(n):
    return [pltpu.SemaphoreType.DMA((n * (N_DEV - 1),)), pltpu.SemaphoreType.DMA((n * (N_DEV - 1),)),
            pltpu.SemaphoreType.DMA((n,))]


ANY_SPEC = pl.BlockSpec(memory_space=pl.ANY)


def _mla_fwd(qb, kb, vb, late):
    t = qb.shape[0]
    tq = _attn_tile(t)
    nt = t // tq
    hps = MLA_HEADS_PER_STEP
    w = hps * SLAB
    pairs = [(i, j) for i in range(nt) for j in range(i + 1)]
    i_tab = jnp.asarray(np.array([p[0] for p in pairs], np.int32))
    j_tab = jnp.asarray(np.array([p[1] for p in pairs], np.int32))

    n_late = len(late)

    def body(it_ref, jt_ref, q_ref, k_ref, vt_ref, *rest):
        late_refs, (o_ref, l_ref) = rest[:n_late], rest[n_late:n_late + 2]
        gathered_refs = rest[n_late + 2:2 * n_late + 2]
        m_s, l_s, acc_s, send_sems, recv_sems, local_sems = rest[2 * n_late + 2:]
        n = pl.program_id(1)
        i, j = it_ref[n], jt_ref[n]
        first_step = jnp.logical_and(pl.program_id(0) == 0, n == 0)
        last_step = jnp.logical_and(pl.program_id(0) == N_HEADS // hps - 1, n == len(pairs) - 1)

        @pl.when(first_step)
        def _():
            _start_copies(*_direct_copies(late_refs, gathered_refs, send_sems, recv_sems, local_sems, True))

        @pl.when(j == 0)
        def _():
            m_s[...] = jnp.full_like(m_s, NEG)
            l_s[...] = jnp.zeros_like(l_s)
            acc_s[...] = jnp.zeros_like(acc_s)

        def update(masked):
            def scores(hh):
                sl = slice(hh * SLAB, (hh + 1) * SLAB)
                return _dot_nt(k_ref[:, sl], q_ref[:, sl])

            def softmax(hh, s):
                if masked:
                    s = jnp.where(_upper_triangle(tq), s, NEG)
                m_old = m_s[hh]
                m_new = jnp.maximum(m_old, jnp.max(s, axis=0, keepdims=True))
                alpha = jnp.exp2(m_old - m_new)
                p = jnp.exp2(s - m_new)
                l_s[hh] = alpha * l_s[hh] + jnp.sum(p, axis=0, keepdims=True)
                m_s[hh] = m_new
                return p.astype(BF16), alpha

            def accumulate(hh, p, alpha):
                sl = slice(hh * SLAB, hh * SLAB + V_DIM_B)
                acc_s[sl, :] = alpha * acc_s[sl, :] + _dot(vt_ref[sl, :], p)

            s_next, pending = scores(0), None
            for hh in range(hps):
                s = s_next
                if hh + 1 < hps:
                    s_next = scores(hh + 1)
                p, alpha = softmax(hh, s)
                if pending is not None:
                    accumulate(*pending)
                pending = (hh, p, alpha)
            accumulate(*pending)

        @pl.when(j < i)
        def _():
            update(False)

        @pl.when(j == i)
        def _():
            update(True)
            for hh in range(hps):
                sl = slice(hh * SLAB, (hh + 1) * SLAB)
                o_ref[:, sl] = (acc_s[sl, :] / l_s[hh]).T.astype(BF16)
                l_ref[hh] = m_s[hh] + jnp.log2(l_s[hh])

        @pl.when(last_step)
        def _():
            _wait_copies(*_direct_copies(late_refs, gathered_refs, send_sems, recv_sems, local_sems, True))

    grid_spec = pltpu.PrefetchScalarGridSpec(
        num_scalar_prefetch=2, grid=(N_HEADS // hps, len(pairs)),
        in_specs=[pl.BlockSpec((tq, w), lambda h, n, it, jt: (it[n], h)),
                  pl.BlockSpec((tq, w), lambda h, n, it, jt: (jt[n], h)),
                  pl.BlockSpec((w, tq), lambda h, n, it, jt: (h, jt[n]))] + [ANY_SPEC] * n_late,
        out_specs=[pl.BlockSpec((tq, w), lambda h, n, it, jt: (it[n], h)),
                   pl.BlockSpec((hps, 1, tq), lambda h, n, it, jt: (h, 0, it[n]))] + [ANY_SPEC] * n_late,
        scratch_shapes=[pltpu.VMEM((hps, 1, tq), F32), pltpu.VMEM((hps, 1, tq), F32), pltpu.VMEM((w, tq), F32)]
        + _exchange_scratch(n_late))
    outs = pl.pallas_call(
        body, name="mla_fwd", grid_spec=grid_spec,
        out_shape=[jax.ShapeDtypeStruct((t, HM), BF16), jax.ShapeDtypeStruct((N_HEADS, 1, t), F32)]
        + [jax.ShapeDtypeStruct((N_DEV,) + a.shape, a.dtype) for a in late],
        compiler_params=_params(("arbitrary", "arbitrary")),
    )(i_tab, j_tab, qb, kb, vb.T, *late)
    return outs[0], outs[1], list(outs[2:])


def _mla_bwd(qb, kb, vb, d_ob, lse, delta, grad_slices):
    t = qb.shape[0]
    tq = _attn_tile(t)
    nt = t // tq
    hps = MLA_HEADS_PER_STEP
    w = hps * SLAB
    pairs = [(j, i) for j in range(nt) for i in range(j, nt)]
    j_tab = jnp.asarray(np.array([p[0] for p in pairs], np.int32))
    i_tab = jnp.asarray(np.array([p[1] for p in pairs], np.int32))

    n_ex = len(grad_slices)

    def body(jt_ref, it_ref, q_ref, qt_ref, do_ref, dot_ref, l_ref, dl_ref, k_ref, kt_ref, v_ref, *rest):
        slice_refs, (dqt_ref, dkt_ref, dvt_ref) = rest[:n_ex], rest[n_ex:n_ex + 3]
        part_refs = rest[n_ex + 3:2 * n_ex + 3]
        dk_s, dv_s, send_sems, recv_sems, local_sems = rest[2 * n_ex + 3:]
        n = pl.program_id(1)
        j, i = jt_ref[n], it_ref[n]
        first_step = jnp.logical_and(pl.program_id(0) == 0, n == 0)
        last_step = jnp.logical_and(pl.program_id(0) == N_HEADS // hps - 1, n == len(pairs) - 1)

        @pl.when(first_step)
        def _():
            _start_copies(*_direct_copies(slice_refs, part_refs, send_sems, recv_sems, local_sems, False))

        @pl.when(n == 0)
        def _():
            dqt_ref[...] = jnp.zeros_like(dqt_ref)

        def update(diagonal):
            cols = pl.ds(pl.multiple_of(i * tq, tq), tq)

            def products(hh):
                sl = slice(hh * SLAB, (hh + 1) * SLAB)
                return _dot_nt(k_ref[:, sl], q_ref[:, sl]), _dot_nt(v_ref[:, sl], do_ref[:, sl])

            def softmax_bwd(hh, s, dp):
                if diagonal:
                    s = jnp.where(_upper_triangle(tq), s, NEG)
                p = jnp.exp2(s - l_ref[hh])
                return p.astype(BF16), (p * (dp - dl_ref[hh])).astype(BF16)

            def gradients(hh, p, ds):
                base = hh * SLAB
                vrows = slice(base, base + V_DIM_B)
                qrows = slice(base, base + QK_NOPE + QK_ROPE)
                dv = _dot_nt(dot_ref[vrows, :], p)
                dk = _dot_nt(qt_ref[qrows, :], ds)
                if diagonal:
                    dv_s[base:base + SLAB, :] = jnp.concatenate([dv, jnp.zeros((SLAB - V_DIM_B, tq), F32)], axis=0)
                    dk_s[base:base + SLAB, :] = jnp.concatenate(
                        [dk, jnp.zeros((SLAB - QK_NOPE - QK_ROPE, tq), F32)], axis=0)
                else:
                    dv_s[vrows, :] += dv
                    dk_s[qrows, :] += dk
                dqt_ref[qrows, cols] += _dot(kt_ref[qrows, :], ds)

            for hh in range(hps):
                gradients(hh, *softmax_bwd(hh, *products(hh)))

        @pl.when(i == j)
        def _():
            update(True)

        @pl.when(i > j)
        def _():
            update(False)

        @pl.when(i == nt - 1)
        def _():
            dkt_ref[...] = (dk_s[...] * (1.0 / LOG2E)).astype(BF16)
            dvt_ref[...] = dv_s[...].astype(BF16)

        @pl.when(last_step)
        def _():
            _wait_copies(*_direct_copies(slice_refs, part_refs, send_sems, recv_sems, local_sems, False))

    grid_spec = pltpu.PrefetchScalarGridSpec(
        num_scalar_prefetch=2, grid=(N_HEADS // hps, len(pairs)),
        in_specs=[pl.BlockSpec((tq, w), lambda h, n, jt, it: (it[n], h)),
                  pl.BlockSpec((w, tq), lambda h, n, jt, it: (h, it[n])),
                  pl.BlockSpec((tq, w), lambda h, n, jt, it: (it[n], h)),
                  pl.BlockSpec((w, tq), lambda h, n, jt, it: (h, it[n])),
                  pl.BlockSpec((hps, 1, tq), lambda h, n, jt, it: (h, 0, it[n])),
                  pl.BlockSpec((hps, 1, tq), lambda h, n, jt, it: (h, 0, it[n])),
                  pl.BlockSpec((tq, w), lambda h, n, jt, it: (jt[n], h)),
                  pl.BlockSpec((w, tq), lambda h, n, jt, it: (h, jt[n])),
                  pl.BlockSpec((tq, w), lambda h, n, jt, it: (jt[n], h))] + [ANY_SPEC] * n_ex,
        out_specs=[pl.BlockSpec((w, t), lambda h, n, jt, it: (h, 0)),
                   pl.BlockSpec((w, tq), lambda h, n, jt, it: (h, jt[n])),
                   pl.BlockSpec((w, tq), lambda h, n, jt, it: (h, jt[n]))] + [ANY_SPEC] * n_ex,
        scratch_shapes=[pltpu.VMEM((w, tq), F32), pltpu.VMEM((w, tq), F32)] + _exchange_scratch(n_ex))
    outs = pl.pallas_call(
        body, name="mla_bwd", grid_spec=grid_spec,
        out_shape=[jax.ShapeDtypeStruct((HM, t), F32), jax.ShapeDtypeStruct((HM, t), BF16),
                   jax.ShapeDtypeStruct((HM, t), BF16)]
        + [jax.ShapeDtypeStruct(a.shape, a.dtype) for a in grad_slices],
        compiler_params=_params(("arbitrary", "arbitrary")),
    )(j_tab, i_tab, qb, qb.T, d_ob, d_ob.T, lse, delta, kb, kb.T, vb, *grad_slices)
    return outs[0].T, outs[1].T, outs[2].T, list(outs[3:])


def _merge_fwd(out_a, out_b, gates, x, w_oa, w_ob, w_out, g2, g3):
    t = x.shape[0]
    tm = _token_tile(t)

    def body(oa_ref, ob_ref, gates_ref, x_ref, woa_ref, wob_ref, wout_ref, g2_ref, g3_ref,
             oap_ref, obp_ref, merged_ref, y_ref, x1_ref, h2_ref):
        oa_p = _dot(oa_ref[...], woa_ref[...])
        ob_p = _dot(ob_ref[...], wob_ref[...])
        oap_ref[...] = oa_p.astype(BF16)
        obp_ref[...] = ob_p.astype(BF16)
        sa = _sigmoid(gates_ref[:, 0:D_MODEL])
        sb = _sigmoid(gates_ref[:, D_MODEL:2 * D_MODEL])
        merged = (sa * oa_p + sb * ob_p).astype(BF16)
        merged_ref[...] = merged
        y = _dot(merged, wout_ref[...])
        y_ref[...] = y
        x1 = x_ref[...] + y * _rms_r(y) * g2_ref[...]
        x1_ref[...] = x1
        h2_ref[...] = (x1 * _rms_r(x1) * g3_ref[...]).astype(BF16)

    def sds(dt):
        return jax.ShapeDtypeStruct((t, D_MODEL), dt)

    row = _row_spec(tm, D_MODEL)
    return pl.pallas_call(
        body, name="merge_fwd", grid=(t // tm,),
        in_specs=[_row_spec(tm, HM), _row_spec(tm, HM), _row_spec(tm, 2 * D_MODEL), row,
                  _full_spec((HM, D_MODEL)), _full_spec((HM, D_MODEL)), _full_spec((D_MODEL, D_MODEL)),
                  _full_spec((1, D_MODEL)), _full_spec((1, D_MODEL))],
        out_specs=[row] * 6,
        out_shape=[sds(BF16), sds(BF16), sds(BF16), sds(F32), sds(F32), sds(BF16)],
        compiler_params=_params(("parallel",)),
    )(out_a, out_b, gates, x, w_oa, w_ob, w_out, g2, g3)


def _row_sums(x):
    hi = x.astype(BF16)
    lo = (x - hi.astype(F32)).astype(BF16)
    ones = jnp.ones((8, SLAB), BF16)
    return (_dot_nt(ones, hi) + _dot_nt(ones, lo))[0:1]


def _merge_bwd(dx1, y, gates, oa_p, ob_p, out_a, out_b, w_oa, w_ob, w_out, g2):
    t = dx1.shape[0]
    tm = _token_tile(t)

    def body(dx1_ref, y_ref, gates_ref, oap_ref, obp_ref, oa_ref, ob_ref, woa_ref, wob_ref, wout_ref, g2_ref,
             dy_ref, doap_ref, dobp_ref, dgates_ref, doa_ref, dob_ref, dla_ref, dlb_ref, dg2_ref):
        dx1v = dx1_ref[...]
        yv = y_ref[...]
        r2 = _rms_r(yv)
        _acc_rows(dg2_ref, dx1v * yv * r2)
        dy = _rms_bwd(yv, r2, g2_ref[...], dx1v).astype(BF16)
        dy_ref[...] = dy
        dm = _dot_nt(dy, wout_ref[...])
        sa = _sigmoid(gates_ref[:, 0:D_MODEL])
        sb = _sigmoid(gates_ref[:, D_MODEL:2 * D_MODEL])
        d_oap = (dm * sa).astype(BF16)
        d_obp = (dm * sb).astype(BF16)
        doap_ref[...] = d_oap
        dobp_ref[...] = d_obp
        dgates_ref[:, 0:D_MODEL] = (dm * oap_ref[...].astype(F32) * sa * (1.0 - sa)).astype(BF16)
        dgates_ref[:, D_MODEL:2 * D_MODEL] = (dm * obp_ref[...].astype(F32) * sb * (1.0 - sb)).astype(BF16)
        d_oa = _dot_nt(d_oap, woa_ref[...])
        d_ob = _dot_nt(d_obp, wob_ref[...])
        doa_ref[...] = d_oa.astype(BF16)
        dob_ref[...] = d_ob.astype(BF16)
        for hd in range(N_HEADS):
            sl = slice(hd * SLAB, (hd + 1) * SLAB)
            dla_ref[hd] = _row_sums(d_oa[:, sl] * oa_ref[:, sl].astype(F32))
            dlb_ref[hd] = _row_sums(d_ob[:, sl] * ob_ref[:, sl].astype(F32))

    def sds(n, dt):
        return jax.ShapeDtypeStruct((t, n), dt)

    row = _row_spec(tm, D_MODEL)
    head3 = pl.BlockSpec((N_HEADS, 1, tm), lambda i: (0, 0, i))
    return pl.pallas_call(
        body, name="merge_bwd", grid=(t // tm,),
        in_specs=[row, row, _row_spec(tm, 2 * D_MODEL), row, row, _row_spec(tm, HM), _row_spec(tm, HM),
                  _full_spec((HM, D_MODEL)), _full_spec((HM, D_MODEL)), _full_spec((D_MODEL, D_MODEL)),
                  _full_spec((1, D_MODEL))],
        out_specs=[row, row, row, _row_spec(tm, 2 * D_MODEL), _row_spec(tm, HM), _row_spec(tm, HM),
                   head3, head3, _full_spec((1, D_MODEL))],
        out_shape=[sds(D_MODEL, BF16), sds(D_MODEL, BF16), sds(D_MODEL, BF16), sds(2 * D_MODEL, BF16),
                   sds(HM, BF16), sds(HM, BF16),
                   jax.ShapeDtypeStruct((N_HEADS, 1, t), F32), jax.ShapeDtypeStruct((N_HEADS, 1, t), F32),
                   jax.ShapeDtypeStruct((1, D_MODEL), F32)],
        compiler_params=_params(("arbitrary",)),
    )(dx1, y, gates, oa_p, ob_p, out_a, out_b, w_oa, w_ob, w_out, g2)


def _mlp_fwd_bwd(x1, h2, target, w_up, w_down, g3, g4):
    t = x1.shape[0]
    tm = _token_tile(t)
    fs = D_FF // N_DEV

    def body(x1_ref, h2_ref, tgt_ref, wup_ref, wdown_ref, g3_ref, g4_ref,
             a_ref, du_ref, dy2_ref, dx1_ref, loss_ref, dg3_ref, dg4_ref):
        x1v = x1_ref[...]
        h2v = h2_ref[...]
        u = jnp.concatenate([_dot(h2v, wup_ref[s]) for s in range(N_DEV)], axis=1)
        ru = jnp.maximum(u, 0.0)
        a = (ru * ru).astype(BF16)
        a_ref[...] = a
        y2 = _dot(a, wdown_ref[...])
        r4 = _rms_r(y2)
        diff = x1v + y2 * r4 * g4_ref[...] - tgt_ref[...]
        _acc_rows(loss_ref, jnp.sum(diff * diff, axis=-1, keepdims=True) * (0.5 / D_MODEL)
                  * jnp.ones((1, SLAB), F32))
        dx2 = diff * (1.0 / D_MODEL)
        _acc_rows(dg4_ref, dx2 * y2 * r4)
        dy2 = _rms_bwd(y2, r4, g4_ref[...], dx2).astype(BF16)
        dy2_ref[...] = dy2
        du = (_dot_nt(dy2, wdown_ref[...]) * (2.0 * ru)).astype(BF16)
        du_ref[...] = du
        dh2 = _dot_nt(du[:, 0:fs], wup_ref[0])
        for s in range(1, N_DEV):
            dh2 += _dot_nt(du[:, s * fs:(s + 1) * fs], wup_ref[s])
        r3 = _rms_r(x1v)
        _acc_rows(dg3_ref, dh2 * x1v * r3)
        dx1_ref[...] = dx2 + _rms_bwd(x1v, r3, g3_ref[...], dh2)

    row = _row_spec(tm, D_MODEL)
    frow = _row_spec(tm, D_FF)
    vec = _full_spec((1, D_MODEL))
    return pl.pallas_call(
        body, name="mlp_fwd_bwd", grid=(t // tm,),
        in_specs=[row, row, row, _full_spec((N_DEV, D_MODEL, fs)), _full_spec((D_FF, D_MODEL)), vec, vec],
        out_specs=[frow, frow, row, row, _full_spec((1, SLAB)), vec, vec],
        out_shape=[jax.ShapeDtypeStruct((t, D_FF), BF16), jax.ShapeDtypeStruct((t, D_FF), BF16),
                   jax.ShapeDtypeStruct((t, D_MODEL), BF16), jax.ShapeDtypeStruct((t, D_MODEL), F32),
                   jax.ShapeDtypeStruct((1, SLAB), F32), jax.ShapeDtypeStruct((1, D_MODEL), F32),
                   jax.ShapeDtypeStruct((1, D_MODEL), F32)],
        compiler_params=_params(("arbitrary",)),
    )(x1, h2, target, w_up, w_down, g3, g4)


def _inproj_bwd(dgates, dqa, dka, dva, dqb, dkb, dvb, cq, ckv, x, dx1, rope_c, rope_s1, rope_s2,
                g1, g_q, g_kv, w_in, w_qb, w_kvb):
    t = x.shape[0]
    tm = _token_tile(t)

    def body(dgates_ref, dqa_ref, dka_ref, dva_ref, dqb_ref, dkb_ref, dvb_ref, cq_ref, ckv_ref, x_ref, dx1_ref,
             c_ref, s1_ref, s2_ref, g1_ref, gq_ref, gkv_ref, win_ref, wqb_ref, wkvb_ref,
             dproj_ref, dqbr_ref, dkvb_ref, dx_ref, dg1_ref, dgq_ref, dgkv_ref):
        c, s1, s2 = c_ref[...], s1_ref[...], s2_ref[...]
        dk_sum = jnp.zeros((tm, SLAB), F32)
        for hd in range(N_HEADS):
            sl = slice(hd * SLAB, (hd + 1) * SLAB)
            dqbr_ref[:, sl] = _rope_bwd(dqb_ref[:, sl] * SCALE_B, c, s1, s2).astype(BF16)
            dk_sum += dkb_ref[:, sl].astype(F32)
        dkvb_ref[:, 0:HM] = dkb_ref[...]
        dkvb_ref[:, HM:2 * HM] = dvb_ref[...]
        dkr = _rope_bwd(dk_sum, c, s1, s2)
        dcqn = _dot_nt(dqbr_ref[...], wqb_ref[...])
        cq = cq_ref[...]
        rq = _rms_r(cq)
        _acc_rows(dgq_ref, dcqn * cq * rq)
        dcq = _rms_bwd(cq, rq, gq_ref[...], dcqn)
        dckvn = _dot_nt(dkvb_ref[...], wkvb_ref[...])
        ckv = ckv_ref[...]
        rkv = _rms_r(ckv)
        _acc_rows(dgkv_ref, dckvn * ckv * rkv)
        dckv = _rms_bwd(ckv, rkv, gkv_ref[...], dckvn)
        dproj_ref[:, C_GATES:C_QA] = dgates_ref[...]
        dproj_ref[:, C_QA:C_KA] = dqa_ref[...]
        dproj_ref[:, C_KA:C_VA] = dka_ref[...]
        dproj_ref[:, C_VA:C_CQ] = dva_ref[...]
        dproj_ref[:, C_CQ:C_CKV] = dcq.astype(BF16)
        dproj_ref[:, C_CKV:C_KR] = dckv.astype(BF16)
        dproj_ref[:, C_KR:D_IN_PAD] = dkr.astype(BF16)
        dh = _dot_nt(dproj_ref[...], win_ref[...])
        xv = x_ref[...]
        r1 = _rms_r(xv)
        _acc_rows(dg1_ref, dh * xv * r1)
        dx_ref[...] = dx1_ref[...] + _rms_bwd(xv, r1, g1_ref[...], dh)

    kvw = N_KV_A * SLAB
    row = _row_spec(tm, D_MODEL)
    hm = _row_spec(tm, HM)
    tab = _row_spec(tm, SLAB)
    return pl.pallas_call(
        body, name="inproj_bwd", grid=(t // tm,),
        in_specs=[_row_spec(tm, 2 * D_MODEL), hm, _row_spec(tm, kvw), _row_spec(tm, kvw), hm, hm, hm,
                  _row_spec(tm, Q_LORA), _row_spec(tm, KV_LORA), row, row, tab, tab, tab,
                  _full_spec((1, D_MODEL)), _full_spec((1, Q_LORA)), _full_spec((1, KV_LORA)),
                  _full_spec((D_MODEL, D_IN_PAD)), _full_spec((Q_LORA, HM)), _full_spec((KV_LORA, 2 * HM))],
        out_specs=[_row_spec(tm, D_IN_PAD), hm, _row_spec(tm, 2 * HM), row,
                   _full_spec((1, D_MODEL)), _full_spec((1, Q_LORA)), _full_spec((1, KV_LORA))],
        out_shape=[jax.ShapeDtypeStruct((t, D_IN_PAD), BF16), jax.ShapeDtypeStruct((t, HM), BF16),
                   jax.ShapeDtypeStruct((t, 2 * HM), BF16), jax.ShapeDtypeStruct((t, D_MODEL), F32),
                   jax.ShapeDtypeStruct((1, D_MODEL), F32), jax.ShapeDtypeStruct((1, Q_LORA), F32),
                   jax.ShapeDtypeStruct((1, KV_LORA), F32)],
        compiler_params=_params(("arbitrary",)),
    )(dgates, dqa, dka, dva, dqb, dkb, dvb, cq, ckv, x, dx1, rope_c, rope_s1, rope_s2,
      g1, g_q, g_kv, w_in, w_qb, w_kvb)


def _matmul_tn(a, b, name, out_dtype=F32, n_shards=1):
    t, k = a.shape
    n = b.shape[1]
    bt = min(t, 512)
    bk = min(k, 1024)
    bn = n // n_shards if n_shards > 1 else min(n, 1024)
    steps = t // bt

    def body(a_ref, b_ref, o_ref, acc):
        s = pl.program_id(2)

        @pl.when(s == 0)
        def _():
            acc[...] = jnp.zeros_like(acc)

        acc[...] += _dot_tn(a_ref[...], b_ref[...])

        @pl.when(s == steps - 1)
        def _():
            if n_shards > 1:
                o_ref[0] = acc[...].astype(out_dtype)
            else:
                o_ref[...] = acc[...].astype(out_dtype)

    if n_shards > 1:
        out_spec = pl.BlockSpec((1, bk, bn), lambda i, j, s: (j, i, 0))
        out_shape = jax.ShapeDtypeStruct((n_shards, k, bn), out_dtype)
    else:
        out_spec = pl.BlockSpec((bk, bn), lambda i, j, s: (i, j))
        out_shape = jax.ShapeDtypeStruct((k, n), out_dtype)
    return pl.pallas_call(
        body, name=name, grid=(k // bk, n // bn, steps),
        in_specs=[pl.BlockSpec((bt, bk), lambda i, j, s: (s, i)), pl.BlockSpec((bt, bn), lambda i, j, s: (s, j))],
        out_specs=out_spec, out_shape=out_shape, scratch_shapes=[pltpu.VMEM((bk, bn), F32)],
        compiler_params=_params(("parallel", "parallel", "arbitrary")),
    )(a, b)


def _all_gather(shards):
    n = len(shards)

    def body(*refs):
        srcs, dsts = refs[:n], refs[n:2 * n]
        copies = _direct_copies(srcs, dsts, *refs[2 * n:], True)
        _start_copies(*copies)
        _wait_copies(*copies)

    return pl.pallas_call(
        body, name="all_gather_early",
        out_shape=[jax.ShapeDtypeStruct((N_DEV,) + a.shape, a.dtype) for a in shards],
        in_specs=[ANY_SPEC] * n, out_specs=[ANY_SPEC] * n, scratch_shapes=_exchange_scratch(n),
    )(*shards)


def _exchange_grads(slices, small):
    n = len(slices)

    def body(*refs):
        srcs, s_ref = refs[:n], refs[n]
        dsts, s_dst = refs[n + 1:2 * n + 1], refs[2 * n + 1]
        sems = refs[2 * n + 2:]
        parts = _direct_copies(srcs, dsts, *sems, False)
        smalls = _direct_copies([s_ref], [s_dst], *sems, True, sem_base=n)
        _start_copies(*parts)
        _start_copies(*smalls)
        _wait_copies(*parts)
        _wait_copies(*smalls)

    outs = pl.pallas_call(
        body, name="exchange_grads",
        out_shape=[jax.ShapeDtypeStruct(a.shape, a.dtype) for a in slices]
        + [jax.ShapeDtypeStruct((N_DEV,) + small.shape, small.dtype)],
        in_specs=[ANY_SPEC] * (n + 1), out_specs=[ANY_SPEC] * (n + 1), scratch_shapes=_exchange_scratch(n + 1),
    )(*slices, small)
    return list(outs[:n]), outs[n]


def _adamw(parts, w, m, v, name):
    _, k, n = parts.shape
    bk = min(k, ADAM_ROWS)
    c1 = 1.0 - ADAM_B1 ** ADAM_STEP
    c2 = 1.0 - ADAM_B2 ** ADAM_STEP

    def body(p_ref, w_ref, m_ref, v_ref, g_ref, d_ref, mo_ref, vo_ref):
        g = p_ref[0].astype(F32)
        for s in range(1, N_DEV):
            g = g + p_ref[s].astype(F32)
        g_ref[0] = g
        m_new = ADAM_B1 * m_ref[0] + (1.0 - ADAM_B1) * g
        v_new = ADAM_B2 * v_ref[0] + (1.0 - ADAM_B2) * (g * g)
        mo_ref[0] = m_new
        vo_ref[0] = v_new
        m_hat = m_new / c1
        v_hat = v_new / c2
        d_ref[0] = -ADAM_LR * (m_hat / (jnp.sqrt(v_hat) + ADAM_EPS) + ADAM_WD * w_ref[0])

    blk = pl.BlockSpec((1, bk, n), lambda i: (0, i, 0))
    out = jax.ShapeDtypeStruct((1, k, n), F32)
    return pl.pallas_call(
        body, name=name, grid=(k // bk,),
        in_specs=[pl.BlockSpec((N_DEV, bk, n), lambda i: (0, i, 0)), blk, blk, blk],
        out_specs=[blk] * 4, out_shape=[out] * 4,
        compiler_params=_params(("parallel",)),
    )(parts, w, m, v)


def _pad_heads_cols(w, heads, width):
    k = w.shape[0]
    w = w.reshape(k, heads, width)
    return jnp.pad(w, ((0, 0), (0, 0), (0, SLAB - width))).reshape(k, heads * SLAB)


def _unpad_heads_cols(w, heads, width):
    k = w.shape[0]
    return w.reshape(k, heads, SLAB)[:, :, :width].reshape(k, heads * width)


def _pad_heads_rows(w, heads, width):
    n = w.shape[1]
    w = w.reshape(heads, width, n)
    return jnp.pad(w, ((0, 0), (0, SLAB - width), (0, 0))).reshape(heads * SLAB, n)


def _unpad_heads_rows(w, heads, width):
    n = w.shape[1]
    return w.reshape(heads, SLAB, n)[:, :width, :].reshape(heads * width, n)


def _pad_w_in(w_in):
    o = 2 * D_MODEL
    qa = _pad_heads_cols(w_in[:, o:o + 512], N_HEADS, HEAD_A)
    ka = _pad_heads_cols(w_in[:, o + 512:o + 640], N_KV_A, HEAD_A)
    va = _pad_heads_cols(w_in[:, o + 640:o + 768], N_KV_A, HEAD_A)
    kr = jnp.pad(w_in[:, o + 1152:o + 1184], ((0, 0), (QK_NOPE, SLAB - QK_NOPE - QK_ROPE)))
    return jnp.concatenate([w_in[:, :o], qa, ka, va, w_in[:, o + 768:o + 1152], kr], axis=1)


def _unpad_w_in(w):
    qa = _unpad_heads_cols(w[:, C_QA:C_KA], N_HEADS, HEAD_A)
    ka = _unpad_heads_cols(w[:, C_KA:C_VA], N_KV_A, HEAD_A)
    va = _unpad_heads_cols(w[:, C_VA:C_CQ], N_KV_A, HEAD_A)
    kr = w[:, C_KR + QK_NOPE:C_KR + QK_NOPE + QK_ROPE]
    return jnp.concatenate([w[:, :C_QA], qa, ka, va, w[:, C_CQ:C_KR], kr], axis=1)


def _pad_w_kvb(w_kvb):
    w = w_kvb.reshape(KV_LORA, N_HEADS, QK_NOPE + V_DIM_B)
    k = jnp.pad(w[:, :, :QK_NOPE], ((0, 0), (0, 0), (0, SLAB - QK_NOPE))).reshape(KV_LORA, HM)
    v = jnp.pad(w[:, :, QK_NOPE:], ((0, 0), (0, 0), (0, SLAB - V_DIM_B))).reshape(KV_LORA, HM)
    return jnp.concatenate([k, v], axis=1)


def _unpad_w_kvb(w):
    k = w[:, :HM].reshape(KV_LORA, N_HEADS, SLAB)[:, :, :QK_NOPE]
    v = w[:, HM:].reshape(KV_LORA, N_HEADS, SLAB)[:, :, :V_DIM_B]
    return jnp.concatenate([k, v], axis=2).reshape(KV_LORA, N_HEADS * (QK_NOPE + V_DIM_B))


def _col_shards(w):
    k, n = w.shape
    return w.reshape(k, N_DEV, n // N_DEV).transpose(1, 0, 2)


def _from_col_shards(s):
    _, k, ns = s.shape
    return s.transpose(1, 0, 2).reshape(k, N_DEV * ns)


def _freq_row():
    freqs = ROPE_THETA ** (-jnp.arange(0, QK_ROPE, 2, dtype=F32) / QK_ROPE)
    return jnp.concatenate([jnp.zeros((QK_NOPE,), F32), freqs, freqs,
                            jnp.zeros((SLAB - QK_NOPE - QK_ROPE,), F32)]).reshape(1, SLAB)


SMALL_D_ROWS = ("pre_norm_mix", "post_norm_mix", "pre_norm_mlp", "post_norm_mlp")
SMALL_Q_OFF, SMALL_KV_OFF, SMALL_SINK_OFF, SMALL_LOSS_OFF = 0, 256, 384, 392


def _pack_small(vals):
    row4 = jnp.concatenate([vals["q_a_norm"].reshape(-1), vals["kv_a_norm"].reshape(-1), vals["sinks"].reshape(-1),
                            vals["loss"].reshape(-1), jnp.zeros((1024 - 393,), F32)])
    rows = [vals[n].reshape(1024) for n in SMALL_D_ROWS] + [row4]
    return jnp.concatenate([jnp.stack(rows), jnp.zeros((SMALL_ROWS - 5, 1024), F32)], axis=0)


def _unpack_small(blk):
    out = {n: blk[i].reshape(1, 1024) for i, n in enumerate(SMALL_D_ROWS)}
    out["q_a_norm"] = blk[4, SMALL_Q_OFF:SMALL_Q_OFF + 256].reshape(1, 256)
    out["kv_a_norm"] = blk[4, SMALL_KV_OFF:SMALL_KV_OFF + 128].reshape(1, 128)
    out["sinks"] = blk[4, SMALL_SINK_OFF:SMALL_SINK_OFF + 8].reshape(1, 8)
    out["loss"] = blk[4, SMALL_LOSS_OFF]
    return out


WEIGHT_ORDER = ("pre_norm_mix", "w_in", "q_a_norm", "w_q_b", "kv_a_norm", "w_kv_b", "sinks", "w_o_a", "w_o_b",
                "w_out", "post_norm_mix", "pre_norm_mlp", "w_up", "w_down", "post_norm_mlp")
SMALL_NAMES = ("pre_norm_mix", "q_a_norm", "kv_a_norm", "sinks", "post_norm_mix", "pre_norm_mlp", "post_norm_mlp")


def kernel(x, positions, pre_norm_mix, w_in, q_a_norm, w_q_b, kv_a_norm, w_kv_b, sinks, w_o_a, w_o_b, w_out, post_norm_mix, pre_norm_mlp, w_up, w_down, post_norm_mlp, loss_target, m_pre_norm_mix, m_w_in, m_q_a_norm, m_w_q_b, m_kv_a_norm, m_w_kv_b, m_sinks, m_w_o_a, m_w_o_b, m_w_out, m_post_norm_mix, m_pre_norm_mlp, m_w_up, m_w_down, m_post_norm_mlp, v_pre_norm_mix, v_w_in, v_q_a_norm, v_w_q_b, v_kv_a_norm, v_w_kv_b, v_sinks, v_w_o_a, v_w_o_b, v_w_out, v_post_norm_mix, v_pre_norm_mlp, v_w_up, v_w_down, v_post_norm_mlp):
    weights = dict(pre_norm_mix=pre_norm_mix, w_in=w_in, q_a_norm=q_a_norm, w_q_b=w_q_b, kv_a_norm=kv_a_norm,
                   w_kv_b=w_kv_b, sinks=sinks, w_o_a=w_o_a, w_o_b=w_o_b, w_out=w_out, post_norm_mix=post_norm_mix,
                   pre_norm_mlp=pre_norm_mlp, w_up=w_up, w_down=w_down, post_norm_mlp=post_norm_mlp)
    m_in = dict(pre_norm_mix=m_pre_norm_mix, w_in=m_w_in, q_a_norm=m_q_a_norm, w_q_b=m_w_q_b, kv_a_norm=m_kv_a_norm,
                w_kv_b=m_w_kv_b, sinks=m_sinks, w_o_a=m_w_o_a, w_o_b=m_w_o_b, w_out=m_w_out,
                post_norm_mix=m_post_norm_mix, pre_norm_mlp=m_pre_norm_mlp, w_up=m_w_up, w_down=m_w_down,
                post_norm_mlp=m_post_norm_mlp)
    v_in = dict(pre_norm_mix=v_pre_norm_mix, w_in=v_w_in, q_a_norm=v_q_a_norm, w_q_b=v_w_q_b, kv_a_norm=v_kv_a_norm,
                w_kv_b=v_w_kv_b, sinks=v_sinks, w_o_a=v_w_o_a, w_o_b=v_w_o_b, w_out=v_w_out,
                post_norm_mix=v_post_norm_mix, pre_norm_mlp=v_pre_norm_mlp, w_up=v_w_up, w_down=v_w_down,
                post_norm_mlp=v_post_norm_mlp)

    xs, pos, target = x[0], positions[0], loss_target[0]
    t = xs.shape[0]
    pos_col = pos.reshape(t, 1)
    pos_row = pos.reshape(1, t)
    g1, g2, g3, g4 = (weights[n] for n in SMALL_D_ROWS)
    g_q, g_kv = q_a_norm, kv_a_norm
    sink_vec = sinks.reshape(N_HEADS)
    shard = {n: weights[n][0].astype(BF16) for n in EARLY + LATE}

    e_in, e_qb, e_kvb = _all_gather([shard[n] for n in EARLY])
    w_in_p = _pad_w_in(_from_col_shards(e_in))
    w_qb = _pad_heads_cols(_from_col_shards(e_qb), N_HEADS, QK_NOPE + QK_ROPE)
    w_kvb = _pad_w_kvb(_from_col_shards(e_kvb))

    rc, rs1, rs2 = _rope_tables(pos_col, _freq_row())
    (h, gates, qa, ka, va, cq, ckv, cqn, ckvn, qb, kb, vb) = _inproj_fwd(
        xs, g1, w_in_p, g_q, w_qb, g_kv, w_kvb, rc, rs1, rs2)
    out_a, lse_a = _swa_fwd(qa, ka, va, pos_col, pos_row, sink_vec)
    out_b, lse_b, (l_oa, l_ob, l_out, w_up_s, l_down) = _mla_fwd(qb, kb, vb, [shard[n] for n in LATE])
    w_oa = _pad_heads_rows(_from_col_shards(l_oa), N_HEADS, HEAD_A)
    w_ob = _pad_heads_rows(_from_col_shards(l_ob), N_HEADS, V_DIM_B)
    w_out_f = l_out.reshape(D_MODEL, D_MODEL)
    w_down_f = l_down.reshape(D_FF, D_MODEL)

    oa_p, ob_p, merged, y, x1, h2 = _merge_fwd(out_a, out_b, gates, xs, w_oa, w_ob, w_out_f, g2, g3)
    a, du, dy2, dx1, loss, dg3, dg4 = _mlp_fwd_bwd(x1, h2, target, w_up_s, w_down_f, g3, g4)
    (dy, d_oap, d_obp, dgates, d_oa, d_ob, delta_a, delta_b, dg2) = _merge_bwd(
        dx1, y, gates, oa_p, ob_p, out_a, out_b, w_oa, w_ob, w_out_f, g2)
    late_slices = [
        _col_shards(_unpad_heads_rows(_matmul_tn(out_a, d_oap, "dw_o_a"), N_HEADS, HEAD_A)).astype(BF16),
        _col_shards(_unpad_heads_rows(_matmul_tn(out_b, d_obp, "dw_o_b"), N_HEADS, V_DIM_B)).astype(BF16),
        _matmul_tn(merged, dy, "dw_out", BF16).reshape(N_DEV, D_MODEL // N_DEV, D_MODEL),
        _matmul_tn(h2, du, "dw_up", BF16, N_DEV),
        _matmul_tn(a, dy2, "dw_down", BF16).reshape(N_DEV, D_FF // N_DEV, D_MODEL),
    ]
    dqa, dka, dva, dsink = _swa_bwd(qa, ka, va, d_oa, lse_a, delta_a, pos_col, pos_row, sink_vec)
    dqb, dkb, dvb, late_parts = _mla_bwd(qb, kb, vb, d_ob, lse_b, delta_b, late_slices)
    dproj, dqbr, dkvb, dx, dg1, dgq, dgkv = _inproj_bwd(
        dgates, dqa, dka, dva, dqb, dkb, dvb, cq, ckv, xs, dx1, rc, rs1, rs2, g1, g_q, g_kv, w_in_p, w_qb, w_kvb)
    early_slices = [
        _col_shards(_unpad_w_in(_matmul_tn(h, dproj, "dw_in"))).astype(BF16),
        _col_shards(_unpad_heads_cols(_matmul_tn(cqn, dqbr, "dw_q_b"), N_HEADS, QK_NOPE + QK_ROPE)).astype(BF16),
        _col_shards(_unpad_w_kvb(_matmul_tn(ckvn, dkvb, "dw_kv_b"))).astype(BF16),
    ]
    small_grads = {"pre_norm_mix": dg1, "post_norm_mix": dg2, "pre_norm_mlp": dg3, "post_norm_mlp": dg4,
                   "q_a_norm": dgq, "kv_a_norm": dgkv, "sinks": dsink.reshape(N_HEADS, BLOCK).sum(axis=1),
                   "loss": loss[0, 0:1]}
    early_parts, s_parts = _exchange_grads(early_slices, _pack_small(small_grads))

    updates = {}
    for name, parts in zip(EARLY + LATE, early_parts + late_parts):
        outs = _adamw(parts, weights[name], m_in[name], v_in[name], "adamw_" + name)
        for kind, arr in zip(("g", "d", "m", "v"), outs):
            updates[kind, name] = arr
    zero = jnp.zeros((), F32)
    pack = lambda src: _pack_small({**{n: src[n] for n in SMALL_NAMES}, "loss": zero})[None]
    smalls = _adamw(s_parts, pack(weights), pack(m_in), pack(v_in), "adamw_small")
    for kind, blk in zip(("g", "d", "m", "v"), smalls):
        for wname, piece in _unpack_small(blk[0]).items():
            updates[kind, wname] = piece
    results = [updates[kind, name] for kind in ("g", "d", "m", "v") for name in WEIGHT_ORDER]
    return (updates["g", "loss"], dx[None], *results)
```

```python
import functools

import numpy as np
import jax
import jax.numpy as jnp
from jax import lax
from jax.experimental import pallas as pl
from jax.experimental.pallas import tpu as pltpu

F32 = jnp.float32
BF16 = jnp.bfloat16

D_MODEL = 1024
D_FF = 4096
N_HEADS = 8
N_KV_A = 2
GROUP_A = N_HEADS // N_KV_A
HEAD_A = 64
QK_NOPE = 64
QK_ROPE = 32
V_DIM_B = 64
Q_LORA = 256
KV_LORA = 128
BLOCK = 128
SLAB = 128
ROPE_THETA = 10000.0
EPS = 1e-6
N_DEV = 8
NEG = -1e30

SCALE_A = HEAD_A ** -0.5
SCALE_B = (QK_NOPE + QK_ROPE) ** -0.5
LOG2E = 1.4426950408889634
SCORE_B = SCALE_B * LOG2E
MLA_HEADS_PER_STEP = 4
SLOPES_A = tuple(2.0 ** (-8.0 * (h + 1) / N_HEADS) for h in range(N_HEADS))

ADAM_LR = 0.001
ADAM_B1 = 0.9
ADAM_B2 = 0.999
ADAM_EPS = 1e-08
ADAM_WD = 0.01
ADAM_STEP = 10

HM = N_HEADS * SLAB
C_GATES = 0
C_QA = 2 * D_MODEL
C_KA = C_QA + HM
C_VA = C_KA + N_KV_A * SLAB
C_CQ = C_VA + N_KV_A * SLAB
C_CKV = C_CQ + Q_LORA
C_KR = C_CKV + KV_LORA
D_IN_PAD = C_KR + SLAB

VMEM_LIMIT = 56 * 1024 * 1024

EARLY = ("w_in", "w_q_b", "w_kv_b")
LATE = ("w_o_a", "w_o_b", "w_out", "w_up", "w_down")
ADAM_ROWS = 256
SMALL_ROWS = 8


def _token_tile(t):
    return min(256, t)


def _attn_tile(t):
    return 512 if t >= 2048 else 128


def _params(sem, vmem=VMEM_LIMIT):
    return pltpu.CompilerParams(dimension_semantics=sem, vmem_limit_bytes=vmem)


def _dot(a, b):
    return jnp.dot(a, b, preferred_element_type=F32)


def _dot_nt(a, b):
    return lax.dot_general(a, b, (((1,), (1,)), ((), ())), preferred_element_type=F32)


def _dot_tn(a, b):
    return lax.dot_general(a, b, (((0,), (0,)), ((), ())), preferred_element_type=F32)


def _rms_r(x):
    return lax.rsqrt(jnp.mean(x * x, axis=-1, keepdims=True) + EPS)


def _rms_bwd(x, r, g, dy):
    t = dy * g
    return r * t - x * (r * r * r) * jnp.mean(x * t, axis=-1, keepdims=True)


def _sigmoid(x):
    return 1.0 / (1.0 + jnp.exp(-x))


def _rope(x, c, s1, s2):
    return x * c + pltpu.roll(x, SLAB - 16, 1) * s1 + pltpu.roll(x, 16, 1) * s2


def _rope_bwd(d, c, s1, s2):
    return d * c + pltpu.roll(d * s1, 16, 1) + pltpu.roll(d * s2, SLAB - 16, 1)


def _row_spec(tm, n):
    return pl.BlockSpec((tm, n), lambda i: (i, 0))


def _full_spec(shape):
    nd = len(shape)
    return pl.BlockSpec(shape, lambda i: (0,) * nd, pipeline_mode=pl.Buffered(1))


def _acc_rows(ref, val):
    @pl.when(pl.program_id(0) == 0)
    def _():
        ref[...] = jnp.zeros_like(ref)
    ref[...] += jnp.sum(val, axis=0, keepdims=True)


def _rope_tables(pos_col, freq_row):
    t = pos_col.shape[0]
    tm = _token_tile(t)

    def body(pos_ref, f_ref, c_ref, s1_ref, s2_ref):
        ang = pos_ref[...].astype(F32) * f_ref[...]
        lane = lax.broadcasted_iota(jnp.int32, ang.shape, 1)
        s = jnp.sin(ang)
        c_ref[...] = jnp.cos(ang)
        s1_ref[...] = jnp.where((lane >= 64) & (lane < 80), -s, 0.0)
        s2_ref[...] = jnp.where((lane >= 80) & (lane < 96), s, 0.0)

    tab = jax.ShapeDtypeStruct((t, SLAB), F32)
    return pl.pallas_call(
        body, name="rope_tables", grid=(t // tm,),
        in_specs=[_row_spec(tm, 1), _full_spec((1, SLAB))],
        out_specs=[_row_spec(tm, SLAB)] * 3, out_shape=[tab] * 3,
        compiler_params=_params(("parallel",)),
    )(pos_col, freq_row)


def _inproj_fwd(x, g1, w_in, g_q, w_qb, g_kv, w_kvb, rope_c, rope_s1, rope_s2):
    t = x.shape[0]
    tm = _token_tile(t)

    def body(x_ref, g1_ref, win_ref, gq_ref, wqb_ref, gkv_ref, wkvb_ref, c_ref, s1_ref, s2_ref,
             h_ref, gates_ref, qa_ref, ka_ref, va_ref, cq_ref, ckv_ref, cqn_ref, ckvn_ref,
             qb_ref, kb_ref, vb_ref):
        xv = x_ref[...]
        h = (xv * _rms_r(xv) * g1_ref[...]).astype(BF16)
        h_ref[...] = h
        proj = _dot(h, win_ref[...])
        gates_ref[...] = proj[:, C_GATES:C_QA]
        qa_ref[...] = proj[:, C_QA:C_KA].astype(BF16)
        ka_ref[...] = proj[:, C_KA:C_VA].astype(BF16)
        va_ref[...] = proj[:, C_VA:C_CQ].astype(BF16)
        cq = proj[:, C_CQ:C_CKV]
        ckv = proj[:, C_CKV:C_KR]
        kr = proj[:, C_KR:D_IN_PAD]
        cq_ref[...] = cq
        ckv_ref[...] = ckv
        cqn = (cq * _rms_r(cq) * gq_ref[...]).astype(BF16)
        ckvn = (ckv * _rms_r(ckv) * gkv_ref[...]).astype(BF16)
        cqn_ref[...] = cqn
        ckvn_ref[...] = ckvn
        c, s1, s2 = c_ref[...], s1_ref[...], s2_ref[...]
        qb = _dot(cqn, wqb_ref[...])
        kvb = _dot(ckvn, wkvb_ref[...])
        kr_rot = _rope(kr, c, s1, s2)
        for hd in range(N_HEADS):
            sl = slice(hd * SLAB, (hd + 1) * SLAB)
            qb_ref[:, sl] = (_rope(qb[:, sl], c, s1, s2) * SCORE_B).astype(BF16)
            kb_ref[:, sl] = (kvb[:, sl] + kr_rot).astype(BF16)
        vb_ref[...] = kvb[:, HM:2 * HM].astype(BF16)

    def sds(n, dt):
        return jax.ShapeDtypeStruct((t, n), dt)

    outs = [(D_MODEL, BF16), (2 * D_MODEL, F32), (HM, BF16), (N_KV_A * SLAB, BF16), (N_KV_A * SLAB, BF16),
            (Q_LORA, F32), (KV_LORA, F32), (Q_LORA, BF16), (KV_LORA, BF16), (HM, BF16), (HM, BF16), (HM, BF16)]
    return pl.pallas_call(
        body, name="inproj_fwd", grid=(t // tm,),
        in_specs=[_row_spec(tm, D_MODEL), _full_spec((1, D_MODEL)), _full_spec((D_MODEL, D_IN_PAD)),
                  _full_spec((1, Q_LORA)), _full_spec((Q_LORA, HM)), _full_spec((1, KV_LORA)),
                  _full_spec((KV_LORA, 2 * HM)), _row_spec(tm, SLAB), _row_spec(tm, SLAB), _row_spec(tm, SLAB)],
        out_specs=[_row_spec(tm, n) for n, _ in outs],
        out_shape=[sds(n, dt) for n, dt in outs],
        compiler_params=_params(("parallel",)),
    )(x, g1, w_in, g_q, w_qb, g_kv, w_kvb, rope_c, rope_s1, rope_s2)


def _tile_group(a):
    return jnp.concatenate([a] * GROUP_A, axis=1)


def _swa_masks():
    row = lax.broadcasted_iota(jnp.int32, (BLOCK, GROUP_A * BLOCK), 0)
    col = lax.broadcasted_iota(jnp.int32, (BLOCK, GROUP_A * BLOCK), 1) & (BLOCK - 1)
    return row <= col, row > col


def _heads_beside(ref, g):
    return jnp.concatenate([ref[:, (g * GROUP_A + hh) * SLAB:(g * GROUP_A + hh + 1) * SLAB].T
                            for hh in range(GROUP_A)], axis=1)


def _rows_beside(ref, g):
    return jnp.concatenate([ref[g * GROUP_A + hh] for hh in range(GROUP_A)], axis=1)


def _swa_rows(sinks):
    slopes = jnp.repeat(jnp.asarray(SLOPES_A, F32).reshape(N_KV_A, GROUP_A, 1), BLOCK, axis=2)
    sink_rows = jnp.repeat(sinks.reshape(N_KV_A, GROUP_A, 1), BLOCK, axis=2)
    return slopes.reshape(N_KV_A, 1, GROUP_A * BLOCK), sink_rows.reshape(N_KV_A, 1, GROUP_A * BLOCK)


def _swa_fwd(qa, ka, va, pos_col, pos_row, sinks):
    t = qa.shape[0]
    nb = t // BLOCK
    gw = GROUP_A * BLOCK
    slope_rows, sink_rows = _swa_rows(sinks)

    def body(q_ref, kc_ref, kp_ref, vc_ref, vp_ref, pkc_ref, pkp_ref, pq_ref, slope_ref, sink_ref, o_ref, l_ref):
        i = pl.program_id(0)
        pq = pq_ref[...]
        dist_c = _tile_group(jnp.abs(pkc_ref[...] - pq).astype(F32))
        dist_p = _tile_group(jnp.abs(pkp_ref[...] - pq).astype(F32))
        mask_c, older = _swa_masks()
        mask_p = jnp.logical_and(older, i > 0)
        for g in range(N_KV_A):
            gs = slice(g * SLAB, (g + 1) * SLAB)
            x = _heads_beside(q_ref, g)
            slope, sink = slope_ref[g], sink_ref[g]
            s_c = jnp.where(mask_c, _dot(kc_ref[:, gs], x) * SCALE_A - slope * dist_c, NEG)
            s_p = jnp.where(mask_p, _dot(kp_ref[:, gs], x) * SCALE_A - slope * dist_p, NEG)
            m = jnp.maximum(jnp.maximum(jnp.max(s_c, axis=0, keepdims=True),
                                        jnp.max(s_p, axis=0, keepdims=True)), sink)
            e_c = jnp.exp(s_c - m)
            e_p = jnp.exp(s_p - m)
            den = jnp.sum(e_c, axis=0, keepdims=True) + jnp.sum(e_p, axis=0, keepdims=True) + jnp.exp(sink - m)
            inv = 1.0 / den
            ot = (_dot_tn(vc_ref[:, gs], (e_c * inv).astype(BF16))
                  + _dot_tn(vp_ref[:, gs], (e_p * inv).astype(BF16)))
            lse = m + jnp.log(den)
            for hh in range(GROUP_A):
                hd = g * GROUP_A + hh
                seg = slice(hh * BLOCK, (hh + 1) * BLOCK)
                o_ref[:, hd * SLAB:(hd + 1) * SLAB] = ot[:, seg].T.astype(BF16)
                l_ref[hd] = lse[:, seg]

    cur = lambda i: (i, 0)
    prev = lambda i: (jnp.maximum(i - 1, 0), 0)
    kvw = N_KV_A * SLAB
    rows = pl.BlockSpec((N_KV_A, 1, gw), lambda i: (0, 0, 0))
    return pl.pallas_call(
        body, name="swa_fwd", grid=(nb,),
        in_specs=[pl.BlockSpec((BLOCK, HM), cur),
                  pl.BlockSpec((BLOCK, kvw), cur), pl.BlockSpec((BLOCK, kvw), prev),
                  pl.BlockSpec((BLOCK, kvw), cur), pl.BlockSpec((BLOCK, kvw), prev),
                  pl.BlockSpec((BLOCK, 1), cur), pl.BlockSpec((BLOCK, 1), prev),
                  pl.BlockSpec((1, BLOCK), lambda i: (0, i)), rows, rows],
        out_specs=[pl.BlockSpec((BLOCK, HM), cur), pl.BlockSpec((N_HEADS, 1, BLOCK), lambda i: (0, 0, i))],
        out_shape=[jax.ShapeDtypeStruct((t, HM), BF16), jax.ShapeDtypeStruct((N_HEADS, 1, t), F32)],
        compiler_params=_params(("parallel",)),
    )(qa, ka, ka, va, va, pos_col, pos_col, pos_row, slope_rows, sink_rows)


def _swa_bwd(qa, ka, va, d_oa, lse, delta, pos_col, pos_row, sinks):
    t = qa.shape[0]
    nb = t // BLOCK
    gw = GROUP_A * BLOCK
    slope_rows, sink_rows = _swa_rows(sinks)

    def body(q_ref, qn_ref, do_ref, don_ref, l_ref, ln_ref, dl_ref, dln_ref, kp_ref, kc_ref, vp_ref, vc_ref,
             pkp_ref, pkc_ref, pq_ref, pqn_ref, slope_ref, sink_ref, dq_ref, dk_ref, dv_ref, dsink_ref):
        j = pl.program_id(0)
        pkc, pkp = pkc_ref[...], pkp_ref[...]
        dist_cc = _tile_group(jnp.abs(pkc - pq_ref[...]).astype(F32))
        dist_cp = _tile_group(jnp.abs(pkp - pq_ref[...]).astype(F32))
        dist_nc = _tile_group(jnp.abs(pkc - pqn_ref[...]).astype(F32))
        mask_cc, older = _swa_masks()
        mask_cp = jnp.logical_and(older, j > 0)
        mask_nc = jnp.logical_and(older, j < nb - 1)

        @pl.when(j == 0)
        def _():
            dsink_ref[...] = jnp.zeros_like(dsink_ref)

        def tile(k, v, x, dox, lrow, drow, dist, mask, slope):
            s = jnp.where(mask, _dot(k, x) * SCALE_A - slope * dist, NEG)
            p = jnp.exp(s - lrow)
            ds = p * (_dot(v, dox) - drow)
            return p.astype(BF16), ds.astype(BF16)

        for g in range(N_KV_A):
            gs = slice(g * SLAB, (g + 1) * SLAB)
            kc, kp, vc, vp = kc_ref[:, gs], kp_ref[:, gs], vc_ref[:, gs], vp_ref[:, gs]
            slope, sink = slope_ref[g], sink_ref[g]
            x, xn = _heads_beside(q_ref, g), _heads_beside(qn_ref, g)
            dox, doxn = _heads_beside(do_ref, g), _heads_beside(don_ref, g)
            lrow, drow = _rows_beside(l_ref, g), _rows_beside(dl_ref, g)
            lrown, drown = _rows_beside(ln_ref, g), _rows_beside(dln_ref, g)
            p_cc, ds_cc = tile(kc, vc, x, dox, lrow, drow, dist_cc, mask_cc, slope)
            _, ds_cp = tile(kp, vp, x, dox, lrow, drow, dist_cp, mask_cp, slope)
            p_nc, ds_nc = tile(kc, vc, xn, doxn, lrown, drown, dist_nc, mask_nc, slope)
            dqt = (_dot_tn(kc, ds_cc) + _dot_tn(kp, ds_cp)) * SCALE_A
            for hh in range(GROUP_A):
                hd = g * GROUP_A + hh
                dq_ref[:, hd * SLAB:(hd + 1) * SLAB] = dqt[:, hh * BLOCK:(hh + 1) * BLOCK].T.astype(BF16)
            dk_ref[:, gs] = ((_dot_nt(ds_cc, x) + _dot_nt(ds_nc, xn)) * SCALE_A).astype(BF16)
            dv_ref[:, gs] = (_dot_nt(p_cc, dox) + _dot_nt(p_nc, doxn)).astype(BF16)
            dsink_ref[g] -= jnp.exp(sink - lrow) * drow

    cur = lambda j: (j, 0)
    prev = lambda j: (jnp.maximum(j - 1, 0), 0)
    nxt = lambda j: (jnp.minimum(j + 1, nb - 1), 0)
    cur3 = lambda j: (0, 0, j)
    nxt3 = lambda j: (0, 0, jnp.minimum(j + 1, nb - 1))
    kvw = N_KV_A * SLAB
    rows = pl.BlockSpec((N_KV_A, 1, gw), lambda j: (0, 0, 0))
    stat = lambda im: pl.BlockSpec((N_HEADS, 1, BLOCK), im)
    return pl.pallas_call(
        body, name="swa_bwd", grid=(nb,),
        in_specs=[pl.BlockSpec((BLOCK, HM), cur), pl.BlockSpec((BLOCK, HM), nxt),
                  pl.BlockSpec((BLOCK, HM), cur), pl.BlockSpec((BLOCK, HM), nxt),
                  stat(cur3), stat(nxt3), stat(cur3), stat(nxt3),
                  pl.BlockSpec((BLOCK, kvw), prev), pl.BlockSpec((BLOCK, kvw), cur),
                  pl.BlockSpec((BLOCK, kvw), prev), pl.BlockSpec((BLOCK, kvw), cur),
                  pl.BlockSpec((BLOCK, 1), prev), pl.BlockSpec((BLOCK, 1), cur),
                  pl.BlockSpec((1, BLOCK), lambda j: (0, j)),
                  pl.BlockSpec((1, BLOCK), lambda j: (0, jnp.minimum(j + 1, nb - 1))), rows, rows],
        out_specs=[pl.BlockSpec((BLOCK, HM), cur), pl.BlockSpec((BLOCK, kvw), cur),
                   pl.BlockSpec((BLOCK, kvw), cur), rows],
        out_shape=[jax.ShapeDtypeStruct((t, HM), BF16), jax.ShapeDtypeStruct((t, kvw), BF16),
                   jax.ShapeDtypeStruct((t, kvw), BF16), jax.ShapeDtypeStruct((N_KV_A, 1, gw), F32)],
        compiler_params=_params(("arbitrary",)),
    )(qa, qa, d_oa, d_oa, lse, lse, delta, delta, ka, ka, va, va,
      pos_col, pos_col, pos_row, pos_row, slope_rows, sink_rows)


def _lower_triangle(n):
    row = lax.broadcasted_iota(jnp.int32, (n, n), 0)
    col = lax.broadcasted_iota(jnp.int32, (n, n), 1)
    return row >= col


def _upper_triangle(n):
    row = lax.broadcasted_iota(jnp.int32, (n, n), 0)
    col = lax.broadcasted_iota(jnp.int32, (n, n), 1)
    return row <= col


def _mesh_pos():
    return lax.axis_index("x"), lax.axis_index("y"), lax.axis_index("c")


def _flip(v, bit):
    return 1 - v if bit else v


def _direct_copies(srcs, dsts, send_sems, recv_sems, local_sems, gather, sem_base=0):
    x, y, c = _mesh_pos()
    me = 4 * x + 2 * y + c
    local, remote = [], []
    for a, (src, dst) in enumerate(zip(srcs, dsts)):
        local.append(pltpu.make_async_copy(src if gather else src.at[me], dst.at[me], local_sems.at[sem_base + a]))
        for r in range(1, N_DEV):
            px, py, pc = _flip(x, r & 4), _flip(y, r & 2), _flip(c, r & 1)
            sem = (N_DEV - 1) * (sem_base + a) + r - 1
            remote.append(pltpu.make_async_remote_copy(
                src_ref=src if gather else src.at[4 * px + 2 * py + pc], dst_ref=dst.at[me],
                send_sem=send_sems.at[sem], recv_sem=recv_sems.at[sem],
                device_id=(px, py, pc), device_id_type=pl.DeviceIdType.MESH))
    return local, remote


def _start_copies(local, remote):
    for cp in local + remote:
        cp.start()


def _wait_copies(local, remote):
    for cp in remote:
        cp.wait_recv()
    for cp in remote:
        cp.wait_send()
    for cp in local:
        cp.wait()


def _exchange_scratch(n):
    return [pltpu.SemaphoreType.DMA((n * (N_DEV - 1),)), pltpu.SemaphoreType.DMA((n * (N_DEV - 1),)),
            pltpu.SemaphoreType.DMA((n,))]


ANY_SPEC = pl.BlockSpec(memory_space=pl.ANY)


def _mla_fwd(qb, kb, vb, late):
    t = qb.shape[0]
    tq = _attn_tile(t)
    nt = t // tq
    hps = MLA_HEADS_PER_STEP
    w = hps * SLAB
    pairs = [(i, j) for i in range(nt) for j in range(i + 1)]
    i_tab = jnp.asarray(np.array([p[0] for p in pairs], np.int32))
    j_tab = jnp.asarray(np.array([p[1] for p in pairs], np.int32))

    n_late = len(late)

    def body(it_ref, jt_ref, q_ref, k_ref, vt_ref, *rest):
        late_refs, (o_ref, l_ref) = rest[:n_late], rest[n_late:n_late + 2]
        gathered_refs = rest[n_late + 2:2 * n_late + 2]
        m_s, l_s, acc_s, send_sems, recv_sems, local_sems = rest[2 * n_late + 2:]
        n = pl.program_id(1)
        i, j = it_ref[n], jt_ref[n]
        first_step = jnp.logical_and(pl.program_id(0) == 0, n == 0)
        last_step = jnp.logical_and(pl.program_id(0) == N_HEADS // hps - 1, n == len(pairs) - 1)

        @pl.when(first_step)
        def _():
            _start_copies(*_direct_copies(late_refs, gathered_refs, send_sems, recv_sems, local_sems, True))

        @pl.when(j == 0)
        def _():
            m_s[...] = jnp.full_like(m_s, NEG)
            l_s[...] = jnp.zeros_like(l_s)
            acc_s[...] = jnp.zeros_like(acc_s)

        def update(masked):
            def scores(hh):
                sl = slice(hh * SLAB, (hh + 1) * SLAB)
                return _dot_nt(k_ref[:, sl], q_ref[:, sl])

            def softmax(hh, s):
                if masked:
                    s = jnp.where(_upper_triangle(tq), s, NEG)
                m_old = m_s[hh]
                m_new = jnp.maximum(m_old, jnp.max(s, axis=0, keepdims=True))
                alpha = jnp.exp2(m_old - m_new)
                p = jnp.exp2(s - m_new)
                l_s[hh] = alpha * l_s[hh] + jnp.sum(p, axis=0, keepdims=True)
                m_s[hh] = m_new
                return p.astype(BF16), alpha

            def accumulate(hh, p, alpha):
                sl = slice(hh * SLAB, hh * SLAB + V_DIM_B)
                acc_s[sl, :] = alpha * acc_s[sl, :] + _dot(vt_ref[sl, :], p)

            s_next, pending = scores(0), None
            for hh in range(hps):
                s = s_next
                if hh + 1 < hps:
                    s_next = scores(hh + 1)
                p, alpha = softmax(hh, s)
                if pending is not None:
                    accumulate(*pending)
                pending = (hh, p, alpha)
            accumulate(*pending)

        @pl.when(j < i)
        def _():
            update(False)

        @pl.when(j == i)
        def _():
            update(True)
            for hh in range(hps):
                sl = slice(hh * SLAB, (hh + 1) * SLAB)
                o_ref[:, sl] = (acc_s[sl, :] / l_s[hh]).T.astype(BF16)
                l_ref[hh] = m_s[hh] + jnp.log2(l_s[hh])

        @pl.when(last_step)
        def _():
            _wait_copies(*_direct_copies(late_refs, gathered_refs, send_sems, recv_sems, local_sems, True))

    grid_spec = pltpu.PrefetchScalarGridSpec(
        num_scalar_prefetch=2, grid=(N_HEADS // hps, len(pairs)),
        in_specs=[pl.BlockSpec((tq, w), lambda h, n, it, jt: (it[n], h)),
                  pl.BlockSpec((tq, w), lambda h, n, it, jt: (jt[n], h)),
                  pl.BlockSpec((w, tq), lambda h, n, it, jt: (h, jt[n]))] + [ANY_SPEC] * n_late,
        out_specs=[pl.BlockSpec((tq, w), lambda h, n, it, jt: (it[n], h)),
                   pl.BlockSpec((hps, 1, tq), lambda h, n, it, jt: (h, 0, it[n]))] + [ANY_SPEC] * n_late,
        scratch_shapes=[pltpu.VMEM((hps, 1, tq), F32), pltpu.VMEM((hps, 1, tq), F32), pltpu.VMEM((w, tq), F32)]
        + _exchange_scratch(n_late))
    outs = pl.pallas_call(
        body, name="mla_fwd", grid_spec=grid_spec,
        out_shape=[jax.ShapeDtypeStruct((t, HM), BF16), jax.ShapeDtypeStruct((N_HEADS, 1, t), F32)]
        + [jax.ShapeDtypeStruct((N_DEV,) + a.shape, a.dtype) for a in late],
        compiler_params=_params(("arbitrary", "arbitrary")),
    )(i_tab, j_tab, qb, kb, vb.T, *late)
    return outs[0], outs[1], list(outs[2:])


def _mla_bwd(qb, kb, vb, d_ob, lse, delta, grad_slices):
    t = qb.shape[0]
    tq = _attn_tile(t)
    nt = t // tq
    hps = MLA_HEADS_PER_STEP
    w = hps * SLAB
    pairs = [(j, i) for j in range(nt) for i in range(j, nt)]
    j_tab = jnp.asarray(np.array([p[0] for p in pairs], np.int32))
    i_tab = jnp.asarray(np.array([p[1] for p in pairs], np.int32))

    n_ex = len(grad_slices)

    def body(jt_ref, it_ref, q_ref, qt_ref, do_ref, dot_ref, l_ref, dl_ref, k_ref, kt_ref, v_ref, *rest):
        slice_refs, (dqt_ref, dkt_ref, dvt_ref) = rest[:n_ex], rest[n_ex:n_ex + 3]
        part_refs = rest[n_ex + 3:2 * n_ex + 3]
        dk_s, dv_s, send_sems, recv_sems, local_sems = rest[2 * n_ex + 3:]
        n = pl.program_id(1)
        j, i = jt_ref[n], it_ref[n]
        first_step = jnp.logical_and(pl.program_id(0) == 0, n == 0)
        last_step = jnp.logical_and(pl.program_id(0) == N_HEADS // hps - 1, n == len(pairs) - 1)

        @pl.when(first_step)
        def _():
            _start_copies(*_direct_copies(slice_refs, part_refs, send_sems, recv_sems, local_sems, False))

        @pl.when(n == 0)
        def _():
            dqt_ref[...] = jnp.zeros_like(dqt_ref)

        def update(diagonal):
            cols = pl.ds(pl.multiple_of(i * tq, tq), tq)

            def products(hh):
                sl = slice(hh * SLAB, (hh + 1) * SLAB)
                return _dot_nt(k_ref[:, sl], q_ref[:, sl]), _dot_nt(v_ref[:, sl], do_ref[:, sl])

            def softmax_bwd(hh, s, dp):
                if diagonal:
                    s = jnp.where(_upper_triangle(tq), s, NEG)
                p = jnp.exp2(s - l_ref[hh])
                return p.astype(BF16), (p * (dp - dl_ref[hh])).astype(BF16)

            def gradients(hh, p, ds):
                base = hh * SLAB
                vrows = slice(base, base + V_DIM_B)
                qrows = slice(base, base + QK_NOPE + QK_ROPE)
                dv = _dot_nt(dot_ref[vrows, :], p)
                dk = _dot_nt(qt_ref[qrows, :], ds)
                if diagonal:
                    dv_s[base:base + SLAB, :] = jnp.concatenate([dv, jnp.zeros((SLAB - V_DIM_B, tq), F32)], axis=0)
                    dk_s[base:base + SLAB, :] = jnp.concatenate(
                        [dk, jnp.zeros((SLAB - QK_NOPE - QK_ROPE, tq), F32)], axis=0)
                else:
                    dv_s[vrows, :] += dv
                    dk_s[qrows, :] += dk
                dqt_ref[qrows, cols] += _dot(kt_ref[qrows, :], ds)

            for hh in range(hps):
                gradients(hh, *softmax_bwd(hh, *products(hh)))

        @pl.when(i == j)
        def _():
            update(True)

        @pl.when(i > j)
        def _():
            update(False)

        @pl.when(i == nt - 1)
        def _():
            dkt_ref[...] = (dk_s[...] * (1.0 / LOG2E)).astype(BF16)
            dvt_ref[...] = dv_s[...].astype(BF16)

        @pl.when(last_step)
        def _():
            _wait_copies(*_direct_copies(slice_refs, part_refs, send_sems, recv_sems, local_sems, False))

    grid_spec = pltpu.PrefetchScalarGridSpec(
        num_scalar_prefetch=2, grid=(N_HEADS // hps, len(pairs)),
        in_specs=[pl.BlockSpec((tq, w), lambda h, n, jt, it: (it[n], h)),
                  pl.BlockSpec((w, tq), lambda h, n, jt, it: (h, it[n])),
                  pl.BlockSpec((tq, w), lambda h, n, jt, it: (it[n], h)),
                  pl.BlockSpec((w, tq), lambda h, n, jt, it: (h, it[n])),
                  pl.BlockSpec((hps, 1, tq), lambda h, n, jt, it: (h, 0, it[n])),
                  pl.BlockSpec((hps, 1, tq), lambda h, n, jt, it: (h, 0, it[n])),
                  pl.BlockSpec((tq, w), lambda h, n, jt, it: (jt[n], h)),
                  pl.BlockSpec((w, tq), lambda h, n, jt, it: (h, jt[n])),
                  pl.BlockSpec((tq, w), lambda h, n, jt, it: (jt[n], h))] + [ANY_SPEC] * n_ex,
        out_specs=[pl.BlockSpec((w, t), lambda h, n, jt, it: (h, 0)),
                   pl.BlockSpec((w, tq), lambda h, n, jt, it: (h, jt[n])),
                   pl.BlockSpec((w, tq), lambda h, n, jt, it: (h, jt[n]))] + [ANY_SPEC] * n_ex,
        scratch_shapes=[pltpu.VMEM((w, tq), F32), pltpu.VMEM((w, tq), F32)] + _exchange_scratch(n_ex))
    outs = pl.pallas_call(
        body, name="mla_bwd", grid_spec=grid_spec,
        out_shape=[jax.ShapeDtypeStruct((HM, t), F32), jax.ShapeDtypeStruct((HM, t), BF16),
                   jax.ShapeDtypeStruct((HM, t), BF16)]
        + [jax.ShapeDtypeStruct(a.shape, a.dtype) for a in grad_slices],
        compiler_params=_params(("arbitrary", "arbitrary")),
    )(j_tab, i_tab, qb, qb.T, d_ob, d_ob.T, lse, delta, kb, kb.T, vb, *grad_slices)
    return outs[0].T, outs[1].T, outs[2].T, list(outs[3:])


def _merge_fwd(out_a, out_b, gates, x, w_oa, w_ob, w_out, g2, g3):
    t = x.shape[0]
    tm = _token_tile(t)

    def body(oa_ref, ob_ref, gates_ref, x_ref, woa_ref, wob_ref, wout_ref, g2_ref, g3_ref,
             oap_ref, obp_ref, merged_ref, y_ref, x1_ref, h2_ref):
        oa_p = _dot(oa_ref[...], woa_ref[...])
        ob_p = _dot(ob_ref[...], wob_ref[...])
        oap_ref[...] = oa_p.astype(BF16)
        obp_ref[...] = ob_p.astype(BF16)
        sa = _sigmoid(gates_ref[:, 0:D_MODEL])
        sb = _sigmoid(gates_ref[:, D_MODEL:2 * D_MODEL])
        merged = (sa * oa_p + sb * ob_p).astype(BF16)
        merged_ref[...] = merged
        y = _dot(merged, wout_ref[...])
        y_ref[...] = y
        x1 = x_ref[...] + y * _rms_r(y) * g2_ref[...]
        x1_ref[...] = x1
        h2_ref[...] = (x1 * _rms_r(x1) * g3_ref[...]).astype(BF16)

    def sds(dt):
        return jax.ShapeDtypeStruct((t, D_MODEL), dt)

    row = _row_spec(tm, D_MODEL)
    return pl.pallas_call(
        body, name="merge_fwd", grid=(t // tm,),
        in_specs=[_row_spec(tm, HM), _row_spec(tm, HM), _row_spec(tm, 2 * D_MODEL), row,
                  _full_spec((HM, D_MODEL)), _full_spec((HM, D_MODEL)), _full_spec((D_MODEL, D_MODEL)),
                  _full_spec((1, D_MODEL)), _full_spec((1, D_MODEL))],
        out_specs=[row] * 6,
        out_shape=[sds(BF16), sds(BF16), sds(BF16), sds(F32), sds(F32), sds(BF16)],
        compiler_params=_params(("parallel",)),
    )(out_a, out_b, gates, x, w_oa, w_ob, w_out, g2, g3)


def _row_sums(x):
    hi = x.astype(BF16)
    lo = (x - hi.astype(F32)).astype(BF16)
    ones = jnp.ones((8, SLAB), BF16)
    return (_dot_nt(ones, hi) + _dot_nt(ones, lo))[0:1]


def _merge_bwd(dx1, y, gates, oa_p, ob_p, out_a, out_b, w_oa, w_ob, w_out, g2):
    t = dx1.shape[0]
    tm = _token_tile(t)

    def body(dx1_ref, y_ref, gates_ref, oap_ref, obp_ref, oa_ref, ob_ref, woa_ref, wob_ref, wout_ref, g2_ref,
             dy_ref, doap_ref, dobp_ref, dgates_ref, doa_ref, dob_ref, dla_ref, dlb_ref, dg2_ref):
        dx1v = dx1_ref[...]
        yv = y_ref[...]
        r2 = _rms_r(yv)
        _acc_rows(dg2_ref, dx1v * yv * r2)
        dy = _rms_bwd(yv, r2, g2_ref[...], dx1v).astype(BF16)
        dy_ref[...] = dy
        dm = _dot_nt(dy, wout_ref[...])
        sa = _sigmoid(gates_ref[:, 0:D_MODEL])
        sb = _sigmoid(gates_ref[:, D_MODEL:2 * D_MODEL])
        d_oap = (dm * sa).astype(BF16)
        d_obp = (dm * sb).astype(BF16)
        doap_ref[...] = d_oap
        dobp_ref[...] = d_obp
        dgates_ref[:, 0:D_MODEL] = (dm * oap_ref[...].astype(F32) * sa * (1.0 - sa)).astype(BF16)
        dgates_ref[:, D_MODEL:2 * D_MODEL] = (dm * obp_ref[...].astype(F32) * sb * (1.0 - sb)).astype(BF16)
        d_oa = _dot_nt(d_oap, woa_ref[...])
        d_ob = _dot_nt(d_obp, wob_ref[...])
        doa_ref[...] = d_oa.astype(BF16)
        dob_ref[...] = d_ob.astype(BF16)
        for hd in range(N_HEADS):
            sl = slice(hd * SLAB, (hd + 1) * SLAB)
            dla_ref[hd] = _row_sums(d_oa[:, sl] * oa_ref[:, sl].astype(F32))
            dlb_ref[hd] = _row_sums(d_ob[:, sl] * ob_ref[:, sl].astype(F32))

    def sds(n, dt):
        return jax.ShapeDtypeStruct((t, n), dt)

    row = _row_spec(tm, D_MODEL)
    head3 = pl.BlockSpec((N_HEADS, 1, tm), lambda i: (0, 0, i))
    return pl.pallas_call(
        body, name="merge_bwd", grid=(t // tm,),
        in_specs=[row, row, _row_spec(tm, 2 * D_MODEL), row, row, _row_spec(tm, HM), _row_spec(tm, HM),
                  _full_spec((HM, D_MODEL)), _full_spec((HM, D_MODEL)), _full_spec((D_MODEL, D_MODEL)),
                  _full_spec((1, D_MODEL))],
        out_specs=[row, row, row, _row_spec(tm, 2 * D_MODEL), _row_spec(tm, HM), _row_spec(tm, HM),
                   head3, head3, _full_spec((1, D_MODEL))],
        out_shape=[sds(D_MODEL, BF16), sds(D_MODEL, BF16), sds(D_MODEL, BF16), sds(2 * D_MODEL, BF16),
                   sds(HM, BF16), sds(HM, BF16),
                   jax.ShapeDtypeStruct((N_HEADS, 1, t), F32), jax.ShapeDtypeStruct((N_HEADS, 1, t), F32),
                   jax.ShapeDtypeStruct((1, D_MODEL), F32)],
        compiler_params=_params(("arbitrary",)),
    )(dx1, y, gates, oa_p, ob_p, out_a, out_b, w_oa, w_ob, w_out, g2)


def _mlp_fwd_bwd(x1, h2, target, w_up, w_down, g3, g4):
    t = x1.shape[0]
    tm = _token_tile(t)
    fs = D_FF // N_DEV

    def body(x1_ref, h2_ref, tgt_ref, wup_ref, wdown_ref, g3_ref, g4_ref,
             a_ref, du_ref, dy2_ref, dx1_ref, loss_ref, dg3_ref, dg4_ref):
        x1v = x1_ref[...]
        h2v = h2_ref[...]
        u = jnp.concatenate([_dot(h2v, wup_ref[s]) for s in range(N_DEV)], axis=1)
        ru = jnp.maximum(u, 0.0)
        a = (ru * ru).astype(BF16)
        a_ref[...] = a
        y2 = _dot(a, wdown_ref[...])
        r4 = _rms_r(y2)
        diff = x1v + y2 * r4 * g4_ref[...] - tgt_ref[...]
        _acc_rows(loss_ref, jnp.sum(diff * diff, axis=-1, keepdims=True) * (0.5 / D_MODEL)
                  * jnp.ones((1, SLAB), F32))
        dx2 = diff * (1.0 / D_MODEL)
        _acc_rows(dg4_ref, dx2 * y2 * r4)
        dy2 = _rms_bwd(y2, r4, g4_ref[...], dx2).astype(BF16)
        dy2_ref[...] = dy2
        du = (_dot_nt(dy2, wdown_ref[...]) * (2.0 * ru)).astype(BF16)
        du_ref[...] = du
        dh2 = _dot_nt(du[:, 0:fs], wup_ref[0])
        for s in range(1, N_DEV):
            dh2 += _dot_nt(du[:, s * fs:(s + 1) * fs], wup_ref[s])
        r3 = _rms_r(x1v)
        _acc_rows(dg3_ref, dh2 * x1v * r3)
        dx1_ref[...] = dx2 + _rms_bwd(x1v, r3, g3_ref[...], dh2)

    row = _row_spec(tm, D_MODEL)
    frow = _row_spec(tm, D_FF)
    vec = _full_spec((1, D_MODEL))
    return pl.pallas_call(
        body, name="mlp_fwd_bwd", grid=(t // tm,),
        in_specs=[row, row, row, _full_spec((N_DEV, D_MODEL, fs)), _full_spec((D_FF, D_MODEL)), vec, vec],
        out_specs=[frow, frow, row, row, _full_spec((1, SLAB)), vec, vec],
        out_shape=[jax.ShapeDtypeStruct((t, D_FF), BF16), jax.ShapeDtypeStruct((t, D_FF), BF16),
                   jax.ShapeDtypeStruct((t, D_MODEL), BF16), jax.ShapeDtypeStruct((t, D_MODEL), F32),
                   jax.ShapeDtypeStruct((1, SLAB), F32), jax.ShapeDtypeStruct((1, D_MODEL), F32),
                   jax.ShapeDtypeStruct((1, D_MODEL), F32)],
        compiler_params=_params(("arbitrary",)),
    )(x1, h2, target, w_up, w_down, g3, g4)


def _inproj_bwd(dgates, dqa, dka, dva, dqb, dkb, dvb, cq, ckv, x, dx1, rope_c, rope_s1, rope_s2,
                g1, g_q, g_kv, w_in, w_qb, w_kvb):
    t = x.shape[0]
    tm = _token_tile(t)

    def body(dgates_ref, dqa_ref, dka_ref, dva_ref, dqb_ref, dkb_ref, dvb_ref, cq_ref, ckv_ref, x_ref, dx1_ref,
             c_ref, s1_ref, s2_ref, g1_ref, gq_ref, gkv_ref, win_ref, wqb_ref, wkvb_ref,
             dproj_ref, dqbr_ref, dkvb_ref, dx_ref, dg1_ref, dgq_ref, dgkv_ref):
        c, s1, s2 = c_ref[...], s1_ref[...], s2_ref[...]
        dk_sum = jnp.zeros((tm, SLAB), F32)
        for hd in range(N_HEADS):
            sl = slice(hd * SLAB, (hd + 1) * SLAB)
            dqbr_ref[:, sl] = _rope_bwd(dqb_ref[:, sl] * SCALE_B, c, s1, s2).astype(BF16)
            dk_sum += dkb_ref[:, sl].astype(F32)
        dkvb_ref[:, 0:HM] = dkb_ref[...]
        dkvb_ref[:, HM:2 * HM] = dvb_ref[...]
        dkr = _rope_bwd(dk_sum, c, s1, s2)
        dcqn = _dot_nt(dqbr_ref[...], wqb_ref[...])
        cq = cq_ref[...]
        rq = _rms_r(cq)
        _acc_rows(dgq_ref, dcqn * cq * rq)
        dcq = _rms_bwd(cq, rq, gq_ref[...], dcqn)
        dckvn = _dot_nt(dkvb_ref[...], wkvb_ref[...])
        ckv = ckv_ref[...]
        rkv = _rms_r(ckv)
        _acc_rows(dgkv_ref, dckvn * ckv * rkv)
        dckv = _rms_bwd(ckv, rkv, gkv_ref[...], dckvn)
        dproj_ref[:, C_GATES:C_QA] = dgates_ref[...]
        dproj_ref[:, C_QA:C_KA] = dqa_ref[...]
        dproj_ref[:, C_KA:C_VA] = dka_ref[...]
        dproj_ref[:, C_VA:C_CQ] = dva_ref[...]
        dproj_ref[:, C_CQ:C_CKV] = dcq.astype(BF16)
        dproj_ref[:, C_CKV:C_KR] = dckv.astype(BF16)
        dproj_ref[:, C_KR:D_IN_PAD] = dkr.astype(BF16)
        dh = _dot_nt(dproj_ref[...], win_ref[...])
        xv = x_ref[...]
        r1 = _rms_r(xv)
        _acc_rows(dg1_ref, dh * xv * r1)
        dx_ref[...] = dx1_ref[...] + _rms_bwd(xv, r1, g1_ref[...], dh)

    kvw = N_KV_A * SLAB
    row = _row_spec(tm, D_MODEL)
    hm = _row_spec(tm, HM)
    tab = _row_spec(tm, SLAB)
    return pl.pallas_call(
        body, name="inproj_bwd", grid=(t // tm,),
        in_specs=[_row_spec(tm, 2 * D_MODEL), hm, _row_spec(tm, kvw), _row_spec(tm, kvw), hm, hm, hm,
                  _row_spec(tm, Q_LORA), _row_spec(tm, KV_LORA), row, row, tab, tab, tab,
                  _full_spec((1, D_MODEL)), _full_spec((1, Q_LORA)), _full_spec((1, KV_LORA)),
                  _full_spec((D_MODEL, D_IN_PAD)), _full_spec((Q_LORA, HM)), _full_spec((KV_LORA, 2 * HM))],
        out_specs=[_row_spec(tm, D_IN_PAD), hm, _row_spec(tm, 2 * HM), row,
                   _full_spec((1, D_MODEL)), _full_spec((1, Q_LORA)), _full_spec((1, KV_LORA))],
        out_shape=[jax.ShapeDtypeStruct((t, D_IN_PAD), BF16), jax.ShapeDtypeStruct((t, HM), BF16),
                   jax.ShapeDtypeStruct((t, 2 * HM), BF16), jax.ShapeDtypeStruct((t, D_MODEL), F32),
                   jax.ShapeDtypeStruct((1, D_MODEL), F32), jax.ShapeDtypeStruct((1, Q_LORA), F32),
                   jax.ShapeDtypeStruct((1, KV_LORA), F32)],
        compiler_params=_params(("arbitrary",)),
    )(dgates, dqa, dka, dva, dqb, dkb, dvb, cq, ckv, x, dx1, rope_c, rope_s1, rope_s2,
      g1, g_q, g_kv, w_in, w_qb, w_kvb)


def _matmul_tn(a, b, name, out_dtype=F32, n_shards=1):
    t, k = a.shape
    n = b.shape[1]
    bt = min(t, 512)
    bk = min(k, 1024)
    bn = min(n, 1024)
    ns = n // n_shards
    per_block = bn // ns
    steps = t // bt

    def body(a_ref, b_ref, o_ref, acc):
        s = pl.program_id(2)

        @pl.when(s == 0)
        def _():
            acc[...] = jnp.zeros_like(acc)

        acc[...] += _dot_tn(a_ref[...], b_ref[...])

        @pl.when(s == steps - 1)
        def _():
            if n_shards > 1:
                for p in range(per_block):
                    o_ref[p] = acc[:, p * ns:(p + 1) * ns].astype(out_dtype)
            else:
                o_ref[...] = acc[...].astype(out_dtype)

    if n_shards > 1:
        out_spec = pl.BlockSpec((per_block, bk, ns), lambda i, j, s: (j, i, 0))
        out_shape = jax.ShapeDtypeStruct((n_shards, k, ns), out_dtype)
    else:
        out_spec = pl.BlockSpec((bk, bn), lambda i, j, s: (i, j))
        out_shape = jax.ShapeDtypeStruct((k, n), out_dtype)
    return pl.pallas_call(
        body, name=name, grid=(k // bk, n // bn, steps),
        in_specs=[pl.BlockSpec((bt, bk), lambda i, j, s: (s, i)), pl.BlockSpec((bt, bn), lambda i, j, s: (s, j))],
        out_specs=out_spec, out_shape=out_shape, scratch_shapes=[pltpu.VMEM((bk, bn), F32)],
        compiler_params=_params(("parallel", "parallel", "arbitrary")),
    )(a, b)


def _all_gather(shards):
    n = len(shards)

    def body(*refs):
        srcs, dsts = refs[:n], refs[n:2 * n]
        send_sems, recv_sems, local_sems = refs[2 * n:]
        x, y, c = _mesh_pos()
        me, sibling = (x, y, c), (x, y, 1 - c)
        chips = [(1 - x, y), (x, 1 - y), (1 - x, 1 - y)]

        def slot(a, px, py, pc):
            return dsts[a].at[4 * px + 2 * py + pc]

        def copy(a, k, block, to, src=None):
            return pltpu.make_async_remote_copy(
                src_ref=slot(a, *block) if src is None else src, dst_ref=slot(a, *block),
                send_sem=send_sems.at[(N_DEV - 1) * a + k], recv_sem=recv_sems.at[(N_DEV - 1) * a + k],
                device_id=to, device_id_type=pl.DeviceIdType.MESH)

        mine = [pltpu.make_async_copy(srcs[a], slot(a, *me), local_sems.at[a]) for a in range(n)]
        first = []
        for a in range(n):
            first.append(copy(a, 0, me, sibling, src=srcs[a]))
            first += [copy(a, 1 + j, me, (*chip, c), src=srcs[a]) for j, chip in enumerate(chips)]
        for cp in mine + first:
            cp.start()
        passed = []
        for j, chip in enumerate(chips):
            for a in range(n):
                copy(a, 1 + j, (*chip, c), me).wait_recv()
                passed.append(copy(a, 4 + j, (*chip, c), sibling))
                passed[-1].start()
        for a in range(n):
            copy(a, 0, sibling, me).wait_recv()
        for j, chip in enumerate(chips):
            for a in range(n):
                copy(a, 4 + j, (*chip, 1 - c), me).wait_recv()
        for cp in first + passed:
            cp.wait_send()
        for cp in mine:
            cp.wait()

    return pl.pallas_call(
        body, name="all_gather_early",
        out_shape=[jax.ShapeDtypeStruct((N_DEV,) + a.shape, a.dtype) for a in shards],
        in_specs=[ANY_SPEC] * n, out_specs=[ANY_SPEC] * n, scratch_shapes=_exchange_scratch(n),
    )(*shards)


def _exchange_grads(slices, small):
    n = len(slices)

    def body(*refs):
        srcs, s_ref = refs[:n], refs[n]
        dsts, s_dst = refs[n + 1:2 * n + 1], refs[2 * n + 1]
        sems = refs[2 * n + 2:]
        parts = _direct_copies(srcs, dsts, *sems, False)
        smalls = _direct_copies([s_ref], [s_dst], *sems, True, sem_base=n)
        _start_copies(*parts)
        _start_copies(*smalls)
        _wait_copies(*parts)
        _wait_copies(*smalls)

    outs = pl.pallas_call(
        body, name="exchange_grads",
        out_shape=[jax.ShapeDtypeStruct(a.shape, a.dtype) for a in slices]
        + [jax.ShapeDtypeStruct((N_DEV,) + small.shape, small.dtype)],
        in_specs=[ANY_SPEC] * (n + 1), out_specs=[ANY_SPEC] * (n + 1), scratch_shapes=_exchange_scratch(n + 1),
    )(*slices, small)
    return list(outs[:n]), outs[n]


def _adamw(parts, w, m, v, name):
    _, k, n = parts.shape
    bk = min(k, ADAM_ROWS)
    c1 = 1.0 - ADAM_B1 ** ADAM_STEP
    c2 = 1.0 - ADAM_B2 ** ADAM_STEP

    def body(p_ref, w_ref, m_ref, v_ref, g_ref, d_ref, mo_ref, vo_ref):
        g = p_ref[0].astype(F32)
        for s in range(1, N_DEV):
            g = g + p_ref[s].astype(F32)
        g_ref[0] = g
        m_new = ADAM_B1 * m_ref[0] + (1.0 - ADAM_B1) * g
        v_new = ADAM_B2 * v_ref[0] + (1.0 - ADAM_B2) * (g * g)
        mo_ref[0] = m_new
        vo_ref[0] = v_new
        m_hat = m_new / c1
        v_hat = v_new / c2
        d_ref[0] = -ADAM_LR * (m_hat / (jnp.sqrt(v_hat) + ADAM_EPS) + ADAM_WD * w_ref[0])

    blk = pl.BlockSpec((1, bk, n), lambda i: (0, i, 0))
    out = jax.ShapeDtypeStruct((1, k, n), F32)
    return pl.pallas_call(
        body, name=name, grid=(k // bk,),
        in_specs=[pl.BlockSpec((N_DEV, bk, n), lambda i: (0, i, 0)), blk, blk, blk],
        out_specs=[blk] * 4, out_shape=[out] * 4,
        compiler_params=_params(("parallel",)),
    )(parts, w, m, v)


def _pad_heads_cols(w, heads, width):
    k = w.shape[0]
    w = w.reshape(k, heads, width)
    return jnp.pad(w, ((0, 0), (0, 0), (0, SLAB - width))).reshape(k, heads * SLAB)


def _unpad_heads_cols(w, heads, width):
    k = w.shape[0]
    return w.reshape(k, heads, SLAB)[:, :, :width].reshape(k, heads * width)


def _pad_heads_rows(w, heads, width):
    n = w.shape[1]
    w = w.reshape(heads, width, n)
    return jnp.pad(w, ((0, 0), (0, SLAB - width), (0, 0))).reshape(heads * SLAB, n)


def _unpad_heads_rows(w, heads, width):
    n = w.shape[1]
    return w.reshape(heads, SLAB, n)[:, :width, :].reshape(heads * width, n)


def _pad_w_in(w_in):
    o = 2 * D_MODEL
    qa = _pad_heads_cols(w_in[:, o:o + 512], N_HEADS, HEAD_A)
    ka = _pad_heads_cols(w_in[:, o + 512:o + 640], N_KV_A, HEAD_A)
    va = _pad_heads_cols(w_in[:, o + 640:o + 768], N_KV_A, HEAD_A)
    kr = jnp.pad(w_in[:, o + 1152:o + 1184], ((0, 0), (QK_NOPE, SLAB - QK_NOPE - QK_ROPE)))
    return jnp.concatenate([w_in[:, :o], qa, ka, va, w_in[:, o + 768:o + 1152], kr], axis=1)


def _unpad_w_in(w):
    qa = _unpad_heads_cols(w[:, C_QA:C_KA], N_HEADS, HEAD_A)
    ka = _unpad_heads_cols(w[:, C_KA:C_VA], N_KV_A, HEAD_A)
    va = _unpad_heads_cols(w[:, C_VA:C_CQ], N_KV_A, HEAD_A)
    kr = w[:, C_KR + QK_NOPE:C_KR + QK_NOPE + QK_ROPE]
    return jnp.concatenate([w[:, :C_QA], qa, ka, va, w[:, C_CQ:C_KR], kr], axis=1)


def _pad_w_kvb(w_kvb):
    w = w_kvb.reshape(KV_LORA, N_HEADS, QK_NOPE + V_DIM_B)
    k = jnp.pad(w[:, :, :QK_NOPE], ((0, 0), (0, 0), (0, SLAB - QK_NOPE))).reshape(KV_LORA, HM)
    v = jnp.pad(w[:, :, QK_NOPE:], ((0, 0), (0, 0), (0, SLAB - V_DIM_B))).reshape(KV_LORA, HM)
    return jnp.concatenate([k, v], axis=1)


def _unpad_w_kvb(w):
    k = w[:, :HM].reshape(KV_LORA, N_HEADS, SLAB)[:, :, :QK_NOPE]
    v = w[:, HM:].reshape(KV_LORA, N_HEADS, SLAB)[:, :, :V_DIM_B]
    return jnp.concatenate([k, v], axis=2).reshape(KV_LORA, N_HEADS * (QK_NOPE + V_DIM_B))


def _col_shards(w):
    k, n = w.shape
    return w.reshape(k, N_DEV, n // N_DEV).transpose(1, 0, 2)


def _from_col_shards(s):
    _, k, ns = s.shape
    return s.transpose(1, 0, 2).reshape(k, N_DEV * ns)


def _freq_row():
    freqs = ROPE_THETA ** (-jnp.arange(0, QK_ROPE, 2, dtype=F32) / QK_ROPE)
    return jnp.concatenate([jnp.zeros((QK_NOPE,), F32), freqs, freqs,
                            jnp.zeros((SLAB - QK_NOPE - QK_ROPE,), F32)]).reshape(1, SLAB)


SMALL_D_ROWS = ("pre_norm_mix", "post_norm_mix", "pre_norm_mlp", "post_norm_mlp")
SMALL_Q_OFF, SMALL_KV_OFF, SMALL_SINK_OFF, SMALL_LOSS_OFF = 0, 256, 384, 392


def _pack_small(vals):
    row4 = jnp.concatenate([vals["q_a_norm"].reshape(-1), vals["kv_a_norm"].reshape(-1), vals["sinks"].reshape(-1),
                            vals["loss"].reshape(-1), jnp.zeros((1024 - 393,), F32)])
    rows = [vals[n].reshape(1024) for n in SMALL_D_ROWS] + [row4]
    return jnp.concatenate([jnp.stack(rows), jnp.zeros((SMALL_ROWS - 5, 1024), F32)], axis=0)


def _unpack_small(blk):
    out = {n: blk[i].reshape(1, 1024) for i, n in enumerate(SMALL_D_ROWS)}
    out["q_a_norm"] = blk[4, SMALL_Q_OFF:SMALL_Q_OFF + 256].reshape(1, 256)
    out["kv_a_norm"] = blk[4, SMALL_KV_OFF:SMALL_KV_OFF + 128].reshape(1, 128)
    out["sinks"] = blk[4, SMALL_SINK_OFF:SMALL_SINK_OFF + 8].reshape(1, 8)
    out["loss"] = blk[4, SMALL_LOSS_OFF]
    return out


WEIGHT_ORDER = ("pre_norm_mix", "w_in", "q_a_norm", "w_q_b", "kv_a_norm", "w_kv_b", "sinks", "w_o_a", "w_o_b",
                "w_out", "post_norm_mix", "pre_norm_mlp", "w_up", "w_down", "post_norm_mlp")
SMALL_NAMES = ("pre_norm_mix", "q_a_norm", "kv_a_norm", "sinks", "post_norm_mix", "pre_norm_mlp", "post_norm_mlp")


def kernel(x, positions, pre_norm_mix, w_in, q_a_norm, w_q_b, kv_a_norm, w_kv_b, sinks, w_o_a, w_o_b, w_out, post_norm_mix, pre_norm_mlp, w_up, w_down, post_norm_mlp, loss_target, m_pre_norm_mix, m_w_in, m_q_a_norm, m_w_q_b, m_kv_a_norm, m_w_kv_b, m_sinks, m_w_o_a, m_w_o_b, m_w_out, m_post_norm_mix, m_pre_norm_mlp, m_w_up, m_w_down, m_post_norm_mlp, v_pre_norm_mix, v_w_in, v_q_a_norm, v_w_q_b, v_kv_a_norm, v_w_kv_b, v_sinks, v_w_o_a, v_w_o_b, v_w_out, v_post_norm_mix, v_pre_norm_mlp, v_w_up, v_w_down, v_post_norm_mlp):
    weights = dict(pre_norm_mix=pre_norm_mix, w_in=w_in, q_a_norm=q_a_norm, w_q_b=w_q_b, kv_a_norm=kv_a_norm,
                   w_kv_b=w_kv_b, sinks=sinks, w_o_a=w_o_a, w_o_b=w_o_b, w_out=w_out, post_norm_mix=post_norm_mix,
                   pre_norm_mlp=pre_norm_mlp, w_up=w_up, w_down=w_down, post_norm_mlp=post_norm_mlp)
    m_in = dict(pre_norm_mix=m_pre_norm_mix, w_in=m_w_in, q_a_norm=m_q_a_norm, w_q_b=m_w_q_b, kv_a_norm=m_kv_a_norm,
                w_kv_b=m_w_kv_b, sinks=m_sinks, w_o_a=m_w_o_a, w_o_b=m_w_o_b, w_out=m_w_out,
                post_norm_mix=m_post_norm_mix, pre_norm_mlp=m_pre_norm_mlp, w_up=m_w_up, w_down=m_w_down,
                post_norm_mlp=m_post_norm_mlp)
    v_in = dict(pre_norm_mix=v_pre_norm_mix, w_in=v_w_in, q_a_norm=v_q_a_norm, w_q_b=v_w_q_b, kv_a_norm=v_kv_a_norm,
                w_kv_b=v_w_kv_b, sinks=v_sinks, w_o_a=v_w_o_a, w_o_b=v_w_o_b, w_out=v_w_out,
                post_norm_mix=v_post_norm_mix, pre_norm_mlp=v_pre_norm_mlp, w_up=v_w_up, w_down=v_w_down,
                post_norm_mlp=v_post_norm_mlp)

    xs, pos, target = x[0], positions[0], loss_target[0]
    t = xs.shape[0]
    pos_col = pos.reshape(t, 1)
    pos_row = pos.reshape(1, t)
    g1, g2, g3, g4 = (weights[n] for n in SMALL_D_ROWS)
    g_q, g_kv = q_a_norm, kv_a_norm
    sink_vec = sinks.reshape(N_HEADS)
    shard = {n: weights[n][0].astype(BF16) for n in EARLY + LATE}

    e_in, e_qb, e_kvb = _all_gather([shard[n] for n in EARLY])
    w_in_p = _pad_w_in(_from_col_shards(e_in))
    w_qb = _pad_heads_cols(_from_col_shards(e_qb), N_HEADS, QK_NOPE + QK_ROPE)
    w_kvb = _pad_w_kvb(_from_col_shards(e_kvb))

    rc, rs1, rs2 = _rope_tables(pos_col, _freq_row())
    (h, gates, qa, ka, va, cq, ckv, cqn, ckvn, qb, kb, vb) = _inproj_fwd(
        xs, g1, w_in_p, g_q, w_qb, g_kv, w_kvb, rc, rs1, rs2)
    out_a, lse_a = _swa_fwd(qa, ka, va, pos_col, pos_row, sink_vec)
    out_b, lse_b, (l_oa, l_ob, l_out, w_up_s, l_down) = _mla_fwd(qb, kb, vb, [shard[n] for n in LATE])
    w_oa = _pad_heads_rows(_from_col_shards(l_oa), N_HEADS, HEAD_A)
    w_ob = _pad_heads_rows(_from_col_shards(l_ob), N_HEADS, V_DIM_B)
    w_out_f = l_out.reshape(D_MODEL, D_MODEL)
    w_down_f = l_down.reshape(D_FF, D_MODEL)

    oa_p, ob_p, merged, y, x1, h2 = _merge_fwd(out_a, out_b, gates, xs, w_oa, w_ob, w_out_f, g2, g3)
    a, du, dy2, dx1, loss, dg3, dg4 = _mlp_fwd_bwd(x1, h2, target, w_up_s, w_down_f, g3, g4)
    (dy, d_oap, d_obp, dgates, d_oa, d_ob, delta_a, delta_b, dg2) = _merge_bwd(
        dx1, y, gates, oa_p, ob_p, out_a, out_b, w_oa, w_ob, w_out_f, g2)
    late_slices = [
        _col_shards(_unpad_heads_rows(_matmul_tn(out_a, d_oap, "dw_o_a"), N_HEADS, HEAD_A)).astype(BF16),
        _col_shards(_unpad_heads_rows(_matmul_tn(out_b, d_obp, "dw_o_b"), N_HEADS, V_DIM_B)).astype(BF16),
        _matmul_tn(merged, dy, "dw_out", BF16).reshape(N_DEV, D_MODEL // N_DEV, D_MODEL),
        _matmul_tn(h2, du, "dw_up", BF16, N_DEV),
        _matmul_tn(a, dy2, "dw_down", BF16).reshape(N_DEV, D_FF // N_DEV, D_MODEL),
    ]
    dqa, dka, dva, dsink = _swa_bwd(qa, ka, va, d_oa, lse_a, delta_a, pos_col, pos_row, sink_vec)
    dqb, dkb, dvb, late_parts = _mla_bwd(qb, kb, vb, d_ob, lse_b, delta_b, late_slices)
    dproj, dqbr, dkvb, dx, dg1, dgq, dgkv = _inproj_bwd(
        dgates, dqa, dka, dva, dqb, dkb, dvb, cq, ckv, xs, dx1, rc, rs1, rs2, g1, g_q, g_kv, w_in_p, w_qb, w_kvb)
    early_slices = [
        _col_shards(_unpad_w_in(_matmul_tn(h, dproj, "dw_in"))).astype(BF16),
        _col_shards(_unpad_heads_cols(_matmul_tn(cqn, dqbr, "dw_q_b"), N_HEADS, QK_NOPE + QK_ROPE)).astype(BF16),
        _col_shards(_unpad_w_kvb(_matmul_tn(ckvn, dkvb, "dw_kv_b"))).astype(BF16),
    ]
    small_grads = {"pre_norm_mix": dg1, "post_norm_mix": dg2, "pre_norm_mlp": dg3, "post_norm_mlp": dg4,
                   "q_a_norm": dgq, "kv_a_norm": dgkv, "sinks": dsink.reshape(N_HEADS, BLOCK).sum(axis=1),
                   "loss": loss[0, 0:1]}
    early_parts, s_parts = _exchange_grads(early_slices, _pack_small(small_grads))

    updates = {}
    for name, parts in zip(EARLY + LATE, early_parts + late_parts):
        outs = _adamw(parts, weights[name], m_in[name], v_in[name], "adamw_" + name)
        for kind, arr in zip(("g", "d", "m", "v"), outs):
            updates[kind, name] = arr
    zero = jnp.zeros((), F32)
    pack = lambda src: _pack_small({**{n: src[n] for n in SMALL_NAMES}, "loss": zero})[None]
    smalls = _adamw(s_parts, pack(weights), pack(m_in), pack(v_in), "adamw_small")
    for kind, blk in zip(("g", "d", "m", "v"), smalls):
        for wname, piece in _unpack_small(blk[0]).items():
            updates[kind, wname] = piece
    results = [updates[kind, name] for kind in ("g", "d", "m", "v") for name in WEIGHT_ORDER]
    return (updates["g", "loss"], dx[None], *results)
```

```python
import functools

import numpy as np
import jax
import jax.numpy as jnp
from jax import lax
from jax.experimental import pallas as pl
from jax.experimental.pallas import tpu as pltpu

F32 = jnp.float32
BF16 = jnp.bfloat16

D_MODEL = 1024
D_FF = 4096
N_HEADS = 8
N_KV_A = 2
GROUP_A = N_HEADS // N_KV_A
HEAD_A = 64
QK_NOPE = 64
QK_ROPE = 32
V_DIM_B = 64
Q_LORA = 256
KV_LORA = 128
BLOCK = 128
SLAB = 128
ROPE_THETA = 10000.0
EPS = 1e-6
N_DEV = 8
NEG = -1e30

SCALE_A = HEAD_A ** -0.5
SCALE_B = (QK_NOPE + QK_ROPE) ** -0.5
LOG2E = 1.4426950408889634
SCORE_B = SCALE_B * LOG2E
MLA_HEADS_PER_STEP = 4
SLOPES_A = tuple(2.0 ** (-8.0 * (h + 1) / N_HEADS) for h in range(N_HEADS))

ADAM_LR = 0.001
ADAM_B1 = 0.9
ADAM_B2 = 0.999
ADAM_EPS = 1e-08
ADAM_WD = 0.01
ADAM_STEP = 10

HM = N_HEADS * SLAB
C_GATES = 0
C_QA = 2 * D_MODEL
C_KA = C_QA + HM
C_VA = C_KA + N_KV_A * SLAB
C_CQ = C_VA + N_KV_A * SLAB
C_CKV = C_CQ + Q_LORA
C_KR = C_CKV + KV_LORA
D_IN_PAD = C_KR + SLAB

VMEM_LIMIT = 56 * 1024 * 1024

EARLY = ("w_in", "w_q_b", "w_kv_b")
LATE = ("w_o_a", "w_o_b", "w_out", "w_up", "w_down")
ADAM_ROWS = 256
SMALL_ROWS = 8


def _token_tile(t):
    return min(256, t)


def _attn_tile(t):
    return 512 if t >= 2048 else 128


def _params(sem, vmem=VMEM_LIMIT):
    return pltpu.CompilerParams(dimension_semantics=sem, vmem_limit_bytes=vmem)


def _dot(a, b):
    return jnp.dot(a, b, preferred_element_type=F32)


def _dot_nt(a, b):
    return lax.dot_general(a, b, (((1,), (1,)), ((), ())), preferred_element_type=F32)


def _dot_tn(a, b):
    return lax.dot_general(a, b, (((0,), (0,)), ((), ())), preferred_element_type=F32)


def _rms_r(x):
    return lax.rsqrt(jnp.mean(x * x, axis=-1, keepdims=True) + EPS)


def _rms_bwd(x, r, g, dy):
    t = dy * g
    return r * t - x * (r * r * r) * jnp.mean(x * t, axis=-1, keepdims=True)


def _sigmoid(x):
    return 1.0 / (1.0 + jnp.exp(-x))


def _rope(x, c, s1, s2):
    return x * c + pltpu.roll(x, SLAB - 16, 1) * s1 + pltpu.roll(x, 16, 1) * s2


def _rope_bwd(d, c, s1, s2):
    return d * c + pltpu.roll(d * s1, 16, 1) + pltpu.roll(d * s2, SLAB - 16, 1)


def _roll_rows(x, shift):
    return jnp.concatenate([x[-shift:], x[:-shift]], axis=0)


def _rope_t(x, c, s1, s2):
    return x * c + _roll_rows(x, SLAB - 16) * s1 + _roll_rows(x, 16) * s2


def _rope_t_bwd(d, c, s1, s2):
    return d * c + _roll_rows(d * s1, 16) + _roll_rows(d * s2, SLAB - 16)


def _row_spec(tm, n):
    return pl.BlockSpec((tm, n), lambda i: (i, 0))


def _col_spec(n, tm):
    return pl.BlockSpec((n, tm), lambda i: (0, i))


def _full_spec(shape):
    nd = len(shape)
    return pl.BlockSpec(shape, lambda i: (0,) * nd, pipeline_mode=pl.Buffered(1))


def _acc_rows(ref, val):
    @pl.when(pl.program_id(0) == 0)
    def _():
        ref[...] = jnp.zeros_like(ref)
    ref[...] += jnp.sum(val, axis=0, keepdims=True)


def _rope_tables(pos_col, pos_row, freq_row):
    t = pos_col.shape[0]
    tm = _token_tile(t)

    def tables(ang, idx):
        s = jnp.sin(ang)
        return (jnp.cos(ang), jnp.where((idx >= 64) & (idx < 80), -s, 0.0),
                jnp.where((idx >= 80) & (idx < 96), s, 0.0))

    def body(pos_ref, posr_ref, f_ref, fc_ref, c_ref, s1_ref, s2_ref, ct_ref, s1t_ref, s2t_ref):
        ang = pos_ref[...].astype(F32) * f_ref[...]
        c_ref[...], s1_ref[...], s2_ref[...] = tables(ang, lax.broadcasted_iota(jnp.int32, ang.shape, 1))
        angt = posr_ref[...].astype(F32) * fc_ref[...]
        ct_ref[...], s1t_ref[...], s2t_ref[...] = tables(angt, lax.broadcasted_iota(jnp.int32, angt.shape, 0))

    tab = jax.ShapeDtypeStruct((t, SLAB), F32)
    tabt = jax.ShapeDtypeStruct((SLAB, t), F32)
    return pl.pallas_call(
        body, name="rope_tables", grid=(t // tm,),
        in_specs=[_row_spec(tm, 1), _col_spec(1, tm), _full_spec((1, SLAB)), _full_spec((SLAB, 1))],
        out_specs=[_row_spec(tm, SLAB)] * 3 + [_col_spec(SLAB, tm)] * 3, out_shape=[tab] * 3 + [tabt] * 3,
        compiler_params=_params(("parallel",)),
    )(pos_col, pos_row, freq_row, freq_row.reshape(SLAB, 1))


def _inproj_fwd(x, g1, w_in, g_q, g_kv, w_kvb, w_qb_t, w_kb_t, w_vb_t, w_kr_t, tables):
    t = x.shape[0]
    tm = _token_tile(t)

    def body(x_ref, g1_ref, win_ref, gq_ref, gkv_ref, wkvb_ref, wqbt_ref, wkbt_ref, wvbt_ref, wkrt_ref,
             c_ref, s1_ref, s2_ref, ct_ref, s1t_ref, s2t_ref,
             h_ref, gates_ref, qa_ref, ka_ref, va_ref, cq_ref, ckv_ref, cqn_ref, ckvn_ref,
             kb_ref, vb_ref, qt_ref, kt_ref, vt_ref):
        xv = x_ref[...]
        h = (xv * _rms_r(xv) * g1_ref[...]).astype(BF16)
        h_ref[...] = h
        proj = _dot(h, win_ref[...])
        gates_ref[...] = proj[:, C_GATES:C_QA]
        qa_ref[...] = proj[:, C_QA:C_KA].astype(BF16)
        ka_ref[...] = proj[:, C_KA:C_VA].astype(BF16)
        va_ref[...] = proj[:, C_VA:C_CQ].astype(BF16)
        cq = proj[:, C_CQ:C_CKV]
        ckv = proj[:, C_CKV:C_KR]
        kr = proj[:, C_KR:D_IN_PAD]
        cq_ref[...] = cq
        ckv_ref[...] = ckv
        cqn = (cq * _rms_r(cq) * gq_ref[...]).astype(BF16)
        ckvn = (ckv * _rms_r(ckv) * gkv_ref[...]).astype(BF16)
        cqn_ref[...] = cqn
        ckvn_ref[...] = ckvn
        c, s1, s2 = c_ref[...], s1_ref[...], s2_ref[...]
        kvb = _dot(ckvn, wkvb_ref[...])
        kr_rot = _rope(kr, c, s1, s2)
        ct, s1t, s2t = ct_ref[...], s1t_ref[...], s2t_ref[...]
        q_t = _dot_nt(wqbt_ref[...], cqn)
        k_t = _dot_nt(wkbt_ref[...], ckvn)
        kr_t = _rope_t(_dot_nt(wkrt_ref[...], h), ct, s1t, s2t)
        for hd in range(N_HEADS):
            sl = slice(hd * SLAB, (hd + 1) * SLAB)
            kb_ref[:, sl] = (kvb[:, sl] + kr_rot).astype(BF16)
            qt_ref[sl, :] = (_rope_t(q_t[sl, :], ct, s1t, s2t) * SCORE_B).astype(BF16)
            kt_ref[sl, :] = (k_t[sl, :] + kr_t).astype(BF16)
        vb_ref[...] = kvb[:, HM:2 * HM].astype(BF16)
        vt_ref[...] = _dot_nt(wvbt_ref[...], ckvn).astype(BF16)

    def sds(n, dt):
        return jax.ShapeDtypeStruct((t, n), dt)

    outs = [(D_MODEL, BF16), (2 * D_MODEL, F32), (HM, BF16), (N_KV_A * SLAB, BF16), (N_KV_A * SLAB, BF16),
            (Q_LORA, F32), (KV_LORA, F32), (Q_LORA, BF16), (KV_LORA, BF16), (HM, BF16), (HM, BF16)]
    tab, tabt = _row_spec(tm, SLAB), _col_spec(SLAB, tm)
    return pl.pallas_call(
        body, name="inproj_fwd", grid=(t // tm,),
        in_specs=[_row_spec(tm, D_MODEL), _full_spec((1, D_MODEL)), _full_spec((D_MODEL, D_IN_PAD)),
                  _full_spec((1, Q_LORA)), _full_spec((1, KV_LORA)), _full_spec((KV_LORA, 2 * HM)),
                  _full_spec((HM, Q_LORA)), _full_spec((HM, KV_LORA)), _full_spec((HM, KV_LORA)),
                  _full_spec((SLAB, D_MODEL)), tab, tab, tab, tabt, tabt, tabt],
        out_specs=[_row_spec(tm, n) for n, _ in outs] + [_col_spec(HM, tm)] * 3,
        out_shape=[sds(n, dt) for n, dt in outs] + [jax.ShapeDtypeStruct((HM, t), BF16)] * 3,
        compiler_params=_params(("parallel",)),
    )(x, g1, w_in, g_q, g_kv, w_kvb, w_qb_t, w_kb_t, w_vb_t, w_kr_t, *tables)


def _tile_group(a):
    return jnp.concatenate([a] * GROUP_A, axis=1)


def _swa_masks():
    row = lax.broadcasted_iota(jnp.int32, (BLOCK, GROUP_A * BLOCK), 0)
    col = lax.broadcasted_iota(jnp.int32, (BLOCK, GROUP_A * BLOCK), 1) & (BLOCK - 1)
    return row <= col, row > col


def _heads_beside(ref, g):
    return jnp.concatenate([ref[:, (g * GROUP_A + hh) * SLAB:(g * GROUP_A + hh + 1) * SLAB].T
                            for hh in range(GROUP_A)], axis=1)


def _rows_beside(ref, g):
    return jnp.concatenate([ref[g * GROUP_A + hh] for hh in range(GROUP_A)], axis=1)


def _swa_rows(sinks):
    slopes = jnp.repeat(jnp.asarray(SLOPES_A, F32).reshape(N_KV_A, GROUP_A, 1), BLOCK, axis=2)
    sink_rows = jnp.repeat(sinks.reshape(N_KV_A, GROUP_A, 1), BLOCK, axis=2)
    return slopes.reshape(N_KV_A, 1, GROUP_A * BLOCK), sink_rows.reshape(N_KV_A, 1, GROUP_A * BLOCK)


def _swa_fwd(qa, ka, va, pos_col, pos_row, sinks):
    t = qa.shape[0]
    nb = t // BLOCK
    gw = GROUP_A * BLOCK
    slope_rows, sink_rows = _swa_rows(sinks)

    def body(q_ref, kc_ref, kp_ref, vc_ref, vp_ref, pkc_ref, pkp_ref, pq_ref, slope_ref, sink_ref, o_ref, l_ref):
        i = pl.program_id(0)
        pq = pq_ref[...]
        dist_c = _tile_group(jnp.abs(pkc_ref[...] - pq).astype(F32))
        dist_p = _tile_group(jnp.abs(pkp_ref[...] - pq).astype(F32))
        mask_c, older = _swa_masks()
        mask_p = jnp.logical_and(older, i > 0)
        for g in range(N_KV_A):
            gs = slice(g * SLAB, (g + 1) * SLAB)
            x = _heads_beside(q_ref, g)
            slope, sink = slope_ref[g], sink_ref[g]
            s_c = jnp.where(mask_c, _dot(kc_ref[:, gs], x) * SCALE_A - slope * dist_c, NEG)
            s_p = jnp.where(mask_p, _dot(kp_ref[:, gs], x) * SCALE_A - slope * dist_p, NEG)
            m = jnp.maximum(jnp.maximum(jnp.max(s_c, axis=0, keepdims=True),
                                        jnp.max(s_p, axis=0, keepdims=True)), sink)
            e_c = jnp.exp(s_c - m)
            e_p = jnp.exp(s_p - m)
            den = jnp.sum(e_c, axis=0, keepdims=True) + jnp.sum(e_p, axis=0, keepdims=True) + jnp.exp(sink - m)
            inv = 1.0 / den
            ot = (_dot_tn(vc_ref[:, gs], (e_c * inv).astype(BF16))
                  + _dot_tn(vp_ref[:, gs], (e_p * inv).astype(BF16)))
            lse = m + jnp.log(den)
            for hh in range(GROUP_A):
                hd = g * GROUP_A + hh
                seg = slice(hh * BLOCK, (hh + 1) * BLOCK)
                o_ref[:, hd * SLAB:(hd + 1) * SLAB] = ot[:, seg].T.astype(BF16)
                l_ref[hd] = lse[:, seg]

    cur = lambda i: (i, 0)
    prev = lambda i: (jnp.maximum(i - 1, 0), 0)
    kvw = N_KV_A * SLAB
    rows = pl.BlockSpec((N_KV_A, 1, gw), lambda i: (0, 0, 0))
    return pl.pallas_call(
        body, name="swa_fwd", grid=(nb,),
        in_specs=[pl.BlockSpec((BLOCK, HM), cur),
                  pl.BlockSpec((BLOCK, kvw), cur), pl.BlockSpec((BLOCK, kvw), prev),
                  pl.BlockSpec((BLOCK, kvw), cur), pl.BlockSpec((BLOCK, kvw), prev),
                  pl.BlockSpec((BLOCK, 1), cur), pl.BlockSpec((BLOCK, 1), prev),
                  pl.BlockSpec((1, BLOCK), lambda i: (0, i)), rows, rows],
        out_specs=[pl.BlockSpec((BLOCK, HM), cur), pl.BlockSpec((N_HEADS, 1, BLOCK), lambda i: (0, 0, i))],
        out_shape=[jax.ShapeDtypeStruct((t, HM), BF16), jax.ShapeDtypeStruct((N_HEADS, 1, t), F32)],
        compiler_params=_params(("parallel",)),
    )(qa, ka, ka, va, va, pos_col, pos_col, pos_row, slope_rows, sink_rows)


def _swa_bwd(qa, ka, va, out_a, d_oa, lse, pos_col, pos_row, sinks):
    t = qa.shape[0]
    nb = t // BLOCK
    gw = GROUP_A * BLOCK
    slope_rows, sink_rows = _swa_rows(sinks)

    def body(q_ref, qn_ref, do_ref, don_ref, l_ref, ln_ref, o_ref, on_ref, kp_ref, kc_ref, vp_ref, vc_ref,
             pkp_ref, pkc_ref, pq_ref, pqn_ref, slope_ref, sink_ref, dq_ref, dk_ref, dv_ref, dsink_ref):
        j = pl.program_id(0)
        pkc, pkp = pkc_ref[...], pkp_ref[...]
        dist_cc = _tile_group(jnp.abs(pkc - pq_ref[...]).astype(F32))
        dist_cp = _tile_group(jnp.abs(pkp - pq_ref[...]).astype(F32))
        dist_nc = _tile_group(jnp.abs(pkc - pqn_ref[...]).astype(F32))
        mask_cc, older = _swa_masks()
        mask_cp = jnp.logical_and(older, j > 0)
        mask_nc = jnp.logical_and(older, j < nb - 1)

        @pl.when(j == 0)
        def _():
            dsink_ref[...] = jnp.zeros_like(dsink_ref)

        def tile(k, v, x, dox, lrow, drow, dist, mask, slope):
            s = jnp.where(mask, _dot(k, x) * SCALE_A - slope * dist, NEG)
            p = jnp.exp(s - lrow)
            ds = p * (_dot(v, dox) - drow)
            return p.astype(BF16), ds.astype(BF16)

        for g in range(N_KV_A):
            gs = slice(g * SLAB, (g + 1) * SLAB)
            kc, kp, vc, vp = kc_ref[:, gs], kp_ref[:, gs], vc_ref[:, gs], vp_ref[:, gs]
            slope, sink = slope_ref[g], sink_ref[g]
            x, xn = _heads_beside(q_ref, g), _heads_beside(qn_ref, g)
            dox, doxn = _heads_beside(do_ref, g), _heads_beside(don_ref, g)
            lrow, lrown = _rows_beside(l_ref, g), _rows_beside(ln_ref, g)
            drow = jnp.sum(dox.astype(F32) * _heads_beside(o_ref, g).astype(F32), axis=0, keepdims=True)
            drown = jnp.sum(doxn.astype(F32) * _heads_beside(on_ref, g).astype(F32), axis=0, keepdims=True)
            p_cc, ds_cc = tile(kc, vc, x, dox, lrow, drow, dist_cc, mask_cc, slope)
            _, ds_cp = tile(kp, vp, x, dox, lrow, drow, dist_cp, mask_cp, slope)
            p_nc, ds_nc = tile(kc, vc, xn, doxn, lrown, drown, dist_nc, mask_nc, slope)
            dqt = (_dot_tn(kc, ds_cc) + _dot_tn(kp, ds_cp)) * SCALE_A
            for hh in range(GROUP_A):
                hd = g * GROUP_A + hh
                dq_ref[:, hd * SLAB:(hd + 1) * SLAB] = dqt[:, hh * BLOCK:(hh + 1) * BLOCK].T.astype(BF16)
            dk_ref[:, gs] = ((_dot_nt(ds_cc, x) + _dot_nt(ds_nc, xn)) * SCALE_A).astype(BF16)
            dv_ref[:, gs] = (_dot_nt(p_cc, dox) + _dot_nt(p_nc, doxn)).astype(BF16)
            dsink_ref[g] -= jnp.exp(sink - lrow) * drow

    cur = lambda j: (j, 0)
    prev = lambda j: (jnp.maximum(j - 1, 0), 0)
    nxt = lambda j: (jnp.minimum(j + 1, nb - 1), 0)
    cur3 = lambda j: (0, 0, j)
    nxt3 = lambda j: (0, 0, jnp.minimum(j + 1, nb - 1))
    kvw = N_KV_A * SLAB
    rows = pl.BlockSpec((N_KV_A, 1, gw), lambda j: (0, 0, 0))
    stat = lambda im: pl.BlockSpec((N_HEADS, 1, BLOCK), im)
    return pl.pallas_call(
        body, name="swa_bwd", grid=(nb,),
        in_specs=[pl.BlockSpec((BLOCK, HM), cur), pl.BlockSpec((BLOCK, HM), nxt),
                  pl.BlockSpec((BLOCK, HM), cur), pl.BlockSpec((BLOCK, HM), nxt),
                  stat(cur3), stat(nxt3), pl.BlockSpec((BLOCK, HM), cur), pl.BlockSpec((BLOCK, HM), nxt),
                  pl.BlockSpec((BLOCK, kvw), prev), pl.BlockSpec((BLOCK, kvw), cur),
                  pl.BlockSpec((BLOCK, kvw), prev), pl.BlockSpec((BLOCK, kvw), cur),
                  pl.BlockSpec((BLOCK, 1), prev), pl.BlockSpec((BLOCK, 1), cur),
                  pl.BlockSpec((1, BLOCK), lambda j: (0, j)),
                  pl.BlockSpec((1, BLOCK), lambda j: (0, jnp.minimum(j + 1, nb - 1))), rows, rows],
        out_specs=[pl.BlockSpec((BLOCK, HM), cur), pl.BlockSpec((BLOCK, kvw), cur),
                   pl.BlockSpec((BLOCK, kvw), cur), rows],
        out_shape=[jax.ShapeDtypeStruct((t, HM), BF16), jax.ShapeDtypeStruct((t, kvw), BF16),
                   jax.ShapeDtypeStruct((t, kvw), BF16), jax.ShapeDtypeStruct((N_KV_A, 1, gw), F32)],
        compiler_params=_params(("arbitrary",)),
    )(qa, qa, d_oa, d_oa, lse, lse, out_a, out_a, ka, ka, va, va,
      pos_col, pos_col, pos_row, pos_row, slope_rows, sink_rows)


def _lower_triangle(n):
    row = lax.broadcasted_iota(jnp.int32, (n, n), 0)
    col = lax.broadcasted_iota(jnp.int32, (n, n), 1)
    return row >= col


def _upper_triangle(n):
    row = lax.broadcasted_iota(jnp.int32, (n, n), 0)
    col = lax.broadcasted_iota(jnp.int32, (n, n), 1)
    return row <= col


def _mesh_pos():
    return lax.axis_index("x"), lax.axis_index("y"), lax.axis_index("c")


def _flip(v, bit):
    return 1 - v if bit else v


def _direct_copies(srcs, dsts, send_sems, recv_sems, local_sems, gather, sem_base=0):
    x, y, c = _mesh_pos()
    me = 4 * x + 2 * y + c
    local, remote = [], []
    for a, (src, dst) in enumerate(zip(srcs, dsts)):
        local.append(pltpu.make_async_copy(src if gather else src.at[me], dst.at[me], local_sems.at[sem_base + a]))
        for r in range(1, N_DEV):
            px, py, pc = _flip(x, r & 4), _flip(y, r & 2), _flip(c, r & 1)
            sem = (N_DEV - 1) * (sem_base + a) + r - 1
            remote.append(pltpu.make_async_remote_copy(
                src_ref=src if gather else src.at[4 * px + 2 * py + pc], dst_ref=dst.at[me],
                send_sem=send_sems.at[sem], recv_sem=recv_sems.at[sem],
                device_id=(px, py, pc), device_id_type=pl.DeviceIdType.MESH))
    return local, remote


def _start_copies(local, remote):
    for cp in local + remote:
        cp.start()


def _wait_copies(local, remote):
    for cp in remote:
        cp.wait_recv()
    for cp in remote:
        cp.wait_send()
    for cp in local:
        cp.wait()


def _exchange_scratch(n):
    return [pltpu.SemaphoreType.DMA((n * (N_DEV - 1),)), pltpu.SemaphoreType.DMA((n * (N_DEV - 1),)),
            pltpu.SemaphoreType.DMA((n,))]


ANY_SPEC = pl.BlockSpec(memory_space=pl.ANY)


def _mla_fwd(qt, kb, vt, late):
    t = kb.shape[0]
    tq = _attn_tile(t)
    nt = t // tq
    hps = MLA_HEADS_PER_STEP
    w = hps * SLAB
    pairs = [(i, j) for i in range(nt) for j in range(i + 1)]
    i_tab = jnp.asarray(np.array([p[0] for p in pairs], np.int32))
    j_tab = jnp.asarray(np.array([p[1] for p in pairs], np.int32))

    n_late = len(late)

    def body(it_ref, jt_ref, qt_ref, k_ref, vt_ref, *rest):
        late_refs, (o_ref, ot_ref, l_ref) = rest[:n_late], rest[n_late:n_late + 3]
        gathered_refs = rest[n_late + 3:2 * n_late + 3]
        m_s, l_s, acc_s, send_sems, recv_sems, local_sems = rest[2 * n_late + 3:]
        n = pl.program_id(1)
        i, j = it_ref[n], jt_ref[n]
        first_step = jnp.logical_and(pl.program_id(0) == 0, n == 0)
        last_step = jnp.logical_and(pl.program_id(0) == N_HEADS // hps - 1, n == len(pairs) - 1)

        @pl.when(first_step)
        def _():
            _start_copies(*_direct_copies(late_refs, gathered_refs, send_sems, recv_sems, local_sems, True))

        @pl.when(j == 0)
        def _():
            m_s[...] = jnp.full_like(m_s, NEG)
            l_s[...] = jnp.zeros_like(l_s)
            acc_s[...] = jnp.zeros_like(acc_s)

        def update(masked):
            def scores(hh):
                sl = slice(hh * SLAB, (hh + 1) * SLAB)
                return _dot(k_ref[:, sl], qt_ref[sl, :])

            def softmax(hh, s):
                if masked:
                    s = jnp.where(_upper_triangle(tq), s, NEG)
                m_old = m_s[hh]
                m_new = jnp.maximum(m_old, jnp.max(s, axis=0, keepdims=True))
                alpha = jnp.exp2(m_old - m_new)
                p = jnp.exp2(s - m_new)
                l_s[hh] = alpha * l_s[hh] + jnp.sum(p, axis=0, keepdims=True)
                m_s[hh] = m_new
                return p.astype(BF16), alpha

            def accumulate(hh, p, alpha):
                sl = slice(hh * SLAB, hh * SLAB + V_DIM_B)
                acc_s[sl, :] = alpha * acc_s[sl, :] + _dot(vt_ref[sl, :], p)

            s_next, pending = scores(0), None
            for hh in range(hps):
                s = s_next
                if hh + 1 < hps:
                    s_next = scores(hh + 1)
                p, alpha = softmax(hh, s)
                if pending is not None:
                    accumulate(*pending)
                pending = (hh, p, alpha)
            accumulate(*pending)

        @pl.when(j < i)
        def _():
            update(False)

        @pl.when(j == i)
        def _():
            update(True)
            for hh in range(hps):
                sl = slice(hh * SLAB, (hh + 1) * SLAB)
                ot = acc_s[sl, :] / l_s[hh]
                ot_ref[sl, :] = ot.astype(BF16)
                o_ref[:, sl] = ot.T.astype(BF16)
                l_ref[hh] = m_s[hh] + jnp.log2(l_s[hh])

        @pl.when(last_step)
        def _():
            _wait_copies(*_direct_copies(late_refs, gathered_refs, send_sems, recv_sems, local_sems, True))

    grid_spec = pltpu.PrefetchScalarGridSpec(
        num_scalar_prefetch=2, grid=(N_HEADS // hps, len(pairs)),
        in_specs=[pl.BlockSpec((w, tq), lambda h, n, it, jt: (h, it[n])),
                  pl.BlockSpec((tq, w), lambda h, n, it, jt: (jt[n], h)),
                  pl.BlockSpec((w, tq), lambda h, n, it, jt: (h, jt[n]))] + [ANY_SPEC] * n_late,
        out_specs=[pl.BlockSpec((tq, w), lambda h, n, it, jt: (it[n], h)),
                   pl.BlockSpec((w, tq), lambda h, n, it, jt: (h, it[n])),
                   pl.BlockSpec((hps, 1, tq), lambda h, n, it, jt: (h, 0, it[n]))] + [ANY_SPEC] * n_late,
        scratch_shapes=[pltpu.VMEM((hps, 1, tq), F32), pltpu.VMEM((hps, 1, tq), F32), pltpu.VMEM((w, tq), F32)]
        + _exchange_scratch(n_late))
    outs = pl.pallas_call(
        body, name="mla_fwd", grid_spec=grid_spec,
        out_shape=[jax.ShapeDtypeStruct((t, HM), BF16), jax.ShapeDtypeStruct((HM, t), BF16),
                   jax.ShapeDtypeStruct((N_HEADS, 1, t), F32)]
        + [jax.ShapeDtypeStruct((N_DEV,) + a.shape, a.dtype) for a in late],
        compiler_params=_params(("arbitrary", "arbitrary")),
    )(i_tab, j_tab, qt, kb, vt, *late)
    return outs[0], outs[1], outs[2], list(outs[3:])


def _mla_bwd(qt, kb, kt, vb, d_ob_t, lse, delta, grad_slices):
    t = kb.shape[0]
    tq = _attn_tile(t)
    nt = t // tq
    hps = MLA_HEADS_PER_STEP
    w = hps * SLAB
    pairs = [(j, i) for j in range(nt) for i in range(j, nt)]
    j_tab = jnp.asarray(np.array([p[0] for p in pairs], np.int32))
    i_tab = jnp.asarray(np.array([p[1] for p in pairs], np.int32))

    n_ex = len(grad_slices)

    def body(jt_ref, it_ref, qt_ref, dot_ref, l_ref, dl_ref, k_ref, kt_ref, v_ref, *rest):
        slice_refs, (dqt_ref, dkt_ref, dvt_ref) = rest[:n_ex], rest[n_ex:n_ex + 3]
        part_refs = rest[n_ex + 3:2 * n_ex + 3]
        dk_s, dv_s, send_sems, recv_sems, local_sems = rest[2 * n_ex + 3:]
        n = pl.program_id(1)
        j, i = jt_ref[n], it_ref[n]
        first_step = jnp.logical_and(pl.program_id(0) == 0, n == 0)
        last_step = jnp.logical_and(pl.program_id(0) == N_HEADS // hps - 1, n == len(pairs) - 1)

        @pl.when(first_step)
        def _():
            _start_copies(*_direct_copies(slice_refs, part_refs, send_sems, recv_sems, local_sems, False))

        @pl.when(n == 0)
        def _():
            dqt_ref[...] = jnp.zeros_like(dqt_ref)

        def update(diagonal):
            cols = pl.ds(pl.multiple_of(i * tq, tq), tq)

            def products(hh):
                sl = slice(hh * SLAB, (hh + 1) * SLAB)
                return _dot(k_ref[:, sl], qt_ref[sl, :]), _dot(v_ref[:, sl], dot_ref[sl, :])

            def softmax_bwd(hh, s, dp):
                if diagonal:
                    s = jnp.where(_upper_triangle(tq), s, NEG)
                p = jnp.exp2(s - l_ref[hh])
                return p.astype(BF16), (p * (dp - dl_ref[hh])).astype(BF16)

            def gradients(hh, p, ds):
                base = hh * SLAB
                vrows = slice(base, base + V_DIM_B)
                qrows = slice(base, base + QK_NOPE + QK_ROPE)
                dv = _dot_nt(dot_ref[vrows, :], p)
                dk = _dot_nt(qt_ref[qrows, :], ds)
                if diagonal:
                    dv_s[base:base + SLAB, :] = jnp.concatenate([dv, jnp.zeros((SLAB - V_DIM_B, tq), F32)], axis=0)
                    dk_s[base:base + SLAB, :] = jnp.concatenate(
                        [dk, jnp.zeros((SLAB - QK_NOPE - QK_ROPE, tq), F32)], axis=0)
                else:
                    dv_s[vrows, :] += dv
                    dk_s[qrows, :] += dk
                dqt_ref[qrows, cols] += _dot(kt_ref[qrows, :], ds)

            for hh in range(hps):
                gradients(hh, *softmax_bwd(hh, *products(hh)))

        @pl.when(i == j)
        def _():
            update(True)

        @pl.when(i > j)
        def _():
            update(False)

        @pl.when(i == nt - 1)
        def _():
            dkt_ref[...] = (dk_s[...] * (1.0 / LOG2E)).astype(BF16)
            dvt_ref[...] = dv_s[...].astype(BF16)

        @pl.when(last_step)
        def _():
            _wait_copies(*_direct_copies(slice_refs, part_refs, send_sems, recv_sems, local_sems, False))

    grid_spec = pltpu.PrefetchScalarGridSpec(
        num_scalar_prefetch=2, grid=(N_HEADS // hps, len(pairs)),
        in_specs=[pl.BlockSpec((w, tq), lambda h, n, jt, it: (h, it[n])),
                  pl.BlockSpec((w, tq), lambda h, n, jt, it: (h, it[n])),
                  pl.BlockSpec((hps, 1, tq), lambda h, n, jt, it: (h, 0, it[n])),
                  pl.BlockSpec((hps, 1, tq), lambda h, n, jt, it: (h, 0, it[n])),
                  pl.BlockSpec((tq, w), lambda h, n, jt, it: (jt[n], h)),
                  pl.BlockSpec((w, tq), lambda h, n, jt, it: (h, jt[n])),
                  pl.BlockSpec((tq, w), lambda h, n, jt, it: (jt[n], h))] + [ANY_SPEC] * n_ex,
        out_specs=[pl.BlockSpec((w, t), lambda h, n, jt, it: (h, 0)),
                   pl.BlockSpec((w, tq), lambda h, n, jt, it: (h, jt[n])),
                   pl.BlockSpec((w, tq), lambda h, n, jt, it: (h, jt[n]))] + [ANY_SPEC] * n_ex,
        scratch_shapes=[pltpu.VMEM((w, tq), F32), pltpu.VMEM((w, tq), F32)] + _exchange_scratch(n_ex))
    outs = pl.pallas_call(
        body, name="mla_bwd", grid_spec=grid_spec,
        out_shape=[jax.ShapeDtypeStruct((HM, t), F32), jax.ShapeDtypeStruct((HM, t), BF16),
                   jax.ShapeDtypeStruct((HM, t), BF16)]
        + [jax.ShapeDtypeStruct(a.shape, a.dtype) for a in grad_slices],
        compiler_params=_params(("arbitrary", "arbitrary")),
    )(j_tab, i_tab, qt, d_ob_t, lse, delta, kb, kt, vb, *grad_slices)
    return outs[0], outs[1], outs[2], list(outs[3:])


def _merge_fwd(out_a, out_b, gates, x, w_oa, w_ob, w_out, g2, g3):
    t = x.shape[0]
    tm = _token_tile(t)

    def body(oa_ref, ob_ref, gates_ref, x_ref, woa_ref, wob_ref, wout_ref, g2_ref, g3_ref,
             oap_ref, obp_ref, merged_ref, y_ref, x1_ref, h2_ref):
        oa_p = _dot(oa_ref[...], woa_ref[...])
        ob_p = _dot(ob_ref[...], wob_ref[...])
        oap_ref[...] = oa_p.astype(BF16)
        obp_ref[...] = ob_p.astype(BF16)
        sa = _sigmoid(gates_ref[:, 0:D_MODEL])
        sb = _sigmoid(gates_ref[:, D_MODEL:2 * D_MODEL])
        merged = (sa * oa_p + sb * ob_p).astype(BF16)
        merged_ref[...] = merged
        y = _dot(merged, wout_ref[...])
        y_ref[...] = y
        x1 = x_ref[...] + y * _rms_r(y) * g2_ref[...]
        x1_ref[...] = x1
        h2_ref[...] = (x1 * _rms_r(x1) * g3_ref[...]).astype(BF16)

    def sds(dt):
        return jax.ShapeDtypeStruct((t, D_MODEL), dt)

    row = _row_spec(tm, D_MODEL)
    return pl.pallas_call(
        body, name="merge_fwd", grid=(t // tm,),
        in_specs=[_row_spec(tm, HM), _row_spec(tm, HM), _row_spec(tm, 2 * D_MODEL), row,
                  _full_spec((HM, D_MODEL)), _full_spec((HM, D_MODEL)), _full_spec((D_MODEL, D_MODEL)),
                  _full_spec((1, D_MODEL)), _full_spec((1, D_MODEL))],
        out_specs=[row] * 6,
        out_shape=[sds(BF16), sds(BF16), sds(BF16), sds(F32), sds(F32), sds(BF16)],
        compiler_params=_params(("parallel",)),
    )(out_a, out_b, gates, x, w_oa, w_ob, w_out, g2, g3)


def _merge_bwd(dx1, y, gates, oa_p, ob_p, out_b_t, w_oa, w_ob, w_out, g2):
    t = dx1.shape[0]
    tm = _token_tile(t)

    def body(dx1_ref, y_ref, gates_ref, oap_ref, obp_ref, obt_ref, woa_ref, wob_ref, wout_ref, g2_ref,
             dy_ref, doap_ref, dobp_ref, dgates_ref, doa_ref, dobt_ref, dlb_ref, dg2_ref):
        dx1v = dx1_ref[...]
        yv = y_ref[...]
        r2 = _rms_r(yv)
        _acc_rows(dg2_ref, dx1v * yv * r2)
        dy = _rms_bwd(yv, r2, g2_ref[...], dx1v).astype(BF16)
        dy_ref[...] = dy
        dm = _dot_nt(dy, wout_ref[...])
        sa = _sigmoid(gates_ref[:, 0:D_MODEL])
        sb = _sigmoid(gates_ref[:, D_MODEL:2 * D_MODEL])
        d_oap = (dm * sa).astype(BF16)
        d_obp = (dm * sb).astype(BF16)
        doap_ref[...] = d_oap
        dobp_ref[...] = d_obp
        dgates_ref[:, 0:D_MODEL] = (dm * oap_ref[...].astype(F32) * sa * (1.0 - sa)).astype(BF16)
        dgates_ref[:, D_MODEL:2 * D_MODEL] = (dm * obp_ref[...].astype(F32) * sb * (1.0 - sb)).astype(BF16)
        doa_ref[...] = _dot_nt(d_oap, woa_ref[...]).astype(BF16)
        d_ob_t = _dot_nt(wob_ref[...], d_obp)
        dobt_ref[...] = d_ob_t.astype(BF16)
        for hd in range(N_HEADS):
            sl = slice(hd * SLAB, (hd + 1) * SLAB)
            dlb_ref[hd] = jnp.sum(d_ob_t[sl, :] * obt_ref[sl, :].astype(F32), axis=0, keepdims=True)

    def sds(n, dt):
        return jax.ShapeDtypeStruct((t, n), dt)

    row = _row_spec(tm, D_MODEL)
    head3 = pl.BlockSpec((N_HEADS, 1, tm), lambda i: (0, 0, i))
    return pl.pallas_call(
        body, name="merge_bwd", grid=(t // tm,),
        in_specs=[row, row, _row_spec(tm, 2 * D_MODEL), row, row, _col_spec(HM, tm),
                  _full_spec((HM, D_MODEL)), _full_spec((HM, D_MODEL)), _full_spec((D_MODEL, D_MODEL)),
                  _full_spec((1, D_MODEL))],
        out_specs=[row, row, row, _row_spec(tm, 2 * D_MODEL), _row_spec(tm, HM), _col_spec(HM, tm),
                   head3, _full_spec((1, D_MODEL))],
        out_shape=[sds(D_MODEL, BF16), sds(D_MODEL, BF16), sds(D_MODEL, BF16), sds(2 * D_MODEL, BF16),
                   sds(HM, BF16), jax.ShapeDtypeStruct((HM, t), BF16),
                   jax.ShapeDtypeStruct((N_HEADS, 1, t), F32), jax.ShapeDtypeStruct((1, D_MODEL), F32)],
        compiler_params=_params(("arbitrary",)),
    )(dx1, y, gates, oa_p, ob_p, out_b_t, w_oa, w_ob, w_out, g2)


def _mlp_fwd_bwd(x1, h2, target, w_up, w_down, g3, g4):
    t = x1.shape[0]
    tm = _token_tile(t)
    fs = D_FF // N_DEV

    def body(x1_ref, h2_ref, tgt_ref, wup_ref, wdown_ref, g3_ref, g4_ref,
             a_ref, du_ref, dy2_ref, dx1_ref, loss_ref, dg3_ref, dg4_ref):
        x1v = x1_ref[...]
        h2v = h2_ref[...]
        u = jnp.concatenate([_dot(h2v, wup_ref[s]) for s in range(N_DEV)], axis=1)
        ru = jnp.maximum(u, 0.0)
        a = (ru * ru).astype(BF16)
        a_ref[...] = a
        y2 = _dot(a, wdown_ref[...])
        r4 = _rms_r(y2)
        diff = x1v + y2 * r4 * g4_ref[...] - tgt_ref[...]
        _acc_rows(loss_ref, jnp.sum(diff * diff, axis=-1, keepdims=True) * (0.5 / D_MODEL)
                  * jnp.ones((1, SLAB), F32))
        dx2 = diff * (1.0 / D_MODEL)
        _acc_rows(dg4_ref, dx2 * y2 * r4)
        dy2 = _rms_bwd(y2, r4, g4_ref[...], dx2).astype(BF16)
        dy2_ref[...] = dy2
        du = (_dot_nt(dy2, wdown_ref[...]) * (2.0 * ru)).astype(BF16)
        du_ref[...] = du
        dh2 = _dot_nt(du[:, 0:fs], wup_ref[0])
        for s in range(1, N_DEV):
            dh2 += _dot_nt(du[:, s * fs:(s + 1) * fs], wup_ref[s])
        r3 = _rms_r(x1v)
        _acc_rows(dg3_ref, dh2 * x1v * r3)
        dx1_ref[...] = dx2 + _rms_bwd(x1v, r3, g3_ref[...], dh2)

    row = _row_spec(tm, D_MODEL)
    frow = _row_spec(tm, D_FF)
    vec = _full_spec((1, D_MODEL))
    return pl.pallas_call(
        body, name="mlp_fwd_bwd", grid=(t // tm,),
        in_specs=[row, row, row, _full_spec((N_DEV, D_MODEL, fs)), _full_spec((D_FF, D_MODEL)), vec, vec],
        out_specs=[frow, frow, row, row, _full_spec((1, SLAB)), vec, vec],
        out_shape=[jax.ShapeDtypeStruct((t, D_FF), BF16), jax.ShapeDtypeStruct((t, D_FF), BF16),
                   jax.ShapeDtypeStruct((t, D_MODEL), BF16), jax.ShapeDtypeStruct((t, D_MODEL), F32),
                   jax.ShapeDtypeStruct((1, SLAB), F32), jax.ShapeDtypeStruct((1, D_MODEL), F32),
                   jax.ShapeDtypeStruct((1, D_MODEL), F32)],
        compiler_params=_params(("arbitrary",)),
    )(x1, h2, target, w_up, w_down, g3, g4)


def _inproj_bwd(dgates, dqa, dka, dva, dqb_t, dkb_t, dvb_t, cq, ckv, x, dx1, rope_ct, rope_s1t, rope_s2t,
                g1, g_q, g_kv, w_in, w_qb, w_kvb):
    t = x.shape[0]
    tm = _token_tile(t)

    def body(dgates_ref, dqa_ref, dka_ref, dva_ref, dqt_ref, dkt_ref, dvt_ref, cq_ref, ckv_ref, x_ref, dx1_ref,
             ct_ref, s1t_ref, s2t_ref, g1_ref, gq_ref, gkv_ref, win_ref, wqb_ref, wkvb_ref,
             dproj_ref, dqbrt_ref, dkvbt_ref, dx_ref, dg1_ref, dgq_ref, dgkv_ref):
        ct, s1t, s2t = ct_ref[...], s1t_ref[...], s2t_ref[...]
        dk_sum_t = jnp.zeros((SLAB, tm), F32)
        for hd in range(N_HEADS):
            sl = slice(hd * SLAB, (hd + 1) * SLAB)
            dqbrt_ref[sl, :] = _rope_t_bwd(dqt_ref[sl, :] * SCALE_B, ct, s1t, s2t).astype(BF16)
            dk_sum_t += dkt_ref[sl, :].astype(F32)
        dkvbt_ref[0:HM, :] = dkt_ref[...]
        dkvbt_ref[HM:2 * HM, :] = dvt_ref[...]
        dkr = _rope_t_bwd(dk_sum_t, ct, s1t, s2t).T
        dcqn = _dot(wqb_ref[...], dqbrt_ref[...]).T
        cq = cq_ref[...]
        rq = _rms_r(cq)
        _acc_rows(dgq_ref, dcqn * cq * rq)
        dcq = _rms_bwd(cq, rq, gq_ref[...], dcqn)
        dckvn = _dot(wkvb_ref[...], dkvbt_ref[...]).T
        ckv = ckv_ref[...]
        rkv = _rms_r(ckv)
        _acc_rows(dgkv_ref, dckvn * ckv * rkv)
        dckv = _rms_bwd(ckv, rkv, gkv_ref[...], dckvn)
        dproj_ref[:, C_GATES:C_QA] = dgates_ref[...]
        dproj_ref[:, C_QA:C_KA] = dqa_ref[...]
        dproj_ref[:, C_KA:C_VA] = dka_ref[...]
        dproj_ref[:, C_VA:C_CQ] = dva_ref[...]
        dproj_ref[:, C_CQ:C_CKV] = dcq.astype(BF16)
        dproj_ref[:, C_CKV:C_KR] = dckv.astype(BF16)
        dproj_ref[:, C_KR:D_IN_PAD] = dkr.astype(BF16)
        dh = _dot_nt(dproj_ref[...], win_ref[...])
        xv = x_ref[...]
        r1 = _rms_r(xv)
        _acc_rows(dg1_ref, dh * xv * r1)
        dx_ref[...] = dx1_ref[...] + _rms_bwd(xv, r1, g1_ref[...], dh)

    kvw = N_KV_A * SLAB
    row = _row_spec(tm, D_MODEL)
    hm = _row_spec(tm, HM)
    hmt = _col_spec(HM, tm)
    tab = _col_spec(SLAB, tm)
    return pl.pallas_call(
        body, name="inproj_bwd", grid=(t // tm,),
        in_specs=[_row_spec(tm, 2 * D_MODEL), hm, _row_spec(tm, kvw), _row_spec(tm, kvw), hmt, hmt, hmt,
                  _row_spec(tm, Q_LORA), _row_spec(tm, KV_LORA), row, row, tab, tab, tab,
                  _full_spec((1, D_MODEL)), _full_spec((1, Q_LORA)), _full_spec((1, KV_LORA)),
                  _full_spec((D_MODEL, D_IN_PAD)), _full_spec((Q_LORA, HM)), _full_spec((KV_LORA, 2 * HM))],
        out_specs=[_row_spec(tm, D_IN_PAD), hmt, _col_spec(2 * HM, tm), row,
                   _full_spec((1, D_MODEL)), _full_spec((1, Q_LORA)), _full_spec((1, KV_LORA))],
        out_shape=[jax.ShapeDtypeStruct((t, D_IN_PAD), BF16), jax.ShapeDtypeStruct((HM, t), BF16),
                   jax.ShapeDtypeStruct((2 * HM, t), BF16), jax.ShapeDtypeStruct((t, D_MODEL), F32),
                   jax.ShapeDtypeStruct((1, D_MODEL), F32), jax.ShapeDtypeStruct((1, Q_LORA), F32),
                   jax.ShapeDtypeStruct((1, KV_LORA), F32)],
        compiler_params=_params(("arbitrary",)),
    )(dgates, dqa, dka, dva, dqb_t, dkb_t, dvb_t, cq, ckv, x, dx1, rope_ct, rope_s1t, rope_s2t,
      g1, g_q, g_kv, w_in, w_qb, w_kvb)


def _matmul_tn(a, b, name, out_dtype=F32, n_shards=1):
    t, k = a.shape
    n = b.shape[1]
    bt = min(t, 512)
    bk = min(k, 1024)
    bn = min(n, 1024)
    ns = n // n_shards
    per_block = bn // ns
    steps = t // bt

    def body(a_ref, b_ref, o_ref, acc):
        s = pl.program_id(2)

        @pl.when(s == 0)
        def _():
            acc[...] = jnp.zeros_like(acc)

        acc[...] += _dot_tn(a_ref[...], b_ref[...])

        @pl.when(s == steps - 1)
        def _():
            if n_shards > 1:
                for p in range(per_block):
                    o_ref[p] = acc[:, p * ns:(p + 1) * ns].astype(out_dtype)
            else:
                o_ref[...] = acc[...].astype(out_dtype)

    if n_shards > 1:
        out_spec = pl.BlockSpec((per_block, bk, ns), lambda i, j, s: (j, i, 0))
        out_shape = jax.ShapeDtypeStruct((n_shards, k, ns), out_dtype)
    else:
        out_spec = pl.BlockSpec((bk, bn), lambda i, j, s: (i, j))
        out_shape = jax.ShapeDtypeStruct((k, n), out_dtype)
    return pl.pallas_call(
        body, name=name, grid=(k // bk, n // bn, steps),
        in_specs=[pl.BlockSpec((bt, bk), lambda i, j, s: (s, i)), pl.BlockSpec((bt, bn), lambda i, j, s: (s, j))],
        out_specs=out_spec, out_shape=out_shape, scratch_shapes=[pltpu.VMEM((bk, bn), F32)],
        compiler_params=_params(("parallel", "parallel", "arbitrary")),
    )(a, b)


def _matmul_nn(a_t, b, name):
    m, t = a_t.shape
    n = b.shape[1]
    bt = min(t, 512)
    steps = t // bt

    def body(a_ref, b_ref, o_ref, acc):
        s = pl.program_id(0)

        @pl.when(s == 0)
        def _():
            acc[...] = jnp.zeros_like(acc)

        acc[...] += _dot(a_ref[...], b_ref[...])

        @pl.when(s == steps - 1)
        def _():
            o_ref[...] = acc[...]

    return pl.pallas_call(
        body, name=name, grid=(steps,),
        in_specs=[pl.BlockSpec((m, bt), lambda s: (0, s)), pl.BlockSpec((bt, n), lambda s: (s, 0))],
        out_specs=pl.BlockSpec((m, n), lambda s: (0, 0)), out_shape=jax.ShapeDtypeStruct((m, n), F32),
        scratch_shapes=[pltpu.VMEM((m, n), F32)],
        compiler_params=_params(("arbitrary",)),
    )(a_t, b)


def _all_gather(shards):
    n = len(shards)

    def body(*refs):
        srcs, dsts = refs[:n], refs[n:2 * n]
        send_sems, recv_sems, local_sems = refs[2 * n:]
        x, y, c = _mesh_pos()
        me, sibling = (x, y, c), (x, y, 1 - c)
        chips = [(1 - x, y), (x, 1 - y), (1 - x, 1 - y)]

        def slot(a, px, py, pc):
            return dsts[a].at[4 * px + 2 * py + pc]

        def copy(a, k, block, to, src=None):
            return pltpu.make_async_remote_copy(
                src_ref=slot(a, *block) if src is None else src, dst_ref=slot(a, *block),
                send_sem=send_sems.at[(N_DEV - 1) * a + k], recv_sem=recv_sems.at[(N_DEV - 1) * a + k],
                device_id=to, device_id_type=pl.DeviceIdType.MESH)

        mine = [pltpu.make_async_copy(srcs[a], slot(a, *me), local_sems.at[a]) for a in range(n)]
        first = []
        for a in range(n):
            first.append(copy(a, 0, me, sibling, src=srcs[a]))
            first += [copy(a, 1 + j, me, (*chip, c), src=srcs[a]) for j, chip in enumerate(chips)]
        for cp in mine + first:
            cp.start()
        passed = []
        for j, chip in enumerate(chips):
            for a in range(n):
                copy(a, 1 + j, (*chip, c), me).wait_recv()
                passed.append(copy(a, 4 + j, (*chip, c), sibling))
                passed[-1].start()
        for a in range(n):
            copy(a, 0, sibling, me).wait_recv()
        for j, chip in enumerate(chips):
            for a in range(n):
                copy(a, 4 + j, (*chip, 1 - c), me).wait_recv()
        for cp in first + passed:
            cp.wait_send()
        for cp in mine:
            cp.wait()

    return pl.pallas_call(
        body, name="all_gather_early",
        out_shape=[jax.ShapeDtypeStruct((N_DEV,) + a.shape, a.dtype) for a in shards],
        in_specs=[ANY_SPEC] * n, out_specs=[ANY_SPEC] * n, scratch_shapes=_exchange_scratch(n),
    )(*shards)


def _exchange_grads(slices, small):
    n = len(slices)

    def body(*refs):
        srcs, s_ref = refs[:n], refs[n]
        dsts, s_dst = refs[n + 1:2 * n + 1], refs[2 * n + 1]
        sems = refs[2 * n + 2:]
        parts = _direct_copies(srcs, dsts, *sems, False)
        smalls = _direct_copies([s_ref], [s_dst], *sems, True, sem_base=n)
        _start_copies(*parts)
        _start_copies(*smalls)
        _wait_copies(*parts)
        _wait_copies(*smalls)

    outs = pl.pallas_call(
        body, name="exchange_grads",
        out_shape=[jax.ShapeDtypeStruct(a.shape, a.dtype) for a in slices]
        + [jax.ShapeDtypeStruct((N_DEV,) + small.shape, small.dtype)],
        in_specs=[ANY_SPEC] * (n + 1), out_specs=[ANY_SPEC] * (n + 1), scratch_shapes=_exchange_scratch(n + 1),
    )(*slices, small)
    return list(outs[:n]), outs[n]


def _adamw(parts, w, m, v, name):
    _, k, n = parts.shape
    bk = min(k, ADAM_ROWS)
    c1 = 1.0 - ADAM_B1 ** ADAM_STEP
    c2 = 1.0 - ADAM_B2 ** ADAM_STEP

    def body(p_ref, w_ref, m_ref, v_ref, g_ref, d_ref, mo_ref, vo_ref):
        g = p_ref[0].astype(F32)
        for s in range(1, N_DEV):
            g = g + p_ref[s].astype(F32)
        g_ref[0] = g
        m_new = ADAM_B1 * m_ref[0] + (1.0 - ADAM_B1) * g
        v_new = ADAM_B2 * v_ref[0] + (1.0 - ADAM_B2) * (g * g)
        mo_ref[0] = m_new
        vo_ref[0] = v_new
        m_hat = m_new / c1
        v_hat = v_new / c2
        d_ref[0] = -ADAM_LR * (m_hat / (jnp.sqrt(v_hat) + ADAM_EPS) + ADAM_WD * w_ref[0])

    blk = pl.BlockSpec((1, bk, n), lambda i: (0, i, 0))
    out = jax.ShapeDtypeStruct((1, k, n), F32)
    return pl.pallas_call(
        body, name=name, grid=(k // bk,),
        in_specs=[pl.BlockSpec((N_DEV, bk, n), lambda i: (0, i, 0)), blk, blk, blk],
        out_specs=[blk] * 4, out_shape=[out] * 4,
        compiler_params=_params(("parallel",)),
    )(parts, w, m, v)


def _pad_heads_cols(w, heads, width):
    k = w.shape[0]
    w = w.reshape(k, heads, width)
    return jnp.pad(w, ((0, 0), (0, 0), (0, SLAB - width))).reshape(k, heads * SLAB)


def _unpad_heads_cols(w, heads, width):
    k = w.shape[0]
    return w.reshape(k, heads, SLAB)[:, :, :width].reshape(k, heads * width)


def _pad_heads_rows(w, heads, width):
    n = w.shape[1]
    w = w.reshape(heads, width, n)
    return jnp.pad(w, ((0, 0), (0, SLAB - width), (0, 0))).reshape(heads * SLAB, n)


def _unpad_heads_rows(w, heads, width):
    n = w.shape[1]
    return w.reshape(heads, SLAB, n)[:, :width, :].reshape(heads * width, n)


def _pad_w_in(w_in):
    o = 2 * D_MODEL
    qa = _pad_heads_cols(w_in[:, o:o + 512], N_HEADS, HEAD_A)
    ka = _pad_heads_cols(w_in[:, o + 512:o + 640], N_KV_A, HEAD_A)
    va = _pad_heads_cols(w_in[:, o + 640:o + 768], N_KV_A, HEAD_A)
    kr = jnp.pad(w_in[:, o + 1152:o + 1184], ((0, 0), (QK_NOPE, SLAB - QK_NOPE - QK_ROPE)))
    return jnp.concatenate([w_in[:, :o], qa, ka, va, w_in[:, o + 768:o + 1152], kr], axis=1)


def _unpad_w_in(w):
    qa = _unpad_heads_cols(w[:, C_QA:C_KA], N_HEADS, HEAD_A)
    ka = _unpad_heads_cols(w[:, C_KA:C_VA], N_KV_A, HEAD_A)
    va = _unpad_heads_cols(w[:, C_VA:C_CQ], N_KV_A, HEAD_A)
    kr = w[:, C_KR + QK_NOPE:C_KR + QK_NOPE + QK_ROPE]
    return jnp.concatenate([w[:, :C_QA], qa, ka, va, w[:, C_CQ:C_KR], kr], axis=1)


def _pad_w_kvb(w_kvb):
    w = w_kvb.reshape(KV_LORA, N_HEADS, QK_NOPE + V_DIM_B)
    k = jnp.pad(w[:, :, :QK_NOPE], ((0, 0), (0, 0), (0, SLAB - QK_NOPE))).reshape(KV_LORA, HM)
    v = jnp.pad(w[:, :, QK_NOPE:], ((0, 0), (0, 0), (0, SLAB - V_DIM_B))).reshape(KV_LORA, HM)
    return jnp.concatenate([k, v], axis=1)


def _unpad_w_kvb(w):
    k = w[:, :HM].reshape(KV_LORA, N_HEADS, SLAB)[:, :, :QK_NOPE]
    v = w[:, HM:].reshape(KV_LORA, N_HEADS, SLAB)[:, :, :V_DIM_B]
    return jnp.concatenate([k, v], axis=2).reshape(KV_LORA, N_HEADS * (QK_NOPE + V_DIM_B))


def _col_shards(w):
    k, n = w.shape
    return w.reshape(k, N_DEV, n // N_DEV).transpose(1, 0, 2)


def _from_col_shards(s):
    _, k, ns = s.shape
    return s.transpose(1, 0, 2).reshape(k, N_DEV * ns)


def _freq_row():
    freqs = ROPE_THETA ** (-jnp.arange(0, QK_ROPE, 2, dtype=F32) / QK_ROPE)
    return jnp.concatenate([jnp.zeros((QK_NOPE,), F32), freqs, freqs,
                            jnp.zeros((SLAB - QK_NOPE - QK_ROPE,), F32)]).reshape(1, SLAB)


SMALL_D_ROWS = ("pre_norm_mix", "post_norm_mix", "pre_norm_mlp", "post_norm_mlp")
SMALL_Q_OFF, SMALL_KV_OFF, SMALL_SINK_OFF, SMALL_LOSS_OFF = 0, 256, 384, 392


def _pack_small(vals):
    row4 = jnp.concatenate([vals["q_a_norm"].reshape(-1), vals["kv_a_norm"].reshape(-1), vals["sinks"].reshape(-1),
                            vals["loss"].reshape(-1), jnp.zeros((1024 - 393,), F32)])
    rows = [vals[n].reshape(1024) for n in SMALL_D_ROWS] + [row4]
    return jnp.concatenate([jnp.stack(rows), jnp.zeros((SMALL_ROWS - 5, 1024), F32)], axis=0)


def _unpack_small(blk):
    out = {n: blk[i].reshape(1, 1024) for i, n in enumerate(SMALL_D_ROWS)}
    out["q_a_norm"] = blk[4, SMALL_Q_OFF:SMALL_Q_OFF + 256].reshape(1, 256)
    out["kv_a_norm"] = blk[4, SMALL_KV_OFF:SMALL_KV_OFF + 128].reshape(1, 128)
    out["sinks"] = blk[4, SMALL_SINK_OFF:SMALL_SINK_OFF + 8].reshape(1, 8)
    out["loss"] = blk[4, SMALL_LOSS_OFF]
    return out


WEIGHT_ORDER = ("pre_norm_mix", "w_in", "q_a_norm", "w_q_b", "kv_a_norm", "w_kv_b", "sinks", "w_o_a", "w_o_b",
                "w_out", "post_norm_mix", "pre_norm_mlp", "w_up", "w_down", "post_norm_mlp")
SMALL_NAMES = ("pre_norm_mix", "q_a_norm", "kv_a_norm", "sinks", "post_norm_mix", "pre_norm_mlp", "post_norm_mlp")


def kernel(x, positions, pre_norm_mix, w_in, q_a_norm, w_q_b, kv_a_norm, w_kv_b, sinks, w_o_a, w_o_b, w_out, post_norm_mix, pre_norm_mlp, w_up, w_down, post_norm_mlp, loss_target, m_pre_norm_mix, m_w_in, m_q_a_norm, m_w_q_b, m_kv_a_norm, m_w_kv_b, m_sinks, m_w_o_a, m_w_o_b, m_w_out, m_post_norm_mix, m_pre_norm_mlp, m_w_up, m_w_down, m_post_norm_mlp, v_pre_norm_mix, v_w_in, v_q_a_norm, v_w_q_b, v_kv_a_norm, v_w_kv_b, v_sinks, v_w_o_a, v_w_o_b, v_w_out, v_post_norm_mix, v_pre_norm_mlp, v_w_up, v_w_down, v_post_norm_mlp):
    weights = dict(pre_norm_mix=pre_norm_mix, w_in=w_in, q_a_norm=q_a_norm, w_q_b=w_q_b, kv_a_norm=kv_a_norm,
                   w_kv_b=w_kv_b, sinks=sinks, w_o_a=w_o_a, w_o_b=w_o_b, w_out=w_out, post_norm_mix=post_norm_mix,
                   pre_norm_mlp=pre_norm_mlp, w_up=w_up, w_down=w_down, post_norm_mlp=post_norm_mlp)
    m_in = dict(pre_norm_mix=m_pre_norm_mix, w_in=m_w_in, q_a_norm=m_q_a_norm, w_q_b=m_w_q_b, kv_a_norm=m_kv_a_norm,
                w_kv_b=m_w_kv_b, sinks=m_sinks, w_o_a=m_w_o_a, w_o_b=m_w_o_b, w_out=m_w_out,
                post_norm_mix=m_post_norm_mix, pre_norm_mlp=m_pre_norm_mlp, w_up=m_w_up, w_down=m_w_down,
                post_norm_mlp=m_post_norm_mlp)
    v_in = dict(pre_norm_mix=v_pre_norm_mix, w_in=v_w_in, q_a_norm=v_q_a_norm, w_q_b=v_w_q_b, kv_a_norm=v_kv_a_norm,
                w_kv_b=v_w_kv_b, sinks=v_sinks, w_o_a=v_w_o_a, w_o_b=v_w_o_b, w_out=v_w_out,
                post_norm_mix=v_post_norm_mix, pre_norm_mlp=v_pre_norm_mlp, w_up=v_w_up, w_down=v_w_down,
                post_norm_mlp=v_post_norm_mlp)

    xs, pos, target = x[0], positions[0], loss_target[0]
    t = xs.shape[0]
    pos_col = pos.reshape(t, 1)
    pos_row = pos.reshape(1, t)
    g1, g2, g3, g4 = (weights[n] for n in SMALL_D_ROWS)
    g_q, g_kv = q_a_norm, kv_a_norm
    sink_vec = sinks.reshape(N_HEADS)
    shard = {n: weights[n][0].astype(BF16) for n in EARLY + LATE}

    e_in, e_qb, e_kvb = _all_gather([shard[n] for n in EARLY])
    w_in_p = _pad_w_in(_from_col_shards(e_in))
    w_qb = _pad_heads_cols(_from_col_shards(e_qb), N_HEADS, QK_NOPE + QK_ROPE)
    w_kvb = _pad_w_kvb(_from_col_shards(e_kvb))

    tables = _rope_tables(pos_col, pos_row, _freq_row())
    (h, gates, qa, ka, va, cq, ckv, cqn, ckvn, kb, vb, qt, kt, vt) = _inproj_fwd(
        xs, g1, w_in_p, g_q, g_kv, w_kvb, w_qb.T, w_kvb[:, :HM].T, w_kvb[:, HM:].T, w_in_p[:, C_KR:].T, tables)
    out_a, lse_a = _swa_fwd(qa, ka, va, pos_col, pos_row, sink_vec)
    out_b, out_b_t, lse_b, (l_oa, l_ob, l_out, w_up_s, l_down) = _mla_fwd(qt, kb, vt, [shard[n] for n in LATE])
    w_oa = _pad_heads_rows(_from_col_shards(l_oa), N_HEADS, HEAD_A)
    w_ob = _pad_heads_rows(_from_col_shards(l_ob), N_HEADS, V_DIM_B)
    w_out_f = l_out.reshape(D_MODEL, D_MODEL)
    w_down_f = l_down.reshape(D_FF, D_MODEL)

    oa_p, ob_p, merged, y, x1, h2 = _merge_fwd(out_a, out_b, gates, xs, w_oa, w_ob, w_out_f, g2, g3)
    a, du, dy2, dx1, loss, dg3, dg4 = _mlp_fwd_bwd(x1, h2, target, w_up_s, w_down_f, g3, g4)
    (dy, d_oap, d_obp, dgates, d_oa, d_ob_t, delta_b, dg2) = _merge_bwd(
        dx1, y, gates, oa_p, ob_p, out_b_t, w_oa, w_ob, w_out_f, g2)
    late_slices = [
        _col_shards(_unpad_heads_rows(_matmul_tn(out_a, d_oap, "dw_o_a"), N_HEADS, HEAD_A)).astype(BF16),
        _col_shards(_unpad_heads_rows(_matmul_tn(out_b, d_obp, "dw_o_b"), N_HEADS, V_DIM_B)).astype(BF16),
        _matmul_tn(merged, dy, "dw_out", BF16).reshape(N_DEV, D_MODEL // N_DEV, D_MODEL),
        _matmul_tn(h2, du, "dw_up", BF16, N_DEV),
        _matmul_tn(a, dy2, "dw_down", BF16).reshape(N_DEV, D_FF // N_DEV, D_MODEL),
    ]
    dqa, dka, dva, dsink = _swa_bwd(qa, ka, va, out_a, d_oa, lse_a, pos_col, pos_row, sink_vec)
    dqb_t, dkb_t, dvb_t, late_parts = _mla_bwd(qt, kb, kt, vb, d_ob_t, lse_b, delta_b, late_slices)
    dproj, dqbr_t, dkvb_t, dx, dg1, dgq, dgkv = _inproj_bwd(
        dgates, dqa, dka, dva, dqb_t, dkb_t, dvb_t, cq, ckv, xs, dx1, *tables[3:], g1, g_q, g_kv,
        w_in_p, w_qb, w_kvb)
    early_slices = [
        _col_shards(_unpad_w_in(_matmul_tn(h, dproj, "dw_in"))).astype(BF16),
        _col_shards(_unpad_heads_cols(_matmul_nn(dqbr_t, cqn, "dw_q_b").T, N_HEADS, QK_NOPE + QK_ROPE)).astype(BF16),
        _col_shards(_unpad_w_kvb(_matmul_nn(dkvb_t, ckvn, "dw_kv_b").T)).astype(BF16),
    ]
    small_grads = {"pre_norm_mix": dg1, "post_norm_mix": dg2, "pre_norm_mlp": dg3, "post_norm_mlp": dg4,
                   "q_a_norm": dgq, "kv_a_norm": dgkv, "sinks": dsink.reshape(N_HEADS, BLOCK).sum(axis=1),
                   "loss": loss[0, 0:1]}
    early_parts, s_parts = _exchange_grads(early_slices, _pack_small(small_grads))

    updates = {}
    for name, parts in zip(EARLY + LATE, early_parts + late_parts):
        outs = _adamw(parts, weights[name], m_in[name], v_in[name], "adamw_" + name)
        for kind, arr in zip(("g", "d", "m", "v"), outs):
            updates[kind, name] = arr
    zero = jnp.zeros((), F32)
    pack = lambda src: _pack_small({**{n: src[n] for n in SMALL_NAMES}, "loss": zero})[None]
    smalls = _adamw(s_parts, pack(weights), pack(m_in), pack(v_in), "adamw_small")
    for kind, blk in zip(("g", "d", "m", "v"), smalls):
        for wname, piece in _unpack_small(blk[0]).items():
            updates[kind, wname] = piece
    results = [updates[kind, name] for kind in ("g", "d", "m", "v") for name in WEIGHT_ORDER]
    return (updates["g", "loss"], dx[None], *results)
```

```python
import functools

import numpy as np
import jax
import jax.numpy as jnp
from jax import lax
from jax.experimental import pallas as pl
from jax.experimental.pallas import tpu as pltpu

F32 = jnp.float32
BF16 = jnp.bfloat16

D_MODEL = 1024
D_FF = 4096
N_HEADS = 8
N_KV_A = 2
GROUP_A = N_HEADS // N_KV_A
HEAD_A = 64
QK_NOPE = 64
QK_ROPE = 32
V_DIM_B = 64
Q_LORA = 256
KV_LORA = 128
BLOCK = 128
SLAB = 128
ROPE_THETA = 10000.0
EPS = 1e-6
N_DEV = 8
NEG = -1e30

SCALE_A = HEAD_A ** -0.5
SCALE_B = (QK_NOPE + QK_ROPE) ** -0.5
LOG2E = 1.4426950408889634
SCORE_B = SCALE_B * LOG2E
MLA_HEADS_PER_STEP = 4
MLA_FWD_HEADS_PER_STEP = 8
SLOPES_A = tuple(2.0 ** (-8.0 * (h + 1) / N_HEADS) for h in range(N_HEADS))

ADAM_LR = 0.001
ADAM_B1 = 0.9
ADAM_B2 = 0.999
ADAM_EPS = 1e-08
ADAM_WD = 0.01
ADAM_STEP = 10

HM = N_HEADS * SLAB
C_GATES = 0
C_QA = 2 * D_MODEL
C_KA = C_QA + HM
C_VA = C_KA + N_KV_A * SLAB
C_CQ = C_VA + N_KV_A * SLAB
C_CKV = C_CQ + Q_LORA
C_KR = C_CKV + KV_LORA
D_IN_PAD = C_KR + SLAB

VMEM_LIMIT = 56 * 1024 * 1024

EARLY = ("w_in", "w_q_b", "w_kv_b")
LATE = ("w_o_a", "w_o_b", "w_out", "w_up", "w_down")
ADAM_ROWS = 256
SMALL_ROWS = 8


def _token_tile(t):
    return min(256, t)


def _attn_tile(t):
    return 512 if t >= 2048 else 128


def _params(sem, vmem=VMEM_LIMIT):
    return pltpu.CompilerParams(dimension_semantics=sem, vmem_limit_bytes=vmem)


def _dot(a, b):
    return jnp.dot(a, b, preferred_element_type=F32)


def _dot_nt(a, b):
    return lax.dot_general(a, b, (((1,), (1,)), ((), ())), preferred_element_type=F32)


def _dot_tn(a, b):
    return lax.dot_general(a, b, (((0,), (0,)), ((), ())), preferred_element_type=F32)


def _rms_r(x):
    return lax.rsqrt(jnp.mean(x * x, axis=-1, keepdims=True) + EPS)


def _rms_bwd(x, r, g, dy):
    t = dy * g
    return r * t - x * (r * r * r) * jnp.mean(x * t, axis=-1, keepdims=True)


def _sigmoid(x):
    return 1.0 / (1.0 + jnp.exp(-x))


def _rope(x, c, s1, s2):
    return x * c + pltpu.roll(x, SLAB - 16, 1) * s1 + pltpu.roll(x, 16, 1) * s2


def _rope_bwd(d, c, s1, s2):
    return d * c + pltpu.roll(d * s1, 16, 1) + pltpu.roll(d * s2, SLAB - 16, 1)


def _roll_rows(x, shift):
    return jnp.concatenate([x[-shift:], x[:-shift]], axis=0)


def _rope_t(x, c, s1, s2):
    return x * c + _roll_rows(x, SLAB - 16) * s1 + _roll_rows(x, 16) * s2


def _rope_t_bwd(d, c, s1, s2):
    return d * c + _roll_rows(d * s1, 16) + _roll_rows(d * s2, SLAB - 16)


def _row_spec(tm, n):
    return pl.BlockSpec((tm, n), lambda i: (i, 0))


def _col_spec(n, tm):
    return pl.BlockSpec((n, tm), lambda i: (0, i))


def _full_spec(shape):
    nd = len(shape)
    return pl.BlockSpec(shape, lambda i: (0,) * nd, pipeline_mode=pl.Buffered(1))


def _acc_rows(ref, val):
    @pl.when(pl.program_id(0) == 0)
    def _():
        ref[...] = jnp.zeros_like(ref)
    ref[...] += jnp.sum(val, axis=0, keepdims=True)


def _rope_tables(pos_col, pos_row, freq_row):
    t = pos_col.shape[0]
    tm = _token_tile(t)

    def tables(ang, idx):
        s = jnp.sin(ang)
        return (jnp.cos(ang), jnp.where((idx >= 64) & (idx < 80), -s, 0.0),
                jnp.where((idx >= 80) & (idx < 96), s, 0.0))

    def body(pos_ref, posr_ref, f_ref, fc_ref, c_ref, s1_ref, s2_ref, ct_ref, s1t_ref, s2t_ref):
        ang = pos_ref[...].astype(F32) * f_ref[...]
        c_ref[...], s1_ref[...], s2_ref[...] = tables(ang, lax.broadcasted_iota(jnp.int32, ang.shape, 1))
        angt = posr_ref[...].astype(F32) * fc_ref[...]
        ct_ref[...], s1t_ref[...], s2t_ref[...] = tables(angt, lax.broadcasted_iota(jnp.int32, angt.shape, 0))

    tab = jax.ShapeDtypeStruct((t, SLAB), F32)
    tabt = jax.ShapeDtypeStruct((SLAB, t), F32)
    return pl.pallas_call(
        body, name="rope_tables", grid=(t // tm,),
        in_specs=[_row_spec(tm, 1), _col_spec(1, tm), _full_spec((1, SLAB)), _full_spec((SLAB, 1))],
        out_specs=[_row_spec(tm, SLAB)] * 3 + [_col_spec(SLAB, tm)] * 3, out_shape=[tab] * 3 + [tabt] * 3,
        compiler_params=_params(("parallel",)),
    )(pos_col, pos_row, freq_row, freq_row.reshape(SLAB, 1))


def _inproj_fwd(x, g1, w_in, g_q, g_kv, w_kvb, w_qb_t, w_kb_t, w_vb_t, w_kr_t, tables):
    t = x.shape[0]
    tm = _token_tile(t)

    def body(x_ref, g1_ref, win_ref, gq_ref, gkv_ref, wkvb_ref, wqbt_ref, wkbt_ref, wvbt_ref, wkrt_ref,
             c_ref, s1_ref, s2_ref, ct_ref, s1t_ref, s2t_ref,
             h_ref, gates_ref, qa_ref, ka_ref, va_ref, cq_ref, ckv_ref, cqn_ref, ckvn_ref,
             kb_ref, vb_ref, qt_ref, kt_ref, vt_ref):
        xv = x_ref[...]
        h = (xv * _rms_r(xv) * g1_ref[...]).astype(BF16)
        h_ref[...] = h
        proj = _dot(h, win_ref[...])
        gates_ref[...] = proj[:, C_GATES:C_QA]
        qa_ref[...] = proj[:, C_QA:C_KA].astype(BF16)
        ka_ref[...] = proj[:, C_KA:C_VA].astype(BF16)
        va_ref[...] = proj[:, C_VA:C_CQ].astype(BF16)
        cq = proj[:, C_CQ:C_CKV]
        ckv = proj[:, C_CKV:C_KR]
        kr = proj[:, C_KR:D_IN_PAD]
        cq_ref[...] = cq
        ckv_ref[...] = ckv
        cqn = (cq * _rms_r(cq) * gq_ref[...]).astype(BF16)
        ckvn = (ckv * _rms_r(ckv) * gkv_ref[...]).astype(BF16)
        cqn_ref[...] = cqn
        ckvn_ref[...] = ckvn
        c, s1, s2 = c_ref[...], s1_ref[...], s2_ref[...]
        kvb = _dot(ckvn, wkvb_ref[...])
        kr_rot = _rope(kr, c, s1, s2)
        ct, s1t, s2t = ct_ref[...], s1t_ref[...], s2t_ref[...]
        q_t = _dot_nt(wqbt_ref[...], cqn)
        k_t = _dot_nt(wkbt_ref[...], ckvn)
        kr_t = _rope_t(_dot_nt(wkrt_ref[...], h), ct, s1t, s2t)
        for hd in range(N_HEADS):
            sl = slice(hd * SLAB, (hd + 1) * SLAB)
            kb_ref[:, sl] = (kvb[:, sl] + kr_rot).astype(BF16)
            qt_ref[sl, :] = (_rope_t(q_t[sl, :], ct, s1t, s2t) * SCORE_B).astype(BF16)
            kt_ref[sl, :] = (k_t[sl, :] + kr_t).astype(BF16)
        vb_ref[...] = kvb[:, HM:2 * HM].astype(BF16)
        vt_ref[...] = _dot_nt(wvbt_ref[...], ckvn).astype(BF16)

    def sds(n, dt):
        return jax.ShapeDtypeStruct((t, n), dt)

    outs = [(D_MODEL, BF16), (2 * D_MODEL, F32), (HM, BF16), (N_KV_A * SLAB, BF16), (N_KV_A * SLAB, BF16),
            (Q_LORA, F32), (KV_LORA, F32), (Q_LORA, BF16), (KV_LORA, BF16), (HM, BF16), (HM, BF16)]
    tab, tabt = _row_spec(tm, SLAB), _col_spec(SLAB, tm)
    return pl.pallas_call(
        body, name="inproj_fwd", grid=(t // tm,),
        in_specs=[_row_spec(tm, D_MODEL), _full_spec((1, D_MODEL)), _full_spec((D_MODEL, D_IN_PAD)),
                  _full_spec((1, Q_LORA)), _full_spec((1, KV_LORA)), _full_spec((KV_LORA, 2 * HM)),
                  _full_spec((HM, Q_LORA)), _full_spec((HM, KV_LORA)), _full_spec((HM, KV_LORA)),
                  _full_spec((SLAB, D_MODEL)), tab, tab, tab, tabt, tabt, tabt],
        out_specs=[_row_spec(tm, n) for n, _ in outs] + [_col_spec(HM, tm)] * 3,
        out_shape=[sds(n, dt) for n, dt in outs] + [jax.ShapeDtypeStruct((HM, t), BF16)] * 3,
        compiler_params=_params(("parallel",)),
    )(x, g1, w_in, g_q, g_kv, w_kvb, w_qb_t, w_kb_t, w_vb_t, w_kr_t, *tables)


def _tile_group(a):
    return jnp.concatenate([a] * GROUP_A, axis=1)


def _swa_masks():
    row = lax.broadcasted_iota(jnp.int32, (BLOCK, GROUP_A * BLOCK), 0)
    col = lax.broadcasted_iota(jnp.int32, (BLOCK, GROUP_A * BLOCK), 1) & (BLOCK - 1)
    return row <= col, row > col


def _heads_beside(ref, g):
    return jnp.concatenate([ref[:, (g * GROUP_A + hh) * SLAB:(g * GROUP_A + hh + 1) * SLAB].T
                            for hh in range(GROUP_A)], axis=1)


def _rows_beside(ref, g):
    return jnp.concatenate([ref[g * GROUP_A + hh] for hh in range(GROUP_A)], axis=1)


def _swa_rows(sinks):
    slopes = jnp.repeat(jnp.asarray(SLOPES_A, F32).reshape(N_KV_A, GROUP_A, 1), BLOCK, axis=2)
    sink_rows = jnp.repeat(sinks.reshape(N_KV_A, GROUP_A, 1), BLOCK, axis=2)
    return slopes.reshape(N_KV_A, 1, GROUP_A * BLOCK), sink_rows.reshape(N_KV_A, 1, GROUP_A * BLOCK)


def _swa_fwd(qa, ka, va, pos_col, pos_row, sinks):
    t = qa.shape[0]
    nb = t // BLOCK
    gw = GROUP_A * BLOCK
    slope_rows, sink_rows = _swa_rows(sinks)

    def body(q_ref, kc_ref, kp_ref, vc_ref, vp_ref, pkc_ref, pkp_ref, pq_ref, slope_ref, sink_ref, o_ref, l_ref):
        i = pl.program_id(0)
        pq = pq_ref[...]
        dist_c = _tile_group(jnp.abs(pkc_ref[...] - pq).astype(F32))
        dist_p = _tile_group(jnp.abs(pkp_ref[...] - pq).astype(F32))
        mask_c, older = _swa_masks()
        mask_p = jnp.logical_and(older, i > 0)
        for g in range(N_KV_A):
            gs = slice(g * SLAB, (g + 1) * SLAB)
            x = _heads_beside(q_ref, g)
            slope, sink = slope_ref[g], sink_ref[g]
            s_c = jnp.where(mask_c, _dot(kc_ref[:, gs], x) * SCALE_A - slope * dist_c, NEG)
            s_p = jnp.where(mask_p, _dot(kp_ref[:, gs], x) * SCALE_A - slope * dist_p, NEG)
            m = jnp.maximum(jnp.maximum(jnp.max(s_c, axis=0, keepdims=True),
                                        jnp.max(s_p, axis=0, keepdims=True)), sink)
            e_c = jnp.exp(s_c - m)
            e_p = jnp.exp(s_p - m)
            den = jnp.sum(e_c, axis=0, keepdims=True) + jnp.sum(e_p, axis=0, keepdims=True) + jnp.exp(sink - m)
            inv = 1.0 / den
            ot = (_dot_tn(vc_ref[:, gs], (e_c * inv).astype(BF16))
                  + _dot_tn(vp_ref[:, gs], (e_p * inv).astype(BF16)))
            lse = m + jnp.log(den)
            for hh in range(GROUP_A):
                hd = g * GROUP_A + hh
                seg = slice(hh * BLOCK, (hh + 1) * BLOCK)
                o_ref[:, hd * SLAB:(hd + 1) * SLAB] = ot[:, seg].T.astype(BF16)
                l_ref[hd] = lse[:, seg]

    cur = lambda i: (i, 0)
    prev = lambda i: (jnp.maximum(i - 1, 0), 0)
    kvw = N_KV_A * SLAB
    rows = pl.BlockSpec((N_KV_A, 1, gw), lambda i: (0, 0, 0))
    return pl.pallas_call(
        body, name="swa_fwd", grid=(nb,),
        in_specs=[pl.BlockSpec((BLOCK, HM), cur),
                  pl.BlockSpec((BLOCK, kvw), cur), pl.BlockSpec((BLOCK, kvw), prev),
                  pl.BlockSpec((BLOCK, kvw), cur), pl.BlockSpec((BLOCK, kvw), prev),
                  pl.BlockSpec((BLOCK, 1), cur), pl.BlockSpec((BLOCK, 1), prev),
                  pl.BlockSpec((1, BLOCK), lambda i: (0, i)), rows, rows],
        out_specs=[pl.BlockSpec((BLOCK, HM), cur), pl.BlockSpec((N_HEADS, 1, BLOCK), lambda i: (0, 0, i))],
        out_shape=[jax.ShapeDtypeStruct((t, HM), BF16), jax.ShapeDtypeStruct((N_HEADS, 1, t), F32)],
        compiler_params=_params(("parallel",)),
    )(qa, ka, ka, va, va, pos_col, pos_col, pos_row, slope_rows, sink_rows)


def _swa_bwd(qa, ka, va, out_a, d_oa, lse, pos_col, pos_row, sinks):
    t = qa.shape[0]
    nb = t // BLOCK
    gw = GROUP_A * BLOCK
    slope_rows, sink_rows = _swa_rows(sinks)

    def body(q_ref, qn_ref, do_ref, don_ref, l_ref, ln_ref, o_ref, on_ref, kp_ref, kc_ref, vp_ref, vc_ref,
             pkp_ref, pkc_ref, pq_ref, pqn_ref, slope_ref, sink_ref, dq_ref, dk_ref, dv_ref, dsink_ref):
        j = pl.program_id(0)
        pkc, pkp = pkc_ref[...], pkp_ref[...]
        dist_cc = _tile_group(jnp.abs(pkc - pq_ref[...]).astype(F32))
        dist_cp = _tile_group(jnp.abs(pkp - pq_ref[...]).astype(F32))
        dist_nc = _tile_group(jnp.abs(pkc - pqn_ref[...]).astype(F32))
        mask_cc, older = _swa_masks()
        mask_cp = jnp.logical_and(older, j > 0)
        mask_nc = jnp.logical_and(older, j < nb - 1)

        @pl.when(j == 0)
        def _():
            dsink_ref[...] = jnp.zeros_like(dsink_ref)

        def tile(k, v, x, dox, lrow, drow, dist, mask, slope):
            s = jnp.where(mask, _dot(k, x) * SCALE_A - slope * dist, NEG)
            p = jnp.exp(s - lrow)
            ds = p * (_dot(v, dox) - drow)
            return p.astype(BF16), ds.astype(BF16)

        for g in range(N_KV_A):
            gs = slice(g * SLAB, (g + 1) * SLAB)
            kc, kp, vc, vp = kc_ref[:, gs], kp_ref[:, gs], vc_ref[:, gs], vp_ref[:, gs]
            slope, sink = slope_ref[g], sink_ref[g]
            x, xn = _heads_beside(q_ref, g), _heads_beside(qn_ref, g)
            dox, doxn = _heads_beside(do_ref, g), _heads_beside(don_ref, g)
            lrow, lrown = _rows_beside(l_ref, g), _rows_beside(ln_ref, g)
            drow = jnp.sum(dox.astype(F32) * _heads_beside(o_ref, g).astype(F32), axis=0, keepdims=True)
            drown = jnp.sum(doxn.astype(F32) * _heads_beside(on_ref, g).astype(F32), axis=0, keepdims=True)
            p_cc, ds_cc = tile(kc, vc, x, dox, lrow, drow, dist_cc, mask_cc, slope)
            _, ds_cp = tile(kp, vp, x, dox, lrow, drow, dist_cp, mask_cp, slope)
            p_nc, ds_nc = tile(kc, vc, xn, doxn, lrown, drown, dist_nc, mask_nc, slope)
            dqt = (_dot_tn(kc, ds_cc) + _dot_tn(kp, ds_cp)) * SCALE_A
            for hh in range(GROUP_A):
                hd = g * GROUP_A + hh
                dq_ref[:, hd * SLAB:(hd + 1) * SLAB] = dqt[:, hh * BLOCK:(hh + 1) * BLOCK].T.astype(BF16)
            dk_ref[:, gs] = ((_dot_nt(ds_cc, x) + _dot_nt(ds_nc, xn)) * SCALE_A).astype(BF16)
            dv_ref[:, gs] = (_dot_nt(p_cc, dox) + _dot_nt(p_nc, doxn)).astype(BF16)
            dsink_ref[g] -= jnp.exp(sink - lrow) * drow

    cur = lambda j: (j, 0)
    prev = lambda j: (jnp.maximum(j - 1, 0), 0)
    nxt = lambda j: (jnp.minimum(j + 1, nb - 1), 0)
    cur3 = lambda j: (0, 0, j)
    nxt3 = lambda j: (0, 0, jnp.minimum(j + 1, nb - 1))
    kvw = N_KV_A * SLAB
    rows = pl.BlockSpec((N_KV_A, 1, gw), lambda j: (0, 0, 0))
    stat = lambda im: pl.BlockSpec((N_HEADS, 1, BLOCK), im)
    return pl.pallas_call(
        body, name="swa_bwd", grid=(nb,),
        in_specs=[pl.BlockSpec((BLOCK, HM), cur), pl.BlockSpec((BLOCK, HM), nxt),
                  pl.BlockSpec((BLOCK, HM), cur), pl.BlockSpec((BLOCK, HM), nxt),
                  stat(cur3), stat(nxt3), pl.BlockSpec((BLOCK, HM), cur), pl.BlockSpec((BLOCK, HM), nxt),
                  pl.BlockSpec((BLOCK, kvw), prev), pl.BlockSpec((BLOCK, kvw), cur),
                  pl.BlockSpec((BLOCK, kvw), prev), pl.BlockSpec((BLOCK, kvw), cur),
                  pl.BlockSpec((BLOCK, 1), prev), pl.BlockSpec((BLOCK, 1), cur),
                  pl.BlockSpec((1, BLOCK), lambda j: (0, j)),
                  pl.BlockSpec((1, BLOCK), lambda j: (0, jnp.minimum(j + 1, nb - 1))), rows, rows],
        out_specs=[pl.BlockSpec((BLOCK, HM), cur), pl.BlockSpec((BLOCK, kvw), cur),
                   pl.BlockSpec((BLOCK, kvw), cur), rows],
        out_shape=[jax.ShapeDtypeStruct((t, HM), BF16), jax.ShapeDtypeStruct((t, kvw), BF16),
                   jax.ShapeDtypeStruct((t, kvw), BF16), jax.ShapeDtypeStruct((N_KV_A, 1, gw), F32)],
        compiler_params=_params(("arbitrary",)),
    )(qa, qa, d_oa, d_oa, lse, lse, out_a, out_a, ka, ka, va, va,
      pos_col, pos_col, pos_row, pos_row, slope_rows, sink_rows)


def _lower_triangle(n):
    row = lax.broadcasted_iota(jnp.int32, (n, n), 0)
    col = lax.broadcasted_iota(jnp.int32, (n, n), 1)
    return row >= col


def _upper_triangle(n):
    row = lax.broadcasted_iota(jnp.int32, (n, n), 0)
    col = lax.broadcasted_iota(jnp.int32, (n, n), 1)
    return row <= col


def _mesh_pos():
    return lax.axis_index("x"), lax.axis_index("y"), lax.axis_index("c")


def _flip(v, bit):
    return 1 - v if bit else v


def _direct_copies(srcs, dsts, send_sems, recv_sems, local_sems, gather, sem_base=0):
    x, y, c = _mesh_pos()
    me = 4 * x + 2 * y + c
    local, remote = [], []
    for a, (src, dst) in enumerate(zip(srcs, dsts)):
        local.append(pltpu.make_async_copy(src if gather else src.at[me], dst.at[me], local_sems.at[sem_base + a]))
        for r in range(1, N_DEV):
            px, py, pc = _flip(x, r & 4), _flip(y, r & 2), _flip(c, r & 1)
            sem = (N_DEV - 1) * (sem_base + a) + r - 1
            remote.append(pltpu.make_async_remote_copy(
                src_ref=src if gather else src.at[4 * px + 2 * py + pc], dst_ref=dst.at[me],
                send_sem=send_sems.at[sem], recv_sem=recv_sems.at[sem],
                device_id=(px, py, pc), device_id_type=pl.DeviceIdType.MESH))
    return local, remote


def _start_copies(local, remote):
    for cp in local + remote:
        cp.start()


def _wait_copies(local, remote):
    for cp in remote:
        cp.wait_recv()
    for cp in remote:
        cp.wait_send()
    for cp in local:
        cp.wait()


def _exchange_scratch(n):
    return [pltpu.SemaphoreType.DMA((n * (N_DEV - 1),)), pltpu.SemaphoreType.DMA((n * (N_DEV - 1),)),
            pltpu.SemaphoreType.DMA((n,))]


ANY_SPEC = pl.BlockSpec(memory_space=pl.ANY)


def _mla_fwd(qt, kb, vt, late):
    t = kb.shape[0]
    tq = _attn_tile(t)
    nt = t // tq
    hps = MLA_FWD_HEADS_PER_STEP
    w = hps * SLAB
    pairs = [(i, j) for i in range(nt) for j in range(i + 1)]
    i_tab = jnp.asarray(np.array([p[0] for p in pairs], np.int32))
    j_tab = jnp.asarray(np.array([p[1] for p in pairs], np.int32))

    n_late = len(late)

    def body(it_ref, jt_ref, qt_ref, k_ref, vt_ref, *rest):
        late_refs, (o_ref, ot_ref, l_ref) = rest[:n_late], rest[n_late:n_late + 3]
        gathered_refs = rest[n_late + 3:2 * n_late + 3]
        m_s, l_s, acc_s, send_sems, recv_sems, local_sems = rest[2 * n_late + 3:]
        n = pl.program_id(1)
        i, j = it_ref[n], jt_ref[n]
        first_step = jnp.logical_and(pl.program_id(0) == 0, n == 0)
        last_step = jnp.logical_and(pl.program_id(0) == N_HEADS // hps - 1, n == len(pairs) - 1)

        @pl.when(first_step)
        def _():
            _start_copies(*_direct_copies(late_refs, gathered_refs, send_sems, recv_sems, local_sems, True))

        @pl.when(j == 0)
        def _():
            m_s[...] = jnp.full_like(m_s, NEG)
            l_s[...] = jnp.zeros_like(l_s)
            acc_s[...] = jnp.zeros_like(acc_s)

        def update(masked):
            def scores(hh):
                sl = slice(hh * SLAB, (hh + 1) * SLAB)
                return _dot(k_ref[:, sl], qt_ref[sl, :])

            def softmax(hh, s):
                if masked:
                    s = jnp.where(_upper_triangle(tq), s, NEG)
                m_old = m_s[hh]
                m_new = jnp.maximum(m_old, jnp.max(s, axis=0, keepdims=True))
                alpha = jnp.exp2(m_old - m_new)
                p = jnp.exp2(s - m_new)
                l_s[hh] = alpha * l_s[hh] + jnp.sum(p, axis=0, keepdims=True)
                m_s[hh] = m_new
                return p.astype(BF16), alpha

            def accumulate(hh, p, alpha):
                sl = slice(hh * SLAB, hh * SLAB + V_DIM_B)
                acc_s[sl, :] = alpha * acc_s[sl, :] + _dot(vt_ref[sl, :], p)

            s_next, pending = scores(0), None
            for hh in range(hps):
                s = s_next
                if hh + 1 < hps:
                    s_next = scores(hh + 1)
                p, alpha = softmax(hh, s)
                if pending is not None:
                    accumulate(*pending)
                pending = (hh, p, alpha)
            accumulate(*pending)

        @pl.when(j < i)
        def _():
            update(False)

        @pl.when(j == i)
        def _():
            update(True)
            for hh in range(hps):
                sl = slice(hh * SLAB, (hh + 1) * SLAB)
                ot = acc_s[sl, :] / l_s[hh]
                ot_ref[sl, :] = ot.astype(BF16)
                o_ref[:, sl] = ot.T.astype(BF16)
                l_ref[hh] = m_s[hh] + jnp.log2(l_s[hh])

        @pl.when(last_step)
        def _():
            _wait_copies(*_direct_copies(late_refs, gathered_refs, send_sems, recv_sems, local_sems, True))

    grid_spec = pltpu.PrefetchScalarGridSpec(
        num_scalar_prefetch=2, grid=(N_HEADS // hps, len(pairs)),
        in_specs=[pl.BlockSpec((w, tq), lambda h, n, it, jt: (h, it[n])),
                  pl.BlockSpec((tq, w), lambda h, n, it, jt: (jt[n], h)),
                  pl.BlockSpec((w, tq), lambda h, n, it, jt: (h, jt[n]))] + [ANY_SPEC] * n_late,
        out_specs=[pl.BlockSpec((tq, w), lambda h, n, it, jt: (it[n], h)),
                   pl.BlockSpec((w, tq), lambda h, n, it, jt: (h, it[n])),
                   pl.BlockSpec((hps, 1, tq), lambda h, n, it, jt: (h, 0, it[n]))] + [ANY_SPEC] * n_late,
        scratch_shapes=[pltpu.VMEM((hps, 1, tq), F32), pltpu.VMEM((hps, 1, tq), F32), pltpu.VMEM((w, tq), F32)]
        + _exchange_scratch(n_late))
    outs = pl.pallas_call(
        body, name="mla_fwd", grid_spec=grid_spec,
        out_shape=[jax.ShapeDtypeStruct((t, HM), BF16), jax.ShapeDtypeStruct((HM, t), BF16),
                   jax.ShapeDtypeStruct((N_HEADS, 1, t), F32)]
        + [jax.ShapeDtypeStruct((N_DEV,) + a.shape, a.dtype) for a in late],
        compiler_params=_params(("arbitrary", "arbitrary")),
    )(i_tab, j_tab, qt, kb, vt, *late)
    return outs[0], outs[1], outs[2], list(outs[3:])


def _mla_bwd(qt, kb, kt, vb, d_ob_t, lse, delta, grad_slices):
    t = kb.shape[0]
    tk = _attn_tile(t)
    ratio = 2 if t >= 2 * tk else 1
    tq = ratio * tk
    nk, nq = t // tk, t // tq
    hps = MLA_HEADS_PER_STEP
    w = hps * SLAB
    pairs = [(j, i) for j in range(nk) for i in range(j // ratio, nq)]
    j_tab = jnp.asarray(np.array([p[0] for p in pairs], np.int32))
    i_tab = jnp.asarray(np.array([p[1] for p in pairs], np.int32))

    n_ex = len(grad_slices)

    def body(jt_ref, it_ref, qt_ref, dot_ref, l_ref, dl_ref, k_ref, kt_ref, v_ref, *rest):
        slice_refs, (dqt_ref, dkt_ref, dvt_ref) = rest[:n_ex], rest[n_ex:n_ex + 3]
        part_refs = rest[n_ex + 3:2 * n_ex + 3]
        dk_s, dv_s, send_sems, recv_sems, local_sems = rest[2 * n_ex + 3:]
        n = pl.program_id(1)
        j, i = jt_ref[n], it_ref[n]
        first_step = jnp.logical_and(pl.program_id(0) == 0, n == 0)
        last_step = jnp.logical_and(pl.program_id(0) == N_HEADS // hps - 1, n == len(pairs) - 1)

        @pl.when(first_step)
        def _():
            _start_copies(*_direct_copies(slice_refs, part_refs, send_sems, recv_sems, local_sems, False))

        @pl.when(n == 0)
        def _():
            dqt_ref[...] = jnp.zeros_like(dqt_ref)

        def update(diagonal, q0):
            qc = slice(q0, tq)
            cols = pl.ds(pl.multiple_of(i * tq + q0, tk), tq - q0)

            def products(hh):
                sl = slice(hh * SLAB, (hh + 1) * SLAB)
                return _dot(k_ref[:, sl], qt_ref[sl, qc]), _dot(v_ref[:, sl], dot_ref[sl, qc])

            def softmax_bwd(hh, s, dp):
                if diagonal:
                    s = jnp.where(lax.broadcasted_iota(jnp.int32, s.shape, 0)
                                  <= lax.broadcasted_iota(jnp.int32, s.shape, 1), s, NEG)
                p = jnp.exp2(s - l_ref[hh][:, qc])
                return p.astype(BF16), (p * (dp - dl_ref[hh][:, qc])).astype(BF16)

            def gradients(hh, p, ds):
                base = hh * SLAB
                vrows = slice(base, base + V_DIM_B)
                qrows = slice(base, base + QK_NOPE + QK_ROPE)
                dv = _dot_nt(dot_ref[vrows, qc], p)
                dk = _dot_nt(qt_ref[qrows, qc], ds)
                if diagonal:
                    dv_s[base:base + SLAB, :] = jnp.concatenate([dv, jnp.zeros((SLAB - V_DIM_B, tk), F32)], axis=0)
                    dk_s[base:base + SLAB, :] = jnp.concatenate(
                        [dk, jnp.zeros((SLAB - QK_NOPE - QK_ROPE, tk), F32)], axis=0)
                else:
                    dv_s[vrows, :] += dv
                    dk_s[qrows, :] += dk
                dqt_ref[qrows, cols] += _dot(kt_ref[qrows, :], ds)

            for hh in range(hps):
                gradients(hh, *softmax_bwd(hh, *products(hh)))

        first_tile = lax.div(j, ratio)
        for part in range(ratio):
            @pl.when(jnp.logical_and(i == first_tile, lax.rem(j, ratio) == part))
            def _():
                update(True, part * tk)

        @pl.when(i > first_tile)
        def _():
            update(False, 0)

        @pl.when(i == nq - 1)
        def _():
            dkt_ref[...] = (dk_s[...] * (1.0 / LOG2E)).astype(BF16)
            dvt_ref[...] = dv_s[...].astype(BF16)

        @pl.when(last_step)
        def _():
            _wait_copies(*_direct_copies(slice_refs, part_refs, send_sems, recv_sems, local_sems, False))

    grid_spec = pltpu.PrefetchScalarGridSpec(
        num_scalar_prefetch=2, grid=(N_HEADS // hps, len(pairs)),
        in_specs=[pl.BlockSpec((w, tq), lambda h, n, jt, it: (h, it[n])),
                  pl.BlockSpec((w, tq), lambda h, n, jt, it: (h, it[n])),
                  pl.BlockSpec((hps, 1, tq), lambda h, n, jt, it: (h, 0, it[n])),
                  pl.BlockSpec((hps, 1, tq), lambda h, n, jt, it: (h, 0, it[n])),
                  pl.BlockSpec((tk, w), lambda h, n, jt, it: (jt[n], h)),
                  pl.BlockSpec((w, tk), lambda h, n, jt, it: (h, jt[n])),
                  pl.BlockSpec((tk, w), lambda h, n, jt, it: (jt[n], h))] + [ANY_SPEC] * n_ex,
        out_specs=[pl.BlockSpec((w, t), lambda h, n, jt, it: (h, 0)),
                   pl.BlockSpec((w, tk), lambda h, n, jt, it: (h, jt[n])),
                   pl.BlockSpec((w, tk), lambda h, n, jt, it: (h, jt[n]))] + [ANY_SPEC] * n_ex,
        scratch_shapes=[pltpu.VMEM((w, tk), F32), pltpu.VMEM((w, tk), F32)] + _exchange_scratch(n_ex))
    outs = pl.pallas_call(
        body, name="mla_bwd", grid_spec=grid_spec,
        out_shape=[jax.ShapeDtypeStruct((HM, t), F32), jax.ShapeDtypeStruct((HM, t), BF16),
                   jax.ShapeDtypeStruct((HM, t), BF16)]
        + [jax.ShapeDtypeStruct(a.shape, a.dtype) for a in grad_slices],
        compiler_params=_params(("arbitrary", "arbitrary")),
    )(j_tab, i_tab, qt, d_ob_t, lse, delta, kb, kt, vb, *grad_slices)
    return outs[0], outs[1], outs[2], list(outs[3:])


def _merge_fwd(out_a, out_b, gates, x, w_oa, w_ob, w_out, g2, g3):
    t = x.shape[0]
    tm = _token_tile(t)

    def body(oa_ref, ob_ref, gates_ref, x_ref, woa_ref, wob_ref, wout_ref, g2_ref, g3_ref,
             oap_ref, obp_ref, merged_ref, y_ref, x1_ref, h2_ref):
        oa_p = _dot(oa_ref[...], woa_ref[...])
        ob_p = _dot(ob_ref[...], wob_ref[...])
        oap_ref[...] = oa_p.astype(BF16)
        obp_ref[...] = ob_p.astype(BF16)
        sa = _sigmoid(gates_ref[:, 0:D_MODEL])
        sb = _sigmoid(gates_ref[:, D_MODEL:2 * D_MODEL])
        merged = (sa * oa_p + sb * ob_p).astype(BF16)
        merged_ref[...] = merged
        y = _dot(merged, wout_ref[...])
        y_ref[...] = y
        x1 = x_ref[...] + y * _rms_r(y) * g2_ref[...]
        x1_ref[...] = x1
        h2_ref[...] = (x1 * _rms_r(x1) * g3_ref[...]).astype(BF16)

    def sds(dt):
        return jax.ShapeDtypeStruct((t, D_MODEL), dt)

    row = _row_spec(tm, D_MODEL)
    return pl.pallas_call(
        body, name="merge_fwd", grid=(t // tm,),
        in_specs=[_row_spec(tm, HM), _row_spec(tm, HM), _row_spec(tm, 2 * D_MODEL), row,
                  _full_spec((HM, D_MODEL)), _full_spec((HM, D_MODEL)), _full_spec((D_MODEL, D_MODEL)),
                  _full_spec((1, D_MODEL)), _full_spec((1, D_MODEL))],
        out_specs=[row] * 6,
        out_shape=[sds(BF16), sds(BF16), sds(BF16), sds(F32), sds(F32), sds(BF16)],
        compiler_params=_params(("parallel",)),
    )(out_a, out_b, gates, x, w_oa, w_ob, w_out, g2, g3)


def _merge_bwd(dx1, y, gates, oa_p, ob_p, out_b_t, w_oa, w_ob, w_out, g2):
    t = dx1.shape[0]
    tm = _token_tile(t)

    def body(dx1_ref, y_ref, gates_ref, oap_ref, obp_ref, obt_ref, woa_ref, wob_ref, wout_ref, g2_ref,
             dy_ref, doap_ref, dobp_ref, dgates_ref, doa_ref, dobt_ref, dlb_ref, dg2_ref):
        dx1v = dx1_ref[...]
        yv = y_ref[...]
        r2 = _rms_r(yv)
        _acc_rows(dg2_ref, dx1v * yv * r2)
        dy = _rms_bwd(yv, r2, g2_ref[...], dx1v).astype(BF16)
        dy_ref[...] = dy
        dm = _dot_nt(dy, wout_ref[...])
        sa = _sigmoid(gates_ref[:, 0:D_MODEL])
        sb = _sigmoid(gates_ref[:, D_MODEL:2 * D_MODEL])
        d_oap = (dm * sa).astype(BF16)
        d_obp = (dm * sb).astype(BF16)
        doap_ref[...] = d_oap
        dobp_ref[...] = d_obp
        dgates_ref[:, 0:D_MODEL] = (dm * oap_ref[...].astype(F32) * sa * (1.0 - sa)).astype(BF16)
        dgates_ref[:, D_MODEL:2 * D_MODEL] = (dm * obp_ref[...].astype(F32) * sb * (1.0 - sb)).astype(BF16)
        doa_ref[...] = _dot_nt(d_oap, woa_ref[...]).astype(BF16)
        d_ob_t = _dot_nt(wob_ref[...], d_obp)
        dobt_ref[...] = d_ob_t.astype(BF16)
        for hd in range(N_HEADS):
            sl = slice(hd * SLAB, (hd + 1) * SLAB)
            dlb_ref[hd] = jnp.sum(d_ob_t[sl, :] * obt_ref[sl, :].astype(F32), axis=0, keepdims=True)

    def sds(n, dt):
        return jax.ShapeDtypeStruct((t, n), dt)

    row = _row_spec(tm, D_MODEL)
    head3 = pl.BlockSpec((N_HEADS, 1, tm), lambda i: (0, 0, i))
    return pl.pallas_call(
        body, name="merge_bwd", grid=(t // tm,),
        in_specs=[row, row, _row_spec(tm, 2 * D_MODEL), row, row, _col_spec(HM, tm),
                  _full_spec((HM, D_MODEL)), _full_spec((HM, D_MODEL)), _full_spec((D_MODEL, D_MODEL)),
                  _full_spec((1, D_MODEL))],
        out_specs=[row, row, row, _row_spec(tm, 2 * D_MODEL), _row_spec(tm, HM), _col_spec(HM, tm),
                   head3, _full_spec((1, D_MODEL))],
        out_shape=[sds(D_MODEL, BF16), sds(D_MODEL, BF16), sds(D_MODEL, BF16), sds(2 * D_MODEL, BF16),
                   sds(HM, BF16), jax.ShapeDtypeStruct((HM, t), BF16),
                   jax.ShapeDtypeStruct((N_HEADS, 1, t), F32), jax.ShapeDtypeStruct((1, D_MODEL), F32)],
        compiler_params=_params(("arbitrary",)),
    )(dx1, y, gates, oa_p, ob_p, out_b_t, w_oa, w_ob, w_out, g2)


def _mlp_fwd_bwd(x1, h2, target, w_up, w_down, g3, g4):
    t = x1.shape[0]
    tm = _token_tile(t)
    fs = D_FF // N_DEV

    def body(x1_ref, h2_ref, tgt_ref, wup_ref, wdown_ref, g3_ref, g4_ref,
             a_ref, du_ref, dy2_ref, dx1_ref, loss_ref, dg3_ref, dg4_ref):
        x1v = x1_ref[...]
        h2v = h2_ref[...]
        u = jnp.concatenate([_dot(h2v, wup_ref[s]) for s in range(N_DEV)], axis=1)
        ru = jnp.maximum(u, 0.0)
        a = (ru * ru).astype(BF16)
        a_ref[...] = a
        y2 = _dot(a, wdown_ref[...])
        r4 = _rms_r(y2)
        diff = x1v + y2 * r4 * g4_ref[...] - tgt_ref[...]
        _acc_rows(loss_ref, jnp.sum(diff * diff, axis=-1, keepdims=True) * (0.5 / D_MODEL)
                  * jnp.ones((1, SLAB), F32))
        dx2 = diff * (1.0 / D_MODEL)
        _acc_rows(dg4_ref, dx2 * y2 * r4)
        dy2 = _rms_bwd(y2, r4, g4_ref[...], dx2).astype(BF16)
        dy2_ref[...] = dy2
        du = (_dot_nt(dy2, wdown_ref[...]) * (2.0 * ru)).astype(BF16)
        du_ref[...] = du
        dh2 = _dot_nt(du[:, 0:fs], wup_ref[0])
        for s in range(1, N_DEV):
            dh2 += _dot_nt(du[:, s * fs:(s + 1) * fs], wup_ref[s])
        r3 = _rms_r(x1v)
        _acc_rows(dg3_ref, dh2 * x1v * r3)
        dx1_ref[...] = dx2 + _rms_bwd(x1v, r3, g3_ref[...], dh2)

    row = _row_spec(tm, D_MODEL)
    frow = _row_spec(tm, D_FF)
    vec = _full_spec((1, D_MODEL))
    return pl.pallas_call(
        body, name="mlp_fwd_bwd", grid=(t // tm,),
        in_specs=[row, row, row, _full_spec((N_DEV, D_MODEL, fs)), _full_spec((D_FF, D_MODEL)), vec, vec],
        out_specs=[frow, frow, row, row, _full_spec((1, SLAB)), vec, vec],
        out_shape=[jax.ShapeDtypeStruct((t, D_FF), BF16), jax.ShapeDtypeStruct((t, D_FF), BF16),
                   jax.ShapeDtypeStruct((t, D_MODEL), BF16), jax.ShapeDtypeStruct((t, D_MODEL), F32),
                   jax.ShapeDtypeStruct((1, SLAB), F32), jax.ShapeDtypeStruct((1, D_MODEL), F32),
                   jax.ShapeDtypeStruct((1, D_MODEL), F32)],
        compiler_params=_params(("arbitrary",)),
    )(x1, h2, target, w_up, w_down, g3, g4)


def _inproj_bwd(dgates, dqa, dka, dva, dqb_t, dkb_t, dvb_t, cq, ckv, x, dx1, rope_ct, rope_s1t, rope_s2t,
                g1, g_q, g_kv, w_in, w_qb, w_kvb):
    t = x.shape[0]
    tm = _token_tile(t)

    def body(dgates_ref, dqa_ref, dka_ref, dva_ref, dqt_ref, dkt_ref, dvt_ref, cq_ref, ckv_ref, x_ref, dx1_ref,
             ct_ref, s1t_ref, s2t_ref, g1_ref, gq_ref, gkv_ref, win_ref, wqb_ref, wkvb_ref,
             dproj_ref, dqbrt_ref, dkvbt_ref, dx_ref, dg1_ref, dgq_ref, dgkv_ref):
        ct, s1t, s2t = ct_ref[...], s1t_ref[...], s2t_ref[...]
        dk_sum_t = jnp.zeros((SLAB, tm), F32)
        for hd in range(N_HEADS):
            sl = slice(hd * SLAB, (hd + 1) * SLAB)
            dqbrt_ref[sl, :] = _rope_t_bwd(dqt_ref[sl, :] * SCALE_B, ct, s1t, s2t).astype(BF16)
            dk_sum_t += dkt_ref[sl, :].astype(F32)
        dkvbt_ref[0:HM, :] = dkt_ref[...]
        dkvbt_ref[HM:2 * HM, :] = dvt_ref[...]
        dkr = _rope_t_bwd(dk_sum_t, ct, s1t, s2t).T
        dcqn = _dot(wqb_ref[...], dqbrt_ref[...]).T
        cq = cq_ref[...]
        rq = _rms_r(cq)
        _acc_rows(dgq_ref, dcqn * cq * rq)
        dcq = _rms_bwd(cq, rq, gq_ref[...], dcqn)
        dckvn = _dot(wkvb_ref[...], dkvbt_ref[...]).T
        ckv = ckv_ref[...]
        rkv = _rms_r(ckv)
        _acc_rows(dgkv_ref, dckvn * ckv * rkv)
        dckv = _rms_bwd(ckv, rkv, gkv_ref[...], dckvn)
        dproj_ref[:, C_GATES:C_QA] = dgates_ref[...]
        dproj_ref[:, C_QA:C_KA] = dqa_ref[...]
        dproj_ref[:, C_KA:C_VA] = dka_ref[...]
        dproj_ref[:, C_VA:C_CQ] = dva_ref[...]
        dproj_ref[:, C_CQ:C_CKV] = dcq.astype(BF16)
        dproj_ref[:, C_CKV:C_KR] = dckv.astype(BF16)
        dproj_ref[:, C_KR:D_IN_PAD] = dkr.astype(BF16)
        dh = _dot_nt(dproj_ref[...], win_ref[...])
        xv = x_ref[...]
        r1 = _rms_r(xv)
        _acc_rows(dg1_ref, dh * xv * r1)
        dx_ref[...] = dx1_ref[...] + _rms_bwd(xv, r1, g1_ref[...], dh)

    kvw = N_KV_A * SLAB
    row = _row_spec(tm, D_MODEL)
    hm = _row_spec(tm, HM)
    hmt = _col_spec(HM, tm)
    tab = _col_spec(SLAB, tm)
    return pl.pallas_call(
        body, name="inproj_bwd", grid=(t // tm,),
        in_specs=[_row_spec(tm, 2 * D_MODEL), hm, _row_spec(tm, kvw), _row_spec(tm, kvw), hmt, hmt, hmt,
                  _row_spec(tm, Q_LORA), _row_spec(tm, KV_LORA), row, row, tab, tab, tab,
                  _full_spec((1, D_MODEL)), _full_spec((1, Q_LORA)), _full_spec((1, KV_LORA)),
                  _full_spec((D_MODEL, D_IN_PAD)), _full_spec((Q_LORA, HM)), _full_spec((KV_LORA, 2 * HM))],
        out_specs=[_row_spec(tm, D_IN_PAD), hmt, _col_spec(2 * HM, tm), row,
                   _full_spec((1, D_MODEL)), _full_spec((1, Q_LORA)), _full_spec((1, KV_LORA))],
        out_shape=[jax.ShapeDtypeStruct((t, D_IN_PAD), BF16), jax.ShapeDtypeStruct((HM, t), BF16),
                   jax.ShapeDtypeStruct((2 * HM, t), BF16), jax.ShapeDtypeStruct((t, D_MODEL), F32),
                   jax.ShapeDtypeStruct((1, D_MODEL), F32), jax.ShapeDtypeStruct((1, Q_LORA), F32),
                   jax.ShapeDtypeStruct((1, KV_LORA), F32)],
        compiler_params=_params(("arbitrary",)),
    )(dgates, dqa, dka, dva, dqb_t, dkb_t, dvb_t, cq, ckv, x, dx1, rope_ct, rope_s1t, rope_s2t,
      g1, g_q, g_kv, w_in, w_qb, w_kvb)


def _matmul_tn(a, b, name, out_dtype=F32, n_shards=1):
    t, k = a.shape
    n = b.shape[1]
    bt = min(t, 512)
    bk = min(k, 1024)
    bn = min(n, 1024)
    ns = n // n_shards
    per_block = bn // ns
    steps = t // bt

    def body(a_ref, b_ref, o_ref, acc):
        s = pl.program_id(2)

        @pl.when(s == 0)
        def _():
            acc[...] = jnp.zeros_like(acc)

        acc[...] += _dot_tn(a_ref[...], b_ref[...])

        @pl.when(s == steps - 1)
        def _():
            if n_shards > 1:
                for p in range(per_block):
                    o_ref[p] = acc[:, p * ns:(p + 1) * ns].astype(out_dtype)
            else:
                o_ref[...] = acc[...].astype(out_dtype)

    if n_shards > 1:
        out_spec = pl.BlockSpec((per_block, bk, ns), lambda i, j, s: (j, i, 0))
        out_shape = jax.ShapeDtypeStruct((n_shards, k, ns), out_dtype)
    else:
        out_spec = pl.BlockSpec((bk, bn), lambda i, j, s: (i, j))
        out_shape = jax.ShapeDtypeStruct((k, n), out_dtype)
    return pl.pallas_call(
        body, name=name, grid=(k // bk, n // bn, steps),
        in_specs=[pl.BlockSpec((bt, bk), lambda i, j, s: (s, i)), pl.BlockSpec((bt, bn), lambda i, j, s: (s, j))],
        out_specs=out_spec, out_shape=out_shape, scratch_shapes=[pltpu.VMEM((bk, bn), F32)],
        compiler_params=_params(("parallel", "parallel", "arbitrary")),
    )(a, b)


def _matmul_nn(a_t, b, name):
    m, t = a_t.shape
    n = b.shape[1]
    bt = min(t, 512)
    steps = t // bt

    def body(a_ref, b_ref, o_ref, acc):
        s = pl.program_id(0)

        @pl.when(s == 0)
        def _():
            acc[...] = jnp.zeros_like(acc)

        acc[...] += _dot(a_ref[...], b_ref[...])

        @pl.when(s == steps - 1)
        def _():
            o_ref[...] = acc[...]

    return pl.pallas_call(
        body, name=name, grid=(steps,),
        in_specs=[pl.BlockSpec((m, bt), lambda s: (0, s)), pl.BlockSpec((bt, n), lambda s: (s, 0))],
        out_specs=pl.BlockSpec((m, n), lambda s: (0, 0)), out_shape=jax.ShapeDtypeStruct((m, n), F32),
        scratch_shapes=[pltpu.VMEM((m, n), F32)],
        compiler_params=_params(("arbitrary",)),
    )(a_t, b)


def _all_gather(shards):
    n = len(shards)

    def body(*refs):
        srcs, dsts = refs[:n], refs[n:2 * n]
        send_sems, recv_sems, local_sems = refs[2 * n:]
        x, y, c = _mesh_pos()
        me, sibling = (x, y, c), (x, y, 1 - c)
        chips = [(1 - x, y), (x, 1 - y), (1 - x, 1 - y)]

        def slot(a, px, py, pc):
            return dsts[a].at[4 * px + 2 * py + pc]

        def copy(a, k, block, to, src=None):
            return pltpu.make_async_remote_copy(
                src_ref=slot(a, *block) if src is None else src, dst_ref=slot(a, *block),
                send_sem=send_sems.at[(N_DEV - 1) * a + k], recv_sem=recv_sems.at[(N_DEV - 1) * a + k],
                device_id=to, device_id_type=pl.DeviceIdType.MESH)

        mine = [pltpu.make_async_copy(srcs[a], slot(a, *me), local_sems.at[a]) for a in range(n)]
        first = []
        for a in range(n):
            first.append(copy(a, 0, me, sibling, src=srcs[a]))
            first += [copy(a, 1 + j, me, (*chip, c), src=srcs[a]) for j, chip in enumerate(chips)]
        for cp in mine + first:
            cp.start()
        passed = []
        for j, chip in enumerate(chips):
            for a in range(n):
                copy(a, 1 + j, (*chip, c), me).wait_recv()
                passed.append(copy(a, 4 + j, (*chip, c), sibling))
                passed[-1].start()
        for a in range(n):
            copy(a, 0, sibling, me).wait_recv()
        for j, chip in enumerate(chips):
            for a in range(n):
                copy(a, 4 + j, (*chip, 1 - c), me).wait_recv()
        for cp in first + passed:
            cp.wait_send()
        for cp in mine:
            cp.wait()

    return pl.pallas_call(
        body, name="all_gather_early",
        out_shape=[jax.ShapeDtypeStruct((N_DEV,) + a.shape, a.dtype) for a in shards],
        in_specs=[ANY_SPEC] * n, out_specs=[ANY_SPEC] * n, scratch_shapes=_exchange_scratch(n),
    )(*shards)


def _exchange_grads(slices, small):
    n = len(slices)

    def body(*refs):
        srcs, s_ref = refs[:n], refs[n]
        dsts, s_dst = refs[n + 1:2 * n + 1], refs[2 * n + 1]
        sems = refs[2 * n + 2:]
        parts = _direct_copies(srcs, dsts, *sems, False)
        smalls = _direct_copies([s_ref], [s_dst], *sems, True, sem_base=n)
        _start_copies(*parts)
        _start_copies(*smalls)
        _wait_copies(*parts)
        _wait_copies(*smalls)

    outs = pl.pallas_call(
        body, name="exchange_grads",
        out_shape=[jax.ShapeDtypeStruct(a.shape, a.dtype) for a in slices]
        + [jax.ShapeDtypeStruct((N_DEV,) + small.shape, small.dtype)],
        in_specs=[ANY_SPEC] * (n + 1), out_specs=[ANY_SPEC] * (n + 1), scratch_shapes=_exchange_scratch(n + 1),
    )(*slices, small)
    return list(outs[:n]), outs[n]


def _adamw(parts, w, m, v, name):
    _, k, n = parts.shape
    bk = min(k, ADAM_ROWS)
    c1 = 1.0 - ADAM_B1 ** ADAM_STEP
    c2 = 1.0 - ADAM_B2 ** ADAM_STEP

    def body(p_ref, w_ref, m_ref, v_ref, g_ref, d_ref, mo_ref, vo_ref):
        g = p_ref[0].astype(F32)
        for s in range(1, N_DEV):
            g = g + p_ref[s].astype(F32)
        g_ref[0] = g
        m_new = ADAM_B1 * m_ref[0] + (1.0 - ADAM_B1) * g
        v_new = ADAM_B2 * v_ref[0] + (1.0 - ADAM_B2) * (g * g)
        mo_ref[0] = m_new
        vo_ref[0] = v_new
        m_hat = m_new / c1
        v_hat = v_new / c2
        d_ref[0] = -ADAM_LR * (m_hat / (jnp.sqrt(v_hat) + ADAM_EPS) + ADAM_WD * w_ref[0])

    blk = pl.BlockSpec((1, bk, n), lambda i: (0, i, 0))
    out = jax.ShapeDtypeStruct((1, k, n), F32)
    return pl.pallas_call(
        body, name=name, grid=(k // bk,),
        in_specs=[pl.BlockSpec((N_DEV, bk, n), lambda i: (0, i, 0)), blk, blk, blk],
        out_specs=[blk] * 4, out_shape=[out] * 4,
        compiler_params=_params(("parallel",)),
    )(parts, w, m, v)


def _pad_heads_cols(w, heads, width):
    k = w.shape[0]
    w = w.reshape(k, heads, width)
    return jnp.pad(w, ((0, 0), (0, 0), (0, SLAB - width))).reshape(k, heads * SLAB)


def _unpad_heads_cols(w, heads, width):
    k = w.shape[0]
    return w.reshape(k, heads, SLAB)[:, :, :width].reshape(k, heads * width)


def _pad_heads_rows(w, heads, width):
    n = w.shape[1]
    w = w.reshape(heads, width, n)
    return jnp.pad(w, ((0, 0), (0, SLAB - width), (0, 0))).reshape(heads * SLAB, n)


def _unpad_heads_rows(w, heads, width):
    n = w.shape[1]
    return w.reshape(heads, SLAB, n)[:, :width, :].reshape(heads * width, n)


def _pad_w_in(w_in):
    o = 2 * D_MODEL
    qa = _pad_heads_cols(w_in[:, o:o + 512], N_HEADS, HEAD_A)
    ka = _pad_heads_cols(w_in[:, o + 512:o + 640], N_KV_A, HEAD_A)
    va = _pad_heads_cols(w_in[:, o + 640:o + 768], N_KV_A, HEAD_A)
    kr = jnp.pad(w_in[:, o + 1152:o + 1184], ((0, 0), (QK_NOPE, SLAB - QK_NOPE - QK_ROPE)))
    return jnp.concatenate([w_in[:, :o], qa, ka, va, w_in[:, o + 768:o + 1152], kr], axis=1)


def _unpad_w_in(w):
    qa = _unpad_heads_cols(w[:, C_QA:C_KA], N_HEADS, HEAD_A)
    ka = _unpad_heads_cols(w[:, C_KA:C_VA], N_KV_A, HEAD_A)
    va = _unpad_heads_cols(w[:, C_VA:C_CQ], N_KV_A, HEAD_A)
    kr = w[:, C_KR + QK_NOPE:C_KR + QK_NOPE + QK_ROPE]
    return jnp.concatenate([w[:, :C_QA], qa, ka, va, w[:, C_CQ:C_KR], kr], axis=1)


def _pad_w_kvb(w_kvb):
    w = w_kvb.reshape(KV_LORA, N_HEADS, QK_NOPE + V_DIM_B)
    k = jnp.pad(w[:, :, :QK_NOPE], ((0, 0), (0, 0), (0, SLAB - QK_NOPE))).reshape(KV_LORA, HM)
    v = jnp.pad(w[:, :, QK_NOPE:], ((0, 0), (0, 0), (0, SLAB - V_DIM_B))).reshape(KV_LORA, HM)
    return jnp.concatenate([k, v], axis=1)


def _unpad_w_kvb(w):
    k = w[:, :HM].reshape(KV_LORA, N_HEADS, SLAB)[:, :, :QK_NOPE]
    v = w[:, HM:].reshape(KV_LORA, N_HEADS, SLAB)[:, :, :V_DIM_B]
    return jnp.concatenate([k, v], axis=2).reshape(KV_LORA, N_HEADS * (QK_NOPE + V_DIM_B))


def _col_shards(w):
    k, n = w.shape
    return w.reshape(k, N_DEV, n // N_DEV).transpose(1, 0, 2)


def _from_col_shards(s):
    _, k, ns = s.shape
    return s.transpose(1, 0, 2).reshape(k, N_DEV * ns)


def _freq_row():
    freqs = ROPE_THETA ** (-jnp.arange(0, QK_ROPE, 2, dtype=F32) / QK_ROPE)
    return jnp.concatenate([jnp.zeros((QK_NOPE,), F32), freqs, freqs,
                            jnp.zeros((SLAB - QK_NOPE - QK_ROPE,), F32)]).reshape(1, SLAB)


SMALL_D_ROWS = ("pre_norm_mix", "post_norm_mix", "pre_norm_mlp", "post_norm_mlp")
SMALL_Q_OFF, SMALL_KV_OFF, SMALL_SINK_OFF, SMALL_LOSS_OFF = 0, 256, 384, 392


def _pack_small(vals):
    row4 = jnp.concatenate([vals["q_a_norm"].reshape(-1), vals["kv_a_norm"].reshape(-1), vals["sinks"].reshape(-1),
                            vals["loss"].reshape(-1), jnp.zeros((1024 - 393,), F32)])
    rows = [vals[n].reshape(1024) for n in SMALL_D_ROWS] + [row4]
    return jnp.concatenate([jnp.stack(rows), jnp.zeros((SMALL_ROWS - 5, 1024), F32)], axis=0)


def _unpack_small(blk):
    out = {n: blk[i].reshape(1, 1024) for i, n in enumerate(SMALL_D_ROWS)}
    out["q_a_norm"] = blk[4, SMALL_Q_OFF:SMALL_Q_OFF + 256].reshape(1, 256)
    out["kv_a_norm"] = blk[4, SMALL_KV_OFF:SMALL_KV_OFF + 128].reshape(1, 128)
    out["sinks"] = blk[4, SMALL_SINK_OFF:SMALL_SINK_OFF + 8].reshape(1, 8)
    out["loss"] = blk[4, SMALL_LOSS_OFF]
    return out


WEIGHT_ORDER = ("pre_norm_mix", "w_in", "q_a_norm", "w_q_b", "kv_a_norm", "w_kv_b", "sinks", "w_o_a", "w_o_b",
                "w_out", "post_norm_mix", "pre_norm_mlp", "w_up", "w_down", "post_norm_mlp")
SMALL_NAMES = ("pre_norm_mix", "q_a_norm", "kv_a_norm", "sinks", "post_norm_mix", "pre_norm_mlp", "post_norm_mlp")


def kernel(x, positions, pre_norm_mix, w_in, q_a_norm, w_q_b, kv_a_norm, w_kv_b, sinks, w_o_a, w_o_b, w_out, post_norm_mix, pre_norm_mlp, w_up, w_down, post_norm_mlp, loss_target, m_pre_norm_mix, m_w_in, m_q_a_norm, m_w_q_b, m_kv_a_norm, m_w_kv_b, m_sinks, m_w_o_a, m_w_o_b, m_w_out, m_post_norm_mix, m_pre_norm_mlp, m_w_up, m_w_down, m_post_norm_mlp, v_pre_norm_mix, v_w_in, v_q_a_norm, v_w_q_b, v_kv_a_norm, v_w_kv_b, v_sinks, v_w_o_a, v_w_o_b, v_w_out, v_post_norm_mix, v_pre_norm_mlp, v_w_up, v_w_down, v_post_norm_mlp):
    weights = dict(pre_norm_mix=pre_norm_mix, w_in=w_in, q_a_norm=q_a_norm, w_q_b=w_q_b, kv_a_norm=kv_a_norm,
                   w_kv_b=w_kv_b, sinks=sinks, w_o_a=w_o_a, w_o_b=w_o_b, w_out=w_out, post_norm_mix=post_norm_mix,
                   pre_norm_mlp=pre_norm_mlp, w_up=w_up, w_down=w_down, post_norm_mlp=post_norm_mlp)
    m_in = dict(pre_norm_mix=m_pre_norm_mix, w_in=m_w_in, q_a_norm=m_q_a_norm, w_q_b=m_w_q_b, kv_a_norm=m_kv_a_norm,
                w_kv_b=m_w_kv_b, sinks=m_sinks, w_o_a=m_w_o_a, w_o_b=m_w_o_b, w_out=m_w_out,
                post_norm_mix=m_post_norm_mix, pre_norm_mlp=m_pre_norm_mlp, w_up=m_w_up, w_down=m_w_down,
                post_norm_mlp=m_post_norm_mlp)
    v_in = dict(pre_norm_mix=v_pre_norm_mix, w_in=v_w_in, q_a_norm=v_q_a_norm, w_q_b=v_w_q_b, kv_a_norm=v_kv_a_norm,
                w_kv_b=v_w_kv_b, sinks=v_sinks, w_o_a=v_w_o_a, w_o_b=v_w_o_b, w_out=v_w_out,
                post_norm_mix=v_post_norm_mix, pre_norm_mlp=v_pre_norm_mlp, w_up=v_w_up, w_down=v_w_down,
                post_norm_mlp=v_post_norm_mlp)

    xs, pos, target = x[0], positions[0], loss_target[0]
    t = xs.shape[0]
    pos_col = pos.reshape(t, 1)
    pos_row = pos.reshape(1, t)
    g1, g2, g3, g4 = (weights[n] for n in SMALL_D_ROWS)
    g_q, g_kv = q_a_norm, kv_a_norm
    sink_vec = sinks.reshape(N_HEADS)
    shard = {n: weights[n][0].astype(BF16) for n in EARLY + LATE}

    e_in, e_qb, e_kvb = _all_gather([shard[n] for n in EARLY])
    w_in_p = _pad_w_in(_from_col_shards(e_in))
    w_qb = _pad_heads_cols(_from_col_shards(e_qb), N_HEADS, QK_NOPE + QK_ROPE)
    w_kvb = _pad_w_kvb(_from_col_shards(e_kvb))

    tables = _rope_tables(pos_col, pos_row, _freq_row())
    (h, gates, qa, ka, va, cq, ckv, cqn, ckvn, kb, vb, qt, kt, vt) = _inproj_fwd(
        xs, g1, w_in_p, g_q, g_kv, w_kvb, w_qb.T, w_kvb[:, :HM].T, w_kvb[:, HM:].T, w_in_p[:, C_KR:].T, tables)
    out_a, lse_a = _swa_fwd(qa, ka, va, pos_col, pos_row, sink_vec)
    out_b, out_b_t, lse_b, (l_oa, l_ob, l_out, w_up_s, l_down) = _mla_fwd(qt, kb, vt, [shard[n] for n in LATE])
    w_oa = _pad_heads_rows(_from_col_shards(l_oa), N_HEADS, HEAD_A)
    w_ob = _pad_heads_rows(_from_col_shards(l_ob), N_HEADS, V_DIM_B)
    w_out_f = l_out.reshape(D_MODEL, D_MODEL)
    w_down_f = l_down.reshape(D_FF, D_MODEL)

    oa_p, ob_p, merged, y, x1, h2 = _merge_fwd(out_a, out_b, gates, xs, w_oa, w_ob, w_out_f, g2, g3)
    a, du, dy2, dx1, loss, dg3, dg4 = _mlp_fwd_bwd(x1, h2, target, w_up_s, w_down_f, g3, g4)
    (dy, d_oap, d_obp, dgates, d_oa, d_ob_t, delta_b, dg2) = _merge_bwd(
        dx1, y, gates, oa_p, ob_p, out_b_t, w_oa, w_ob, w_out_f, g2)
    late_slices = [
        _col_shards(_unpad_heads_rows(_matmul_tn(out_a, d_oap, "dw_o_a"), N_HEADS, HEAD_A)).astype(BF16),
        _col_shards(_unpad_heads_rows(_matmul_tn(out_b, d_obp, "dw_o_b"), N_HEADS, V_DIM_B)).astype(BF16),
        _matmul_tn(merged, dy, "dw_out", BF16).reshape(N_DEV, D_MODEL // N_DEV, D_MODEL),
        _matmul_tn(h2, du, "dw_up", BF16, N_DEV),
        _matmul_tn(a, dy2, "dw_down", BF16).reshape(N_DEV, D_FF // N_DEV, D_MODEL),
    ]
    dqa, dka, dva, dsink = _swa_bwd(qa, ka, va, out_a, d_oa, lse_a, pos_col, pos_row, sink_vec)
    dqb_t, dkb_t, dvb_t, late_parts = _mla_bwd(qt, kb, kt, vb, d_ob_t, lse_b, delta_b, late_slices)
    dproj, dqbr_t, dkvb_t, dx, dg1, dgq, dgkv = _inproj_bwd(
        dgates, dqa, dka, dva, dqb_t, dkb_t, dvb_t, cq, ckv, xs, dx1, *tables[3:], g1, g_q, g_kv,
        w_in_p, w_qb, w_kvb)
    early_slices = [
        _col_shards(_unpad_w_in(_matmul_tn(h, dproj, "dw_in"))).astype(BF16),
        _col_shards(_unpad_heads_cols(_matmul_nn(dqbr_t, cqn, "dw_q_b").T, N_HEADS, QK_NOPE + QK_ROPE)).astype(BF16),
        _col_shards(_unpad_w_kvb(_matmul_nn(dkvb_t, ckvn, "dw_kv_b").T)).astype(BF16),
    ]
    small_grads = {"pre_norm_mix": dg1, "post_norm_mix": dg2, "pre_norm_mlp": dg3, "post_norm_mlp": dg4,
                   "q_a_norm": dgq, "kv_a_norm": dgkv, "sinks": dsink.reshape(N_HEADS, BLOCK).sum(axis=1),
                   "loss": loss[0, 0:1]}
    early_parts, s_parts = _exchange_grads(early_slices, _pack_small(small_grads))

    updates = {}
    for name, parts in zip(EARLY + LATE, early_parts + late_parts):
        outs = _adamw(parts, weights[name], m_in[name], v_in[name], "adamw_" + name)
        for kind, arr in zip(("g", "d", "m", "v"), outs):
            updates[kind, name] = arr
    zero = jnp.zeros((), F32)
    pack = lambda src: _pack_small({**{n: src[n] for n in SMALL_NAMES}, "loss": zero})[None]
    smalls = _adamw(s_parts, pack(weights), pack(m_in), pack(v_in), "adamw_small")
    for kind, blk in zip(("g", "d", "m", "v"), smalls):
        for wname, piece in _unpack_small(blk[0]).items():
            updates[kind, wname] = piece
    results = [updates[kind, name] for kind in ("g", "d", "m", "v") for name in WEIGHT_ORDER]
    return (updates["g", "loss"], dx[None], *results)
```

```python
import functools

import numpy as np
import jax
import jax.numpy as jnp
from jax import lax
from jax.experimental import pallas as pl
from jax.experimental.pallas import tpu as pltpu

F32 = jnp.float32
BF16 = jnp.bfloat16

D_MODEL = 1024
D_FF = 4096
N_HEADS = 8
N_KV_A = 2
GROUP_A = N_HEADS // N_KV_A
HEAD_A = 64
QK_NOPE = 64
QK_ROPE = 32
V_DIM_B = 64
Q_LORA = 256
KV_LORA = 128
BLOCK = 128
SLAB = 128
ROPE_THETA = 10000.0
EPS = 1e-6
N_DEV = 8
NEG = -1e30

SCALE_A = HEAD_A ** -0.5
SCALE_B = (QK_NOPE + QK_ROPE) ** -0.5
LOG2E = 1.4426950408889634
SCORE_B = SCALE_B * LOG2E
MLA_HEADS_PER_STEP = 4
MLA_FWD_HEADS_PER_STEP = 8
ONES_ROWS = 16
SLOPES_A = tuple(2.0 ** (-8.0 * (h + 1) / N_HEADS) for h in range(N_HEADS))

ADAM_LR = 0.001
ADAM_B1 = 0.9
ADAM_B2 = 0.999
ADAM_EPS = 1e-08
ADAM_WD = 0.01
ADAM_STEP = 10

HM = N_HEADS * SLAB
C_GATES = 0
C_QA = 2 * D_MODEL
C_KA = C_QA + HM
C_VA = C_KA + N_KV_A * SLAB
C_CQ = C_VA + N_KV_A * SLAB
C_CKV = C_CQ + Q_LORA
C_KR = C_CKV + KV_LORA
D_IN_PAD = C_KR + SLAB

VMEM_LIMIT = 56 * 1024 * 1024

EARLY = ("w_in", "w_q_b", "w_kv_b")
LATE = ("w_o_a", "w_o_b", "w_out", "w_up", "w_down")
ADAM_ROWS = 256
SMALL_ROWS = 8


def _token_tile(t):
    return min(256, t)


def _attn_tile(t):
    return 512 if t >= 2048 else 128


def _params(sem, vmem=VMEM_LIMIT):
    return pltpu.CompilerParams(dimension_semantics=sem, vmem_limit_bytes=vmem)


def _dot(a, b):
    return jnp.dot(a, b, preferred_element_type=F32)


def _dot_nt(a, b):
    return lax.dot_general(a, b, (((1,), (1,)), ((), ())), preferred_element_type=F32)


def _dot_tn(a, b):
    return lax.dot_general(a, b, (((0,), (0,)), ((), ())), preferred_element_type=F32)


def _rms_r(x):
    return lax.rsqrt(jnp.mean(x * x, axis=-1, keepdims=True) + EPS)


def _rms_bwd(x, r, g, dy):
    t = dy * g
    return r * t - x * (r * r * r) * jnp.mean(x * t, axis=-1, keepdims=True)


def _sigmoid(x):
    return 1.0 / (1.0 + jnp.exp(-x))


def _rope(x, c, s1, s2):
    return x * c + pltpu.roll(x, SLAB - 16, 1) * s1 + pltpu.roll(x, 16, 1) * s2


def _rope_bwd(d, c, s1, s2):
    return d * c + pltpu.roll(d * s1, 16, 1) + pltpu.roll(d * s2, SLAB - 16, 1)


def _roll_rows(x, shift):
    return jnp.concatenate([x[-shift:], x[:-shift]], axis=0)


def _rope_t(x, c, s1, s2):
    return x * c + _roll_rows(x, SLAB - 16) * s1 + _roll_rows(x, 16) * s2


def _rope_t_bwd(d, c, s1, s2):
    return d * c + _roll_rows(d * s1, 16) + _roll_rows(d * s2, SLAB - 16)


def _row_spec(tm, n):
    return pl.BlockSpec((tm, n), lambda i: (i, 0))


def _col_spec(n, tm):
    return pl.BlockSpec((n, tm), lambda i: (0, i))


def _full_spec(shape):
    nd = len(shape)
    return pl.BlockSpec(shape, lambda i: (0,) * nd, pipeline_mode=pl.Buffered(1))


def _acc_rows(ref, val):
    @pl.when(pl.program_id(0) == 0)
    def _():
        ref[...] = jnp.zeros_like(ref)
    ref[...] += jnp.sum(val, axis=0, keepdims=True)


def _rope_tables(pos_col, pos_row, freq_row):
    t = pos_col.shape[0]
    tm = _token_tile(t)

    def tables(ang, idx):
        s = jnp.sin(ang)
        return (jnp.cos(ang), jnp.where((idx >= 64) & (idx < 80), -s, 0.0),
                jnp.where((idx >= 80) & (idx < 96), s, 0.0))

    def body(pos_ref, posr_ref, f_ref, fc_ref, c_ref, s1_ref, s2_ref, ct_ref, s1t_ref, s2t_ref):
        ang = pos_ref[...].astype(F32) * f_ref[...]
        c_ref[...], s1_ref[...], s2_ref[...] = tables(ang, lax.broadcasted_iota(jnp.int32, ang.shape, 1))
        angt = posr_ref[...].astype(F32) * fc_ref[...]
        ct_ref[...], s1t_ref[...], s2t_ref[...] = tables(angt, lax.broadcasted_iota(jnp.int32, angt.shape, 0))

    tab = jax.ShapeDtypeStruct((t, SLAB), F32)
    tabt = jax.ShapeDtypeStruct((SLAB, t), F32)
    return pl.pallas_call(
        body, name="rope_tables", grid=(t // tm,),
        in_specs=[_row_spec(tm, 1), _col_spec(1, tm), _full_spec((1, SLAB)), _full_spec((SLAB, 1))],
        out_specs=[_row_spec(tm, SLAB)] * 3 + [_col_spec(SLAB, tm)] * 3, out_shape=[tab] * 3 + [tabt] * 3,
        compiler_params=_params(("parallel",)),
    )(pos_col, pos_row, freq_row, freq_row.reshape(SLAB, 1))


def _inproj_fwd(x, g1, w_in, g_q, g_kv, w_kvb, w_qb_t, w_kb_t, w_vb_t, w_kr_t, tables):
    t = x.shape[0]
    tm = _token_tile(t)

    def body(x_ref, g1_ref, win_ref, gq_ref, gkv_ref, wkvb_ref, wqbt_ref, wkbt_ref, wvbt_ref, wkrt_ref,
             c_ref, s1_ref, s2_ref, ct_ref, s1t_ref, s2t_ref,
             h_ref, gates_ref, qa_ref, ka_ref, va_ref, cq_ref, ckv_ref, cqn_ref, ckvn_ref,
             kb_ref, vb_ref, qt_ref, kt_ref, vt_ref):
        xv = x_ref[...]
        h = (xv * _rms_r(xv) * g1_ref[...]).astype(BF16)
        h_ref[...] = h
        proj = _dot(h, win_ref[...])
        gates_ref[...] = proj[:, C_GATES:C_QA]
        qa_ref[...] = proj[:, C_QA:C_KA].astype(BF16)
        ka_ref[...] = proj[:, C_KA:C_VA].astype(BF16)
        va_ref[...] = proj[:, C_VA:C_CQ].astype(BF16)
        cq = proj[:, C_CQ:C_CKV]
        ckv = proj[:, C_CKV:C_KR]
        kr = proj[:, C_KR:D_IN_PAD]
        cq_ref[...] = cq
        ckv_ref[...] = ckv
        cqn = (cq * _rms_r(cq) * gq_ref[...]).astype(BF16)
        ckvn = (ckv * _rms_r(ckv) * gkv_ref[...]).astype(BF16)
        cqn_ref[...] = cqn
        ckvn_ref[...] = ckvn
        c, s1, s2 = c_ref[...], s1_ref[...], s2_ref[...]
        kvb = _dot(ckvn, wkvb_ref[...])
        kr_rot = _rope(kr, c, s1, s2)
        ct, s1t, s2t = ct_ref[...], s1t_ref[...], s2t_ref[...]
        q_t = _dot_nt(wqbt_ref[...], cqn)
        k_t = _dot_nt(wkbt_ref[...], ckvn)
        kr_t = _rope_t(_dot_nt(wkrt_ref[...], h), ct, s1t, s2t)
        for hd in range(N_HEADS):
            sl = slice(hd * SLAB, (hd + 1) * SLAB)
            kb_ref[:, sl] = (kvb[:, sl] + kr_rot).astype(BF16)
            qt_ref[sl, :] = (_rope_t(q_t[sl, :], ct, s1t, s2t) * SCORE_B).astype(BF16)
            kt_ref[sl, :] = (k_t[sl, :] + kr_t).astype(BF16)
        vb_ref[...] = kvb[:, HM:2 * HM].astype(BF16)
        pad_row = lax.broadcasted_iota(jnp.int32, (HM, 1), 0) & (SLAB - 1)
        ones_rows = jnp.where((pad_row >= V_DIM_B) & (pad_row < V_DIM_B + ONES_ROWS), 1.0, 0.0)
        vt_ref[...] = (_dot_nt(wvbt_ref[...], ckvn) + ones_rows).astype(BF16)

    def sds(n, dt):
        return jax.ShapeDtypeStruct((t, n), dt)

    outs = [(D_MODEL, BF16), (2 * D_MODEL, F32), (HM, BF16), (N_KV_A * SLAB, BF16), (N_KV_A * SLAB, BF16),
            (Q_LORA, F32), (KV_LORA, F32), (Q_LORA, BF16), (KV_LORA, BF16), (HM, BF16), (HM, BF16)]
    tab, tabt = _row_spec(tm, SLAB), _col_spec(SLAB, tm)
    return pl.pallas_call(
        body, name="inproj_fwd", grid=(t // tm,),
        in_specs=[_row_spec(tm, D_MODEL), _full_spec((1, D_MODEL)), _full_spec((D_MODEL, D_IN_PAD)),
                  _full_spec((1, Q_LORA)), _full_spec((1, KV_LORA)), _full_spec((KV_LORA, 2 * HM)),
                  _full_spec((HM, Q_LORA)), _full_spec((HM, KV_LORA)), _full_spec((HM, KV_LORA)),
                  _full_spec((SLAB, D_MODEL)), tab, tab, tab, tabt, tabt, tabt],
        out_specs=[_row_spec(tm, n) for n, _ in outs] + [_col_spec(HM, tm)] * 3,
        out_shape=[sds(n, dt) for n, dt in outs] + [jax.ShapeDtypeStruct((HM, t), BF16)] * 3,
        compiler_params=_params(("parallel",)),
    )(x, g1, w_in, g_q, g_kv, w_kvb, w_qb_t, w_kb_t, w_vb_t, w_kr_t, *tables)


def _tile_group(a):
    return jnp.concatenate([a] * GROUP_A, axis=1)


def _swa_masks():
    row = lax.broadcasted_iota(jnp.int32, (BLOCK, GROUP_A * BLOCK), 0)
    col = lax.broadcasted_iota(jnp.int32, (BLOCK, GROUP_A * BLOCK), 1) & (BLOCK - 1)
    return row <= col, row > col


def _heads_beside(ref, g):
    return jnp.concatenate([ref[:, (g * GROUP_A + hh) * SLAB:(g * GROUP_A + hh + 1) * SLAB].T
                            for hh in range(GROUP_A)], axis=1)


def _rows_beside(ref, g):
    return jnp.concatenate([ref[g * GROUP_A + hh] for hh in range(GROUP_A)], axis=1)


def _swa_rows(sinks):
    slopes = jnp.repeat(jnp.asarray(SLOPES_A, F32).reshape(N_KV_A, GROUP_A, 1), BLOCK, axis=2)
    sink_rows = jnp.repeat(sinks.reshape(N_KV_A, GROUP_A, 1), BLOCK, axis=2)
    return slopes.reshape(N_KV_A, 1, GROUP_A * BLOCK), sink_rows.reshape(N_KV_A, 1, GROUP_A * BLOCK)


def _swa_fwd(qa, ka, va, pos_col, pos_row, sinks):
    t = qa.shape[0]
    nb = t // BLOCK
    gw = GROUP_A * BLOCK
    slope_rows, sink_rows = _swa_rows(sinks)

    def body(q_ref, kc_ref, kp_ref, vc_ref, vp_ref, pkc_ref, pkp_ref, pq_ref, slope_ref, sink_ref, o_ref, l_ref):
        i = pl.program_id(0)
        pq = pq_ref[...]
        dist_c = _tile_group(jnp.abs(pkc_ref[...] - pq).astype(F32))
        dist_p = _tile_group(jnp.abs(pkp_ref[...] - pq).astype(F32))
        mask_c, older = _swa_masks()
        mask_p = jnp.logical_and(older, i > 0)
        for g in range(N_KV_A):
            gs = slice(g * SLAB, (g + 1) * SLAB)
            x = _heads_beside(q_ref, g)
            slope, sink = slope_ref[g], sink_ref[g]
            s_c = jnp.where(mask_c, _dot(kc_ref[:, gs], x) * SCALE_A - slope * dist_c, NEG)
            s_p = jnp.where(mask_p, _dot(kp_ref[:, gs], x) * SCALE_A - slope * dist_p, NEG)
            m = jnp.maximum(jnp.maximum(jnp.max(s_c, axis=0, keepdims=True),
                                        jnp.max(s_p, axis=0, keepdims=True)), sink)
            e_c = jnp.exp(s_c - m)
            e_p = jnp.exp(s_p - m)
            den = jnp.sum(e_c, axis=0, keepdims=True) + jnp.sum(e_p, axis=0, keepdims=True) + jnp.exp(sink - m)
            inv = 1.0 / den
            ot = (_dot_tn(vc_ref[:, gs], (e_c * inv).astype(BF16))
                  + _dot_tn(vp_ref[:, gs], (e_p * inv).astype(BF16)))
            lse = m + jnp.log(den)
            for hh in range(GROUP_A):
                hd = g * GROUP_A + hh
                seg = slice(hh * BLOCK, (hh + 1) * BLOCK)
                o_ref[:, hd * SLAB:(hd + 1) * SLAB] = ot[:, seg].T.astype(BF16)
                l_ref[hd] = lse[:, seg]

    cur = lambda i: (i, 0)
    prev = lambda i: (jnp.maximum(i - 1, 0), 0)
    kvw = N_KV_A * SLAB
    rows = pl.BlockSpec((N_KV_A, 1, gw), lambda i: (0, 0, 0))
    return pl.pallas_call(
        body, name="swa_fwd", grid=(nb,),
        in_specs=[pl.BlockSpec((BLOCK, HM), cur),
                  pl.BlockSpec((BLOCK, kvw), cur), pl.BlockSpec((BLOCK, kvw), prev),
                  pl.BlockSpec((BLOCK, kvw), cur), pl.BlockSpec((BLOCK, kvw), prev),
                  pl.BlockSpec((BLOCK, 1), cur), pl.BlockSpec((BLOCK, 1), prev),
                  pl.BlockSpec((1, BLOCK), lambda i: (0, i)), rows, rows],
        out_specs=[pl.BlockSpec((BLOCK, HM), cur), pl.BlockSpec((N_HEADS, 1, BLOCK), lambda i: (0, 0, i))],
        out_shape=[jax.ShapeDtypeStruct((t, HM), BF16), jax.ShapeDtypeStruct((N_HEADS, 1, t), F32)],
        compiler_params=_params(("parallel",)),
    )(qa, ka, ka, va, va, pos_col, pos_col, pos_row, slope_rows, sink_rows)


def _swa_bwd(qa, ka, va, out_a, d_oa, lse, pos_col, pos_row, sinks):
    t = qa.shape[0]
    nb = t // BLOCK
    gw = GROUP_A * BLOCK
    slope_rows, sink_rows = _swa_rows(sinks)

    def body(q_ref, qn_ref, do_ref, don_ref, l_ref, ln_ref, o_ref, on_ref, kp_ref, kc_ref, vp_ref, vc_ref,
             pkp_ref, pkc_ref, pq_ref, pqn_ref, slope_ref, sink_ref, dq_ref, dk_ref, dv_ref, dsink_ref):
        j = pl.program_id(0)
        pkc, pkp = pkc_ref[...], pkp_ref[...]
        dist_cc = _tile_group(jnp.abs(pkc - pq_ref[...]).astype(F32))
        dist_cp = _tile_group(jnp.abs(pkp - pq_ref[...]).astype(F32))
        dist_nc = _tile_group(jnp.abs(pkc - pqn_ref[...]).astype(F32))
        mask_cc, older = _swa_masks()
        mask_cp = jnp.logical_and(older, j > 0)
        mask_nc = jnp.logical_and(older, j < nb - 1)

        @pl.when(j == 0)
        def _():
            dsink_ref[...] = jnp.zeros_like(dsink_ref)

        def tile(k, v, x, dox, lrow, drow, dist, mask, slope):
            s = jnp.where(mask, _dot(k, x) * SCALE_A - slope * dist, NEG)
            p = jnp.exp(s - lrow)
            ds = p * (_dot(v, dox) - drow)
            return p.astype(BF16), ds.astype(BF16)

        for g in range(N_KV_A):
            gs = slice(g * SLAB, (g + 1) * SLAB)
            kc, kp, vc, vp = kc_ref[:, gs], kp_ref[:, gs], vc_ref[:, gs], vp_ref[:, gs]
            slope, sink = slope_ref[g], sink_ref[g]
            x, xn = _heads_beside(q_ref, g), _heads_beside(qn_ref, g)
            dox, doxn = _heads_beside(do_ref, g), _heads_beside(don_ref, g)
            lrow, lrown = _rows_beside(l_ref, g), _rows_beside(ln_ref, g)
            drow = jnp.sum(dox.astype(F32) * _heads_beside(o_ref, g).astype(F32), axis=0, keepdims=True)
            drown = jnp.sum(doxn.astype(F32) * _heads_beside(on_ref, g).astype(F32), axis=0, keepdims=True)
            p_cc, ds_cc = tile(kc, vc, x, dox, lrow, drow, dist_cc, mask_cc, slope)
            _, ds_cp = tile(kp, vp, x, dox, lrow, drow, dist_cp, mask_cp, slope)
            p_nc, ds_nc = tile(kc, vc, xn, doxn, lrown, drown, dist_nc, mask_nc, slope)
            dqt = (_dot_tn(kc, ds_cc) + _dot_tn(kp, ds_cp)) * SCALE_A
            for hh in range(GROUP_A):
                hd = g * GROUP_A + hh
                dq_ref[:, hd * SLAB:(hd + 1) * SLAB] = dqt[:, hh * BLOCK:(hh + 1) * BLOCK].T.astype(BF16)
            dk_ref[:, gs] = ((_dot_nt(ds_cc, x) + _dot_nt(ds_nc, xn)) * SCALE_A).astype(BF16)
            dv_ref[:, gs] = (_dot_nt(p_cc, dox) + _dot_nt(p_nc, doxn)).astype(BF16)
            dsink_ref[g] -= jnp.exp(sink - lrow) * drow

    cur = lambda j: (j, 0)
    prev = lambda j: (jnp.maximum(j - 1, 0), 0)
    nxt = lambda j: (jnp.minimum(j + 1, nb - 1), 0)
    cur3 = lambda j: (0, 0, j)
    nxt3 = lambda j: (0, 0, jnp.minimum(j + 1, nb - 1))
    kvw = N_KV_A * SLAB
    rows = pl.BlockSpec((N_KV_A, 1, gw), lambda j: (0, 0, 0))
    stat = lambda im: pl.BlockSpec((N_HEADS, 1, BLOCK), im)
    return pl.pallas_call(
        body, name="swa_bwd", grid=(nb,),
        in_specs=[pl.BlockSpec((BLOCK, HM), cur), pl.BlockSpec((BLOCK, HM), nxt),
                  pl.BlockSpec((BLOCK, HM), cur), pl.BlockSpec((BLOCK, HM), nxt),
                  stat(cur3), stat(nxt3), pl.BlockSpec((BLOCK, HM), cur), pl.BlockSpec((BLOCK, HM), nxt),
                  pl.BlockSpec((BLOCK, kvw), prev), pl.BlockSpec((BLOCK, kvw), cur),
                  pl.BlockSpec((BLOCK, kvw), prev), pl.BlockSpec((BLOCK, kvw), cur),
                  pl.BlockSpec((BLOCK, 1), prev), pl.BlockSpec((BLOCK, 1), cur),
                  pl.BlockSpec((1, BLOCK), lambda j: (0, j)),
                  pl.BlockSpec((1, BLOCK), lambda j: (0, jnp.minimum(j + 1, nb - 1))), rows, rows],
        out_specs=[pl.BlockSpec((BLOCK, HM), cur), pl.BlockSpec((BLOCK, kvw), cur),
                   pl.BlockSpec((BLOCK, kvw), cur), rows],
        out_shape=[jax.ShapeDtypeStruct((t, HM), BF16), jax.ShapeDtypeStruct((t, kvw), BF16),
                   jax.ShapeDtypeStruct((t, kvw), BF16), jax.ShapeDtypeStruct((N_KV_A, 1, gw), F32)],
        compiler_params=_params(("arbitrary",)),
    )(qa, qa, d_oa, d_oa, lse, lse, out_a, out_a, ka, ka, va, va,
      pos_col, pos_col, pos_row, pos_row, slope_rows, sink_rows)


def _mesh_pos():
    return lax.axis_index("x"), lax.axis_index("y"), lax.axis_index("c")


def _flip(v, bit):
    return 1 - v if bit else v


def _direct_copies(srcs, dsts, send_sems, recv_sems, local_sems, gather, sem_base=0):
    x, y, c = _mesh_pos()
    me = 4 * x + 2 * y + c
    local, remote = [], []
    for a, (src, dst) in enumerate(zip(srcs, dsts)):
        local.append(pltpu.make_async_copy(src if gather else src.at[me], dst.at[me], local_sems.at[sem_base + a]))
        for r in range(1, N_DEV):
            px, py, pc = _flip(x, r & 4), _flip(y, r & 2), _flip(c, r & 1)
            sem = (N_DEV - 1) * (sem_base + a) + r - 1
            remote.append(pltpu.make_async_remote_copy(
                src_ref=src if gather else src.at[4 * px + 2 * py + pc], dst_ref=dst.at[me],
                send_sem=send_sems.at[sem], recv_sem=recv_sems.at[sem],
                device_id=(px, py, pc), device_id_type=pl.DeviceIdType.MESH))
    return local, remote


def _start_copies(local, remote):
    for cp in local + remote:
        cp.start()


def _wait_copies(local, remote):
    for cp in remote:
        cp.wait_recv()
    for cp in remote:
        cp.wait_send()
    for cp in local:
        cp.wait()


def _exchange_scratch(n):
    return [pltpu.SemaphoreType.DMA((n * (N_DEV - 1),)), pltpu.SemaphoreType.DMA((n * (N_DEV - 1),)),
            pltpu.SemaphoreType.DMA((n,))]


ANY_SPEC = pl.BlockSpec(memory_space=pl.ANY)


def _mla_fwd(qt, kb, vt, late):
    t = kb.shape[0]
    tk = _attn_tile(t)
    ratio = 2 if t >= 2 * tk else 1
    tq = ratio * tk
    nq = t // tq
    hps = MLA_FWD_HEADS_PER_STEP
    w = hps * SLAB
    pairs = [(i, j) for i in range(nq) for j in range(ratio * (i + 1))]
    i_tab = jnp.asarray(np.array([p[0] for p in pairs], np.int32))
    j_tab = jnp.asarray(np.array([p[1] for p in pairs], np.int32))

    n_late = len(late)

    def body(it_ref, jt_ref, qt_ref, k_ref, vt_ref, *rest):
        late_refs, (o_ref, ot_ref, l_ref) = rest[:n_late], rest[n_late:n_late + 3]
        gathered_refs = rest[n_late + 3:2 * n_late + 3]
        m_s, acc_s, send_sems, recv_sems, local_sems = rest[2 * n_late + 3:]
        n = pl.program_id(1)
        i, j = it_ref[n], jt_ref[n]
        first_step = jnp.logical_and(pl.program_id(0) == 0, n == 0)
        last_step = jnp.logical_and(pl.program_id(0) == N_HEADS // hps - 1, n == len(pairs) - 1)

        @pl.when(first_step)
        def _():
            _start_copies(*_direct_copies(late_refs, gathered_refs, send_sems, recv_sems, local_sems, True))

        @pl.when(j == 0)
        def _():
            m_s[...] = jnp.full_like(m_s, NEG)
            acc_s[...] = jnp.zeros_like(acc_s)

        def update(masked, q0):
            qc = slice(q0, tq)

            def scores(hh):
                sl = slice(hh * SLAB, (hh + 1) * SLAB)
                return _dot(k_ref[:, sl], qt_ref[sl, qc])

            def softmax(hh, s):
                if masked:
                    s = jnp.where(lax.broadcasted_iota(jnp.int32, s.shape, 0)
                                  <= lax.broadcasted_iota(jnp.int32, s.shape, 1), s, NEG)
                m_old = m_s[hh][:, qc]
                m_new = jnp.maximum(m_old, jnp.max(s, axis=0, keepdims=True))
                m_s[hh, :, qc] = m_new
                return jnp.exp2(s - m_new).astype(BF16), jnp.exp2(m_old - m_new)

            def accumulate(hh, p, alpha):
                sl = slice(hh * SLAB, hh * SLAB + V_DIM_B + ONES_ROWS)
                acc_s[sl, qc] = alpha * acc_s[sl, qc] + _dot(vt_ref[sl, :], p)

            s_next, pending = scores(0), None
            for hh in range(hps):
                s = s_next
                if hh + 1 < hps:
                    s_next = scores(hh + 1)
                p, alpha = softmax(hh, s)
                if pending is not None:
                    accumulate(*pending)
                pending = (hh, p, alpha)
            accumulate(*pending)

        @pl.when(j < ratio * i)
        def _():
            update(False, 0)

        for part in range(ratio):
            @pl.when(j == ratio * i + part)
            def _():
                update(True, part * tk)

        @pl.when(j == ratio * i + ratio - 1)
        def _():
            for hh in range(hps):
                sl = slice(hh * SLAB, (hh + 1) * SLAB)
                den = acc_s[hh * SLAB + V_DIM_B:hh * SLAB + V_DIM_B + 1, :]
                values = lax.broadcasted_iota(jnp.int32, (SLAB, tq), 0) < V_DIM_B
                ot = jnp.where(values, acc_s[sl, :] / den, 0.0)
                ot_ref[sl, :] = ot.astype(BF16)
                o_ref[:, sl] = ot.T.astype(BF16)
                l_ref[hh] = m_s[hh] + jnp.log2(den)

        @pl.when(last_step)
        def _():
            _wait_copies(*_direct_copies(late_refs, gathered_refs, send_sems, recv_sems, local_sems, True))

    grid_spec = pltpu.PrefetchScalarGridSpec(
        num_scalar_prefetch=2, grid=(N_HEADS // hps, len(pairs)),
        in_specs=[pl.BlockSpec((w, tq), lambda h, n, it, jt: (h, it[n])),
                  pl.BlockSpec((tk, w), lambda h, n, it, jt: (jt[n], h)),
                  pl.BlockSpec((w, tk), lambda h, n, it, jt: (h, jt[n]))] + [ANY_SPEC] * n_late,
        out_specs=[pl.BlockSpec((tq, w), lambda h, n, it, jt: (it[n], h)),
                   pl.BlockSpec((w, tq), lambda h, n, it, jt: (h, it[n])),
                   pl.BlockSpec((hps, 1, tq), lambda h, n, it, jt: (h, 0, it[n]))] + [ANY_SPEC] * n_late,
        scratch_shapes=[pltpu.VMEM((hps, 1, tq), F32), pltpu.VMEM((w, tq), F32)] + _exchange_scratch(n_late))
    outs = pl.pallas_call(
        body, name="mla_fwd", grid_spec=grid_spec,
        out_shape=[jax.ShapeDtypeStruct((t, HM), BF16), jax.ShapeDtypeStruct((HM, t), BF16),
                   jax.ShapeDtypeStruct((N_HEADS, 1, t), F32)]
        + [jax.ShapeDtypeStruct((N_DEV,) + a.shape, a.dtype) for a in late],
        compiler_params=_params(("arbitrary", "arbitrary")),
    )(i_tab, j_tab, qt, kb, vt, *late)
    return outs[0], outs[1], outs[2], list(outs[3:])


def _mla_bwd(qt, kb, kt, vb, d_ob_t, lse, delta, grad_slices):
    t = kb.shape[0]
    tk = _attn_tile(t)
    ratio = 2 if t >= 2 * tk else 1
    tq = ratio * tk
    nk, nq = t // tk, t // tq
    hps = MLA_HEADS_PER_STEP
    w = hps * SLAB
    pairs = [(j, i) for j in range(nk) for i in range(j // ratio, nq)]
    j_tab = jnp.asarray(np.array([p[0] for p in pairs], np.int32))
    i_tab = jnp.asarray(np.array([p[1] for p in pairs], np.int32))

    n_ex = len(grad_slices)

    def body(jt_ref, it_ref, qt_ref, dot_ref, l_ref, dl_ref, k_ref, kt_ref, v_ref, *rest):
        slice_refs, (dqt_ref, dkt_ref, dvt_ref) = rest[:n_ex], rest[n_ex:n_ex + 3]
        part_refs = rest[n_ex + 3:2 * n_ex + 3]
        dk_s, dv_s, send_sems, recv_sems, local_sems = rest[2 * n_ex + 3:]
        n = pl.program_id(1)
        j, i = jt_ref[n], it_ref[n]
        first_step = jnp.logical_and(pl.program_id(0) == 0, n == 0)
        last_step = jnp.logical_and(pl.program_id(0) == N_HEADS // hps - 1, n == len(pairs) - 1)

        @pl.when(first_step)
        def _():
            _start_copies(*_direct_copies(slice_refs, part_refs, send_sems, recv_sems, local_sems, False))

        @pl.when(n == 0)
        def _():
            dqt_ref[...] = jnp.zeros_like(dqt_ref)

        def update(diagonal, q0):
            qc = slice(q0, tq)
            cols = pl.ds(pl.multiple_of(i * tq + q0, tk), tq - q0)

            def products(hh):
                sl = slice(hh * SLAB, (hh + 1) * SLAB)
                return _dot(k_ref[:, sl], qt_ref[sl, qc]), _dot(v_ref[:, sl], dot_ref[sl, qc])

            def softmax_bwd(hh, s, dp):
                if diagonal:
                    s = jnp.where(lax.broadcasted_iota(jnp.int32, s.shape, 0)
                                  <= lax.broadcasted_iota(jnp.int32, s.shape, 1), s, NEG)
                p = jnp.exp2(s - l_ref[hh][:, qc])
                return p.astype(BF16), (p * (dp - dl_ref[hh][:, qc])).astype(BF16)

            def gradients(hh, p, ds):
                base = hh * SLAB
                vrows = slice(base, base + V_DIM_B)
                qrows = slice(base, base + QK_NOPE + QK_ROPE)
                dv = _dot_nt(dot_ref[vrows, qc], p)
                dk = _dot_nt(qt_ref[qrows, qc], ds)
                if diagonal:
                    dv_s[base:base + SLAB, :] = jnp.concatenate([dv, jnp.zeros((SLAB - V_DIM_B, tk), F32)], axis=0)
                    dk_s[base:base + SLAB, :] = jnp.concatenate(
                        [dk, jnp.zeros((SLAB - QK_NOPE - QK_ROPE, tk), F32)], axis=0)
                else:
                    dv_s[vrows, :] += dv
                    dk_s[qrows, :] += dk
                dqt_ref[qrows, cols] += _dot(kt_ref[qrows, :], ds)

            for hh in range(hps):
                gradients(hh, *softmax_bwd(hh, *products(hh)))

        first_tile = lax.div(j, ratio)
        for part in range(ratio):
            @pl.when(jnp.logical_and(i == first_tile, lax.rem(j, ratio) == part))
            def _():
                update(True, part * tk)

        @pl.when(i > first_tile)
        def _():
            update(False, 0)

        @pl.when(i == nq - 1)
        def _():
            dkt_ref[...] = (dk_s[...] * (1.0 / LOG2E)).astype(BF16)
            dvt_ref[...] = dv_s[...].astype(BF16)

        @pl.when(last_step)
        def _():
            _wait_copies(*_direct_copies(slice_refs, part_refs, send_sems, recv_sems, local_sems, False))

    grid_spec = pltpu.PrefetchScalarGridSpec(
        num_scalar_prefetch=2, grid=(N_HEADS // hps, len(pairs)),
        in_specs=[pl.BlockSpec((w, tq), lambda h, n, jt, it: (h, it[n])),
                  pl.BlockSpec((w, tq), lambda h, n, jt, it: (h, it[n])),
                  pl.BlockSpec((hps, 1, tq), lambda h, n, jt, it: (h, 0, it[n])),
                  pl.BlockSpec((hps, 1, tq), lambda h, n, jt, it: (h, 0, it[n])),
                  pl.BlockSpec((tk, w), lambda h, n, jt, it: (jt[n], h)),
                  pl.BlockSpec((w, tk), lambda h, n, jt, it: (h, jt[n])),
                  pl.BlockSpec((tk, w), lambda h, n, jt, it: (jt[n], h))] + [ANY_SPEC] * n_ex,
        out_specs=[pl.BlockSpec((w, t), lambda h, n, jt, it: (h, 0)),
                   pl.BlockSpec((w, tk), lambda h, n, jt, it: (h, jt[n])),
                   pl.BlockSpec((w, tk), lambda h, n, jt, it: (h, jt[n]))] + [ANY_SPEC] * n_ex,
        scratch_shapes=[pltpu.VMEM((w, tk), F32), pltpu.VMEM((w, tk), F32)] + _exchange_scratch(n_ex))
    outs = pl.pallas_call(
        body, name="mla_bwd", grid_spec=grid_spec,
        out_shape=[jax.ShapeDtypeStruct((HM, t), F32), jax.ShapeDtypeStruct((HM, t), BF16),
                   jax.ShapeDtypeStruct((HM, t), BF16)]
        + [jax.ShapeDtypeStruct(a.shape, a.dtype) for a in grad_slices],
        compiler_params=_params(("arbitrary", "arbitrary")),
    )(j_tab, i_tab, qt, d_ob_t, lse, delta, kb, kt, vb, *grad_slices)
    return outs[0], outs[1], outs[2], list(outs[3:])


def _merge_fwd(out_a, out_b, gates, x, w_oa, w_ob, w_out, g2, g3):
    t = x.shape[0]
    tm = _token_tile(t)

    def body(oa_ref, ob_ref, gates_ref, x_ref, woa_ref, wob_ref, wout_ref, g2_ref, g3_ref,
             oap_ref, obp_ref, merged_ref, y_ref, x1_ref, h2_ref):
        oa_p = _dot(oa_ref[...], woa_ref[...])
        ob_p = _dot(ob_ref[...], wob_ref[...])
        oap_ref[...] = oa_p.astype(BF16)
        obp_ref[...] = ob_p.astype(BF16)
        sa = _sigmoid(gates_ref[:, 0:D_MODEL])
        sb = _sigmoid(gates_ref[:, D_MODEL:2 * D_MODEL])
        merged = (sa * oa_p + sb * ob_p).astype(BF16)
        merged_ref[...] = merged
        y = _dot(merged, wout_ref[...])
        y_ref[...] = y
        x1 = x_ref[...] + y * _rms_r(y) * g2_ref[...]
        x1_ref[...] = x1
        h2_ref[...] = (x1 * _rms_r(x1) * g3_ref[...]).astype(BF16)

    def sds(dt):
        return jax.ShapeDtypeStruct((t, D_MODEL), dt)

    row = _row_spec(tm, D_MODEL)
    return pl.pallas_call(
        body, name="merge_fwd", grid=(t // tm,),
        in_specs=[_row_spec(tm, HM), _row_spec(tm, HM), _row_spec(tm, 2 * D_MODEL), row,
                  _full_spec((HM, D_MODEL)), _full_spec((HM, D_MODEL)), _full_spec((D_MODEL, D_MODEL)),
                  _full_spec((1, D_MODEL)), _full_spec((1, D_MODEL))],
        out_specs=[row] * 6,
        out_shape=[sds(BF16), sds(BF16), sds(BF16), sds(F32), sds(F32), sds(BF16)],
        compiler_params=_params(("parallel",)),
    )(out_a, out_b, gates, x, w_oa, w_ob, w_out, g2, g3)


def _merge_bwd(dx1, y, gates, oa_p, ob_p, out_b_t, w_oa, w_ob, w_out, g2):
    t = dx1.shape[0]
    tm = _token_tile(t)

    def body(dx1_ref, y_ref, gates_ref, oap_ref, obp_ref, obt_ref, woa_ref, wob_ref, wout_ref, g2_ref,
             dy_ref, doap_ref, dobp_ref, dgates_ref, doa_ref, dobt_ref, dlb_ref, dg2_ref):
        dx1v = dx1_ref[...]
        yv = y_ref[...]
        r2 = _rms_r(yv)
        _acc_rows(dg2_ref, dx1v * yv * r2)
        dy = _rms_bwd(yv, r2, g2_ref[...], dx1v).astype(BF16)
        dy_ref[...] = dy
        dm = _dot_nt(dy, wout_ref[...])
        sa = _sigmoid(gates_ref[:, 0:D_MODEL])
        sb = _sigmoid(gates_ref[:, D_MODEL:2 * D_MODEL])
        d_oap = (dm * sa).astype(BF16)
        d_obp = (dm * sb).astype(BF16)
        doap_ref[...] = d_oap
        dobp_ref[...] = d_obp
        dgates_ref[:, 0:D_MODEL] = (dm * oap_ref[...].astype(F32) * sa * (1.0 - sa)).astype(BF16)
        dgates_ref[:, D_MODEL:2 * D_MODEL] = (dm * obp_ref[...].astype(F32) * sb * (1.0 - sb)).astype(BF16)
        doa_ref[...] = _dot_nt(d_oap, woa_ref[...]).astype(BF16)
        d_ob_t = _dot_nt(wob_ref[...], d_obp)
        dobt_ref[...] = d_ob_t.astype(BF16)
        for hd in range(N_HEADS):
            sl = slice(hd * SLAB, (hd + 1) * SLAB)
            dlb_ref[hd] = jnp.sum(d_ob_t[sl, :] * obt_ref[sl, :].astype(F32), axis=0, keepdims=True)

    def sds(n, dt):
        return jax.ShapeDtypeStruct((t, n), dt)

    row = _row_spec(tm, D_MODEL)
    head3 = pl.BlockSpec((N_HEADS, 1, tm), lambda i: (0, 0, i))
    return pl.pallas_call(
        body, name="merge_bwd", grid=(t // tm,),
        in_specs=[row, row, _row_spec(tm, 2 * D_MODEL), row, row, _col_spec(HM, tm),
                  _full_spec((HM, D_MODEL)), _full_spec((HM, D_MODEL)), _full_spec((D_MODEL, D_MODEL)),
                  _full_spec((1, D_MODEL))],
        out_specs=[row, row, row, _row_spec(tm, 2 * D_MODEL), _row_spec(tm, HM), _col_spec(HM, tm),
                   head3, _full_spec((1, D_MODEL))],
        out_shape=[sds(D_MODEL, BF16), sds(D_MODEL, BF16), sds(D_MODEL, BF16), sds(2 * D_MODEL, BF16),
                   sds(HM, BF16), jax.ShapeDtypeStruct((HM, t), BF16),
                   jax.ShapeDtypeStruct((N_HEADS, 1, t), F32), jax.ShapeDtypeStruct((1, D_MODEL), F32)],
        compiler_params=_params(("arbitrary",)),
    )(dx1, y, gates, oa_p, ob_p, out_b_t, w_oa, w_ob, w_out, g2)


def _mlp_fwd_bwd(x1, h2, target, w_up, w_down, g3, g4):
    t = x1.shape[0]
    tm = _token_tile(t)
    fs = D_FF // N_DEV

    def body(x1_ref, h2_ref, tgt_ref, wup_ref, wdown_ref, g3_ref, g4_ref,
             a_ref, du_ref, dy2_ref, dx1_ref, loss_ref, dg3_ref, dg4_ref):
        x1v = x1_ref[...]
        h2v = h2_ref[...]
        u = jnp.concatenate([_dot(h2v, wup_ref[s]) for s in range(N_DEV)], axis=1)
        ru = jnp.maximum(u, 0.0)
        a = (ru * ru).astype(BF16)
        a_ref[...] = a
        y2 = _dot(a, wdown_ref[...])
        r4 = _rms_r(y2)
        diff = x1v + y2 * r4 * g4_ref[...] - tgt_ref[...]
        _acc_rows(loss_ref, jnp.sum(diff * diff, axis=-1, keepdims=True) * (0.5 / D_MODEL)
                  * jnp.ones((1, SLAB), F32))
        dx2 = diff * (1.0 / D_MODEL)
        _acc_rows(dg4_ref, dx2 * y2 * r4)
        dy2 = _rms_bwd(y2, r4, g4_ref[...], dx2).astype(BF16)
        dy2_ref[...] = dy2
        du = (_dot_nt(dy2, wdown_ref[...]) * (2.0 * ru)).astype(BF16)
        du_ref[...] = du
        dh2 = _dot_nt(du[:, 0:fs], wup_ref[0])
        for s in range(1, N_DEV):
            dh2 += _dot_nt(du[:, s * fs:(s + 1) * fs], wup_ref[s])
        r3 = _rms_r(x1v)
        _acc_rows(dg3_ref, dh2 * x1v * r3)
        dx1_ref[...] = dx2 + _rms_bwd(x1v, r3, g3_ref[...], dh2)

    row = _row_spec(tm, D_MODEL)
    frow = _row_spec(tm, D_FF)
    vec = _full_spec((1, D_MODEL))
    return pl.pallas_call(
        body, name="mlp_fwd_bwd", grid=(t // tm,),
        in_specs=[row, row, row, _full_spec((N_DEV, D_MODEL, fs)), _full_spec((D_FF, D_MODEL)), vec, vec],
        out_specs=[frow, frow, row, row, _full_spec((1, SLAB)), vec, vec],
        out_shape=[jax.ShapeDtypeStruct((t, D_FF), BF16), jax.ShapeDtypeStruct((t, D_FF), BF16),
                   jax.ShapeDtypeStruct((t, D_MODEL), BF16), jax.ShapeDtypeStruct((t, D_MODEL), F32),
                   jax.ShapeDtypeStruct((1, SLAB), F32), jax.ShapeDtypeStruct((1, D_MODEL), F32),
                   jax.ShapeDtypeStruct((1, D_MODEL), F32)],
        compiler_params=_params(("arbitrary",)),
    )(x1, h2, target, w_up, w_down, g3, g4)


def _inproj_bwd(dgates, dqa, dka, dva, dqb_t, dkb_t, dvb_t, cq, ckv, x, dx1, rope_ct, rope_s1t, rope_s2t,
                g1, g_q, g_kv, w_in, w_qb, w_kvb):
    t = x.shape[0]
    tm = _token_tile(t)

    def body(dgates_ref, dqa_ref, dka_ref, dva_ref, dqt_ref, dkt_ref, dvt_ref, cq_ref, ckv_ref, x_ref, dx1_ref,
             ct_ref, s1t_ref, s2t_ref, g1_ref, gq_ref, gkv_ref, win_ref, wqb_ref, wkvb_ref,
             dproj_ref, dqbrt_ref, dkvbt_ref, dx_ref, dg1_ref, dgq_ref, dgkv_ref):
        ct, s1t, s2t = ct_ref[...], s1t_ref[...], s2t_ref[...]
        dk_sum_t = jnp.zeros((SLAB, tm), F32)
        for hd in range(N_HEADS):
            sl = slice(hd * SLAB, (hd + 1) * SLAB)
            dqbrt_ref[sl, :] = _rope_t_bwd(dqt_ref[sl, :] * SCALE_B, ct, s1t, s2t).astype(BF16)
            dk_sum_t += dkt_ref[sl, :].astype(F32)
        dkvbt_ref[0:HM, :] = dkt_ref[...]
        dkvbt_ref[HM:2 * HM, :] = dvt_ref[...]
        dkr = _rope_t_bwd(dk_sum_t, ct, s1t, s2t).T
        dcqn = _dot(wqb_ref[...], dqbrt_ref[...]).T
        cq = cq_ref[...]
        rq = _rms_r(cq)
        _acc_rows(dgq_ref, dcqn * cq * rq)
        dcq = _rms_bwd(cq, rq, gq_ref[...], dcqn)
        dckvn = _dot(wkvb_ref[...], dkvbt_ref[...]).T
        ckv = ckv_ref[...]
        rkv = _rms_r(ckv)
        _acc_rows(dgkv_ref, dckvn * ckv * rkv)
        dckv = _rms_bwd(ckv, rkv, gkv_ref[...], dckvn)
        dproj_ref[:, C_GATES:C_QA] = dgates_ref[...]
        dproj_ref[:, C_QA:C_KA] = dqa_ref[...]
        dproj_ref[:, C_KA:C_VA] = dka_ref[...]
        dproj_ref[:, C_VA:C_CQ] = dva_ref[...]
        dproj_ref[:, C_CQ:C_CKV] = dcq.astype(BF16)
        dproj_ref[:, C_CKV:C_KR] = dckv.astype(BF16)
        dproj_ref[:, C_KR:D_IN_PAD] = dkr.astype(BF16)
        dh = _dot_nt(dproj_ref[...], win_ref[...])
        xv = x_ref[...]
        r1 = _rms_r(xv)
        _acc_rows(dg1_ref, dh * xv * r1)
        dx_ref[...] = dx1_ref[...] + _rms_bwd(xv, r1, g1_ref[...], dh)

    kvw = N_KV_A * SLAB
    row = _row_spec(tm, D_MODEL)
    hm = _row_spec(tm, HM)
    hmt = _col_spec(HM, tm)
    tab = _col_spec(SLAB, tm)
    return pl.pallas_call(
        body, name="inproj_bwd", grid=(t // tm,),
        in_specs=[_row_spec(tm, 2 * D_MODEL), hm, _row_spec(tm, kvw), _row_spec(tm, kvw), hmt, hmt, hmt,
                  _row_spec(tm, Q_LORA), _row_spec(tm, KV_LORA), row, row, tab, tab, tab,
                  _full_spec((1, D_MODEL)), _full_spec((1, Q_LORA)), _full_spec((1, KV_LORA)),
                  _full_spec((D_MODEL, D_IN_PAD)), _full_spec((Q_LORA, HM)), _full_spec((KV_LORA, 2 * HM))],
        out_specs=[_row_spec(tm, D_IN_PAD), hmt, _col_spec(2 * HM, tm), row,
                   _full_spec((1, D_MODEL)), _full_spec((1, Q_LORA)), _full_spec((1, KV_LORA))],
        out_shape=[jax.ShapeDtypeStruct((t, D_IN_PAD), BF16), jax.ShapeDtypeStruct((HM, t), BF16),
                   jax.ShapeDtypeStruct((2 * HM, t), BF16), jax.ShapeDtypeStruct((t, D_MODEL), F32),
                   jax.ShapeDtypeStruct((1, D_MODEL), F32), jax.ShapeDtypeStruct((1, Q_LORA), F32),
                   jax.ShapeDtypeStruct((1, KV_LORA), F32)],
        compiler_params=_params(("arbitrary",)),
    )(dgates, dqa, dka, dva, dqb_t, dkb_t, dvb_t, cq, ckv, x, dx1, rope_ct, rope_s1t, rope_s2t,
      g1, g_q, g_kv, w_in, w_qb, w_kvb)


def _matmul_tn(a, b, name, out_dtype=F32, n_shards=1):
    t, k = a.shape
    n = b.shape[1]
    bt = min(t, 512)
    bk = min(k, 1024)
    bn = min(n, 1024)
    ns = n // n_shards
    per_block = bn // ns
    steps = t // bt

    def body(a_ref, b_ref, o_ref, acc):
        s = pl.program_id(2)

        @pl.when(s == 0)
        def _():
            acc[...] = jnp.zeros_like(acc)

        acc[...] += _dot_tn(a_ref[...], b_ref[...])

        @pl.when(s == steps - 1)
        def _():
            if n_shards > 1:
                for p in range(per_block):
                    o_ref[p] = acc[:, p * ns:(p + 1) * ns].astype(out_dtype)
            else:
                o_ref[...] = acc[...].astype(out_dtype)

    if n_shards > 1:
        out_spec = pl.BlockSpec((per_block, bk, ns), lambda i, j, s: (j, i, 0))
        out_shape = jax.ShapeDtypeStruct((n_shards, k, ns), out_dtype)
    else:
        out_spec = pl.BlockSpec((bk, bn), lambda i, j, s: (i, j))
        out_shape = jax.ShapeDtypeStruct((k, n), out_dtype)
    return pl.pallas_call(
        body, name=name, grid=(k // bk, n // bn, steps),
        in_specs=[pl.BlockSpec((bt, bk), lambda i, j, s: (s, i)), pl.BlockSpec((bt, bn), lambda i, j, s: (s, j))],
        out_specs=out_spec, out_shape=out_shape, scratch_shapes=[pltpu.VMEM((bk, bn), F32)],
        compiler_params=_params(("parallel", "parallel", "arbitrary")),
    )(a, b)


def _matmul_nn(a_t, b, name):
    m, t = a_t.shape
    n = b.shape[1]
    bt = min(t, 512)
    steps = t // bt

    def body(a_ref, b_ref, o_ref, acc):
        s = pl.program_id(0)

        @pl.when(s == 0)
        def _():
            acc[...] = jnp.zeros_like(acc)

        acc[...] += _dot(a_ref[...], b_ref[...])

        @pl.when(s == steps - 1)
        def _():
            o_ref[...] = acc[...]

    return pl.pallas_call(
        body, name=name, grid=(steps,),
        in_specs=[pl.BlockSpec((m, bt), lambda s: (0, s)), pl.BlockSpec((bt, n), lambda s: (s, 0))],
        out_specs=pl.BlockSpec((m, n), lambda s: (0, 0)), out_shape=jax.ShapeDtypeStruct((m, n), F32),
        scratch_shapes=[pltpu.VMEM((m, n), F32)],
        compiler_params=_params(("arbitrary",)),
    )(a_t, b)


def _all_gather(shards):
    n = len(shards)

    def body(*refs):
        srcs, dsts = refs[:n], refs[n:2 * n]
        send_sems, recv_sems, local_sems = refs[2 * n:]
        x, y, c = _mesh_pos()
        me, sibling = (x, y, c), (x, y, 1 - c)
        chips = [(1 - x, y), (x, 1 - y), (1 - x, 1 - y)]

        def slot(a, px, py, pc):
            return dsts[a].at[4 * px + 2 * py + pc]

        def copy(a, k, block, to, src=None):
            return pltpu.make_async_remote_copy(
                src_ref=slot(a, *block) if src is None else src, dst_ref=slot(a, *block),
                send_sem=send_sems.at[(N_DEV - 1) * a + k], recv_sem=recv_sems.at[(N_DEV - 1) * a + k],
                device_id=to, device_id_type=pl.DeviceIdType.MESH)

        mine = [pltpu.make_async_copy(srcs[a], slot(a, *me), local_sems.at[a]) for a in range(n)]
        first = []
        for a in range(n):
            first.append(copy(a, 0, me, sibling, src=srcs[a]))
            first += [copy(a, 1 + j, me, (*chip, c), src=srcs[a]) for j, chip in enumerate(chips)]
        for cp in mine + first:
            cp.start()
        passed = []
        for j, chip in enumerate(chips):
            for a in range(n):
                copy(a, 1 + j, (*chip, c), me).wait_recv()
                passed.append(copy(a, 4 + j, (*chip, c), sibling))
                passed[-1].start()
        for a in range(n):
            copy(a, 0, sibling, me).wait_recv()
        for j, chip in enumerate(chips):
            for a in range(n):
                copy(a, 4 + j, (*chip, 1 - c), me).wait_recv()
        for cp in first + passed:
            cp.wait_send()
        for cp in mine:
            cp.wait()

    return pl.pallas_call(
        body, name="all_gather_early",
        out_shape=[jax.ShapeDtypeStruct((N_DEV,) + a.shape, a.dtype) for a in shards],
        in_specs=[ANY_SPEC] * n, out_specs=[ANY_SPEC] * n, scratch_shapes=_exchange_scratch(n),
    )(*shards)


def _exchange_grads(slices, small):
    n = len(slices)

    def body(*refs):
        srcs, s_ref = refs[:n], refs[n]
        dsts, s_dst = refs[n + 1:2 * n + 1], refs[2 * n + 1]
        sems = refs[2 * n + 2:]
        parts = _direct_copies(srcs, dsts, *sems, False)
        smalls = _direct_copies([s_ref], [s_dst], *sems, True, sem_base=n)
        _start_copies(*parts)
        _start_copies(*smalls)
        _wait_copies(*parts)
        _wait_copies(*smalls)

    outs = pl.pallas_call(
        body, name="exchange_grads",
        out_shape=[jax.ShapeDtypeStruct(a.shape, a.dtype) for a in slices]
        + [jax.ShapeDtypeStruct((N_DEV,) + small.shape, small.dtype)],
        in_specs=[ANY_SPEC] * (n + 1), out_specs=[ANY_SPEC] * (n + 1), scratch_shapes=_exchange_scratch(n + 1),
    )(*slices, small)
    return list(outs[:n]), outs[n]


def _adamw(parts, w, m, v, name):
    _, k, n = parts.shape
    bk = min(k, ADAM_ROWS)
    c1 = 1.0 - ADAM_B1 ** ADAM_STEP
    c2 = 1.0 - ADAM_B2 ** ADAM_STEP

    def body(p_ref, w_ref, m_ref, v_ref, g_ref, d_ref, mo_ref, vo_ref):
        g = p_ref[0].astype(F32)
        for s in range(1, N_DEV):
            g = g + p_ref[s].astype(F32)
        g_ref[0] = g
        m_new = ADAM_B1 * m_ref[0] + (1.0 - ADAM_B1) * g
        v_new = ADAM_B2 * v_ref[0] + (1.0 - ADAM_B2) * (g * g)
        mo_ref[0] = m_new
        vo_ref[0] = v_new
        m_hat = m_new / c1
        v_hat = v_new / c2
        d_ref[0] = -ADAM_LR * (m_hat / (jnp.sqrt(v_hat) + ADAM_EPS) + ADAM_WD * w_ref[0])

    blk = pl.BlockSpec((1, bk, n), lambda i: (0, i, 0))
    out = jax.ShapeDtypeStruct((1, k, n), F32)
    return pl.pallas_call(
        body, name=name, grid=(k // bk,),
        in_specs=[pl.BlockSpec((N_DEV, bk, n), lambda i: (0, i, 0)), blk, blk, blk],
        out_specs=[blk] * 4, out_shape=[out] * 4,
        compiler_params=_params(("parallel",)),
    )(parts, w, m, v)


def _pad_heads_cols(w, heads, width):
    k = w.shape[0]
    w = w.reshape(k, heads, width)
    return jnp.pad(w, ((0, 0), (0, 0), (0, SLAB - width))).reshape(k, heads * SLAB)


def _unpad_heads_cols(w, heads, width):
    k = w.shape[0]
    return w.reshape(k, heads, SLAB)[:, :, :width].reshape(k, heads * width)


def _pad_heads_rows(w, heads, width):
    n = w.shape[1]
    w = w.reshape(heads, width, n)
    return jnp.pad(w, ((0, 0), (0, SLAB - width), (0, 0))).reshape(heads * SLAB, n)


def _unpad_heads_rows(w, heads, width):
    n = w.shape[1]
    return w.reshape(heads, SLAB, n)[:, :width, :].reshape(heads * width, n)


def _pad_w_in(w_in):
    o = 2 * D_MODEL
    qa = _pad_heads_cols(w_in[:, o:o + 512], N_HEADS, HEAD_A)
    ka = _pad_heads_cols(w_in[:, o + 512:o + 640], N_KV_A, HEAD_A)
    va = _pad_heads_cols(w_in[:, o + 640:o + 768], N_KV_A, HEAD_A)
    kr = jnp.pad(w_in[:, o + 1152:o + 1184], ((0, 0), (QK_NOPE, SLAB - QK_NOPE - QK_ROPE)))
    return jnp.concatenate([w_in[:, :o], qa, ka, va, w_in[:, o + 768:o + 1152], kr], axis=1)


def _unpad_w_in(w):
    qa = _unpad_heads_cols(w[:, C_QA:C_KA], N_HEADS, HEAD_A)
    ka = _unpad_heads_cols(w[:, C_KA:C_VA], N_KV_A, HEAD_A)
    va = _unpad_heads_cols(w[:, C_VA:C_CQ], N_KV_A, HEAD_A)
    kr = w[:, C_KR + QK_NOPE:C_KR + QK_NOPE + QK_ROPE]
    return jnp.concatenate([w[:, :C_QA], qa, ka, va, w[:, C_CQ:C_KR], kr], axis=1)


def _pad_w_kvb(w_kvb):
    w = w_kvb.reshape(KV_LORA, N_HEADS, QK_NOPE + V_DIM_B)
    k = jnp.pad(w[:, :, :QK_NOPE], ((0, 0), (0, 0), (0, SLAB - QK_NOPE))).reshape(KV_LORA, HM)
    v = jnp.pad(w[:, :, QK_NOPE:], ((0, 0), (0, 0), (0, SLAB - V_DIM_B))).reshape(KV_LORA, HM)
    return jnp.concatenate([k, v], axis=1)


def _unpad_w_kvb(w):
    k = w[:, :HM].reshape(KV_LORA, N_HEADS, SLAB)[:, :, :QK_NOPE]
    v = w[:, HM:].reshape(KV_LORA, N_HEADS, SLAB)[:, :, :V_DIM_B]
    return jnp.concatenate([k, v], axis=2).reshape(KV_LORA, N_HEADS * (QK_NOPE + V_DIM_B))


def _col_shards(w):
    k, n = w.shape
    return w.reshape(k, N_DEV, n // N_DEV).transpose(1, 0, 2)


def _from_col_shards(s):
    _, k, ns = s.shape
    return s.transpose(1, 0, 2).reshape(k, N_DEV * ns)


def _freq_row():
    freqs = ROPE_THETA ** (-jnp.arange(0, QK_ROPE, 2, dtype=F32) / QK_ROPE)
    return jnp.concatenate([jnp.zeros((QK_NOPE,), F32), freqs, freqs,
                            jnp.zeros((SLAB - QK_NOPE - QK_ROPE,), F32)]).reshape(1, SLAB)


SMALL_D_ROWS = ("pre_norm_mix", "post_norm_mix", "pre_norm_mlp", "post_norm_mlp")
SMALL_Q_OFF, SMALL_KV_OFF, SMALL_SINK_OFF, SMALL_LOSS_OFF = 0, 256, 384, 392


def _pack_small(vals):
    row4 = jnp.concatenate([vals["q_a_norm"].reshape(-1), vals["kv_a_norm"].reshape(-1), vals["sinks"].reshape(-1),
                            vals["loss"].reshape(-1), jnp.zeros((1024 - 393,), F32)])
    rows = [vals[n].reshape(1024) for n in SMALL_D_ROWS] + [row4]
    return jnp.concatenate([jnp.stack(rows), jnp.zeros((SMALL_ROWS - 5, 1024), F32)], axis=0)


def _unpack_small(blk):
    out = {n: blk[i].reshape(1, 1024) for i, n in enumerate(SMALL_D_ROWS)}
    out["q_a_norm"] = blk[4, SMALL_Q_OFF:SMALL_Q_OFF + 256].reshape(1, 256)
    out["kv_a_norm"] = blk[4, SMALL_KV_OFF:SMALL_KV_OFF + 128].reshape(1, 128)
    out["sinks"] = blk[4, SMALL_SINK_OFF:SMALL_SINK_OFF + 8].reshape(1, 8)
    out["loss"] = blk[4, SMALL_LOSS_OFF]
    return out


WEIGHT_ORDER = ("pre_norm_mix", "w_in", "q_a_norm", "w_q_b", "kv_a_norm", "w_kv_b", "sinks", "w_o_a", "w_o_b",
                "w_out", "post_norm_mix", "pre_norm_mlp", "w_up", "w_down", "post_norm_mlp")
SMALL_NAMES = ("pre_norm_mix", "q_a_norm", "kv_a_norm", "sinks", "post_norm_mix", "pre_norm_mlp", "post_norm_mlp")


def kernel(x, positions, pre_norm_mix, w_in, q_a_norm, w_q_b, kv_a_norm, w_kv_b, sinks, w_o_a, w_o_b, w_out, post_norm_mix, pre_norm_mlp, w_up, w_down, post_norm_mlp, loss_target, m_pre_norm_mix, m_w_in, m_q_a_norm, m_w_q_b, m_kv_a_norm, m_w_kv_b, m_sinks, m_w_o_a, m_w_o_b, m_w_out, m_post_norm_mix, m_pre_norm_mlp, m_w_up, m_w_down, m_post_norm_mlp, v_pre_norm_mix, v_w_in, v_q_a_norm, v_w_q_b, v_kv_a_norm, v_w_kv_b, v_sinks, v_w_o_a, v_w_o_b, v_w_out, v_post_norm_mix, v_pre_norm_mlp, v_w_up, v_w_down, v_post_norm_mlp):
    weights = dict(pre_norm_mix=pre_norm_mix, w_in=w_in, q_a_norm=q_a_norm, w_q_b=w_q_b, kv_a_norm=kv_a_norm,
                   w_kv_b=w_kv_b, sinks=sinks, w_o_a=w_o_a, w_o_b=w_o_b, w_out=w_out, post_norm_mix=post_norm_mix,
                   pre_norm_mlp=pre_norm_mlp, w_up=w_up, w_down=w_down, post_norm_mlp=post_norm_mlp)
    m_in = dict(pre_norm_mix=m_pre_norm_mix, w_in=m_w_in, q_a_norm=m_q_a_norm, w_q_b=m_w_q_b, kv_a_norm=m_kv_a_norm,
                w_kv_b=m_w_kv_b, sinks=m_sinks, w_o_a=m_w_o_a, w_o_b=m_w_o_b, w_out=m_w_out,
                post_norm_mix=m_post_norm_mix, pre_norm_mlp=m_pre_norm_mlp, w_up=m_w_up, w_down=m_w_down,
                post_norm_mlp=m_post_norm_mlp)
    v_in = dict(pre_norm_mix=v_pre_norm_mix, w_in=v_w_in, q_a_norm=v_q_a_norm, w_q_b=v_w_q_b, kv_a_norm=v_kv_a_norm,
                w_kv_b=v_w_kv_b, sinks=v_sinks, w_o_a=v_w_o_a, w_o_b=v_w_o_b, w_out=v_w_out,
                post_norm_mix=v_post_norm_mix, pre_norm_mlp=v_pre_norm_mlp, w_up=v_w_up, w_down=v_w_down,
                post_norm_mlp=v_post_norm_mlp)

    xs, pos, target = x[0], positions[0], loss_target[0]
    t = xs.shape[0]
    pos_col = pos.reshape(t, 1)
    pos_row = pos.reshape(1, t)
    g1, g2, g3, g4 = (weights[n] for n in SMALL_D_ROWS)
    g_q, g_kv = q_a_norm, kv_a_norm
    sink_vec = sinks.reshape(N_HEADS)
    shard = {n: weights[n][0].astype(BF16) for n in EARLY + LATE}

    e_in, e_qb, e_kvb = _all_gather([shard[n] for n in EARLY])
    w_in_p = _pad_w_in(_from_col_shards(e_in))
    w_qb = _pad_heads_cols(_from_col_shards(e_qb), N_HEADS, QK_NOPE + QK_ROPE)
    w_kvb = _pad_w_kvb(_from_col_shards(e_kvb))

    tables = _rope_tables(pos_col, pos_row, _freq_row())
    (h, gates, qa, ka, va, cq, ckv, cqn, ckvn, kb, vb, qt, kt, vt) = _inproj_fwd(
        xs, g1, w_in_p, g_q, g_kv, w_kvb, w_qb.T, w_kvb[:, :HM].T, w_kvb[:, HM:].T, w_in_p[:, C_KR:].T, tables)
    out_a, lse_a = _swa_fwd(qa, ka, va, pos_col, pos_row, sink_vec)
    out_b, out_b_t, lse_b, (l_oa, l_ob, l_out, w_up_s, l_down) = _mla_fwd(qt, kb, vt, [shard[n] for n in LATE])
    w_oa = _pad_heads_rows(_from_col_shards(l_oa), N_HEADS, HEAD_A)
    w_ob = _pad_heads_rows(_from_col_shards(l_ob), N_HEADS, V_DIM_B)
    w_out_f = l_out.reshape(D_MODEL, D_MODEL)
    w_down_f = l_down.reshape(D_FF, D_MODEL)

    oa_p, ob_p, merged, y, x1, h2 = _merge_fwd(out_a, out_b, gates, xs, w_oa, w_ob, w_out_f, g2, g3)
    a, du, dy2, dx1, loss, dg3, dg4 = _mlp_fwd_bwd(x1, h2, target, w_up_s, w_down_f, g3, g4)
    (dy, d_oap, d_obp, dgates, d_oa, d_ob_t, delta_b, dg2) = _merge_bwd(
        dx1, y, gates, oa_p, ob_p, out_b_t, w_oa, w_ob, w_out_f, g2)
    late_slices = [
        _col_shards(_unpad_heads_rows(_matmul_tn(out_a, d_oap, "dw_o_a"), N_HEADS, HEAD_A)).astype(BF16),
        _col_shards(_unpad_heads_rows(_matmul_tn(out_b, d_obp, "dw_o_b"), N_HEADS, V_DIM_B)).astype(BF16),
        _matmul_tn(merged, dy, "dw_out", BF16).reshape(N_DEV, D_MODEL // N_DEV, D_MODEL),
        _matmul_tn(h2, du, "dw_up", BF16, N_DEV),
        _matmul_tn(a, dy2, "dw_down", BF16).reshape(N_DEV, D_FF // N_DEV, D_MODEL),
    ]
    dqa, dka, dva, dsink = _swa_bwd(qa, ka, va, out_a, d_oa, lse_a, pos_col, pos_row, sink_vec)
    dqb_t, dkb_t, dvb_t, late_parts = _mla_bwd(qt, kb, kt, vb, d_ob_t, lse_b, delta_b, late_slices)
    dproj, dqbr_t, dkvb_t, dx, dg1, dgq, dgkv = _inproj_bwd(
        dgates, dqa, dka, dva, dqb_t, dkb_t, dvb_t, cq, ckv, xs, dx1, *tables[3:], g1, g_q, g_kv,
        w_in_p, w_qb, w_kvb)
    early_slices = [
        _col_shards(_unpad_w_in(_matmul_tn(h, dproj, "dw_in"))).astype(BF16),
        _col_shards(_unpad_heads_cols(_matmul_nn(dqbr_t, cqn, "dw_q_b").T, N_HEADS, QK_NOPE + QK_ROPE)).astype(BF16),
        _col_shards(_unpad_w_kvb(_matmul_nn(dkvb_t, ckvn, "dw_kv_b").T)).astype(BF16),
    ]
    small_grads = {"pre_norm_mix": dg1, "post_norm_mix": dg2, "pre_norm_mlp": dg3, "post_norm_mlp": dg4,
                   "q_a_norm": dgq, "kv_a_norm": dgkv, "sinks": dsink.reshape(N_HEADS, BLOCK).sum(axis=1),
                   "loss": loss[0, 0:1]}
    early_parts, s_parts = _exchange_grads(early_slices, _pack_small(small_grads))

    updates = {}
    for name, parts in zip(EARLY + LATE, early_parts + late_parts):
        outs = _adamw(parts, weights[name], m_in[name], v_in[name], "adamw_" + name)
        for kind, arr in zip(("g", "d", "m", "v"), outs):
            updates[kind, name] = arr
    zero = jnp.zeros((), F32)
    pack = lambda src: _pack_small({**{n: src[n] for n in SMALL_NAMES}, "loss": zero})[None]
    smalls = _adamw(s_parts, pack(weights), pack(m_in), pack(v_in), "adamw_small")
    for kind, blk in zip(("g", "d", "m", "v"), smalls):
        for wname, piece in _unpack_small(blk[0]).items():
            updates[kind, wname] = piece
    results = [updates[kind, name] for kind in ("g", "d", "m", "v") for name in WEIGHT_ORDER]
    return (updates["g", "loss"], dx[None], *results)
```

```python
import functools

import numpy as np
import jax
import jax.numpy as jnp
from jax import lax
from jax.experimental import pallas as pl
from jax.experimental.pallas import tpu as pltpu

F32 = jnp.float32
BF16 = jnp.bfloat16

D_MODEL = 1024
D_FF = 4096
N_HEADS = 8
N_KV_A = 2
GROUP_A = N_HEADS // N_KV_A
HEAD_A = 64
QK_NOPE = 64
QK_ROPE = 32
V_DIM_B = 64
Q_LORA = 256
KV_LORA = 128
BLOCK = 128
SLAB = 128
ROPE_THETA = 10000.0
EPS = 1e-6
N_DEV = 8
NEG = -1e30

SCALE_A = HEAD_A ** -0.5
SCALE_B = (QK_NOPE + QK_ROPE) ** -0.5
LOG2E = 1.4426950408889634
SCORE_B = SCALE_B * LOG2E
MLA_HEADS_PER_STEP = 4
MLA_FWD_HEADS_PER_STEP = 8
ONES_ROWS = 16
SLOPES_A = tuple(2.0 ** (-8.0 * (h + 1) / N_HEADS) for h in range(N_HEADS))

ADAM_LR = 0.001
ADAM_B1 = 0.9
ADAM_B2 = 0.999
ADAM_EPS = 1e-08
ADAM_WD = 0.01
ADAM_STEP = 10

HM = N_HEADS * SLAB
C_GATES = 0
C_QA = 2 * D_MODEL
C_KA = C_QA + HM
C_VA = C_KA + N_KV_A * SLAB
C_CQ = C_VA + N_KV_A * SLAB
C_CKV = C_CQ + Q_LORA
C_KR = C_CKV + KV_LORA
D_IN_PAD = C_KR + SLAB

VMEM_LIMIT = 56 * 1024 * 1024

EARLY = ("w_in", "w_q_b", "w_kv_b")
LATE = ("w_o_a", "w_o_b", "w_out", "w_up", "w_down")
ADAM_ROWS = 256
SMALL_ROWS = 8


def _token_tile(t):
    return min(256, t)


def _attn_tile(t):
    return 512 if t >= 2048 else 128


def _params(sem, vmem=VMEM_LIMIT):
    return pltpu.CompilerParams(dimension_semantics=sem, vmem_limit_bytes=vmem)


def _dot(a, b):
    return jnp.dot(a, b, preferred_element_type=F32)


def _dot_nt(a, b):
    return lax.dot_general(a, b, (((1,), (1,)), ((), ())), preferred_element_type=F32)


def _dot_tn(a, b):
    return lax.dot_general(a, b, (((0,), (0,)), ((), ())), preferred_element_type=F32)


def _rms_r(x):
    return lax.rsqrt(jnp.mean(x * x, axis=-1, keepdims=True) + EPS)


def _rms_bwd(x, r, g, dy):
    t = dy * g
    return r * t - x * (r * r * r) * jnp.mean(x * t, axis=-1, keepdims=True)


def _sigmoid(x):
    return 1.0 / (1.0 + jnp.exp(-x))


def _rope(x, c, s1, s2):
    return x * c + pltpu.roll(x, SLAB - 16, 1) * s1 + pltpu.roll(x, 16, 1) * s2


def _rope_bwd(d, c, s1, s2):
    return d * c + pltpu.roll(d * s1, 16, 1) + pltpu.roll(d * s2, SLAB - 16, 1)


def _roll_rows(x, shift):
    return jnp.concatenate([x[-shift:], x[:-shift]], axis=0)


def _rope_t(x, c, s1, s2):
    return x * c + _roll_rows(x, SLAB - 16) * s1 + _roll_rows(x, 16) * s2


def _rope_t_bwd(d, c, s1, s2):
    return d * c + _roll_rows(d * s1, 16) + _roll_rows(d * s2, SLAB - 16)


def _row_spec(tm, n):
    return pl.BlockSpec((tm, n), lambda i: (i, 0))


def _col_spec(n, tm):
    return pl.BlockSpec((n, tm), lambda i: (0, i))


def _full_spec(shape):
    nd = len(shape)
    return pl.BlockSpec(shape, lambda i: (0,) * nd, pipeline_mode=pl.Buffered(1))


def _acc_rows(ref, val):
    @pl.when(pl.program_id(0) == 0)
    def _():
        ref[...] = jnp.zeros_like(ref)
    ref[...] += jnp.sum(val, axis=0, keepdims=True)


def _rope_tables(pos_col, freq_row):
    t = pos_col.shape[0]
    tm = _token_tile(t)

    def body(pos_ref, f_ref, c_ref, s1_ref, s2_ref, ct_ref, s1t_ref, s2t_ref):
        ang = pos_ref[...].astype(F32) * f_ref[...]
        lane = lax.broadcasted_iota(jnp.int32, ang.shape, 1)
        s = jnp.sin(ang)
        c = jnp.cos(ang)
        s1 = jnp.where((lane >= 64) & (lane < 80), -s, 0.0)
        s2 = jnp.where((lane >= 80) & (lane < 96), s, 0.0)
        c_ref[...], s1_ref[...], s2_ref[...] = c, s1, s2
        ct_ref[...], s1t_ref[...], s2t_ref[...] = c.T, s1.T, s2.T

    tab = jax.ShapeDtypeStruct((t, SLAB), F32)
    tabt = jax.ShapeDtypeStruct((SLAB, t), F32)
    return pl.pallas_call(
        body, name="rope_tables", grid=(t // tm,),
        in_specs=[_row_spec(tm, 1), _full_spec((1, SLAB))],
        out_specs=[_row_spec(tm, SLAB)] * 3 + [_col_spec(SLAB, tm)] * 3, out_shape=[tab] * 3 + [tabt] * 3,
        compiler_params=_params(("parallel",)),
    )(pos_col, freq_row)


def _inproj_fwd(x, g1, w_in, g_q, g_kv, w_kvb, w_qb_t, w_kb_t, w_vb_t, w_kr_t, tables):
    t = x.shape[0]
    tm = _token_tile(t)

    def body(x_ref, g1_ref, win_ref, gq_ref, gkv_ref, wkvb_ref, wqbt_ref, wkbt_ref, wvbt_ref, wkrt_ref,
             c_ref, s1_ref, s2_ref, ct_ref, s1t_ref, s2t_ref,
             h_ref, gates_ref, qa_ref, ka_ref, va_ref, cq_ref, ckv_ref, cqn_ref, ckvn_ref,
             kb_ref, vb_ref, qt_ref, kt_ref, vt_ref):
        xv = x_ref[...]
        h = (xv * _rms_r(xv) * g1_ref[...]).astype(BF16)
        h_ref[...] = h
        proj = _dot(h, win_ref[...])
        gates_ref[...] = proj[:, C_GATES:C_QA]
        qa_ref[...] = proj[:, C_QA:C_KA].astype(BF16)
        ka_ref[...] = proj[:, C_KA:C_VA].astype(BF16)
        va_ref[...] = proj[:, C_VA:C_CQ].astype(BF16)
        cq = proj[:, C_CQ:C_CKV]
        ckv = proj[:, C_CKV:C_KR]
        kr = proj[:, C_KR:D_IN_PAD]
        cq_ref[...] = cq
        ckv_ref[...] = ckv
        cqn = (cq * _rms_r(cq) * gq_ref[...]).astype(BF16)
        ckvn = (ckv * _rms_r(ckv) * gkv_ref[...]).astype(BF16)
        cqn_ref[...] = cqn
        ckvn_ref[...] = ckvn
        c, s1, s2 = c_ref[...], s1_ref[...], s2_ref[...]
        kvb = _dot(ckvn, wkvb_ref[...])
        kr_rot = _rope(kr, c, s1, s2)
        ct, s1t, s2t = ct_ref[...], s1t_ref[...], s2t_ref[...]
        q_t = _dot_nt(wqbt_ref[...], cqn)
        k_t = _dot_nt(wkbt_ref[...], ckvn)
        kr_t = _rope_t(_dot_nt(wkrt_ref[...], h), ct, s1t, s2t)
        for hd in range(N_HEADS):
            sl = slice(hd * SLAB, (hd + 1) * SLAB)
            kb_ref[:, sl] = (kvb[:, sl] + kr_rot).astype(BF16)
            qt_ref[sl, :] = (_rope_t(q_t[sl, :], ct, s1t, s2t) * SCORE_B).astype(BF16)
            kt_ref[sl, :] = (k_t[sl, :] + kr_t).astype(BF16)
        vb_ref[...] = kvb[:, HM:2 * HM].astype(BF16)
        pad_row = lax.broadcasted_iota(jnp.int32, (HM, 1), 0) & (SLAB - 1)
        ones_rows = jnp.where((pad_row >= V_DIM_B) & (pad_row < V_DIM_B + ONES_ROWS), 1.0, 0.0)
        vt_ref[...] = (_dot_nt(wvbt_ref[...], ckvn) + ones_rows).astype(BF16)

    def sds(n, dt):
        return jax.ShapeDtypeStruct((t, n), dt)

    outs = [(D_MODEL, BF16), (2 * D_MODEL, F32), (HM, BF16), (N_KV_A * SLAB, BF16), (N_KV_A * SLAB, BF16),
            (Q_LORA, F32), (KV_LORA, F32), (Q_LORA, BF16), (KV_LORA, BF16), (HM, BF16), (HM, BF16)]
    tab, tabt = _row_spec(tm, SLAB), _col_spec(SLAB, tm)
    return pl.pallas_call(
        body, name="inproj_fwd", grid=(t // tm,),
        in_specs=[_row_spec(tm, D_MODEL), _full_spec((1, D_MODEL)), _full_spec((D_MODEL, D_IN_PAD)),
                  _full_spec((1, Q_LORA)), _full_spec((1, KV_LORA)), _full_spec((KV_LORA, 2 * HM)),
                  _full_spec((HM, Q_LORA)), _full_spec((HM, KV_LORA)), _full_spec((HM, KV_LORA)),
                  _full_spec((SLAB, D_MODEL)), tab, tab, tab, tabt, tabt, tabt],
        out_specs=[_row_spec(tm, n) for n, _ in outs] + [_col_spec(HM, tm)] * 3,
        out_shape=[sds(n, dt) for n, dt in outs] + [jax.ShapeDtypeStruct((HM, t), BF16)] * 3,
        compiler_params=_params(("parallel",)),
    )(x, g1, w_in, g_q, g_kv, w_kvb, w_qb_t, w_kb_t, w_vb_t, w_kr_t, *tables)


def _tile_group(a):
    return jnp.concatenate([a] * GROUP_A, axis=1)


def _swa_masks():
    row = lax.broadcasted_iota(jnp.int32, (BLOCK, GROUP_A * BLOCK), 0)
    col = lax.broadcasted_iota(jnp.int32, (BLOCK, GROUP_A * BLOCK), 1) & (BLOCK - 1)
    return row <= col, row > col


def _heads_beside(ref, g):
    return jnp.concatenate([ref[:, (g * GROUP_A + hh) * SLAB:(g * GROUP_A + hh + 1) * SLAB].T
                            for hh in range(GROUP_A)], axis=1)


def _rows_beside(ref, g):
    return jnp.concatenate([ref[g * GROUP_A + hh] for hh in range(GROUP_A)], axis=1)


def _swa_rows(sinks):
    slopes = jnp.repeat(jnp.asarray(SLOPES_A, F32).reshape(N_KV_A, GROUP_A, 1), BLOCK, axis=2)
    sink_rows = jnp.repeat(sinks.reshape(N_KV_A, GROUP_A, 1), BLOCK, axis=2)
    return slopes.reshape(N_KV_A, 1, GROUP_A * BLOCK), sink_rows.reshape(N_KV_A, 1, GROUP_A * BLOCK)


def _swa_fwd(qa, ka, va, pos_col, pos_row, sinks):
    t = qa.shape[0]
    nb = t // BLOCK
    gw = GROUP_A * BLOCK
    slope_rows, sink_rows = _swa_rows(sinks)

    def body(q_ref, kc_ref, kp_ref, vc_ref, vp_ref, pkc_ref, pkp_ref, pq_ref, slope_ref, sink_ref, o_ref, l_ref):
        i = pl.program_id(0)
        pq = pq_ref[...]
        dist_c = _tile_group(jnp.abs(pkc_ref[...] - pq).astype(F32))
        dist_p = _tile_group(jnp.abs(pkp_ref[...] - pq).astype(F32))
        mask_c, older = _swa_masks()
        mask_p = jnp.logical_and(older, i > 0)
        for g in range(N_KV_A):
            gs = slice(g * SLAB, (g + 1) * SLAB)
            x = _heads_beside(q_ref, g)
            slope, sink = slope_ref[g], sink_ref[g]
            s_c = jnp.where(mask_c, _dot(kc_ref[:, gs], x) * SCALE_A - slope * dist_c, NEG)
            s_p = jnp.where(mask_p, _dot(kp_ref[:, gs], x) * SCALE_A - slope * dist_p, NEG)
            m = jnp.maximum(jnp.maximum(jnp.max(s_c, axis=0, keepdims=True),
                                        jnp.max(s_p, axis=0, keepdims=True)), sink)
            e_c = jnp.exp(s_c - m)
            e_p = jnp.exp(s_p - m)
            den = jnp.sum(e_c, axis=0, keepdims=True) + jnp.sum(e_p, axis=0, keepdims=True) + jnp.exp(sink - m)
            inv = 1.0 / den
            ot = (_dot_tn(vc_ref[:, gs], (e_c * inv).astype(BF16))
                  + _dot_tn(vp_ref[:, gs], (e_p * inv).astype(BF16)))
            lse = m + jnp.log(den)
            for hh in range(GROUP_A):
                hd = g * GROUP_A + hh
                seg = slice(hh * BLOCK, (hh + 1) * BLOCK)
                o_ref[:, hd * SLAB:(hd + 1) * SLAB] = ot[:, seg].T.astype(BF16)
                l_ref[hd] = lse[:, seg]

    cur = lambda i: (i, 0)
    prev = lambda i: (jnp.maximum(i - 1, 0), 0)
    kvw = N_KV_A * SLAB
    rows = pl.BlockSpec((N_KV_A, 1, gw), lambda i: (0, 0, 0))
    return pl.pallas_call(
        body, name="swa_fwd", grid=(nb,),
        in_specs=[pl.BlockSpec((BLOCK, HM), cur),
                  pl.BlockSpec((BLOCK, kvw), cur), pl.BlockSpec((BLOCK, kvw), prev),
                  pl.BlockSpec((BLOCK, kvw), cur), pl.BlockSpec((BLOCK, kvw), prev),
                  pl.BlockSpec((BLOCK, 1), cur), pl.BlockSpec((BLOCK, 1), prev),
                  pl.BlockSpec((1, BLOCK), lambda i: (0, i)), rows, rows],
        out_specs=[pl.BlockSpec((BLOCK, HM), cur), pl.BlockSpec((N_HEADS, 1, BLOCK), lambda i: (0, 0, i))],
        out_shape=[jax.ShapeDtypeStruct((t, HM), BF16), jax.ShapeDtypeStruct((N_HEADS, 1, t), F32)],
        compiler_params=_params(("parallel",)),
    )(qa, ka, ka, va, va, pos_col, pos_col, pos_row, slope_rows, sink_rows)


def _swa_bwd(qa, ka, va, out_a, d_oa, lse, pos_col, pos_row, sinks):
    t = qa.shape[0]
    nb = t // BLOCK
    gw = GROUP_A * BLOCK
    slope_rows, sink_rows = _swa_rows(sinks)

    def body(q_ref, qn_ref, do_ref, don_ref, l_ref, ln_ref, o_ref, on_ref, kp_ref, kc_ref, vp_ref, vc_ref,
             pkp_ref, pkc_ref, pq_ref, pqn_ref, slope_ref, sink_ref, dq_ref, dk_ref, dv_ref, dsink_ref):
        j = pl.program_id(0)
        pkc, pkp = pkc_ref[...], pkp_ref[...]
        dist_cc = _tile_group(jnp.abs(pkc - pq_ref[...]).astype(F32))
        dist_cp = _tile_group(jnp.abs(pkp - pq_ref[...]).astype(F32))
        dist_nc = _tile_group(jnp.abs(pkc - pqn_ref[...]).astype(F32))
        mask_cc, older = _swa_masks()
        mask_cp = jnp.logical_and(older, j > 0)
        mask_nc = jnp.logical_and(older, j < nb - 1)

        @pl.when(j == 0)
        def _():
            dsink_ref[...] = jnp.zeros_like(dsink_ref)

        def tile(k, v, x, dox, lrow, drow, dist, mask, slope):
            s = jnp.where(mask, _dot(k, x) * SCALE_A - slope * dist, NEG)
            p = jnp.exp(s - lrow)
            ds = p * (_dot(v, dox) - drow)
            return p.astype(BF16), ds.astype(BF16)

        for g in range(N_KV_A):
            gs = slice(g * SLAB, (g + 1) * SLAB)
            kc, kp, vc, vp = kc_ref[:, gs], kp_ref[:, gs], vc_ref[:, gs], vp_ref[:, gs]
            slope, sink = slope_ref[g], sink_ref[g]
            x, xn = _heads_beside(q_ref, g), _heads_beside(qn_ref, g)
            dox, doxn = _heads_beside(do_ref, g), _heads_beside(don_ref, g)
            lrow, lrown = _rows_beside(l_ref, g), _rows_beside(ln_ref, g)
            drow = jnp.sum(dox.astype(F32) * _heads_beside(o_ref, g).astype(F32), axis=0, keepdims=True)
            drown = jnp.sum(doxn.astype(F32) * _heads_beside(on_ref, g).astype(F32), axis=0, keepdims=True)
            p_cc, ds_cc = tile(kc, vc, x, dox, lrow, drow, dist_cc, mask_cc, slope)
            _, ds_cp = tile(kp, vp, x, dox, lrow, drow, dist_cp, mask_cp, slope)
            p_nc, ds_nc = tile(kc, vc, xn, doxn, lrown, drown, dist_nc, mask_nc, slope)
            dqt = (_dot_tn(kc, ds_cc) + _dot_tn(kp, ds_cp)) * SCALE_A
            for hh in range(GROUP_A):
                hd = g * GROUP_A + hh
                dq_ref[:, hd * SLAB:(hd + 1) * SLAB] = dqt[:, hh * BLOCK:(hh + 1) * BLOCK].T.astype(BF16)
            dk_ref[:, gs] = ((_dot_nt(ds_cc, x) + _dot_nt(ds_nc, xn)) * SCALE_A).astype(BF16)
            dv_ref[:, gs] = (_dot_nt(p_cc, dox) + _dot_nt(p_nc, doxn)).astype(BF16)
            dsink_ref[g] -= jnp.exp(sink - lrow) * drow

    cur = lambda j: (j, 0)
    prev = lambda j: (jnp.maximum(j - 1, 0), 0)
    nxt = lambda j: (jnp.minimum(j + 1, nb - 1), 0)
    cur3 = lambda j: (0, 0, j)
    nxt3 = lambda j: (0, 0, jnp.minimum(j + 1, nb - 1))
    kvw = N_KV_A * SLAB
    rows = pl.BlockSpec((N_KV_A, 1, gw), lambda j: (0, 0, 0))
    stat = lambda im: pl.BlockSpec((N_HEADS, 1, BLOCK), im)
    return pl.pallas_call(
        body, name="swa_bwd", grid=(nb,),
        in_specs=[pl.BlockSpec((BLOCK, HM), cur), pl.BlockSpec((BLOCK, HM), nxt),
                  pl.BlockSpec((BLOCK, HM), cur), pl.BlockSpec((BLOCK, HM), nxt),
                  stat(cur3), stat(nxt3), pl.BlockSpec((BLOCK, HM), cur), pl.BlockSpec((BLOCK, HM), nxt),
                  pl.BlockSpec((BLOCK, kvw), prev), pl.BlockSpec((BLOCK, kvw), cur),
                  pl.BlockSpec((BLOCK, kvw), prev), pl.BlockSpec((BLOCK, kvw), cur),
                  pl.BlockSpec((BLOCK, 1), prev), pl.BlockSpec((BLOCK, 1), cur),
                  pl.BlockSpec((1, BLOCK), lambda j: (0, j)),
                  pl.BlockSpec((1, BLOCK), lambda j: (0, jnp.minimum(j + 1, nb - 1))), rows, rows],
        out_specs=[pl.BlockSpec((BLOCK, HM), cur), pl.BlockSpec((BLOCK, kvw), cur),
                   pl.BlockSpec((BLOCK, kvw), cur), rows],
        out_shape=[jax.ShapeDtypeStruct((t, HM), BF16), jax.ShapeDtypeStruct((t, kvw), BF16),
                   jax.ShapeDtypeStruct((t, kvw), BF16), jax.ShapeDtypeStruct((N_KV_A, 1, gw), F32)],
        compiler_params=_params(("arbitrary",)),
    )(qa, qa, d_oa, d_oa, lse, lse, out_a, out_a, ka, ka, va, va,
      pos_col, pos_col, pos_row, pos_row, slope_rows, sink_rows)


def _mesh_pos():
    return lax.axis_index("x"), lax.axis_index("y"), lax.axis_index("c")


def _flip(v, bit):
    return 1 - v if bit else v


def _direct_copies(srcs, dsts, send_sems, recv_sems, local_sems, gather, sem_base=0):
    x, y, c = _mesh_pos()
    me = 4 * x + 2 * y + c
    local, remote = [], []
    for a, (src, dst) in enumerate(zip(srcs, dsts)):
        local.append(pltpu.make_async_copy(src if gather else src.at[me], dst.at[me], local_sems.at[sem_base + a]))
        for r in range(1, N_DEV):
            px, py, pc = _flip(x, r & 4), _flip(y, r & 2), _flip(c, r & 1)
            sem = (N_DEV - 1) * (sem_base + a) + r - 1
            remote.append(pltpu.make_async_remote_copy(
                src_ref=src if gather else src.at[4 * px + 2 * py + pc], dst_ref=dst.at[me],
                send_sem=send_sems.at[sem], recv_sem=recv_sems.at[sem],
                device_id=(px, py, pc), device_id_type=pl.DeviceIdType.MESH))
    return local, remote


def _start_copies(local, remote):
    for cp in local + remote:
        cp.start()


def _wait_copies(local, remote):
    for cp in remote:
        cp.wait_recv()
    for cp in remote:
        cp.wait_send()
    for cp in local:
        cp.wait()


def _exchange_scratch(n):
    return [pltpu.SemaphoreType.DMA((n * (N_DEV - 1),)), pltpu.SemaphoreType.DMA((n * (N_DEV - 1),)),
            pltpu.SemaphoreType.DMA((n,))]


ANY_SPEC = pl.BlockSpec(memory_space=pl.ANY)


def _mla_fwd(qt, kb, vt, late):
    t = kb.shape[0]
    tk = _attn_tile(t)
    ratio = 2 if t >= 2 * tk else 1
    tq = ratio * tk
    nq = t // tq
    hps = MLA_FWD_HEADS_PER_STEP
    w = hps * SLAB
    pairs = [(i, j) for i in range(nq) for j in range(ratio * (i + 1))]
    i_tab = jnp.asarray(np.array([p[0] for p in pairs], np.int32))
    j_tab = jnp.asarray(np.array([p[1] for p in pairs], np.int32))

    n_late = len(late)

    def body(it_ref, jt_ref, qt_ref, k_ref, vt_ref, *rest):
        late_refs, (o_ref, ot_ref, l_ref) = rest[:n_late], rest[n_late:n_late + 3]
        gathered_refs = rest[n_late + 3:2 * n_late + 3]
        m_s, acc_s, send_sems, recv_sems, local_sems = rest[2 * n_late + 3:]
        n = pl.program_id(1)
        i, j = it_ref[n], jt_ref[n]
        first_step = jnp.logical_and(pl.program_id(0) == 0, n == 0)
        last_step = jnp.logical_and(pl.program_id(0) == N_HEADS // hps - 1, n == len(pairs) - 1)

        @pl.when(first_step)
        def _():
            _start_copies(*_direct_copies(late_refs, gathered_refs, send_sems, recv_sems, local_sems, True))

        @pl.when(j == 0)
        def _():
            m_s[...] = jnp.full_like(m_s, NEG)
            acc_s[...] = jnp.zeros_like(acc_s)

        def update(masked, q0):
            qc = slice(q0, tq)

            def scores(hh):
                sl = slice(hh * SLAB, (hh + 1) * SLAB)
                return _dot(k_ref[:, sl], qt_ref[sl, qc])

            def softmax(hh, s):
                if masked:
                    s = jnp.where(lax.broadcasted_iota(jnp.int32, s.shape, 0)
                                  <= lax.broadcasted_iota(jnp.int32, s.shape, 1), s, NEG)
                m_old = m_s[hh][:, qc]
                m_new = jnp.maximum(m_old, jnp.max(s, axis=0, keepdims=True))
                m_s[hh, :, qc] = m_new
                return jnp.exp2(s - m_new).astype(BF16), jnp.exp2(m_old - m_new)

            def accumulate(hh, p, alpha):
                sl = slice(hh * SLAB, hh * SLAB + V_DIM_B + ONES_ROWS)
                acc_s[sl, qc] = alpha * acc_s[sl, qc] + _dot(vt_ref[sl, :], p)

            s_next, pending = scores(0), None
            for hh in range(hps):
                s = s_next
                if hh + 1 < hps:
                    s_next = scores(hh + 1)
                p, alpha = softmax(hh, s)
                if pending is not None:
                    accumulate(*pending)
                pending = (hh, p, alpha)
            accumulate(*pending)

        @pl.when(j < ratio * i)
        def _():
            update(False, 0)

        for part in range(ratio):
            @pl.when(j == ratio * i + part)
            def _():
                update(True, part * tk)

        @pl.when(j == ratio * i + ratio - 1)
        def _():
            for hh in range(hps):
                sl = slice(hh * SLAB, (hh + 1) * SLAB)
                den = acc_s[hh * SLAB + V_DIM_B:hh * SLAB + V_DIM_B + 1, :]
                values = lax.broadcasted_iota(jnp.int32, (SLAB, tq), 0) < V_DIM_B
                ot = jnp.where(values, acc_s[sl, :] / den, 0.0)
                ot_ref[sl, :] = ot.astype(BF16)
                o_ref[:, sl] = ot.T.astype(BF16)
                l_ref[hh] = m_s[hh] + jnp.log2(den)

        @pl.when(last_step)
        def _():
            _wait_copies(*_direct_copies(late_refs, gathered_refs, send_sems, recv_sems, local_sems, True))

    grid_spec = pltpu.PrefetchScalarGridSpec(
        num_scalar_prefetch=2, grid=(N_HEADS // hps, len(pairs)),
        in_specs=[pl.BlockSpec((w, tq), lambda h, n, it, jt: (h, it[n])),
                  pl.BlockSpec((tk, w), lambda h, n, it, jt: (jt[n], h)),
                  pl.BlockSpec((w, tk), lambda h, n, it, jt: (h, jt[n]))] + [ANY_SPEC] * n_late,
        out_specs=[pl.BlockSpec((tq, w), lambda h, n, it, jt: (it[n], h)),
                   pl.BlockSpec((w, tq), lambda h, n, it, jt: (h, it[n])),
                   pl.BlockSpec((hps, 1, tq), lambda h, n, it, jt: (h, 0, it[n]))] + [ANY_SPEC] * n_late,
        scratch_shapes=[pltpu.VMEM((hps, 1, tq), F32), pltpu.VMEM((w, tq), F32)] + _exchange_scratch(n_late))
    outs = pl.pallas_call(
        body, name="mla_fwd", grid_spec=grid_spec,
        out_shape=[jax.ShapeDtypeStruct((t, HM), BF16), jax.ShapeDtypeStruct((HM, t), BF16),
                   jax.ShapeDtypeStruct((N_HEADS, 1, t), F32)]
        + [jax.ShapeDtypeStruct((N_DEV,) + a.shape, a.dtype) for a in late],
        compiler_params=_params(("arbitrary", "arbitrary")),
    )(i_tab, j_tab, qt, kb, vt, *late)
    return outs[0], outs[1], outs[2], list(outs[3:])


def _mla_bwd(qt, kb, kt, vb, d_ob_t, lse, delta, grad_slices):
    t = kb.shape[0]
    tk = _attn_tile(t)
    ratio = 2 if t >= 2 * tk else 1
    tq = ratio * tk
    nk, nq = t // tk, t // tq
    hps = MLA_HEADS_PER_STEP
    w = hps * SLAB
    pairs = [(j, i) for j in range(nk) for i in range(j // ratio, nq)]
    j_tab = jnp.asarray(np.array([p[0] for p in pairs], np.int32))
    i_tab = jnp.asarray(np.array([p[1] for p in pairs], np.int32))

    n_ex = len(grad_slices)

    def body(jt_ref, it_ref, qt_ref, dot_ref, l_ref, dl_ref, k_ref, kt_ref, v_ref, *rest):
        slice_refs, (dqt_ref, dkt_ref, dvt_ref) = rest[:n_ex], rest[n_ex:n_ex + 3]
        part_refs = rest[n_ex + 3:2 * n_ex + 3]
        dk_s, dv_s, send_sems, recv_sems, local_sems = rest[2 * n_ex + 3:]
        n = pl.program_id(1)
        j, i = jt_ref[n], it_ref[n]
        first_step = jnp.logical_and(pl.program_id(0) == 0, n == 0)
        last_step = jnp.logical_and(pl.program_id(0) == N_HEADS // hps - 1, n == len(pairs) - 1)

        @pl.when(first_step)
        def _():
            _start_copies(*_direct_copies(slice_refs, part_refs, send_sems, recv_sems, local_sems, False))

        @pl.when(n == 0)
        def _():
            dqt_ref[...] = jnp.zeros_like(dqt_ref)

        def update(diagonal, q0):
            qc = slice(q0, tq)
            cols = pl.ds(pl.multiple_of(i * tq + q0, tk), tq - q0)

            def products(hh):
                sl = slice(hh * SLAB, (hh + 1) * SLAB)
                return _dot(k_ref[:, sl], qt_ref[sl, qc]), _dot(v_ref[:, sl], dot_ref[sl, qc])

            def softmax_bwd(hh, s, dp):
                if diagonal:
                    s = jnp.where(lax.broadcasted_iota(jnp.int32, s.shape, 0)
                                  <= lax.broadcasted_iota(jnp.int32, s.shape, 1), s, NEG)
                p = jnp.exp2(s - l_ref[hh][:, qc])
                return p.astype(BF16), (p * (dp - dl_ref[hh][:, qc])).astype(BF16)

            def gradients(hh, p, ds):
                base = hh * SLAB
                vrows = slice(base, base + V_DIM_B)
                qrows = slice(base, base + QK_NOPE + QK_ROPE)
                dv = _dot_nt(dot_ref[vrows, qc], p)
                dk = _dot_nt(qt_ref[qrows, qc], ds)
                if diagonal:
                    dv_s[base:base + SLAB, :] = jnp.concatenate([dv, jnp.zeros((SLAB - V_DIM_B, tk), F32)], axis=0)
                    dk_s[base:base + SLAB, :] = jnp.concatenate(
                        [dk, jnp.zeros((SLAB - QK_NOPE - QK_ROPE, tk), F32)], axis=0)
                else:
                    dv_s[vrows, :] += dv
                    dk_s[qrows, :] += dk
                dqt_ref[qrows, cols] += _dot(kt_ref[qrows, :], ds)

            for hh in range(hps):
                gradients(hh, *softmax_bwd(hh, *products(hh)))

        first_tile = lax.div(j, ratio)
        for part in range(ratio):
            @pl.when(jnp.logical_and(i == first_tile, lax.rem(j, ratio) == part))
            def _():
                update(True, part * tk)

        @pl.when(i > first_tile)
        def _():
            update(False, 0)

        @pl.when(i == nq - 1)
        def _():
            dkt_ref[...] = (dk_s[...] * (1.0 / LOG2E)).astype(BF16)
            dvt_ref[...] = dv_s[...].astype(BF16)

        @pl.when(last_step)
        def _():
            _wait_copies(*_direct_copies(slice_refs, part_refs, send_sems, recv_sems, local_sems, False))

    grid_spec = pltpu.PrefetchScalarGridSpec(
        num_scalar_prefetch=2, grid=(N_HEADS // hps, len(pairs)),
        in_specs=[pl.BlockSpec((w, tq), lambda h, n, jt, it: (h, it[n])),
                  pl.BlockSpec((w, tq), lambda h, n, jt, it: (h, it[n])),
                  pl.BlockSpec((hps, 1, tq), lambda h, n, jt, it: (h, 0, it[n])),
                  pl.BlockSpec((hps, 1, tq), lambda h, n, jt, it: (h, 0, it[n])),
                  pl.BlockSpec((tk, w), lambda h, n, jt, it: (jt[n], h)),
                  pl.BlockSpec((w, tk), lambda h, n, jt, it: (h, jt[n])),
                  pl.BlockSpec((tk, w), lambda h, n, jt, it: (jt[n], h))] + [ANY_SPEC] * n_ex,
        out_specs=[pl.BlockSpec((w, t), lambda h, n, jt, it: (h, 0)),
                   pl.BlockSpec((w, tk), lambda h, n, jt, it: (h, jt[n])),
                   pl.BlockSpec((w, tk), lambda h, n, jt, it: (h, jt[n]))] + [ANY_SPEC] * n_ex,
        scratch_shapes=[pltpu.VMEM((w, tk), F32), pltpu.VMEM((w, tk), F32)] + _exchange_scratch(n_ex))
    outs = pl.pallas_call(
        body, name="mla_bwd", grid_spec=grid_spec,
        out_shape=[jax.ShapeDtypeStruct((HM, t), F32), jax.ShapeDtypeStruct((HM, t), BF16),
                   jax.ShapeDtypeStruct((HM, t), BF16)]
        + [jax.ShapeDtypeStruct(a.shape, a.dtype) for a in grad_slices],
        compiler_params=_params(("arbitrary", "arbitrary")),
    )(j_tab, i_tab, qt, d_ob_t, lse, delta, kb, kt, vb, *grad_slices)
    return outs[0], outs[1], outs[2], list(outs[3:])


def _merge_fwd(out_a, out_b, gates, x, w_oa, w_ob, w_out, g2, g3):
    t = x.shape[0]
    tm = _token_tile(t)

    def body(oa_ref, ob_ref, gates_ref, x_ref, woa_ref, wob_ref, wout_ref, g2_ref, g3_ref,
             oap_ref, obp_ref, merged_ref, y_ref, x1_ref, h2_ref):
        oa_p = _dot(oa_ref[...], woa_ref[...])
        ob_p = _dot(ob_ref[...], wob_ref[...])
        oap_ref[...] = oa_p.astype(BF16)
        obp_ref[...] = ob_p.astype(BF16)
        sa = _sigmoid(gates_ref[:, 0:D_MODEL])
        sb = _sigmoid(gates_ref[:, D_MODEL:2 * D_MODEL])
        merged = (sa * oa_p + sb * ob_p).astype(BF16)
        merged_ref[...] = merged
        y = _dot(merged, wout_ref[...])
        y_ref[...] = y
        x1 = x_ref[...] + y * _rms_r(y) * g2_ref[...]
        x1_ref[...] = x1
        h2_ref[...] = (x1 * _rms_r(x1) * g3_ref[...]).astype(BF16)

    def sds(dt):
        return jax.ShapeDtypeStruct((t, D_MODEL), dt)

    row = _row_spec(tm, D_MODEL)
    return pl.pallas_call(
        body, name="merge_fwd", grid=(t // tm,),
        in_specs=[_row_spec(tm, HM), _row_spec(tm, HM), _row_spec(tm, 2 * D_MODEL), row,
                  _full_spec((HM, D_MODEL)), _full_spec((HM, D_MODEL)), _full_spec((D_MODEL, D_MODEL)),
                  _full_spec((1, D_MODEL)), _full_spec((1, D_MODEL))],
        out_specs=[row] * 6,
        out_shape=[sds(BF16), sds(BF16), sds(BF16), sds(F32), sds(F32), sds(BF16)],
        compiler_params=_params(("parallel",)),
    )(out_a, out_b, gates, x, w_oa, w_ob, w_out, g2, g3)


def _merge_bwd(dx1, y, gates, oa_p, ob_p, out_b_t, w_oa, w_ob, w_out, g2):
    t = dx1.shape[0]
    tm = _token_tile(t)

    def body(dx1_ref, y_ref, gates_ref, oap_ref, obp_ref, obt_ref, woa_ref, wob_ref, wout_ref, g2_ref,
             dy_ref, doap_ref, dobp_ref, dgates_ref, doa_ref, dobt_ref, dlb_ref, dg2_ref):
        dx1v = dx1_ref[...]
        yv = y_ref[...]
        r2 = _rms_r(yv)
        _acc_rows(dg2_ref, dx1v * yv * r2)
        dy = _rms_bwd(yv, r2, g2_ref[...], dx1v).astype(BF16)
        dy_ref[...] = dy
        dm = _dot_nt(dy, wout_ref[...])
        sa = _sigmoid(gates_ref[:, 0:D_MODEL])
        sb = _sigmoid(gates_ref[:, D_MODEL:2 * D_MODEL])
        d_oap = (dm * sa).astype(BF16)
        d_obp = (dm * sb).astype(BF16)
        doap_ref[...] = d_oap
        dobp_ref[...] = d_obp
        dgates_ref[:, 0:D_MODEL] = (dm * oap_ref[...].astype(F32) * sa * (1.0 - sa)).astype(BF16)
        dgates_ref[:, D_MODEL:2 * D_MODEL] = (dm * obp_ref[...].astype(F32) * sb * (1.0 - sb)).astype(BF16)
        doa_ref[...] = _dot_nt(d_oap, woa_ref[...]).astype(BF16)
        d_ob_t = _dot_nt(wob_ref[...], d_obp)
        dobt_ref[...] = d_ob_t.astype(BF16)
        for hd in range(N_HEADS):
            sl = slice(hd * SLAB, (hd + 1) * SLAB)
            dlb_ref[hd] = jnp.sum(d_ob_t[sl, :] * obt_ref[sl, :].astype(F32), axis=0, keepdims=True)

    def sds(n, dt):
        return jax.ShapeDtypeStruct((t, n), dt)

    row = _row_spec(tm, D_MODEL)
    head3 = pl.BlockSpec((N_HEADS, 1, tm), lambda i: (0, 0, i))
    return pl.pallas_call(
        body, name="merge_bwd", grid=(t // tm,),
        in_specs=[row, row, _row_spec(tm, 2 * D_MODEL), row, row, _col_spec(HM, tm),
                  _full_spec((HM, D_MODEL)), _full_spec((HM, D_MODEL)), _full_spec((D_MODEL, D_MODEL)),
                  _full_spec((1, D_MODEL))],
        out_specs=[row, row, row, _row_spec(tm, 2 * D_MODEL), _row_spec(tm, HM), _col_spec(HM, tm),
                   head3, _full_spec((1, D_MODEL))],
        out_shape=[sds(D_MODEL, BF16), sds(D_MODEL, BF16), sds(D_MODEL, BF16), sds(2 * D_MODEL, BF16),
                   sds(HM, BF16), jax.ShapeDtypeStruct((HM, t), BF16),
                   jax.ShapeDtypeStruct((N_HEADS, 1, t), F32), jax.ShapeDtypeStruct((1, D_MODEL), F32)],
        compiler_params=_params(("arbitrary",)),
    )(dx1, y, gates, oa_p, ob_p, out_b_t, w_oa, w_ob, w_out, g2)


def _mlp_fwd_bwd(x1, h2, target, w_up, w_down, g3, g4):
    t = x1.shape[0]
    tm = _token_tile(t)
    fs = D_FF // N_DEV

    def body(x1_ref, h2_ref, tgt_ref, wup_ref, wdown_ref, g3_ref, g4_ref,
             a_ref, du_ref, dy2_ref, dx1_ref, loss_ref, dg3_ref, dg4_ref):
        x1v = x1_ref[...]
        h2v = h2_ref[...]
        u = jnp.concatenate([_dot(h2v, wup_ref[s]) for s in range(N_DEV)], axis=1)
        ru = jnp.maximum(u, 0.0)
        a = (ru * ru).astype(BF16)
        a_ref[...] = a
        y2 = _dot(a, wdown_ref[...])
        r4 = _rms_r(y2)
        diff = x1v + y2 * r4 * g4_ref[...] - tgt_ref[...]
        _acc_rows(loss_ref, jnp.sum(diff * diff, axis=-1, keepdims=True) * (0.5 / D_MODEL)
                  * jnp.ones((1, SLAB), F32))
        dx2 = diff * (1.0 / D_MODEL)
        _acc_rows(dg4_ref, dx2 * y2 * r4)
        dy2 = _rms_bwd(y2, r4, g4_ref[...], dx2).astype(BF16)
        dy2_ref[...] = dy2
        du = (_dot_nt(dy2, wdown_ref[...]) * (2.0 * ru)).astype(BF16)
        du_ref[...] = du
        dh2 = _dot_nt(du[:, 0:fs], wup_ref[0])
        for s in range(1, N_DEV):
            dh2 += _dot_nt(du[:, s * fs:(s + 1) * fs], wup_ref[s])
        r3 = _rms_r(x1v)
        _acc_rows(dg3_ref, dh2 * x1v * r3)
        dx1_ref[...] = dx2 + _rms_bwd(x1v, r3, g3_ref[...], dh2)

    row = _row_spec(tm, D_MODEL)
    frow = _row_spec(tm, D_FF)
    vec = _full_spec((1, D_MODEL))
    return pl.pallas_call(
        body, name="mlp_fwd_bwd", grid=(t // tm,),
        in_specs=[row, row, row, _full_spec((N_DEV, D_MODEL, fs)), _full_spec((D_FF, D_MODEL)), vec, vec],
        out_specs=[frow, frow, row, row, _full_spec((1, SLAB)), vec, vec],
        out_shape=[jax.ShapeDtypeStruct((t, D_FF), BF16), jax.ShapeDtypeStruct((t, D_FF), BF16),
                   jax.ShapeDtypeStruct((t, D_MODEL), BF16), jax.ShapeDtypeStruct((t, D_MODEL), F32),
                   jax.ShapeDtypeStruct((1, SLAB), F32), jax.ShapeDtypeStruct((1, D_MODEL), F32),
                   jax.ShapeDtypeStruct((1, D_MODEL), F32)],
        compiler_params=_params(("arbitrary",)),
    )(x1, h2, target, w_up, w_down, g3, g4)


def _inproj_bwd(dgates, dqa, dka, dva, dqb_t, dkb_t, dvb_t, cq, ckv, x, dx1, rope_ct, rope_s1t, rope_s2t,
                g1, g_q, g_kv, w_in, w_qb, w_kvb):
    t = x.shape[0]
    tm = _token_tile(t)

    def body(dgates_ref, dqa_ref, dka_ref, dva_ref, dqt_ref, dkt_ref, dvt_ref, cq_ref, ckv_ref, x_ref, dx1_ref,
             ct_ref, s1t_ref, s2t_ref, g1_ref, gq_ref, gkv_ref, win_ref, wqb_ref, wkvb_ref,
             dproj_ref, dqbrt_ref, dkvbt_ref, dx_ref, dg1_ref, dgq_ref, dgkv_ref):
        ct, s1t, s2t = ct_ref[...], s1t_ref[...], s2t_ref[...]
        dk_sum_t = jnp.zeros((SLAB, tm), F32)
        for hd in range(N_HEADS):
            sl = slice(hd * SLAB, (hd + 1) * SLAB)
            dqbrt_ref[sl, :] = _rope_t_bwd(dqt_ref[sl, :] * SCALE_B, ct, s1t, s2t).astype(BF16)
            dk_sum_t += dkt_ref[sl, :].astype(F32)
        dkvbt_ref[0:HM, :] = dkt_ref[...]
        dkvbt_ref[HM:2 * HM, :] = dvt_ref[...]
        dkr = _rope_t_bwd(dk_sum_t, ct, s1t, s2t).T
        dcqn = _dot(wqb_ref[...], dqbrt_ref[...]).T
        cq = cq_ref[...]
        rq = _rms_r(cq)
        _acc_rows(dgq_ref, dcqn * cq * rq)
        dcq = _rms_bwd(cq, rq, gq_ref[...], dcqn)
        dckvn = _dot(wkvb_ref[...], dkvbt_ref[...]).T
        ckv = ckv_ref[...]
        rkv = _rms_r(ckv)
        _acc_rows(dgkv_ref, dckvn * ckv * rkv)
        dckv = _rms_bwd(ckv, rkv, gkv_ref[...], dckvn)
        dproj_ref[:, C_GATES:C_QA] = dgates_ref[...]
        dproj_ref[:, C_QA:C_KA] = dqa_ref[...]
        dproj_ref[:, C_KA:C_VA] = dka_ref[...]
        dproj_ref[:, C_VA:C_CQ] = dva_ref[...]
        dproj_ref[:, C_CQ:C_CKV] = dcq.astype(BF16)
        dproj_ref[:, C_CKV:C_KR] = dckv.astype(BF16)
        dproj_ref[:, C_KR:D_IN_PAD] = dkr.astype(BF16)
        dh = _dot_nt(dproj_ref[...], win_ref[...])
        xv = x_ref[...]
        r1 = _rms_r(xv)
        _acc_rows(dg1_ref, dh * xv * r1)
        dx_ref[...] = dx1_ref[...] + _rms_bwd(xv, r1, g1_ref[...], dh)

    kvw = N_KV_A * SLAB
    row = _row_spec(tm, D_MODEL)
    hm = _row_spec(tm, HM)
    hmt = _col_spec(HM, tm)
    tab = _col_spec(SLAB, tm)
    return pl.pallas_call(
        body, name="inproj_bwd", grid=(t // tm,),
        in_specs=[_row_spec(tm, 2 * D_MODEL), hm, _row_spec(tm, kvw), _row_spec(tm, kvw), hmt, hmt, hmt,
                  _row_spec(tm, Q_LORA), _row_spec(tm, KV_LORA), row, row, tab, tab, tab,
                  _full_spec((1, D_MODEL)), _full_spec((1, Q_LORA)), _full_spec((1, KV_LORA)),
                  _full_spec((D_MODEL, D_IN_PAD)), _full_spec((Q_LORA, HM)), _full_spec((KV_LORA, 2 * HM))],
        out_specs=[_row_spec(tm, D_IN_PAD), hmt, _col_spec(2 * HM, tm), row,
                   _full_spec((1, D_MODEL)), _full_spec((1, Q_LORA)), _full_spec((1, KV_LORA))],
        out_shape=[jax.ShapeDtypeStruct((t, D_IN_PAD), BF16), jax.ShapeDtypeStruct((HM, t), BF16),
                   jax.ShapeDtypeStruct((2 * HM, t), BF16), jax.ShapeDtypeStruct((t, D_MODEL), F32),
                   jax.ShapeDtypeStruct((1, D_MODEL), F32), jax.ShapeDtypeStruct((1, Q_LORA), F32),
                   jax.ShapeDtypeStruct((1, KV_LORA), F32)],
        compiler_params=_params(("arbitrary",)),
    )(dgates, dqa, dka, dva, dqb_t, dkb_t, dvb_t, cq, ckv, x, dx1, rope_ct, rope_s1t, rope_s2t,
      g1, g_q, g_kv, w_in, w_qb, w_kvb)


def _matmul_tn(a, b, name, out_dtype=F32, n_shards=1):
    t, k = a.shape
    n = b.shape[1]
    bt = min(t, 512)
    bk = min(k, 1024)
    bn = min(n, 2048)
    ns = n // n_shards
    per_block = bn // ns
    steps = t // bt

    def body(a_ref, b_ref, o_ref, acc):
        s = pl.program_id(2)

        @pl.when(s == 0)
        def _():
            acc[...] = jnp.zeros_like(acc)

        acc[...] += _dot_tn(a_ref[...], b_ref[...])

        @pl.when(s == steps - 1)
        def _():
            if n_shards > 1:
                for p in range(per_block):
                    o_ref[p] = acc[:, p * ns:(p + 1) * ns].astype(out_dtype)
            else:
                o_ref[...] = acc[...].astype(out_dtype)

    if n_shards > 1:
        out_spec = pl.BlockSpec((per_block, bk, ns), lambda i, j, s: (j, i, 0))
        out_shape = jax.ShapeDtypeStruct((n_shards, k, ns), out_dtype)
    else:
        out_spec = pl.BlockSpec((bk, bn), lambda i, j, s: (i, j))
        out_shape = jax.ShapeDtypeStruct((k, n), out_dtype)
    return pl.pallas_call(
        body, name=name, grid=(k // bk, n // bn, steps),
        in_specs=[pl.BlockSpec((bt, bk), lambda i, j, s: (s, i)), pl.BlockSpec((bt, bn), lambda i, j, s: (s, j))],
        out_specs=out_spec, out_shape=out_shape, scratch_shapes=[pltpu.VMEM((bk, bn), F32)],
        compiler_params=_params(("parallel", "parallel", "arbitrary")),
    )(a, b)


def _matmul_nn(a_t, b, name):
    m, t = a_t.shape
    n = b.shape[1]
    bt = min(t, 512)
    steps = t // bt

    def body(a_ref, b_ref, o_ref, acc):
        s = pl.program_id(0)

        @pl.when(s == 0)
        def _():
            acc[...] = jnp.zeros_like(acc)

        acc[...] += _dot(a_ref[...], b_ref[...])

        @pl.when(s == steps - 1)
        def _():
            o_ref[...] = acc[...]

    return pl.pallas_call(
        body, name=name, grid=(steps,),
        in_specs=[pl.BlockSpec((m, bt), lambda s: (0, s)), pl.BlockSpec((bt, n), lambda s: (s, 0))],
        out_specs=pl.BlockSpec((m, n), lambda s: (0, 0)), out_shape=jax.ShapeDtypeStruct((m, n), F32),
        scratch_shapes=[pltpu.VMEM((m, n), F32)],
        compiler_params=_params(("arbitrary",)),
    )(a_t, b)


def _all_gather(shards):
    n = len(shards)

    def body(*refs):
        srcs, dsts = refs[:n], refs[n:2 * n]
        send_sems, recv_sems, local_sems = refs[2 * n:]
        x, y, c = _mesh_pos()
        me, sibling = (x, y, c), (x, y, 1 - c)
        chips = [(1 - x, y), (x, 1 - y), (1 - x, 1 - y)]

        def slot(a, px, py, pc):
            return dsts[a].at[4 * px + 2 * py + pc]

        def copy(a, k, block, to, src=None):
            return pltpu.make_async_remote_copy(
                src_ref=slot(a, *block) if src is None else src, dst_ref=slot(a, *block),
                send_sem=send_sems.at[(N_DEV - 1) * a + k], recv_sem=recv_sems.at[(N_DEV - 1) * a + k],
                device_id=to, device_id_type=pl.DeviceIdType.MESH)

        mine = [pltpu.make_async_copy(srcs[a], slot(a, *me), local_sems.at[a]) for a in range(n)]
        first = []
        for a in range(n):
            first.append(copy(a, 0, me, sibling, src=srcs[a]))
            first += [copy(a, 1 + j, me, (*chip, c), src=srcs[a]) for j, chip in enumerate(chips)]
        for cp in mine + first:
            cp.start()
        passed = []
        for j, chip in enumerate(chips):
            for a in range(n):
                copy(a, 1 + j, (*chip, c), me).wait_recv()
                passed.append(copy(a, 4 + j, (*chip, c), sibling))
                passed[-1].start()
        for a in range(n):
            copy(a, 0, sibling, me).wait_recv()
        for j, chip in enumerate(chips):
            for a in range(n):
                copy(a, 4 + j, (*chip, 1 - c), me).wait_recv()
        for cp in first + passed:
            cp.wait_send()
        for cp in mine:
            cp.wait()

    return pl.pallas_call(
        body, name="all_gather_early",
        out_shape=[jax.ShapeDtypeStruct((N_DEV,) + a.shape, a.dtype) for a in shards],
        in_specs=[ANY_SPEC] * n, out_specs=[ANY_SPEC] * n, scratch_shapes=_exchange_scratch(n),
    )(*shards)


def _exchange_grads(slices, small):
    n = len(slices)

    def body(*refs):
        srcs, s_ref = refs[:n], refs[n]
        dsts, s_dst = refs[n + 1:2 * n + 1], refs[2 * n + 1]
        sems = refs[2 * n + 2:]
        parts = _direct_copies(srcs, dsts, *sems, False)
        smalls = _direct_copies([s_ref], [s_dst], *sems, True, sem_base=n)
        _start_copies(*parts)
        _start_copies(*smalls)
        _wait_copies(*parts)
        _wait_copies(*smalls)

    outs = pl.pallas_call(
        body, name="exchange_grads",
        out_shape=[jax.ShapeDtypeStruct(a.shape, a.dtype) for a in slices]
        + [jax.ShapeDtypeStruct((N_DEV,) + small.shape, small.dtype)],
        in_specs=[ANY_SPEC] * (n + 1), out_specs=[ANY_SPEC] * (n + 1), scratch_shapes=_exchange_scratch(n + 1),
    )(*slices, small)
    return list(outs[:n]), outs[n]


def _adamw(parts, w, m, v, name):
    _, k, n = parts.shape
    bk = min(k, ADAM_ROWS)
    c1 = 1.0 - ADAM_B1 ** ADAM_STEP
    c2 = 1.0 - ADAM_B2 ** ADAM_STEP

    def body(p_ref, w_ref, m_ref, v_ref, g_ref, d_ref, mo_ref, vo_ref):
        g = p_ref[0].astype(F32)
        for s in range(1, N_DEV):
            g = g + p_ref[s].astype(F32)
        g_ref[0] = g
        m_new = ADAM_B1 * m_ref[0] + (1.0 - ADAM_B1) * g
        v_new = ADAM_B2 * v_ref[0] + (1.0 - ADAM_B2) * (g * g)
        mo_ref[0] = m_new
        vo_ref[0] = v_new
        m_hat = m_new / c1
        v_hat = v_new / c2
        d_ref[0] = -ADAM_LR * (m_hat / (jnp.sqrt(v_hat) + ADAM_EPS) + ADAM_WD * w_ref[0])

    blk = pl.BlockSpec((1, bk, n), lambda i: (0, i, 0))
    out = jax.ShapeDtypeStruct((1, k, n), F32)
    return pl.pallas_call(
        body, name=name, grid=(k // bk,),
        in_specs=[pl.BlockSpec((N_DEV, bk, n), lambda i: (0, i, 0)), blk, blk, blk],
        out_specs=[blk] * 4, out_shape=[out] * 4,
        compiler_params=_params(("parallel",)),
    )(parts, w, m, v)


def _pad_heads_cols(w, heads, width):
    k = w.shape[0]
    w = w.reshape(k, heads, width)
    return jnp.pad(w, ((0, 0), (0, 0), (0, SLAB - width))).reshape(k, heads * SLAB)


def _unpad_heads_cols(w, heads, width):
    k = w.shape[0]
    return w.reshape(k, heads, SLAB)[:, :, :width].reshape(k, heads * width)


def _pad_heads_rows(w, heads, width):
    n = w.shape[1]
    w = w.reshape(heads, width, n)
    return jnp.pad(w, ((0, 0), (0, SLAB - width), (0, 0))).reshape(heads * SLAB, n)


def _unpad_heads_rows(w, heads, width):
    n = w.shape[1]
    return w.reshape(heads, SLAB, n)[:, :width, :].reshape(heads * width, n)


def _pad_w_in(w_in):
    o = 2 * D_MODEL
    qa = _pad_heads_cols(w_in[:, o:o + 512], N_HEADS, HEAD_A)
    ka = _pad_heads_cols(w_in[:, o + 512:o + 640], N_KV_A, HEAD_A)
    va = _pad_heads_cols(w_in[:, o + 640:o + 768], N_KV_A, HEAD_A)
    kr = jnp.pad(w_in[:, o + 1152:o + 1184], ((0, 0), (QK_NOPE, SLAB - QK_NOPE - QK_ROPE)))
    return jnp.concatenate([w_in[:, :o], qa, ka, va, w_in[:, o + 768:o + 1152], kr], axis=1)


def _unpad_w_in(w):
    qa = _unpad_heads_cols(w[:, C_QA:C_KA], N_HEADS, HEAD_A)
    ka = _unpad_heads_cols(w[:, C_KA:C_VA], N_KV_A, HEAD_A)
    va = _unpad_heads_cols(w[:, C_VA:C_CQ], N_KV_A, HEAD_A)
    kr = w[:, C_KR + QK_NOPE:C_KR + QK_NOPE + QK_ROPE]
    return jnp.concatenate([w[:, :C_QA], qa, ka, va, w[:, C_CQ:C_KR], kr], axis=1)


def _pad_w_kvb(w_kvb):
    w = w_kvb.reshape(KV_LORA, N_HEADS, QK_NOPE + V_DIM_B)
    k = jnp.pad(w[:, :, :QK_NOPE], ((0, 0), (0, 0), (0, SLAB - QK_NOPE))).reshape(KV_LORA, HM)
    v = jnp.pad(w[:, :, QK_NOPE:], ((0, 0), (0, 0), (0, SLAB - V_DIM_B))).reshape(KV_LORA, HM)
    return jnp.concatenate([k, v], axis=1)


def _unpad_w_kvb(w):
    k = w[:, :HM].reshape(KV_LORA, N_HEADS, SLAB)[:, :, :QK_NOPE]
    v = w[:, HM:].reshape(KV_LORA, N_HEADS, SLAB)[:, :, :V_DIM_B]
    return jnp.concatenate([k, v], axis=2).reshape(KV_LORA, N_HEADS * (QK_NOPE + V_DIM_B))


def _col_shards(w):
    k, n = w.shape
    return w.reshape(k, N_DEV, n // N_DEV).transpose(1, 0, 2)


def _from_col_shards(s):
    _, k, ns = s.shape
    return s.transpose(1, 0, 2).reshape(k, N_DEV * ns)


def _freq_row():
    freqs = ROPE_THETA ** (-jnp.arange(0, QK_ROPE, 2, dtype=F32) / QK_ROPE)
    return jnp.concatenate([jnp.zeros((QK_NOPE,), F32), freqs, freqs,
                            jnp.zeros((SLAB - QK_NOPE - QK_ROPE,), F32)]).reshape(1, SLAB)


SMALL_D_ROWS = ("pre_norm_mix", "post_norm_mix", "pre_norm_mlp", "post_norm_mlp")
SMALL_Q_OFF, SMALL_KV_OFF, SMALL_SINK_OFF, SMALL_LOSS_OFF = 0, 256, 384, 392


def _pack_small(vals):
    row4 = jnp.concatenate([vals["q_a_norm"].reshape(-1), vals["kv_a_norm"].reshape(-1), vals["sinks"].reshape(-1),
                            vals["loss"].reshape(-1), jnp.zeros((1024 - 393,), F32)])
    rows = [vals[n].reshape(1024) for n in SMALL_D_ROWS] + [row4]
    return jnp.concatenate([jnp.stack(rows), jnp.zeros((SMALL_ROWS - 5, 1024), F32)], axis=0)


def _unpack_small(blk):
    out = {n: blk[i].reshape(1, 1024) for i, n in enumerate(SMALL_D_ROWS)}
    out["q_a_norm"] = blk[4, SMALL_Q_OFF:SMALL_Q_OFF + 256].reshape(1, 256)
    out["kv_a_norm"] = blk[4, SMALL_KV_OFF:SMALL_KV_OFF + 128].reshape(1, 128)
    out["sinks"] = blk[4, SMALL_SINK_OFF:SMALL_SINK_OFF + 8].reshape(1, 8)
    out["loss"] = blk[4, SMALL_LOSS_OFF]
    return out


WEIGHT_ORDER = ("pre_norm_mix", "w_in", "q_a_norm", "w_q_b", "kv_a_norm", "w_kv_b", "sinks", "w_o_a", "w_o_b",
                "w_out", "post_norm_mix", "pre_norm_mlp", "w_up", "w_down", "post_norm_mlp")
SMALL_NAMES = ("pre_norm_mix", "q_a_norm", "kv_a_norm", "sinks", "post_norm_mix", "pre_norm_mlp", "post_norm_mlp")


def kernel(x, positions, pre_norm_mix, w_in, q_a_norm, w_q_b, kv_a_norm, w_kv_b, sinks, w_o_a, w_o_b, w_out, post_norm_mix, pre_norm_mlp, w_up, w_down, post_norm_mlp, loss_target, m_pre_norm_mix, m_w_in, m_q_a_norm, m_w_q_b, m_kv_a_norm, m_w_kv_b, m_sinks, m_w_o_a, m_w_o_b, m_w_out, m_post_norm_mix, m_pre_norm_mlp, m_w_up, m_w_down, m_post_norm_mlp, v_pre_norm_mix, v_w_in, v_q_a_norm, v_w_q_b, v_kv_a_norm, v_w_kv_b, v_sinks, v_w_o_a, v_w_o_b, v_w_out, v_post_norm_mix, v_pre_norm_mlp, v_w_up, v_w_down, v_post_norm_mlp):
    weights = dict(pre_norm_mix=pre_norm_mix, w_in=w_in, q_a_norm=q_a_norm, w_q_b=w_q_b, kv_a_norm=kv_a_norm,
                   w_kv_b=w_kv_b, sinks=sinks, w_o_a=w_o_a, w_o_b=w_o_b, w_out=w_out, post_norm_mix=post_norm_mix,
                   pre_norm_mlp=pre_norm_mlp, w_up=w_up, w_down=w_down, post_norm_mlp=post_norm_mlp)
    m_in = dict(pre_norm_mix=m_pre_norm_mix, w_in=m_w_in, q_a_norm=m_q_a_norm, w_q_b=m_w_q_b, kv_a_norm=m_kv_a_norm,
                w_kv_b=m_w_kv_b, sinks=m_sinks, w_o_a=m_w_o_a, w_o_b=m_w_o_b, w_out=m_w_out,
                post_norm_mix=m_post_norm_mix, pre_norm_mlp=m_pre_norm_mlp, w_up=m_w_up, w_down=m_w_down,
                post_norm_mlp=m_post_norm_mlp)
    v_in = dict(pre_norm_mix=v_pre_norm_mix, w_in=v_w_in, q_a_norm=v_q_a_norm, w_q_b=v_w_q_b, kv_a_norm=v_kv_a_norm,
                w_kv_b=v_w_kv_b, sinks=v_sinks, w_o_a=v_w_o_a, w_o_b=v_w_o_b, w_out=v_w_out,
                post_norm_mix=v_post_norm_mix, pre_norm_mlp=v_pre_norm_mlp, w_up=v_w_up, w_down=v_w_down,
                post_norm_mlp=v_post_norm_mlp)

    xs, pos, target = x[0], positions[0], loss_target[0]
    t = xs.shape[0]
    pos_col = pos.reshape(t, 1)
    pos_row = pos.reshape(1, t)
    g1, g2, g3, g4 = (weights[n] for n in SMALL_D_ROWS)
    g_q, g_kv = q_a_norm, kv_a_norm
    sink_vec = sinks.reshape(N_HEADS)
    shard = {n: weights[n][0].astype(BF16) for n in EARLY + LATE}

    e_in, e_qb, e_kvb = _all_gather([shard[n] for n in EARLY])
    w_in_p = _pad_w_in(_from_col_shards(e_in))
    w_qb = _pad_heads_cols(_from_col_shards(e_qb), N_HEADS, QK_NOPE + QK_ROPE)
    w_kvb = _pad_w_kvb(_from_col_shards(e_kvb))

    tables = _rope_tables(pos_col, _freq_row())
    (h, gates, qa, ka, va, cq, ckv, cqn, ckvn, kb, vb, qt, kt, vt) = _inproj_fwd(
        xs, g1, w_in_p, g_q, g_kv, w_kvb, w_qb.T, w_kvb[:, :HM].T, w_kvb[:, HM:].T, w_in_p[:, C_KR:].T, tables)
    out_a, lse_a = _swa_fwd(qa, ka, va, pos_col, pos_row, sink_vec)
    out_b, out_b_t, lse_b, (l_oa, l_ob, l_out, w_up_s, l_down) = _mla_fwd(qt, kb, vt, [shard[n] for n in LATE])
    w_oa = _pad_heads_rows(_from_col_shards(l_oa), N_HEADS, HEAD_A)
    w_ob = _pad_heads_rows(_from_col_shards(l_ob), N_HEADS, V_DIM_B)
    w_out_f = l_out.reshape(D_MODEL, D_MODEL)
    w_down_f = l_down.reshape(D_FF, D_MODEL)

    oa_p, ob_p, merged, y, x1, h2 = _merge_fwd(out_a, out_b, gates, xs, w_oa, w_ob, w_out_f, g2, g3)
    a, du, dy2, dx1, loss, dg3, dg4 = _mlp_fwd_bwd(x1, h2, target, w_up_s, w_down_f, g3, g4)
    (dy, d_oap, d_obp, dgates, d_oa, d_ob_t, delta_b, dg2) = _merge_bwd(
        dx1, y, gates, oa_p, ob_p, out_b_t, w_oa, w_ob, w_out_f, g2)
    late_slices = [
        _col_shards(_unpad_heads_rows(_matmul_tn(out_a, d_oap, "dw_o_a"), N_HEADS, HEAD_A)).astype(BF16),
        _col_shards(_unpad_heads_rows(_matmul_tn(out_b, d_obp, "dw_o_b"), N_HEADS, V_DIM_B)).astype(BF16),
        _matmul_tn(merged, dy, "dw_out", BF16).reshape(N_DEV, D_MODEL // N_DEV, D_MODEL),
        _matmul_tn(h2, du, "dw_up", BF16, N_DEV),
        _matmul_tn(a, dy2, "dw_down", BF16).reshape(N_DEV, D_FF // N_DEV, D_MODEL),
    ]
    dqa, dka, dva, dsink = _swa_bwd(qa, ka, va, out_a, d_oa, lse_a, pos_col, pos_row, sink_vec)
    dqb_t, dkb_t, dvb_t, late_parts = _mla_bwd(qt, kb, kt, vb, d_ob_t, lse_b, delta_b, late_slices)
    dproj, dqbr_t, dkvb_t, dx, dg1, dgq, dgkv = _inproj_bwd(
        dgates, dqa, dka, dva, dqb_t, dkb_t, dvb_t, cq, ckv, xs, dx1, *tables[3:], g1, g_q, g_kv,
        w_in_p, w_qb, w_kvb)
    early_slices = [
        _col_shards(_unpad_w_in(_matmul_tn(h, dproj, "dw_in"))).astype(BF16),
        _col_shards(_unpad_heads_cols(_matmul_nn(dqbr_t, cqn, "dw_q_b").T, N_HEADS, QK_NOPE + QK_ROPE)).astype(BF16),
        _col_shards(_unpad_w_kvb(_matmul_nn(dkvb_t, ckvn, "dw_kv_b").T)).astype(BF16),
    ]
    small_grads = {"pre_norm_mix": dg1, "post_norm_mix": dg2, "pre_norm_mlp": dg3, "post_norm_mlp": dg4,
                   "q_a_norm": dgq, "kv_a_norm": dgkv, "sinks": dsink.reshape(N_HEADS, BLOCK).sum(axis=1),
                   "loss": loss[0, 0:1]}
    early_parts, s_parts = _exchange_grads(early_slices, _pack_small(small_grads))

    updates = {}
    for name, parts in zip(EARLY + LATE, early_parts + late_parts):
        outs = _adamw(parts, weights[name], m_in[name], v_in[name], "adamw_" + name)
        for kind, arr in zip(("g", "d", "m", "v"), outs):
            updates[kind, name] = arr
    zero = jnp.zeros((), F32)
    pack = lambda src: _pack_small({**{n: src[n] for n in SMALL_NAMES}, "loss": zero})[None]
    smalls = _adamw(s_parts, pack(weights), pack(m_in), pack(v_in), "adamw_small")
    for kind, blk in zip(("g", "d", "m", "v"), smalls):
        for wname, piece in _unpack_small(blk[0]).items():
            updates[kind, wname] = piece
    results = [updates[kind, name] for kind in ("g", "d", "m", "v") for name in WEIGHT_ORDER]
    return (updates["g", "loss"], dx[None], *results)
```

```python
import functools

import numpy as np
import jax
import jax.numpy as jnp
from jax import lax
from jax.experimental import pallas as pl
from jax.experimental.pallas import tpu as pltpu

F32 = jnp.float32
BF16 = jnp.bfloat16

D_MODEL = 1024
D_FF = 4096
N_HEADS = 8
N_KV_A = 2
GROUP_A = N_HEADS // N_KV_A
HEAD_A = 64
QK_NOPE = 64
QK_ROPE = 32
V_DIM_B = 64
Q_LORA = 256
KV_LORA = 128
BLOCK = 128
SLAB = 128
ROPE_THETA = 10000.0
EPS = 1e-6
N_DEV = 8
NEG = -1e30

SCALE_A = HEAD_A ** -0.5
SCALE_B = (QK_NOPE + QK_ROPE) ** -0.5
LOG2E = 1.4426950408889634
SCORE_B = SCALE_B * LOG2E
MLA_HEADS_PER_STEP = 4
MLA_FWD_HEADS_PER_STEP = 8
ONES_ROWS = 16
SLOPES_A = tuple(2.0 ** (-8.0 * (h + 1) / N_HEADS) for h in range(N_HEADS))

ADAM_LR = 0.001
ADAM_B1 = 0.9
ADAM_B2 = 0.999
ADAM_EPS = 1e-08
ADAM_WD = 0.01
ADAM_STEP = 10

HM = N_HEADS * SLAB
C_GATES = 0
C_QA = 2 * D_MODEL
C_KA = C_QA + HM
C_VA = C_KA + N_KV_A * SLAB
C_CQ = C_VA + N_KV_A * SLAB
C_CKV = C_CQ + Q_LORA
C_KR = C_CKV + KV_LORA
D_IN_PAD = C_KR + SLAB

VMEM_LIMIT = 56 * 1024 * 1024

EARLY = ("w_in", "w_q_b", "w_kv_b")
LATE = ("w_o_a", "w_o_b", "w_out", "w_up", "w_down")
ADAM_ROWS = 256
SMALL_ROWS = 8


def _token_tile(t):
    return min(256, t)


def _attn_tile(t):
    return 512 if t >= 2048 else 128


def _params(sem, vmem=VMEM_LIMIT):
    return pltpu.CompilerParams(dimension_semantics=sem, vmem_limit_bytes=vmem)


def _dot(a, b):
    return jnp.dot(a, b, preferred_element_type=F32)


def _dot_nt(a, b):
    return lax.dot_general(a, b, (((1,), (1,)), ((), ())), preferred_element_type=F32)


def _dot_tn(a, b):
    return lax.dot_general(a, b, (((0,), (0,)), ((), ())), preferred_element_type=F32)


def _rms_r(x):
    return lax.rsqrt(jnp.mean(x * x, axis=-1, keepdims=True) + EPS)


def _rms_bwd(x, r, g, dy):
    t = dy * g
    return r * t - x * (r * r * r) * jnp.mean(x * t, axis=-1, keepdims=True)


def _sigmoid(x):
    return 1.0 / (1.0 + jnp.exp(-x))


def _rope(x, c, s1, s2):
    return x * c + pltpu.roll(x, SLAB - 16, 1) * s1 + pltpu.roll(x, 16, 1) * s2


def _rope_bwd(d, c, s1, s2):
    return d * c + pltpu.roll(d * s1, 16, 1) + pltpu.roll(d * s2, SLAB - 16, 1)


def _roll_rows(x, shift):
    return jnp.concatenate([x[-shift:], x[:-shift]], axis=0)


def _rope_t(x, c, s1, s2):
    return x * c + _roll_rows(x, SLAB - 16) * s1 + _roll_rows(x, 16) * s2


def _rope_t_bwd(d, c, s1, s2):
    return d * c + _roll_rows(d * s1, 16) + _roll_rows(d * s2, SLAB - 16)


def _row_spec(tm, n):
    return pl.BlockSpec((tm, n), lambda i: (i, 0))


def _col_spec(n, tm):
    return pl.BlockSpec((n, tm), lambda i: (0, i))


def _full_spec(shape):
    nd = len(shape)
    return pl.BlockSpec(shape, lambda i: (0,) * nd, pipeline_mode=pl.Buffered(1))


def _acc_rows(ref, val):
    @pl.when(pl.program_id(0) == 0)
    def _():
        ref[...] = jnp.zeros_like(ref)
    ref[...] += jnp.sum(val, axis=0, keepdims=True)


def _rope_tables(pos_col, freq_row, early):
    t = pos_col.shape[0]
    tm = _token_tile(t)
    n = len(early)

    def body(pos_ref, f_ref, *rest):
        shard_refs, (c_ref, s1_ref, s2_ref, ct_ref, s1t_ref, s2t_ref) = rest[:n], rest[n:n + 6]
        start, finish = _two_level_gather(shard_refs, rest[n + 6:2 * n + 6], *rest[2 * n + 6:])
        pl.when(pl.program_id(0) == 0)(start)
        ang = pos_ref[...].astype(F32) * f_ref[...]
        lane = lax.broadcasted_iota(jnp.int32, ang.shape, 1)
        s = jnp.sin(ang)
        c = jnp.cos(ang)
        s1 = jnp.where((lane >= 64) & (lane < 80), -s, 0.0)
        s2 = jnp.where((lane >= 80) & (lane < 96), s, 0.0)
        c_ref[...], s1_ref[...], s2_ref[...] = c, s1, s2
        ct_ref[...], s1t_ref[...], s2t_ref[...] = c.T, s1.T, s2.T
        pl.when(pl.program_id(0) == t // tm - 1)(finish)

    tab = jax.ShapeDtypeStruct((t, SLAB), F32)
    tabt = jax.ShapeDtypeStruct((SLAB, t), F32)
    outs = pl.pallas_call(
        body, name="rope_tables", grid=(t // tm,),
        in_specs=[_row_spec(tm, 1), _full_spec((1, SLAB))] + [ANY_SPEC] * n,
        out_specs=[_row_spec(tm, SLAB)] * 3 + [_col_spec(SLAB, tm)] * 3 + [ANY_SPEC] * n,
        out_shape=[tab] * 3 + [tabt] * 3 + [jax.ShapeDtypeStruct((N_DEV,) + a.shape, a.dtype) for a in early],
        scratch_shapes=_exchange_scratch(n),
        compiler_params=_params(("arbitrary",)),
    )(pos_col, freq_row, *early)
    return outs[:6], outs[6:]


def _inproj_fwd(x, g1, w_in, g_q, g_kv, w_kvb, w_qb_t, w_kb_t, w_vb_t, w_kr_t, tables):
    t = x.shape[0]
    tm = _token_tile(t)

    def body(x_ref, g1_ref, win_ref, gq_ref, gkv_ref, wkvb_ref, wqbt_ref, wkbt_ref, wvbt_ref, wkrt_ref,
             c_ref, s1_ref, s2_ref, ct_ref, s1t_ref, s2t_ref,
             h_ref, gates_ref, qa_ref, ka_ref, va_ref, cq_ref, ckv_ref, cqn_ref, ckvn_ref,
             kb_ref, vb_ref, qt_ref, kt_ref, vt_ref):
        xv = x_ref[...]
        h = (xv * _rms_r(xv) * g1_ref[...]).astype(BF16)
        h_ref[...] = h
        proj = _dot(h, win_ref[...])
        gates_ref[...] = proj[:, C_GATES:C_QA]
        qa_ref[...] = proj[:, C_QA:C_KA].astype(BF16)
        ka_ref[...] = proj[:, C_KA:C_VA].astype(BF16)
        va_ref[...] = proj[:, C_VA:C_CQ].astype(BF16)
        cq = proj[:, C_CQ:C_CKV]
        ckv = proj[:, C_CKV:C_KR]
        kr = proj[:, C_KR:D_IN_PAD]
        cq_ref[...] = cq
        ckv_ref[...] = ckv
        cqn = (cq * _rms_r(cq) * gq_ref[...]).astype(BF16)
        ckvn = (ckv * _rms_r(ckv) * gkv_ref[...]).astype(BF16)
        cqn_ref[...] = cqn
        ckvn_ref[...] = ckvn
        c, s1, s2 = c_ref[...], s1_ref[...], s2_ref[...]
        kvb = _dot(ckvn, wkvb_ref[...])
        kr_rot = _rope(kr, c, s1, s2)
        ct, s1t, s2t = ct_ref[...], s1t_ref[...], s2t_ref[...]
        q_t = _dot_nt(wqbt_ref[...], cqn)
        k_t = _dot_nt(wkbt_ref[...], ckvn)
        kr_t = _rope_t(_dot_nt(wkrt_ref[...], h), ct, s1t, s2t)
        for hd in range(N_HEADS):
            sl = slice(hd * SLAB, (hd + 1) * SLAB)
            kb_ref[:, sl] = (kvb[:, sl] + kr_rot).astype(BF16)
            qt_ref[sl, :] = (_rope_t(q_t[sl, :], ct, s1t, s2t) * SCORE_B).astype(BF16)
            kt_ref[sl, :] = (k_t[sl, :] + kr_t).astype(BF16)
        vb_ref[...] = kvb[:, HM:2 * HM].astype(BF16)
        pad_row = lax.broadcasted_iota(jnp.int32, (HM, 1), 0) & (SLAB - 1)
        ones_rows = jnp.where((pad_row >= V_DIM_B) & (pad_row < V_DIM_B + ONES_ROWS), 1.0, 0.0)
        vt_ref[...] = (_dot_nt(wvbt_ref[...], ckvn) + ones_rows).astype(BF16)

    def sds(n, dt):
        return jax.ShapeDtypeStruct((t, n), dt)

    outs = [(D_MODEL, BF16), (2 * D_MODEL, F32), (HM, BF16), (N_KV_A * SLAB, BF16), (N_KV_A * SLAB, BF16),
            (Q_LORA, F32), (KV_LORA, F32), (Q_LORA, BF16), (KV_LORA, BF16), (HM, BF16), (HM, BF16)]
    tab, tabt = _row_spec(tm, SLAB), _col_spec(SLAB, tm)
    return pl.pallas_call(
        body, name="inproj_fwd", grid=(t // tm,),
        in_specs=[_row_spec(tm, D_MODEL), _full_spec((1, D_MODEL)), _full_spec((D_MODEL, D_IN_PAD)),
                  _full_spec((1, Q_LORA)), _full_spec((1, KV_LORA)), _full_spec((KV_LORA, 2 * HM)),
                  _full_spec((HM, Q_LORA)), _full_spec((HM, KV_LORA)), _full_spec((HM, KV_LORA)),
                  _full_spec((SLAB, D_MODEL)), tab, tab, tab, tabt, tabt, tabt],
        out_specs=[_row_spec(tm, n) for n, _ in outs] + [_col_spec(HM, tm)] * 3,
        out_shape=[sds(n, dt) for n, dt in outs] + [jax.ShapeDtypeStruct((HM, t), BF16)] * 3,
        compiler_params=_params(("parallel",)),
    )(x, g1, w_in, g_q, g_kv, w_kvb, w_qb_t, w_kb_t, w_vb_t, w_kr_t, *tables)


def _tile_group(a):
    return jnp.concatenate([a] * GROUP_A, axis=1)


def _swa_masks():
    row = lax.broadcasted_iota(jnp.int32, (BLOCK, GROUP_A * BLOCK), 0)
    col = lax.broadcasted_iota(jnp.int32, (BLOCK, GROUP_A * BLOCK), 1) & (BLOCK - 1)
    return row <= col, row > col


def _heads_beside(ref, g):
    return jnp.concatenate([ref[:, (g * GROUP_A + hh) * SLAB:(g * GROUP_A + hh + 1) * SLAB].T
                            for hh in range(GROUP_A)], axis=1)


def _rows_beside(ref, g):
    return jnp.concatenate([ref[g * GROUP_A + hh] for hh in range(GROUP_A)], axis=1)


def _swa_rows(sinks):
    slopes = jnp.repeat(jnp.asarray(SLOPES_A, F32).reshape(N_KV_A, GROUP_A, 1), BLOCK, axis=2)
    sink_rows = jnp.repeat(sinks.reshape(N_KV_A, GROUP_A, 1), BLOCK, axis=2)
    return slopes.reshape(N_KV_A, 1, GROUP_A * BLOCK), sink_rows.reshape(N_KV_A, 1, GROUP_A * BLOCK)


def _swa_fwd(qa, ka, va, pos_col, pos_row, sinks):
    t = qa.shape[0]
    nb = t // BLOCK
    gw = GROUP_A * BLOCK
    slope_rows, sink_rows = _swa_rows(sinks)

    def body(q_ref, kc_ref, kp_ref, vc_ref, vp_ref, pkc_ref, pkp_ref, pq_ref, slope_ref, sink_ref, o_ref, l_ref):
        i = pl.program_id(0)
        pq = pq_ref[...]
        dist_c = _tile_group(jnp.abs(pkc_ref[...] - pq).astype(F32))
        dist_p = _tile_group(jnp.abs(pkp_ref[...] - pq).astype(F32))
        mask_c, older = _swa_masks()
        mask_p = jnp.logical_and(older, i > 0)
        for g in range(N_KV_A):
            gs = slice(g * SLAB, (g + 1) * SLAB)
            x = _heads_beside(q_ref, g)
            slope, sink = slope_ref[g], sink_ref[g]
            s_c = jnp.where(mask_c, _dot(kc_ref[:, gs], x) * SCALE_A - slope * dist_c, NEG)
            s_p = jnp.where(mask_p, _dot(kp_ref[:, gs], x) * SCALE_A - slope * dist_p, NEG)
            m = jnp.maximum(jnp.maximum(jnp.max(s_c, axis=0, keepdims=True),
                                        jnp.max(s_p, axis=0, keepdims=True)), sink)
            e_c = jnp.exp(s_c - m)
            e_p = jnp.exp(s_p - m)
            den = jnp.sum(e_c, axis=0, keepdims=True) + jnp.sum(e_p, axis=0, keepdims=True) + jnp.exp(sink - m)
            inv = 1.0 / den
            ot = (_dot_tn(vc_ref[:, gs], (e_c * inv).astype(BF16))
                  + _dot_tn(vp_ref[:, gs], (e_p * inv).astype(BF16)))
            lse = m + jnp.log(den)
            for hh in range(GROUP_A):
                hd = g * GROUP_A + hh
                seg = slice(hh * BLOCK, (hh + 1) * BLOCK)
                o_ref[:, hd * SLAB:(hd + 1) * SLAB] = ot[:, seg].T.astype(BF16)
                l_ref[hd] = lse[:, seg]

    cur = lambda i: (i, 0)
    prev = lambda i: (jnp.maximum(i - 1, 0), 0)
    kvw = N_KV_A * SLAB
    rows = pl.BlockSpec((N_KV_A, 1, gw), lambda i: (0, 0, 0))
    return pl.pallas_call(
        body, name="swa_fwd", grid=(nb,),
        in_specs=[pl.BlockSpec((BLOCK, HM), cur),
                  pl.BlockSpec((BLOCK, kvw), cur), pl.BlockSpec((BLOCK, kvw), prev),
                  pl.BlockSpec((BLOCK, kvw), cur), pl.BlockSpec((BLOCK, kvw), prev),
                  pl.BlockSpec((BLOCK, 1), cur), pl.BlockSpec((BLOCK, 1), prev),
                  pl.BlockSpec((1, BLOCK), lambda i: (0, i)), rows, rows],
        out_specs=[pl.BlockSpec((BLOCK, HM), cur), pl.BlockSpec((N_HEADS, 1, BLOCK), lambda i: (0, 0, i))],
        out_shape=[jax.ShapeDtypeStruct((t, HM), BF16), jax.ShapeDtypeStruct((N_HEADS, 1, t), F32)],
        compiler_params=_params(("parallel",)),
    )(qa, ka, ka, va, va, pos_col, pos_col, pos_row, slope_rows, sink_rows)


def _swa_bwd(qa, ka, va, out_a, d_oa, lse, pos_col, pos_row, sinks):
    t = qa.shape[0]
    nb = t // BLOCK
    gw = GROUP_A * BLOCK
    slope_rows, sink_rows = _swa_rows(sinks)

    def body(q_ref, qn_ref, do_ref, don_ref, l_ref, ln_ref, o_ref, on_ref, kp_ref, kc_ref, vp_ref, vc_ref,
             pkp_ref, pkc_ref, pq_ref, pqn_ref, slope_ref, sink_ref, dq_ref, dk_ref, dv_ref, dsink_ref):
        j = pl.program_id(0)
        pkc, pkp = pkc_ref[...], pkp_ref[...]
        dist_cc = _tile_group(jnp.abs(pkc - pq_ref[...]).astype(F32))
        dist_cp = _tile_group(jnp.abs(pkp - pq_ref[...]).astype(F32))
        dist_nc = _tile_group(jnp.abs(pkc - pqn_ref[...]).astype(F32))
        mask_cc, older = _swa_masks()
        mask_cp = jnp.logical_and(older, j > 0)
        mask_nc = jnp.logical_and(older, j < nb - 1)

        @pl.when(j == 0)
        def _():
            dsink_ref[...] = jnp.zeros_like(dsink_ref)

        def tile(k, v, x, dox, lrow, drow, dist, mask, slope):
            s = jnp.where(mask, _dot(k, x) * SCALE_A - slope * dist, NEG)
            p = jnp.exp(s - lrow)
            ds = p * (_dot(v, dox) - drow)
            return p.astype(BF16), ds.astype(BF16)

        for g in range(N_KV_A):
            gs = slice(g * SLAB, (g + 1) * SLAB)
            kc, kp, vc, vp = kc_ref[:, gs], kp_ref[:, gs], vc_ref[:, gs], vp_ref[:, gs]
            slope, sink = slope_ref[g], sink_ref[g]
            x, xn = _heads_beside(q_ref, g), _heads_beside(qn_ref, g)
            dox, doxn = _heads_beside(do_ref, g), _heads_beside(don_ref, g)
            lrow, lrown = _rows_beside(l_ref, g), _rows_beside(ln_ref, g)
            drow = jnp.sum(dox.astype(F32) * _heads_beside(o_ref, g).astype(F32), axis=0, keepdims=True)
            drown = jnp.sum(doxn.astype(F32) * _heads_beside(on_ref, g).astype(F32), axis=0, keepdims=True)
            p_cc, ds_cc = tile(kc, vc, x, dox, lrow, drow, dist_cc, mask_cc, slope)
            _, ds_cp = tile(kp, vp, x, dox, lrow, drow, dist_cp, mask_cp, slope)
            p_nc, ds_nc = tile(kc, vc, xn, doxn, lrown, drown, dist_nc, mask_nc, slope)
            dqt = (_dot_tn(kc, ds_cc) + _dot_tn(kp, ds_cp)) * SCALE_A
            for hh in range(GROUP_A):
                hd = g * GROUP_A + hh
                dq_ref[:, hd * SLAB:(hd + 1) * SLAB] = dqt[:, hh * BLOCK:(hh + 1) * BLOCK].T.astype(BF16)
            dk_ref[:, gs] = ((_dot_nt(ds_cc, x) + _dot_nt(ds_nc, xn)) * SCALE_A).astype(BF16)
            dv_ref[:, gs] = (_dot_nt(p_cc, dox) + _dot_nt(p_nc, doxn)).astype(BF16)
            dsink_ref[g] -= jnp.exp(sink - lrow) * drow

    cur = lambda j: (j, 0)
    prev = lambda j: (jnp.maximum(j - 1, 0), 0)
    nxt = lambda j: (jnp.minimum(j + 1, nb - 1), 0)
    cur3 = lambda j: (0, 0, j)
    nxt3 = lambda j: (0, 0, jnp.minimum(j + 1, nb - 1))
    kvw = N_KV_A * SLAB
    rows = pl.BlockSpec((N_KV_A, 1, gw), lambda j: (0, 0, 0))
    stat = lambda im: pl.BlockSpec((N_HEADS, 1, BLOCK), im)
    return pl.pallas_call(
        body, name="swa_bwd", grid=(nb,),
        in_specs=[pl.BlockSpec((BLOCK, HM), cur), pl.BlockSpec((BLOCK, HM), nxt),
                  pl.BlockSpec((BLOCK, HM), cur), pl.BlockSpec((BLOCK, HM), nxt),
                  stat(cur3), stat(nxt3), pl.BlockSpec((BLOCK, HM), cur), pl.BlockSpec((BLOCK, HM), nxt),
                  pl.BlockSpec((BLOCK, kvw), prev), pl.BlockSpec((BLOCK, kvw), cur),
                  pl.BlockSpec((BLOCK, kvw), prev), pl.BlockSpec((BLOCK, kvw), cur),
                  pl.BlockSpec((BLOCK, 1), prev), pl.BlockSpec((BLOCK, 1), cur),
                  pl.BlockSpec((1, BLOCK), lambda j: (0, j)),
                  pl.BlockSpec((1, BLOCK), lambda j: (0, jnp.minimum(j + 1, nb - 1))), rows, rows],
        out_specs=[pl.BlockSpec((BLOCK, HM), cur), pl.BlockSpec((BLOCK, kvw), cur),
                   pl.BlockSpec((BLOCK, kvw), cur), rows],
        out_shape=[jax.ShapeDtypeStruct((t, HM), BF16), jax.ShapeDtypeStruct((t, kvw), BF16),
                   jax.ShapeDtypeStruct((t, kvw), BF16), jax.ShapeDtypeStruct((N_KV_A, 1, gw), F32)],
        compiler_params=_params(("arbitrary",)),
    )(qa, qa, d_oa, d_oa, lse, lse, out_a, out_a, ka, ka, va, va,
      pos_col, pos_col, pos_row, pos_row, slope_rows, sink_rows)


def _mesh_pos():
    return lax.axis_index("x"), lax.axis_index("y"), lax.axis_index("c")


def _flip(v, bit):
    return 1 - v if bit else v


def _direct_copies(srcs, dsts, send_sems, recv_sems, local_sems, gather, sem_base=0):
    x, y, c = _mesh_pos()
    me = 4 * x + 2 * y + c
    local, remote = [], []
    for a, (src, dst) in enumerate(zip(srcs, dsts)):
        local.append(pltpu.make_async_copy(src if gather else src.at[me], dst.at[me], local_sems.at[sem_base + a]))
        for r in range(1, N_DEV):
            px, py, pc = _flip(x, r & 4), _flip(y, r & 2), _flip(c, r & 1)
            sem = (N_DEV - 1) * (sem_base + a) + r - 1
            remote.append(pltpu.make_async_remote_copy(
                src_ref=src if gather else src.at[4 * px + 2 * py + pc], dst_ref=dst.at[me],
                send_sem=send_sems.at[sem], recv_sem=recv_sems.at[sem],
                device_id=(px, py, pc), device_id_type=pl.DeviceIdType.MESH))
    return local, remote


def _start_copies(local, remote):
    for cp in local + remote:
        cp.start()


def _wait_copies(local, remote):
    for cp in remote:
        cp.wait_recv()
    for cp in remote:
        cp.wait_send()
    for cp in local:
        cp.wait()


def _exchange_scratch(n):
    return [pltpu.SemaphoreType.DMA((n * (N_DEV - 1),)), pltpu.SemaphoreType.DMA((n * (N_DEV - 1),)),
            pltpu.SemaphoreType.DMA((n,))]


ANY_SPEC = pl.BlockSpec(memory_space=pl.ANY)


def _mla_fwd(qt, kb, vt, late):
    t = kb.shape[0]
    tk = _attn_tile(t)
    ratio = 2 if t >= 2 * tk else 1
    tq = ratio * tk
    nq = t // tq
    hps = MLA_FWD_HEADS_PER_STEP
    w = hps * SLAB
    pairs = [(i, j) for i in range(nq) for j in range(ratio * (i + 1))]
    i_tab = jnp.asarray(np.array([p[0] for p in pairs], np.int32))
    j_tab = jnp.asarray(np.array([p[1] for p in pairs], np.int32))

    n_late = len(late)

    def body(it_ref, jt_ref, qt_ref, k_ref, vt_ref, *rest):
        late_refs, (o_ref, ot_ref, l_ref) = rest[:n_late], rest[n_late:n_late + 3]
        gathered_refs = rest[n_late + 3:2 * n_late + 3]
        m_s, acc_s, send_sems, recv_sems, local_sems = rest[2 * n_late + 3:]
        n = pl.program_id(1)
        i, j = it_ref[n], jt_ref[n]
        first_step = jnp.logical_and(pl.program_id(0) == 0, n == 0)
        last_step = jnp.logical_and(pl.program_id(0) == N_HEADS // hps - 1, n == len(pairs) - 1)

        @pl.when(first_step)
        def _():
            _start_copies(*_direct_copies(late_refs, gathered_refs, send_sems, recv_sems, local_sems, True))

        @pl.when(j == 0)
        def _():
            m_s[...] = jnp.full_like(m_s, NEG)
            acc_s[...] = jnp.zeros_like(acc_s)

        def update(masked, q0):
            qc = slice(q0, tq)

            def scores(hh):
                sl = slice(hh * SLAB, (hh + 1) * SLAB)
                return _dot(k_ref[:, sl], qt_ref[sl, qc])

            def softmax(hh, s):
                if masked:
                    s = jnp.where(lax.broadcasted_iota(jnp.int32, s.shape, 0)
                                  <= lax.broadcasted_iota(jnp.int32, s.shape, 1), s, NEG)
                m_old = m_s[hh][:, qc]
                m_new = jnp.maximum(m_old, jnp.max(s, axis=0, keepdims=True))
                m_s[hh, :, qc] = m_new
                return jnp.exp2(s - m_new).astype(BF16), jnp.exp2(m_old - m_new)

            def accumulate(hh, p, alpha):
                sl = slice(hh * SLAB, hh * SLAB + V_DIM_B + ONES_ROWS)
                acc_s[sl, qc] = alpha * acc_s[sl, qc] + _dot(vt_ref[sl, :], p)

            s_next, pending = scores(0), None
            for hh in range(hps):
                s = s_next
                if hh + 1 < hps:
                    s_next = scores(hh + 1)
                p, alpha = softmax(hh, s)
                if pending is not None:
                    accumulate(*pending)
                pending = (hh, p, alpha)
            accumulate(*pending)

        @pl.when(j < ratio * i)
        def _():
            update(False, 0)

        for part in range(ratio):
            @pl.when(j == ratio * i + part)
            def _():
                update(True, part * tk)

        @pl.when(j == ratio * i + ratio - 1)
        def _():
            for hh in range(hps):
                sl = slice(hh * SLAB, (hh + 1) * SLAB)
                den = acc_s[hh * SLAB + V_DIM_B:hh * SLAB + V_DIM_B + 1, :]
                values = lax.broadcasted_iota(jnp.int32, (SLAB, tq), 0) < V_DIM_B
                ot = jnp.where(values, acc_s[sl, :] / den, 0.0)
                ot_ref[sl, :] = ot.astype(BF16)
                o_ref[:, sl] = ot.T.astype(BF16)
                l_ref[hh] = m_s[hh] + jnp.log2(den)

        @pl.when(last_step)
        def _():
            _wait_copies(*_direct_copies(late_refs, gathered_refs, send_sems, recv_sems, local_sems, True))

    grid_spec = pltpu.PrefetchScalarGridSpec(
        num_scalar_prefetch=2, grid=(N_HEADS // hps, len(pairs)),
        in_specs=[pl.BlockSpec((w, tq), lambda h, n, it, jt: (h, it[n])),
                  pl.BlockSpec((tk, w), lambda h, n, it, jt: (jt[n], h)),
                  pl.BlockSpec((w, tk), lambda h, n, it, jt: (h, jt[n]))] + [ANY_SPEC] * n_late,
        out_specs=[pl.BlockSpec((tq, w), lambda h, n, it, jt: (it[n], h)),
                   pl.BlockSpec((w, tq), lambda h, n, it, jt: (h, it[n])),
                   pl.BlockSpec((hps, 1, tq), lambda h, n, it, jt: (h, 0, it[n]))] + [ANY_SPEC] * n_late,
        scratch_shapes=[pltpu.VMEM((hps, 1, tq), F32), pltpu.VMEM((w, tq), F32)] + _exchange_scratch(n_late))
    outs = pl.pallas_call(
        body, name="mla_fwd", grid_spec=grid_spec,
        out_shape=[jax.ShapeDtypeStruct((t, HM), BF16), jax.ShapeDtypeStruct((HM, t), BF16),
                   jax.ShapeDtypeStruct((N_HEADS, 1, t), F32)]
        + [jax.ShapeDtypeStruct((N_DEV,) + a.shape, a.dtype) for a in late],
        compiler_params=_params(("arbitrary", "arbitrary")),
    )(i_tab, j_tab, qt, kb, vt, *late)
    return outs[0], outs[1], outs[2], list(outs[3:])


def _mla_bwd(qt, kb, kt, vb, d_ob_t, lse, delta, grad_slices):
    t = kb.shape[0]
    tk = _attn_tile(t)
    ratio = 2 if t >= 2 * tk else 1
    tq = ratio * tk
    nk, nq = t // tk, t // tq
    hps = MLA_HEADS_PER_STEP
    w = hps * SLAB
    pairs = [(j, i) for j in range(nk) for i in range(j // ratio, nq)]
    j_tab = jnp.asarray(np.array([p[0] for p in pairs], np.int32))
    i_tab = jnp.asarray(np.array([p[1] for p in pairs], np.int32))

    n_ex = len(grad_slices)

    def body(jt_ref, it_ref, qt_ref, dot_ref, l_ref, dl_ref, k_ref, kt_ref, v_ref, *rest):
        slice_refs, (dqt_ref, dkt_ref, dvt_ref) = rest[:n_ex], rest[n_ex:n_ex + 3]
        part_refs = rest[n_ex + 3:2 * n_ex + 3]
        dk_s, dv_s, send_sems, recv_sems, local_sems = rest[2 * n_ex + 3:]
        n = pl.program_id(1)
        j, i = jt_ref[n], it_ref[n]
        first_step = jnp.logical_and(pl.program_id(0) == 0, n == 0)
        last_step = jnp.logical_and(pl.program_id(0) == N_HEADS // hps - 1, n == len(pairs) - 1)

        @pl.when(first_step)
        def _():
            _start_copies(*_direct_copies(slice_refs, part_refs, send_sems, recv_sems, local_sems, False))

        @pl.when(n == 0)
        def _():
            dqt_ref[...] = jnp.zeros_like(dqt_ref)

        def update(diagonal, q0):
            qc = slice(q0, tq)
            cols = pl.ds(pl.multiple_of(i * tq + q0, tk), tq - q0)

            def products(hh):
                sl = slice(hh * SLAB, (hh + 1) * SLAB)
                return _dot(k_ref[:, sl], qt_ref[sl, qc]), _dot(v_ref[:, sl], dot_ref[sl, qc])

            def softmax_bwd(hh, s, dp):
                if diagonal:
                    s = jnp.where(lax.broadcasted_iota(jnp.int32, s.shape, 0)
                                  <= lax.broadcasted_iota(jnp.int32, s.shape, 1), s, NEG)
                p = jnp.exp2(s - l_ref[hh][:, qc])
                return p.astype(BF16), (p * (dp - dl_ref[hh][:, qc])).astype(BF16)

            def gradients(hh, p, ds):
                base = hh * SLAB
                vrows = slice(base, base + V_DIM_B)
                qrows = slice(base, base + QK_NOPE + QK_ROPE)
                dv = _dot_nt(dot_ref[vrows, qc], p)
                dk = _dot_nt(qt_ref[qrows, qc], ds)
                if diagonal:
                    dv_s[base:base + SLAB, :] = jnp.concatenate([dv, jnp.zeros((SLAB - V_DIM_B, tk), F32)], axis=0)
                    dk_s[base:base + SLAB, :] = jnp.concatenate(
                        [dk, jnp.zeros((SLAB - QK_NOPE - QK_ROPE, tk), F32)], axis=0)
                else:
                    dv_s[vrows, :] += dv
                    dk_s[qrows, :] += dk
                dqt_ref[qrows, cols] += _dot(kt_ref[qrows, :], ds)

            for hh in range(hps):
                gradients(hh, *softmax_bwd(hh, *products(hh)))

        first_tile = lax.div(j, ratio)
        for part in range(ratio):
            @pl.when(jnp.logical_and(i == first_tile, lax.rem(j, ratio) == part))
            def _():
                update(True, part * tk)

        @pl.when(i > first_tile)
        def _():
            update(False, 0)

        @pl.when(i == nq - 1)
        def _():
            dkt_ref[...] = (dk_s[...] * (1.0 / LOG2E)).astype(BF16)
            dvt_ref[...] = dv_s[...].astype(BF16)

        @pl.when(last_step)
        def _():
            _wait_copies(*_direct_copies(slice_refs, part_refs, send_sems, recv_sems, local_sems, False))

    grid_spec = pltpu.PrefetchScalarGridSpec(
        num_scalar_prefetch=2, grid=(N_HEADS // hps, len(pairs)),
        in_specs=[pl.BlockSpec((w, tq), lambda h, n, jt, it: (h, it[n])),
                  pl.BlockSpec((w, tq), lambda h, n, jt, it: (h, it[n])),
                  pl.BlockSpec((hps, 1, tq), lambda h, n, jt, it: (h, 0, it[n])),
                  pl.BlockSpec((hps, 1, tq), lambda h, n, jt, it: (h, 0, it[n])),
                  pl.BlockSpec((tk, w), lambda h, n, jt, it: (jt[n], h)),
                  pl.BlockSpec((w, tk), lambda h, n, jt, it: (h, jt[n])),
                  pl.BlockSpec((tk, w), lambda h, n, jt, it: (jt[n], h))] + [ANY_SPEC] * n_ex,
        out_specs=[pl.BlockSpec((w, t), lambda h, n, jt, it: (h, 0)),
                   pl.BlockSpec((w, tk), lambda h, n, jt, it: (h, jt[n])),
                   pl.BlockSpec((w, tk), lambda h, n, jt, it: (h, jt[n]))] + [ANY_SPEC] * n_ex,
        scratch_shapes=[pltpu.VMEM((w, tk), F32), pltpu.VMEM((w, tk), F32)] + _exchange_scratch(n_ex))
    outs = pl.pallas_call(
        body, name="mla_bwd", grid_spec=grid_spec,
        out_shape=[jax.ShapeDtypeStruct((HM, t), F32), jax.ShapeDtypeStruct((HM, t), BF16),
                   jax.ShapeDtypeStruct((HM, t), BF16)]
        + [jax.ShapeDtypeStruct(a.shape, a.dtype) for a in grad_slices],
        compiler_params=_params(("arbitrary", "arbitrary")),
    )(j_tab, i_tab, qt, d_ob_t, lse, delta, kb, kt, vb, *grad_slices)
    return outs[0], outs[1], outs[2], list(outs[3:])


def _merge_fwd(out_a, out_b, gates, x, w_oa, w_ob, w_out, g2, g3):
    t = x.shape[0]
    tm = _token_tile(t)

    def body(oa_ref, ob_ref, gates_ref, x_ref, woa_ref, wob_ref, wout_ref, g2_ref, g3_ref,
             oap_ref, obp_ref, merged_ref, y_ref, x1_ref, h2_ref):
        oa_p = _dot(oa_ref[...], woa_ref[...])
        ob_p = _dot(ob_ref[...], wob_ref[...])
        oap_ref[...] = oa_p.astype(BF16)
        obp_ref[...] = ob_p.astype(BF16)
        sa = _sigmoid(gates_ref[:, 0:D_MODEL])
        sb = _sigmoid(gates_ref[:, D_MODEL:2 * D_MODEL])
        merged = (sa * oa_p + sb * ob_p).astype(BF16)
        merged_ref[...] = merged
        y = _dot(merged, wout_ref[...])
        y_ref[...] = y
        x1 = x_ref[...] + y * _rms_r(y) * g2_ref[...]
        x1_ref[...] = x1
        h2_ref[...] = (x1 * _rms_r(x1) * g3_ref[...]).astype(BF16)

    def sds(dt):
        return jax.ShapeDtypeStruct((t, D_MODEL), dt)

    row = _row_spec(tm, D_MODEL)
    return pl.pallas_call(
        body, name="merge_fwd", grid=(t // tm,),
        in_specs=[_row_spec(tm, HM), _row_spec(tm, HM), _row_spec(tm, 2 * D_MODEL), row,
                  _full_spec((HM, D_MODEL)), _full_spec((HM, D_MODEL)), _full_spec((D_MODEL, D_MODEL)),
                  _full_spec((1, D_MODEL)), _full_spec((1, D_MODEL))],
        out_specs=[row] * 6,
        out_shape=[sds(BF16), sds(BF16), sds(BF16), sds(F32), sds(F32), sds(BF16)],
        compiler_params=_params(("parallel",)),
    )(out_a, out_b, gates, x, w_oa, w_ob, w_out, g2, g3)


def _merge_bwd(dx1, y, gates, oa_p, ob_p, out_b_t, w_oa, w_ob, w_out, g2):
    t = dx1.shape[0]
    tm = _token_tile(t)

    def body(dx1_ref, y_ref, gates_ref, oap_ref, obp_ref, obt_ref, woa_ref, wob_ref, wout_ref, g2_ref,
             dy_ref, doap_ref, dobp_ref, dgates_ref, doa_ref, dobt_ref, dlb_ref, dg2_ref):
        dx1v = dx1_ref[...]
        yv = y_ref[...]
        r2 = _rms_r(yv)
        _acc_rows(dg2_ref, dx1v * yv * r2)
        dy = _rms_bwd(yv, r2, g2_ref[...], dx1v).astype(BF16)
        dy_ref[...] = dy
        dm = _dot_nt(dy, wout_ref[...])
        sa = _sigmoid(gates_ref[:, 0:D_MODEL])
        sb = _sigmoid(gates_ref[:, D_MODEL:2 * D_MODEL])
        d_oap = (dm * sa).astype(BF16)
        d_obp = (dm * sb).astype(BF16)
        doap_ref[...] = d_oap
        dobp_ref[...] = d_obp
        dgates_ref[:, 0:D_MODEL] = (dm * oap_ref[...].astype(F32) * sa * (1.0 - sa)).astype(BF16)
        dgates_ref[:, D_MODEL:2 * D_MODEL] = (dm * obp_ref[...].astype(F32) * sb * (1.0 - sb)).astype(BF16)
        doa_ref[...] = _dot_nt(d_oap, woa_ref[...]).astype(BF16)
        d_ob_t = _dot_nt(wob_ref[...], d_obp)
        dobt_ref[...] = d_ob_t.astype(BF16)
        for hd in range(N_HEADS):
            sl = slice(hd * SLAB, (hd + 1) * SLAB)
            dlb_ref[hd] = jnp.sum(d_ob_t[sl, :] * obt_ref[sl, :].astype(F32), axis=0, keepdims=True)

    def sds(n, dt):
        return jax.ShapeDtypeStruct((t, n), dt)

    row = _row_spec(tm, D_MODEL)
    head3 = pl.BlockSpec((N_HEADS, 1, tm), lambda i: (0, 0, i))
    return pl.pallas_call(
        body, name="merge_bwd", grid=(t // tm,),
        in_specs=[row, row, _row_spec(tm, 2 * D_MODEL), row, row, _col_spec(HM, tm),
                  _full_spec((HM, D_MODEL)), _full_spec((HM, D_MODEL)), _full_spec((D_MODEL, D_MODEL)),
                  _full_spec((1, D_MODEL))],
        out_specs=[row, row, row, _row_spec(tm, 2 * D_MODEL), _row_spec(tm, HM), _col_spec(HM, tm),
                   head3, _full_spec((1, D_MODEL))],
        out_shape=[sds(D_MODEL, BF16), sds(D_MODEL, BF16), sds(D_MODEL, BF16), sds(2 * D_MODEL, BF16),
                   sds(HM, BF16), jax.ShapeDtypeStruct((HM, t), BF16),
                   jax.ShapeDtypeStruct((N_HEADS, 1, t), F32), jax.ShapeDtypeStruct((1, D_MODEL), F32)],
        compiler_params=_params(("arbitrary",)),
    )(dx1, y, gates, oa_p, ob_p, out_b_t, w_oa, w_ob, w_out, g2)


def _mlp_fwd_bwd(x1, h2, target, w_up, w_down, g3, g4):
    t = x1.shape[0]
    tm = _token_tile(t)
    fs = D_FF // N_DEV

    def body(x1_ref, h2_ref, tgt_ref, wup_ref, wdown_ref, g3_ref, g4_ref,
             a_ref, du_ref, dy2_ref, dx1_ref, loss_ref, dg3_ref, dg4_ref):
        x1v = x1_ref[...]
        h2v = h2_ref[...]
        u = jnp.concatenate([_dot(h2v, wup_ref[s]) for s in range(N_DEV)], axis=1)
        ru = jnp.maximum(u, 0.0)
        a = (ru * ru).astype(BF16)
        a_ref[...] = a
        y2 = _dot(a, wdown_ref[...])
        r4 = _rms_r(y2)
        diff = x1v + y2 * r4 * g4_ref[...] - tgt_ref[...]
        _acc_rows(loss_ref, jnp.sum(diff * diff, axis=-1, keepdims=True) * (0.5 / D_MODEL)
                  * jnp.ones((1, SLAB), F32))
        dx2 = diff * (1.0 / D_MODEL)
        _acc_rows(dg4_ref, dx2 * y2 * r4)
        dy2 = _rms_bwd(y2, r4, g4_ref[...], dx2).astype(BF16)
        dy2_ref[...] = dy2
        du = (_dot_nt(dy2, wdown_ref[...]) * (2.0 * ru)).astype(BF16)
        du_ref[...] = du
        dh2 = _dot_nt(du[:, 0:fs], wup_ref[0])
        for s in range(1, N_DEV):
            dh2 += _dot_nt(du[:, s * fs:(s + 1) * fs], wup_ref[s])
        r3 = _rms_r(x1v)
        _acc_rows(dg3_ref, dh2 * x1v * r3)
        dx1_ref[...] = dx2 + _rms_bwd(x1v, r3, g3_ref[...], dh2)

    row = _row_spec(tm, D_MODEL)
    frow = _row_spec(tm, D_FF)
    vec = _full_spec((1, D_MODEL))
    return pl.pallas_call(
        body, name="mlp_fwd_bwd", grid=(t // tm,),
        in_specs=[row, row, row, _full_spec((N_DEV, D_MODEL, fs)), _full_spec((D_FF, D_MODEL)), vec, vec],
        out_specs=[frow, frow, row, row, _full_spec((1, SLAB)), vec, vec],
        out_shape=[jax.ShapeDtypeStruct((t, D_FF), BF16), jax.ShapeDtypeStruct((t, D_FF), BF16),
                   jax.ShapeDtypeStruct((t, D_MODEL), BF16), jax.ShapeDtypeStruct((t, D_MODEL), F32),
                   jax.ShapeDtypeStruct((1, SLAB), F32), jax.ShapeDtypeStruct((1, D_MODEL), F32),
                   jax.ShapeDtypeStruct((1, D_MODEL), F32)],
        compiler_params=_params(("arbitrary",)),
    )(x1, h2, target, w_up, w_down, g3, g4)


def _inproj_bwd(dgates, dqa, dka, dva, dqb_t, dkb_t, dvb_t, cq, ckv, x, dx1, rope_ct, rope_s1t, rope_s2t,
                g1, g_q, g_kv, w_in, w_qb, w_kvb):
    t = x.shape[0]
    tm = _token_tile(t)

    def body(dgates_ref, dqa_ref, dka_ref, dva_ref, dqt_ref, dkt_ref, dvt_ref, cq_ref, ckv_ref, x_ref, dx1_ref,
             ct_ref, s1t_ref, s2t_ref, g1_ref, gq_ref, gkv_ref, win_ref, wqb_ref, wkvb_ref,
             dproj_ref, dqbrt_ref, dkvbt_ref, dx_ref, dg1_ref, dgq_ref, dgkv_ref):
        ct, s1t, s2t = ct_ref[...], s1t_ref[...], s2t_ref[...]
        dk_sum_t = jnp.zeros((SLAB, tm), F32)
        for hd in range(N_HEADS):
            sl = slice(hd * SLAB, (hd + 1) * SLAB)
            dqbrt_ref[sl, :] = _rope_t_bwd(dqt_ref[sl, :] * SCALE_B, ct, s1t, s2t).astype(BF16)
            dk_sum_t += dkt_ref[sl, :].astype(F32)
        dkvbt_ref[0:HM, :] = dkt_ref[...]
        dkvbt_ref[HM:2 * HM, :] = dvt_ref[...]
        dkr = _rope_t_bwd(dk_sum_t, ct, s1t, s2t).T
        dcqn = _dot(wqb_ref[...], dqbrt_ref[...]).T
        cq = cq_ref[...]
        rq = _rms_r(cq)
        _acc_rows(dgq_ref, dcqn * cq * rq)
        dcq = _rms_bwd(cq, rq, gq_ref[...], dcqn)
        dckvn = _dot(wkvb_ref[...], dkvbt_ref[...]).T
        ckv = ckv_ref[...]
        rkv = _rms_r(ckv)
        _acc_rows(dgkv_ref, dckvn * ckv * rkv)
        dckv = _rms_bwd(ckv, rkv, gkv_ref[...], dckvn)
        dproj_ref[:, C_GATES:C_QA] = dgates_ref[...]
        dproj_ref[:, C_QA:C_KA] = dqa_ref[...]
        dproj_ref[:, C_KA:C_VA] = dka_ref[...]
        dproj_ref[:, C_VA:C_CQ] = dva_ref[...]
        dproj_ref[:, C_CQ:C_CKV] = dcq.astype(BF16)
        dproj_ref[:, C_CKV:C_KR] = dckv.astype(BF16)
        dproj_ref[:, C_KR:D_IN_PAD] = dkr.astype(BF16)
        dh = _dot_nt(dproj_ref[...], win_ref[...])
        xv = x_ref[...]
        r1 = _rms_r(xv)
        _acc_rows(dg1_ref, dh * xv * r1)
        dx_ref[...] = dx1_ref[...] + _rms_bwd(xv, r1, g1_ref[...], dh)

    kvw = N_KV_A * SLAB
    row = _row_spec(tm, D_MODEL)
    hm = _row_spec(tm, HM)
    hmt = _col_spec(HM, tm)
    tab = _col_spec(SLAB, tm)
    return pl.pallas_call(
        body, name="inproj_bwd", grid=(t // tm,),
        in_specs=[_row_spec(tm, 2 * D_MODEL), hm, _row_spec(tm, kvw), _row_spec(tm, kvw), hmt, hmt, hmt,
                  _row_spec(tm, Q_LORA), _row_spec(tm, KV_LORA), row, row, tab, tab, tab,
                  _full_spec((1, D_MODEL)), _full_spec((1, Q_LORA)), _full_spec((1, KV_LORA)),
                  _full_spec((D_MODEL, D_IN_PAD)), _full_spec((Q_LORA, HM)), _full_spec((KV_LORA, 2 * HM))],
        out_specs=[_row_spec(tm, D_IN_PAD), hmt, _col_spec(2 * HM, tm), row,
                   _full_spec((1, D_MODEL)), _full_spec((1, Q_LORA)), _full_spec((1, KV_LORA))],
        out_shape=[jax.ShapeDtypeStruct((t, D_IN_PAD), BF16), jax.ShapeDtypeStruct((HM, t), BF16),
                   jax.ShapeDtypeStruct((2 * HM, t), BF16), jax.ShapeDtypeStruct((t, D_MODEL), F32),
                   jax.ShapeDtypeStruct((1, D_MODEL), F32), jax.ShapeDtypeStruct((1, Q_LORA), F32),
                   jax.ShapeDtypeStruct((1, KV_LORA), F32)],
        compiler_params=_params(("arbitrary",)),
    )(dgates, dqa, dka, dva, dqb_t, dkb_t, dvb_t, cq, ckv, x, dx1, rope_ct, rope_s1t, rope_s2t,
      g1, g_q, g_kv, w_in, w_qb, w_kvb)


def _matmul_tn(a, b, name, out_dtype=F32, n_shards=1):
    t, k = a.shape
    n = b.shape[1]
    bt = min(t, 512)
    bn = min(n, 2048)
    bk = min(k, 2048 * 1024 // bn)
    ns = n // n_shards
    per_block = bn // ns
    steps = t // bt

    def body(a_ref, b_ref, o_ref, acc):
        s = pl.program_id(2)

        @pl.when(s == 0)
        def _():
            acc[...] = jnp.zeros_like(acc)

        acc[...] += _dot_tn(a_ref[...], b_ref[...])

        @pl.when(s == steps - 1)
        def _():
            if n_shards > 1:
                for p in range(per_block):
                    o_ref[p] = acc[:, p * ns:(p + 1) * ns].astype(out_dtype)
            else:
                o_ref[...] = acc[...].astype(out_dtype)

    if n_shards > 1:
        out_spec = pl.BlockSpec((per_block, bk, ns), lambda i, j, s: (j, i, 0))
        out_shape = jax.ShapeDtypeStruct((n_shards, k, ns), out_dtype)
    else:
        out_spec = pl.BlockSpec((bk, bn), lambda i, j, s: (i, j))
        out_shape = jax.ShapeDtypeStruct((k, n), out_dtype)
    return pl.pallas_call(
        body, name=name, grid=(k // bk, n // bn, steps),
        in_specs=[pl.BlockSpec((bt, bk), lambda i, j, s: (s, i)), pl.BlockSpec((bt, bn), lambda i, j, s: (s, j))],
        out_specs=out_spec, out_shape=out_shape, scratch_shapes=[pltpu.VMEM((bk, bn), F32)],
        compiler_params=_params(("parallel", "parallel", "arbitrary")),
    )(a, b)


def _matmul_nn(a_t, b, name):
    m, t = a_t.shape
    n = b.shape[1]
    bt = min(t, 512)
    steps = t // bt

    def body(a_ref, b_ref, o_ref, acc):
        s = pl.program_id(0)

        @pl.when(s == 0)
        def _():
            acc[...] = jnp.zeros_like(acc)

        acc[...] += _dot(a_ref[...], b_ref[...])

        @pl.when(s == steps - 1)
        def _():
            o_ref[...] = acc[...]

    return pl.pallas_call(
        body, name=name, grid=(steps,),
        in_specs=[pl.BlockSpec((m, bt), lambda s: (0, s)), pl.BlockSpec((bt, n), lambda s: (s, 0))],
        out_specs=pl.BlockSpec((m, n), lambda s: (0, 0)), out_shape=jax.ShapeDtypeStruct((m, n), F32),
        scratch_shapes=[pltpu.VMEM((m, n), F32)],
        compiler_params=_params(("arbitrary",)),
    )(a_t, b)


def _two_level_gather(srcs, dsts, send_sems, recv_sems, local_sems):
    n = len(srcs)
    x, y, c = _mesh_pos()
    me, sibling = (x, y, c), (x, y, 1 - c)
    chips = [(1 - x, y), (x, 1 - y), (1 - x, 1 - y)]

    def slot(a, px, py, pc):
        return dsts[a].at[4 * px + 2 * py + pc]

    def copy(a, k, block, to, src=None):
        return pltpu.make_async_remote_copy(
            src_ref=slot(a, *block) if src is None else src, dst_ref=slot(a, *block),
            send_sem=send_sems.at[(N_DEV - 1) * a + k], recv_sem=recv_sems.at[(N_DEV - 1) * a + k],
            device_id=to, device_id_type=pl.DeviceIdType.MESH)

    def own_copies():
        mine = [pltpu.make_async_copy(srcs[a], slot(a, *me), local_sems.at[a]) for a in range(n)]
        first = []
        for a in range(n):
            first.append(copy(a, 0, me, sibling, src=srcs[a]))
            first += [copy(a, 1 + j, me, (*chip, c), src=srcs[a]) for j, chip in enumerate(chips)]
        return mine, first

    def start():
        mine, first = own_copies()
        for cp in mine + first:
            cp.start()

    def finish():
        mine, first = own_copies()
        passed = []
        for j, chip in enumerate(chips):
            for a in range(n):
                copy(a, 1 + j, (*chip, c), me).wait_recv()
                passed.append(copy(a, 4 + j, (*chip, c), sibling))
                passed[-1].start()
        for a in range(n):
            copy(a, 0, sibling, me).wait_recv()
        for j, chip in enumerate(chips):
            for a in range(n):
                copy(a, 4 + j, (*chip, 1 - c), me).wait_recv()
        for cp in first + passed:
            cp.wait_send()
        for cp in mine:
            cp.wait()

    return start, finish


def _exchange_grads(slices, small):
    n = len(slices)

    def body(*refs):
        srcs, s_ref = refs[:n], refs[n]
        dsts, s_dst = refs[n + 1:2 * n + 1], refs[2 * n + 1]
        sems = refs[2 * n + 2:]
        parts = _direct_copies(srcs, dsts, *sems, False)
        smalls = _direct_copies([s_ref], [s_dst], *sems, True, sem_base=n)
        _start_copies(*parts)
        _start_copies(*smalls)
        _wait_copies(*parts)
        _wait_copies(*smalls)

    outs = pl.pallas_call(
        body, name="exchange_grads",
        out_shape=[jax.ShapeDtypeStruct(a.shape, a.dtype) for a in slices]
        + [jax.ShapeDtypeStruct((N_DEV,) + small.shape, small.dtype)],
        in_specs=[ANY_SPEC] * (n + 1), out_specs=[ANY_SPEC] * (n + 1), scratch_shapes=_exchange_scratch(n + 1),
    )(*slices, small)
    return list(outs[:n]), outs[n]


def _adamw(parts, w, m, v, name):
    _, k, n = parts.shape
    bk = min(k, ADAM_ROWS)
    c1 = 1.0 - ADAM_B1 ** ADAM_STEP
    c2 = 1.0 - ADAM_B2 ** ADAM_STEP

    def body(p_ref, w_ref, m_ref, v_ref, g_ref, d_ref, mo_ref, vo_ref):
        g = p_ref[0].astype(F32)
        for s in range(1, N_DEV):
            g = g + p_ref[s].astype(F32)
        g_ref[0] = g
        m_new = ADAM_B1 * m_ref[0] + (1.0 - ADAM_B1) * g
        v_new = ADAM_B2 * v_ref[0] + (1.0 - ADAM_B2) * (g * g)
        mo_ref[0] = m_new
        vo_ref[0] = v_new
        m_hat = m_new / c1
        v_hat = v_new / c2
        d_ref[0] = -ADAM_LR * (m_hat / (jnp.sqrt(v_hat) + ADAM_EPS) + ADAM_WD * w_ref[0])

    blk = pl.BlockSpec((1, bk, n), lambda i: (0, i, 0))
    out = jax.ShapeDtypeStruct((1, k, n), F32)
    return pl.pallas_call(
        body, name=name, grid=(k // bk,),
        in_specs=[pl.BlockSpec((N_DEV, bk, n), lambda i: (0, i, 0)), blk, blk, blk],
        out_specs=[blk] * 4, out_shape=[out] * 4,
        compiler_params=_params(("parallel",)),
    )(parts, w, m, v)


def _pad_heads_cols(w, heads, width):
    k = w.shape[0]
    w = w.reshape(k, heads, width)
    return jnp.pad(w, ((0, 0), (0, 0), (0, SLAB - width))).reshape(k, heads * SLAB)


def _unpad_heads_cols(w, heads, width):
    k = w.shape[0]
    return w.reshape(k, heads, SLAB)[:, :, :width].reshape(k, heads * width)


def _pad_heads_rows(w, heads, width):
    n = w.shape[1]
    w = w.reshape(heads, width, n)
    return jnp.pad(w, ((0, 0), (0, SLAB - width), (0, 0))).reshape(heads * SLAB, n)


def _unpad_heads_rows(w, heads, width):
    n = w.shape[1]
    return w.reshape(heads, SLAB, n)[:, :width, :].reshape(heads * width, n)


def _pad_w_in(w_in):
    o = 2 * D_MODEL
    qa = _pad_heads_cols(w_in[:, o:o + 512], N_HEADS, HEAD_A)
    ka = _pad_heads_cols(w_in[:, o + 512:o + 640], N_KV_A, HEAD_A)
    va = _pad_heads_cols(w_in[:, o + 640:o + 768], N_KV_A, HEAD_A)
    kr = jnp.pad(w_in[:, o + 1152:o + 1184], ((0, 0), (QK_NOPE, SLAB - QK_NOPE - QK_ROPE)))
    return jnp.concatenate([w_in[:, :o], qa, ka, va, w_in[:, o + 768:o + 1152], kr], axis=1)


def _unpad_w_in(w):
    qa = _unpad_heads_cols(w[:, C_QA:C_KA], N_HEADS, HEAD_A)
    ka = _unpad_heads_cols(w[:, C_KA:C_VA], N_KV_A, HEAD_A)
    va = _unpad_heads_cols(w[:, C_VA:C_CQ], N_KV_A, HEAD_A)
    kr = w[:, C_KR + QK_NOPE:C_KR + QK_NOPE + QK_ROPE]
    return jnp.concatenate([w[:, :C_QA], qa, ka, va, w[:, C_CQ:C_KR], kr], axis=1)


def _pad_w_kvb(w_kvb):
    w = w_kvb.reshape(KV_LORA, N_HEADS, QK_NOPE + V_DIM_B)
    k = jnp.pad(w[:, :, :QK_NOPE], ((0, 0), (0, 0), (0, SLAB - QK_NOPE))).reshape(KV_LORA, HM)
    v = jnp.pad(w[:, :, QK_NOPE:], ((0, 0), (0, 0), (0, SLAB - V_DIM_B))).reshape(KV_LORA, HM)
    return jnp.concatenate([k, v], axis=1)


def _unpad_w_kvb(w):
    k = w[:, :HM].reshape(KV_LORA, N_HEADS, SLAB)[:, :, :QK_NOPE]
    v = w[:, HM:].reshape(KV_LORA, N_HEADS, SLAB)[:, :, :V_DIM_B]
    return jnp.concatenate([k, v], axis=2).reshape(KV_LORA, N_HEADS * (QK_NOPE + V_DIM_B))


def _col_shards(w):
    k, n = w.shape
    return w.reshape(k, N_DEV, n // N_DEV).transpose(1, 0, 2)


def _from_col_shards(s):
    _, k, ns = s.shape
    return s.transpose(1, 0, 2).reshape(k, N_DEV * ns)


def _freq_row():
    freqs = ROPE_THETA ** (-jnp.arange(0, QK_ROPE, 2, dtype=F32) / QK_ROPE)
    return jnp.concatenate([jnp.zeros((QK_NOPE,), F32), freqs, freqs,
                            jnp.zeros((SLAB - QK_NOPE - QK_ROPE,), F32)]).reshape(1, SLAB)


SMALL_D_ROWS = ("pre_norm_mix", "post_norm_mix", "pre_norm_mlp", "post_norm_mlp")
SMALL_Q_OFF, SMALL_KV_OFF, SMALL_SINK_OFF, SMALL_LOSS_OFF = 0, 256, 384, 392


def _pack_small(vals):
    row4 = jnp.concatenate([vals["q_a_norm"].reshape(-1), vals["kv_a_norm"].reshape(-1), vals["sinks"].reshape(-1),
                            vals["loss"].reshape(-1), jnp.zeros((1024 - 393,), F32)])
    rows = [vals[n].reshape(1024) for n in SMALL_D_ROWS] + [row4]
    return jnp.concatenate([jnp.stack(rows), jnp.zeros((SMALL_ROWS - 5, 1024), F32)], axis=0)


def _unpack_small(blk):
    out = {n: blk[i].reshape(1, 1024) for i, n in enumerate(SMALL_D_ROWS)}
    out["q_a_norm"] = blk[4, SMALL_Q_OFF:SMALL_Q_OFF + 256].reshape(1, 256)
    out["kv_a_norm"] = blk[4, SMALL_KV_OFF:SMALL_KV_OFF + 128].reshape(1, 128)
    out["sinks"] = blk[4, SMALL_SINK_OFF:SMALL_SINK_OFF + 8].reshape(1, 8)
    out["loss"] = blk[4, SMALL_LOSS_OFF]
    return out


WEIGHT_ORDER = ("pre_norm_mix", "w_in", "q_a_norm", "w_q_b", "kv_a_norm", "w_kv_b", "sinks", "w_o_a", "w_o_b",
                "w_out", "post_norm_mix", "pre_norm_mlp", "w_up", "w_down", "post_norm_mlp")
SMALL_NAMES = ("pre_norm_mix", "q_a_norm", "kv_a_norm", "sinks", "post_norm_mix", "pre_norm_mlp", "post_norm_mlp")


def kernel(x, positions, pre_norm_mix, w_in, q_a_norm, w_q_b, kv_a_norm, w_kv_b, sinks, w_o_a, w_o_b, w_out, post_norm_mix, pre_norm_mlp, w_up, w_down, post_norm_mlp, loss_target, m_pre_norm_mix, m_w_in, m_q_a_norm, m_w_q_b, m_kv_a_norm, m_w_kv_b, m_sinks, m_w_o_a, m_w_o_b, m_w_out, m_post_norm_mix, m_pre_norm_mlp, m_w_up, m_w_down, m_post_norm_mlp, v_pre_norm_mix, v_w_in, v_q_a_norm, v_w_q_b, v_kv_a_norm, v_w_kv_b, v_sinks, v_w_o_a, v_w_o_b, v_w_out, v_post_norm_mix, v_pre_norm_mlp, v_w_up, v_w_down, v_post_norm_mlp):
    weights = dict(pre_norm_mix=pre_norm_mix, w_in=w_in, q_a_norm=q_a_norm, w_q_b=w_q_b, kv_a_norm=kv_a_norm,
                   w_kv_b=w_kv_b, sinks=sinks, w_o_a=w_o_a, w_o_b=w_o_b, w_out=w_out, post_norm_mix=post_norm_mix,
                   pre_norm_mlp=pre_norm_mlp, w_up=w_up, w_down=w_down, post_norm_mlp=post_norm_mlp)
    m_in = dict(pre_norm_mix=m_pre_norm_mix, w_in=m_w_in, q_a_norm=m_q_a_norm, w_q_b=m_w_q_b, kv_a_norm=m_kv_a_norm,
                w_kv_b=m_w_kv_b, sinks=m_sinks, w_o_a=m_w_o_a, w_o_b=m_w_o_b, w_out=m_w_out,
                post_norm_mix=m_post_norm_mix, pre_norm_mlp=m_pre_norm_mlp, w_up=m_w_up, w_down=m_w_down,
                post_norm_mlp=m_post_norm_mlp)
    v_in = dict(pre_norm_mix=v_pre_norm_mix, w_in=v_w_in, q_a_norm=v_q_a_norm, w_q_b=v_w_q_b, kv_a_norm=v_kv_a_norm,
                w_kv_b=v_w_kv_b, sinks=v_sinks, w_o_a=v_w_o_a, w_o_b=v_w_o_b, w_out=v_w_out,
                post_norm_mix=v_post_norm_mix, pre_norm_mlp=v_pre_norm_mlp, w_up=v_w_up, w_down=v_w_down,
                post_norm_mlp=v_post_norm_mlp)

    xs, pos, target = x[0], positions[0], loss_target[0]
    t = xs.shape[0]
    pos_col = pos.reshape(t, 1)
    pos_row = pos.reshape(1, t)
    g1, g2, g3, g4 = (weights[n] for n in SMALL_D_ROWS)
    g_q, g_kv = q_a_norm, kv_a_norm
    sink_vec = sinks.reshape(N_HEADS)
    shard = {n: weights[n][0].astype(BF16) for n in EARLY + LATE}

    tables, (e_in, e_qb, e_kvb) = _rope_tables(pos_col, _freq_row(), [shard[n] for n in EARLY])
    w_in_p = _pad_w_in(_from_col_shards(e_in))
    w_qb = _pad_heads_cols(_from_col_shards(e_qb), N_HEADS, QK_NOPE + QK_ROPE)
    w_kvb = _pad_w_kvb(_from_col_shards(e_kvb))

    (h, gates, qa, ka, va, cq, ckv, cqn, ckvn, kb, vb, qt, kt, vt) = _inproj_fwd(
        xs, g1, w_in_p, g_q, g_kv, w_kvb, w_qb.T, w_kvb[:, :HM].T, w_kvb[:, HM:].T, w_in_p[:, C_KR:].T, tables)
    out_a, lse_a = _swa_fwd(qa, ka, va, pos_col, pos_row, sink_vec)
    out_b, out_b_t, lse_b, (l_oa, l_ob, l_out, w_up_s, l_down) = _mla_fwd(qt, kb, vt, [shard[n] for n in LATE])
    w_oa = _pad_heads_rows(_from_col_shards(l_oa), N_HEADS, HEAD_A)
    w_ob = _pad_heads_rows(_from_col_shards(l_ob), N_HEADS, V_DIM_B)
    w_out_f = l_out.reshape(D_MODEL, D_MODEL)
    w_down_f = l_down.reshape(D_FF, D_MODEL)

    oa_p, ob_p, merged, y, x1, h2 = _merge_fwd(out_a, out_b, gates, xs, w_oa, w_ob, w_out_f, g2, g3)
    a, du, dy2, dx1, loss, dg3, dg4 = _mlp_fwd_bwd(x1, h2, target, w_up_s, w_down_f, g3, g4)
    (dy, d_oap, d_obp, dgates, d_oa, d_ob_t, delta_b, dg2) = _merge_bwd(
        dx1, y, gates, oa_p, ob_p, out_b_t, w_oa, w_ob, w_out_f, g2)
    late_slices = [
        _col_shards(_unpad_heads_rows(_matmul_tn(out_a, d_oap, "dw_o_a"), N_HEADS, HEAD_A)).astype(BF16),
        _col_shards(_unpad_heads_rows(_matmul_tn(out_b, d_obp, "dw_o_b"), N_HEADS, V_DIM_B)).astype(BF16),
        _matmul_tn(merged, dy, "dw_out", BF16).reshape(N_DEV, D_MODEL // N_DEV, D_MODEL),
        _matmul_tn(h2, du, "dw_up", BF16, N_DEV),
        _matmul_tn(a, dy2, "dw_down", BF16).reshape(N_DEV, D_FF // N_DEV, D_MODEL),
    ]
    dqa, dka, dva, dsink = _swa_bwd(qa, ka, va, out_a, d_oa, lse_a, pos_col, pos_row, sink_vec)
    dqb_t, dkb_t, dvb_t, late_parts = _mla_bwd(qt, kb, kt, vb, d_ob_t, lse_b, delta_b, late_slices)
    dproj, dqbr_t, dkvb_t, dx, dg1, dgq, dgkv = _inproj_bwd(
        dgates, dqa, dka, dva, dqb_t, dkb_t, dvb_t, cq, ckv, xs, dx1, *tables[3:], g1, g_q, g_kv,
        w_in_p, w_qb, w_kvb)
    early_slices = [
        _col_shards(_unpad_w_in(_matmul_tn(h, dproj, "dw_in"))).astype(BF16),
        _col_shards(_unpad_heads_cols(_matmul_nn(dqbr_t, cqn, "dw_q_b").T, N_HEADS, QK_NOPE + QK_ROPE)).astype(BF16),
        _col_shards(_unpad_w_kvb(_matmul_nn(dkvb_t, ckvn, "dw_kv_b").T)).astype(BF16),
    ]
    small_grads = {"pre_norm_mix": dg1, "post_norm_mix": dg2, "pre_norm_mlp": dg3, "post_norm_mlp": dg4,
                   "q_a_norm": dgq, "kv_a_norm": dgkv, "sinks": dsink.reshape(N_HEADS, BLOCK).sum(axis=1),
                   "loss": loss[0, 0:1]}
    early_parts, s_parts = _exchange_grads(early_slices, _pack_small(small_grads))

    updates = {}
    for name, parts in zip(EARLY + LATE, early_parts + late_parts):
        outs = _adamw(parts, weights[name], m_in[name], v_in[name], "adamw_" + name)
        for kind, arr in zip(("g", "d", "m", "v"), outs):
            updates[kind, name] = arr
    zero = jnp.zeros((), F32)
    pack = lambda src: _pack_small({**{n: src[n] for n in SMALL_NAMES}, "loss": zero})[None]
    smalls = _adamw(s_parts, pack(weights), pack(m_in), pack(v_in), "adamw_small")
    for kind, blk in zip(("g", "d", "m", "v"), smalls):
        for wname, piece in _unpack_small(blk[0]).items():
            updates[kind, wname] = piece
    results = [updates[kind, name] for kind in ("g", "d", "m", "v") for name in WEIGHT_ORDER]
    return (updates["g", "loss"], dx[None], *results)
```

```python
import functools

import numpy as np
import jax
import jax.numpy as jnp
from jax import lax
from jax.experimental import pallas as pl
from jax.experimental.pallas import tpu as pltpu

F32 = jnp.float32
BF16 = jnp.bfloat16

D_MODEL = 1024
D_FF = 4096
N_HEADS = 8
N_KV_A = 2
GROUP_A = N_HEADS // N_KV_A
HEAD_A = 64
QK_NOPE = 64
QK_ROPE = 32
V_DIM_B = 64
Q_LORA = 256
KV_LORA = 128
BLOCK = 128
SLAB = 128
ROPE_THETA = 10000.0
EPS = 1e-6
N_DEV = 8
NEG = -1e30

SCALE_A = HEAD_A ** -0.5
SCALE_B = (QK_NOPE + QK_ROPE) ** -0.5
LOG2E = 1.4426950408889634
SCORE_B = SCALE_B * LOG2E
MLA_HEADS_PER_STEP = 4
MLA_FWD_HEADS_PER_STEP = 8
ONES_ROWS = 16
SLOPES_A = tuple(2.0 ** (-8.0 * (h + 1) / N_HEADS) for h in range(N_HEADS))

ADAM_LR = 0.001
ADAM_B1 = 0.9
ADAM_B2 = 0.999
ADAM_EPS = 1e-08
ADAM_WD = 0.01
ADAM_STEP = 10

HM = N_HEADS * SLAB
C_GATES = 0
C_QA = 2 * D_MODEL
C_KA = C_QA + HM
C_VA = C_KA + N_KV_A * SLAB
C_CQ = C_VA + N_KV_A * SLAB
C_CKV = C_CQ + Q_LORA
C_KR = C_CKV + KV_LORA
D_IN_PAD = C_KR + SLAB

VMEM_LIMIT = 56 * 1024 * 1024

EARLY = ("w_in", "w_q_b", "w_kv_b")
LATE = ("w_o_a", "w_o_b", "w_out", "w_up", "w_down")
ADAM_ROWS = 256
SMALL_ROWS = 8


def _token_tile(t):
    return min(256, t)


def _attn_tile(t):
    return 512 if t >= 2048 else 128


def _params(sem, vmem=VMEM_LIMIT):
    return pltpu.CompilerParams(dimension_semantics=sem, vmem_limit_bytes=vmem)


def _dot(a, b):
    return jnp.dot(a, b, preferred_element_type=F32)


def _dot_nt(a, b):
    return lax.dot_general(a, b, (((1,), (1,)), ((), ())), preferred_element_type=F32)


def _dot_tn(a, b):
    return lax.dot_general(a, b, (((0,), (0,)), ((), ())), preferred_element_type=F32)


def _rms_r(x):
    return lax.rsqrt(jnp.mean(x * x, axis=-1, keepdims=True) + EPS)


def _rms_bwd(x, r, g, dy):
    t = dy * g
    return r * t - x * (r * r * r) * jnp.mean(x * t, axis=-1, keepdims=True)


def _sigmoid(x):
    return 1.0 / (1.0 + jnp.exp(-x))


def _rope(x, c, s1, s2):
    return x * c + pltpu.roll(x, SLAB - 16, 1) * s1 + pltpu.roll(x, 16, 1) * s2


def _rope_bwd(d, c, s1, s2):
    return d * c + pltpu.roll(d * s1, 16, 1) + pltpu.roll(d * s2, SLAB - 16, 1)


def _roll_rows(x, shift):
    return jnp.concatenate([x[-shift:], x[:-shift]], axis=0)


def _rope_t(x, c, s1, s2):
    return x * c + _roll_rows(x, SLAB - 16) * s1 + _roll_rows(x, 16) * s2


def _rope_t_bwd(d, c, s1, s2):
    return d * c + _roll_rows(d * s1, 16) + _roll_rows(d * s2, SLAB - 16)


def _row_spec(tm, n):
    return pl.BlockSpec((tm, n), lambda i: (i, 0))


def _col_spec(n, tm):
    return pl.BlockSpec((n, tm), lambda i: (0, i))


def _full_spec(shape):
    nd = len(shape)
    return pl.BlockSpec(shape, lambda i: (0,) * nd, pipeline_mode=pl.Buffered(1))


def _acc_rows(ref, val):
    @pl.when(pl.program_id(0) == 0)
    def _():
        ref[...] = jnp.zeros_like(ref)
    ref[...] += jnp.sum(val, axis=0, keepdims=True)


def _rope_tables(pos_col, freq_row, early):
    t = pos_col.shape[0]
    tm = _token_tile(t)
    n = len(early)

    def body(pos_ref, f_ref, *rest):
        shard_refs, (c_ref, s1_ref, s2_ref, ct_ref, s1t_ref, s2t_ref) = rest[:n], rest[n:n + 6]
        start, finish = _two_level_gather(shard_refs, rest[n + 6:2 * n + 6], *rest[2 * n + 6:])
        pl.when(pl.program_id(0) == 0)(start)
        ang = pos_ref[...].astype(F32) * f_ref[...]
        lane = lax.broadcasted_iota(jnp.int32, ang.shape, 1)
        s = jnp.sin(ang)
        c = jnp.cos(ang)
        s1 = jnp.where((lane >= 64) & (lane < 80), -s, 0.0)
        s2 = jnp.where((lane >= 80) & (lane < 96), s, 0.0)
        c_ref[...], s1_ref[...], s2_ref[...] = c, s1, s2
        ct_ref[...], s1t_ref[...], s2t_ref[...] = c.T, s1.T, s2.T
        pl.when(pl.program_id(0) == t // tm - 1)(finish)

    tab = jax.ShapeDtypeStruct((t, SLAB), F32)
    tabt = jax.ShapeDtypeStruct((SLAB, t), F32)
    outs = pl.pallas_call(
        body, name="rope_tables", grid=(t // tm,),
        in_specs=[_row_spec(tm, 1), _full_spec((1, SLAB))] + [ANY_SPEC] * n,
        out_specs=[_row_spec(tm, SLAB)] * 3 + [_col_spec(SLAB, tm)] * 3 + [ANY_SPEC] * n,
        out_shape=[tab] * 3 + [tabt] * 3 + [jax.ShapeDtypeStruct((N_DEV,) + a.shape, a.dtype) for a in early],
        scratch_shapes=_exchange_scratch(n),
        compiler_params=_params(("arbitrary",)),
    )(pos_col, freq_row, *early)
    return outs[:6], outs[6:]


def _inproj_fwd(x, g1, w_in, g_q, g_kv, w_kvb, w_qb_t, w_kb_t, w_vb_t, w_kr_t, tables):
    t = x.shape[0]
    tm = _token_tile(t)

    def body(x_ref, g1_ref, win_ref, gq_ref, gkv_ref, wkvb_ref, wqbt_ref, wkbt_ref, wvbt_ref, wkrt_ref,
             c_ref, s1_ref, s2_ref, ct_ref, s1t_ref, s2t_ref,
             h_ref, gates_ref, qa_ref, ka_ref, va_ref, cq_ref, ckv_ref, cqn_ref, ckvn_ref,
             kb_ref, vb_ref, qt_ref, kt_ref, vt_ref):
        xv = x_ref[...]
        h = (xv * _rms_r(xv) * g1_ref[...]).astype(BF16)
        h_ref[...] = h
        proj = _dot(h, win_ref[...])
        gates_ref[...] = proj[:, C_GATES:C_QA].astype(BF16)
        qa_ref[...] = proj[:, C_QA:C_KA].astype(BF16)
        ka_ref[...] = proj[:, C_KA:C_VA].astype(BF16)
        va_ref[...] = proj[:, C_VA:C_CQ].astype(BF16)
        cq = proj[:, C_CQ:C_CKV]
        ckv = proj[:, C_CKV:C_KR]
        kr = proj[:, C_KR:D_IN_PAD]
        cq_ref[...] = cq
        ckv_ref[...] = ckv
        cqn = (cq * _rms_r(cq) * gq_ref[...]).astype(BF16)
        ckvn = (ckv * _rms_r(ckv) * gkv_ref[...]).astype(BF16)
        cqn_ref[...] = cqn
        ckvn_ref[...] = ckvn
        c, s1, s2 = c_ref[...], s1_ref[...], s2_ref[...]
        kvb = _dot(ckvn, wkvb_ref[...])
        kr_rot = _rope(kr, c, s1, s2)
        ct, s1t, s2t = ct_ref[...], s1t_ref[...], s2t_ref[...]
        q_t = _dot_nt(wqbt_ref[...], cqn)
        k_t = _dot_nt(wkbt_ref[...], ckvn)
        kr_t = _rope_t(_dot_nt(wkrt_ref[...], h), ct, s1t, s2t)
        for hd in range(N_HEADS):
            sl = slice(hd * SLAB, (hd + 1) * SLAB)
            kb_ref[:, sl] = (kvb[:, sl] + kr_rot).astype(BF16)
            qt_ref[sl, :] = (_rope_t(q_t[sl, :], ct, s1t, s2t) * SCORE_B).astype(BF16)
            kt_ref[sl, :] = (k_t[sl, :] + kr_t).astype(BF16)
        vb_ref[...] = kvb[:, HM:2 * HM].astype(BF16)
        pad_row = lax.broadcasted_iota(jnp.int32, (HM, 1), 0) & (SLAB - 1)
        ones_rows = jnp.where((pad_row >= V_DIM_B) & (pad_row < V_DIM_B + ONES_ROWS), 1.0, 0.0)
        vt_ref[...] = (_dot_nt(wvbt_ref[...], ckvn) + ones_rows).astype(BF16)

    def sds(n, dt):
        return jax.ShapeDtypeStruct((t, n), dt)

    outs = [(D_MODEL, BF16), (2 * D_MODEL, BF16), (HM, BF16), (N_KV_A * SLAB, BF16), (N_KV_A * SLAB, BF16),
            (Q_LORA, F32), (KV_LORA, F32), (Q_LORA, BF16), (KV_LORA, BF16), (HM, BF16), (HM, BF16)]
    tab, tabt = _row_spec(tm, SLAB), _col_spec(SLAB, tm)
    return pl.pallas_call(
        body, name="inproj_fwd", grid=(t // tm,),
        in_specs=[_row_spec(tm, D_MODEL), _full_spec((1, D_MODEL)), _full_spec((D_MODEL, D_IN_PAD)),
                  _full_spec((1, Q_LORA)), _full_spec((1, KV_LORA)), _full_spec((KV_LORA, 2 * HM)),
                  _full_spec((HM, Q_LORA)), _full_spec((HM, KV_LORA)), _full_spec((HM, KV_LORA)),
                  _full_spec((SLAB, D_MODEL)), tab, tab, tab, tabt, tabt, tabt],
        out_specs=[_row_spec(tm, n) for n, _ in outs] + [_col_spec(HM, tm)] * 3,
        out_shape=[sds(n, dt) for n, dt in outs] + [jax.ShapeDtypeStruct((HM, t), BF16)] * 3,
        compiler_params=_params(("parallel",)),
    )(x, g1, w_in, g_q, g_kv, w_kvb, w_qb_t, w_kb_t, w_vb_t, w_kr_t, *tables)


def _tile_group(a):
    return jnp.concatenate([a] * GROUP_A, axis=1)


def _swa_masks():
    row = lax.broadcasted_iota(jnp.int32, (BLOCK, GROUP_A * BLOCK), 0)
    col = lax.broadcasted_iota(jnp.int32, (BLOCK, GROUP_A * BLOCK), 1) & (BLOCK - 1)
    return row <= col, row > col


def _heads_beside(ref, g):
    return jnp.concatenate([ref[:, (g * GROUP_A + hh) * SLAB:(g * GROUP_A + hh + 1) * SLAB].T
                            for hh in range(GROUP_A)], axis=1)


def _rows_beside(ref, g):
    return jnp.concatenate([ref[g * GROUP_A + hh] for hh in range(GROUP_A)], axis=1)


def _swa_rows(sinks):
    slopes = jnp.repeat(jnp.asarray(SLOPES_A, F32).reshape(N_KV_A, GROUP_A, 1), BLOCK, axis=2)
    sink_rows = jnp.repeat(sinks.reshape(N_KV_A, GROUP_A, 1), BLOCK, axis=2)
    return slopes.reshape(N_KV_A, 1, GROUP_A * BLOCK), sink_rows.reshape(N_KV_A, 1, GROUP_A * BLOCK)


def _swa_fwd(qa, ka, va, pos_col, pos_row, sinks):
    t = qa.shape[0]
    nb = t // BLOCK
    gw = GROUP_A * BLOCK
    slope_rows, sink_rows = _swa_rows(sinks)

    def body(q_ref, kc_ref, kp_ref, vc_ref, vp_ref, pkc_ref, pkp_ref, pq_ref, slope_ref, sink_ref, o_ref, l_ref):
        i = pl.program_id(0)
        pq = pq_ref[...]
        dist_c = _tile_group(jnp.abs(pkc_ref[...] - pq).astype(F32))
        dist_p = _tile_group(jnp.abs(pkp_ref[...] - pq).astype(F32))
        mask_c, older = _swa_masks()
        mask_p = jnp.logical_and(older, i > 0)
        for g in range(N_KV_A):
            gs = slice(g * SLAB, (g + 1) * SLAB)
            x = _heads_beside(q_ref, g)
            slope, sink = slope_ref[g], sink_ref[g]
            s_c = jnp.where(mask_c, _dot(kc_ref[:, gs], x) * SCALE_A - slope * dist_c, NEG)
            s_p = jnp.where(mask_p, _dot(kp_ref[:, gs], x) * SCALE_A - slope * dist_p, NEG)
            m = jnp.maximum(jnp.maximum(jnp.max(s_c, axis=0, keepdims=True),
                                        jnp.max(s_p, axis=0, keepdims=True)), sink)
            e_c = jnp.exp(s_c - m)
            e_p = jnp.exp(s_p - m)
            den = jnp.sum(e_c, axis=0, keepdims=True) + jnp.sum(e_p, axis=0, keepdims=True) + jnp.exp(sink - m)
            inv = 1.0 / den
            ot = (_dot_tn(vc_ref[:, gs], (e_c * inv).astype(BF16))
                  + _dot_tn(vp_ref[:, gs], (e_p * inv).astype(BF16)))
            lse = m + jnp.log(den)
            for hh in range(GROUP_A):
                hd = g * GROUP_A + hh
                seg = slice(hh * BLOCK, (hh + 1) * BLOCK)
                o_ref[:, hd * SLAB:(hd + 1) * SLAB] = ot[:, seg].T.astype(BF16)
                l_ref[hd] = lse[:, seg]

    cur = lambda i: (i, 0)
    prev = lambda i: (jnp.maximum(i - 1, 0), 0)
    kvw = N_KV_A * SLAB
    rows = pl.BlockSpec((N_KV_A, 1, gw), lambda i: (0, 0, 0))
    return pl.pallas_call(
        body, name="swa_fwd", grid=(nb,),
        in_specs=[pl.BlockSpec((BLOCK, HM), cur),
                  pl.BlockSpec((BLOCK, kvw), cur), pl.BlockSpec((BLOCK, kvw), prev),
                  pl.BlockSpec((BLOCK, kvw), cur), pl.BlockSpec((BLOCK, kvw), prev),
                  pl.BlockSpec((BLOCK, 1), cur), pl.BlockSpec((BLOCK, 1), prev),
                  pl.BlockSpec((1, BLOCK), lambda i: (0, i)), rows, rows],
        out_specs=[pl.BlockSpec((BLOCK, HM), cur), pl.BlockSpec((N_HEADS, 1, BLOCK), lambda i: (0, 0, i))],
        out_shape=[jax.ShapeDtypeStruct((t, HM), BF16), jax.ShapeDtypeStruct((N_HEADS, 1, t), F32)],
        compiler_params=_params(("parallel",)),
    )(qa, ka, ka, va, va, pos_col, pos_col, pos_row, slope_rows, sink_rows)


def _swa_bwd(qa, ka, va, out_a, d_oa, lse, pos_col, pos_row, sinks):
    t = qa.shape[0]
    nb = t // BLOCK
    gw = GROUP_A * BLOCK
    slope_rows, sink_rows = _swa_rows(sinks)

    def body(q_ref, qn_ref, do_ref, don_ref, l_ref, ln_ref, o_ref, on_ref, kp_ref, kc_ref, vp_ref, vc_ref,
             pkp_ref, pkc_ref, pq_ref, pqn_ref, slope_ref, sink_ref, dq_ref, dk_ref, dv_ref, dsink_ref):
        j = pl.program_id(0)
        pkc, pkp = pkc_ref[...], pkp_ref[...]
        dist_cc = _tile_group(jnp.abs(pkc - pq_ref[...]).astype(F32))
        dist_cp = _tile_group(jnp.abs(pkp - pq_ref[...]).astype(F32))
        dist_nc = _tile_group(jnp.abs(pkc - pqn_ref[...]).astype(F32))
        mask_cc, older = _swa_masks()
        mask_cp = jnp.logical_and(older, j > 0)
        mask_nc = jnp.logical_and(older, j < nb - 1)

        @pl.when(j == 0)
        def _():
            dsink_ref[...] = jnp.zeros_like(dsink_ref)

        def tile(k, v, x, dox, lrow, drow, dist, mask, slope):
            s = jnp.where(mask, _dot(k, x) * SCALE_A - slope * dist, NEG)
            p = jnp.exp(s - lrow)
            ds = p * (_dot(v, dox) - drow)
            return p.astype(BF16), ds.astype(BF16)

        for g in range(N_KV_A):
            gs = slice(g * SLAB, (g + 1) * SLAB)
            kc, kp, vc, vp = kc_ref[:, gs], kp_ref[:, gs], vc_ref[:, gs], vp_ref[:, gs]
            slope, sink = slope_ref[g], sink_ref[g]
            x, xn = _heads_beside(q_ref, g), _heads_beside(qn_ref, g)
            dox, doxn = _heads_beside(do_ref, g), _heads_beside(don_ref, g)
            lrow, lrown = _rows_beside(l_ref, g), _rows_beside(ln_ref, g)
            drow = jnp.sum(dox.astype(F32) * _heads_beside(o_ref, g).astype(F32), axis=0, keepdims=True)
            drown = jnp.sum(doxn.astype(F32) * _heads_beside(on_ref, g).astype(F32), axis=0, keepdims=True)
            p_cc, ds_cc = tile(kc, vc, x, dox, lrow, drow, dist_cc, mask_cc, slope)
            _, ds_cp = tile(kp, vp, x, dox, lrow, drow, dist_cp, mask_cp, slope)
            p_nc, ds_nc = tile(kc, vc, xn, doxn, lrown, drown, dist_nc, mask_nc, slope)
            dqt = (_dot_tn(kc, ds_cc) + _dot_tn(kp, ds_cp)) * SCALE_A
            for hh in range(GROUP_A):
                hd = g * GROUP_A + hh
                dq_ref[:, hd * SLAB:(hd + 1) * SLAB] = dqt[:, hh * BLOCK:(hh + 1) * BLOCK].T.astype(BF16)
            dk_ref[:, gs] = ((_dot_nt(ds_cc, x) + _dot_nt(ds_nc, xn)) * SCALE_A).astype(BF16)
            dv_ref[:, gs] = (_dot_nt(p_cc, dox) + _dot_nt(p_nc, doxn)).astype(BF16)
            dsink_ref[g] -= jnp.exp(sink - lrow) * drow

    cur = lambda j: (j, 0)
    prev = lambda j: (jnp.maximum(j - 1, 0), 0)
    nxt = lambda j: (jnp.minimum(j + 1, nb - 1), 0)
    cur3 = lambda j: (0, 0, j)
    nxt3 = lambda j: (0, 0, jnp.minimum(j + 1, nb - 1))
    kvw = N_KV_A * SLAB
    rows = pl.BlockSpec((N_KV_A, 1, gw), lambda j: (0, 0, 0))
    stat = lambda im: pl.BlockSpec((N_HEADS, 1, BLOCK), im)
    return pl.pallas_call(
        body, name="swa_bwd", grid=(nb,),
        in_specs=[pl.BlockSpec((BLOCK, HM), cur), pl.BlockSpec((BLOCK, HM), nxt),
                  pl.BlockSpec((BLOCK, HM), cur), pl.BlockSpec((BLOCK, HM), nxt),
                  stat(cur3), stat(nxt3), pl.BlockSpec((BLOCK, HM), cur), pl.BlockSpec((BLOCK, HM), nxt),
                  pl.BlockSpec((BLOCK, kvw), prev), pl.BlockSpec((BLOCK, kvw), cur),
                  pl.BlockSpec((BLOCK, kvw), prev), pl.BlockSpec((BLOCK, kvw), cur),
                  pl.BlockSpec((BLOCK, 1), prev), pl.BlockSpec((BLOCK, 1), cur),
                  pl.BlockSpec((1, BLOCK), lambda j: (0, j)),
                  pl.BlockSpec((1, BLOCK), lambda j: (0, jnp.minimum(j + 1, nb - 1))), rows, rows],
        out_specs=[pl.BlockSpec((BLOCK, HM), cur), pl.BlockSpec((BLOCK, kvw), cur),
                   pl.BlockSpec((BLOCK, kvw), cur), rows],
        out_shape=[jax.ShapeDtypeStruct((t, HM), BF16), jax.ShapeDtypeStruct((t, kvw), BF16),
                   jax.ShapeDtypeStruct((t, kvw), BF16), jax.ShapeDtypeStruct((N_KV_A, 1, gw), F32)],
        compiler_params=_params(("arbitrary",)),
    )(qa, qa, d_oa, d_oa, lse, lse, out_a, out_a, ka, ka, va, va,
      pos_col, pos_col, pos_row, pos_row, slope_rows, sink_rows)


def _mesh_pos():
    return lax.axis_index("x"), lax.axis_index("y"), lax.axis_index("c")


def _flip(v, bit):
    return 1 - v if bit else v


def _direct_copies(srcs, dsts, send_sems, recv_sems, local_sems, gather, sem_base=0):
    x, y, c = _mesh_pos()
    me = 4 * x + 2 * y + c
    local, remote = [], []
    for a, (src, dst) in enumerate(zip(srcs, dsts)):
        local.append(pltpu.make_async_copy(src if gather else src.at[me], dst.at[me], local_sems.at[sem_base + a]))
        for r in range(1, N_DEV):
            px, py, pc = _flip(x, r & 4), _flip(y, r & 2), _flip(c, r & 1)
            sem = (N_DEV - 1) * (sem_base + a) + r - 1
            remote.append(pltpu.make_async_remote_copy(
                src_ref=src if gather else src.at[4 * px + 2 * py + pc], dst_ref=dst.at[me],
                send_sem=send_sems.at[sem], recv_sem=recv_sems.at[sem],
                device_id=(px, py, pc), device_id_type=pl.DeviceIdType.MESH))
    return local, remote


def _start_copies(local, remote):
    for cp in local + remote:
        cp.start()


def _wait_copies(local, remote):
    for cp in remote:
        cp.wait_recv()
    for cp in remote:
        cp.wait_send()
    for cp in local:
        cp.wait()


def _exchange_scratch(n):
    return [pltpu.SemaphoreType.DMA((n * (N_DEV - 1),)), pltpu.SemaphoreType.DMA((n * (N_DEV - 1),)),
            pltpu.SemaphoreType.DMA((n,))]


ANY_SPEC = pl.BlockSpec(memory_space=pl.ANY)


def _mla_fwd(qt, kb, vt, late):
    t = kb.shape[0]
    tk = _attn_tile(t)
    ratio = 2 if t >= 2 * tk else 1
    tq = ratio * tk
    nq = t // tq
    hps = MLA_FWD_HEADS_PER_STEP
    w = hps * SLAB
    pairs = [(i, j) for i in range(nq) for j in range(ratio * (i + 1))]
    i_tab = jnp.asarray(np.array([p[0] for p in pairs], np.int32))
    j_tab = jnp.asarray(np.array([p[1] for p in pairs], np.int32))

    n_late = len(late)

    def body(it_ref, jt_ref, qt_ref, k_ref, vt_ref, *rest):
        late_refs, (o_ref, ot_ref, l_ref) = rest[:n_late], rest[n_late:n_late + 3]
        gathered_refs = rest[n_late + 3:2 * n_late + 3]
        m_s, acc_s, send_sems, recv_sems, local_sems = rest[2 * n_late + 3:]
        n = pl.program_id(1)
        i, j = it_ref[n], jt_ref[n]
        first_step = jnp.logical_and(pl.program_id(0) == 0, n == 0)
        last_step = jnp.logical_and(pl.program_id(0) == N_HEADS // hps - 1, n == len(pairs) - 1)

        @pl.when(first_step)
        def _():
            _start_copies(*_direct_copies(late_refs, gathered_refs, send_sems, recv_sems, local_sems, True))

        @pl.when(j == 0)
        def _():
            m_s[...] = jnp.full_like(m_s, NEG)
            acc_s[...] = jnp.zeros_like(acc_s)

        def update(masked, q0):
            qc = slice(q0, tq)

            def scores(hh):
                sl = slice(hh * SLAB, (hh + 1) * SLAB)
                return _dot(k_ref[:, sl], qt_ref[sl, qc])

            def softmax(hh, s):
                if masked:
                    s = jnp.where(lax.broadcasted_iota(jnp.int32, s.shape, 0)
                                  <= lax.broadcasted_iota(jnp.int32, s.shape, 1), s, NEG)
                m_old = m_s[hh][:, qc]
                m_new = jnp.maximum(m_old, jnp.max(s, axis=0, keepdims=True))
                m_s[hh, :, qc] = m_new
                return jnp.exp2(s - m_new).astype(BF16), jnp.exp2(m_old - m_new)

            def accumulate(hh, p, alpha):
                sl = slice(hh * SLAB, hh * SLAB + V_DIM_B + ONES_ROWS)
                acc_s[sl, qc] = alpha * acc_s[sl, qc] + _dot(vt_ref[sl, :], p)

            s_next, pending = scores(0), None
            for hh in range(hps):
                s = s_next
                if hh + 1 < hps:
                    s_next = scores(hh + 1)
                p, alpha = softmax(hh, s)
                if pending is not None:
                    accumulate(*pending)
                pending = (hh, p, alpha)
            accumulate(*pending)

        @pl.when(j < ratio * i)
        def _():
            update(False, 0)

        for part in range(ratio):
            @pl.when(j == ratio * i + part)
            def _():
                update(True, part * tk)

        @pl.when(j == ratio * i + ratio - 1)
        def _():
            for hh in range(hps):
                sl = slice(hh * SLAB, (hh + 1) * SLAB)
                den = acc_s[hh * SLAB + V_DIM_B:hh * SLAB + V_DIM_B + 1, :]
                values = lax.broadcasted_iota(jnp.int32, (SLAB, tq), 0) < V_DIM_B
                ot = jnp.where(values, acc_s[sl, :] / den, 0.0)
                ot_ref[sl, :] = ot.astype(BF16)
                o_ref[:, sl] = ot.T.astype(BF16)
                l_ref[hh] = m_s[hh] + jnp.log2(den)

        @pl.when(last_step)
        def _():
            _wait_copies(*_direct_copies(late_refs, gathered_refs, send_sems, recv_sems, local_sems, True))

    grid_spec = pltpu.PrefetchScalarGridSpec(
        num_scalar_prefetch=2, grid=(N_HEADS // hps, len(pairs)),
        in_specs=[pl.BlockSpec((w, tq), lambda h, n, it, jt: (h, it[n])),
                  pl.BlockSpec((tk, w), lambda h, n, it, jt: (jt[n], h)),
                  pl.BlockSpec((w, tk), lambda h, n, it, jt: (h, jt[n]))] + [ANY_SPEC] * n_late,
        out_specs=[pl.BlockSpec((tq, w), lambda h, n, it, jt: (it[n], h)),
                   pl.BlockSpec((w, tq), lambda h, n, it, jt: (h, it[n])),
                   pl.BlockSpec((hps, 1, tq), lambda h, n, it, jt: (h, 0, it[n]))] + [ANY_SPEC] * n_late,
        scratch_shapes=[pltpu.VMEM((hps, 1, tq), F32), pltpu.VMEM((w, tq), F32)] + _exchange_scratch(n_late))
    outs = pl.pallas_call(
        body, name="mla_fwd", grid_spec=grid_spec,
        out_shape=[jax.ShapeDtypeStruct((t, HM), BF16), jax.ShapeDtypeStruct((HM, t), BF16),
                   jax.ShapeDtypeStruct((N_HEADS, 1, t), F32)]
        + [jax.ShapeDtypeStruct((N_DEV,) + a.shape, a.dtype) for a in late],
        compiler_params=_params(("arbitrary", "arbitrary")),
    )(i_tab, j_tab, qt, kb, vt, *late)
    return outs[0], outs[1], outs[2], list(outs[3:])


def _mla_bwd(qt, kb, kt, vb, d_ob_t, lse, delta, grad_slices):
    t = kb.shape[0]
    tk = _attn_tile(t)
    ratio = 2 if t >= 2 * tk else 1
    tq = ratio * tk
    nk, nq = t // tk, t // tq
    hps = MLA_HEADS_PER_STEP
    w = hps * SLAB
    pairs = [(j, i) for j in range(nk) for i in range(j // ratio, nq)]
    j_tab = jnp.asarray(np.array([p[0] for p in pairs], np.int32))
    i_tab = jnp.asarray(np.array([p[1] for p in pairs], np.int32))

    n_ex = len(grad_slices)

    def body(jt_ref, it_ref, qt_ref, dot_ref, l_ref, dl_ref, k_ref, kt_ref, v_ref, *rest):
        slice_refs, (dqt_ref, dkt_ref, dvt_ref) = rest[:n_ex], rest[n_ex:n_ex + 3]
        part_refs = rest[n_ex + 3:2 * n_ex + 3]
        dk_s, dv_s, send_sems, recv_sems, local_sems = rest[2 * n_ex + 3:]
        n = pl.program_id(1)
        j, i = jt_ref[n], it_ref[n]
        first_step = jnp.logical_and(pl.program_id(0) == 0, n == 0)
        last_step = jnp.logical_and(pl.program_id(0) == N_HEADS // hps - 1, n == len(pairs) - 1)

        @pl.when(first_step)
        def _():
            _start_copies(*_direct_copies(slice_refs, part_refs, send_sems, recv_sems, local_sems, False))

        @pl.when(n == 0)
        def _():
            dqt_ref[...] = jnp.zeros_like(dqt_ref)

        def update(diagonal, q0):
            qc = slice(q0, tq)
            cols = pl.ds(pl.multiple_of(i * tq + q0, tk), tq - q0)

            def products(hh):
                sl = slice(hh * SLAB, (hh + 1) * SLAB)
                return _dot(k_ref[:, sl], qt_ref[sl, qc]), _dot(v_ref[:, sl], dot_ref[sl, qc])

            def softmax_bwd(hh, s, dp):
                if diagonal:
                    s = jnp.where(lax.broadcasted_iota(jnp.int32, s.shape, 0)
                                  <= lax.broadcasted_iota(jnp.int32, s.shape, 1), s, NEG)
                p = jnp.exp2(s - l_ref[hh][:, qc])
                return p.astype(BF16), (p * (dp - dl_ref[hh][:, qc])).astype(BF16)

            def gradients(hh, p, ds):
                base = hh * SLAB
                vrows = slice(base, base + V_DIM_B)
                qrows = slice(base, base + QK_NOPE + QK_ROPE)
                dv = _dot_nt(dot_ref[vrows, qc], p)
                dk = _dot_nt(qt_ref[qrows, qc], ds)
                if diagonal:
                    dv_s[base:base + SLAB, :] = jnp.concatenate([dv, jnp.zeros((SLAB - V_DIM_B, tk), F32)], axis=0)
                    dk_s[base:base + SLAB, :] = jnp.concatenate(
                        [dk, jnp.zeros((SLAB - QK_NOPE - QK_ROPE, tk), F32)], axis=0)
                else:
                    dv_s[vrows, :] += dv
                    dk_s[qrows, :] += dk
                dqt_ref[qrows, cols] += _dot(kt_ref[qrows, :], ds)

            for hh in range(hps):
                gradients(hh, *softmax_bwd(hh, *products(hh)))

        first_tile = lax.div(j, ratio)
        for part in range(ratio):
            @pl.when(jnp.logical_and(i == first_tile, lax.rem(j, ratio) == part))
            def _():
                update(True, part * tk)

        @pl.when(i > first_tile)
        def _():
            update(False, 0)

        @pl.when(i == nq - 1)
        def _():
            dkt_ref[...] = (dk_s[...] * (1.0 / LOG2E)).astype(BF16)
            dvt_ref[...] = dv_s[...].astype(BF16)

        @pl.when(last_step)
        def _():
            _wait_copies(*_direct_copies(slice_refs, part_refs, send_sems, recv_sems, local_sems, False))

    grid_spec = pltpu.PrefetchScalarGridSpec(
        num_scalar_prefetch=2, grid=(N_HEADS // hps, len(pairs)),
        in_specs=[pl.BlockSpec((w, tq), lambda h, n, jt, it: (h, it[n])),
                  pl.BlockSpec((w, tq), lambda h, n, jt, it: (h, it[n])),
                  pl.BlockSpec((hps, 1, tq), lambda h, n, jt, it: (h, 0, it[n])),
                  pl.BlockSpec((hps, 1, tq), lambda h, n, jt, it: (h, 0, it[n])),
                  pl.BlockSpec((tk, w), lambda h, n, jt, it: (jt[n], h)),
                  pl.BlockSpec((w, tk), lambda h, n, jt, it: (h, jt[n])),
                  pl.BlockSpec((tk, w), lambda h, n, jt, it: (jt[n], h))] + [ANY_SPEC] * n_ex,
        out_specs=[pl.BlockSpec((w, t), lambda h, n, jt, it: (h, 0)),
                   pl.BlockSpec((w, tk), lambda h, n, jt, it: (h, jt[n])),
                   pl.BlockSpec((w, tk), lambda h, n, jt, it: (h, jt[n]))] + [ANY_SPEC] * n_ex,
        scratch_shapes=[pltpu.VMEM((w, tk), F32), pltpu.VMEM((w, tk), F32)] + _exchange_scratch(n_ex))
    outs = pl.pallas_call(
        body, name="mla_bwd", grid_spec=grid_spec,
        out_shape=[jax.ShapeDtypeStruct((HM, t), F32), jax.ShapeDtypeStruct((HM, t), BF16),
                   jax.ShapeDtypeStruct((HM, t), BF16)]
        + [jax.ShapeDtypeStruct(a.shape, a.dtype) for a in grad_slices],
        compiler_params=_params(("arbitrary", "arbitrary")),
    )(j_tab, i_tab, qt, d_ob_t, lse, delta, kb, kt, vb, *grad_slices)
    return outs[0], outs[1], outs[2], list(outs[3:])


def _merge_fwd(out_a, out_b, gates, x, w_oa, w_ob, w_out, g2, g3):
    t = x.shape[0]
    tm = _token_tile(t)

    def body(oa_ref, ob_ref, gates_ref, x_ref, woa_ref, wob_ref, wout_ref, g2_ref, g3_ref,
             oap_ref, obp_ref, merged_ref, y_ref, x1_ref, h2_ref):
        oa_p = _dot(oa_ref[...], woa_ref[...])
        ob_p = _dot(ob_ref[...], wob_ref[...])
        oap_ref[...] = oa_p.astype(BF16)
        obp_ref[...] = ob_p.astype(BF16)
        sa = _sigmoid(gates_ref[:, 0:D_MODEL].astype(F32))
        sb = _sigmoid(gates_ref[:, D_MODEL:2 * D_MODEL].astype(F32))
        merged = (sa * oa_p + sb * ob_p).astype(BF16)
        merged_ref[...] = merged
        y = _dot(merged, wout_ref[...])
        y_ref[...] = y
        x1 = x_ref[...] + y * _rms_r(y) * g2_ref[...]
        x1_ref[...] = x1
        h2_ref[...] = (x1 * _rms_r(x1) * g3_ref[...]).astype(BF16)

    def sds(dt):
        return jax.ShapeDtypeStruct((t, D_MODEL), dt)

    row = _row_spec(tm, D_MODEL)
    return pl.pallas_call(
        body, name="merge_fwd", grid=(t // tm,),
        in_specs=[_row_spec(tm, HM), _row_spec(tm, HM), _row_spec(tm, 2 * D_MODEL), row,
                  _full_spec((HM, D_MODEL)), _full_spec((HM, D_MODEL)), _full_spec((D_MODEL, D_MODEL)),
                  _full_spec((1, D_MODEL)), _full_spec((1, D_MODEL))],
        out_specs=[row] * 6,
        out_shape=[sds(BF16), sds(BF16), sds(BF16), sds(F32), sds(F32), sds(BF16)],
        compiler_params=_params(("parallel",)),
    )(out_a, out_b, gates, x, w_oa, w_ob, w_out, g2, g3)


def _merge_bwd(dx1, y, gates, oa_p, ob_p, out_b_t, w_oa, w_ob, w_out, g2):
    t = dx1.shape[0]
    tm = _token_tile(t)

    def body(dx1_ref, y_ref, gates_ref, oap_ref, obp_ref, obt_ref, woa_ref, wob_ref, wout_ref, g2_ref,
             dy_ref, doap_ref, dobp_ref, dgates_ref, doa_ref, dobt_ref, dlb_ref, dg2_ref):
        dx1v = dx1_ref[...]
        yv = y_ref[...]
        r2 = _rms_r(yv)
        _acc_rows(dg2_ref, dx1v * yv * r2)
        dy = _rms_bwd(yv, r2, g2_ref[...], dx1v).astype(BF16)
        dy_ref[...] = dy
        dm = _dot_nt(dy, wout_ref[...])
        sa = _sigmoid(gates_ref[:, 0:D_MODEL].astype(F32))
        sb = _sigmoid(gates_ref[:, D_MODEL:2 * D_MODEL].astype(F32))
        d_oap = (dm * sa).astype(BF16)
        d_obp = (dm * sb).astype(BF16)
        doap_ref[...] = d_oap
        dobp_ref[...] = d_obp
        dgates_ref[:, 0:D_MODEL] = (dm * oap_ref[...].astype(F32) * sa * (1.0 - sa)).astype(BF16)
        dgates_ref[:, D_MODEL:2 * D_MODEL] = (dm * obp_ref[...].astype(F32) * sb * (1.0 - sb)).astype(BF16)
        doa_ref[...] = _dot_nt(d_oap, woa_ref[...]).astype(BF16)
        d_ob_t = _dot_nt(wob_ref[...], d_obp)
        dobt_ref[...] = d_ob_t.astype(BF16)
        for hd in range(N_HEADS):
            sl = slice(hd * SLAB, (hd + 1) * SLAB)
            dlb_ref[hd] = jnp.sum(d_ob_t[sl, :] * obt_ref[sl, :].astype(F32), axis=0, keepdims=True)

    def sds(n, dt):
        return jax.ShapeDtypeStruct((t, n), dt)

    row = _row_spec(tm, D_MODEL)
    head3 = pl.BlockSpec((N_HEADS, 1, tm), lambda i: (0, 0, i))
    return pl.pallas_call(
        body, name="merge_bwd", grid=(t // tm,),
        in_specs=[row, row, _row_spec(tm, 2 * D_MODEL), row, row, _col_spec(HM, tm),
                  _full_spec((HM, D_MODEL)), _full_spec((HM, D_MODEL)), _full_spec((D_MODEL, D_MODEL)),
                  _full_spec((1, D_MODEL))],
        out_specs=[row, row, row, _row_spec(tm, 2 * D_MODEL), _row_spec(tm, HM), _col_spec(HM, tm),
                   head3, _full_spec((1, D_MODEL))],
        out_shape=[sds(D_MODEL, BF16), sds(D_MODEL, BF16), sds(D_MODEL, BF16), sds(2 * D_MODEL, BF16),
                   sds(HM, BF16), jax.ShapeDtypeStruct((HM, t), BF16),
                   jax.ShapeDtypeStruct((N_HEADS, 1, t), F32), jax.ShapeDtypeStruct((1, D_MODEL), F32)],
        compiler_params=_params(("arbitrary",)),
    )(dx1, y, gates, oa_p, ob_p, out_b_t, w_oa, w_ob, w_out, g2)


def _mlp_fwd_bwd(x1, h2, target, w_up, w_down, g3, g4):
    t = x1.shape[0]
    tm = _token_tile(t)
    fs = D_FF // N_DEV

    def body(x1_ref, h2_ref, tgt_ref, wup_ref, wdown_ref, g3_ref, g4_ref,
             a_ref, du_ref, dy2_ref, dx1_ref, loss_ref, dg3_ref, dg4_ref):
        x1v = x1_ref[...]
        h2v = h2_ref[...]
        u = jnp.concatenate([_dot(h2v, wup_ref[s]) for s in range(N_DEV)], axis=1)
        ru = jnp.maximum(u, 0.0)
        a = (ru * ru).astype(BF16)
        a_ref[...] = a
        y2 = _dot(a, wdown_ref[...])
        r4 = _rms_r(y2)
        diff = x1v + y2 * r4 * g4_ref[...] - tgt_ref[...]
        _acc_rows(loss_ref, jnp.sum(diff * diff, axis=-1, keepdims=True) * (0.5 / D_MODEL)
                  * jnp.ones((1, SLAB), F32))
        dx2 = diff * (1.0 / D_MODEL)
        _acc_rows(dg4_ref, dx2 * y2 * r4)
        dy2 = _rms_bwd(y2, r4, g4_ref[...], dx2).astype(BF16)
        dy2_ref[...] = dy2
        du = (_dot_nt(dy2, wdown_ref[...]) * (2.0 * ru)).astype(BF16)
        du_ref[...] = du
        dh2 = _dot_nt(du[:, 0:fs], wup_ref[0])
        for s in range(1, N_DEV):
            dh2 += _dot_nt(du[:, s * fs:(s + 1) * fs], wup_ref[s])
        r3 = _rms_r(x1v)
        _acc_rows(dg3_ref, dh2 * x1v * r3)
        dx1_ref[...] = dx2 + _rms_bwd(x1v, r3, g3_ref[...], dh2)

    row = _row_spec(tm, D_MODEL)
    frow = _row_spec(tm, D_FF)
    vec = _full_spec((1, D_MODEL))
    return pl.pallas_call(
        body, name="mlp_fwd_bwd", grid=(t // tm,),
        in_specs=[row, row, row, _full_spec((N_DEV, D_MODEL, fs)), _full_spec((D_FF, D_MODEL)), vec, vec],
        out_specs=[frow, frow, row, row, _full_spec((1, SLAB)), vec, vec],
        out_shape=[jax.ShapeDtypeStruct((t, D_FF), BF16), jax.ShapeDtypeStruct((t, D_FF), BF16),
                   jax.ShapeDtypeStruct((t, D_MODEL), BF16), jax.ShapeDtypeStruct((t, D_MODEL), F32),
                   jax.ShapeDtypeStruct((1, SLAB), F32), jax.ShapeDtypeStruct((1, D_MODEL), F32),
                   jax.ShapeDtypeStruct((1, D_MODEL), F32)],
        compiler_params=_params(("arbitrary",)),
    )(x1, h2, target, w_up, w_down, g3, g4)


def _inproj_bwd(dgates, dqa, dka, dva, dqb_t, dkb_t, dvb_t, cq, ckv, cqn, ckvn, x, dx1, rope_ct, rope_s1t, rope_s2t,
                g1, g_q, g_kv, w_in, w_qb, w_kvb):
    t = x.shape[0]
    tm = _token_tile(t)

    def body(dgates_ref, dqa_ref, dka_ref, dva_ref, dqt_ref, dkt_ref, dvt_ref, cq_ref, ckv_ref, cqn_ref, ckvn_ref,
             x_ref, dx1_ref, ct_ref, s1t_ref, s2t_ref, g1_ref, gq_ref, gkv_ref, win_ref, wqb_ref, wkvb_ref,
             dproj_ref, dx_ref, dg1_ref, dgq_ref, dgkv_ref, dwqb_ref, dwkvb_ref, dqbrt_ref, dkvbt_ref):
        @pl.when(pl.program_id(0) == 0)
        def _():
            dwqb_ref[...] = jnp.zeros_like(dwqb_ref)
            dwkvb_ref[...] = jnp.zeros_like(dwkvb_ref)

        ct, s1t, s2t = ct_ref[...], s1t_ref[...], s2t_ref[...]
        dk_sum_t = jnp.zeros((SLAB, tm), F32)
        for hd in range(N_HEADS):
            sl = slice(hd * SLAB, (hd + 1) * SLAB)
            dqbrt_ref[sl, :] = _rope_t_bwd(dqt_ref[sl, :] * SCALE_B, ct, s1t, s2t).astype(BF16)
            dk_sum_t += dkt_ref[sl, :].astype(F32)
        dkvbt_ref[0:HM, :] = dkt_ref[...]
        dkvbt_ref[HM:2 * HM, :] = dvt_ref[...]
        dkr = _rope_t_bwd(dk_sum_t, ct, s1t, s2t).T
        dwqb_ref[...] += _dot(dqbrt_ref[...], cqn_ref[...])
        dwkvb_ref[...] += _dot(dkvbt_ref[...], ckvn_ref[...])
        dcqn = _dot(wqb_ref[...], dqbrt_ref[...]).T
        cq = cq_ref[...]
        rq = _rms_r(cq)
        _acc_rows(dgq_ref, dcqn * cq * rq)
        dcq = _rms_bwd(cq, rq, gq_ref[...], dcqn)
        dckvn = _dot(wkvb_ref[...], dkvbt_ref[...]).T
        ckv = ckv_ref[...]
        rkv = _rms_r(ckv)
        _acc_rows(dgkv_ref, dckvn * ckv * rkv)
        dckv = _rms_bwd(ckv, rkv, gkv_ref[...], dckvn)
        dproj_ref[:, C_GATES:C_QA] = dgates_ref[...]
        dproj_ref[:, C_QA:C_KA] = dqa_ref[...]
        dproj_ref[:, C_KA:C_VA] = dka_ref[...]
        dproj_ref[:, C_VA:C_CQ] = dva_ref[...]
        dproj_ref[:, C_CQ:C_CKV] = dcq.astype(BF16)
        dproj_ref[:, C_CKV:C_KR] = dckv.astype(BF16)
        dproj_ref[:, C_KR:D_IN_PAD] = dkr.astype(BF16)
        dh = _dot_nt(dproj_ref[...], win_ref[...])
        xv = x_ref[...]
        r1 = _rms_r(xv)
        _acc_rows(dg1_ref, dh * xv * r1)
        dx_ref[...] = dx1_ref[...] + _rms_bwd(xv, r1, g1_ref[...], dh)

    kvw = N_KV_A * SLAB
    row = _row_spec(tm, D_MODEL)
    hm = _row_spec(tm, HM)
    hmt = _col_spec(HM, tm)
    tab = _col_spec(SLAB, tm)
    return pl.pallas_call(
        body, name="inproj_bwd", grid=(t // tm,),
        in_specs=[_row_spec(tm, 2 * D_MODEL), hm, _row_spec(tm, kvw), _row_spec(tm, kvw), hmt, hmt, hmt,
                  _row_spec(tm, Q_LORA), _row_spec(tm, KV_LORA), _row_spec(tm, Q_LORA), _row_spec(tm, KV_LORA),
                  row, row, tab, tab, tab,
                  _full_spec((1, D_MODEL)), _full_spec((1, Q_LORA)), _full_spec((1, KV_LORA)),
                  _full_spec((D_MODEL, D_IN_PAD)), _full_spec((Q_LORA, HM)), _full_spec((KV_LORA, 2 * HM))],
        out_specs=[_row_spec(tm, D_IN_PAD), row,
                   _full_spec((1, D_MODEL)), _full_spec((1, Q_LORA)), _full_spec((1, KV_LORA)),
                   _full_spec((HM, Q_LORA)), _full_spec((2 * HM, KV_LORA))],
        out_shape=[jax.ShapeDtypeStruct((t, D_IN_PAD), BF16), jax.ShapeDtypeStruct((t, D_MODEL), F32),
                   jax.ShapeDtypeStruct((1, D_MODEL), F32), jax.ShapeDtypeStruct((1, Q_LORA), F32),
                   jax.ShapeDtypeStruct((1, KV_LORA), F32),
                   jax.ShapeDtypeStruct((HM, Q_LORA), F32), jax.ShapeDtypeStruct((2 * HM, KV_LORA), F32)],
        scratch_shapes=[pltpu.VMEM((HM, tm), BF16), pltpu.VMEM((2 * HM, tm), BF16)],
        compiler_params=_params(("arbitrary",)),
    )(dgates, dqa, dka, dva, dqb_t, dkb_t, dvb_t, cq, ckv, cqn, ckvn, x, dx1, rope_ct, rope_s1t, rope_s2t,
      g1, g_q, g_kv, w_in, w_qb, w_kvb)


def _matmul_tn(a, b, name, out_dtype=F32, n_shards=1):
    t, k = a.shape
    n = b.shape[1]
    bt = min(t, 512)
    bn = min(n, 2048)
    bk = min(k, 2048 * 1024 // bn)
    ns = n // n_shards
    per_block = bn // ns
    steps = t // bt

    def body(a_ref, b_ref, o_ref, acc):
        s = pl.program_id(2)

        @pl.when(s == 0)
        def _():
            acc[...] = jnp.zeros_like(acc)

        acc[...] += _dot_tn(a_ref[...], b_ref[...])

        @pl.when(s == steps - 1)
        def _():
            if n_shards > 1:
                for p in range(per_block):
                    o_ref[p] = acc[:, p * ns:(p + 1) * ns].astype(out_dtype)
            else:
                o_ref[...] = acc[...].astype(out_dtype)

    if n_shards > 1:
        out_spec = pl.BlockSpec((per_block, bk, ns), lambda i, j, s: (j, i, 0))
        out_shape = jax.ShapeDtypeStruct((n_shards, k, ns), out_dtype)
    else:
        out_spec = pl.BlockSpec((bk, bn), lambda i, j, s: (i, j))
        out_shape = jax.ShapeDtypeStruct((k, n), out_dtype)
    return pl.pallas_call(
        body, name=name, grid=(k // bk, n // bn, steps),
        in_specs=[pl.BlockSpec((bt, bk), lambda i, j, s: (s, i)), pl.BlockSpec((bt, bn), lambda i, j, s: (s, j))],
        out_specs=out_spec, out_shape=out_shape, scratch_shapes=[pltpu.VMEM((bk, bn), F32)],
        compiler_params=_params(("parallel", "parallel", "arbitrary")),
    )(a, b)


def _two_level_gather(srcs, dsts, send_sems, recv_sems, local_sems):
    n = len(srcs)
    x, y, c = _mesh_pos()
    me, sibling = (x, y, c), (x, y, 1 - c)
    chips = [(1 - x, y), (x, 1 - y), (1 - x, 1 - y)]

    def slot(a, px, py, pc):
        return dsts[a].at[4 * px + 2 * py + pc]

    def copy(a, k, block, to, src=None):
        return pltpu.make_async_remote_copy(
            src_ref=slot(a, *block) if src is None else src, dst_ref=slot(a, *block),
            send_sem=send_sems.at[(N_DEV - 1) * a + k], recv_sem=recv_sems.at[(N_DEV - 1) * a + k],
            device_id=to, device_id_type=pl.DeviceIdType.MESH)

    def own_copies():
        mine = [pltpu.make_async_copy(srcs[a], slot(a, *me), local_sems.at[a]) for a in range(n)]
        first = []
        for a in range(n):
            first.append(copy(a, 0, me, sibling, src=srcs[a]))
            first += [copy(a, 1 + j, me, (*chip, c), src=srcs[a]) for j, chip in enumerate(chips)]
        return mine, first

    def start():
        mine, first = own_copies()
        for cp in mine + first:
            cp.start()

    def finish():
        mine, first = own_copies()
        passed = []
        for j, chip in enumerate(chips):
            for a in range(n):
                copy(a, 1 + j, (*chip, c), me).wait_recv()
                passed.append(copy(a, 4 + j, (*chip, c), sibling))
                passed[-1].start()
        for a in range(n):
            copy(a, 0, sibling, me).wait_recv()
        for j, chip in enumerate(chips):
            for a in range(n):
                copy(a, 4 + j, (*chip, 1 - c), me).wait_recv()
        for cp in first + passed:
            cp.wait_send()
        for cp in mine:
            cp.wait()

    return start, finish


def _exchange_grads(slices, small):
    n = len(slices)

    def body(*refs):
        srcs, s_ref = refs[:n], refs[n]
        dsts, s_dst = refs[n + 1:2 * n + 1], refs[2 * n + 1]
        sems = refs[2 * n + 2:]
        parts = _direct_copies(srcs, dsts, *sems, False)
        smalls = _direct_copies([s_ref], [s_dst], *sems, True, sem_base=n)
        _start_copies(*parts)
        _start_copies(*smalls)
        _wait_copies(*parts)
        _wait_copies(*smalls)

    outs = pl.pallas_call(
        body, name="exchange_grads",
        out_shape=[jax.ShapeDtypeStruct(a.shape, a.dtype) for a in slices]
        + [jax.ShapeDtypeStruct((N_DEV,) + small.shape, small.dtype)],
        in_specs=[ANY_SPEC] * (n + 1), out_specs=[ANY_SPEC] * (n + 1), scratch_shapes=_exchange_scratch(n + 1),
    )(*slices, small)
    return list(outs[:n]), outs[n]


def _adamw(parts, w, m, v, name):
    _, k, n = parts.shape
    bk = min(k, ADAM_ROWS)
    c1 = 1.0 - ADAM_B1 ** ADAM_STEP
    c2 = 1.0 - ADAM_B2 ** ADAM_STEP

    def body(p_ref, w_ref, m_ref, v_ref, g_ref, d_ref, mo_ref, vo_ref):
        g = p_ref[0].astype(F32)
        for s in range(1, N_DEV):
            g = g + p_ref[s].astype(F32)
        g_ref[0] = g
        m_new = ADAM_B1 * m_ref[0] + (1.0 - ADAM_B1) * g
        v_new = ADAM_B2 * v_ref[0] + (1.0 - ADAM_B2) * (g * g)
        mo_ref[0] = m_new
        vo_ref[0] = v_new
        m_hat = m_new / c1
        v_hat = v_new / c2
        d_ref[0] = -ADAM_LR * (m_hat / (jnp.sqrt(v_hat) + ADAM_EPS) + ADAM_WD * w_ref[0])

    blk = pl.BlockSpec((1, bk, n), lambda i: (0, i, 0))
    out = jax.ShapeDtypeStruct((1, k, n), F32)
    return pl.pallas_call(
        body, name=name, grid=(k // bk,),
        in_specs=[pl.BlockSpec((N_DEV, bk, n), lambda i: (0, i, 0)), blk, blk, blk],
        out_specs=[blk] * 4, out_shape=[out] * 4,
        compiler_params=_params(("parallel",)),
    )(parts, w, m, v)


def _pad_heads_cols(w, heads, width):
    k = w.shape[0]
    w = w.reshape(k, heads, width)
    return jnp.pad(w, ((0, 0), (0, 0), (0, SLAB - width))).reshape(k, heads * SLAB)


def _unpad_heads_cols(w, heads, width):
    k = w.shape[0]
    return w.reshape(k, heads, SLAB)[:, :, :width].reshape(k, heads * width)


def _pad_heads_rows(w, heads, width):
    n = w.shape[1]
    w = w.reshape(heads, width, n)
    return jnp.pad(w, ((0, 0), (0, SLAB - width), (0, 0))).reshape(heads * SLAB, n)


def _unpad_heads_rows(w, heads, width):
    n = w.shape[1]
    return w.reshape(heads, SLAB, n)[:, :width, :].reshape(heads * width, n)


def _pad_w_in(w_in):
    o = 2 * D_MODEL
    qa = _pad_heads_cols(w_in[:, o:o + 512], N_HEADS, HEAD_A)
    ka = _pad_heads_cols(w_in[:, o + 512:o + 640], N_KV_A, HEAD_A)
    va = _pad_heads_cols(w_in[:, o + 640:o + 768], N_KV_A, HEAD_A)
    kr = jnp.pad(w_in[:, o + 1152:o + 1184], ((0, 0), (QK_NOPE, SLAB - QK_NOPE - QK_ROPE)))
    return jnp.concatenate([w_in[:, :o], qa, ka, va, w_in[:, o + 768:o + 1152], kr], axis=1)


def _unpad_w_in(w):
    qa = _unpad_heads_cols(w[:, C_QA:C_KA], N_HEADS, HEAD_A)
    ka = _unpad_heads_cols(w[:, C_KA:C_VA], N_KV_A, HEAD_A)
    va = _unpad_heads_cols(w[:, C_VA:C_CQ], N_KV_A, HEAD_A)
    kr = w[:, C_KR + QK_NOPE:C_KR + QK_NOPE + QK_ROPE]
    return jnp.concatenate([w[:, :C_QA], qa, ka, va, w[:, C_CQ:C_KR], kr], axis=1)


def _pad_w_kvb(w_kvb):
    w = w_kvb.reshape(KV_LORA, N_HEADS, QK_NOPE + V_DIM_B)
    k = jnp.pad(w[:, :, :QK_NOPE], ((0, 0), (0, 0), (0, SLAB - QK_NOPE))).reshape(KV_LORA, HM)
    v = jnp.pad(w[:, :, QK_NOPE:], ((0, 0), (0, 0), (0, SLAB - V_DIM_B))).reshape(KV_LORA, HM)
    return jnp.concatenate([k, v], axis=1)


def _unpad_w_kvb(w):
    k = w[:, :HM].reshape(KV_LORA, N_HEADS, SLAB)[:, :, :QK_NOPE]
    v = w[:, HM:].reshape(KV_LORA, N_HEADS, SLAB)[:, :, :V_DIM_B]
    return jnp.concatenate([k, v], axis=2).reshape(KV_LORA, N_HEADS * (QK_NOPE + V_DIM_B))


def _col_shards(w):
    k, n = w.shape
    return w.reshape(k, N_DEV, n // N_DEV).transpose(1, 0, 2)


def _from_col_shards(s):
    _, k, ns = s.shape
    return s.transpose(1, 0, 2).reshape(k, N_DEV * ns)


def _freq_row():
    freqs = ROPE_THETA ** (-jnp.arange(0, QK_ROPE, 2, dtype=F32) / QK_ROPE)
    return jnp.concatenate([jnp.zeros((QK_NOPE,), F32), freqs, freqs,
                            jnp.zeros((SLAB - QK_NOPE - QK_ROPE,), F32)]).reshape(1, SLAB)


SMALL_D_ROWS = ("pre_norm_mix", "post_norm_mix", "pre_norm_mlp", "post_norm_mlp")
SMALL_Q_OFF, SMALL_KV_OFF, SMALL_SINK_OFF, SMALL_LOSS_OFF = 0, 256, 384, 392


def _pack_small(vals):
    row4 = jnp.concatenate([vals["q_a_norm"].reshape(-1), vals["kv_a_norm"].reshape(-1), vals["sinks"].reshape(-1),
                            vals["loss"].reshape(-1), jnp.zeros((1024 - 393,), F32)])
    rows = [vals[n].reshape(1024) for n in SMALL_D_ROWS] + [row4]
    return jnp.concatenate([jnp.stack(rows), jnp.zeros((SMALL_ROWS - 5, 1024), F32)], axis=0)


def _unpack_small(blk):
    out = {n: blk[i].reshape(1, 1024) for i, n in enumerate(SMALL_D_ROWS)}
    out["q_a_norm"] = blk[4, SMALL_Q_OFF:SMALL_Q_OFF + 256].reshape(1, 256)
    out["kv_a_norm"] = blk[4, SMALL_KV_OFF:SMALL_KV_OFF + 128].reshape(1, 128)
    out["sinks"] = blk[4, SMALL_SINK_OFF:SMALL_SINK_OFF + 8].reshape(1, 8)
    out["loss"] = blk[4, SMALL_LOSS_OFF]
    return out


WEIGHT_ORDER = ("pre_norm_mix", "w_in", "q_a_norm", "w_q_b", "kv_a_norm", "w_kv_b", "sinks", "w_o_a", "w_o_b",
                "w_out", "post_norm_mix", "pre_norm_mlp", "w_up", "w_down", "post_norm_mlp")
SMALL_NAMES = ("pre_norm_mix", "q_a_norm", "kv_a_norm", "sinks", "post_norm_mix", "pre_norm_mlp", "post_norm_mlp")


def kernel(x, positions, pre_norm_mix, w_in, q_a_norm, w_q_b, kv_a_norm, w_kv_b, sinks, w_o_a, w_o_b, w_out, post_norm_mix, pre_norm_mlp, w_up, w_down, post_norm_mlp, loss_target, m_pre_norm_mix, m_w_in, m_q_a_norm, m_w_q_b, m_kv_a_norm, m_w_kv_b, m_sinks, m_w_o_a, m_w_o_b, m_w_out, m_post_norm_mix, m_pre_norm_mlp, m_w_up, m_w_down, m_post_norm_mlp, v_pre_norm_mix, v_w_in, v_q_a_norm, v_w_q_b, v_kv_a_norm, v_w_kv_b, v_sinks, v_w_o_a, v_w_o_b, v_w_out, v_post_norm_mix, v_pre_norm_mlp, v_w_up, v_w_down, v_post_norm_mlp):
    weights = dict(pre_norm_mix=pre_norm_mix, w_in=w_in, q_a_norm=q_a_norm, w_q_b=w_q_b, kv_a_norm=kv_a_norm,
                   w_kv_b=w_kv_b, sinks=sinks, w_o_a=w_o_a, w_o_b=w_o_b, w_out=w_out, post_norm_mix=post_norm_mix,
                   pre_norm_mlp=pre_norm_mlp, w_up=w_up, w_down=w_down, post_norm_mlp=post_norm_mlp)
    m_in = dict(pre_norm_mix=m_pre_norm_mix, w_in=m_w_in, q_a_norm=m_q_a_norm, w_q_b=m_w_q_b, kv_a_norm=m_kv_a_norm,
                w_kv_b=m_w_kv_b, sinks=m_sinks, w_o_a=m_w_o_a, w_o_b=m_w_o_b, w_out=m_w_out,
                post_norm_mix=m_post_norm_mix, pre_norm_mlp=m_pre_norm_mlp, w_up=m_w_up, w_down=m_w_down,
                post_norm_mlp=m_post_norm_mlp)
    v_in = dict(pre_norm_mix=v_pre_norm_mix, w_in=v_w_in, q_a_norm=v_q_a_norm, w_q_b=v_w_q_b, kv_a_norm=v_kv_a_norm,
                w_kv_b=v_w_kv_b, sinks=v_sinks, w_o_a=v_w_o_a, w_o_b=v_w_o_b, w_out=v_w_out,
                post_norm_mix=v_post_norm_mix, pre_norm_mlp=v_pre_norm_mlp, w_up=v_w_up, w_down=v_w_down,
                post_norm_mlp=v_post_norm_mlp)

    xs, pos, target = x[0], positions[0], loss_target[0]
    t = xs.shape[0]
    pos_col = pos.reshape(t, 1)
    pos_row = pos.reshape(1, t)
    g1, g2, g3, g4 = (weights[n] for n in SMALL_D_ROWS)
    g_q, g_kv = q_a_norm, kv_a_norm
    sink_vec = sinks.reshape(N_HEADS)
    shard = {n: weights[n][0].astype(BF16) for n in EARLY + LATE}

    tables, (e_in, e_qb, e_kvb) = _rope_tables(pos_col, _freq_row(), [shard[n] for n in EARLY])
    w_in_p = _pad_w_in(_from_col_shards(e_in))
    w_qb = _pad_heads_cols(_from_col_shards(e_qb), N_HEADS, QK_NOPE + QK_ROPE)
    w_kvb = _pad_w_kvb(_from_col_shards(e_kvb))

    (h, gates, qa, ka, va, cq, ckv, cqn, ckvn, kb, vb, qt, kt, vt) = _inproj_fwd(
        xs, g1, w_in_p, g_q, g_kv, w_kvb, w_qb.T, w_kvb[:, :HM].T, w_kvb[:, HM:].T, w_in_p[:, C_KR:].T, tables)
    out_a, lse_a = _swa_fwd(qa, ka, va, pos_col, pos_row, sink_vec)
    out_b, out_b_t, lse_b, (l_oa, l_ob, l_out, w_up_s, l_down) = _mla_fwd(qt, kb, vt, [shard[n] for n in LATE])
    w_oa = _pad_heads_rows(_from_col_shards(l_oa), N_HEADS, HEAD_A)
    w_ob = _pad_heads_rows(_from_col_shards(l_ob), N_HEADS, V_DIM_B)
    w_out_f = l_out.reshape(D_MODEL, D_MODEL)
    w_down_f = l_down.reshape(D_FF, D_MODEL)

    oa_p, ob_p, merged, y, x1, h2 = _merge_fwd(out_a, out_b, gates, xs, w_oa, w_ob, w_out_f, g2, g3)
    a, du, dy2, dx1, loss, dg3, dg4 = _mlp_fwd_bwd(x1, h2, target, w_up_s, w_down_f, g3, g4)
    (dy, d_oap, d_obp, dgates, d_oa, d_ob_t, delta_b, dg2) = _merge_bwd(
        dx1, y, gates, oa_p, ob_p, out_b_t, w_oa, w_ob, w_out_f, g2)
    late_slices = [
        _col_shards(_unpad_heads_rows(_matmul_tn(out_a, d_oap, "dw_o_a"), N_HEADS, HEAD_A)).astype(BF16),
        _col_shards(_unpad_heads_rows(_matmul_tn(out_b, d_obp, "dw_o_b"), N_HEADS, V_DIM_B)).astype(BF16),
        _matmul_tn(merged, dy, "dw_out", BF16).reshape(N_DEV, D_MODEL // N_DEV, D_MODEL),
        _matmul_tn(h2, du, "dw_up", BF16, N_DEV),
        _matmul_tn(a, dy2, "dw_down", BF16).reshape(N_DEV, D_FF // N_DEV, D_MODEL),
    ]
    dqa, dka, dva, dsink = _swa_bwd(qa, ka, va, out_a, d_oa, lse_a, pos_col, pos_row, sink_vec)
    dqb_t, dkb_t, dvb_t, late_parts = _mla_bwd(qt, kb, kt, vb, d_ob_t, lse_b, delta_b, late_slices)
    dproj, dx, dg1, dgq, dgkv, dw_qb_t, dw_kvb_t = _inproj_bwd(
        dgates, dqa, dka, dva, dqb_t, dkb_t, dvb_t, cq, ckv, cqn, ckvn, xs, dx1, *tables[3:], g1, g_q, g_kv,
        w_in_p, w_qb, w_kvb)
    early_slices = [
        _col_shards(_unpad_w_in(_matmul_tn(h, dproj, "dw_in"))).astype(BF16),
        _col_shards(_unpad_heads_cols(dw_qb_t.T, N_HEADS, QK_NOPE + QK_ROPE)).astype(BF16),
        _col_shards(_unpad_w_kvb(dw_kvb_t.T)).astype(BF16),
    ]
    small_grads = {"pre_norm_mix": dg1, "post_norm_mix": dg2, "pre_norm_mlp": dg3, "post_norm_mlp": dg4,
                   "q_a_norm": dgq, "kv_a_norm": dgkv, "sinks": dsink.reshape(N_HEADS, BLOCK).sum(axis=1),
                   "loss": loss[0, 0:1]}
    early_parts, s_parts = _exchange_grads(early_slices, _pack_small(small_grads))

    updates = {}
    for name, parts in zip(EARLY + LATE, early_parts + late_parts):
        outs = _adamw(parts, weights[name], m_in[name], v_in[name], "adamw_" + name)
        for kind, arr in zip(("g", "d", "m", "v"), outs):
            updates[kind, name] = arr
    zero = jnp.zeros((), F32)
    pack = lambda src: _pack_small({**{n: src[n] for n in SMALL_NAMES}, "loss": zero})[None]
    smalls = _adamw(s_parts, pack(weights), pack(m_in), pack(v_in), "adamw_small")
    for kind, blk in zip(("g", "d", "m", "v"), smalls):
        for wname, piece in _unpack_small(blk[0]).items():
            updates[kind, wname] = piece
    results = [updates[kind, name] for kind in ("g", "d", "m", "v") for name in WEIGHT_ORDER]
    return (updates["g", "loss"], dx[None], *results)
```

```python
import functools

import numpy as np
import jax
import jax.numpy as jnp
from jax import lax
from jax.experimental import pallas as pl
from jax.experimental.pallas import tpu as pltpu

F32 = jnp.float32
BF16 = jnp.bfloat16

D_MODEL = 1024
D_FF = 4096
N_HEADS = 8
N_KV_A = 2
GROUP_A = N_HEADS // N_KV_A
HEAD_A = 64
QK_NOPE = 64
QK_ROPE = 32
V_DIM_B = 64
Q_LORA = 256
KV_LORA = 128
BLOCK = 128
SLAB = 128
ROPE_THETA = 10000.0
EPS = 1e-6
N_DEV = 8
NEG = -1e30

SCALE_A = HEAD_A ** -0.5
SCALE_B = (QK_NOPE + QK_ROPE) ** -0.5
LOG2E = 1.4426950408889634
SCORE_B = SCALE_B * LOG2E
MLA_HEADS_PER_STEP = 4
MLA_FWD_HEADS_PER_STEP = 8
ONES_ROWS = 16
SLOPES_A = tuple(2.0 ** (-8.0 * (h + 1) / N_HEADS) for h in range(N_HEADS))

ADAM_LR = 0.001
ADAM_B1 = 0.9
ADAM_B2 = 0.999
ADAM_EPS = 1e-08
ADAM_WD = 0.01
ADAM_STEP = 10

HM = N_HEADS * SLAB
C_GATES = 0
C_QA = 2 * D_MODEL
C_KA = C_QA + HM
C_VA = C_KA + N_KV_A * SLAB
C_CQ = C_VA + N_KV_A * SLAB
C_CKV = C_CQ + Q_LORA
C_KR = C_CKV + KV_LORA
D_IN_PAD = C_KR + SLAB

VMEM_LIMIT = 56 * 1024 * 1024

EARLY = ("w_in", "w_q_b", "w_kv_b")
LATE = ("w_o_a", "w_o_b", "w_out", "w_up", "w_down")
ADAM_ROWS = 256
SMALL_ROWS = 8


def _token_tile(t):
    return min(256, t)


def _attn_tile(t):
    return 512 if t >= 2048 else 128


def _params(sem, vmem=VMEM_LIMIT):
    return pltpu.CompilerParams(dimension_semantics=sem, vmem_limit_bytes=vmem)


def _dot(a, b):
    return jnp.dot(a, b, preferred_element_type=F32)


def _dot_nt(a, b):
    return lax.dot_general(a, b, (((1,), (1,)), ((), ())), preferred_element_type=F32)


def _dot_tn(a, b):
    return lax.dot_general(a, b, (((0,), (0,)), ((), ())), preferred_element_type=F32)


def _rms_r(x):
    return lax.rsqrt(jnp.mean(x * x, axis=-1, keepdims=True) + EPS)


def _rms_bwd(x, r, g, dy):
    t = dy * g
    return r * t - x * (r * r * r) * jnp.mean(x * t, axis=-1, keepdims=True)


def _sigmoid(x):
    return 1.0 / (1.0 + jnp.exp(-x))


def _rope(x, c, s1, s2):
    return x * c + pltpu.roll(x, SLAB - 16, 1) * s1 + pltpu.roll(x, 16, 1) * s2


def _rope_bwd(d, c, s1, s2):
    return d * c + pltpu.roll(d * s1, 16, 1) + pltpu.roll(d * s2, SLAB - 16, 1)


def _roll_rows(x, shift):
    return jnp.concatenate([x[-shift:], x[:-shift]], axis=0)


def _rope_t(x, c, s1, s2):
    return x * c + _roll_rows(x, SLAB - 16) * s1 + _roll_rows(x, 16) * s2


def _rope_t_bwd(d, c, s1, s2):
    return d * c + _roll_rows(d * s1, 16) + _roll_rows(d * s2, SLAB - 16)


def _row_spec(tm, n):
    return pl.BlockSpec((tm, n), lambda i: (i, 0))


def _col_spec(n, tm):
    return pl.BlockSpec((n, tm), lambda i: (0, i))


def _full_spec(shape):
    nd = len(shape)
    return pl.BlockSpec(shape, lambda i: (0,) * nd, pipeline_mode=pl.Buffered(1))


def _acc_rows(ref, val):
    @pl.when(pl.program_id(0) == 0)
    def _():
        ref[...] = jnp.zeros_like(ref)
    ref[...] += jnp.sum(val, axis=0, keepdims=True)


def _rope_tables(pos_col, freq_row, early):
    t = pos_col.shape[0]
    tm = _token_tile(t)
    n = len(early)

    def body(pos_ref, f_ref, *rest):
        shard_refs, (c_ref, s1_ref, s2_ref, ct_ref, s1t_ref, s2t_ref) = rest[:n], rest[n:n + 6]
        start, finish = _two_level_gather(shard_refs, rest[n + 6:2 * n + 6], *rest[2 * n + 6:])
        pl.when(pl.program_id(0) == 0)(start)
        ang = pos_ref[...].astype(F32) * f_ref[...]
        lane = lax.broadcasted_iota(jnp.int32, ang.shape, 1)
        s = jnp.sin(ang)
        c = jnp.cos(ang)
        s1 = jnp.where((lane >= 64) & (lane < 80), -s, 0.0)
        s2 = jnp.where((lane >= 80) & (lane < 96), s, 0.0)
        c_ref[...], s1_ref[...], s2_ref[...] = c, s1, s2
        ct_ref[...], s1t_ref[...], s2t_ref[...] = c.T, s1.T, s2.T
        pl.when(pl.program_id(0) == t // tm - 1)(finish)

    tab = jax.ShapeDtypeStruct((t, SLAB), F32)
    tabt = jax.ShapeDtypeStruct((SLAB, t), F32)
    outs = pl.pallas_call(
        body, name="rope_tables", grid=(t // tm,),
        in_specs=[_row_spec(tm, 1), _full_spec((1, SLAB))] + [ANY_SPEC] * n,
        out_specs=[_row_spec(tm, SLAB)] * 3 + [_col_spec(SLAB, tm)] * 3 + [ANY_SPEC] * n,
        out_shape=[tab] * 3 + [tabt] * 3 + [jax.ShapeDtypeStruct((N_DEV,) + a.shape, a.dtype) for a in early],
        scratch_shapes=_exchange_scratch(n),
        compiler_params=_params(("arbitrary",)),
    )(pos_col, freq_row, *early)
    return outs[:6], outs[6:]


def _inproj_fwd(x, g1, w_in, g_q, g_kv, w_kvb, w_qb_t, w_kb_t, w_vb_t, w_kr_t, tables):
    t = x.shape[0]
    tm = _token_tile(t)

    def body(x_ref, g1_ref, win_ref, gq_ref, gkv_ref, wkvb_ref, wqbt_ref, wkbt_ref, wvbt_ref, wkrt_ref,
             c_ref, s1_ref, s2_ref, ct_ref, s1t_ref, s2t_ref,
             h_ref, gates_ref, qa_ref, ka_ref, va_ref, cq_ref, ckv_ref, cqn_ref, ckvn_ref,
             kb_ref, vb_ref, qt_ref, kt_ref, vt_ref):
        xv = x_ref[...]
        h = (xv * _rms_r(xv) * g1_ref[...]).astype(BF16)
        h_ref[...] = h
        proj = _dot(h, win_ref[...])
        gates_ref[...] = proj[:, C_GATES:C_QA].astype(BF16)
        qa_ref[...] = proj[:, C_QA:C_KA].astype(BF16)
        ka_ref[...] = proj[:, C_KA:C_VA].astype(BF16)
        va_ref[...] = proj[:, C_VA:C_CQ].astype(BF16)
        cq = proj[:, C_CQ:C_CKV]
        ckv = proj[:, C_CKV:C_KR]
        kr = proj[:, C_KR:D_IN_PAD]
        cq_ref[...] = cq
        ckv_ref[...] = ckv
        cqn = (cq * _rms_r(cq) * gq_ref[...]).astype(BF16)
        ckvn = (ckv * _rms_r(ckv) * gkv_ref[...]).astype(BF16)
        cqn_ref[...] = cqn
        ckvn_ref[...] = ckvn
        c, s1, s2 = c_ref[...], s1_ref[...], s2_ref[...]
        kvb = _dot(ckvn, wkvb_ref[...])
        kr_rot = _rope(kr, c, s1, s2)
        ct, s1t, s2t = ct_ref[...], s1t_ref[...], s2t_ref[...]
        q_t = _dot_nt(wqbt_ref[...], cqn)
        k_t = _dot_nt(wkbt_ref[...], ckvn)
        kr_t = _rope_t(_dot_nt(wkrt_ref[...], h), ct, s1t, s2t)
        for hd in range(N_HEADS):
            sl = slice(hd * SLAB, (hd + 1) * SLAB)
            kb_ref[:, sl] = (kvb[:, sl] + kr_rot).astype(BF16)
            qt_ref[sl, :] = (_rope_t(q_t[sl, :], ct, s1t, s2t) * SCORE_B).astype(BF16)
            kt_ref[sl, :] = (k_t[sl, :] + kr_t).astype(BF16)
        vb_ref[...] = kvb[:, HM:2 * HM].astype(BF16)
        pad_row = lax.broadcasted_iota(jnp.int32, (HM, 1), 0) & (SLAB - 1)
        ones_rows = jnp.where((pad_row >= V_DIM_B) & (pad_row < V_DIM_B + ONES_ROWS), 1.0, 0.0)
        vt_ref[...] = (_dot_nt(wvbt_ref[...], ckvn) + ones_rows).astype(BF16)

    def sds(n, dt):
        return jax.ShapeDtypeStruct((t, n), dt)

    outs = [(D_MODEL, BF16), (2 * D_MODEL, BF16), (HM, BF16), (N_KV_A * SLAB, BF16), (N_KV_A * SLAB, BF16),
            (Q_LORA, F32), (KV_LORA, F32), (Q_LORA, BF16), (KV_LORA, BF16), (HM, BF16), (HM, BF16)]
    tab, tabt = _row_spec(tm, SLAB), _col_spec(SLAB, tm)
    return pl.pallas_call(
        body, name="inproj_fwd", grid=(t // tm,),
        in_specs=[_row_spec(tm, D_MODEL), _full_spec((1, D_MODEL)), _full_spec((D_MODEL, D_IN_PAD)),
                  _full_spec((1, Q_LORA)), _full_spec((1, KV_LORA)), _full_spec((KV_LORA, 2 * HM)),
                  _full_spec((HM, Q_LORA)), _full_spec((HM, KV_LORA)), _full_spec((HM, KV_LORA)),
                  _full_spec((SLAB, D_MODEL)), tab, tab, tab, tabt, tabt, tabt],
        out_specs=[_row_spec(tm, n) for n, _ in outs] + [_col_spec(HM, tm)] * 3,
        out_shape=[sds(n, dt) for n, dt in outs] + [jax.ShapeDtypeStruct((HM, t), BF16)] * 3,
        compiler_params=_params(("parallel",)),
    )(x, g1, w_in, g_q, g_kv, w_kvb, w_qb_t, w_kb_t, w_vb_t, w_kr_t, *tables)


def _tile_group(a):
    return jnp.concatenate([a] * GROUP_A, axis=1)


def _swa_masks():
    row = lax.broadcasted_iota(jnp.int32, (BLOCK, GROUP_A * BLOCK), 0)
    col = lax.broadcasted_iota(jnp.int32, (BLOCK, GROUP_A * BLOCK), 1) & (BLOCK - 1)
    return row <= col, row > col


def _heads_beside(ref, g):
    return jnp.concatenate([ref[:, (g * GROUP_A + hh) * SLAB:(g * GROUP_A + hh + 1) * SLAB].T
                            for hh in range(GROUP_A)], axis=1)


def _rows_beside(ref, g):
    return jnp.concatenate([ref[g * GROUP_A + hh] for hh in range(GROUP_A)], axis=1)


def _swa_rows(sinks):
    slopes = jnp.repeat(jnp.asarray(SLOPES_A, F32).reshape(N_KV_A, GROUP_A, 1), BLOCK, axis=2)
    sink_rows = jnp.repeat(sinks.reshape(N_KV_A, GROUP_A, 1), BLOCK, axis=2)
    return slopes.reshape(N_KV_A, 1, GROUP_A * BLOCK), sink_rows.reshape(N_KV_A, 1, GROUP_A * BLOCK)


def _swa_fwd(qa, ka, va, pos_col, pos_row, sinks):
    t = qa.shape[0]
    nb = t // BLOCK
    gw = GROUP_A * BLOCK
    slope_rows, sink_rows = _swa_rows(sinks)

    def body(q_ref, kc_ref, kp_ref, vc_ref, vp_ref, pkc_ref, pkp_ref, pq_ref, slope_ref, sink_ref, o_ref, l_ref):
        i = pl.program_id(0)
        pq = pq_ref[...]
        dist_c = _tile_group(jnp.abs(pkc_ref[...] - pq).astype(F32))
        dist_p = _tile_group(jnp.abs(pkp_ref[...] - pq).astype(F32))
        mask_c, older = _swa_masks()
        mask_p = jnp.logical_and(older, i > 0)
        for g in range(N_KV_A):
            gs = slice(g * SLAB, (g + 1) * SLAB)
            x = _heads_beside(q_ref, g)
            slope, sink = slope_ref[g], sink_ref[g]
            s_c = jnp.where(mask_c, _dot(kc_ref[:, gs], x) * SCALE_A - slope * dist_c, NEG)
            s_p = jnp.where(mask_p, _dot(kp_ref[:, gs], x) * SCALE_A - slope * dist_p, NEG)
            m = jnp.maximum(jnp.maximum(jnp.max(s_c, axis=0, keepdims=True),
                                        jnp.max(s_p, axis=0, keepdims=True)), sink)
            e_c = jnp.exp(s_c - m)
            e_p = jnp.exp(s_p - m)
            den = jnp.sum(e_c, axis=0, keepdims=True) + jnp.sum(e_p, axis=0, keepdims=True) + jnp.exp(sink - m)
            inv = 1.0 / den
            ot = (_dot_tn(vc_ref[:, gs], (e_c * inv).astype(BF16))
                  + _dot_tn(vp_ref[:, gs], (e_p * inv).astype(BF16)))
            lse = m + jnp.log(den)
            for hh in range(GROUP_A):
                hd = g * GROUP_A + hh
                seg = slice(hh * BLOCK, (hh + 1) * BLOCK)
                o_ref[:, hd * SLAB:(hd + 1) * SLAB] = ot[:, seg].T.astype(BF16)
                l_ref[hd] = lse[:, seg]

    cur = lambda i: (i, 0)
    prev = lambda i: (jnp.maximum(i - 1, 0), 0)
    kvw = N_KV_A * SLAB
    rows = pl.BlockSpec((N_KV_A, 1, gw), lambda i: (0, 0, 0))
    return pl.pallas_call(
        body, name="swa_fwd", grid=(nb,),
        in_specs=[pl.BlockSpec((BLOCK, HM), cur),
                  pl.BlockSpec((BLOCK, kvw), cur), pl.BlockSpec((BLOCK, kvw), prev),
                  pl.BlockSpec((BLOCK, kvw), cur), pl.BlockSpec((BLOCK, kvw), prev),
                  pl.BlockSpec((BLOCK, 1), cur), pl.BlockSpec((BLOCK, 1), prev),
                  pl.BlockSpec((1, BLOCK), lambda i: (0, i)), rows, rows],
        out_specs=[pl.BlockSpec((BLOCK, HM), cur), pl.BlockSpec((N_HEADS, 1, BLOCK), lambda i: (0, 0, i))],
        out_shape=[jax.ShapeDtypeStruct((t, HM), BF16), jax.ShapeDtypeStruct((N_HEADS, 1, t), F32)],
        compiler_params=_params(("parallel",)),
    )(qa, ka, ka, va, va, pos_col, pos_col, pos_row, slope_rows, sink_rows)


def _swa_bwd(qa, ka, va, out_a, d_oa, lse, pos_col, pos_row, sinks):
    t = qa.shape[0]
    nb = t // BLOCK
    gw = GROUP_A * BLOCK
    slope_rows, sink_rows = _swa_rows(sinks)

    def body(q_ref, qn_ref, do_ref, don_ref, l_ref, ln_ref, o_ref, on_ref, kp_ref, kc_ref, vp_ref, vc_ref,
             pkp_ref, pkc_ref, pq_ref, pqn_ref, slope_ref, sink_ref, dq_ref, dk_ref, dv_ref, dsink_ref):
        j = pl.program_id(0)
        pkc, pkp = pkc_ref[...], pkp_ref[...]
        dist_cc = _tile_group(jnp.abs(pkc - pq_ref[...]).astype(F32))
        dist_cp = _tile_group(jnp.abs(pkp - pq_ref[...]).astype(F32))
        dist_nc = _tile_group(jnp.abs(pkc - pqn_ref[...]).astype(F32))
        mask_cc, older = _swa_masks()
        mask_cp = jnp.logical_and(older, j > 0)
        mask_nc = jnp.logical_and(older, j < nb - 1)

        @pl.when(j == 0)
        def _():
            dsink_ref[...] = jnp.zeros_like(dsink_ref)

        def tile(k, v, x, dox, lrow, drow, dist, mask, slope):
            s = jnp.where(mask, _dot(k, x) * SCALE_A - slope * dist, NEG)
            p = jnp.exp(s - lrow)
            ds = p * (_dot(v, dox) - drow)
            return p.astype(BF16), ds.astype(BF16)

        for g in range(N_KV_A):
            gs = slice(g * SLAB, (g + 1) * SLAB)
            kc, kp, vc, vp = kc_ref[:, gs], kp_ref[:, gs], vc_ref[:, gs], vp_ref[:, gs]
            slope, sink = slope_ref[g], sink_ref[g]
            x, xn = _heads_beside(q_ref, g), _heads_beside(qn_ref, g)
            dox, doxn = _heads_beside(do_ref, g), _heads_beside(don_ref, g)
            lrow, lrown = _rows_beside(l_ref, g), _rows_beside(ln_ref, g)
            drow = jnp.sum(dox.astype(F32) * _heads_beside(o_ref, g).astype(F32), axis=0, keepdims=True)
            drown = jnp.sum(doxn.astype(F32) * _heads_beside(on_ref, g).astype(F32), axis=0, keepdims=True)
            p_cc, ds_cc = tile(kc, vc, x, dox, lrow, drow, dist_cc, mask_cc, slope)
            _, ds_cp = tile(kp, vp, x, dox, lrow, drow, dist_cp, mask_cp, slope)
            p_nc, ds_nc = tile(kc, vc, xn, doxn, lrown, drown, dist_nc, mask_nc, slope)
            dqt = (_dot_tn(kc, ds_cc) + _dot_tn(kp, ds_cp)) * SCALE_A
            for hh in range(GROUP_A):
                hd = g * GROUP_A + hh
                dq_ref[:, hd * SLAB:(hd + 1) * SLAB] = dqt[:, hh * BLOCK:(hh + 1) * BLOCK].T.astype(BF16)
            dk_ref[:, gs] = ((_dot_nt(ds_cc, x) + _dot_nt(ds_nc, xn)) * SCALE_A).astype(BF16)
            dv_ref[:, gs] = (_dot_nt(p_cc, dox) + _dot_nt(p_nc, doxn)).astype(BF16)
            dsink_ref[g] -= jnp.exp(sink - lrow) * drow

    cur = lambda j: (j, 0)
    prev = lambda j: (jnp.maximum(j - 1, 0), 0)
    nxt = lambda j: (jnp.minimum(j + 1, nb - 1), 0)
    cur3 = lambda j: (0, 0, j)
    nxt3 = lambda j: (0, 0, jnp.minimum(j + 1, nb - 1))
    kvw = N_KV_A * SLAB
    rows = pl.BlockSpec((N_KV_A, 1, gw), lambda j: (0, 0, 0))
    stat = lambda im: pl.BlockSpec((N_HEADS, 1, BLOCK), im)
    return pl.pallas_call(
        body, name="swa_bwd", grid=(nb,),
        in_specs=[pl.BlockSpec((BLOCK, HM), cur), pl.BlockSpec((BLOCK, HM), nxt),
                  pl.BlockSpec((BLOCK, HM), cur), pl.BlockSpec((BLOCK, HM), nxt),
                  stat(cur3), stat(nxt3), pl.BlockSpec((BLOCK, HM), cur), pl.BlockSpec((BLOCK, HM), nxt),
                  pl.BlockSpec((BLOCK, kvw), prev), pl.BlockSpec((BLOCK, kvw), cur),
                  pl.BlockSpec((BLOCK, kvw), prev), pl.BlockSpec((BLOCK, kvw), cur),
                  pl.BlockSpec((BLOCK, 1), prev), pl.BlockSpec((BLOCK, 1), cur),
                  pl.BlockSpec((1, BLOCK), lambda j: (0, j)),
                  pl.BlockSpec((1, BLOCK), lambda j: (0, jnp.minimum(j + 1, nb - 1))), rows, rows],
        out_specs=[pl.BlockSpec((BLOCK, HM), cur), pl.BlockSpec((BLOCK, kvw), cur),
                   pl.BlockSpec((BLOCK, kvw), cur), rows],
        out_shape=[jax.ShapeDtypeStruct((t, HM), BF16), jax.ShapeDtypeStruct((t, kvw), BF16),
                   jax.ShapeDtypeStruct((t, kvw), BF16), jax.ShapeDtypeStruct((N_KV_A, 1, gw), F32)],
        compiler_params=_params(("arbitrary",)),
    )(qa, qa, d_oa, d_oa, lse, lse, out_a, out_a, ka, ka, va, va,
      pos_col, pos_col, pos_row, pos_row, slope_rows, sink_rows)


def _mesh_pos():
    return lax.axis_index("x"), lax.axis_index("y"), lax.axis_index("c")


def _flip(v, bit):
    return 1 - v if bit else v


def _direct_copies(srcs, dsts, send_sems, recv_sems, local_sems, gather, sem_base=0):
    x, y, c = _mesh_pos()
    me = 4 * x + 2 * y + c
    local, remote = [], []
    for a, (src, dst) in enumerate(zip(srcs, dsts)):
        local.append(pltpu.make_async_copy(src if gather else src.at[me], dst.at[me], local_sems.at[sem_base + a]))
        for r in range(1, N_DEV):
            px, py, pc = _flip(x, r & 4), _flip(y, r & 2), _flip(c, r & 1)
            sem = (N_DEV - 1) * (sem_base + a) + r - 1
            remote.append(pltpu.make_async_remote_copy(
                src_ref=src if gather else src.at[4 * px + 2 * py + pc], dst_ref=dst.at[me],
                send_sem=send_sems.at[sem], recv_sem=recv_sems.at[sem],
                device_id=(px, py, pc), device_id_type=pl.DeviceIdType.MESH))
    return local, remote


def _start_copies(local, remote):
    for cp in local + remote:
        cp.start()


def _wait_copies(local, remote):
    for cp in remote:
        cp.wait_recv()
    for cp in remote:
        cp.wait_send()
    for cp in local:
        cp.wait()


def _exchange_scratch(n):
    return [pltpu.SemaphoreType.DMA((n * (N_DEV - 1),)), pltpu.SemaphoreType.DMA((n * (N_DEV - 1),)),
            pltpu.SemaphoreType.DMA((n,))]


ANY_SPEC = pl.BlockSpec(memory_space=pl.ANY)


def _mla_fwd(qt, kb, vt, late):
    t = kb.shape[0]
    tk = _attn_tile(t)
    ratio = 2 if t >= 2 * tk else 1
    tq = ratio * tk
    nq = t // tq
    hps = MLA_FWD_HEADS_PER_STEP
    w = hps * SLAB
    pairs = [(i, j) for i in range(nq) for j in range(ratio * (i + 1))]
    i_tab = jnp.asarray(np.array([p[0] for p in pairs], np.int32))
    j_tab = jnp.asarray(np.array([p[1] for p in pairs], np.int32))

    n_late = len(late)

    def body(it_ref, jt_ref, qt_ref, k_ref, vt_ref, *rest):
        late_refs, (o_ref, ot_ref, l_ref) = rest[:n_late], rest[n_late:n_late + 3]
        gathered_refs = rest[n_late + 3:2 * n_late + 3]
        m_s, acc_s, send_sems, recv_sems, local_sems = rest[2 * n_late + 3:]
        n = pl.program_id(1)
        i, j = it_ref[n], jt_ref[n]
        first_step = jnp.logical_and(pl.program_id(0) == 0, n == 0)
        last_step = jnp.logical_and(pl.program_id(0) == N_HEADS // hps - 1, n == len(pairs) - 1)

        @pl.when(first_step)
        def _():
            _start_copies(*_direct_copies(late_refs, gathered_refs, send_sems, recv_sems, local_sems, True))

        @pl.when(j == 0)
        def _():
            m_s[...] = jnp.full_like(m_s, NEG)
            acc_s[...] = jnp.zeros_like(acc_s)

        def update(masked, q0):
            qc = slice(q0, tq)

            def scores(hh):
                sl = slice(hh * SLAB, (hh + 1) * SLAB)
                return _dot(k_ref[:, sl], qt_ref[sl, qc])

            def softmax(hh, s):
                if masked:
                    s = jnp.where(lax.broadcasted_iota(jnp.int32, s.shape, 0)
                                  <= lax.broadcasted_iota(jnp.int32, s.shape, 1), s, NEG)
                m_old = m_s[hh][:, qc]
                m_new = jnp.maximum(m_old, jnp.max(s, axis=0, keepdims=True))
                m_s[hh, :, qc] = m_new
                return jnp.exp2(s - m_new).astype(BF16), jnp.exp2(m_old - m_new)

            def accumulate(hh, p, alpha):
                sl = slice(hh * SLAB, hh * SLAB + V_DIM_B + ONES_ROWS)
                acc_s[sl, qc] = alpha * acc_s[sl, qc] + _dot(vt_ref[sl, :], p)

            s_next, pending = scores(0), None
            for hh in range(hps):
                s = s_next
                if hh + 1 < hps:
                    s_next = scores(hh + 1)
                p, alpha = softmax(hh, s)
                if pending is not None:
                    accumulate(*pending)
                pending = (hh, p, alpha)
            accumulate(*pending)

        @pl.when(j < ratio * i)
        def _():
            update(False, 0)

        for part in range(ratio):
            @pl.when(j == ratio * i + part)
            def _():
                update(True, part * tk)

        @pl.when(j == ratio * i + ratio - 1)
        def _():
            for hh in range(hps):
                sl = slice(hh * SLAB, (hh + 1) * SLAB)
                den = acc_s[hh * SLAB + V_DIM_B:hh * SLAB + V_DIM_B + 1, :]
                values = lax.broadcasted_iota(jnp.int32, (SLAB, tq), 0) < V_DIM_B
                ot = jnp.where(values, acc_s[sl, :] / den, 0.0)
                ot_ref[sl, :] = ot.astype(BF16)
                o_ref[:, sl] = ot.T.astype(BF16)
                l_ref[hh] = m_s[hh] + jnp.log2(den)

        @pl.when(last_step)
        def _():
            _wait_copies(*_direct_copies(late_refs, gathered_refs, send_sems, recv_sems, local_sems, True))

    grid_spec = pltpu.PrefetchScalarGridSpec(
        num_scalar_prefetch=2, grid=(N_HEADS // hps, len(pairs)),
        in_specs=[pl.BlockSpec((w, tq), lambda h, n, it, jt: (h, it[n])),
                  pl.BlockSpec((tk, w), lambda h, n, it, jt: (jt[n], h)),
                  pl.BlockSpec((w, tk), lambda h, n, it, jt: (h, jt[n]))] + [ANY_SPEC] * n_late,
        out_specs=[pl.BlockSpec((tq, w), lambda h, n, it, jt: (it[n], h)),
                   pl.BlockSpec((w, tq), lambda h, n, it, jt: (h, it[n])),
                   pl.BlockSpec((hps, 1, tq), lambda h, n, it, jt: (h, 0, it[n]))] + [ANY_SPEC] * n_late,
        scratch_shapes=[pltpu.VMEM((hps, 1, tq), F32), pltpu.VMEM((w, tq), F32)] + _exchange_scratch(n_late))
    outs = pl.pallas_call(
        body, name="mla_fwd", grid_spec=grid_spec,
        out_shape=[jax.ShapeDtypeStruct((t, HM), BF16), jax.ShapeDtypeStruct((HM, t), BF16),
                   jax.ShapeDtypeStruct((N_HEADS, 1, t), F32)]
        + [jax.ShapeDtypeStruct((N_DEV,) + a.shape, a.dtype) for a in late],
        compiler_params=_params(("arbitrary", "arbitrary")),
    )(i_tab, j_tab, qt, kb, vt, *late)
    return outs[0], outs[1], outs[2], list(outs[3:])


def _mla_bwd(qt, kb, kt, vb, d_ob_t, lse, delta, grad_slices):
    t = kb.shape[0]
    tk = _attn_tile(t)
    ratio = 2 if t >= 2 * tk else 1
    tq = ratio * tk
    nk, nq = t // tk, t // tq
    hps = MLA_HEADS_PER_STEP
    w = hps * SLAB
    pairs = [(j, i) for j in range(nk) for i in range(j // ratio, nq)]
    j_tab = jnp.asarray(np.array([p[0] for p in pairs], np.int32))
    i_tab = jnp.asarray(np.array([p[1] for p in pairs], np.int32))

    n_ex = len(grad_slices)

    def body(jt_ref, it_ref, qt_ref, dot_ref, l_ref, dl_ref, k_ref, kt_ref, v_ref, *rest):
        slice_refs, (dqt_ref, dkt_ref, dvt_ref) = rest[:n_ex], rest[n_ex:n_ex + 3]
        part_refs = rest[n_ex + 3:2 * n_ex + 3]
        dk_s, dv_s, send_sems, recv_sems, local_sems = rest[2 * n_ex + 3:]
        n = pl.program_id(1)
        j, i = jt_ref[n], it_ref[n]
        first_step = jnp.logical_and(pl.program_id(0) == 0, n == 0)
        last_step = jnp.logical_and(pl.program_id(0) == N_HEADS // hps - 1, n == len(pairs) - 1)

        @pl.when(first_step)
        def _():
            _start_copies(*_direct_copies(slice_refs, part_refs, send_sems, recv_sems, local_sems, False))

        @pl.when(n == 0)
        def _():
            dqt_ref[...] = jnp.zeros_like(dqt_ref)

        def update(diagonal, q0):
            qc = slice(q0, tq)
            cols = pl.ds(pl.multiple_of(i * tq + q0, tk), tq - q0)

            def products(hh):
                sl = slice(hh * SLAB, (hh + 1) * SLAB)
                return _dot(k_ref[:, sl], qt_ref[sl, qc]), _dot(v_ref[:, sl], dot_ref[sl, qc])

            def softmax_bwd(hh, s, dp):
                if diagonal:
                    s = jnp.where(lax.broadcasted_iota(jnp.int32, s.shape, 0)
                                  <= lax.broadcasted_iota(jnp.int32, s.shape, 1), s, NEG)
                p = jnp.exp2(s - l_ref[hh][:, qc])
                return p.astype(BF16), (p * (dp - dl_ref[hh][:, qc])).astype(BF16)

            def gradients(hh, p, ds):
                base = hh * SLAB
                vrows = slice(base, base + V_DIM_B)
                qrows = slice(base, base + QK_NOPE + QK_ROPE)
                dv = _dot_nt(dot_ref[vrows, qc], p)
                dk = _dot_nt(qt_ref[qrows, qc], ds)
                if diagonal:
                    dv_s[base:base + SLAB, :] = jnp.concatenate([dv, jnp.zeros((SLAB - V_DIM_B, tk), F32)], axis=0)
                    dk_s[base:base + SLAB, :] = jnp.concatenate(
                        [dk, jnp.zeros((SLAB - QK_NOPE - QK_ROPE, tk), F32)], axis=0)
                else:
                    dv_s[vrows, :] += dv
                    dk_s[qrows, :] += dk
                dqt_ref[qrows, cols] += _dot(kt_ref[qrows, :], ds)

            for hh in range(hps):
                gradients(hh, *softmax_bwd(hh, *products(hh)))

        first_tile = lax.div(j, ratio)
        for part in range(ratio):
            @pl.when(jnp.logical_and(i == first_tile, lax.rem(j, ratio) == part))
            def _():
                update(True, part * tk)

        @pl.when(i > first_tile)
        def _():
            update(False, 0)

        @pl.when(i == nq - 1)
        def _():
            dkt_ref[...] = (dk_s[...] * (1.0 / LOG2E)).astype(BF16)
            dvt_ref[...] = dv_s[...].astype(BF16)

        @pl.when(last_step)
        def _():
            _wait_copies(*_direct_copies(slice_refs, part_refs, send_sems, recv_sems, local_sems, False))

    grid_spec = pltpu.PrefetchScalarGridSpec(
        num_scalar_prefetch=2, grid=(N_HEADS // hps, len(pairs)),
        in_specs=[pl.BlockSpec((w, tq), lambda h, n, jt, it: (h, it[n])),
                  pl.BlockSpec((w, tq), lambda h, n, jt, it: (h, it[n])),
                  pl.BlockSpec((hps, 1, tq), lambda h, n, jt, it: (h, 0, it[n])),
                  pl.BlockSpec((hps, 1, tq), lambda h, n, jt, it: (h, 0, it[n])),
                  pl.BlockSpec((tk, w), lambda h, n, jt, it: (jt[n], h)),
                  pl.BlockSpec((w, tk), lambda h, n, jt, it: (h, jt[n])),
                  pl.BlockSpec((tk, w), lambda h, n, jt, it: (jt[n], h))] + [ANY_SPEC] * n_ex,
        out_specs=[pl.BlockSpec((w, t), lambda h, n, jt, it: (h, 0)),
                   pl.BlockSpec((w, tk), lambda h, n, jt, it: (h, jt[n])),
                   pl.BlockSpec((w, tk), lambda h, n, jt, it: (h, jt[n]))] + [ANY_SPEC] * n_ex,
        scratch_shapes=[pltpu.VMEM((w, tk), F32), pltpu.VMEM((w, tk), F32)] + _exchange_scratch(n_ex))
    outs = pl.pallas_call(
        body, name="mla_bwd", grid_spec=grid_spec,
        out_shape=[jax.ShapeDtypeStruct((HM, t), F32), jax.ShapeDtypeStruct((HM, t), BF16),
                   jax.ShapeDtypeStruct((HM, t), BF16)]
        + [jax.ShapeDtypeStruct(a.shape, a.dtype) for a in grad_slices],
        compiler_params=_params(("arbitrary", "arbitrary")),
    )(j_tab, i_tab, qt, d_ob_t, lse, delta, kb, kt, vb, *grad_slices)
    return outs[0], outs[1], outs[2], list(outs[3:])


def _merge_fwd(out_a, out_b, gates, x, w_oa, w_ob, w_out, g2, g3):
    t = x.shape[0]
    tm = _token_tile(t)

    def body(oa_ref, ob_ref, gates_ref, x_ref, woa_ref, wob_ref, wout_ref, g2_ref, g3_ref,
             oap_ref, obp_ref, merged_ref, y_ref, x1_ref, h2_ref):
        oa_p = _dot(oa_ref[...], woa_ref[...])
        ob_p = _dot(ob_ref[...], wob_ref[...])
        oap_ref[...] = oa_p.astype(BF16)
        obp_ref[...] = ob_p.astype(BF16)
        sa = _sigmoid(gates_ref[:, 0:D_MODEL].astype(F32))
        sb = _sigmoid(gates_ref[:, D_MODEL:2 * D_MODEL].astype(F32))
        merged = (sa * oa_p + sb * ob_p).astype(BF16)
        merged_ref[...] = merged
        y = _dot(merged, wout_ref[...])
        y_ref[...] = y
        x1 = x_ref[...] + y * _rms_r(y) * g2_ref[...]
        x1_ref[...] = x1
        h2_ref[...] = (x1 * _rms_r(x1) * g3_ref[...]).astype(BF16)

    def sds(dt):
        return jax.ShapeDtypeStruct((t, D_MODEL), dt)

    row = _row_spec(tm, D_MODEL)
    return pl.pallas_call(
        body, name="merge_fwd", grid=(t // tm,),
        in_specs=[_row_spec(tm, HM), _row_spec(tm, HM), _row_spec(tm, 2 * D_MODEL), row,
                  _full_spec((HM, D_MODEL)), _full_spec((HM, D_MODEL)), _full_spec((D_MODEL, D_MODEL)),
                  _full_spec((1, D_MODEL)), _full_spec((1, D_MODEL))],
        out_specs=[row] * 6,
        out_shape=[sds(BF16), sds(BF16), sds(BF16), sds(F32), sds(F32), sds(BF16)],
        compiler_params=_params(("parallel",)),
    )(out_a, out_b, gates, x, w_oa, w_ob, w_out, g2, g3)


def _merge_bwd(dx1, y, gates, oa_p, ob_p, out_a, out_b, out_b_t, merged, w_oa, w_ob, w_out, g2):
    t = dx1.shape[0]
    tm = _token_tile(t)

    def body(dx1_ref, y_ref, gates_ref, oap_ref, obp_ref, oa_ref, ob_ref, obt_ref, merged_ref,
             woa_ref, wob_ref, wout_ref, g2_ref,
             dgates_ref, doa_ref, dobt_ref, dlb_ref, dg2_ref, dwoa_ref, dwob_ref, dwout_ref):
        @pl.when(pl.program_id(0) == 0)
        def _():
            dwoa_ref[...] = jnp.zeros_like(dwoa_ref)
            dwob_ref[...] = jnp.zeros_like(dwob_ref)
            dwout_ref[...] = jnp.zeros_like(dwout_ref)

        dx1v = dx1_ref[...]
        yv = y_ref[...]
        r2 = _rms_r(yv)
        _acc_rows(dg2_ref, dx1v * yv * r2)
        dy = _rms_bwd(yv, r2, g2_ref[...], dx1v).astype(BF16)
        dwout_ref[...] += _dot_tn(merged_ref[...], dy)
        dm = _dot_nt(dy, wout_ref[...])
        sa = _sigmoid(gates_ref[:, 0:D_MODEL].astype(F32))
        sb = _sigmoid(gates_ref[:, D_MODEL:2 * D_MODEL].astype(F32))
        d_oap = (dm * sa).astype(BF16)
        d_obp = (dm * sb).astype(BF16)
        dwoa_ref[...] += _dot_tn(oa_ref[...], d_oap)
        dwob_ref[...] += _dot_tn(ob_ref[...], d_obp)
        dgates_ref[:, 0:D_MODEL] = (dm * oap_ref[...].astype(F32) * sa * (1.0 - sa)).astype(BF16)
        dgates_ref[:, D_MODEL:2 * D_MODEL] = (dm * obp_ref[...].astype(F32) * sb * (1.0 - sb)).astype(BF16)
        doa_ref[...] = _dot_nt(d_oap, woa_ref[...]).astype(BF16)
        d_ob_t = _dot_nt(wob_ref[...], d_obp)
        dobt_ref[...] = d_ob_t.astype(BF16)
        for hd in range(N_HEADS):
            sl = slice(hd * SLAB, (hd + 1) * SLAB)
            dlb_ref[hd] = jnp.sum(d_ob_t[sl, :] * obt_ref[sl, :].astype(F32), axis=0, keepdims=True)

    def sds(n, dt):
        return jax.ShapeDtypeStruct((t, n), dt)

    row = _row_spec(tm, D_MODEL)
    head3 = pl.BlockSpec((N_HEADS, 1, tm), lambda i: (0, 0, i))
    return pl.pallas_call(
        body, name="merge_bwd", grid=(t // tm,),
        in_specs=[row, row, _row_spec(tm, 2 * D_MODEL), row, row, _row_spec(tm, HM), _row_spec(tm, HM),
                  _col_spec(HM, tm), row,
                  _full_spec((HM, D_MODEL)), _full_spec((HM, D_MODEL)), _full_spec((D_MODEL, D_MODEL)),
                  _full_spec((1, D_MODEL))],
        out_specs=[_row_spec(tm, 2 * D_MODEL), _row_spec(tm, HM), _col_spec(HM, tm),
                   head3, _full_spec((1, D_MODEL)),
                   _full_spec((HM, D_MODEL)), _full_spec((HM, D_MODEL)), _full_spec((D_MODEL, D_MODEL))],
        out_shape=[sds(2 * D_MODEL, BF16), sds(HM, BF16), jax.ShapeDtypeStruct((HM, t), BF16),
                   jax.ShapeDtypeStruct((N_HEADS, 1, t), F32), jax.ShapeDtypeStruct((1, D_MODEL), F32),
                   jax.ShapeDtypeStruct((HM, D_MODEL), F32), jax.ShapeDtypeStruct((HM, D_MODEL), F32),
                   jax.ShapeDtypeStruct((D_MODEL, D_MODEL), F32)],
        compiler_params=_params(("arbitrary",)),
    )(dx1, y, gates, oa_p, ob_p, out_a, out_b, out_b_t, merged, w_oa, w_ob, w_out, g2)


def _mlp_fwd_bwd(x1, h2, target, w_up, w_down, g3, g4):
    t = x1.shape[0]
    tm = _token_tile(t)
    fs = D_FF // N_DEV

    def body(x1_ref, h2_ref, tgt_ref, wup_ref, wdown_ref, g3_ref, g4_ref,
             a_ref, du_ref, dy2_ref, dx1_ref, loss_ref, dg3_ref, dg4_ref):
        x1v = x1_ref[...]
        h2v = h2_ref[...]
        u = jnp.concatenate([_dot(h2v, wup_ref[s]) for s in range(N_DEV)], axis=1)
        ru = jnp.maximum(u, 0.0)
        a = (ru * ru).astype(BF16)
        a_ref[...] = a
        y2 = _dot(a, wdown_ref[...])
        r4 = _rms_r(y2)
        diff = x1v + y2 * r4 * g4_ref[...] - tgt_ref[...]
        _acc_rows(loss_ref, jnp.sum(diff * diff, axis=-1, keepdims=True) * (0.5 / D_MODEL)
                  * jnp.ones((1, SLAB), F32))
        dx2 = diff * (1.0 / D_MODEL)
        _acc_rows(dg4_ref, dx2 * y2 * r4)
        dy2 = _rms_bwd(y2, r4, g4_ref[...], dx2).astype(BF16)
        dy2_ref[...] = dy2
        du = (_dot_nt(dy2, wdown_ref[...]) * (2.0 * ru)).astype(BF16)
        du_ref[...] = du
        dh2 = _dot_nt(du[:, 0:fs], wup_ref[0])
        for s in range(1, N_DEV):
            dh2 += _dot_nt(du[:, s * fs:(s + 1) * fs], wup_ref[s])
        r3 = _rms_r(x1v)
        _acc_rows(dg3_ref, dh2 * x1v * r3)
        dx1_ref[...] = dx2 + _rms_bwd(x1v, r3, g3_ref[...], dh2)

    row = _row_spec(tm, D_MODEL)
    frow = _row_spec(tm, D_FF)
    vec = _full_spec((1, D_MODEL))
    return pl.pallas_call(
        body, name="mlp_fwd_bwd", grid=(t // tm,),
        in_specs=[row, row, row, _full_spec((N_DEV, D_MODEL, fs)), _full_spec((D_FF, D_MODEL)), vec, vec],
        out_specs=[frow, frow, row, row, _full_spec((1, SLAB)), vec, vec],
        out_shape=[jax.ShapeDtypeStruct((t, D_FF), BF16), jax.ShapeDtypeStruct((t, D_FF), BF16),
                   jax.ShapeDtypeStruct((t, D_MODEL), BF16), jax.ShapeDtypeStruct((t, D_MODEL), F32),
                   jax.ShapeDtypeStruct((1, SLAB), F32), jax.ShapeDtypeStruct((1, D_MODEL), F32),
                   jax.ShapeDtypeStruct((1, D_MODEL), F32)],
        compiler_params=_params(("arbitrary",)),
    )(x1, h2, target, w_up, w_down, g3, g4)


def _inproj_bwd(dgates, dqa, dka, dva, dqb_t, dkb_t, dvb_t, cq, ckv, cqn, ckvn, x, dx1, rope_ct, rope_s1t, rope_s2t,
                g1, g_q, g_kv, w_in, w_qb, w_kvb):
    t = x.shape[0]
    tm = _token_tile(t)

    def body(dgates_ref, dqa_ref, dka_ref, dva_ref, dqt_ref, dkt_ref, dvt_ref, cq_ref, ckv_ref, cqn_ref, ckvn_ref,
             x_ref, dx1_ref, ct_ref, s1t_ref, s2t_ref, g1_ref, gq_ref, gkv_ref, win_ref, wqb_ref, wkvb_ref,
             dproj_ref, dx_ref, dg1_ref, dgq_ref, dgkv_ref, dwqb_ref, dwkvb_ref, dqbrt_ref, dkvbt_ref):
        @pl.when(pl.program_id(0) == 0)
        def _():
            dwqb_ref[...] = jnp.zeros_like(dwqb_ref)
            dwkvb_ref[...] = jnp.zeros_like(dwkvb_ref)

        ct, s1t, s2t = ct_ref[...], s1t_ref[...], s2t_ref[...]
        dk_sum_t = jnp.zeros((SLAB, tm), F32)
        for hd in range(N_HEADS):
            sl = slice(hd * SLAB, (hd + 1) * SLAB)
            dqbrt_ref[sl, :] = _rope_t_bwd(dqt_ref[sl, :] * SCALE_B, ct, s1t, s2t).astype(BF16)
            dk_sum_t += dkt_ref[sl, :].astype(F32)
        dkvbt_ref[0:HM, :] = dkt_ref[...]
        dkvbt_ref[HM:2 * HM, :] = dvt_ref[...]
        dkr = _rope_t_bwd(dk_sum_t, ct, s1t, s2t).T
        dwqb_ref[...] += _dot(dqbrt_ref[...], cqn_ref[...])
        dwkvb_ref[...] += _dot(dkvbt_ref[...], ckvn_ref[...])
        dcqn = _dot(wqb_ref[...], dqbrt_ref[...]).T
        cq = cq_ref[...]
        rq = _rms_r(cq)
        _acc_rows(dgq_ref, dcqn * cq * rq)
        dcq = _rms_bwd(cq, rq, gq_ref[...], dcqn)
        dckvn = _dot(wkvb_ref[...], dkvbt_ref[...]).T
        ckv = ckv_ref[...]
        rkv = _rms_r(ckv)
        _acc_rows(dgkv_ref, dckvn * ckv * rkv)
        dckv = _rms_bwd(ckv, rkv, gkv_ref[...], dckvn)
        dproj_ref[:, C_GATES:C_QA] = dgates_ref[...]
        dproj_ref[:, C_QA:C_KA] = dqa_ref[...]
        dproj_ref[:, C_KA:C_VA] = dka_ref[...]
        dproj_ref[:, C_VA:C_CQ] = dva_ref[...]
        dproj_ref[:, C_CQ:C_CKV] = dcq.astype(BF16)
        dproj_ref[:, C_CKV:C_KR] = dckv.astype(BF16)
        dproj_ref[:, C_KR:D_IN_PAD] = dkr.astype(BF16)
        dh = _dot_nt(dproj_ref[...], win_ref[...])
        xv = x_ref[...]
        r1 = _rms_r(xv)
        _acc_rows(dg1_ref, dh * xv * r1)
        dx_ref[...] = dx1_ref[...] + _rms_bwd(xv, r1, g1_ref[...], dh)

    kvw = N_KV_A * SLAB
    row = _row_spec(tm, D_MODEL)
    hm = _row_spec(tm, HM)
    hmt = _col_spec(HM, tm)
    tab = _col_spec(SLAB, tm)
    return pl.pallas_call(
        body, name="inproj_bwd", grid=(t // tm,),
        in_specs=[_row_spec(tm, 2 * D_MODEL), hm, _row_spec(tm, kvw), _row_spec(tm, kvw), hmt, hmt, hmt,
                  _row_spec(tm, Q_LORA), _row_spec(tm, KV_LORA), _row_spec(tm, Q_LORA), _row_spec(tm, KV_LORA),
                  row, row, tab, tab, tab,
                  _full_spec((1, D_MODEL)), _full_spec((1, Q_LORA)), _full_spec((1, KV_LORA)),
                  _full_spec((D_MODEL, D_IN_PAD)), _full_spec((Q_LORA, HM)), _full_spec((KV_LORA, 2 * HM))],
        out_specs=[_row_spec(tm, D_IN_PAD), row,
                   _full_spec((1, D_MODEL)), _full_spec((1, Q_LORA)), _full_spec((1, KV_LORA)),
                   _full_spec((HM, Q_LORA)), _full_spec((2 * HM, KV_LORA))],
        out_shape=[jax.ShapeDtypeStruct((t, D_IN_PAD), BF16), jax.ShapeDtypeStruct((t, D_MODEL), F32),
                   jax.ShapeDtypeStruct((1, D_MODEL), F32), jax.ShapeDtypeStruct((1, Q_LORA), F32),
                   jax.ShapeDtypeStruct((1, KV_LORA), F32),
                   jax.ShapeDtypeStruct((HM, Q_LORA), F32), jax.ShapeDtypeStruct((2 * HM, KV_LORA), F32)],
        scratch_shapes=[pltpu.VMEM((HM, tm), BF16), pltpu.VMEM((2 * HM, tm), BF16)],
        compiler_params=_params(("arbitrary",)),
    )(dgates, dqa, dka, dva, dqb_t, dkb_t, dvb_t, cq, ckv, cqn, ckvn, x, dx1, rope_ct, rope_s1t, rope_s2t,
      g1, g_q, g_kv, w_in, w_qb, w_kvb)


def _matmul_tn(a, b, name, out_dtype=F32, n_shards=1):
    t, k = a.shape
    n = b.shape[1]
    bt = min(t, 512)
    bn = min(n, 2048)
    bk = min(k, 2048 * 1024 // bn)
    ns = n // n_shards
    per_block = bn // ns
    steps = t // bt

    def body(a_ref, b_ref, o_ref, acc):
        s = pl.program_id(2)

        @pl.when(s == 0)
        def _():
            acc[...] = jnp.zeros_like(acc)

        acc[...] += _dot_tn(a_ref[...], b_ref[...])

        @pl.when(s == steps - 1)
        def _():
            if n_shards > 1:
                for p in range(per_block):
                    o_ref[p] = acc[:, p * ns:(p + 1) * ns].astype(out_dtype)
            else:
                o_ref[...] = acc[...].astype(out_dtype)

    if n_shards > 1:
        out_spec = pl.BlockSpec((per_block, bk, ns), lambda i, j, s: (j, i, 0))
        out_shape = jax.ShapeDtypeStruct((n_shards, k, ns), out_dtype)
    else:
        out_spec = pl.BlockSpec((bk, bn), lambda i, j, s: (i, j))
        out_shape = jax.ShapeDtypeStruct((k, n), out_dtype)
    return pl.pallas_call(
        body, name=name, grid=(k // bk, n // bn, steps),
        in_specs=[pl.BlockSpec((bt, bk), lambda i, j, s: (s, i)), pl.BlockSpec((bt, bn), lambda i, j, s: (s, j))],
        out_specs=out_spec, out_shape=out_shape, scratch_shapes=[pltpu.VMEM((bk, bn), F32)],
        compiler_params=_params(("parallel", "parallel", "arbitrary")),
    )(a, b)


def _two_level_gather(srcs, dsts, send_sems, recv_sems, local_sems):
    n = len(srcs)
    x, y, c = _mesh_pos()
    me, sibling = (x, y, c), (x, y, 1 - c)
    chips = [(1 - x, y), (x, 1 - y), (1 - x, 1 - y)]

    def slot(a, px, py, pc):
        return dsts[a].at[4 * px + 2 * py + pc]

    def copy(a, k, block, to, src=None):
        return pltpu.make_async_remote_copy(
            src_ref=slot(a, *block) if src is None else src, dst_ref=slot(a, *block),
            send_sem=send_sems.at[(N_DEV - 1) * a + k], recv_sem=recv_sems.at[(N_DEV - 1) * a + k],
            device_id=to, device_id_type=pl.DeviceIdType.MESH)

    def own_copies():
        mine = [pltpu.make_async_copy(srcs[a], slot(a, *me), local_sems.at[a]) for a in range(n)]
        first = []
        for a in range(n):
            first.append(copy(a, 0, me, sibling, src=srcs[a]))
            first += [copy(a, 1 + j, me, (*chip, c), src=srcs[a]) for j, chip in enumerate(chips)]
        return mine, first

    def start():
        mine, first = own_copies()
        for cp in mine + first:
            cp.start()

    def finish():
        mine, first = own_copies()
        passed = []
        for j, chip in enumerate(chips):
            for a in range(n):
                copy(a, 1 + j, (*chip, c), me).wait_recv()
                passed.append(copy(a, 4 + j, (*chip, c), sibling))
                passed[-1].start()
        for a in range(n):
            copy(a, 0, sibling, me).wait_recv()
        for j, chip in enumerate(chips):
            for a in range(n):
                copy(a, 4 + j, (*chip, 1 - c), me).wait_recv()
        for cp in first + passed:
            cp.wait_send()
        for cp in mine:
            cp.wait()

    return start, finish


def _exchange_grads(slices, small):
    n = len(slices)

    def body(*refs):
        srcs, s_ref = refs[:n], refs[n]
        dsts, s_dst = refs[n + 1:2 * n + 1], refs[2 * n + 1]
        sems = refs[2 * n + 2:]
        parts = _direct_copies(srcs, dsts, *sems, False)
        smalls = _direct_copies([s_ref], [s_dst], *sems, True, sem_base=n)
        _start_copies(*parts)
        _start_copies(*smalls)
        _wait_copies(*parts)
        _wait_copies(*smalls)

    outs = pl.pallas_call(
        body, name="exchange_grads",
        out_shape=[jax.ShapeDtypeStruct(a.shape, a.dtype) for a in slices]
        + [jax.ShapeDtypeStruct((N_DEV,) + small.shape, small.dtype)],
        in_specs=[ANY_SPEC] * (n + 1), out_specs=[ANY_SPEC] * (n + 1), scratch_shapes=_exchange_scratch(n + 1),
    )(*slices, small)
    return list(outs[:n]), outs[n]


def _adamw(parts, w, m, v, name):
    _, k, n = parts.shape
    bk = min(k, ADAM_ROWS)
    c1 = 1.0 - ADAM_B1 ** ADAM_STEP
    c2 = 1.0 - ADAM_B2 ** ADAM_STEP

    def body(p_ref, w_ref, m_ref, v_ref, g_ref, d_ref, mo_ref, vo_ref):
        g = p_ref[0].astype(F32)
        for s in range(1, N_DEV):
            g = g + p_ref[s].astype(F32)
        g_ref[0] = g
        m_new = ADAM_B1 * m_ref[0] + (1.0 - ADAM_B1) * g
        v_new = ADAM_B2 * v_ref[0] + (1.0 - ADAM_B2) * (g * g)
        mo_ref[0] = m_new
        vo_ref[0] = v_new
        m_hat = m_new / c1
        v_hat = v_new / c2
        d_ref[0] = -ADAM_LR * (m_hat / (jnp.sqrt(v_hat) + ADAM_EPS) + ADAM_WD * w_ref[0])

    blk = pl.BlockSpec((1, bk, n), lambda i: (0, i, 0))
    out = jax.ShapeDtypeStruct((1, k, n), F32)
    return pl.pallas_call(
        body, name=name, grid=(k // bk,),
        in_specs=[pl.BlockSpec((N_DEV, bk, n), lambda i: (0, i, 0)), blk, blk, blk],
        out_specs=[blk] * 4, out_shape=[out] * 4,
        compiler_params=_params(("parallel",)),
    )(parts, w, m, v)


def _pad_heads_cols(w, heads, width):
    k = w.shape[0]
    w = w.reshape(k, heads, width)
    return jnp.pad(w, ((0, 0), (0, 0), (0, SLAB - width))).reshape(k, heads * SLAB)


def _unpad_heads_cols(w, heads, width):
    k = w.shape[0]
    return w.reshape(k, heads, SLAB)[:, :, :width].reshape(k, heads * width)


def _pad_heads_rows(w, heads, width):
    n = w.shape[1]
    w = w.reshape(heads, width, n)
    return jnp.pad(w, ((0, 0), (0, SLAB - width), (0, 0))).reshape(heads * SLAB, n)


def _unpad_heads_rows(w, heads, width):
    n = w.shape[1]
    return w.reshape(heads, SLAB, n)[:, :width, :].reshape(heads * width, n)


def _pad_w_in(w_in):
    o = 2 * D_MODEL
    qa = _pad_heads_cols(w_in[:, o:o + 512], N_HEADS, HEAD_A)
    ka = _pad_heads_cols(w_in[:, o + 512:o + 640], N_KV_A, HEAD_A)
    va = _pad_heads_cols(w_in[:, o + 640:o + 768], N_KV_A, HEAD_A)
    kr = jnp.pad(w_in[:, o + 1152:o + 1184], ((0, 0), (QK_NOPE, SLAB - QK_NOPE - QK_ROPE)))
    return jnp.concatenate([w_in[:, :o], qa, ka, va, w_in[:, o + 768:o + 1152], kr], axis=1)


def _unpad_w_in(w):
    qa = _unpad_heads_cols(w[:, C_QA:C_KA], N_HEADS, HEAD_A)
    ka = _unpad_heads_cols(w[:, C_KA:C_VA], N_KV_A, HEAD_A)
    va = _unpad_heads_cols(w[:, C_VA:C_CQ], N_KV_A, HEAD_A)
    kr = w[:, C_KR + QK_NOPE:C_KR + QK_NOPE + QK_ROPE]
    return jnp.concatenate([w[:, :C_QA], qa, ka, va, w[:, C_CQ:C_KR], kr], axis=1)


def _pad_w_kvb(w_kvb):
    w = w_kvb.reshape(KV_LORA, N_HEADS, QK_NOPE + V_DIM_B)
    k = jnp.pad(w[:, :, :QK_NOPE], ((0, 0), (0, 0), (0, SLAB - QK_NOPE))).reshape(KV_LORA, HM)
    v = jnp.pad(w[:, :, QK_NOPE:], ((0, 0), (0, 0), (0, SLAB - V_DIM_B))).reshape(KV_LORA, HM)
    return jnp.concatenate([k, v], axis=1)


def _unpad_w_kvb(w):
    k = w[:, :HM].reshape(KV_LORA, N_HEADS, SLAB)[:, :, :QK_NOPE]
    v = w[:, HM:].reshape(KV_LORA, N_HEADS, SLAB)[:, :, :V_DIM_B]
    return jnp.concatenate([k, v], axis=2).reshape(KV_LORA, N_HEADS * (QK_NOPE + V_DIM_B))


def _col_shards(w):
    k, n = w.shape
    return w.reshape(k, N_DEV, n // N_DEV).transpose(1, 0, 2)


def _from_col_shards(s):
    _, k, ns = s.shape
    return s.transpose(1, 0, 2).reshape(k, N_DEV * ns)


def _freq_row():
    freqs = ROPE_THETA ** (-jnp.arange(0, QK_ROPE, 2, dtype=F32) / QK_ROPE)
    return jnp.concatenate([jnp.zeros((QK_NOPE,), F32), freqs, freqs,
                            jnp.zeros((SLAB - QK_NOPE - QK_ROPE,), F32)]).reshape(1, SLAB)


SMALL_D_ROWS = ("pre_norm_mix", "post_norm_mix", "pre_norm_mlp", "post_norm_mlp")
SMALL_Q_OFF, SMALL_KV_OFF, SMALL_SINK_OFF, SMALL_LOSS_OFF = 0, 256, 384, 392


def _pack_small(vals):
    row4 = jnp.concatenate([vals["q_a_norm"].reshape(-1), vals["kv_a_norm"].reshape(-1), vals["sinks"].reshape(-1),
                            vals["loss"].reshape(-1), jnp.zeros((1024 - 393,), F32)])
    rows = [vals[n].reshape(1024) for n in SMALL_D_ROWS] + [row4]
    return jnp.concatenate([jnp.stack(rows), jnp.zeros((SMALL_ROWS - 5, 1024), F32)], axis=0)


def _unpack_small(blk):
    out = {n: blk[i].reshape(1, 1024) for i, n in enumerate(SMALL_D_ROWS)}
    out["q_a_norm"] = blk[4, SMALL_Q_OFF:SMALL_Q_OFF + 256].reshape(1, 256)
    out["kv_a_norm"] = blk[4, SMALL_KV_OFF:SMALL_KV_OFF + 128].reshape(1, 128)
    out["sinks"] = blk[4, SMALL_SINK_OFF:SMALL_SINK_OFF + 8].reshape(1, 8)
    out["loss"] = blk[4, SMALL_LOSS_OFF]
    return out


WEIGHT_ORDER = ("pre_norm_mix", "w_in", "q_a_norm", "w_q_b", "kv_a_norm", "w_kv_b", "sinks", "w_o_a", "w_o_b",
                "w_out", "post_norm_mix", "pre_norm_mlp", "w_up", "w_down", "post_norm_mlp")
SMALL_NAMES = ("pre_norm_mix", "q_a_norm", "kv_a_norm", "sinks", "post_norm_mix", "pre_norm_mlp", "post_norm_mlp")


def kernel(x, positions, pre_norm_mix, w_in, q_a_norm, w_q_b, kv_a_norm, w_kv_b, sinks, w_o_a, w_o_b, w_out, post_norm_mix, pre_norm_mlp, w_up, w_down, post_norm_mlp, loss_target, m_pre_norm_mix, m_w_in, m_q_a_norm, m_w_q_b, m_kv_a_norm, m_w_kv_b, m_sinks, m_w_o_a, m_w_o_b, m_w_out, m_post_norm_mix, m_pre_norm_mlp, m_w_up, m_w_down, m_post_norm_mlp, v_pre_norm_mix, v_w_in, v_q_a_norm, v_w_q_b, v_kv_a_norm, v_w_kv_b, v_sinks, v_w_o_a, v_w_o_b, v_w_out, v_post_norm_mix, v_pre_norm_mlp, v_w_up, v_w_down, v_post_norm_mlp):
    weights = dict(pre_norm_mix=pre_norm_mix, w_in=w_in, q_a_norm=q_a_norm, w_q_b=w_q_b, kv_a_norm=kv_a_norm,
                   w_kv_b=w_kv_b, sinks=sinks, w_o_a=w_o_a, w_o_b=w_o_b, w_out=w_out, post_norm_mix=post_norm_mix,
                   pre_norm_mlp=pre_norm_mlp, w_up=w_up, w_down=w_down, post_norm_mlp=post_norm_mlp)
    m_in = dict(pre_norm_mix=m_pre_norm_mix, w_in=m_w_in, q_a_norm=m_q_a_norm, w_q_b=m_w_q_b, kv_a_norm=m_kv_a_norm,
                w_kv_b=m_w_kv_b, sinks=m_sinks, w_o_a=m_w_o_a, w_o_b=m_w_o_b, w_out=m_w_out,
                post_norm_mix=m_post_norm_mix, pre_norm_mlp=m_pre_norm_mlp, w_up=m_w_up, w_down=m_w_down,
                post_norm_mlp=m_post_norm_mlp)
    v_in = dict(pre_norm_mix=v_pre_norm_mix, w_in=v_w_in, q_a_norm=v_q_a_norm, w_q_b=v_w_q_b, kv_a_norm=v_kv_a_norm,
                w_kv_b=v_w_kv_b, sinks=v_sinks, w_o_a=v_w_o_a, w_o_b=v_w_o_b, w_out=v_w_out,
                post_norm_mix=v_post_norm_mix, pre_norm_mlp=v_pre_norm_mlp, w_up=v_w_up, w_down=v_w_down,
                post_norm_mlp=v_post_norm_mlp)

    xs, pos, target = x[0], positions[0], loss_target[0]
    t = xs.shape[0]
    pos_col = pos.reshape(t, 1)
    pos_row = pos.reshape(1, t)
    g1, g2, g3, g4 = (weights[n] for n in SMALL_D_ROWS)
    g_q, g_kv = q_a_norm, kv_a_norm
    sink_vec = sinks.reshape(N_HEADS)
    shard = {n: weights[n][0].astype(BF16) for n in EARLY + LATE}

    tables, (e_in, e_qb, e_kvb) = _rope_tables(pos_col, _freq_row(), [shard[n] for n in EARLY])
    w_in_p = _pad_w_in(_from_col_shards(e_in))
    w_qb = _pad_heads_cols(_from_col_shards(e_qb), N_HEADS, QK_NOPE + QK_ROPE)
    w_kvb = _pad_w_kvb(_from_col_shards(e_kvb))

    (h, gates, qa, ka, va, cq, ckv, cqn, ckvn, kb, vb, qt, kt, vt) = _inproj_fwd(
        xs, g1, w_in_p, g_q, g_kv, w_kvb, w_qb.T, w_kvb[:, :HM].T, w_kvb[:, HM:].T, w_in_p[:, C_KR:].T, tables)
    out_a, lse_a = _swa_fwd(qa, ka, va, pos_col, pos_row, sink_vec)
    out_b, out_b_t, lse_b, (l_oa, l_ob, l_out, w_up_s, l_down) = _mla_fwd(qt, kb, vt, [shard[n] for n in LATE])
    w_oa = _pad_heads_rows(_from_col_shards(l_oa), N_HEADS, HEAD_A)
    w_ob = _pad_heads_rows(_from_col_shards(l_ob), N_HEADS, V_DIM_B)
    w_out_f = l_out.reshape(D_MODEL, D_MODEL)
    w_down_f = l_down.reshape(D_FF, D_MODEL)

    oa_p, ob_p, merged, y, x1, h2 = _merge_fwd(out_a, out_b, gates, xs, w_oa, w_ob, w_out_f, g2, g3)
    a, du, dy2, dx1, loss, dg3, dg4 = _mlp_fwd_bwd(x1, h2, target, w_up_s, w_down_f, g3, g4)
    (dgates, d_oa, d_ob_t, delta_b, dg2, dw_oa, dw_ob, dw_out) = _merge_bwd(
        dx1, y, gates, oa_p, ob_p, out_a, out_b, out_b_t, merged, w_oa, w_ob, w_out_f, g2)
    late_slices = [
        _col_shards(_unpad_heads_rows(dw_oa, N_HEADS, HEAD_A)).astype(BF16),
        _col_shards(_unpad_heads_rows(dw_ob, N_HEADS, V_DIM_B)).astype(BF16),
        dw_out.astype(BF16).reshape(N_DEV, D_MODEL // N_DEV, D_MODEL),
        _matmul_tn(h2, du, "dw_up", BF16, N_DEV),
        _matmul_tn(a, dy2, "dw_down", BF16).reshape(N_DEV, D_FF // N_DEV, D_MODEL),
    ]
    dqa, dka, dva, dsink = _swa_bwd(qa, ka, va, out_a, d_oa, lse_a, pos_col, pos_row, sink_vec)
    dqb_t, dkb_t, dvb_t, late_parts = _mla_bwd(qt, kb, kt, vb, d_ob_t, lse_b, delta_b, late_slices)
    dproj, dx, dg1, dgq, dgkv, dw_qb_t, dw_kvb_t = _inproj_bwd(
        dgates, dqa, dka, dva, dqb_t, dkb_t, dvb_t, cq, ckv, cqn, ckvn, xs, dx1, *tables[3:], g1, g_q, g_kv,
        w_in_p, w_qb, w_kvb)
    early_slices = [
        _col_shards(_unpad_w_in(_matmul_tn(h, dproj, "dw_in"))).astype(BF16),
        _col_shards(_unpad_heads_cols(dw_qb_t.T, N_HEADS, QK_NOPE + QK_ROPE)).astype(BF16),
        _col_shards(_unpad_w_kvb(dw_kvb_t.T)).astype(BF16),
    ]
    small_grads = {"pre_norm_mix": dg1, "post_norm_mix": dg2, "pre_norm_mlp": dg3, "post_norm_mlp": dg4,
                   "q_a_norm": dgq, "kv_a_norm": dgkv, "sinks": dsink.reshape(N_HEADS, BLOCK).sum(axis=1),
                   "loss": loss[0, 0:1]}
    early_parts, s_parts = _exchange_grads(early_slices, _pack_small(small_grads))

    updates = {}
    for name, parts in zip(EARLY + LATE, early_parts + late_parts):
        outs = _adamw(parts, weights[name], m_in[name], v_in[name], "adamw_" + name)
        for kind, arr in zip(("g", "d", "m", "v"), outs):
            updates[kind, name] = arr
    zero = jnp.zeros((), F32)
    pack = lambda src: _pack_small({**{n: src[n] for n in SMALL_NAMES}, "loss": zero})[None]
    smalls = _adamw(s_parts, pack(weights), pack(m_in), pack(v_in), "adamw_small")
    for kind, blk in zip(("g", "d", "m", "v"), smalls):
        for wname, piece in _unpack_small(blk[0]).items():
            updates[kind, wname] = piece
    results = [updates[kind, name] for kind in ("g", "d", "m", "v") for name in WEIGHT_ORDER]
    return (updates["g", "loss"], dx[None], *results)
```

```python
import functools

import numpy as np
import jax
import jax.numpy as jnp
from jax import lax
from jax.experimental import pallas as pl
from jax.experimental.pallas import tpu as pltpu

F32 = jnp.float32
BF16 = jnp.bfloat16

D_MODEL = 1024
D_FF = 4096
N_HEADS = 8
N_KV_A = 2
GROUP_A = N_HEADS // N_KV_A
HEAD_A = 64
QK_NOPE = 64
QK_ROPE = 32
V_DIM_B = 64
Q_LORA = 256
KV_LORA = 128
BLOCK = 128
SLAB = 128
ROPE_THETA = 10000.0
EPS = 1e-6
N_DEV = 8
NEG = -1e30

SCALE_A = HEAD_A ** -0.5
SCALE_B = (QK_NOPE + QK_ROPE) ** -0.5
LOG2E = 1.4426950408889634
SCORE_B = SCALE_B * LOG2E
MLA_HEADS_PER_STEP = 4
MLA_FWD_HEADS_PER_STEP = 8
ONES_ROWS = 16
SLOPES_A = tuple(2.0 ** (-8.0 * (h + 1) / N_HEADS) for h in range(N_HEADS))

ADAM_LR = 0.001
ADAM_B1 = 0.9
ADAM_B2 = 0.999
ADAM_EPS = 1e-08
ADAM_WD = 0.01
ADAM_STEP = 10

HM = N_HEADS * SLAB
C_GATES = 0
C_QA = 2 * D_MODEL
C_KA = C_QA + HM
C_VA = C_KA + N_KV_A * SLAB
C_CQ = C_VA + N_KV_A * SLAB
C_CKV = C_CQ + Q_LORA
C_KR = C_CKV + KV_LORA
D_IN_PAD = C_KR + SLAB

VMEM_LIMIT = 56 * 1024 * 1024

EARLY = ("w_in", "w_q_b", "w_kv_b")
LATE = ("w_o_a", "w_o_b", "w_out", "w_up", "w_down")
ADAM_ROWS = 256
SMALL_ROWS = 8


def _token_tile(t):
    return min(256, t)


def _attn_tile(t):
    return 512 if t >= 2048 else 128


def _params(sem, vmem=VMEM_LIMIT):
    return pltpu.CompilerParams(dimension_semantics=sem, vmem_limit_bytes=vmem)


def _dot(a, b):
    return jnp.dot(a, b, preferred_element_type=F32)


def _dot_nt(a, b):
    return lax.dot_general(a, b, (((1,), (1,)), ((), ())), preferred_element_type=F32)


def _dot_tn(a, b):
    return lax.dot_general(a, b, (((0,), (0,)), ((), ())), preferred_element_type=F32)


def _rms_r(x):
    return lax.rsqrt(jnp.mean(x * x, axis=-1, keepdims=True) + EPS)


def _rms_bwd(x, r, g, dy):
    t = dy * g
    return r * t - x * (r * r * r) * jnp.mean(x * t, axis=-1, keepdims=True)


def _sigmoid(x):
    return 1.0 / (1.0 + jnp.exp(-x))


def _rope(x, c, s1, s2):
    return x * c + pltpu.roll(x, SLAB - 16, 1) * s1 + pltpu.roll(x, 16, 1) * s2


def _rope_bwd(d, c, s1, s2):
    return d * c + pltpu.roll(d * s1, 16, 1) + pltpu.roll(d * s2, SLAB - 16, 1)


def _roll_rows(x, shift):
    return jnp.concatenate([x[-shift:], x[:-shift]], axis=0)


def _rope_t(x, c, s1, s2):
    return x * c + _roll_rows(x, SLAB - 16) * s1 + _roll_rows(x, 16) * s2


def _rope_t_bwd(d, c, s1, s2):
    return d * c + _roll_rows(d * s1, 16) + _roll_rows(d * s2, SLAB - 16)


def _row_spec(tm, n):
    return pl.BlockSpec((tm, n), lambda i: (i, 0))


def _col_spec(n, tm):
    return pl.BlockSpec((n, tm), lambda i: (0, i))


def _full_spec(shape):
    nd = len(shape)
    return pl.BlockSpec(shape, lambda i: (0,) * nd, pipeline_mode=pl.Buffered(1))


def _acc_rows(ref, val):
    @pl.when(pl.program_id(0) == 0)
    def _():
        ref[...] = jnp.zeros_like(ref)
    ref[...] += jnp.sum(val, axis=0, keepdims=True)


def _rope_tables(pos_col, freq_row, early):
    t = pos_col.shape[0]
    tm = _token_tile(t)
    n = len(early)

    def body(pos_ref, f_ref, *rest):
        shard_refs, (c_ref, s1_ref, s2_ref, ct_ref, s1t_ref, s2t_ref) = rest[:n], rest[n:n + 6]
        start, finish = _two_level_gather(shard_refs, rest[n + 6:2 * n + 6], *rest[2 * n + 6:])
        pl.when(pl.program_id(0) == 0)(start)
        ang = pos_ref[...].astype(F32) * f_ref[...]
        lane = lax.broadcasted_iota(jnp.int32, ang.shape, 1)
        s = jnp.sin(ang)
        c = jnp.cos(ang)
        s1 = jnp.where((lane >= 64) & (lane < 80), -s, 0.0)
        s2 = jnp.where((lane >= 80) & (lane < 96), s, 0.0)
        c_ref[...], s1_ref[...], s2_ref[...] = c, s1, s2
        ct_ref[...], s1t_ref[...], s2t_ref[...] = c.T, s1.T, s2.T
        pl.when(pl.program_id(0) == t // tm - 1)(finish)

    tab = jax.ShapeDtypeStruct((t, SLAB), F32)
    tabt = jax.ShapeDtypeStruct((SLAB, t), F32)
    outs = pl.pallas_call(
        body, name="rope_tables", grid=(t // tm,),
        in_specs=[_row_spec(tm, 1), _full_spec((1, SLAB))] + [ANY_SPEC] * n,
        out_specs=[_row_spec(tm, SLAB)] * 3 + [_col_spec(SLAB, tm)] * 3 + [ANY_SPEC] * n,
        out_shape=[tab] * 3 + [tabt] * 3 + [jax.ShapeDtypeStruct((N_DEV,) + a.shape, a.dtype) for a in early],
        scratch_shapes=_exchange_scratch(n),
        compiler_params=_params(("arbitrary",)),
    )(pos_col, freq_row, *early)
    return outs[:6], outs[6:]


def _inproj_fwd(x, g1, w_in, g_q, g_kv, w_kvb, w_qb_t, w_kb_t, w_vb_t, w_kr_t, tables):
    t = x.shape[0]
    tm = _token_tile(t)

    def body(x_ref, g1_ref, win_ref, gq_ref, gkv_ref, wkvb_ref, wqbt_ref, wkbt_ref, wvbt_ref, wkrt_ref,
             c_ref, s1_ref, s2_ref, ct_ref, s1t_ref, s2t_ref,
             h_ref, gates_ref, qa_ref, ka_ref, va_ref, cq_ref, ckv_ref, cqn_ref, ckvn_ref,
             kb_ref, vb_ref, qt_ref, kt_ref, vt_ref):
        xv = x_ref[...]
        h = (xv * _rms_r(xv) * g1_ref[...]).astype(BF16)
        h_ref[...] = h
        proj = _dot(h, win_ref[...])
        gates_ref[...] = proj[:, C_GATES:C_QA].astype(BF16)
        qa_ref[...] = proj[:, C_QA:C_KA].astype(BF16)
        ka_ref[...] = proj[:, C_KA:C_VA].astype(BF16)
        va_ref[...] = proj[:, C_VA:C_CQ].astype(BF16)
        cq = proj[:, C_CQ:C_CKV]
        ckv = proj[:, C_CKV:C_KR]
        kr = proj[:, C_KR:D_IN_PAD]
        cq_ref[...] = cq
        ckv_ref[...] = ckv
        cqn = (cq * _rms_r(cq) * gq_ref[...]).astype(BF16)
        ckvn = (ckv * _rms_r(ckv) * gkv_ref[...]).astype(BF16)
        cqn_ref[...] = cqn
        ckvn_ref[...] = ckvn
        c, s1, s2 = c_ref[...], s1_ref[...], s2_ref[...]
        kvb = _dot(ckvn, wkvb_ref[...])
        kr_rot = _rope(kr, c, s1, s2)
        ct, s1t, s2t = ct_ref[...], s1t_ref[...], s2t_ref[...]
        q_t = _dot_nt(wqbt_ref[...], cqn)
        k_t = _dot_nt(wkbt_ref[...], ckvn)
        kr_t = _rope_t(_dot_nt(wkrt_ref[...], h), ct, s1t, s2t)
        for hd in range(N_HEADS):
            sl = slice(hd * SLAB, (hd + 1) * SLAB)
            kb_ref[:, sl] = (kvb[:, sl] + kr_rot).astype(BF16)
            qt_ref[sl, :] = (_rope_t(q_t[sl, :], ct, s1t, s2t) * SCORE_B).astype(BF16)
            kt_ref[sl, :] = (k_t[sl, :] + kr_t).astype(BF16)
        vb_ref[...] = kvb[:, HM:2 * HM].astype(BF16)
        pad_row = lax.broadcasted_iota(jnp.int32, (HM, 1), 0) & (SLAB - 1)
        ones_rows = jnp.where((pad_row >= V_DIM_B) & (pad_row < V_DIM_B + ONES_ROWS), 1.0, 0.0)
        vt_ref[...] = (_dot_nt(wvbt_ref[...], ckvn) + ones_rows).astype(BF16)

    def sds(n, dt):
        return jax.ShapeDtypeStruct((t, n), dt)

    outs = [(D_MODEL, BF16), (2 * D_MODEL, BF16), (HM, BF16), (N_KV_A * SLAB, BF16), (N_KV_A * SLAB, BF16),
            (Q_LORA, F32), (KV_LORA, F32), (Q_LORA, BF16), (KV_LORA, BF16), (HM, BF16), (HM, BF16)]
    tab, tabt = _row_spec(tm, SLAB), _col_spec(SLAB, tm)
    return pl.pallas_call(
        body, name="inproj_fwd", grid=(t // tm,),
        in_specs=[_row_spec(tm, D_MODEL), _full_spec((1, D_MODEL)), _full_spec((D_MODEL, D_IN_PAD)),
                  _full_spec((1, Q_LORA)), _full_spec((1, KV_LORA)), _full_spec((KV_LORA, 2 * HM)),
                  _full_spec((HM, Q_LORA)), _full_spec((HM, KV_LORA)), _full_spec((HM, KV_LORA)),
                  _full_spec((SLAB, D_MODEL)), tab, tab, tab, tabt, tabt, tabt],
        out_specs=[_row_spec(tm, n) for n, _ in outs] + [_col_spec(HM, tm)] * 3,
        out_shape=[sds(n, dt) for n, dt in outs] + [jax.ShapeDtypeStruct((HM, t), BF16)] * 3,
        compiler_params=_params(("parallel",)),
    )(x, g1, w_in, g_q, g_kv, w_kvb, w_qb_t, w_kb_t, w_vb_t, w_kr_t, *tables)


def _tile_group(a):
    return jnp.concatenate([a] * GROUP_A, axis=1)


def _swa_masks():
    row = lax.broadcasted_iota(jnp.int32, (BLOCK, GROUP_A * BLOCK), 0)
    col = lax.broadcasted_iota(jnp.int32, (BLOCK, GROUP_A * BLOCK), 1) & (BLOCK - 1)
    return row <= col, row > col


def _heads_beside(ref, g):
    return jnp.concatenate([ref[:, (g * GROUP_A + hh) * SLAB:(g * GROUP_A + hh + 1) * SLAB].T
                            for hh in range(GROUP_A)], axis=1)


def _rows_beside(ref, g):
    return jnp.concatenate([ref[g * GROUP_A + hh] for hh in range(GROUP_A)], axis=1)


def _swa_rows(sinks):
    slopes = jnp.repeat(jnp.asarray(SLOPES_A, F32).reshape(N_KV_A, GROUP_A, 1), BLOCK, axis=2)
    sink_rows = jnp.repeat(sinks.reshape(N_KV_A, GROUP_A, 1), BLOCK, axis=2)
    return slopes.reshape(N_KV_A, 1, GROUP_A * BLOCK), sink_rows.reshape(N_KV_A, 1, GROUP_A * BLOCK)


def _swa_fwd(qa, ka, va, pos_col, pos_row, sinks):
    t = qa.shape[0]
    nb = t // BLOCK
    gw = GROUP_A * BLOCK
    slope_rows, sink_rows = _swa_rows(sinks)

    def body(q_ref, kc_ref, kp_ref, vc_ref, vp_ref, pkc_ref, pkp_ref, pq_ref, slope_ref, sink_ref, o_ref, l_ref):
        i = pl.program_id(0)
        pq = pq_ref[...]
        dist_c = _tile_group(jnp.abs(pkc_ref[...] - pq).astype(F32))
        dist_p = _tile_group(jnp.abs(pkp_ref[...] - pq).astype(F32))
        mask_c, older = _swa_masks()
        mask_p = jnp.logical_and(older, i > 0)
        for g in range(N_KV_A):
            gs = slice(g * SLAB, (g + 1) * SLAB)
            x = _heads_beside(q_ref, g)
            slope, sink = slope_ref[g], sink_ref[g]
            s_c = jnp.where(mask_c, _dot(kc_ref[:, gs], x) * SCALE_A - slope * dist_c, NEG)
            s_p = jnp.where(mask_p, _dot(kp_ref[:, gs], x) * SCALE_A - slope * dist_p, NEG)
            m = jnp.maximum(jnp.maximum(jnp.max(s_c, axis=0, keepdims=True),
                                        jnp.max(s_p, axis=0, keepdims=True)), sink)
            e_c = jnp.exp(s_c - m)
            e_p = jnp.exp(s_p - m)
            den = jnp.sum(e_c, axis=0, keepdims=True) + jnp.sum(e_p, axis=0, keepdims=True) + jnp.exp(sink - m)
            inv = 1.0 / den
            ot = (_dot_tn(vc_ref[:, gs], (e_c * inv).astype(BF16))
                  + _dot_tn(vp_ref[:, gs], (e_p * inv).astype(BF16)))
            lse = m + jnp.log(den)
            for hh in range(GROUP_A):
                hd = g * GROUP_A + hh
                seg = slice(hh * BLOCK, (hh + 1) * BLOCK)
                o_ref[:, hd * SLAB:(hd + 1) * SLAB] = ot[:, seg].T.astype(BF16)
                l_ref[hd] = lse[:, seg]

    cur = lambda i: (i, 0)
    prev = lambda i: (jnp.maximum(i - 1, 0), 0)
    kvw = N_KV_A * SLAB
    rows = pl.BlockSpec((N_KV_A, 1, gw), lambda i: (0, 0, 0))
    return pl.pallas_call(
        body, name="swa_fwd", grid=(nb,),
        in_specs=[pl.BlockSpec((BLOCK, HM), cur),
                  pl.BlockSpec((BLOCK, kvw), cur), pl.BlockSpec((BLOCK, kvw), prev),
                  pl.BlockSpec((BLOCK, kvw), cur), pl.BlockSpec((BLOCK, kvw), prev),
                  pl.BlockSpec((BLOCK, 1), cur), pl.BlockSpec((BLOCK, 1), prev),
                  pl.BlockSpec((1, BLOCK), lambda i: (0, i)), rows, rows],
        out_specs=[pl.BlockSpec((BLOCK, HM), cur), pl.BlockSpec((N_HEADS, 1, BLOCK), lambda i: (0, 0, i))],
        out_shape=[jax.ShapeDtypeStruct((t, HM), BF16), jax.ShapeDtypeStruct((N_HEADS, 1, t), F32)],
        compiler_params=_params(("parallel",)),
    )(qa, ka, ka, va, va, pos_col, pos_col, pos_row, slope_rows, sink_rows)


def _swa_bwd(qa, ka, va, out_a, d_oa, lse, pos_col, pos_row, sinks):
    t = qa.shape[0]
    nb = t // BLOCK
    gw = GROUP_A * BLOCK
    slope_rows, sink_rows = _swa_rows(sinks)

    def body(q_ref, qn_ref, do_ref, don_ref, l_ref, ln_ref, o_ref, on_ref, kp_ref, kc_ref, vp_ref, vc_ref,
             pkp_ref, pkc_ref, pq_ref, pqn_ref, slope_ref, sink_ref, dq_ref, dk_ref, dv_ref, dsink_ref):
        j = pl.program_id(0)
        pkc, pkp = pkc_ref[...], pkp_ref[...]
        dist_cc = _tile_group(jnp.abs(pkc - pq_ref[...]).astype(F32))
        dist_cp = _tile_group(jnp.abs(pkp - pq_ref[...]).astype(F32))
        dist_nc = _tile_group(jnp.abs(pkc - pqn_ref[...]).astype(F32))
        mask_cc, older = _swa_masks()
        mask_cp = jnp.logical_and(older, j > 0)
        mask_nc = jnp.logical_and(older, j < nb - 1)

        @pl.when(j == 0)
        def _():
            dsink_ref[...] = jnp.zeros_like(dsink_ref)

        def tile(k, v, x, dox, lrow, drow, dist, mask, slope):
            s = jnp.where(mask, _dot(k, x) * SCALE_A - slope * dist, NEG)
            p = jnp.exp(s - lrow)
            ds = p * (_dot(v, dox) - drow)
            return p.astype(BF16), ds.astype(BF16)

        for g in range(N_KV_A):
            gs = slice(g * SLAB, (g + 1) * SLAB)
            kc, kp, vc, vp = kc_ref[:, gs], kp_ref[:, gs], vc_ref[:, gs], vp_ref[:, gs]
            slope, sink = slope_ref[g], sink_ref[g]
            x, xn = _heads_beside(q_ref, g), _heads_beside(qn_ref, g)
            dox, doxn = _heads_beside(do_ref, g), _heads_beside(don_ref, g)
            lrow, lrown = _rows_beside(l_ref, g), _rows_beside(ln_ref, g)
            drow = jnp.sum(dox.astype(F32) * _heads_beside(o_ref, g).astype(F32), axis=0, keepdims=True)
            drown = jnp.sum(doxn.astype(F32) * _heads_beside(on_ref, g).astype(F32), axis=0, keepdims=True)
            p_cc, ds_cc = tile(kc, vc, x, dox, lrow, drow, dist_cc, mask_cc, slope)
            _, ds_cp = tile(kp, vp, x, dox, lrow, drow, dist_cp, mask_cp, slope)
            p_nc, ds_nc = tile(kc, vc, xn, doxn, lrown, drown, dist_nc, mask_nc, slope)
            dqt = (_dot_tn(kc, ds_cc) + _dot_tn(kp, ds_cp)) * SCALE_A
            for hh in range(GROUP_A):
                hd = g * GROUP_A + hh
                dq_ref[:, hd * SLAB:(hd + 1) * SLAB] = dqt[:, hh * BLOCK:(hh + 1) * BLOCK].T.astype(BF16)
            dk_ref[:, gs] = ((_dot_nt(ds_cc, x) + _dot_nt(ds_nc, xn)) * SCALE_A).astype(BF16)
            dv_ref[:, gs] = (_dot_nt(p_cc, dox) + _dot_nt(p_nc, doxn)).astype(BF16)
            dsink_ref[g] -= jnp.exp(sink - lrow) * drow

    cur = lambda j: (j, 0)
    prev = lambda j: (jnp.maximum(j - 1, 0), 0)
    nxt = lambda j: (jnp.minimum(j + 1, nb - 1), 0)
    cur3 = lambda j: (0, 0, j)
    nxt3 = lambda j: (0, 0, jnp.minimum(j + 1, nb - 1))
    kvw = N_KV_A * SLAB
    rows = pl.BlockSpec((N_KV_A, 1, gw), lambda j: (0, 0, 0))
    stat = lambda im: pl.BlockSpec((N_HEADS, 1, BLOCK), im)
    return pl.pallas_call(
        body, name="swa_bwd", grid=(nb,),
        in_specs=[pl.BlockSpec((BLOCK, HM), cur), pl.BlockSpec((BLOCK, HM), nxt),
                  pl.BlockSpec((BLOCK, HM), cur), pl.BlockSpec((BLOCK, HM), nxt),
                  stat(cur3), stat(nxt3), pl.BlockSpec((BLOCK, HM), cur), pl.BlockSpec((BLOCK, HM), nxt),
                  pl.BlockSpec((BLOCK, kvw), prev), pl.BlockSpec((BLOCK, kvw), cur),
                  pl.BlockSpec((BLOCK, kvw), prev), pl.BlockSpec((BLOCK, kvw), cur),
                  pl.BlockSpec((BLOCK, 1), prev), pl.BlockSpec((BLOCK, 1), cur),
                  pl.BlockSpec((1, BLOCK), lambda j: (0, j)),
                  pl.BlockSpec((1, BLOCK), lambda j: (0, jnp.minimum(j + 1, nb - 1))), rows, rows],
        out_specs=[pl.BlockSpec((BLOCK, HM), cur), pl.BlockSpec((BLOCK, kvw), cur),
                   pl.BlockSpec((BLOCK, kvw), cur), rows],
        out_shape=[jax.ShapeDtypeStruct((t, HM), BF16), jax.ShapeDtypeStruct((t, kvw), BF16),
                   jax.ShapeDtypeStruct((t, kvw), BF16), jax.ShapeDtypeStruct((N_KV_A, 1, gw), F32)],
        compiler_params=_params(("arbitrary",)),
    )(qa, qa, d_oa, d_oa, lse, lse, out_a, out_a, ka, ka, va, va,
      pos_col, pos_col, pos_row, pos_row, slope_rows, sink_rows)


def _mesh_pos():
    return lax.axis_index("x"), lax.axis_index("y"), lax.axis_index("c")


def _flip(v, bit):
    return 1 - v if bit else v


def _direct_copies(srcs, dsts, send_sems, recv_sems, local_sems, gather, sem_base=0):
    x, y, c = _mesh_pos()
    me = 4 * x + 2 * y + c
    local, remote = [], []
    for a, (src, dst) in enumerate(zip(srcs, dsts)):
        local.append(pltpu.make_async_copy(src if gather else src.at[me], dst.at[me], local_sems.at[sem_base + a]))
        for r in range(1, N_DEV):
            px, py, pc = _flip(x, r & 4), _flip(y, r & 2), _flip(c, r & 1)
            sem = (N_DEV - 1) * (sem_base + a) + r - 1
            remote.append(pltpu.make_async_remote_copy(
                src_ref=src if gather else src.at[4 * px + 2 * py + pc], dst_ref=dst.at[me],
                send_sem=send_sems.at[sem], recv_sem=recv_sems.at[sem],
                device_id=(px, py, pc), device_id_type=pl.DeviceIdType.MESH))
    return local, remote


def _start_copies(local, remote):
    for cp in local + remote:
        cp.start()


def _wait_copies(local, remote):
    for cp in remote:
        cp.wait_recv()
    for cp in remote:
        cp.wait_send()
    for cp in local:
        cp.wait()


def _exchange_scratch(n):
    return [pltpu.SemaphoreType.DMA((n * (N_DEV - 1),)), pltpu.SemaphoreType.DMA((n * (N_DEV - 1),)),
            pltpu.SemaphoreType.DMA((n,))]


ANY_SPEC = pl.BlockSpec(memory_space=pl.ANY)


def _mla_fwd(qt, kb, vt, late):
    t = kb.shape[0]
    tk = _attn_tile(t)
    ratio = 2 if t >= 2 * tk else 1
    tq = ratio * tk
    nq = t // tq
    hps = MLA_FWD_HEADS_PER_STEP
    w = hps * SLAB
    pairs = [(i, j) for i in range(nq) for j in range(ratio * (i + 1))]
    i_tab = jnp.asarray(np.array([p[0] for p in pairs], np.int32))
    j_tab = jnp.asarray(np.array([p[1] for p in pairs], np.int32))

    n_late = len(late)

    def body(it_ref, jt_ref, qt_ref, k_ref, vt_ref, *rest):
        late_refs, (o_ref, ot_ref, l_ref) = rest[:n_late], rest[n_late:n_late + 3]
        gathered_refs = rest[n_late + 3:2 * n_late + 3]
        m_s, acc_s, send_sems, recv_sems, local_sems = rest[2 * n_late + 3:]
        n = pl.program_id(1)
        i, j = it_ref[n], jt_ref[n]
        first_step = jnp.logical_and(pl.program_id(0) == 0, n == 0)
        last_step = jnp.logical_and(pl.program_id(0) == N_HEADS // hps - 1, n == len(pairs) - 1)

        @pl.when(first_step)
        def _():
            _start_copies(*_direct_copies(late_refs, gathered_refs, send_sems, recv_sems, local_sems, True))

        @pl.when(j == 0)
        def _():
            m_s[...] = jnp.full_like(m_s, NEG)
            acc_s[...] = jnp.zeros_like(acc_s)

        def update(masked, q0):
            qc = slice(q0, tq)

            def scores(hh):
                sl = slice(hh * SLAB, (hh + 1) * SLAB)
                return _dot(k_ref[:, sl], qt_ref[sl, qc])

            def softmax(hh, s):
                if masked:
                    s = jnp.where(lax.broadcasted_iota(jnp.int32, s.shape, 0)
                                  <= lax.broadcasted_iota(jnp.int32, s.shape, 1), s, NEG)
                m_old = m_s[hh][:, qc]
                m_new = jnp.maximum(m_old, jnp.max(s, axis=0, keepdims=True))
                m_s[hh, :, qc] = m_new
                return jnp.exp2(s - m_new).astype(BF16), jnp.exp2(m_old - m_new)

            def accumulate(hh, p, alpha):
                sl = slice(hh * SLAB, hh * SLAB + V_DIM_B + ONES_ROWS)
                acc_s[sl, qc] = alpha * acc_s[sl, qc] + _dot(vt_ref[sl, :], p)

            s_next, pending = scores(0), None
            for hh in range(hps):
                s = s_next
                if hh + 1 < hps:
                    s_next = scores(hh + 1)
                p, alpha = softmax(hh, s)
                if pending is not None:
                    accumulate(*pending)
                pending = (hh, p, alpha)
            accumulate(*pending)

        @pl.when(j < ratio * i)
        def _():
            update(False, 0)

        for part in range(ratio):
            @pl.when(j == ratio * i + part)
            def _():
                update(True, part * tk)

        @pl.when(j == ratio * i + ratio - 1)
        def _():
            for hh in range(hps):
                sl = slice(hh * SLAB, (hh + 1) * SLAB)
                den = acc_s[hh * SLAB + V_DIM_B:hh * SLAB + V_DIM_B + 1, :]
                values = lax.broadcasted_iota(jnp.int32, (SLAB, tq), 0) < V_DIM_B
                ot = jnp.where(values, acc_s[sl, :] / den, 0.0)
                ot_ref[sl, :] = ot.astype(BF16)
                o_ref[:, sl] = ot.T.astype(BF16)
                l_ref[hh] = m_s[hh] + jnp.log2(den)

        @pl.when(last_step)
        def _():
            _wait_copies(*_direct_copies(late_refs, gathered_refs, send_sems, recv_sems, local_sems, True))

    grid_spec = pltpu.PrefetchScalarGridSpec(
        num_scalar_prefetch=2, grid=(N_HEADS // hps, len(pairs)),
        in_specs=[pl.BlockSpec((w, tq), lambda h, n, it, jt: (h, it[n])),
                  pl.BlockSpec((tk, w), lambda h, n, it, jt: (jt[n], h)),
                  pl.BlockSpec((w, tk), lambda h, n, it, jt: (h, jt[n]))] + [ANY_SPEC] * n_late,
        out_specs=[pl.BlockSpec((tq, w), lambda h, n, it, jt: (it[n], h)),
                   pl.BlockSpec((w, tq), lambda h, n, it, jt: (h, it[n])),
                   pl.BlockSpec((hps, 1, tq), lambda h, n, it, jt: (h, 0, it[n]))] + [ANY_SPEC] * n_late,
        scratch_shapes=[pltpu.VMEM((hps, 1, tq), F32), pltpu.VMEM((w, tq), F32)] + _exchange_scratch(n_late))
    outs = pl.pallas_call(
        body, name="mla_fwd", grid_spec=grid_spec,
        out_shape=[jax.ShapeDtypeStruct((t, HM), BF16), jax.ShapeDtypeStruct((HM, t), BF16),
                   jax.ShapeDtypeStruct((N_HEADS, 1, t), F32)]
        + [jax.ShapeDtypeStruct((N_DEV,) + a.shape, a.dtype) for a in late],
        compiler_params=_params(("arbitrary", "arbitrary")),
    )(i_tab, j_tab, qt, kb, vt, *late)
    return outs[0], outs[1], outs[2], list(outs[3:])


def _mla_bwd(qt, kb, kt, vb, d_ob_t, lse, delta, grad_slices):
    t = kb.shape[0]
    tk = _attn_tile(t)
    ratio = 2 if t >= 2 * tk else 1
    tq = ratio * tk
    nk, nq = t // tk, t // tq
    hps = MLA_HEADS_PER_STEP
    w = hps * SLAB
    pairs = [(j, i) for j in range(nk) for i in range(j // ratio, nq)]
    j_tab = jnp.asarray(np.array([p[0] for p in pairs], np.int32))
    i_tab = jnp.asarray(np.array([p[1] for p in pairs], np.int32))

    n_ex = len(grad_slices)

    def body(jt_ref, it_ref, qt_ref, dot_ref, l_ref, dl_ref, k_ref, kt_ref, v_ref, *rest):
        slice_refs, (dqt_ref, dkt_ref, dvt_ref) = rest[:n_ex], rest[n_ex:n_ex + 3]
        part_refs = rest[n_ex + 3:2 * n_ex + 3]
        dk_s, dv_s, send_sems, recv_sems, local_sems = rest[2 * n_ex + 3:]
        n = pl.program_id(1)
        j, i = jt_ref[n], it_ref[n]
        first_step = jnp.logical_and(pl.program_id(0) == 0, n == 0)
        last_step = jnp.logical_and(pl.program_id(0) == N_HEADS // hps - 1, n == len(pairs) - 1)

        @pl.when(first_step)
        def _():
            _start_copies(*_direct_copies(slice_refs, part_refs, send_sems, recv_sems, local_sems, False))

        @pl.when(n == 0)
        def _():
            dqt_ref[...] = jnp.zeros_like(dqt_ref)

        def update(diagonal, q0):
            qc = slice(q0, tq)
            cols = pl.ds(pl.multiple_of(i * tq + q0, tk), tq - q0)

            def products(hh):
                sl = slice(hh * SLAB, (hh + 1) * SLAB)
                return _dot(k_ref[:, sl], qt_ref[sl, qc]), _dot(v_ref[:, sl], dot_ref[sl, qc])

            def softmax_bwd(hh, s, dp):
                if diagonal:
                    s = jnp.where(lax.broadcasted_iota(jnp.int32, s.shape, 0)
                                  <= lax.broadcasted_iota(jnp.int32, s.shape, 1), s, NEG)
                p = jnp.exp2(s - l_ref[hh][:, qc])
                return p.astype(BF16), (p * (dp - dl_ref[hh][:, qc])).astype(BF16)

            def gradients(hh, p, ds):
                base = hh * SLAB
                vrows = slice(base, base + V_DIM_B)
                qrows = slice(base, base + QK_NOPE + QK_ROPE)
                dv = _dot_nt(dot_ref[vrows, qc], p)
                dk = _dot_nt(qt_ref[qrows, qc], ds)
                if diagonal:
                    dv_s[base:base + SLAB, :] = jnp.concatenate([dv, jnp.zeros((SLAB - V_DIM_B, tk), F32)], axis=0)
                    dk_s[base:base + SLAB, :] = jnp.concatenate(
                        [dk, jnp.zeros((SLAB - QK_NOPE - QK_ROPE, tk), F32)], axis=0)
                else:
                    dv_s[vrows, :] += dv
                    dk_s[qrows, :] += dk
                dqt_ref[qrows, cols] += _dot(kt_ref[qrows, :], ds)

            for hh in range(hps):
                gradients(hh, *softmax_bwd(hh, *products(hh)))

        first_tile = lax.div(j, ratio)
        for part in range(ratio):
            @pl.when(jnp.logical_and(i == first_tile, lax.rem(j, ratio) == part))
            def _():
                update(True, part * tk)

        @pl.when(i > first_tile)
        def _():
            update(False, 0)

        @pl.when(i == nq - 1)
        def _():
            dkt_ref[...] = (dk_s[...] * (1.0 / LOG2E)).astype(BF16)
            dvt_ref[...] = dv_s[...].astype(BF16)

        @pl.when(last_step)
        def _():
            _wait_copies(*_direct_copies(slice_refs, part_refs, send_sems, recv_sems, local_sems, False))

    grid_spec = pltpu.PrefetchScalarGridSpec(
        num_scalar_prefetch=2, grid=(N_HEADS // hps, len(pairs)),
        in_specs=[pl.BlockSpec((w, tq), lambda h, n, jt, it: (h, it[n])),
                  pl.BlockSpec((w, tq), lambda h, n, jt, it: (h, it[n])),
                  pl.BlockSpec((hps, 1, tq), lambda h, n, jt, it: (h, 0, it[n])),
                  pl.BlockSpec((hps, 1, tq), lambda h, n, jt, it: (h, 0, it[n])),
                  pl.BlockSpec((tk, w), lambda h, n, jt, it: (jt[n], h)),
                  pl.BlockSpec((w, tk), lambda h, n, jt, it: (h, jt[n])),
                  pl.BlockSpec((tk, w), lambda h, n, jt, it: (jt[n], h))] + [ANY_SPEC] * n_ex,
        out_specs=[pl.BlockSpec((w, t), lambda h, n, jt, it: (h, 0)),
                   pl.BlockSpec((w, tk), lambda h, n, jt, it: (h, jt[n])),
                   pl.BlockSpec((w, tk), lambda h, n, jt, it: (h, jt[n]))] + [ANY_SPEC] * n_ex,
        scratch_shapes=[pltpu.VMEM((w, tk), F32), pltpu.VMEM((w, tk), F32)] + _exchange_scratch(n_ex))
    outs = pl.pallas_call(
        body, name="mla_bwd", grid_spec=grid_spec,
        out_shape=[jax.ShapeDtypeStruct((HM, t), F32), jax.ShapeDtypeStruct((HM, t), BF16),
                   jax.ShapeDtypeStruct((HM, t), BF16)]
        + [jax.ShapeDtypeStruct(a.shape, a.dtype) for a in grad_slices],
        compiler_params=_params(("arbitrary", "arbitrary")),
    )(j_tab, i_tab, qt, d_ob_t, lse, delta, kb, kt, vb, *grad_slices)
    return outs[0], outs[1], outs[2], list(outs[3:])


def _merge_fwd(out_a, out_b, gates, x, w_oa, w_ob, w_out, g2, g3):
    t = x.shape[0]
    tm = _token_tile(t)

    def body(oa_ref, ob_ref, gates_ref, x_ref, woa_ref, wob_ref, wout_ref, g2_ref, g3_ref,
             oap_ref, obp_ref, merged_ref, y_ref, x1_ref, h2_ref):
        oa_p = _dot(oa_ref[...], woa_ref[...])
        ob_p = _dot(ob_ref[...], wob_ref[...])
        oap_ref[...] = oa_p.astype(BF16)
        obp_ref[...] = ob_p.astype(BF16)
        sa = _sigmoid(gates_ref[:, 0:D_MODEL].astype(F32))
        sb = _sigmoid(gates_ref[:, D_MODEL:2 * D_MODEL].astype(F32))
        merged = (sa * oa_p + sb * ob_p).astype(BF16)
        merged_ref[...] = merged
        y = _dot(merged, wout_ref[...])
        y_ref[...] = y
        x1 = x_ref[...] + y * _rms_r(y) * g2_ref[...]
        x1_ref[...] = x1
        h2_ref[...] = (x1 * _rms_r(x1) * g3_ref[...]).astype(BF16)

    def sds(dt):
        return jax.ShapeDtypeStruct((t, D_MODEL), dt)

    row = _row_spec(tm, D_MODEL)
    return pl.pallas_call(
        body, name="merge_fwd", grid=(t // tm,),
        in_specs=[_row_spec(tm, HM), _row_spec(tm, HM), _row_spec(tm, 2 * D_MODEL), row,
                  _full_spec((HM, D_MODEL)), _full_spec((HM, D_MODEL)), _full_spec((D_MODEL, D_MODEL)),
                  _full_spec((1, D_MODEL)), _full_spec((1, D_MODEL))],
        out_specs=[row] * 6,
        out_shape=[sds(BF16), sds(BF16), sds(BF16), sds(F32), sds(F32), sds(BF16)],
        compiler_params=_params(("parallel",)),
    )(out_a, out_b, gates, x, w_oa, w_ob, w_out, g2, g3)


def _merge_bwd(dx1, y, gates, oa_p, ob_p, out_a, out_b, out_b_t, merged, w_oa, w_ob, w_out, g2):
    t = dx1.shape[0]
    tm = _token_tile(t)

    def body(dx1_ref, y_ref, gates_ref, oap_ref, obp_ref, oa_ref, ob_ref, obt_ref, merged_ref,
             woa_ref, wob_ref, wout_ref, g2_ref,
             dgates_ref, doa_ref, dobt_ref, dlb_ref, dg2_ref, dwoa_ref, dwob_ref, dwout_ref):
        @pl.when(pl.program_id(0) == 0)
        def _():
            dwoa_ref[...] = jnp.zeros_like(dwoa_ref)
            dwob_ref[...] = jnp.zeros_like(dwob_ref)
            dwout_ref[...] = jnp.zeros_like(dwout_ref)

        dx1v = dx1_ref[...]
        yv = y_ref[...]
        r2 = _rms_r(yv)
        _acc_rows(dg2_ref, dx1v * yv * r2)
        dy = _rms_bwd(yv, r2, g2_ref[...], dx1v).astype(BF16)
        dwout_ref[...] += _dot_tn(merged_ref[...], dy)
        dm = _dot_nt(dy, wout_ref[...])
        sa = _sigmoid(gates_ref[:, 0:D_MODEL].astype(F32))
        sb = _sigmoid(gates_ref[:, D_MODEL:2 * D_MODEL].astype(F32))
        d_oap = (dm * sa).astype(BF16)
        d_obp = (dm * sb).astype(BF16)
        dwoa_ref[...] += _dot_tn(oa_ref[...], d_oap)
        dwob_ref[...] += _dot_tn(ob_ref[...], d_obp)
        dgates_ref[:, 0:D_MODEL] = (dm * oap_ref[...].astype(F32) * sa * (1.0 - sa)).astype(BF16)
        dgates_ref[:, D_MODEL:2 * D_MODEL] = (dm * obp_ref[...].astype(F32) * sb * (1.0 - sb)).astype(BF16)
        doa_ref[...] = _dot_nt(d_oap, woa_ref[...]).astype(BF16)
        d_ob_t = _dot_nt(wob_ref[...], d_obp)
        dobt_ref[...] = d_ob_t.astype(BF16)
        for hd in range(N_HEADS):
            sl = slice(hd * SLAB, (hd + 1) * SLAB)
            dlb_ref[hd] = jnp.sum(d_ob_t[sl, :] * obt_ref[sl, :].astype(F32), axis=0, keepdims=True)

    def sds(n, dt):
        return jax.ShapeDtypeStruct((t, n), dt)

    row = _row_spec(tm, D_MODEL)
    head3 = pl.BlockSpec((N_HEADS, 1, tm), lambda i: (0, 0, i))
    return pl.pallas_call(
        body, name="merge_bwd", grid=(t // tm,),
        in_specs=[row, row, _row_spec(tm, 2 * D_MODEL), row, row, _row_spec(tm, HM), _row_spec(tm, HM),
                  _col_spec(HM, tm), row,
                  _full_spec((HM, D_MODEL)), _full_spec((HM, D_MODEL)), _full_spec((D_MODEL, D_MODEL)),
                  _full_spec((1, D_MODEL))],
        out_specs=[_row_spec(tm, 2 * D_MODEL), _row_spec(tm, HM), _col_spec(HM, tm),
                   head3, _full_spec((1, D_MODEL)),
                   _full_spec((HM, D_MODEL)), _full_spec((HM, D_MODEL)), _full_spec((D_MODEL, D_MODEL))],
        out_shape=[sds(2 * D_MODEL, BF16), sds(HM, BF16), jax.ShapeDtypeStruct((HM, t), BF16),
                   jax.ShapeDtypeStruct((N_HEADS, 1, t), F32), jax.ShapeDtypeStruct((1, D_MODEL), F32),
                   jax.ShapeDtypeStruct((HM, D_MODEL), F32), jax.ShapeDtypeStruct((HM, D_MODEL), F32),
                   jax.ShapeDtypeStruct((D_MODEL, D_MODEL), F32)],
        compiler_params=_params(("arbitrary",)),
    )(dx1, y, gates, oa_p, ob_p, out_a, out_b, out_b_t, merged, w_oa, w_ob, w_out, g2)


def _mlp_fwd_bwd(x1, h2, target, w_up, w_down, g3, g4):
    t = x1.shape[0]
    tm = _token_tile(t)
    fs = D_FF // N_DEV

    def body(x1_ref, h2_ref, tgt_ref, wup_ref, wdown_ref, g3_ref, g4_ref,
             a_ref, du_ref, dy2_ref, dx1_ref, loss_ref, dg3_ref, dg4_ref):
        x1v = x1_ref[...]
        h2v = h2_ref[...]
        u = jnp.concatenate([_dot(h2v, wup_ref[s]) for s in range(N_DEV)], axis=1)
        ru = jnp.maximum(u, 0.0)
        a = (ru * ru).astype(BF16)
        a_ref[...] = a
        y2 = _dot(a, wdown_ref[...])
        r4 = _rms_r(y2)
        diff = x1v + y2 * r4 * g4_ref[...] - tgt_ref[...]
        _acc_rows(loss_ref, jnp.sum(diff * diff, axis=-1, keepdims=True) * (0.5 / D_MODEL)
                  * jnp.ones((1, SLAB), F32))
        dx2 = diff * (1.0 / D_MODEL)
        _acc_rows(dg4_ref, dx2 * y2 * r4)
        dy2 = _rms_bwd(y2, r4, g4_ref[...], dx2).astype(BF16)
        dy2_ref[...] = dy2
        du = (_dot_nt(dy2, wdown_ref[...]) * (2.0 * ru)).astype(BF16)
        du_ref[...] = du
        dh2 = _dot_nt(du[:, 0:fs], wup_ref[0])
        for s in range(1, N_DEV):
            dh2 += _dot_nt(du[:, s * fs:(s + 1) * fs], wup_ref[s])
        r3 = _rms_r(x1v)
        _acc_rows(dg3_ref, dh2 * x1v * r3)
        dx1_ref[...] = dx2 + _rms_bwd(x1v, r3, g3_ref[...], dh2)

    row = _row_spec(tm, D_MODEL)
    frow = _row_spec(tm, D_FF)
    vec = _full_spec((1, D_MODEL))
    return pl.pallas_call(
        body, name="mlp_fwd_bwd", grid=(t // tm,),
        in_specs=[row, row, row, _full_spec((N_DEV, D_MODEL, fs)), _full_spec((D_FF, D_MODEL)), vec, vec],
        out_specs=[frow, frow, row, row, _full_spec((1, SLAB)), vec, vec],
        out_shape=[jax.ShapeDtypeStruct((t, D_FF), BF16), jax.ShapeDtypeStruct((t, D_FF), BF16),
                   jax.ShapeDtypeStruct((t, D_MODEL), BF16), jax.ShapeDtypeStruct((t, D_MODEL), F32),
                   jax.ShapeDtypeStruct((1, SLAB), F32), jax.ShapeDtypeStruct((1, D_MODEL), F32),
                   jax.ShapeDtypeStruct((1, D_MODEL), F32)],
        compiler_params=_params(("arbitrary",)),
    )(x1, h2, target, w_up, w_down, g3, g4)


def _inproj_bwd(dgates, dqa, dka, dva, dqb_t, dkb_t, dvb_t, cq, ckv, cqn, ckvn, x, dx1, rope_ct, rope_s1t, rope_s2t,
                g1, g_q, g_kv, w_in, w_qb, w_kvb):
    t = x.shape[0]
    tm = _token_tile(t)

    def body(dgates_ref, dqa_ref, dka_ref, dva_ref, dqt_ref, dkt_ref, dvt_ref, cq_ref, ckv_ref, cqn_ref, ckvn_ref,
             x_ref, dx1_ref, ct_ref, s1t_ref, s2t_ref, g1_ref, gq_ref, gkv_ref, win_ref, wqb_ref, wkvb_ref,
             dproj_ref, dx_ref, dg1_ref, dgq_ref, dgkv_ref, dwqb_ref, dwkvb_ref, dqbrt_ref, dkvbt_ref):
        @pl.when(pl.program_id(0) == 0)
        def _():
            dwqb_ref[...] = jnp.zeros_like(dwqb_ref)
            dwkvb_ref[...] = jnp.zeros_like(dwkvb_ref)

        ct, s1t, s2t = ct_ref[...], s1t_ref[...], s2t_ref[...]
        dk_sum_t = jnp.zeros((SLAB, tm), F32)
        for hd in range(N_HEADS):
            sl = slice(hd * SLAB, (hd + 1) * SLAB)
            dqbrt_ref[sl, :] = _rope_t_bwd(dqt_ref[sl, :] * SCALE_B, ct, s1t, s2t).astype(BF16)
            dk_sum_t += dkt_ref[sl, :].astype(F32)
        dkvbt_ref[0:HM, :] = dkt_ref[...]
        dkvbt_ref[HM:2 * HM, :] = dvt_ref[...]
        dkr = _rope_t_bwd(dk_sum_t, ct, s1t, s2t).T
        dwqb_ref[...] += _dot(dqbrt_ref[...], cqn_ref[...])
        dwkvb_ref[...] += _dot(dkvbt_ref[...], ckvn_ref[...])
        dcqn = _dot(wqb_ref[...], dqbrt_ref[...]).T
        cq = cq_ref[...]
        rq = _rms_r(cq)
        _acc_rows(dgq_ref, dcqn * cq * rq)
        dcq = _rms_bwd(cq, rq, gq_ref[...], dcqn)
        dckvn = _dot(wkvb_ref[...], dkvbt_ref[...]).T
        ckv = ckv_ref[...]
        rkv = _rms_r(ckv)
        _acc_rows(dgkv_ref, dckvn * ckv * rkv)
        dckv = _rms_bwd(ckv, rkv, gkv_ref[...], dckvn)
        dproj_ref[:, C_GATES:C_QA] = dgates_ref[...]
        dproj_ref[:, C_QA:C_KA] = dqa_ref[...]
        dproj_ref[:, C_KA:C_VA] = dka_ref[...]
        dproj_ref[:, C_VA:C_CQ] = dva_ref[...]
        dproj_ref[:, C_CQ:C_CKV] = dcq.astype(BF16)
        dproj_ref[:, C_CKV:C_KR] = dckv.astype(BF16)
        dproj_ref[:, C_KR:D_IN_PAD] = dkr.astype(BF16)
        dh = _dot_nt(dproj_ref[...], win_ref[...])
        xv = x_ref[...]
        r1 = _rms_r(xv)
        _acc_rows(dg1_ref, dh * xv * r1)
        dx_ref[...] = dx1_ref[...] + _rms_bwd(xv, r1, g1_ref[...], dh)

    kvw = N_KV_A * SLAB
    row = _row_spec(tm, D_MODEL)
    hm = _row_spec(tm, HM)
    hmt = _col_spec(HM, tm)
    tab = _col_spec(SLAB, tm)
    return pl.pallas_call(
        body, name="inproj_bwd", grid=(t // tm,),
        in_specs=[_row_spec(tm, 2 * D_MODEL), hm, _row_spec(tm, kvw), _row_spec(tm, kvw), hmt, hmt, hmt,
                  _row_spec(tm, Q_LORA), _row_spec(tm, KV_LORA), _row_spec(tm, Q_LORA), _row_spec(tm, KV_LORA),
                  row, row, tab, tab, tab,
                  _full_spec((1, D_MODEL)), _full_spec((1, Q_LORA)), _full_spec((1, KV_LORA)),
                  _full_spec((D_MODEL, D_IN_PAD)), _full_spec((Q_LORA, HM)), _full_spec((KV_LORA, 2 * HM))],
        out_specs=[_row_spec(tm, D_IN_PAD), row,
                   _full_spec((1, D_MODEL)), _full_spec((1, Q_LORA)), _full_spec((1, KV_LORA)),
                   _full_spec((HM, Q_LORA)), _full_spec((2 * HM, KV_LORA))],
        out_shape=[jax.ShapeDtypeStruct((t, D_IN_PAD), BF16), jax.ShapeDtypeStruct((t, D_MODEL), F32),
                   jax.ShapeDtypeStruct((1, D_MODEL), F32), jax.ShapeDtypeStruct((1, Q_LORA), F32),
                   jax.ShapeDtypeStruct((1, KV_LORA), F32),
                   jax.ShapeDtypeStruct((HM, Q_LORA), F32), jax.ShapeDtypeStruct((2 * HM, KV_LORA), F32)],
        scratch_shapes=[pltpu.VMEM((HM, tm), BF16), pltpu.VMEM((2 * HM, tm), BF16)],
        compiler_params=_params(("arbitrary",)),
    )(dgates, dqa, dka, dva, dqb_t, dkb_t, dvb_t, cq, ckv, cqn, ckvn, x, dx1, rope_ct, rope_s1t, rope_s2t,
      g1, g_q, g_kv, w_in, w_qb, w_kvb)


def _matmul_tn(a, b, name, out_dtype=F32, n_shards=1):
    t, k = a.shape
    n = b.shape[1]
    bt = min(t, 512)
    bn = min(n, 2048)
    bk = min(k, 2048 * 1024 // bn)
    ns = n // n_shards
    per_block = bn // ns
    steps = t // bt

    def body(a_ref, b_ref, o_ref, acc):
        s = pl.program_id(2)

        @pl.when(s == 0)
        def _():
            acc[...] = jnp.zeros_like(acc)

        acc[...] += _dot_tn(a_ref[...], b_ref[...])

        @pl.when(s == steps - 1)
        def _():
            if n_shards > 1:
                for p in range(per_block):
                    o_ref[p] = acc[:, p * ns:(p + 1) * ns].astype(out_dtype)
            else:
                o_ref[...] = acc[...].astype(out_dtype)

    if n_shards > 1:
        out_spec = pl.BlockSpec((per_block, bk, ns), lambda i, j, s: (j, i, 0))
        out_shape = jax.ShapeDtypeStruct((n_shards, k, ns), out_dtype)
    else:
        out_spec = pl.BlockSpec((bk, bn), lambda i, j, s: (i, j))
        out_shape = jax.ShapeDtypeStruct((k, n), out_dtype)
    return pl.pallas_call(
        body, name=name, grid=(k // bk, n // bn, steps),
        in_specs=[pl.BlockSpec((bt, bk), lambda i, j, s: (s, i)), pl.BlockSpec((bt, bn), lambda i, j, s: (s, j))],
        out_specs=out_spec, out_shape=out_shape, scratch_shapes=[pltpu.VMEM((bk, bn), F32)],
        compiler_params=_params(("parallel", "parallel", "arbitrary")),
    )(a, b)


def _two_level_gather(srcs, dsts, send_sems, recv_sems, local_sems):
    n = len(srcs)
    x, y, c = _mesh_pos()
    me, sibling = (x, y, c), (x, y, 1 - c)
    chips = [(1 - x, y), (x, 1 - y), (1 - x, 1 - y)]

    def slot(a, px, py, pc):
        return dsts[a].at[4 * px + 2 * py + pc]

    def copy(a, k, block, to, src=None):
        return pltpu.make_async_remote_copy(
            src_ref=slot(a, *block) if src is None else src, dst_ref=slot(a, *block),
            send_sem=send_sems.at[(N_DEV - 1) * a + k], recv_sem=recv_sems.at[(N_DEV - 1) * a + k],
            device_id=to, device_id_type=pl.DeviceIdType.MESH)

    def own_copies():
        mine = [pltpu.make_async_copy(srcs[a], slot(a, *me), local_sems.at[a]) for a in range(n)]
        first = []
        for a in range(n):
            first.append(copy(a, 0, me, sibling, src=srcs[a]))
            first += [copy(a, 1 + j, me, (*chip, c), src=srcs[a]) for j, chip in enumerate(chips)]
        return mine, first

    def start():
        mine, first = own_copies()
        for cp in mine + first:
            cp.start()

    def finish():
        mine, first = own_copies()
        passed = []
        for j, chip in enumerate(chips):
            for a in range(n):
                copy(a, 1 + j, (*chip, c), me).wait_recv()
                passed.append(copy(a, 4 + j, (*chip, c), sibling))
                passed[-1].start()
        for a in range(n):
            copy(a, 0, sibling, me).wait_recv()
        for j, chip in enumerate(chips):
            for a in range(n):
                copy(a, 4 + j, (*chip, 1 - c), me).wait_recv()
        for cp in first + passed:
            cp.wait_send()
        for cp in mine:
            cp.wait()

    return start, finish


def _exchange_grads(slices, small):
    n = len(slices)

    def body(*refs):
        srcs, s_ref = refs[:n], refs[n]
        dsts, s_dst = refs[n + 1:2 * n + 1], refs[2 * n + 1]
        sems = refs[2 * n + 2:]
        parts = _direct_copies(srcs, dsts, *sems, False)
        smalls = _direct_copies([s_ref], [s_dst], *sems, True, sem_base=n)
        _start_copies(*parts)
        _start_copies(*smalls)
        _wait_copies(*parts)
        _wait_copies(*smalls)

    outs = pl.pallas_call(
        body, name="exchange_grads",
        out_shape=[jax.ShapeDtypeStruct(a.shape, a.dtype) for a in slices]
        + [jax.ShapeDtypeStruct((N_DEV,) + small.shape, small.dtype)],
        in_specs=[ANY_SPEC] * (n + 1), out_specs=[ANY_SPEC] * (n + 1), scratch_shapes=_exchange_scratch(n + 1),
    )(*slices, small)
    return list(outs[:n]), outs[n]


def _adamw(parts, w, m, v, name):
    _, k, n = parts.shape
    bk = min(k, ADAM_ROWS)
    c1 = 1.0 - ADAM_B1 ** ADAM_STEP
    c2 = 1.0 - ADAM_B2 ** ADAM_STEP

    def body(p_ref, w_ref, m_ref, v_ref, g_ref, d_ref, mo_ref, vo_ref):
        g = p_ref[0].astype(F32)
        for s in range(1, N_DEV):
            g = g + p_ref[s].astype(F32)
        g_ref[0] = g
        m_new = ADAM_B1 * m_ref[0] + (1.0 - ADAM_B1) * g
        v_new = ADAM_B2 * v_ref[0] + (1.0 - ADAM_B2) * (g * g)
        mo_ref[0] = m_new
        vo_ref[0] = v_new
        m_hat = m_new / c1
        v_hat = v_new / c2
        d_ref[0] = -ADAM_LR * (m_hat / (jnp.sqrt(v_hat) + ADAM_EPS) + ADAM_WD * w_ref[0])

    blk = pl.BlockSpec((1, bk, n), lambda i: (0, i, 0))
    out = jax.ShapeDtypeStruct((1, k, n), F32)
    return pl.pallas_call(
        body, name=name, grid=(k // bk,),
        in_specs=[pl.BlockSpec((N_DEV, bk, n), lambda i: (0, i, 0)), blk, blk, blk],
        out_specs=[blk] * 4, out_shape=[out] * 4,
        compiler_params=_params(("parallel",)),
    )(parts, w, m, v)


def _adamw_small(parts, w, m, v):
    k = len(SMALL_LAYOUT)
    c1 = 1.0 - ADAM_B1 ** ADAM_STEP
    c2 = 1.0 - ADAM_B2 ** ADAM_STEP

    def body(p_ref, *refs):
        w_refs, m_refs, v_refs, outs = refs[:k], refs[k:2 * k], refs[2 * k:3 * k], refs[3 * k:]
        total = p_ref[0]
        for s in range(1, N_DEV):
            total = total + p_ref[s]
        for i, (_, row, off, width) in enumerate(SMALL_LAYOUT):
            g = total[row:row + 1, off:off + width]
            m_new = ADAM_B1 * m_refs[i][...] + (1.0 - ADAM_B1) * g
            v_new = ADAM_B2 * v_refs[i][...] + (1.0 - ADAM_B2) * (g * g)
            outs[4 * i][...] = g
            outs[4 * i + 1][...] = -ADAM_LR * ((m_new / c1) / (jnp.sqrt(v_new / c2) + ADAM_EPS)
                                               + ADAM_WD * w_refs[i][...])
            outs[4 * i + 2][...] = m_new
            outs[4 * i + 3][...] = v_new
        outs[4 * k][...] = total[SMALL_LOSS_ROW:SMALL_LOSS_ROW + 1, SMALL_LOSS_OFF:SMALL_LOSS_OFF + 1]

    names = [name for name, *_ in SMALL_LAYOUT]
    out_shape = [jax.ShapeDtypeStruct(w[name].shape, F32) for name in names for _ in range(4)]
    outs = pl.pallas_call(
        body, name="adamw_small", out_shape=out_shape + [jax.ShapeDtypeStruct((1, 1), F32)],
    )(parts, *[w[n] for n in names], *[m[n] for n in names], *[v[n] for n in names])
    return {name: tuple(outs[4 * i:4 * i + 4]) for i, name in enumerate(names)}, outs[4 * k]


def _pad_heads_cols(w, heads, width):
    k = w.shape[0]
    w = w.reshape(k, heads, width)
    return jnp.pad(w, ((0, 0), (0, 0), (0, SLAB - width))).reshape(k, heads * SLAB)


def _unpad_heads_cols(w, heads, width):
    k = w.shape[0]
    return w.reshape(k, heads, SLAB)[:, :, :width].reshape(k, heads * width)


def _pad_heads_rows(w, heads, width):
    n = w.shape[1]
    w = w.reshape(heads, width, n)
    return jnp.pad(w, ((0, 0), (0, SLAB - width), (0, 0))).reshape(heads * SLAB, n)


def _unpad_heads_rows(w, heads, width):
    n = w.shape[1]
    return w.reshape(heads, SLAB, n)[:, :width, :].reshape(heads * width, n)


def _pad_w_in(w_in):
    o = 2 * D_MODEL
    qa = _pad_heads_cols(w_in[:, o:o + 512], N_HEADS, HEAD_A)
    ka = _pad_heads_cols(w_in[:, o + 512:o + 640], N_KV_A, HEAD_A)
    va = _pad_heads_cols(w_in[:, o + 640:o + 768], N_KV_A, HEAD_A)
    kr = jnp.pad(w_in[:, o + 1152:o + 1184], ((0, 0), (QK_NOPE, SLAB - QK_NOPE - QK_ROPE)))
    return jnp.concatenate([w_in[:, :o], qa, ka, va, w_in[:, o + 768:o + 1152], kr], axis=1)


def _unpad_w_in(w):
    qa = _unpad_heads_cols(w[:, C_QA:C_KA], N_HEADS, HEAD_A)
    ka = _unpad_heads_cols(w[:, C_KA:C_VA], N_KV_A, HEAD_A)
    va = _unpad_heads_cols(w[:, C_VA:C_CQ], N_KV_A, HEAD_A)
    kr = w[:, C_KR + QK_NOPE:C_KR + QK_NOPE + QK_ROPE]
    return jnp.concatenate([w[:, :C_QA], qa, ka, va, w[:, C_CQ:C_KR], kr], axis=1)


def _pad_w_kvb(w_kvb):
    w = w_kvb.reshape(KV_LORA, N_HEADS, QK_NOPE + V_DIM_B)
    k = jnp.pad(w[:, :, :QK_NOPE], ((0, 0), (0, 0), (0, SLAB - QK_NOPE))).reshape(KV_LORA, HM)
    v = jnp.pad(w[:, :, QK_NOPE:], ((0, 0), (0, 0), (0, SLAB - V_DIM_B))).reshape(KV_LORA, HM)
    return jnp.concatenate([k, v], axis=1)


def _unpad_w_kvb(w):
    k = w[:, :HM].reshape(KV_LORA, N_HEADS, SLAB)[:, :, :QK_NOPE]
    v = w[:, HM:].reshape(KV_LORA, N_HEADS, SLAB)[:, :, :V_DIM_B]
    return jnp.concatenate([k, v], axis=2).reshape(KV_LORA, N_HEADS * (QK_NOPE + V_DIM_B))


def _col_shards(w):
    k, n = w.shape
    return w.reshape(k, N_DEV, n // N_DEV).transpose(1, 0, 2)


def _from_col_shards(s):
    _, k, ns = s.shape
    return s.transpose(1, 0, 2).reshape(k, N_DEV * ns)


def _freq_row():
    freqs = ROPE_THETA ** (-jnp.arange(0, QK_ROPE, 2, dtype=F32) / QK_ROPE)
    return jnp.concatenate([jnp.zeros((QK_NOPE,), F32), freqs, freqs,
                            jnp.zeros((SLAB - QK_NOPE - QK_ROPE,), F32)]).reshape(1, SLAB)


SMALL_D_ROWS = ("pre_norm_mix", "post_norm_mix", "pre_norm_mlp", "post_norm_mlp")
SMALL_LAYOUT = tuple((name, i, 0, D_MODEL) for i, name in enumerate(SMALL_D_ROWS)) + (
    ("q_a_norm", 4, 0, Q_LORA), ("kv_a_norm", 4, 256, KV_LORA), ("sinks", 4, 384, N_HEADS))
SMALL_LOSS_ROW, SMALL_LOSS_OFF = 4, 512


def _pack_small(vals):
    row4 = jnp.concatenate([vals["q_a_norm"].reshape(-1), vals["kv_a_norm"].reshape(-1), vals["sinks"].reshape(-1),
                            jnp.zeros((SMALL_LOSS_OFF - 392,), F32), vals["loss"].reshape(-1),
                            jnp.zeros((1024 - SMALL_LOSS_OFF - 1,), F32)])
    rows = [vals[n].reshape(1024) for n in SMALL_D_ROWS] + [row4]
    return jnp.concatenate([jnp.stack(rows), jnp.zeros((SMALL_ROWS - 5, 1024), F32)], axis=0)


WEIGHT_ORDER = ("pre_norm_mix", "w_in", "q_a_norm", "w_q_b", "kv_a_norm", "w_kv_b", "sinks", "w_o_a", "w_o_b",
                "w_out", "post_norm_mix", "pre_norm_mlp", "w_up", "w_down", "post_norm_mlp")


def kernel(x, positions, pre_norm_mix, w_in, q_a_norm, w_q_b, kv_a_norm, w_kv_b, sinks, w_o_a, w_o_b, w_out, post_norm_mix, pre_norm_mlp, w_up, w_down, post_norm_mlp, loss_target, m_pre_norm_mix, m_w_in, m_q_a_norm, m_w_q_b, m_kv_a_norm, m_w_kv_b, m_sinks, m_w_o_a, m_w_o_b, m_w_out, m_post_norm_mix, m_pre_norm_mlp, m_w_up, m_w_down, m_post_norm_mlp, v_pre_norm_mix, v_w_in, v_q_a_norm, v_w_q_b, v_kv_a_norm, v_w_kv_b, v_sinks, v_w_o_a, v_w_o_b, v_w_out, v_post_norm_mix, v_pre_norm_mlp, v_w_up, v_w_down, v_post_norm_mlp):
    weights = dict(pre_norm_mix=pre_norm_mix, w_in=w_in, q_a_norm=q_a_norm, w_q_b=w_q_b, kv_a_norm=kv_a_norm,
                   w_kv_b=w_kv_b, sinks=sinks, w_o_a=w_o_a, w_o_b=w_o_b, w_out=w_out, post_norm_mix=post_norm_mix,
                   pre_norm_mlp=pre_norm_mlp, w_up=w_up, w_down=w_down, post_norm_mlp=post_norm_mlp)
    m_in = dict(pre_norm_mix=m_pre_norm_mix, w_in=m_w_in, q_a_norm=m_q_a_norm, w_q_b=m_w_q_b, kv_a_norm=m_kv_a_norm,
                w_kv_b=m_w_kv_b, sinks=m_sinks, w_o_a=m_w_o_a, w_o_b=m_w_o_b, w_out=m_w_out,
                post_norm_mix=m_post_norm_mix, pre_norm_mlp=m_pre_norm_mlp, w_up=m_w_up, w_down=m_w_down,
                post_norm_mlp=m_post_norm_mlp)
    v_in = dict(pre_norm_mix=v_pre_norm_mix, w_in=v_w_in, q_a_norm=v_q_a_norm, w_q_b=v_w_q_b, kv_a_norm=v_kv_a_norm,
                w_kv_b=v_w_kv_b, sinks=v_sinks, w_o_a=v_w_o_a, w_o_b=v_w_o_b, w_out=v_w_out,
                post_norm_mix=v_post_norm_mix, pre_norm_mlp=v_pre_norm_mlp, w_up=v_w_up, w_down=v_w_down,
                post_norm_mlp=v_post_norm_mlp)

    xs, pos, target = x[0], positions[0], loss_target[0]
    t = xs.shape[0]
    pos_col = pos.reshape(t, 1)
    pos_row = pos.reshape(1, t)
    g1, g2, g3, g4 = (weights[n] for n in SMALL_D_ROWS)
    g_q, g_kv = q_a_norm, kv_a_norm
    sink_vec = sinks.reshape(N_HEADS)
    shard = {n: weights[n][0].astype(BF16) for n in EARLY + LATE}

    tables, (e_in, e_qb, e_kvb) = _rope_tables(pos_col, _freq_row(), [shard[n] for n in EARLY])
    w_in_p = _pad_w_in(_from_col_shards(e_in))
    w_qb = _pad_heads_cols(_from_col_shards(e_qb), N_HEADS, QK_NOPE + QK_ROPE)
    w_kvb = _pad_w_kvb(_from_col_shards(e_kvb))

    (h, gates, qa, ka, va, cq, ckv, cqn, ckvn, kb, vb, qt, kt, vt) = _inproj_fwd(
        xs, g1, w_in_p, g_q, g_kv, w_kvb, w_qb.T, w_kvb[:, :HM].T, w_kvb[:, HM:].T, w_in_p[:, C_KR:].T, tables)
    out_a, lse_a = _swa_fwd(qa, ka, va, pos_col, pos_row, sink_vec)
    out_b, out_b_t, lse_b, (l_oa, l_ob, l_out, w_up_s, l_down) = _mla_fwd(qt, kb, vt, [shard[n] for n in LATE])
    w_oa = _pad_heads_rows(_from_col_shards(l_oa), N_HEADS, HEAD_A)
    w_ob = _pad_heads_rows(_from_col_shards(l_ob), N_HEADS, V_DIM_B)
    w_out_f = l_out.reshape(D_MODEL, D_MODEL)
    w_down_f = l_down.reshape(D_FF, D_MODEL)

    oa_p, ob_p, merged, y, x1, h2 = _merge_fwd(out_a, out_b, gates, xs, w_oa, w_ob, w_out_f, g2, g3)
    a, du, dy2, dx1, loss, dg3, dg4 = _mlp_fwd_bwd(x1, h2, target, w_up_s, w_down_f, g3, g4)
    (dgates, d_oa, d_ob_t, delta_b, dg2, dw_oa, dw_ob, dw_out) = _merge_bwd(
        dx1, y, gates, oa_p, ob_p, out_a, out_b, out_b_t, merged, w_oa, w_ob, w_out_f, g2)
    late_slices = [
        _col_shards(_unpad_heads_rows(dw_oa, N_HEADS, HEAD_A)).astype(BF16),
        _col_shards(_unpad_heads_rows(dw_ob, N_HEADS, V_DIM_B)).astype(BF16),
        dw_out.astype(BF16).reshape(N_DEV, D_MODEL // N_DEV, D_MODEL),
        _matmul_tn(h2, du, "dw_up", BF16, N_DEV),
        _matmul_tn(a, dy2, "dw_down", BF16).reshape(N_DEV, D_FF // N_DEV, D_MODEL),
    ]
    dqa, dka, dva, dsink = _swa_bwd(qa, ka, va, out_a, d_oa, lse_a, pos_col, pos_row, sink_vec)
    dqb_t, dkb_t, dvb_t, late_parts = _mla_bwd(qt, kb, kt, vb, d_ob_t, lse_b, delta_b, late_slices)
    dproj, dx, dg1, dgq, dgkv, dw_qb_t, dw_kvb_t = _inproj_bwd(
        dgates, dqa, dka, dva, dqb_t, dkb_t, dvb_t, cq, ckv, cqn, ckvn, xs, dx1, *tables[3:], g1, g_q, g_kv,
        w_in_p, w_qb, w_kvb)
    early_slices = [
        _col_shards(_unpad_w_in(_matmul_tn(h, dproj, "dw_in"))).astype(BF16),
        _col_shards(_unpad_heads_cols(dw_qb_t.T, N_HEADS, QK_NOPE + QK_ROPE)).astype(BF16),
        _col_shards(_unpad_w_kvb(dw_kvb_t.T)).astype(BF16),
    ]
    small_grads = {"pre_norm_mix": dg1, "post_norm_mix": dg2, "pre_norm_mlp": dg3, "post_norm_mlp": dg4,
                   "q_a_norm": dgq, "kv_a_norm": dgkv, "sinks": dsink.reshape(N_HEADS, BLOCK).sum(axis=1),
                   "loss": loss[0, 0:1]}
    early_parts, s_parts = _exchange_grads(early_slices, _pack_small(small_grads))

    updates = {}
    for name, parts in zip(EARLY + LATE, early_parts + late_parts):
        outs = _adamw(parts, weights[name], m_in[name], v_in[name], "adamw_" + name)
        for kind, arr in zip(("g", "d", "m", "v"), outs):
            updates[kind, name] = arr
    small_out, loss_sum = _adamw_small(s_parts, weights, m_in, v_in)
    for name, outs in small_out.items():
        for kind, arr in zip(("g", "d", "m", "v"), outs):
            updates[kind, name] = arr
    results = [updates[kind, name] for kind in ("g", "d", "m", "v") for name in WEIGHT_ORDER]
    return (loss_sum.reshape(()), dx[None], *results)
```

```python
import functools

import numpy as np
import jax
import jax.numpy as jnp
from jax import lax
from jax.experimental import pallas as pl
from jax.experimental.pallas import tpu as pltpu

F32 = jnp.float32
BF16 = jnp.bfloat16

D_MODEL = 1024
D_FF = 4096
N_HEADS = 8
N_KV_A = 2
GROUP_A = N_HEADS // N_KV_A
HEAD_A = 64
QK_NOPE = 64
QK_ROPE = 32
V_DIM_B = 64
Q_LORA = 256
KV_LORA = 128
BLOCK = 128
SLAB = 128
ROPE_THETA = 10000.0
EPS = 1e-6
N_DEV = 8
NEG = -1e30

SCALE_A = HEAD_A ** -0.5
SCALE_B = (QK_NOPE + QK_ROPE) ** -0.5
LOG2E = 1.4426950408889634
SCORE_B = SCALE_B * LOG2E
MLA_HEADS_PER_STEP = 4
MLA_FWD_HEADS_PER_STEP = 8
Q_HEAD_B = QK_NOPE + QK_ROPE
ONES_ROWS = 16
SLOPES_A = tuple(2.0 ** (-8.0 * (h + 1) / N_HEADS) for h in range(N_HEADS))

ADAM_LR = 0.001
ADAM_B1 = 0.9
ADAM_B2 = 0.999
ADAM_EPS = 1e-08
ADAM_WD = 0.01
ADAM_STEP = 10

HM = N_HEADS * SLAB
C_GATES = 0
C_QA = 2 * D_MODEL
C_KA = C_QA + HM
C_VA = C_KA + N_KV_A * SLAB
C_CQ = C_VA + N_KV_A * SLAB
C_CKV = C_CQ + Q_LORA
C_KR = C_CKV + KV_LORA
D_IN_PAD = C_KR + SLAB

VMEM_LIMIT = 56 * 1024 * 1024

EARLY = ("w_in", "w_q_b", "w_kv_b")
LATE = ("w_o_a", "w_o_b", "w_out", "w_up", "w_down")
ADAM_ROWS = 256
SMALL_ROWS = 8


def _token_tile(t):
    return min(256, t)


def _attn_tile(t):
    return 512 if t >= 2048 else 128


def _params(sem, vmem=VMEM_LIMIT):
    return pltpu.CompilerParams(dimension_semantics=sem, vmem_limit_bytes=vmem)


def _dot(a, b):
    return jnp.dot(a, b, preferred_element_type=F32)


def _dot_nt(a, b):
    return lax.dot_general(a, b, (((1,), (1,)), ((), ())), preferred_element_type=F32)


def _dot_tn(a, b):
    return lax.dot_general(a, b, (((0,), (0,)), ((), ())), preferred_element_type=F32)


def _rms_r(x):
    return lax.rsqrt(jnp.mean(x * x, axis=-1, keepdims=True) + EPS)


def _rms_bwd(x, r, g, dy):
    t = dy * g
    return r * t - x * (r * r * r) * jnp.mean(x * t, axis=-1, keepdims=True)


def _sigmoid(x):
    return 1.0 / (1.0 + jnp.exp(-x))


def _rope(x, c, s1, s2):
    return x * c + pltpu.roll(x, SLAB - 16, 1) * s1 + pltpu.roll(x, 16, 1) * s2


def _rope_bwd(d, c, s1, s2):
    return d * c + pltpu.roll(d * s1, 16, 1) + pltpu.roll(d * s2, SLAB - 16, 1)


def _roll_rows(x, shift):
    return jnp.concatenate([x[-shift:], x[:-shift]], axis=0)


def _rope_t(x, c, s1, s2):
    return x * c + _roll_rows(x, SLAB - 16) * s1 + _roll_rows(x, 16) * s2


def _rope_t_bwd(d, c, s1, s2):
    return d * c + _roll_rows(d * s1, 16) + _roll_rows(d * s2, SLAB - 16)


def _plant_rows(slab, row, vals):
    hi = vals.astype(BF16).astype(F32)
    lo = (vals - hi).astype(BF16).astype(F32)
    idx = lax.broadcasted_iota(jnp.int32, slab.shape, 0)
    return jnp.where(idx == row, -hi, jnp.where(idx == row + 1, -lo, slab))


def _row_spec(tm, n):
    return pl.BlockSpec((tm, n), lambda i: (i, 0))


def _col_spec(n, tm):
    return pl.BlockSpec((n, tm), lambda i: (0, i))


def _full_spec(shape):
    nd = len(shape)
    return pl.BlockSpec(shape, lambda i: (0,) * nd, pipeline_mode=pl.Buffered(1))


def _acc_rows(ref, val):
    @pl.when(pl.program_id(0) == 0)
    def _():
        ref[...] = jnp.zeros_like(ref)
    ref[...] += jnp.sum(val, axis=0, keepdims=True)


def _rope_tables(pos_col, freq_row, early):
    t = pos_col.shape[0]
    tm = _token_tile(t)
    n = len(early)

    def body(pos_ref, f_ref, *rest):
        shard_refs, (c_ref, s1_ref, s2_ref, ct_ref, s1t_ref, s2t_ref) = rest[:n], rest[n:n + 6]
        start, finish = _two_level_gather(shard_refs, rest[n + 6:2 * n + 6], *rest[2 * n + 6:])
        pl.when(pl.program_id(0) == 0)(start)
        ang = pos_ref[...].astype(F32) * f_ref[...]
        lane = lax.broadcasted_iota(jnp.int32, ang.shape, 1)
        s = jnp.sin(ang)
        c = jnp.cos(ang)
        s1 = jnp.where((lane >= 64) & (lane < 80), -s, 0.0)
        s2 = jnp.where((lane >= 80) & (lane < 96), s, 0.0)
        c_ref[...], s1_ref[...], s2_ref[...] = c, s1, s2
        ct_ref[...], s1t_ref[...], s2t_ref[...] = c.T, s1.T, s2.T
        pl.when(pl.program_id(0) == t // tm - 1)(finish)

    tab = jax.ShapeDtypeStruct((t, SLAB), F32)
    tabt = jax.ShapeDtypeStruct((SLAB, t), F32)
    outs = pl.pallas_call(
        body, name="rope_tables", grid=(t // tm,),
        in_specs=[_row_spec(tm, 1), _full_spec((1, SLAB))] + [ANY_SPEC] * n,
        out_specs=[_row_spec(tm, SLAB)] * 3 + [_col_spec(SLAB, tm)] * 3 + [ANY_SPEC] * n,
        out_shape=[tab] * 3 + [tabt] * 3 + [jax.ShapeDtypeStruct((N_DEV,) + a.shape, a.dtype) for a in early],
        scratch_shapes=_exchange_scratch(n),
        compiler_params=_params(("arbitrary",)),
    )(pos_col, freq_row, *early)
    return outs[:6], outs[6:]


def _inproj_fwd(x, g1, w_in, g_q, g_kv, w_kvb, w_qb_t, w_kb_t, w_vb_t, w_kr_t, tables):
    t = x.shape[0]
    tm = _token_tile(t)

    def body(x_ref, g1_ref, win_ref, gq_ref, gkv_ref, wkvb_ref, wqbt_ref, wkbt_ref, wvbt_ref, wkrt_ref,
             c_ref, s1_ref, s2_ref, ct_ref, s1t_ref, s2t_ref,
             h_ref, gates_ref, qa_ref, ka_ref, va_ref, cq_ref, ckv_ref, cqn_ref, ckvn_ref,
             kb_ref, vb_ref, qt_ref, kt_ref, vt_ref):
        xv = x_ref[...]
        h = (xv * _rms_r(xv) * g1_ref[...]).astype(BF16)
        h_ref[...] = h
        proj = _dot(h, win_ref[...])
        gates_ref[...] = proj[:, C_GATES:C_QA].astype(BF16)
        qa_ref[...] = proj[:, C_QA:C_KA].astype(BF16)
        ka_ref[...] = proj[:, C_KA:C_VA].astype(BF16)
        va_ref[...] = proj[:, C_VA:C_CQ].astype(BF16)
        cq = proj[:, C_CQ:C_CKV]
        ckv = proj[:, C_CKV:C_KR]
        kr = proj[:, C_KR:D_IN_PAD]
        cq_ref[...] = cq
        ckv_ref[...] = ckv
        cqn = (cq * _rms_r(cq) * gq_ref[...]).astype(BF16)
        ckvn = (ckv * _rms_r(ckv) * gkv_ref[...]).astype(BF16)
        cqn_ref[...] = cqn
        ckvn_ref[...] = ckvn
        c, s1, s2 = c_ref[...], s1_ref[...], s2_ref[...]
        kvb = _dot(ckvn, wkvb_ref[...])
        kr_rot = _rope(kr, c, s1, s2)
        ct, s1t, s2t = ct_ref[...], s1t_ref[...], s2t_ref[...]
        q_t = _dot_nt(wqbt_ref[...], cqn)
        k_t = _dot_nt(wkbt_ref[...], ckvn)
        kr_t = _rope_t(_dot_nt(wkrt_ref[...], h), ct, s1t, s2t)
        k_lane = lax.broadcasted_iota(jnp.int32, (1, SLAB), 1)
        k_ones = jnp.where((k_lane == Q_HEAD_B) | (k_lane == Q_HEAD_B + 1), 1.0, 0.0)
        for hd in range(N_HEADS):
            sl = slice(hd * SLAB, (hd + 1) * SLAB)
            kb_ref[:, sl] = (kvb[:, sl] + kr_rot + k_ones).astype(BF16)
            qt_ref[sl, :] = (_rope_t(q_t[sl, :], ct, s1t, s2t) * SCORE_B).astype(BF16)
            kt_ref[sl, :] = (k_t[sl, :] + kr_t).astype(BF16)
        v_lane = lax.broadcasted_iota(jnp.int32, (1, HM), 1) & (SLAB - 1)
        v_ones = jnp.where((v_lane == V_DIM_B) | (v_lane == V_DIM_B + 1), 1.0, 0.0)
        vb_ref[...] = (kvb[:, HM:2 * HM] + v_ones).astype(BF16)
        pad_row = lax.broadcasted_iota(jnp.int32, (HM, 1), 0) & (SLAB - 1)
        ones_rows = jnp.where((pad_row >= V_DIM_B) & (pad_row < V_DIM_B + ONES_ROWS), 1.0, 0.0)
        vt_ref[...] = (_dot_nt(wvbt_ref[...], ckvn) + ones_rows).astype(BF16)

    def sds(n, dt):
        return jax.ShapeDtypeStruct((t, n), dt)

    outs = [(D_MODEL, BF16), (2 * D_MODEL, BF16), (HM, BF16), (N_KV_A * SLAB, BF16), (N_KV_A * SLAB, BF16),
            (Q_LORA, F32), (KV_LORA, F32), (Q_LORA, BF16), (KV_LORA, BF16), (HM, BF16), (HM, BF16)]
    tab, tabt = _row_spec(tm, SLAB), _col_spec(SLAB, tm)
    return pl.pallas_call(
        body, name="inproj_fwd", grid=(t // tm,),
        in_specs=[_row_spec(tm, D_MODEL), _full_spec((1, D_MODEL)), _full_spec((D_MODEL, D_IN_PAD)),
                  _full_spec((1, Q_LORA)), _full_spec((1, KV_LORA)), _full_spec((KV_LORA, 2 * HM)),
                  _full_spec((HM, Q_LORA)), _full_spec((HM, KV_LORA)), _full_spec((HM, KV_LORA)),
                  _full_spec((SLAB, D_MODEL)), tab, tab, tab, tabt, tabt, tabt],
        out_specs=[_row_spec(tm, n) for n, _ in outs] + [_col_spec(HM, tm)] * 3,
        out_shape=[sds(n, dt) for n, dt in outs] + [jax.ShapeDtypeStruct((HM, t), BF16)] * 3,
        compiler_params=_params(("parallel",)),
    )(x, g1, w_in, g_q, g_kv, w_kvb, w_qb_t, w_kb_t, w_vb_t, w_kr_t, *tables)


def _tile_group(a):
    return jnp.concatenate([a] * GROUP_A, axis=1)


def _swa_masks():
    row = lax.broadcasted_iota(jnp.int32, (BLOCK, GROUP_A * BLOCK), 0)
    col = lax.broadcasted_iota(jnp.int32, (BLOCK, GROUP_A * BLOCK), 1) & (BLOCK - 1)
    return row <= col, row > col


def _heads_beside(ref, g):
    return jnp.concatenate([ref[:, (g * GROUP_A + hh) * SLAB:(g * GROUP_A + hh + 1) * SLAB].T
                            for hh in range(GROUP_A)], axis=1)


def _rows_beside(ref, g):
    return jnp.concatenate([ref[g * GROUP_A + hh] for hh in range(GROUP_A)], axis=1)


def _swa_rows(sinks):
    slopes = jnp.repeat(jnp.asarray(SLOPES_A, F32).reshape(N_KV_A, GROUP_A, 1), BLOCK, axis=2)
    sink_rows = jnp.repeat(sinks.reshape(N_KV_A, GROUP_A, 1), BLOCK, axis=2)
    return slopes.reshape(N_KV_A, 1, GROUP_A * BLOCK), sink_rows.reshape(N_KV_A, 1, GROUP_A * BLOCK)


def _swa_fwd(qa, ka, va, pos_col, pos_row, sinks):
    t = qa.shape[0]
    nb = t // BLOCK
    gw = GROUP_A * BLOCK
    slope_rows, sink_rows = _swa_rows(sinks)

    def body(q_ref, kc_ref, kp_ref, vc_ref, vp_ref, pkc_ref, pkp_ref, pq_ref, slope_ref, sink_ref, o_ref, l_ref):
        i = pl.program_id(0)
        pq = pq_ref[...]
        dist_c = _tile_group(jnp.abs(pkc_ref[...] - pq).astype(F32))
        dist_p = _tile_group(jnp.abs(pkp_ref[...] - pq).astype(F32))
        mask_c, older = _swa_masks()
        mask_p = jnp.logical_and(older, i > 0)
        for g in range(N_KV_A):
            gs = slice(g * SLAB, (g + 1) * SLAB)
            x = _heads_beside(q_ref, g)
            slope, sink = slope_ref[g], sink_ref[g]
            s_c = jnp.where(mask_c, _dot(kc_ref[:, gs], x) * SCALE_A - slope * dist_c, NEG)
            s_p = jnp.where(mask_p, _dot(kp_ref[:, gs], x) * SCALE_A - slope * dist_p, NEG)
            m = jnp.maximum(jnp.maximum(jnp.max(s_c, axis=0, keepdims=True),
                                        jnp.max(s_p, axis=0, keepdims=True)), sink)
            e_c = jnp.exp(s_c - m)
            e_p = jnp.exp(s_p - m)
            den = jnp.sum(e_c, axis=0, keepdims=True) + jnp.sum(e_p, axis=0, keepdims=True) + jnp.exp(sink - m)
            inv = 1.0 / den
            ot = (_dot_tn(vc_ref[:, gs], (e_c * inv).astype(BF16))
                  + _dot_tn(vp_ref[:, gs], (e_p * inv).astype(BF16)))
            lse = m + jnp.log(den)
            for hh in range(GROUP_A):
                hd = g * GROUP_A + hh
                seg = slice(hh * BLOCK, (hh + 1) * BLOCK)
                o_ref[:, hd * SLAB:(hd + 1) * SLAB] = ot[:, seg].T.astype(BF16)
                l_ref[hd] = lse[:, seg]

    cur = lambda i: (i, 0)
    prev = lambda i: (jnp.maximum(i - 1, 0), 0)
    kvw = N_KV_A * SLAB
    rows = pl.BlockSpec((N_KV_A, 1, gw), lambda i: (0, 0, 0))
    return pl.pallas_call(
        body, name="swa_fwd", grid=(nb,),
        in_specs=[pl.BlockSpec((BLOCK, HM), cur),
                  pl.BlockSpec((BLOCK, kvw), cur), pl.BlockSpec((BLOCK, kvw), prev),
                  pl.BlockSpec((BLOCK, kvw), cur), pl.BlockSpec((BLOCK, kvw), prev),
                  pl.BlockSpec((BLOCK, 1), cur), pl.BlockSpec((BLOCK, 1), prev),
                  pl.BlockSpec((1, BLOCK), lambda i: (0, i)), rows, rows],
        out_specs=[pl.BlockSpec((BLOCK, HM), cur), pl.BlockSpec((N_HEADS, 1, BLOCK), lambda i: (0, 0, i))],
        out_shape=[jax.ShapeDtypeStruct((t, HM), BF16), jax.ShapeDtypeStruct((N_HEADS, 1, t), F32)],
        compiler_params=_params(("parallel",)),
    )(qa, ka, ka, va, va, pos_col, pos_col, pos_row, slope_rows, sink_rows)


def _swa_bwd(qa, ka, va, out_a, d_oa, lse, pos_col, pos_row, sinks):
    t = qa.shape[0]
    nb = t // BLOCK
    gw = GROUP_A * BLOCK
    slope_rows, sink_rows = _swa_rows(sinks)

    def body(q_ref, qn_ref, do_ref, don_ref, l_ref, ln_ref, o_ref, on_ref, kp_ref, kc_ref, vp_ref, vc_ref,
             pkp_ref, pkc_ref, pq_ref, pqn_ref, slope_ref, sink_ref, dq_ref, dk_ref, dv_ref, dsink_ref):
        j = pl.program_id(0)
        pkc, pkp = pkc_ref[...], pkp_ref[...]
        dist_cc = _tile_group(jnp.abs(pkc - pq_ref[...]).astype(F32))
        dist_cp = _tile_group(jnp.abs(pkp - pq_ref[...]).astype(F32))
        dist_nc = _tile_group(jnp.abs(pkc - pqn_ref[...]).astype(F32))
        mask_cc, older = _swa_masks()
        mask_cp = jnp.logical_and(older, j > 0)
        mask_nc = jnp.logical_and(older, j < nb - 1)

        @pl.when(j == 0)
        def _():
            dsink_ref[...] = jnp.zeros_like(dsink_ref)

        def tile(k, v, x, dox, lrow, drow, dist, mask, slope):
            s = jnp.where(mask, _dot(k, x) * SCALE_A - slope * dist, NEG)
            p = jnp.exp(s - lrow)
            ds = p * (_dot(v, dox) - drow)
            return p.astype(BF16), ds.astype(BF16)

        for g in range(N_KV_A):
            gs = slice(g * SLAB, (g + 1) * SLAB)
            kc, kp, vc, vp = kc_ref[:, gs], kp_ref[:, gs], vc_ref[:, gs], vp_ref[:, gs]
            slope, sink = slope_ref[g], sink_ref[g]
            x, xn = _heads_beside(q_ref, g), _heads_beside(qn_ref, g)
            dox, doxn = _heads_beside(do_ref, g), _heads_beside(don_ref, g)
            lrow, lrown = _rows_beside(l_ref, g), _rows_beside(ln_ref, g)
            drow = jnp.sum(dox.astype(F32) * _heads_beside(o_ref, g).astype(F32), axis=0, keepdims=True)
            drown = jnp.sum(doxn.astype(F32) * _heads_beside(on_ref, g).astype(F32), axis=0, keepdims=True)
            p_cc, ds_cc = tile(kc, vc, x, dox, lrow, drow, dist_cc, mask_cc, slope)
            _, ds_cp = tile(kp, vp, x, dox, lrow, drow, dist_cp, mask_cp, slope)
            p_nc, ds_nc = tile(kc, vc, xn, doxn, lrown, drown, dist_nc, mask_nc, slope)
            dqt = (_dot_tn(kc, ds_cc) + _dot_tn(kp, ds_cp)) * SCALE_A
            for hh in range(GROUP_A):
                hd = g * GROUP_A + hh
                dq_ref[:, hd * SLAB:(hd + 1) * SLAB] = dqt[:, hh * BLOCK:(hh + 1) * BLOCK].T.astype(BF16)
            dk_ref[:, gs] = ((_dot_nt(ds_cc, x) + _dot_nt(ds_nc, xn)) * SCALE_A).astype(BF16)
            dv_ref[:, gs] = (_dot_nt(p_cc, dox) + _dot_nt(p_nc, doxn)).astype(BF16)
            dsink_ref[g] -= jnp.exp(sink - lrow) * drow

    cur = lambda j: (j, 0)
    prev = lambda j: (jnp.maximum(j - 1, 0), 0)
    nxt = lambda j: (jnp.minimum(j + 1, nb - 1), 0)
    cur3 = lambda j: (0, 0, j)
    nxt3 = lambda j: (0, 0, jnp.minimum(j + 1, nb - 1))
    kvw = N_KV_A * SLAB
    rows = pl.BlockSpec((N_KV_A, 1, gw), lambda j: (0, 0, 0))
    stat = lambda im: pl.BlockSpec((N_HEADS, 1, BLOCK), im)
    return pl.pallas_call(
        body, name="swa_bwd", grid=(nb,),
        in_specs=[pl.BlockSpec((BLOCK, HM), cur), pl.BlockSpec((BLOCK, HM), nxt),
                  pl.BlockSpec((BLOCK, HM), cur), pl.BlockSpec((BLOCK, HM), nxt),
                  stat(cur3), stat(nxt3), pl.BlockSpec((BLOCK, HM), cur), pl.BlockSpec((BLOCK, HM), nxt),
                  pl.BlockSpec((BLOCK, kvw), prev), pl.BlockSpec((BLOCK, kvw), cur),
                  pl.BlockSpec((BLOCK, kvw), prev), pl.BlockSpec((BLOCK, kvw), cur),
                  pl.BlockSpec((BLOCK, 1), prev), pl.BlockSpec((BLOCK, 1), cur),
                  pl.BlockSpec((1, BLOCK), lambda j: (0, j)),
                  pl.BlockSpec((1, BLOCK), lambda j: (0, jnp.minimum(j + 1, nb - 1))), rows, rows],
        out_specs=[pl.BlockSpec((BLOCK, HM), cur), pl.BlockSpec((BLOCK, kvw), cur),
                   pl.BlockSpec((BLOCK, kvw), cur), rows],
        out_shape=[jax.ShapeDtypeStruct((t, HM), BF16), jax.ShapeDtypeStruct((t, kvw), BF16),
                   jax.ShapeDtypeStruct((t, kvw), BF16), jax.ShapeDtypeStruct((N_KV_A, 1, gw), F32)],
        compiler_params=_params(("arbitrary",)),
    )(qa, qa, d_oa, d_oa, lse, lse, out_a, out_a, ka, ka, va, va,
      pos_col, pos_col, pos_row, pos_row, slope_rows, sink_rows)


def _mesh_pos():
    return lax.axis_index("x"), lax.axis_index("y"), lax.axis_index("c")


def _flip(v, bit):
    return 1 - v if bit else v


def _direct_copies(srcs, dsts, send_sems, recv_sems, local_sems, gather, sem_base=0):
    x, y, c = _mesh_pos()
    me = 4 * x + 2 * y + c
    local, remote = [], []
    for a, (src, dst) in enumerate(zip(srcs, dsts)):
        local.append(pltpu.make_async_copy(src if gather else src.at[me], dst.at[me], local_sems.at[sem_base + a]))
        for r in range(1, N_DEV):
            px, py, pc = _flip(x, r & 4), _flip(y, r & 2), _flip(c, r & 1)
            sem = (N_DEV - 1) * (sem_base + a) + r - 1
            remote.append(pltpu.make_async_remote_copy(
                src_ref=src if gather else src.at[4 * px + 2 * py + pc], dst_ref=dst.at[me],
                send_sem=send_sems.at[sem], recv_sem=recv_sems.at[sem],
                device_id=(px, py, pc), device_id_type=pl.DeviceIdType.MESH))
    return local, remote


def _start_copies(local, remote):
    for cp in local + remote:
        cp.start()


def _wait_copies(local, remote):
    for cp in remote:
        cp.wait_recv()
    for cp in remote:
        cp.wait_send()
    for cp in local:
        cp.wait()


def _exchange_scratch(n):
    return [pltpu.SemaphoreType.DMA((n * (N_DEV - 1),)), pltpu.SemaphoreType.DMA((n * (N_DEV - 1),)),
            pltpu.SemaphoreType.DMA((n,))]


ANY_SPEC = pl.BlockSpec(memory_space=pl.ANY)


def _mla_fwd(qt, kb, vt, late):
    t = kb.shape[0]
    tk = _attn_tile(t)
    ratio = 2 if t >= 2 * tk else 1
    tq = ratio * tk
    nq = t // tq
    hps = MLA_FWD_HEADS_PER_STEP
    w = hps * SLAB
    pairs = [(i, j) for i in range(nq) for j in range(ratio * (i + 1))]
    i_tab = jnp.asarray(np.array([p[0] for p in pairs], np.int32))
    j_tab = jnp.asarray(np.array([p[1] for p in pairs], np.int32))

    n_late = len(late)

    def body(it_ref, jt_ref, qt_ref, k_ref, vt_ref, *rest):
        late_refs, (o_ref, ot_ref, qa_ref) = rest[:n_late], rest[n_late:n_late + 3]
        gathered_refs = rest[n_late + 3:2 * n_late + 3]
        m_s, acc_s, send_sems, recv_sems, local_sems = rest[2 * n_late + 3:]
        n = pl.program_id(1)
        i, j = it_ref[n], jt_ref[n]
        first_step = jnp.logical_and(pl.program_id(0) == 0, n == 0)
        last_step = jnp.logical_and(pl.program_id(0) == N_HEADS // hps - 1, n == len(pairs) - 1)

        @pl.when(first_step)
        def _():
            _start_copies(*_direct_copies(late_refs, gathered_refs, send_sems, recv_sems, local_sems, True))

        @pl.when(j == 0)
        def _():
            m_s[...] = jnp.full_like(m_s, NEG)
            acc_s[...] = jnp.zeros_like(acc_s)

        def update(masked, q0):
            qc = slice(q0, tq)

            def scores(hh):
                sl = slice(hh * SLAB, (hh + 1) * SLAB)
                return _dot(k_ref[:, sl], qt_ref[sl, qc])

            def softmax(hh, s):
                if masked:
                    s = jnp.where(lax.broadcasted_iota(jnp.int32, s.shape, 0)
                                  <= lax.broadcasted_iota(jnp.int32, s.shape, 1), s, NEG)
                m_old = m_s[hh][:, qc]
                m_new = jnp.maximum(m_old, jnp.max(s, axis=0, keepdims=True))
                m_s[hh, :, qc] = m_new
                return jnp.exp2(s - m_new).astype(BF16), jnp.exp2(m_old - m_new)

            def accumulate(hh, p, alpha):
                sl = slice(hh * SLAB, hh * SLAB + V_DIM_B + ONES_ROWS)
                acc_s[sl, qc] = alpha * acc_s[sl, qc] + _dot(vt_ref[sl, :], p)

            s_next, pending = scores(0), None
            for hh in range(hps):
                s = s_next
                if hh + 1 < hps:
                    s_next = scores(hh + 1)
                p, alpha = softmax(hh, s)
                if pending is not None:
                    accumulate(*pending)
                pending = (hh, p, alpha)
            accumulate(*pending)

        @pl.when(j < ratio * i)
        def _():
            update(False, 0)

        for part in range(ratio):
            @pl.when(j == ratio * i + part)
            def _():
                update(True, part * tk)

        @pl.when(j == ratio * i + ratio - 1)
        def _():
            for hh in range(hps):
                sl = slice(hh * SLAB, (hh + 1) * SLAB)
                den = acc_s[hh * SLAB + V_DIM_B:hh * SLAB + V_DIM_B + 1, :]
                values = lax.broadcasted_iota(jnp.int32, (SLAB, tq), 0) < V_DIM_B
                ot = jnp.where(values, acc_s[sl, :] / den, 0.0)
                ot_ref[sl, :] = ot.astype(BF16)
                o_ref[:, sl] = ot.T.astype(BF16)
                lse = m_s[hh] + jnp.log2(den)
                qa_ref[sl, :] = _plant_rows(qt_ref[sl, :].astype(F32), Q_HEAD_B, lse).astype(BF16)

        @pl.when(last_step)
        def _():
            _wait_copies(*_direct_copies(late_refs, gathered_refs, send_sems, recv_sems, local_sems, True))

    grid_spec = pltpu.PrefetchScalarGridSpec(
        num_scalar_prefetch=2, grid=(N_HEADS // hps, len(pairs)),
        in_specs=[pl.BlockSpec((w, tq), lambda h, n, it, jt: (h, it[n])),
                  pl.BlockSpec((tk, w), lambda h, n, it, jt: (jt[n], h)),
                  pl.BlockSpec((w, tk), lambda h, n, it, jt: (h, jt[n]))] + [ANY_SPEC] * n_late,
        out_specs=[pl.BlockSpec((tq, w), lambda h, n, it, jt: (it[n], h)),
                   pl.BlockSpec((w, tq), lambda h, n, it, jt: (h, it[n])),
                   pl.BlockSpec((w, tq), lambda h, n, it, jt: (h, it[n]))] + [ANY_SPEC] * n_late,
        scratch_shapes=[pltpu.VMEM((hps, 1, tq), F32), pltpu.VMEM((w, tq), F32)] + _exchange_scratch(n_late))
    outs = pl.pallas_call(
        body, name="mla_fwd", grid_spec=grid_spec,
        out_shape=[jax.ShapeDtypeStruct((t, HM), BF16), jax.ShapeDtypeStruct((HM, t), BF16),
                   jax.ShapeDtypeStruct((HM, t), BF16)]
        + [jax.ShapeDtypeStruct((N_DEV,) + a.shape, a.dtype) for a in late],
        compiler_params=_params(("arbitrary", "arbitrary")),
    )(i_tab, j_tab, qt, kb, vt, *late)
    return outs[0], outs[1], outs[2], list(outs[3:])


def _mla_bwd(qt, kb, kt, vb, d_ob_t, grad_slices):
    t = kb.shape[0]
    tk = _attn_tile(t)
    ratio = 2 if t >= 2 * tk else 1
    tq = ratio * tk
    nk, nq = t // tk, t // tq
    hps = MLA_HEADS_PER_STEP
    w = hps * SLAB
    pairs = [(j, i) for j in range(nk) for i in range(j // ratio, nq)]
    j_tab = jnp.asarray(np.array([p[0] for p in pairs], np.int32))
    i_tab = jnp.asarray(np.array([p[1] for p in pairs], np.int32))

    n_ex = len(grad_slices)

    def body(jt_ref, it_ref, qt_ref, dot_ref, k_ref, kt_ref, v_ref, *rest):
        slice_refs, (dqt_ref, dkt_ref, dvt_ref) = rest[:n_ex], rest[n_ex:n_ex + 3]
        part_refs = rest[n_ex + 3:2 * n_ex + 3]
        dk_s, dv_s, send_sems, recv_sems, local_sems = rest[2 * n_ex + 3:]
        n = pl.program_id(1)
        j, i = jt_ref[n], it_ref[n]
        first_step = jnp.logical_and(pl.program_id(0) == 0, n == 0)
        last_step = jnp.logical_and(pl.program_id(0) == N_HEADS // hps - 1, n == len(pairs) - 1)

        @pl.when(first_step)
        def _():
            _start_copies(*_direct_copies(slice_refs, part_refs, send_sems, recv_sems, local_sems, False))

        @pl.when(n == 0)
        def _():
            dqt_ref[...] = jnp.zeros_like(dqt_ref)

        def update(diagonal, q0):
            qc = slice(q0, tq)
            cols = pl.ds(pl.multiple_of(i * tq + q0, tk), tq - q0)

            def softmax_bwd(hh, s, dp):
                if diagonal:
                    s = jnp.where(lax.broadcasted_iota(jnp.int32, s.shape, 0)
                                  <= lax.broadcasted_iota(jnp.int32, s.shape, 1), s, NEG)
                p = jnp.exp2(s)
                return p.astype(BF16), (p * dp).astype(BF16)

            def gradients(hh, p, ds):
                base = hh * SLAB
                vrows = slice(base, base + V_DIM_B)
                qrows = slice(base, base + QK_NOPE + QK_ROPE)
                dv = _dot_nt(dot_ref[vrows, qc], p)
                dk = _dot_nt(qt_ref[qrows, qc], ds)
                if diagonal:
                    dv_s[base:base + SLAB, :] = jnp.concatenate([dv, jnp.zeros((SLAB - V_DIM_B, tk), F32)], axis=0)
                    dk_s[base:base + SLAB, :] = jnp.concatenate(
                        [dk, jnp.zeros((SLAB - QK_NOPE - QK_ROPE, tk), F32)], axis=0)
                else:
                    dv_s[vrows, :] += dv
                    dk_s[qrows, :] += dk
                dqt_ref[qrows, cols] += _dot(kt_ref[qrows, :], ds)

            def scores(hh):
                sl = slice(hh * SLAB, (hh + 1) * SLAB)
                return _dot(k_ref[:, sl], qt_ref[sl, qc])

            def dprod(hh):
                sl = slice(hh * SLAB, (hh + 1) * SLAB)
                return _dot(v_ref[:, sl], dot_ref[sl, qc])

            s_next = scores(0)
            for hh in range(hps):
                s = s_next
                dp = dprod(hh)
                if hh + 1 < hps:
                    s_next = scores(hh + 1)
                gradients(hh, *softmax_bwd(hh, s, dp))

        first_tile = lax.div(j, ratio)
        for part in range(ratio):
            @pl.when(jnp.logical_and(i == first_tile, lax.rem(j, ratio) == part))
            def _():
                update(True, part * tk)

        @pl.when(i > first_tile)
        def _():
            update(False, 0)

        @pl.when(i == nq - 1)
        def _():
            dkt_ref[...] = (dk_s[...] * (1.0 / LOG2E)).astype(BF16)
            dvt_ref[...] = dv_s[...].astype(BF16)

        @pl.when(last_step)
        def _():
            _wait_copies(*_direct_copies(slice_refs, part_refs, send_sems, recv_sems, local_sems, False))

    grid_spec = pltpu.PrefetchScalarGridSpec(
        num_scalar_prefetch=2, grid=(N_HEADS // hps, len(pairs)),
        in_specs=[pl.BlockSpec((w, tq), lambda h, n, jt, it: (h, it[n])),
                  pl.BlockSpec((w, tq), lambda h, n, jt, it: (h, it[n])),
                  pl.BlockSpec((tk, w), lambda h, n, jt, it: (jt[n], h)),
                  pl.BlockSpec((w, tk), lambda h, n, jt, it: (h, jt[n])),
                  pl.BlockSpec((tk, w), lambda h, n, jt, it: (jt[n], h))] + [ANY_SPEC] * n_ex,
        out_specs=[pl.BlockSpec((w, t), lambda h, n, jt, it: (h, 0)),
                   pl.BlockSpec((w, tk), lambda h, n, jt, it: (h, jt[n])),
                   pl.BlockSpec((w, tk), lambda h, n, jt, it: (h, jt[n]))] + [ANY_SPEC] * n_ex,
        scratch_shapes=[pltpu.VMEM((w, tk), F32), pltpu.VMEM((w, tk), F32)] + _exchange_scratch(n_ex))
    outs = pl.pallas_call(
        body, name="mla_bwd", grid_spec=grid_spec,
        out_shape=[jax.ShapeDtypeStruct((HM, t), F32), jax.ShapeDtypeStruct((HM, t), BF16),
                   jax.ShapeDtypeStruct((HM, t), BF16)]
        + [jax.ShapeDtypeStruct(a.shape, a.dtype) for a in grad_slices],
        compiler_params=_params(("arbitrary", "arbitrary")),
    )(j_tab, i_tab, qt, d_ob_t, kb, kt, vb, *grad_slices)
    return outs[0], outs[1], outs[2], list(outs[3:])


def _merge_fwd(out_a, out_b, gates, x, w_oa, w_ob, w_out, g2, g3):
    t = x.shape[0]
    tm = _token_tile(t)

    def body(oa_ref, ob_ref, gates_ref, x_ref, woa_ref, wob_ref, wout_ref, g2_ref, g3_ref,
             oap_ref, obp_ref, merged_ref, y_ref, x1_ref, h2_ref):
        oa_p = _dot(oa_ref[...], woa_ref[...])
        ob_p = _dot(ob_ref[...], wob_ref[...])
        oap_ref[...] = oa_p.astype(BF16)
        obp_ref[...] = ob_p.astype(BF16)
        sa = _sigmoid(gates_ref[:, 0:D_MODEL].astype(F32))
        sb = _sigmoid(gates_ref[:, D_MODEL:2 * D_MODEL].astype(F32))
        merged = (sa * oa_p + sb * ob_p).astype(BF16)
        merged_ref[...] = merged
        y = _dot(merged, wout_ref[...])
        y_ref[...] = y
        x1 = x_ref[...] + y * _rms_r(y) * g2_ref[...]
        x1_ref[...] = x1
        h2_ref[...] = (x1 * _rms_r(x1) * g3_ref[...]).astype(BF16)

    def sds(dt):
        return jax.ShapeDtypeStruct((t, D_MODEL), dt)

    row = _row_spec(tm, D_MODEL)
    return pl.pallas_call(
        body, name="merge_fwd", grid=(t // tm,),
        in_specs=[_row_spec(tm, HM), _row_spec(tm, HM), _row_spec(tm, 2 * D_MODEL), row,
                  _full_spec((HM, D_MODEL)), _full_spec((HM, D_MODEL)), _full_spec((D_MODEL, D_MODEL)),
                  _full_spec((1, D_MODEL)), _full_spec((1, D_MODEL))],
        out_specs=[row] * 6,
        out_shape=[sds(BF16), sds(BF16), sds(BF16), sds(F32), sds(F32), sds(BF16)],
        compiler_params=_params(("parallel",)),
    )(out_a, out_b, gates, x, w_oa, w_ob, w_out, g2, g3)


def _merge_bwd(dx1, y, gates, oa_p, ob_p, out_a, out_b, out_b_t, merged, w_oa, w_ob, w_out, g2):
    t = dx1.shape[0]
    tm = _token_tile(t)

    def body(dx1_ref, y_ref, gates_ref, oap_ref, obp_ref, oa_ref, ob_ref, obt_ref, merged_ref,
             woa_ref, wob_ref, wout_ref, g2_ref,
             dgates_ref, doa_ref, dobt_ref, dg2_ref, dwoa_ref, dwob_ref, dwout_ref):
        @pl.when(pl.program_id(0) == 0)
        def _():
            dwoa_ref[...] = jnp.zeros_like(dwoa_ref)
            dwob_ref[...] = jnp.zeros_like(dwob_ref)
            dwout_ref[...] = jnp.zeros_like(dwout_ref)

        dx1v = dx1_ref[...]
        yv = y_ref[...]
        r2 = _rms_r(yv)
        _acc_rows(dg2_ref, dx1v * yv * r2)
        dy = _rms_bwd(yv, r2, g2_ref[...], dx1v).astype(BF16)
        dwout_ref[...] += _dot_tn(merged_ref[...], dy)
        dm = _dot_nt(dy, wout_ref[...])
        sa = _sigmoid(gates_ref[:, 0:D_MODEL].astype(F32))
        sb = _sigmoid(gates_ref[:, D_MODEL:2 * D_MODEL].astype(F32))
        d_oap = (dm * sa).astype(BF16)
        d_obp = (dm * sb).astype(BF16)
        dwoa_ref[...] += _dot_tn(oa_ref[...], d_oap)
        dwob_ref[...] += _dot_tn(ob_ref[...], d_obp)
        dgates_ref[:, 0:D_MODEL] = (dm * oap_ref[...].astype(F32) * sa * (1.0 - sa)).astype(BF16)
        dgates_ref[:, D_MODEL:2 * D_MODEL] = (dm * obp_ref[...].astype(F32) * sb * (1.0 - sb)).astype(BF16)
        doa_ref[...] = _dot_nt(d_oap, woa_ref[...]).astype(BF16)
        d_ob_t = _dot_nt(wob_ref[...], d_obp)
        for hd in range(N_HEADS):
            sl = slice(hd * SLAB, (hd + 1) * SLAB)
            delta = jnp.sum(d_ob_t[sl, :] * obt_ref[sl, :].astype(F32), axis=0, keepdims=True)
            dobt_ref[sl, :] = _plant_rows(d_ob_t[sl, :], V_DIM_B, delta).astype(BF16)

    def sds(n, dt):
        return jax.ShapeDtypeStruct((t, n), dt)

    row = _row_spec(tm, D_MODEL)
    return pl.pallas_call(
        body, name="merge_bwd", grid=(t // tm,),
        in_specs=[row, row, _row_spec(tm, 2 * D_MODEL), row, row, _row_spec(tm, HM), _row_spec(tm, HM),
                  _col_spec(HM, tm), row,
                  _full_spec((HM, D_MODEL)), _full_spec((HM, D_MODEL)), _full_spec((D_MODEL, D_MODEL)),
                  _full_spec((1, D_MODEL))],
        out_specs=[_row_spec(tm, 2 * D_MODEL), _row_spec(tm, HM), _col_spec(HM, tm), _full_spec((1, D_MODEL)),
                   _full_spec((HM, D_MODEL)), _full_spec((HM, D_MODEL)), _full_spec((D_MODEL, D_MODEL))],
        out_shape=[sds(2 * D_MODEL, BF16), sds(HM, BF16), jax.ShapeDtypeStruct((HM, t), BF16),
                   jax.ShapeDtypeStruct((1, D_MODEL), F32),
                   jax.ShapeDtypeStruct((HM, D_MODEL), F32), jax.ShapeDtypeStruct((HM, D_MODEL), F32),
                   jax.ShapeDtypeStruct((D_MODEL, D_MODEL), F32)],
        compiler_params=_params(("arbitrary",)),
    )(dx1, y, gates, oa_p, ob_p, out_a, out_b, out_b_t, merged, w_oa, w_ob, w_out, g2)


def _mlp_fwd_bwd(x1, h2, target, w_up, w_down, g3, g4):
    t = x1.shape[0]
    tm = _token_tile(t)
    fs = D_FF // N_DEV

    def body(x1_ref, h2_ref, tgt_ref, wup_ref, wdown_ref, g3_ref, g4_ref,
             a_ref, du_ref, dy2_ref, dx1_ref, loss_ref, dg3_ref, dg4_ref):
        x1v = x1_ref[...]
        h2v = h2_ref[...]
        u = jnp.concatenate([_dot(h2v, wup_ref[s]) for s in range(N_DEV)], axis=1)
        ru = jnp.maximum(u, 0.0)
        a = (ru * ru).astype(BF16)
        a_ref[...] = a
        y2 = _dot(a, wdown_ref[...])
        r4 = _rms_r(y2)
        diff = x1v + y2 * r4 * g4_ref[...] - tgt_ref[...]
        _acc_rows(loss_ref, jnp.sum(diff * diff, axis=-1, keepdims=True) * (0.5 / D_MODEL)
                  * jnp.ones((1, SLAB), F32))
        dx2 = diff * (1.0 / D_MODEL)
        _acc_rows(dg4_ref, dx2 * y2 * r4)
        dy2 = _rms_bwd(y2, r4, g4_ref[...], dx2).astype(BF16)
        dy2_ref[...] = dy2
        du = (_dot_nt(dy2, wdown_ref[...]) * (2.0 * ru)).astype(BF16)
        du_ref[...] = du
        dh2 = _dot_nt(du[:, 0:fs], wup_ref[0])
        for s in range(1, N_DEV):
            dh2 += _dot_nt(du[:, s * fs:(s + 1) * fs], wup_ref[s])
        r3 = _rms_r(x1v)
        _acc_rows(dg3_ref, dh2 * x1v * r3)
        dx1_ref[...] = dx2 + _rms_bwd(x1v, r3, g3_ref[...], dh2)

    row = _row_spec(tm, D_MODEL)
    frow = _row_spec(tm, D_FF)
    vec = _full_spec((1, D_MODEL))
    return pl.pallas_call(
        body, name="mlp_fwd_bwd", grid=(t // tm,),
        in_specs=[row, row, row, _full_spec((N_DEV, D_MODEL, fs)), _full_spec((D_FF, D_MODEL)), vec, vec],
        out_specs=[frow, frow, row, row, _full_spec((1, SLAB)), vec, vec],
        out_shape=[jax.ShapeDtypeStruct((t, D_FF), BF16), jax.ShapeDtypeStruct((t, D_FF), BF16),
                   jax.ShapeDtypeStruct((t, D_MODEL), BF16), jax.ShapeDtypeStruct((t, D_MODEL), F32),
                   jax.ShapeDtypeStruct((1, SLAB), F32), jax.ShapeDtypeStruct((1, D_MODEL), F32),
                   jax.ShapeDtypeStruct((1, D_MODEL), F32)],
        compiler_params=_params(("arbitrary",)),
    )(x1, h2, target, w_up, w_down, g3, g4)


def _inproj_bwd(dgates, dqa, dka, dva, dqb_t, dkb_t, dvb_t, cq, ckv, cqn, ckvn, x, dx1, rope_ct, rope_s1t, rope_s2t,
                g1, g_q, g_kv, w_in, w_qb, w_kvb):
    t = x.shape[0]
    tm = _token_tile(t)

    def body(dgates_ref, dqa_ref, dka_ref, dva_ref, dqt_ref, dkt_ref, dvt_ref, cq_ref, ckv_ref, cqn_ref, ckvn_ref,
             x_ref, dx1_ref, ct_ref, s1t_ref, s2t_ref, g1_ref, gq_ref, gkv_ref, win_ref, wqb_ref, wkvb_ref,
             dproj_ref, dx_ref, dg1_ref, dgq_ref, dgkv_ref, dwqb_ref, dwkvb_ref, dqbrt_ref, dkvbt_ref):
        @pl.when(pl.program_id(0) == 0)
        def _():
            dwqb_ref[...] = jnp.zeros_like(dwqb_ref)
            dwkvb_ref[...] = jnp.zeros_like(dwkvb_ref)

        ct, s1t, s2t = ct_ref[...], s1t_ref[...], s2t_ref[...]
        dk_sum_t = jnp.zeros((SLAB, tm), F32)
        for hd in range(N_HEADS):
            sl = slice(hd * SLAB, (hd + 1) * SLAB)
            dqbrt_ref[sl, :] = _rope_t_bwd(dqt_ref[sl, :] * SCALE_B, ct, s1t, s2t).astype(BF16)
            dk_sum_t += dkt_ref[sl, :].astype(F32)
        dkvbt_ref[0:HM, :] = dkt_ref[...]
        dkvbt_ref[HM:2 * HM, :] = dvt_ref[...]
        dkr = _rope_t_bwd(dk_sum_t, ct, s1t, s2t).T
        dwqb_ref[...] += _dot(dqbrt_ref[...], cqn_ref[...])
        dwkvb_ref[...] += _dot(dkvbt_ref[...], ckvn_ref[...])
        dcqn = _dot(wqb_ref[...], dqbrt_ref[...]).T
        cq = cq_ref[...]
        rq = _rms_r(cq)
        _acc_rows(dgq_ref, dcqn * cq * rq)
        dcq = _rms_bwd(cq, rq, gq_ref[...], dcqn)
        dckvn = _dot(wkvb_ref[...], dkvbt_ref[...]).T
        ckv = ckv_ref[...]
        rkv = _rms_r(ckv)
        _acc_rows(dgkv_ref, dckvn * ckv * rkv)
        dckv = _rms_bwd(ckv, rkv, gkv_ref[...], dckvn)
        dproj_ref[:, C_GATES:C_QA] = dgates_ref[...]
        dproj_ref[:, C_QA:C_KA] = dqa_ref[...]
        dproj_ref[:, C_KA:C_VA] = dka_ref[...]
        dproj_ref[:, C_VA:C_CQ] = dva_ref[...]
        dproj_ref[:, C_CQ:C_CKV] = dcq.astype(BF16)
        dproj_ref[:, C_CKV:C_KR] = dckv.astype(BF16)
        dproj_ref[:, C_KR:D_IN_PAD] = dkr.astype(BF16)
        dh = _dot_nt(dproj_ref[...], win_ref[...])
        xv = x_ref[...]
        r1 = _rms_r(xv)
        _acc_rows(dg1_ref, dh * xv * r1)
        dx_ref[...] = dx1_ref[...] + _rms_bwd(xv, r1, g1_ref[...], dh)

    kvw = N_KV_A * SLAB
    row = _row_spec(tm, D_MODEL)
    hm = _row_spec(tm, HM)
    hmt = _col_spec(HM, tm)
    tab = _col_spec(SLAB, tm)
    return pl.pallas_call(
        body, name="inproj_bwd", grid=(t // tm,),
        in_specs=[_row_spec(tm, 2 * D_MODEL), hm, _row_spec(tm, kvw), _row_spec(tm, kvw), hmt, hmt, hmt,
                  _row_spec(tm, Q_LORA), _row_spec(tm, KV_LORA), _row_spec(tm, Q_LORA), _row_spec(tm, KV_LORA),
                  row, row, tab, tab, tab,
                  _full_spec((1, D_MODEL)), _full_spec((1, Q_LORA)), _full_spec((1, KV_LORA)),
                  _full_spec((D_MODEL, D_IN_PAD)), _full_spec((Q_LORA, HM)), _full_spec((KV_LORA, 2 * HM))],
        out_specs=[_row_spec(tm, D_IN_PAD), row,
                   _full_spec((1, D_MODEL)), _full_spec((1, Q_LORA)), _full_spec((1, KV_LORA)),
                   _full_spec((HM, Q_LORA)), _full_spec((2 * HM, KV_LORA))],
        out_shape=[jax.ShapeDtypeStruct((t, D_IN_PAD), BF16), jax.ShapeDtypeStruct((t, D_MODEL), F32),
                   jax.ShapeDtypeStruct((1, D_MODEL), F32), jax.ShapeDtypeStruct((1, Q_LORA), F32),
                   jax.ShapeDtypeStruct((1, KV_LORA), F32),
                   jax.ShapeDtypeStruct((HM, Q_LORA), F32), jax.ShapeDtypeStruct((2 * HM, KV_LORA), F32)],
        scratch_shapes=[pltpu.VMEM((HM, tm), BF16), pltpu.VMEM((2 * HM, tm), BF16)],
        compiler_params=_params(("arbitrary",)),
    )(dgates, dqa, dka, dva, dqb_t, dkb_t, dvb_t, cq, ckv, cqn, ckvn, x, dx1, rope_ct, rope_s1t, rope_s2t,
      g1, g_q, g_kv, w_in, w_qb, w_kvb)


def _matmul_tn(a, b, name, out_dtype=F32, n_shards=1):
    t, k = a.shape
    n = b.shape[1]
    bt = min(t, 512)
    bn = min(n, 2048)
    bk = min(k, 2048 * 1024 // bn)
    ns = n // n_shards
    per_block = bn // ns
    steps = t // bt

    def body(a_ref, b_ref, o_ref, acc):
        s = pl.program_id(2)

        @pl.when(s == 0)
        def _():
            acc[...] = jnp.zeros_like(acc)

        acc[...] += _dot_tn(a_ref[...], b_ref[...])

        @pl.when(s == steps - 1)
        def _():
            if n_shards > 1:
                for p in range(per_block):
                    o_ref[p] = acc[:, p * ns:(p + 1) * ns].astype(out_dtype)
            else:
                o_ref[...] = acc[...].astype(out_dtype)

    if n_shards > 1:
        out_spec = pl.BlockSpec((per_block, bk, ns), lambda i, j, s: (j, i, 0))
        out_shape = jax.ShapeDtypeStruct((n_shards, k, ns), out_dtype)
    else:
        out_spec = pl.BlockSpec((bk, bn), lambda i, j, s: (i, j))
        out_shape = jax.ShapeDtypeStruct((k, n), out_dtype)
    return pl.pallas_call(
        body, name=name, grid=(k // bk, n // bn, steps),
        in_specs=[pl.BlockSpec((bt, bk), lambda i, j, s: (s, i)), pl.BlockSpec((bt, bn), lambda i, j, s: (s, j))],
        out_specs=out_spec, out_shape=out_shape, scratch_shapes=[pltpu.VMEM((bk, bn), F32)],
        compiler_params=_params(("parallel", "parallel", "arbitrary")),
    )(a, b)


def _two_level_gather(srcs, dsts, send_sems, recv_sems, local_sems):
    n = len(srcs)
    x, y, c = _mesh_pos()
    me, sibling = (x, y, c), (x, y, 1 - c)
    chips = [(1 - x, y), (x, 1 - y), (1 - x, 1 - y)]

    def slot(a, px, py, pc):
        return dsts[a].at[4 * px + 2 * py + pc]

    def copy(a, k, block, to, src=None):
        return pltpu.make_async_remote_copy(
            src_ref=slot(a, *block) if src is None else src, dst_ref=slot(a, *block),
            send_sem=send_sems.at[(N_DEV - 1) * a + k], recv_sem=recv_sems.at[(N_DEV - 1) * a + k],
            device_id=to, device_id_type=pl.DeviceIdType.MESH)

    def own_copies():
        mine = [pltpu.make_async_copy(srcs[a], slot(a, *me), local_sems.at[a]) for a in range(n)]
        first = []
        for a in range(n):
            first.append(copy(a, 0, me, sibling, src=srcs[a]))
            first += [copy(a, 1 + j, me, (*chip, c), src=srcs[a]) for j, chip in enumerate(chips)]
        return mine, first

    def start():
        mine, first = own_copies()
        for cp in mine + first:
            cp.start()

    def finish():
        mine, first = own_copies()
        passed = []
        for j, chip in enumerate(chips):
            for a in range(n):
                copy(a, 1 + j, (*chip, c), me).wait_recv()
                passed.append(copy(a, 4 + j, (*chip, c), sibling))
                passed[-1].start()
        for a in range(n):
            copy(a, 0, sibling, me).wait_recv()
        for j, chip in enumerate(chips):
            for a in range(n):
                copy(a, 4 + j, (*chip, 1 - c), me).wait_recv()
        for cp in first + passed:
            cp.wait_send()
        for cp in mine:
            cp.wait()

    return start, finish


def _exchange_grads(slices, small):
    n = len(slices)

    def body(*refs):
        srcs, s_ref = refs[:n], refs[n]
        dsts, s_dst = refs[n + 1:2 * n + 1], refs[2 * n + 1]
        sems = refs[2 * n + 2:]
        parts = _direct_copies(srcs, dsts, *sems, False)
        smalls = _direct_copies([s_ref], [s_dst], *sems, True, sem_base=n)
        _start_copies(*parts)
        _start_copies(*smalls)
        _wait_copies(*parts)
        _wait_copies(*smalls)

    outs = pl.pallas_call(
        body, name="exchange_grads",
        out_shape=[jax.ShapeDtypeStruct(a.shape, a.dtype) for a in slices]
        + [jax.ShapeDtypeStruct((N_DEV,) + small.shape, small.dtype)],
        in_specs=[ANY_SPEC] * (n + 1), out_specs=[ANY_SPEC] * (n + 1), scratch_shapes=_exchange_scratch(n + 1),
    )(*slices, small)
    return list(outs[:n]), outs[n]


def _adamw(parts, w, m, v, name):
    _, k, n = parts.shape
    bk = min(k, ADAM_ROWS)
    c1 = 1.0 - ADAM_B1 ** ADAM_STEP
    c2 = 1.0 - ADAM_B2 ** ADAM_STEP

    def body(p_ref, w_ref, m_ref, v_ref, g_ref, d_ref, mo_ref, vo_ref):
        g = p_ref[0].astype(F32)
        for s in range(1, N_DEV):
            g = g + p_ref[s].astype(F32)
        g_ref[0] = g
        m_new = ADAM_B1 * m_ref[0] + (1.0 - ADAM_B1) * g
        v_new = ADAM_B2 * v_ref[0] + (1.0 - ADAM_B2) * (g * g)
        mo_ref[0] = m_new
        vo_ref[0] = v_new
        m_hat = m_new / c1
        v_hat = v_new / c2
        d_ref[0] = -ADAM_LR * (m_hat / (jnp.sqrt(v_hat) + ADAM_EPS) + ADAM_WD * w_ref[0])

    blk = pl.BlockSpec((1, bk, n), lambda i: (0, i, 0))
    out = jax.ShapeDtypeStruct((1, k, n), F32)
    return pl.pallas_call(
        body, name=name, grid=(k // bk,),
        in_specs=[pl.BlockSpec((N_DEV, bk, n), lambda i: (0, i, 0)), blk, blk, blk],
        out_specs=[blk] * 4, out_shape=[out] * 4,
        compiler_params=_params(("parallel",)),
    )(parts, w, m, v)


def _adamw_small(parts, w, m, v):
    k = len(SMALL_LAYOUT)
    c1 = 1.0 - ADAM_B1 ** ADAM_STEP
    c2 = 1.0 - ADAM_B2 ** ADAM_STEP

    def body(p_ref, *refs):
        w_refs, m_refs, v_refs, outs = refs[:k], refs[k:2 * k], refs[2 * k:3 * k], refs[3 * k:]
        total = p_ref[0]
        for s in range(1, N_DEV):
            total = total + p_ref[s]
        for i, (_, row, off, width) in enumerate(SMALL_LAYOUT):
            g = total[row:row + 1, off:off + width]
            m_new = ADAM_B1 * m_refs[i][...] + (1.0 - ADAM_B1) * g
            v_new = ADAM_B2 * v_refs[i][...] + (1.0 - ADAM_B2) * (g * g)
            outs[4 * i][...] = g
            outs[4 * i + 1][...] = -ADAM_LR * ((m_new / c1) / (jnp.sqrt(v_new / c2) + ADAM_EPS)
                                               + ADAM_WD * w_refs[i][...])
            outs[4 * i + 2][...] = m_new
            outs[4 * i + 3][...] = v_new
        outs[4 * k][...] = total[SMALL_LOSS_ROW:SMALL_LOSS_ROW + 1, SMALL_LOSS_OFF:SMALL_LOSS_OFF + 1]

    names = [name for name, *_ in SMALL_LAYOUT]
    out_shape = [jax.ShapeDtypeStruct(w[name].shape, F32) for name in names for _ in range(4)]
    outs = pl.pallas_call(
        body, name="adamw_small", out_shape=out_shape + [jax.ShapeDtypeStruct((1, 1), F32)],
    )(parts, *[w[n] for n in names], *[m[n] for n in names], *[v[n] for n in names])
    return {name: tuple(outs[4 * i:4 * i + 4]) for i, name in enumerate(names)}, outs[4 * k]


def _pad_heads_cols(w, heads, width):
    k = w.shape[0]
    w = w.reshape(k, heads, width)
    return jnp.pad(w, ((0, 0), (0, 0), (0, SLAB - width))).reshape(k, heads * SLAB)


def _unpad_heads_cols(w, heads, width):
    k = w.shape[0]
    return w.reshape(k, heads, SLAB)[:, :, :width].reshape(k, heads * width)


def _pad_heads_rows(w, heads, width):
    n = w.shape[1]
    w = w.reshape(heads, width, n)
    return jnp.pad(w, ((0, 0), (0, SLAB - width), (0, 0))).reshape(heads * SLAB, n)


def _unpad_heads_rows(w, heads, width):
    n = w.shape[1]
    return w.reshape(heads, SLAB, n)[:, :width, :].reshape(heads * width, n)


def _pad_w_in(w_in):
    o = 2 * D_MODEL
    qa = _pad_heads_cols(w_in[:, o:o + 512], N_HEADS, HEAD_A)
    ka = _pad_heads_cols(w_in[:, o + 512:o + 640], N_KV_A, HEAD_A)
    va = _pad_heads_cols(w_in[:, o + 640:o + 768], N_KV_A, HEAD_A)
    kr = jnp.pad(w_in[:, o + 1152:o + 1184], ((0, 0), (QK_NOPE, SLAB - QK_NOPE - QK_ROPE)))
    return jnp.concatenate([w_in[:, :o], qa, ka, va, w_in[:, o + 768:o + 1152], kr], axis=1)


def _unpad_w_in(w):
    qa = _unpad_heads_cols(w[:, C_QA:C_KA], N_HEADS, HEAD_A)
    ka = _unpad_heads_cols(w[:, C_KA:C_VA], N_KV_A, HEAD_A)
    va = _unpad_heads_cols(w[:, C_VA:C_CQ], N_KV_A, HEAD_A)
    kr = w[:, C_KR + QK_NOPE:C_KR + QK_NOPE + QK_ROPE]
    return jnp.concatenate([w[:, :C_QA], qa, ka, va, w[:, C_CQ:C_KR], kr], axis=1)


def _pad_w_kvb(w_kvb):
    w = w_kvb.reshape(KV_LORA, N_HEADS, QK_NOPE + V_DIM_B)
    k = jnp.pad(w[:, :, :QK_NOPE], ((0, 0), (0, 0), (0, SLAB - QK_NOPE))).reshape(KV_LORA, HM)
    v = jnp.pad(w[:, :, QK_NOPE:], ((0, 0), (0, 0), (0, SLAB - V_DIM_B))).reshape(KV_LORA, HM)
    return jnp.concatenate([k, v], axis=1)


def _unpad_w_kvb(w):
    k = w[:, :HM].reshape(KV_LORA, N_HEADS, SLAB)[:, :, :QK_NOPE]
    v = w[:, HM:].reshape(KV_LORA, N_HEADS, SLAB)[:, :, :V_DIM_B]
    return jnp.concatenate([k, v], axis=2).reshape(KV_LORA, N_HEADS * (QK_NOPE + V_DIM_B))


def _col_shards(w):
    k, n = w.shape
    return w.reshape(k, N_DEV, n // N_DEV).transpose(1, 0, 2)


def _from_col_shards(s):
    _, k, ns = s.shape
    return s.transpose(1, 0, 2).reshape(k, N_DEV * ns)


def _freq_row():
    freqs = ROPE_THETA ** (-jnp.arange(0, QK_ROPE, 2, dtype=F32) / QK_ROPE)
    return jnp.concatenate([jnp.zeros((QK_NOPE,), F32), freqs, freqs,
                            jnp.zeros((SLAB - QK_NOPE - QK_ROPE,), F32)]).reshape(1, SLAB)


SMALL_D_ROWS = ("pre_norm_mix", "post_norm_mix", "pre_norm_mlp", "post_norm_mlp")
SMALL_LAYOUT = tuple((name, i, 0, D_MODEL) for i, name in enumerate(SMALL_D_ROWS)) + (
    ("q_a_norm", 4, 0, Q_LORA), ("kv_a_norm", 4, 256, KV_LORA), ("sinks", 4, 384, N_HEADS))
SMALL_LOSS_ROW, SMALL_LOSS_OFF = 4, 512


def _pack_small(vals):
    row4 = jnp.concatenate([vals["q_a_norm"].reshape(-1), vals["kv_a_norm"].reshape(-1), vals["sinks"].reshape(-1),
                            jnp.zeros((SMALL_LOSS_OFF - 392,), F32), vals["loss"].reshape(-1),
                            jnp.zeros((1024 - SMALL_LOSS_OFF - 1,), F32)])
    rows = [vals[n].reshape(1024) for n in SMALL_D_ROWS] + [row4]
    return jnp.concatenate([jnp.stack(rows), jnp.zeros((SMALL_ROWS - 5, 1024), F32)], axis=0)


WEIGHT_ORDER = ("pre_norm_mix", "w_in", "q_a_norm", "w_q_b", "kv_a_norm", "w_kv_b", "sinks", "w_o_a", "w_o_b",
                "w_out", "post_norm_mix", "pre_norm_mlp", "w_up", "w_down", "post_norm_mlp")


def kernel(x, positions, pre_norm_mix, w_in, q_a_norm, w_q_b, kv_a_norm, w_kv_b, sinks, w_o_a, w_o_b, w_out, post_norm_mix, pre_norm_mlp, w_up, w_down, post_norm_mlp, loss_target, m_pre_norm_mix, m_w_in, m_q_a_norm, m_w_q_b, m_kv_a_norm, m_w_kv_b, m_sinks, m_w_o_a, m_w_o_b, m_w_out, m_post_norm_mix, m_pre_norm_mlp, m_w_up, m_w_down, m_post_norm_mlp, v_pre_norm_mix, v_w_in, v_q_a_norm, v_w_q_b, v_kv_a_norm, v_w_kv_b, v_sinks, v_w_o_a, v_w_o_b, v_w_out, v_post_norm_mix, v_pre_norm_mlp, v_w_up, v_w_down, v_post_norm_mlp):
    weights = dict(pre_norm_mix=pre_norm_mix, w_in=w_in, q_a_norm=q_a_norm, w_q_b=w_q_b, kv_a_norm=kv_a_norm,
                   w_kv_b=w_kv_b, sinks=sinks, w_o_a=w_o_a, w_o_b=w_o_b, w_out=w_out, post_norm_mix=post_norm_mix,
                   pre_norm_mlp=pre_norm_mlp, w_up=w_up, w_down=w_down, post_norm_mlp=post_norm_mlp)
    m_in = dict(pre_norm_mix=m_pre_norm_mix, w_in=m_w_in, q_a_norm=m_q_a_norm, w_q_b=m_w_q_b, kv_a_norm=m_kv_a_norm,
                w_kv_b=m_w_kv_b, sinks=m_sinks, w_o_a=m_w_o_a, w_o_b=m_w_o_b, w_out=m_w_out,
                post_norm_mix=m_post_norm_mix, pre_norm_mlp=m_pre_norm_mlp, w_up=m_w_up, w_down=m_w_down,
                post_norm_mlp=m_post_norm_mlp)
    v_in = dict(pre_norm_mix=v_pre_norm_mix, w_in=v_w_in, q_a_norm=v_q_a_norm, w_q_b=v_w_q_b, kv_a_norm=v_kv_a_norm,
                w_kv_b=v_w_kv_b, sinks=v_sinks, w_o_a=v_w_o_a, w_o_b=v_w_o_b, w_out=v_w_out,
                post_norm_mix=v_post_norm_mix, pre_norm_mlp=v_pre_norm_mlp, w_up=v_w_up, w_down=v_w_down,
                post_norm_mlp=v_post_norm_mlp)

    xs, pos, target = x[0], positions[0], loss_target[0]
    t = xs.shape[0]
    pos_col = pos.reshape(t, 1)
    pos_row = pos.reshape(1, t)
    g1, g2, g3, g4 = (weights[n] for n in SMALL_D_ROWS)
    g_q, g_kv = q_a_norm, kv_a_norm
    sink_vec = sinks.reshape(N_HEADS)
    shard = {n: weights[n][0].astype(BF16) for n in EARLY + LATE}

    tables, (e_in, e_qb, e_kvb) = _rope_tables(pos_col, _freq_row(), [shard[n] for n in EARLY])
    w_in_p = _pad_w_in(_from_col_shards(e_in))
    w_qb = _pad_heads_cols(_from_col_shards(e_qb), N_HEADS, QK_NOPE + QK_ROPE)
    w_kvb = _pad_w_kvb(_from_col_shards(e_kvb))

    (h, gates, qa, ka, va, cq, ckv, cqn, ckvn, kb, vb, qt, kt, vt) = _inproj_fwd(
        xs, g1, w_in_p, g_q, g_kv, w_kvb, w_qb.T, w_kvb[:, :HM].T, w_kvb[:, HM:].T, w_in_p[:, C_KR:].T, tables)
    out_a, lse_a = _swa_fwd(qa, ka, va, pos_col, pos_row, sink_vec)
    out_b, out_b_t, qt_lse, (l_oa, l_ob, l_out, w_up_s, l_down) = _mla_fwd(qt, kb, vt, [shard[n] for n in LATE])
    w_oa = _pad_heads_rows(_from_col_shards(l_oa), N_HEADS, HEAD_A)
    w_ob = _pad_heads_rows(_from_col_shards(l_ob), N_HEADS, V_DIM_B)
    w_out_f = l_out.reshape(D_MODEL, D_MODEL)
    w_down_f = l_down.reshape(D_FF, D_MODEL)

    oa_p, ob_p, merged, y, x1, h2 = _merge_fwd(out_a, out_b, gates, xs, w_oa, w_ob, w_out_f, g2, g3)
    a, du, dy2, dx1, loss, dg3, dg4 = _mlp_fwd_bwd(x1, h2, target, w_up_s, w_down_f, g3, g4)
    (dgates, d_oa, d_ob_t, dg2, dw_oa, dw_ob, dw_out) = _merge_bwd(
        dx1, y, gates, oa_p, ob_p, out_a, out_b, out_b_t, merged, w_oa, w_ob, w_out_f, g2)
    late_slices = [
        _col_shards(_unpad_heads_rows(dw_oa, N_HEADS, HEAD_A)).astype(BF16),
        _col_shards(_unpad_heads_rows(dw_ob, N_HEADS, V_DIM_B)).astype(BF16),
        dw_out.astype(BF16).reshape(N_DEV, D_MODEL // N_DEV, D_MODEL),
        _matmul_tn(h2, du, "dw_up", BF16, N_DEV),
        _matmul_tn(a, dy2, "dw_down", BF16).reshape(N_DEV, D_FF // N_DEV, D_MODEL),
    ]
    dqa, dka, dva, dsink = _swa_bwd(qa, ka, va, out_a, d_oa, lse_a, pos_col, pos_row, sink_vec)
    dqb_t, dkb_t, dvb_t, late_parts = _mla_bwd(qt_lse, kb, kt, vb, d_ob_t, late_slices)
    dproj, dx, dg1, dgq, dgkv, dw_qb_t, dw_kvb_t = _inproj_bwd(
        dgates, dqa, dka, dva, dqb_t, dkb_t, dvb_t, cq, ckv, cqn, ckvn, xs, dx1, *tables[3:], g1, g_q, g_kv,
        w_in_p, w_qb, w_kvb)
    early_slices = [
        _col_shards(_unpad_w_in(_matmul_tn(h, dproj, "dw_in"))).astype(BF16),
        _col_shards(_unpad_heads_cols(dw_qb_t.T, N_HEADS, QK_NOPE + QK_ROPE)).astype(BF16),
        _col_shards(_unpad_w_kvb(dw_kvb_t.T)).astype(BF16),
    ]
    small_grads = {"pre_norm_mix": dg1, "post_norm_mix": dg2, "pre_norm_mlp": dg3, "post_norm_mlp": dg4,
                   "q_a_norm": dgq, "kv_a_norm": dgkv, "sinks": dsink.reshape(N_HEADS, BLOCK).sum(axis=1),
                   "loss": loss[0, 0:1]}
    early_parts, s_parts = _exchange_grads(early_slices, _pack_small(small_grads))

    updates = {}
    for name, parts in zip(EARLY + LATE, early_parts + late_parts):
        outs = _adamw(parts, weights[name], m_in[name], v_in[name], "adamw_" + name)
        for kind, arr in zip(("g", "d", "m", "v"), outs):
            updates[kind, name] = arr
    small_out, loss_sum = _adamw_small(s_parts, weights, m_in, v_in)
    for name, outs in small_out.items():
        for kind, arr in zip(("g", "d", "m", "v"), outs):
            updates[kind, name] = arr
    results = [updates[kind, name] for kind in ("g", "d", "m", "v") for name in WEIGHT_ORDER]
    return (loss_sum.reshape(()), dx[None], *results)
```

```python
import functools

import numpy as np
import jax
import jax.numpy as jnp
from jax import lax
from jax.experimental import pallas as pl
from jax.experimental.pallas import tpu as pltpu

F32 = jnp.float32
BF16 = jnp.bfloat16

D_MODEL = 1024
D_FF = 4096
N_HEADS = 8
N_KV_A = 2
GROUP_A = N_HEADS // N_KV_A
HEAD_A = 64
QK_NOPE = 64
QK_ROPE = 32
V_DIM_B = 64
Q_LORA = 256
KV_LORA = 128
BLOCK = 128
SLAB = 128
ROPE_THETA = 10000.0
EPS = 1e-6
N_DEV = 8
NEG = -1e30

SCALE_A = HEAD_A ** -0.5
SCALE_B = (QK_NOPE + QK_ROPE) ** -0.5
LOG2E = 1.4426950408889634
SCORE_B = SCALE_B * LOG2E
MLA_HEADS_PER_STEP = 4
MLA_FWD_HEADS_PER_STEP = 8
Q_HEAD_B = QK_NOPE + QK_ROPE
ONES_ROWS = 16
SLOPES_A = tuple(2.0 ** (-8.0 * (h + 1) / N_HEADS) for h in range(N_HEADS))

ADAM_LR = 0.001
ADAM_B1 = 0.9
ADAM_B2 = 0.999
ADAM_EPS = 1e-08
ADAM_WD = 0.01
ADAM_STEP = 10

HM = N_HEADS * SLAB
C_GATES = 0
C_QA = 2 * D_MODEL
C_KA = C_QA + HM
C_VA = C_KA + N_KV_A * SLAB
C_CQ = C_VA + N_KV_A * SLAB
C_CKV = C_CQ + Q_LORA
C_KR = C_CKV + KV_LORA
D_IN_PAD = C_KR + SLAB

VMEM_LIMIT = 56 * 1024 * 1024

EARLY = ("w_in", "w_q_b", "w_kv_b")
LATE = ("w_o_a", "w_o_b", "w_out", "w_up", "w_down")
ADAM_ROWS = 256
SMALL_ROWS = 8


def _token_tile(t):
    return min(256, t)


def _attn_tile(t):
    return 512 if t >= 2048 else 128


def _params(sem, vmem=VMEM_LIMIT):
    return pltpu.CompilerParams(dimension_semantics=sem, vmem_limit_bytes=vmem)


def _dot(a, b):
    return jnp.dot(a, b, preferred_element_type=F32)


def _dot_nt(a, b):
    return lax.dot_general(a, b, (((1,), (1,)), ((), ())), preferred_element_type=F32)


def _dot_tn(a, b):
    return lax.dot_general(a, b, (((0,), (0,)), ((), ())), preferred_element_type=F32)


def _rms_r(x):
    return lax.rsqrt(jnp.mean(x * x, axis=-1, keepdims=True) + EPS)


def _rms_bwd(x, r, g, dy):
    t = dy * g
    return r * t - x * (r * r * r) * jnp.mean(x * t, axis=-1, keepdims=True)


def _sigmoid(x):
    return 1.0 / (1.0 + jnp.exp(-x))


def _rope(x, c, s1, s2):
    return x * c + pltpu.roll(x, SLAB - 16, 1) * s1 + pltpu.roll(x, 16, 1) * s2


def _rope_bwd(d, c, s1, s2):
    return d * c + pltpu.roll(d * s1, 16, 1) + pltpu.roll(d * s2, SLAB - 16, 1)


def _roll_rows(x, shift):
    return jnp.concatenate([x[-shift:], x[:-shift]], axis=0)


def _rope_t(x, c, s1, s2):
    return x * c + _roll_rows(x, SLAB - 16) * s1 + _roll_rows(x, 16) * s2


def _rope_t_bwd(d, c, s1, s2):
    return d * c + _roll_rows(d * s1, 16) + _roll_rows(d * s2, SLAB - 16)


def _plant_rows(slab, row, vals):
    hi = vals.astype(BF16).astype(F32)
    lo = (vals - hi).astype(BF16).astype(F32)
    idx = lax.broadcasted_iota(jnp.int32, slab.shape, 0)
    return jnp.where(idx == row, -hi, jnp.where(idx == row + 1, -lo, slab))


def _row_spec(tm, n):
    return pl.BlockSpec((tm, n), lambda i: (i, 0))


def _col_spec(n, tm):
    return pl.BlockSpec((n, tm), lambda i: (0, i))


def _full_spec(shape):
    nd = len(shape)
    return pl.BlockSpec(shape, lambda i: (0,) * nd, pipeline_mode=pl.Buffered(1))


def _acc_rows(ref, val):
    @pl.when(pl.program_id(0) == 0)
    def _():
        ref[...] = jnp.zeros_like(ref)
    ref[...] += jnp.sum(val, axis=0, keepdims=True)


def _rope_tables(pos_col, freq_row, early):
    t = pos_col.shape[0]
    tm = _token_tile(t)
    n = len(early)

    def body(pos_ref, f_ref, *rest):
        shard_refs, (c_ref, s1_ref, s2_ref, ct_ref, s1t_ref, s2t_ref) = rest[:n], rest[n:n + 6]
        start, finish = _two_level_gather(shard_refs, rest[n + 6:2 * n + 6], *rest[2 * n + 6:])
        pl.when(pl.program_id(0) == 0)(start)
        ang = pos_ref[...].astype(F32) * f_ref[...]
        lane = lax.broadcasted_iota(jnp.int32, ang.shape, 1)
        s = jnp.sin(ang)
        c = jnp.cos(ang)
        s1 = jnp.where((lane >= 64) & (lane < 80), -s, 0.0)
        s2 = jnp.where((lane >= 80) & (lane < 96), s, 0.0)
        c_ref[...], s1_ref[...], s2_ref[...] = c, s1, s2
        ct_ref[...], s1t_ref[...], s2t_ref[...] = c.T, s1.T, s2.T
        pl.when(pl.program_id(0) == t // tm - 1)(finish)

    tab = jax.ShapeDtypeStruct((t, SLAB), F32)
    tabt = jax.ShapeDtypeStruct((SLAB, t), F32)
    outs = pl.pallas_call(
        body, name="rope_tables", grid=(t // tm,),
        in_specs=[_row_spec(tm, 1), _full_spec((1, SLAB))] + [ANY_SPEC] * n,
        out_specs=[_row_spec(tm, SLAB)] * 3 + [_col_spec(SLAB, tm)] * 3 + [ANY_SPEC] * n,
        out_shape=[tab] * 3 + [tabt] * 3 + [jax.ShapeDtypeStruct((N_DEV,) + a.shape, a.dtype) for a in early],
        scratch_shapes=_exchange_scratch(n),
        compiler_params=_params(("arbitrary",)),
    )(pos_col, freq_row, *early)
    return outs[:6], outs[6:]


def _inproj_fwd(x, g1, w_in, g_q, g_kv, w_kvb, w_qb_t, w_kb_t, w_vb_t, w_kr_t, tables):
    t = x.shape[0]
    tm = _token_tile(t)

    def body(x_ref, g1_ref, win_ref, gq_ref, gkv_ref, wkvb_ref, wqbt_ref, wkbt_ref, wvbt_ref, wkrt_ref,
             c_ref, s1_ref, s2_ref, ct_ref, s1t_ref, s2t_ref,
             h_ref, gates_ref, qa_ref, ka_ref, va_ref, cq_ref, ckv_ref, cqn_ref, ckvn_ref,
             kb_ref, vb_ref, qt_ref, kt_ref, vt_ref):
        xv = x_ref[...]
        h = (xv * _rms_r(xv) * g1_ref[...]).astype(BF16)
        h_ref[...] = h
        proj = _dot(h, win_ref[...])
        gates_ref[...] = proj[:, C_GATES:C_QA].astype(BF16)
        qa_ref[...] = proj[:, C_QA:C_KA].astype(BF16)
        ka_ref[...] = proj[:, C_KA:C_VA].astype(BF16)
        va_ref[...] = proj[:, C_VA:C_CQ].astype(BF16)
        cq = proj[:, C_CQ:C_CKV]
        ckv = proj[:, C_CKV:C_KR]
        kr = proj[:, C_KR:D_IN_PAD]
        cq_ref[...] = cq
        ckv_ref[...] = ckv
        cqn = (cq * _rms_r(cq) * gq_ref[...]).astype(BF16)
        ckvn = (ckv * _rms_r(ckv) * gkv_ref[...]).astype(BF16)
        cqn_ref[...] = cqn
        ckvn_ref[...] = ckvn
        c, s1, s2 = c_ref[...], s1_ref[...], s2_ref[...]
        kvb = _dot(ckvn, wkvb_ref[...])
        kr_rot = _rope(kr, c, s1, s2)
        ct, s1t, s2t = ct_ref[...], s1t_ref[...], s2t_ref[...]
        q_t = _dot_nt(wqbt_ref[...], cqn)
        k_t = _dot_nt(wkbt_ref[...], ckvn)
        kr_t = _rope_t(_dot_nt(wkrt_ref[...], h), ct, s1t, s2t)
        k_lane = lax.broadcasted_iota(jnp.int32, (1, SLAB), 1)
        k_ones = jnp.where((k_lane == Q_HEAD_B) | (k_lane == Q_HEAD_B + 1), 1.0, 0.0)
        for hd in range(N_HEADS):
            sl = slice(hd * SLAB, (hd + 1) * SLAB)
            kb_ref[:, sl] = (kvb[:, sl] + kr_rot + k_ones).astype(BF16)
            qt_ref[sl, :] = (_rope_t(q_t[sl, :], ct, s1t, s2t) * SCORE_B).astype(BF16)
            kt_ref[sl, :] = (k_t[sl, :] + kr_t).astype(BF16)
        v_lane = lax.broadcasted_iota(jnp.int32, (1, HM), 1) & (SLAB - 1)
        v_ones = jnp.where((v_lane == V_DIM_B) | (v_lane == V_DIM_B + 1), 1.0, 0.0)
        vb_ref[...] = (kvb[:, HM:2 * HM] + v_ones).astype(BF16)
        pad_row = lax.broadcasted_iota(jnp.int32, (HM, 1), 0) & (SLAB - 1)
        ones_rows = jnp.where((pad_row >= V_DIM_B) & (pad_row < V_DIM_B + ONES_ROWS), 1.0, 0.0)
        vt_ref[...] = (_dot_nt(wvbt_ref[...], ckvn) + ones_rows).astype(BF16)

    def sds(n, dt):
        return jax.ShapeDtypeStruct((t, n), dt)

    outs = [(D_MODEL, BF16), (2 * D_MODEL, BF16), (HM, BF16), (N_KV_A * SLAB, BF16), (N_KV_A * SLAB, BF16),
            (Q_LORA, F32), (KV_LORA, F32), (Q_LORA, BF16), (KV_LORA, BF16), (HM, BF16), (HM, BF16)]
    tab, tabt = _row_spec(tm, SLAB), _col_spec(SLAB, tm)
    return pl.pallas_call(
        body, name="inproj_fwd", grid=(t // tm,),
        in_specs=[_row_spec(tm, D_MODEL), _full_spec((1, D_MODEL)), _full_spec((D_MODEL, D_IN_PAD)),
                  _full_spec((1, Q_LORA)), _full_spec((1, KV_LORA)), _full_spec((KV_LORA, 2 * HM)),
                  _full_spec((HM, Q_LORA)), _full_spec((HM, KV_LORA)), _full_spec((HM, KV_LORA)),
                  _full_spec((SLAB, D_MODEL)), tab, tab, tab, tabt, tabt, tabt],
        out_specs=[_row_spec(tm, n) for n, _ in outs] + [_col_spec(HM, tm)] * 3,
        out_shape=[sds(n, dt) for n, dt in outs] + [jax.ShapeDtypeStruct((HM, t), BF16)] * 3,
        compiler_params=_params(("parallel",)),
    )(x, g1, w_in, g_q, g_kv, w_kvb, w_qb_t, w_kb_t, w_vb_t, w_kr_t, *tables)


def _tile_group(a):
    return jnp.concatenate([a] * GROUP_A, axis=1)


def _swa_masks():
    row = lax.broadcasted_iota(jnp.int32, (BLOCK, GROUP_A * BLOCK), 0)
    col = lax.broadcasted_iota(jnp.int32, (BLOCK, GROUP_A * BLOCK), 1) & (BLOCK - 1)
    return row <= col, row > col


def _heads_beside(ref, g):
    return jnp.concatenate([ref[:, (g * GROUP_A + hh) * SLAB:(g * GROUP_A + hh + 1) * SLAB].T
                            for hh in range(GROUP_A)], axis=1)


def _rows_beside(ref, g):
    return jnp.concatenate([ref[g * GROUP_A + hh] for hh in range(GROUP_A)], axis=1)


def _swa_rows(sinks):
    slopes = jnp.repeat(jnp.asarray(SLOPES_A, F32).reshape(N_KV_A, GROUP_A, 1), BLOCK, axis=2)
    sink_rows = jnp.repeat(sinks.reshape(N_KV_A, GROUP_A, 1), BLOCK, axis=2)
    return slopes.reshape(N_KV_A, 1, GROUP_A * BLOCK), sink_rows.reshape(N_KV_A, 1, GROUP_A * BLOCK)


def _swa_fwd(qa, ka, va, pos_col, pos_row, sinks):
    t = qa.shape[0]
    nb = t // BLOCK
    gw = GROUP_A * BLOCK
    slope_rows, sink_rows = _swa_rows(sinks)

    def body(q_ref, kc_ref, kp_ref, vc_ref, vp_ref, pkc_ref, pkp_ref, pq_ref, slope_ref, sink_ref, o_ref, l_ref):
        i = pl.program_id(0)
        pq = pq_ref[...]
        dist_c = _tile_group(jnp.abs(pkc_ref[...] - pq).astype(F32))
        dist_p = _tile_group(jnp.abs(pkp_ref[...] - pq).astype(F32))
        mask_c, older = _swa_masks()
        mask_p = jnp.logical_and(older, i > 0)
        raw = []
        for g in range(N_KV_A):
            gs = slice(g * SLAB, (g + 1) * SLAB)
            x = _heads_beside(q_ref, g)
            raw.append((_dot(kc_ref[:, gs], x), _dot(kp_ref[:, gs], x)))
        for g in range(N_KV_A):
            gs = slice(g * SLAB, (g + 1) * SLAB)
            slope, sink = slope_ref[g], sink_ref[g]
            s_c = jnp.where(mask_c, raw[g][0] * SCALE_A - slope * dist_c, NEG)
            s_p = jnp.where(mask_p, raw[g][1] * SCALE_A - slope * dist_p, NEG)
            m = jnp.maximum(jnp.maximum(jnp.max(s_c, axis=0, keepdims=True),
                                        jnp.max(s_p, axis=0, keepdims=True)), sink)
            e_c = jnp.exp(s_c - m)
            e_p = jnp.exp(s_p - m)
            den = jnp.sum(e_c, axis=0, keepdims=True) + jnp.sum(e_p, axis=0, keepdims=True) + jnp.exp(sink - m)
            inv = 1.0 / den
            ot = (_dot_tn(vc_ref[:, gs], (e_c * inv).astype(BF16))
                  + _dot_tn(vp_ref[:, gs], (e_p * inv).astype(BF16)))
            lse = m + jnp.log(den)
            for hh in range(GROUP_A):
                hd = g * GROUP_A + hh
                seg = slice(hh * BLOCK, (hh + 1) * BLOCK)
                o_ref[:, hd * SLAB:(hd + 1) * SLAB] = ot[:, seg].T.astype(BF16)
                l_ref[hd] = lse[:, seg]

    cur = lambda i: (i, 0)
    prev = lambda i: (jnp.maximum(i - 1, 0), 0)
    kvw = N_KV_A * SLAB
    rows = pl.BlockSpec((N_KV_A, 1, gw), lambda i: (0, 0, 0))
    return pl.pallas_call(
        body, name="swa_fwd", grid=(nb,),
        in_specs=[pl.BlockSpec((BLOCK, HM), cur),
                  pl.BlockSpec((BLOCK, kvw), cur), pl.BlockSpec((BLOCK, kvw), prev),
                  pl.BlockSpec((BLOCK, kvw), cur), pl.BlockSpec((BLOCK, kvw), prev),
                  pl.BlockSpec((BLOCK, 1), cur), pl.BlockSpec((BLOCK, 1), prev),
                  pl.BlockSpec((1, BLOCK), lambda i: (0, i)), rows, rows],
        out_specs=[pl.BlockSpec((BLOCK, HM), cur), pl.BlockSpec((N_HEADS, 1, BLOCK), lambda i: (0, 0, i))],
        out_shape=[jax.ShapeDtypeStruct((t, HM), BF16), jax.ShapeDtypeStruct((N_HEADS, 1, t), F32)],
        compiler_params=_params(("parallel",)),
    )(qa, ka, ka, va, va, pos_col, pos_col, pos_row, slope_rows, sink_rows)


def _swa_bwd(qa, ka, va, out_a, d_oa, lse, pos_col, pos_row, sinks):
    t = qa.shape[0]
    nb = t // BLOCK
    gw = GROUP_A * BLOCK
    kvw = N_KV_A * SLAB
    slope_rows, sink_rows = _swa_rows(sinks)

    def body(q_ref, qn_ref, do_ref, don_ref, l_ref, ln_ref, o_ref, on_ref, kp_ref, kc_ref, vp_ref, vc_ref,
             pkp_ref, pkc_ref, pq_ref, pqn_ref, slope_ref, sink_ref, dqkv_ref, dsink_ref):
        j = pl.program_id(0)
        pkc, pkp = pkc_ref[...], pkp_ref[...]
        dist_cc = _tile_group(jnp.abs(pkc - pq_ref[...]).astype(F32))
        dist_cp = _tile_group(jnp.abs(pkp - pq_ref[...]).astype(F32))
        dist_nc = _tile_group(jnp.abs(pkc - pqn_ref[...]).astype(F32))
        mask_cc, older = _swa_masks()
        mask_cp = jnp.logical_and(older, j > 0)
        mask_nc = jnp.logical_and(older, j < nb - 1)

        @pl.when(j == 0)
        def _():
            dsink_ref[...] = jnp.zeros_like(dsink_ref)

        def tile(k, v, x, dox, lrow, drow, dist, mask, slope):
            s = jnp.where(mask, _dot(k, x) * SCALE_A - slope * dist, NEG)
            p = jnp.exp(s - lrow)
            ds = p * (_dot(v, dox) - drow)
            return p.astype(BF16), ds.astype(BF16)

        for g in range(N_KV_A):
            gs = slice(g * SLAB, (g + 1) * SLAB)
            kc, kp, vc, vp = kc_ref[:, gs], kp_ref[:, gs], vc_ref[:, gs], vp_ref[:, gs]
            slope, sink = slope_ref[g], sink_ref[g]
            x, xn = _heads_beside(q_ref, g), _heads_beside(qn_ref, g)
            dox, doxn = _heads_beside(do_ref, g), _heads_beside(don_ref, g)
            lrow, lrown = _rows_beside(l_ref, g), _rows_beside(ln_ref, g)
            drow = jnp.sum(dox.astype(F32) * _heads_beside(o_ref, g).astype(F32), axis=0, keepdims=True)
            drown = jnp.sum(doxn.astype(F32) * _heads_beside(on_ref, g).astype(F32), axis=0, keepdims=True)
            p_cc, ds_cc = tile(kc, vc, x, dox, lrow, drow, dist_cc, mask_cc, slope)
            _, ds_cp = tile(kp, vp, x, dox, lrow, drow, dist_cp, mask_cp, slope)
            p_nc, ds_nc = tile(kc, vc, xn, doxn, lrown, drown, dist_nc, mask_nc, slope)
            dqt = (_dot_tn(kc, ds_cc) + _dot_tn(kp, ds_cp)) * SCALE_A
            for hh in range(GROUP_A):
                hd = g * GROUP_A + hh
                dqkv_ref[:, hd * SLAB:(hd + 1) * SLAB] = dqt[:, hh * BLOCK:(hh + 1) * BLOCK].T.astype(BF16)
            dqkv_ref[:, HM + g * SLAB:HM + (g + 1) * SLAB] = (
                (_dot_nt(ds_cc, x) + _dot_nt(ds_nc, xn)) * SCALE_A).astype(BF16)
            dqkv_ref[:, HM + kvw + g * SLAB:HM + kvw + (g + 1) * SLAB] = (
                _dot_nt(p_cc, dox) + _dot_nt(p_nc, doxn)).astype(BF16)
            dsink_ref[g] -= jnp.exp(sink - lrow) * drow

    cur = lambda j: (j, 0)
    prev = lambda j: (jnp.maximum(j - 1, 0), 0)
    nxt = lambda j: (jnp.minimum(j + 1, nb - 1), 0)
    cur3 = lambda j: (0, 0, j)
    nxt3 = lambda j: (0, 0, jnp.minimum(j + 1, nb - 1))
    kvw = N_KV_A * SLAB
    rows = pl.BlockSpec((N_KV_A, 1, gw), lambda j: (0, 0, 0))
    stat = lambda im: pl.BlockSpec((N_HEADS, 1, BLOCK), im)
    return pl.pallas_call(
        body, name="swa_bwd", grid=(nb,),
        in_specs=[pl.BlockSpec((BLOCK, HM), cur), pl.BlockSpec((BLOCK, HM), nxt),
                  pl.BlockSpec((BLOCK, HM), cur), pl.BlockSpec((BLOCK, HM), nxt),
                  stat(cur3), stat(nxt3), pl.BlockSpec((BLOCK, HM), cur), pl.BlockSpec((BLOCK, HM), nxt),
                  pl.BlockSpec((BLOCK, kvw), prev), pl.BlockSpec((BLOCK, kvw), cur),
                  pl.BlockSpec((BLOCK, kvw), prev), pl.BlockSpec((BLOCK, kvw), cur),
                  pl.BlockSpec((BLOCK, 1), prev), pl.BlockSpec((BLOCK, 1), cur),
                  pl.BlockSpec((1, BLOCK), lambda j: (0, j)),
                  pl.BlockSpec((1, BLOCK), lambda j: (0, jnp.minimum(j + 1, nb - 1))), rows, rows],
        out_specs=[pl.BlockSpec((BLOCK, HM + 2 * kvw), cur), rows],
        out_shape=[jax.ShapeDtypeStruct((t, HM + 2 * kvw), BF16), jax.ShapeDtypeStruct((N_KV_A, 1, gw), F32)],
        compiler_params=_params(("arbitrary",)),
    )(qa, qa, d_oa, d_oa, lse, lse, out_a, out_a, ka, ka, va, va,
      pos_col, pos_col, pos_row, pos_row, slope_rows, sink_rows)


def _mesh_pos():
    return lax.axis_index("x"), lax.axis_index("y"), lax.axis_index("c")


def _flip(v, bit):
    return 1 - v if bit else v


def _direct_copies(srcs, dsts, send_sems, recv_sems, local_sems, gather, sem_base=0, only=None):
    x, y, c = _mesh_pos()
    me = 4 * x + 2 * y + c

    def among(idx, dests):
        ok = idx == dests[0]
        for d in dests[1:]:
            ok = jnp.logical_or(ok, idx == d)
        return ok

    local, remote = [], []
    for a, (src, dst) in enumerate(zip(srcs, dsts)):
        dests = None if only is None else only[a]
        recv_ok = None if dests is None else among(me, dests)
        local.append((pltpu.make_async_copy(src if gather else src.at[me], dst.at[me],
                                            local_sems.at[sem_base + a]), recv_ok))
        for r in range(1, N_DEV):
            px, py, pc = _flip(x, r & 4), _flip(y, r & 2), _flip(c, r & 1)
            peer = 4 * px + 2 * py + pc
            sem = (N_DEV - 1) * (sem_base + a) + r - 1
            copy = pltpu.make_async_remote_copy(
                src_ref=src if gather else src.at[peer], dst_ref=dst.at[me],
                send_sem=send_sems.at[sem], recv_sem=recv_sems.at[sem],
                device_id=(px, py, pc), device_id_type=pl.DeviceIdType.MESH)
            remote.append((copy, None if dests is None else among(peer, dests), recv_ok))
    return local, remote


def _when(cond, fn):
    if cond is None:
        fn()
    else:
        pl.when(cond)(fn)


def _start_copies(local, remote):
    for cp, ok in local:
        _when(ok, cp.start)
    for cp, send_ok, _ in remote:
        _when(send_ok, cp.start)


def _wait_copies(local, remote):
    for cp, _, recv_ok in remote:
        _when(recv_ok, cp.wait_recv)
    for cp, send_ok, _ in remote:
        _when(send_ok, cp.wait_send)
    for cp, ok in local:
        _when(ok, cp.wait)


def _exchange_scratch(n):
    return [pltpu.SemaphoreType.DMA((n * (N_DEV - 1),)), pltpu.SemaphoreType.DMA((n * (N_DEV - 1),)),
            pltpu.SemaphoreType.DMA((n,))]


ANY_SPEC = pl.BlockSpec(memory_space=pl.ANY)


def _mla_fwd(qt, kb, vt, late):
    t = kb.shape[0]
    tk = _attn_tile(t)
    ratio = 2 if t >= 2 * tk else 1
    tq = ratio * tk
    nq = t // tq
    hps = MLA_FWD_HEADS_PER_STEP
    w = hps * SLAB
    pairs = [(i, j) for i in range(nq) for j in range(ratio * (i + 1))]
    i_tab = jnp.asarray(np.array([p[0] for p in pairs], np.int32))
    j_tab = jnp.asarray(np.array([p[1] for p in pairs], np.int32))

    n_late = len(late)

    def body(it_ref, jt_ref, qt_ref, k_ref, vt_ref, *rest):
        late_refs, (o_ref, ot_ref, qa_ref) = rest[:n_late], rest[n_late:n_late + 3]
        gathered_refs = rest[n_late + 3:2 * n_late + 3]
        m_s, acc_s, send_sems, recv_sems, local_sems = rest[2 * n_late + 3:]
        n = pl.program_id(1)
        i, j = it_ref[n], jt_ref[n]
        first_step = jnp.logical_and(pl.program_id(0) == 0, n == 0)
        last_step = jnp.logical_and(pl.program_id(0) == N_HEADS // hps - 1, n == len(pairs) - 1)

        @pl.when(first_step)
        def _():
            _start_copies(*_direct_copies(late_refs, gathered_refs, send_sems, recv_sems, local_sems, True))

        @pl.when(j == 0)
        def _():
            m_s[...] = jnp.full_like(m_s, NEG)
            acc_s[...] = jnp.zeros_like(acc_s)

        def update(masked, q0):
            qc = slice(q0, tq)

            def scores(hh):
                sl = slice(hh * SLAB, (hh + 1) * SLAB)
                return _dot(k_ref[:, sl], qt_ref[sl, qc])

            def softmax(hh, s):
                if masked:
                    s = jnp.where(lax.broadcasted_iota(jnp.int32, s.shape, 0)
                                  <= lax.broadcasted_iota(jnp.int32, s.shape, 1), s, NEG)
                m_old = m_s[hh][:, qc]
                m_new = jnp.maximum(m_old, jnp.max(s, axis=0, keepdims=True))
                m_s[hh, :, qc] = m_new
                return jnp.exp2(s - m_new).astype(BF16), jnp.exp2(m_old - m_new)

            def accumulate(hh, p, alpha):
                sl = slice(hh * SLAB, hh * SLAB + V_DIM_B + ONES_ROWS)
                acc_s[sl, qc] = alpha * acc_s[sl, qc] + _dot(vt_ref[sl, :], p)

            s_next, pending = scores(0), None
            for hh in range(hps):
                s = s_next
                if hh + 1 < hps:
                    s_next = scores(hh + 1)
                p, alpha = softmax(hh, s)
                if pending is not None:
                    accumulate(*pending)
                pending = (hh, p, alpha)
            accumulate(*pending)

        @pl.when(j < ratio * i)
        def _():
            update(False, 0)

        for part in range(ratio):
            @pl.when(j == ratio * i + part)
            def _():
                update(True, part * tk)

        @pl.when(j == ratio * i + ratio - 1)
        def _():
            for hh in range(hps):
                sl = slice(hh * SLAB, (hh + 1) * SLAB)
                den = acc_s[hh * SLAB + V_DIM_B:hh * SLAB + V_DIM_B + 1, :]
                values = lax.broadcasted_iota(jnp.int32, (SLAB, tq), 0) < V_DIM_B
                ot = jnp.where(values, acc_s[sl, :] / den, 0.0)
                ot_ref[sl, :] = ot.astype(BF16)
                o_ref[:, sl] = ot.T.astype(BF16)
                lse = m_s[hh] + jnp.log2(den)
                qa_ref[sl, :] = _plant_rows(qt_ref[sl, :].astype(F32), Q_HEAD_B, lse).astype(BF16)

        @pl.when(last_step)
        def _():
            _wait_copies(*_direct_copies(late_refs, gathered_refs, send_sems, recv_sems, local_sems, True))

    grid_spec = pltpu.PrefetchScalarGridSpec(
        num_scalar_prefetch=2, grid=(N_HEADS // hps, len(pairs)),
        in_specs=[pl.BlockSpec((w, tq), lambda h, n, it, jt: (h, it[n])),
                  pl.BlockSpec((tk, w), lambda h, n, it, jt: (jt[n], h)),
                  pl.BlockSpec((w, tk), lambda h, n, it, jt: (h, jt[n]))] + [ANY_SPEC] * n_late,
        out_specs=[pl.BlockSpec((tq, w), lambda h, n, it, jt: (it[n], h)),
                   pl.BlockSpec((w, tq), lambda h, n, it, jt: (h, it[n])),
                   pl.BlockSpec((w, tq), lambda h, n, it, jt: (h, it[n]))] + [ANY_SPEC] * n_late,
        scratch_shapes=[pltpu.VMEM((hps, 1, tq), F32), pltpu.VMEM((w, tq), F32)] + _exchange_scratch(n_late))
    outs = pl.pallas_call(
        body, name="mla_fwd", grid_spec=grid_spec,
        out_shape=[jax.ShapeDtypeStruct((t, HM), BF16), jax.ShapeDtypeStruct((HM, t), BF16),
                   jax.ShapeDtypeStruct((HM, t), BF16)]
        + [jax.ShapeDtypeStruct((N_DEV,) + a.shape, a.dtype) for a in late],
        compiler_params=_params(("arbitrary", "arbitrary")),
    )(i_tab, j_tab, qt, kb, vt, *late)
    return outs[0], outs[1], outs[2], list(outs[3:])


def _mla_bwd(qt, kb, kt, vb, d_ob_t, grad_slices):
    t = kb.shape[0]
    tk = _attn_tile(t)
    ratio = 2 if t >= 2 * tk else 1
    tq = ratio * tk
    nk, nq = t // tk, t // tq
    hps = MLA_HEADS_PER_STEP
    w = hps * SLAB
    pairs = [(j, i) for j in range(nk) for i in range(j // ratio, nq)]
    j_tab = jnp.asarray(np.array([p[0] for p in pairs], np.int32))
    i_tab = jnp.asarray(np.array([p[1] for p in pairs], np.int32))

    n_ex = len(grad_slices)

    def body(jt_ref, it_ref, qt_ref, dot_ref, k_ref, kt_ref, v_ref, *rest):
        slice_refs, (dqt_ref, dkt_ref, dvt_ref) = rest[:n_ex], rest[n_ex:n_ex + 3]
        part_refs = rest[n_ex + 3:2 * n_ex + 3]
        dk_s, dv_s, send_sems, recv_sems, local_sems = rest[2 * n_ex + 3:]
        n = pl.program_id(1)
        j, i = jt_ref[n], it_ref[n]
        first_step = jnp.logical_and(pl.program_id(0) == 0, n == 0)
        last_step = jnp.logical_and(pl.program_id(0) == N_HEADS // hps - 1, n == len(pairs) - 1)

        @pl.when(first_step)
        def _():
            _start_copies(*_direct_copies(slice_refs, part_refs, send_sems, recv_sems, local_sems, False))

        @pl.when(n == 0)
        def _():
            dqt_ref[...] = jnp.zeros_like(dqt_ref)

        def update(diagonal, q0):
            qc = slice(q0, tq)
            cols = pl.ds(pl.multiple_of(i * tq + q0, tk), tq - q0)

            def softmax_bwd(hh, s, dp):
                if diagonal:
                    s = jnp.where(lax.broadcasted_iota(jnp.int32, s.shape, 0)
                                  <= lax.broadcasted_iota(jnp.int32, s.shape, 1), s, NEG)
                p = jnp.exp2(s)
                return p.astype(BF16), (p * dp).astype(BF16)

            def gradients(hh, p, ds):
                base = hh * SLAB
                vrows = slice(base, base + V_DIM_B)
                qrows = slice(base, base + QK_NOPE + QK_ROPE)
                dv = _dot_nt(dot_ref[vrows, qc], p)
                dk = _dot_nt(qt_ref[qrows, qc], ds)
                if diagonal:
                    dv_s[base:base + SLAB, :] = jnp.concatenate([dv, jnp.zeros((SLAB - V_DIM_B, tk), F32)], axis=0)
                    dk_s[base:base + SLAB, :] = jnp.concatenate(
                        [dk, jnp.zeros((SLAB - QK_NOPE - QK_ROPE, tk), F32)], axis=0)
                else:
                    dv_s[vrows, :] += dv
                    dk_s[qrows, :] += dk
                dqt_ref[qrows, cols] += _dot(kt_ref[qrows, :], ds)

            def scores(hh):
                sl = slice(hh * SLAB, (hh + 1) * SLAB)
                return _dot(k_ref[:, sl], qt_ref[sl, qc])

            def dprod(hh):
                sl = slice(hh * SLAB, (hh + 1) * SLAB)
                return _dot(v_ref[:, sl], dot_ref[sl, qc])

            s_next = scores(0)
            for hh in range(hps):
                s = s_next
                dp = dprod(hh)
                if hh + 1 < hps:
                    s_next = scores(hh + 1)
                gradients(hh, *softmax_bwd(hh, s, dp))

        first_tile = lax.div(j, ratio)
        for part in range(ratio):
            @pl.when(jnp.logical_and(i == first_tile, lax.rem(j, ratio) == part))
            def _():
                update(True, part * tk)

        @pl.when(i > first_tile)
        def _():
            update(False, 0)

        @pl.when(i == nq - 1)
        def _():
            dkt_ref[...] = (dk_s[...] * (1.0 / LOG2E)).astype(BF16)
            dvt_ref[...] = dv_s[...].astype(BF16)

        @pl.when(last_step)
        def _():
            _wait_copies(*_direct_copies(slice_refs, part_refs, send_sems, recv_sems, local_sems, False))

    grid_spec = pltpu.PrefetchScalarGridSpec(
        num_scalar_prefetch=2, grid=(N_HEADS // hps, len(pairs)),
        in_specs=[pl.BlockSpec((w, tq), lambda h, n, jt, it: (h, it[n])),
                  pl.BlockSpec((w, tq), lambda h, n, jt, it: (h, it[n])),
                  pl.BlockSpec((tk, w), lambda h, n, jt, it: (jt[n], h)),
                  pl.BlockSpec((w, tk), lambda h, n, jt, it: (h, jt[n])),
                  pl.BlockSpec((tk, w), lambda h, n, jt, it: (jt[n], h))] + [ANY_SPEC] * n_ex,
        out_specs=[pl.BlockSpec((w, t), lambda h, n, jt, it: (h, 0)),
                   pl.BlockSpec((w, tk), lambda h, n, jt, it: (h, jt[n])),
                   pl.BlockSpec((w, tk), lambda h, n, jt, it: (h, jt[n]))] + [ANY_SPEC] * n_ex,
        scratch_shapes=[pltpu.VMEM((w, tk), F32), pltpu.VMEM((w, tk), F32)] + _exchange_scratch(n_ex))
    outs = pl.pallas_call(
        body, name="mla_bwd", grid_spec=grid_spec,
        out_shape=[jax.ShapeDtypeStruct((HM, t), F32), jax.ShapeDtypeStruct((HM, t), BF16),
                   jax.ShapeDtypeStruct((HM, t), BF16)]
        + [jax.ShapeDtypeStruct(a.shape, a.dtype) for a in grad_slices],
        compiler_params=_params(("arbitrary", "arbitrary")),
    )(j_tab, i_tab, qt, d_ob_t, kb, kt, vb, *grad_slices)
    return outs[0], outs[1], outs[2], list(outs[3:])


def _merge_fwd(out_a, out_b, gates, x, w_oa, w_ob, w_out, g2, g3):
    t = x.shape[0]
    tm = _token_tile(t)

    def body(oa_ref, ob_ref, gates_ref, x_ref, woa_ref, wob_ref, wout_ref, g2_ref, g3_ref,
             oap_ref, obp_ref, merged_ref, y_ref, x1_ref, h2_ref):
        oa_p = _dot(oa_ref[...], woa_ref[...])
        ob_p = _dot(ob_ref[...], wob_ref[...])
        oap_ref[...] = oa_p.astype(BF16)
        obp_ref[...] = ob_p.astype(BF16)
        sa = _sigmoid(gates_ref[:, 0:D_MODEL].astype(F32))
        sb = _sigmoid(gates_ref[:, D_MODEL:2 * D_MODEL].astype(F32))
        merged = (sa * oa_p + sb * ob_p).astype(BF16)
        merged_ref[...] = merged
        y = _dot(merged, wout_ref[...])
        y_ref[...] = y
        x1 = x_ref[...] + y * _rms_r(y) * g2_ref[...]
        x1_ref[...] = x1
        h2_ref[...] = (x1 * _rms_r(x1) * g3_ref[...]).astype(BF16)

    def sds(dt):
        return jax.ShapeDtypeStruct((t, D_MODEL), dt)

    row = _row_spec(tm, D_MODEL)
    return pl.pallas_call(
        body, name="merge_fwd", grid=(t // tm,),
        in_specs=[_row_spec(tm, HM), _row_spec(tm, HM), _row_spec(tm, 2 * D_MODEL), row,
                  _full_spec((HM, D_MODEL)), _full_spec((HM, D_MODEL)), _full_spec((D_MODEL, D_MODEL)),
                  _full_spec((1, D_MODEL)), _full_spec((1, D_MODEL))],
        out_specs=[row] * 6,
        out_shape=[sds(BF16), sds(BF16), sds(BF16), sds(F32), sds(F32), sds(BF16)],
        compiler_params=_params(("parallel",)),
    )(out_a, out_b, gates, x, w_oa, w_ob, w_out, g2, g3)


def _merge_bwd(dx1, y, gates, oa_p, ob_p, out_a, out_b, out_b_t, merged, w_oa, w_ob, w_out, g2):
    t = dx1.shape[0]
    tm = _token_tile(t)

    def body(dx1_ref, y_ref, gates_ref, oap_ref, obp_ref, oa_ref, ob_ref, obt_ref, merged_ref,
             woa_ref, wob_ref, wout_ref, g2_ref,
             dgates_ref, doa_ref, dobt_ref, dg2_ref, dwoa_ref, dwob_ref, dwout_ref):
        @pl.when(pl.program_id(0) == 0)
        def _():
            dwoa_ref[...] = jnp.zeros_like(dwoa_ref)
            dwob_ref[...] = jnp.zeros_like(dwob_ref)
            dwout_ref[...] = jnp.zeros_like(dwout_ref)

        dx1v = dx1_ref[...]
        yv = y_ref[...]
        r2 = _rms_r(yv)
        _acc_rows(dg2_ref, dx1v * yv * r2)
        dy = _rms_bwd(yv, r2, g2_ref[...], dx1v).astype(BF16)
        dwout_ref[...] += _dot_tn(merged_ref[...], dy)
        dm = _dot_nt(dy, wout_ref[...])
        sa = _sigmoid(gates_ref[:, 0:D_MODEL].astype(F32))
        sb = _sigmoid(gates_ref[:, D_MODEL:2 * D_MODEL].astype(F32))
        d_oap = (dm * sa).astype(BF16)
        d_obp = (dm * sb).astype(BF16)
        dwoa_ref[...] += _dot_tn(oa_ref[...], d_oap)
        dwob_ref[...] += _dot_tn(ob_ref[...], d_obp)
        dgates_ref[:, 0:D_MODEL] = (dm * oap_ref[...].astype(F32) * sa * (1.0 - sa)).astype(BF16)
        dgates_ref[:, D_MODEL:2 * D_MODEL] = (dm * obp_ref[...].astype(F32) * sb * (1.0 - sb)).astype(BF16)
        doa_ref[...] = _dot_nt(d_oap, woa_ref[...]).astype(BF16)
        d_ob_t = _dot_nt(wob_ref[...], d_obp)
        for hd in range(N_HEADS):
            sl = slice(hd * SLAB, (hd + 1) * SLAB)
            delta = jnp.sum(d_ob_t[sl, :] * obt_ref[sl, :].astype(F32), axis=0, keepdims=True)
            dobt_ref[sl, :] = _plant_rows(d_ob_t[sl, :], V_DIM_B, delta).astype(BF16)

    def sds(n, dt):
        return jax.ShapeDtypeStruct((t, n), dt)

    row = _row_spec(tm, D_MODEL)
    return pl.pallas_call(
        body, name="merge_bwd", grid=(t // tm,),
        in_specs=[row, row, _row_spec(tm, 2 * D_MODEL), row, row, _row_spec(tm, HM), _row_spec(tm, HM),
                  _col_spec(HM, tm), row,
                  _full_spec((HM, D_MODEL)), _full_spec((HM, D_MODEL)), _full_spec((D_MODEL, D_MODEL)),
                  _full_spec((1, D_MODEL))],
        out_specs=[_row_spec(tm, 2 * D_MODEL), _row_spec(tm, HM), _col_spec(HM, tm), _full_spec((1, D_MODEL)),
                   _full_spec((HM, D_MODEL)), _full_spec((HM, D_MODEL)), _full_spec((D_MODEL, D_MODEL))],
        out_shape=[sds(2 * D_MODEL, BF16), sds(HM, BF16), jax.ShapeDtypeStruct((HM, t), BF16),
                   jax.ShapeDtypeStruct((1, D_MODEL), F32),
                   jax.ShapeDtypeStruct((HM, D_MODEL), F32), jax.ShapeDtypeStruct((HM, D_MODEL), F32),
                   jax.ShapeDtypeStruct((D_MODEL, D_MODEL), F32)],
        compiler_params=_params(("arbitrary",)),
    )(dx1, y, gates, oa_p, ob_p, out_a, out_b, out_b_t, merged, w_oa, w_ob, w_out, g2)


def _mlp_fwd_bwd(x1, h2, target, w_up, w_down, g3, g4):
    t = x1.shape[0]
    tm = _token_tile(t)
    fs = D_FF // N_DEV

    def body(x1_ref, h2_ref, tgt_ref, wup_ref, wdown_ref, g3_ref, g4_ref,
             a_ref, du_ref, dy2_ref, dx1_ref, loss_ref, dg3_ref, dg4_ref):
        x1v = x1_ref[...]
        h2v = h2_ref[...]
        u = jnp.concatenate([_dot(h2v, wup_ref[s]) for s in range(N_DEV)], axis=1)
        ru = jnp.maximum(u, 0.0)
        a = (ru * ru).astype(BF16)
        a_ref[...] = a
        y2 = _dot(a, wdown_ref[...])
        r4 = _rms_r(y2)
        diff = x1v + y2 * r4 * g4_ref[...] - tgt_ref[...]
        _acc_rows(loss_ref, jnp.sum(diff * diff, axis=-1, keepdims=True) * (0.5 / D_MODEL)
                  * jnp.ones((1, SLAB), F32))
        dx2 = diff * (1.0 / D_MODEL)
        _acc_rows(dg4_ref, dx2 * y2 * r4)
        dy2 = _rms_bwd(y2, r4, g4_ref[...], dx2).astype(BF16)
        dy2_ref[...] = dy2
        du = (_dot_nt(dy2, wdown_ref[...]) * (2.0 * ru)).astype(BF16)
        du_ref[...] = du
        dh2 = _dot_nt(du[:, 0:fs], wup_ref[0])
        for s in range(1, N_DEV):
            dh2 += _dot_nt(du[:, s * fs:(s + 1) * fs], wup_ref[s])
        r3 = _rms_r(x1v)
        _acc_rows(dg3_ref, dh2 * x1v * r3)
        dx1_ref[...] = dx2 + _rms_bwd(x1v, r3, g3_ref[...], dh2)

    row = _row_spec(tm, D_MODEL)
    frow = _row_spec(tm, D_FF)
    vec = _full_spec((1, D_MODEL))
    return pl.pallas_call(
        body, name="mlp_fwd_bwd", grid=(t // tm,),
        in_specs=[row, row, row, _full_spec((N_DEV, D_MODEL, fs)), _full_spec((D_FF, D_MODEL)), vec, vec],
        out_specs=[frow, frow, row, row, _full_spec((1, SLAB)), vec, vec],
        out_shape=[jax.ShapeDtypeStruct((t, D_FF), BF16), jax.ShapeDtypeStruct((t, D_FF), BF16),
                   jax.ShapeDtypeStruct((t, D_MODEL), BF16), jax.ShapeDtypeStruct((t, D_MODEL), F32),
                   jax.ShapeDtypeStruct((1, SLAB), F32), jax.ShapeDtypeStruct((1, D_MODEL), F32),
                   jax.ShapeDtypeStruct((1, D_MODEL), F32)],
        compiler_params=_params(("arbitrary",)),
    )(x1, h2, target, w_up, w_down, g3, g4)


def _inproj_bwd(dgates, dqkv, dqb_t, dkb_t, dvb_t, cq, ckv, cqn, ckvn, x, dx1, rope_ct, rope_s1t, rope_s2t,
                g1, g_q, g_kv, w_in, w_qb, w_kvb):
    t = x.shape[0]
    tm = _token_tile(t)

    def body(dgates_ref, dqkv_ref, dqt_ref, dkt_ref, dvt_ref, cq_ref, ckv_ref, cqn_ref, ckvn_ref,
             x_ref, dx1_ref, ct_ref, s1t_ref, s2t_ref, g1_ref, gq_ref, gkv_ref, win_ref, wqb_ref, wkvb_ref,
             dlate_ref, dx_ref, dg1_ref, dgq_ref, dgkv_ref, dwqb_ref, dwkvb_ref, dqbrt_ref, dkvbt_ref, dproj_ref):
        @pl.when(pl.program_id(0) == 0)
        def _():
            dwqb_ref[...] = jnp.zeros_like(dwqb_ref)
            dwkvb_ref[...] = jnp.zeros_like(dwkvb_ref)

        ct, s1t, s2t = ct_ref[...], s1t_ref[...], s2t_ref[...]
        dk_sum_t = jnp.zeros((SLAB, tm), F32)
        for hd in range(N_HEADS):
            sl = slice(hd * SLAB, (hd + 1) * SLAB)
            dqbrt_ref[sl, :] = _rope_t_bwd(dqt_ref[sl, :] * SCALE_B, ct, s1t, s2t).astype(BF16)
            dk_sum_t += dkt_ref[sl, :].astype(F32)
        dkvbt_ref[0:HM, :] = dkt_ref[...]
        dkvbt_ref[HM:2 * HM, :] = dvt_ref[...]
        dkr = _rope_t_bwd(dk_sum_t, ct, s1t, s2t).T
        dwqb_ref[...] += _dot(dqbrt_ref[...], cqn_ref[...])
        dwkvb_ref[...] += _dot(dkvbt_ref[...], ckvn_ref[...])
        dcqn = _dot(wqb_ref[...], dqbrt_ref[...]).T
        cq = cq_ref[...]
        rq = _rms_r(cq)
        _acc_rows(dgq_ref, dcqn * cq * rq)
        dcq = _rms_bwd(cq, rq, gq_ref[...], dcqn)
        dckvn = _dot(wkvb_ref[...], dkvbt_ref[...]).T
        ckv = ckv_ref[...]
        rkv = _rms_r(ckv)
        _acc_rows(dgkv_ref, dckvn * ckv * rkv)
        dckv = _rms_bwd(ckv, rkv, gkv_ref[...], dckvn)
        dproj_ref[:, C_GATES:C_QA] = dgates_ref[...]
        dproj_ref[:, C_QA:C_CQ] = dqkv_ref[...]
        dproj_ref[:, C_CQ:C_CKV] = dcq.astype(BF16)
        dproj_ref[:, C_CKV:C_KR] = dckv.astype(BF16)
        dproj_ref[:, C_KR:D_IN_PAD] = dkr.astype(BF16)
        dlate_ref[...] = dproj_ref[:, C_CQ:D_IN_PAD]
        dh = _dot_nt(dproj_ref[...], win_ref[...])
        xv = x_ref[...]
        r1 = _rms_r(xv)
        _acc_rows(dg1_ref, dh * xv * r1)
        dx_ref[...] = dx1_ref[...] + _rms_bwd(xv, r1, g1_ref[...], dh)

    kvw = N_KV_A * SLAB
    row = _row_spec(tm, D_MODEL)
    hmt = _col_spec(HM, tm)
    tab = _col_spec(SLAB, tm)
    return pl.pallas_call(
        body, name="inproj_bwd", grid=(t // tm,),
        in_specs=[_row_spec(tm, 2 * D_MODEL), _row_spec(tm, HM + 2 * kvw), hmt, hmt, hmt,
                  _row_spec(tm, Q_LORA), _row_spec(tm, KV_LORA), _row_spec(tm, Q_LORA), _row_spec(tm, KV_LORA),
                  row, row, tab, tab, tab,
                  _full_spec((1, D_MODEL)), _full_spec((1, Q_LORA)), _full_spec((1, KV_LORA)),
                  _full_spec((D_MODEL, D_IN_PAD)), _full_spec((Q_LORA, HM)), _full_spec((KV_LORA, 2 * HM))],
        out_specs=[_row_spec(tm, D_IN_PAD - C_CQ), row,
                   _full_spec((1, D_MODEL)), _full_spec((1, Q_LORA)), _full_spec((1, KV_LORA)),
                   _full_spec((HM, Q_LORA)), _full_spec((2 * HM, KV_LORA))],
        out_shape=[jax.ShapeDtypeStruct((t, D_IN_PAD - C_CQ), BF16), jax.ShapeDtypeStruct((t, D_MODEL), F32),
                   jax.ShapeDtypeStruct((1, D_MODEL), F32), jax.ShapeDtypeStruct((1, Q_LORA), F32),
                   jax.ShapeDtypeStruct((1, KV_LORA), F32),
                   jax.ShapeDtypeStruct((HM, Q_LORA), F32), jax.ShapeDtypeStruct((2 * HM, KV_LORA), F32)],
        scratch_shapes=[pltpu.VMEM((HM, tm), BF16), pltpu.VMEM((2 * HM, tm), BF16), pltpu.VMEM((tm, D_IN_PAD), BF16)],
        compiler_params=_params(("arbitrary",)),
    )(dgates, dqkv, dqb_t, dkb_t, dvb_t, cq, ckv, cqn, ckvn, x, dx1, rope_ct, rope_s1t, rope_s2t,
      g1, g_q, g_kv, w_in, w_qb, w_kvb)


def _matmul_tn(a, b, name, out_dtype=F32, n_shards=1):
    t, k = a.shape
    n = b.shape[1]
    bt = min(t, 512)
    bn = min(n, 2048)
    bk = min(k, 2048 * 1024 // bn)
    ns = n // n_shards
    per_block = bn // ns
    steps = t // bt

    def body(a_ref, b_ref, o_ref, acc):
        s = pl.program_id(2)

        @pl.when(s == 0)
        def _():
            acc[...] = jnp.zeros_like(acc)

        acc[...] += _dot_tn(a_ref[...], b_ref[...])

        @pl.when(s == steps - 1)
        def _():
            if n_shards > 1:
                for p in range(per_block):
                    o_ref[p] = acc[:, p * ns:(p + 1) * ns].astype(out_dtype)
            else:
                o_ref[...] = acc[...].astype(out_dtype)

    if n_shards > 1:
        out_spec = pl.BlockSpec((per_block, bk, ns), lambda i, j, s: (j, i, 0))
        out_shape = jax.ShapeDtypeStruct((n_shards, k, ns), out_dtype)
    else:
        out_spec = pl.BlockSpec((bk, bn), lambda i, j, s: (i, j))
        out_shape = jax.ShapeDtypeStruct((k, n), out_dtype)
    return pl.pallas_call(
        body, name=name, grid=(k // bk, n // bn, steps),
        in_specs=[pl.BlockSpec((bt, bk), lambda i, j, s: (s, i)), pl.BlockSpec((bt, bn), lambda i, j, s: (s, j))],
        out_specs=out_spec, out_shape=out_shape, scratch_shapes=[pltpu.VMEM((bk, bn), F32)],
        compiler_params=_params(("parallel", "parallel", "arbitrary")),
    )(a, b)


def _two_level_gather(srcs, dsts, send_sems, recv_sems, local_sems):
    n = len(srcs)
    x, y, c = _mesh_pos()
    me, sibling = (x, y, c), (x, y, 1 - c)
    chips = [(1 - x, y), (x, 1 - y), (1 - x, 1 - y)]

    def slot(a, px, py, pc):
        return dsts[a].at[4 * px + 2 * py + pc]

    def copy(a, k, block, to, src=None):
        return pltpu.make_async_remote_copy(
            src_ref=slot(a, *block) if src is None else src, dst_ref=slot(a, *block),
            send_sem=send_sems.at[(N_DEV - 1) * a + k], recv_sem=recv_sems.at[(N_DEV - 1) * a + k],
            device_id=to, device_id_type=pl.DeviceIdType.MESH)

    def own_copies():
        mine = [pltpu.make_async_copy(srcs[a], slot(a, *me), local_sems.at[a]) for a in range(n)]
        first = []
        for a in range(n):
            first.append(copy(a, 0, me, sibling, src=srcs[a]))
            first += [copy(a, 1 + j, me, (*chip, c), src=srcs[a]) for j, chip in enumerate(chips)]
        return mine, first

    def start():
        mine, first = own_copies()
        for cp in mine + first:
            cp.start()

    def finish():
        mine, first = own_copies()
        passed = []
        for j, chip in enumerate(chips):
            for a in range(n):
                copy(a, 1 + j, (*chip, c), me).wait_recv()
                passed.append(copy(a, 4 + j, (*chip, c), sibling))
                passed[-1].start()
        for a in range(n):
            copy(a, 0, sibling, me).wait_recv()
        for j, chip in enumerate(chips):
            for a in range(n):
                copy(a, 4 + j, (*chip, 1 - c), me).wait_recv()
        for cp in first + passed:
            cp.wait_send()
        for cp in mine:
            cp.wait()

    return start, finish


def _exchange_grads(slices, small, only):
    n = len(slices)
    zeroed = [a for a in range(n) if only[a] is not None]

    def body(*refs):
        srcs, s_ref = refs[:n], refs[n]
        outs = refs[n + 1 + len(zeroed):]
        dsts, s_dst, sems = outs[:n], outs[n], outs[n + 1:]
        parts = _direct_copies(srcs, dsts, *sems, False, only=only)
        smalls = _direct_copies([s_ref], [s_dst], *sems, True, sem_base=n)
        _start_copies(*parts)
        _start_copies(*smalls)
        _wait_copies(*parts)
        _wait_copies(*smalls)

    outs = pl.pallas_call(
        body, name="exchange_grads",
        out_shape=[jax.ShapeDtypeStruct(a.shape, a.dtype) for a in slices]
        + [jax.ShapeDtypeStruct((N_DEV,) + small.shape, small.dtype)],
        in_specs=[ANY_SPEC] * (n + 1 + len(zeroed)), out_specs=[ANY_SPEC] * (n + 1),
        input_output_aliases={n + 1 + i: a for i, a in enumerate(zeroed)},
        scratch_shapes=_exchange_scratch(n + 1),
    )(*slices, small, *[jnp.zeros_like(slices[a]) for a in zeroed])
    return list(outs[:n]), outs[n]


def _adamw(parts, w, m, v, name):
    n_parts = len(parts)
    _, k, n = parts[0].shape
    bk = min(k, ADAM_ROWS)
    c1 = 1.0 - ADAM_B1 ** ADAM_STEP
    c2 = 1.0 - ADAM_B2 ** ADAM_STEP

    def body(*refs):
        p_refs, (w_ref, m_ref, v_ref, g_ref, d_ref, mo_ref, vo_ref) = refs[:n_parts], refs[n_parts:]
        g = p_refs[0][0].astype(F32)
        for p_ref in p_refs:
            for s in range(N_DEV):
                if p_ref is not p_refs[0] or s > 0:
                    g = g + p_ref[s].astype(F32)
        g_ref[0] = g
        m_new = ADAM_B1 * m_ref[0] + (1.0 - ADAM_B1) * g
        v_new = ADAM_B2 * v_ref[0] + (1.0 - ADAM_B2) * (g * g)
        mo_ref[0] = m_new
        vo_ref[0] = v_new
        m_hat = m_new / c1
        v_hat = v_new / c2
        d_ref[0] = -ADAM_LR * (m_hat / (jnp.sqrt(v_hat) + ADAM_EPS) + ADAM_WD * w_ref[0])

    blk = pl.BlockSpec((1, bk, n), lambda i: (0, i, 0))
    out = jax.ShapeDtypeStruct((1, k, n), F32)
    return pl.pallas_call(
        body, name=name, grid=(k // bk,),
        in_specs=[pl.BlockSpec((N_DEV, bk, n), lambda i: (0, i, 0))] * n_parts + [blk, blk, blk],
        out_specs=[blk] * 4, out_shape=[out] * 4,
        compiler_params=_params(("parallel",)),
    )(*parts, w, m, v)


def _adamw_small(parts, w, m, v):
    k = len(SMALL_LAYOUT)
    c1 = 1.0 - ADAM_B1 ** ADAM_STEP
    c2 = 1.0 - ADAM_B2 ** ADAM_STEP

    def body(p_ref, *refs):
        w_refs, m_refs, v_refs, outs = refs[:k], refs[k:2 * k], refs[2 * k:3 * k], refs[3 * k:]
        total = p_ref[0]
        for s in range(1, N_DEV):
            total = total + p_ref[s]
        for i, (_, row, off, width) in enumerate(SMALL_LAYOUT):
            g = total[row:row + 1, off:off + width]
            m_new = ADAM_B1 * m_refs[i][...] + (1.0 - ADAM_B1) * g
            v_new = ADAM_B2 * v_refs[i][...] + (1.0 - ADAM_B2) * (g * g)
            outs[4 * i][...] = g
            outs[4 * i + 1][...] = -ADAM_LR * ((m_new / c1) / (jnp.sqrt(v_new / c2) + ADAM_EPS)
                                               + ADAM_WD * w_refs[i][...])
            outs[4 * i + 2][...] = m_new
            outs[4 * i + 3][...] = v_new
        outs[4 * k][...] = total[SMALL_LOSS_ROW:SMALL_LOSS_ROW + 1, SMALL_LOSS_OFF:SMALL_LOSS_OFF + 1]

    names = [name for name, *_ in SMALL_LAYOUT]
    out_shape = [jax.ShapeDtypeStruct(w[name].shape, F32) for name in names for _ in range(4)]
    outs = pl.pallas_call(
        body, name="adamw_small", out_shape=out_shape + [jax.ShapeDtypeStruct((1, 1), F32)],
    )(parts, *[w[n] for n in names], *[m[n] for n in names], *[v[n] for n in names])
    return {name: tuple(outs[4 * i:4 * i + 4]) for i, name in enumerate(names)}, outs[4 * k]


def _pad_heads_cols(w, heads, width):
    k = w.shape[0]
    w = w.reshape(k, heads, width)
    return jnp.pad(w, ((0, 0), (0, 0), (0, SLAB - width))).reshape(k, heads * SLAB)


def _unpad_heads_cols(w, heads, width):
    k = w.shape[0]
    return w.reshape(k, heads, SLAB)[:, :, :width].reshape(k, heads * width)


def _pad_heads_rows(w, heads, width):
    n = w.shape[1]
    w = w.reshape(heads, width, n)
    return jnp.pad(w, ((0, 0), (0, SLAB - width), (0, 0))).reshape(heads * SLAB, n)


def _unpad_heads_rows(w, heads, width):
    n = w.shape[1]
    return w.reshape(heads, SLAB, n)[:, :width, :].reshape(heads * width, n)


def _pad_w_in(w_in):
    o = 2 * D_MODEL
    qa = _pad_heads_cols(w_in[:, o:o + 512], N_HEADS, HEAD_A)
    ka = _pad_heads_cols(w_in[:, o + 512:o + 640], N_KV_A, HEAD_A)
    va = _pad_heads_cols(w_in[:, o + 640:o + 768], N_KV_A, HEAD_A)
    kr = jnp.pad(w_in[:, o + 1152:o + 1184], ((0, 0), (QK_NOPE, SLAB - QK_NOPE - QK_ROPE)))
    return jnp.concatenate([w_in[:, :o], qa, ka, va, w_in[:, o + 768:o + 1152], kr], axis=1)


def _unpad_w_in(w):
    qa = _unpad_heads_cols(w[:, C_QA:C_KA], N_HEADS, HEAD_A)
    ka = _unpad_heads_cols(w[:, C_KA:C_VA], N_KV_A, HEAD_A)
    va = _unpad_heads_cols(w[:, C_VA:C_CQ], N_KV_A, HEAD_A)
    kr = w[:, C_KR + QK_NOPE:C_KR + QK_NOPE + QK_ROPE]
    return jnp.concatenate([w[:, :C_QA], qa, ka, va, w[:, C_CQ:C_KR], kr], axis=1)


def _pad_w_kvb(w_kvb):
    w = w_kvb.reshape(KV_LORA, N_HEADS, QK_NOPE + V_DIM_B)
    k = jnp.pad(w[:, :, :QK_NOPE], ((0, 0), (0, 0), (0, SLAB - QK_NOPE))).reshape(KV_LORA, HM)
    v = jnp.pad(w[:, :, QK_NOPE:], ((0, 0), (0, 0), (0, SLAB - V_DIM_B))).reshape(KV_LORA, HM)
    return jnp.concatenate([k, v], axis=1)


def _unpad_w_kvb(w):
    k = w[:, :HM].reshape(KV_LORA, N_HEADS, SLAB)[:, :, :QK_NOPE]
    v = w[:, HM:].reshape(KV_LORA, N_HEADS, SLAB)[:, :, :V_DIM_B]
    return jnp.concatenate([k, v], axis=2).reshape(KV_LORA, N_HEADS * (QK_NOPE + V_DIM_B))


def _col_shards(w):
    k, n = w.shape
    return w.reshape(k, N_DEV, n // N_DEV).transpose(1, 0, 2)


def _from_col_shards(s):
    _, k, ns = s.shape
    return s.transpose(1, 0, 2).reshape(k, N_DEV * ns)


def _freq_row():
    freqs = ROPE_THETA ** (-jnp.arange(0, QK_ROPE, 2, dtype=F32) / QK_ROPE)
    return jnp.concatenate([jnp.zeros((QK_NOPE,), F32), freqs, freqs,
                            jnp.zeros((SLAB - QK_NOPE - QK_ROPE,), F32)]).reshape(1, SLAB)


SMALL_D_ROWS = ("pre_norm_mix", "post_norm_mix", "pre_norm_mlp", "post_norm_mlp")
SMALL_LAYOUT = tuple((name, i, 0, D_MODEL) for i, name in enumerate(SMALL_D_ROWS)) + (
    ("q_a_norm", 4, 0, Q_LORA), ("kv_a_norm", 4, 256, KV_LORA), ("sinks", 4, 384, N_HEADS))
SMALL_LOSS_ROW, SMALL_LOSS_OFF = 4, 512


def _pack_small(vals):
    row4 = jnp.concatenate([vals["q_a_norm"].reshape(-1), vals["kv_a_norm"].reshape(-1), vals["sinks"].reshape(-1),
                            jnp.zeros((SMALL_LOSS_OFF - 392,), F32), vals["loss"].reshape(-1),
                            jnp.zeros((1024 - SMALL_LOSS_OFF - 1,), F32)])
    rows = [vals[n].reshape(1024) for n in SMALL_D_ROWS] + [row4]
    return jnp.concatenate([jnp.stack(rows), jnp.zeros((SMALL_ROWS - 5, 1024), F32)], axis=0)


WEIGHT_ORDER = ("pre_norm_mix", "w_in", "q_a_norm", "w_q_b", "kv_a_norm", "w_kv_b", "sinks", "w_o_a", "w_o_b",
                "w_out", "post_norm_mix", "pre_norm_mlp", "w_up", "w_down", "post_norm_mlp")


def kernel(x, positions, pre_norm_mix, w_in, q_a_norm, w_q_b, kv_a_norm, w_kv_b, sinks, w_o_a, w_o_b, w_out, post_norm_mix, pre_norm_mlp, w_up, w_down, post_norm_mlp, loss_target, m_pre_norm_mix, m_w_in, m_q_a_norm, m_w_q_b, m_kv_a_norm, m_w_kv_b, m_sinks, m_w_o_a, m_w_o_b, m_w_out, m_post_norm_mix, m_pre_norm_mlp, m_w_up, m_w_down, m_post_norm_mlp, v_pre_norm_mix, v_w_in, v_q_a_norm, v_w_q_b, v_kv_a_norm, v_w_kv_b, v_sinks, v_w_o_a, v_w_o_b, v_w_out, v_post_norm_mix, v_pre_norm_mlp, v_w_up, v_w_down, v_post_norm_mlp):
    weights = dict(pre_norm_mix=pre_norm_mix, w_in=w_in, q_a_norm=q_a_norm, w_q_b=w_q_b, kv_a_norm=kv_a_norm,
                   w_kv_b=w_kv_b, sinks=sinks, w_o_a=w_o_a, w_o_b=w_o_b, w_out=w_out, post_norm_mix=post_norm_mix,
                   pre_norm_mlp=pre_norm_mlp, w_up=w_up, w_down=w_down, post_norm_mlp=post_norm_mlp)
    m_in = dict(pre_norm_mix=m_pre_norm_mix, w_in=m_w_in, q_a_norm=m_q_a_norm, w_q_b=m_w_q_b, kv_a_norm=m_kv_a_norm,
                w_kv_b=m_w_kv_b, sinks=m_sinks, w_o_a=m_w_o_a, w_o_b=m_w_o_b, w_out=m_w_out,
                post_norm_mix=m_post_norm_mix, pre_norm_mlp=m_pre_norm_mlp, w_up=m_w_up, w_down=m_w_down,
                post_norm_mlp=m_post_norm_mlp)
    v_in = dict(pre_norm_mix=v_pre_norm_mix, w_in=v_w_in, q_a_norm=v_q_a_norm, w_q_b=v_w_q_b, kv_a_norm=v_kv_a_norm,
                w_kv_b=v_w_kv_b, sinks=v_sinks, w_o_a=v_w_o_a, w_o_b=v_w_o_b, w_out=v_w_out,
                post_norm_mix=v_post_norm_mix, pre_norm_mlp=v_pre_norm_mlp, w_up=v_w_up, w_down=v_w_down,
                post_norm_mlp=v_post_norm_mlp)

    xs, pos, target = x[0], positions[0], loss_target[0]
    t = xs.shape[0]
    pos_col = pos.reshape(t, 1)
    pos_row = pos.reshape(1, t)
    g1, g2, g3, g4 = (weights[n] for n in SMALL_D_ROWS)
    g_q, g_kv = q_a_norm, kv_a_norm
    sink_vec = sinks.reshape(N_HEADS)
    shard = {n: weights[n][0].astype(BF16) for n in EARLY + LATE}

    tables, (e_in, e_qb, e_kvb) = _rope_tables(pos_col, _freq_row(), [shard[n] for n in EARLY])
    w_in_p = _pad_w_in(_from_col_shards(e_in))
    w_qb = _pad_heads_cols(_from_col_shards(e_qb), N_HEADS, QK_NOPE + QK_ROPE)
    w_kvb = _pad_w_kvb(_from_col_shards(e_kvb))

    (h, gates, qa, ka, va, cq, ckv, cqn, ckvn, kb, vb, qt, kt, vt) = _inproj_fwd(
        xs, g1, w_in_p, g_q, g_kv, w_kvb, w_qb.T, w_kvb[:, :HM].T, w_kvb[:, HM:].T, w_in_p[:, C_KR:].T, tables)
    out_a, lse_a = _swa_fwd(qa, ka, va, pos_col, pos_row, sink_vec)
    out_b, out_b_t, qt_lse, (l_oa, l_ob, l_out, w_up_s, l_down) = _mla_fwd(qt, kb, vt, [shard[n] for n in LATE])
    w_oa = _pad_heads_rows(_from_col_shards(l_oa), N_HEADS, HEAD_A)
    w_ob = _pad_heads_rows(_from_col_shards(l_ob), N_HEADS, V_DIM_B)
    w_out_f = l_out.reshape(D_MODEL, D_MODEL)
    w_down_f = l_down.reshape(D_FF, D_MODEL)

    oa_p, ob_p, merged, y, x1, h2 = _merge_fwd(out_a, out_b, gates, xs, w_oa, w_ob, w_out_f, g2, g3)
    a, du, dy2, dx1, loss, dg3, dg4 = _mlp_fwd_bwd(x1, h2, target, w_up_s, w_down_f, g3, g4)
    (dgates, d_oa, d_ob_t, dg2, dw_oa, dw_ob, dw_out) = _merge_bwd(
        dx1, y, gates, oa_p, ob_p, out_a, out_b, out_b_t, merged, w_oa, w_ob, w_out_f, g2)
    late_slices = [
        _col_shards(_unpad_heads_rows(dw_oa, N_HEADS, HEAD_A)).astype(BF16),
        _col_shards(_unpad_heads_rows(dw_ob, N_HEADS, V_DIM_B)).astype(BF16),
        dw_out.astype(BF16).reshape(N_DEV, D_MODEL // N_DEV, D_MODEL),
        _matmul_tn(h2, du, "dw_up", BF16, N_DEV),
        _matmul_tn(a, dy2, "dw_down", BF16).reshape(N_DEV, D_FF // N_DEV, D_MODEL),
    ]
    dqkv_a, dsink = _swa_bwd(qa, ka, va, out_a, d_oa, lse_a, pos_col, pos_row, sink_vec)
    dw_in_early = jnp.concatenate([_matmul_tn(h, dgates, "dw_in_gates"), _matmul_tn(h, dqkv_a, "dw_in_mixer_a"),
                                   jnp.zeros((D_MODEL, D_IN_PAD - C_CQ), F32)], axis=1)
    late_slices.append(_col_shards(_unpad_w_in(dw_in_early)).astype(BF16))
    dqb_t, dkb_t, dvb_t, late_parts = _mla_bwd(qt_lse, kb, kt, vb, d_ob_t, late_slices)
    w_in_early_parts = late_parts.pop()
    dproj_late, dx, dg1, dgq, dgkv, dw_qb_t, dw_kvb_t = _inproj_bwd(
        dgates, dqkv_a, dqb_t, dkb_t, dvb_t, cq, ckv, cqn, ckvn, xs, dx1, *tables[3:], g1, g_q, g_kv,
        w_in_p, w_qb, w_kvb)
    dw_l = _matmul_tn(h, dproj_late, "dw_in_latents")
    late_cols = jnp.concatenate([dw_l[:, :Q_LORA + KV_LORA], dw_l[:, C_KR - C_CQ + QK_NOPE:C_KR - C_CQ + Q_HEAD_B]],
                                axis=1)
    shard_cols = w_in.shape[2]
    head = late_cols.shape[1] - shard_cols
    w_in_late = jnp.concatenate([
        jnp.zeros((N_DEV - 2, D_MODEL, shard_cols), F32),
        jnp.pad(late_cols[:, :head], ((0, 0), (shard_cols - head, 0)))[None], late_cols[:, head:][None]])
    early_slices = [
        w_in_late.astype(BF16),
        _col_shards(_unpad_heads_cols(dw_qb_t.T, N_HEADS, QK_NOPE + QK_ROPE)).astype(BF16),
        _col_shards(_unpad_w_kvb(dw_kvb_t.T)).astype(BF16),
    ]
    small_grads = {"pre_norm_mix": dg1, "post_norm_mix": dg2, "pre_norm_mlp": dg3, "post_norm_mlp": dg4,
                   "q_a_norm": dgq, "kv_a_norm": dgkv, "sinks": dsink.reshape(N_HEADS, BLOCK).sum(axis=1),
                   "loss": loss[0, 0:1]}
    early_parts, s_parts = _exchange_grads(early_slices, _pack_small(small_grads),
                                            only=[(N_DEV - 2, N_DEV - 1), None, None])

    updates = {}
    all_parts = [[w_in_early_parts, early_parts[0]]] + [[p] for p in early_parts[1:] + late_parts]
    for name, parts in zip(EARLY + LATE, all_parts):
        outs = _adamw(parts, weights[name], m_in[name], v_in[name], "adamw_" + name)
        for kind, arr in zip(("g", "d", "m", "v"), outs):
            updates[kind, name] = arr
    small_out, loss_sum = _adamw_small(s_parts, weights, m_in, v_in)
    for name, outs in small_out.items():
        for kind, arr in zip(("g", "d", "m", "v"), outs):
            updates[kind, name] = arr
    results = [updates[kind, name] for kind in ("g", "d", "m", "v") for name in WEIGHT_ORDER]
    return (loss_sum.reshape(()), dx[None], *results)
```

```python
import functools

import numpy as np
import jax
import jax.numpy as jnp
from jax import lax
from jax.experimental import pallas as pl
from jax.experimental.pallas import tpu as pltpu

F32 = jnp.float32
BF16 = jnp.bfloat16

D_MODEL = 1024
D_FF = 4096
N_HEADS = 8
N_KV_A = 2
GROUP_A = N_HEADS // N_KV_A
HEAD_A = 64
QK_NOPE = 64
QK_ROPE = 32
V_DIM_B = 64
Q_LORA = 256
KV_LORA = 128
BLOCK = 128
SLAB = 128
ROPE_THETA = 10000.0
EPS = 1e-6
N_DEV = 8
NEG = -1e30

SCALE_A = HEAD_A ** -0.5
SCALE_B = (QK_NOPE + QK_ROPE) ** -0.5
LOG2E = 1.4426950408889634
SCORE_B = SCALE_B * LOG2E
MLA_HEADS_PER_STEP = 4
MLA_FWD_HEADS_PER_STEP = 8
Q_HEAD_B = QK_NOPE + QK_ROPE
ONES_ROWS = 16
SLOPES_A = tuple(2.0 ** (-8.0 * (h + 1) / N_HEADS) for h in range(N_HEADS))

ADAM_LR = 0.001
ADAM_B1 = 0.9
ADAM_B2 = 0.999
ADAM_EPS = 1e-08
ADAM_WD = 0.01
ADAM_STEP = 10

HM = N_HEADS * SLAB
C_GATES = 0
C_QA = 2 * D_MODEL
C_KA = C_QA + HM
C_VA = C_KA + N_KV_A * SLAB
C_CQ = C_VA + N_KV_A * SLAB
C_CKV = C_CQ + Q_LORA
C_KR = C_CKV + KV_LORA
D_IN_PAD = C_KR + SLAB

VMEM_LIMIT = 56 * 1024 * 1024

EARLY = ("w_in", "w_q_b", "w_kv_b")
LATE = ("w_o_a", "w_o_b", "w_out", "w_up", "w_down")
ADAM_ROWS = 256
SMALL_ROWS = 8


def _token_tile(t):
    return min(256, t)


def _attn_tile(t):
    return 512 if t >= 2048 else 128


def _params(sem, vmem=VMEM_LIMIT):
    return pltpu.CompilerParams(dimension_semantics=sem, vmem_limit_bytes=vmem)


def _dot(a, b):
    return jnp.dot(a, b, preferred_element_type=F32)


def _dot_nt(a, b):
    return lax.dot_general(a, b, (((1,), (1,)), ((), ())), preferred_element_type=F32)


def _dot_tn(a, b):
    return lax.dot_general(a, b, (((0,), (0,)), ((), ())), preferred_element_type=F32)


def _rms_r(x):
    return lax.rsqrt(jnp.mean(x * x, axis=-1, keepdims=True) + EPS)


def _rms_bwd(x, r, g, dy):
    t = dy * g
    return r * t - x * (r * r * r) * jnp.mean(x * t, axis=-1, keepdims=True)


def _sigmoid(x):
    return 1.0 / (1.0 + jnp.exp(-x))


def _rope(x, c, s1, s2):
    return x * c + pltpu.roll(x, SLAB - 16, 1) * s1 + pltpu.roll(x, 16, 1) * s2


def _rope_bwd(d, c, s1, s2):
    return d * c + pltpu.roll(d * s1, 16, 1) + pltpu.roll(d * s2, SLAB - 16, 1)


def _roll_rows(x, shift):
    return jnp.concatenate([x[-shift:], x[:-shift]], axis=0)


def _rope_t(x, c, s1, s2):
    return x * c + _roll_rows(x, SLAB - 16) * s1 + _roll_rows(x, 16) * s2


def _rope_t_bwd(d, c, s1, s2):
    return d * c + _roll_rows(d * s1, 16) + _roll_rows(d * s2, SLAB - 16)


def _plant_rows(slab, row, vals):
    hi = vals.astype(BF16).astype(F32)
    lo = (vals - hi).astype(BF16).astype(F32)
    idx = lax.broadcasted_iota(jnp.int32, slab.shape, 0)
    return jnp.where(idx == row, -hi, jnp.where(idx == row + 1, -lo, slab))


def _row_spec(tm, n):
    return pl.BlockSpec((tm, n), lambda i: (i, 0))


def _col_spec(n, tm):
    return pl.BlockSpec((n, tm), lambda i: (0, i))


def _full_spec(shape):
    nd = len(shape)
    return pl.BlockSpec(shape, lambda i: (0,) * nd, pipeline_mode=pl.Buffered(1))


def _acc_rows(ref, val):
    @pl.when(pl.program_id(0) == 0)
    def _():
        ref[...] = jnp.zeros_like(ref)
    ref[...] += jnp.sum(val, axis=0, keepdims=True)


def _rope_tables(pos_col, freq_row, early):
    t = pos_col.shape[0]
    tm = _token_tile(t)
    n = len(early)

    def body(pos_ref, f_ref, *rest):
        shard_refs, (c_ref, s1_ref, s2_ref, ct_ref, s1t_ref, s2t_ref) = rest[:n], rest[n:n + 6]
        start, finish = _two_level_gather(shard_refs, rest[n + 6:2 * n + 6], *rest[2 * n + 6:])
        pl.when(pl.program_id(0) == 0)(start)
        ang = pos_ref[...].astype(F32) * f_ref[...]
        lane = lax.broadcasted_iota(jnp.int32, ang.shape, 1)
        s = jnp.sin(ang)
        c = jnp.cos(ang)
        s1 = jnp.where((lane >= 64) & (lane < 80), -s, 0.0)
        s2 = jnp.where((lane >= 80) & (lane < 96), s, 0.0)
        c_ref[...], s1_ref[...], s2_ref[...] = c, s1, s2
        ct_ref[...], s1t_ref[...], s2t_ref[...] = c.T, s1.T, s2.T
        pl.when(pl.program_id(0) == t // tm - 1)(finish)

    tab = jax.ShapeDtypeStruct((t, SLAB), F32)
    tabt = jax.ShapeDtypeStruct((SLAB, t), F32)
    outs = pl.pallas_call(
        body, name="rope_tables", grid=(t // tm,),
        in_specs=[_row_spec(tm, 1), _full_spec((1, SLAB))] + [ANY_SPEC] * n,
        out_specs=[_row_spec(tm, SLAB)] * 3 + [_col_spec(SLAB, tm)] * 3 + [ANY_SPEC] * n,
        out_shape=[tab] * 3 + [tabt] * 3 + [jax.ShapeDtypeStruct((N_DEV,) + a.shape, a.dtype) for a in early],
        scratch_shapes=_exchange_scratch(n),
        compiler_params=_params(("arbitrary",)),
    )(pos_col, freq_row, *early)
    return outs[:6], outs[6:]


def _inproj_fwd(x, g1, w_in, g_q, g_kv, w_kvb, w_qb_t, w_kb_t, w_vb_t, w_kr_t, tables):
    t = x.shape[0]
    tm = _token_tile(t)

    def body(x_ref, g1_ref, win_ref, gq_ref, gkv_ref, wkvb_ref, wqbt_ref, wkbt_ref, wvbt_ref, wkrt_ref,
             c_ref, s1_ref, s2_ref, ct_ref, s1t_ref, s2t_ref,
             h_ref, gates_ref, qa_ref, ka_ref, va_ref, cq_ref, ckv_ref, cqn_ref, ckvn_ref,
             kb_ref, vb_ref, qt_ref, kt_ref, vt_ref):
        xv = x_ref[...]
        h = (xv * _rms_r(xv) * g1_ref[...]).astype(BF16)
        h_ref[...] = h
        proj = _dot(h, win_ref[...])
        gates_ref[...] = proj[:, C_GATES:C_QA].astype(BF16)
        qa_ref[...] = proj[:, C_QA:C_KA].astype(BF16)
        ka_ref[...] = proj[:, C_KA:C_VA].astype(BF16)
        va_ref[...] = proj[:, C_VA:C_CQ].astype(BF16)
        cq = proj[:, C_CQ:C_CKV]
        ckv = proj[:, C_CKV:C_KR]
        kr = proj[:, C_KR:D_IN_PAD]
        cq_ref[...] = cq
        ckv_ref[...] = ckv
        cqn = (cq * _rms_r(cq) * gq_ref[...]).astype(BF16)
        ckvn = (ckv * _rms_r(ckv) * gkv_ref[...]).astype(BF16)
        cqn_ref[...] = cqn
        ckvn_ref[...] = ckvn
        c, s1, s2 = c_ref[...], s1_ref[...], s2_ref[...]
        kvb = _dot(ckvn, wkvb_ref[...])
        kr_rot = _rope(kr, c, s1, s2)
        ct, s1t, s2t = ct_ref[...], s1t_ref[...], s2t_ref[...]
        q_t = _dot_nt(wqbt_ref[...], cqn)
        k_t = _dot_nt(wkbt_ref[...], ckvn)
        kr_t = _rope_t(_dot_nt(wkrt_ref[...], h), ct, s1t, s2t)
        k_lane = lax.broadcasted_iota(jnp.int32, (1, SLAB), 1)
        k_ones = jnp.where((k_lane == Q_HEAD_B) | (k_lane == Q_HEAD_B + 1), 1.0, 0.0)
        for hd in range(N_HEADS):
            sl = slice(hd * SLAB, (hd + 1) * SLAB)
            kb_ref[:, sl] = (kvb[:, sl] + kr_rot + k_ones).astype(BF16)
            qt_ref[sl, :] = (_rope_t(q_t[sl, :], ct, s1t, s2t) * SCORE_B).astype(BF16)
            kt_ref[sl, :] = (k_t[sl, :] + kr_t).astype(BF16)
        v_lane = lax.broadcasted_iota(jnp.int32, (1, HM), 1) & (SLAB - 1)
        v_ones = jnp.where((v_lane == V_DIM_B) | (v_lane == V_DIM_B + 1), 1.0, 0.0)
        vb_ref[...] = (kvb[:, HM:2 * HM] + v_ones).astype(BF16)
        pad_row = lax.broadcasted_iota(jnp.int32, (HM, 1), 0) & (SLAB - 1)
        ones_rows = jnp.where((pad_row >= V_DIM_B) & (pad_row < V_DIM_B + ONES_ROWS), 1.0, 0.0)
        vt_ref[...] = (_dot_nt(wvbt_ref[...], ckvn) + ones_rows).astype(BF16)

    def sds(n, dt):
        return jax.ShapeDtypeStruct((t, n), dt)

    outs = [(D_MODEL, BF16), (2 * D_MODEL, BF16), (HM, BF16), (N_KV_A * SLAB, BF16), (N_KV_A * SLAB, BF16),
            (Q_LORA, F32), (KV_LORA, F32), (Q_LORA, BF16), (KV_LORA, BF16), (HM, BF16), (HM, BF16)]
    tab, tabt = _row_spec(tm, SLAB), _col_spec(SLAB, tm)
    return pl.pallas_call(
        body, name="inproj_fwd", grid=(t // tm,),
        in_specs=[_row_spec(tm, D_MODEL), _full_spec((1, D_MODEL)), _full_spec((D_MODEL, D_IN_PAD)),
                  _full_spec((1, Q_LORA)), _full_spec((1, KV_LORA)), _full_spec((KV_LORA, 2 * HM)),
                  _full_spec((HM, Q_LORA)), _full_spec((HM, KV_LORA)), _full_spec((HM, KV_LORA)),
                  _full_spec((SLAB, D_MODEL)), tab, tab, tab, tabt, tabt, tabt],
        out_specs=[_row_spec(tm, n) for n, _ in outs] + [_col_spec(HM, tm)] * 3,
        out_shape=[sds(n, dt) for n, dt in outs] + [jax.ShapeDtypeStruct((HM, t), BF16)] * 3,
        compiler_params=_params(("parallel",)),
    )(x, g1, w_in, g_q, g_kv, w_kvb, w_qb_t, w_kb_t, w_vb_t, w_kr_t, *tables)


def _tile_group(a):
    return jnp.concatenate([a] * GROUP_A, axis=1)


def _swa_masks():
    row = lax.broadcasted_iota(jnp.int32, (BLOCK, GROUP_A * BLOCK), 0)
    col = lax.broadcasted_iota(jnp.int32, (BLOCK, GROUP_A * BLOCK), 1) & (BLOCK - 1)
    return row <= col, row > col


def _heads_beside(ref, g):
    return jnp.concatenate([ref[:, (g * GROUP_A + hh) * SLAB:(g * GROUP_A + hh + 1) * SLAB].T
                            for hh in range(GROUP_A)], axis=1)


def _rows_beside(ref, g):
    return jnp.concatenate([ref[g * GROUP_A + hh] for hh in range(GROUP_A)], axis=1)


def _swa_rows(sinks):
    slopes = jnp.repeat(jnp.asarray(SLOPES_A, F32).reshape(N_KV_A, GROUP_A, 1), BLOCK, axis=2)
    sink_rows = jnp.repeat(sinks.reshape(N_KV_A, GROUP_A, 1), BLOCK, axis=2)
    return slopes.reshape(N_KV_A, 1, GROUP_A * BLOCK), sink_rows.reshape(N_KV_A, 1, GROUP_A * BLOCK)


def _swa_fwd(qa, ka, va, pos_col, pos_row, sinks):
    t = qa.shape[0]
    nb = t // BLOCK
    gw = GROUP_A * BLOCK
    slope_rows, sink_rows = _swa_rows(sinks)

    def body(q_ref, kc_ref, kp_ref, vc_ref, vp_ref, pkc_ref, pkp_ref, pq_ref, slope_ref, sink_ref, o_ref, l_ref):
        i = pl.program_id(0)
        pq = pq_ref[...]
        dist_c = _tile_group(jnp.abs(pkc_ref[...] - pq).astype(F32))
        dist_p = _tile_group(jnp.abs(pkp_ref[...] - pq).astype(F32))
        mask_c, older = _swa_masks()
        mask_p = jnp.logical_and(older, i > 0)
        raw = []
        for g in range(N_KV_A):
            gs = slice(g * SLAB, (g + 1) * SLAB)
            x = _heads_beside(q_ref, g)
            raw.append((_dot(kc_ref[:, gs], x), _dot(kp_ref[:, gs], x)))
        for g in range(N_KV_A):
            gs = slice(g * SLAB, (g + 1) * SLAB)
            slope, sink = slope_ref[g], sink_ref[g]
            s_c = jnp.where(mask_c, raw[g][0] * SCALE_A - slope * dist_c, NEG)
            s_p = jnp.where(mask_p, raw[g][1] * SCALE_A - slope * dist_p, NEG)
            m = jnp.maximum(jnp.maximum(jnp.max(s_c, axis=0, keepdims=True),
                                        jnp.max(s_p, axis=0, keepdims=True)), sink)
            e_c = jnp.exp(s_c - m)
            e_p = jnp.exp(s_p - m)
            den = jnp.sum(e_c, axis=0, keepdims=True) + jnp.sum(e_p, axis=0, keepdims=True) + jnp.exp(sink - m)
            inv = 1.0 / den
            ot = (_dot_tn(vc_ref[:, gs], (e_c * inv).astype(BF16))
                  + _dot_tn(vp_ref[:, gs], (e_p * inv).astype(BF16)))
            lse = m + jnp.log(den)
            for hh in range(GROUP_A):
                hd = g * GROUP_A + hh
                seg = slice(hh * BLOCK, (hh + 1) * BLOCK)
                o_ref[:, hd * SLAB:(hd + 1) * SLAB] = ot[:, seg].T.astype(BF16)
                l_ref[hd] = lse[:, seg]

    cur = lambda i: (i, 0)
    prev = lambda i: (jnp.maximum(i - 1, 0), 0)
    kvw = N_KV_A * SLAB
    rows = pl.BlockSpec((N_KV_A, 1, gw), lambda i: (0, 0, 0))
    return pl.pallas_call(
        body, name="swa_fwd", grid=(nb,),
        in_specs=[pl.BlockSpec((BLOCK, HM), cur),
                  pl.BlockSpec((BLOCK, kvw), cur), pl.BlockSpec((BLOCK, kvw), prev),
                  pl.BlockSpec((BLOCK, kvw), cur), pl.BlockSpec((BLOCK, kvw), prev),
                  pl.BlockSpec((BLOCK, 1), cur), pl.BlockSpec((BLOCK, 1), prev),
                  pl.BlockSpec((1, BLOCK), lambda i: (0, i)), rows, rows],
        out_specs=[pl.BlockSpec((BLOCK, HM), cur), pl.BlockSpec((N_HEADS, 1, BLOCK), lambda i: (0, 0, i))],
        out_shape=[jax.ShapeDtypeStruct((t, HM), BF16), jax.ShapeDtypeStruct((N_HEADS, 1, t), F32)],
        compiler_params=_params(("parallel",)),
    )(qa, ka, ka, va, va, pos_col, pos_col, pos_row, slope_rows, sink_rows)


def _swa_bwd(qa, ka, va, out_a, d_oa, lse, pos_col, pos_row, sinks):
    t = qa.shape[0]
    nb = t // BLOCK
    gw = GROUP_A * BLOCK
    kvw = N_KV_A * SLAB
    slope_rows, sink_rows = _swa_rows(sinks)

    def body(q_ref, qn_ref, do_ref, don_ref, l_ref, ln_ref, o_ref, on_ref, kp_ref, kc_ref, vp_ref, vc_ref,
             pkp_ref, pkc_ref, pq_ref, pqn_ref, slope_ref, sink_ref, dqkv_ref, dsink_ref):
        j = pl.program_id(0)
        pkc, pkp = pkc_ref[...], pkp_ref[...]
        dist_cc = _tile_group(jnp.abs(pkc - pq_ref[...]).astype(F32))
        dist_cp = _tile_group(jnp.abs(pkp - pq_ref[...]).astype(F32))
        dist_nc = _tile_group(jnp.abs(pkc - pqn_ref[...]).astype(F32))
        mask_cc, older = _swa_masks()
        mask_cp = jnp.logical_and(older, j > 0)
        mask_nc = jnp.logical_and(older, j < nb - 1)

        @pl.when(j == 0)
        def _():
            dsink_ref[...] = jnp.zeros_like(dsink_ref)

        def tile(k, v, x, dox, lrow, drow, dist, mask, slope):
            s = jnp.where(mask, _dot(k, x) * SCALE_A - slope * dist, NEG)
            p = jnp.exp(s - lrow)
            ds = p * (_dot(v, dox) - drow)
            return p.astype(BF16), ds.astype(BF16)

        for g in range(N_KV_A):
            gs = slice(g * SLAB, (g + 1) * SLAB)
            kc, kp, vc, vp = kc_ref[:, gs], kp_ref[:, gs], vc_ref[:, gs], vp_ref[:, gs]
            slope, sink = slope_ref[g], sink_ref[g]
            x, xn = _heads_beside(q_ref, g), _heads_beside(qn_ref, g)
            dox, doxn = _heads_beside(do_ref, g), _heads_beside(don_ref, g)
            lrow, lrown = _rows_beside(l_ref, g), _rows_beside(ln_ref, g)
            drow = jnp.sum(dox.astype(F32) * _heads_beside(o_ref, g).astype(F32), axis=0, keepdims=True)
            drown = jnp.sum(doxn.astype(F32) * _heads_beside(on_ref, g).astype(F32), axis=0, keepdims=True)
            p_cc, ds_cc = tile(kc, vc, x, dox, lrow, drow, dist_cc, mask_cc, slope)
            _, ds_cp = tile(kp, vp, x, dox, lrow, drow, dist_cp, mask_cp, slope)
            p_nc, ds_nc = tile(kc, vc, xn, doxn, lrown, drown, dist_nc, mask_nc, slope)
            dqt = (_dot_tn(kc, ds_cc) + _dot_tn(kp, ds_cp)) * SCALE_A
            for hh in range(GROUP_A):
                hd = g * GROUP_A + hh
                dqkv_ref[:, hd * SLAB:(hd + 1) * SLAB] = dqt[:, hh * BLOCK:(hh + 1) * BLOCK].T.astype(BF16)
            dqkv_ref[:, HM + g * SLAB:HM + (g + 1) * SLAB] = (
                (_dot_nt(ds_cc, x) + _dot_nt(ds_nc, xn)) * SCALE_A).astype(BF16)
            dqkv_ref[:, HM + kvw + g * SLAB:HM + kvw + (g + 1) * SLAB] = (
                _dot_nt(p_cc, dox) + _dot_nt(p_nc, doxn)).astype(BF16)
            dsink_ref[g] -= jnp.exp(sink - lrow) * drow

    cur = lambda j: (j, 0)
    prev = lambda j: (jnp.maximum(j - 1, 0), 0)
    nxt = lambda j: (jnp.minimum(j + 1, nb - 1), 0)
    cur3 = lambda j: (0, 0, j)
    nxt3 = lambda j: (0, 0, jnp.minimum(j + 1, nb - 1))
    kvw = N_KV_A * SLAB
    rows = pl.BlockSpec((N_KV_A, 1, gw), lambda j: (0, 0, 0))
    stat = lambda im: pl.BlockSpec((N_HEADS, 1, BLOCK), im)
    return pl.pallas_call(
        body, name="swa_bwd", grid=(nb,),
        in_specs=[pl.BlockSpec((BLOCK, HM), cur), pl.BlockSpec((BLOCK, HM), nxt),
                  pl.BlockSpec((BLOCK, HM), cur), pl.BlockSpec((BLOCK, HM), nxt),
                  stat(cur3), stat(nxt3), pl.BlockSpec((BLOCK, HM), cur), pl.BlockSpec((BLOCK, HM), nxt),
                  pl.BlockSpec((BLOCK, kvw), prev), pl.BlockSpec((BLOCK, kvw), cur),
                  pl.BlockSpec((BLOCK, kvw), prev), pl.BlockSpec((BLOCK, kvw), cur),
                  pl.BlockSpec((BLOCK, 1), prev), pl.BlockSpec((BLOCK, 1), cur),
                  pl.BlockSpec((1, BLOCK), lambda j: (0, j)),
                  pl.BlockSpec((1, BLOCK), lambda j: (0, jnp.minimum(j + 1, nb - 1))), rows, rows],
        out_specs=[pl.BlockSpec((BLOCK, HM + 2 * kvw), cur), rows],
        out_shape=[jax.ShapeDtypeStruct((t, HM + 2 * kvw), BF16), jax.ShapeDtypeStruct((N_KV_A, 1, gw), F32)],
        compiler_params=_params(("arbitrary",)),
    )(qa, qa, d_oa, d_oa, lse, lse, out_a, out_a, ka, ka, va, va,
      pos_col, pos_col, pos_row, pos_row, slope_rows, sink_rows)


def _mesh_pos():
    return lax.axis_index("x"), lax.axis_index("y"), lax.axis_index("c")


def _flip(v, bit):
    return 1 - v if bit else v


def _direct_copies(srcs, dsts, send_sems, recv_sems, local_sems, gather, sem_base=0, only=None):
    x, y, c = _mesh_pos()
    me = 4 * x + 2 * y + c

    def among(idx, dests):
        ok = idx == dests[0]
        for d in dests[1:]:
            ok = jnp.logical_or(ok, idx == d)
        return ok

    local, remote = [], []
    for a, (src, dst) in enumerate(zip(srcs, dsts)):
        dests = None if only is None else only[a]
        recv_ok = None if dests is None else among(me, dests)
        local.append((pltpu.make_async_copy(src if gather else src.at[me], dst.at[me],
                                            local_sems.at[sem_base + a]), recv_ok))
        for r in range(1, N_DEV):
            px, py, pc = _flip(x, r & 4), _flip(y, r & 2), _flip(c, r & 1)
            peer = 4 * px + 2 * py + pc
            sem = (N_DEV - 1) * (sem_base + a) + r - 1
            copy = pltpu.make_async_remote_copy(
                src_ref=src if gather else src.at[peer], dst_ref=dst.at[me],
                send_sem=send_sems.at[sem], recv_sem=recv_sems.at[sem],
                device_id=(px, py, pc), device_id_type=pl.DeviceIdType.MESH)
            remote.append((copy, None if dests is None else among(peer, dests), recv_ok))
    return local, remote


def _when(cond, fn):
    if cond is None:
        fn()
    else:
        pl.when(cond)(fn)


def _start_copies(local, remote):
    for cp, ok in local:
        _when(ok, cp.start)
    for cp, send_ok, _ in remote:
        _when(send_ok, cp.start)


def _wait_copies(local, remote):
    for cp, _, recv_ok in remote:
        _when(recv_ok, cp.wait_recv)
    for cp, send_ok, _ in remote:
        _when(send_ok, cp.wait_send)
    for cp, ok in local:
        _when(ok, cp.wait)


def _exchange_scratch(n):
    return [pltpu.SemaphoreType.DMA((n * (N_DEV - 1),)), pltpu.SemaphoreType.DMA((n * (N_DEV - 1),)),
            pltpu.SemaphoreType.DMA((n,))]


ANY_SPEC = pl.BlockSpec(memory_space=pl.ANY)


def _mla_fwd(qt, kb, vt, late):
    t = kb.shape[0]
    tk = _attn_tile(t)
    ratio = 2 if t >= 2 * tk else 1
    tq = ratio * tk
    nq = t // tq
    hps = MLA_FWD_HEADS_PER_STEP
    w = hps * SLAB
    pairs = [(i, j) for i in range(nq) for j in range(ratio * (i + 1))]
    i_tab = jnp.asarray(np.array([p[0] for p in pairs], np.int32))
    j_tab = jnp.asarray(np.array([p[1] for p in pairs], np.int32))

    n_late = len(late)

    def body(it_ref, jt_ref, qt_ref, k_ref, vt_ref, *rest):
        late_refs, (o_ref, ot_ref, qa_ref) = rest[:n_late], rest[n_late:n_late + 3]
        gathered_refs = rest[n_late + 3:2 * n_late + 3]
        m_s, acc_s, send_sems, recv_sems, local_sems = rest[2 * n_late + 3:]
        n = pl.program_id(1)
        i, j = it_ref[n], jt_ref[n]
        first_step = jnp.logical_and(pl.program_id(0) == 0, n == 0)
        last_step = jnp.logical_and(pl.program_id(0) == N_HEADS // hps - 1, n == len(pairs) - 1)

        @pl.when(first_step)
        def _():
            _start_copies(*_direct_copies(late_refs, gathered_refs, send_sems, recv_sems, local_sems, True))

        @pl.when(j == 0)
        def _():
            m_s[...] = jnp.full_like(m_s, NEG)
            acc_s[...] = jnp.zeros_like(acc_s)

        def update(masked, q0):
            qc = slice(q0, tq)

            def scores(hh):
                sl = slice(hh * SLAB, (hh + 1) * SLAB)
                return _dot(k_ref[:, sl], qt_ref[sl, qc])

            def softmax(hh, s):
                if masked:
                    s = jnp.where(lax.broadcasted_iota(jnp.int32, s.shape, 0)
                                  <= lax.broadcasted_iota(jnp.int32, s.shape, 1), s, NEG)
                m_old = m_s[hh][:, qc]
                m_new = jnp.maximum(m_old, jnp.max(s, axis=0, keepdims=True))
                m_s[hh, :, qc] = m_new
                return jnp.exp2(s - m_new).astype(BF16), jnp.exp2(m_old - m_new)

            def accumulate(hh, p, alpha):
                sl = slice(hh * SLAB, hh * SLAB + V_DIM_B + ONES_ROWS)
                acc_s[sl, qc] = alpha * acc_s[sl, qc] + _dot(vt_ref[sl, :], p)

            s_next, pending = scores(0), None
            for hh in range(hps):
                s = s_next
                if hh + 1 < hps:
                    s_next = scores(hh + 1)
                p, alpha = softmax(hh, s)
                if pending is not None:
                    accumulate(*pending)
                pending = (hh, p, alpha)
            accumulate(*pending)

        @pl.when(j < ratio * i)
        def _():
            update(False, 0)

        for part in range(ratio):
            @pl.when(j == ratio * i + part)
            def _():
                update(True, part * tk)

        @pl.when(j == ratio * i + ratio - 1)
        def _():
            for hh in range(hps):
                sl = slice(hh * SLAB, (hh + 1) * SLAB)
                den = acc_s[hh * SLAB + V_DIM_B:hh * SLAB + V_DIM_B + 1, :]
                values = lax.broadcasted_iota(jnp.int32, (SLAB, tq), 0) < V_DIM_B
                ot = jnp.where(values, acc_s[sl, :] / den, 0.0)
                ot_ref[sl, :] = ot.astype(BF16)
                o_ref[:, sl] = ot.T.astype(BF16)
                lse = m_s[hh] + jnp.log2(den)
                qa_ref[sl, :] = _plant_rows(qt_ref[sl, :].astype(F32), Q_HEAD_B, lse).astype(BF16)

        @pl.when(last_step)
        def _():
            _wait_copies(*_direct_copies(late_refs, gathered_refs, send_sems, recv_sems, local_sems, True))

    grid_spec = pltpu.PrefetchScalarGridSpec(
        num_scalar_prefetch=2, grid=(N_HEADS // hps, len(pairs)),
        in_specs=[pl.BlockSpec((w, tq), lambda h, n, it, jt: (h, it[n])),
                  pl.BlockSpec((tk, w), lambda h, n, it, jt: (jt[n], h)),
                  pl.BlockSpec((w, tk), lambda h, n, it, jt: (h, jt[n]))] + [ANY_SPEC] * n_late,
        out_specs=[pl.BlockSpec((tq, w), lambda h, n, it, jt: (it[n], h)),
                   pl.BlockSpec((w, tq), lambda h, n, it, jt: (h, it[n])),
                   pl.BlockSpec((w, tq), lambda h, n, it, jt: (h, it[n]))] + [ANY_SPEC] * n_late,
        scratch_shapes=[pltpu.VMEM((hps, 1, tq), F32), pltpu.VMEM((w, tq), F32)] + _exchange_scratch(n_late))
    outs = pl.pallas_call(
        body, name="mla_fwd", grid_spec=grid_spec,
        out_shape=[jax.ShapeDtypeStruct((t, HM), BF16), jax.ShapeDtypeStruct((HM, t), BF16),
                   jax.ShapeDtypeStruct((HM, t), BF16)]
        + [jax.ShapeDtypeStruct((N_DEV,) + a.shape, a.dtype) for a in late],
        compiler_params=_params(("arbitrary", "arbitrary")),
    )(i_tab, j_tab, qt, kb, vt, *late)
    return outs[0], outs[1], outs[2], list(outs[3:])


def _mla_bwd(qt, kb, kt, vb, d_ob_t, grad_slices):
    t = kb.shape[0]
    tk = _attn_tile(t)
    ratio = 2 if t >= 2 * tk else 1
    tq = ratio * tk
    nk, nq = t // tk, t // tq
    hps = MLA_HEADS_PER_STEP
    w = hps * SLAB
    pairs = [(j, i) for j in range(nk) for i in range(j // ratio, nq)]
    j_tab = jnp.asarray(np.array([p[0] for p in pairs], np.int32))
    i_tab = jnp.asarray(np.array([p[1] for p in pairs], np.int32))

    n_ex = len(grad_slices)

    def body(jt_ref, it_ref, qt_ref, dot_ref, k_ref, kt_ref, v_ref, *rest):
        slice_refs, (dqt_ref, dkt_ref, dvt_ref) = rest[:n_ex], rest[n_ex:n_ex + 3]
        part_refs = rest[n_ex + 3:2 * n_ex + 3]
        dk_s, dv_s, send_sems, recv_sems, local_sems = rest[2 * n_ex + 3:]
        n = pl.program_id(1)
        j, i = jt_ref[n], it_ref[n]
        first_step = jnp.logical_and(pl.program_id(0) == 0, n == 0)
        last_step = jnp.logical_and(pl.program_id(0) == N_HEADS // hps - 1, n == len(pairs) - 1)

        @pl.when(first_step)
        def _():
            _start_copies(*_direct_copies(slice_refs, part_refs, send_sems, recv_sems, local_sems, False))

        @pl.when(n == 0)
        def _():
            dqt_ref[...] = jnp.zeros_like(dqt_ref)

        def update(diagonal, q0):
            qc = slice(q0, tq)
            cols = pl.ds(pl.multiple_of(i * tq + q0, tk), tq - q0)

            def softmax_bwd(hh, s, dp):
                if diagonal:
                    s = jnp.where(lax.broadcasted_iota(jnp.int32, s.shape, 0)
                                  <= lax.broadcasted_iota(jnp.int32, s.shape, 1), s, NEG)
                p = jnp.exp2(s)
                return p.astype(BF16), (p * dp).astype(BF16)

            def gradients(hh, p, ds):
                base = hh * SLAB
                vrows = slice(base, base + V_DIM_B)
                qrows = slice(base, base + QK_NOPE + QK_ROPE)
                dv = _dot_nt(dot_ref[vrows, qc], p)
                dk = _dot_nt(qt_ref[qrows, qc], ds)
                if diagonal:
                    dv_s[base:base + SLAB, :] = jnp.concatenate([dv, jnp.zeros((SLAB - V_DIM_B, tk), F32)], axis=0)
                    dk_s[base:base + SLAB, :] = jnp.concatenate(
                        [dk, jnp.zeros((SLAB - QK_NOPE - QK_ROPE, tk), F32)], axis=0)
                else:
                    dv_s[vrows, :] += dv
                    dk_s[qrows, :] += dk
                dqt_ref[qrows, cols] += _dot(kt_ref[qrows, :], ds)

            def scores(hh):
                sl = slice(hh * SLAB, (hh + 1) * SLAB)
                return _dot(k_ref[:, sl], qt_ref[sl, qc])

            def dprod(hh):
                sl = slice(hh * SLAB, (hh + 1) * SLAB)
                return _dot(v_ref[:, sl], dot_ref[sl, qc])

            s_next = scores(0)
            for hh in range(hps):
                s = s_next
                dp = dprod(hh)
                if hh + 1 < hps:
                    s_next = scores(hh + 1)
                gradients(hh, *softmax_bwd(hh, s, dp))

        first_tile = lax.div(j, ratio)
        for part in range(ratio):
            @pl.when(jnp.logical_and(i == first_tile, lax.rem(j, ratio) == part))
            def _():
                update(True, part * tk)

        @pl.when(i > first_tile)
        def _():
            update(False, 0)

        @pl.when(i == nq - 1)
        def _():
            dkt_ref[...] = (dk_s[...] * (1.0 / LOG2E)).astype(BF16)
            dvt_ref[...] = dv_s[...].astype(BF16)

        @pl.when(last_step)
        def _():
            _wait_copies(*_direct_copies(slice_refs, part_refs, send_sems, recv_sems, local_sems, False))

    grid_spec = pltpu.PrefetchScalarGridSpec(
        num_scalar_prefetch=2, grid=(N_HEADS // hps, len(pairs)),
        in_specs=[pl.BlockSpec((w, tq), lambda h, n, jt, it: (h, it[n])),
                  pl.BlockSpec((w, tq), lambda h, n, jt, it: (h, it[n])),
                  pl.BlockSpec((tk, w), lambda h, n, jt, it: (jt[n], h)),
                  pl.BlockSpec((w, tk), lambda h, n, jt, it: (h, jt[n])),
                  pl.BlockSpec((tk, w), lambda h, n, jt, it: (jt[n], h))] + [ANY_SPEC] * n_ex,
        out_specs=[pl.BlockSpec((w, t), lambda h, n, jt, it: (h, 0)),
                   pl.BlockSpec((w, tk), lambda h, n, jt, it: (h, jt[n])),
                   pl.BlockSpec((w, tk), lambda h, n, jt, it: (h, jt[n]))] + [ANY_SPEC] * n_ex,
        scratch_shapes=[pltpu.VMEM((w, tk), F32), pltpu.VMEM((w, tk), F32)] + _exchange_scratch(n_ex))
    outs = pl.pallas_call(
        body, name="mla_bwd", grid_spec=grid_spec,
        out_shape=[jax.ShapeDtypeStruct((HM, t), F32), jax.ShapeDtypeStruct((HM, t), BF16),
                   jax.ShapeDtypeStruct((HM, t), BF16)]
        + [jax.ShapeDtypeStruct(a.shape, a.dtype) for a in grad_slices],
        compiler_params=_params(("arbitrary", "arbitrary")),
    )(j_tab, i_tab, qt, d_ob_t, kb, kt, vb, *grad_slices)
    return outs[0], outs[1], outs[2], list(outs[3:])


def _merge_fwd(out_a, out_b, gates, x, w_oa, w_ob, w_out, g2, g3):
    t = x.shape[0]
    tm = _token_tile(t)

    def body(oa_ref, ob_ref, gates_ref, x_ref, woa_ref, wob_ref, wout_ref, g2_ref, g3_ref,
             oap_ref, obp_ref, merged_ref, y_ref, x1_ref, h2_ref):
        oa_p = _dot(oa_ref[...], woa_ref[...])
        ob_p = _dot(ob_ref[...], wob_ref[...])
        oap_ref[...] = oa_p.astype(BF16)
        obp_ref[...] = ob_p.astype(BF16)
        sa = _sigmoid(gates_ref[:, 0:D_MODEL].astype(F32))
        sb = _sigmoid(gates_ref[:, D_MODEL:2 * D_MODEL].astype(F32))
        merged = (sa * oa_p + sb * ob_p).astype(BF16)
        merged_ref[...] = merged
        y = _dot(merged, wout_ref[...])
        y_ref[...] = y
        x1 = x_ref[...] + y * _rms_r(y) * g2_ref[...]
        x1_ref[...] = x1
        h2_ref[...] = (x1 * _rms_r(x1) * g3_ref[...]).astype(BF16)

    def sds(dt):
        return jax.ShapeDtypeStruct((t, D_MODEL), dt)

    row = _row_spec(tm, D_MODEL)
    return pl.pallas_call(
        body, name="merge_fwd", grid=(t // tm,),
        in_specs=[_row_spec(tm, HM), _row_spec(tm, HM), _row_spec(tm, 2 * D_MODEL), row,
                  _full_spec((HM, D_MODEL)), _full_spec((HM, D_MODEL)), _full_spec((D_MODEL, D_MODEL)),
                  _full_spec((1, D_MODEL)), _full_spec((1, D_MODEL))],
        out_specs=[row] * 6,
        out_shape=[sds(BF16), sds(BF16), sds(BF16), sds(F32), sds(F32), sds(BF16)],
        compiler_params=_params(("parallel",)),
    )(out_a, out_b, gates, x, w_oa, w_ob, w_out, g2, g3)


def _merge_bwd(dx1, y, gates, oa_p, ob_p, out_a, out_b, out_b_t, merged, w_oa, w_ob, w_out, g2):
    t = dx1.shape[0]
    tm = _token_tile(t)

    def body(dx1_ref, y_ref, gates_ref, oap_ref, obp_ref, oa_ref, ob_ref, obt_ref, merged_ref,
             woa_ref, wob_ref, wout_ref, g2_ref,
             dgates_ref, doa_ref, dobt_ref, dg2_ref, dwoa_ref, dwob_ref, dwout_ref):
        @pl.when(pl.program_id(0) == 0)
        def _():
            dwoa_ref[...] = jnp.zeros_like(dwoa_ref)
            dwob_ref[...] = jnp.zeros_like(dwob_ref)
            dwout_ref[...] = jnp.zeros_like(dwout_ref)

        dx1v = dx1_ref[...]
        yv = y_ref[...]
        r2 = _rms_r(yv)
        _acc_rows(dg2_ref, dx1v * yv * r2)
        dy = _rms_bwd(yv, r2, g2_ref[...], dx1v).astype(BF16)
        dwout_ref[...] += _dot_tn(merged_ref[...], dy)
        dm = _dot_nt(dy, wout_ref[...])
        sa = _sigmoid(gates_ref[:, 0:D_MODEL].astype(F32))
        sb = _sigmoid(gates_ref[:, D_MODEL:2 * D_MODEL].astype(F32))
        d_oap = (dm * sa).astype(BF16)
        d_obp = (dm * sb).astype(BF16)
        dwoa_ref[...] += _dot_tn(oa_ref[...], d_oap)
        dwob_ref[...] += _dot_tn(ob_ref[...], d_obp)
        dgates_ref[:, 0:D_MODEL] = (dm * oap_ref[...].astype(F32) * sa * (1.0 - sa)).astype(BF16)
        dgates_ref[:, D_MODEL:2 * D_MODEL] = (dm * obp_ref[...].astype(F32) * sb * (1.0 - sb)).astype(BF16)
        doa_ref[...] = _dot_nt(d_oap, woa_ref[...]).astype(BF16)
        d_ob_t = _dot_nt(wob_ref[...], d_obp)
        for hd in range(N_HEADS):
            sl = slice(hd * SLAB, (hd + 1) * SLAB)
            delta = jnp.sum(d_ob_t[sl, :] * obt_ref[sl, :].astype(F32), axis=0, keepdims=True)
            dobt_ref[sl, :] = _plant_rows(d_ob_t[sl, :], V_DIM_B, delta).astype(BF16)

    def sds(n, dt):
        return jax.ShapeDtypeStruct((t, n), dt)

    row = _row_spec(tm, D_MODEL)
    return pl.pallas_call(
        body, name="merge_bwd", grid=(t // tm,),
        in_specs=[row, row, _row_spec(tm, 2 * D_MODEL), row, row, _row_spec(tm, HM), _row_spec(tm, HM),
                  _col_spec(HM, tm), row,
                  _full_spec((HM, D_MODEL)), _full_spec((HM, D_MODEL)), _full_spec((D_MODEL, D_MODEL)),
                  _full_spec((1, D_MODEL))],
        out_specs=[_row_spec(tm, 2 * D_MODEL), _row_spec(tm, HM), _col_spec(HM, tm), _full_spec((1, D_MODEL)),
                   _full_spec((HM, D_MODEL)), _full_spec((HM, D_MODEL)), _full_spec((D_MODEL, D_MODEL))],
        out_shape=[sds(2 * D_MODEL, BF16), sds(HM, BF16), jax.ShapeDtypeStruct((HM, t), BF16),
                   jax.ShapeDtypeStruct((1, D_MODEL), F32),
                   jax.ShapeDtypeStruct((HM, D_MODEL), F32), jax.ShapeDtypeStruct((HM, D_MODEL), F32),
                   jax.ShapeDtypeStruct((D_MODEL, D_MODEL), F32)],
        compiler_params=_params(("arbitrary",)),
    )(dx1, y, gates, oa_p, ob_p, out_a, out_b, out_b_t, merged, w_oa, w_ob, w_out, g2)


def _mlp_fwd_bwd(x1, h2, target, w_up, w_down, g3, g4):
    t = x1.shape[0]
    tm = _token_tile(t)
    fs = D_FF // N_DEV

    def body(x1_ref, h2_ref, tgt_ref, wup_ref, wdown_ref, g3_ref, g4_ref,
             a_ref, du_ref, dy2_ref, dx1_ref, loss_ref, dg3_ref, dg4_ref):
        x1v = x1_ref[...]
        h2v = h2_ref[...]
        u = jnp.concatenate([_dot(h2v, wup_ref[s]) for s in range(N_DEV)], axis=1)
        ru = jnp.maximum(u, 0.0)
        a = (ru * ru).astype(BF16)
        a_ref[...] = a
        y2 = _dot(a, wdown_ref[...])
        r4 = _rms_r(y2)
        diff = x1v + y2 * r4 * g4_ref[...] - tgt_ref[...]
        _acc_rows(loss_ref, jnp.sum(diff * diff, axis=-1, keepdims=True) * (0.5 / D_MODEL)
                  * jnp.ones((1, SLAB), F32))
        dx2 = diff * (1.0 / D_MODEL)
        _acc_rows(dg4_ref, dx2 * y2 * r4)
        dy2 = _rms_bwd(y2, r4, g4_ref[...], dx2).astype(BF16)
        dy2_ref[...] = dy2
        du = (_dot_nt(dy2, wdown_ref[...]) * (2.0 * ru)).astype(BF16)
        du_ref[...] = du
        dh2 = _dot_nt(du[:, 0:fs], wup_ref[0])
        for s in range(1, N_DEV):
            dh2 += _dot_nt(du[:, s * fs:(s + 1) * fs], wup_ref[s])
        r3 = _rms_r(x1v)
        _acc_rows(dg3_ref, dh2 * x1v * r3)
        dx1_ref[...] = dx2 + _rms_bwd(x1v, r3, g3_ref[...], dh2)

    row = _row_spec(tm, D_MODEL)
    frow = _row_spec(tm, D_FF)
    vec = _full_spec((1, D_MODEL))
    return pl.pallas_call(
        body, name="mlp_fwd_bwd", grid=(t // tm,),
        in_specs=[row, row, row, _full_spec((N_DEV, D_MODEL, fs)), _full_spec((D_FF, D_MODEL)), vec, vec],
        out_specs=[frow, frow, row, row, _full_spec((1, SLAB)), vec, vec],
        out_shape=[jax.ShapeDtypeStruct((t, D_FF), BF16), jax.ShapeDtypeStruct((t, D_FF), BF16),
                   jax.ShapeDtypeStruct((t, D_MODEL), BF16), jax.ShapeDtypeStruct((t, D_MODEL), F32),
                   jax.ShapeDtypeStruct((1, SLAB), F32), jax.ShapeDtypeStruct((1, D_MODEL), F32),
                   jax.ShapeDtypeStruct((1, D_MODEL), F32)],
        compiler_params=_params(("arbitrary",)),
    )(x1, h2, target, w_up, w_down, g3, g4)


def _latent_bwd(dqb_t, dkb_t, dvb_t, cq, ckv, cqn, ckvn, rope_ct, rope_s1t, rope_s2t, g_q, g_kv, w_qb, w_kvb):
    t = cq.shape[0]
    tm = _token_tile(t)

    def body(dqt_ref, dkt_ref, dvt_ref, cq_ref, ckv_ref, cqn_ref, ckvn_ref, ct_ref, s1t_ref, s2t_ref,
             gq_ref, gkv_ref, wqb_ref, wkvb_ref,
             dlate_ref, dgq_ref, dgkv_ref, dwqb_ref, dwkvb_ref, dqbrt_ref, dkvbt_ref):
        @pl.when(pl.program_id(0) == 0)
        def _():
            dwqb_ref[...] = jnp.zeros_like(dwqb_ref)
            dwkvb_ref[...] = jnp.zeros_like(dwkvb_ref)

        ct, s1t, s2t = ct_ref[...], s1t_ref[...], s2t_ref[...]
        dk_sum_t = jnp.zeros((SLAB, tm), F32)
        for hd in range(N_HEADS):
            sl = slice(hd * SLAB, (hd + 1) * SLAB)
            dqbrt_ref[sl, :] = _rope_t_bwd(dqt_ref[sl, :] * SCALE_B, ct, s1t, s2t).astype(BF16)
            dk_sum_t += dkt_ref[sl, :].astype(F32)
        dkvbt_ref[0:HM, :] = dkt_ref[...]
        dkvbt_ref[HM:2 * HM, :] = dvt_ref[...]
        dkr = _rope_t_bwd(dk_sum_t, ct, s1t, s2t).T
        dwqb_ref[...] += _dot(dqbrt_ref[...], cqn_ref[...])
        dwkvb_ref[...] += _dot(dkvbt_ref[...], ckvn_ref[...])
        dcqn = _dot(wqb_ref[...], dqbrt_ref[...]).T
        cq = cq_ref[...]
        rq = _rms_r(cq)
        _acc_rows(dgq_ref, dcqn * cq * rq)
        dcq = _rms_bwd(cq, rq, gq_ref[...], dcqn)
        dckvn = _dot(wkvb_ref[...], dkvbt_ref[...]).T
        ckv = ckv_ref[...]
        rkv = _rms_r(ckv)
        _acc_rows(dgkv_ref, dckvn * ckv * rkv)
        dckv = _rms_bwd(ckv, rkv, gkv_ref[...], dckvn)
        dlate_ref[:, 0:C_CKV - C_CQ] = dcq.astype(BF16)
        dlate_ref[:, C_CKV - C_CQ:C_KR - C_CQ] = dckv.astype(BF16)
        dlate_ref[:, C_KR - C_CQ:D_IN_PAD - C_CQ] = dkr.astype(BF16)

    hmt = _col_spec(HM, tm)
    tab = _col_spec(SLAB, tm)
    return pl.pallas_call(
        body, name="latent_bwd", grid=(t // tm,),
        in_specs=[hmt, hmt, hmt,
                  _row_spec(tm, Q_LORA), _row_spec(tm, KV_LORA), _row_spec(tm, Q_LORA), _row_spec(tm, KV_LORA),
                  tab, tab, tab, _full_spec((1, Q_LORA)), _full_spec((1, KV_LORA)),
                  _full_spec((Q_LORA, HM)), _full_spec((KV_LORA, 2 * HM))],
        out_specs=[_row_spec(tm, D_IN_PAD - C_CQ), _full_spec((1, Q_LORA)), _full_spec((1, KV_LORA)),
                   _full_spec((HM, Q_LORA)), _full_spec((2 * HM, KV_LORA))],
        out_shape=[jax.ShapeDtypeStruct((t, D_IN_PAD - C_CQ), BF16),
                   jax.ShapeDtypeStruct((1, Q_LORA), F32), jax.ShapeDtypeStruct((1, KV_LORA), F32),
                   jax.ShapeDtypeStruct((HM, Q_LORA), F32), jax.ShapeDtypeStruct((2 * HM, KV_LORA), F32)],
        scratch_shapes=[pltpu.VMEM((HM, tm), BF16), pltpu.VMEM((2 * HM, tm), BF16)],
        compiler_params=_params(("arbitrary",)),
    )(dqb_t, dkb_t, dvb_t, cq, ckv, cqn, ckvn, rope_ct, rope_s1t, rope_s2t, g_q, g_kv, w_qb, w_kvb)


def _inproj_bwd(dgates, dqkv, dlate, x, dx1, g1, w_in, grad_slices, only):
    t = x.shape[0]
    tm = _token_tile(t)
    n_ex = len(grad_slices)
    zeroed = [a for a in range(n_ex) if only[a] is not None]

    def body(dgates_ref, dqkv_ref, dlate_ref, x_ref, dx1_ref, g1_ref, win_ref, *rest):
        slice_refs = rest[:n_ex]
        dx_ref, dg1_ref = rest[n_ex + len(zeroed):n_ex + len(zeroed) + 2]
        part_refs = rest[n_ex + len(zeroed) + 2:2 * n_ex + len(zeroed) + 2]
        dproj_ref, send_sems, recv_sems, local_sems = rest[2 * n_ex + len(zeroed) + 2:]

        @pl.when(pl.program_id(0) == 0)
        def _():
            _start_copies(*_direct_copies(slice_refs, part_refs, send_sems, recv_sems, local_sems, False,
                                          only=only))

        dproj_ref[:, C_GATES:C_QA] = dgates_ref[...]
        dproj_ref[:, C_QA:C_CQ] = dqkv_ref[...]
        dproj_ref[:, C_CQ:D_IN_PAD] = dlate_ref[...]
        dh = _dot_nt(dproj_ref[...], win_ref[...])
        xv = x_ref[...]
        r1 = _rms_r(xv)
        _acc_rows(dg1_ref, dh * xv * r1)
        dx_ref[...] = dx1_ref[...] + _rms_bwd(xv, r1, g1_ref[...], dh)

        @pl.when(pl.program_id(0) == t // tm - 1)
        def _():
            _wait_copies(*_direct_copies(slice_refs, part_refs, send_sems, recv_sems, local_sems, False,
                                         only=only))

    kvw = N_KV_A * SLAB
    row = _row_spec(tm, D_MODEL)
    outs = pl.pallas_call(
        body, name="inproj_bwd", grid=(t // tm,),
        in_specs=[_row_spec(tm, 2 * D_MODEL), _row_spec(tm, HM + 2 * kvw), _row_spec(tm, D_IN_PAD - C_CQ),
                  row, row, _full_spec((1, D_MODEL)), _full_spec((D_MODEL, D_IN_PAD))]
        + [ANY_SPEC] * (n_ex + len(zeroed)),
        out_specs=[row, _full_spec((1, D_MODEL))] + [ANY_SPEC] * n_ex,
        out_shape=[jax.ShapeDtypeStruct((t, D_MODEL), F32), jax.ShapeDtypeStruct((1, D_MODEL), F32)]
        + [jax.ShapeDtypeStruct(a.shape, a.dtype) for a in grad_slices],
        input_output_aliases={7 + n_ex + i: 2 + a for i, a in enumerate(zeroed)},
        scratch_shapes=[pltpu.VMEM((tm, D_IN_PAD), BF16)] + _exchange_scratch(n_ex),
        compiler_params=_params(("arbitrary",)),
    )(dgates, dqkv, dlate, x, dx1, g1, w_in, *grad_slices, *[jnp.zeros_like(grad_slices[a]) for a in zeroed])
    return outs[0], outs[1], list(outs[2:])


def _matmul_tn(a, b, name, out_dtype=F32, n_shards=1):
    t, k = a.shape
    n = b.shape[1]
    bt = min(t, 512)
    bn = min(n, 2048)
    bk = min(k, 2048 * 1024 // bn)
    ns = n // n_shards
    per_block = bn // ns
    steps = t // bt

    def body(a_ref, b_ref, o_ref, acc):
        s = pl.program_id(2)

        @pl.when(s == 0)
        def _():
            acc[...] = jnp.zeros_like(acc)

        acc[...] += _dot_tn(a_ref[...], b_ref[...])

        @pl.when(s == steps - 1)
        def _():
            if n_shards > 1:
                for p in range(per_block):
                    o_ref[p] = acc[:, p * ns:(p + 1) * ns].astype(out_dtype)
            else:
                o_ref[...] = acc[...].astype(out_dtype)

    if n_shards > 1:
        out_spec = pl.BlockSpec((per_block, bk, ns), lambda i, j, s: (j, i, 0))
        out_shape = jax.ShapeDtypeStruct((n_shards, k, ns), out_dtype)
    else:
        out_spec = pl.BlockSpec((bk, bn), lambda i, j, s: (i, j))
        out_shape = jax.ShapeDtypeStruct((k, n), out_dtype)
    return pl.pallas_call(
        body, name=name, grid=(k // bk, n // bn, steps),
        in_specs=[pl.BlockSpec((bt, bk), lambda i, j, s: (s, i)), pl.BlockSpec((bt, bn), lambda i, j, s: (s, j))],
        out_specs=out_spec, out_shape=out_shape, scratch_shapes=[pltpu.VMEM((bk, bn), F32)],
        compiler_params=_params(("parallel", "parallel", "arbitrary")),
    )(a, b)


def _two_level_gather(srcs, dsts, send_sems, recv_sems, local_sems):
    n = len(srcs)
    x, y, c = _mesh_pos()
    me, sibling = (x, y, c), (x, y, 1 - c)
    chips = [(1 - x, y), (x, 1 - y), (1 - x, 1 - y)]

    def slot(a, px, py, pc):
        return dsts[a].at[4 * px + 2 * py + pc]

    def copy(a, k, block, to, src=None):
        return pltpu.make_async_remote_copy(
            src_ref=slot(a, *block) if src is None else src, dst_ref=slot(a, *block),
            send_sem=send_sems.at[(N_DEV - 1) * a + k], recv_sem=recv_sems.at[(N_DEV - 1) * a + k],
            device_id=to, device_id_type=pl.DeviceIdType.MESH)

    def own_copies():
        mine = [pltpu.make_async_copy(srcs[a], slot(a, *me), local_sems.at[a]) for a in range(n)]
        first = []
        for a in range(n):
            first.append(copy(a, 0, me, sibling, src=srcs[a]))
            first += [copy(a, 1 + j, me, (*chip, c), src=srcs[a]) for j, chip in enumerate(chips)]
        return mine, first

    def start():
        mine, first = own_copies()
        for cp in mine + first:
            cp.start()

    def finish():
        mine, first = own_copies()
        passed = []
        for j, chip in enumerate(chips):
            for a in range(n):
                copy(a, 1 + j, (*chip, c), me).wait_recv()
                passed.append(copy(a, 4 + j, (*chip, c), sibling))
                passed[-1].start()
        for a in range(n):
            copy(a, 0, sibling, me).wait_recv()
        for j, chip in enumerate(chips):
            for a in range(n):
                copy(a, 4 + j, (*chip, 1 - c), me).wait_recv()
        for cp in first + passed:
            cp.wait_send()
        for cp in mine:
            cp.wait()

    return start, finish


def _gather_small(small):
    def body(s_ref, s_dst, *sems):
        smalls = _direct_copies([s_ref], [s_dst], *sems, True)
        _start_copies(*smalls)
        _wait_copies(*smalls)

    return pl.pallas_call(
        body, name="gather_small",
        out_shape=jax.ShapeDtypeStruct((N_DEV,) + small.shape, small.dtype),
        in_specs=[ANY_SPEC], out_specs=ANY_SPEC,
        scratch_shapes=_exchange_scratch(1),
    )(small)


def _adamw(parts, w, m, v, name):
    n_parts = len(parts)
    _, k, n = parts[0].shape
    bk = min(k, ADAM_ROWS)
    c1 = 1.0 - ADAM_B1 ** ADAM_STEP
    c2 = 1.0 - ADAM_B2 ** ADAM_STEP

    def body(*refs):
        p_refs, (w_ref, m_ref, v_ref, g_ref, d_ref, mo_ref, vo_ref) = refs[:n_parts], refs[n_parts:]
        g = p_refs[0][0].astype(F32)
        for p_ref in p_refs:
            for s in range(N_DEV):
                if p_ref is not p_refs[0] or s > 0:
                    g = g + p_ref[s].astype(F32)
        g_ref[0] = g
        m_new = ADAM_B1 * m_ref[0] + (1.0 - ADAM_B1) * g
        v_new = ADAM_B2 * v_ref[0] + (1.0 - ADAM_B2) * (g * g)
        mo_ref[0] = m_new
        vo_ref[0] = v_new
        m_hat = m_new / c1
        v_hat = v_new / c2
        d_ref[0] = -ADAM_LR * (m_hat / (jnp.sqrt(v_hat) + ADAM_EPS) + ADAM_WD * w_ref[0])

    blk = pl.BlockSpec((1, bk, n), lambda i: (0, i, 0))
    out = jax.ShapeDtypeStruct((1, k, n), F32)
    return pl.pallas_call(
        body, name=name, grid=(k // bk,),
        in_specs=[pl.BlockSpec((N_DEV, bk, n), lambda i: (0, i, 0))] * n_parts + [blk, blk, blk],
        out_specs=[blk] * 4, out_shape=[out] * 4,
        compiler_params=_params(("parallel",)),
    )(*parts, w, m, v)


def _adamw_small(parts, w, m, v):
    k = len(SMALL_LAYOUT)
    c1 = 1.0 - ADAM_B1 ** ADAM_STEP
    c2 = 1.0 - ADAM_B2 ** ADAM_STEP

    def body(p_ref, *refs):
        w_refs, m_refs, v_refs, outs = refs[:k], refs[k:2 * k], refs[2 * k:3 * k], refs[3 * k:]
        total = p_ref[0]
        for s in range(1, N_DEV):
            total = total + p_ref[s]
        for i, (_, row, off, width) in enumerate(SMALL_LAYOUT):
            g = total[row:row + 1, off:off + width]
            m_new = ADAM_B1 * m_refs[i][...] + (1.0 - ADAM_B1) * g
            v_new = ADAM_B2 * v_refs[i][...] + (1.0 - ADAM_B2) * (g * g)
            outs[4 * i][...] = g
            outs[4 * i + 1][...] = -ADAM_LR * ((m_new / c1) / (jnp.sqrt(v_new / c2) + ADAM_EPS)
                                               + ADAM_WD * w_refs[i][...])
            outs[4 * i + 2][...] = m_new
            outs[4 * i + 3][...] = v_new
        outs[4 * k][...] = total[SMALL_LOSS_ROW:SMALL_LOSS_ROW + 1, SMALL_LOSS_OFF:SMALL_LOSS_OFF + 1]

    names = [name for name, *_ in SMALL_LAYOUT]
    out_shape = [jax.ShapeDtypeStruct(w[name].shape, F32) for name in names for _ in range(4)]
    outs = pl.pallas_call(
        body, name="adamw_small", out_shape=out_shape + [jax.ShapeDtypeStruct((1, 1), F32)],
    )(parts, *[w[n] for n in names], *[m[n] for n in names], *[v[n] for n in names])
    return {name: tuple(outs[4 * i:4 * i + 4]) for i, name in enumerate(names)}, outs[4 * k]


def _pad_heads_cols(w, heads, width):
    k = w.shape[0]
    w = w.reshape(k, heads, width)
    return jnp.pad(w, ((0, 0), (0, 0), (0, SLAB - width))).reshape(k, heads * SLAB)


def _unpad_heads_cols(w, heads, width):
    k = w.shape[0]
    return w.reshape(k, heads, SLAB)[:, :, :width].reshape(k, heads * width)


def _pad_heads_rows(w, heads, width):
    n = w.shape[1]
    w = w.reshape(heads, width, n)
    return jnp.pad(w, ((0, 0), (0, SLAB - width), (0, 0))).reshape(heads * SLAB, n)


def _unpad_heads_rows(w, heads, width):
    n = w.shape[1]
    return w.reshape(heads, SLAB, n)[:, :width, :].reshape(heads * width, n)


def _pad_w_in(w_in):
    o = 2 * D_MODEL
    qa = _pad_heads_cols(w_in[:, o:o + 512], N_HEADS, HEAD_A)
    ka = _pad_heads_cols(w_in[:, o + 512:o + 640], N_KV_A, HEAD_A)
    va = _pad_heads_cols(w_in[:, o + 640:o + 768], N_KV_A, HEAD_A)
    kr = jnp.pad(w_in[:, o + 1152:o + 1184], ((0, 0), (QK_NOPE, SLAB - QK_NOPE - QK_ROPE)))
    return jnp.concatenate([w_in[:, :o], qa, ka, va, w_in[:, o + 768:o + 1152], kr], axis=1)


def _unpad_w_in(w):
    qa = _unpad_heads_cols(w[:, C_QA:C_KA], N_HEADS, HEAD_A)
    ka = _unpad_heads_cols(w[:, C_KA:C_VA], N_KV_A, HEAD_A)
    va = _unpad_heads_cols(w[:, C_VA:C_CQ], N_KV_A, HEAD_A)
    kr = w[:, C_KR + QK_NOPE:C_KR + QK_NOPE + QK_ROPE]
    return jnp.concatenate([w[:, :C_QA], qa, ka, va, w[:, C_CQ:C_KR], kr], axis=1)


def _pad_w_kvb(w_kvb):
    w = w_kvb.reshape(KV_LORA, N_HEADS, QK_NOPE + V_DIM_B)
    k = jnp.pad(w[:, :, :QK_NOPE], ((0, 0), (0, 0), (0, SLAB - QK_NOPE))).reshape(KV_LORA, HM)
    v = jnp.pad(w[:, :, QK_NOPE:], ((0, 0), (0, 0), (0, SLAB - V_DIM_B))).reshape(KV_LORA, HM)
    return jnp.concatenate([k, v], axis=1)


def _unpad_w_kvb(w):
    k = w[:, :HM].reshape(KV_LORA, N_HEADS, SLAB)[:, :, :QK_NOPE]
    v = w[:, HM:].reshape(KV_LORA, N_HEADS, SLAB)[:, :, :V_DIM_B]
    return jnp.concatenate([k, v], axis=2).reshape(KV_LORA, N_HEADS * (QK_NOPE + V_DIM_B))


def _col_shards(w):
    k, n = w.shape
    return w.reshape(k, N_DEV, n // N_DEV).transpose(1, 0, 2)


def _from_col_shards(s):
    _, k, ns = s.shape
    return s.transpose(1, 0, 2).reshape(k, N_DEV * ns)


def _freq_row():
    freqs = ROPE_THETA ** (-jnp.arange(0, QK_ROPE, 2, dtype=F32) / QK_ROPE)
    return jnp.concatenate([jnp.zeros((QK_NOPE,), F32), freqs, freqs,
                            jnp.zeros((SLAB - QK_NOPE - QK_ROPE,), F32)]).reshape(1, SLAB)


SMALL_D_ROWS = ("pre_norm_mix", "post_norm_mix", "pre_norm_mlp", "post_norm_mlp")
SMALL_LAYOUT = tuple((name, i, 0, D_MODEL) for i, name in enumerate(SMALL_D_ROWS)) + (
    ("q_a_norm", 4, 0, Q_LORA), ("kv_a_norm", 4, 256, KV_LORA), ("sinks", 4, 384, N_HEADS))
SMALL_LOSS_ROW, SMALL_LOSS_OFF = 4, 512


def _pack_small(vals):
    row4 = jnp.concatenate([vals["q_a_norm"].reshape(-1), vals["kv_a_norm"].reshape(-1), vals["sinks"].reshape(-1),
                            jnp.zeros((SMALL_LOSS_OFF - 392,), F32), vals["loss"].reshape(-1),
                            jnp.zeros((1024 - SMALL_LOSS_OFF - 1,), F32)])
    rows = [vals[n].reshape(1024) for n in SMALL_D_ROWS] + [row4]
    return jnp.concatenate([jnp.stack(rows), jnp.zeros((SMALL_ROWS - 5, 1024), F32)], axis=0)


WEIGHT_ORDER = ("pre_norm_mix", "w_in", "q_a_norm", "w_q_b", "kv_a_norm", "w_kv_b", "sinks", "w_o_a", "w_o_b",
                "w_out", "post_norm_mix", "pre_norm_mlp", "w_up", "w_down", "post_norm_mlp")


def kernel(x, positions, pre_norm_mix, w_in, q_a_norm, w_q_b, kv_a_norm, w_kv_b, sinks, w_o_a, w_o_b, w_out, post_norm_mix, pre_norm_mlp, w_up, w_down, post_norm_mlp, loss_target, m_pre_norm_mix, m_w_in, m_q_a_norm, m_w_q_b, m_kv_a_norm, m_w_kv_b, m_sinks, m_w_o_a, m_w_o_b, m_w_out, m_post_norm_mix, m_pre_norm_mlp, m_w_up, m_w_down, m_post_norm_mlp, v_pre_norm_mix, v_w_in, v_q_a_norm, v_w_q_b, v_kv_a_norm, v_w_kv_b, v_sinks, v_w_o_a, v_w_o_b, v_w_out, v_post_norm_mix, v_pre_norm_mlp, v_w_up, v_w_down, v_post_norm_mlp):
    weights = dict(pre_norm_mix=pre_norm_mix, w_in=w_in, q_a_norm=q_a_norm, w_q_b=w_q_b, kv_a_norm=kv_a_norm,
                   w_kv_b=w_kv_b, sinks=sinks, w_o_a=w_o_a, w_o_b=w_o_b, w_out=w_out, post_norm_mix=post_norm_mix,
                   pre_norm_mlp=pre_norm_mlp, w_up=w_up, w_down=w_down, post_norm_mlp=post_norm_mlp)
    m_in = dict(pre_norm_mix=m_pre_norm_mix, w_in=m_w_in, q_a_norm=m_q_a_norm, w_q_b=m_w_q_b, kv_a_norm=m_kv_a_norm,
                w_kv_b=m_w_kv_b, sinks=m_sinks, w_o_a=m_w_o_a, w_o_b=m_w_o_b, w_out=m_w_out,
                post_norm_mix=m_post_norm_mix, pre_norm_mlp=m_pre_norm_mlp, w_up=m_w_up, w_down=m_w_down,
                post_norm_mlp=m_post_norm_mlp)
    v_in = dict(pre_norm_mix=v_pre_norm_mix, w_in=v_w_in, q_a_norm=v_q_a_norm, w_q_b=v_w_q_b, kv_a_norm=v_kv_a_norm,
                w_kv_b=v_w_kv_b, sinks=v_sinks, w_o_a=v_w_o_a, w_o_b=v_w_o_b, w_out=v_w_out,
                post_norm_mix=v_post_norm_mix, pre_norm_mlp=v_pre_norm_mlp, w_up=v_w_up, w_down=v_w_down,
                post_norm_mlp=v_post_norm_mlp)

    xs, pos, target = x[0], positions[0], loss_target[0]
    t = xs.shape[0]
    pos_col = pos.reshape(t, 1)
    pos_row = pos.reshape(1, t)
    g1, g2, g3, g4 = (weights[n] for n in SMALL_D_ROWS)
    g_q, g_kv = q_a_norm, kv_a_norm
    sink_vec = sinks.reshape(N_HEADS)
    shard = {n: weights[n][0].astype(BF16) for n in EARLY + LATE}

    tables, (e_in, e_qb, e_kvb) = _rope_tables(pos_col, _freq_row(), [shard[n] for n in EARLY])
    w_in_p = _pad_w_in(_from_col_shards(e_in))
    w_qb = _pad_heads_cols(_from_col_shards(e_qb), N_HEADS, QK_NOPE + QK_ROPE)
    w_kvb = _pad_w_kvb(_from_col_shards(e_kvb))

    (h, gates, qa, ka, va, cq, ckv, cqn, ckvn, kb, vb, qt, kt, vt) = _inproj_fwd(
        xs, g1, w_in_p, g_q, g_kv, w_kvb, w_qb.T, w_kvb[:, :HM].T, w_kvb[:, HM:].T, w_in_p[:, C_KR:].T, tables)
    out_a, lse_a = _swa_fwd(qa, ka, va, pos_col, pos_row, sink_vec)
    out_b, out_b_t, qt_lse, (l_oa, l_ob, l_out, w_up_s, l_down) = _mla_fwd(qt, kb, vt, [shard[n] for n in LATE])
    w_oa = _pad_heads_rows(_from_col_shards(l_oa), N_HEADS, HEAD_A)
    w_ob = _pad_heads_rows(_from_col_shards(l_ob), N_HEADS, V_DIM_B)
    w_out_f = l_out.reshape(D_MODEL, D_MODEL)
    w_down_f = l_down.reshape(D_FF, D_MODEL)

    oa_p, ob_p, merged, y, x1, h2 = _merge_fwd(out_a, out_b, gates, xs, w_oa, w_ob, w_out_f, g2, g3)
    a, du, dy2, dx1, loss, dg3, dg4 = _mlp_fwd_bwd(x1, h2, target, w_up_s, w_down_f, g3, g4)
    (dgates, d_oa, d_ob_t, dg2, dw_oa, dw_ob, dw_out) = _merge_bwd(
        dx1, y, gates, oa_p, ob_p, out_a, out_b, out_b_t, merged, w_oa, w_ob, w_out_f, g2)
    late_slices = [
        _col_shards(_unpad_heads_rows(dw_oa, N_HEADS, HEAD_A)).astype(BF16),
        _col_shards(_unpad_heads_rows(dw_ob, N_HEADS, V_DIM_B)).astype(BF16),
        dw_out.astype(BF16).reshape(N_DEV, D_MODEL // N_DEV, D_MODEL),
        _matmul_tn(h2, du, "dw_up", BF16, N_DEV),
        _matmul_tn(a, dy2, "dw_down", BF16).reshape(N_DEV, D_FF // N_DEV, D_MODEL),
    ]
    dqkv_a, dsink = _swa_bwd(qa, ka, va, out_a, d_oa, lse_a, pos_col, pos_row, sink_vec)
    dw_in_early = jnp.concatenate([_matmul_tn(h, dgates, "dw_in_gates"), _matmul_tn(h, dqkv_a, "dw_in_mixer_a"),
                                   jnp.zeros((D_MODEL, D_IN_PAD - C_CQ), F32)], axis=1)
    late_slices.append(_col_shards(_unpad_w_in(dw_in_early)).astype(BF16))
    dqb_t, dkb_t, dvb_t, late_parts = _mla_bwd(qt_lse, kb, kt, vb, d_ob_t, late_slices)
    w_in_early_parts = late_parts.pop()
    dproj_late, dgq, dgkv, dw_qb_t, dw_kvb_t = _latent_bwd(
        dqb_t, dkb_t, dvb_t, cq, ckv, cqn, ckvn, *tables[3:], g_q, g_kv, w_qb, w_kvb)
    dw_l = _matmul_tn(h, dproj_late, "dw_in_latents")
    late_cols = jnp.concatenate([dw_l[:, :Q_LORA + KV_LORA], dw_l[:, C_KR - C_CQ + QK_NOPE:C_KR - C_CQ + Q_HEAD_B]],
                                axis=1)
    shard_cols = w_in.shape[2]
    head = late_cols.shape[1] - shard_cols
    w_in_late = jnp.concatenate([
        jnp.zeros((N_DEV - 2, D_MODEL, shard_cols), F32),
        jnp.pad(late_cols[:, :head], ((0, 0), (shard_cols - head, 0)))[None], late_cols[:, head:][None]])
    early_slices = [
        w_in_late.astype(BF16),
        _col_shards(_unpad_heads_cols(dw_qb_t.T, N_HEADS, QK_NOPE + QK_ROPE)).astype(BF16),
        _col_shards(_unpad_w_kvb(dw_kvb_t.T)).astype(BF16),
    ]
    dx, dg1, early_parts = _inproj_bwd(dgates, dqkv_a, dproj_late, xs, dx1, g1, w_in_p, early_slices,
                                       only=[(N_DEV - 2, N_DEV - 1), None, None])
    small_grads = {"pre_norm_mix": dg1, "post_norm_mix": dg2, "pre_norm_mlp": dg3, "post_norm_mlp": dg4,
                   "q_a_norm": dgq, "kv_a_norm": dgkv, "sinks": dsink.reshape(N_HEADS, BLOCK).sum(axis=1),
                   "loss": loss[0, 0:1]}
    s_parts = _gather_small(_pack_small(small_grads))

    updates = {}
    all_parts = [[w_in_early_parts, early_parts[0]]] + [[p] for p in early_parts[1:] + late_parts]
    for name, parts in zip(EARLY + LATE, all_parts):
        outs = _adamw(parts, weights[name], m_in[name], v_in[name], "adamw_" + name)
        for kind, arr in zip(("g", "d", "m", "v"), outs):
            updates[kind, name] = arr
    small_out, loss_sum = _adamw_small(s_parts, weights, m_in, v_in)
    for name, outs in small_out.items():
        for kind, arr in zip(("g", "d", "m", "v"), outs):
            updates[kind, name] = arr
    results = [updates[kind, name] for kind in ("g", "d", "m", "v") for name in WEIGHT_ORDER]
    return (loss_sum.reshape(()), dx[None], *results)
```

```python
import functools

import numpy as np
import jax
import jax.numpy as jnp
from jax import lax
from jax.experimental import pallas as pl
from jax.experimental.pallas import tpu as pltpu

F32 = jnp.float32
BF16 = jnp.bfloat16

D_MODEL = 1024
D_FF = 4096
N_HEADS = 8
N_KV_A = 2
GROUP_A = N_HEADS // N_KV_A
HEAD_A = 64
QK_NOPE = 64
QK_ROPE = 32
V_DIM_B = 64
Q_LORA = 256
KV_LORA = 128
BLOCK = 128
SLAB = 128
ROPE_THETA = 10000.0
EPS = 1e-6
N_DEV = 8
NEG = -1e30

SCALE_A = HEAD_A ** -0.5
SCALE_B = (QK_NOPE + QK_ROPE) ** -0.5
LOG2E = 1.4426950408889634
SCORE_B = SCALE_B * LOG2E
MLA_HEADS_PER_STEP = 4
MLA_FWD_HEADS_PER_STEP = 8
Q_HEAD_B = QK_NOPE + QK_ROPE
ONES_ROWS = 16
SLOPES_A = tuple(2.0 ** (-8.0 * (h + 1) / N_HEADS) for h in range(N_HEADS))

ADAM_LR = 0.001
ADAM_B1 = 0.9
ADAM_B2 = 0.999
ADAM_EPS = 1e-08
ADAM_WD = 0.01
ADAM_STEP = 10

HM = N_HEADS * SLAB
C_GATES = 0
C_QA = 2 * D_MODEL
C_KA = C_QA + HM
C_VA = C_KA + N_KV_A * SLAB
C_CQ = C_VA + N_KV_A * SLAB
C_CKV = C_CQ + Q_LORA
C_KR = C_CKV + KV_LORA
D_IN_PAD = C_KR + SLAB

VMEM_LIMIT = 56 * 1024 * 1024

EARLY = ("w_in", "w_q_b", "w_kv_b")
LATE = ("w_o_a", "w_o_b", "w_out", "w_up", "w_down")
ADAM_ROWS = 256
SMALL_ROWS = 8


def _token_tile(t):
    return min(256, t)


def _wide_token_tile(t):
    return min(512, t)


def _attn_tile(t):
    return 512 if t >= 2048 else 128


def _params(sem, vmem=VMEM_LIMIT):
    return pltpu.CompilerParams(dimension_semantics=sem, vmem_limit_bytes=vmem)


def _dot(a, b):
    return jnp.dot(a, b, preferred_element_type=F32)


def _dot_nt(a, b):
    return lax.dot_general(a, b, (((1,), (1,)), ((), ())), preferred_element_type=F32)


def _dot_tn(a, b):
    return lax.dot_general(a, b, (((0,), (0,)), ((), ())), preferred_element_type=F32)


def _rms_r(x):
    return lax.rsqrt(jnp.mean(x * x, axis=-1, keepdims=True) + EPS)


def _rms_bwd(x, r, g, dy):
    t = dy * g
    return r * t - x * (r * r * r) * jnp.mean(x * t, axis=-1, keepdims=True)


def _sigmoid(x):
    return 1.0 / (1.0 + jnp.exp(-x))


def _rope(x, c, s1, s2):
    return x * c + pltpu.roll(x, SLAB - 16, 1) * s1 + pltpu.roll(x, 16, 1) * s2


def _rope_bwd(d, c, s1, s2):
    return d * c + pltpu.roll(d * s1, 16, 1) + pltpu.roll(d * s2, SLAB - 16, 1)


def _roll_rows(x, shift):
    return jnp.concatenate([x[-shift:], x[:-shift]], axis=0)


def _rope_t(x, c, s1, s2):
    return x * c + _roll_rows(x, SLAB - 16) * s1 + _roll_rows(x, 16) * s2


def _rope_t_bwd(d, c, s1, s2):
    return d * c + _roll_rows(d * s1, 16) + _roll_rows(d * s2, SLAB - 16)


def _plant_rows(slab, row, vals):
    hi = vals.astype(BF16).astype(F32)
    lo = (vals - hi).astype(BF16).astype(F32)
    idx = lax.broadcasted_iota(jnp.int32, slab.shape, 0)
    return jnp.where(idx == row, -hi, jnp.where(idx == row + 1, -lo, slab))


def _row_spec(tm, n):
    return pl.BlockSpec((tm, n), lambda i: (i, 0))


def _col_spec(n, tm):
    return pl.BlockSpec((n, tm), lambda i: (0, i))


def _full_spec(shape):
    nd = len(shape)
    return pl.BlockSpec(shape, lambda i: (0,) * nd, pipeline_mode=pl.Buffered(1))


def _acc_rows(ref, val):
    @pl.when(pl.program_id(0) == 0)
    def _():
        ref[...] = jnp.zeros_like(ref)
    ref[...] += jnp.sum(val, axis=0, keepdims=True)


def _rope_tables(pos_col, freq_row, early):
    t = pos_col.shape[0]
    tm = _token_tile(t)
    n = len(early)

    def body(pos_ref, f_ref, *rest):
        shard_refs, (c_ref, s1_ref, s2_ref, ct_ref, s1t_ref, s2t_ref) = rest[:n], rest[n:n + 6]
        start, finish = _two_level_gather(shard_refs, rest[n + 6:2 * n + 6], *rest[2 * n + 6:])
        pl.when(pl.program_id(0) == 0)(start)
        ang = pos_ref[...].astype(F32) * f_ref[...]
        lane = lax.broadcasted_iota(jnp.int32, ang.shape, 1)
        s = jnp.sin(ang)
        c = jnp.cos(ang)
        s1 = jnp.where((lane >= 64) & (lane < 80), -s, 0.0)
        s2 = jnp.where((lane >= 80) & (lane < 96), s, 0.0)
        c_ref[...], s1_ref[...], s2_ref[...] = c, s1, s2
        ct_ref[...], s1t_ref[...], s2t_ref[...] = c.T, s1.T, s2.T
        pl.when(pl.program_id(0) == t // tm - 1)(finish)

    tab = jax.ShapeDtypeStruct((t, SLAB), F32)
    tabt = jax.ShapeDtypeStruct((SLAB, t), F32)
    outs = pl.pallas_call(
        body, name="rope_tables", grid=(t // tm,),
        in_specs=[_row_spec(tm, 1), _full_spec((1, SLAB))] + [ANY_SPEC] * n,
        out_specs=[_row_spec(tm, SLAB)] * 3 + [_col_spec(SLAB, tm)] * 3 + [ANY_SPEC] * n,
        out_shape=[tab] * 3 + [tabt] * 3 + [jax.ShapeDtypeStruct((N_DEV,) + a.shape, a.dtype) for a in early],
        scratch_shapes=_exchange_scratch(n),
        compiler_params=_params(("arbitrary",)),
    )(pos_col, freq_row, *early)
    return outs[:6], outs[6:]


def _inproj_fwd(x, g1, w_in, g_q, g_kv, w_kvb, w_qb_t, w_kb_t, w_vb_t, w_kr_t, tables):
    t = x.shape[0]
    tm = _token_tile(t)

    def body(x_ref, g1_ref, win_ref, gq_ref, gkv_ref, wkvb_ref, wqbt_ref, wkbt_ref, wvbt_ref, wkrt_ref,
             c_ref, s1_ref, s2_ref, ct_ref, s1t_ref, s2t_ref,
             h_ref, gates_ref, qa_ref, ka_ref, va_ref, cq_ref, ckv_ref, cqn_ref, ckvn_ref,
             kb_ref, vb_ref, qt_ref, kt_ref, vt_ref):
        xv = x_ref[...]
        h = (xv * _rms_r(xv) * g1_ref[...]).astype(BF16)
        h_ref[...] = h
        proj = _dot(h, win_ref[...])
        gates_ref[...] = proj[:, C_GATES:C_QA].astype(BF16)
        qa_ref[...] = proj[:, C_QA:C_KA].astype(BF16)
        ka_ref[...] = proj[:, C_KA:C_VA].astype(BF16)
        va_ref[...] = proj[:, C_VA:C_CQ].astype(BF16)
        cq = proj[:, C_CQ:C_CKV]
        ckv = proj[:, C_CKV:C_KR]
        kr = proj[:, C_KR:D_IN_PAD]
        cq_ref[...] = cq
        ckv_ref[...] = ckv
        cqn = (cq * _rms_r(cq) * gq_ref[...]).astype(BF16)
        ckvn = (ckv * _rms_r(ckv) * gkv_ref[...]).astype(BF16)
        cqn_ref[...] = cqn
        ckvn_ref[...] = ckvn
        c, s1, s2 = c_ref[...], s1_ref[...], s2_ref[...]
        kvb = _dot(ckvn, wkvb_ref[...])
        kr_rot = _rope(kr, c, s1, s2)
        ct, s1t, s2t = ct_ref[...], s1t_ref[...], s2t_ref[...]
        q_t = _dot_nt(wqbt_ref[...], cqn)
        k_t = _dot_nt(wkbt_ref[...], ckvn)
        kr_t = _rope_t(_dot_nt(wkrt_ref[...], h), ct, s1t, s2t)
        k_lane = lax.broadcasted_iota(jnp.int32, (1, SLAB), 1)
        k_ones = jnp.where((k_lane == Q_HEAD_B) | (k_lane == Q_HEAD_B + 1), 1.0, 0.0)
        for hd in range(N_HEADS):
            sl = slice(hd * SLAB, (hd + 1) * SLAB)
            kb_ref[:, sl] = (kvb[:, sl] + kr_rot + k_ones).astype(BF16)
            qt_ref[sl, :] = (_rope_t(q_t[sl, :], ct, s1t, s2t) * SCORE_B).astype(BF16)
            kt_ref[sl, :] = (k_t[sl, :] + kr_t).astype(BF16)
        v_lane = lax.broadcasted_iota(jnp.int32, (1, HM), 1) & (SLAB - 1)
        v_ones = jnp.where((v_lane == V_DIM_B) | (v_lane == V_DIM_B + 1), 1.0, 0.0)
        vb_ref[...] = (kvb[:, HM:2 * HM] + v_ones).astype(BF16)
        pad_row = lax.broadcasted_iota(jnp.int32, (HM, 1), 0) & (SLAB - 1)
        ones_rows = jnp.where((pad_row >= V_DIM_B) & (pad_row < V_DIM_B + ONES_ROWS), 1.0, 0.0)
        vt_ref[...] = (_dot_nt(wvbt_ref[...], ckvn) + ones_rows).astype(BF16)

    def sds(n, dt):
        return jax.ShapeDtypeStruct((t, n), dt)

    outs = [(D_MODEL, BF16), (2 * D_MODEL, BF16), (HM, BF16), (N_KV_A * SLAB, BF16), (N_KV_A * SLAB, BF16),
            (Q_LORA, F32), (KV_LORA, F32), (Q_LORA, BF16), (KV_LORA, BF16), (HM, BF16), (HM, BF16)]
    tab, tabt = _row_spec(tm, SLAB), _col_spec(SLAB, tm)
    return pl.pallas_call(
        body, name="inproj_fwd", grid=(t // tm,),
        in_specs=[_row_spec(tm, D_MODEL), _full_spec((1, D_MODEL)), _full_spec((D_MODEL, D_IN_PAD)),
                  _full_spec((1, Q_LORA)), _full_spec((1, KV_LORA)), _full_spec((KV_LORA, 2 * HM)),
                  _full_spec((HM, Q_LORA)), _full_spec((HM, KV_LORA)), _full_spec((HM, KV_LORA)),
                  _full_spec((SLAB, D_MODEL)), tab, tab, tab, tabt, tabt, tabt],
        out_specs=[_row_spec(tm, n) for n, _ in outs] + [_col_spec(HM, tm)] * 3,
        out_shape=[sds(n, dt) for n, dt in outs] + [jax.ShapeDtypeStruct((HM, t), BF16)] * 3,
        compiler_params=_params(("parallel",)),
    )(x, g1, w_in, g_q, g_kv, w_kvb, w_qb_t, w_kb_t, w_vb_t, w_kr_t, *tables)


def _tile_group(a):
    return jnp.concatenate([a] * GROUP_A, axis=1)


def _swa_masks():
    row = lax.broadcasted_iota(jnp.int32, (BLOCK, GROUP_A * BLOCK), 0)
    col = lax.broadcasted_iota(jnp.int32, (BLOCK, GROUP_A * BLOCK), 1) & (BLOCK - 1)
    return row <= col, row > col


def _heads_beside(ref, g):
    return jnp.concatenate([ref[:, (g * GROUP_A + hh) * SLAB:(g * GROUP_A + hh + 1) * SLAB].T
                            for hh in range(GROUP_A)], axis=1)


def _rows_beside(ref, g):
    return jnp.concatenate([ref[g * GROUP_A + hh] for hh in range(GROUP_A)], axis=1)


def _swa_rows(sinks):
    slopes = jnp.repeat(jnp.asarray(SLOPES_A, F32).reshape(N_KV_A, GROUP_A, 1), BLOCK, axis=2)
    sink_rows = jnp.repeat(sinks.reshape(N_KV_A, GROUP_A, 1), BLOCK, axis=2)
    return slopes.reshape(N_KV_A, 1, GROUP_A * BLOCK), sink_rows.reshape(N_KV_A, 1, GROUP_A * BLOCK)


def _swa_fwd(qa, ka, va, pos_col, pos_row, sinks):
    t = qa.shape[0]
    nb = t // BLOCK
    gw = GROUP_A * BLOCK
    slope_rows, sink_rows = _swa_rows(sinks)

    def body(q_ref, kc_ref, kp_ref, vc_ref, vp_ref, pkc_ref, pkp_ref, pq_ref, slope_ref, sink_ref, o_ref, l_ref):
        i = pl.program_id(0)
        pq = pq_ref[...]
        dist_c = _tile_group(jnp.abs(pkc_ref[...] - pq).astype(F32))
        dist_p = _tile_group(jnp.abs(pkp_ref[...] - pq).astype(F32))
        mask_c, older = _swa_masks()
        mask_p = jnp.logical_and(older, i > 0)
        raw = []
        for g in range(N_KV_A):
            gs = slice(g * SLAB, (g + 1) * SLAB)
            x = _heads_beside(q_ref, g)
            raw.append((_dot(kc_ref[:, gs], x), _dot(kp_ref[:, gs], x)))
        for g in range(N_KV_A):
            gs = slice(g * SLAB, (g + 1) * SLAB)
            slope, sink = slope_ref[g], sink_ref[g]
            s_c = jnp.where(mask_c, raw[g][0] * SCALE_A - slope * dist_c, NEG)
            s_p = jnp.where(mask_p, raw[g][1] * SCALE_A - slope * dist_p, NEG)
            m = jnp.maximum(jnp.maximum(jnp.max(s_c, axis=0, keepdims=True),
                                        jnp.max(s_p, axis=0, keepdims=True)), sink)
            e_c = jnp.exp(s_c - m)
            e_p = jnp.exp(s_p - m)
            den = jnp.sum(e_c, axis=0, keepdims=True) + jnp.sum(e_p, axis=0, keepdims=True) + jnp.exp(sink - m)
            inv = 1.0 / den
            ot = (_dot_tn(vc_ref[:, gs], (e_c * inv).astype(BF16))
                  + _dot_tn(vp_ref[:, gs], (e_p * inv).astype(BF16)))
            lse = m + jnp.log(den)
            for hh in range(GROUP_A):
                hd = g * GROUP_A + hh
                seg = slice(hh * BLOCK, (hh + 1) * BLOCK)
                o_ref[:, hd * SLAB:(hd + 1) * SLAB] = ot[:, seg].T.astype(BF16)
                l_ref[hd] = lse[:, seg]

    cur = lambda i: (i, 0)
    prev = lambda i: (jnp.maximum(i - 1, 0), 0)
    kvw = N_KV_A * SLAB
    rows = pl.BlockSpec((N_KV_A, 1, gw), lambda i: (0, 0, 0))
    return pl.pallas_call(
        body, name="swa_fwd", grid=(nb,),
        in_specs=[pl.BlockSpec((BLOCK, HM), cur),
                  pl.BlockSpec((BLOCK, kvw), cur), pl.BlockSpec((BLOCK, kvw), prev),
                  pl.BlockSpec((BLOCK, kvw), cur), pl.BlockSpec((BLOCK, kvw), prev),
                  pl.BlockSpec((BLOCK, 1), cur), pl.BlockSpec((BLOCK, 1), prev),
                  pl.BlockSpec((1, BLOCK), lambda i: (0, i)), rows, rows],
        out_specs=[pl.BlockSpec((BLOCK, HM), cur), pl.BlockSpec((N_HEADS, 1, BLOCK), lambda i: (0, 0, i))],
        out_shape=[jax.ShapeDtypeStruct((t, HM), BF16), jax.ShapeDtypeStruct((N_HEADS, 1, t), F32)],
        compiler_params=_params(("parallel",)),
    )(qa, ka, ka, va, va, pos_col, pos_col, pos_row, slope_rows, sink_rows)


def _swa_bwd(qa, ka, va, out_a, d_oa, lse, pos_col, pos_row, sinks):
    t = qa.shape[0]
    nb = t // BLOCK
    gw = GROUP_A * BLOCK
    kvw = N_KV_A * SLAB
    slope_rows, sink_rows = _swa_rows(sinks)

    def body(q_ref, qn_ref, do_ref, don_ref, l_ref, ln_ref, o_ref, on_ref, kp_ref, kc_ref, vp_ref, vc_ref,
             pkp_ref, pkc_ref, pq_ref, pqn_ref, slope_ref, sink_ref, dqkv_ref, dsink_ref):
        j = pl.program_id(0)
        pkc, pkp = pkc_ref[...], pkp_ref[...]
        dist_cc = _tile_group(jnp.abs(pkc - pq_ref[...]).astype(F32))
        dist_cp = _tile_group(jnp.abs(pkp - pq_ref[...]).astype(F32))
        dist_nc = _tile_group(jnp.abs(pkc - pqn_ref[...]).astype(F32))
        mask_cc, older = _swa_masks()
        mask_cp = jnp.logical_and(older, j > 0)
        mask_nc = jnp.logical_and(older, j < nb - 1)

        @pl.when(j == 0)
        def _():
            dsink_ref[...] = jnp.zeros_like(dsink_ref)

        def tile(k, v, x, dox, lrow, drow, dist, mask, slope):
            s = jnp.where(mask, _dot(k, x) * SCALE_A - slope * dist, NEG)
            p = jnp.exp(s - lrow)
            ds = p * (_dot(v, dox) - drow)
            return p.astype(BF16), ds.astype(BF16)

        for g in range(N_KV_A):
            gs = slice(g * SLAB, (g + 1) * SLAB)
            kc, kp, vc, vp = kc_ref[:, gs], kp_ref[:, gs], vc_ref[:, gs], vp_ref[:, gs]
            slope, sink = slope_ref[g], sink_ref[g]
            x, xn = _heads_beside(q_ref, g), _heads_beside(qn_ref, g)
            dox, doxn = _heads_beside(do_ref, g), _heads_beside(don_ref, g)
            lrow, lrown = _rows_beside(l_ref, g), _rows_beside(ln_ref, g)
            drow = jnp.sum(dox.astype(F32) * _heads_beside(o_ref, g).astype(F32), axis=0, keepdims=True)
            drown = jnp.sum(doxn.astype(F32) * _heads_beside(on_ref, g).astype(F32), axis=0, keepdims=True)
            p_cc, ds_cc = tile(kc, vc, x, dox, lrow, drow, dist_cc, mask_cc, slope)
            _, ds_cp = tile(kp, vp, x, dox, lrow, drow, dist_cp, mask_cp, slope)
            p_nc, ds_nc = tile(kc, vc, xn, doxn, lrown, drown, dist_nc, mask_nc, slope)
            dqt = (_dot_tn(kc, ds_cc) + _dot_tn(kp, ds_cp)) * SCALE_A
            for hh in range(GROUP_A):
                hd = g * GROUP_A + hh
                dqkv_ref[:, hd * SLAB:(hd + 1) * SLAB] = dqt[:, hh * BLOCK:(hh + 1) * BLOCK].T.astype(BF16)
            dqkv_ref[:, HM + g * SLAB:HM + (g + 1) * SLAB] = (
                (_dot_nt(ds_cc, x) + _dot_nt(ds_nc, xn)) * SCALE_A).astype(BF16)
            dqkv_ref[:, HM + kvw + g * SLAB:HM + kvw + (g + 1) * SLAB] = (
                _dot_nt(p_cc, dox) + _dot_nt(p_nc, doxn)).astype(BF16)
            dsink_ref[g] -= jnp.exp(sink - lrow) * drow

    cur = lambda j: (j, 0)
    prev = lambda j: (jnp.maximum(j - 1, 0), 0)
    nxt = lambda j: (jnp.minimum(j + 1, nb - 1), 0)
    cur3 = lambda j: (0, 0, j)
    nxt3 = lambda j: (0, 0, jnp.minimum(j + 1, nb - 1))
    kvw = N_KV_A * SLAB
    rows = pl.BlockSpec((N_KV_A, 1, gw), lambda j: (0, 0, 0))
    stat = lambda im: pl.BlockSpec((N_HEADS, 1, BLOCK), im)
    return pl.pallas_call(
        body, name="swa_bwd", grid=(nb,),
        in_specs=[pl.BlockSpec((BLOCK, HM), cur), pl.BlockSpec((BLOCK, HM), nxt),
                  pl.BlockSpec((BLOCK, HM), cur), pl.BlockSpec((BLOCK, HM), nxt),
                  stat(cur3), stat(nxt3), pl.BlockSpec((BLOCK, HM), cur), pl.BlockSpec((BLOCK, HM), nxt),
                  pl.BlockSpec((BLOCK, kvw), prev), pl.BlockSpec((BLOCK, kvw), cur),
                  pl.BlockSpec((BLOCK, kvw), prev), pl.BlockSpec((BLOCK, kvw), cur),
                  pl.BlockSpec((BLOCK, 1), prev), pl.BlockSpec((BLOCK, 1), cur),
                  pl.BlockSpec((1, BLOCK), lambda j: (0, j)),
                  pl.BlockSpec((1, BLOCK), lambda j: (0, jnp.minimum(j + 1, nb - 1))), rows, rows],
        out_specs=[pl.BlockSpec((BLOCK, HM + 2 * kvw), cur), rows],
        out_shape=[jax.ShapeDtypeStruct((t, HM + 2 * kvw), BF16), jax.ShapeDtypeStruct((N_KV_A, 1, gw), F32)],
        compiler_params=_params(("arbitrary",)),
    )(qa, qa, d_oa, d_oa, lse, lse, out_a, out_a, ka, ka, va, va,
      pos_col, pos_col, pos_row, pos_row, slope_rows, sink_rows)


def _mesh_pos():
    return lax.axis_index("x"), lax.axis_index("y"), lax.axis_index("c")


def _flip(v, bit):
    return 1 - v if bit else v


def _direct_copies(srcs, dsts, send_sems, recv_sems, local_sems, gather, sem_base=0, only=None):
    x, y, c = _mesh_pos()
    me = 4 * x + 2 * y + c

    def among(idx, dests):
        ok = idx == dests[0]
        for d in dests[1:]:
            ok = jnp.logical_or(ok, idx == d)
        return ok

    local, remote = [], []
    for a, (src, dst) in enumerate(zip(srcs, dsts)):
        dests = None if only is None else only[a]
        recv_ok = None if dests is None else among(me, dests)
        local.append((pltpu.make_async_copy(src if gather else src.at[me], dst.at[me],
                                            local_sems.at[sem_base + a]), recv_ok))
        for r in range(1, N_DEV):
            px, py, pc = _flip(x, r & 4), _flip(y, r & 2), _flip(c, r & 1)
            peer = 4 * px + 2 * py + pc
            sem = (N_DEV - 1) * (sem_base + a) + r - 1
            copy = pltpu.make_async_remote_copy(
                src_ref=src if gather else src.at[peer], dst_ref=dst.at[me],
                send_sem=send_sems.at[sem], recv_sem=recv_sems.at[sem],
                device_id=(px, py, pc), device_id_type=pl.DeviceIdType.MESH)
            remote.append((copy, None if dests is None else among(peer, dests), recv_ok))
    return local, remote


def _when(cond, fn):
    if cond is None:
        fn()
    else:
        pl.when(cond)(fn)


def _start_copies(local, remote):
    for cp, ok in local:
        _when(ok, cp.start)
    for cp, send_ok, _ in remote:
        _when(send_ok, cp.start)


def _wait_copies(local, remote):
    for cp, _, recv_ok in remote:
        _when(recv_ok, cp.wait_recv)
    for cp, send_ok, _ in remote:
        _when(send_ok, cp.wait_send)
    for cp, ok in local:
        _when(ok, cp.wait)


def _exchange_scratch(n):
    return [pltpu.SemaphoreType.DMA((n * (N_DEV - 1),)), pltpu.SemaphoreType.DMA((n * (N_DEV - 1),)),
            pltpu.SemaphoreType.DMA((n,))]


ANY_SPEC = pl.BlockSpec(memory_space=pl.ANY)


def _mla_fwd(qt, kb, vt, late):
    t = kb.shape[0]
    tk = _attn_tile(t)
    ratio = 2 if t >= 2 * tk else 1
    tq = ratio * tk
    nq = t // tq
    hps = MLA_FWD_HEADS_PER_STEP
    w = hps * SLAB
    pairs = [(i, j) for i in range(nq) for j in range(ratio * (i + 1))]
    i_tab = jnp.asarray(np.array([p[0] for p in pairs], np.int32))
    j_tab = jnp.asarray(np.array([p[1] for p in pairs], np.int32))

    n_late = len(late)

    def body(it_ref, jt_ref, qt_ref, k_ref, vt_ref, *rest):
        late_refs, (o_ref, ot_ref, qa_ref) = rest[:n_late], rest[n_late:n_late + 3]
        gathered_refs = rest[n_late + 3:2 * n_late + 3]
        m_s, acc_s, send_sems, recv_sems, local_sems = rest[2 * n_late + 3:]
        n = pl.program_id(1)
        i, j = it_ref[n], jt_ref[n]
        first_step = jnp.logical_and(pl.program_id(0) == 0, n == 0)
        last_step = jnp.logical_and(pl.program_id(0) == N_HEADS // hps - 1, n == len(pairs) - 1)

        @pl.when(first_step)
        def _():
            _start_copies(*_direct_copies(late_refs, gathered_refs, send_sems, recv_sems, local_sems, True))

        @pl.when(j == 0)
        def _():
            m_s[...] = jnp.full_like(m_s, NEG)
            acc_s[...] = jnp.zeros_like(acc_s)

        def update(masked, q0):
            qc = slice(q0, tq)

            def scores(hh):
                sl = slice(hh * SLAB, (hh + 1) * SLAB)
                return _dot(k_ref[:, sl], qt_ref[sl, qc])

            def softmax(hh, s):
                if masked:
                    s = jnp.where(lax.broadcasted_iota(jnp.int32, s.shape, 0)
                                  <= lax.broadcasted_iota(jnp.int32, s.shape, 1), s, NEG)
                m_old = m_s[hh][:, qc]
                m_new = jnp.maximum(m_old, jnp.max(s, axis=0, keepdims=True))
                m_s[hh, :, qc] = m_new
                return jnp.exp2(s - m_new).astype(BF16), jnp.exp2(m_old - m_new)

            def accumulate(hh, p, alpha):
                sl = slice(hh * SLAB, hh * SLAB + V_DIM_B + ONES_ROWS)
                acc_s[sl, qc] = alpha * acc_s[sl, qc] + _dot(vt_ref[sl, :], p)

            s_next, pending = scores(0), None
            for hh in range(hps):
                s = s_next
                if hh + 1 < hps:
                    s_next = scores(hh + 1)
                p, alpha = softmax(hh, s)
                if pending is not None:
                    accumulate(*pending)
                pending = (hh, p, alpha)
            accumulate(*pending)

        @pl.when(j < ratio * i)
        def _():
            update(False, 0)

        for part in range(ratio):
            @pl.when(j == ratio * i + part)
            def _():
                update(True, part * tk)

        @pl.when(j == ratio * i + ratio - 1)
        def _():
            for hh in range(hps):
                sl = slice(hh * SLAB, (hh + 1) * SLAB)
                den = acc_s[hh * SLAB + V_DIM_B:hh * SLAB + V_DIM_B + 1, :]
                values = lax.broadcasted_iota(jnp.int32, (SLAB, tq), 0) < V_DIM_B
                ot = jnp.where(values, acc_s[sl, :] / den, 0.0)
                ot_ref[sl, :] = ot.astype(BF16)
                o_ref[:, sl] = ot.T.astype(BF16)
                lse = m_s[hh] + jnp.log2(den)
                qa_ref[sl, :] = _plant_rows(qt_ref[sl, :].astype(F32), Q_HEAD_B, lse).astype(BF16)

        @pl.when(last_step)
        def _():
            _wait_copies(*_direct_copies(late_refs, gathered_refs, send_sems, recv_sems, local_sems, True))

    grid_spec = pltpu.PrefetchScalarGridSpec(
        num_scalar_prefetch=2, grid=(N_HEADS // hps, len(pairs)),
        in_specs=[pl.BlockSpec((w, tq), lambda h, n, it, jt: (h, it[n])),
                  pl.BlockSpec((tk, w), lambda h, n, it, jt: (jt[n], h)),
                  pl.BlockSpec((w, tk), lambda h, n, it, jt: (h, jt[n]))] + [ANY_SPEC] * n_late,
        out_specs=[pl.BlockSpec((tq, w), lambda h, n, it, jt: (it[n], h)),
                   pl.BlockSpec((w, tq), lambda h, n, it, jt: (h, it[n])),
                   pl.BlockSpec((w, tq), lambda h, n, it, jt: (h, it[n]))] + [ANY_SPEC] * n_late,
        scratch_shapes=[pltpu.VMEM((hps, 1, tq), F32), pltpu.VMEM((w, tq), F32)] + _exchange_scratch(n_late))
    outs = pl.pallas_call(
        body, name="mla_fwd", grid_spec=grid_spec,
        out_shape=[jax.ShapeDtypeStruct((t, HM), BF16), jax.ShapeDtypeStruct((HM, t), BF16),
                   jax.ShapeDtypeStruct((HM, t), BF16)]
        + [jax.ShapeDtypeStruct((N_DEV,) + a.shape, a.dtype) for a in late],
        compiler_params=_params(("arbitrary", "arbitrary")),
    )(i_tab, j_tab, qt, kb, vt, *late)
    return outs[0], outs[1], outs[2], list(outs[3:])


def _mla_bwd(qt, kb, kt, vb, d_ob_t, grad_slices):
    t = kb.shape[0]
    tk = _attn_tile(t)
    ratio = 2 if t >= 2 * tk else 1
    tq = ratio * tk
    nk, nq = t // tk, t // tq
    hps = MLA_HEADS_PER_STEP
    w = hps * SLAB
    pairs = [(j, i) for j in range(nk) for i in range(j // ratio, nq)]
    j_tab = jnp.asarray(np.array([p[0] for p in pairs], np.int32))
    i_tab = jnp.asarray(np.array([p[1] for p in pairs], np.int32))

    n_ex = len(grad_slices)

    def body(jt_ref, it_ref, qt_ref, dot_ref, k_ref, kt_ref, v_ref, *rest):
        slice_refs, (dqt_ref, dkt_ref, dvt_ref) = rest[:n_ex], rest[n_ex:n_ex + 3]
        part_refs = rest[n_ex + 3:2 * n_ex + 3]
        dk_s, dv_s, send_sems, recv_sems, local_sems = rest[2 * n_ex + 3:]
        n = pl.program_id(1)
        j, i = jt_ref[n], it_ref[n]
        first_step = jnp.logical_and(pl.program_id(0) == 0, n == 0)
        last_step = jnp.logical_and(pl.program_id(0) == N_HEADS // hps - 1, n == len(pairs) - 1)

        @pl.when(first_step)
        def _():
            _start_copies(*_direct_copies(slice_refs, part_refs, send_sems, recv_sems, local_sems, False))

        @pl.when(n == 0)
        def _():
            dqt_ref[...] = jnp.zeros_like(dqt_ref)

        def update(diagonal, q0):
            qc = slice(q0, tq)
            cols = pl.ds(pl.multiple_of(i * tq + q0, tk), tq - q0)

            def softmax_bwd(hh, s, dp):
                if diagonal:
                    s = jnp.where(lax.broadcasted_iota(jnp.int32, s.shape, 0)
                                  <= lax.broadcasted_iota(jnp.int32, s.shape, 1), s, NEG)
                p = jnp.exp2(s)
                return p.astype(BF16), (p * dp).astype(BF16)

            def gradients(hh, p, ds):
                base = hh * SLAB
                vrows = slice(base, base + V_DIM_B)
                qrows = slice(base, base + QK_NOPE + QK_ROPE)
                dv = _dot_nt(dot_ref[vrows, qc], p)
                dk = _dot_nt(qt_ref[qrows, qc], ds)
                if diagonal:
                    dv_s[base:base + SLAB, :] = jnp.concatenate([dv, jnp.zeros((SLAB - V_DIM_B, tk), F32)], axis=0)
                    dk_s[base:base + SLAB, :] = jnp.concatenate(
                        [dk, jnp.zeros((SLAB - QK_NOPE - QK_ROPE, tk), F32)], axis=0)
                else:
                    dv_s[vrows, :] += dv
                    dk_s[qrows, :] += dk
                dqt_ref[qrows, cols] += _dot(kt_ref[qrows, :], ds)

            def scores(hh):
                sl = slice(hh * SLAB, (hh + 1) * SLAB)
                return _dot(k_ref[:, sl], qt_ref[sl, qc])

            def dprod(hh):
                sl = slice(hh * SLAB, (hh + 1) * SLAB)
                return _dot(v_ref[:, sl], dot_ref[sl, qc])

            s_next = scores(0)
            for hh in range(hps):
                s = s_next
                dp = dprod(hh)
                if hh + 1 < hps:
                    s_next = scores(hh + 1)
                gradients(hh, *softmax_bwd(hh, s, dp))

        first_tile = lax.div(j, ratio)
        for part in range(ratio):
            @pl.when(jnp.logical_and(i == first_tile, lax.rem(j, ratio) == part))
            def _():
                update(True, part * tk)

        @pl.when(i > first_tile)
        def _():
            update(False, 0)

        @pl.when(i == nq - 1)
        def _():
            dkt_ref[...] = (dk_s[...] * (1.0 / LOG2E)).astype(BF16)
            dvt_ref[...] = dv_s[...].astype(BF16)

        @pl.when(last_step)
        def _():
            _wait_copies(*_direct_copies(slice_refs, part_refs, send_sems, recv_sems, local_sems, False))

    grid_spec = pltpu.PrefetchScalarGridSpec(
        num_scalar_prefetch=2, grid=(N_HEADS // hps, len(pairs)),
        in_specs=[pl.BlockSpec((w, tq), lambda h, n, jt, it: (h, it[n])),
                  pl.BlockSpec((w, tq), lambda h, n, jt, it: (h, it[n])),
                  pl.BlockSpec((tk, w), lambda h, n, jt, it: (jt[n], h)),
                  pl.BlockSpec((w, tk), lambda h, n, jt, it: (h, jt[n])),
                  pl.BlockSpec((tk, w), lambda h, n, jt, it: (jt[n], h))] + [ANY_SPEC] * n_ex,
        out_specs=[pl.BlockSpec((w, t), lambda h, n, jt, it: (h, 0)),
                   pl.BlockSpec((w, tk), lambda h, n, jt, it: (h, jt[n])),
                   pl.BlockSpec((w, tk), lambda h, n, jt, it: (h, jt[n]))] + [ANY_SPEC] * n_ex,
        scratch_shapes=[pltpu.VMEM((w, tk), F32), pltpu.VMEM((w, tk), F32)] + _exchange_scratch(n_ex))
    outs = pl.pallas_call(
        body, name="mla_bwd", grid_spec=grid_spec,
        out_shape=[jax.ShapeDtypeStruct((HM, t), F32), jax.ShapeDtypeStruct((HM, t), BF16),
                   jax.ShapeDtypeStruct((HM, t), BF16)]
        + [jax.ShapeDtypeStruct(a.shape, a.dtype) for a in grad_slices],
        compiler_params=_params(("arbitrary", "arbitrary")),
    )(j_tab, i_tab, qt, d_ob_t, kb, kt, vb, *grad_slices)
    return outs[0], outs[1], outs[2], list(outs[3:])


def _merge_fwd(out_a, out_b, gates, x, w_oa, w_ob, w_out, g2, g3):
    t = x.shape[0]
    tm = _token_tile(t)

    def body(oa_ref, ob_ref, gates_ref, x_ref, woa_ref, wob_ref, wout_ref, g2_ref, g3_ref,
             oap_ref, obp_ref, merged_ref, y_ref, x1_ref, h2_ref):
        oa_p = _dot(oa_ref[...], woa_ref[...])
        ob_p = _dot(ob_ref[...], wob_ref[...])
        oap_ref[...] = oa_p.astype(BF16)
        obp_ref[...] = ob_p.astype(BF16)
        sa = _sigmoid(gates_ref[:, 0:D_MODEL].astype(F32))
        sb = _sigmoid(gates_ref[:, D_MODEL:2 * D_MODEL].astype(F32))
        merged = (sa * oa_p + sb * ob_p).astype(BF16)
        merged_ref[...] = merged
        y = _dot(merged, wout_ref[...])
        y_ref[...] = y
        x1 = x_ref[...] + y * _rms_r(y) * g2_ref[...]
        x1_ref[...] = x1
        h2_ref[...] = (x1 * _rms_r(x1) * g3_ref[...]).astype(BF16)

    def sds(dt):
        return jax.ShapeDtypeStruct((t, D_MODEL), dt)

    row = _row_spec(tm, D_MODEL)
    return pl.pallas_call(
        body, name="merge_fwd", grid=(t // tm,),
        in_specs=[_row_spec(tm, HM), _row_spec(tm, HM), _row_spec(tm, 2 * D_MODEL), row,
                  _full_spec((HM, D_MODEL)), _full_spec((HM, D_MODEL)), _full_spec((D_MODEL, D_MODEL)),
                  _full_spec((1, D_MODEL)), _full_spec((1, D_MODEL))],
        out_specs=[row] * 6,
        out_shape=[sds(BF16), sds(BF16), sds(BF16), sds(F32), sds(F32), sds(BF16)],
        compiler_params=_params(("parallel",)),
    )(out_a, out_b, gates, x, w_oa, w_ob, w_out, g2, g3)


def _merge_bwd(dx1, y, gates, oa_p, ob_p, out_a, out_b, out_b_t, merged, w_oa, w_ob, w_out, g2):
    t = dx1.shape[0]
    tm = _token_tile(t)

    def body(dx1_ref, y_ref, gates_ref, oap_ref, obp_ref, oa_ref, ob_ref, obt_ref, merged_ref,
             woa_ref, wob_ref, wout_ref, g2_ref,
             dgates_ref, doa_ref, dobt_ref, dg2_ref, dwoa_ref, dwob_ref, dwout_ref):
        @pl.when(pl.program_id(0) == 0)
        def _():
            dwoa_ref[...] = jnp.zeros_like(dwoa_ref)
            dwob_ref[...] = jnp.zeros_like(dwob_ref)
            dwout_ref[...] = jnp.zeros_like(dwout_ref)

        dx1v = dx1_ref[...]
        yv = y_ref[...]
        r2 = _rms_r(yv)
        _acc_rows(dg2_ref, dx1v * yv * r2)
        dy = _rms_bwd(yv, r2, g2_ref[...], dx1v).astype(BF16)
        dwout_ref[...] += _dot_tn(merged_ref[...], dy)
        dm = _dot_nt(dy, wout_ref[...])
        sa = _sigmoid(gates_ref[:, 0:D_MODEL].astype(F32))
        sb = _sigmoid(gates_ref[:, D_MODEL:2 * D_MODEL].astype(F32))
        d_oap = (dm * sa).astype(BF16)
        d_obp = (dm * sb).astype(BF16)
        dwoa_ref[...] += _dot_tn(oa_ref[...], d_oap)
        dwob_ref[...] += _dot_tn(ob_ref[...], d_obp)
        dgates_ref[:, 0:D_MODEL] = (dm * oap_ref[...].astype(F32) * sa * (1.0 - sa)).astype(BF16)
        dgates_ref[:, D_MODEL:2 * D_MODEL] = (dm * obp_ref[...].astype(F32) * sb * (1.0 - sb)).astype(BF16)
        doa_ref[...] = _dot_nt(d_oap, woa_ref[...]).astype(BF16)
        d_ob_t = _dot_nt(wob_ref[...], d_obp)
        for hd in range(N_HEADS):
            sl = slice(hd * SLAB, (hd + 1) * SLAB)
            delta = jnp.sum(d_ob_t[sl, :] * obt_ref[sl, :].astype(F32), axis=0, keepdims=True)
            dobt_ref[sl, :] = _plant_rows(d_ob_t[sl, :], V_DIM_B, delta).astype(BF16)

    def sds(n, dt):
        return jax.ShapeDtypeStruct((t, n), dt)

    row = _row_spec(tm, D_MODEL)
    return pl.pallas_call(
        body, name="merge_bwd", grid=(t // tm,),
        in_specs=[row, row, _row_spec(tm, 2 * D_MODEL), row, row, _row_spec(tm, HM), _row_spec(tm, HM),
                  _col_spec(HM, tm), row,
                  _full_spec((HM, D_MODEL)), _full_spec((HM, D_MODEL)), _full_spec((D_MODEL, D_MODEL)),
                  _full_spec((1, D_MODEL))],
        out_specs=[_row_spec(tm, 2 * D_MODEL), _row_spec(tm, HM), _col_spec(HM, tm), _full_spec((1, D_MODEL)),
                   _full_spec((HM, D_MODEL)), _full_spec((HM, D_MODEL)), _full_spec((D_MODEL, D_MODEL))],
        out_shape=[sds(2 * D_MODEL, BF16), sds(HM, BF16), jax.ShapeDtypeStruct((HM, t), BF16),
                   jax.ShapeDtypeStruct((1, D_MODEL), F32),
                   jax.ShapeDtypeStruct((HM, D_MODEL), F32), jax.ShapeDtypeStruct((HM, D_MODEL), F32),
                   jax.ShapeDtypeStruct((D_MODEL, D_MODEL), F32)],
        compiler_params=_params(("arbitrary",)),
    )(dx1, y, gates, oa_p, ob_p, out_a, out_b, out_b_t, merged, w_oa, w_ob, w_out, g2)


def _mlp_fwd_bwd(x1, h2, target, w_up, w_down, g3, g4):
    t = x1.shape[0]
    tm = _token_tile(t)
    fs = D_FF // N_DEV

    def body(x1_ref, h2_ref, tgt_ref, wup_ref, wdown_ref, g3_ref, g4_ref,
             a_ref, du_ref, dy2_ref, dx1_ref, loss_ref, dg3_ref, dg4_ref):
        x1v = x1_ref[...]
        h2v = h2_ref[...]
        u = jnp.concatenate([_dot(h2v, wup_ref[s]) for s in range(N_DEV)], axis=1)
        ru = jnp.maximum(u, 0.0)
        a = (ru * ru).astype(BF16)
        a_ref[...] = a
        y2 = _dot(a, wdown_ref[...])
        r4 = _rms_r(y2)
        diff = x1v + y2 * r4 * g4_ref[...] - tgt_ref[...]
        _acc_rows(loss_ref, jnp.sum(diff * diff, axis=-1, keepdims=True) * (0.5 / D_MODEL)
                  * jnp.ones((1, SLAB), F32))
        dx2 = diff * (1.0 / D_MODEL)
        _acc_rows(dg4_ref, dx2 * y2 * r4)
        dy2 = _rms_bwd(y2, r4, g4_ref[...], dx2).astype(BF16)
        dy2_ref[...] = dy2
        du = (_dot_nt(dy2, wdown_ref[...]) * (2.0 * ru)).astype(BF16)
        du_ref[...] = du
        dh2 = _dot_nt(du[:, 0:fs], wup_ref[0])
        for s in range(1, N_DEV):
            dh2 += _dot_nt(du[:, s * fs:(s + 1) * fs], wup_ref[s])
        r3 = _rms_r(x1v)
        _acc_rows(dg3_ref, dh2 * x1v * r3)
        dx1_ref[...] = dx2 + _rms_bwd(x1v, r3, g3_ref[...], dh2)

    row = _row_spec(tm, D_MODEL)
    frow = _row_spec(tm, D_FF)
    vec = _full_spec((1, D_MODEL))
    return pl.pallas_call(
        body, name="mlp_fwd_bwd", grid=(t // tm,),
        in_specs=[row, row, row, _full_spec((N_DEV, D_MODEL, fs)), _full_spec((D_FF, D_MODEL)), vec, vec],
        out_specs=[frow, frow, row, row, _full_spec((1, SLAB)), vec, vec],
        out_shape=[jax.ShapeDtypeStruct((t, D_FF), BF16), jax.ShapeDtypeStruct((t, D_FF), BF16),
                   jax.ShapeDtypeStruct((t, D_MODEL), BF16), jax.ShapeDtypeStruct((t, D_MODEL), F32),
                   jax.ShapeDtypeStruct((1, SLAB), F32), jax.ShapeDtypeStruct((1, D_MODEL), F32),
                   jax.ShapeDtypeStruct((1, D_MODEL), F32)],
        compiler_params=_params(("arbitrary",)),
    )(x1, h2, target, w_up, w_down, g3, g4)


def _latent_bwd(dqb_t, dkb_t, dvb_t, cq, ckv, cqn, ckvn, rope_ct, rope_s1t, rope_s2t, g_q, g_kv, w_qb, w_kvb):
    t = cq.shape[0]
    tm = _wide_token_tile(t)

    def body(dqt_ref, dkt_ref, dvt_ref, cq_ref, ckv_ref, cqn_ref, ckvn_ref, ct_ref, s1t_ref, s2t_ref,
             gq_ref, gkv_ref, wqb_ref, wkvb_ref,
             dlate_ref, dgq_ref, dgkv_ref, dwqb_ref, dwkvb_ref, dqbrt_ref, dkvbt_ref):
        @pl.when(pl.program_id(0) == 0)
        def _():
            dwqb_ref[...] = jnp.zeros_like(dwqb_ref)
            dwkvb_ref[...] = jnp.zeros_like(dwkvb_ref)

        ct, s1t, s2t = ct_ref[...], s1t_ref[...], s2t_ref[...]
        dk_sum_t = jnp.zeros((SLAB, tm), F32)
        for hd in range(N_HEADS):
            sl = slice(hd * SLAB, (hd + 1) * SLAB)
            dqbrt_ref[sl, :] = _rope_t_bwd(dqt_ref[sl, :] * SCALE_B, ct, s1t, s2t).astype(BF16)
            dk_sum_t += dkt_ref[sl, :].astype(F32)
        dkvbt_ref[0:HM, :] = dkt_ref[...]
        dkvbt_ref[HM:2 * HM, :] = dvt_ref[...]
        dkr = _rope_t_bwd(dk_sum_t, ct, s1t, s2t).T
        dwqb_ref[...] += _dot(dqbrt_ref[...], cqn_ref[...])
        dwkvb_ref[...] += _dot(dkvbt_ref[...], ckvn_ref[...])
        dcqn = _dot(wqb_ref[...], dqbrt_ref[...]).T
        cq = cq_ref[...]
        rq = _rms_r(cq)
        _acc_rows(dgq_ref, dcqn * cq * rq)
        dcq = _rms_bwd(cq, rq, gq_ref[...], dcqn)
        dckvn = _dot(wkvb_ref[...], dkvbt_ref[...]).T
        ckv = ckv_ref[...]
        rkv = _rms_r(ckv)
        _acc_rows(dgkv_ref, dckvn * ckv * rkv)
        dckv = _rms_bwd(ckv, rkv, gkv_ref[...], dckvn)
        dlate_ref[:, 0:C_CKV - C_CQ] = dcq.astype(BF16)
        dlate_ref[:, C_CKV - C_CQ:C_KR - C_CQ] = dckv.astype(BF16)
        dlate_ref[:, C_KR - C_CQ:D_IN_PAD - C_CQ] = dkr.astype(BF16)

    hmt = _col_spec(HM, tm)
    tab = _col_spec(SLAB, tm)
    return pl.pallas_call(
        body, name="latent_bwd", grid=(t // tm,),
        in_specs=[hmt, hmt, hmt,
                  _row_spec(tm, Q_LORA), _row_spec(tm, KV_LORA), _row_spec(tm, Q_LORA), _row_spec(tm, KV_LORA),
                  tab, tab, tab, _full_spec((1, Q_LORA)), _full_spec((1, KV_LORA)),
                  _full_spec((Q_LORA, HM)), _full_spec((KV_LORA, 2 * HM))],
        out_specs=[_row_spec(tm, D_IN_PAD - C_CQ), _full_spec((1, Q_LORA)), _full_spec((1, KV_LORA)),
                   _full_spec((HM, Q_LORA)), _full_spec((2 * HM, KV_LORA))],
        out_shape=[jax.ShapeDtypeStruct((t, D_IN_PAD - C_CQ), BF16),
                   jax.ShapeDtypeStruct((1, Q_LORA), F32), jax.ShapeDtypeStruct((1, KV_LORA), F32),
                   jax.ShapeDtypeStruct((HM, Q_LORA), F32), jax.ShapeDtypeStruct((2 * HM, KV_LORA), F32)],
        scratch_shapes=[pltpu.VMEM((HM, tm), BF16), pltpu.VMEM((2 * HM, tm), BF16)],
        compiler_params=_params(("arbitrary",)),
    )(dqb_t, dkb_t, dvb_t, cq, ckv, cqn, ckvn, rope_ct, rope_s1t, rope_s2t, g_q, g_kv, w_qb, w_kvb)


def _inproj_bwd(dgates, dqkv, dlate, x, dx1, g1, w_in, grad_slices, only):
    t = x.shape[0]
    tm = _wide_token_tile(t)
    n_ex = len(grad_slices)
    zeroed = [a for a in range(n_ex) if only[a] is not None]

    def body(dgates_ref, dqkv_ref, dlate_ref, x_ref, dx1_ref, g1_ref, win_ref, *rest):
        slice_refs = rest[:n_ex]
        dx_ref, dg1_ref = rest[n_ex:n_ex + 2]
        part_refs = rest[n_ex + 2:2 * n_ex + 2]
        dproj_ref, send_sems, recv_sems, local_sems = rest[2 * n_ex + 2:2 * n_ex + 6]
        zero_refs, zero_sem = rest[2 * n_ex + 6:-1], rest[-1]

        @pl.when(pl.program_id(0) == 0)
        def _():
            _start_copies(*_direct_copies(slice_refs, part_refs, send_sems, recv_sems, local_sems, False,
                                          only=only))
            x_, y_, c_ = _mesh_pos()
            me = 4 * x_ + 2 * y_ + c_
            for a, z_ref in zip(zeroed, zero_refs):
                outside = me != only[a][0]
                for d in only[a][1:]:
                    outside = jnp.logical_and(outside, me != d)

                @pl.when(outside)
                def _():
                    z_ref[...] = jnp.zeros_like(z_ref)
                    fills = [pltpu.make_async_copy(z_ref, part_refs[a].at[k], zero_sem.at[k])
                             for k in range(N_DEV)]
                    for cp in fills:
                        cp.start()
                    for cp in fills:
                        cp.wait()

        dproj_ref[:, C_GATES:C_QA] = dgates_ref[...]
        dproj_ref[:, C_QA:C_CQ] = dqkv_ref[...]
        dproj_ref[:, C_CQ:D_IN_PAD] = dlate_ref[...]
        dh = _dot_nt(dproj_ref[...], win_ref[...])
        xv = x_ref[...]
        r1 = _rms_r(xv)
        _acc_rows(dg1_ref, dh * xv * r1)
        dx_ref[...] = dx1_ref[...] + _rms_bwd(xv, r1, g1_ref[...], dh)

        @pl.when(pl.program_id(0) == t // tm - 1)
        def _():
            _wait_copies(*_direct_copies(slice_refs, part_refs, send_sems, recv_sems, local_sems, False,
                                         only=only))

    kvw = N_KV_A * SLAB
    row = _row_spec(tm, D_MODEL)
    outs = pl.pallas_call(
        body, name="inproj_bwd", grid=(t // tm,),
        in_specs=[_row_spec(tm, 2 * D_MODEL), _row_spec(tm, HM + 2 * kvw), _row_spec(tm, D_IN_PAD - C_CQ),
                  row, row, _full_spec((1, D_MODEL)), _full_spec((D_MODEL, D_IN_PAD))]
        + [ANY_SPEC] * n_ex,
        out_specs=[row, _full_spec((1, D_MODEL))] + [ANY_SPEC] * n_ex,
        out_shape=[jax.ShapeDtypeStruct((t, D_MODEL), F32), jax.ShapeDtypeStruct((1, D_MODEL), F32)]
        + [jax.ShapeDtypeStruct(a.shape, a.dtype) for a in grad_slices],
        scratch_shapes=[pltpu.VMEM((tm, D_IN_PAD), BF16)] + _exchange_scratch(n_ex)
        + [pltpu.VMEM(grad_slices[a].shape[1:], grad_slices[a].dtype) for a in zeroed]
        + [pltpu.SemaphoreType.DMA((N_DEV,))],
        compiler_params=_params(("arbitrary",)),
    )(dgates, dqkv, dlate, x, dx1, g1, w_in, *grad_slices)
    return outs[0], outs[1], list(outs[2:])


def _matmul_tn(a, b, name, out_dtype=F32, n_shards=1):
    t, k = a.shape
    n = b.shape[1]
    bt = min(t, 512)
    bn = min(n, 2048)
    bk = min(k, 2048 * 1024 // bn)
    ns = n // n_shards
    per_block = bn // ns
    steps = t // bt

    def body(a_ref, b_ref, o_ref, acc):
        s = pl.program_id(2)

        @pl.when(s == 0)
        def _():
            acc[...] = jnp.zeros_like(acc)

        acc[...] += _dot_tn(a_ref[...], b_ref[...])

        @pl.when(s == steps - 1)
        def _():
            if n_shards > 1:
                for p in range(per_block):
                    o_ref[p] = acc[:, p * ns:(p + 1) * ns].astype(out_dtype)
            else:
                o_ref[...] = acc[...].astype(out_dtype)

    if n_shards > 1:
        out_spec = pl.BlockSpec((per_block, bk, ns), lambda i, j, s: (j, i, 0))
        out_shape = jax.ShapeDtypeStruct((n_shards, k, ns), out_dtype)
    else:
        out_spec = pl.BlockSpec((bk, bn), lambda i, j, s: (i, j))
        out_shape = jax.ShapeDtypeStruct((k, n), out_dtype)
    return pl.pallas_call(
        body, name=name, grid=(k // bk, n // bn, steps),
        in_specs=[pl.BlockSpec((bt, bk), lambda i, j, s: (s, i)), pl.BlockSpec((bt, bn), lambda i, j, s: (s, j))],
        out_specs=out_spec, out_shape=out_shape, scratch_shapes=[pltpu.VMEM((bk, bn), F32)],
        compiler_params=_params(("parallel", "parallel", "arbitrary")),
    )(a, b)


def _two_level_gather(srcs, dsts, send_sems, recv_sems, local_sems):
    n = len(srcs)
    x, y, c = _mesh_pos()
    me, sibling = (x, y, c), (x, y, 1 - c)
    chips = [(1 - x, y), (x, 1 - y), (1 - x, 1 - y)]

    def slot(a, px, py, pc):
        return dsts[a].at[4 * px + 2 * py + pc]

    def copy(a, k, block, to, src=None):
        return pltpu.make_async_remote_copy(
            src_ref=slot(a, *block) if src is None else src, dst_ref=slot(a, *block),
            send_sem=send_sems.at[(N_DEV - 1) * a + k], recv_sem=recv_sems.at[(N_DEV - 1) * a + k],
            device_id=to, device_id_type=pl.DeviceIdType.MESH)

    def own_copies():
        mine = [pltpu.make_async_copy(srcs[a], slot(a, *me), local_sems.at[a]) for a in range(n)]
        first = []
        for a in range(n):
            first.append(copy(a, 0, me, sibling, src=srcs[a]))
            first += [copy(a, 1 + j, me, (*chip, c), src=srcs[a]) for j, chip in enumerate(chips)]
        return mine, first

    def start():
        mine, first = own_copies()
        for cp in mine + first:
            cp.start()

    def finish():
        mine, first = own_copies()
        passed = []
        for j, chip in enumerate(chips):
            for a in range(n):
                copy(a, 1 + j, (*chip, c), me).wait_recv()
                passed.append(copy(a, 4 + j, (*chip, c), sibling))
                passed[-1].start()
        for a in range(n):
            copy(a, 0, sibling, me).wait_recv()
        for j, chip in enumerate(chips):
            for a in range(n):
                copy(a, 4 + j, (*chip, 1 - c), me).wait_recv()
        for cp in first + passed:
            cp.wait_send()
        for cp in mine:
            cp.wait()

    return start, finish


def _gather_small(small):
    def body(s_ref, s_dst, *sems):
        smalls = _direct_copies([s_ref], [s_dst], *sems, True)
        _start_copies(*smalls)
        _wait_copies(*smalls)

    return pl.pallas_call(
        body, name="gather_small",
        out_shape=jax.ShapeDtypeStruct((N_DEV,) + small.shape, small.dtype),
        in_specs=[ANY_SPEC], out_specs=ANY_SPEC,
        scratch_shapes=_exchange_scratch(1),
    )(small)


def _adamw(parts, w, m, v, name):
    n_parts = len(parts)
    _, k, n = parts[0].shape
    bk = min(k, ADAM_ROWS)
    c1 = 1.0 - ADAM_B1 ** ADAM_STEP
    c2 = 1.0 - ADAM_B2 ** ADAM_STEP

    def body(*refs):
        p_refs, (w_ref, m_ref, v_ref, g_ref, d_ref, mo_ref, vo_ref) = refs[:n_parts], refs[n_parts:]
        g = p_refs[0][0].astype(F32)
        for p_ref in p_refs:
            for s in range(N_DEV):
                if p_ref is not p_refs[0] or s > 0:
                    g = g + p_ref[s].astype(F32)
        g_ref[0] = g
        m_new = ADAM_B1 * m_ref[0] + (1.0 - ADAM_B1) * g
        v_new = ADAM_B2 * v_ref[0] + (1.0 - ADAM_B2) * (g * g)
        mo_ref[0] = m_new
        vo_ref[0] = v_new
        m_hat = m_new / c1
        v_hat = v_new / c2
        d_ref[0] = -ADAM_LR * (m_hat / (jnp.sqrt(v_hat) + ADAM_EPS) + ADAM_WD * w_ref[0])

    blk = pl.BlockSpec((1, bk, n), lambda i: (0, i, 0))
    out = jax.ShapeDtypeStruct((1, k, n), F32)
    return pl.pallas_call(
        body, name=name, grid=(k // bk,),
        in_specs=[pl.BlockSpec((N_DEV, bk, n), lambda i: (0, i, 0))] * n_parts + [blk, blk, blk],
        out_specs=[blk] * 4, out_shape=[out] * 4,
        compiler_params=_params(("parallel",)),
    )(*parts, w, m, v)


def _adamw_small(parts, w, m, v):
    k = len(SMALL_LAYOUT)
    c1 = 1.0 - ADAM_B1 ** ADAM_STEP
    c2 = 1.0 - ADAM_B2 ** ADAM_STEP

    def body(p_ref, *refs):
        w_refs, m_refs, v_refs, outs = refs[:k], refs[k:2 * k], refs[2 * k:3 * k], refs[3 * k:]
        total = p_ref[0]
        for s in range(1, N_DEV):
            total = total + p_ref[s]
        for i, (_, row, off, width) in enumerate(SMALL_LAYOUT):
            g = total[row:row + 1, off:off + width]
            m_new = ADAM_B1 * m_refs[i][...] + (1.0 - ADAM_B1) * g
            v_new = ADAM_B2 * v_refs[i][...] + (1.0 - ADAM_B2) * (g * g)
            outs[4 * i][...] = g
            outs[4 * i + 1][...] = -ADAM_LR * ((m_new / c1) / (jnp.sqrt(v_new / c2) + ADAM_EPS)
                                               + ADAM_WD * w_refs[i][...])
            outs[4 * i + 2][...] = m_new
            outs[4 * i + 3][...] = v_new
        outs[4 * k][...] = total[SMALL_LOSS_ROW:SMALL_LOSS_ROW + 1, SMALL_LOSS_OFF:SMALL_LOSS_OFF + 1]

    names = [name for name, *_ in SMALL_LAYOUT]
    out_shape = [jax.ShapeDtypeStruct(w[name].shape, F32) for name in names for _ in range(4)]
    outs = pl.pallas_call(
        body, name="adamw_small", out_shape=out_shape + [jax.ShapeDtypeStruct((1, 1), F32)],
    )(parts, *[w[n] for n in names], *[m[n] for n in names], *[v[n] for n in names])
    return {name: tuple(outs[4 * i:4 * i + 4]) for i, name in enumerate(names)}, outs[4 * k]


def _pad_heads_cols(w, heads, width):
    k = w.shape[0]
    w = w.reshape(k, heads, width)
    return jnp.pad(w, ((0, 0), (0, 0), (0, SLAB - width))).reshape(k, heads * SLAB)


def _unpad_heads_cols(w, heads, width):
    k = w.shape[0]
    return w.reshape(k, heads, SLAB)[:, :, :width].reshape(k, heads * width)


def _pad_heads_rows(w, heads, width):
    n = w.shape[1]
    w = w.reshape(heads, width, n)
    return jnp.pad(w, ((0, 0), (0, SLAB - width), (0, 0))).reshape(heads * SLAB, n)


def _unpad_heads_rows(w, heads, width):
    n = w.shape[1]
    return w.reshape(heads, SLAB, n)[:, :width, :].reshape(heads * width, n)


def _pad_w_in(w_in):
    o = 2 * D_MODEL
    qa = _pad_heads_cols(w_in[:, o:o + 512], N_HEADS, HEAD_A)
    ka = _pad_heads_cols(w_in[:, o + 512:o + 640], N_KV_A, HEAD_A)
    va = _pad_heads_cols(w_in[:, o + 640:o + 768], N_KV_A, HEAD_A)
    kr = jnp.pad(w_in[:, o + 1152:o + 1184], ((0, 0), (QK_NOPE, SLAB - QK_NOPE - QK_ROPE)))
    return jnp.concatenate([w_in[:, :o], qa, ka, va, w_in[:, o + 768:o + 1152], kr], axis=1)


def _unpad_w_in(w):
    qa = _unpad_heads_cols(w[:, C_QA:C_KA], N_HEADS, HEAD_A)
    ka = _unpad_heads_cols(w[:, C_KA:C_VA], N_KV_A, HEAD_A)
    va = _unpad_heads_cols(w[:, C_VA:C_CQ], N_KV_A, HEAD_A)
    kr = w[:, C_KR + QK_NOPE:C_KR + QK_NOPE + QK_ROPE]
    return jnp.concatenate([w[:, :C_QA], qa, ka, va, w[:, C_CQ:C_KR], kr], axis=1)


def _pad_w_kvb(w_kvb):
    w = w_kvb.reshape(KV_LORA, N_HEADS, QK_NOPE + V_DIM_B)
    k = jnp.pad(w[:, :, :QK_NOPE], ((0, 0), (0, 0), (0, SLAB - QK_NOPE))).reshape(KV_LORA, HM)
    v = jnp.pad(w[:, :, QK_NOPE:], ((0, 0), (0, 0), (0, SLAB - V_DIM_B))).reshape(KV_LORA, HM)
    return jnp.concatenate([k, v], axis=1)


def _unpad_w_kvb(w):
    k = w[:, :HM].reshape(KV_LORA, N_HEADS, SLAB)[:, :, :QK_NOPE]
    v = w[:, HM:].reshape(KV_LORA, N_HEADS, SLAB)[:, :, :V_DIM_B]
    return jnp.concatenate([k, v], axis=2).reshape(KV_LORA, N_HEADS * (QK_NOPE + V_DIM_B))


def _col_shards(w):
    k, n = w.shape
    ns = n // N_DEV
    if ns % SLAB:
        return jnp.stack([w[:, d * ns:(d + 1) * ns] for d in range(N_DEV)])
    return w.reshape(k, N_DEV, ns).transpose(1, 0, 2)


def _from_col_shards(s):
    _, k, ns = s.shape
    if ns % SLAB:
        return jnp.concatenate([s[d] for d in range(N_DEV)], axis=1)
    return s.transpose(1, 0, 2).reshape(k, N_DEV * ns)


def _freq_row():
    freqs = ROPE_THETA ** (-jnp.arange(0, QK_ROPE, 2, dtype=F32) / QK_ROPE)
    return jnp.concatenate([jnp.zeros((QK_NOPE,), F32), freqs, freqs,
                            jnp.zeros((SLAB - QK_NOPE - QK_ROPE,), F32)]).reshape(1, SLAB)


SMALL_D_ROWS = ("pre_norm_mix", "post_norm_mix", "pre_norm_mlp", "post_norm_mlp")
SMALL_LAYOUT = tuple((name, i, 0, D_MODEL) for i, name in enumerate(SMALL_D_ROWS)) + (
    ("q_a_norm", 4, 0, Q_LORA), ("kv_a_norm", 4, 256, KV_LORA), ("sinks", 4, 384, N_HEADS))
SMALL_LOSS_ROW, SMALL_LOSS_OFF = 4, 512


def _pack_small(vals):
    row4 = jnp.concatenate([vals["q_a_norm"].reshape(-1), vals["kv_a_norm"].reshape(-1), vals["sinks"].reshape(-1),
                            jnp.zeros((SMALL_LOSS_OFF - 392,), F32), vals["loss"].reshape(-1),
                            jnp.zeros((1024 - SMALL_LOSS_OFF - 1,), F32)])
    rows = [vals[n].reshape(1024) for n in SMALL_D_ROWS] + [row4]
    return jnp.concatenate([jnp.stack(rows), jnp.zeros((SMALL_ROWS - 5, 1024), F32)], axis=0)


WEIGHT_ORDER = ("pre_norm_mix", "w_in", "q_a_norm", "w_q_b", "kv_a_norm", "w_kv_b", "sinks", "w_o_a", "w_o_b",
                "w_out", "post_norm_mix", "pre_norm_mlp", "w_up", "w_down", "post_norm_mlp")


def kernel(x, positions, pre_norm_mix, w_in, q_a_norm, w_q_b, kv_a_norm, w_kv_b, sinks, w_o_a, w_o_b, w_out, post_norm_mix, pre_norm_mlp, w_up, w_down, post_norm_mlp, loss_target, m_pre_norm_mix, m_w_in, m_q_a_norm, m_w_q_b, m_kv_a_norm, m_w_kv_b, m_sinks, m_w_o_a, m_w_o_b, m_w_out, m_post_norm_mix, m_pre_norm_mlp, m_w_up, m_w_down, m_post_norm_mlp, v_pre_norm_mix, v_w_in, v_q_a_norm, v_w_q_b, v_kv_a_norm, v_w_kv_b, v_sinks, v_w_o_a, v_w_o_b, v_w_out, v_post_norm_mix, v_pre_norm_mlp, v_w_up, v_w_down, v_post_norm_mlp):
    weights = dict(pre_norm_mix=pre_norm_mix, w_in=w_in, q_a_norm=q_a_norm, w_q_b=w_q_b, kv_a_norm=kv_a_norm,
                   w_kv_b=w_kv_b, sinks=sinks, w_o_a=w_o_a, w_o_b=w_o_b, w_out=w_out, post_norm_mix=post_norm_mix,
                   pre_norm_mlp=pre_norm_mlp, w_up=w_up, w_down=w_down, post_norm_mlp=post_norm_mlp)
    m_in = dict(pre_norm_mix=m_pre_norm_mix, w_in=m_w_in, q_a_norm=m_q_a_norm, w_q_b=m_w_q_b, kv_a_norm=m_kv_a_norm,
                w_kv_b=m_w_kv_b, sinks=m_sinks, w_o_a=m_w_o_a, w_o_b=m_w_o_b, w_out=m_w_out,
                post_norm_mix=m_post_norm_mix, pre_norm_mlp=m_pre_norm_mlp, w_up=m_w_up, w_down=m_w_down,
                post_norm_mlp=m_post_norm_mlp)
    v_in = dict(pre_norm_mix=v_pre_norm_mix, w_in=v_w_in, q_a_norm=v_q_a_norm, w_q_b=v_w_q_b, kv_a_norm=v_kv_a_norm,
                w_kv_b=v_w_kv_b, sinks=v_sinks, w_o_a=v_w_o_a, w_o_b=v_w_o_b, w_out=v_w_out,
                post_norm_mix=v_post_norm_mix, pre_norm_mlp=v_pre_norm_mlp, w_up=v_w_up, w_down=v_w_down,
                post_norm_mlp=v_post_norm_mlp)

    xs, pos, target = x[0], positions[0], loss_target[0]
    t = xs.shape[0]
    pos_col = pos.reshape(t, 1)
    pos_row = pos.reshape(1, t)
    g1, g2, g3, g4 = (weights[n] for n in SMALL_D_ROWS)
    g_q, g_kv = q_a_norm, kv_a_norm
    sink_vec = sinks.reshape(N_HEADS)
    shard = {n: weights[n][0].astype(BF16) for n in EARLY + LATE}

    tables, (e_in, e_qb, e_kvb) = _rope_tables(pos_col, _freq_row(), [shard[n] for n in EARLY])
    w_in_p = _pad_w_in(_from_col_shards(e_in))
    w_qb = _pad_heads_cols(_from_col_shards(e_qb), N_HEADS, QK_NOPE + QK_ROPE)
    w_kvb = _pad_w_kvb(_from_col_shards(e_kvb))

    (h, gates, qa, ka, va, cq, ckv, cqn, ckvn, kb, vb, qt, kt, vt) = _inproj_fwd(
        xs, g1, w_in_p, g_q, g_kv, w_kvb, w_qb.T, w_kvb[:, :HM].T, w_kvb[:, HM:].T, w_in_p[:, C_KR:].T, tables)
    out_a, lse_a = _swa_fwd(qa, ka, va, pos_col, pos_row, sink_vec)
    out_b, out_b_t, qt_lse, (l_oa, l_ob, l_out, w_up_s, l_down) = _mla_fwd(qt, kb, vt, [shard[n] for n in LATE])
    w_oa = _pad_heads_rows(_from_col_shards(l_oa), N_HEADS, HEAD_A)
    w_ob = _pad_heads_rows(_from_col_shards(l_ob), N_HEADS, V_DIM_B)
    w_out_f = l_out.reshape(D_MODEL, D_MODEL)
    w_down_f = l_down.reshape(D_FF, D_MODEL)

    oa_p, ob_p, merged, y, x1, h2 = _merge_fwd(out_a, out_b, gates, xs, w_oa, w_ob, w_out_f, g2, g3)
    a, du, dy2, dx1, loss, dg3, dg4 = _mlp_fwd_bwd(x1, h2, target, w_up_s, w_down_f, g3, g4)
    (dgates, d_oa, d_ob_t, dg2, dw_oa, dw_ob, dw_out) = _merge_bwd(
        dx1, y, gates, oa_p, ob_p, out_a, out_b, out_b_t, merged, w_oa, w_ob, w_out_f, g2)
    late_slices = [
        _col_shards(_unpad_heads_rows(dw_oa, N_HEADS, HEAD_A)).astype(BF16),
        _col_shards(_unpad_heads_rows(dw_ob, N_HEADS, V_DIM_B)).astype(BF16),
        dw_out.astype(BF16).reshape(N_DEV, D_MODEL // N_DEV, D_MODEL),
        _matmul_tn(h2, du, "dw_up", BF16, N_DEV),
        _matmul_tn(a, dy2, "dw_down", BF16).reshape(N_DEV, D_FF // N_DEV, D_MODEL),
    ]
    dqkv_a, dsink = _swa_bwd(qa, ka, va, out_a, d_oa, lse_a, pos_col, pos_row, sink_vec)
    dw_in_early = jnp.concatenate([_matmul_tn(h, dgates, "dw_in_gates"), _matmul_tn(h, dqkv_a, "dw_in_mixer_a"),
                                   jnp.zeros((D_MODEL, D_IN_PAD - C_CQ), F32)], axis=1)
    late_slices.append(_col_shards(_unpad_w_in(dw_in_early)).astype(BF16))
    dqb_t, dkb_t, dvb_t, late_parts = _mla_bwd(qt_lse, kb, kt, vb, d_ob_t, late_slices)
    w_in_early_parts = late_parts.pop()
    dproj_late, dgq, dgkv, dw_qb_t, dw_kvb_t = _latent_bwd(
        dqb_t, dkb_t, dvb_t, cq, ckv, cqn, ckvn, *tables[3:], g_q, g_kv, w_qb, w_kvb)
    dw_l = _matmul_tn(h, dproj_late, "dw_in_latents")
    late_cols = jnp.concatenate([dw_l[:, :Q_LORA + KV_LORA], dw_l[:, C_KR - C_CQ + QK_NOPE:C_KR - C_CQ + Q_HEAD_B]],
                                axis=1)
    shard_cols = w_in.shape[2]
    head = late_cols.shape[1] - shard_cols
    w_in_late = jnp.concatenate([
        jnp.zeros((N_DEV - 2, D_MODEL, shard_cols), F32),
        jnp.pad(late_cols[:, :head], ((0, 0), (shard_cols - head, 0)))[None], late_cols[:, head:][None]])
    early_slices = [
        w_in_late.astype(BF16),
        _col_shards(_unpad_heads_cols(dw_qb_t.T, N_HEADS, QK_NOPE + QK_ROPE)).astype(BF16),
        _col_shards(_unpad_w_kvb(dw_kvb_t.T)).astype(BF16),
    ]
    dx, dg1, early_parts = _inproj_bwd(dgates, dqkv_a, dproj_late, xs, dx1, g1, w_in_p, early_slices,
                                       only=[(N_DEV - 2, N_DEV - 1), None, None])
    small_grads = {"pre_norm_mix": dg1, "post_norm_mix": dg2, "pre_norm_mlp": dg3, "post_norm_mlp": dg4,
                   "q_a_norm": dgq, "kv_a_norm": dgkv, "sinks": dsink.reshape(N_HEADS, BLOCK).sum(axis=1),
                   "loss": loss[0, 0:1]}
    s_parts = _gather_small(_pack_small(small_grads))

    updates = {}
    all_parts = [[w_in_early_parts, early_parts[0]]] + [[p] for p in early_parts[1:] + late_parts]
    for name, parts in zip(EARLY + LATE, all_parts):
        outs = _adamw(parts, weights[name], m_in[name], v_in[name], "adamw_" + name)
        for kind, arr in zip(("g", "d", "m", "v"), outs):
            updates[kind, name] = arr
    small_out, loss_sum = _adamw_small(s_parts, weights, m_in, v_in)
    for name, outs in small_out.items():
        for kind, arr in zip(("g", "d", "m", "v"), outs):
            updates[kind, name] = arr
    results = [updates[kind, name] for kind in ("g", "d", "m", "v") for name in WEIGHT_ORDER]
    return (loss_sum.reshape(()), dx[None], *results)
```

```python
import functools

import numpy as np
import jax
import jax.numpy as jnp
from jax import lax
from jax.experimental import pallas as pl
from jax.experimental.pallas import tpu as pltpu

F32 = jnp.float32
BF16 = jnp.bfloat16

D_MODEL = 1024
D_FF = 4096
N_HEADS = 8
N_KV_A = 2
GROUP_A = N_HEADS // N_KV_A
HEAD_A = 64
QK_NOPE = 64
QK_ROPE = 32
V_DIM_B = 64
Q_LORA = 256
KV_LORA = 128
BLOCK = 128
SLAB = 128
ROPE_THETA = 10000.0
EPS = 1e-6
N_DEV = 8
NEG = -1e30

SCALE_A = HEAD_A ** -0.5
SCALE_B = (QK_NOPE + QK_ROPE) ** -0.5
LOG2E = 1.4426950408889634
SCORE_B = SCALE_B * LOG2E
MLA_HEADS_PER_STEP = 4
MLA_FWD_HEADS_PER_STEP = 8
Q_HEAD_B = QK_NOPE + QK_ROPE
ONES_ROWS = 16
SLOPES_A = tuple(2.0 ** (-8.0 * (h + 1) / N_HEADS) for h in range(N_HEADS))

ADAM_LR = 0.001
ADAM_B1 = 0.9
ADAM_B2 = 0.999
ADAM_EPS = 1e-08
ADAM_WD = 0.01
ADAM_STEP = 10

HM = N_HEADS * SLAB
C_GATES = 0
C_QA = 2 * D_MODEL
C_KA = C_QA + HM
C_VA = C_KA + N_KV_A * SLAB
C_CQ = C_VA + N_KV_A * SLAB
C_CKV = C_CQ + Q_LORA
C_KR = C_CKV + KV_LORA
D_IN_PAD = C_KR + SLAB

VMEM_LIMIT = 56 * 1024 * 1024

EARLY = ("w_in", "w_q_b", "w_kv_b")
LATE = ("w_o_a", "w_o_b", "w_out", "w_up", "w_down")
ADAM_ROWS = 256
SMALL_ROWS = 8


def _token_tile(t):
    return min(256, t)


def _wide_token_tile(t):
    return min(512, t)


def _attn_tile(t):
    return 512 if t >= 2048 else 128


def _params(sem, vmem=VMEM_LIMIT):
    return pltpu.CompilerParams(dimension_semantics=sem, vmem_limit_bytes=vmem)


def _dot(a, b):
    return jnp.dot(a, b, preferred_element_type=F32)


def _dot_nt(a, b):
    return lax.dot_general(a, b, (((1,), (1,)), ((), ())), preferred_element_type=F32)


def _dot_tn(a, b):
    return lax.dot_general(a, b, (((0,), (0,)), ((), ())), preferred_element_type=F32)


def _rms_r(x):
    return lax.rsqrt(jnp.mean(x * x, axis=-1, keepdims=True) + EPS)


def _rms_bwd(x, r, g, dy):
    t = dy * g
    return r * t - x * (r * r * r) * jnp.mean(x * t, axis=-1, keepdims=True)


def _sigmoid(x):
    return 1.0 / (1.0 + jnp.exp(-x))


def _rope(x, c, s1, s2):
    return x * c + pltpu.roll(x, SLAB - 16, 1) * s1 + pltpu.roll(x, 16, 1) * s2


def _rope_bwd(d, c, s1, s2):
    return d * c + pltpu.roll(d * s1, 16, 1) + pltpu.roll(d * s2, SLAB - 16, 1)


def _roll_rows(x, shift):
    return jnp.concatenate([x[-shift:], x[:-shift]], axis=0)


def _rope_t(x, c, s1, s2):
    return x * c + _roll_rows(x, SLAB - 16) * s1 + _roll_rows(x, 16) * s2


def _rope_t_bwd(d, c, s1, s2):
    return d * c + _roll_rows(d * s1, 16) + _roll_rows(d * s2, SLAB - 16)


def _plant_rows(slab, row, vals):
    hi = vals.astype(BF16).astype(F32)
    lo = (vals - hi).astype(BF16).astype(F32)
    idx = lax.broadcasted_iota(jnp.int32, slab.shape, 0)
    return jnp.where(idx == row, -hi, jnp.where(idx == row + 1, -lo, slab))


def _row_spec(tm, n):
    return pl.BlockSpec((tm, n), lambda i: (i, 0))


def _col_spec(n, tm):
    return pl.BlockSpec((n, tm), lambda i: (0, i))


def _full_spec(shape):
    nd = len(shape)
    return pl.BlockSpec(shape, lambda i: (0,) * nd, pipeline_mode=pl.Buffered(1))


def _acc_rows(ref, val):
    @pl.when(pl.program_id(0) == 0)
    def _():
        ref[...] = jnp.zeros_like(ref)
    ref[...] += jnp.sum(val, axis=0, keepdims=True)


def _rope_tables(pos_col, freq_row, early):
    t = pos_col.shape[0]
    tm = _token_tile(t)
    n = len(early)

    def body(pos_ref, f_ref, *rest):
        shard_refs, (c_ref, s1_ref, s2_ref, ct_ref, s1t_ref, s2t_ref) = rest[:n], rest[n:n + 6]
        start, finish = _two_level_gather(shard_refs, rest[n + 6:2 * n + 6], *rest[2 * n + 6:])
        pl.when(pl.program_id(0) == 0)(start)
        ang = pos_ref[...].astype(F32) * f_ref[...]
        lane = lax.broadcasted_iota(jnp.int32, ang.shape, 1)
        s = jnp.sin(ang)
        c = jnp.cos(ang)
        s1 = jnp.where((lane >= 64) & (lane < 80), -s, 0.0)
        s2 = jnp.where((lane >= 80) & (lane < 96), s, 0.0)
        c_ref[...], s1_ref[...], s2_ref[...] = c, s1, s2
        ct_ref[...], s1t_ref[...], s2t_ref[...] = c.T, s1.T, s2.T
        pl.when(pl.program_id(0) == t // tm - 1)(finish)

    tab = jax.ShapeDtypeStruct((t, SLAB), F32)
    tabt = jax.ShapeDtypeStruct((SLAB, t), F32)
    outs = pl.pallas_call(
        body, name="rope_tables", grid=(t // tm,),
        in_specs=[_row_spec(tm, 1), _full_spec((1, SLAB))] + [ANY_SPEC] * n,
        out_specs=[_row_spec(tm, SLAB)] * 3 + [_col_spec(SLAB, tm)] * 3 + [ANY_SPEC] * n,
        out_shape=[tab] * 3 + [tabt] * 3 + [jax.ShapeDtypeStruct((N_DEV,) + a.shape, a.dtype) for a in early],
        scratch_shapes=_exchange_scratch(n),
        compiler_params=_params(("arbitrary",)),
    )(pos_col, freq_row, *early)
    return outs[:6], outs[6:]


def _inproj_fwd(x, g1, w_in, g_q, g_kv, w_kvb, w_qb_t, w_kb_t, w_vb_t, w_kr_t, tables):
    t = x.shape[0]
    tm = _wide_token_tile(t)

    def body(x_ref, g1_ref, win_ref, gq_ref, gkv_ref, wkvb_ref, wqbt_ref, wkbt_ref, wvbt_ref, wkrt_ref,
             c_ref, s1_ref, s2_ref, ct_ref, s1t_ref, s2t_ref,
             h_ref, gates_ref, qa_ref, ka_ref, va_ref, cq_ref, ckv_ref, cqn_ref, ckvn_ref,
             kb_ref, vb_ref, qt_ref, kt_ref, vt_ref):
        xv = x_ref[...]
        h = (xv * _rms_r(xv) * g1_ref[...]).astype(BF16)
        h_ref[...] = h
        proj = _dot(h, win_ref[...])
        gates_ref[...] = proj[:, C_GATES:C_QA].astype(BF16)
        qa_ref[...] = proj[:, C_QA:C_KA].astype(BF16)
        ka_ref[...] = proj[:, C_KA:C_VA].astype(BF16)
        va_ref[...] = proj[:, C_VA:C_CQ].astype(BF16)
        cq = proj[:, C_CQ:C_CKV]
        ckv = proj[:, C_CKV:C_KR]
        kr = proj[:, C_KR:D_IN_PAD]
        cq_ref[...] = cq
        ckv_ref[...] = ckv
        cqn = (cq * _rms_r(cq) * gq_ref[...]).astype(BF16)
        ckvn = (ckv * _rms_r(ckv) * gkv_ref[...]).astype(BF16)
        cqn_ref[...] = cqn
        ckvn_ref[...] = ckvn
        c, s1, s2 = c_ref[...], s1_ref[...], s2_ref[...]
        kvb = _dot(ckvn, wkvb_ref[...])
        kr_rot = _rope(kr, c, s1, s2)
        ct, s1t, s2t = ct_ref[...], s1t_ref[...], s2t_ref[...]
        q_t = _dot_nt(wqbt_ref[...], cqn)
        k_t = _dot_nt(wkbt_ref[...], ckvn)
        kr_t = _rope_t(_dot_nt(wkrt_ref[...], h), ct, s1t, s2t)
        k_lane = lax.broadcasted_iota(jnp.int32, (1, SLAB), 1)
        k_ones = jnp.where((k_lane == Q_HEAD_B) | (k_lane == Q_HEAD_B + 1), 1.0, 0.0)
        for hd in range(N_HEADS):
            sl = slice(hd * SLAB, (hd + 1) * SLAB)
            kb_ref[:, sl] = (kvb[:, sl] + kr_rot + k_ones).astype(BF16)
            qt_ref[sl, :] = (_rope_t(q_t[sl, :], ct, s1t, s2t) * SCORE_B).astype(BF16)
            kt_ref[sl, :] = (k_t[sl, :] + kr_t).astype(BF16)
        v_lane = lax.broadcasted_iota(jnp.int32, (1, HM), 1) & (SLAB - 1)
        v_ones = jnp.where((v_lane == V_DIM_B) | (v_lane == V_DIM_B + 1), 1.0, 0.0)
        vb_ref[...] = (kvb[:, HM:2 * HM] + v_ones).astype(BF16)
        pad_row = lax.broadcasted_iota(jnp.int32, (HM, 1), 0) & (SLAB - 1)
        ones_rows = jnp.where((pad_row >= V_DIM_B) & (pad_row < V_DIM_B + ONES_ROWS), 1.0, 0.0)
        vt_ref[...] = (_dot_nt(wvbt_ref[...], ckvn) + ones_rows).astype(BF16)

    def sds(n, dt):
        return jax.ShapeDtypeStruct((t, n), dt)

    outs = [(D_MODEL, BF16), (2 * D_MODEL, BF16), (HM, BF16), (N_KV_A * SLAB, BF16), (N_KV_A * SLAB, BF16),
            (Q_LORA, F32), (KV_LORA, F32), (Q_LORA, BF16), (KV_LORA, BF16), (HM, BF16), (HM, BF16)]
    tab, tabt = _row_spec(tm, SLAB), _col_spec(SLAB, tm)
    return pl.pallas_call(
        body, name="inproj_fwd", grid=(t // tm,),
        in_specs=[_row_spec(tm, D_MODEL), _full_spec((1, D_MODEL)), _full_spec((D_MODEL, D_IN_PAD)),
                  _full_spec((1, Q_LORA)), _full_spec((1, KV_LORA)), _full_spec((KV_LORA, 2 * HM)),
                  _full_spec((HM, Q_LORA)), _full_spec((HM, KV_LORA)), _full_spec((HM, KV_LORA)),
                  _full_spec((SLAB, D_MODEL)), tab, tab, tab, tabt, tabt, tabt],
        out_specs=[_row_spec(tm, n) for n, _ in outs] + [_col_spec(HM, tm)] * 3,
        out_shape=[sds(n, dt) for n, dt in outs] + [jax.ShapeDtypeStruct((HM, t), BF16)] * 3,
        compiler_params=_params(("parallel",)),
    )(x, g1, w_in, g_q, g_kv, w_kvb, w_qb_t, w_kb_t, w_vb_t, w_kr_t, *tables)


def _tile_group(a):
    return jnp.concatenate([a] * GROUP_A, axis=1)


def _swa_masks():
    row = lax.broadcasted_iota(jnp.int32, (BLOCK, GROUP_A * BLOCK), 0)
    col = lax.broadcasted_iota(jnp.int32, (BLOCK, GROUP_A * BLOCK), 1) & (BLOCK - 1)
    return row <= col, row > col


def _heads_beside(ref, g):
    return jnp.concatenate([ref[:, (g * GROUP_A + hh) * SLAB:(g * GROUP_A + hh + 1) * SLAB].T
                            for hh in range(GROUP_A)], axis=1)


def _rows_beside(ref, g):
    return jnp.concatenate([ref[g * GROUP_A + hh] for hh in range(GROUP_A)], axis=1)


def _swa_rows(sinks):
    slopes = jnp.repeat(jnp.asarray(SLOPES_A, F32).reshape(N_KV_A, GROUP_A, 1), BLOCK, axis=2)
    sink_rows = jnp.repeat(sinks.reshape(N_KV_A, GROUP_A, 1), BLOCK, axis=2)
    return slopes.reshape(N_KV_A, 1, GROUP_A * BLOCK), sink_rows.reshape(N_KV_A, 1, GROUP_A * BLOCK)


def _swa_fwd(qa, ka, va, pos_col, pos_row, sinks):
    t = qa.shape[0]
    nb = t // BLOCK
    gw = GROUP_A * BLOCK
    slope_rows, sink_rows = _swa_rows(sinks)

    def body(q_ref, kc_ref, kp_ref, vc_ref, vp_ref, pkc_ref, pkp_ref, pq_ref, slope_ref, sink_ref, o_ref, l_ref):
        i = pl.program_id(0)
        pq = pq_ref[...]
        dist_c = _tile_group(jnp.abs(pkc_ref[...] - pq).astype(F32))
        dist_p = _tile_group(jnp.abs(pkp_ref[...] - pq).astype(F32))
        mask_c, older = _swa_masks()
        mask_p = jnp.logical_and(older, i > 0)
        raw = []
        for g in range(N_KV_A):
            gs = slice(g * SLAB, (g + 1) * SLAB)
            x = _heads_beside(q_ref, g)
            raw.append((_dot(kc_ref[:, gs], x), _dot(kp_ref[:, gs], x)))
        for g in range(N_KV_A):
            gs = slice(g * SLAB, (g + 1) * SLAB)
            slope, sink = slope_ref[g], sink_ref[g]
            s_c = jnp.where(mask_c, raw[g][0] * SCALE_A - slope * dist_c, NEG)
            s_p = jnp.where(mask_p, raw[g][1] * SCALE_A - slope * dist_p, NEG)
            m = jnp.maximum(jnp.maximum(jnp.max(s_c, axis=0, keepdims=True),
                                        jnp.max(s_p, axis=0, keepdims=True)), sink)
            e_c = jnp.exp(s_c - m)
            e_p = jnp.exp(s_p - m)
            den = jnp.sum(e_c, axis=0, keepdims=True) + jnp.sum(e_p, axis=0, keepdims=True) + jnp.exp(sink - m)
            inv = 1.0 / den
            ot = (_dot_tn(vc_ref[:, gs], (e_c * inv).astype(BF16))
                  + _dot_tn(vp_ref[:, gs], (e_p * inv).astype(BF16)))
            lse = m + jnp.log(den)
            for hh in range(GROUP_A):
                hd = g * GROUP_A + hh
                seg = slice(hh * BLOCK, (hh + 1) * BLOCK)
                o_ref[:, hd * SLAB:(hd + 1) * SLAB] = ot[:, seg].T.astype(BF16)
                l_ref[hd] = lse[:, seg]

    cur = lambda i: (i, 0)
    prev = lambda i: (jnp.maximum(i - 1, 0), 0)
    kvw = N_KV_A * SLAB
    rows = pl.BlockSpec((N_KV_A, 1, gw), lambda i: (0, 0, 0))
    return pl.pallas_call(
        body, name="swa_fwd", grid=(nb,),
        in_specs=[pl.BlockSpec((BLOCK, HM), cur),
                  pl.BlockSpec((BLOCK, kvw), cur), pl.BlockSpec((BLOCK, kvw), prev),
                  pl.BlockSpec((BLOCK, kvw), cur), pl.BlockSpec((BLOCK, kvw), prev),
                  pl.BlockSpec((BLOCK, 1), cur), pl.BlockSpec((BLOCK, 1), prev),
                  pl.BlockSpec((1, BLOCK), lambda i: (0, i)), rows, rows],
        out_specs=[pl.BlockSpec((BLOCK, HM), cur), pl.BlockSpec((N_HEADS, 1, BLOCK), lambda i: (0, 0, i))],
        out_shape=[jax.ShapeDtypeStruct((t, HM), BF16), jax.ShapeDtypeStruct((N_HEADS, 1, t), F32)],
        compiler_params=_params(("parallel",)),
    )(qa, ka, ka, va, va, pos_col, pos_col, pos_row, slope_rows, sink_rows)


def _swa_bwd(qa, ka, va, out_a, d_oa, lse, pos_col, pos_row, sinks):
    t = qa.shape[0]
    nb = t // BLOCK
    gw = GROUP_A * BLOCK
    kvw = N_KV_A * SLAB
    slope_rows, sink_rows = _swa_rows(sinks)

    def body(q_ref, qn_ref, do_ref, don_ref, l_ref, ln_ref, o_ref, on_ref, kp_ref, kc_ref, vp_ref, vc_ref,
             pkp_ref, pkc_ref, pq_ref, pqn_ref, slope_ref, sink_ref, dqkv_ref, dsink_ref):
        j = pl.program_id(0)
        pkc, pkp = pkc_ref[...], pkp_ref[...]
        dist_cc = _tile_group(jnp.abs(pkc - pq_ref[...]).astype(F32))
        dist_cp = _tile_group(jnp.abs(pkp - pq_ref[...]).astype(F32))
        dist_nc = _tile_group(jnp.abs(pkc - pqn_ref[...]).astype(F32))
        mask_cc, older = _swa_masks()
        mask_cp = jnp.logical_and(older, j > 0)
        mask_nc = jnp.logical_and(older, j < nb - 1)

        @pl.when(j == 0)
        def _():
            dsink_ref[...] = jnp.zeros_like(dsink_ref)

        def tile(k, v, x, dox, lrow, drow, dist, mask, slope):
            s = jnp.where(mask, _dot(k, x) * SCALE_A - slope * dist, NEG)
            p = jnp.exp(s - lrow)
            ds = p * (_dot(v, dox) - drow)
            return p.astype(BF16), ds.astype(BF16)

        for g in range(N_KV_A):
            gs = slice(g * SLAB, (g + 1) * SLAB)
            kc, kp, vc, vp = kc_ref[:, gs], kp_ref[:, gs], vc_ref[:, gs], vp_ref[:, gs]
            slope, sink = slope_ref[g], sink_ref[g]
            x, xn = _heads_beside(q_ref, g), _heads_beside(qn_ref, g)
            dox, doxn = _heads_beside(do_ref, g), _heads_beside(don_ref, g)
            lrow, lrown = _rows_beside(l_ref, g), _rows_beside(ln_ref, g)
            drow = jnp.sum(dox.astype(F32) * _heads_beside(o_ref, g).astype(F32), axis=0, keepdims=True)
            drown = jnp.sum(doxn.astype(F32) * _heads_beside(on_ref, g).astype(F32), axis=0, keepdims=True)
            p_cc, ds_cc = tile(kc, vc, x, dox, lrow, drow, dist_cc, mask_cc, slope)
            _, ds_cp = tile(kp, vp, x, dox, lrow, drow, dist_cp, mask_cp, slope)
            p_nc, ds_nc = tile(kc, vc, xn, doxn, lrown, drown, dist_nc, mask_nc, slope)
            dqt = (_dot_tn(kc, ds_cc) + _dot_tn(kp, ds_cp)) * SCALE_A
            for hh in range(GROUP_A):
                hd = g * GROUP_A + hh
                dqkv_ref[:, hd * SLAB:(hd + 1) * SLAB] = dqt[:, hh * BLOCK:(hh + 1) * BLOCK].T.astype(BF16)
            dqkv_ref[:, HM + g * SLAB:HM + (g + 1) * SLAB] = (
                (_dot_nt(ds_cc, x) + _dot_nt(ds_nc, xn)) * SCALE_A).astype(BF16)
            dqkv_ref[:, HM + kvw + g * SLAB:HM + kvw + (g + 1) * SLAB] = (
                _dot_nt(p_cc, dox) + _dot_nt(p_nc, doxn)).astype(BF16)
            dsink_ref[g] -= jnp.exp(sink - lrow) * drow

    cur = lambda j: (j, 0)
    prev = lambda j: (jnp.maximum(j - 1, 0), 0)
    nxt = lambda j: (jnp.minimum(j + 1, nb - 1), 0)
    cur3 = lambda j: (0, 0, j)
    nxt3 = lambda j: (0, 0, jnp.minimum(j + 1, nb - 1))
    kvw = N_KV_A * SLAB
    rows = pl.BlockSpec((N_KV_A, 1, gw), lambda j: (0, 0, 0))
    stat = lambda im: pl.BlockSpec((N_HEADS, 1, BLOCK), im)
    return pl.pallas_call(
        body, name="swa_bwd", grid=(nb,),
        in_specs=[pl.BlockSpec((BLOCK, HM), cur), pl.BlockSpec((BLOCK, HM), nxt),
                  pl.BlockSpec((BLOCK, HM), cur), pl.BlockSpec((BLOCK, HM), nxt),
                  stat(cur3), stat(nxt3), pl.BlockSpec((BLOCK, HM), cur), pl.BlockSpec((BLOCK, HM), nxt),
                  pl.BlockSpec((BLOCK, kvw), prev), pl.BlockSpec((BLOCK, kvw), cur),
                  pl.BlockSpec((BLOCK, kvw), prev), pl.BlockSpec((BLOCK, kvw), cur),
                  pl.BlockSpec((BLOCK, 1), prev), pl.BlockSpec((BLOCK, 1), cur),
                  pl.BlockSpec((1, BLOCK), lambda j: (0, j)),
                  pl.BlockSpec((1, BLOCK), lambda j: (0, jnp.minimum(j + 1, nb - 1))), rows, rows],
        out_specs=[pl.BlockSpec((BLOCK, HM + 2 * kvw), cur), rows],
        out_shape=[jax.ShapeDtypeStruct((t, HM + 2 * kvw), BF16), jax.ShapeDtypeStruct((N_KV_A, 1, gw), F32)],
        compiler_params=_params(("arbitrary",)),
    )(qa, qa, d_oa, d_oa, lse, lse, out_a, out_a, ka, ka, va, va,
      pos_col, pos_col, pos_row, pos_row, slope_rows, sink_rows)


def _mesh_pos():
    return lax.axis_index("x"), lax.axis_index("y"), lax.axis_index("c")


def _flip(v, bit):
    return 1 - v if bit else v


def _direct_copies(srcs, dsts, send_sems, recv_sems, local_sems, gather, sem_base=0, only=None):
    x, y, c = _mesh_pos()
    me = 4 * x + 2 * y + c

    def among(idx, dests):
        ok = idx == dests[0]
        for d in dests[1:]:
            ok = jnp.logical_or(ok, idx == d)
        return ok

    local, remote = [], []
    for a, (src, dst) in enumerate(zip(srcs, dsts)):
        dests = None if only is None else only[a]
        recv_ok = None if dests is None else among(me, dests)
        local.append((pltpu.make_async_copy(src if gather else src.at[me], dst.at[me],
                                            local_sems.at[sem_base + a]), recv_ok))
        for r in range(1, N_DEV):
            px, py, pc = _flip(x, r & 4), _flip(y, r & 2), _flip(c, r & 1)
            peer = 4 * px + 2 * py + pc
            sem = (N_DEV - 1) * (sem_base + a) + r - 1
            copy = pltpu.make_async_remote_copy(
                src_ref=src if gather else src.at[peer], dst_ref=dst.at[me],
                send_sem=send_sems.at[sem], recv_sem=recv_sems.at[sem],
                device_id=(px, py, pc), device_id_type=pl.DeviceIdType.MESH)
            remote.append((copy, None if dests is None else among(peer, dests), recv_ok))
    return local, remote


def _when(cond, fn):
    if cond is None:
        fn()
    else:
        pl.when(cond)(fn)


def _start_copies(local, remote):
    for cp, ok in local:
        _when(ok, cp.start)
    for cp, send_ok, _ in remote:
        _when(send_ok, cp.start)


def _wait_copies(local, remote):
    for cp, _, recv_ok in remote:
        _when(recv_ok, cp.wait_recv)
    for cp, send_ok, _ in remote:
        _when(send_ok, cp.wait_send)
    for cp, ok in local:
        _when(ok, cp.wait)


def _exchange_scratch(n):
    return [pltpu.SemaphoreType.DMA((n * (N_DEV - 1),)), pltpu.SemaphoreType.DMA((n * (N_DEV - 1),)),
            pltpu.SemaphoreType.DMA((n,))]


ANY_SPEC = pl.BlockSpec(memory_space=pl.ANY)


def _mla_fwd(qt, kb, vt, late):
    t = kb.shape[0]
    tk = _attn_tile(t)
    ratio = 2 if t >= 2 * tk else 1
    tq = ratio * tk
    nq = t // tq
    hps = MLA_FWD_HEADS_PER_STEP
    w = hps * SLAB
    pairs = [(i, j) for i in range(nq) for j in range(ratio * (i + 1))]
    i_tab = jnp.asarray(np.array([p[0] for p in pairs], np.int32))
    j_tab = jnp.asarray(np.array([p[1] for p in pairs], np.int32))

    n_late = len(late)

    def body(it_ref, jt_ref, qt_ref, k_ref, vt_ref, *rest):
        late_refs, (o_ref, ot_ref, qa_ref) = rest[:n_late], rest[n_late:n_late + 3]
        gathered_refs = rest[n_late + 3:2 * n_late + 3]
        m_s, acc_s, send_sems, recv_sems, local_sems = rest[2 * n_late + 3:]
        n = pl.program_id(1)
        i, j = it_ref[n], jt_ref[n]
        first_step = jnp.logical_and(pl.program_id(0) == 0, n == 0)
        last_step = jnp.logical_and(pl.program_id(0) == N_HEADS // hps - 1, n == len(pairs) - 1)

        @pl.when(first_step)
        def _():
            _start_copies(*_direct_copies(late_refs, gathered_refs, send_sems, recv_sems, local_sems, True))

        @pl.when(j == 0)
        def _():
            m_s[...] = jnp.full_like(m_s, NEG)
            acc_s[...] = jnp.zeros_like(acc_s)

        def update(masked, q0):
            qc = slice(q0, tq)

            def scores(hh):
                sl = slice(hh * SLAB, (hh + 1) * SLAB)
                return _dot(k_ref[:, sl], qt_ref[sl, qc])

            def softmax(hh, s):
                if masked:
                    s = jnp.where(lax.broadcasted_iota(jnp.int32, s.shape, 0)
                                  <= lax.broadcasted_iota(jnp.int32, s.shape, 1), s, NEG)
                m_old = m_s[hh][:, qc]
                m_new = jnp.maximum(m_old, jnp.max(s, axis=0, keepdims=True))
                m_s[hh, :, qc] = m_new
                return jnp.exp2(s - m_new).astype(BF16), jnp.exp2(m_old - m_new)

            def accumulate(hh, p, alpha):
                sl = slice(hh * SLAB, hh * SLAB + V_DIM_B + ONES_ROWS)
                acc_s[sl, qc] = alpha * acc_s[sl, qc] + _dot(vt_ref[sl, :], p)

            s_next, pending = scores(0), None
            for hh in range(hps):
                s = s_next
                if hh + 1 < hps:
                    s_next = scores(hh + 1)
                p, alpha = softmax(hh, s)
                if pending is not None:
                    accumulate(*pending)
                pending = (hh, p, alpha)
            accumulate(*pending)

        @pl.when(j < ratio * i)
        def _():
            update(False, 0)

        for part in range(ratio):
            @pl.when(j == ratio * i + part)
            def _():
                update(True, part * tk)

        @pl.when(j == ratio * i + ratio - 1)
        def _():
            for hh in range(hps):
                sl = slice(hh * SLAB, (hh + 1) * SLAB)
                den = acc_s[hh * SLAB + V_DIM_B:hh * SLAB + V_DIM_B + 1, :]
                values = lax.broadcasted_iota(jnp.int32, (SLAB, tq), 0) < V_DIM_B
                ot = jnp.where(values, acc_s[sl, :] / den, 0.0)
                ot_ref[sl, :] = ot.astype(BF16)
                o_ref[:, sl] = ot.T.astype(BF16)
                lse = m_s[hh] + jnp.log2(den)
                qa_ref[sl, :] = _plant_rows(qt_ref[sl, :].astype(F32), Q_HEAD_B, lse).astype(BF16)

        @pl.when(last_step)
        def _():
            _wait_copies(*_direct_copies(late_refs, gathered_refs, send_sems, recv_sems, local_sems, True))

    grid_spec = pltpu.PrefetchScalarGridSpec(
        num_scalar_prefetch=2, grid=(N_HEADS // hps, len(pairs)),
        in_specs=[pl.BlockSpec((w, tq), lambda h, n, it, jt: (h, it[n])),
                  pl.BlockSpec((tk, w), lambda h, n, it, jt: (jt[n], h)),
                  pl.BlockSpec((w, tk), lambda h, n, it, jt: (h, jt[n]))] + [ANY_SPEC] * n_late,
        out_specs=[pl.BlockSpec((tq, w), lambda h, n, it, jt: (it[n], h)),
                   pl.BlockSpec((w, tq), lambda h, n, it, jt: (h, it[n])),
                   pl.BlockSpec((w, tq), lambda h, n, it, jt: (h, it[n]))] + [ANY_SPEC] * n_late,
        scratch_shapes=[pltpu.VMEM((hps, 1, tq), F32), pltpu.VMEM((w, tq), F32)] + _exchange_scratch(n_late))
    outs = pl.pallas_call(
        body, name="mla_fwd", grid_spec=grid_spec,
        out_shape=[jax.ShapeDtypeStruct((t, HM), BF16), jax.ShapeDtypeStruct((HM, t), BF16),
                   jax.ShapeDtypeStruct((HM, t), BF16)]
        + [jax.ShapeDtypeStruct((N_DEV,) + a.shape, a.dtype) for a in late],
        compiler_params=_params(("arbitrary", "arbitrary")),
    )(i_tab, j_tab, qt, kb, vt, *late)
    return outs[0], outs[1], outs[2], list(outs[3:])


def _mla_bwd(qt, kb, kt, vb, d_ob_t, grad_slices):
    t = kb.shape[0]
    tk = _attn_tile(t)
    ratio = 2 if t >= 2 * tk else 1
    tq = ratio * tk
    nk, nq = t // tk, t // tq
    hps = MLA_HEADS_PER_STEP
    w = hps * SLAB
    pairs = [(j, i) for j in range(nk) for i in range(j // ratio, nq)]
    j_tab = jnp.asarray(np.array([p[0] for p in pairs], np.int32))
    i_tab = jnp.asarray(np.array([p[1] for p in pairs], np.int32))

    n_ex = len(grad_slices)

    def body(jt_ref, it_ref, qt_ref, dot_ref, k_ref, kt_ref, v_ref, *rest):
        slice_refs, (dqt_ref, dkt_ref, dvt_ref) = rest[:n_ex], rest[n_ex:n_ex + 3]
        part_refs = rest[n_ex + 3:2 * n_ex + 3]
        dk_s, dv_s, send_sems, recv_sems, local_sems = rest[2 * n_ex + 3:]
        n = pl.program_id(1)
        j, i = jt_ref[n], it_ref[n]
        first_step = jnp.logical_and(pl.program_id(0) == 0, n == 0)
        last_step = jnp.logical_and(pl.program_id(0) == N_HEADS // hps - 1, n == len(pairs) - 1)

        @pl.when(first_step)
        def _():
            _start_copies(*_direct_copies(slice_refs, part_refs, send_sems, recv_sems, local_sems, False))

        @pl.when(n == 0)
        def _():
            dqt_ref[...] = jnp.zeros_like(dqt_ref)

        def update(diagonal, q0):
            qc = slice(q0, tq)
            cols = pl.ds(pl.multiple_of(i * tq + q0, tk), tq - q0)

            def softmax_bwd(hh, s, dp):
                if diagonal:
                    s = jnp.where(lax.broadcasted_iota(jnp.int32, s.shape, 0)
                                  <= lax.broadcasted_iota(jnp.int32, s.shape, 1), s, NEG)
                p = jnp.exp2(s)
                return p.astype(BF16), (p * dp).astype(BF16)

            def gradients(hh, p, ds):
                base = hh * SLAB
                vrows = slice(base, base + V_DIM_B)
                qrows = slice(base, base + QK_NOPE + QK_ROPE)
                dv = _dot_nt(dot_ref[vrows, qc], p)
                dk = _dot_nt(qt_ref[qrows, qc], ds)
                if diagonal:
                    dv_s[base:base + SLAB, :] = jnp.concatenate([dv, jnp.zeros((SLAB - V_DIM_B, tk), F32)], axis=0)
                    dk_s[base:base + SLAB, :] = jnp.concatenate(
                        [dk, jnp.zeros((SLAB - QK_NOPE - QK_ROPE, tk), F32)], axis=0)
                else:
                    dv_s[vrows, :] += dv
                    dk_s[qrows, :] += dk
                dqt_ref[qrows, cols] += _dot(kt_ref[qrows, :], ds)

            def scores(hh):
                sl = slice(hh * SLAB, (hh + 1) * SLAB)
                return _dot(k_ref[:, sl], qt_ref[sl, qc])

            def dprod(hh):
                sl = slice(hh * SLAB, (hh + 1) * SLAB)
                return _dot(v_ref[:, sl], dot_ref[sl, qc])

            s_next = scores(0)
            for hh in range(hps):
                s = s_next
                dp = dprod(hh)
                if hh + 1 < hps:
                    s_next = scores(hh + 1)
                gradients(hh, *softmax_bwd(hh, s, dp))

        first_tile = lax.div(j, ratio)
        for part in range(ratio):
            @pl.when(jnp.logical_and(i == first_tile, lax.rem(j, ratio) == part))
            def _():
                update(True, part * tk)

        @pl.when(i > first_tile)
        def _():
            update(False, 0)

        @pl.when(i == nq - 1)
        def _():
            dkt_ref[...] = (dk_s[...] * (1.0 / LOG2E)).astype(BF16)
            dvt_ref[...] = dv_s[...].astype(BF16)

        @pl.when(last_step)
        def _():
            _wait_copies(*_direct_copies(slice_refs, part_refs, send_sems, recv_sems, local_sems, False))

    grid_spec = pltpu.PrefetchScalarGridSpec(
        num_scalar_prefetch=2, grid=(N_HEADS // hps, len(pairs)),
        in_specs=[pl.BlockSpec((w, tq), lambda h, n, jt, it: (h, it[n])),
                  pl.BlockSpec((w, tq), lambda h, n, jt, it: (h, it[n])),
                  pl.BlockSpec((tk, w), lambda h, n, jt, it: (jt[n], h)),
                  pl.BlockSpec((w, tk), lambda h, n, jt, it: (h, jt[n])),
                  pl.BlockSpec((tk, w), lambda h, n, jt, it: (jt[n], h))] + [ANY_SPEC] * n_ex,
        out_specs=[pl.BlockSpec((w, t), lambda h, n, jt, it: (h, 0)),
                   pl.BlockSpec((w, tk), lambda h, n, jt, it: (h, jt[n])),
                   pl.BlockSpec((w, tk), lambda h, n, jt, it: (h, jt[n]))] + [ANY_SPEC] * n_ex,
        scratch_shapes=[pltpu.VMEM((w, tk), F32), pltpu.VMEM((w, tk), F32)] + _exchange_scratch(n_ex))
    outs = pl.pallas_call(
        body, name="mla_bwd", grid_spec=grid_spec,
        out_shape=[jax.ShapeDtypeStruct((HM, t), F32), jax.ShapeDtypeStruct((HM, t), BF16),
                   jax.ShapeDtypeStruct((HM, t), BF16)]
        + [jax.ShapeDtypeStruct(a.shape, a.dtype) for a in grad_slices],
        compiler_params=_params(("arbitrary", "arbitrary")),
    )(j_tab, i_tab, qt, d_ob_t, kb, kt, vb, *grad_slices)
    return outs[0], outs[1], outs[2], list(outs[3:])


def _merge_fwd(out_a, out_b, gates, x, w_oa, w_ob, w_out, g2, g3):
    t = x.shape[0]
    tm = _wide_token_tile(t)

    def body(oa_ref, ob_ref, gates_ref, x_ref, woa_ref, wob_ref, wout_ref, g2_ref, g3_ref,
             oap_ref, obp_ref, merged_ref, y_ref, x1_ref, h2_ref):
        oa_p = _dot(oa_ref[...], woa_ref[...])
        ob_p = _dot(ob_ref[...], wob_ref[...])
        oap_ref[...] = oa_p.astype(BF16)
        obp_ref[...] = ob_p.astype(BF16)
        sa = _sigmoid(gates_ref[:, 0:D_MODEL].astype(F32))
        sb = _sigmoid(gates_ref[:, D_MODEL:2 * D_MODEL].astype(F32))
        merged = (sa * oa_p + sb * ob_p).astype(BF16)
        merged_ref[...] = merged
        y = _dot(merged, wout_ref[...])
        y_ref[...] = y
        x1 = x_ref[...] + y * _rms_r(y) * g2_ref[...]
        x1_ref[...] = x1
        h2_ref[...] = (x1 * _rms_r(x1) * g3_ref[...]).astype(BF16)

    def sds(dt):
        return jax.ShapeDtypeStruct((t, D_MODEL), dt)

    row = _row_spec(tm, D_MODEL)
    return pl.pallas_call(
        body, name="merge_fwd", grid=(t // tm,),
        in_specs=[_row_spec(tm, HM), _row_spec(tm, HM), _row_spec(tm, 2 * D_MODEL), row,
                  _full_spec((HM, D_MODEL)), _full_spec((HM, D_MODEL)), _full_spec((D_MODEL, D_MODEL)),
                  _full_spec((1, D_MODEL)), _full_spec((1, D_MODEL))],
        out_specs=[row] * 6,
        out_shape=[sds(BF16), sds(BF16), sds(BF16), sds(F32), sds(F32), sds(BF16)],
        compiler_params=_params(("parallel",)),
    )(out_a, out_b, gates, x, w_oa, w_ob, w_out, g2, g3)


def _merge_bwd(dx1, y, gates, oa_p, ob_p, out_a, out_b, out_b_t, merged, w_oa, w_ob, w_out, g2):
    t = dx1.shape[0]
    tm = _token_tile(t)

    def body(dx1_ref, y_ref, gates_ref, oap_ref, obp_ref, oa_ref, ob_ref, obt_ref, merged_ref,
             woa_ref, wob_ref, wout_ref, g2_ref,
             dgates_ref, doa_ref, dobt_ref, dg2_ref, dwoa_ref, dwob_ref, dwout_ref):
        @pl.when(pl.program_id(0) == 0)
        def _():
            dwoa_ref[...] = jnp.zeros_like(dwoa_ref)
            dwob_ref[...] = jnp.zeros_like(dwob_ref)
            dwout_ref[...] = jnp.zeros_like(dwout_ref)

        dx1v = dx1_ref[...]
        yv = y_ref[...]
        r2 = _rms_r(yv)
        _acc_rows(dg2_ref, dx1v * yv * r2)
        dy = _rms_bwd(yv, r2, g2_ref[...], dx1v).astype(BF16)
        dwout_ref[...] += _dot_tn(merged_ref[...], dy)
        dm = _dot_nt(dy, wout_ref[...])
        sa = _sigmoid(gates_ref[:, 0:D_MODEL].astype(F32))
        sb = _sigmoid(gates_ref[:, D_MODEL:2 * D_MODEL].astype(F32))
        d_oap = (dm * sa).astype(BF16)
        d_obp = (dm * sb).astype(BF16)
        dwoa_ref[...] += _dot_tn(oa_ref[...], d_oap)
        dwob_ref[...] += _dot_tn(ob_ref[...], d_obp)
        dgates_ref[:, 0:D_MODEL] = (dm * oap_ref[...].astype(F32) * sa * (1.0 - sa)).astype(BF16)
        dgates_ref[:, D_MODEL:2 * D_MODEL] = (dm * obp_ref[...].astype(F32) * sb * (1.0 - sb)).astype(BF16)
        doa_ref[...] = _dot_nt(d_oap, woa_ref[...]).astype(BF16)
        d_ob_t = _dot_nt(wob_ref[...], d_obp)
        for hd in range(N_HEADS):
            sl = slice(hd * SLAB, (hd + 1) * SLAB)
            delta = jnp.sum(d_ob_t[sl, :] * obt_ref[sl, :].astype(F32), axis=0, keepdims=True)
            dobt_ref[sl, :] = _plant_rows(d_ob_t[sl, :], V_DIM_B, delta).astype(BF16)

    def sds(n, dt):
        return jax.ShapeDtypeStruct((t, n), dt)

    row = _row_spec(tm, D_MODEL)
    return pl.pallas_call(
        body, name="merge_bwd", grid=(t // tm,),
        in_specs=[row, row, _row_spec(tm, 2 * D_MODEL), row, row, _row_spec(tm, HM), _row_spec(tm, HM),
                  _col_spec(HM, tm), row,
                  _full_spec((HM, D_MODEL)), _full_spec((HM, D_MODEL)), _full_spec((D_MODEL, D_MODEL)),
                  _full_spec((1, D_MODEL))],
        out_specs=[_row_spec(tm, 2 * D_MODEL), _row_spec(tm, HM), _col_spec(HM, tm), _full_spec((1, D_MODEL)),
                   _full_spec((HM, D_MODEL)), _full_spec((HM, D_MODEL)), _full_spec((D_MODEL, D_MODEL))],
        out_shape=[sds(2 * D_MODEL, BF16), sds(HM, BF16), jax.ShapeDtypeStruct((HM, t), BF16),
                   jax.ShapeDtypeStruct((1, D_MODEL), F32),
                   jax.ShapeDtypeStruct((HM, D_MODEL), F32), jax.ShapeDtypeStruct((HM, D_MODEL), F32),
                   jax.ShapeDtypeStruct((D_MODEL, D_MODEL), F32)],
        compiler_params=_params(("arbitrary",)),
    )(dx1, y, gates, oa_p, ob_p, out_a, out_b, out_b_t, merged, w_oa, w_ob, w_out, g2)


def _mlp_fwd_bwd(x1, h2, target, w_up, w_down, g3, g4):
    t = x1.shape[0]
    tm = _token_tile(t)
    fs = D_FF // N_DEV

    def body(x1_ref, h2_ref, tgt_ref, wup_ref, wdown_ref, g3_ref, g4_ref,
             a_ref, du_ref, dy2_ref, dx1_ref, loss_ref, dg3_ref, dg4_ref):
        x1v = x1_ref[...]
        h2v = h2_ref[...]
        u = jnp.concatenate([_dot(h2v, wup_ref[s]) for s in range(N_DEV)], axis=1)
        ru = jnp.maximum(u, 0.0)
        a = (ru * ru).astype(BF16)
        a_ref[...] = a
        y2 = _dot(a, wdown_ref[...])
        r4 = _rms_r(y2)
        diff = x1v + y2 * r4 * g4_ref[...] - tgt_ref[...]
        _acc_rows(loss_ref, jnp.sum(diff * diff, axis=-1, keepdims=True) * (0.5 / D_MODEL)
                  * jnp.ones((1, SLAB), F32))
        dx2 = diff * (1.0 / D_MODEL)
        _acc_rows(dg4_ref, dx2 * y2 * r4)
        dy2 = _rms_bwd(y2, r4, g4_ref[...], dx2).astype(BF16)
        dy2_ref[...] = dy2
        du = (_dot_nt(dy2, wdown_ref[...]) * (2.0 * ru)).astype(BF16)
        du_ref[...] = du
        dh2 = _dot_nt(du[:, 0:fs], wup_ref[0])
        for s in range(1, N_DEV):
            dh2 += _dot_nt(du[:, s * fs:(s + 1) * fs], wup_ref[s])
        r3 = _rms_r(x1v)
        _acc_rows(dg3_ref, dh2 * x1v * r3)
        dx1_ref[...] = dx2 + _rms_bwd(x1v, r3, g3_ref[...], dh2)

    row = _row_spec(tm, D_MODEL)
    frow = _row_spec(tm, D_FF)
    vec = _full_spec((1, D_MODEL))
    return pl.pallas_call(
        body, name="mlp_fwd_bwd", grid=(t // tm,),
        in_specs=[row, row, row, _full_spec((N_DEV, D_MODEL, fs)), _full_spec((D_FF, D_MODEL)), vec, vec],
        out_specs=[frow, frow, row, row, _full_spec((1, SLAB)), vec, vec],
        out_shape=[jax.ShapeDtypeStruct((t, D_FF), BF16), jax.ShapeDtypeStruct((t, D_FF), BF16),
                   jax.ShapeDtypeStruct((t, D_MODEL), BF16), jax.ShapeDtypeStruct((t, D_MODEL), F32),
                   jax.ShapeDtypeStruct((1, SLAB), F32), jax.ShapeDtypeStruct((1, D_MODEL), F32),
                   jax.ShapeDtypeStruct((1, D_MODEL), F32)],
        compiler_params=_params(("arbitrary",)),
    )(x1, h2, target, w_up, w_down, g3, g4)


def _latent_bwd(dqb_t, dkb_t, dvb_t, cq, ckv, cqn, ckvn, rope_ct, rope_s1t, rope_s2t, g_q, g_kv, w_qb, w_kvb):
    t = cq.shape[0]
    tm = _wide_token_tile(t)

    def body(dqt_ref, dkt_ref, dvt_ref, cq_ref, ckv_ref, cqn_ref, ckvn_ref, ct_ref, s1t_ref, s2t_ref,
             gq_ref, gkv_ref, wqb_ref, wkvb_ref,
             dlate_ref, dgq_ref, dgkv_ref, dwqb_ref, dwkvb_ref, dqbrt_ref, dkvbt_ref):
        @pl.when(pl.program_id(0) == 0)
        def _():
            dwqb_ref[...] = jnp.zeros_like(dwqb_ref)
            dwkvb_ref[...] = jnp.zeros_like(dwkvb_ref)

        ct, s1t, s2t = ct_ref[...], s1t_ref[...], s2t_ref[...]
        dk_sum_t = jnp.zeros((SLAB, tm), F32)
        for hd in range(N_HEADS):
            sl = slice(hd * SLAB, (hd + 1) * SLAB)
            dqbrt_ref[sl, :] = _rope_t_bwd(dqt_ref[sl, :] * SCALE_B, ct, s1t, s2t).astype(BF16)
            dk_sum_t += dkt_ref[sl, :].astype(F32)
        dkvbt_ref[0:HM, :] = dkt_ref[...]
        dkvbt_ref[HM:2 * HM, :] = dvt_ref[...]
        dkr = _rope_t_bwd(dk_sum_t, ct, s1t, s2t).T
        dwqb_ref[...] += _dot(dqbrt_ref[...], cqn_ref[...])
        dwkvb_ref[...] += _dot(dkvbt_ref[...], ckvn_ref[...])
        dcqn = _dot(wqb_ref[...], dqbrt_ref[...]).T
        cq = cq_ref[...]
        rq = _rms_r(cq)
        _acc_rows(dgq_ref, dcqn * cq * rq)
        dcq = _rms_bwd(cq, rq, gq_ref[...], dcqn)
        dckvn = _dot(wkvb_ref[...], dkvbt_ref[...]).T
        ckv = ckv_ref[...]
        rkv = _rms_r(ckv)
        _acc_rows(dgkv_ref, dckvn * ckv * rkv)
        dckv = _rms_bwd(ckv, rkv, gkv_ref[...], dckvn)
        dlate_ref[:, 0:C_CKV - C_CQ] = dcq.astype(BF16)
        dlate_ref[:, C_CKV - C_CQ:C_KR - C_CQ] = dckv.astype(BF16)
        dlate_ref[:, C_KR - C_CQ:D_IN_PAD - C_CQ] = dkr.astype(BF16)

    hmt = _col_spec(HM, tm)
    tab = _col_spec(SLAB, tm)
    return pl.pallas_call(
        body, name="latent_bwd", grid=(t // tm,),
        in_specs=[hmt, hmt, hmt,
                  _row_spec(tm, Q_LORA), _row_spec(tm, KV_LORA), _row_spec(tm, Q_LORA), _row_spec(tm, KV_LORA),
                  tab, tab, tab, _full_spec((1, Q_LORA)), _full_spec((1, KV_LORA)),
                  _full_spec((Q_LORA, HM)), _full_spec((KV_LORA, 2 * HM))],
        out_specs=[_row_spec(tm, D_IN_PAD - C_CQ), _full_spec((1, Q_LORA)), _full_spec((1, KV_LORA)),
                   _full_spec((HM, Q_LORA)), _full_spec((2 * HM, KV_LORA))],
        out_shape=[jax.ShapeDtypeStruct((t, D_IN_PAD - C_CQ), BF16),
                   jax.ShapeDtypeStruct((1, Q_LORA), F32), jax.ShapeDtypeStruct((1, KV_LORA), F32),
                   jax.ShapeDtypeStruct((HM, Q_LORA), F32), jax.ShapeDtypeStruct((2 * HM, KV_LORA), F32)],
        scratch_shapes=[pltpu.VMEM((HM, tm), BF16), pltpu.VMEM((2 * HM, tm), BF16)],
        compiler_params=_params(("arbitrary",)),
    )(dqb_t, dkb_t, dvb_t, cq, ckv, cqn, ckvn, rope_ct, rope_s1t, rope_s2t, g_q, g_kv, w_qb, w_kvb)


def _inproj_bwd(dgates, dqkv, dlate, x, dx1, g1, w_in, grad_slices, only):
    t = x.shape[0]
    tm = _wide_token_tile(t)
    n_ex = len(grad_slices)
    zeroed = [a for a in range(n_ex) if only[a] is not None]

    def body(dgates_ref, dqkv_ref, dlate_ref, x_ref, dx1_ref, g1_ref, win_ref, *rest):
        slice_refs = rest[:n_ex]
        dx_ref, dg1_ref = rest[n_ex:n_ex + 2]
        part_refs = rest[n_ex + 2:2 * n_ex + 2]
        dproj_ref, send_sems, recv_sems, local_sems = rest[2 * n_ex + 2:2 * n_ex + 6]
        zero_refs, zero_sem = rest[2 * n_ex + 6:-1], rest[-1]

        @pl.when(pl.program_id(0) == 0)
        def _():
            _start_copies(*_direct_copies(slice_refs, part_refs, send_sems, recv_sems, local_sems, False,
                                          only=only))
            x_, y_, c_ = _mesh_pos()
            me = 4 * x_ + 2 * y_ + c_
            for a, z_ref in zip(zeroed, zero_refs):
                outside = me != only[a][0]
                for d in only[a][1:]:
                    outside = jnp.logical_and(outside, me != d)

                @pl.when(outside)
                def _():
                    z_ref[...] = jnp.zeros_like(z_ref)
                    fills = [pltpu.make_async_copy(z_ref, part_refs[a].at[k], zero_sem.at[k])
                             for k in range(N_DEV)]
                    for cp in fills:
                        cp.start()
                    for cp in fills:
                        cp.wait()

        dproj_ref[:, C_GATES:C_QA] = dgates_ref[...]
        dproj_ref[:, C_QA:C_CQ] = dqkv_ref[...]
        dproj_ref[:, C_CQ:D_IN_PAD] = dlate_ref[...]
        dh = _dot_nt(dproj_ref[...], win_ref[...])
        xv = x_ref[...]
        r1 = _rms_r(xv)
        _acc_rows(dg1_ref, dh * xv * r1)
        dx_ref[...] = dx1_ref[...] + _rms_bwd(xv, r1, g1_ref[...], dh)

        @pl.when(pl.program_id(0) == t // tm - 1)
        def _():
            _wait_copies(*_direct_copies(slice_refs, part_refs, send_sems, recv_sems, local_sems, False,
                                         only=only))

    kvw = N_KV_A * SLAB
    row = _row_spec(tm, D_MODEL)
    outs = pl.pallas_call(
        body, name="inproj_bwd", grid=(t // tm,),
        in_specs=[_row_spec(tm, 2 * D_MODEL), _row_spec(tm, HM + 2 * kvw), _row_spec(tm, D_IN_PAD - C_CQ),
                  row, row, _full_spec((1, D_MODEL)), _full_spec((D_MODEL, D_IN_PAD))]
        + [ANY_SPEC] * n_ex,
        out_specs=[row, _full_spec((1, D_MODEL))] + [ANY_SPEC] * n_ex,
        out_shape=[jax.ShapeDtypeStruct((t, D_MODEL), F32), jax.ShapeDtypeStruct((1, D_MODEL), F32)]
        + [jax.ShapeDtypeStruct(a.shape, a.dtype) for a in grad_slices],
        scratch_shapes=[pltpu.VMEM((tm, D_IN_PAD), BF16)] + _exchange_scratch(n_ex)
        + [pltpu.VMEM(grad_slices[a].shape[1:], grad_slices[a].dtype) for a in zeroed]
        + [pltpu.SemaphoreType.DMA((N_DEV,))],
        compiler_params=_params(("arbitrary",)),
    )(dgates, dqkv, dlate, x, dx1, g1, w_in, *grad_slices)
    return outs[0], outs[1], list(outs[2:])


def _matmul_tn(a, b, name, out_dtype=F32, n_shards=1):
    t, k = a.shape
    n = b.shape[1]
    bt = min(t, 512)
    bn = min(n, 2048)
    bk = min(k, 2048 * 1024 // bn)
    ns = n // n_shards
    per_block = bn // ns
    steps = t // bt

    def body(a_ref, b_ref, o_ref, acc):
        s = pl.program_id(2)

        @pl.when(s == 0)
        def _():
            acc[...] = jnp.zeros_like(acc)

        acc[...] += _dot_tn(a_ref[...], b_ref[...])

        @pl.when(s == steps - 1)
        def _():
            if n_shards > 1:
                for p in range(per_block):
                    o_ref[p] = acc[:, p * ns:(p + 1) * ns].astype(out_dtype)
            else:
                o_ref[...] = acc[...].astype(out_dtype)

    if n_shards > 1:
        out_spec = pl.BlockSpec((per_block, bk, ns), lambda i, j, s: (j, i, 0))
        out_shape = jax.ShapeDtypeStruct((n_shards, k, ns), out_dtype)
    else:
        out_spec = pl.BlockSpec((bk, bn), lambda i, j, s: (i, j))
        out_shape = jax.ShapeDtypeStruct((k, n), out_dtype)
    return pl.pallas_call(
        body, name=name, grid=(k // bk, n // bn, steps),
        in_specs=[pl.BlockSpec((bt, bk), lambda i, j, s: (s, i)), pl.BlockSpec((bt, bn), lambda i, j, s: (s, j))],
        out_specs=out_spec, out_shape=out_shape, scratch_shapes=[pltpu.VMEM((bk, bn), F32)],
        compiler_params=_params(("parallel", "parallel", "arbitrary")),
    )(a, b)


def _two_level_gather(srcs, dsts, send_sems, recv_sems, local_sems):
    n = len(srcs)
    x, y, c = _mesh_pos()
    me, sibling = (x, y, c), (x, y, 1 - c)
    chips = [(1 - x, y), (x, 1 - y), (1 - x, 1 - y)]

    def slot(a, px, py, pc):
        return dsts[a].at[4 * px + 2 * py + pc]

    def copy(a, k, block, to, src=None):
        return pltpu.make_async_remote_copy(
            src_ref=slot(a, *block) if src is None else src, dst_ref=slot(a, *block),
            send_sem=send_sems.at[(N_DEV - 1) * a + k], recv_sem=recv_sems.at[(N_DEV - 1) * a + k],
            device_id=to, device_id_type=pl.DeviceIdType.MESH)

    def own_copies():
        mine = [pltpu.make_async_copy(srcs[a], slot(a, *me), local_sems.at[a]) for a in range(n)]
        first = []
        for a in range(n):
            first.append(copy(a, 0, me, sibling, src=srcs[a]))
            first += [copy(a, 1 + j, me, (*chip, c), src=srcs[a]) for j, chip in enumerate(chips)]
        return mine, first

    def start():
        mine, first = own_copies()
        for cp in mine + first:
            cp.start()

    def finish():
        mine, first = own_copies()
        passed = []
        for j, chip in enumerate(chips):
            for a in range(n):
                copy(a, 1 + j, (*chip, c), me).wait_recv()
                passed.append(copy(a, 4 + j, (*chip, c), sibling))
                passed[-1].start()
        for a in range(n):
            copy(a, 0, sibling, me).wait_recv()
        for j, chip in enumerate(chips):
            for a in range(n):
                copy(a, 4 + j, (*chip, 1 - c), me).wait_recv()
        for cp in first + passed:
            cp.wait_send()
        for cp in mine:
            cp.wait()

    return start, finish


def _gather_small(small):
    def body(s_ref, s_dst, *sems):
        smalls = _direct_copies([s_ref], [s_dst], *sems, True)
        _start_copies(*smalls)
        _wait_copies(*smalls)

    return pl.pallas_call(
        body, name="gather_small",
        out_shape=jax.ShapeDtypeStruct((N_DEV,) + small.shape, small.dtype),
        in_specs=[ANY_SPEC], out_specs=ANY_SPEC,
        scratch_shapes=_exchange_scratch(1),
    )(small)


def _adamw(parts, w, m, v, name):
    n_parts = len(parts)
    _, k, n = parts[0].shape
    bk = min(k, ADAM_ROWS)
    c1 = 1.0 - ADAM_B1 ** ADAM_STEP
    c2 = 1.0 - ADAM_B2 ** ADAM_STEP

    def body(*refs):
        p_refs, (w_ref, m_ref, v_ref, g_ref, d_ref, mo_ref, vo_ref) = refs[:n_parts], refs[n_parts:]
        g = p_refs[0][0].astype(F32)
        for p_ref in p_refs:
            for s in range(N_DEV):
                if p_ref is not p_refs[0] or s > 0:
                    g = g + p_ref[s].astype(F32)
        g_ref[0] = g
        m_new = ADAM_B1 * m_ref[0] + (1.0 - ADAM_B1) * g
        v_new = ADAM_B2 * v_ref[0] + (1.0 - ADAM_B2) * (g * g)
        mo_ref[0] = m_new
        vo_ref[0] = v_new
        m_hat = m_new / c1
        v_hat = v_new / c2
        d_ref[0] = -ADAM_LR * (m_hat / (jnp.sqrt(v_hat) + ADAM_EPS) + ADAM_WD * w_ref[0])

    blk = pl.BlockSpec((1, bk, n), lambda i: (0, i, 0))
    out = jax.ShapeDtypeStruct((1, k, n), F32)
    return pl.pallas_call(
        body, name=name, grid=(k // bk,),
        in_specs=[pl.BlockSpec((N_DEV, bk, n), lambda i: (0, i, 0))] * n_parts + [blk, blk, blk],
        out_specs=[blk] * 4, out_shape=[out] * 4,
        compiler_params=_params(("parallel",)),
    )(*parts, w, m, v)


def _adamw_small(parts, w, m, v):
    k = len(SMALL_LAYOUT)
    c1 = 1.0 - ADAM_B1 ** ADAM_STEP
    c2 = 1.0 - ADAM_B2 ** ADAM_STEP

    def body(p_ref, *refs):
        w_refs, m_refs, v_refs, outs = refs[:k], refs[k:2 * k], refs[2 * k:3 * k], refs[3 * k:]
        total = p_ref[0]
        for s in range(1, N_DEV):
            total = total + p_ref[s]
        for i, (_, row, off, width) in enumerate(SMALL_LAYOUT):
            g = total[row:row + 1, off:off + width]
            m_new = ADAM_B1 * m_refs[i][...] + (1.0 - ADAM_B1) * g
            v_new = ADAM_B2 * v_refs[i][...] + (1.0 - ADAM_B2) * (g * g)
            outs[4 * i][...] = g
            outs[4 * i + 1][...] = -ADAM_LR * ((m_new / c1) / (jnp.sqrt(v_new / c2) + ADAM_EPS)
                                               + ADAM_WD * w_refs[i][...])
            outs[4 * i + 2][...] = m_new
            outs[4 * i + 3][...] = v_new
        outs[4 * k][...] = total[SMALL_LOSS_ROW:SMALL_LOSS_ROW + 1, SMALL_LOSS_OFF:SMALL_LOSS_OFF + 1]

    names = [name for name, *_ in SMALL_LAYOUT]
    out_shape = [jax.ShapeDtypeStruct(w[name].shape, F32) for name in names for _ in range(4)]
    outs = pl.pallas_call(
        body, name="adamw_small", out_shape=out_shape + [jax.ShapeDtypeStruct((1, 1), F32)],
    )(parts, *[w[n] for n in names], *[m[n] for n in names], *[v[n] for n in names])
    return {name: tuple(outs[4 * i:4 * i + 4]) for i, name in enumerate(names)}, outs[4 * k]


def _pad_heads_cols(w, heads, width):
    k = w.shape[0]
    w = w.reshape(k, heads, width)
    return jnp.pad(w, ((0, 0), (0, 0), (0, SLAB - width))).reshape(k, heads * SLAB)


def _unpad_heads_cols(w, heads, width):
    k = w.shape[0]
    return w.reshape(k, heads, SLAB)[:, :, :width].reshape(k, heads * width)


def _pad_heads_rows(w, heads, width):
    n = w.shape[1]
    w = w.reshape(heads, width, n)
    return jnp.pad(w, ((0, 0), (0, SLAB - width), (0, 0))).reshape(heads * SLAB, n)


def _unpad_heads_rows(w, heads, width):
    n = w.shape[1]
    return w.reshape(heads, SLAB, n)[:, :width, :].reshape(heads * width, n)


def _pad_w_in(w_in):
    o = 2 * D_MODEL
    qa = _pad_heads_cols(w_in[:, o:o + 512], N_HEADS, HEAD_A)
    ka = _pad_heads_cols(w_in[:, o + 512:o + 640], N_KV_A, HEAD_A)
    va = _pad_heads_cols(w_in[:, o + 640:o + 768], N_KV_A, HEAD_A)
    kr = jnp.pad(w_in[:, o + 1152:o + 1184], ((0, 0), (QK_NOPE, SLAB - QK_NOPE - QK_ROPE)))
    return jnp.concatenate([w_in[:, :o], qa, ka, va, w_in[:, o + 768:o + 1152], kr], axis=1)


def _unpad_w_in(w):
    qa = _unpad_heads_cols(w[:, C_QA:C_KA], N_HEADS, HEAD_A)
    ka = _unpad_heads_cols(w[:, C_KA:C_VA], N_KV_A, HEAD_A)
    va = _unpad_heads_cols(w[:, C_VA:C_CQ], N_KV_A, HEAD_A)
    kr = w[:, C_KR + QK_NOPE:C_KR + QK_NOPE + QK_ROPE]
    return jnp.concatenate([w[:, :C_QA], qa, ka, va, w[:, C_CQ:C_KR], kr], axis=1)


def _pad_w_kvb(w_kvb):
    w = w_kvb.reshape(KV_LORA, N_HEADS, QK_NOPE + V_DIM_B)
    k = jnp.pad(w[:, :, :QK_NOPE], ((0, 0), (0, 0), (0, SLAB - QK_NOPE))).reshape(KV_LORA, HM)
    v = jnp.pad(w[:, :, QK_NOPE:], ((0, 0), (0, 0), (0, SLAB - V_DIM_B))).reshape(KV_LORA, HM)
    return jnp.concatenate([k, v], axis=1)


def _unpad_w_kvb(w):
    k = w[:, :HM].reshape(KV_LORA, N_HEADS, SLAB)[:, :, :QK_NOPE]
    v = w[:, HM:].reshape(KV_LORA, N_HEADS, SLAB)[:, :, :V_DIM_B]
    return jnp.concatenate([k, v], axis=2).reshape(KV_LORA, N_HEADS * (QK_NOPE + V_DIM_B))


def _col_shards(w):
    k, n = w.shape
    ns = n // N_DEV
    if ns % SLAB:
        return jnp.stack([w[:, d * ns:(d + 1) * ns] for d in range(N_DEV)])
    return w.reshape(k, N_DEV, ns).transpose(1, 0, 2)


def _from_col_shards(s):
    _, k, ns = s.shape
    if ns % SLAB:
        return jnp.concatenate([s[d] for d in range(N_DEV)], axis=1)
    return s.transpose(1, 0, 2).reshape(k, N_DEV * ns)


def _freq_row():
    freqs = ROPE_THETA ** (-jnp.arange(0, QK_ROPE, 2, dtype=F32) / QK_ROPE)
    return jnp.concatenate([jnp.zeros((QK_NOPE,), F32), freqs, freqs,
                            jnp.zeros((SLAB - QK_NOPE - QK_ROPE,), F32)]).reshape(1, SLAB)


SMALL_D_ROWS = ("pre_norm_mix", "post_norm_mix", "pre_norm_mlp", "post_norm_mlp")
SMALL_LAYOUT = tuple((name, i, 0, D_MODEL) for i, name in enumerate(SMALL_D_ROWS)) + (
    ("q_a_norm", 4, 0, Q_LORA), ("kv_a_norm", 4, 256, KV_LORA), ("sinks", 4, 384, N_HEADS))
SMALL_LOSS_ROW, SMALL_LOSS_OFF = 4, 512


def _pack_small(vals):
    row4 = jnp.concatenate([vals["q_a_norm"].reshape(-1), vals["kv_a_norm"].reshape(-1), vals["sinks"].reshape(-1),
                            jnp.zeros((SMALL_LOSS_OFF - 392,), F32), vals["loss"].reshape(-1),
                            jnp.zeros((1024 - SMALL_LOSS_OFF - 1,), F32)])
    rows = [vals[n].reshape(1024) for n in SMALL_D_ROWS] + [row4]
    return jnp.concatenate([jnp.stack(rows), jnp.zeros((SMALL_ROWS - 5, 1024), F32)], axis=0)


WEIGHT_ORDER = ("pre_norm_mix", "w_in", "q_a_norm", "w_q_b", "kv_a_norm", "w_kv_b", "sinks", "w_o_a", "w_o_b",
                "w_out", "post_norm_mix", "pre_norm_mlp", "w_up", "w_down", "post_norm_mlp")


def kernel(x, positions, pre_norm_mix, w_in, q_a_norm, w_q_b, kv_a_norm, w_kv_b, sinks, w_o_a, w_o_b, w_out, post_norm_mix, pre_norm_mlp, w_up, w_down, post_norm_mlp, loss_target, m_pre_norm_mix, m_w_in, m_q_a_norm, m_w_q_b, m_kv_a_norm, m_w_kv_b, m_sinks, m_w_o_a, m_w_o_b, m_w_out, m_post_norm_mix, m_pre_norm_mlp, m_w_up, m_w_down, m_post_norm_mlp, v_pre_norm_mix, v_w_in, v_q_a_norm, v_w_q_b, v_kv_a_norm, v_w_kv_b, v_sinks, v_w_o_a, v_w_o_b, v_w_out, v_post_norm_mix, v_pre_norm_mlp, v_w_up, v_w_down, v_post_norm_mlp):
    weights = dict(pre_norm_mix=pre_norm_mix, w_in=w_in, q_a_norm=q_a_norm, w_q_b=w_q_b, kv_a_norm=kv_a_norm,
                   w_kv_b=w_kv_b, sinks=sinks, w_o_a=w_o_a, w_o_b=w_o_b, w_out=w_out, post_norm_mix=post_norm_mix,
                   pre_norm_mlp=pre_norm_mlp, w_up=w_up, w_down=w_down, post_norm_mlp=post_norm_mlp)
    m_in = dict(pre_norm_mix=m_pre_norm_mix, w_in=m_w_in, q_a_norm=m_q_a_norm, w_q_b=m_w_q_b, kv_a_norm=m_kv_a_norm,
                w_kv_b=m_w_kv_b, sinks=m_sinks, w_o_a=m_w_o_a, w_o_b=m_w_o_b, w_out=m_w_out,
                post_norm_mix=m_post_norm_mix, pre_norm_mlp=m_pre_norm_mlp, w_up=m_w_up, w_down=m_w_down,
                post_norm_mlp=m_post_norm_mlp)
    v_in = dict(pre_norm_mix=v_pre_norm_mix, w_in=v_w_in, q_a_norm=v_q_a_norm, w_q_b=v_w_q_b, kv_a_norm=v_kv_a_norm,
                w_kv_b=v_w_kv_b, sinks=v_sinks, w_o_a=v_w_o_a, w_o_b=v_w_o_b, w_out=v_w_out,
                post_norm_mix=v_post_norm_mix, pre_norm_mlp=v_pre_norm_mlp, w_up=v_w_up, w_down=v_w_down,
                post_norm_mlp=v_post_norm_mlp)

    xs, pos, target = x[0], positions[0], loss_target[0]
    t = xs.shape[0]
    pos_col = pos.reshape(t, 1)
    pos_row = pos.reshape(1, t)
    g1, g2, g3, g4 = (weights[n] for n in SMALL_D_ROWS)
    g_q, g_kv = q_a_norm, kv_a_norm
    sink_vec = sinks.reshape(N_HEADS)
    shard = {n: weights[n][0].astype(BF16) for n in EARLY + LATE}

    tables, (e_in, e_qb, e_kvb) = _rope_tables(pos_col, _freq_row(), [shard[n] for n in EARLY])
    w_in_p = _pad_w_in(_from_col_shards(e_in))
    w_qb = _pad_heads_cols(_from_col_shards(e_qb), N_HEADS, QK_NOPE + QK_ROPE)
    w_kvb = _pad_w_kvb(_from_col_shards(e_kvb))

    (h, gates, qa, ka, va, cq, ckv, cqn, ckvn, kb, vb, qt, kt, vt) = _inproj_fwd(
        xs, g1, w_in_p, g_q, g_kv, w_kvb, w_qb.T, w_kvb[:, :HM].T, w_kvb[:, HM:].T, w_in_p[:, C_KR:].T, tables)
    out_a, lse_a = _swa_fwd(qa, ka, va, pos_col, pos_row, sink_vec)
    out_b, out_b_t, qt_lse, (l_oa, l_ob, l_out, w_up_s, l_down) = _mla_fwd(qt, kb, vt, [shard[n] for n in LATE])
    w_oa = _pad_heads_rows(_from_col_shards(l_oa), N_HEADS, HEAD_A)
    w_ob = _pad_heads_rows(_from_col_shards(l_ob), N_HEADS, V_DIM_B)
    w_out_f = l_out.reshape(D_MODEL, D_MODEL)
    w_down_f = l_down.reshape(D_FF, D_MODEL)

    oa_p, ob_p, merged, y, x1, h2 = _merge_fwd(out_a, out_b, gates, xs, w_oa, w_ob, w_out_f, g2, g3)
    a, du, dy2, dx1, loss, dg3, dg4 = _mlp_fwd_bwd(x1, h2, target, w_up_s, w_down_f, g3, g4)
    (dgates, d_oa, d_ob_t, dg2, dw_oa, dw_ob, dw_out) = _merge_bwd(
        dx1, y, gates, oa_p, ob_p, out_a, out_b, out_b_t, merged, w_oa, w_ob, w_out_f, g2)
    late_slices = [
        _col_shards(_unpad_heads_rows(dw_oa, N_HEADS, HEAD_A)).astype(BF16),
        _col_shards(_unpad_heads_rows(dw_ob, N_HEADS, V_DIM_B)).astype(BF16),
        dw_out.astype(BF16).reshape(N_DEV, D_MODEL // N_DEV, D_MODEL),
        _matmul_tn(h2, du, "dw_up", BF16, N_DEV),
        _matmul_tn(a, dy2, "dw_down", BF16).reshape(N_DEV, D_FF // N_DEV, D_MODEL),
    ]
    dqkv_a, dsink = _swa_bwd(qa, ka, va, out_a, d_oa, lse_a, pos_col, pos_row, sink_vec)
    dw_in_early = jnp.concatenate([_matmul_tn(h, dgates, "dw_in_gates"), _matmul_tn(h, dqkv_a, "dw_in_mixer_a"),
                                   jnp.zeros((D_MODEL, D_IN_PAD - C_CQ), F32)], axis=1)
    late_slices.append(_col_shards(_unpad_w_in(dw_in_early)).astype(BF16))
    dqb_t, dkb_t, dvb_t, late_parts = _mla_bwd(qt_lse, kb, kt, vb, d_ob_t, late_slices)
    w_in_early_parts = late_parts.pop()
    dproj_late, dgq, dgkv, dw_qb_t, dw_kvb_t = _latent_bwd(
        dqb_t, dkb_t, dvb_t, cq, ckv, cqn, ckvn, *tables[3:], g_q, g_kv, w_qb, w_kvb)
    dw_l = _matmul_tn(h, dproj_late, "dw_in_latents")
    late_cols = jnp.concatenate([dw_l[:, :Q_LORA + KV_LORA], dw_l[:, C_KR - C_CQ + QK_NOPE:C_KR - C_CQ + Q_HEAD_B]],
                                axis=1)
    shard_cols = w_in.shape[2]
    head = late_cols.shape[1] - shard_cols
    w_in_late = jnp.concatenate([
        jnp.zeros((N_DEV - 2, D_MODEL, shard_cols), F32),
        jnp.pad(late_cols[:, :head], ((0, 0), (shard_cols - head, 0)))[None], late_cols[:, head:][None]])
    early_slices = [
        w_in_late.astype(BF16),
        _col_shards(_unpad_heads_cols(dw_qb_t.T, N_HEADS, QK_NOPE + QK_ROPE)).astype(BF16),
        _col_shards(_unpad_w_kvb(dw_kvb_t.T)).astype(BF16),
    ]
    dx, dg1, early_parts = _inproj_bwd(dgates, dqkv_a, dproj_late, xs, dx1, g1, w_in_p, early_slices,
                                       only=[(N_DEV - 2, N_DEV - 1), None, None])
    small_grads = {"pre_norm_mix": dg1, "post_norm_mix": dg2, "pre_norm_mlp": dg3, "post_norm_mlp": dg4,
                   "q_a_norm": dgq, "kv_a_norm": dgkv, "sinks": dsink.reshape(N_HEADS, BLOCK).sum(axis=1),
                   "loss": loss[0, 0:1]}
    s_parts = _gather_small(_pack_small(small_grads))

    updates = {}
    all_parts = [[w_in_early_parts, early_parts[0]]] + [[p] for p in early_parts[1:] + late_parts]
    for name, parts in zip(EARLY + LATE, all_parts):
        outs = _adamw(parts, weights[name], m_in[name], v_in[name], "adamw_" + name)
        for kind, arr in zip(("g", "d", "m", "v"), outs):
            updates[kind, name] = arr
    small_out, loss_sum = _adamw_small(s_parts, weights, m_in, v_in)
    for name, outs in small_out.items():
        for kind, arr in zip(("g", "d", "m", "v"), outs):
            updates[kind, name] = arr
    results = [updates[kind, name] for kind in ("g", "d", "m", "v") for name in WEIGHT_ORDER]
    return (loss_sum.reshape(()), dx[None], *results)
```

```python
import functools

import numpy as np
import jax
import jax.numpy as jnp
from jax import lax
from jax.experimental import pallas as pl
from jax.experimental.pallas import tpu as pltpu

F32 = jnp.float32
BF16 = jnp.bfloat16

D_MODEL = 1024
D_FF = 4096
N_HEADS = 8
N_KV_A = 2
GROUP_A = N_HEADS // N_KV_A
HEAD_A = 64
QK_NOPE = 64
QK_ROPE = 32
V_DIM_B = 64
Q_LORA = 256
KV_LORA = 128
BLOCK = 128
SLAB = 128
ROPE_THETA = 10000.0
EPS = 1e-6
N_DEV = 8
NEG = -1e30

SCALE_A = HEAD_A ** -0.5
SCALE_B = (QK_NOPE + QK_ROPE) ** -0.5
LOG2E = 1.4426950408889634
SCORE_B = SCALE_B * LOG2E
MLA_HEADS_PER_STEP = 4
MLA_FWD_HEADS_PER_STEP = 8
Q_HEAD_B = QK_NOPE + QK_ROPE
ONES_ROWS = 16
SLOPES_A = tuple(2.0 ** (-8.0 * (h + 1) / N_HEADS) for h in range(N_HEADS))

ADAM_LR = 0.001
ADAM_B1 = 0.9
ADAM_B2 = 0.999
ADAM_EPS = 1e-08
ADAM_WD = 0.01
ADAM_STEP = 10

HM = N_HEADS * SLAB
C_GATES = 0
C_QA = 2 * D_MODEL
C_KA = C_QA + HM
C_VA = C_KA + N_KV_A * SLAB
C_CQ = C_VA + N_KV_A * SLAB
C_CKV = C_CQ + Q_LORA
C_KR = C_CKV + KV_LORA
D_IN_PAD = C_KR + SLAB

VMEM_LIMIT = 56 * 1024 * 1024
VMEM_LIMIT_MERGE_BWD = 60 * 1024 * 1024

EARLY = ("w_in", "w_q_b", "w_kv_b")
LATE = ("w_o_a", "w_o_b", "w_out", "w_up", "w_down")
ADAM_ROWS = 256
SMALL_ROWS = 8


def _token_tile(t):
    return min(256, t)


def _wide_token_tile(t):
    return min(512, t)


def _attn_tile(t):
    return 512 if t >= 2048 else 128


def _params(sem, vmem=VMEM_LIMIT):
    return pltpu.CompilerParams(dimension_semantics=sem, vmem_limit_bytes=vmem)


def _dot(a, b):
    return jnp.dot(a, b, preferred_element_type=F32)


def _dot_nt(a, b):
    return lax.dot_general(a, b, (((1,), (1,)), ((), ())), preferred_element_type=F32)


def _dot_tn(a, b):
    return lax.dot_general(a, b, (((0,), (0,)), ((), ())), preferred_element_type=F32)


def _rms_r(x):
    return lax.rsqrt(jnp.mean(x * x, axis=-1, keepdims=True) + EPS)


def _rms_bwd(x, r, g, dy):
    t = dy * g
    return r * t - x * (r * r * r) * jnp.mean(x * t, axis=-1, keepdims=True)


def _sigmoid(x):
    return 1.0 / (1.0 + jnp.exp(-x))


def _rope(x, c, s1, s2):
    return x * c + pltpu.roll(x, SLAB - 16, 1) * s1 + pltpu.roll(x, 16, 1) * s2


def _rope_bwd(d, c, s1, s2):
    return d * c + pltpu.roll(d * s1, 16, 1) + pltpu.roll(d * s2, SLAB - 16, 1)


def _roll_rows(x, shift):
    return jnp.concatenate([x[-shift:], x[:-shift]], axis=0)


def _rope_t(x, c, s1, s2):
    return x * c + _roll_rows(x, SLAB - 16) * s1 + _roll_rows(x, 16) * s2


def _rope_t_bwd(d, c, s1, s2):
    return d * c + _roll_rows(d * s1, 16) + _roll_rows(d * s2, SLAB - 16)


def _plant_rows(slab, row, vals):
    hi = vals.astype(BF16).astype(F32)
    lo = (vals - hi).astype(BF16).astype(F32)
    idx = lax.broadcasted_iota(jnp.int32, slab.shape, 0)
    return jnp.where(idx == row, -hi, jnp.where(idx == row + 1, -lo, slab))


def _row_spec(tm, n):
    return pl.BlockSpec((tm, n), lambda i: (i, 0))


def _col_spec(n, tm):
    return pl.BlockSpec((n, tm), lambda i: (0, i))


def _full_spec(shape):
    nd = len(shape)
    return pl.BlockSpec(shape, lambda i: (0,) * nd, pipeline_mode=pl.Buffered(1))


def _acc_rows(ref, val):
    @pl.when(pl.program_id(0) == 0)
    def _():
        ref[...] = jnp.zeros_like(ref)
    ref[...] += jnp.sum(val, axis=0, keepdims=True)


def _rope_tables(pos_col, freq_row, early):
    t = pos_col.shape[0]
    tm = _token_tile(t)
    n = len(early)

    def body(pos_ref, f_ref, *rest):
        shard_refs, (c_ref, s1_ref, s2_ref, ct_ref, s1t_ref, s2t_ref) = rest[:n], rest[n:n + 6]
        start, finish = _two_level_gather(shard_refs, rest[n + 6:2 * n + 6], *rest[2 * n + 6:])
        pl.when(pl.program_id(0) == 0)(start)
        ang = pos_ref[...].astype(F32) * f_ref[...]
        lane = lax.broadcasted_iota(jnp.int32, ang.shape, 1)
        s = jnp.sin(ang)
        c = jnp.cos(ang)
        s1 = jnp.where((lane >= 64) & (lane < 80), -s, 0.0)
        s2 = jnp.where((lane >= 80) & (lane < 96), s, 0.0)
        c_ref[...], s1_ref[...], s2_ref[...] = c, s1, s2
        ct_ref[...], s1t_ref[...], s2t_ref[...] = c.T, s1.T, s2.T
        pl.when(pl.program_id(0) == t // tm - 1)(finish)

    tab = jax.ShapeDtypeStruct((t, SLAB), F32)
    tabt = jax.ShapeDtypeStruct((SLAB, t), F32)
    outs = pl.pallas_call(
        body, name="rope_tables", grid=(t // tm,),
        in_specs=[_row_spec(tm, 1), _full_spec((1, SLAB))] + [ANY_SPEC] * n,
        out_specs=[_row_spec(tm, SLAB)] * 3 + [_col_spec(SLAB, tm)] * 3 + [ANY_SPEC] * n,
        out_shape=[tab] * 3 + [tabt] * 3 + [jax.ShapeDtypeStruct((N_DEV,) + a.shape, a.dtype) for a in early],
        scratch_shapes=_exchange_scratch(n),
        compiler_params=_params(("arbitrary",)),
    )(pos_col, freq_row, *early)
    return outs[:6], outs[6:]


def _inproj_fwd(x, g1, w_in, g_q, g_kv, w_kvb, w_qb_t, w_kb_t, w_vb_t, w_kr_t, tables):
    t = x.shape[0]
    tm = _wide_token_tile(t)

    def body(x_ref, g1_ref, win_ref, gq_ref, gkv_ref, wkvb_ref, wqbt_ref, wkbt_ref, wvbt_ref, wkrt_ref,
             c_ref, s1_ref, s2_ref, ct_ref, s1t_ref, s2t_ref,
             h_ref, gates_ref, qa_ref, ka_ref, va_ref, cq_ref, ckv_ref, cqn_ref, ckvn_ref,
             kb_ref, vb_ref, qt_ref, kt_ref, vt_ref):
        xv = x_ref[...]
        h = (xv * _rms_r(xv) * g1_ref[...]).astype(BF16)
        h_ref[...] = h
        proj = _dot(h, win_ref[...])
        gates_ref[...] = proj[:, C_GATES:C_QA].astype(BF16)
        qa_ref[...] = proj[:, C_QA:C_KA].astype(BF16)
        ka_ref[...] = proj[:, C_KA:C_VA].astype(BF16)
        va_ref[...] = proj[:, C_VA:C_CQ].astype(BF16)
        cq = proj[:, C_CQ:C_CKV]
        ckv = proj[:, C_CKV:C_KR]
        kr = proj[:, C_KR:D_IN_PAD]
        cq_ref[...] = cq
        ckv_ref[...] = ckv
        cqn = (cq * _rms_r(cq) * gq_ref[...]).astype(BF16)
        ckvn = (ckv * _rms_r(ckv) * gkv_ref[...]).astype(BF16)
        cqn_ref[...] = cqn
        ckvn_ref[...] = ckvn
        c, s1, s2 = c_ref[...], s1_ref[...], s2_ref[...]
        kvb = _dot(ckvn, wkvb_ref[...])
        kr_rot = _rope(kr, c, s1, s2)
        ct, s1t, s2t = ct_ref[...], s1t_ref[...], s2t_ref[...]
        q_t = _dot_nt(wqbt_ref[...], cqn)
        k_t = _dot_nt(wkbt_ref[...], ckvn)
        kr_t = _rope_t(_dot_nt(wkrt_ref[...], h), ct, s1t, s2t)
        k_lane = lax.broadcasted_iota(jnp.int32, (1, SLAB), 1)
        k_ones = jnp.where((k_lane == Q_HEAD_B) | (k_lane == Q_HEAD_B + 1), 1.0, 0.0)
        for hd in range(N_HEADS):
            sl = slice(hd * SLAB, (hd + 1) * SLAB)
            kb_ref[:, sl] = (kvb[:, sl] + kr_rot + k_ones).astype(BF16)
            qt_ref[sl, :] = (_rope_t(q_t[sl, :], ct, s1t, s2t) * SCORE_B).astype(BF16)
            kt_ref[sl, :] = (k_t[sl, :] + kr_t).astype(BF16)
        v_lane = lax.broadcasted_iota(jnp.int32, (1, HM), 1) & (SLAB - 1)
        v_ones = jnp.where((v_lane == V_DIM_B) | (v_lane == V_DIM_B + 1), 1.0, 0.0)
        vb_ref[...] = (kvb[:, HM:2 * HM] + v_ones).astype(BF16)
        pad_row = lax.broadcasted_iota(jnp.int32, (HM, 1), 0) & (SLAB - 1)
        ones_rows = jnp.where((pad_row >= V_DIM_B) & (pad_row < V_DIM_B + ONES_ROWS), 1.0, 0.0)
        vt_ref[...] = (_dot_nt(wvbt_ref[...], ckvn) + ones_rows).astype(BF16)

    def sds(n, dt):
        return jax.ShapeDtypeStruct((t, n), dt)

    outs = [(D_MODEL, BF16), (2 * D_MODEL, BF16), (HM, BF16), (N_KV_A * SLAB, BF16), (N_KV_A * SLAB, BF16),
            (Q_LORA, F32), (KV_LORA, F32), (Q_LORA, BF16), (KV_LORA, BF16), (HM, BF16), (HM, BF16)]
    tab, tabt = _row_spec(tm, SLAB), _col_spec(SLAB, tm)
    return pl.pallas_call(
        body, name="inproj_fwd", grid=(t // tm,),
        in_specs=[_row_spec(tm, D_MODEL), _full_spec((1, D_MODEL)), _full_spec((D_MODEL, D_IN_PAD)),
                  _full_spec((1, Q_LORA)), _full_spec((1, KV_LORA)), _full_spec((KV_LORA, 2 * HM)),
                  _full_spec((HM, Q_LORA)), _full_spec((HM, KV_LORA)), _full_spec((HM, KV_LORA)),
                  _full_spec((SLAB, D_MODEL)), tab, tab, tab, tabt, tabt, tabt],
        out_specs=[_row_spec(tm, n) for n, _ in outs] + [_col_spec(HM, tm)] * 3,
        out_shape=[sds(n, dt) for n, dt in outs] + [jax.ShapeDtypeStruct((HM, t), BF16)] * 3,
        compiler_params=_params(("parallel",)),
    )(x, g1, w_in, g_q, g_kv, w_kvb, w_qb_t, w_kb_t, w_vb_t, w_kr_t, *tables)


def _tile_group(a):
    return jnp.concatenate([a] * GROUP_A, axis=1)


def _swa_masks():
    row = lax.broadcasted_iota(jnp.int32, (BLOCK, GROUP_A * BLOCK), 0)
    col = lax.broadcasted_iota(jnp.int32, (BLOCK, GROUP_A * BLOCK), 1) & (BLOCK - 1)
    return row <= col, row > col


def _heads_beside(ref, g):
    return jnp.concatenate([ref[:, (g * GROUP_A + hh) * SLAB:(g * GROUP_A + hh + 1) * SLAB].T
                            for hh in range(GROUP_A)], axis=1)


def _rows_beside(ref, g):
    return jnp.concatenate([ref[g * GROUP_A + hh] for hh in range(GROUP_A)], axis=1)


def _swa_rows(sinks):
    slopes = jnp.repeat(jnp.asarray(SLOPES_A, F32).reshape(N_KV_A, GROUP_A, 1), BLOCK, axis=2)
    sink_rows = jnp.repeat(sinks.reshape(N_KV_A, GROUP_A, 1), BLOCK, axis=2)
    return slopes.reshape(N_KV_A, 1, GROUP_A * BLOCK), sink_rows.reshape(N_KV_A, 1, GROUP_A * BLOCK)


def _swa_fwd(qa, ka, va, pos_col, pos_row, sinks):
    t = qa.shape[0]
    nb = t // BLOCK
    gw = GROUP_A * BLOCK
    slope_rows, sink_rows = _swa_rows(sinks)

    def body(q_ref, kc_ref, kp_ref, vc_ref, vp_ref, pkc_ref, pkp_ref, pq_ref, slope_ref, sink_ref, o_ref, l_ref):
        i = pl.program_id(0)
        pq = pq_ref[...]
        dist_c = _tile_group(jnp.abs(pkc_ref[...] - pq).astype(F32))
        dist_p = _tile_group(jnp.abs(pkp_ref[...] - pq).astype(F32))
        mask_c, older = _swa_masks()
        mask_p = jnp.logical_and(older, i > 0)
        raw = []
        for g in range(N_KV_A):
            gs = slice(g * SLAB, (g + 1) * SLAB)
            x = _heads_beside(q_ref, g)
            raw.append((_dot(kc_ref[:, gs], x), _dot(kp_ref[:, gs], x)))
        for g in range(N_KV_A):
            gs = slice(g * SLAB, (g + 1) * SLAB)
            slope, sink = slope_ref[g], sink_ref[g]
            s_c = jnp.where(mask_c, raw[g][0] * SCALE_A - slope * dist_c, NEG)
            s_p = jnp.where(mask_p, raw[g][1] * SCALE_A - slope * dist_p, NEG)
            m = jnp.maximum(jnp.maximum(jnp.max(s_c, axis=0, keepdims=True),
                                        jnp.max(s_p, axis=0, keepdims=True)), sink)
            e_c = jnp.exp(s_c - m)
            e_p = jnp.exp(s_p - m)
            den = jnp.sum(e_c, axis=0, keepdims=True) + jnp.sum(e_p, axis=0, keepdims=True) + jnp.exp(sink - m)
            inv = 1.0 / den
            ot = (_dot_tn(vc_ref[:, gs], (e_c * inv).astype(BF16))
                  + _dot_tn(vp_ref[:, gs], (e_p * inv).astype(BF16)))
            lse = m + jnp.log(den)
            for hh in range(GROUP_A):
                hd = g * GROUP_A + hh
                seg = slice(hh * BLOCK, (hh + 1) * BLOCK)
                o_ref[:, hd * SLAB:(hd + 1) * SLAB] = ot[:, seg].T.astype(BF16)
                l_ref[hd] = lse[:, seg]

    cur = lambda i: (i, 0)
    prev = lambda i: (jnp.maximum(i - 1, 0), 0)
    kvw = N_KV_A * SLAB
    rows = pl.BlockSpec((N_KV_A, 1, gw), lambda i: (0, 0, 0))
    return pl.pallas_call(
        body, name="swa_fwd", grid=(nb,),
        in_specs=[pl.BlockSpec((BLOCK, HM), cur),
                  pl.BlockSpec((BLOCK, kvw), cur), pl.BlockSpec((BLOCK, kvw), prev),
                  pl.BlockSpec((BLOCK, kvw), cur), pl.BlockSpec((BLOCK, kvw), prev),
                  pl.BlockSpec((BLOCK, 1), cur), pl.BlockSpec((BLOCK, 1), prev),
                  pl.BlockSpec((1, BLOCK), lambda i: (0, i)), rows, rows],
        out_specs=[pl.BlockSpec((BLOCK, HM), cur), pl.BlockSpec((N_HEADS, 1, BLOCK), lambda i: (0, 0, i))],
        out_shape=[jax.ShapeDtypeStruct((t, HM), BF16), jax.ShapeDtypeStruct((N_HEADS, 1, t), F32)],
        compiler_params=_params(("parallel",)),
    )(qa, ka, ka, va, va, pos_col, pos_col, pos_row, slope_rows, sink_rows)


def _swa_bwd(qa, ka, va, out_a, d_oa, lse, pos_col, pos_row, sinks):
    t = qa.shape[0]
    nb = t // BLOCK
    gw = GROUP_A * BLOCK
    kvw = N_KV_A * SLAB
    slope_rows, sink_rows = _swa_rows(sinks)

    def body(q_ref, qn_ref, do_ref, don_ref, l_ref, ln_ref, o_ref, on_ref, kp_ref, kc_ref, vp_ref, vc_ref,
             pkp_ref, pkc_ref, pq_ref, pqn_ref, slope_ref, sink_ref, dqkv_ref, dsink_ref):
        j = pl.program_id(0)
        pkc, pkp = pkc_ref[...], pkp_ref[...]
        dist_cc = _tile_group(jnp.abs(pkc - pq_ref[...]).astype(F32))
        dist_cp = _tile_group(jnp.abs(pkp - pq_ref[...]).astype(F32))
        dist_nc = _tile_group(jnp.abs(pkc - pqn_ref[...]).astype(F32))
        mask_cc, older = _swa_masks()
        mask_cp = jnp.logical_and(older, j > 0)
        mask_nc = jnp.logical_and(older, j < nb - 1)

        @pl.when(j == 0)
        def _():
            dsink_ref[...] = jnp.zeros_like(dsink_ref)

        def tile(k, v, x, dox, lrow, drow, dist, mask, slope):
            s = jnp.where(mask, _dot(k, x) * SCALE_A - slope * dist, NEG)
            p = jnp.exp(s - lrow)
            ds = p * (_dot(v, dox) - drow)
            return p.astype(BF16), ds.astype(BF16)

        for g in range(N_KV_A):
            gs = slice(g * SLAB, (g + 1) * SLAB)
            kc, kp, vc, vp = kc_ref[:, gs], kp_ref[:, gs], vc_ref[:, gs], vp_ref[:, gs]
            slope, sink = slope_ref[g], sink_ref[g]
            x, xn = _heads_beside(q_ref, g), _heads_beside(qn_ref, g)
            dox, doxn = _heads_beside(do_ref, g), _heads_beside(don_ref, g)
            lrow, lrown = _rows_beside(l_ref, g), _rows_beside(ln_ref, g)
            drow = jnp.sum(dox.astype(F32) * _heads_beside(o_ref, g).astype(F32), axis=0, keepdims=True)
            drown = jnp.sum(doxn.astype(F32) * _heads_beside(on_ref, g).astype(F32), axis=0, keepdims=True)
            p_cc, ds_cc = tile(kc, vc, x, dox, lrow, drow, dist_cc, mask_cc, slope)
            _, ds_cp = tile(kp, vp, x, dox, lrow, drow, dist_cp, mask_cp, slope)
            p_nc, ds_nc = tile(kc, vc, xn, doxn, lrown, drown, dist_nc, mask_nc, slope)
            dqt = (_dot_tn(kc, ds_cc) + _dot_tn(kp, ds_cp)) * SCALE_A
            for hh in range(GROUP_A):
                hd = g * GROUP_A + hh
                dqkv_ref[:, hd * SLAB:(hd + 1) * SLAB] = dqt[:, hh * BLOCK:(hh + 1) * BLOCK].T.astype(BF16)
            dqkv_ref[:, HM + g * SLAB:HM + (g + 1) * SLAB] = (
                (_dot_nt(ds_cc, x) + _dot_nt(ds_nc, xn)) * SCALE_A).astype(BF16)
            dqkv_ref[:, HM + kvw + g * SLAB:HM + kvw + (g + 1) * SLAB] = (
                _dot_nt(p_cc, dox) + _dot_nt(p_nc, doxn)).astype(BF16)
            dsink_ref[g] -= jnp.exp(sink - lrow) * drow

    cur = lambda j: (j, 0)
    prev = lambda j: (jnp.maximum(j - 1, 0), 0)
    nxt = lambda j: (jnp.minimum(j + 1, nb - 1), 0)
    cur3 = lambda j: (0, 0, j)
    nxt3 = lambda j: (0, 0, jnp.minimum(j + 1, nb - 1))
    kvw = N_KV_A * SLAB
    rows = pl.BlockSpec((N_KV_A, 1, gw), lambda j: (0, 0, 0))
    stat = lambda im: pl.BlockSpec((N_HEADS, 1, BLOCK), im)
    return pl.pallas_call(
        body, name="swa_bwd", grid=(nb,),
        in_specs=[pl.BlockSpec((BLOCK, HM), cur), pl.BlockSpec((BLOCK, HM), nxt),
                  pl.BlockSpec((BLOCK, HM), cur), pl.BlockSpec((BLOCK, HM), nxt),
                  stat(cur3), stat(nxt3), pl.BlockSpec((BLOCK, HM), cur), pl.BlockSpec((BLOCK, HM), nxt),
                  pl.BlockSpec((BLOCK, kvw), prev), pl.BlockSpec((BLOCK, kvw), cur),
                  pl.BlockSpec((BLOCK, kvw), prev), pl.BlockSpec((BLOCK, kvw), cur),
                  pl.BlockSpec((BLOCK, 1), prev), pl.BlockSpec((BLOCK, 1), cur),
                  pl.BlockSpec((1, BLOCK), lambda j: (0, j)),
                  pl.BlockSpec((1, BLOCK), lambda j: (0, jnp.minimum(j + 1, nb - 1))), rows, rows],
        out_specs=[pl.BlockSpec((BLOCK, HM + 2 * kvw), cur), rows],
        out_shape=[jax.ShapeDtypeStruct((t, HM + 2 * kvw), BF16), jax.ShapeDtypeStruct((N_KV_A, 1, gw), F32)],
        compiler_params=_params(("arbitrary",)),
    )(qa, qa, d_oa, d_oa, lse, lse, out_a, out_a, ka, ka, va, va,
      pos_col, pos_col, pos_row, pos_row, slope_rows, sink_rows)


def _mesh_pos():
    return lax.axis_index("x"), lax.axis_index("y"), lax.axis_index("c")


def _flip(v, bit):
    return 1 - v if bit else v


def _direct_copies(srcs, dsts, send_sems, recv_sems, local_sems, gather, sem_base=0, only=None):
    x, y, c = _mesh_pos()
    me = 4 * x + 2 * y + c

    def among(idx, dests):
        ok = idx == dests[0]
        for d in dests[1:]:
            ok = jnp.logical_or(ok, idx == d)
        return ok

    local, remote = [], []
    for a, (src, dst) in enumerate(zip(srcs, dsts)):
        dests = None if only is None else only[a]
        recv_ok = None if dests is None else among(me, dests)
        local.append((pltpu.make_async_copy(src if gather else src.at[me], dst.at[me],
                                            local_sems.at[sem_base + a]), recv_ok))
        for r in range(1, N_DEV):
            px, py, pc = _flip(x, r & 4), _flip(y, r & 2), _flip(c, r & 1)
            peer = 4 * px + 2 * py + pc
            sem = (N_DEV - 1) * (sem_base + a) + r - 1
            copy = pltpu.make_async_remote_copy(
                src_ref=src if gather else src.at[peer], dst_ref=dst.at[me],
                send_sem=send_sems.at[sem], recv_sem=recv_sems.at[sem],
                device_id=(px, py, pc), device_id_type=pl.DeviceIdType.MESH)
            remote.append((copy, None if dests is None else among(peer, dests), recv_ok))
    return local, remote


def _when(cond, fn):
    if cond is None:
        fn()
    else:
        pl.when(cond)(fn)


def _start_copies(local, remote):
    for cp, ok in local:
        _when(ok, cp.start)
    for cp, send_ok, _ in remote:
        _when(send_ok, cp.start)


def _wait_copies(local, remote):
    for cp, _, recv_ok in remote:
        _when(recv_ok, cp.wait_recv)
    for cp, send_ok, _ in remote:
        _when(send_ok, cp.wait_send)
    for cp, ok in local:
        _when(ok, cp.wait)


def _exchange_scratch(n):
    return [pltpu.SemaphoreType.DMA((n * (N_DEV - 1),)), pltpu.SemaphoreType.DMA((n * (N_DEV - 1),)),
            pltpu.SemaphoreType.DMA((n,))]


ANY_SPEC = pl.BlockSpec(memory_space=pl.ANY)


def _mla_fwd(qt, kb, vt, late):
    t = kb.shape[0]
    tk = _attn_tile(t)
    ratio = 2 if t >= 2 * tk else 1
    tq = ratio * tk
    nq = t // tq
    hps = MLA_FWD_HEADS_PER_STEP
    w = hps * SLAB
    pairs = [(i, j) for i in range(nq) for j in range(ratio * (i + 1))]
    i_tab = jnp.asarray(np.array([p[0] for p in pairs], np.int32))
    j_tab = jnp.asarray(np.array([p[1] for p in pairs], np.int32))

    n_late = len(late)

    def body(it_ref, jt_ref, qt_ref, k_ref, vt_ref, *rest):
        late_refs, (o_ref, ot_ref, qa_ref) = rest[:n_late], rest[n_late:n_late + 3]
        gathered_refs = rest[n_late + 3:2 * n_late + 3]
        m_s, acc_s, send_sems, recv_sems, local_sems = rest[2 * n_late + 3:]
        n = pl.program_id(1)
        i, j = it_ref[n], jt_ref[n]
        first_step = jnp.logical_and(pl.program_id(0) == 0, n == 0)
        last_step = jnp.logical_and(pl.program_id(0) == N_HEADS // hps - 1, n == len(pairs) - 1)

        @pl.when(first_step)
        def _():
            _start_copies(*_direct_copies(late_refs, gathered_refs, send_sems, recv_sems, local_sems, True))

        @pl.when(j == 0)
        def _():
            m_s[...] = jnp.full_like(m_s, NEG)
            acc_s[...] = jnp.zeros_like(acc_s)

        def update(masked, q0):
            qc = slice(q0, tq)

            def scores(hh):
                sl = slice(hh * SLAB, (hh + 1) * SLAB)
                return _dot(k_ref[:, sl], qt_ref[sl, qc])

            def softmax(hh, s):
                if masked:
                    s = jnp.where(lax.broadcasted_iota(jnp.int32, s.shape, 0)
                                  <= lax.broadcasted_iota(jnp.int32, s.shape, 1), s, NEG)
                m_old = m_s[hh][:, qc]
                m_new = jnp.maximum(m_old, jnp.max(s, axis=0, keepdims=True))
                m_s[hh, :, qc] = m_new
                return jnp.exp2(s - m_new).astype(BF16), jnp.exp2(m_old - m_new)

            def accumulate(hh, p, alpha):
                sl = slice(hh * SLAB, hh * SLAB + V_DIM_B + ONES_ROWS)
                acc_s[sl, qc] = alpha * acc_s[sl, qc] + _dot(vt_ref[sl, :], p)

            s_next, pending = scores(0), None
            for hh in range(hps):
                s = s_next
                if hh + 1 < hps:
                    s_next = scores(hh + 1)
                p, alpha = softmax(hh, s)
                if pending is not None:
                    accumulate(*pending)
                pending = (hh, p, alpha)
            accumulate(*pending)

        @pl.when(j < ratio * i)
        def _():
            update(False, 0)

        for part in range(ratio):
            @pl.when(j == ratio * i + part)
            def _():
                update(True, part * tk)

        @pl.when(j == ratio * i + ratio - 1)
        def _():
            for hh in range(hps):
                sl = slice(hh * SLAB, (hh + 1) * SLAB)
                den = acc_s[hh * SLAB + V_DIM_B:hh * SLAB + V_DIM_B + 1, :]
                values = lax.broadcasted_iota(jnp.int32, (SLAB, tq), 0) < V_DIM_B
                ot = jnp.where(values, acc_s[sl, :] / den, 0.0)
                ot_ref[sl, :] = ot.astype(BF16)
                o_ref[:, sl] = ot.T.astype(BF16)
                lse = m_s[hh] + jnp.log2(den)
                qa_ref[sl, :] = _plant_rows(qt_ref[sl, :].astype(F32), Q_HEAD_B, lse).astype(BF16)

        @pl.when(last_step)
        def _():
            _wait_copies(*_direct_copies(late_refs, gathered_refs, send_sems, recv_sems, local_sems, True))

    grid_spec = pltpu.PrefetchScalarGridSpec(
        num_scalar_prefetch=2, grid=(N_HEADS // hps, len(pairs)),
        in_specs=[pl.BlockSpec((w, tq), lambda h, n, it, jt: (h, it[n])),
                  pl.BlockSpec((tk, w), lambda h, n, it, jt: (jt[n], h)),
                  pl.BlockSpec((w, tk), lambda h, n, it, jt: (h, jt[n]))] + [ANY_SPEC] * n_late,
        out_specs=[pl.BlockSpec((tq, w), lambda h, n, it, jt: (it[n], h)),
                   pl.BlockSpec((w, tq), lambda h, n, it, jt: (h, it[n])),
                   pl.BlockSpec((w, tq), lambda h, n, it, jt: (h, it[n]))] + [ANY_SPEC] * n_late,
        scratch_shapes=[pltpu.VMEM((hps, 1, tq), F32), pltpu.VMEM((w, tq), F32)] + _exchange_scratch(n_late))
    outs = pl.pallas_call(
        body, name="mla_fwd", grid_spec=grid_spec,
        out_shape=[jax.ShapeDtypeStruct((t, HM), BF16), jax.ShapeDtypeStruct((HM, t), BF16),
                   jax.ShapeDtypeStruct((HM, t), BF16)]
        + [jax.ShapeDtypeStruct((N_DEV,) + a.shape, a.dtype) for a in late],
        compiler_params=_params(("arbitrary", "arbitrary")),
    )(i_tab, j_tab, qt, kb, vt, *late)
    return outs[0], outs[1], outs[2], list(outs[3:])


def _mla_bwd(qt, kb, kt, vb, d_ob_t, grad_slices):
    t = kb.shape[0]
    tk = _attn_tile(t)
    ratio = 2 if t >= 2 * tk else 1
    tq = ratio * tk
    nk, nq = t // tk, t // tq
    hps = MLA_HEADS_PER_STEP
    w = hps * SLAB
    pairs = [(j, i) for j in range(nk) for i in range(j // ratio, nq)]
    j_tab = jnp.asarray(np.array([p[0] for p in pairs], np.int32))
    i_tab = jnp.asarray(np.array([p[1] for p in pairs], np.int32))

    n_ex = len(grad_slices)

    def body(jt_ref, it_ref, qt_ref, dot_ref, k_ref, kt_ref, v_ref, *rest):
        slice_refs, (dqt_ref, dkt_ref, dvt_ref) = rest[:n_ex], rest[n_ex:n_ex + 3]
        part_refs = rest[n_ex + 3:2 * n_ex + 3]
        dk_s, dv_s, send_sems, recv_sems, local_sems = rest[2 * n_ex + 3:]
        n = pl.program_id(1)
        j, i = jt_ref[n], it_ref[n]
        first_step = jnp.logical_and(pl.program_id(0) == 0, n == 0)
        last_step = jnp.logical_and(pl.program_id(0) == N_HEADS // hps - 1, n == len(pairs) - 1)

        @pl.when(first_step)
        def _():
            _start_copies(*_direct_copies(slice_refs, part_refs, send_sems, recv_sems, local_sems, False))

        @pl.when(n == 0)
        def _():
            dqt_ref[...] = jnp.zeros_like(dqt_ref)

        def update(diagonal, q0):
            qc = slice(q0, tq)
            cols = pl.ds(pl.multiple_of(i * tq + q0, tk), tq - q0)

            def softmax_bwd(hh, s, dp):
                if diagonal:
                    s = jnp.where(lax.broadcasted_iota(jnp.int32, s.shape, 0)
                                  <= lax.broadcasted_iota(jnp.int32, s.shape, 1), s, NEG)
                p = jnp.exp2(s)
                return p.astype(BF16), (p * dp).astype(BF16)

            def gradients(hh, p, ds):
                base = hh * SLAB
                vrows = slice(base, base + V_DIM_B)
                qrows = slice(base, base + QK_NOPE + QK_ROPE)
                dv = _dot_nt(dot_ref[vrows, qc], p)
                dk = _dot_nt(qt_ref[qrows, qc], ds)
                if diagonal:
                    dv_s[base:base + SLAB, :] = jnp.concatenate([dv, jnp.zeros((SLAB - V_DIM_B, tk), F32)], axis=0)
                    dk_s[base:base + SLAB, :] = jnp.concatenate(
                        [dk, jnp.zeros((SLAB - QK_NOPE - QK_ROPE, tk), F32)], axis=0)
                else:
                    dv_s[vrows, :] += dv
                    dk_s[qrows, :] += dk
                dqt_ref[qrows, cols] += _dot(kt_ref[qrows, :], ds)

            def scores(hh):
                sl = slice(hh * SLAB, (hh + 1) * SLAB)
                return _dot(k_ref[:, sl], qt_ref[sl, qc])

            def dprod(hh):
                sl = slice(hh * SLAB, (hh + 1) * SLAB)
                return _dot(v_ref[:, sl], dot_ref[sl, qc])

            s_next = scores(0)
            for hh in range(hps):
                s = s_next
                dp = dprod(hh)
                if hh + 1 < hps:
                    s_next = scores(hh + 1)
                gradients(hh, *softmax_bwd(hh, s, dp))

        first_tile = lax.div(j, ratio)
        for part in range(ratio):
            @pl.when(jnp.logical_and(i == first_tile, lax.rem(j, ratio) == part))
            def _():
                update(True, part * tk)

        @pl.when(i > first_tile)
        def _():
            update(False, 0)

        @pl.when(i == nq - 1)
        def _():
            dkt_ref[...] = (dk_s[...] * (1.0 / LOG2E)).astype(BF16)
            dvt_ref[...] = dv_s[...].astype(BF16)

        @pl.when(last_step)
        def _():
            _wait_copies(*_direct_copies(slice_refs, part_refs, send_sems, recv_sems, local_sems, False))

    grid_spec = pltpu.PrefetchScalarGridSpec(
        num_scalar_prefetch=2, grid=(N_HEADS // hps, len(pairs)),
        in_specs=[pl.BlockSpec((w, tq), lambda h, n, jt, it: (h, it[n])),
                  pl.BlockSpec((w, tq), lambda h, n, jt, it: (h, it[n])),
                  pl.BlockSpec((tk, w), lambda h, n, jt, it: (jt[n], h)),
                  pl.BlockSpec((w, tk), lambda h, n, jt, it: (h, jt[n])),
                  pl.BlockSpec((tk, w), lambda h, n, jt, it: (jt[n], h))] + [ANY_SPEC] * n_ex,
        out_specs=[pl.BlockSpec((w, t), lambda h, n, jt, it: (h, 0)),
                   pl.BlockSpec((w, tk), lambda h, n, jt, it: (h, jt[n])),
                   pl.BlockSpec((w, tk), lambda h, n, jt, it: (h, jt[n]))] + [ANY_SPEC] * n_ex,
        scratch_shapes=[pltpu.VMEM((w, tk), F32), pltpu.VMEM((w, tk), F32)] + _exchange_scratch(n_ex))
    outs = pl.pallas_call(
        body, name="mla_bwd", grid_spec=grid_spec,
        out_shape=[jax.ShapeDtypeStruct((HM, t), F32), jax.ShapeDtypeStruct((HM, t), BF16),
                   jax.ShapeDtypeStruct((HM, t), BF16)]
        + [jax.ShapeDtypeStruct(a.shape, a.dtype) for a in grad_slices],
        compiler_params=_params(("arbitrary", "arbitrary")),
    )(j_tab, i_tab, qt, d_ob_t, kb, kt, vb, *grad_slices)
    return outs[0], outs[1], outs[2], list(outs[3:])


def _merge_fwd(out_a, out_b, gates, x, w_oa, w_ob, w_out, g2, g3):
    t = x.shape[0]
    tm = _wide_token_tile(t)

    def body(oa_ref, ob_ref, gates_ref, x_ref, woa_ref, wob_ref, wout_ref, g2_ref, g3_ref,
             oap_ref, obp_ref, merged_ref, y_ref, x1_ref, h2_ref):
        oa_p = _dot(oa_ref[...], woa_ref[...])
        ob_p = _dot(ob_ref[...], wob_ref[...])
        oap_ref[...] = oa_p.astype(BF16)
        obp_ref[...] = ob_p.astype(BF16)
        sa = _sigmoid(gates_ref[:, 0:D_MODEL].astype(F32))
        sb = _sigmoid(gates_ref[:, D_MODEL:2 * D_MODEL].astype(F32))
        merged = (sa * oa_p + sb * ob_p).astype(BF16)
        merged_ref[...] = merged
        y = _dot(merged, wout_ref[...])
        y_ref[...] = y
        x1 = x_ref[...] + y * _rms_r(y) * g2_ref[...]
        x1_ref[...] = x1
        h2_ref[...] = (x1 * _rms_r(x1) * g3_ref[...]).astype(BF16)

    def sds(dt):
        return jax.ShapeDtypeStruct((t, D_MODEL), dt)

    row = _row_spec(tm, D_MODEL)
    return pl.pallas_call(
        body, name="merge_fwd", grid=(t // tm,),
        in_specs=[_row_spec(tm, HM), _row_spec(tm, HM), _row_spec(tm, 2 * D_MODEL), row,
                  _full_spec((HM, D_MODEL)), _full_spec((HM, D_MODEL)), _full_spec((D_MODEL, D_MODEL)),
                  _full_spec((1, D_MODEL)), _full_spec((1, D_MODEL))],
        out_specs=[row] * 6,
        out_shape=[sds(BF16), sds(BF16), sds(BF16), sds(F32), sds(F32), sds(BF16)],
        compiler_params=_params(("parallel",)),
    )(out_a, out_b, gates, x, w_oa, w_ob, w_out, g2, g3)


def _merge_bwd(dx1, y, gates, oa_p, ob_p, out_a, out_b, out_b_t, merged, w_oa, w_ob, w_out, g2):
    t = dx1.shape[0]
    tm = _wide_token_tile(t)

    def body(dx1_ref, y_ref, gates_ref, oap_ref, obp_ref, oa_ref, ob_ref, obt_ref, merged_ref,
             woa_ref, wob_ref, wout_ref, g2_ref,
             dgates_ref, doa_ref, dobt_ref, dg2_ref, dwoa_ref, dwob_ref, dwout_ref):
        @pl.when(pl.program_id(0) == 0)
        def _():
            dwoa_ref[...] = jnp.zeros_like(dwoa_ref)
            dwob_ref[...] = jnp.zeros_like(dwob_ref)
            dwout_ref[...] = jnp.zeros_like(dwout_ref)

        dx1v = dx1_ref[...]
        yv = y_ref[...]
        r2 = _rms_r(yv)
        _acc_rows(dg2_ref, dx1v * yv * r2)
        dy = _rms_bwd(yv, r2, g2_ref[...], dx1v).astype(BF16)
        dwout_ref[...] += _dot_tn(merged_ref[...], dy)
        dm = _dot_nt(dy, wout_ref[...])
        sa = _sigmoid(gates_ref[:, 0:D_MODEL].astype(F32))
        sb = _sigmoid(gates_ref[:, D_MODEL:2 * D_MODEL].astype(F32))
        d_oap = (dm * sa).astype(BF16)
        d_obp = (dm * sb).astype(BF16)
        dwoa_ref[...] += _dot_tn(oa_ref[...], d_oap)
        dwob_ref[...] += _dot_tn(ob_ref[...], d_obp)
        dgates_ref[:, 0:D_MODEL] = (dm * oap_ref[...].astype(F32) * sa * (1.0 - sa)).astype(BF16)
        dgates_ref[:, D_MODEL:2 * D_MODEL] = (dm * obp_ref[...].astype(F32) * sb * (1.0 - sb)).astype(BF16)
        doa_ref[...] = _dot_nt(d_oap, woa_ref[...]).astype(BF16)
        d_ob_t = _dot_nt(wob_ref[...], d_obp)
        for hd in range(N_HEADS):
            sl = slice(hd * SLAB, (hd + 1) * SLAB)
            delta = jnp.sum(d_ob_t[sl, :] * obt_ref[sl, :].astype(F32), axis=0, keepdims=True)
            dobt_ref[sl, :] = _plant_rows(d_ob_t[sl, :], V_DIM_B, delta).astype(BF16)

    def sds(n, dt):
        return jax.ShapeDtypeStruct((t, n), dt)

    row = _row_spec(tm, D_MODEL)
    return pl.pallas_call(
        body, name="merge_bwd", grid=(t // tm,),
        in_specs=[row, row, _row_spec(tm, 2 * D_MODEL), row, row, _row_spec(tm, HM), _row_spec(tm, HM),
                  _col_spec(HM, tm), row,
                  _full_spec((HM, D_MODEL)), _full_spec((HM, D_MODEL)), _full_spec((D_MODEL, D_MODEL)),
                  _full_spec((1, D_MODEL))],
        out_specs=[_row_spec(tm, 2 * D_MODEL), _row_spec(tm, HM), _col_spec(HM, tm), _full_spec((1, D_MODEL)),
                   _full_spec((HM, D_MODEL)), _full_spec((HM, D_MODEL)), _full_spec((D_MODEL, D_MODEL))],
        out_shape=[sds(2 * D_MODEL, BF16), sds(HM, BF16), jax.ShapeDtypeStruct((HM, t), BF16),
                   jax.ShapeDtypeStruct((1, D_MODEL), F32),
                   jax.ShapeDtypeStruct((HM, D_MODEL), F32), jax.ShapeDtypeStruct((HM, D_MODEL), F32),
                   jax.ShapeDtypeStruct((D_MODEL, D_MODEL), F32)],
        compiler_params=_params(("arbitrary",), VMEM_LIMIT_MERGE_BWD),
    )(dx1, y, gates, oa_p, ob_p, out_a, out_b, out_b_t, merged, w_oa, w_ob, w_out, g2)


def _mlp_fwd_bwd(x1, h2, target, w_up, w_down, g3, g4):
    t = x1.shape[0]
    tm = _token_tile(t)
    fs = D_FF // N_DEV

    def body(x1_ref, h2_ref, tgt_ref, wup_ref, wdown_ref, g3_ref, g4_ref,
             a_ref, du_ref, dy2_ref, dx1_ref, loss_ref, dg3_ref, dg4_ref):
        x1v = x1_ref[...]
        h2v = h2_ref[...]
        u = jnp.concatenate([_dot(h2v, wup_ref[s]) for s in range(N_DEV)], axis=1)
        ru = jnp.maximum(u, 0.0)
        a = (ru * ru).astype(BF16)
        a_ref[...] = a
        y2 = _dot(a, wdown_ref[...])
        r4 = _rms_r(y2)
        diff = x1v + y2 * r4 * g4_ref[...] - tgt_ref[...]
        _acc_rows(loss_ref, jnp.sum(diff * diff, axis=-1, keepdims=True) * (0.5 / D_MODEL)
                  * jnp.ones((1, SLAB), F32))
        dx2 = diff * (1.0 / D_MODEL)
        _acc_rows(dg4_ref, dx2 * y2 * r4)
        dy2 = _rms_bwd(y2, r4, g4_ref[...], dx2).astype(BF16)
        dy2_ref[...] = dy2
        du = (_dot_nt(dy2, wdown_ref[...]) * (2.0 * ru)).astype(BF16)
        du_ref[...] = du
        dh2 = _dot_nt(du[:, 0:fs], wup_ref[0])
        for s in range(1, N_DEV):
            dh2 += _dot_nt(du[:, s * fs:(s + 1) * fs], wup_ref[s])
        r3 = _rms_r(x1v)
        _acc_rows(dg3_ref, dh2 * x1v * r3)
        dx1_ref[...] = dx2 + _rms_bwd(x1v, r3, g3_ref[...], dh2)

    row = _row_spec(tm, D_MODEL)
    frow = _row_spec(tm, D_FF)
    vec = _full_spec((1, D_MODEL))
    return pl.pallas_call(
        body, name="mlp_fwd_bwd", grid=(t // tm,),
        in_specs=[row, row, row, _full_spec((N_DEV, D_MODEL, fs)), _full_spec((D_FF, D_MODEL)), vec, vec],
        out_specs=[frow, frow, row, row, _full_spec((1, SLAB)), vec, vec],
        out_shape=[jax.ShapeDtypeStruct((t, D_FF), BF16), jax.ShapeDtypeStruct((t, D_FF), BF16),
                   jax.ShapeDtypeStruct((t, D_MODEL), BF16), jax.ShapeDtypeStruct((t, D_MODEL), F32),
                   jax.ShapeDtypeStruct((1, SLAB), F32), jax.ShapeDtypeStruct((1, D_MODEL), F32),
                   jax.ShapeDtypeStruct((1, D_MODEL), F32)],
        compiler_params=_params(("arbitrary",)),
    )(x1, h2, target, w_up, w_down, g3, g4)


def _latent_bwd(dqb_t, dkb_t, dvb_t, cq, ckv, cqn, ckvn, rope_ct, rope_s1t, rope_s2t, g_q, g_kv, w_qb, w_kvb):
    t = cq.shape[0]
    tm = _wide_token_tile(t)

    def body(dqt_ref, dkt_ref, dvt_ref, cq_ref, ckv_ref, cqn_ref, ckvn_ref, ct_ref, s1t_ref, s2t_ref,
             gq_ref, gkv_ref, wqb_ref, wkvb_ref,
             dlate_ref, dgq_ref, dgkv_ref, dwqb_ref, dwkvb_ref, dqbrt_ref, dkvbt_ref):
        @pl.when(pl.program_id(0) == 0)
        def _():
            dwqb_ref[...] = jnp.zeros_like(dwqb_ref)
            dwkvb_ref[...] = jnp.zeros_like(dwkvb_ref)

        ct, s1t, s2t = ct_ref[...], s1t_ref[...], s2t_ref[...]
        dk_sum_t = jnp.zeros((SLAB, tm), F32)
        for hd in range(N_HEADS):
            sl = slice(hd * SLAB, (hd + 1) * SLAB)
            dqbrt_ref[sl, :] = _rope_t_bwd(dqt_ref[sl, :] * SCALE_B, ct, s1t, s2t).astype(BF16)
            dk_sum_t += dkt_ref[sl, :].astype(F32)
        dkvbt_ref[0:HM, :] = dkt_ref[...]
        dkvbt_ref[HM:2 * HM, :] = dvt_ref[...]
        dkr = _rope_t_bwd(dk_sum_t, ct, s1t, s2t).T
        dwqb_ref[...] += _dot(dqbrt_ref[...], cqn_ref[...])
        dwkvb_ref[...] += _dot(dkvbt_ref[...], ckvn_ref[...])
        dcqn = _dot(wqb_ref[...], dqbrt_ref[...]).T
        cq = cq_ref[...]
        rq = _rms_r(cq)
        _acc_rows(dgq_ref, dcqn * cq * rq)
        dcq = _rms_bwd(cq, rq, gq_ref[...], dcqn)
        dckvn = _dot(wkvb_ref[...], dkvbt_ref[...]).T
        ckv = ckv_ref[...]
        rkv = _rms_r(ckv)
        _acc_rows(dgkv_ref, dckvn * ckv * rkv)
        dckv = _rms_bwd(ckv, rkv, gkv_ref[...], dckvn)
        dlate_ref[:, 0:C_CKV - C_CQ] = dcq.astype(BF16)
        dlate_ref[:, C_CKV - C_CQ:C_KR - C_CQ] = dckv.astype(BF16)
        dlate_ref[:, C_KR - C_CQ:D_IN_PAD - C_CQ] = dkr.astype(BF16)

    hmt = _col_spec(HM, tm)
    tab = _col_spec(SLAB, tm)
    return pl.pallas_call(
        body, name="latent_bwd", grid=(t // tm,),
        in_specs=[hmt, hmt, hmt,
                  _row_spec(tm, Q_LORA), _row_spec(tm, KV_LORA), _row_spec(tm, Q_LORA), _row_spec(tm, KV_LORA),
                  tab, tab, tab, _full_spec((1, Q_LORA)), _full_spec((1, KV_LORA)),
                  _full_spec((Q_LORA, HM)), _full_spec((KV_LORA, 2 * HM))],
        out_specs=[_row_spec(tm, D_IN_PAD - C_CQ), _full_spec((1, Q_LORA)), _full_spec((1, KV_LORA)),
                   _full_spec((HM, Q_LORA)), _full_spec((2 * HM, KV_LORA))],
        out_shape=[jax.ShapeDtypeStruct((t, D_IN_PAD - C_CQ), BF16),
                   jax.ShapeDtypeStruct((1, Q_LORA), F32), jax.ShapeDtypeStruct((1, KV_LORA), F32),
                   jax.ShapeDtypeStruct((HM, Q_LORA), F32), jax.ShapeDtypeStruct((2 * HM, KV_LORA), F32)],
        scratch_shapes=[pltpu.VMEM((HM, tm), BF16), pltpu.VMEM((2 * HM, tm), BF16)],
        compiler_params=_params(("arbitrary",)),
    )(dqb_t, dkb_t, dvb_t, cq, ckv, cqn, ckvn, rope_ct, rope_s1t, rope_s2t, g_q, g_kv, w_qb, w_kvb)


def _inproj_bwd(dgates, dqkv, dlate, x, dx1, g1, w_in, grad_slices, only):
    t = x.shape[0]
    tm = _wide_token_tile(t)
    n_ex = len(grad_slices)
    zeroed = [a for a in range(n_ex) if only[a] is not None]

    def body(dgates_ref, dqkv_ref, dlate_ref, x_ref, dx1_ref, g1_ref, win_ref, *rest):
        slice_refs = rest[:n_ex]
        dx_ref, dg1_ref = rest[n_ex:n_ex + 2]
        part_refs = rest[n_ex + 2:2 * n_ex + 2]
        dproj_ref, send_sems, recv_sems, local_sems = rest[2 * n_ex + 2:2 * n_ex + 6]
        zero_refs, zero_sem = rest[2 * n_ex + 6:-1], rest[-1]

        @pl.when(pl.program_id(0) == 0)
        def _():
            _start_copies(*_direct_copies(slice_refs, part_refs, send_sems, recv_sems, local_sems, False,
                                          only=only))
            x_, y_, c_ = _mesh_pos()
            me = 4 * x_ + 2 * y_ + c_
            for a, z_ref in zip(zeroed, zero_refs):
                outside = me != only[a][0]
                for d in only[a][1:]:
                    outside = jnp.logical_and(outside, me != d)

                @pl.when(outside)
                def _():
                    z_ref[...] = jnp.zeros_like(z_ref)
                    fills = [pltpu.make_async_copy(z_ref, part_refs[a].at[k], zero_sem.at[k])
                             for k in range(N_DEV)]
                    for cp in fills:
                        cp.start()
                    for cp in fills:
                        cp.wait()

        dproj_ref[:, C_GATES:C_QA] = dgates_ref[...]
        dproj_ref[:, C_QA:C_CQ] = dqkv_ref[...]
        dproj_ref[:, C_CQ:D_IN_PAD] = dlate_ref[...]
        dh = _dot_nt(dproj_ref[...], win_ref[...])
        xv = x_ref[...]
        r1 = _rms_r(xv)
        _acc_rows(dg1_ref, dh * xv * r1)
        dx_ref[...] = dx1_ref[...] + _rms_bwd(xv, r1, g1_ref[...], dh)

        @pl.when(pl.program_id(0) == t // tm - 1)
        def _():
            _wait_copies(*_direct_copies(slice_refs, part_refs, send_sems, recv_sems, local_sems, False,
                                         only=only))

    kvw = N_KV_A * SLAB
    row = _row_spec(tm, D_MODEL)
    outs = pl.pallas_call(
        body, name="inproj_bwd", grid=(t // tm,),
        in_specs=[_row_spec(tm, 2 * D_MODEL), _row_spec(tm, HM + 2 * kvw), _row_spec(tm, D_IN_PAD - C_CQ),
                  row, row, _full_spec((1, D_MODEL)), _full_spec((D_MODEL, D_IN_PAD))]
        + [ANY_SPEC] * n_ex,
        out_specs=[row, _full_spec((1, D_MODEL))] + [ANY_SPEC] * n_ex,
        out_shape=[jax.ShapeDtypeStruct((t, D_MODEL), F32), jax.ShapeDtypeStruct((1, D_MODEL), F32)]
        + [jax.ShapeDtypeStruct(a.shape, a.dtype) for a in grad_slices],
        scratch_shapes=[pltpu.VMEM((tm, D_IN_PAD), BF16)] + _exchange_scratch(n_ex)
        + [pltpu.VMEM(grad_slices[a].shape[1:], grad_slices[a].dtype) for a in zeroed]
        + [pltpu.SemaphoreType.DMA((N_DEV,))],
        compiler_params=_params(("arbitrary",)),
    )(dgates, dqkv, dlate, x, dx1, g1, w_in, *grad_slices)
    return outs[0], outs[1], list(outs[2:])


def _matmul_tn(a, b, name, out_dtype=F32, n_shards=1):
    t, k = a.shape
    n = b.shape[1]
    bn = min(n, 2048)
    bt = min(t, 512 * max(1, 1024 // bn))
    bk = min(k, 2048 * 1024 // bn)
    ns = n // n_shards
    per_block = bn // ns
    steps = t // bt

    def body(a_ref, b_ref, o_ref, acc):
        s = pl.program_id(2)

        @pl.when(s == 0)
        def _():
            acc[...] = jnp.zeros_like(acc)

        acc[...] += _dot_tn(a_ref[...], b_ref[...])

        @pl.when(s == steps - 1)
        def _():
            if n_shards > 1:
                for p in range(per_block):
                    o_ref[p] = acc[:, p * ns:(p + 1) * ns].astype(out_dtype)
            else:
                o_ref[...] = acc[...].astype(out_dtype)

    if n_shards > 1:
        out_spec = pl.BlockSpec((per_block, bk, ns), lambda i, j, s: (j, i, 0))
        out_shape = jax.ShapeDtypeStruct((n_shards, k, ns), out_dtype)
    else:
        out_spec = pl.BlockSpec((bk, bn), lambda i, j, s: (i, j))
        out_shape = jax.ShapeDtypeStruct((k, n), out_dtype)
    return pl.pallas_call(
        body, name=name, grid=(k // bk, n // bn, steps),
        in_specs=[pl.BlockSpec((bt, bk), lambda i, j, s: (s, i)), pl.BlockSpec((bt, bn), lambda i, j, s: (s, j))],
        out_specs=out_spec, out_shape=out_shape, scratch_shapes=[pltpu.VMEM((bk, bn), F32)],
        compiler_params=_params(("parallel", "parallel", "arbitrary")),
    )(a, b)


def _two_level_gather(srcs, dsts, send_sems, recv_sems, local_sems):
    n = len(srcs)
    x, y, c = _mesh_pos()
    me, sibling = (x, y, c), (x, y, 1 - c)
    chips = [(1 - x, y), (x, 1 - y), (1 - x, 1 - y)]

    def slot(a, px, py, pc):
        return dsts[a].at[4 * px + 2 * py + pc]

    def copy(a, k, block, to, src=None):
        return pltpu.make_async_remote_copy(
            src_ref=slot(a, *block) if src is None else src, dst_ref=slot(a, *block),
            send_sem=send_sems.at[(N_DEV - 1) * a + k], recv_sem=recv_sems.at[(N_DEV - 1) * a + k],
            device_id=to, device_id_type=pl.DeviceIdType.MESH)

    def own_copies():
        mine = [pltpu.make_async_copy(srcs[a], slot(a, *me), local_sems.at[a]) for a in range(n)]
        first = []
        for a in range(n):
            first.append(copy(a, 0, me, sibling, src=srcs[a]))
            first += [copy(a, 1 + j, me, (*chip, c), src=srcs[a]) for j, chip in enumerate(chips)]
        return mine, first

    def start():
        mine, first = own_copies()
        for cp in mine + first:
            cp.start()

    def finish():
        mine, first = own_copies()
        passed = []
        for j, chip in enumerate(chips):
            for a in range(n):
                copy(a, 1 + j, (*chip, c), me).wait_recv()
                passed.append(copy(a, 4 + j, (*chip, c), sibling))
                passed[-1].start()
        for a in range(n):
            copy(a, 0, sibling, me).wait_recv()
        for j, chip in enumerate(chips):
            for a in range(n):
                copy(a, 4 + j, (*chip, 1 - c), me).wait_recv()
        for cp in first + passed:
            cp.wait_send()
        for cp in mine:
            cp.wait()

    return start, finish


def _gather_small(small):
    def body(s_ref, s_dst, *sems):
        smalls = _direct_copies([s_ref], [s_dst], *sems, True)
        _start_copies(*smalls)
        _wait_copies(*smalls)

    return pl.pallas_call(
        body, name="gather_small",
        out_shape=jax.ShapeDtypeStruct((N_DEV,) + small.shape, small.dtype),
        in_specs=[ANY_SPEC], out_specs=ANY_SPEC,
        scratch_shapes=_exchange_scratch(1),
    )(small)


def _adamw(parts, w, m, v, name):
    n_parts = len(parts)
    _, k, n = parts[0].shape
    bk = min(k, ADAM_ROWS)
    c1 = 1.0 - ADAM_B1 ** ADAM_STEP
    c2 = 1.0 - ADAM_B2 ** ADAM_STEP

    def body(*refs):
        p_refs, (w_ref, m_ref, v_ref, g_ref, d_ref, mo_ref, vo_ref) = refs[:n_parts], refs[n_parts:]
        g = p_refs[0][0].astype(F32)
        for p_ref in p_refs:
            for s in range(N_DEV):
                if p_ref is not p_refs[0] or s > 0:
                    g = g + p_ref[s].astype(F32)
        g_ref[0] = g
        m_new = ADAM_B1 * m_ref[0] + (1.0 - ADAM_B1) * g
        v_new = ADAM_B2 * v_ref[0] + (1.0 - ADAM_B2) * (g * g)
        mo_ref[0] = m_new
        vo_ref[0] = v_new
        m_hat = m_new / c1
        v_hat = v_new / c2
        d_ref[0] = -ADAM_LR * (m_hat / (jnp.sqrt(v_hat) + ADAM_EPS) + ADAM_WD * w_ref[0])

    blk = pl.BlockSpec((1, bk, n), lambda i: (0, i, 0))
    out = jax.ShapeDtypeStruct((1, k, n), F32)
    return pl.pallas_call(
        body, name=name, grid=(k // bk,),
        in_specs=[pl.BlockSpec((N_DEV, bk, n), lambda i: (0, i, 0))] * n_parts + [blk, blk, blk],
        out_specs=[blk] * 4, out_shape=[out] * 4,
        compiler_params=_params(("parallel",)),
    )(*parts, w, m, v)


def _adamw_small(parts, w, m, v):
    k = len(SMALL_LAYOUT)
    c1 = 1.0 - ADAM_B1 ** ADAM_STEP
    c2 = 1.0 - ADAM_B2 ** ADAM_STEP

    def body(p_ref, *refs):
        w_refs, m_refs, v_refs, outs = refs[:k], refs[k:2 * k], refs[2 * k:3 * k], refs[3 * k:]
        total = p_ref[0]
        for s in range(1, N_DEV):
            total = total + p_ref[s]
        for i, (_, row, off, width) in enumerate(SMALL_LAYOUT):
            g = total[row:row + 1, off:off + width]
            m_new = ADAM_B1 * m_refs[i][...] + (1.0 - ADAM_B1) * g
            v_new = ADAM_B2 * v_refs[i][...] + (1.0 - ADAM_B2) * (g * g)
            outs[4 * i][...] = g
            outs[4 * i + 1][...] = -ADAM_LR * ((m_new / c1) / (jnp.sqrt(v_new / c2) + ADAM_EPS)
                                               + ADAM_WD * w_refs[i][...])
            outs[4 * i + 2][...] = m_new
            outs[4 * i + 3][...] = v_new
        outs[4 * k][...] = total[SMALL_LOSS_ROW:SMALL_LOSS_ROW + 1, SMALL_LOSS_OFF:SMALL_LOSS_OFF + 1]

    names = [name for name, *_ in SMALL_LAYOUT]
    out_shape = [jax.ShapeDtypeStruct(w[name].shape, F32) for name in names for _ in range(4)]
    outs = pl.pallas_call(
        body, name="adamw_small", out_shape=out_shape + [jax.ShapeDtypeStruct((1, 1), F32)],
    )(parts, *[w[n] for n in names], *[m[n] for n in names], *[v[n] for n in names])
    return {name: tuple(outs[4 * i:4 * i + 4]) for i, name in enumerate(names)}, outs[4 * k]


def _pad_heads_cols(w, heads, width):
    k = w.shape[0]
    w = w.reshape(k, heads, width)
    return jnp.pad(w, ((0, 0), (0, 0), (0, SLAB - width))).reshape(k, heads * SLAB)


def _unpad_heads_cols(w, heads, width):
    k = w.shape[0]
    return w.reshape(k, heads, SLAB)[:, :, :width].reshape(k, heads * width)


def _pad_heads_rows(w, heads, width):
    n = w.shape[1]
    w = w.reshape(heads, width, n)
    return jnp.pad(w, ((0, 0), (0, SLAB - width), (0, 0))).reshape(heads * SLAB, n)


def _unpad_heads_rows(w, heads, width):
    n = w.shape[1]
    return w.reshape(heads, SLAB, n)[:, :width, :].reshape(heads * width, n)


def _pad_w_in(w_in):
    o = 2 * D_MODEL
    qa = _pad_heads_cols(w_in[:, o:o + 512], N_HEADS, HEAD_A)
    ka = _pad_heads_cols(w_in[:, o + 512:o + 640], N_KV_A, HEAD_A)
    va = _pad_heads_cols(w_in[:, o + 640:o + 768], N_KV_A, HEAD_A)
    kr = jnp.pad(w_in[:, o + 1152:o + 1184], ((0, 0), (QK_NOPE, SLAB - QK_NOPE - QK_ROPE)))
    return jnp.concatenate([w_in[:, :o], qa, ka, va, w_in[:, o + 768:o + 1152], kr], axis=1)


def _unpad_w_in(w):
    qa = _unpad_heads_cols(w[:, C_QA:C_KA], N_HEADS, HEAD_A)
    ka = _unpad_heads_cols(w[:, C_KA:C_VA], N_KV_A, HEAD_A)
    va = _unpad_heads_cols(w[:, C_VA:C_CQ], N_KV_A, HEAD_A)
    kr = w[:, C_KR + QK_NOPE:C_KR + QK_NOPE + QK_ROPE]
    return jnp.concatenate([w[:, :C_QA], qa, ka, va, w[:, C_CQ:C_KR], kr], axis=1)


def _pad_w_kvb(w_kvb):
    w = w_kvb.reshape(KV_LORA, N_HEADS, QK_NOPE + V_DIM_B)
    k = jnp.pad(w[:, :, :QK_NOPE], ((0, 0), (0, 0), (0, SLAB - QK_NOPE))).reshape(KV_LORA, HM)
    v = jnp.pad(w[:, :, QK_NOPE:], ((0, 0), (0, 0), (0, SLAB - V_DIM_B))).reshape(KV_LORA, HM)
    return jnp.concatenate([k, v], axis=1)


def _unpad_w_kvb(w):
    k = w[:, :HM].reshape(KV_LORA, N_HEADS, SLAB)[:, :, :QK_NOPE]
    v = w[:, HM:].reshape(KV_LORA, N_HEADS, SLAB)[:, :, :V_DIM_B]
    return jnp.concatenate([k, v], axis=2).reshape(KV_LORA, N_HEADS * (QK_NOPE + V_DIM_B))


def _col_shards(w):
    k, n = w.shape
    ns = n // N_DEV
    if ns % SLAB:
        return jnp.stack([w[:, d * ns:(d + 1) * ns] for d in range(N_DEV)])
    return w.reshape(k, N_DEV, ns).transpose(1, 0, 2)


def _from_col_shards(s):
    _, k, ns = s.shape
    if ns % SLAB:
        return jnp.concatenate([s[d] for d in range(N_DEV)], axis=1)
    return s.transpose(1, 0, 2).reshape(k, N_DEV * ns)


def _freq_row():
    freqs = ROPE_THETA ** (-jnp.arange(0, QK_ROPE, 2, dtype=F32) / QK_ROPE)
    return jnp.concatenate([jnp.zeros((QK_NOPE,), F32), freqs, freqs,
                            jnp.zeros((SLAB - QK_NOPE - QK_ROPE,), F32)]).reshape(1, SLAB)


SMALL_D_ROWS = ("pre_norm_mix", "post_norm_mix", "pre_norm_mlp", "post_norm_mlp")
SMALL_LAYOUT = tuple((name, i, 0, D_MODEL) for i, name in enumerate(SMALL_D_ROWS)) + (
    ("q_a_norm", 4, 0, Q_LORA), ("kv_a_norm", 4, 256, KV_LORA), ("sinks", 4, 384, N_HEADS))
SMALL_LOSS_ROW, SMALL_LOSS_OFF = 4, 512


def _pack_small(vals):
    row4 = jnp.concatenate([vals["q_a_norm"].reshape(-1), vals["kv_a_norm"].reshape(-1), vals["sinks"].reshape(-1),
                            jnp.zeros((SMALL_LOSS_OFF - 392,), F32), vals["loss"].reshape(-1),
                            jnp.zeros((1024 - SMALL_LOSS_OFF - 1,), F32)])
    rows = [vals[n].reshape(1024) for n in SMALL_D_ROWS] + [row4]
    return jnp.concatenate([jnp.stack(rows), jnp.zeros((SMALL_ROWS - 5, 1024), F32)], axis=0)


WEIGHT_ORDER = ("pre_norm_mix", "w_in", "q_a_norm", "w_q_b", "kv_a_norm", "w_kv_b", "sinks", "w_o_a", "w_o_b",
                "w_out", "post_norm_mix", "pre_norm_mlp", "w_up", "w_down", "post_norm_mlp")


def kernel(x, positions, pre_norm_mix, w_in, q_a_norm, w_q_b, kv_a_norm, w_kv_b, sinks, w_o_a, w_o_b, w_out, post_norm_mix, pre_norm_mlp, w_up, w_down, post_norm_mlp, loss_target, m_pre_norm_mix, m_w_in, m_q_a_norm, m_w_q_b, m_kv_a_norm, m_w_kv_b, m_sinks, m_w_o_a, m_w_o_b, m_w_out, m_post_norm_mix, m_pre_norm_mlp, m_w_up, m_w_down, m_post_norm_mlp, v_pre_norm_mix, v_w_in, v_q_a_norm, v_w_q_b, v_kv_a_norm, v_w_kv_b, v_sinks, v_w_o_a, v_w_o_b, v_w_out, v_post_norm_mix, v_pre_norm_mlp, v_w_up, v_w_down, v_post_norm_mlp):
    weights = dict(pre_norm_mix=pre_norm_mix, w_in=w_in, q_a_norm=q_a_norm, w_q_b=w_q_b, kv_a_norm=kv_a_norm,
                   w_kv_b=w_kv_b, sinks=sinks, w_o_a=w_o_a, w_o_b=w_o_b, w_out=w_out, post_norm_mix=post_norm_mix,
                   pre_norm_mlp=pre_norm_mlp, w_up=w_up, w_down=w_down, post_norm_mlp=post_norm_mlp)
    m_in = dict(pre_norm_mix=m_pre_norm_mix, w_in=m_w_in, q_a_norm=m_q_a_norm, w_q_b=m_w_q_b, kv_a_norm=m_kv_a_norm,
                w_kv_b=m_w_kv_b, sinks=m_sinks, w_o_a=m_w_o_a, w_o_b=m_w_o_b, w_out=m_w_out,
                post_norm_mix=m_post_norm_mix, pre_norm_mlp=m_pre_norm_mlp, w_up=m_w_up, w_down=m_w_down,
                post_norm_mlp=m_post_norm_mlp)
    v_in = dict(pre_norm_mix=v_pre_norm_mix, w_in=v_w_in, q_a_norm=v_q_a_norm, w_q_b=v_w_q_b, kv_a_norm=v_kv_a_norm,
                w_kv_b=v_w_kv_b, sinks=v_sinks, w_o_a=v_w_o_a, w_o_b=v_w_o_b, w_out=v_w_out,
                post_norm_mix=v_post_norm_mix, pre_norm_mlp=v_pre_norm_mlp, w_up=v_w_up, w_down=v_w_down,
                post_norm_mlp=v_post_norm_mlp)

    xs, pos, target = x[0], positions[0], loss_target[0]
    t = xs.shape[0]
    pos_col = pos.reshape(t, 1)
    pos_row = pos.reshape(1, t)
    g1, g2, g3, g4 = (weights[n] for n in SMALL_D_ROWS)
    g_q, g_kv = q_a_norm, kv_a_norm
    sink_vec = sinks.reshape(N_HEADS)
    shard = {n: weights[n][0].astype(BF16) for n in EARLY + LATE}

    tables, (e_in, e_qb, e_kvb) = _rope_tables(pos_col, _freq_row(), [shard[n] for n in EARLY])
    w_in_p = _pad_w_in(_from_col_shards(e_in))
    w_qb = _pad_heads_cols(_from_col_shards(e_qb), N_HEADS, QK_NOPE + QK_ROPE)
    w_kvb = _pad_w_kvb(_from_col_shards(e_kvb))

    (h, gates, qa, ka, va, cq, ckv, cqn, ckvn, kb, vb, qt, kt, vt) = _inproj_fwd(
        xs, g1, w_in_p, g_q, g_kv, w_kvb, w_qb.T, w_kvb[:, :HM].T, w_kvb[:, HM:].T, w_in_p[:, C_KR:].T, tables)
    out_a, lse_a = _swa_fwd(qa, ka, va, pos_col, pos_row, sink_vec)
    out_b, out_b_t, qt_lse, (l_oa, l_ob, l_out, w_up_s, l_down) = _mla_fwd(qt, kb, vt, [shard[n] for n in LATE])
    w_oa = _pad_heads_rows(_from_col_shards(l_oa), N_HEADS, HEAD_A)
    w_ob = _pad_heads_rows(_from_col_shards(l_ob), N_HEADS, V_DIM_B)
    w_out_f = l_out.reshape(D_MODEL, D_MODEL)
    w_down_f = l_down.reshape(D_FF, D_MODEL)

    oa_p, ob_p, merged, y, x1, h2 = _merge_fwd(out_a, out_b, gates, xs, w_oa, w_ob, w_out_f, g2, g3)
    a, du, dy2, dx1, loss, dg3, dg4 = _mlp_fwd_bwd(x1, h2, target, w_up_s, w_down_f, g3, g4)
    (dgates, d_oa, d_ob_t, dg2, dw_oa, dw_ob, dw_out) = _merge_bwd(
        dx1, y, gates, oa_p, ob_p, out_a, out_b, out_b_t, merged, w_oa, w_ob, w_out_f, g2)
    late_slices = [
        _col_shards(_unpad_heads_rows(dw_oa, N_HEADS, HEAD_A)).astype(BF16),
        _col_shards(_unpad_heads_rows(dw_ob, N_HEADS, V_DIM_B)).astype(BF16),
        dw_out.astype(BF16).reshape(N_DEV, D_MODEL // N_DEV, D_MODEL),
        _matmul_tn(h2, du, "dw_up", BF16, N_DEV),
        _matmul_tn(a, dy2, "dw_down", BF16).reshape(N_DEV, D_FF // N_DEV, D_MODEL),
    ]
    dqkv_a, dsink = _swa_bwd(qa, ka, va, out_a, d_oa, lse_a, pos_col, pos_row, sink_vec)
    dw_in_early = jnp.concatenate([_matmul_tn(h, dgates, "dw_in_gates"), _matmul_tn(h, dqkv_a, "dw_in_mixer_a"),
                                   jnp.zeros((D_MODEL, D_IN_PAD - C_CQ), F32)], axis=1)
    late_slices.append(_col_shards(_unpad_w_in(dw_in_early)).astype(BF16))
    dqb_t, dkb_t, dvb_t, late_parts = _mla_bwd(qt_lse, kb, kt, vb, d_ob_t, late_slices)
    w_in_early_parts = late_parts.pop()
    dproj_late, dgq, dgkv, dw_qb_t, dw_kvb_t = _latent_bwd(
        dqb_t, dkb_t, dvb_t, cq, ckv, cqn, ckvn, *tables[3:], g_q, g_kv, w_qb, w_kvb)
    dw_l = _matmul_tn(h, dproj_late, "dw_in_latents")
    late_cols = jnp.concatenate([dw_l[:, :Q_LORA + KV_LORA], dw_l[:, C_KR - C_CQ + QK_NOPE:C_KR - C_CQ + Q_HEAD_B]],
                                axis=1)
    shard_cols = w_in.shape[2]
    head = late_cols.shape[1] - shard_cols
    w_in_late = jnp.concatenate([
        jnp.zeros((N_DEV - 2, D_MODEL, shard_cols), F32),
        jnp.pad(late_cols[:, :head], ((0, 0), (shard_cols - head, 0)))[None], late_cols[:, head:][None]])
    early_slices = [
        w_in_late.astype(BF16),
        _col_shards(_unpad_heads_cols(dw_qb_t.T, N_HEADS, QK_NOPE + QK_ROPE)).astype(BF16),
        _col_shards(_unpad_w_kvb(dw_kvb_t.T)).astype(BF16),
    ]
    dx, dg1, early_parts = _inproj_bwd(dgates, dqkv_a, dproj_late, xs, dx1, g1, w_in_p, early_slices,
                                       only=[(N_DEV - 2, N_DEV - 1), None, None])
    small_grads = {"pre_norm_mix": dg1, "post_norm_mix": dg2, "pre_norm_mlp": dg3, "post_norm_mlp": dg4,
                   "q_a_norm": dgq, "kv_a_norm": dgkv, "sinks": dsink.reshape(N_HEADS, BLOCK).sum(axis=1),
                   "loss": loss[0, 0:1]}
    s_parts = _gather_small(_pack_small(small_grads))

    updates = {}
    all_parts = [[w_in_early_parts, early_parts[0]]] + [[p] for p in early_parts[1:] + late_parts]
    for name, parts in zip(EARLY + LATE, all_parts):
        outs = _adamw(parts, weights[name], m_in[name], v_in[name], "adamw_" + name)
        for kind, arr in zip(("g", "d", "m", "v"), outs):
            updates[kind, name] = arr
    small_out, loss_sum = _adamw_small(s_parts, weights, m_in, v_in)
    for name, outs in small_out.items():
        for kind, arr in zip(("g", "d", "m", "v"), outs):
            updates[kind, name] = arr
    results = [updates[kind, name] for kind in ("g", "d", "m", "v") for name in WEIGHT_ORDER]
    return (loss_sum.reshape(()), dx[None], *results)
```

```python
import functools

import numpy as np
import jax
import jax.numpy as jnp
from jax import lax
from jax.experimental import pallas as pl
from jax.experimental.pallas import tpu as pltpu

F32 = jnp.float32
BF16 = jnp.bfloat16

D_MODEL = 1024
D_FF = 4096
N_HEADS = 8
N_KV_A = 2
GROUP_A = N_HEADS // N_KV_A
HEAD_A = 64
QK_NOPE = 64
QK_ROPE = 32
V_DIM_B = 64
Q_LORA = 256
KV_LORA = 128
BLOCK = 128
SLAB = 128
ROPE_THETA = 10000.0
EPS = 1e-6
N_DEV = 8
NEG = -1e30

SCALE_A = HEAD_A ** -0.5
SCALE_B = (QK_NOPE + QK_ROPE) ** -0.5
LOG2E = 1.4426950408889634
SCORE_B = SCALE_B * LOG2E
MLA_HEADS_PER_STEP = 4
MLA_FWD_HEADS_PER_STEP = 8
Q_HEAD_B = QK_NOPE + QK_ROPE
ONES_ROWS = 16
SLOPES_A = tuple(2.0 ** (-8.0 * (h + 1) / N_HEADS) for h in range(N_HEADS))

ADAM_LR = 0.001
ADAM_B1 = 0.9
ADAM_B2 = 0.999
ADAM_EPS = 1e-08
ADAM_WD = 0.01
ADAM_STEP = 10

HM = N_HEADS * SLAB
C_GATES = 0
C_QA = 2 * D_MODEL
C_KA = C_QA + HM
C_VA = C_KA + N_KV_A * SLAB
C_CQ = C_VA + N_KV_A * SLAB
C_CKV = C_CQ + Q_LORA
C_KR = C_CKV + KV_LORA
D_IN_PAD = C_KR + SLAB

VMEM_LIMIT = 56 * 1024 * 1024
VMEM_LIMIT_MERGE_BWD = 60 * 1024 * 1024

EARLY = ("w_in", "w_q_b", "w_kv_b")
LATE = ("w_o_a", "w_o_b", "w_out", "w_up", "w_down")
ADAM_ROWS = 256
SMALL_ROWS = 8


def _token_tile(t):
    return min(256, t)


def _wide_token_tile(t):
    return min(512, t)


def _attn_tile(t):
    return 512 if t >= 2048 else 128


def _params(sem, vmem=VMEM_LIMIT):
    return pltpu.CompilerParams(dimension_semantics=sem, vmem_limit_bytes=vmem)


def _dot(a, b):
    return jnp.dot(a, b, preferred_element_type=F32)


def _dot_nt(a, b):
    return lax.dot_general(a, b, (((1,), (1,)), ((), ())), preferred_element_type=F32)


def _dot_tn(a, b):
    return lax.dot_general(a, b, (((0,), (0,)), ((), ())), preferred_element_type=F32)


def _rms_r(x):
    return lax.rsqrt(jnp.mean(x * x, axis=-1, keepdims=True) + EPS)


def _rms_bwd(x, r, g, dy):
    t = dy * g
    return r * t - x * (r * r * r) * jnp.mean(x * t, axis=-1, keepdims=True)


def _sigmoid(x):
    return 1.0 / (1.0 + jnp.exp(-x))


def _rope(x, c, s1, s2):
    return x * c + pltpu.roll(x, SLAB - 16, 1) * s1 + pltpu.roll(x, 16, 1) * s2


def _rope_bwd(d, c, s1, s2):
    return d * c + pltpu.roll(d * s1, 16, 1) + pltpu.roll(d * s2, SLAB - 16, 1)


def _roll_rows(x, shift):
    return jnp.concatenate([x[-shift:], x[:-shift]], axis=0)


def _rope_t(x, c, s1, s2):
    return x * c + _roll_rows(x, SLAB - 16) * s1 + _roll_rows(x, 16) * s2


def _rope_t_bwd(d, c, s1, s2):
    return d * c + _roll_rows(d * s1, 16) + _roll_rows(d * s2, SLAB - 16)


def _plant_rows(slab, row, vals):
    hi = vals.astype(BF16).astype(F32)
    lo = (vals - hi).astype(BF16).astype(F32)
    idx = lax.broadcasted_iota(jnp.int32, slab.shape, 0)
    return jnp.where(idx == row, -hi, jnp.where(idx == row + 1, -lo, slab))


def _row_spec(tm, n):
    return pl.BlockSpec((tm, n), lambda i: (i, 0))


def _col_spec(n, tm):
    return pl.BlockSpec((n, tm), lambda i: (0, i))


def _full_spec(shape):
    nd = len(shape)
    return pl.BlockSpec(shape, lambda i: (0,) * nd, pipeline_mode=pl.Buffered(1))


def _acc_rows(ref, val):
    @pl.when(pl.program_id(0) == 0)
    def _():
        ref[...] = jnp.zeros_like(ref)
    ref[...] += jnp.sum(val, axis=0, keepdims=True)


def _rope_tables(pos_col, freq_row, early):
    t = pos_col.shape[0]
    tm = _token_tile(t)
    n = len(early)

    def body(pos_ref, f_ref, *rest):
        shard_refs, (c_ref, s1_ref, s2_ref, ct_ref, s1t_ref, s2t_ref) = rest[:n], rest[n:n + 6]
        start, finish = _two_level_gather(shard_refs, rest[n + 6:2 * n + 6], *rest[2 * n + 6:])
        pl.when(pl.program_id(0) == 0)(start)
        ang = pos_ref[...].astype(F32) * f_ref[...]
        lane = lax.broadcasted_iota(jnp.int32, ang.shape, 1)
        s = jnp.sin(ang)
        c = jnp.cos(ang)
        s1 = jnp.where((lane >= 64) & (lane < 80), -s, 0.0)
        s2 = jnp.where((lane >= 80) & (lane < 96), s, 0.0)
        c_ref[...], s1_ref[...], s2_ref[...] = c, s1, s2
        ct_ref[...], s1t_ref[...], s2t_ref[...] = c.T, s1.T, s2.T
        pl.when(pl.program_id(0) == t // tm - 1)(finish)

    tab = jax.ShapeDtypeStruct((t, SLAB), F32)
    tabt = jax.ShapeDtypeStruct((SLAB, t), F32)
    outs = pl.pallas_call(
        body, name="rope_tables", grid=(t // tm,),
        in_specs=[_row_spec(tm, 1), _full_spec((1, SLAB))] + [ANY_SPEC] * n,
        out_specs=[_row_spec(tm, SLAB)] * 3 + [_col_spec(SLAB, tm)] * 3 + [ANY_SPEC] * n,
        out_shape=[tab] * 3 + [tabt] * 3 + [jax.ShapeDtypeStruct((N_DEV,) + a.shape, a.dtype) for a in early],
        scratch_shapes=_exchange_scratch(n),
        compiler_params=_params(("arbitrary",)),
    )(pos_col, freq_row, *early)
    return outs[:6], outs[6:]


def _inproj_fwd(x, g1, w_in, g_q, g_kv, w_kvb, w_qb_t, w_kb_t, w_vb_t, w_kr_t, tables):
    t = x.shape[0]
    tm = _wide_token_tile(t)

    def body(x_ref, g1_ref, win_ref, gq_ref, gkv_ref, wkvb_ref, wqbt_ref, wkbt_ref, wvbt_ref, wkrt_ref,
             c_ref, s1_ref, s2_ref, ct_ref, s1t_ref, s2t_ref,
             h_ref, gates_ref, qa_ref, ka_ref, va_ref, cq_ref, ckv_ref, cqn_ref, ckvn_ref,
             kb_ref, vb_ref, qt_ref, kt_ref, vt_ref):
        xv = x_ref[...]
        h = (xv * _rms_r(xv) * g1_ref[...]).astype(BF16)
        h_ref[...] = h
        proj = _dot(h, win_ref[...])
        gates_ref[...] = proj[:, C_GATES:C_QA].astype(BF16)
        qa_ref[...] = proj[:, C_QA:C_KA].astype(BF16)
        ka_ref[...] = proj[:, C_KA:C_VA].astype(BF16)
        va_ref[...] = proj[:, C_VA:C_CQ].astype(BF16)
        cq = proj[:, C_CQ:C_CKV]
        ckv = proj[:, C_CKV:C_KR]
        kr = proj[:, C_KR:D_IN_PAD]
        cq_ref[...] = cq
        ckv_ref[...] = ckv
        cqn = (cq * _rms_r(cq) * gq_ref[...]).astype(BF16)
        ckvn = (ckv * _rms_r(ckv) * gkv_ref[...]).astype(BF16)
        cqn_ref[...] = cqn
        ckvn_ref[...] = ckvn
        c, s1, s2 = c_ref[...], s1_ref[...], s2_ref[...]
        kvb = _dot(ckvn, wkvb_ref[...])
        kr_rot = _rope(kr, c, s1, s2)
        ct, s1t, s2t = ct_ref[...], s1t_ref[...], s2t_ref[...]
        q_t = _dot_nt(wqbt_ref[...], cqn)
        k_t = _dot_nt(wkbt_ref[...], ckvn)
        kr_t = _rope_t(_dot_nt(wkrt_ref[...], h), ct, s1t, s2t)
        k_lane = lax.broadcasted_iota(jnp.int32, (1, SLAB), 1)
        k_ones = jnp.where((k_lane == Q_HEAD_B) | (k_lane == Q_HEAD_B + 1), 1.0, 0.0)
        for hd in range(N_HEADS):
            sl = slice(hd * SLAB, (hd + 1) * SLAB)
            kb_ref[:, sl] = (kvb[:, sl] + kr_rot + k_ones).astype(BF16)
            qt_ref[sl, :] = (_rope_t(q_t[sl, :], ct, s1t, s2t) * SCORE_B).astype(BF16)
            kt_ref[sl, :] = (k_t[sl, :] + kr_t).astype(BF16)
        v_lane = lax.broadcasted_iota(jnp.int32, (1, HM), 1) & (SLAB - 1)
        v_ones = jnp.where((v_lane == V_DIM_B) | (v_lane == V_DIM_B + 1), 1.0, 0.0)
        vb_ref[...] = (kvb[:, HM:2 * HM] + v_ones).astype(BF16)
        pad_row = lax.broadcasted_iota(jnp.int32, (HM, 1), 0) & (SLAB - 1)
        ones_rows = jnp.where((pad_row >= V_DIM_B) & (pad_row < V_DIM_B + ONES_ROWS), 1.0, 0.0)
        vt_ref[...] = (_dot_nt(wvbt_ref[...], ckvn) + ones_rows).astype(BF16)

    def sds(n, dt):
        return jax.ShapeDtypeStruct((t, n), dt)

    outs = [(D_MODEL, BF16), (2 * D_MODEL, BF16), (HM, BF16), (N_KV_A * SLAB, BF16), (N_KV_A * SLAB, BF16),
            (Q_LORA, F32), (KV_LORA, F32), (Q_LORA, BF16), (KV_LORA, BF16), (HM, BF16), (HM, BF16)]
    tab, tabt = _row_spec(tm, SLAB), _col_spec(SLAB, tm)
    return pl.pallas_call(
        body, name="inproj_fwd", grid=(t // tm,),
        in_specs=[_row_spec(tm, D_MODEL), _full_spec((1, D_MODEL)), _full_spec((D_MODEL, D_IN_PAD)),
                  _full_spec((1, Q_LORA)), _full_spec((1, KV_LORA)), _full_spec((KV_LORA, 2 * HM)),
                  _full_spec((HM, Q_LORA)), _full_spec((HM, KV_LORA)), _full_spec((HM, KV_LORA)),
                  _full_spec((SLAB, D_MODEL)), tab, tab, tab, tabt, tabt, tabt],
        out_specs=[_row_spec(tm, n) for n, _ in outs] + [_col_spec(HM, tm)] * 3,
        out_shape=[sds(n, dt) for n, dt in outs] + [jax.ShapeDtypeStruct((HM, t), BF16)] * 3,
        compiler_params=_params(("parallel",)),
    )(x, g1, w_in, g_q, g_kv, w_kvb, w_qb_t, w_kb_t, w_vb_t, w_kr_t, *tables)


def _tile_group(a):
    return jnp.concatenate([a] * GROUP_A, axis=1)


def _swa_masks():
    row = lax.broadcasted_iota(jnp.int32, (BLOCK, GROUP_A * BLOCK), 0)
    col = lax.broadcasted_iota(jnp.int32, (BLOCK, GROUP_A * BLOCK), 1) & (BLOCK - 1)
    return row <= col, row > col


def _heads_beside(ref, g):
    return jnp.concatenate([ref[:, (g * GROUP_A + hh) * SLAB:(g * GROUP_A + hh + 1) * SLAB].T
                            for hh in range(GROUP_A)], axis=1)


def _rows_beside(ref, g):
    return jnp.concatenate([ref[g * GROUP_A + hh] for hh in range(GROUP_A)], axis=1)


def _swa_rows(sinks):
    slopes = jnp.repeat(jnp.asarray(SLOPES_A, F32).reshape(N_KV_A, GROUP_A, 1), BLOCK, axis=2)
    sink_rows = jnp.repeat(sinks.reshape(N_KV_A, GROUP_A, 1), BLOCK, axis=2)
    return slopes.reshape(N_KV_A, 1, GROUP_A * BLOCK), sink_rows.reshape(N_KV_A, 1, GROUP_A * BLOCK)


def _swa_fwd(qa, ka, va, pos_col, pos_row, sinks):
    t = qa.shape[0]
    nb = t // BLOCK
    gw = GROUP_A * BLOCK
    slope_rows, sink_rows = _swa_rows(sinks)

    def body(q_ref, kc_ref, kp_ref, vc_ref, vp_ref, pkc_ref, pkp_ref, pq_ref, slope_ref, sink_ref, o_ref, l_ref):
        i = pl.program_id(0)
        pq = pq_ref[...]
        dist_c = _tile_group(jnp.abs(pkc_ref[...] - pq).astype(F32))
        dist_p = _tile_group(jnp.abs(pkp_ref[...] - pq).astype(F32))
        mask_c, older = _swa_masks()
        mask_p = jnp.logical_and(older, i > 0)
        raw = []
        for g in range(N_KV_A):
            gs = slice(g * SLAB, (g + 1) * SLAB)
            x = _heads_beside(q_ref, g)
            raw.append((_dot(kc_ref[:, gs], x), _dot(kp_ref[:, gs], x)))
        for g in range(N_KV_A):
            gs = slice(g * SLAB, (g + 1) * SLAB)
            slope, sink = slope_ref[g], sink_ref[g]
            s_c = jnp.where(mask_c, raw[g][0] * SCALE_A - slope * dist_c, NEG)
            s_p = jnp.where(mask_p, raw[g][1] * SCALE_A - slope * dist_p, NEG)
            m = jnp.maximum(jnp.maximum(jnp.max(s_c, axis=0, keepdims=True),
                                        jnp.max(s_p, axis=0, keepdims=True)), sink)
            e_c = jnp.exp(s_c - m)
            e_p = jnp.exp(s_p - m)
            den = jnp.sum(e_c, axis=0, keepdims=True) + jnp.sum(e_p, axis=0, keepdims=True) + jnp.exp(sink - m)
            inv = 1.0 / den
            ot = (_dot_tn(vc_ref[:, gs], (e_c * inv).astype(BF16))
                  + _dot_tn(vp_ref[:, gs], (e_p * inv).astype(BF16)))
            lse = m + jnp.log(den)
            for hh in range(GROUP_A):
                hd = g * GROUP_A + hh
                seg = slice(hh * BLOCK, (hh + 1) * BLOCK)
                o_ref[:, hd * SLAB:(hd + 1) * SLAB] = ot[:, seg].T.astype(BF16)
                l_ref[hd] = lse[:, seg]

    cur = lambda i: (i, 0)
    prev = lambda i: (jnp.maximum(i - 1, 0), 0)
    kvw = N_KV_A * SLAB
    rows = pl.BlockSpec((N_KV_A, 1, gw), lambda i: (0, 0, 0))
    return pl.pallas_call(
        body, name="swa_fwd", grid=(nb,),
        in_specs=[pl.BlockSpec((BLOCK, HM), cur),
                  pl.BlockSpec((BLOCK, kvw), cur), pl.BlockSpec((BLOCK, kvw), prev),
                  pl.BlockSpec((BLOCK, kvw), cur), pl.BlockSpec((BLOCK, kvw), prev),
                  pl.BlockSpec((BLOCK, 1), cur), pl.BlockSpec((BLOCK, 1), prev),
                  pl.BlockSpec((1, BLOCK), lambda i: (0, i)), rows, rows],
        out_specs=[pl.BlockSpec((BLOCK, HM), cur), pl.BlockSpec((N_HEADS, 1, BLOCK), lambda i: (0, 0, i))],
        out_shape=[jax.ShapeDtypeStruct((t, HM), BF16), jax.ShapeDtypeStruct((N_HEADS, 1, t), F32)],
        compiler_params=_params(("parallel",)),
    )(qa, ka, ka, va, va, pos_col, pos_col, pos_row, slope_rows, sink_rows)


def _swa_bwd(qa, ka, va, out_a, d_oa, lse, pos_col, pos_row, sinks):
    t = qa.shape[0]
    nb = t // BLOCK
    gw = GROUP_A * BLOCK
    kvw = N_KV_A * SLAB
    slope_rows, sink_rows = _swa_rows(sinks)

    def body(q_ref, qn_ref, do_ref, don_ref, l_ref, ln_ref, o_ref, on_ref, kp_ref, kc_ref, vp_ref, vc_ref,
             pkp_ref, pkc_ref, pq_ref, pqn_ref, slope_ref, sink_ref, dqkv_ref, dsink_ref):
        j = pl.program_id(0)
        pkc, pkp = pkc_ref[...], pkp_ref[...]
        dist_cc = _tile_group(jnp.abs(pkc - pq_ref[...]).astype(F32))
        dist_cp = _tile_group(jnp.abs(pkp - pq_ref[...]).astype(F32))
        dist_nc = _tile_group(jnp.abs(pkc - pqn_ref[...]).astype(F32))
        mask_cc, older = _swa_masks()
        mask_cp = jnp.logical_and(older, j > 0)
        mask_nc = jnp.logical_and(older, j < nb - 1)

        @pl.when(j == 0)
        def _():
            dsink_ref[...] = jnp.zeros_like(dsink_ref)

        def tile(k, v, x, dox, lrow, drow, dist, mask, slope):
            s = jnp.where(mask, _dot(k, x) * SCALE_A - slope * dist, NEG)
            p = jnp.exp(s - lrow)
            ds = p * (_dot(v, dox) - drow)
            return p.astype(BF16), ds.astype(BF16)

        for g in range(N_KV_A):
            gs = slice(g * SLAB, (g + 1) * SLAB)
            kc, kp, vc, vp = kc_ref[:, gs], kp_ref[:, gs], vc_ref[:, gs], vp_ref[:, gs]
            slope, sink = slope_ref[g], sink_ref[g]
            x, xn = _heads_beside(q_ref, g), _heads_beside(qn_ref, g)
            dox, doxn = _heads_beside(do_ref, g), _heads_beside(don_ref, g)
            lrow, lrown = _rows_beside(l_ref, g), _rows_beside(ln_ref, g)
            drow = jnp.sum(dox.astype(F32) * _heads_beside(o_ref, g).astype(F32), axis=0, keepdims=True)
            drown = jnp.sum(doxn.astype(F32) * _heads_beside(on_ref, g).astype(F32), axis=0, keepdims=True)
            p_cc, ds_cc = tile(kc, vc, x, dox, lrow, drow, dist_cc, mask_cc, slope)
            _, ds_cp = tile(kp, vp, x, dox, lrow, drow, dist_cp, mask_cp, slope)
            p_nc, ds_nc = tile(kc, vc, xn, doxn, lrown, drown, dist_nc, mask_nc, slope)
            dqt = (_dot_tn(kc, ds_cc) + _dot_tn(kp, ds_cp)) * SCALE_A
            for hh in range(GROUP_A):
                hd = g * GROUP_A + hh
                dqkv_ref[:, hd * SLAB:(hd + 1) * SLAB] = dqt[:, hh * BLOCK:(hh + 1) * BLOCK].T.astype(BF16)
            dqkv_ref[:, HM + g * SLAB:HM + (g + 1) * SLAB] = (
                (_dot_nt(ds_cc, x) + _dot_nt(ds_nc, xn)) * SCALE_A).astype(BF16)
            dqkv_ref[:, HM + kvw + g * SLAB:HM + kvw + (g + 1) * SLAB] = (
                _dot_nt(p_cc, dox) + _dot_nt(p_nc, doxn)).astype(BF16)
            dsink_ref[g] -= jnp.exp(sink - lrow) * drow

    cur = lambda j: (j, 0)
    prev = lambda j: (jnp.maximum(j - 1, 0), 0)
    nxt = lambda j: (jnp.minimum(j + 1, nb - 1), 0)
    cur3 = lambda j: (0, 0, j)
    nxt3 = lambda j: (0, 0, jnp.minimum(j + 1, nb - 1))
    kvw = N_KV_A * SLAB
    rows = pl.BlockSpec((N_KV_A, 1, gw), lambda j: (0, 0, 0))
    stat = lambda im: pl.BlockSpec((N_HEADS, 1, BLOCK), im)
    return pl.pallas_call(
        body, name="swa_bwd", grid=(nb,),
        in_specs=[pl.BlockSpec((BLOCK, HM), cur), pl.BlockSpec((BLOCK, HM), nxt),
                  pl.BlockSpec((BLOCK, HM), cur), pl.BlockSpec((BLOCK, HM), nxt),
                  stat(cur3), stat(nxt3), pl.BlockSpec((BLOCK, HM), cur), pl.BlockSpec((BLOCK, HM), nxt),
                  pl.BlockSpec((BLOCK, kvw), prev), pl.BlockSpec((BLOCK, kvw), cur),
                  pl.BlockSpec((BLOCK, kvw), prev), pl.BlockSpec((BLOCK, kvw), cur),
                  pl.BlockSpec((BLOCK, 1), prev), pl.BlockSpec((BLOCK, 1), cur),
                  pl.BlockSpec((1, BLOCK), lambda j: (0, j)),
                  pl.BlockSpec((1, BLOCK), lambda j: (0, jnp.minimum(j + 1, nb - 1))), rows, rows],
        out_specs=[pl.BlockSpec((BLOCK, HM + 2 * kvw), cur), rows],
        out_shape=[jax.ShapeDtypeStruct((t, HM + 2 * kvw), BF16), jax.ShapeDtypeStruct((N_KV_A, 1, gw), F32)],
        compiler_params=_params(("arbitrary",)),
    )(qa, qa, d_oa, d_oa, lse, lse, out_a, out_a, ka, ka, va, va,
      pos_col, pos_col, pos_row, pos_row, slope_rows, sink_rows)


def _mesh_pos():
    return lax.axis_index("x"), lax.axis_index("y"), lax.axis_index("c")


def _flip(v, bit):
    return 1 - v if bit else v


def _direct_copies(srcs, dsts, send_sems, recv_sems, local_sems, gather, sem_base=0, only=None):
    x, y, c = _mesh_pos()
    me = 4 * x + 2 * y + c

    def among(idx, dests):
        ok = idx == dests[0]
        for d in dests[1:]:
            ok = jnp.logical_or(ok, idx == d)
        return ok

    local, remote = [], []
    for a, (src, dst) in enumerate(zip(srcs, dsts)):
        dests = None if only is None else only[a]
        recv_ok = None if dests is None else among(me, dests)
        local.append((pltpu.make_async_copy(src if gather else src.at[me], dst.at[me],
                                            local_sems.at[sem_base + a]), recv_ok))
        for r in range(1, N_DEV):
            px, py, pc = _flip(x, r & 4), _flip(y, r & 2), _flip(c, r & 1)
            peer = 4 * px + 2 * py + pc
            sem = (N_DEV - 1) * (sem_base + a) + r - 1
            copy = pltpu.make_async_remote_copy(
                src_ref=src if gather else src.at[peer], dst_ref=dst.at[me],
                send_sem=send_sems.at[sem], recv_sem=recv_sems.at[sem],
                device_id=(px, py, pc), device_id_type=pl.DeviceIdType.MESH)
            remote.append((copy, None if dests is None else among(peer, dests), recv_ok))
    return local, remote


def _when(cond, fn):
    if cond is None:
        fn()
    else:
        pl.when(cond)(fn)


def _start_copies(local, remote):
    for cp, ok in local:
        _when(ok, cp.start)
    for cp, send_ok, _ in remote:
        _when(send_ok, cp.start)


def _wait_copies(local, remote):
    for cp, _, recv_ok in remote:
        _when(recv_ok, cp.wait_recv)
    for cp, send_ok, _ in remote:
        _when(send_ok, cp.wait_send)
    for cp, ok in local:
        _when(ok, cp.wait)


def _exchange_scratch(n):
    return [pltpu.SemaphoreType.DMA((n * (N_DEV - 1),)), pltpu.SemaphoreType.DMA((n * (N_DEV - 1),)),
            pltpu.SemaphoreType.DMA((n,))]


ANY_SPEC = pl.BlockSpec(memory_space=pl.ANY)


def _mla_fwd(qt, kb, vt, late):
    t = kb.shape[0]
    tk = _attn_tile(t)
    ratio = 2 if t >= 2 * tk else 1
    tq = ratio * tk
    nq = t // tq
    hps = MLA_FWD_HEADS_PER_STEP
    w = hps * SLAB
    pairs = [(i, j) for i in range(nq) for j in range(ratio * (i + 1))]
    i_tab = jnp.asarray(np.array([p[0] for p in pairs], np.int32))
    j_tab = jnp.asarray(np.array([p[1] for p in pairs], np.int32))

    n_late = len(late)

    def body(it_ref, jt_ref, qt_ref, k_ref, vt_ref, *rest):
        late_refs, (o_ref, ot_ref, qa_ref) = rest[:n_late], rest[n_late:n_late + 3]
        gathered_refs = rest[n_late + 3:2 * n_late + 3]
        m_s, acc_s, send_sems, recv_sems, local_sems = rest[2 * n_late + 3:]
        n = pl.program_id(1)
        i, j = it_ref[n], jt_ref[n]
        first_step = jnp.logical_and(pl.program_id(0) == 0, n == 0)
        last_step = jnp.logical_and(pl.program_id(0) == N_HEADS // hps - 1, n == len(pairs) - 1)

        @pl.when(first_step)
        def _():
            _start_copies(*_direct_copies(late_refs, gathered_refs, send_sems, recv_sems, local_sems, True))

        @pl.when(j == 0)
        def _():
            m_s[...] = jnp.full_like(m_s, NEG)
            acc_s[...] = jnp.zeros_like(acc_s)

        def update(masked, q0):
            qc = slice(q0, tq)

            def scores(hh):
                sl = slice(hh * SLAB, (hh + 1) * SLAB)
                return _dot(k_ref[:, sl], qt_ref[sl, qc])

            def softmax(hh, s):
                if masked:
                    s = jnp.where(lax.broadcasted_iota(jnp.int32, s.shape, 0)
                                  <= lax.broadcasted_iota(jnp.int32, s.shape, 1), s, NEG)
                m_old = m_s[hh][:, qc]
                m_new = jnp.maximum(m_old, jnp.max(s, axis=0, keepdims=True))
                m_s[hh, :, qc] = m_new
                return jnp.exp2(s - m_new).astype(BF16), jnp.exp2(m_old - m_new)

            def accumulate(hh, p, alpha):
                sl = slice(hh * SLAB, hh * SLAB + V_DIM_B + ONES_ROWS)
                acc_s[sl, qc] = alpha * acc_s[sl, qc] + _dot(vt_ref[sl, :], p)

            s_next, pending = scores(0), None
            for hh in range(hps):
                s = s_next
                if hh + 1 < hps:
                    s_next = scores(hh + 1)
                p, alpha = softmax(hh, s)
                if pending is not None:
                    accumulate(*pending)
                pending = (hh, p, alpha)
            accumulate(*pending)

        @pl.when(j < ratio * i)
        def _():
            update(False, 0)

        for part in range(ratio):
            @pl.when(j == ratio * i + part)
            def _():
                update(True, part * tk)

        @pl.when(j == ratio * i + ratio - 1)
        def _():
            for hh in range(hps):
                sl = slice(hh * SLAB, (hh + 1) * SLAB)
                den = acc_s[hh * SLAB + V_DIM_B:hh * SLAB + V_DIM_B + 1, :]
                values = lax.broadcasted_iota(jnp.int32, (SLAB, tq), 0) < V_DIM_B
                ot = jnp.where(values, acc_s[sl, :] / den, 0.0)
                ot_ref[sl, :] = ot.astype(BF16)
                o_ref[:, sl] = ot.T.astype(BF16)
                lse = m_s[hh] + jnp.log2(den)
                qa_ref[sl, :] = _plant_rows(qt_ref[sl, :].astype(F32), Q_HEAD_B, lse).astype(BF16)

        @pl.when(last_step)
        def _():
            _wait_copies(*_direct_copies(late_refs, gathered_refs, send_sems, recv_sems, local_sems, True))

    grid_spec = pltpu.PrefetchScalarGridSpec(
        num_scalar_prefetch=2, grid=(N_HEADS // hps, len(pairs)),
        in_specs=[pl.BlockSpec((w, tq), lambda h, n, it, jt: (h, it[n])),
                  pl.BlockSpec((tk, w), lambda h, n, it, jt: (jt[n], h)),
                  pl.BlockSpec((w, tk), lambda h, n, it, jt: (h, jt[n]))] + [ANY_SPEC] * n_late,
        out_specs=[pl.BlockSpec((tq, w), lambda h, n, it, jt: (it[n], h)),
                   pl.BlockSpec((w, tq), lambda h, n, it, jt: (h, it[n])),
                   pl.BlockSpec((w, tq), lambda h, n, it, jt: (h, it[n]))] + [ANY_SPEC] * n_late,
        scratch_shapes=[pltpu.VMEM((hps, 1, tq), F32), pltpu.VMEM((w, tq), F32)] + _exchange_scratch(n_late))
    outs = pl.pallas_call(
        body, name="mla_fwd", grid_spec=grid_spec,
        out_shape=[jax.ShapeDtypeStruct((t, HM), BF16), jax.ShapeDtypeStruct((HM, t), BF16),
                   jax.ShapeDtypeStruct((HM, t), BF16)]
        + [jax.ShapeDtypeStruct((N_DEV,) + a.shape, a.dtype) for a in late],
        compiler_params=_params(("arbitrary", "arbitrary")),
    )(i_tab, j_tab, qt, kb, vt, *late)
    return outs[0], outs[1], outs[2], list(outs[3:])


def _mla_bwd(qt, kb, kt, vb, d_ob_t, grad_slices):
    t = kb.shape[0]
    tk = _attn_tile(t)
    ratio = 2 if t >= 2 * tk else 1
    tq = ratio * tk
    nk, nq = t // tk, t // tq
    hps = MLA_HEADS_PER_STEP
    w = hps * SLAB
    pairs = [(j, i) for j in range(nk) for i in range(j // ratio, nq)]
    j_tab = jnp.asarray(np.array([p[0] for p in pairs], np.int32))
    i_tab = jnp.asarray(np.array([p[1] for p in pairs], np.int32))

    n_ex = len(grad_slices)

    def body(jt_ref, it_ref, qt_ref, dot_ref, k_ref, kt_ref, v_ref, *rest):
        slice_refs, (dqt_ref, dkt_ref, dvt_ref) = rest[:n_ex], rest[n_ex:n_ex + 3]
        part_refs = rest[n_ex + 3:2 * n_ex + 3]
        dk_s, dv_s, send_sems, recv_sems, local_sems = rest[2 * n_ex + 3:]
        n = pl.program_id(1)
        j, i = jt_ref[n], it_ref[n]
        first_step = jnp.logical_and(pl.program_id(0) == 0, n == 0)
        last_step = jnp.logical_and(pl.program_id(0) == N_HEADS // hps - 1, n == len(pairs) - 1)

        @pl.when(first_step)
        def _():
            _start_copies(*_direct_copies(slice_refs, part_refs, send_sems, recv_sems, local_sems, False))

        @pl.when(n == 0)
        def _():
            dqt_ref[...] = jnp.zeros_like(dqt_ref)

        def update(diagonal, q0):
            qc = slice(q0, tq)
            cols = pl.ds(pl.multiple_of(i * tq + q0, tk), tq - q0)

            def softmax_bwd(hh, s, dp):
                if diagonal:
                    s = jnp.where(lax.broadcasted_iota(jnp.int32, s.shape, 0)
                                  <= lax.broadcasted_iota(jnp.int32, s.shape, 1), s, NEG)
                p = jnp.exp2(s)
                return p.astype(BF16), (p * dp).astype(BF16)

            def gradients(hh, p, ds):
                base = hh * SLAB
                vrows = slice(base, base + V_DIM_B)
                qrows = slice(base, base + QK_NOPE + QK_ROPE)
                dv = _dot_nt(dot_ref[vrows, qc], p)
                dk = _dot_nt(qt_ref[qrows, qc], ds)
                if diagonal:
                    dv_s[base:base + SLAB, :] = jnp.concatenate([dv, jnp.zeros((SLAB - V_DIM_B, tk), F32)], axis=0)
                    dk_s[base:base + SLAB, :] = jnp.concatenate(
                        [dk, jnp.zeros((SLAB - QK_NOPE - QK_ROPE, tk), F32)], axis=0)
                else:
                    dv_s[vrows, :] += dv
                    dk_s[qrows, :] += dk
                dqt_ref[qrows, cols] += _dot(kt_ref[qrows, :], ds)

            def scores(hh):
                sl = slice(hh * SLAB, (hh + 1) * SLAB)
                return _dot(k_ref[:, sl], qt_ref[sl, qc])

            def dprod(hh):
                sl = slice(hh * SLAB, (hh + 1) * SLAB)
                return _dot(v_ref[:, sl], dot_ref[sl, qc])

            s_next = scores(0)
            for hh in range(hps):
                s = s_next
                dp = dprod(hh)
                if hh + 1 < hps:
                    s_next = scores(hh + 1)
                gradients(hh, *softmax_bwd(hh, s, dp))

        first_tile = lax.div(j, ratio)
        for part in range(ratio):
            @pl.when(jnp.logical_and(i == first_tile, lax.rem(j, ratio) == part))
            def _():
                update(True, part * tk)

        @pl.when(i > first_tile)
        def _():
            update(False, 0)

        @pl.when(i == nq - 1)
        def _():
            dkt_ref[...] = (dk_s[...] * (1.0 / LOG2E)).astype(BF16)
            dvt_ref[...] = dv_s[...].astype(BF16)

        @pl.when(last_step)
        def _():
            _wait_copies(*_direct_copies(slice_refs, part_refs, send_sems, recv_sems, local_sems, False))

    grid_spec = pltpu.PrefetchScalarGridSpec(
        num_scalar_prefetch=2, grid=(N_HEADS // hps, len(pairs)),
        in_specs=[pl.BlockSpec((w, tq), lambda h, n, jt, it: (h, it[n])),
                  pl.BlockSpec((w, tq), lambda h, n, jt, it: (h, it[n])),
                  pl.BlockSpec((tk, w), lambda h, n, jt, it: (jt[n], h)),
                  pl.BlockSpec((w, tk), lambda h, n, jt, it: (h, jt[n])),
                  pl.BlockSpec((tk, w), lambda h, n, jt, it: (jt[n], h))] + [ANY_SPEC] * n_ex,
        out_specs=[pl.BlockSpec((w, t), lambda h, n, jt, it: (h, 0)),
                   pl.BlockSpec((w, tk), lambda h, n, jt, it: (h, jt[n])),
                   pl.BlockSpec((w, tk), lambda h, n, jt, it: (h, jt[n]))] + [ANY_SPEC] * n_ex,
        scratch_shapes=[pltpu.VMEM((w, tk), F32), pltpu.VMEM((w, tk), F32)] + _exchange_scratch(n_ex))
    outs = pl.pallas_call(
        body, name="mla_bwd", grid_spec=grid_spec,
        out_shape=[jax.ShapeDtypeStruct((HM, t), F32), jax.ShapeDtypeStruct((HM, t), BF16),
                   jax.ShapeDtypeStruct((HM, t), BF16)]
        + [jax.ShapeDtypeStruct(a.shape, a.dtype) for a in grad_slices],
        compiler_params=_params(("arbitrary", "arbitrary")),
    )(j_tab, i_tab, qt, d_ob_t, kb, kt, vb, *grad_slices)
    return outs[0], outs[1], outs[2], list(outs[3:])


def _merge_fwd(out_a, out_b, gates, x, w_oa, w_ob, w_out, g2, g3):
    t = x.shape[0]
    tm = _wide_token_tile(t)

    def body(oa_ref, ob_ref, gates_ref, x_ref, woa_ref, wob_ref, wout_ref, g2_ref, g3_ref,
             oap_ref, obp_ref, merged_ref, y_ref, x1_ref, h2_ref):
        oa_p = _dot(oa_ref[...], woa_ref[...])
        ob_p = _dot(ob_ref[...], wob_ref[...])
        oap_ref[...] = oa_p.astype(BF16)
        obp_ref[...] = ob_p.astype(BF16)
        sa = _sigmoid(gates_ref[:, 0:D_MODEL].astype(F32))
        sb = _sigmoid(gates_ref[:, D_MODEL:2 * D_MODEL].astype(F32))
        merged = (sa * oa_p + sb * ob_p).astype(BF16)
        merged_ref[...] = merged
        y = _dot(merged, wout_ref[...])
        y_ref[...] = y
        x1 = x_ref[...] + y * _rms_r(y) * g2_ref[...]
        x1_ref[...] = x1
        h2_ref[...] = (x1 * _rms_r(x1) * g3_ref[...]).astype(BF16)

    def sds(dt):
        return jax.ShapeDtypeStruct((t, D_MODEL), dt)

    row = _row_spec(tm, D_MODEL)
    return pl.pallas_call(
        body, name="merge_fwd", grid=(t // tm,),
        in_specs=[_row_spec(tm, HM), _row_spec(tm, HM), _row_spec(tm, 2 * D_MODEL), row,
                  _full_spec((HM, D_MODEL)), _full_spec((HM, D_MODEL)), _full_spec((D_MODEL, D_MODEL)),
                  _full_spec((1, D_MODEL)), _full_spec((1, D_MODEL))],
        out_specs=[row] * 6,
        out_shape=[sds(BF16), sds(BF16), sds(BF16), sds(F32), sds(F32), sds(BF16)],
        compiler_params=_params(("parallel",)),
    )(out_a, out_b, gates, x, w_oa, w_ob, w_out, g2, g3)


def _merge_bwd(dx1, y, gates, oa_p, ob_p, out_a, out_b, out_b_t, merged, w_oa, w_ob, w_out, g2):
    t = dx1.shape[0]
    tm = _wide_token_tile(t)

    def body(dx1_ref, y_ref, gates_ref, oap_ref, obp_ref, oa_ref, ob_ref, obt_ref, merged_ref,
             woa_ref, wob_ref, wout_ref, g2_ref,
             dgates_ref, doa_ref, dobt_ref, dg2_ref, dwoa_ref, dwob_ref, dwout_ref):
        @pl.when(pl.program_id(0) == 0)
        def _():
            dwoa_ref[...] = jnp.zeros_like(dwoa_ref)
            dwob_ref[...] = jnp.zeros_like(dwob_ref)
            dwout_ref[...] = jnp.zeros_like(dwout_ref)

        dx1v = dx1_ref[...]
        yv = y_ref[...]
        r2 = _rms_r(yv)
        _acc_rows(dg2_ref, dx1v * yv * r2)
        dy = _rms_bwd(yv, r2, g2_ref[...], dx1v).astype(BF16)
        dwout_ref[...] += _dot_tn(merged_ref[...], dy)
        dm = _dot_nt(dy, wout_ref[...])
        sa = _sigmoid(gates_ref[:, 0:D_MODEL].astype(F32))
        sb = _sigmoid(gates_ref[:, D_MODEL:2 * D_MODEL].astype(F32))
        d_oap = (dm * sa).astype(BF16)
        d_obp = (dm * sb).astype(BF16)
        dwoa_ref[...] += _dot_tn(oa_ref[...], d_oap)
        dwob_ref[...] += _dot_tn(ob_ref[...], d_obp)
        dgates_ref[:, 0:D_MODEL] = (dm * oap_ref[...].astype(F32) * sa * (1.0 - sa)).astype(BF16)
        dgates_ref[:, D_MODEL:2 * D_MODEL] = (dm * obp_ref[...].astype(F32) * sb * (1.0 - sb)).astype(BF16)
        doa_ref[...] = _dot_nt(d_oap, woa_ref[...]).astype(BF16)
        d_ob_t = _dot_nt(wob_ref[...], d_obp)
        for hd in range(N_HEADS):
            sl = slice(hd * SLAB, (hd + 1) * SLAB)
            delta = jnp.sum(d_ob_t[sl, :] * obt_ref[sl, :].astype(F32), axis=0, keepdims=True)
            dobt_ref[sl, :] = _plant_rows(d_ob_t[sl, :], V_DIM_B, delta).astype(BF16)

    def sds(n, dt):
        return jax.ShapeDtypeStruct((t, n), dt)

    row = _row_spec(tm, D_MODEL)
    return pl.pallas_call(
        body, name="merge_bwd", grid=(t // tm,),
        in_specs=[row, row, _row_spec(tm, 2 * D_MODEL), row, row, _row_spec(tm, HM), _row_spec(tm, HM),
                  _col_spec(HM, tm), row,
                  _full_spec((HM, D_MODEL)), _full_spec((HM, D_MODEL)), _full_spec((D_MODEL, D_MODEL)),
                  _full_spec((1, D_MODEL))],
        out_specs=[_row_spec(tm, 2 * D_MODEL), _row_spec(tm, HM), _col_spec(HM, tm), _full_spec((1, D_MODEL)),
                   _full_spec((HM, D_MODEL)), _full_spec((HM, D_MODEL)), _full_spec((D_MODEL, D_MODEL))],
        out_shape=[sds(2 * D_MODEL, BF16), sds(HM, BF16), jax.ShapeDtypeStruct((HM, t), BF16),
                   jax.ShapeDtypeStruct((1, D_MODEL), F32),
                   jax.ShapeDtypeStruct((HM, D_MODEL), F32), jax.ShapeDtypeStruct((HM, D_MODEL), F32),
                   jax.ShapeDtypeStruct((D_MODEL, D_MODEL), F32)],
        compiler_params=_params(("arbitrary",), VMEM_LIMIT_MERGE_BWD),
    )(dx1, y, gates, oa_p, ob_p, out_a, out_b, out_b_t, merged, w_oa, w_ob, w_out, g2)


def _mlp_fwd_bwd(x1, h2, target, w_up, w_down, g3, g4):
    t = x1.shape[0]
    tm = _token_tile(t)
    fs = D_FF // N_DEV

    def body(x1_ref, h2_ref, tgt_ref, wup_ref, wdown_ref, g3_ref, g4_ref,
             a_ref, du_ref, dy2_ref, dx1_ref, loss_ref, dg3_ref, dg4_ref):
        x1v = x1_ref[...]
        h2v = h2_ref[...]
        u = jnp.concatenate([_dot(h2v, wup_ref[s]) for s in range(N_DEV)], axis=1)
        ru = jnp.maximum(u, 0.0)
        a = (ru * ru).astype(BF16)
        a_ref[...] = a
        y2 = _dot(a, wdown_ref[...])
        r4 = _rms_r(y2)
        diff = x1v + y2 * r4 * g4_ref[...] - tgt_ref[...]
        _acc_rows(loss_ref, jnp.sum(diff * diff, axis=-1, keepdims=True) * (0.5 / D_MODEL)
                  * jnp.ones((1, SLAB), F32))
        dx2 = diff * (1.0 / D_MODEL)
        _acc_rows(dg4_ref, dx2 * y2 * r4)
        dy2 = _rms_bwd(y2, r4, g4_ref[...], dx2).astype(BF16)
        dy2_ref[...] = dy2
        du = (_dot_nt(dy2, wdown_ref[...]) * (2.0 * ru)).astype(BF16)
        du_ref[...] = du
        dh2 = _dot_nt(du[:, 0:fs], wup_ref[0])
        for s in range(1, N_DEV):
            dh2 += _dot_nt(du[:, s * fs:(s + 1) * fs], wup_ref[s])
        r3 = _rms_r(x1v)
        _acc_rows(dg3_ref, dh2 * x1v * r3)
        dx1_ref[...] = dx2 + _rms_bwd(x1v, r3, g3_ref[...], dh2)

    row = _row_spec(tm, D_MODEL)
    frow = _row_spec(tm, D_FF)
    vec = _full_spec((1, D_MODEL))
    return pl.pallas_call(
        body, name="mlp_fwd_bwd", grid=(t // tm,),
        in_specs=[row, row, row, _full_spec((N_DEV, D_MODEL, fs)), _full_spec((D_FF, D_MODEL)), vec, vec],
        out_specs=[frow, frow, row, row, _full_spec((1, SLAB)), vec, vec],
        out_shape=[jax.ShapeDtypeStruct((t, D_FF), BF16), jax.ShapeDtypeStruct((t, D_FF), BF16),
                   jax.ShapeDtypeStruct((t, D_MODEL), BF16), jax.ShapeDtypeStruct((t, D_MODEL), F32),
                   jax.ShapeDtypeStruct((1, SLAB), F32), jax.ShapeDtypeStruct((1, D_MODEL), F32),
                   jax.ShapeDtypeStruct((1, D_MODEL), F32)],
        compiler_params=_params(("arbitrary",)),
    )(x1, h2, target, w_up, w_down, g3, g4)


def _latent_bwd(dqb_t, dkb_t, dvb_t, cq, ckv, cqn, ckvn, rope_ct, rope_s1t, rope_s2t, g_q, g_kv, w_qb, w_kvb):
    t = cq.shape[0]
    tm = _wide_token_tile(t)

    def body(dqt_ref, dkt_ref, dvt_ref, cq_ref, ckv_ref, cqn_ref, ckvn_ref, ct_ref, s1t_ref, s2t_ref,
             gq_ref, gkv_ref, wqb_ref, wkvb_ref,
             dlate_ref, dgq_ref, dgkv_ref, dwqb_ref, dwkvb_ref, dqbrt_ref, dkvbt_ref):
        @pl.when(pl.program_id(0) == 0)
        def _():
            dwqb_ref[...] = jnp.zeros_like(dwqb_ref)
            dwkvb_ref[...] = jnp.zeros_like(dwkvb_ref)

        ct, s1t, s2t = ct_ref[...], s1t_ref[...], s2t_ref[...]
        dk_sum_t = jnp.zeros((SLAB, tm), F32)
        for hd in range(N_HEADS):
            sl = slice(hd * SLAB, (hd + 1) * SLAB)
            dqbrt_ref[sl, :] = _rope_t_bwd(dqt_ref[sl, :] * SCALE_B, ct, s1t, s2t).astype(BF16)
            dk_sum_t += dkt_ref[sl, :].astype(F32)
        dkvbt_ref[0:HM, :] = dkt_ref[...]
        dkvbt_ref[HM:2 * HM, :] = dvt_ref[...]
        dkr = _rope_t_bwd(dk_sum_t, ct, s1t, s2t).T
        dwqb_ref[...] += _dot(dqbrt_ref[...], cqn_ref[...])
        dwkvb_ref[...] += _dot(dkvbt_ref[...], ckvn_ref[...])
        dcqn = _dot(wqb_ref[...], dqbrt_ref[...]).T
        cq = cq_ref[...]
        rq = _rms_r(cq)
        _acc_rows(dgq_ref, dcqn * cq * rq)
        dcq = _rms_bwd(cq, rq, gq_ref[...], dcqn)
        dckvn = _dot(wkvb_ref[...], dkvbt_ref[...]).T
        ckv = ckv_ref[...]
        rkv = _rms_r(ckv)
        _acc_rows(dgkv_ref, dckvn * ckv * rkv)
        dckv = _rms_bwd(ckv, rkv, gkv_ref[...], dckvn)
        dlate_ref[:, 0:C_CKV - C_CQ] = dcq.astype(BF16)
        dlate_ref[:, C_CKV - C_CQ:C_KR - C_CQ] = dckv.astype(BF16)
        dlate_ref[:, C_KR - C_CQ:D_IN_PAD - C_CQ] = dkr.astype(BF16)

    hmt = _col_spec(HM, tm)
    tab = _col_spec(SLAB, tm)
    return pl.pallas_call(
        body, name="latent_bwd", grid=(t // tm,),
        in_specs=[hmt, hmt, hmt,
                  _row_spec(tm, Q_LORA), _row_spec(tm, KV_LORA), _row_spec(tm, Q_LORA), _row_spec(tm, KV_LORA),
                  tab, tab, tab, _full_spec((1, Q_LORA)), _full_spec((1, KV_LORA)),
                  _full_spec((Q_LORA, HM)), _full_spec((KV_LORA, 2 * HM))],
        out_specs=[_row_spec(tm, D_IN_PAD - C_CQ), _full_spec((1, Q_LORA)), _full_spec((1, KV_LORA)),
                   _full_spec((HM, Q_LORA)), _full_spec((2 * HM, KV_LORA))],
        out_shape=[jax.ShapeDtypeStruct((t, D_IN_PAD - C_CQ), BF16),
                   jax.ShapeDtypeStruct((1, Q_LORA), F32), jax.ShapeDtypeStruct((1, KV_LORA), F32),
                   jax.ShapeDtypeStruct((HM, Q_LORA), F32), jax.ShapeDtypeStruct((2 * HM, KV_LORA), F32)],
        scratch_shapes=[pltpu.VMEM((HM, tm), BF16), pltpu.VMEM((2 * HM, tm), BF16)],
        compiler_params=_params(("arbitrary",)),
    )(dqb_t, dkb_t, dvb_t, cq, ckv, cqn, ckvn, rope_ct, rope_s1t, rope_s2t, g_q, g_kv, w_qb, w_kvb)


def _inproj_bwd(dgates, dqkv, dlate, x, dx1, g1, w_in, grad_slices, only):
    t = x.shape[0]
    tm = _wide_token_tile(t)
    n_ex = len(grad_slices)
    zeroed = [a for a in range(n_ex) if only[a] is not None]

    def body(dgates_ref, dqkv_ref, dlate_ref, x_ref, dx1_ref, g1_ref, win_ref, *rest):
        slice_refs = rest[:n_ex]
        dx_ref, dg1_ref = rest[n_ex:n_ex + 2]
        part_refs = rest[n_ex + 2:2 * n_ex + 2]
        dproj_ref, send_sems, recv_sems, local_sems = rest[2 * n_ex + 2:2 * n_ex + 6]
        zero_refs, zero_sem = rest[2 * n_ex + 6:-1], rest[-1]

        @pl.when(pl.program_id(0) == 0)
        def _():
            _start_copies(*_direct_copies(slice_refs, part_refs, send_sems, recv_sems, local_sems, False,
                                          only=only))
            x_, y_, c_ = _mesh_pos()
            me = 4 * x_ + 2 * y_ + c_
            for a, z_ref in zip(zeroed, zero_refs):
                outside = me != only[a][0]
                for d in only[a][1:]:
                    outside = jnp.logical_and(outside, me != d)

                @pl.when(outside)
                def _():
                    z_ref[...] = jnp.zeros_like(z_ref)
                    fills = [pltpu.make_async_copy(z_ref, part_refs[a].at[k], zero_sem.at[k])
                             for k in range(N_DEV)]
                    for cp in fills:
                        cp.start()
                    for cp in fills:
                        cp.wait()

        dproj_ref[:, C_GATES:C_QA] = dgates_ref[...]
        dproj_ref[:, C_QA:C_CQ] = dqkv_ref[...]
        dproj_ref[:, C_CQ:D_IN_PAD] = dlate_ref[...]
        dh = _dot_nt(dproj_ref[...], win_ref[...])
        xv = x_ref[...]
        r1 = _rms_r(xv)
        _acc_rows(dg1_ref, dh * xv * r1)
        dx_ref[...] = dx1_ref[...] + _rms_bwd(xv, r1, g1_ref[...], dh)

        @pl.when(pl.program_id(0) == t // tm - 1)
        def _():
            _wait_copies(*_direct_copies(slice_refs, part_refs, send_sems, recv_sems, local_sems, False,
                                         only=only))

    kvw = N_KV_A * SLAB
    row = _row_spec(tm, D_MODEL)
    outs = pl.pallas_call(
        body, name="inproj_bwd", grid=(t // tm,),
        in_specs=[_row_spec(tm, 2 * D_MODEL), _row_spec(tm, HM + 2 * kvw), _row_spec(tm, D_IN_PAD - C_CQ),
                  row, row, _full_spec((1, D_MODEL)), _full_spec((D_MODEL, D_IN_PAD))]
        + [ANY_SPEC] * n_ex,
        out_specs=[row, _full_spec((1, D_MODEL))] + [ANY_SPEC] * n_ex,
        out_shape=[jax.ShapeDtypeStruct((t, D_MODEL), F32), jax.ShapeDtypeStruct((1, D_MODEL), F32)]
        + [jax.ShapeDtypeStruct(a.shape, a.dtype) for a in grad_slices],
        scratch_shapes=[pltpu.VMEM((tm, D_IN_PAD), BF16)] + _exchange_scratch(n_ex)
        + [pltpu.VMEM(grad_slices[a].shape[1:], grad_slices[a].dtype) for a in zeroed]
        + [pltpu.SemaphoreType.DMA((N_DEV,))],
        compiler_params=_params(("arbitrary",)),
    )(dgates, dqkv, dlate, x, dx1, g1, w_in, *grad_slices)
    return outs[0], outs[1], list(outs[2:])


def _matmul_tn(a, b, name, out_dtype=F32, n_shards=1):
    t, k = a.shape
    n = b.shape[1]
    bn = min(n, 2048)
    bt = min(t, 2048)
    bk = min(k, 2048 * 1024 // bn)
    ns = n // n_shards
    per_block = bn // ns
    steps = t // bt

    def body(a_ref, b_ref, o_ref, acc):
        s = pl.program_id(2)

        @pl.when(s == 0)
        def _():
            acc[...] = jnp.zeros_like(acc)

        acc[...] += _dot_tn(a_ref[...], b_ref[...])

        @pl.when(s == steps - 1)
        def _():
            if n_shards > 1:
                for p in range(per_block):
                    o_ref[p] = acc[:, p * ns:(p + 1) * ns].astype(out_dtype)
            else:
                o_ref[...] = acc[...].astype(out_dtype)

    if n_shards > 1:
        out_spec = pl.BlockSpec((per_block, bk, ns), lambda i, j, s: (j, i, 0))
        out_shape = jax.ShapeDtypeStruct((n_shards, k, ns), out_dtype)
    else:
        out_spec = pl.BlockSpec((bk, bn), lambda i, j, s: (i, j))
        out_shape = jax.ShapeDtypeStruct((k, n), out_dtype)
    return pl.pallas_call(
        body, name=name, grid=(k // bk, n // bn, steps),
        in_specs=[pl.BlockSpec((bt, bk), lambda i, j, s: (s, i)), pl.BlockSpec((bt, bn), lambda i, j, s: (s, j))],
        out_specs=out_spec, out_shape=out_shape, scratch_shapes=[pltpu.VMEM((bk, bn), F32)],
        compiler_params=_params(("parallel", "parallel", "arbitrary")),
    )(a, b)


def _two_level_gather(srcs, dsts, send_sems, recv_sems, local_sems):
    n = len(srcs)
    x, y, c = _mesh_pos()
    me, sibling = (x, y, c), (x, y, 1 - c)
    chips = [(1 - x, y), (x, 1 - y), (1 - x, 1 - y)]

    def slot(a, px, py, pc):
        return dsts[a].at[4 * px + 2 * py + pc]

    def copy(a, k, block, to, src=None):
        return pltpu.make_async_remote_copy(
            src_ref=slot(a, *block) if src is None else src, dst_ref=slot(a, *block),
            send_sem=send_sems.at[(N_DEV - 1) * a + k], recv_sem=recv_sems.at[(N_DEV - 1) * a + k],
            device_id=to, device_id_type=pl.DeviceIdType.MESH)

    def own_copies():
        mine = [pltpu.make_async_copy(srcs[a], slot(a, *me), local_sems.at[a]) for a in range(n)]
        first = []
        for a in range(n):
            first.append(copy(a, 0, me, sibling, src=srcs[a]))
            first += [copy(a, 1 + j, me, (*chip, c), src=srcs[a]) for j, chip in enumerate(chips)]
        return mine, first

    def start():
        mine, first = own_copies()
        for cp in mine + first:
            cp.start()

    def finish():
        mine, first = own_copies()
        passed = []
        for j, chip in enumerate(chips):
            for a in range(n):
                copy(a, 1 + j, (*chip, c), me).wait_recv()
                passed.append(copy(a, 4 + j, (*chip, c), sibling))
                passed[-1].start()
        for a in range(n):
            copy(a, 0, sibling, me).wait_recv()
        for j, chip in enumerate(chips):
            for a in range(n):
                copy(a, 4 + j, (*chip, 1 - c), me).wait_recv()
        for cp in first + passed:
            cp.wait_send()
        for cp in mine:
            cp.wait()

    return start, finish


def _gather_small(small):
    def body(s_ref, s_dst, *sems):
        smalls = _direct_copies([s_ref], [s_dst], *sems, True)
        _start_copies(*smalls)
        _wait_copies(*smalls)

    return pl.pallas_call(
        body, name="gather_small",
        out_shape=jax.ShapeDtypeStruct((N_DEV,) + small.shape, small.dtype),
        in_specs=[ANY_SPEC], out_specs=ANY_SPEC,
        scratch_shapes=_exchange_scratch(1),
    )(small)


def _adamw(parts, w, m, v, name):
    n_parts = len(parts)
    _, k, n = parts[0].shape
    bk = min(k, ADAM_ROWS)
    c1 = 1.0 - ADAM_B1 ** ADAM_STEP
    c2 = 1.0 - ADAM_B2 ** ADAM_STEP

    def body(*refs):
        p_refs, (w_ref, m_ref, v_ref, g_ref, d_ref, mo_ref, vo_ref) = refs[:n_parts], refs[n_parts:]
        g = p_refs[0][0].astype(F32)
        for p_ref in p_refs:
            for s in range(N_DEV):
                if p_ref is not p_refs[0] or s > 0:
                    g = g + p_ref[s].astype(F32)
        g_ref[0] = g
        m_new = ADAM_B1 * m_ref[0] + (1.0 - ADAM_B1) * g
        v_new = ADAM_B2 * v_ref[0] + (1.0 - ADAM_B2) * (g * g)
        mo_ref[0] = m_new
        vo_ref[0] = v_new
        m_hat = m_new / c1
        v_hat = v_new / c2
        d_ref[0] = -ADAM_LR * (m_hat / (jnp.sqrt(v_hat) + ADAM_EPS) + ADAM_WD * w_ref[0])

    blk = pl.BlockSpec((1, bk, n), lambda i: (0, i, 0))
    out = jax.ShapeDtypeStruct((1, k, n), F32)
    return pl.pallas_call(
        body, name=name, grid=(k // bk,),
        in_specs=[pl.BlockSpec((N_DEV, bk, n), lambda i: (0, i, 0))] * n_parts + [blk, blk, blk],
        out_specs=[blk] * 4, out_shape=[out] * 4,
        compiler_params=_params(("parallel",)),
    )(*parts, w, m, v)


def _adamw_small(parts, w, m, v):
    k = len(SMALL_LAYOUT)
    c1 = 1.0 - ADAM_B1 ** ADAM_STEP
    c2 = 1.0 - ADAM_B2 ** ADAM_STEP

    def body(p_ref, *refs):
        w_refs, m_refs, v_refs, outs = refs[:k], refs[k:2 * k], refs[2 * k:3 * k], refs[3 * k:]
        total = p_ref[0]
        for s in range(1, N_DEV):
            total = total + p_ref[s]
        for i, (_, row, off, width) in enumerate(SMALL_LAYOUT):
            g = total[row:row + 1, off:off + width]
            m_new = ADAM_B1 * m_refs[i][...] + (1.0 - ADAM_B1) * g
            v_new = ADAM_B2 * v_refs[i][...] + (1.0 - ADAM_B2) * (g * g)
            outs[4 * i][...] = g
            outs[4 * i + 1][...] = -ADAM_LR * ((m_new / c1) / (jnp.sqrt(v_new / c2) + ADAM_EPS)
                                               + ADAM_WD * w_refs[i][...])
            outs[4 * i + 2][...] = m_new
            outs[4 * i + 3][...] = v_new
        outs[4 * k][...] = total[SMALL_LOSS_ROW:SMALL_LOSS_ROW + 1, SMALL_LOSS_OFF:SMALL_LOSS_OFF + 1]

    names = [name for name, *_ in SMALL_LAYOUT]
    out_shape = [jax.ShapeDtypeStruct(w[name].shape, F32) for name in names for _ in range(4)]
    outs = pl.pallas_call(
        body, name="adamw_small", out_shape=out_shape + [jax.ShapeDtypeStruct((1, 1), F32)],
    )(parts, *[w[n] for n in names], *[m[n] for n in names], *[v[n] for n in names])
    return {name: tuple(outs[4 * i:4 * i + 4]) for i, name in enumerate(names)}, outs[4 * k]


def _pad_heads_cols(w, heads, width):
    k = w.shape[0]
    w = w.reshape(k, heads, width)
    return jnp.pad(w, ((0, 0), (0, 0), (0, SLAB - width))).reshape(k, heads * SLAB)


def _unpad_heads_cols(w, heads, width):
    k = w.shape[0]
    return w.reshape(k, heads, SLAB)[:, :, :width].reshape(k, heads * width)


def _pad_heads_rows(w, heads, width):
    n = w.shape[1]
    w = w.reshape(heads, width, n)
    return jnp.pad(w, ((0, 0), (0, SLAB - width), (0, 0))).reshape(heads * SLAB, n)


def _unpad_heads_rows(w, heads, width):
    n = w.shape[1]
    return w.reshape(heads, SLAB, n)[:, :width, :].reshape(heads * width, n)


def _pad_w_in(w_in):
    o = 2 * D_MODEL
    qa = _pad_heads_cols(w_in[:, o:o + 512], N_HEADS, HEAD_A)
    ka = _pad_heads_cols(w_in[:, o + 512:o + 640], N_KV_A, HEAD_A)
    va = _pad_heads_cols(w_in[:, o + 640:o + 768], N_KV_A, HEAD_A)
    kr = jnp.pad(w_in[:, o + 1152:o + 1184], ((0, 0), (QK_NOPE, SLAB - QK_NOPE - QK_ROPE)))
    return jnp.concatenate([w_in[:, :o], qa, ka, va, w_in[:, o + 768:o + 1152], kr], axis=1)


def _unpad_w_in(w):
    qa = _unpad_heads_cols(w[:, C_QA:C_KA], N_HEADS, HEAD_A)
    ka = _unpad_heads_cols(w[:, C_KA:C_VA], N_KV_A, HEAD_A)
    va = _unpad_heads_cols(w[:, C_VA:C_CQ], N_KV_A, HEAD_A)
    kr = w[:, C_KR + QK_NOPE:C_KR + QK_NOPE + QK_ROPE]
    return jnp.concatenate([w[:, :C_QA], qa, ka, va, w[:, C_CQ:C_KR], kr], axis=1)


def _pad_w_kvb(w_kvb):
    w = w_kvb.reshape(KV_LORA, N_HEADS, QK_NOPE + V_DIM_B)
    k = jnp.pad(w[:, :, :QK_NOPE], ((0, 0), (0, 0), (0, SLAB - QK_NOPE))).reshape(KV_LORA, HM)
    v = jnp.pad(w[:, :, QK_NOPE:], ((0, 0), (0, 0), (0, SLAB - V_DIM_B))).reshape(KV_LORA, HM)
    return jnp.concatenate([k, v], axis=1)


def _unpad_w_kvb(w):
    k = w[:, :HM].reshape(KV_LORA, N_HEADS, SLAB)[:, :, :QK_NOPE]
    v = w[:, HM:].reshape(KV_LORA, N_HEADS, SLAB)[:, :, :V_DIM_B]
    return jnp.concatenate([k, v], axis=2).reshape(KV_LORA, N_HEADS * (QK_NOPE + V_DIM_B))


def _col_shards(w):
    k, n = w.shape
    ns = n // N_DEV
    if ns % SLAB:
        return jnp.stack([w[:, d * ns:(d + 1) * ns] for d in range(N_DEV)])
    return w.reshape(k, N_DEV, ns).transpose(1, 0, 2)


def _from_col_shards(s):
    _, k, ns = s.shape
    if ns % SLAB:
        return jnp.concatenate([s[d] for d in range(N_DEV)], axis=1)
    return s.transpose(1, 0, 2).reshape(k, N_DEV * ns)


def _freq_row():
    freqs = ROPE_THETA ** (-jnp.arange(0, QK_ROPE, 2, dtype=F32) / QK_ROPE)
    return jnp.concatenate([jnp.zeros((QK_NOPE,), F32), freqs, freqs,
                            jnp.zeros((SLAB - QK_NOPE - QK_ROPE,), F32)]).reshape(1, SLAB)


SMALL_D_ROWS = ("pre_norm_mix", "post_norm_mix", "pre_norm_mlp", "post_norm_mlp")
SMALL_LAYOUT = tuple((name, i, 0, D_MODEL) for i, name in enumerate(SMALL_D_ROWS)) + (
    ("q_a_norm", 4, 0, Q_LORA), ("kv_a_norm", 4, 256, KV_LORA), ("sinks", 4, 384, N_HEADS))
SMALL_LOSS_ROW, SMALL_LOSS_OFF = 4, 512


def _pack_small(vals):
    row4 = jnp.concatenate([vals["q_a_norm"].reshape(-1), vals["kv_a_norm"].reshape(-1), vals["sinks"].reshape(-1),
                            jnp.zeros((SMALL_LOSS_OFF - 392,), F32), vals["loss"].reshape(-1),
                            jnp.zeros((1024 - SMALL_LOSS_OFF - 1,), F32)])
    rows = [vals[n].reshape(1024) for n in SMALL_D_ROWS] + [row4]
    return jnp.concatenate([jnp.stack(rows), jnp.zeros((SMALL_ROWS - 5, 1024), F32)], axis=0)


WEIGHT_ORDER = ("pre_norm_mix", "w_in", "q_a_norm", "w_q_b", "kv_a_norm", "w_kv_b", "sinks", "w_o_a", "w_o_b",
                "w_out", "post_norm_mix", "pre_norm_mlp", "w_up", "w_down", "post_norm_mlp")


def kernel(x, positions, pre_norm_mix, w_in, q_a_norm, w_q_b, kv_a_norm, w_kv_b, sinks, w_o_a, w_o_b, w_out, post_norm_mix, pre_norm_mlp, w_up, w_down, post_norm_mlp, loss_target, m_pre_norm_mix, m_w_in, m_q_a_norm, m_w_q_b, m_kv_a_norm, m_w_kv_b, m_sinks, m_w_o_a, m_w_o_b, m_w_out, m_post_norm_mix, m_pre_norm_mlp, m_w_up, m_w_down, m_post_norm_mlp, v_pre_norm_mix, v_w_in, v_q_a_norm, v_w_q_b, v_kv_a_norm, v_w_kv_b, v_sinks, v_w_o_a, v_w_o_b, v_w_out, v_post_norm_mix, v_pre_norm_mlp, v_w_up, v_w_down, v_post_norm_mlp):
    weights = dict(pre_norm_mix=pre_norm_mix, w_in=w_in, q_a_norm=q_a_norm, w_q_b=w_q_b, kv_a_norm=kv_a_norm,
                   w_kv_b=w_kv_b, sinks=sinks, w_o_a=w_o_a, w_o_b=w_o_b, w_out=w_out, post_norm_mix=post_norm_mix,
                   pre_norm_mlp=pre_norm_mlp, w_up=w_up, w_down=w_down, post_norm_mlp=post_norm_mlp)
    m_in = dict(pre_norm_mix=m_pre_norm_mix, w_in=m_w_in, q_a_norm=m_q_a_norm, w_q_b=m_w_q_b, kv_a_norm=m_kv_a_norm,
                w_kv_b=m_w_kv_b, sinks=m_sinks, w_o_a=m_w_o_a, w_o_b=m_w_o_b, w_out=m_w_out,
                post_norm_mix=m_post_norm_mix, pre_norm_mlp=m_pre_norm_mlp, w_up=m_w_up, w_down=m_w_down,
                post_norm_mlp=m_post_norm_mlp)
    v_in = dict(pre_norm_mix=v_pre_norm_mix, w_in=v_w_in, q_a_norm=v_q_a_norm, w_q_b=v_w_q_b, kv_a_norm=v_kv_a_norm,
                w_kv_b=v_w_kv_b, sinks=v_sinks, w_o_a=v_w_o_a, w_o_b=v_w_o_b, w_out=v_w_out,
                post_norm_mix=v_post_norm_mix, pre_norm_mlp=v_pre_norm_mlp, w_up=v_w_up, w_down=v_w_down,
                post_norm_mlp=v_post_norm_mlp)

    xs, pos, target = x[0], positions[0], loss_target[0]
    t = xs.shape[0]
    pos_col = pos.reshape(t, 1)
    pos_row = pos.reshape(1, t)
    g1, g2, g3, g4 = (weights[n] for n in SMALL_D_ROWS)
    g_q, g_kv = q_a_norm, kv_a_norm
    sink_vec = sinks.reshape(N_HEADS)
    shard = {n: weights[n][0].astype(BF16) for n in EARLY + LATE}

    tables, (e_in, e_qb, e_kvb) = _rope_tables(pos_col, _freq_row(), [shard[n] for n in EARLY])
    w_in_p = _pad_w_in(_from_col_shards(e_in))
    w_qb = _pad_heads_cols(_from_col_shards(e_qb), N_HEADS, QK_NOPE + QK_ROPE)
    w_kvb = _pad_w_kvb(_from_col_shards(e_kvb))

    (h, gates, qa, ka, va, cq, ckv, cqn, ckvn, kb, vb, qt, kt, vt) = _inproj_fwd(
        xs, g1, w_in_p, g_q, g_kv, w_kvb, w_qb.T, w_kvb[:, :HM].T, w_kvb[:, HM:].T, w_in_p[:, C_KR:].T, tables)
    out_a, lse_a = _swa_fwd(qa, ka, va, pos_col, pos_row, sink_vec)
    out_b, out_b_t, qt_lse, (l_oa, l_ob, l_out, w_up_s, l_down) = _mla_fwd(qt, kb, vt, [shard[n] for n in LATE])
    w_oa = _pad_heads_rows(_from_col_shards(l_oa), N_HEADS, HEAD_A)
    w_ob = _pad_heads_rows(_from_col_shards(l_ob), N_HEADS, V_DIM_B)
    w_out_f = l_out.reshape(D_MODEL, D_MODEL)
    w_down_f = l_down.reshape(D_FF, D_MODEL)

    oa_p, ob_p, merged, y, x1, h2 = _merge_fwd(out_a, out_b, gates, xs, w_oa, w_ob, w_out_f, g2, g3)
    a, du, dy2, dx1, loss, dg3, dg4 = _mlp_fwd_bwd(x1, h2, target, w_up_s, w_down_f, g3, g4)
    (dgates, d_oa, d_ob_t, dg2, dw_oa, dw_ob, dw_out) = _merge_bwd(
        dx1, y, gates, oa_p, ob_p, out_a, out_b, out_b_t, merged, w_oa, w_ob, w_out_f, g2)
    late_slices = [
        _col_shards(_unpad_heads_rows(dw_oa, N_HEADS, HEAD_A)).astype(BF16),
        _col_shards(_unpad_heads_rows(dw_ob, N_HEADS, V_DIM_B)).astype(BF16),
        dw_out.astype(BF16).reshape(N_DEV, D_MODEL // N_DEV, D_MODEL),
        _matmul_tn(h2, du, "dw_up", BF16, N_DEV),
        _matmul_tn(a, dy2, "dw_down", BF16).reshape(N_DEV, D_FF // N_DEV, D_MODEL),
    ]
    dqkv_a, dsink = _swa_bwd(qa, ka, va, out_a, d_oa, lse_a, pos_col, pos_row, sink_vec)
    dw_in_early = jnp.concatenate([_matmul_tn(h, dgates, "dw_in_gates"), _matmul_tn(h, dqkv_a, "dw_in_mixer_a"),
                                   jnp.zeros((D_MODEL, D_IN_PAD - C_CQ), F32)], axis=1)
    late_slices.append(_col_shards(_unpad_w_in(dw_in_early)).astype(BF16))
    dqb_t, dkb_t, dvb_t, late_parts = _mla_bwd(qt_lse, kb, kt, vb, d_ob_t, late_slices)
    w_in_early_parts = late_parts.pop()
    dproj_late, dgq, dgkv, dw_qb_t, dw_kvb_t = _latent_bwd(
        dqb_t, dkb_t, dvb_t, cq, ckv, cqn, ckvn, *tables[3:], g_q, g_kv, w_qb, w_kvb)
    dw_l = _matmul_tn(h, dproj_late, "dw_in_latents")
    late_cols = jnp.concatenate([dw_l[:, :Q_LORA + KV_LORA], dw_l[:, C_KR - C_CQ + QK_NOPE:C_KR - C_CQ + Q_HEAD_B]],
                                axis=1)
    shard_cols = w_in.shape[2]
    head = late_cols.shape[1] - shard_cols
    w_in_late = jnp.concatenate([
        jnp.zeros((N_DEV - 2, D_MODEL, shard_cols), F32),
        jnp.pad(late_cols[:, :head], ((0, 0), (shard_cols - head, 0)))[None], late_cols[:, head:][None]])
    early_slices = [
        w_in_late.astype(BF16),
        _col_shards(_unpad_heads_cols(dw_qb_t.T, N_HEADS, QK_NOPE + QK_ROPE)).astype(BF16),
        _col_shards(_unpad_w_kvb(dw_kvb_t.T)).astype(BF16),
    ]
    dx, dg1, early_parts = _inproj_bwd(dgates, dqkv_a, dproj_late, xs, dx1, g1, w_in_p, early_slices,
                                       only=[(N_DEV - 2, N_DEV - 1), None, None])
    small_grads = {"pre_norm_mix": dg1, "post_norm_mix": dg2, "pre_norm_mlp": dg3, "post_norm_mlp": dg4,
                   "q_a_norm": dgq, "kv_a_norm": dgkv, "sinks": dsink.reshape(N_HEADS, BLOCK).sum(axis=1),
                   "loss": loss[0, 0:1]}
    s_parts = _gather_small(_pack_small(small_grads))

    updates = {}
    all_parts = [[w_in_early_parts, early_parts[0]]] + [[p] for p in early_parts[1:] + late_parts]
    for name, parts in zip(EARLY + LATE, all_parts):
        outs = _adamw(parts, weights[name], m_in[name], v_in[name], "adamw_" + name)
        for kind, arr in zip(("g", "d", "m", "v"), outs):
            updates[kind, name] = arr
    small_out, loss_sum = _adamw_small(s_parts, weights, m_in, v_in)
    for name, outs in small_out.items():
        for kind, arr in zip(("g", "d", "m", "v"), outs):
            updates[kind, name] = arr
    results = [updates[kind, name] for kind in ("g", "d", "m", "v") for name in WEIGHT_ORDER]
    return (loss_sum.reshape(()), dx[None], *results)
```

```python
import functools

import numpy as np
import jax
import jax.numpy as jnp
from jax import lax
from jax.experimental import pallas as pl
from jax.experimental.pallas import tpu as pltpu

F32 = jnp.float32
BF16 = jnp.bfloat16

D_MODEL = 1024
D_FF = 4096
N_HEADS = 8
N_KV_A = 2
GROUP_A = N_HEADS // N_KV_A
HEAD_A = 64
QK_NOPE = 64
QK_ROPE = 32
V_DIM_B = 64
Q_LORA = 256
KV_LORA = 128
BLOCK = 128
SLAB = 128
ROPE_THETA = 10000.0
EPS = 1e-6
N_DEV = 8
NEG = -1e30

SCALE_A = HEAD_A ** -0.5
SCALE_B = (QK_NOPE + QK_ROPE) ** -0.5
LOG2E = 1.4426950408889634
SCORE_B = SCALE_B * LOG2E
MLA_HEADS_PER_STEP = 4
MLA_FWD_HEADS_PER_STEP = 8
Q_HEAD_B = QK_NOPE + QK_ROPE
ONES_ROWS = 16
SLOPES_A = tuple(2.0 ** (-8.0 * (h + 1) / N_HEADS) for h in range(N_HEADS))

ADAM_LR = 0.001
ADAM_B1 = 0.9
ADAM_B2 = 0.999
ADAM_EPS = 1e-08
ADAM_WD = 0.01
ADAM_STEP = 10

HM = N_HEADS * SLAB
C_GATES = 0
C_QA = 2 * D_MODEL
C_KA = C_QA + HM
C_VA = C_KA + N_KV_A * SLAB
C_CQ = C_VA + N_KV_A * SLAB
C_CKV = C_CQ + Q_LORA
C_KR = C_CKV + KV_LORA
D_IN_PAD = C_KR + SLAB

VMEM_LIMIT = 56 * 1024 * 1024
VMEM_LIMIT_MERGE_BWD = 60 * 1024 * 1024

EARLY = ("w_in", "w_q_b", "w_kv_b")
LATE = ("w_o_a", "w_o_b", "w_out", "w_up", "w_down")
ADAM_ROWS = 512
SMALL_ROWS = 8


def _token_tile(t):
    return min(256, t)


def _wide_token_tile(t):
    return min(512, t)


def _attn_tile(t):
    return 512 if t >= 2048 else 128


def _params(sem, vmem=VMEM_LIMIT):
    return pltpu.CompilerParams(dimension_semantics=sem, vmem_limit_bytes=vmem)


def _dot(a, b):
    return jnp.dot(a, b, preferred_element_type=F32)


def _dot_nt(a, b):
    return lax.dot_general(a, b, (((1,), (1,)), ((), ())), preferred_element_type=F32)


def _dot_tn(a, b):
    return lax.dot_general(a, b, (((0,), (0,)), ((), ())), preferred_element_type=F32)


def _rms_r(x):
    return lax.rsqrt(jnp.mean(x * x, axis=-1, keepdims=True) + EPS)


def _rms_bwd(x, r, g, dy):
    t = dy * g
    return r * t - x * (r * r * r) * jnp.mean(x * t, axis=-1, keepdims=True)


def _sigmoid(x):
    return 1.0 / (1.0 + jnp.exp(-x))


def _rope(x, c, s1, s2):
    return x * c + pltpu.roll(x, SLAB - 16, 1) * s1 + pltpu.roll(x, 16, 1) * s2


def _rope_bwd(d, c, s1, s2):
    return d * c + pltpu.roll(d * s1, 16, 1) + pltpu.roll(d * s2, SLAB - 16, 1)


def _roll_rows(x, shift):
    return jnp.concatenate([x[-shift:], x[:-shift]], axis=0)


def _rope_t(x, c, s1, s2):
    return x * c + _roll_rows(x, SLAB - 16) * s1 + _roll_rows(x, 16) * s2


def _rope_t_bwd(d, c, s1, s2):
    return d * c + _roll_rows(d * s1, 16) + _roll_rows(d * s2, SLAB - 16)


def _plant_rows(slab, row, vals):
    hi = vals.astype(BF16).astype(F32)
    lo = (vals - hi).astype(BF16).astype(F32)
    idx = lax.broadcasted_iota(jnp.int32, slab.shape, 0)
    return jnp.where(idx == row, -hi, jnp.where(idx == row + 1, -lo, slab))


def _row_spec(tm, n):
    return pl.BlockSpec((tm, n), lambda i: (i, 0))


def _col_spec(n, tm):
    return pl.BlockSpec((n, tm), lambda i: (0, i))


def _full_spec(shape):
    nd = len(shape)
    return pl.BlockSpec(shape, lambda i: (0,) * nd, pipeline_mode=pl.Buffered(1))


def _acc_rows(ref, val):
    @pl.when(pl.program_id(0) == 0)
    def _():
        ref[...] = jnp.zeros_like(ref)
    ref[...] += jnp.sum(val, axis=0, keepdims=True)


def _rope_tables(pos_col, freq_row, early):
    t = pos_col.shape[0]
    tm = _token_tile(t)
    n = len(early)

    def body(pos_ref, f_ref, *rest):
        shard_refs, (c_ref, s1_ref, s2_ref, ct_ref, s1t_ref, s2t_ref) = rest[:n], rest[n:n + 6]
        start, finish = _two_level_gather(shard_refs, rest[n + 6:2 * n + 6], *rest[2 * n + 6:])
        pl.when(pl.program_id(0) == 0)(start)
        ang = pos_ref[...].astype(F32) * f_ref[...]
        lane = lax.broadcasted_iota(jnp.int32, ang.shape, 1)
        s = jnp.sin(ang)
        c = jnp.cos(ang)
        s1 = jnp.where((lane >= 64) & (lane < 80), -s, 0.0)
        s2 = jnp.where((lane >= 80) & (lane < 96), s, 0.0)
        c_ref[...], s1_ref[...], s2_ref[...] = c, s1, s2
        ct_ref[...], s1t_ref[...], s2t_ref[...] = c.T, s1.T, s2.T
        pl.when(pl.program_id(0) == t // tm - 1)(finish)

    tab = jax.ShapeDtypeStruct((t, SLAB), F32)
    tabt = jax.ShapeDtypeStruct((SLAB, t), F32)
    outs = pl.pallas_call(
        body, name="rope_tables", grid=(t // tm,),
        in_specs=[_row_spec(tm, 1), _full_spec((1, SLAB))] + [ANY_SPEC] * n,
        out_specs=[_row_spec(tm, SLAB)] * 3 + [_col_spec(SLAB, tm)] * 3 + [ANY_SPEC] * n,
        out_shape=[tab] * 3 + [tabt] * 3 + [jax.ShapeDtypeStruct((N_DEV,) + a.shape, a.dtype) for a in early],
        scratch_shapes=_exchange_scratch(n),
        compiler_params=_params(("arbitrary",)),
    )(pos_col, freq_row, *early)
    return outs[:6], outs[6:]


def _inproj_fwd(x, g1, w_in, g_q, g_kv, w_kvb, w_qb_t, w_kb_t, w_vb_t, w_kr_t, tables):
    t = x.shape[0]
    tm = _wide_token_tile(t)

    def body(x_ref, g1_ref, win_ref, gq_ref, gkv_ref, wkvb_ref, wqbt_ref, wkbt_ref, wvbt_ref, wkrt_ref,
             c_ref, s1_ref, s2_ref, ct_ref, s1t_ref, s2t_ref,
             h_ref, gates_ref, qa_ref, ka_ref, va_ref, cq_ref, ckv_ref, cqn_ref, ckvn_ref,
             kb_ref, vb_ref, qt_ref, kt_ref, vt_ref):
        xv = x_ref[...]
        h = (xv * _rms_r(xv) * g1_ref[...]).astype(BF16)
        h_ref[...] = h
        proj = _dot(h, win_ref[...])
        gates_ref[...] = proj[:, C_GATES:C_QA].astype(BF16)
        qa_ref[...] = proj[:, C_QA:C_KA].astype(BF16)
        ka_ref[...] = proj[:, C_KA:C_VA].astype(BF16)
        va_ref[...] = proj[:, C_VA:C_CQ].astype(BF16)
        cq = proj[:, C_CQ:C_CKV]
        ckv = proj[:, C_CKV:C_KR]
        kr = proj[:, C_KR:D_IN_PAD]
        cq_ref[...] = cq
        ckv_ref[...] = ckv
        cqn = (cq * _rms_r(cq) * gq_ref[...]).astype(BF16)
        ckvn = (ckv * _rms_r(ckv) * gkv_ref[...]).astype(BF16)
        cqn_ref[...] = cqn
        ckvn_ref[...] = ckvn
        c, s1, s2 = c_ref[...], s1_ref[...], s2_ref[...]
        kvb = _dot(ckvn, wkvb_ref[...])
        kr_rot = _rope(kr, c, s1, s2)
        ct, s1t, s2t = ct_ref[...], s1t_ref[...], s2t_ref[...]
        q_t = _dot_nt(wqbt_ref[...], cqn)
        k_t = _dot_nt(wkbt_ref[...], ckvn)
        kr_t = _rope_t(_dot_nt(wkrt_ref[...], h), ct, s1t, s2t)
        k_lane = lax.broadcasted_iota(jnp.int32, (1, SLAB), 1)
        k_ones = jnp.where((k_lane == Q_HEAD_B) | (k_lane == Q_HEAD_B + 1), 1.0, 0.0)
        for hd in range(N_HEADS):
            sl = slice(hd * SLAB, (hd + 1) * SLAB)
            kb_ref[:, sl] = (kvb[:, sl] + kr_rot + k_ones).astype(BF16)
            qt_ref[sl, :] = (_rope_t(q_t[sl, :], ct, s1t, s2t) * SCORE_B).astype(BF16)
            kt_ref[sl, :] = (k_t[sl, :] + kr_t).astype(BF16)
        v_lane = lax.broadcasted_iota(jnp.int32, (1, HM), 1) & (SLAB - 1)
        v_ones = jnp.where((v_lane == V_DIM_B) | (v_lane == V_DIM_B + 1), 1.0, 0.0)
        vb_ref[...] = (kvb[:, HM:2 * HM] + v_ones).astype(BF16)
        pad_row = lax.broadcasted_iota(jnp.int32, (HM, 1), 0) & (SLAB - 1)
        ones_rows = jnp.where((pad_row >= V_DIM_B) & (pad_row < V_DIM_B + ONES_ROWS), 1.0, 0.0)
        vt_ref[...] = (_dot_nt(wvbt_ref[...], ckvn) + ones_rows).astype(BF16)

    def sds(n, dt):
        return jax.ShapeDtypeStruct((t, n), dt)

    outs = [(D_MODEL, BF16), (2 * D_MODEL, BF16), (HM, BF16), (N_KV_A * SLAB, BF16), (N_KV_A * SLAB, BF16),
            (Q_LORA, F32), (KV_LORA, F32), (Q_LORA, BF16), (KV_LORA, BF16), (HM, BF16), (HM, BF16)]
    tab, tabt = _row_spec(tm, SLAB), _col_spec(SLAB, tm)
    return pl.pallas_call(
        body, name="inproj_fwd", grid=(t // tm,),
        in_specs=[_row_spec(tm, D_MODEL), _full_spec((1, D_MODEL)), _full_spec((D_MODEL, D_IN_PAD)),
                  _full_spec((1, Q_LORA)), _full_spec((1, KV_LORA)), _full_spec((KV_LORA, 2 * HM)),
                  _full_spec((HM, Q_LORA)), _full_spec((HM, KV_LORA)), _full_spec((HM, KV_LORA)),
                  _full_spec((SLAB, D_MODEL)), tab, tab, tab, tabt, tabt, tabt],
        out_specs=[_row_spec(tm, n) for n, _ in outs] + [_col_spec(HM, tm)] * 3,
        out_shape=[sds(n, dt) for n, dt in outs] + [jax.ShapeDtypeStruct((HM, t), BF16)] * 3,
        compiler_params=_params(("parallel",)),
    )(x, g1, w_in, g_q, g_kv, w_kvb, w_qb_t, w_kb_t, w_vb_t, w_kr_t, *tables)


def _tile_group(a):
    return jnp.concatenate([a] * GROUP_A, axis=1)


def _swa_masks():
    row = lax.broadcasted_iota(jnp.int32, (BLOCK, GROUP_A * BLOCK), 0)
    col = lax.broadcasted_iota(jnp.int32, (BLOCK, GROUP_A * BLOCK), 1) & (BLOCK - 1)
    return row <= col, row > col


def _heads_beside(ref, g):
    return jnp.concatenate([ref[:, (g * GROUP_A + hh) * SLAB:(g * GROUP_A + hh + 1) * SLAB].T
                            for hh in range(GROUP_A)], axis=1)


def _rows_beside(ref, g):
    return jnp.concatenate([ref[g * GROUP_A + hh] for hh in range(GROUP_A)], axis=1)


def _swa_rows(sinks):
    slopes = jnp.repeat(jnp.asarray(SLOPES_A, F32).reshape(N_KV_A, GROUP_A, 1), BLOCK, axis=2)
    sink_rows = jnp.repeat(sinks.reshape(N_KV_A, GROUP_A, 1), BLOCK, axis=2)
    return slopes.reshape(N_KV_A, 1, GROUP_A * BLOCK), sink_rows.reshape(N_KV_A, 1, GROUP_A * BLOCK)


def _swa_fwd(qa, ka, va, pos_col, pos_row, sinks):
    t = qa.shape[0]
    nb = t // BLOCK
    gw = GROUP_A * BLOCK
    slope_rows, sink_rows = _swa_rows(sinks)

    def body(q_ref, kc_ref, kp_ref, vc_ref, vp_ref, pkc_ref, pkp_ref, pq_ref, slope_ref, sink_ref, o_ref, l_ref):
        i = pl.program_id(0)
        pq = pq_ref[...]
        dist_c = _tile_group(jnp.abs(pkc_ref[...] - pq).astype(F32))
        dist_p = _tile_group(jnp.abs(pkp_ref[...] - pq).astype(F32))
        mask_c, older = _swa_masks()
        mask_p = jnp.logical_and(older, i > 0)
        raw = []
        for g in range(N_KV_A):
            gs = slice(g * SLAB, (g + 1) * SLAB)
            x = _heads_beside(q_ref, g)
            raw.append((_dot(kc_ref[:, gs], x), _dot(kp_ref[:, gs], x)))
        for g in range(N_KV_A):
            gs = slice(g * SLAB, (g + 1) * SLAB)
            slope, sink = slope_ref[g], sink_ref[g]
            s_c = jnp.where(mask_c, raw[g][0] * SCALE_A - slope * dist_c, NEG)
            s_p = jnp.where(mask_p, raw[g][1] * SCALE_A - slope * dist_p, NEG)
            m = jnp.maximum(jnp.maximum(jnp.max(s_c, axis=0, keepdims=True),
                                        jnp.max(s_p, axis=0, keepdims=True)), sink)
            e_c = jnp.exp(s_c - m)
            e_p = jnp.exp(s_p - m)
            den = jnp.sum(e_c, axis=0, keepdims=True) + jnp.sum(e_p, axis=0, keepdims=True) + jnp.exp(sink - m)
            inv = 1.0 / den
            ot = (_dot_tn(vc_ref[:, gs], (e_c * inv).astype(BF16))
                  + _dot_tn(vp_ref[:, gs], (e_p * inv).astype(BF16)))
            lse = m + jnp.log(den)
            for hh in range(GROUP_A):
                hd = g * GROUP_A + hh
                seg = slice(hh * BLOCK, (hh + 1) * BLOCK)
                o_ref[:, hd * SLAB:(hd + 1) * SLAB] = ot[:, seg].T.astype(BF16)
                l_ref[hd] = lse[:, seg]

    cur = lambda i: (i, 0)
    prev = lambda i: (jnp.maximum(i - 1, 0), 0)
    kvw = N_KV_A * SLAB
    rows = pl.BlockSpec((N_KV_A, 1, gw), lambda i: (0, 0, 0))
    return pl.pallas_call(
        body, name="swa_fwd", grid=(nb,),
        in_specs=[pl.BlockSpec((BLOCK, HM), cur),
                  pl.BlockSpec((BLOCK, kvw), cur), pl.BlockSpec((BLOCK, kvw), prev),
                  pl.BlockSpec((BLOCK, kvw), cur), pl.BlockSpec((BLOCK, kvw), prev),
                  pl.BlockSpec((BLOCK, 1), cur), pl.BlockSpec((BLOCK, 1), prev),
                  pl.BlockSpec((1, BLOCK), lambda i: (0, i)), rows, rows],
        out_specs=[pl.BlockSpec((BLOCK, HM), cur), pl.BlockSpec((N_HEADS, 1, BLOCK), lambda i: (0, 0, i))],
        out_shape=[jax.ShapeDtypeStruct((t, HM), BF16), jax.ShapeDtypeStruct((N_HEADS, 1, t), F32)],
        compiler_params=_params(("parallel",)),
    )(qa, ka, ka, va, va, pos_col, pos_col, pos_row, slope_rows, sink_rows)


def _swa_bwd(qa, ka, va, out_a, d_oa, lse, pos_col, pos_row, sinks):
    t = qa.shape[0]
    nb = t // BLOCK
    gw = GROUP_A * BLOCK
    kvw = N_KV_A * SLAB
    slope_rows, sink_rows = _swa_rows(sinks)

    def body(q_ref, qn_ref, do_ref, don_ref, l_ref, ln_ref, o_ref, on_ref, kp_ref, kc_ref, vp_ref, vc_ref,
             pkp_ref, pkc_ref, pq_ref, pqn_ref, slope_ref, sink_ref, dqkv_ref, dsink_ref):
        j = pl.program_id(0)
        pkc, pkp = pkc_ref[...], pkp_ref[...]
        dist_cc = _tile_group(jnp.abs(pkc - pq_ref[...]).astype(F32))
        dist_cp = _tile_group(jnp.abs(pkp - pq_ref[...]).astype(F32))
        dist_nc = _tile_group(jnp.abs(pkc - pqn_ref[...]).astype(F32))
        mask_cc, older = _swa_masks()
        mask_cp = jnp.logical_and(older, j > 0)
        mask_nc = jnp.logical_and(older, j < nb - 1)

        @pl.when(j == 0)
        def _():
            dsink_ref[...] = jnp.zeros_like(dsink_ref)

        def tile(k, v, x, dox, lrow, drow, dist, mask, slope):
            s = jnp.where(mask, _dot(k, x) * SCALE_A - slope * dist, NEG)
            p = jnp.exp(s - lrow)
            ds = p * (_dot(v, dox) - drow)
            return p.astype(BF16), ds.astype(BF16)

        for g in range(N_KV_A):
            gs = slice(g * SLAB, (g + 1) * SLAB)
            kc, kp, vc, vp = kc_ref[:, gs], kp_ref[:, gs], vc_ref[:, gs], vp_ref[:, gs]
            slope, sink = slope_ref[g], sink_ref[g]
            x, xn = _heads_beside(q_ref, g), _heads_beside(qn_ref, g)
            dox, doxn = _heads_beside(do_ref, g), _heads_beside(don_ref, g)
            lrow, lrown = _rows_beside(l_ref, g), _rows_beside(ln_ref, g)
            drow = jnp.sum(dox.astype(F32) * _heads_beside(o_ref, g).astype(F32), axis=0, keepdims=True)
            drown = jnp.sum(doxn.astype(F32) * _heads_beside(on_ref, g).astype(F32), axis=0, keepdims=True)
            p_cc, ds_cc = tile(kc, vc, x, dox, lrow, drow, dist_cc, mask_cc, slope)
            _, ds_cp = tile(kp, vp, x, dox, lrow, drow, dist_cp, mask_cp, slope)
            p_nc, ds_nc = tile(kc, vc, xn, doxn, lrown, drown, dist_nc, mask_nc, slope)
            dqt = (_dot_tn(kc, ds_cc) + _dot_tn(kp, ds_cp)) * SCALE_A
            for hh in range(GROUP_A):
                hd = g * GROUP_A + hh
                dqkv_ref[:, hd * SLAB:(hd + 1) * SLAB] = dqt[:, hh * BLOCK:(hh + 1) * BLOCK].T.astype(BF16)
            dqkv_ref[:, HM + g * SLAB:HM + (g + 1) * SLAB] = (
                (_dot_nt(ds_cc, x) + _dot_nt(ds_nc, xn)) * SCALE_A).astype(BF16)
            dqkv_ref[:, HM + kvw + g * SLAB:HM + kvw + (g + 1) * SLAB] = (
                _dot_nt(p_cc, dox) + _dot_nt(p_nc, doxn)).astype(BF16)
            dsink_ref[g] -= jnp.exp(sink - lrow) * drow

    cur = lambda j: (j, 0)
    prev = lambda j: (jnp.maximum(j - 1, 0), 0)
    nxt = lambda j: (jnp.minimum(j + 1, nb - 1), 0)
    cur3 = lambda j: (0, 0, j)
    nxt3 = lambda j: (0, 0, jnp.minimum(j + 1, nb - 1))
    kvw = N_KV_A * SLAB
    rows = pl.BlockSpec((N_KV_A, 1, gw), lambda j: (0, 0, 0))
    stat = lambda im: pl.BlockSpec((N_HEADS, 1, BLOCK), im)
    return pl.pallas_call(
        body, name="swa_bwd", grid=(nb,),
        in_specs=[pl.BlockSpec((BLOCK, HM), cur), pl.BlockSpec((BLOCK, HM), nxt),
                  pl.BlockSpec((BLOCK, HM), cur), pl.BlockSpec((BLOCK, HM), nxt),
                  stat(cur3), stat(nxt3), pl.BlockSpec((BLOCK, HM), cur), pl.BlockSpec((BLOCK, HM), nxt),
                  pl.BlockSpec((BLOCK, kvw), prev), pl.BlockSpec((BLOCK, kvw), cur),
                  pl.BlockSpec((BLOCK, kvw), prev), pl.BlockSpec((BLOCK, kvw), cur),
                  pl.BlockSpec((BLOCK, 1), prev), pl.BlockSpec((BLOCK, 1), cur),
                  pl.BlockSpec((1, BLOCK), lambda j: (0, j)),
                  pl.BlockSpec((1, BLOCK), lambda j: (0, jnp.minimum(j + 1, nb - 1))), rows, rows],
        out_specs=[pl.BlockSpec((BLOCK, HM + 2 * kvw), cur), rows],
        out_shape=[jax.ShapeDtypeStruct((t, HM + 2 * kvw), BF16), jax.ShapeDtypeStruct((N_KV_A, 1, gw), F32)],
        compiler_params=_params(("arbitrary",)),
    )(qa, qa, d_oa, d_oa, lse, lse, out_a, out_a, ka, ka, va, va,
      pos_col, pos_col, pos_row, pos_row, slope_rows, sink_rows)


def _mesh_pos():
    return lax.axis_index("x"), lax.axis_index("y"), lax.axis_index("c")


def _flip(v, bit):
    return 1 - v if bit else v


def _direct_copies(srcs, dsts, send_sems, recv_sems, local_sems, gather, sem_base=0, only=None):
    x, y, c = _mesh_pos()
    me = 4 * x + 2 * y + c

    def among(idx, dests):
        ok = idx == dests[0]
        for d in dests[1:]:
            ok = jnp.logical_or(ok, idx == d)
        return ok

    local, remote = [], []
    for a, (src, dst) in enumerate(zip(srcs, dsts)):
        dests = None if only is None else only[a]
        recv_ok = None if dests is None else among(me, dests)
        local.append((pltpu.make_async_copy(src if gather else src.at[me], dst.at[me],
                                            local_sems.at[sem_base + a]), recv_ok))
        for r in range(1, N_DEV):
            px, py, pc = _flip(x, r & 4), _flip(y, r & 2), _flip(c, r & 1)
            peer = 4 * px + 2 * py + pc
            sem = (N_DEV - 1) * (sem_base + a) + r - 1
            copy = pltpu.make_async_remote_copy(
                src_ref=src if gather else src.at[peer], dst_ref=dst.at[me],
                send_sem=send_sems.at[sem], recv_sem=recv_sems.at[sem],
                device_id=(px, py, pc), device_id_type=pl.DeviceIdType.MESH)
            remote.append((copy, None if dests is None else among(peer, dests), recv_ok))
    return local, remote


def _when(cond, fn):
    if cond is None:
        fn()
    else:
        pl.when(cond)(fn)


def _start_copies(local, remote):
    for cp, ok in local:
        _when(ok, cp.start)
    for cp, send_ok, _ in remote:
        _when(send_ok, cp.start)


def _wait_copies(local, remote):
    for cp, _, recv_ok in remote:
        _when(recv_ok, cp.wait_recv)
    for cp, send_ok, _ in remote:
        _when(send_ok, cp.wait_send)
    for cp, ok in local:
        _when(ok, cp.wait)


def _exchange_scratch(n):
    return [pltpu.SemaphoreType.DMA((n * (N_DEV - 1),)), pltpu.SemaphoreType.DMA((n * (N_DEV - 1),)),
            pltpu.SemaphoreType.DMA((n,))]


ANY_SPEC = pl.BlockSpec(memory_space=pl.ANY)


def _mla_fwd(qt, kb, vt, late):
    t = kb.shape[0]
    tk = _attn_tile(t)
    ratio = 2 if t >= 2 * tk else 1
    tq = ratio * tk
    nq = t // tq
    hps = MLA_FWD_HEADS_PER_STEP
    w = hps * SLAB
    pairs = [(i, j) for i in range(nq) for j in range(ratio * (i + 1))]
    i_tab = jnp.asarray(np.array([p[0] for p in pairs], np.int32))
    j_tab = jnp.asarray(np.array([p[1] for p in pairs], np.int32))

    n_late = len(late)

    def body(it_ref, jt_ref, qt_ref, k_ref, vt_ref, *rest):
        late_refs, (o_ref, ot_ref, qa_ref) = rest[:n_late], rest[n_late:n_late + 3]
        gathered_refs = rest[n_late + 3:2 * n_late + 3]
        m_s, acc_s, send_sems, recv_sems, local_sems = rest[2 * n_late + 3:]
        n = pl.program_id(1)
        i, j = it_ref[n], jt_ref[n]
        first_step = jnp.logical_and(pl.program_id(0) == 0, n == 0)
        last_step = jnp.logical_and(pl.program_id(0) == N_HEADS // hps - 1, n == len(pairs) - 1)

        @pl.when(first_step)
        def _():
            _start_copies(*_direct_copies(late_refs, gathered_refs, send_sems, recv_sems, local_sems, True))

        @pl.when(j == 0)
        def _():
            m_s[...] = jnp.full_like(m_s, NEG)
            acc_s[...] = jnp.zeros_like(acc_s)

        def update(masked, q0):
            qc = slice(q0, tq)

            def scores(hh):
                sl = slice(hh * SLAB, (hh + 1) * SLAB)
                return _dot(k_ref[:, sl], qt_ref[sl, qc])

            def softmax(hh, s):
                if masked:
                    s = jnp.where(lax.broadcasted_iota(jnp.int32, s.shape, 0)
                                  <= lax.broadcasted_iota(jnp.int32, s.shape, 1), s, NEG)
                m_old = m_s[hh][:, qc]
                m_new = jnp.maximum(m_old, jnp.max(s, axis=0, keepdims=True))
                m_s[hh, :, qc] = m_new
                return jnp.exp2(s - m_new).astype(BF16), jnp.exp2(m_old - m_new)

            def accumulate(hh, p, alpha):
                sl = slice(hh * SLAB, hh * SLAB + V_DIM_B + ONES_ROWS)
                acc_s[sl, qc] = alpha * acc_s[sl, qc] + _dot(vt_ref[sl, :], p)

            s_next, pending = scores(0), None
            for hh in range(hps):
                s = s_next
                if hh + 1 < hps:
                    s_next = scores(hh + 1)
                p, alpha = softmax(hh, s)
                if pending is not None:
                    accumulate(*pending)
                pending = (hh, p, alpha)
            accumulate(*pending)

        @pl.when(j < ratio * i)
        def _():
            update(False, 0)

        for part in range(ratio):
            @pl.when(j == ratio * i + part)
            def _():
                update(True, part * tk)

        @pl.when(j == ratio * i + ratio - 1)
        def _():
            for hh in range(hps):
                sl = slice(hh * SLAB, (hh + 1) * SLAB)
                den = acc_s[hh * SLAB + V_DIM_B:hh * SLAB + V_DIM_B + 1, :]
                values = lax.broadcasted_iota(jnp.int32, (SLAB, tq), 0) < V_DIM_B
                ot = jnp.where(values, acc_s[sl, :] / den, 0.0)
                ot_ref[sl, :] = ot.astype(BF16)
                o_ref[:, sl] = ot.T.astype(BF16)
                lse = m_s[hh] + jnp.log2(den)
                qa_ref[sl, :] = _plant_rows(qt_ref[sl, :].astype(F32), Q_HEAD_B, lse).astype(BF16)

        @pl.when(last_step)
        def _():
            _wait_copies(*_direct_copies(late_refs, gathered_refs, send_sems, recv_sems, local_sems, True))

    grid_spec = pltpu.PrefetchScalarGridSpec(
        num_scalar_prefetch=2, grid=(N_HEADS // hps, len(pairs)),
        in_specs=[pl.BlockSpec((w, tq), lambda h, n, it, jt: (h, it[n])),
                  pl.BlockSpec((tk, w), lambda h, n, it, jt: (jt[n], h)),
                  pl.BlockSpec((w, tk), lambda h, n, it, jt: (h, jt[n]))] + [ANY_SPEC] * n_late,
        out_specs=[pl.BlockSpec((tq, w), lambda h, n, it, jt: (it[n], h)),
                   pl.BlockSpec((w, tq), lambda h, n, it, jt: (h, it[n])),
                   pl.BlockSpec((w, tq), lambda h, n, it, jt: (h, it[n]))] + [ANY_SPEC] * n_late,
        scratch_shapes=[pltpu.VMEM((hps, 1, tq), F32), pltpu.VMEM((w, tq), F32)] + _exchange_scratch(n_late))
    outs = pl.pallas_call(
        body, name="mla_fwd", grid_spec=grid_spec,
        out_shape=[jax.ShapeDtypeStruct((t, HM), BF16), jax.ShapeDtypeStruct((HM, t), BF16),
                   jax.ShapeDtypeStruct((HM, t), BF16)]
        + [jax.ShapeDtypeStruct((N_DEV,) + a.shape, a.dtype) for a in late],
        compiler_params=_params(("arbitrary", "arbitrary")),
    )(i_tab, j_tab, qt, kb, vt, *late)
    return outs[0], outs[1], outs[2], list(outs[3:])


def _mla_bwd(qt, kb, kt, vb, d_ob_t, grad_slices):
    t = kb.shape[0]
    tk = _attn_tile(t)
    ratio = 2 if t >= 2 * tk else 1
    tq = ratio * tk
    nk, nq = t // tk, t // tq
    hps = MLA_HEADS_PER_STEP
    w = hps * SLAB
    pairs = [(j, i) for j in range(nk) for i in range(j // ratio, nq)]
    j_tab = jnp.asarray(np.array([p[0] for p in pairs], np.int32))
    i_tab = jnp.asarray(np.array([p[1] for p in pairs], np.int32))

    n_ex = len(grad_slices)

    def body(jt_ref, it_ref, qt_ref, dot_ref, k_ref, kt_ref, v_ref, *rest):
        slice_refs, (dqt_ref, dkt_ref, dvt_ref) = rest[:n_ex], rest[n_ex:n_ex + 3]
        part_refs = rest[n_ex + 3:2 * n_ex + 3]
        dk_s, dv_s, send_sems, recv_sems, local_sems = rest[2 * n_ex + 3:]
        n = pl.program_id(1)
        j, i = jt_ref[n], it_ref[n]
        first_step = jnp.logical_and(pl.program_id(0) == 0, n == 0)
        last_step = jnp.logical_and(pl.program_id(0) == N_HEADS // hps - 1, n == len(pairs) - 1)

        @pl.when(first_step)
        def _():
            _start_copies(*_direct_copies(slice_refs, part_refs, send_sems, recv_sems, local_sems, False))

        @pl.when(n == 0)
        def _():
            dqt_ref[...] = jnp.zeros_like(dqt_ref)

        def update(diagonal, q0):
            qc = slice(q0, tq)
            cols = pl.ds(pl.multiple_of(i * tq + q0, tk), tq - q0)

            def softmax_bwd(hh, s, dp):
                if diagonal:
                    s = jnp.where(lax.broadcasted_iota(jnp.int32, s.shape, 0)
                                  <= lax.broadcasted_iota(jnp.int32, s.shape, 1), s, NEG)
                p = jnp.exp2(s)
                return p.astype(BF16), (p * dp).astype(BF16)

            def gradients(hh, p, ds):
                base = hh * SLAB
                vrows = slice(base, base + V_DIM_B)
                qrows = slice(base, base + QK_NOPE + QK_ROPE)
                dv = _dot_nt(dot_ref[vrows, qc], p)
                dk = _dot_nt(qt_ref[qrows, qc], ds)
                if diagonal:
                    dv_s[base:base + SLAB, :] = jnp.concatenate([dv, jnp.zeros((SLAB - V_DIM_B, tk), F32)], axis=0)
                    dk_s[base:base + SLAB, :] = jnp.concatenate(
                        [dk, jnp.zeros((SLAB - QK_NOPE - QK_ROPE, tk), F32)], axis=0)
                else:
                    dv_s[vrows, :] += dv
                    dk_s[qrows, :] += dk
                dqt_ref[qrows, cols] += _dot(kt_ref[qrows, :], ds)

            def scores(hh):
                sl = slice(hh * SLAB, (hh + 1) * SLAB)
                return _dot(k_ref[:, sl], qt_ref[sl, qc])

            def dprod(hh):
                sl = slice(hh * SLAB, (hh + 1) * SLAB)
                return _dot(v_ref[:, sl], dot_ref[sl, qc])

            s_next = scores(0)
            for hh in range(hps):
                s = s_next
                dp = dprod(hh)
                if hh + 1 < hps:
                    s_next = scores(hh + 1)
                gradients(hh, *softmax_bwd(hh, s, dp))

        first_tile = lax.div(j, ratio)
        for part in range(ratio):
            @pl.when(jnp.logical_and(i == first_tile, lax.rem(j, ratio) == part))
            def _():
                update(True, part * tk)

        @pl.when(i > first_tile)
        def _():
            update(False, 0)

        @pl.when(i == nq - 1)
        def _():
            dkt_ref[...] = (dk_s[...] * (1.0 / LOG2E)).astype(BF16)
            dvt_ref[...] = dv_s[...].astype(BF16)

        @pl.when(last_step)
        def _():
            _wait_copies(*_direct_copies(slice_refs, part_refs, send_sems, recv_sems, local_sems, False))

    grid_spec = pltpu.PrefetchScalarGridSpec(
        num_scalar_prefetch=2, grid=(N_HEADS // hps, len(pairs)),
        in_specs=[pl.BlockSpec((w, tq), lambda h, n, jt, it: (h, it[n])),
                  pl.BlockSpec((w, tq), lambda h, n, jt, it: (h, it[n])),
                  pl.BlockSpec((tk, w), lambda h, n, jt, it: (jt[n], h)),
                  pl.BlockSpec((w, tk), lambda h, n, jt, it: (h, jt[n])),
                  pl.BlockSpec((tk, w), lambda h, n, jt, it: (jt[n], h))] + [ANY_SPEC] * n_ex,
        out_specs=[pl.BlockSpec((w, t), lambda h, n, jt, it: (h, 0)),
                   pl.BlockSpec((w, tk), lambda h, n, jt, it: (h, jt[n])),
                   pl.BlockSpec((w, tk), lambda h, n, jt, it: (h, jt[n]))] + [ANY_SPEC] * n_ex,
        scratch_shapes=[pltpu.VMEM((w, tk), F32), pltpu.VMEM((w, tk), F32)] + _exchange_scratch(n_ex))
    outs = pl.pallas_call(
        body, name="mla_bwd", grid_spec=grid_spec,
        out_shape=[jax.ShapeDtypeStruct((HM, t), F32), jax.ShapeDtypeStruct((HM, t), BF16),
                   jax.ShapeDtypeStruct((HM, t), BF16)]
        + [jax.ShapeDtypeStruct(a.shape, a.dtype) for a in grad_slices],
        compiler_params=_params(("arbitrary", "arbitrary")),
    )(j_tab, i_tab, qt, d_ob_t, kb, kt, vb, *grad_slices)
    return outs[0], outs[1], outs[2], list(outs[3:])


def _merge_fwd(out_a, out_b, gates, x, w_oa, w_ob, w_out, g2, g3):
    t = x.shape[0]
    tm = _wide_token_tile(t)

    def body(oa_ref, ob_ref, gates_ref, x_ref, woa_ref, wob_ref, wout_ref, g2_ref, g3_ref,
             oap_ref, obp_ref, merged_ref, y_ref, x1_ref, h2_ref):
        oa_p = _dot(oa_ref[...], woa_ref[...])
        ob_p = _dot(ob_ref[...], wob_ref[...])
        oap_ref[...] = oa_p.astype(BF16)
        obp_ref[...] = ob_p.astype(BF16)
        sa = _sigmoid(gates_ref[:, 0:D_MODEL].astype(F32))
        sb = _sigmoid(gates_ref[:, D_MODEL:2 * D_MODEL].astype(F32))
        merged = (sa * oa_p + sb * ob_p).astype(BF16)
        merged_ref[...] = merged
        y = _dot(merged, wout_ref[...])
        y_ref[...] = y
        x1 = x_ref[...] + y * _rms_r(y) * g2_ref[...]
        x1_ref[...] = x1
        h2_ref[...] = (x1 * _rms_r(x1) * g3_ref[...]).astype(BF16)

    def sds(dt):
        return jax.ShapeDtypeStruct((t, D_MODEL), dt)

    row = _row_spec(tm, D_MODEL)
    return pl.pallas_call(
        body, name="merge_fwd", grid=(t // tm,),
        in_specs=[_row_spec(tm, HM), _row_spec(tm, HM), _row_spec(tm, 2 * D_MODEL), row,
                  _full_spec((HM, D_MODEL)), _full_spec((HM, D_MODEL)), _full_spec((D_MODEL, D_MODEL)),
                  _full_spec((1, D_MODEL)), _full_spec((1, D_MODEL))],
        out_specs=[row] * 6,
        out_shape=[sds(BF16), sds(BF16), sds(BF16), sds(F32), sds(F32), sds(BF16)],
        compiler_params=_params(("parallel",)),
    )(out_a, out_b, gates, x, w_oa, w_ob, w_out, g2, g3)


def _merge_bwd(dx1, y, gates, oa_p, ob_p, out_a, out_b, out_b_t, merged, w_oa, w_ob, w_out, g2):
    t = dx1.shape[0]
    tm = _wide_token_tile(t)

    def body(dx1_ref, y_ref, gates_ref, oap_ref, obp_ref, oa_ref, ob_ref, obt_ref, merged_ref,
             woa_ref, wob_ref, wout_ref, g2_ref,
             dgates_ref, doa_ref, dobt_ref, dg2_ref, dwoa_ref, dwob_ref, dwout_ref):
        @pl.when(pl.program_id(0) == 0)
        def _():
            dwoa_ref[...] = jnp.zeros_like(dwoa_ref)
            dwob_ref[...] = jnp.zeros_like(dwob_ref)
            dwout_ref[...] = jnp.zeros_like(dwout_ref)

        dx1v = dx1_ref[...]
        yv = y_ref[...]
        r2 = _rms_r(yv)
        _acc_rows(dg2_ref, dx1v * yv * r2)
        dy = _rms_bwd(yv, r2, g2_ref[...], dx1v).astype(BF16)
        dwout_ref[...] += _dot_tn(merged_ref[...], dy)
        dm = _dot_nt(dy, wout_ref[...])
        sa = _sigmoid(gates_ref[:, 0:D_MODEL].astype(F32))
        sb = _sigmoid(gates_ref[:, D_MODEL:2 * D_MODEL].astype(F32))
        d_oap = (dm * sa).astype(BF16)
        d_obp = (dm * sb).astype(BF16)
        dwoa_ref[...] += _dot_tn(oa_ref[...], d_oap)
        dwob_ref[...] += _dot_tn(ob_ref[...], d_obp)
        dgates_ref[:, 0:D_MODEL] = (dm * oap_ref[...].astype(F32) * sa * (1.0 - sa)).astype(BF16)
        dgates_ref[:, D_MODEL:2 * D_MODEL] = (dm * obp_ref[...].astype(F32) * sb * (1.0 - sb)).astype(BF16)
        doa_ref[...] = _dot_nt(d_oap, woa_ref[...]).astype(BF16)
        d_ob_t = _dot_nt(wob_ref[...], d_obp)
        for hd in range(N_HEADS):
            sl = slice(hd * SLAB, (hd + 1) * SLAB)
            delta = jnp.sum(d_ob_t[sl, :] * obt_ref[sl, :].astype(F32), axis=0, keepdims=True)
            dobt_ref[sl, :] = _plant_rows(d_ob_t[sl, :], V_DIM_B, delta).astype(BF16)

    def sds(n, dt):
        return jax.ShapeDtypeStruct((t, n), dt)

    row = _row_spec(tm, D_MODEL)
    return pl.pallas_call(
        body, name="merge_bwd", grid=(t // tm,),
        in_specs=[row, row, _row_spec(tm, 2 * D_MODEL), row, row, _row_spec(tm, HM), _row_spec(tm, HM),
                  _col_spec(HM, tm), row,
                  _full_spec((HM, D_MODEL)), _full_spec((HM, D_MODEL)), _full_spec((D_MODEL, D_MODEL)),
                  _full_spec((1, D_MODEL))],
        out_specs=[_row_spec(tm, 2 * D_MODEL), _row_spec(tm, HM), _col_spec(HM, tm), _full_spec((1, D_MODEL)),
                   _full_spec((HM, D_MODEL)), _full_spec((HM, D_MODEL)), _full_spec((D_MODEL, D_MODEL))],
        out_shape=[sds(2 * D_MODEL, BF16), sds(HM, BF16), jax.ShapeDtypeStruct((HM, t), BF16),
                   jax.ShapeDtypeStruct((1, D_MODEL), F32),
                   jax.ShapeDtypeStruct((HM, D_MODEL), F32), jax.ShapeDtypeStruct((HM, D_MODEL), F32),
                   jax.ShapeDtypeStruct((D_MODEL, D_MODEL), F32)],
        compiler_params=_params(("arbitrary",), VMEM_LIMIT_MERGE_BWD),
    )(dx1, y, gates, oa_p, ob_p, out_a, out_b, out_b_t, merged, w_oa, w_ob, w_out, g2)


def _mlp_fwd_bwd(x1, h2, target, w_up, w_down, g3, g4):
    t = x1.shape[0]
    tm = _token_tile(t)
    fs = D_FF // N_DEV

    def body(x1_ref, h2_ref, tgt_ref, wup_ref, wdown_ref, g3_ref, g4_ref,
             a_ref, du_ref, dy2_ref, dx1_ref, loss_ref, dg3_ref, dg4_ref):
        x1v = x1_ref[...]
        h2v = h2_ref[...]
        u = jnp.concatenate([_dot(h2v, wup_ref[s]) for s in range(N_DEV)], axis=1)
        ru = jnp.maximum(u, 0.0)
        a = (ru * ru).astype(BF16)
        a_ref[...] = a
        y2 = _dot(a, wdown_ref[...])
        r4 = _rms_r(y2)
        diff = x1v + y2 * r4 * g4_ref[...] - tgt_ref[...]
        _acc_rows(loss_ref, jnp.sum(diff * diff, axis=-1, keepdims=True) * (0.5 / D_MODEL)
                  * jnp.ones((1, SLAB), F32))
        dx2 = diff * (1.0 / D_MODEL)
        _acc_rows(dg4_ref, dx2 * y2 * r4)
        dy2 = _rms_bwd(y2, r4, g4_ref[...], dx2).astype(BF16)
        dy2_ref[...] = dy2
        du = (_dot_nt(dy2, wdown_ref[...]) * (2.0 * ru)).astype(BF16)
        du_ref[...] = du
        dh2 = _dot_nt(du[:, 0:fs], wup_ref[0])
        for s in range(1, N_DEV):
            dh2 += _dot_nt(du[:, s * fs:(s + 1) * fs], wup_ref[s])
        r3 = _rms_r(x1v)
        _acc_rows(dg3_ref, dh2 * x1v * r3)
        dx1_ref[...] = dx2 + _rms_bwd(x1v, r3, g3_ref[...], dh2)

    row = _row_spec(tm, D_MODEL)
    frow = _row_spec(tm, D_FF)
    vec = _full_spec((1, D_MODEL))
    return pl.pallas_call(
        body, name="mlp_fwd_bwd", grid=(t // tm,),
        in_specs=[row, row, row, _full_spec((N_DEV, D_MODEL, fs)), _full_spec((D_FF, D_MODEL)), vec, vec],
        out_specs=[frow, frow, row, row, _full_spec((1, SLAB)), vec, vec],
        out_shape=[jax.ShapeDtypeStruct((t, D_FF), BF16), jax.ShapeDtypeStruct((t, D_FF), BF16),
                   jax.ShapeDtypeStruct((t, D_MODEL), BF16), jax.ShapeDtypeStruct((t, D_MODEL), F32),
                   jax.ShapeDtypeStruct((1, SLAB), F32), jax.ShapeDtypeStruct((1, D_MODEL), F32),
                   jax.ShapeDtypeStruct((1, D_MODEL), F32)],
        compiler_params=_params(("arbitrary",)),
    )(x1, h2, target, w_up, w_down, g3, g4)


def _latent_bwd(dqb_t, dkb_t, dvb_t, cq, ckv, cqn, ckvn, rope_ct, rope_s1t, rope_s2t, g_q, g_kv, w_qb, w_kvb):
    t = cq.shape[0]
    tm = min(t, 2 * _wide_token_tile(t))

    def body(dqt_ref, dkt_ref, dvt_ref, cq_ref, ckv_ref, cqn_ref, ckvn_ref, ct_ref, s1t_ref, s2t_ref,
             gq_ref, gkv_ref, wqb_ref, wkvb_ref,
             dlate_ref, dgq_ref, dgkv_ref, dwqb_ref, dwkvb_ref, dqbrt_ref, dkvbt_ref):
        @pl.when(pl.program_id(0) == 0)
        def _():
            dwqb_ref[...] = jnp.zeros_like(dwqb_ref)
            dwkvb_ref[...] = jnp.zeros_like(dwkvb_ref)

        ct, s1t, s2t = ct_ref[...], s1t_ref[...], s2t_ref[...]
        dk_sum_t = jnp.zeros((SLAB, tm), F32)
        for hd in range(N_HEADS):
            sl = slice(hd * SLAB, (hd + 1) * SLAB)
            dqbrt_ref[sl, :] = _rope_t_bwd(dqt_ref[sl, :] * SCALE_B, ct, s1t, s2t).astype(BF16)
            dk_sum_t += dkt_ref[sl, :].astype(F32)
        dkvbt_ref[0:HM, :] = dkt_ref[...]
        dkvbt_ref[HM:2 * HM, :] = dvt_ref[...]
        dkr = _rope_t_bwd(dk_sum_t, ct, s1t, s2t).T
        dwqb_ref[...] += _dot(dqbrt_ref[...], cqn_ref[...])
        dwkvb_ref[...] += _dot(dkvbt_ref[...], ckvn_ref[...])
        dcqn = _dot(wqb_ref[...], dqbrt_ref[...]).T
        cq = cq_ref[...]
        rq = _rms_r(cq)
        _acc_rows(dgq_ref, dcqn * cq * rq)
        dcq = _rms_bwd(cq, rq, gq_ref[...], dcqn)
        dckvn = _dot(wkvb_ref[...], dkvbt_ref[...]).T
        ckv = ckv_ref[...]
        rkv = _rms_r(ckv)
        _acc_rows(dgkv_ref, dckvn * ckv * rkv)
        dckv = _rms_bwd(ckv, rkv, gkv_ref[...], dckvn)
        dlate_ref[:, 0:C_CKV - C_CQ] = dcq.astype(BF16)
        dlate_ref[:, C_CKV - C_CQ:C_KR - C_CQ] = dckv.astype(BF16)
        dlate_ref[:, C_KR - C_CQ:D_IN_PAD - C_CQ] = dkr.astype(BF16)

    hmt = _col_spec(HM, tm)
    tab = _col_spec(SLAB, tm)
    return pl.pallas_call(
        body, name="latent_bwd", grid=(t // tm,),
        in_specs=[hmt, hmt, hmt,
                  _row_spec(tm, Q_LORA), _row_spec(tm, KV_LORA), _row_spec(tm, Q_LORA), _row_spec(tm, KV_LORA),
                  tab, tab, tab, _full_spec((1, Q_LORA)), _full_spec((1, KV_LORA)),
                  _full_spec((Q_LORA, HM)), _full_spec((KV_LORA, 2 * HM))],
        out_specs=[_row_spec(tm, D_IN_PAD - C_CQ), _full_spec((1, Q_LORA)), _full_spec((1, KV_LORA)),
                   _full_spec((HM, Q_LORA)), _full_spec((2 * HM, KV_LORA))],
        out_shape=[jax.ShapeDtypeStruct((t, D_IN_PAD - C_CQ), BF16),
                   jax.ShapeDtypeStruct((1, Q_LORA), F32), jax.ShapeDtypeStruct((1, KV_LORA), F32),
                   jax.ShapeDtypeStruct((HM, Q_LORA), F32), jax.ShapeDtypeStruct((2 * HM, KV_LORA), F32)],
        scratch_shapes=[pltpu.VMEM((HM, tm), BF16), pltpu.VMEM((2 * HM, tm), BF16)],
        compiler_params=_params(("arbitrary",)),
    )(dqb_t, dkb_t, dvb_t, cq, ckv, cqn, ckvn, rope_ct, rope_s1t, rope_s2t, g_q, g_kv, w_qb, w_kvb)


def _inproj_bwd(dgates, dqkv, dlate, x, dx1, g1, w_in, grad_slices, only, small):
    t = x.shape[0]
    tm = _wide_token_tile(t)
    n_ex = len(grad_slices)
    zeroed = [a for a in range(n_ex) if only[a] is not None]

    def body(dgates_ref, dqkv_ref, dlate_ref, x_ref, dx1_ref, g1_ref, win_ref, *rest):
        slice_refs, small_ref = rest[:n_ex], rest[n_ex]
        dx_ref = rest[n_ex + 1]
        part_refs = rest[n_ex + 2:2 * n_ex + 2]
        small_dst, dg1_dst = rest[2 * n_ex + 2:2 * n_ex + 4]
        dproj_ref, dg1_ref, send_sems, recv_sems, local_sems = rest[2 * n_ex + 4:2 * n_ex + 9]
        zero_refs, zero_sem = rest[2 * n_ex + 9:-1], rest[-1]
        sems = (send_sems, recv_sems, local_sems)

        def slice_copies():
            return _direct_copies(slice_refs, part_refs, *sems, False, only=only)

        def small_copies():
            return _direct_copies([small_ref], [small_dst], *sems, True, sem_base=n_ex)

        @pl.when(pl.program_id(0) == 0)
        def _():
            _start_copies(*slice_copies())
            _start_copies(*small_copies())
            x_, y_, c_ = _mesh_pos()
            me = 4 * x_ + 2 * y_ + c_
            for a, z_ref in zip(zeroed, zero_refs):
                outside = me != only[a][0]
                for d in only[a][1:]:
                    outside = jnp.logical_and(outside, me != d)

                @pl.when(outside)
                def _():
                    z_ref[...] = jnp.zeros_like(z_ref)
                    fills = [pltpu.make_async_copy(z_ref, part_refs[a].at[k], zero_sem.at[k])
                             for k in range(N_DEV)]
                    for cp in fills:
                        cp.start()
                    for cp in fills:
                        cp.wait()

        dproj_ref[:, C_GATES:C_QA] = dgates_ref[...]
        dproj_ref[:, C_QA:C_CQ] = dqkv_ref[...]
        dproj_ref[:, C_CQ:D_IN_PAD] = dlate_ref[...]
        dh = _dot_nt(dproj_ref[...], win_ref[...])
        xv = x_ref[...]
        r1 = _rms_r(xv)
        _acc_rows(dg1_ref, dh * xv * r1)
        dx_ref[...] = dx1_ref[...] + _rms_bwd(xv, r1, g1_ref[...], dh)

        @pl.when(pl.program_id(0) == t // tm - 1)
        def _():
            gain_copies = _direct_copies([dg1_ref], [dg1_dst], *sems, True, sem_base=n_ex + 1)
            _start_copies(*gain_copies)
            _wait_copies(*slice_copies())
            _wait_copies(*small_copies())
            _wait_copies(*gain_copies)

    kvw = N_KV_A * SLAB
    row = _row_spec(tm, D_MODEL)
    outs = pl.pallas_call(
        body, name="inproj_bwd", grid=(t // tm,),
        in_specs=[_row_spec(tm, 2 * D_MODEL), _row_spec(tm, HM + 2 * kvw), _row_spec(tm, D_IN_PAD - C_CQ),
                  row, row, _full_spec((1, D_MODEL)), _full_spec((D_MODEL, D_IN_PAD))]
        + [ANY_SPEC] * (n_ex + 1),
        out_specs=[row] + [ANY_SPEC] * (n_ex + 2),
        out_shape=[jax.ShapeDtypeStruct((t, D_MODEL), F32)]
        + [jax.ShapeDtypeStruct(a.shape, a.dtype) for a in grad_slices]
        + [jax.ShapeDtypeStruct((N_DEV,) + small.shape, F32), jax.ShapeDtypeStruct((N_DEV, 1, D_MODEL), F32)],
        scratch_shapes=[pltpu.VMEM((tm, D_IN_PAD), BF16), pltpu.VMEM((1, D_MODEL), F32)]
        + _exchange_scratch(n_ex + 2)
        + [pltpu.VMEM(grad_slices[a].shape[1:], grad_slices[a].dtype) for a in zeroed]
        + [pltpu.SemaphoreType.DMA((N_DEV,))],
        compiler_params=_params(("arbitrary",)),
    )(dgates, dqkv, dlate, x, dx1, g1, w_in, *grad_slices, small)
    return outs[0], list(outs[1:n_ex + 1]), outs[n_ex + 1], outs[n_ex + 2]


def _matmul_tn(a, b, name, out_dtype=F32, n_shards=1):
    t, k = a.shape
    n = b.shape[1]
    bn = min(n, 2048)
    bt = min(t, 2048)
    bk = min(k, 2048 * 1024 // bn)
    ns = n // n_shards
    per_block = bn // ns
    steps = t // bt

    def body(a_ref, b_ref, o_ref, acc):
        s = pl.program_id(2)

        @pl.when(s == 0)
        def _():
            acc[...] = jnp.zeros_like(acc)

        acc[...] += _dot_tn(a_ref[...], b_ref[...])

        @pl.when(s == steps - 1)
        def _():
            if n_shards > 1:
                for p in range(per_block):
                    o_ref[p] = acc[:, p * ns:(p + 1) * ns].astype(out_dtype)
            else:
                o_ref[...] = acc[...].astype(out_dtype)

    if n_shards > 1:
        out_spec = pl.BlockSpec((per_block, bk, ns), lambda i, j, s: (j, i, 0))
        out_shape = jax.ShapeDtypeStruct((n_shards, k, ns), out_dtype)
    else:
        out_spec = pl.BlockSpec((bk, bn), lambda i, j, s: (i, j))
        out_shape = jax.ShapeDtypeStruct((k, n), out_dtype)
    return pl.pallas_call(
        body, name=name, grid=(k // bk, n // bn, steps),
        in_specs=[pl.BlockSpec((bt, bk), lambda i, j, s: (s, i)), pl.BlockSpec((bt, bn), lambda i, j, s: (s, j))],
        out_specs=out_spec, out_shape=out_shape, scratch_shapes=[pltpu.VMEM((bk, bn), F32)],
        compiler_params=_params(("parallel", "parallel", "arbitrary")),
    )(a, b)


def _two_level_gather(srcs, dsts, send_sems, recv_sems, local_sems):
    n = len(srcs)
    x, y, c = _mesh_pos()
    me, sibling = (x, y, c), (x, y, 1 - c)
    chips = [(1 - x, y), (x, 1 - y), (1 - x, 1 - y)]

    def slot(a, px, py, pc):
        return dsts[a].at[4 * px + 2 * py + pc]

    def copy(a, k, block, to, src=None):
        return pltpu.make_async_remote_copy(
            src_ref=slot(a, *block) if src is None else src, dst_ref=slot(a, *block),
            send_sem=send_sems.at[(N_DEV - 1) * a + k], recv_sem=recv_sems.at[(N_DEV - 1) * a + k],
            device_id=to, device_id_type=pl.DeviceIdType.MESH)

    def own_copies():
        mine = [pltpu.make_async_copy(srcs[a], slot(a, *me), local_sems.at[a]) for a in range(n)]
        first = []
        for a in range(n):
            first.append(copy(a, 0, me, sibling, src=srcs[a]))
            first += [copy(a, 1 + j, me, (*chip, c), src=srcs[a]) for j, chip in enumerate(chips)]
        return mine, first

    def start():
        mine, first = own_copies()
        for cp in mine + first:
            cp.start()

    def finish():
        mine, first = own_copies()
        passed = []
        for j, chip in enumerate(chips):
            for a in range(n):
                copy(a, 1 + j, (*chip, c), me).wait_recv()
                passed.append(copy(a, 4 + j, (*chip, c), sibling))
                passed[-1].start()
        for a in range(n):
            copy(a, 0, sibling, me).wait_recv()
        for j, chip in enumerate(chips):
            for a in range(n):
                copy(a, 4 + j, (*chip, 1 - c), me).wait_recv()
        for cp in first + passed:
            cp.wait_send()
        for cp in mine:
            cp.wait()

    return start, finish


def _adamw(parts, w, m, v, name):
    n_parts = len(parts)
    _, k, n = parts[0].shape
    bk = min(k, ADAM_ROWS)
    c1 = 1.0 - ADAM_B1 ** ADAM_STEP
    c2 = 1.0 - ADAM_B2 ** ADAM_STEP

    def body(*refs):
        p_refs, (w_ref, m_ref, v_ref, g_ref, d_ref, mo_ref, vo_ref) = refs[:n_parts], refs[n_parts:]
        g = p_refs[0][0].astype(F32)
        for p_ref in p_refs:
            for s in range(N_DEV):
                if p_ref is not p_refs[0] or s > 0:
                    g = g + p_ref[s].astype(F32)
        g_ref[0] = g
        m_new = ADAM_B1 * m_ref[0] + (1.0 - ADAM_B1) * g
        v_new = ADAM_B2 * v_ref[0] + (1.0 - ADAM_B2) * (g * g)
        mo_ref[0] = m_new
        vo_ref[0] = v_new
        m_hat = m_new / c1
        v_hat = v_new / c2
        d_ref[0] = -ADAM_LR * (m_hat / (jnp.sqrt(v_hat) + ADAM_EPS) + ADAM_WD * w_ref[0])

    blk = pl.BlockSpec((1, bk, n), lambda i: (0, i, 0))
    out = jax.ShapeDtypeStruct((1, k, n), F32)
    return pl.pallas_call(
        body, name=name, grid=(k // bk,),
        in_specs=[pl.BlockSpec((N_DEV, bk, n), lambda i: (0, i, 0))] * n_parts + [blk, blk, blk],
        out_specs=[blk] * 4, out_shape=[out] * 4,
        compiler_params=_params(("parallel",)),
    )(*parts, w, m, v)


def _adamw_small(parts, w, m, v):
    k = len(SMALL_LAYOUT)
    c1 = 1.0 - ADAM_B1 ** ADAM_STEP
    c2 = 1.0 - ADAM_B2 ** ADAM_STEP

    def body(p_ref, *refs):
        w_refs, m_refs, v_refs, outs = refs[:k], refs[k:2 * k], refs[2 * k:3 * k], refs[3 * k:]
        total = p_ref[0]
        for s in range(1, N_DEV):
            total = total + p_ref[s]
        for i, (_, row, off, width) in enumerate(SMALL_LAYOUT):
            g = total[row:row + 1, off:off + width]
            m_new = ADAM_B1 * m_refs[i][...] + (1.0 - ADAM_B1) * g
            v_new = ADAM_B2 * v_refs[i][...] + (1.0 - ADAM_B2) * (g * g)
            outs[4 * i][...] = g
            outs[4 * i + 1][...] = -ADAM_LR * ((m_new / c1) / (jnp.sqrt(v_new / c2) + ADAM_EPS)
                                               + ADAM_WD * w_refs[i][...])
            outs[4 * i + 2][...] = m_new
            outs[4 * i + 3][...] = v_new
        outs[4 * k][...] = total[SMALL_LOSS_ROW:SMALL_LOSS_ROW + 1, SMALL_LOSS_OFF:SMALL_LOSS_OFF + 1]

    names = [name for name, *_ in SMALL_LAYOUT]
    out_shape = [jax.ShapeDtypeStruct(w[name].shape, F32) for name in names for _ in range(4)]
    outs = pl.pallas_call(
        body, name="adamw_small", out_shape=out_shape + [jax.ShapeDtypeStruct((1, 1), F32)],
    )(parts, *[w[n] for n in names], *[m[n] for n in names], *[v[n] for n in names])
    return {name: tuple(outs[4 * i:4 * i + 4]) for i, name in enumerate(names)}, outs[4 * k]


def _pad_heads_cols(w, heads, width):
    k = w.shape[0]
    w = w.reshape(k, heads, width)
    return jnp.pad(w, ((0, 0), (0, 0), (0, SLAB - width))).reshape(k, heads * SLAB)


def _unpad_heads_cols(w, heads, width):
    k = w.shape[0]
    return w.reshape(k, heads, SLAB)[:, :, :width].reshape(k, heads * width)


def _pad_heads_rows(w, heads, width):
    n = w.shape[1]
    w = w.reshape(heads, width, n)
    return jnp.pad(w, ((0, 0), (0, SLAB - width), (0, 0))).reshape(heads * SLAB, n)


def _unpad_heads_rows(w, heads, width):
    n = w.shape[1]
    return w.reshape(heads, SLAB, n)[:, :width, :].reshape(heads * width, n)


def _pad_w_in(w_in):
    o = 2 * D_MODEL
    qa = _pad_heads_cols(w_in[:, o:o + 512], N_HEADS, HEAD_A)
    ka = _pad_heads_cols(w_in[:, o + 512:o + 640], N_KV_A, HEAD_A)
    va = _pad_heads_cols(w_in[:, o + 640:o + 768], N_KV_A, HEAD_A)
    kr = jnp.pad(w_in[:, o + 1152:o + 1184], ((0, 0), (QK_NOPE, SLAB - QK_NOPE - QK_ROPE)))
    return jnp.concatenate([w_in[:, :o], qa, ka, va, w_in[:, o + 768:o + 1152], kr], axis=1)


def _unpad_w_in(w):
    qa = _unpad_heads_cols(w[:, C_QA:C_KA], N_HEADS, HEAD_A)
    ka = _unpad_heads_cols(w[:, C_KA:C_VA], N_KV_A, HEAD_A)
    va = _unpad_heads_cols(w[:, C_VA:C_CQ], N_KV_A, HEAD_A)
    kr = w[:, C_KR + QK_NOPE:C_KR + QK_NOPE + QK_ROPE]
    return jnp.concatenate([w[:, :C_QA], qa, ka, va, w[:, C_CQ:C_KR], kr], axis=1)


def _pad_w_kvb(w_kvb):
    w = w_kvb.reshape(KV_LORA, N_HEADS, QK_NOPE + V_DIM_B)
    k = jnp.pad(w[:, :, :QK_NOPE], ((0, 0), (0, 0), (0, SLAB - QK_NOPE))).reshape(KV_LORA, HM)
    v = jnp.pad(w[:, :, QK_NOPE:], ((0, 0), (0, 0), (0, SLAB - V_DIM_B))).reshape(KV_LORA, HM)
    return jnp.concatenate([k, v], axis=1)


def _unpad_w_kvb(w):
    k = w[:, :HM].reshape(KV_LORA, N_HEADS, SLAB)[:, :, :QK_NOPE]
    v = w[:, HM:].reshape(KV_LORA, N_HEADS, SLAB)[:, :, :V_DIM_B]
    return jnp.concatenate([k, v], axis=2).reshape(KV_LORA, N_HEADS * (QK_NOPE + V_DIM_B))


def _col_shards(w):
    k, n = w.shape
    ns = n // N_DEV
    if ns % SLAB:
        return jnp.stack([w[:, d * ns:(d + 1) * ns] for d in range(N_DEV)])
    return w.reshape(k, N_DEV, ns).transpose(1, 0, 2)


def _from_col_shards(s):
    _, k, ns = s.shape
    if ns % SLAB:
        return jnp.concatenate([s[d] for d in range(N_DEV)], axis=1)
    return s.transpose(1, 0, 2).reshape(k, N_DEV * ns)


def _freq_row():
    freqs = ROPE_THETA ** (-jnp.arange(0, QK_ROPE, 2, dtype=F32) / QK_ROPE)
    return jnp.concatenate([jnp.zeros((QK_NOPE,), F32), freqs, freqs,
                            jnp.zeros((SLAB - QK_NOPE - QK_ROPE,), F32)]).reshape(1, SLAB)


SMALL_D_ROWS = ("pre_norm_mix", "post_norm_mix", "pre_norm_mlp", "post_norm_mlp")
SMALL_LAYOUT = tuple((name, i, 0, D_MODEL) for i, name in enumerate(SMALL_D_ROWS)) + (
    ("q_a_norm", 4, 0, Q_LORA), ("kv_a_norm", 4, 256, KV_LORA), ("sinks", 4, 384, N_HEADS))
SMALL_LOSS_ROW, SMALL_LOSS_OFF = 4, 512


def _pack_small(vals):
    row4 = jnp.concatenate([vals["q_a_norm"].reshape(-1), vals["kv_a_norm"].reshape(-1), vals["sinks"].reshape(-1),
                            jnp.zeros((SMALL_LOSS_OFF - 392,), F32), vals["loss"].reshape(-1),
                            jnp.zeros((1024 - SMALL_LOSS_OFF - 1,), F32)])
    rows = [vals[n].reshape(1024) for n in SMALL_D_ROWS] + [row4]
    return jnp.concatenate([jnp.stack(rows), jnp.zeros((SMALL_ROWS - 5, 1024), F32)], axis=0)


WEIGHT_ORDER = ("pre_norm_mix", "w_in", "q_a_norm", "w_q_b", "kv_a_norm", "w_kv_b", "sinks", "w_o_a", "w_o_b",
                "w_out", "post_norm_mix", "pre_norm_mlp", "w_up", "w_down", "post_norm_mlp")


def kernel(x, positions, pre_norm_mix, w_in, q_a_norm, w_q_b, kv_a_norm, w_kv_b, sinks, w_o_a, w_o_b, w_out, post_norm_mix, pre_norm_mlp, w_up, w_down, post_norm_mlp, loss_target, m_pre_norm_mix, m_w_in, m_q_a_norm, m_w_q_b, m_kv_a_norm, m_w_kv_b, m_sinks, m_w_o_a, m_w_o_b, m_w_out, m_post_norm_mix, m_pre_norm_mlp, m_w_up, m_w_down, m_post_norm_mlp, v_pre_norm_mix, v_w_in, v_q_a_norm, v_w_q_b, v_kv_a_norm, v_w_kv_b, v_sinks, v_w_o_a, v_w_o_b, v_w_out, v_post_norm_mix, v_pre_norm_mlp, v_w_up, v_w_down, v_post_norm_mlp):
    weights = dict(pre_norm_mix=pre_norm_mix, w_in=w_in, q_a_norm=q_a_norm, w_q_b=w_q_b, kv_a_norm=kv_a_norm,
                   w_kv_b=w_kv_b, sinks=sinks, w_o_a=w_o_a, w_o_b=w_o_b, w_out=w_out, post_norm_mix=post_norm_mix,
                   pre_norm_mlp=pre_norm_mlp, w_up=w_up, w_down=w_down, post_norm_mlp=post_norm_mlp)
    m_in = dict(pre_norm_mix=m_pre_norm_mix, w_in=m_w_in, q_a_norm=m_q_a_norm, w_q_b=m_w_q_b, kv_a_norm=m_kv_a_norm,
                w_kv_b=m_w_kv_b, sinks=m_sinks, w_o_a=m_w_o_a, w_o_b=m_w_o_b, w_out=m_w_out,
                post_norm_mix=m_post_norm_mix, pre_norm_mlp=m_pre_norm_mlp, w_up=m_w_up, w_down=m_w_down,
                post_norm_mlp=m_post_norm_mlp)
    v_in = dict(pre_norm_mix=v_pre_norm_mix, w_in=v_w_in, q_a_norm=v_q_a_norm, w_q_b=v_w_q_b, kv_a_norm=v_kv_a_norm,
                w_kv_b=v_w_kv_b, sinks=v_sinks, w_o_a=v_w_o_a, w_o_b=v_w_o_b, w_out=v_w_out,
                post_norm_mix=v_post_norm_mix, pre_norm_mlp=v_pre_norm_mlp, w_up=v_w_up, w_down=v_w_down,
                post_norm_mlp=v_post_norm_mlp)

    xs, pos, target = x[0], positions[0], loss_target[0]
    t = xs.shape[0]
    pos_col = pos.reshape(t, 1)
    pos_row = pos.reshape(1, t)
    g1, g2, g3, g4 = (weights[n] for n in SMALL_D_ROWS)
    g_q, g_kv = q_a_norm, kv_a_norm
    sink_vec = sinks.reshape(N_HEADS)
    shard = {n: weights[n][0].astype(BF16) for n in EARLY + LATE}

    tables, (e_in, e_qb, e_kvb) = _rope_tables(pos_col, _freq_row(), [shard[n] for n in EARLY])
    w_in_p = _pad_w_in(_from_col_shards(e_in))
    w_qb = _pad_heads_cols(_from_col_shards(e_qb), N_HEADS, QK_NOPE + QK_ROPE)
    w_kvb = _pad_w_kvb(_from_col_shards(e_kvb))

    (h, gates, qa, ka, va, cq, ckv, cqn, ckvn, kb, vb, qt, kt, vt) = _inproj_fwd(
        xs, g1, w_in_p, g_q, g_kv, w_kvb, w_qb.T, w_kvb[:, :HM].T, w_kvb[:, HM:].T, w_in_p[:, C_KR:].T, tables)
    out_a, lse_a = _swa_fwd(qa, ka, va, pos_col, pos_row, sink_vec)
    out_b, out_b_t, qt_lse, (l_oa, l_ob, l_out, w_up_s, l_down) = _mla_fwd(qt, kb, vt, [shard[n] for n in LATE])
    w_oa = _pad_heads_rows(_from_col_shards(l_oa), N_HEADS, HEAD_A)
    w_ob = _pad_heads_rows(_from_col_shards(l_ob), N_HEADS, V_DIM_B)
    w_out_f = l_out.reshape(D_MODEL, D_MODEL)
    w_down_f = l_down.reshape(D_FF, D_MODEL)

    oa_p, ob_p, merged, y, x1, h2 = _merge_fwd(out_a, out_b, gates, xs, w_oa, w_ob, w_out_f, g2, g3)
    a, du, dy2, dx1, loss, dg3, dg4 = _mlp_fwd_bwd(x1, h2, target, w_up_s, w_down_f, g3, g4)
    (dgates, d_oa, d_ob_t, dg2, dw_oa, dw_ob, dw_out) = _merge_bwd(
        dx1, y, gates, oa_p, ob_p, out_a, out_b, out_b_t, merged, w_oa, w_ob, w_out_f, g2)
    late_slices = [
        _col_shards(_unpad_heads_rows(dw_oa, N_HEADS, HEAD_A)).astype(BF16),
        _col_shards(_unpad_heads_rows(dw_ob, N_HEADS, V_DIM_B)).astype(BF16),
        dw_out.astype(BF16).reshape(N_DEV, D_MODEL // N_DEV, D_MODEL),
        _matmul_tn(h2, du, "dw_up", BF16, N_DEV),
        _matmul_tn(a, dy2, "dw_down", BF16).reshape(N_DEV, D_FF // N_DEV, D_MODEL),
    ]
    dqkv_a, dsink = _swa_bwd(qa, ka, va, out_a, d_oa, lse_a, pos_col, pos_row, sink_vec)
    dw_in_early = jnp.concatenate([_matmul_tn(h, dgates, "dw_in_gates"), _matmul_tn(h, dqkv_a, "dw_in_mixer_a"),
                                   jnp.zeros((D_MODEL, D_IN_PAD - C_CQ), F32)], axis=1)
    late_slices.append(_col_shards(_unpad_w_in(dw_in_early)).astype(BF16))
    dqb_t, dkb_t, dvb_t, late_parts = _mla_bwd(qt_lse, kb, kt, vb, d_ob_t, late_slices)
    w_in_early_parts = late_parts.pop()
    dproj_late, dgq, dgkv, dw_qb_t, dw_kvb_t = _latent_bwd(
        dqb_t, dkb_t, dvb_t, cq, ckv, cqn, ckvn, *tables[3:], g_q, g_kv, w_qb, w_kvb)
    dw_l = _matmul_tn(h, dproj_late, "dw_in_latents")
    late_cols = jnp.concatenate([dw_l[:, :Q_LORA + KV_LORA], dw_l[:, C_KR - C_CQ + QK_NOPE:C_KR - C_CQ + Q_HEAD_B]],
                                axis=1)
    shard_cols = w_in.shape[2]
    head = late_cols.shape[1] - shard_cols
    w_in_late = jnp.concatenate([
        jnp.zeros((N_DEV - 2, D_MODEL, shard_cols), F32),
        jnp.pad(late_cols[:, :head], ((0, 0), (shard_cols - head, 0)))[None], late_cols[:, head:][None]])
    early_slices = [
        w_in_late.astype(BF16),
        _col_shards(_unpad_heads_cols(dw_qb_t.T, N_HEADS, QK_NOPE + QK_ROPE)).astype(BF16),
        _col_shards(_unpad_w_kvb(dw_kvb_t.T)).astype(BF16),
    ]
    small_grads = {"pre_norm_mix": jnp.zeros((1, D_MODEL), F32), "post_norm_mix": dg2, "pre_norm_mlp": dg3,
                   "post_norm_mlp": dg4, "q_a_norm": dgq, "kv_a_norm": dgkv,
                   "sinks": dsink.reshape(N_HEADS, BLOCK).sum(axis=1), "loss": loss[0, 0:1]}
    dx, early_parts, s_parts, dg1_parts = _inproj_bwd(
        dgates, dqkv_a, dproj_late, xs, dx1, g1, w_in_p, early_slices,
        only=[(N_DEV - 2, N_DEV - 1), None, None], small=_pack_small(small_grads))
    s_parts = s_parts.at[:, SMALL_D_ROWS.index("pre_norm_mix"), :].set(dg1_parts[:, 0, :])

    updates = {}
    all_parts = [[w_in_early_parts, early_parts[0]]] + [[p] for p in early_parts[1:] + late_parts]
    for name, parts in zip(EARLY + LATE, all_parts):
        outs = _adamw(parts, weights[name], m_in[name], v_in[name], "adamw_" + name)
        for kind, arr in zip(("g", "d", "m", "v"), outs):
            updates[kind, name] = arr
    small_out, loss_sum = _adamw_small(s_parts, weights, m_in, v_in)
    for name, outs in small_out.items():
        for kind, arr in zip(("g", "d", "m", "v"), outs):
            updates[kind, name] = arr
    results = [updates[kind, name] for kind in ("g", "d", "m", "v") for name in WEIGHT_ORDER]
    return (loss_sum.reshape(()), dx[None], *results)
```

```python
import functools

import numpy as np
import jax
import jax.numpy as jnp
from jax import lax
from jax.experimental import pallas as pl
from jax.experimental.pallas import tpu as pltpu

F32 = jnp.float32
BF16 = jnp.bfloat16

D_MODEL = 1024
D_FF = 4096
N_HEADS = 8
N_KV_A = 2
GROUP_A = N_HEADS // N_KV_A
HEAD_A = 64
QK_NOPE = 64
QK_ROPE = 32
V_DIM_B = 64
Q_LORA = 256
KV_LORA = 128
BLOCK = 128
SLAB = 128
ROPE_THETA = 10000.0
EPS = 1e-6
N_DEV = 8
NEG = -1e30

SCALE_A = HEAD_A ** -0.5
SCALE_B = (QK_NOPE + QK_ROPE) ** -0.5
LOG2E = 1.4426950408889634
SCORE_B = SCALE_B * LOG2E
MLA_HEADS_PER_STEP = 4
MLA_FWD_HEADS_PER_STEP = 8
Q_HEAD_B = QK_NOPE + QK_ROPE
ONES_ROWS = 16
SLOPES_A = tuple(2.0 ** (-8.0 * (h + 1) / N_HEADS) for h in range(N_HEADS))

ADAM_LR = 0.001
ADAM_B1 = 0.9
ADAM_B2 = 0.999
ADAM_EPS = 1e-08
ADAM_WD = 0.01
ADAM_STEP = 10

HM = N_HEADS * SLAB
C_GATES = 0
C_QA = 2 * D_MODEL
C_KA = C_QA + HM
C_VA = C_KA + N_KV_A * SLAB
C_CQ = C_VA + N_KV_A * SLAB
C_CKV = C_CQ + Q_LORA
C_KR = C_CKV + KV_LORA
D_IN_PAD = C_KR + SLAB

VMEM_LIMIT = 56 * 1024 * 1024
VMEM_LIMIT_MERGE_BWD = 60 * 1024 * 1024

EARLY = ("w_in", "w_q_b", "w_kv_b")
LATE = ("w_o_a", "w_o_b", "w_out", "w_up", "w_down")
ADAM_ROWS = 256
SMALL_ROWS = 8


def _token_tile(t):
    return min(256, t)


def _wide_token_tile(t):
    return min(512, t)


def _attn_tile(t):
    return 512 if t >= 2048 else 128


def _params(sem, vmem=VMEM_LIMIT):
    return pltpu.CompilerParams(dimension_semantics=sem, vmem_limit_bytes=vmem)


def _dot(a, b):
    return jnp.dot(a, b, preferred_element_type=F32)


def _dot_nt(a, b):
    return lax.dot_general(a, b, (((1,), (1,)), ((), ())), preferred_element_type=F32)


def _dot_tn(a, b):
    return lax.dot_general(a, b, (((0,), (0,)), ((), ())), preferred_element_type=F32)


def _rms_r(x):
    return lax.rsqrt(jnp.mean(x * x, axis=-1, keepdims=True) + EPS)


def _rms_bwd(x, r, g, dy):
    t = dy * g
    return r * t - x * (r * r * r) * jnp.mean(x * t, axis=-1, keepdims=True)


def _sigmoid(x):
    return 1.0 / (1.0 + jnp.exp(-x))


def _rope(x, c, s1, s2):
    return x * c + pltpu.roll(x, SLAB - 16, 1) * s1 + pltpu.roll(x, 16, 1) * s2


def _rope_bwd(d, c, s1, s2):
    return d * c + pltpu.roll(d * s1, 16, 1) + pltpu.roll(d * s2, SLAB - 16, 1)


def _roll_rows(x, shift):
    return jnp.concatenate([x[-shift:], x[:-shift]], axis=0)


def _rope_t(x, c, s1, s2):
    return x * c + _roll_rows(x, SLAB - 16) * s1 + _roll_rows(x, 16) * s2


def _rope_t_bwd(d, c, s1, s2):
    return d * c + _roll_rows(d * s1, 16) + _roll_rows(d * s2, SLAB - 16)


def _plant_rows(slab, row, vals):
    hi = vals.astype(BF16).astype(F32)
    lo = (vals - hi).astype(BF16).astype(F32)
    idx = lax.broadcasted_iota(jnp.int32, slab.shape, 0)
    return jnp.where(idx == row, -hi, jnp.where(idx == row + 1, -lo, slab))


def _row_spec(tm, n):
    return pl.BlockSpec((tm, n), lambda i: (i, 0))


def _col_spec(n, tm):
    return pl.BlockSpec((n, tm), lambda i: (0, i))


def _full_spec(shape):
    nd = len(shape)
    return pl.BlockSpec(shape, lambda i: (0,) * nd, pipeline_mode=pl.Buffered(1))


def _acc_rows(ref, val):
    @pl.when(pl.program_id(0) == 0)
    def _():
        ref[...] = jnp.zeros_like(ref)
    ref[...] += jnp.sum(val, axis=0, keepdims=True)


def _rope_tables(pos_col, freq_row, early):
    t = pos_col.shape[0]
    tm = _token_tile(t)
    n = len(early)

    def body(pos_ref, f_ref, *rest):
        shard_refs, (c_ref, s1_ref, s2_ref, ct_ref, s1t_ref, s2t_ref) = rest[:n], rest[n:n + 6]
        start, finish = _two_level_gather(shard_refs, rest[n + 6:2 * n + 6], *rest[2 * n + 6:])
        pl.when(pl.program_id(0) == 0)(start)
        ang = pos_ref[...].astype(F32) * f_ref[...]
        lane = lax.broadcasted_iota(jnp.int32, ang.shape, 1)
        s = jnp.sin(ang)
        c = jnp.cos(ang)
        s1 = jnp.where((lane >= 64) & (lane < 80), -s, 0.0)
        s2 = jnp.where((lane >= 80) & (lane < 96), s, 0.0)
        c_ref[...], s1_ref[...], s2_ref[...] = c, s1, s2
        ct_ref[...], s1t_ref[...], s2t_ref[...] = c.T, s1.T, s2.T
        pl.when(pl.program_id(0) == t // tm - 1)(finish)

    tab = jax.ShapeDtypeStruct((t, SLAB), F32)
    tabt = jax.ShapeDtypeStruct((SLAB, t), F32)
    outs = pl.pallas_call(
        body, name="rope_tables", grid=(t // tm,),
        in_specs=[_row_spec(tm, 1), _full_spec((1, SLAB))] + [ANY_SPEC] * n,
        out_specs=[_row_spec(tm, SLAB)] * 3 + [_col_spec(SLAB, tm)] * 3 + [ANY_SPEC] * n,
        out_shape=[tab] * 3 + [tabt] * 3 + [jax.ShapeDtypeStruct((N_DEV,) + a.shape, a.dtype) for a in early],
        scratch_shapes=_exchange_scratch(n),
        compiler_params=_params(("arbitrary",)),
    )(pos_col, freq_row, *early)
    return outs[:6], outs[6:]


def _inproj_fwd(x, g1, w_in, g_q, g_kv, w_kvb, w_qb_t, w_kb_t, w_vb_t, w_kr_t, tables):
    t = x.shape[0]
    tm = _wide_token_tile(t)

    def body(x_ref, g1_ref, win_ref, gq_ref, gkv_ref, wkvb_ref, wqbt_ref, wkbt_ref, wvbt_ref, wkrt_ref,
             c_ref, s1_ref, s2_ref, ct_ref, s1t_ref, s2t_ref,
             h_ref, gates_ref, qa_ref, ka_ref, va_ref, cq_ref, ckv_ref, cqn_ref, ckvn_ref,
             kb_ref, vb_ref, qt_ref, kt_ref, vt_ref):
        xv = x_ref[...]
        h = (xv * _rms_r(xv) * g1_ref[...]).astype(BF16)
        h_ref[...] = h
        proj = _dot(h, win_ref[...])
        gates_ref[...] = proj[:, C_GATES:C_QA].astype(BF16)
        qa_ref[...] = proj[:, C_QA:C_KA].astype(BF16)
        ka_ref[...] = proj[:, C_KA:C_VA].astype(BF16)
        va_ref[...] = proj[:, C_VA:C_CQ].astype(BF16)
        cq = proj[:, C_CQ:C_CKV]
        ckv = proj[:, C_CKV:C_KR]
        kr = proj[:, C_KR:D_IN_PAD]
        cq_ref[...] = cq
        ckv_ref[...] = ckv
        cqn = (cq * _rms_r(cq) * gq_ref[...]).astype(BF16)
        ckvn = (ckv * _rms_r(ckv) * gkv_ref[...]).astype(BF16)
        cqn_ref[...] = cqn
        ckvn_ref[...] = ckvn
        c, s1, s2 = c_ref[...], s1_ref[...], s2_ref[...]
        kvb = _dot(ckvn, wkvb_ref[...])
        kr_rot = _rope(kr, c, s1, s2)
        ct, s1t, s2t = ct_ref[...], s1t_ref[...], s2t_ref[...]
        q_t = _dot_nt(wqbt_ref[...], cqn)
        k_t = _dot_nt(wkbt_ref[...], ckvn)
        kr_t = _rope_t(_dot_nt(wkrt_ref[...], h), ct, s1t, s2t)
        k_lane = lax.broadcasted_iota(jnp.int32, (1, SLAB), 1)
        k_ones = jnp.where((k_lane == Q_HEAD_B) | (k_lane == Q_HEAD_B + 1), 1.0, 0.0)
        for hd in range(N_HEADS):
            sl = slice(hd * SLAB, (hd + 1) * SLAB)
            kb_ref[:, sl] = (kvb[:, sl] + kr_rot + k_ones).astype(BF16)
            qt_ref[sl, :] = (_rope_t(q_t[sl, :], ct, s1t, s2t) * SCORE_B).astype(BF16)
            kt_ref[sl, :] = (k_t[sl, :] + kr_t).astype(BF16)
        v_lane = lax.broadcasted_iota(jnp.int32, (1, HM), 1) & (SLAB - 1)
        v_ones = jnp.where((v_lane == V_DIM_B) | (v_lane == V_DIM_B + 1), 1.0, 0.0)
        vb_ref[...] = (kvb[:, HM:2 * HM] + v_ones).astype(BF16)
        pad_row = lax.broadcasted_iota(jnp.int32, (HM, 1), 0) & (SLAB - 1)
        ones_rows = jnp.where((pad_row >= V_DIM_B) & (pad_row < V_DIM_B + ONES_ROWS), 1.0, 0.0)
        vt_ref[...] = (_dot_nt(wvbt_ref[...], ckvn) + ones_rows).astype(BF16)

    def sds(n, dt):
        return jax.ShapeDtypeStruct((t, n), dt)

    outs = [(D_MODEL, BF16), (2 * D_MODEL, BF16), (HM, BF16), (N_KV_A * SLAB, BF16), (N_KV_A * SLAB, BF16),
            (Q_LORA, F32), (KV_LORA, F32), (Q_LORA, BF16), (KV_LORA, BF16), (HM, BF16), (HM, BF16)]
    tab, tabt = _row_spec(tm, SLAB), _col_spec(SLAB, tm)
    return pl.pallas_call(
        body, name="inproj_fwd", grid=(t // tm,),
        in_specs=[_row_spec(tm, D_MODEL), _full_spec((1, D_MODEL)), _full_spec((D_MODEL, D_IN_PAD)),
                  _full_spec((1, Q_LORA)), _full_spec((1, KV_LORA)), _full_spec((KV_LORA, 2 * HM)),
                  _full_spec((HM, Q_LORA)), _full_spec((HM, KV_LORA)), _full_spec((HM, KV_LORA)),
                  _full_spec((SLAB, D_MODEL)), tab, tab, tab, tabt, tabt, tabt],
        out_specs=[_row_spec(tm, n) for n, _ in outs] + [_col_spec(HM, tm)] * 3,
        out_shape=[sds(n, dt) for n, dt in outs] + [jax.ShapeDtypeStruct((HM, t), BF16)] * 3,
        compiler_params=_params(("parallel",)),
    )(x, g1, w_in, g_q, g_kv, w_kvb, w_qb_t, w_kb_t, w_vb_t, w_kr_t, *tables)


def _tile_group(a):
    return jnp.concatenate([a] * GROUP_A, axis=1)


def _swa_masks():
    row = lax.broadcasted_iota(jnp.int32, (BLOCK, GROUP_A * BLOCK), 0)
    col = lax.broadcasted_iota(jnp.int32, (BLOCK, GROUP_A * BLOCK), 1) & (BLOCK - 1)
    return row <= col, row > col


def _heads_beside(ref, g):
    return jnp.concatenate([ref[:, (g * GROUP_A + hh) * SLAB:(g * GROUP_A + hh + 1) * SLAB].T
                            for hh in range(GROUP_A)], axis=1)


def _rows_beside(ref, g):
    return jnp.concatenate([ref[g * GROUP_A + hh] for hh in range(GROUP_A)], axis=1)


def _swa_rows(sinks):
    slopes = jnp.repeat(jnp.asarray(SLOPES_A, F32).reshape(N_KV_A, GROUP_A, 1), BLOCK, axis=2)
    sink_rows = jnp.repeat(sinks.reshape(N_KV_A, GROUP_A, 1), BLOCK, axis=2)
    return slopes.reshape(N_KV_A, 1, GROUP_A * BLOCK), sink_rows.reshape(N_KV_A, 1, GROUP_A * BLOCK)


def _swa_fwd(qa, ka, va, pos_col, pos_row, sinks):
    t = qa.shape[0]
    nb = t // BLOCK
    gw = GROUP_A * BLOCK
    slope_rows, sink_rows = _swa_rows(sinks)

    def body(q_ref, kc_ref, kp_ref, vc_ref, vp_ref, pkc_ref, pkp_ref, pq_ref, slope_ref, sink_ref, o_ref, l_ref):
        i = pl.program_id(0)
        pq = pq_ref[...]
        dist_c = _tile_group(jnp.abs(pkc_ref[...] - pq).astype(F32))
        dist_p = _tile_group(jnp.abs(pkp_ref[...] - pq).astype(F32))
        mask_c, older = _swa_masks()
        mask_p = jnp.logical_and(older, i > 0)
        raw = []
        for g in range(N_KV_A):
            gs = slice(g * SLAB, (g + 1) * SLAB)
            x = _heads_beside(q_ref, g)
            raw.append((_dot(kc_ref[:, gs], x), _dot(kp_ref[:, gs], x)))
        for g in range(N_KV_A):
            gs = slice(g * SLAB, (g + 1) * SLAB)
            slope, sink = slope_ref[g], sink_ref[g]
            s_c = jnp.where(mask_c, raw[g][0] * SCALE_A - slope * dist_c, NEG)
            s_p = jnp.where(mask_p, raw[g][1] * SCALE_A - slope * dist_p, NEG)
            m = jnp.maximum(jnp.maximum(jnp.max(s_c, axis=0, keepdims=True),
                                        jnp.max(s_p, axis=0, keepdims=True)), sink)
            e_c = jnp.exp(s_c - m)
            e_p = jnp.exp(s_p - m)
            den = jnp.sum(e_c, axis=0, keepdims=True) + jnp.sum(e_p, axis=0, keepdims=True) + jnp.exp(sink - m)
            inv = 1.0 / den
            ot = (_dot_tn(vc_ref[:, gs], (e_c * inv).astype(BF16))
                  + _dot_tn(vp_ref[:, gs], (e_p * inv).astype(BF16)))
            lse = m + jnp.log(den)
            for hh in range(GROUP_A):
                hd = g * GROUP_A + hh
                seg = slice(hh * BLOCK, (hh + 1) * BLOCK)
                o_ref[:, hd * SLAB:(hd + 1) * SLAB] = ot[:, seg].T.astype(BF16)
                l_ref[hd] = lse[:, seg]

    cur = lambda i: (i, 0)
    prev = lambda i: (jnp.maximum(i - 1, 0), 0)
    kvw = N_KV_A * SLAB
    rows = pl.BlockSpec((N_KV_A, 1, gw), lambda i: (0, 0, 0))
    return pl.pallas_call(
        body, name="swa_fwd", grid=(nb,),
        in_specs=[pl.BlockSpec((BLOCK, HM), cur),
                  pl.BlockSpec((BLOCK, kvw), cur), pl.BlockSpec((BLOCK, kvw), prev),
                  pl.BlockSpec((BLOCK, kvw), cur), pl.BlockSpec((BLOCK, kvw), prev),
                  pl.BlockSpec((BLOCK, 1), cur), pl.BlockSpec((BLOCK, 1), prev),
                  pl.BlockSpec((1, BLOCK), lambda i: (0, i)), rows, rows],
        out_specs=[pl.BlockSpec((BLOCK, HM), cur), pl.BlockSpec((N_HEADS, 1, BLOCK), lambda i: (0, 0, i))],
        out_shape=[jax.ShapeDtypeStruct((t, HM), BF16), jax.ShapeDtypeStruct((N_HEADS, 1, t), F32)],
        compiler_params=_params(("parallel",)),
    )(qa, ka, ka, va, va, pos_col, pos_col, pos_row, slope_rows, sink_rows)


def _swa_bwd(qa, ka, va, out_a, d_oa, lse, pos_col, pos_row, sinks):
    t = qa.shape[0]
    nb = t // BLOCK
    gw = GROUP_A * BLOCK
    kvw = N_KV_A * SLAB
    slope_rows, sink_rows = _swa_rows(sinks)

    def body(q_ref, qn_ref, do_ref, don_ref, l_ref, ln_ref, o_ref, on_ref, kp_ref, kc_ref, vp_ref, vc_ref,
             pkp_ref, pkc_ref, pq_ref, pqn_ref, slope_ref, sink_ref, dqkv_ref, dsink_ref):
        j = pl.program_id(0)
        pkc, pkp = pkc_ref[...], pkp_ref[...]
        dist_cc = _tile_group(jnp.abs(pkc - pq_ref[...]).astype(F32))
        dist_cp = _tile_group(jnp.abs(pkp - pq_ref[...]).astype(F32))
        dist_nc = _tile_group(jnp.abs(pkc - pqn_ref[...]).astype(F32))
        mask_cc, older = _swa_masks()
        mask_cp = jnp.logical_and(older, j > 0)
        mask_nc = jnp.logical_and(older, j < nb - 1)

        @pl.when(j == 0)
        def _():
            dsink_ref[...] = jnp.zeros_like(dsink_ref)

        def tile(k, v, x, dox, lrow, drow, dist, mask, slope):
            s = jnp.where(mask, _dot(k, x) * SCALE_A - slope * dist, NEG)
            p = jnp.exp(s - lrow)
            ds = p * (_dot(v, dox) - drow)
            return p.astype(BF16), ds.astype(BF16)

        for g in range(N_KV_A):
            gs = slice(g * SLAB, (g + 1) * SLAB)
            kc, kp, vc, vp = kc_ref[:, gs], kp_ref[:, gs], vc_ref[:, gs], vp_ref[:, gs]
            slope, sink = slope_ref[g], sink_ref[g]
            x, xn = _heads_beside(q_ref, g), _heads_beside(qn_ref, g)
            dox, doxn = _heads_beside(do_ref, g), _heads_beside(don_ref, g)
            lrow, lrown = _rows_beside(l_ref, g), _rows_beside(ln_ref, g)
            drow = jnp.sum(dox.astype(F32) * _heads_beside(o_ref, g).astype(F32), axis=0, keepdims=True)
            drown = jnp.sum(doxn.astype(F32) * _heads_beside(on_ref, g).astype(F32), axis=0, keepdims=True)
            p_cc, ds_cc = tile(kc, vc, x, dox, lrow, drow, dist_cc, mask_cc, slope)
            _, ds_cp = tile(kp, vp, x, dox, lrow, drow, dist_cp, mask_cp, slope)
            p_nc, ds_nc = tile(kc, vc, xn, doxn, lrown, drown, dist_nc, mask_nc, slope)
            dqt = (_dot_tn(kc, ds_cc) + _dot_tn(kp, ds_cp)) * SCALE_A
            for hh in range(GROUP_A):
                hd = g * GROUP_A + hh
                dqkv_ref[:, hd * SLAB:(hd + 1) * SLAB] = dqt[:, hh * BLOCK:(hh + 1) * BLOCK].T.astype(BF16)
            dqkv_ref[:, HM + g * SLAB:HM + (g + 1) * SLAB] = (
                (_dot_nt(ds_cc, x) + _dot_nt(ds_nc, xn)) * SCALE_A).astype(BF16)
            dqkv_ref[:, HM + kvw + g * SLAB:HM + kvw + (g + 1) * SLAB] = (
                _dot_nt(p_cc, dox) + _dot_nt(p_nc, doxn)).astype(BF16)
            dsink_ref[g] -= jnp.exp(sink - lrow) * drow

    cur = lambda j: (j, 0)
    prev = lambda j: (jnp.maximum(j - 1, 0), 0)
    nxt = lambda j: (jnp.minimum(j + 1, nb - 1), 0)
    cur3 = lambda j: (0, 0, j)
    nxt3 = lambda j: (0, 0, jnp.minimum(j + 1, nb - 1))
    kvw = N_KV_A * SLAB
    rows = pl.BlockSpec((N_KV_A, 1, gw), lambda j: (0, 0, 0))
    stat = lambda im: pl.BlockSpec((N_HEADS, 1, BLOCK), im)
    return pl.pallas_call(
        body, name="swa_bwd", grid=(nb,),
        in_specs=[pl.BlockSpec((BLOCK, HM), cur), pl.BlockSpec((BLOCK, HM), nxt),
                  pl.BlockSpec((BLOCK, HM), cur), pl.BlockSpec((BLOCK, HM), nxt),
                  stat(cur3), stat(nxt3), pl.BlockSpec((BLOCK, HM), cur), pl.BlockSpec((BLOCK, HM), nxt),
                  pl.BlockSpec((BLOCK, kvw), prev), pl.BlockSpec((BLOCK, kvw), cur),
                  pl.BlockSpec((BLOCK, kvw), prev), pl.BlockSpec((BLOCK, kvw), cur),
                  pl.BlockSpec((BLOCK, 1), prev), pl.BlockSpec((BLOCK, 1), cur),
                  pl.BlockSpec((1, BLOCK), lambda j: (0, j)),
                  pl.BlockSpec((1, BLOCK), lambda j: (0, jnp.minimum(j + 1, nb - 1))), rows, rows],
        out_specs=[pl.BlockSpec((BLOCK, HM + 2 * kvw), cur), rows],
        out_shape=[jax.ShapeDtypeStruct((t, HM + 2 * kvw), BF16), jax.ShapeDtypeStruct((N_KV_A, 1, gw), F32)],
        compiler_params=_params(("arbitrary",)),
    )(qa, qa, d_oa, d_oa, lse, lse, out_a, out_a, ka, ka, va, va,
      pos_col, pos_col, pos_row, pos_row, slope_rows, sink_rows)


def _mesh_pos():
    return lax.axis_index("x"), lax.axis_index("y"), lax.axis_index("c")


def _flip(v, bit):
    return 1 - v if bit else v


def _direct_copies(srcs, dsts, send_sems, recv_sems, local_sems, gather, sem_base=0, only=None):
    x, y, c = _mesh_pos()
    me = 4 * x + 2 * y + c

    def among(idx, dests):
        ok = idx == dests[0]
        for d in dests[1:]:
            ok = jnp.logical_or(ok, idx == d)
        return ok

    local, remote = [], []
    for a, (src, dst) in enumerate(zip(srcs, dsts)):
        dests = None if only is None else only[a]
        recv_ok = None if dests is None else among(me, dests)
        local.append((pltpu.make_async_copy(src if gather else src.at[me], dst.at[me],
                                            local_sems.at[sem_base + a]), recv_ok))
        for r in range(1, N_DEV):
            px, py, pc = _flip(x, r & 4), _flip(y, r & 2), _flip(c, r & 1)
            peer = 4 * px + 2 * py + pc
            sem = (N_DEV - 1) * (sem_base + a) + r - 1
            copy = pltpu.make_async_remote_copy(
                src_ref=src if gather else src.at[peer], dst_ref=dst.at[me],
                send_sem=send_sems.at[sem], recv_sem=recv_sems.at[sem],
                device_id=(px, py, pc), device_id_type=pl.DeviceIdType.MESH)
            remote.append((copy, None if dests is None else among(peer, dests), recv_ok))
    return local, remote


def _when(cond, fn):
    if cond is None:
        fn()
    else:
        pl.when(cond)(fn)


def _start_copies(local, remote):
    for cp, ok in local:
        _when(ok, cp.start)
    for cp, send_ok, _ in remote:
        _when(send_ok, cp.start)


def _wait_copies(local, remote):
    for cp, _, recv_ok in remote:
        _when(recv_ok, cp.wait_recv)
    for cp, send_ok, _ in remote:
        _when(send_ok, cp.wait_send)
    for cp, ok in local:
        _when(ok, cp.wait)


def _exchange_scratch(n):
    return [pltpu.SemaphoreType.DMA((n * (N_DEV - 1),)), pltpu.SemaphoreType.DMA((n * (N_DEV - 1),)),
            pltpu.SemaphoreType.DMA((n,))]


ANY_SPEC = pl.BlockSpec(memory_space=pl.ANY)


def _mla_fwd(qt, kb, vt, late):
    t = kb.shape[0]
    tk = _attn_tile(t)
    ratio = 2 if t >= 2 * tk else 1
    tq = ratio * tk
    nq = t // tq
    hps = MLA_FWD_HEADS_PER_STEP
    w = hps * SLAB
    pairs = [(i, j) for i in range(nq) for j in range(ratio * (i + 1))]
    i_tab = jnp.asarray(np.array([p[0] for p in pairs], np.int32))
    j_tab = jnp.asarray(np.array([p[1] for p in pairs], np.int32))

    n_late = len(late)

    def body(it_ref, jt_ref, qt_ref, k_ref, vt_ref, *rest):
        late_refs, (o_ref, ot_ref, qa_ref) = rest[:n_late], rest[n_late:n_late + 3]
        gathered_refs = rest[n_late + 3:2 * n_late + 3]
        m_s, acc_s, send_sems, recv_sems, local_sems = rest[2 * n_late + 3:]
        n = pl.program_id(1)
        i, j = it_ref[n], jt_ref[n]
        first_step = jnp.logical_and(pl.program_id(0) == 0, n == 0)
        last_step = jnp.logical_and(pl.program_id(0) == N_HEADS // hps - 1, n == len(pairs) - 1)

        @pl.when(first_step)
        def _():
            _start_copies(*_direct_copies(late_refs, gathered_refs, send_sems, recv_sems, local_sems, True))

        @pl.when(j == 0)
        def _():
            m_s[...] = jnp.full_like(m_s, NEG)
            acc_s[...] = jnp.zeros_like(acc_s)

        def update(masked, q0):
            qc = slice(q0, tq)

            def scores(hh):
                sl = slice(hh * SLAB, (hh + 1) * SLAB)
                return _dot(k_ref[:, sl], qt_ref[sl, qc])

            def softmax(hh, s):
                if masked:
                    s = jnp.where(lax.broadcasted_iota(jnp.int32, s.shape, 0)
                                  <= lax.broadcasted_iota(jnp.int32, s.shape, 1), s, NEG)
                m_old = m_s[hh][:, qc]
                m_new = jnp.maximum(m_old, jnp.max(s, axis=0, keepdims=True))
                m_s[hh, :, qc] = m_new
                return jnp.exp2(s - m_new).astype(BF16), jnp.exp2(m_old - m_new)

            def accumulate(hh, p, alpha):
                sl = slice(hh * SLAB, hh * SLAB + V_DIM_B + ONES_ROWS)
                acc_s[sl, qc] = alpha * acc_s[sl, qc] + _dot(vt_ref[sl, :], p)

            s_next, pending = scores(0), None
            for hh in range(hps):
                s = s_next
                if hh + 1 < hps:
                    s_next = scores(hh + 1)
                p, alpha = softmax(hh, s)
                if pending is not None:
                    accumulate(*pending)
                pending = (hh, p, alpha)
            accumulate(*pending)

        @pl.when(j < ratio * i)
        def _():
            update(False, 0)

        for part in range(ratio):
            @pl.when(j == ratio * i + part)
            def _():
                update(True, part * tk)

        @pl.when(j == ratio * i + ratio - 1)
        def _():
            for hh in range(hps):
                sl = slice(hh * SLAB, (hh + 1) * SLAB)
                den = acc_s[hh * SLAB + V_DIM_B:hh * SLAB + V_DIM_B + 1, :]
                values = lax.broadcasted_iota(jnp.int32, (SLAB, tq), 0) < V_DIM_B
                ot = jnp.where(values, acc_s[sl, :] / den, 0.0)
                ot_ref[sl, :] = ot.astype(BF16)
                o_ref[:, sl] = ot.T.astype(BF16)
                lse = m_s[hh] + jnp.log2(den)
                qa_ref[sl, :] = _plant_rows(qt_ref[sl, :].astype(F32), Q_HEAD_B, lse).astype(BF16)

        @pl.when(last_step)
        def _():
            _wait_copies(*_direct_copies(late_refs, gathered_refs, send_sems, recv_sems, local_sems, True))

    grid_spec = pltpu.PrefetchScalarGridSpec(
        num_scalar_prefetch=2, grid=(N_HEADS // hps, len(pairs)),
        in_specs=[pl.BlockSpec((w, tq), lambda h, n, it, jt: (h, it[n])),
                  pl.BlockSpec((tk, w), lambda h, n, it, jt: (jt[n], h)),
                  pl.BlockSpec((w, tk), lambda h, n, it, jt: (h, jt[n]))] + [ANY_SPEC] * n_late,
        out_specs=[pl.BlockSpec((tq, w), lambda h, n, it, jt: (it[n], h)),
                   pl.BlockSpec((w, tq), lambda h, n, it, jt: (h, it[n])),
                   pl.BlockSpec((w, tq), lambda h, n, it, jt: (h, it[n]))] + [ANY_SPEC] * n_late,
        scratch_shapes=[pltpu.VMEM((hps, 1, tq), F32), pltpu.VMEM((w, tq), F32)] + _exchange_scratch(n_late))
    outs = pl.pallas_call(
        body, name="mla_fwd", grid_spec=grid_spec,
        out_shape=[jax.ShapeDtypeStruct((t, HM), BF16), jax.ShapeDtypeStruct((HM, t), BF16),
                   jax.ShapeDtypeStruct((HM, t), BF16)]
        + [jax.ShapeDtypeStruct((N_DEV,) + a.shape, a.dtype) for a in late],
        compiler_params=_params(("arbitrary", "arbitrary")),
    )(i_tab, j_tab, qt, kb, vt, *late)
    return outs[0], outs[1], outs[2], list(outs[3:])


def _mla_bwd(qt, kb, kt, vb, d_ob_t, grad_slices):
    t = kb.shape[0]
    tk = _attn_tile(t)
    ratio = 2 if t >= 2 * tk else 1
    tq = ratio * tk
    nk, nq = t // tk, t // tq
    hps = MLA_HEADS_PER_STEP
    w = hps * SLAB
    pairs = [(j, i) for j in range(nk) for i in range(j // ratio, nq)]
    j_tab = jnp.asarray(np.array([p[0] for p in pairs], np.int32))
    i_tab = jnp.asarray(np.array([p[1] for p in pairs], np.int32))

    n_ex = len(grad_slices)

    def body(jt_ref, it_ref, qt_ref, dot_ref, k_ref, kt_ref, v_ref, *rest):
        slice_refs, (dqt_ref, dkt_ref, dvt_ref) = rest[:n_ex], rest[n_ex:n_ex + 3]
        part_refs = rest[n_ex + 3:2 * n_ex + 3]
        dk_s, dv_s, send_sems, recv_sems, local_sems = rest[2 * n_ex + 3:]
        n = pl.program_id(1)
        j, i = jt_ref[n], it_ref[n]
        first_step = jnp.logical_and(pl.program_id(0) == 0, n == 0)
        last_step = jnp.logical_and(pl.program_id(0) == N_HEADS // hps - 1, n == len(pairs) - 1)

        @pl.when(first_step)
        def _():
            _start_copies(*_direct_copies(slice_refs, part_refs, send_sems, recv_sems, local_sems, False))

        @pl.when(n == 0)
        def _():
            dqt_ref[...] = jnp.zeros_like(dqt_ref)

        def update(diagonal, q0):
            qc = slice(q0, tq)
            cols = pl.ds(pl.multiple_of(i * tq + q0, tk), tq - q0)

            def softmax_bwd(hh, s, dp):
                if diagonal:
                    s = jnp.where(lax.broadcasted_iota(jnp.int32, s.shape, 0)
                                  <= lax.broadcasted_iota(jnp.int32, s.shape, 1), s, NEG)
                p = jnp.exp2(s)
                return p.astype(BF16), (p * dp).astype(BF16)

            def gradients(hh, p, ds):
                base = hh * SLAB
                vrows = slice(base, base + V_DIM_B)
                qrows = slice(base, base + QK_NOPE + QK_ROPE)
                dv = _dot_nt(dot_ref[vrows, qc], p)
                dk = _dot_nt(qt_ref[qrows, qc], ds)
                if diagonal:
                    dv_s[base:base + SLAB, :] = jnp.concatenate([dv, jnp.zeros((SLAB - V_DIM_B, tk), F32)], axis=0)
                    dk_s[base:base + SLAB, :] = jnp.concatenate(
                        [dk, jnp.zeros((SLAB - QK_NOPE - QK_ROPE, tk), F32)], axis=0)
                else:
                    dv_s[vrows, :] += dv
                    dk_s[qrows, :] += dk
                dqt_ref[qrows, cols] += _dot(kt_ref[qrows, :], ds)

            def scores(hh):
                sl = slice(hh * SLAB, (hh + 1) * SLAB)
                return _dot(k_ref[:, sl], qt_ref[sl, qc])

            def dprod(hh):
                sl = slice(hh * SLAB, (hh + 1) * SLAB)
                return _dot(v_ref[:, sl], dot_ref[sl, qc])

            s_next = scores(0)
            for hh in range(hps):
                s = s_next
                dp = dprod(hh)
                if hh + 1 < hps:
                    s_next = scores(hh + 1)
                gradients(hh, *softmax_bwd(hh, s, dp))

        first_tile = lax.div(j, ratio)
        for part in range(ratio):
            @pl.when(jnp.logical_and(i == first_tile, lax.rem(j, ratio) == part))
            def _():
                update(True, part * tk)

        @pl.when(i > first_tile)
        def _():
            update(False, 0)

        @pl.when(i == nq - 1)
        def _():
            dkt_ref[...] = (dk_s[...] * (1.0 / LOG2E)).astype(BF16)
            dvt_ref[...] = dv_s[...].astype(BF16)

        @pl.when(last_step)
        def _():
            _wait_copies(*_direct_copies(slice_refs, part_refs, send_sems, recv_sems, local_sems, False))

    grid_spec = pltpu.PrefetchScalarGridSpec(
        num_scalar_prefetch=2, grid=(N_HEADS // hps, len(pairs)),
        in_specs=[pl.BlockSpec((w, tq), lambda h, n, jt, it: (h, it[n])),
                  pl.BlockSpec((w, tq), lambda h, n, jt, it: (h, it[n])),
                  pl.BlockSpec((tk, w), lambda h, n, jt, it: (jt[n], h)),
                  pl.BlockSpec((w, tk), lambda h, n, jt, it: (h, jt[n])),
                  pl.BlockSpec((tk, w), lambda h, n, jt, it: (jt[n], h))] + [ANY_SPEC] * n_ex,
        out_specs=[pl.BlockSpec((w, t), lambda h, n, jt, it: (h, 0)),
                   pl.BlockSpec((w, tk), lambda h, n, jt, it: (h, jt[n])),
                   pl.BlockSpec((w, tk), lambda h, n, jt, it: (h, jt[n]))] + [ANY_SPEC] * n_ex,
        scratch_shapes=[pltpu.VMEM((w, tk), F32), pltpu.VMEM((w, tk), F32)] + _exchange_scratch(n_ex))
    outs = pl.pallas_call(
        body, name="mla_bwd", grid_spec=grid_spec,
        out_shape=[jax.ShapeDtypeStruct((HM, t), F32), jax.ShapeDtypeStruct((HM, t), BF16),
                   jax.ShapeDtypeStruct((HM, t), BF16)]
        + [jax.ShapeDtypeStruct(a.shape, a.dtype) for a in grad_slices],
        compiler_params=_params(("arbitrary", "arbitrary")),
    )(j_tab, i_tab, qt, d_ob_t, kb, kt, vb, *grad_slices)
    return outs[0], outs[1], outs[2], list(outs[3:])


def _merge_fwd(out_a, out_b, gates, x, w_oa, w_ob, w_out, g2, g3):
    t = x.shape[0]
    tm = _wide_token_tile(t)

    def body(oa_ref, ob_ref, gates_ref, x_ref, woa_ref, wob_ref, wout_ref, g2_ref, g3_ref,
             oap_ref, obp_ref, merged_ref, y_ref, x1_ref, h2_ref):
        oa_p = _dot(oa_ref[...], woa_ref[...])
        ob_p = _dot(ob_ref[...], wob_ref[...])
        oap_ref[...] = oa_p.astype(BF16)
        obp_ref[...] = ob_p.astype(BF16)
        sa = _sigmoid(gates_ref[:, 0:D_MODEL].astype(F32))
        sb = _sigmoid(gates_ref[:, D_MODEL:2 * D_MODEL].astype(F32))
        merged = (sa * oa_p + sb * ob_p).astype(BF16)
        merged_ref[...] = merged
        y = _dot(merged, wout_ref[...])
        y_ref[...] = y
        x1 = x_ref[...] + y * _rms_r(y) * g2_ref[...]
        x1_ref[...] = x1
        h2_ref[...] = (x1 * _rms_r(x1) * g3_ref[...]).astype(BF16)

    def sds(dt):
        return jax.ShapeDtypeStruct((t, D_MODEL), dt)

    row = _row_spec(tm, D_MODEL)
    return pl.pallas_call(
        body, name="merge_fwd", grid=(t // tm,),
        in_specs=[_row_spec(tm, HM), _row_spec(tm, HM), _row_spec(tm, 2 * D_MODEL), row,
                  _full_spec((HM, D_MODEL)), _full_spec((HM, D_MODEL)), _full_spec((D_MODEL, D_MODEL)),
                  _full_spec((1, D_MODEL)), _full_spec((1, D_MODEL))],
        out_specs=[row] * 6,
        out_shape=[sds(BF16), sds(BF16), sds(BF16), sds(F32), sds(F32), sds(BF16)],
        compiler_params=_params(("parallel",)),
    )(out_a, out_b, gates, x, w_oa, w_ob, w_out, g2, g3)


def _merge_bwd(dx1, y, gates, oa_p, ob_p, out_a, out_b, out_b_t, merged, w_oa, w_ob, w_out, g2):
    t = dx1.shape[0]
    tm = _wide_token_tile(t)

    def body(dx1_ref, y_ref, gates_ref, oap_ref, obp_ref, oa_ref, ob_ref, obt_ref, merged_ref,
             woa_ref, wob_ref, wout_ref, g2_ref,
             dgates_ref, doa_ref, dobt_ref, dg2_ref, dwoa_ref, dwob_ref, dwout_ref):
        @pl.when(pl.program_id(0) == 0)
        def _():
            dwoa_ref[...] = jnp.zeros_like(dwoa_ref)
            dwob_ref[...] = jnp.zeros_like(dwob_ref)
            dwout_ref[...] = jnp.zeros_like(dwout_ref)

        dx1v = dx1_ref[...]
        yv = y_ref[...]
        r2 = _rms_r(yv)
        _acc_rows(dg2_ref, dx1v * yv * r2)
        dy = _rms_bwd(yv, r2, g2_ref[...], dx1v).astype(BF16)
        dwout_ref[...] += _dot_tn(merged_ref[...], dy)
        dm = _dot_nt(dy, wout_ref[...])
        sa = _sigmoid(gates_ref[:, 0:D_MODEL].astype(F32))
        sb = _sigmoid(gates_ref[:, D_MODEL:2 * D_MODEL].astype(F32))
        d_oap = (dm * sa).astype(BF16)
        d_obp = (dm * sb).astype(BF16)
        dwoa_ref[...] += _dot_tn(oa_ref[...], d_oap)
        dwob_ref[...] += _dot_tn(ob_ref[...], d_obp)
        dgates_ref[:, 0:D_MODEL] = (dm * oap_ref[...].astype(F32) * sa * (1.0 - sa)).astype(BF16)
        dgates_ref[:, D_MODEL:2 * D_MODEL] = (dm * obp_ref[...].astype(F32) * sb * (1.0 - sb)).astype(BF16)
        doa_ref[...] = _dot_nt(d_oap, woa_ref[...]).astype(BF16)
        d_ob_t = _dot_nt(wob_ref[...], d_obp)
        for hd in range(N_HEADS):
            sl = slice(hd * SLAB, (hd + 1) * SLAB)
            delta = jnp.sum(d_ob_t[sl, :] * obt_ref[sl, :].astype(F32), axis=0, keepdims=True)
            dobt_ref[sl, :] = _plant_rows(d_ob_t[sl, :], V_DIM_B, delta).astype(BF16)

    def sds(n, dt):
        return jax.ShapeDtypeStruct((t, n), dt)

    row = _row_spec(tm, D_MODEL)
    return pl.pallas_call(
        body, name="merge_bwd", grid=(t // tm,),
        in_specs=[row, row, _row_spec(tm, 2 * D_MODEL), row, row, _row_spec(tm, HM), _row_spec(tm, HM),
                  _col_spec(HM, tm), row,
                  _full_spec((HM, D_MODEL)), _full_spec((HM, D_MODEL)), _full_spec((D_MODEL, D_MODEL)),
                  _full_spec((1, D_MODEL))],
        out_specs=[_row_spec(tm, 2 * D_MODEL), _row_spec(tm, HM), _col_spec(HM, tm), _full_spec((1, D_MODEL)),
                   _full_spec((HM, D_MODEL)), _full_spec((HM, D_MODEL)), _full_spec((D_MODEL, D_MODEL))],
        out_shape=[sds(2 * D_MODEL, BF16), sds(HM, BF16), jax.ShapeDtypeStruct((HM, t), BF16),
                   jax.ShapeDtypeStruct((1, D_MODEL), F32),
                   jax.ShapeDtypeStruct((HM, D_MODEL), F32), jax.ShapeDtypeStruct((HM, D_MODEL), F32),
                   jax.ShapeDtypeStruct((D_MODEL, D_MODEL), F32)],
        compiler_params=_params(("arbitrary",), VMEM_LIMIT_MERGE_BWD),
    )(dx1, y, gates, oa_p, ob_p, out_a, out_b, out_b_t, merged, w_oa, w_ob, w_out, g2)


def _mlp_fwd_bwd(x1, h2, target, w_up, w_down, g3, g4):
    t = x1.shape[0]
    tm = _token_tile(t)
    fs = D_FF // N_DEV

    def body(x1_ref, h2_ref, tgt_ref, wup_ref, wdown_ref, g3_ref, g4_ref,
             a_ref, du_ref, dy2_ref, dx1_ref, loss_ref, dg3_ref, dg4_ref):
        x1v = x1_ref[...]
        h2v = h2_ref[...]
        u = jnp.concatenate([_dot(h2v, wup_ref[s]) for s in range(N_DEV)], axis=1)
        ru = jnp.maximum(u, 0.0)
        a = (ru * ru).astype(BF16)
        a_ref[...] = a
        y2 = _dot(a, wdown_ref[...])
        r4 = _rms_r(y2)
        diff = x1v + y2 * r4 * g4_ref[...] - tgt_ref[...]
        _acc_rows(loss_ref, jnp.sum(diff * diff, axis=-1, keepdims=True) * (0.5 / D_MODEL)
                  * jnp.ones((1, SLAB), F32))
        dx2 = diff * (1.0 / D_MODEL)
        _acc_rows(dg4_ref, dx2 * y2 * r4)
        dy2 = _rms_bwd(y2, r4, g4_ref[...], dx2).astype(BF16)
        dy2_ref[...] = dy2
        du = (_dot_nt(dy2, wdown_ref[...]) * (2.0 * ru)).astype(BF16)
        du_ref[...] = du
        dh2 = _dot_nt(du[:, 0:fs], wup_ref[0])
        for s in range(1, N_DEV):
            dh2 += _dot_nt(du[:, s * fs:(s + 1) * fs], wup_ref[s])
        r3 = _rms_r(x1v)
        _acc_rows(dg3_ref, dh2 * x1v * r3)
        dx1_ref[...] = dx2 + _rms_bwd(x1v, r3, g3_ref[...], dh2)

    row = _row_spec(tm, D_MODEL)
    frow = _row_spec(tm, D_FF)
    vec = _full_spec((1, D_MODEL))
    return pl.pallas_call(
        body, name="mlp_fwd_bwd", grid=(t // tm,),
        in_specs=[row, row, row, _full_spec((N_DEV, D_MODEL, fs)), _full_spec((D_FF, D_MODEL)), vec, vec],
        out_specs=[frow, frow, row, row, _full_spec((1, SLAB)), vec, vec],
        out_shape=[jax.ShapeDtypeStruct((t, D_FF), BF16), jax.ShapeDtypeStruct((t, D_FF), BF16),
                   jax.ShapeDtypeStruct((t, D_MODEL), BF16), jax.ShapeDtypeStruct((t, D_MODEL), F32),
                   jax.ShapeDtypeStruct((1, SLAB), F32), jax.ShapeDtypeStruct((1, D_MODEL), F32),
                   jax.ShapeDtypeStruct((1, D_MODEL), F32)],
        compiler_params=_params(("arbitrary",)),
    )(x1, h2, target, w_up, w_down, g3, g4)


def _latent_bwd(dqb_t, dkb_t, dvb_t, cq, ckv, cqn, ckvn, rope_ct, rope_s1t, rope_s2t, g_q, g_kv, w_qb, w_kvb):
    t = cq.shape[0]
    tm = min(t, 2 * _wide_token_tile(t))

    def body(dqt_ref, dkt_ref, dvt_ref, cq_ref, ckv_ref, cqn_ref, ckvn_ref, ct_ref, s1t_ref, s2t_ref,
             gq_ref, gkv_ref, wqb_ref, wkvb_ref,
             dlate_ref, dgq_ref, dgkv_ref, dwqb_ref, dwkvb_ref, dqbrt_ref, dkvbt_ref):
        @pl.when(pl.program_id(0) == 0)
        def _():
            dwqb_ref[...] = jnp.zeros_like(dwqb_ref)
            dwkvb_ref[...] = jnp.zeros_like(dwkvb_ref)

        ct, s1t, s2t = ct_ref[...], s1t_ref[...], s2t_ref[...]
        dk_sum_t = jnp.zeros((SLAB, tm), F32)
        for hd in range(N_HEADS):
            sl = slice(hd * SLAB, (hd + 1) * SLAB)
            dqbrt_ref[sl, :] = _rope_t_bwd(dqt_ref[sl, :] * SCALE_B, ct, s1t, s2t).astype(BF16)
            dk_sum_t += dkt_ref[sl, :].astype(F32)
        dkvbt_ref[0:HM, :] = dkt_ref[...]
        dkvbt_ref[HM:2 * HM, :] = dvt_ref[...]
        dkr = _rope_t_bwd(dk_sum_t, ct, s1t, s2t).T
        dwqb_ref[...] += _dot(dqbrt_ref[...], cqn_ref[...])
        dwkvb_ref[...] += _dot(dkvbt_ref[...], ckvn_ref[...])
        dcqn = _dot(wqb_ref[...], dqbrt_ref[...]).T
        cq = cq_ref[...]
        rq = _rms_r(cq)
        _acc_rows(dgq_ref, dcqn * cq * rq)
        dcq = _rms_bwd(cq, rq, gq_ref[...], dcqn)
        dckvn = _dot(wkvb_ref[...], dkvbt_ref[...]).T
        ckv = ckv_ref[...]
        rkv = _rms_r(ckv)
        _acc_rows(dgkv_ref, dckvn * ckv * rkv)
        dckv = _rms_bwd(ckv, rkv, gkv_ref[...], dckvn)
        dlate_ref[:, 0:C_CKV - C_CQ] = dcq.astype(BF16)
        dlate_ref[:, C_CKV - C_CQ:C_KR - C_CQ] = dckv.astype(BF16)
        dlate_ref[:, C_KR - C_CQ:D_IN_PAD - C_CQ] = dkr.astype(BF16)

    hmt = _col_spec(HM, tm)
    tab = _col_spec(SLAB, tm)
    return pl.pallas_call(
        body, name="latent_bwd", grid=(t // tm,),
        in_specs=[hmt, hmt, hmt,
                  _row_spec(tm, Q_LORA), _row_spec(tm, KV_LORA), _row_spec(tm, Q_LORA), _row_spec(tm, KV_LORA),
                  tab, tab, tab, _full_spec((1, Q_LORA)), _full_spec((1, KV_LORA)),
                  _full_spec((Q_LORA, HM)), _full_spec((KV_LORA, 2 * HM))],
        out_specs=[_row_spec(tm, D_IN_PAD - C_CQ), _full_spec((1, Q_LORA)), _full_spec((1, KV_LORA)),
                   _full_spec((HM, Q_LORA)), _full_spec((2 * HM, KV_LORA))],
        out_shape=[jax.ShapeDtypeStruct((t, D_IN_PAD - C_CQ), BF16),
                   jax.ShapeDtypeStruct((1, Q_LORA), F32), jax.ShapeDtypeStruct((1, KV_LORA), F32),
                   jax.ShapeDtypeStruct((HM, Q_LORA), F32), jax.ShapeDtypeStruct((2 * HM, KV_LORA), F32)],
        scratch_shapes=[pltpu.VMEM((HM, tm), BF16), pltpu.VMEM((2 * HM, tm), BF16)],
        compiler_params=_params(("arbitrary",)),
    )(dqb_t, dkb_t, dvb_t, cq, ckv, cqn, ckvn, rope_ct, rope_s1t, rope_s2t, g_q, g_kv, w_qb, w_kvb)


def _inproj_bwd(dgates, dqkv, dlate, x, dx1, g1, w_in, grad_slices, only, small):
    t = x.shape[0]
    tm = _wide_token_tile(t)
    n_ex = len(grad_slices)
    zeroed = [a for a in range(n_ex) if only[a] is not None]

    def body(dgates_ref, dqkv_ref, dlate_ref, x_ref, dx1_ref, g1_ref, win_ref, *rest):
        slice_refs, small_ref = rest[:n_ex], rest[n_ex]
        dx_ref = rest[n_ex + 1]
        part_refs = rest[n_ex + 2:2 * n_ex + 2]
        small_dst, dg1_dst = rest[2 * n_ex + 2:2 * n_ex + 4]
        dproj_ref, dg1_ref, send_sems, recv_sems, local_sems = rest[2 * n_ex + 4:2 * n_ex + 9]
        zero_refs, zero_sem = rest[2 * n_ex + 9:-1], rest[-1]
        sems = (send_sems, recv_sems, local_sems)

        def slice_copies():
            return _direct_copies(slice_refs, part_refs, *sems, False, only=only)

        def small_copies():
            return _direct_copies([small_ref], [small_dst], *sems, True, sem_base=n_ex)

        @pl.when(pl.program_id(0) == 0)
        def _():
            _start_copies(*slice_copies())
            _start_copies(*small_copies())
            x_, y_, c_ = _mesh_pos()
            me = 4 * x_ + 2 * y_ + c_
            for a, z_ref in zip(zeroed, zero_refs):
                outside = me != only[a][0]
                for d in only[a][1:]:
                    outside = jnp.logical_and(outside, me != d)

                @pl.when(outside)
                def _():
                    z_ref[...] = jnp.zeros_like(z_ref)
                    fills = [pltpu.make_async_copy(z_ref, part_refs[a].at[k], zero_sem.at[k])
                             for k in range(N_DEV)]
                    for cp in fills:
                        cp.start()
                    for cp in fills:
                        cp.wait()

        dproj_ref[:, C_GATES:C_QA] = dgates_ref[...]
        dproj_ref[:, C_QA:C_CQ] = dqkv_ref[...]
        dproj_ref[:, C_CQ:D_IN_PAD] = dlate_ref[...]
        dh = _dot_nt(dproj_ref[...], win_ref[...])
        xv = x_ref[...]
        r1 = _rms_r(xv)
        _acc_rows(dg1_ref, dh * xv * r1)
        dx_ref[...] = dx1_ref[...] + _rms_bwd(xv, r1, g1_ref[...], dh)

        @pl.when(pl.program_id(0) == t // tm - 1)
        def _():
            gain_copies = _direct_copies([dg1_ref], [dg1_dst], *sems, True, sem_base=n_ex + 1)
            _start_copies(*gain_copies)
            _wait_copies(*slice_copies())
            _wait_copies(*small_copies())
            _wait_copies(*gain_copies)

    kvw = N_KV_A * SLAB
    row = _row_spec(tm, D_MODEL)
    outs = pl.pallas_call(
        body, name="inproj_bwd", grid=(t // tm,),
        in_specs=[_row_spec(tm, 2 * D_MODEL), _row_spec(tm, HM + 2 * kvw), _row_spec(tm, D_IN_PAD - C_CQ),
                  row, row, _full_spec((1, D_MODEL)), _full_spec((D_MODEL, D_IN_PAD))]
        + [ANY_SPEC] * (n_ex + 1),
        out_specs=[row] + [ANY_SPEC] * (n_ex + 2),
        out_shape=[jax.ShapeDtypeStruct((t, D_MODEL), F32)]
        + [jax.ShapeDtypeStruct(a.shape, a.dtype) for a in grad_slices]
        + [jax.ShapeDtypeStruct((N_DEV,) + small.shape, F32), jax.ShapeDtypeStruct((N_DEV, 1, D_MODEL), F32)],
        scratch_shapes=[pltpu.VMEM((tm, D_IN_PAD), BF16), pltpu.VMEM((1, D_MODEL), F32)]
        + _exchange_scratch(n_ex + 2)
        + [pltpu.VMEM(grad_slices[a].shape[1:], grad_slices[a].dtype) for a in zeroed]
        + [pltpu.SemaphoreType.DMA((N_DEV,))],
        compiler_params=_params(("arbitrary",)),
    )(dgates, dqkv, dlate, x, dx1, g1, w_in, *grad_slices, small)
    return outs[0], list(outs[1:n_ex + 1]), outs[n_ex + 1], outs[n_ex + 2]


def _matmul_tn(a, b, name, out_dtype=F32, n_shards=1):
    t, k = a.shape
    n = b.shape[1]
    bn = min(n, 2048)
    bt = min(t, 2048)
    bk = min(k, 2048 * 1024 // bn)
    ns = n // n_shards
    per_block = bn // ns
    steps = t // bt

    def body(a_ref, b_ref, o_ref, acc):
        s = pl.program_id(2)

        @pl.when(s == 0)
        def _():
            acc[...] = jnp.zeros_like(acc)

        acc[...] += _dot_tn(a_ref[...], b_ref[...])

        @pl.when(s == steps - 1)
        def _():
            if n_shards > 1:
                for p in range(per_block):
                    o_ref[p] = acc[:, p * ns:(p + 1) * ns].astype(out_dtype)
            else:
                o_ref[...] = acc[...].astype(out_dtype)

    if n_shards > 1:
        out_spec = pl.BlockSpec((per_block, bk, ns), lambda i, j, s: (j, i, 0))
        out_shape = jax.ShapeDtypeStruct((n_shards, k, ns), out_dtype)
    else:
        out_spec = pl.BlockSpec((bk, bn), lambda i, j, s: (i, j))
        out_shape = jax.ShapeDtypeStruct((k, n), out_dtype)
    return pl.pallas_call(
        body, name=name, grid=(k // bk, n // bn, steps),
        in_specs=[pl.BlockSpec((bt, bk), lambda i, j, s: (s, i)), pl.BlockSpec((bt, bn), lambda i, j, s: (s, j))],
        out_specs=out_spec, out_shape=out_shape, scratch_shapes=[pltpu.VMEM((bk, bn), F32)],
        compiler_params=_params(("parallel", "parallel", "arbitrary")),
    )(a, b)


def _two_level_gather(srcs, dsts, send_sems, recv_sems, local_sems):
    n = len(srcs)
    x, y, c = _mesh_pos()
    me, sibling = (x, y, c), (x, y, 1 - c)
    chips = [(1 - x, y), (x, 1 - y), (1 - x, 1 - y)]

    def slot(a, px, py, pc):
        return dsts[a].at[4 * px + 2 * py + pc]

    def copy(a, k, block, to, src=None):
        return pltpu.make_async_remote_copy(
            src_ref=slot(a, *block) if src is None else src, dst_ref=slot(a, *block),
            send_sem=send_sems.at[(N_DEV - 1) * a + k], recv_sem=recv_sems.at[(N_DEV - 1) * a + k],
            device_id=to, device_id_type=pl.DeviceIdType.MESH)

    def own_copies():
        mine = [pltpu.make_async_copy(srcs[a], slot(a, *me), local_sems.at[a]) for a in range(n)]
        first = []
        for a in range(n):
            first.append(copy(a, 0, me, sibling, src=srcs[a]))
            first += [copy(a, 1 + j, me, (*chip, c), src=srcs[a]) for j, chip in enumerate(chips)]
        return mine, first

    def start():
        mine, first = own_copies()
        for cp in mine + first:
            cp.start()

    def finish():
        mine, first = own_copies()
        passed = []
        for j, chip in enumerate(chips):
            for a in range(n):
                copy(a, 1 + j, (*chip, c), me).wait_recv()
                passed.append(copy(a, 4 + j, (*chip, c), sibling))
                passed[-1].start()
        for a in range(n):
            copy(a, 0, sibling, me).wait_recv()
        for j, chip in enumerate(chips):
            for a in range(n):
                copy(a, 4 + j, (*chip, 1 - c), me).wait_recv()
        for cp in first + passed:
            cp.wait_send()
        for cp in mine:
            cp.wait()

    return start, finish


def _adamw(parts, w, m, v, name):
    n_parts = len(parts)
    _, k, n = parts[0].shape
    bk = min(k, ADAM_ROWS)
    c1 = 1.0 - ADAM_B1 ** ADAM_STEP
    c2 = 1.0 - ADAM_B2 ** ADAM_STEP

    def body(*refs):
        p_refs, (w_ref, m_ref, v_ref, g_ref, d_ref, mo_ref, vo_ref) = refs[:n_parts], refs[n_parts:]
        g = p_refs[0][0].astype(F32)
        for p_ref in p_refs:
            for s in range(N_DEV):
                if p_ref is not p_refs[0] or s > 0:
                    g = g + p_ref[s].astype(F32)
        g_ref[0] = g
        m_new = ADAM_B1 * m_ref[0] + (1.0 - ADAM_B1) * g
        v_new = ADAM_B2 * v_ref[0] + (1.0 - ADAM_B2) * (g * g)
        mo_ref[0] = m_new
        vo_ref[0] = v_new
        m_hat = m_new / c1
        v_hat = v_new / c2
        d_ref[0] = -ADAM_LR * (m_hat / (jnp.sqrt(v_hat) + ADAM_EPS) + ADAM_WD * w_ref[0])

    blk = pl.BlockSpec((1, bk, n), lambda i: (0, i, 0))
    out = jax.ShapeDtypeStruct((1, k, n), F32)
    return pl.pallas_call(
        body, name=name, grid=(k // bk,),
        in_specs=[pl.BlockSpec((N_DEV, bk, n), lambda i: (0, i, 0))] * n_parts + [blk, blk, blk],
        out_specs=[blk] * 4, out_shape=[out] * 4,
        compiler_params=_params(("parallel",)),
    )(*parts, w, m, v)


def _adamw_small(parts, w, m, v):
    k = len(SMALL_LAYOUT)
    c1 = 1.0 - ADAM_B1 ** ADAM_STEP
    c2 = 1.0 - ADAM_B2 ** ADAM_STEP

    def body(p_ref, *refs):
        w_refs, m_refs, v_refs, outs = refs[:k], refs[k:2 * k], refs[2 * k:3 * k], refs[3 * k:]
        total = p_ref[0]
        for s in range(1, N_DEV):
            total = total + p_ref[s]
        for i, (_, row, off, width) in enumerate(SMALL_LAYOUT):
            g = total[row:row + 1, off:off + width]
            m_new = ADAM_B1 * m_refs[i][...] + (1.0 - ADAM_B1) * g
            v_new = ADAM_B2 * v_refs[i][...] + (1.0 - ADAM_B2) * (g * g)
            outs[4 * i][...] = g
            outs[4 * i + 1][...] = -ADAM_LR * ((m_new / c1) / (jnp.sqrt(v_new / c2) + ADAM_EPS)
                                               + ADAM_WD * w_refs[i][...])
            outs[4 * i + 2][...] = m_new
            outs[4 * i + 3][...] = v_new
        outs[4 * k][...] = total[SMALL_LOSS_ROW:SMALL_LOSS_ROW + 1, SMALL_LOSS_OFF:SMALL_LOSS_OFF + 1]

    names = [name for name, *_ in SMALL_LAYOUT]
    out_shape = [jax.ShapeDtypeStruct(w[name].shape, F32) for name in names for _ in range(4)]
    outs = pl.pallas_call(
        body, name="adamw_small", out_shape=out_shape + [jax.ShapeDtypeStruct((1, 1), F32)],
    )(parts, *[w[n] for n in names], *[m[n] for n in names], *[v[n] for n in names])
    return {name: tuple(outs[4 * i:4 * i + 4]) for i, name in enumerate(names)}, outs[4 * k]


def _pad_heads_cols(w, heads, width):
    k = w.shape[0]
    w = w.reshape(k, heads, width)
    return jnp.pad(w, ((0, 0), (0, 0), (0, SLAB - width))).reshape(k, heads * SLAB)


def _unpad_heads_cols(w, heads, width):
    k = w.shape[0]
    return w.reshape(k, heads, SLAB)[:, :, :width].reshape(k, heads * width)


def _pad_heads_rows(w, heads, width):
    n = w.shape[1]
    w = w.reshape(heads, width, n)
    return jnp.pad(w, ((0, 0), (0, SLAB - width), (0, 0))).reshape(heads * SLAB, n)


def _unpad_heads_rows(w, heads, width):
    n = w.shape[1]
    return w.reshape(heads, SLAB, n)[:, :width, :].reshape(heads * width, n)


def _pad_w_in(w_in):
    o = 2 * D_MODEL
    qa = _pad_heads_cols(w_in[:, o:o + 512], N_HEADS, HEAD_A)
    ka = _pad_heads_cols(w_in[:, o + 512:o + 640], N_KV_A, HEAD_A)
    va = _pad_heads_cols(w_in[:, o + 640:o + 768], N_KV_A, HEAD_A)
    kr = jnp.pad(w_in[:, o + 1152:o + 1184], ((0, 0), (QK_NOPE, SLAB - QK_NOPE - QK_ROPE)))
    return jnp.concatenate([w_in[:, :o], qa, ka, va, w_in[:, o + 768:o + 1152], kr], axis=1)


def _unpad_w_in(w):
    qa = _unpad_heads_cols(w[:, C_QA:C_KA], N_HEADS, HEAD_A)
    ka = _unpad_heads_cols(w[:, C_KA:C_VA], N_KV_A, HEAD_A)
    va = _unpad_heads_cols(w[:, C_VA:C_CQ], N_KV_A, HEAD_A)
    kr = w[:, C_KR + QK_NOPE:C_KR + QK_NOPE + QK_ROPE]
    return jnp.concatenate([w[:, :C_QA], qa, ka, va, w[:, C_CQ:C_KR], kr], axis=1)


def _pad_w_kvb(w_kvb):
    w = w_kvb.reshape(KV_LORA, N_HEADS, QK_NOPE + V_DIM_B)
    k = jnp.pad(w[:, :, :QK_NOPE], ((0, 0), (0, 0), (0, SLAB - QK_NOPE))).reshape(KV_LORA, HM)
    v = jnp.pad(w[:, :, QK_NOPE:], ((0, 0), (0, 0), (0, SLAB - V_DIM_B))).reshape(KV_LORA, HM)
    return jnp.concatenate([k, v], axis=1)


def _unpad_w_kvb(w):
    k = w[:, :HM].reshape(KV_LORA, N_HEADS, SLAB)[:, :, :QK_NOPE]
    v = w[:, HM:].reshape(KV_LORA, N_HEADS, SLAB)[:, :, :V_DIM_B]
    return jnp.concatenate([k, v], axis=2).reshape(KV_LORA, N_HEADS * (QK_NOPE + V_DIM_B))


def _col_shards(w):
    k, n = w.shape
    ns = n // N_DEV
    if ns % SLAB:
        return jnp.stack([w[:, d * ns:(d + 1) * ns] for d in range(N_DEV)])
    return w.reshape(k, N_DEV, ns).transpose(1, 0, 2)


def _from_col_shards(s):
    _, k, ns = s.shape
    if ns % SLAB:
        return jnp.concatenate([s[d] for d in range(N_DEV)], axis=1)
    return s.transpose(1, 0, 2).reshape(k, N_DEV * ns)


def _freq_row():
    freqs = ROPE_THETA ** (-jnp.arange(0, QK_ROPE, 2, dtype=F32) / QK_ROPE)
    return jnp.concatenate([jnp.zeros((QK_NOPE,), F32), freqs, freqs,
                            jnp.zeros((SLAB - QK_NOPE - QK_ROPE,), F32)]).reshape(1, SLAB)


SMALL_D_ROWS = ("pre_norm_mix", "post_norm_mix", "pre_norm_mlp", "post_norm_mlp")
SMALL_LAYOUT = tuple((name, i, 0, D_MODEL) for i, name in enumerate(SMALL_D_ROWS)) + (
    ("q_a_norm", 4, 0, Q_LORA), ("kv_a_norm", 4, 256, KV_LORA), ("sinks", 4, 384, N_HEADS))
SMALL_LOSS_ROW, SMALL_LOSS_OFF = 4, 512


def _pack_small(vals):
    row4 = jnp.concatenate([vals["q_a_norm"].reshape(-1), vals["kv_a_norm"].reshape(-1), vals["sinks"].reshape(-1),
                            jnp.zeros((SMALL_LOSS_OFF - 392,), F32), vals["loss"].reshape(-1),
                            jnp.zeros((1024 - SMALL_LOSS_OFF - 1,), F32)])
    rows = [vals[n].reshape(1024) for n in SMALL_D_ROWS] + [row4]
    return jnp.concatenate([jnp.stack(rows), jnp.zeros((SMALL_ROWS - 5, 1024), F32)], axis=0)


WEIGHT_ORDER = ("pre_norm_mix", "w_in", "q_a_norm", "w_q_b", "kv_a_norm", "w_kv_b", "sinks", "w_o_a", "w_o_b",
                "w_out", "post_norm_mix", "pre_norm_mlp", "w_up", "w_down", "post_norm_mlp")


def kernel(x, positions, pre_norm_mix, w_in, q_a_norm, w_q_b, kv_a_norm, w_kv_b, sinks, w_o_a, w_o_b, w_out, post_norm_mix, pre_norm_mlp, w_up, w_down, post_norm_mlp, loss_target, m_pre_norm_mix, m_w_in, m_q_a_norm, m_w_q_b, m_kv_a_norm, m_w_kv_b, m_sinks, m_w_o_a, m_w_o_b, m_w_out, m_post_norm_mix, m_pre_norm_mlp, m_w_up, m_w_down, m_post_norm_mlp, v_pre_norm_mix, v_w_in, v_q_a_norm, v_w_q_b, v_kv_a_norm, v_w_kv_b, v_sinks, v_w_o_a, v_w_o_b, v_w_out, v_post_norm_mix, v_pre_norm_mlp, v_w_up, v_w_down, v_post_norm_mlp):
    weights = dict(pre_norm_mix=pre_norm_mix, w_in=w_in, q_a_norm=q_a_norm, w_q_b=w_q_b, kv_a_norm=kv_a_norm,
                   w_kv_b=w_kv_b, sinks=sinks, w_o_a=w_o_a, w_o_b=w_o_b, w_out=w_out, post_norm_mix=post_norm_mix,
                   pre_norm_mlp=pre_norm_mlp, w_up=w_up, w_down=w_down, post_norm_mlp=post_norm_mlp)
    m_in = dict(pre_norm_mix=m_pre_norm_mix, w_in=m_w_in, q_a_norm=m_q_a_norm, w_q_b=m_w_q_b, kv_a_norm=m_kv_a_norm,
                w_kv_b=m_w_kv_b, sinks=m_sinks, w_o_a=m_w_o_a, w_o_b=m_w_o_b, w_out=m_w_out,
                post_norm_mix=m_post_norm_mix, pre_norm_mlp=m_pre_norm_mlp, w_up=m_w_up, w_down=m_w_down,
                post_norm_mlp=m_post_norm_mlp)
    v_in = dict(pre_norm_mix=v_pre_norm_mix, w_in=v_w_in, q_a_norm=v_q_a_norm, w_q_b=v_w_q_b, kv_a_norm=v_kv_a_norm,
                w_kv_b=v_w_kv_b, sinks=v_sinks, w_o_a=v_w_o_a, w_o_b=v_w_o_b, w_out=v_w_out,
                post_norm_mix=v_post_norm_mix, pre_norm_mlp=v_pre_norm_mlp, w_up=v_w_up, w_down=v_w_down,
                post_norm_mlp=v_post_norm_mlp)

    xs, pos, target = x[0], positions[0], loss_target[0]
    t = xs.shape[0]
    pos_col = pos.reshape(t, 1)
    pos_row = pos.reshape(1, t)
    g1, g2, g3, g4 = (weights[n] for n in SMALL_D_ROWS)
    g_q, g_kv = q_a_norm, kv_a_norm
    sink_vec = sinks.reshape(N_HEADS)
    shard = {n: weights[n][0].astype(BF16) for n in EARLY + LATE}

    tables, (e_in, e_qb, e_kvb) = _rope_tables(pos_col, _freq_row(), [shard[n] for n in EARLY])
    w_in_p = _pad_w_in(_from_col_shards(e_in))
    w_qb = _pad_heads_cols(_from_col_shards(e_qb), N_HEADS, QK_NOPE + QK_ROPE)
    w_kvb = _pad_w_kvb(_from_col_shards(e_kvb))

    (h, gates, qa, ka, va, cq, ckv, cqn, ckvn, kb, vb, qt, kt, vt) = _inproj_fwd(
        xs, g1, w_in_p, g_q, g_kv, w_kvb, w_qb.T, w_kvb[:, :HM].T, w_kvb[:, HM:].T, w_in_p[:, C_KR:].T, tables)
    out_a, lse_a = _swa_fwd(qa, ka, va, pos_col, pos_row, sink_vec)
    out_b, out_b_t, qt_lse, (l_oa, l_ob, l_out, w_up_s, l_down) = _mla_fwd(qt, kb, vt, [shard[n] for n in LATE])
    w_oa = _pad_heads_rows(_from_col_shards(l_oa), N_HEADS, HEAD_A)
    w_ob = _pad_heads_rows(_from_col_shards(l_ob), N_HEADS, V_DIM_B)
    w_out_f = l_out.reshape(D_MODEL, D_MODEL)
    w_down_f = l_down.reshape(D_FF, D_MODEL)

    oa_p, ob_p, merged, y, x1, h2 = _merge_fwd(out_a, out_b, gates, xs, w_oa, w_ob, w_out_f, g2, g3)
    a, du, dy2, dx1, loss, dg3, dg4 = _mlp_fwd_bwd(x1, h2, target, w_up_s, w_down_f, g3, g4)
    (dgates, d_oa, d_ob_t, dg2, dw_oa, dw_ob, dw_out) = _merge_bwd(
        dx1, y, gates, oa_p, ob_p, out_a, out_b, out_b_t, merged, w_oa, w_ob, w_out_f, g2)
    late_slices = [
        _col_shards(_unpad_heads_rows(dw_oa, N_HEADS, HEAD_A)).astype(BF16),
        _col_shards(_unpad_heads_rows(dw_ob, N_HEADS, V_DIM_B)).astype(BF16),
        dw_out.astype(BF16).reshape(N_DEV, D_MODEL // N_DEV, D_MODEL),
        _matmul_tn(h2, du, "dw_up", BF16, N_DEV),
        _matmul_tn(a, dy2, "dw_down", BF16).reshape(N_DEV, D_FF // N_DEV, D_MODEL),
    ]
    dqkv_a, dsink = _swa_bwd(qa, ka, va, out_a, d_oa, lse_a, pos_col, pos_row, sink_vec)
    dw_in_early = jnp.concatenate([_matmul_tn(h, dgates, "dw_in_gates", BF16),
                                   _matmul_tn(h, dqkv_a, "dw_in_mixer_a", BF16),
                                   jnp.zeros((D_MODEL, D_IN_PAD - C_CQ), BF16)], axis=1)
    late_slices.append(_col_shards(_unpad_w_in(dw_in_early)))
    dqb_t, dkb_t, dvb_t, late_parts = _mla_bwd(qt_lse, kb, kt, vb, d_ob_t, late_slices)
    w_in_early_parts = late_parts.pop()
    dproj_late, dgq, dgkv, dw_qb_t, dw_kvb_t = _latent_bwd(
        dqb_t, dkb_t, dvb_t, cq, ckv, cqn, ckvn, *tables[3:], g_q, g_kv, w_qb, w_kvb)
    dw_l = _matmul_tn(h, dproj_late, "dw_in_latents", BF16)
    late_cols = jnp.concatenate([dw_l[:, :Q_LORA + KV_LORA], dw_l[:, C_KR - C_CQ + QK_NOPE:C_KR - C_CQ + Q_HEAD_B]],
                                axis=1)
    shard_cols = w_in.shape[2]
    head = late_cols.shape[1] - shard_cols
    w_in_late = jnp.concatenate([
        jnp.zeros((N_DEV - 2, D_MODEL, shard_cols), BF16),
        jnp.pad(late_cols[:, :head], ((0, 0), (shard_cols - head, 0)))[None], late_cols[:, head:][None]])
    early_slices = [
        w_in_late,
        _col_shards(_unpad_heads_cols(dw_qb_t.T, N_HEADS, QK_NOPE + QK_ROPE)).astype(BF16),
        _col_shards(_unpad_w_kvb(dw_kvb_t.T)).astype(BF16),
    ]
    small_grads = {"pre_norm_mix": jnp.zeros((1, D_MODEL), F32), "post_norm_mix": dg2, "pre_norm_mlp": dg3,
                   "post_norm_mlp": dg4, "q_a_norm": dgq, "kv_a_norm": dgkv,
                   "sinks": dsink.reshape(N_HEADS, BLOCK).sum(axis=1), "loss": loss[0, 0:1]}
    dx, early_parts, s_parts, dg1_parts = _inproj_bwd(
        dgates, dqkv_a, dproj_late, xs, dx1, g1, w_in_p, early_slices,
        only=[(N_DEV - 2, N_DEV - 1), None, None], small=_pack_small(small_grads))
    s_parts = s_parts.at[:, SMALL_D_ROWS.index("pre_norm_mix"), :].set(dg1_parts[:, 0, :])

    updates = {}
    all_parts = [[w_in_early_parts, early_parts[0]]] + [[p] for p in early_parts[1:] + late_parts]
    for name, parts in zip(EARLY + LATE, all_parts):
        outs = _adamw(parts, weights[name], m_in[name], v_in[name], "adamw_" + name)
        for kind, arr in zip(("g", "d", "m", "v"), outs):
            updates[kind, name] = arr
    small_out, loss_sum = _adamw_small(s_parts, weights, m_in, v_in)
    for name, outs in small_out.items():
        for kind, arr in zip(("g", "d", "m", "v"), outs):
            updates[kind, name] = arr
    results = [updates[kind, name] for kind in ("g", "d", "m", "v") for name in WEIGHT_ORDER]
    return (loss_sum.reshape(()), dx[None], *results)
```

```python
import functools

import numpy as np
import jax
import jax.numpy as jnp
from jax import lax
from jax.experimental import pallas as pl
from jax.experimental.pallas import tpu as pltpu

F32 = jnp.float32
BF16 = jnp.bfloat16

D_MODEL = 1024
D_FF = 4096
N_HEADS = 8
N_KV_A = 2
GROUP_A = N_HEADS // N_KV_A
HEAD_A = 64
QK_NOPE = 64
QK_ROPE = 32
V_DIM_B = 64
Q_LORA = 256
KV_LORA = 128
BLOCK = 128
SLAB = 128
ROPE_THETA = 10000.0
EPS = 1e-6
N_DEV = 8
NEG = -1e30

SCALE_A = HEAD_A ** -0.5
SCALE_B = (QK_NOPE + QK_ROPE) ** -0.5
LOG2E = 1.4426950408889634
SCORE_B = SCALE_B * LOG2E
MLA_HEADS_PER_STEP = 4
MLA_FWD_HEADS_PER_STEP = 8
Q_HEAD_B = QK_NOPE + QK_ROPE
SWA_BLOCKS_PER_STEP = 4
ONES_ROWS = 16
SLOPES_A = tuple(2.0 ** (-8.0 * (h + 1) / N_HEADS) for h in range(N_HEADS))

ADAM_LR = 0.001
ADAM_B1 = 0.9
ADAM_B2 = 0.999
ADAM_EPS = 1e-08
ADAM_WD = 0.01
ADAM_STEP = 10

HM = N_HEADS * SLAB
C_GATES = 0
C_QA = 2 * D_MODEL
C_KA = C_QA + HM
C_VA = C_KA + N_KV_A * SLAB
C_CQ = C_VA + N_KV_A * SLAB
C_CKV = C_CQ + Q_LORA
C_KR = C_CKV + KV_LORA
D_IN_PAD = C_KR + SLAB

VMEM_LIMIT = 56 * 1024 * 1024
VMEM_LIMIT_MERGE_BWD = 60 * 1024 * 1024

EARLY = ("w_in", "w_q_b", "w_kv_b")
LATE = ("w_o_a", "w_o_b", "w_out", "w_up", "w_down")
ADAM_ROWS = 256
MATMUL_COLS = 2048
MATMUL_TOKENS = 2048
MATMUL_ACC_ELEMS = 2048 * 1024
SMALL_ROWS = 8


def _token_tile(t):
    return min(256, t)


def _wide_token_tile(t):
    return min(512, t)


def _attn_tile(t):
    return 512 if t >= 2048 else 128


def _params(sem, vmem=VMEM_LIMIT):
    return pltpu.CompilerParams(dimension_semantics=sem, vmem_limit_bytes=vmem)


def _dot(a, b):
    return jnp.dot(a, b, preferred_element_type=F32)


def _dot_nt(a, b):
    return lax.dot_general(a, b, (((1,), (1,)), ((), ())), preferred_element_type=F32)


def _dot_tn(a, b):
    return lax.dot_general(a, b, (((0,), (0,)), ((), ())), preferred_element_type=F32)


def _rms_r(x):
    return lax.rsqrt(jnp.mean(x * x, axis=-1, keepdims=True) + EPS)


def _rms_bwd(x, r, g, dy):
    t = dy * g
    return r * t - x * (r * r * r) * jnp.mean(x * t, axis=-1, keepdims=True)


def _sigmoid(x):
    return 1.0 / (1.0 + jnp.exp(-x))


def _rope(x, c, s1, s2):
    return x * c + pltpu.roll(x, SLAB - 16, 1) * s1 + pltpu.roll(x, 16, 1) * s2


def _rope_bwd(d, c, s1, s2):
    return d * c + pltpu.roll(d * s1, 16, 1) + pltpu.roll(d * s2, SLAB - 16, 1)


def _roll_rows(x, shift):
    return jnp.concatenate([x[-shift:], x[:-shift]], axis=0)


def _rope_t(x, c, s1, s2):
    return x * c + _roll_rows(x, SLAB - 16) * s1 + _roll_rows(x, 16) * s2


def _rope_t_bwd(d, c, s1, s2):
    return d * c + _roll_rows(d * s1, 16) + _roll_rows(d * s2, SLAB - 16)


def _plant_rows(slab, row, vals):
    hi = vals.astype(BF16).astype(F32)
    lo = (vals - hi).astype(BF16).astype(F32)
    idx = lax.broadcasted_iota(jnp.int32, slab.shape, 0)
    return jnp.where(idx == row, -hi, jnp.where(idx == row + 1, -lo, slab))


def _row_spec(tm, n):
    return pl.BlockSpec((tm, n), lambda i: (i, 0))


def _col_spec(n, tm):
    return pl.BlockSpec((n, tm), lambda i: (0, i))


def _full_spec(shape):
    nd = len(shape)
    return pl.BlockSpec(shape, lambda i: (0,) * nd, pipeline_mode=pl.Buffered(1))


def _acc_rows(ref, val):
    @pl.when(pl.program_id(0) == 0)
    def _():
        ref[...] = jnp.zeros_like(ref)
    ref[...] += jnp.sum(val, axis=0, keepdims=True)


def _rope_tables(pos_col, freq_row, early):
    t = pos_col.shape[0]
    tm = _token_tile(t)
    n = len(early)

    def body(pos_ref, f_ref, *rest):
        shard_refs, (c_ref, s1_ref, s2_ref, ct_ref, s1t_ref, s2t_ref) = rest[:n], rest[n:n + 6]
        start, finish = _two_level_gather(shard_refs, rest[n + 6:2 * n + 6], *rest[2 * n + 6:])
        pl.when(pl.program_id(0) == 0)(start)
        ang = pos_ref[...].astype(F32) * f_ref[...]
        lane = lax.broadcasted_iota(jnp.int32, ang.shape, 1)
        s = jnp.sin(ang)
        c = jnp.cos(ang)
        s1 = jnp.where((lane >= 64) & (lane < 80), -s, 0.0)
        s2 = jnp.where((lane >= 80) & (lane < 96), s, 0.0)
        c_ref[...], s1_ref[...], s2_ref[...] = c, s1, s2
        ct_ref[...], s1t_ref[...], s2t_ref[...] = c.T, s1.T, s2.T
        pl.when(pl.program_id(0) == t // tm - 1)(finish)

    tab = jax.ShapeDtypeStruct((t, SLAB), F32)
    tabt = jax.ShapeDtypeStruct((SLAB, t), F32)
    outs = pl.pallas_call(
        body, name="rope_tables", grid=(t // tm,),
        in_specs=[_row_spec(tm, 1), _full_spec((1, SLAB))] + [ANY_SPEC] * n,
        out_specs=[_row_spec(tm, SLAB)] * 3 + [_col_spec(SLAB, tm)] * 3 + [ANY_SPEC] * n,
        out_shape=[tab] * 3 + [tabt] * 3 + [jax.ShapeDtypeStruct((N_DEV,) + a.shape, a.dtype) for a in early],
        scratch_shapes=_exchange_scratch(n),
        compiler_params=_params(("arbitrary",)),
    )(pos_col, freq_row, *early)
    return outs[:6], outs[6:]


def _inproj_fwd(x, g1, w_in, g_q, g_kv, w_kvb, w_qb_t, w_kb_t, w_vb_t, w_kr_t, tables):
    t = x.shape[0]
    tm = _wide_token_tile(t)

    def body(x_ref, g1_ref, win_ref, gq_ref, gkv_ref, wkvb_ref, wqbt_ref, wkbt_ref, wvbt_ref, wkrt_ref,
             c_ref, s1_ref, s2_ref, ct_ref, s1t_ref, s2t_ref,
             h_ref, gates_ref, qa_ref, ka_ref, va_ref, cq_ref, ckv_ref, cqn_ref, ckvn_ref,
             kb_ref, vb_ref, qt_ref, kt_ref, vt_ref):
        xv = x_ref[...]
        h = (xv * _rms_r(xv) * g1_ref[...]).astype(BF16)
        h_ref[...] = h
        proj = _dot(h, win_ref[...])
        gates_ref[...] = proj[:, C_GATES:C_QA].astype(BF16)
        qa_ref[...] = proj[:, C_QA:C_KA].astype(BF16)
        ka_ref[...] = proj[:, C_KA:C_VA].astype(BF16)
        va_ref[...] = proj[:, C_VA:C_CQ].astype(BF16)
        cq = proj[:, C_CQ:C_CKV]
        ckv = proj[:, C_CKV:C_KR]
        kr = proj[:, C_KR:D_IN_PAD]
        cq_ref[...] = cq
        ckv_ref[...] = ckv
        cqn = (cq * _rms_r(cq) * gq_ref[...]).astype(BF16)
        ckvn = (ckv * _rms_r(ckv) * gkv_ref[...]).astype(BF16)
        cqn_ref[...] = cqn
        ckvn_ref[...] = ckvn
        c, s1, s2 = c_ref[...], s1_ref[...], s2_ref[...]
        kvb = _dot(ckvn, wkvb_ref[...])
        kr_rot = _rope(kr, c, s1, s2)
        ct, s1t, s2t = ct_ref[...], s1t_ref[...], s2t_ref[...]
        q_t = _dot_nt(wqbt_ref[...], cqn)
        k_t = _dot_nt(wkbt_ref[...], ckvn)
        kr_t = _rope_t(_dot_nt(wkrt_ref[...], h), ct, s1t, s2t)
        k_lane = lax.broadcasted_iota(jnp.int32, (1, SLAB), 1)
        k_ones = jnp.where((k_lane == Q_HEAD_B) | (k_lane == Q_HEAD_B + 1), 1.0, 0.0)
        for hd in range(N_HEADS):
            sl = slice(hd * SLAB, (hd + 1) * SLAB)
            kb_ref[:, sl] = (kvb[:, sl] + kr_rot + k_ones).astype(BF16)
            qt_ref[sl, :] = (_rope_t(q_t[sl, :], ct, s1t, s2t) * SCORE_B).astype(BF16)
            kt_ref[sl, :] = (k_t[sl, :] + kr_t).astype(BF16)
        v_lane = lax.broadcasted_iota(jnp.int32, (1, HM), 1) & (SLAB - 1)
        v_ones = jnp.where((v_lane == V_DIM_B) | (v_lane == V_DIM_B + 1), 1.0, 0.0)
        vb_ref[...] = (kvb[:, HM:2 * HM] + v_ones).astype(BF16)
        pad_row = lax.broadcasted_iota(jnp.int32, (HM, 1), 0) & (SLAB - 1)
        ones_rows = jnp.where((pad_row >= V_DIM_B) & (pad_row < V_DIM_B + ONES_ROWS), 1.0, 0.0)
        vt_ref[...] = (_dot_nt(wvbt_ref[...], ckvn) + ones_rows).astype(BF16)

    def sds(n, dt):
        return jax.ShapeDtypeStruct((t, n), dt)

    outs = [(D_MODEL, BF16), (2 * D_MODEL, BF16), (HM, BF16), (N_KV_A * SLAB, BF16), (N_KV_A * SLAB, BF16),
            (Q_LORA, F32), (KV_LORA, F32), (Q_LORA, BF16), (KV_LORA, BF16), (HM, BF16), (HM, BF16)]
    tab, tabt = _row_spec(tm, SLAB), _col_spec(SLAB, tm)
    return pl.pallas_call(
        body, name="inproj_fwd", grid=(t // tm,),
        in_specs=[_row_spec(tm, D_MODEL), _full_spec((1, D_MODEL)), _full_spec((D_MODEL, D_IN_PAD)),
                  _full_spec((1, Q_LORA)), _full_spec((1, KV_LORA)), _full_spec((KV_LORA, 2 * HM)),
                  _full_spec((HM, Q_LORA)), _full_spec((HM, KV_LORA)), _full_spec((HM, KV_LORA)),
                  _full_spec((SLAB, D_MODEL)), tab, tab, tab, tabt, tabt, tabt],
        out_specs=[_row_spec(tm, n) for n, _ in outs] + [_col_spec(HM, tm)] * 3,
        out_shape=[sds(n, dt) for n, dt in outs] + [jax.ShapeDtypeStruct((HM, t), BF16)] * 3,
        compiler_params=_params(("parallel",)),
    )(x, g1, w_in, g_q, g_kv, w_kvb, w_qb_t, w_kb_t, w_vb_t, w_kr_t, *tables)


def _tile_group(a):
    return jnp.concatenate([a] * GROUP_A, axis=1)


def _swa_masks():
    row = lax.broadcasted_iota(jnp.int32, (BLOCK, GROUP_A * BLOCK), 0)
    col = lax.broadcasted_iota(jnp.int32, (BLOCK, GROUP_A * BLOCK), 1) & (BLOCK - 1)
    return row <= col, row > col


def _heads_beside(ref, g, rows=slice(None)):
    return jnp.concatenate([ref[rows, (g * GROUP_A + hh) * SLAB:(g * GROUP_A + hh + 1) * SLAB].T
                            for hh in range(GROUP_A)], axis=1)


def _rows_beside(ref, g):
    return jnp.concatenate([ref[g * GROUP_A + hh] for hh in range(GROUP_A)], axis=1)


def _swa_rows(sinks):
    slopes = jnp.repeat(jnp.asarray(SLOPES_A, F32).reshape(N_KV_A, GROUP_A, 1), BLOCK, axis=2)
    sink_rows = jnp.repeat(sinks.reshape(N_KV_A, GROUP_A, 1), BLOCK, axis=2)
    return slopes.reshape(N_KV_A, 1, GROUP_A * BLOCK), sink_rows.reshape(N_KV_A, 1, GROUP_A * BLOCK)


def _swa_fwd(qa, ka, va, pos_col, pos_row, sinks):
    t = qa.shape[0]
    sub = min(SWA_BLOCKS_PER_STEP, t // BLOCK)
    tb = sub * BLOCK
    gw = GROUP_A * BLOCK
    slope_rows, sink_rows = _swa_rows(sinks)

    def body(q_ref, k_ref, kp_ref, v_ref, vp_ref, pk_ref, pkp_ref, pq_ref, slope_ref, sink_ref, o_ref, l_ref):
        i = pl.program_id(0)
        mask_c, older = _swa_masks()
        for s in range(sub):
            rows = slice(s * BLOCK, (s + 1) * BLOCK)
            before = slice((s - 1) * BLOCK, s * BLOCK)
            if s == 0:
                k_p, v_p, pk_p, mask_p = kp_ref, vp_ref, pkp_ref[...], jnp.logical_and(older, i > 0)
                before = slice(None)
            else:
                k_p, v_p, pk_p, mask_p = k_ref, v_ref, pk_ref[before, :], older
            pq = pq_ref[:, rows]
            dist_c = _tile_group(jnp.abs(pk_ref[rows, :] - pq).astype(F32))
            dist_p = _tile_group(jnp.abs(pk_p - pq).astype(F32))
            raw = []
            for g in range(N_KV_A):
                gs = slice(g * SLAB, (g + 1) * SLAB)
                x = _heads_beside(q_ref, g, rows)
                raw.append((_dot(k_ref[rows, gs], x), _dot(k_p[before, gs], x)))
            for g in range(N_KV_A):
                gs = slice(g * SLAB, (g + 1) * SLAB)
                slope, sink = slope_ref[g], sink_ref[g]
                s_c = jnp.where(mask_c, raw[g][0] * SCALE_A - slope * dist_c, NEG)
                s_p = jnp.where(mask_p, raw[g][1] * SCALE_A - slope * dist_p, NEG)
                m = jnp.maximum(jnp.maximum(jnp.max(s_c, axis=0, keepdims=True),
                                            jnp.max(s_p, axis=0, keepdims=True)), sink)
                e_c = jnp.exp(s_c - m)
                e_p = jnp.exp(s_p - m)
                den = (jnp.sum(e_c, axis=0, keepdims=True) + jnp.sum(e_p, axis=0, keepdims=True)
                       + jnp.exp(sink - m))
                inv = 1.0 / den
                ot = (_dot_tn(v_ref[rows, gs], (e_c * inv).astype(BF16))
                      + _dot_tn(v_p[before, gs], (e_p * inv).astype(BF16)))
                lse = m + jnp.log(den)
                for hh in range(GROUP_A):
                    hd = g * GROUP_A + hh
                    seg = slice(hh * BLOCK, (hh + 1) * BLOCK)
                    o_ref[rows, hd * SLAB:(hd + 1) * SLAB] = ot[:, seg].T.astype(BF16)
                    l_ref[hd, :, rows] = lse[:, seg]

    cur = lambda i: (i, 0)
    prev = lambda i: (jnp.maximum(sub * i - 1, 0), 0)
    kvw = N_KV_A * SLAB
    rows = pl.BlockSpec((N_KV_A, 1, gw), lambda i: (0, 0, 0))
    return pl.pallas_call(
        body, name="swa_fwd", grid=(t // tb,),
        in_specs=[pl.BlockSpec((tb, HM), cur),
                  pl.BlockSpec((tb, kvw), cur), pl.BlockSpec((BLOCK, kvw), prev),
                  pl.BlockSpec((tb, kvw), cur), pl.BlockSpec((BLOCK, kvw), prev),
                  pl.BlockSpec((tb, 1), cur), pl.BlockSpec((BLOCK, 1), prev),
                  pl.BlockSpec((1, tb), lambda i: (0, i)), rows, rows],
        out_specs=[pl.BlockSpec((tb, HM), cur), pl.BlockSpec((N_HEADS, 1, tb), lambda i: (0, 0, i))],
        out_shape=[jax.ShapeDtypeStruct((t, HM), BF16), jax.ShapeDtypeStruct((N_HEADS, 1, t), F32)],
        compiler_params=_params(("parallel",)),
    )(qa, ka, ka, va, va, pos_col, pos_col, pos_row, slope_rows, sink_rows)


def _swa_bwd(qa, ka, va, out_a, d_oa, lse, pos_col, pos_row, sinks):
    t = qa.shape[0]
    nb = t // BLOCK
    gw = GROUP_A * BLOCK
    kvw = N_KV_A * SLAB
    slope_rows, sink_rows = _swa_rows(sinks)

    def body(q_ref, qn_ref, do_ref, don_ref, l_ref, ln_ref, o_ref, on_ref, kp_ref, kc_ref, vp_ref, vc_ref,
             pkp_ref, pkc_ref, pq_ref, pqn_ref, slope_ref, sink_ref, dqkv_ref, dsink_ref):
        j = pl.program_id(0)
        pkc, pkp = pkc_ref[...], pkp_ref[...]
        dist_cc = _tile_group(jnp.abs(pkc - pq_ref[...]).astype(F32))
        dist_cp = _tile_group(jnp.abs(pkp - pq_ref[...]).astype(F32))
        dist_nc = _tile_group(jnp.abs(pkc - pqn_ref[...]).astype(F32))
        mask_cc, older = _swa_masks()
        mask_cp = jnp.logical_and(older, j > 0)
        mask_nc = jnp.logical_and(older, j < nb - 1)

        @pl.when(j == 0)
        def _():
            dsink_ref[...] = jnp.zeros_like(dsink_ref)

        def tile(k, v, x, dox, lrow, drow, dist, mask, slope):
            s = jnp.where(mask, _dot(k, x) * SCALE_A - slope * dist, NEG)
            p = jnp.exp(s - lrow)
            ds = p * (_dot(v, dox) - drow)
            return p.astype(BF16), ds.astype(BF16)

        for g in range(N_KV_A):
            gs = slice(g * SLAB, (g + 1) * SLAB)
            kc, kp, vc, vp = kc_ref[:, gs], kp_ref[:, gs], vc_ref[:, gs], vp_ref[:, gs]
            slope, sink = slope_ref[g], sink_ref[g]
            x, xn = _heads_beside(q_ref, g), _heads_beside(qn_ref, g)
            dox, doxn = _heads_beside(do_ref, g), _heads_beside(don_ref, g)
            lrow, lrown = _rows_beside(l_ref, g), _rows_beside(ln_ref, g)
            drow = jnp.sum(dox.astype(F32) * _heads_beside(o_ref, g).astype(F32), axis=0, keepdims=True)
            drown = jnp.sum(doxn.astype(F32) * _heads_beside(on_ref, g).astype(F32), axis=0, keepdims=True)
            p_cc, ds_cc = tile(kc, vc, x, dox, lrow, drow, dist_cc, mask_cc, slope)
            _, ds_cp = tile(kp, vp, x, dox, lrow, drow, dist_cp, mask_cp, slope)
            p_nc, ds_nc = tile(kc, vc, xn, doxn, lrown, drown, dist_nc, mask_nc, slope)
            dqt = (_dot_tn(kc, ds_cc) + _dot_tn(kp, ds_cp)) * SCALE_A
            for hh in range(GROUP_A):
                hd = g * GROUP_A + hh
                dqkv_ref[:, hd * SLAB:(hd + 1) * SLAB] = dqt[:, hh * BLOCK:(hh + 1) * BLOCK].T.astype(BF16)
            dqkv_ref[:, HM + g * SLAB:HM + (g + 1) * SLAB] = (
                (_dot_nt(ds_cc, x) + _dot_nt(ds_nc, xn)) * SCALE_A).astype(BF16)
            dqkv_ref[:, HM + kvw + g * SLAB:HM + kvw + (g + 1) * SLAB] = (
                _dot_nt(p_cc, dox) + _dot_nt(p_nc, doxn)).astype(BF16)
            dsink_ref[g] -= jnp.exp(sink - lrow) * drow

    cur = lambda j: (j, 0)
    prev = lambda j: (jnp.maximum(j - 1, 0), 0)
    nxt = lambda j: (jnp.minimum(j + 1, nb - 1), 0)
    cur3 = lambda j: (0, 0, j)
    nxt3 = lambda j: (0, 0, jnp.minimum(j + 1, nb - 1))
    kvw = N_KV_A * SLAB
    rows = pl.BlockSpec((N_KV_A, 1, gw), lambda j: (0, 0, 0))
    stat = lambda im: pl.BlockSpec((N_HEADS, 1, BLOCK), im)
    return pl.pallas_call(
        body, name="swa_bwd", grid=(nb,),
        in_specs=[pl.BlockSpec((BLOCK, HM), cur), pl.BlockSpec((BLOCK, HM), nxt),
                  pl.BlockSpec((BLOCK, HM), cur), pl.BlockSpec((BLOCK, HM), nxt),
                  stat(cur3), stat(nxt3), pl.BlockSpec((BLOCK, HM), cur), pl.BlockSpec((BLOCK, HM), nxt),
                  pl.BlockSpec((BLOCK, kvw), prev), pl.BlockSpec((BLOCK, kvw), cur),
                  pl.BlockSpec((BLOCK, kvw), prev), pl.BlockSpec((BLOCK, kvw), cur),
                  pl.BlockSpec((BLOCK, 1), prev), pl.BlockSpec((BLOCK, 1), cur),
                  pl.BlockSpec((1, BLOCK), lambda j: (0, j)),
                  pl.BlockSpec((1, BLOCK), lambda j: (0, jnp.minimum(j + 1, nb - 1))), rows, rows],
        out_specs=[pl.BlockSpec((BLOCK, HM + 2 * kvw), cur), rows],
        out_shape=[jax.ShapeDtypeStruct((t, HM + 2 * kvw), BF16), jax.ShapeDtypeStruct((N_KV_A, 1, gw), F32)],
        compiler_params=_params(("arbitrary",)),
    )(qa, qa, d_oa, d_oa, lse, lse, out_a, out_a, ka, ka, va, va,
      pos_col, pos_col, pos_row, pos_row, slope_rows, sink_rows)


def _mesh_pos():
    return lax.axis_index("x"), lax.axis_index("y"), lax.axis_index("c")


def _flip(v, bit):
    return 1 - v if bit else v


def _direct_copies(srcs, dsts, send_sems, recv_sems, local_sems, gather, sem_base=0, only=None):
    x, y, c = _mesh_pos()
    me = 4 * x + 2 * y + c

    def among(idx, dests):
        ok = idx == dests[0]
        for d in dests[1:]:
            ok = jnp.logical_or(ok, idx == d)
        return ok

    local, remote = [], []
    for a, (src, dst) in enumerate(zip(srcs, dsts)):
        dests = None if only is None else only[a]
        recv_ok = None if dests is None else among(me, dests)
        local.append((pltpu.make_async_copy(src if gather else src.at[me], dst.at[me],
                                            local_sems.at[sem_base + a]), recv_ok))
        for r in range(1, N_DEV):
            px, py, pc = _flip(x, r & 4), _flip(y, r & 2), _flip(c, r & 1)
            peer = 4 * px + 2 * py + pc
            sem = (N_DEV - 1) * (sem_base + a) + r - 1
            copy = pltpu.make_async_remote_copy(
                src_ref=src if gather else src.at[peer], dst_ref=dst.at[me],
                send_sem=send_sems.at[sem], recv_sem=recv_sems.at[sem],
                device_id=(px, py, pc), device_id_type=pl.DeviceIdType.MESH)
            remote.append((copy, None if dests is None else among(peer, dests), recv_ok))
    return local, remote


def _when(cond, fn):
    if cond is None:
        fn()
    else:
        pl.when(cond)(fn)


def _start_copies(local, remote):
    for cp, ok in local:
        _when(ok, cp.start)
    for cp, send_ok, _ in remote:
        _when(send_ok, cp.start)


def _wait_copies(local, remote):
    for cp, _, recv_ok in remote:
        _when(recv_ok, cp.wait_recv)
    for cp, send_ok, _ in remote:
        _when(send_ok, cp.wait_send)
    for cp, ok in local:
        _when(ok, cp.wait)


def _exchange_scratch(n):
    return [pltpu.SemaphoreType.DMA((n * (N_DEV - 1),)), pltpu.SemaphoreType.DMA((n * (N_DEV - 1),)),
            pltpu.SemaphoreType.DMA((n,))]


ANY_SPEC = pl.BlockSpec(memory_space=pl.ANY)


def _mla_fwd(qt, kb, vt, late):
    t = kb.shape[0]
    tk = _attn_tile(t)
    ratio = 2 if t >= 2 * tk else 1
    tq = ratio * tk
    nq = t // tq
    hps = MLA_FWD_HEADS_PER_STEP
    w = hps * SLAB
    pairs = [(i, j) for i in range(nq) for j in range(ratio * (i + 1))]
    i_tab = jnp.asarray(np.array([p[0] for p in pairs], np.int32))
    j_tab = jnp.asarray(np.array([p[1] for p in pairs], np.int32))

    n_late = len(late)

    def body(it_ref, jt_ref, qt_ref, k_ref, vt_ref, *rest):
        late_refs, (o_ref, ot_ref, qa_ref) = rest[:n_late], rest[n_late:n_late + 3]
        gathered_refs = rest[n_late + 3:2 * n_late + 3]
        m_s, acc_s, send_sems, recv_sems, local_sems = rest[2 * n_late + 3:]
        n = pl.program_id(1)
        i, j = it_ref[n], jt_ref[n]
        first_step = jnp.logical_and(pl.program_id(0) == 0, n == 0)
        last_step = jnp.logical_and(pl.program_id(0) == N_HEADS // hps - 1, n == len(pairs) - 1)

        @pl.when(first_step)
        def _():
            _start_copies(*_direct_copies(late_refs, gathered_refs, send_sems, recv_sems, local_sems, True))

        @pl.when(j == 0)
        def _():
            m_s[...] = jnp.full_like(m_s, NEG)
            acc_s[...] = jnp.zeros_like(acc_s)

        def update(masked, q0):
            qc = slice(q0, tq)

            def scores(hh):
                sl = slice(hh * SLAB, (hh + 1) * SLAB)
                return _dot(k_ref[:, sl], qt_ref[sl, qc])

            def softmax(hh, s):
                if masked:
                    s = jnp.where(lax.broadcasted_iota(jnp.int32, s.shape, 0)
                                  <= lax.broadcasted_iota(jnp.int32, s.shape, 1), s, NEG)
                m_old = m_s[hh][:, qc]
                m_new = jnp.maximum(m_old, jnp.max(s, axis=0, keepdims=True))
                m_s[hh, :, qc] = m_new
                return jnp.exp2(s - m_new).astype(BF16), jnp.exp2(m_old - m_new)

            def accumulate(hh, p, alpha):
                sl = slice(hh * SLAB, hh * SLAB + V_DIM_B + ONES_ROWS)
                acc_s[sl, qc] = alpha * acc_s[sl, qc] + _dot(vt_ref[sl, :], p)

            s_next, pending = scores(0), None
            for hh in range(hps):
                s = s_next
                if hh + 1 < hps:
                    s_next = scores(hh + 1)
                p, alpha = softmax(hh, s)
                if pending is not None:
                    accumulate(*pending)
                pending = (hh, p, alpha)
            accumulate(*pending)

        @pl.when(j < ratio * i)
        def _():
            update(False, 0)

        for part in range(ratio):
            @pl.when(j == ratio * i + part)
            def _():
                update(True, part * tk)

        @pl.when(j == ratio * i + ratio - 1)
        def _():
            for hh in range(hps):
                sl = slice(hh * SLAB, (hh + 1) * SLAB)
                den = acc_s[hh * SLAB + V_DIM_B:hh * SLAB + V_DIM_B + 1, :]
                values = lax.broadcasted_iota(jnp.int32, (SLAB, tq), 0) < V_DIM_B
                ot = jnp.where(values, acc_s[sl, :] / den, 0.0)
                ot_ref[sl, :] = ot.astype(BF16)
                o_ref[:, sl] = ot.T.astype(BF16)
                lse = m_s[hh] + jnp.log2(den)
                qa_ref[sl, :] = _plant_rows(qt_ref[sl, :].astype(F32), Q_HEAD_B, lse).astype(BF16)

        @pl.when(last_step)
        def _():
            _wait_copies(*_direct_copies(late_refs, gathered_refs, send_sems, recv_sems, local_sems, True))

    grid_spec = pltpu.PrefetchScalarGridSpec(
        num_scalar_prefetch=2, grid=(N_HEADS // hps, len(pairs)),
        in_specs=[pl.BlockSpec((w, tq), lambda h, n, it, jt: (h, it[n])),
                  pl.BlockSpec((tk, w), lambda h, n, it, jt: (jt[n], h)),
                  pl.BlockSpec((w, tk), lambda h, n, it, jt: (h, jt[n]))] + [ANY_SPEC] * n_late,
        out_specs=[pl.BlockSpec((tq, w), lambda h, n, it, jt: (it[n], h)),
                   pl.BlockSpec((w, tq), lambda h, n, it, jt: (h, it[n])),
                   pl.BlockSpec((w, tq), lambda h, n, it, jt: (h, it[n]))] + [ANY_SPEC] * n_late,
        scratch_shapes=[pltpu.VMEM((hps, 1, tq), F32), pltpu.VMEM((w, tq), F32)] + _exchange_scratch(n_late))
    outs = pl.pallas_call(
        body, name="mla_fwd", grid_spec=grid_spec,
        out_shape=[jax.ShapeDtypeStruct((t, HM), BF16), jax.ShapeDtypeStruct((HM, t), BF16),
                   jax.ShapeDtypeStruct((HM, t), BF16)]
        + [jax.ShapeDtypeStruct((N_DEV,) + a.shape, a.dtype) for a in late],
        compiler_params=_params(("arbitrary", "arbitrary")),
    )(i_tab, j_tab, qt, kb, vt, *late)
    return outs[0], outs[1], outs[2], list(outs[3:])


def _mla_bwd(qt, kb, kt, vb, d_ob_t, grad_slices):
    t = kb.shape[0]
    tk = _attn_tile(t)
    ratio = 2 if t >= 2 * tk else 1
    tq = ratio * tk
    nk, nq = t // tk, t // tq
    hps = MLA_HEADS_PER_STEP
    w = hps * SLAB
    pairs = [(j, i) for j in range(nk) for i in range(j // ratio, nq)]
    j_tab = jnp.asarray(np.array([p[0] for p in pairs], np.int32))
    i_tab = jnp.asarray(np.array([p[1] for p in pairs], np.int32))

    n_ex = len(grad_slices)

    def body(jt_ref, it_ref, qt_ref, dot_ref, k_ref, kt_ref, v_ref, *rest):
        slice_refs, (dqt_ref, dkt_ref, dvt_ref) = rest[:n_ex], rest[n_ex:n_ex + 3]
        part_refs = rest[n_ex + 3:2 * n_ex + 3]
        dk_s, dv_s, send_sems, recv_sems, local_sems = rest[2 * n_ex + 3:]
        n = pl.program_id(1)
        j, i = jt_ref[n], it_ref[n]
        first_step = jnp.logical_and(pl.program_id(0) == 0, n == 0)
        last_step = jnp.logical_and(pl.program_id(0) == N_HEADS // hps - 1, n == len(pairs) - 1)

        @pl.when(first_step)
        def _():
            _start_copies(*_direct_copies(slice_refs, part_refs, send_sems, recv_sems, local_sems, False))

        @pl.when(n == 0)
        def _():
            dqt_ref[...] = jnp.zeros_like(dqt_ref)

        def update(diagonal, q0):
            qc = slice(q0, tq)
            cols = pl.ds(pl.multiple_of(i * tq + q0, tk), tq - q0)

            def softmax_bwd(hh, s, dp):
                if diagonal:
                    s = jnp.where(lax.broadcasted_iota(jnp.int32, s.shape, 0)
                                  <= lax.broadcasted_iota(jnp.int32, s.shape, 1), s, NEG)
                p = jnp.exp2(s)
                return p.astype(BF16), (p * dp).astype(BF16)

            def gradients(hh, p, ds):
                base = hh * SLAB
                vrows = slice(base, base + V_DIM_B)
                qrows = slice(base, base + QK_NOPE + QK_ROPE)
                dv = _dot_nt(dot_ref[vrows, qc], p)
                dk = _dot_nt(qt_ref[qrows, qc], ds)
                if diagonal:
                    dv_s[base:base + SLAB, :] = jnp.concatenate([dv, jnp.zeros((SLAB - V_DIM_B, tk), F32)], axis=0)
                    dk_s[base:base + SLAB, :] = jnp.concatenate(
                        [dk, jnp.zeros((SLAB - QK_NOPE - QK_ROPE, tk), F32)], axis=0)
                else:
                    dv_s[vrows, :] += dv
                    dk_s[qrows, :] += dk
                dqt_ref[qrows, cols] += _dot(kt_ref[qrows, :], ds)

            def scores(hh):
                sl = slice(hh * SLAB, (hh + 1) * SLAB)
                return _dot(k_ref[:, sl], qt_ref[sl, qc])

            def dprod(hh):
                sl = slice(hh * SLAB, (hh + 1) * SLAB)
                return _dot(v_ref[:, sl], dot_ref[sl, qc])

            s_next = scores(0)
            for hh in range(hps):
                s = s_next
                dp = dprod(hh)
                if hh + 1 < hps:
                    s_next = scores(hh + 1)
                gradients(hh, *softmax_bwd(hh, s, dp))

        first_tile = lax.div(j, ratio)
        for part in range(ratio):
            @pl.when(jnp.logical_and(i == first_tile, lax.rem(j, ratio) == part))
            def _():
                update(True, part * tk)

        @pl.when(i > first_tile)
        def _():
            update(False, 0)

        @pl.when(i == nq - 1)
        def _():
            dkt_ref[...] = (dk_s[...] * (1.0 / LOG2E)).astype(BF16)
            dvt_ref[...] = dv_s[...].astype(BF16)

        @pl.when(last_step)
        def _():
            _wait_copies(*_direct_copies(slice_refs, part_refs, send_sems, recv_sems, local_sems, False))

    grid_spec = pltpu.PrefetchScalarGridSpec(
        num_scalar_prefetch=2, grid=(N_HEADS // hps, len(pairs)),
        in_specs=[pl.BlockSpec((w, tq), lambda h, n, jt, it: (h, it[n])),
                  pl.BlockSpec((w, tq), lambda h, n, jt, it: (h, it[n])),
                  pl.BlockSpec((tk, w), lambda h, n, jt, it: (jt[n], h)),
                  pl.BlockSpec((w, tk), lambda h, n, jt, it: (h, jt[n])),
                  pl.BlockSpec((tk, w), lambda h, n, jt, it: (jt[n], h))] + [ANY_SPEC] * n_ex,
        out_specs=[pl.BlockSpec((w, t), lambda h, n, jt, it: (h, 0)),
                   pl.BlockSpec((w, tk), lambda h, n, jt, it: (h, jt[n])),
                   pl.BlockSpec((w, tk), lambda h, n, jt, it: (h, jt[n]))] + [ANY_SPEC] * n_ex,
        scratch_shapes=[pltpu.VMEM((w, tk), F32), pltpu.VMEM((w, tk), F32)] + _exchange_scratch(n_ex))
    outs = pl.pallas_call(
        body, name="mla_bwd", grid_spec=grid_spec,
        out_shape=[jax.ShapeDtypeStruct((HM, t), F32), jax.ShapeDtypeStruct((HM, t), BF16),
                   jax.ShapeDtypeStruct((HM, t), BF16)]
        + [jax.ShapeDtypeStruct(a.shape, a.dtype) for a in grad_slices],
        compiler_params=_params(("arbitrary", "arbitrary")),
    )(j_tab, i_tab, qt, d_ob_t, kb, kt, vb, *grad_slices)
    return outs[0], outs[1], outs[2], list(outs[3:])


def _merge_fwd(out_a, out_b, gates, x, w_oa, w_ob, w_out, g2, g3):
    t = x.shape[0]
    tm = _wide_token_tile(t)

    def body(oa_ref, ob_ref, gates_ref, x_ref, woa_ref, wob_ref, wout_ref, g2_ref, g3_ref,
             oap_ref, obp_ref, merged_ref, y_ref, x1_ref, h2_ref):
        oa_p = _dot(oa_ref[...], woa_ref[...])
        ob_p = _dot(ob_ref[...], wob_ref[...])
        oap_ref[...] = oa_p.astype(BF16)
        obp_ref[...] = ob_p.astype(BF16)
        sa = _sigmoid(gates_ref[:, 0:D_MODEL].astype(F32))
        sb = _sigmoid(gates_ref[:, D_MODEL:2 * D_MODEL].astype(F32))
        merged = (sa * oa_p + sb * ob_p).astype(BF16)
        merged_ref[...] = merged
        y = _dot(merged, wout_ref[...])
        y_ref[...] = y
        x1 = x_ref[...] + y * _rms_r(y) * g2_ref[...]
        x1_ref[...] = x1
        h2_ref[...] = (x1 * _rms_r(x1) * g3_ref[...]).astype(BF16)

    def sds(dt):
        return jax.ShapeDtypeStruct((t, D_MODEL), dt)

    row = _row_spec(tm, D_MODEL)
    return pl.pallas_call(
        body, name="merge_fwd", grid=(t // tm,),
        in_specs=[_row_spec(tm, HM), _row_spec(tm, HM), _row_spec(tm, 2 * D_MODEL), row,
                  _full_spec((HM, D_MODEL)), _full_spec((HM, D_MODEL)), _full_spec((D_MODEL, D_MODEL)),
                  _full_spec((1, D_MODEL)), _full_spec((1, D_MODEL))],
        out_specs=[row] * 6,
        out_shape=[sds(BF16), sds(BF16), sds(BF16), sds(F32), sds(F32), sds(BF16)],
        compiler_params=_params(("parallel",)),
    )(out_a, out_b, gates, x, w_oa, w_ob, w_out, g2, g3)


def _merge_bwd(dx1, y, gates, oa_p, ob_p, out_a, out_b, out_b_t, merged, w_oa, w_ob, w_out, g2):
    t = dx1.shape[0]
    tm = _wide_token_tile(t)

    def body(dx1_ref, y_ref, gates_ref, oap_ref, obp_ref, oa_ref, ob_ref, obt_ref, merged_ref,
             woa_ref, wob_ref, wout_ref, g2_ref,
             dgates_ref, doa_ref, dobt_ref, dg2_ref, dwoa_ref, dwob_ref, dwout_ref):
        @pl.when(pl.program_id(0) == 0)
        def _():
            dwoa_ref[...] = jnp.zeros_like(dwoa_ref)
            dwob_ref[...] = jnp.zeros_like(dwob_ref)
            dwout_ref[...] = jnp.zeros_like(dwout_ref)

        dx1v = dx1_ref[...]
        yv = y_ref[...]
        r2 = _rms_r(yv)
        _acc_rows(dg2_ref, dx1v * yv * r2)
        dy = _rms_bwd(yv, r2, g2_ref[...], dx1v).astype(BF16)
        dwout_ref[...] += _dot_tn(merged_ref[...], dy)
        dm = _dot_nt(dy, wout_ref[...])
        sa = _sigmoid(gates_ref[:, 0:D_MODEL].astype(F32))
        sb = _sigmoid(gates_ref[:, D_MODEL:2 * D_MODEL].astype(F32))
        d_oap = (dm * sa).astype(BF16)
        d_obp = (dm * sb).astype(BF16)
        dwoa_ref[...] += _dot_tn(oa_ref[...], d_oap)
        dwob_ref[...] += _dot_tn(ob_ref[...], d_obp)
        dgates_ref[:, 0:D_MODEL] = (dm * oap_ref[...].astype(F32) * sa * (1.0 - sa)).astype(BF16)
        dgates_ref[:, D_MODEL:2 * D_MODEL] = (dm * obp_ref[...].astype(F32) * sb * (1.0 - sb)).astype(BF16)
        doa_ref[...] = _dot_nt(d_oap, woa_ref[...]).astype(BF16)
        d_ob_t = _dot_nt(wob_ref[...], d_obp)
        for hd in range(N_HEADS):
            sl = slice(hd * SLAB, (hd + 1) * SLAB)
            delta = jnp.sum(d_ob_t[sl, :] * obt_ref[sl, :].astype(F32), axis=0, keepdims=True)
            dobt_ref[sl, :] = _plant_rows(d_ob_t[sl, :], V_DIM_B, delta).astype(BF16)

    def sds(n, dt):
        return jax.ShapeDtypeStruct((t, n), dt)

    row = _row_spec(tm, D_MODEL)
    return pl.pallas_call(
        body, name="merge_bwd", grid=(t // tm,),
        in_specs=[row, row, _row_spec(tm, 2 * D_MODEL), row, row, _row_spec(tm, HM), _row_spec(tm, HM),
                  _col_spec(HM, tm), row,
                  _full_spec((HM, D_MODEL)), _full_spec((HM, D_MODEL)), _full_spec((D_MODEL, D_MODEL)),
                  _full_spec((1, D_MODEL))],
        out_specs=[_row_spec(tm, 2 * D_MODEL), _row_spec(tm, HM), _col_spec(HM, tm), _full_spec((1, D_MODEL)),
                   _full_spec((HM, D_MODEL)), _full_spec((HM, D_MODEL)), _full_spec((D_MODEL, D_MODEL))],
        out_shape=[sds(2 * D_MODEL, BF16), sds(HM, BF16), jax.ShapeDtypeStruct((HM, t), BF16),
                   jax.ShapeDtypeStruct((1, D_MODEL), F32),
                   jax.ShapeDtypeStruct((HM, D_MODEL), F32), jax.ShapeDtypeStruct((HM, D_MODEL), F32),
                   jax.ShapeDtypeStruct((D_MODEL, D_MODEL), F32)],
        compiler_params=_params(("arbitrary",), VMEM_LIMIT_MERGE_BWD),
    )(dx1, y, gates, oa_p, ob_p, out_a, out_b, out_b_t, merged, w_oa, w_ob, w_out, g2)


def _mlp_fwd_bwd(x1, h2, target, w_up, w_down, g3, g4):
    t = x1.shape[0]
    tm = _token_tile(t)
    fs = D_FF // N_DEV

    def body(x1_ref, h2_ref, tgt_ref, wup_ref, wdown_ref, g3_ref, g4_ref,
             a_ref, du_ref, dy2_ref, dx1_ref, loss_ref, dg3_ref, dg4_ref):
        x1v = x1_ref[...]
        h2v = h2_ref[...]
        u = jnp.concatenate([_dot(h2v, wup_ref[s]) for s in range(N_DEV)], axis=1)
        ru = jnp.maximum(u, 0.0)
        a = (ru * ru).astype(BF16)
        a_ref[...] = a
        y2 = _dot(a, wdown_ref[...])
        r4 = _rms_r(y2)
        diff = x1v + y2 * r4 * g4_ref[...] - tgt_ref[...]
        _acc_rows(loss_ref, jnp.sum(diff * diff, axis=-1, keepdims=True) * (0.5 / D_MODEL)
                  * jnp.ones((1, SLAB), F32))
        dx2 = diff * (1.0 / D_MODEL)
        _acc_rows(dg4_ref, dx2 * y2 * r4)
        dy2 = _rms_bwd(y2, r4, g4_ref[...], dx2).astype(BF16)
        dy2_ref[...] = dy2
        du = (_dot_nt(dy2, wdown_ref[...]) * (2.0 * ru)).astype(BF16)
        du_ref[...] = du
        dh2 = _dot_nt(du[:, 0:fs], wup_ref[0])
        for s in range(1, N_DEV):
            dh2 += _dot_nt(du[:, s * fs:(s + 1) * fs], wup_ref[s])
        r3 = _rms_r(x1v)
        _acc_rows(dg3_ref, dh2 * x1v * r3)
        dx1_ref[...] = dx2 + _rms_bwd(x1v, r3, g3_ref[...], dh2)

    row = _row_spec(tm, D_MODEL)
    frow = _row_spec(tm, D_FF)
    vec = _full_spec((1, D_MODEL))
    return pl.pallas_call(
        body, name="mlp_fwd_bwd", grid=(t // tm,),
        in_specs=[row, row, row, _full_spec((N_DEV, D_MODEL, fs)), _full_spec((D_FF, D_MODEL)), vec, vec],
        out_specs=[frow, frow, row, row, _full_spec((1, SLAB)), vec, vec],
        out_shape=[jax.ShapeDtypeStruct((t, D_FF), BF16), jax.ShapeDtypeStruct((t, D_FF), BF16),
                   jax.ShapeDtypeStruct((t, D_MODEL), BF16), jax.ShapeDtypeStruct((t, D_MODEL), F32),
                   jax.ShapeDtypeStruct((1, SLAB), F32), jax.ShapeDtypeStruct((1, D_MODEL), F32),
                   jax.ShapeDtypeStruct((1, D_MODEL), F32)],
        compiler_params=_params(("arbitrary",)),
    )(x1, h2, target, w_up, w_down, g3, g4)


def _latent_bwd(dqb_t, dkb_t, dvb_t, cq, ckv, cqn, ckvn, rope_ct, rope_s1t, rope_s2t, g_q, g_kv, w_qb, w_kvb):
    t = cq.shape[0]
    tm = min(t, 2 * _wide_token_tile(t))

    def body(dqt_ref, dkt_ref, dvt_ref, cq_ref, ckv_ref, cqn_ref, ckvn_ref, ct_ref, s1t_ref, s2t_ref,
             gq_ref, gkv_ref, wqb_ref, wkvb_ref,
             dlate_ref, dgq_ref, dgkv_ref, dwqb_ref, dwkvb_ref, dqbrt_ref, dkvbt_ref):
        @pl.when(pl.program_id(0) == 0)
        def _():
            dwqb_ref[...] = jnp.zeros_like(dwqb_ref)
            dwkvb_ref[...] = jnp.zeros_like(dwkvb_ref)

        ct, s1t, s2t = ct_ref[...], s1t_ref[...], s2t_ref[...]
        dk_sum_t = jnp.zeros((SLAB, tm), F32)
        for hd in range(N_HEADS):
            sl = slice(hd * SLAB, (hd + 1) * SLAB)
            dqbrt_ref[sl, :] = _rope_t_bwd(dqt_ref[sl, :] * SCALE_B, ct, s1t, s2t).astype(BF16)
            dk_sum_t += dkt_ref[sl, :].astype(F32)
        dkvbt_ref[0:HM, :] = dkt_ref[...]
        dkvbt_ref[HM:2 * HM, :] = dvt_ref[...]
        dkr = _rope_t_bwd(dk_sum_t, ct, s1t, s2t).T
        dwqb_ref[...] += _dot(dqbrt_ref[...], cqn_ref[...])
        dwkvb_ref[...] += _dot(dkvbt_ref[...], ckvn_ref[...])
        dcqn = _dot(wqb_ref[...], dqbrt_ref[...]).T
        cq = cq_ref[...]
        rq = _rms_r(cq)
        _acc_rows(dgq_ref, dcqn * cq * rq)
        dcq = _rms_bwd(cq, rq, gq_ref[...], dcqn)
        dckvn = _dot(wkvb_ref[...], dkvbt_ref[...]).T
        ckv = ckv_ref[...]
        rkv = _rms_r(ckv)
        _acc_rows(dgkv_ref, dckvn * ckv * rkv)
        dckv = _rms_bwd(ckv, rkv, gkv_ref[...], dckvn)
        dlate_ref[:, 0:C_CKV - C_CQ] = dcq.astype(BF16)
        dlate_ref[:, C_CKV - C_CQ:C_KR - C_CQ] = dckv.astype(BF16)
        dlate_ref[:, C_KR - C_CQ:D_IN_PAD - C_CQ] = dkr.astype(BF16)

    hmt = _col_spec(HM, tm)
    tab = _col_spec(SLAB, tm)
    return pl.pallas_call(
        body, name="latent_bwd", grid=(t // tm,),
        in_specs=[hmt, hmt, hmt,
                  _row_spec(tm, Q_LORA), _row_spec(tm, KV_LORA), _row_spec(tm, Q_LORA), _row_spec(tm, KV_LORA),
                  tab, tab, tab, _full_spec((1, Q_LORA)), _full_spec((1, KV_LORA)),
                  _full_spec((Q_LORA, HM)), _full_spec((KV_LORA, 2 * HM))],
        out_specs=[_row_spec(tm, D_IN_PAD - C_CQ), _full_spec((1, Q_LORA)), _full_spec((1, KV_LORA)),
                   _full_spec((HM, Q_LORA)), _full_spec((2 * HM, KV_LORA))],
        out_shape=[jax.ShapeDtypeStruct((t, D_IN_PAD - C_CQ), BF16),
                   jax.ShapeDtypeStruct((1, Q_LORA), F32), jax.ShapeDtypeStruct((1, KV_LORA), F32),
                   jax.ShapeDtypeStruct((HM, Q_LORA), F32), jax.ShapeDtypeStruct((2 * HM, KV_LORA), F32)],
        scratch_shapes=[pltpu.VMEM((HM, tm), BF16), pltpu.VMEM((2 * HM, tm), BF16)],
        compiler_params=_params(("arbitrary",)),
    )(dqb_t, dkb_t, dvb_t, cq, ckv, cqn, ckvn, rope_ct, rope_s1t, rope_s2t, g_q, g_kv, w_qb, w_kvb)


def _inproj_bwd(dgates, dqkv, dlate, x, dx1, g1, w_in, grad_slices, only, small):
    t = x.shape[0]
    tm = _wide_token_tile(t)
    n_ex = len(grad_slices)
    zeroed = [a for a in range(n_ex) if only[a] is not None]

    def body(dgates_ref, dqkv_ref, dlate_ref, x_ref, dx1_ref, g1_ref, win_ref, *rest):
        slice_refs, small_ref = rest[:n_ex], rest[n_ex]
        dx_ref = rest[n_ex + 1]
        part_refs = rest[n_ex + 2:2 * n_ex + 2]
        small_dst, dg1_dst = rest[2 * n_ex + 2:2 * n_ex + 4]
        dproj_ref, dg1_ref, send_sems, recv_sems, local_sems = rest[2 * n_ex + 4:2 * n_ex + 9]
        zero_refs, zero_sem = rest[2 * n_ex + 9:-1], rest[-1]
        sems = (send_sems, recv_sems, local_sems)

        def slice_copies():
            return _direct_copies(slice_refs, part_refs, *sems, False, only=only)

        def small_copies():
            return _direct_copies([small_ref], [small_dst], *sems, True, sem_base=n_ex)

        @pl.when(pl.program_id(0) == 0)
        def _():
            _start_copies(*slice_copies())
            _start_copies(*small_copies())
            x_, y_, c_ = _mesh_pos()
            me = 4 * x_ + 2 * y_ + c_
            for a, z_ref in zip(zeroed, zero_refs):
                outside = me != only[a][0]
                for d in only[a][1:]:
                    outside = jnp.logical_and(outside, me != d)

                @pl.when(outside)
                def _():
                    z_ref[...] = jnp.zeros_like(z_ref)
                    fills = [pltpu.make_async_copy(z_ref, part_refs[a].at[k], zero_sem.at[k])
                             for k in range(N_DEV)]
                    for cp in fills:
                        cp.start()
                    for cp in fills:
                        cp.wait()

        dproj_ref[:, C_GATES:C_QA] = dgates_ref[...]
        dproj_ref[:, C_QA:C_CQ] = dqkv_ref[...]
        dproj_ref[:, C_CQ:D_IN_PAD] = dlate_ref[...]
        dh = _dot_nt(dproj_ref[...], win_ref[...])
        xv = x_ref[...]
        r1 = _rms_r(xv)
        _acc_rows(dg1_ref, dh * xv * r1)
        dx_ref[...] = dx1_ref[...] + _rms_bwd(xv, r1, g1_ref[...], dh)

        @pl.when(pl.program_id(0) == t // tm - 1)
        def _():
            gain_copies = _direct_copies([dg1_ref], [dg1_dst], *sems, True, sem_base=n_ex + 1)
            _start_copies(*gain_copies)
            _wait_copies(*slice_copies())
            _wait_copies(*small_copies())
            _wait_copies(*gain_copies)

    kvw = N_KV_A * SLAB
    row = _row_spec(tm, D_MODEL)
    outs = pl.pallas_call(
        body, name="inproj_bwd", grid=(t // tm,),
        in_specs=[_row_spec(tm, 2 * D_MODEL), _row_spec(tm, HM + 2 * kvw), _row_spec(tm, D_IN_PAD - C_CQ),
                  row, row, _full_spec((1, D_MODEL)), _full_spec((D_MODEL, D_IN_PAD))]
        + [ANY_SPEC] * (n_ex + 1),
        out_specs=[row] + [ANY_SPEC] * (n_ex + 2),
        out_shape=[jax.ShapeDtypeStruct((t, D_MODEL), F32)]
        + [jax.ShapeDtypeStruct(a.shape, a.dtype) for a in grad_slices]
        + [jax.ShapeDtypeStruct((N_DEV,) + small.shape, F32), jax.ShapeDtypeStruct((N_DEV, 1, D_MODEL), F32)],
        scratch_shapes=[pltpu.VMEM((tm, D_IN_PAD), BF16), pltpu.VMEM((1, D_MODEL), F32)]
        + _exchange_scratch(n_ex + 2)
        + [pltpu.VMEM(grad_slices[a].shape[1:], grad_slices[a].dtype) for a in zeroed]
        + [pltpu.SemaphoreType.DMA((N_DEV,))],
        compiler_params=_params(("arbitrary",)),
    )(dgates, dqkv, dlate, x, dx1, g1, w_in, *grad_slices, small)
    return outs[0], list(outs[1:n_ex + 1]), outs[n_ex + 1], outs[n_ex + 2]


def _matmul_tn(a, b, name, out_dtype=F32, n_shards=1):
    t, k = a.shape
    n = b.shape[1]
    bn = min(n, MATMUL_COLS)
    bt = min(t, MATMUL_TOKENS)
    bk = min(k, MATMUL_ACC_ELEMS // bn)
    ns = n // n_shards
    per_block = bn // ns
    steps = t // bt

    def body(a_ref, b_ref, o_ref, acc):
        s = pl.program_id(2)

        @pl.when(s == 0)
        def _():
            acc[...] = jnp.zeros_like(acc)

        acc[...] += _dot_tn(a_ref[...], b_ref[...])

        @pl.when(s == steps - 1)
        def _():
            if n_shards > 1:
                for p in range(per_block):
                    o_ref[p] = acc[:, p * ns:(p + 1) * ns].astype(out_dtype)
            else:
                o_ref[...] = acc[...].astype(out_dtype)

    if n_shards > 1:
        out_spec = pl.BlockSpec((per_block, bk, ns), lambda i, j, s: (j, i, 0))
        out_shape = jax.ShapeDtypeStruct((n_shards, k, ns), out_dtype)
    else:
        out_spec = pl.BlockSpec((bk, bn), lambda i, j, s: (i, j))
        out_shape = jax.ShapeDtypeStruct((k, n), out_dtype)
    return pl.pallas_call(
        body, name=name, grid=(k // bk, n // bn, steps),
        in_specs=[pl.BlockSpec((bt, bk), lambda i, j, s: (s, i)), pl.BlockSpec((bt, bn), lambda i, j, s: (s, j))],
        out_specs=out_spec, out_shape=out_shape, scratch_shapes=[pltpu.VMEM((bk, bn), F32)],
        compiler_params=_params(("parallel", "parallel", "arbitrary")),
    )(a, b)


def _two_level_gather(srcs, dsts, send_sems, recv_sems, local_sems):
    n = len(srcs)
    x, y, c = _mesh_pos()
    me, sibling = (x, y, c), (x, y, 1 - c)
    chips = [(1 - x, y), (x, 1 - y), (1 - x, 1 - y)]

    def slot(a, px, py, pc):
        return dsts[a].at[4 * px + 2 * py + pc]

    def copy(a, k, block, to, src=None):
        return pltpu.make_async_remote_copy(
            src_ref=slot(a, *block) if src is None else src, dst_ref=slot(a, *block),
            send_sem=send_sems.at[(N_DEV - 1) * a + k], recv_sem=recv_sems.at[(N_DEV - 1) * a + k],
            device_id=to, device_id_type=pl.DeviceIdType.MESH)

    def own_copies():
        mine = [pltpu.make_async_copy(srcs[a], slot(a, *me), local_sems.at[a]) for a in range(n)]
        first = []
        for a in range(n):
            first.append(copy(a, 0, me, sibling, src=srcs[a]))
            first += [copy(a, 1 + j, me, (*chip, c), src=srcs[a]) for j, chip in enumerate(chips)]
        return mine, first

    def start():
        mine, first = own_copies()
        for cp in mine + first:
            cp.start()

    def finish():
        mine, first = own_copies()
        passed = []
        for j, chip in enumerate(chips):
            for a in range(n):
                copy(a, 1 + j, (*chip, c), me).wait_recv()
                passed.append(copy(a, 4 + j, (*chip, c), sibling))
                passed[-1].start()
        for a in range(n):
            copy(a, 0, sibling, me).wait_recv()
        for j, chip in enumerate(chips):
            for a in range(n):
                copy(a, 4 + j, (*chip, 1 - c), me).wait_recv()
        for cp in first + passed:
            cp.wait_send()
        for cp in mine:
            cp.wait()

    return start, finish


def _adamw(parts, w, m, v, name):
    n_parts = len(parts)
    _, k, n = parts[0].shape
    bk = min(k, ADAM_ROWS)
    c1 = 1.0 - ADAM_B1 ** ADAM_STEP
    c2 = 1.0 - ADAM_B2 ** ADAM_STEP

    def body(*refs):
        p_refs, (w_ref, m_ref, v_ref, g_ref, d_ref, mo_ref, vo_ref) = refs[:n_parts], refs[n_parts:]
        g = p_refs[0][0].astype(F32)
        for p_ref in p_refs:
            for s in range(N_DEV):
                if p_ref is not p_refs[0] or s > 0:
                    g = g + p_ref[s].astype(F32)
        g_ref[0] = g
        m_new = ADAM_B1 * m_ref[0] + (1.0 - ADAM_B1) * g
        v_new = ADAM_B2 * v_ref[0] + (1.0 - ADAM_B2) * (g * g)
        mo_ref[0] = m_new
        vo_ref[0] = v_new
        m_hat = m_new / c1
        v_hat = v_new / c2
        d_ref[0] = -ADAM_LR * (m_hat / (jnp.sqrt(v_hat) + ADAM_EPS) + ADAM_WD * w_ref[0])

    blk = pl.BlockSpec((1, bk, n), lambda i: (0, i, 0))
    out = jax.ShapeDtypeStruct((1, k, n), F32)
    return pl.pallas_call(
        body, name=name, grid=(k // bk,),
        in_specs=[pl.BlockSpec((N_DEV, bk, n), lambda i: (0, i, 0))] * n_parts + [blk, blk, blk],
        out_specs=[blk] * 4, out_shape=[out] * 4,
        compiler_params=_params(("parallel",)),
    )(*parts, w, m, v)


def _adamw_small(parts, w, m, v):
    k = len(SMALL_LAYOUT)
    c1 = 1.0 - ADAM_B1 ** ADAM_STEP
    c2 = 1.0 - ADAM_B2 ** ADAM_STEP

    def body(p_ref, *refs):
        w_refs, m_refs, v_refs, outs = refs[:k], refs[k:2 * k], refs[2 * k:3 * k], refs[3 * k:]
        total = p_ref[0]
        for s in range(1, N_DEV):
            total = total + p_ref[s]
        for i, (_, row, off, width) in enumerate(SMALL_LAYOUT):
            g = total[row:row + 1, off:off + width]
            m_new = ADAM_B1 * m_refs[i][...] + (1.0 - ADAM_B1) * g
            v_new = ADAM_B2 * v_refs[i][...] + (1.0 - ADAM_B2) * (g * g)
            outs[4 * i][...] = g
            outs[4 * i + 1][...] = -ADAM_LR * ((m_new / c1) / (jnp.sqrt(v_new / c2) + ADAM_EPS)
                                               + ADAM_WD * w_refs[i][...])
            outs[4 * i + 2][...] = m_new
            outs[4 * i + 3][...] = v_new
        outs[4 * k][...] = total[SMALL_LOSS_ROW:SMALL_LOSS_ROW + 1, SMALL_LOSS_OFF:SMALL_LOSS_OFF + 1]

    names = [name for name, *_ in SMALL_LAYOUT]
    out_shape = [jax.ShapeDtypeStruct(w[name].shape, F32) for name in names for _ in range(4)]
    outs = pl.pallas_call(
        body, name="adamw_small", out_shape=out_shape + [jax.ShapeDtypeStruct((1, 1), F32)],
    )(parts, *[w[n] for n in names], *[m[n] for n in names], *[v[n] for n in names])
    return {name: tuple(outs[4 * i:4 * i + 4]) for i, name in enumerate(names)}, outs[4 * k]


def _pad_heads_cols(w, heads, width):
    k = w.shape[0]
    w = w.reshape(k, heads, width)
    return jnp.pad(w, ((0, 0), (0, 0), (0, SLAB - width))).reshape(k, heads * SLAB)


def _unpad_heads_cols(w, heads, width):
    k = w.shape[0]
    return w.reshape(k, heads, SLAB)[:, :, :width].reshape(k, heads * width)


def _pad_heads_rows(w, heads, width):
    n = w.shape[1]
    w = w.reshape(heads, width, n)
    return jnp.pad(w, ((0, 0), (0, SLAB - width), (0, 0))).reshape(heads * SLAB, n)


def _unpad_heads_rows(w, heads, width):
    n = w.shape[1]
    return w.reshape(heads, SLAB, n)[:, :width, :].reshape(heads * width, n)


def _pad_w_in(w_in):
    o = 2 * D_MODEL
    qa = _pad_heads_cols(w_in[:, o:o + 512], N_HEADS, HEAD_A)
    ka = _pad_heads_cols(w_in[:, o + 512:o + 640], N_KV_A, HEAD_A)
    va = _pad_heads_cols(w_in[:, o + 640:o + 768], N_KV_A, HEAD_A)
    kr = jnp.pad(w_in[:, o + 1152:o + 1184], ((0, 0), (QK_NOPE, SLAB - QK_NOPE - QK_ROPE)))
    return jnp.concatenate([w_in[:, :o], qa, ka, va, w_in[:, o + 768:o + 1152], kr], axis=1)


def _unpad_w_in(w):
    qa = _unpad_heads_cols(w[:, C_QA:C_KA], N_HEADS, HEAD_A)
    ka = _unpad_heads_cols(w[:, C_KA:C_VA], N_KV_A, HEAD_A)
    va = _unpad_heads_cols(w[:, C_VA:C_CQ], N_KV_A, HEAD_A)
    kr = w[:, C_KR + QK_NOPE:C_KR + QK_NOPE + QK_ROPE]
    return jnp.concatenate([w[:, :C_QA], qa, ka, va, w[:, C_CQ:C_KR], kr], axis=1)


def _pad_w_kvb(w_kvb):
    w = w_kvb.reshape(KV_LORA, N_HEADS, QK_NOPE + V_DIM_B)
    k = jnp.pad(w[:, :, :QK_NOPE], ((0, 0), (0, 0), (0, SLAB - QK_NOPE))).reshape(KV_LORA, HM)
    v = jnp.pad(w[:, :, QK_NOPE:], ((0, 0), (0, 0), (0, SLAB - V_DIM_B))).reshape(KV_LORA, HM)
    return jnp.concatenate([k, v], axis=1)


def _unpad_w_kvb(w):
    k = w[:, :HM].reshape(KV_LORA, N_HEADS, SLAB)[:, :, :QK_NOPE]
    v = w[:, HM:].reshape(KV_LORA, N_HEADS, SLAB)[:, :, :V_DIM_B]
    return jnp.concatenate([k, v], axis=2).reshape(KV_LORA, N_HEADS * (QK_NOPE + V_DIM_B))


def _col_shards(w):
    k, n = w.shape
    ns = n // N_DEV
    if ns % SLAB:
        return jnp.stack([w[:, d * ns:(d + 1) * ns] for d in range(N_DEV)])
    return w.reshape(k, N_DEV, ns).transpose(1, 0, 2)


def _from_col_shards(s):
    _, k, ns = s.shape
    if ns % SLAB:
        return jnp.concatenate([s[d] for d in range(N_DEV)], axis=1)
    return s.transpose(1, 0, 2).reshape(k, N_DEV * ns)


def _freq_row():
    freqs = ROPE_THETA ** (-jnp.arange(0, QK_ROPE, 2, dtype=F32) / QK_ROPE)
    return jnp.concatenate([jnp.zeros((QK_NOPE,), F32), freqs, freqs,
                            jnp.zeros((SLAB - QK_NOPE - QK_ROPE,), F32)]).reshape(1, SLAB)


SMALL_D_ROWS = ("pre_norm_mix", "post_norm_mix", "pre_norm_mlp", "post_norm_mlp")
SMALL_LAYOUT = tuple((name, i, 0, D_MODEL) for i, name in enumerate(SMALL_D_ROWS)) + (
    ("q_a_norm", 4, 0, Q_LORA), ("kv_a_norm", 4, 256, KV_LORA), ("sinks", 4, 384, N_HEADS))
SMALL_LOSS_ROW, SMALL_LOSS_OFF = 4, 512


def _pack_small(vals):
    row4 = jnp.concatenate([vals["q_a_norm"].reshape(-1), vals["kv_a_norm"].reshape(-1), vals["sinks"].reshape(-1),
                            jnp.zeros((SMALL_LOSS_OFF - 392,), F32), vals["loss"].reshape(-1),
                            jnp.zeros((1024 - SMALL_LOSS_OFF - 1,), F32)])
    rows = [vals[n].reshape(1024) for n in SMALL_D_ROWS] + [row4]
    return jnp.concatenate([jnp.stack(rows), jnp.zeros((SMALL_ROWS - 5, 1024), F32)], axis=0)


WEIGHT_ORDER = ("pre_norm_mix", "w_in", "q_a_norm", "w_q_b", "kv_a_norm", "w_kv_b", "sinks", "w_o_a", "w_o_b",
                "w_out", "post_norm_mix", "pre_norm_mlp", "w_up", "w_down", "post_norm_mlp")


def kernel(x, positions, pre_norm_mix, w_in, q_a_norm, w_q_b, kv_a_norm, w_kv_b, sinks, w_o_a, w_o_b, w_out, post_norm_mix, pre_norm_mlp, w_up, w_down, post_norm_mlp, loss_target, m_pre_norm_mix, m_w_in, m_q_a_norm, m_w_q_b, m_kv_a_norm, m_w_kv_b, m_sinks, m_w_o_a, m_w_o_b, m_w_out, m_post_norm_mix, m_pre_norm_mlp, m_w_up, m_w_down, m_post_norm_mlp, v_pre_norm_mix, v_w_in, v_q_a_norm, v_w_q_b, v_kv_a_norm, v_w_kv_b, v_sinks, v_w_o_a, v_w_o_b, v_w_out, v_post_norm_mix, v_pre_norm_mlp, v_w_up, v_w_down, v_post_norm_mlp):
    weights = dict(pre_norm_mix=pre_norm_mix, w_in=w_in, q_a_norm=q_a_norm, w_q_b=w_q_b, kv_a_norm=kv_a_norm,
                   w_kv_b=w_kv_b, sinks=sinks, w_o_a=w_o_a, w_o_b=w_o_b, w_out=w_out, post_norm_mix=post_norm_mix,
                   pre_norm_mlp=pre_norm_mlp, w_up=w_up, w_down=w_down, post_norm_mlp=post_norm_mlp)
    m_in = dict(pre_norm_mix=m_pre_norm_mix, w_in=m_w_in, q_a_norm=m_q_a_norm, w_q_b=m_w_q_b, kv_a_norm=m_kv_a_norm,
                w_kv_b=m_w_kv_b, sinks=m_sinks, w_o_a=m_w_o_a, w_o_b=m_w_o_b, w_out=m_w_out,
                post_norm_mix=m_post_norm_mix, pre_norm_mlp=m_pre_norm_mlp, w_up=m_w_up, w_down=m_w_down,
                post_norm_mlp=m_post_norm_mlp)
    v_in = dict(pre_norm_mix=v_pre_norm_mix, w_in=v_w_in, q_a_norm=v_q_a_norm, w_q_b=v_w_q_b, kv_a_norm=v_kv_a_norm,
                w_kv_b=v_w_kv_b, sinks=v_sinks, w_o_a=v_w_o_a, w_o_b=v_w_o_b, w_out=v_w_out,
                post_norm_mix=v_post_norm_mix, pre_norm_mlp=v_pre_norm_mlp, w_up=v_w_up, w_down=v_w_down,
                post_norm_mlp=v_post_norm_mlp)

    xs, pos, target = x[0], positions[0], loss_target[0]
    t = xs.shape[0]
    pos_col = pos.reshape(t, 1)
    pos_row = pos.reshape(1, t)
    g1, g2, g3, g4 = (weights[n] for n in SMALL_D_ROWS)
    g_q, g_kv = q_a_norm, kv_a_norm
    sink_vec = sinks.reshape(N_HEADS)
    shard = {n: weights[n][0].astype(BF16) for n in EARLY + LATE}

    tables, (e_in, e_qb, e_kvb) = _rope_tables(pos_col, _freq_row(), [shard[n] for n in EARLY])
    w_in_p = _pad_w_in(_from_col_shards(e_in))
    w_qb = _pad_heads_cols(_from_col_shards(e_qb), N_HEADS, QK_NOPE + QK_ROPE)
    w_kvb = _pad_w_kvb(_from_col_shards(e_kvb))

    (h, gates, qa, ka, va, cq, ckv, cqn, ckvn, kb, vb, qt, kt, vt) = _inproj_fwd(
        xs, g1, w_in_p, g_q, g_kv, w_kvb, w_qb.T, w_kvb[:, :HM].T, w_kvb[:, HM:].T, w_in_p[:, C_KR:].T, tables)
    out_a, lse_a = _swa_fwd(qa, ka, va, pos_col, pos_row, sink_vec)
    out_b, out_b_t, qt_lse, (l_oa, l_ob, l_out, w_up_s, l_down) = _mla_fwd(qt, kb, vt, [shard[n] for n in LATE])
    w_oa = _pad_heads_rows(_from_col_shards(l_oa), N_HEADS, HEAD_A)
    w_ob = _pad_heads_rows(_from_col_shards(l_ob), N_HEADS, V_DIM_B)
    w_out_f = l_out.reshape(D_MODEL, D_MODEL)
    w_down_f = l_down.reshape(D_FF, D_MODEL)

    oa_p, ob_p, merged, y, x1, h2 = _merge_fwd(out_a, out_b, gates, xs, w_oa, w_ob, w_out_f, g2, g3)
    a, du, dy2, dx1, loss, dg3, dg4 = _mlp_fwd_bwd(x1, h2, target, w_up_s, w_down_f, g3, g4)
    (dgates, d_oa, d_ob_t, dg2, dw_oa, dw_ob, dw_out) = _merge_bwd(
        dx1, y, gates, oa_p, ob_p, out_a, out_b, out_b_t, merged, w_oa, w_ob, w_out_f, g2)
    late_slices = [
        _col_shards(_unpad_heads_rows(dw_oa, N_HEADS, HEAD_A)).astype(BF16),
        _col_shards(_unpad_heads_rows(dw_ob, N_HEADS, V_DIM_B)).astype(BF16),
        dw_out.astype(BF16).reshape(N_DEV, D_MODEL // N_DEV, D_MODEL),
        _matmul_tn(h2, du, "dw_up", BF16, N_DEV),
        _matmul_tn(a, dy2, "dw_down", BF16).reshape(N_DEV, D_FF // N_DEV, D_MODEL),
    ]
    dqkv_a, dsink = _swa_bwd(qa, ka, va, out_a, d_oa, lse_a, pos_col, pos_row, sink_vec)
    dw_in_early = jnp.concatenate([_matmul_tn(h, dgates, "dw_in_gates", BF16),
                                   _matmul_tn(h, dqkv_a, "dw_in_mixer_a", BF16),
                                   jnp.zeros((D_MODEL, D_IN_PAD - C_CQ), BF16)], axis=1)
    late_slices.append(_col_shards(_unpad_w_in(dw_in_early)))
    dqb_t, dkb_t, dvb_t, late_parts = _mla_bwd(qt_lse, kb, kt, vb, d_ob_t, late_slices)
    w_in_early_parts = late_parts.pop()
    dproj_late, dgq, dgkv, dw_qb_t, dw_kvb_t = _latent_bwd(
        dqb_t, dkb_t, dvb_t, cq, ckv, cqn, ckvn, *tables[3:], g_q, g_kv, w_qb, w_kvb)
    dw_l = _matmul_tn(h, dproj_late, "dw_in_latents", BF16)
    late_cols = jnp.concatenate([dw_l[:, :Q_LORA + KV_LORA], dw_l[:, C_KR - C_CQ + QK_NOPE:C_KR - C_CQ + Q_HEAD_B]],
                                axis=1)
    shard_cols = w_in.shape[2]
    head = late_cols.shape[1] - shard_cols
    w_in_late = jnp.concatenate([
        jnp.zeros((N_DEV - 2, D_MODEL, shard_cols), BF16),
        jnp.pad(late_cols[:, :head], ((0, 0), (shard_cols - head, 0)))[None], late_cols[:, head:][None]])
    early_slices = [
        w_in_late,
        _col_shards(_unpad_heads_cols(dw_qb_t.T, N_HEADS, QK_NOPE + QK_ROPE)).astype(BF16),
        _col_shards(_unpad_w_kvb(dw_kvb_t.T)).astype(BF16),
    ]
    small_grads = {"pre_norm_mix": jnp.zeros((1, D_MODEL), F32), "post_norm_mix": dg2, "pre_norm_mlp": dg3,
                   "post_norm_mlp": dg4, "q_a_norm": dgq, "kv_a_norm": dgkv,
                   "sinks": dsink.reshape(N_HEADS, BLOCK).sum(axis=1), "loss": loss[0, 0:1]}
    dx, early_parts, s_parts, dg1_parts = _inproj_bwd(
        dgates, dqkv_a, dproj_late, xs, dx1, g1, w_in_p, early_slices,
        only=[(N_DEV - 2, N_DEV - 1), None, None], small=_pack_small(small_grads))
    s_parts = s_parts.at[:, SMALL_D_ROWS.index("pre_norm_mix"), :].set(dg1_parts[:, 0, :])

    updates = {}
    all_parts = [[w_in_early_parts, early_parts[0]]] + [[p] for p in early_parts[1:] + late_parts]
    for name, parts in zip(EARLY + LATE, all_parts):
        outs = _adamw(parts, weights[name], m_in[name], v_in[name], "adamw_" + name)
        for kind, arr in zip(("g", "d", "m", "v"), outs):
            updates[kind, name] = arr
    small_out, loss_sum = _adamw_small(s_parts, weights, m_in, v_in)
    for name, outs in small_out.items():
        for kind, arr in zip(("g", "d", "m", "v"), outs):
            updates[kind, name] = arr
    results = [updates[kind, name] for kind in ("g", "d", "m", "v") for name in WEIGHT_ORDER]
    return (loss_sum.reshape(()), dx[None], *results)
```

```python
import functools

import numpy as np
import jax
import jax.numpy as jnp
from jax import lax
from jax.experimental import pallas as pl
from jax.experimental.pallas import tpu as pltpu

F32 = jnp.float32
BF16 = jnp.bfloat16

D_MODEL = 1024
D_FF = 4096
N_HEADS = 8
N_KV_A = 2
GROUP_A = N_HEADS // N_KV_A
HEAD_A = 64
QK_NOPE = 64
QK_ROPE = 32
V_DIM_B = 64
Q_LORA = 256
KV_LORA = 128
BLOCK = 128
SLAB = 128
ROPE_THETA = 10000.0
EPS = 1e-6
N_DEV = 8
NEG = -1e30

SCALE_A = HEAD_A ** -0.5
SCALE_B = (QK_NOPE + QK_ROPE) ** -0.5
LOG2E = 1.4426950408889634
SCORE_B = SCALE_B * LOG2E
MLA_HEADS_PER_STEP = 4
MLA_FWD_HEADS_PER_STEP = 8
Q_HEAD_B = QK_NOPE + QK_ROPE
SWA_BLOCKS_PER_STEP = 4
ONES_ROWS = 16
SLOPES_A = tuple(2.0 ** (-8.0 * (h + 1) / N_HEADS) for h in range(N_HEADS))

ADAM_LR = 0.001
ADAM_B1 = 0.9
ADAM_B2 = 0.999
ADAM_EPS = 1e-08
ADAM_WD = 0.01
ADAM_STEP = 10

HM = N_HEADS * SLAB
C_GATES = 0
C_QA = 2 * D_MODEL
C_KA = C_QA + HM
C_VA = C_KA + N_KV_A * SLAB
C_CQ = C_VA + N_KV_A * SLAB
C_CKV = C_CQ + Q_LORA
C_KR = C_CKV + KV_LORA
D_IN_PAD = C_KR + SLAB

VMEM_LIMIT = 56 * 1024 * 1024
VMEM_LIMIT_MERGE_BWD = 60 * 1024 * 1024

EARLY = ("w_in", "w_q_b", "w_kv_b")
LATE = ("w_o_a", "w_o_b", "w_out", "w_up", "w_down")
ADAM_ROWS = 256
MATMUL_COLS = 2048
MATMUL_TOKENS = 2048
MATMUL_ACC_ELEMS = 2048 * 1024
SMALL_ROWS = 8


def _token_tile(t):
    return min(256, t)


def _wide_token_tile(t):
    return min(512, t)


def _attn_tile(t):
    return 512 if t >= 2048 else 128


def _params(sem, vmem=VMEM_LIMIT):
    return pltpu.CompilerParams(dimension_semantics=sem, vmem_limit_bytes=vmem)


def _dot(a, b):
    return jnp.dot(a, b, preferred_element_type=F32)


def _dot_nt(a, b):
    return lax.dot_general(a, b, (((1,), (1,)), ((), ())), preferred_element_type=F32)


def _dot_tn(a, b):
    return lax.dot_general(a, b, (((0,), (0,)), ((), ())), preferred_element_type=F32)


def _rms_r(x):
    return lax.rsqrt(jnp.mean(x * x, axis=-1, keepdims=True) + EPS)


def _rms_bwd(x, r, g, dy):
    t = dy * g
    return r * t - x * (r * r * r) * jnp.mean(x * t, axis=-1, keepdims=True)


def _sigmoid(x):
    return 1.0 / (1.0 + jnp.exp(-x))


def _rope(x, c, s1, s2):
    return x * c + pltpu.roll(x, SLAB - 16, 1) * s1 + pltpu.roll(x, 16, 1) * s2


def _rope_bwd(d, c, s1, s2):
    return d * c + pltpu.roll(d * s1, 16, 1) + pltpu.roll(d * s2, SLAB - 16, 1)


def _roll_rows(x, shift):
    return jnp.concatenate([x[-shift:], x[:-shift]], axis=0)


def _rope_t(x, c, s1, s2):
    return x * c + _roll_rows(x, SLAB - 16) * s1 + _roll_rows(x, 16) * s2


def _rope_t_bwd(d, c, s1, s2):
    return d * c + _roll_rows(d * s1, 16) + _roll_rows(d * s2, SLAB - 16)


def _plant_rows(slab, row, vals):
    hi = vals.astype(BF16).astype(F32)
    lo = (vals - hi).astype(BF16).astype(F32)
    idx = lax.broadcasted_iota(jnp.int32, slab.shape, 0)
    return jnp.where(idx == row, -hi, jnp.where(idx == row + 1, -lo, slab))


def _row_spec(tm, n):
    return pl.BlockSpec((tm, n), lambda i: (i, 0))


def _col_spec(n, tm):
    return pl.BlockSpec((n, tm), lambda i: (0, i))


def _full_spec(shape):
    nd = len(shape)
    return pl.BlockSpec(shape, lambda i: (0,) * nd, pipeline_mode=pl.Buffered(1))


def _acc_rows(ref, val):
    @pl.when(pl.program_id(0) == 0)
    def _():
        ref[...] = jnp.zeros_like(ref)
    ref[...] += jnp.sum(val, axis=0, keepdims=True)


def _rope_tables(pos_col, freq_row, early):
    t = pos_col.shape[0]
    tm = _token_tile(t)
    n = len(early)

    def body(pos_ref, f_ref, *rest):
        shard_refs, (c_ref, s1_ref, s2_ref, ct_ref, s1t_ref, s2t_ref) = rest[:n], rest[n:n + 6]
        start, finish = _two_level_gather(shard_refs, rest[n + 6:2 * n + 6], *rest[2 * n + 6:])
        pl.when(pl.program_id(0) == 0)(start)
        ang = pos_ref[...].astype(F32) * f_ref[...]
        lane = lax.broadcasted_iota(jnp.int32, ang.shape, 1)
        s = jnp.sin(ang)
        c = jnp.cos(ang)
        s1 = jnp.where((lane >= 64) & (lane < 80), -s, 0.0)
        s2 = jnp.where((lane >= 80) & (lane < 96), s, 0.0)
        c_ref[...], s1_ref[...], s2_ref[...] = c, s1, s2
        ct_ref[...], s1t_ref[...], s2t_ref[...] = c.T, s1.T, s2.T
        pl.when(pl.program_id(0) == t // tm - 1)(finish)

    tab = jax.ShapeDtypeStruct((t, SLAB), F32)
    tabt = jax.ShapeDtypeStruct((SLAB, t), F32)
    outs = pl.pallas_call(
        body, name="rope_tables", grid=(t // tm,),
        in_specs=[_row_spec(tm, 1), _full_spec((1, SLAB))] + [ANY_SPEC] * n,
        out_specs=[_row_spec(tm, SLAB)] * 3 + [_col_spec(SLAB, tm)] * 3 + [ANY_SPEC] * n,
        out_shape=[tab] * 3 + [tabt] * 3 + [jax.ShapeDtypeStruct((N_DEV,) + a.shape, a.dtype) for a in early],
        scratch_shapes=_exchange_scratch(n),
        compiler_params=_params(("arbitrary",)),
    )(pos_col, freq_row, *early)
    return outs[:6], outs[6:]


def _inproj_fwd(x, g1, w_in, g_q, g_kv, w_kvb, w_qb_t, w_kb_t, w_vb_t, w_kr_t, tables):
    t = x.shape[0]
    tm = _wide_token_tile(t)

    def body(x_ref, g1_ref, win_ref, gq_ref, gkv_ref, wkvb_ref, wqbt_ref, wkbt_ref, wvbt_ref, wkrt_ref,
             c_ref, s1_ref, s2_ref, ct_ref, s1t_ref, s2t_ref,
             h_ref, gates_ref, qa_ref, ka_ref, va_ref, cq_ref, ckv_ref, cqn_ref, ckvn_ref,
             kb_ref, vb_ref, qt_ref, kt_ref, vt_ref):
        xv = x_ref[...]
        h = (xv * _rms_r(xv) * g1_ref[...]).astype(BF16)
        h_ref[...] = h
        proj = _dot(h, win_ref[...])
        gates_ref[...] = proj[:, C_GATES:C_QA].astype(BF16)
        qa_ref[...] = proj[:, C_QA:C_KA].astype(BF16)
        ka_ref[...] = proj[:, C_KA:C_VA].astype(BF16)
        va_ref[...] = proj[:, C_VA:C_CQ].astype(BF16)
        cq = proj[:, C_CQ:C_CKV]
        ckv = proj[:, C_CKV:C_KR]
        kr = proj[:, C_KR:D_IN_PAD]
        cq_ref[...] = cq
        ckv_ref[...] = ckv
        cqn = (cq * _rms_r(cq) * gq_ref[...]).astype(BF16)
        ckvn = (ckv * _rms_r(ckv) * gkv_ref[...]).astype(BF16)
        cqn_ref[...] = cqn
        ckvn_ref[...] = ckvn
        c, s1, s2 = c_ref[...], s1_ref[...], s2_ref[...]
        kvb = _dot(ckvn, wkvb_ref[...])
        kr_rot = _rope(kr, c, s1, s2)
        ct, s1t, s2t = ct_ref[...], s1t_ref[...], s2t_ref[...]
        q_t = _dot_nt(wqbt_ref[...], cqn)
        k_t = _dot_nt(wkbt_ref[...], ckvn)
        kr_t = _rope_t(_dot_nt(wkrt_ref[...], h), ct, s1t, s2t)
        k_lane = lax.broadcasted_iota(jnp.int32, (1, SLAB), 1)
        k_ones = jnp.where((k_lane == Q_HEAD_B) | (k_lane == Q_HEAD_B + 1), 1.0, 0.0)
        for hd in range(N_HEADS):
            sl = slice(hd * SLAB, (hd + 1) * SLAB)
            kb_ref[:, sl] = (kvb[:, sl] + kr_rot + k_ones).astype(BF16)
            qt_ref[sl, :] = (_rope_t(q_t[sl, :], ct, s1t, s2t) * SCORE_B).astype(BF16)
            kt_ref[sl, :] = (k_t[sl, :] + kr_t).astype(BF16)
        v_lane = lax.broadcasted_iota(jnp.int32, (1, HM), 1) & (SLAB - 1)
        v_ones = jnp.where((v_lane == V_DIM_B) | (v_lane == V_DIM_B + 1), 1.0, 0.0)
        vb_ref[...] = (kvb[:, HM:2 * HM] + v_ones).astype(BF16)
        pad_row = lax.broadcasted_iota(jnp.int32, (HM, 1), 0) & (SLAB - 1)
        ones_rows = jnp.where((pad_row >= V_DIM_B) & (pad_row < V_DIM_B + ONES_ROWS), 1.0, 0.0)
        vt_ref[...] = (_dot_nt(wvbt_ref[...], ckvn) + ones_rows).astype(BF16)

    def sds(n, dt):
        return jax.ShapeDtypeStruct((t, n), dt)

    outs = [(D_MODEL, BF16), (2 * D_MODEL, BF16), (HM, BF16), (N_KV_A * SLAB, BF16), (N_KV_A * SLAB, BF16),
            (Q_LORA, F32), (KV_LORA, F32), (Q_LORA, BF16), (KV_LORA, BF16), (HM, BF16), (HM, BF16)]
    tab, tabt = _row_spec(tm, SLAB), _col_spec(SLAB, tm)
    return pl.pallas_call(
        body, name="inproj_fwd", grid=(t // tm,),
        in_specs=[_row_spec(tm, D_MODEL), _full_spec((1, D_MODEL)), _full_spec((D_MODEL, D_IN_PAD)),
                  _full_spec((1, Q_LORA)), _full_spec((1, KV_LORA)), _full_spec((KV_LORA, 2 * HM)),
                  _full_spec((HM, Q_LORA)), _full_spec((HM, KV_LORA)), _full_spec((HM, KV_LORA)),
                  _full_spec((SLAB, D_MODEL)), tab, tab, tab, tabt, tabt, tabt],
        out_specs=[_row_spec(tm, n) for n, _ in outs] + [_col_spec(HM, tm)] * 3,
        out_shape=[sds(n, dt) for n, dt in outs] + [jax.ShapeDtypeStruct((HM, t), BF16)] * 3,
        compiler_params=_params(("parallel",)),
    )(x, g1, w_in, g_q, g_kv, w_kvb, w_qb_t, w_kb_t, w_vb_t, w_kr_t, *tables)


def _tile_group(a):
    return jnp.concatenate([a] * GROUP_A, axis=1)


def _swa_masks():
    row = lax.broadcasted_iota(jnp.int32, (BLOCK, GROUP_A * BLOCK), 0)
    col = lax.broadcasted_iota(jnp.int32, (BLOCK, GROUP_A * BLOCK), 1) & (BLOCK - 1)
    return row <= col, row > col


def _heads_beside(ref, g, rows=slice(None)):
    return jnp.concatenate([ref[rows, (g * GROUP_A + hh) * SLAB:(g * GROUP_A + hh + 1) * SLAB].T
                            for hh in range(GROUP_A)], axis=1)


def _rows_beside(ref, g, cols=slice(None)):
    return jnp.concatenate([ref[g * GROUP_A + hh, :, cols] for hh in range(GROUP_A)], axis=1)


def _swa_rows(sinks):
    slopes = jnp.repeat(jnp.asarray(SLOPES_A, F32).reshape(N_KV_A, GROUP_A, 1), BLOCK, axis=2)
    sink_rows = jnp.repeat(sinks.reshape(N_KV_A, GROUP_A, 1), BLOCK, axis=2)
    return slopes.reshape(N_KV_A, 1, GROUP_A * BLOCK), sink_rows.reshape(N_KV_A, 1, GROUP_A * BLOCK)


def _swa_fwd(qa, ka, va, pos_col, pos_row, sinks):
    t = qa.shape[0]
    sub = min(SWA_BLOCKS_PER_STEP, t // BLOCK)
    tb = sub * BLOCK
    gw = GROUP_A * BLOCK
    slope_rows, sink_rows = _swa_rows(sinks)

    def body(q_ref, k_ref, kp_ref, v_ref, vp_ref, pk_ref, pkp_ref, pq_ref, slope_ref, sink_ref, o_ref, l_ref):
        i = pl.program_id(0)
        mask_c, older = _swa_masks()
        for s in range(sub):
            rows = slice(s * BLOCK, (s + 1) * BLOCK)
            before = slice((s - 1) * BLOCK, s * BLOCK)
            if s == 0:
                k_p, v_p, pk_p, mask_p = kp_ref, vp_ref, pkp_ref[...], jnp.logical_and(older, i > 0)
                before = slice(None)
            else:
                k_p, v_p, pk_p, mask_p = k_ref, v_ref, pk_ref[before, :], older
            pq = pq_ref[:, rows]
            dist_c = _tile_group(jnp.abs(pk_ref[rows, :] - pq).astype(F32))
            dist_p = _tile_group(jnp.abs(pk_p - pq).astype(F32))
            raw = []
            for g in range(N_KV_A):
                gs = slice(g * SLAB, (g + 1) * SLAB)
                x = _heads_beside(q_ref, g, rows)
                raw.append((_dot(k_ref[rows, gs], x), _dot(k_p[before, gs], x)))
            for g in range(N_KV_A):
                gs = slice(g * SLAB, (g + 1) * SLAB)
                slope, sink = slope_ref[g], sink_ref[g]
                s_c = jnp.where(mask_c, raw[g][0] * SCALE_A - slope * dist_c, NEG)
                s_p = jnp.where(mask_p, raw[g][1] * SCALE_A - slope * dist_p, NEG)
                m = jnp.maximum(jnp.maximum(jnp.max(s_c, axis=0, keepdims=True),
                                            jnp.max(s_p, axis=0, keepdims=True)), sink)
                e_c = jnp.exp(s_c - m)
                e_p = jnp.exp(s_p - m)
                den = (jnp.sum(e_c, axis=0, keepdims=True) + jnp.sum(e_p, axis=0, keepdims=True)
                       + jnp.exp(sink - m))
                inv = 1.0 / den
                ot = (_dot_tn(v_ref[rows, gs], (e_c * inv).astype(BF16))
                      + _dot_tn(v_p[before, gs], (e_p * inv).astype(BF16)))
                lse = m + jnp.log(den)
                for hh in range(GROUP_A):
                    hd = g * GROUP_A + hh
                    seg = slice(hh * BLOCK, (hh + 1) * BLOCK)
                    o_ref[rows, hd * SLAB:(hd + 1) * SLAB] = ot[:, seg].T.astype(BF16)
                    l_ref[hd, :, rows] = lse[:, seg]

    cur = lambda i: (i, 0)
    prev = lambda i: (jnp.maximum(sub * i - 1, 0), 0)
    kvw = N_KV_A * SLAB
    rows = pl.BlockSpec((N_KV_A, 1, gw), lambda i: (0, 0, 0))
    return pl.pallas_call(
        body, name="swa_fwd", grid=(t // tb,),
        in_specs=[pl.BlockSpec((tb, HM), cur),
                  pl.BlockSpec((tb, kvw), cur), pl.BlockSpec((BLOCK, kvw), prev),
                  pl.BlockSpec((tb, kvw), cur), pl.BlockSpec((BLOCK, kvw), prev),
                  pl.BlockSpec((tb, 1), cur), pl.BlockSpec((BLOCK, 1), prev),
                  pl.BlockSpec((1, tb), lambda i: (0, i)), rows, rows],
        out_specs=[pl.BlockSpec((tb, HM), cur), pl.BlockSpec((N_HEADS, 1, tb), lambda i: (0, 0, i))],
        out_shape=[jax.ShapeDtypeStruct((t, HM), BF16), jax.ShapeDtypeStruct((N_HEADS, 1, t), F32)],
        compiler_params=_params(("parallel",)),
    )(qa, ka, ka, va, va, pos_col, pos_col, pos_row, slope_rows, sink_rows)


def _swa_bwd(qa, ka, va, out_a, d_oa, lse, pos_col, pos_row, sinks):
    t = qa.shape[0]
    nb = t // BLOCK
    sub = min(SWA_BLOCKS_PER_STEP, nb)
    tb = sub * BLOCK
    steps = t // tb
    gw = GROUP_A * BLOCK
    kvw = N_KV_A * SLAB
    slope_rows, sink_rows = _swa_rows(sinks)

    def body(q_ref, qn_ref, do_ref, don_ref, l_ref, ln_ref, o_ref, on_ref, kp_ref, k_ref, vp_ref, v_ref,
             pkp_ref, pk_ref, pq_ref, pqn_ref, slope_ref, sink_ref, dqkv_ref, dsink_ref):
        j = pl.program_id(0)
        mask_cc, older = _swa_masks()

        @pl.when(j == 0)
        def _():
            dsink_ref[...] = jnp.zeros_like(dsink_ref)

        def tile(k, v, x, dox, lrow, drow, dist, mask, slope):
            s = jnp.where(mask, _dot(k, x) * SCALE_A - slope * dist, NEG)
            p = jnp.exp(s - lrow)
            ds = p * (_dot(v, dox) - drow)
            return p.astype(BF16), ds.astype(BF16)

        def queries(q, do, o, l, rows):
            out = []
            for g in range(N_KV_A):
                x, dox = _heads_beside(q, g, rows), _heads_beside(do, g, rows)
                drow = jnp.sum(dox.astype(F32) * _heads_beside(o, g, rows).astype(F32), axis=0, keepdims=True)
                out.append((x, dox, _rows_beside(l, g, rows), drow))
            return out

        cur_q = queries(q_ref, do_ref, o_ref, l_ref, slice(0, BLOCK))
        for s in range(sub):
            rows = slice(s * BLOCK, (s + 1) * BLOCK)
            if s == 0:
                k_p, v_p, before, pkp = kp_ref, vp_ref, slice(None), pkp_ref[...]
                mask_cp = jnp.logical_and(older, j > 0)
            else:
                before = slice((s - 1) * BLOCK, s * BLOCK)
                k_p, v_p, pkp, mask_cp = k_ref, v_ref, pk_ref[before, :], older
            if s == sub - 1:
                nxt_q = queries(qn_ref, don_ref, on_ref, ln_ref, slice(None))
                pqn, mask_nc = pqn_ref[...], jnp.logical_and(older, j < steps - 1)
            else:
                after = slice((s + 1) * BLOCK, (s + 2) * BLOCK)
                nxt_q = queries(q_ref, do_ref, o_ref, l_ref, after)
                pqn, mask_nc = pq_ref[:, after], older
            pkc, pq = pk_ref[rows, :], pq_ref[:, rows]
            dist_cc = _tile_group(jnp.abs(pkc - pq).astype(F32))
            dist_cp = _tile_group(jnp.abs(pkp - pq).astype(F32))
            dist_nc = _tile_group(jnp.abs(pkc - pqn).astype(F32))
            for g in range(N_KV_A):
                gs = slice(g * SLAB, (g + 1) * SLAB)
                kc, kp, vc, vp = k_ref[rows, gs], k_p[before, gs], v_ref[rows, gs], v_p[before, gs]
                slope, sink = slope_ref[g], sink_ref[g]
                x, dox, lrow, drow = cur_q[g]
                xn, doxn, lrown, drown = nxt_q[g]
                p_cc, ds_cc = tile(kc, vc, x, dox, lrow, drow, dist_cc, mask_cc, slope)
                _, ds_cp = tile(kp, vp, x, dox, lrow, drow, dist_cp, mask_cp, slope)
                p_nc, ds_nc = tile(kc, vc, xn, doxn, lrown, drown, dist_nc, mask_nc, slope)
                dqt = (_dot_tn(kc, ds_cc) + _dot_tn(kp, ds_cp)) * SCALE_A
                for hh in range(GROUP_A):
                    hd = g * GROUP_A + hh
                    dqkv_ref[rows, hd * SLAB:(hd + 1) * SLAB] = dqt[:, hh * BLOCK:(hh + 1) * BLOCK].T.astype(BF16)
                dqkv_ref[rows, HM + g * SLAB:HM + (g + 1) * SLAB] = (
                    (_dot_nt(ds_cc, x) + _dot_nt(ds_nc, xn)) * SCALE_A).astype(BF16)
                dqkv_ref[rows, HM + kvw + g * SLAB:HM + kvw + (g + 1) * SLAB] = (
                    _dot_nt(p_cc, dox) + _dot_nt(p_nc, doxn)).astype(BF16)
                dsink_ref[g] -= jnp.exp(sink - lrow) * drow
            cur_q = nxt_q

    cur = lambda j: (j, 0)
    prev = lambda j: (jnp.maximum(sub * j - 1, 0), 0)
    nxt = lambda j: (jnp.minimum(sub * (j + 1), nb - 1), 0)
    cur3 = lambda j: (0, 0, j)
    nxt3 = lambda j: (0, 0, jnp.minimum(sub * (j + 1), nb - 1))
    rows = pl.BlockSpec((N_KV_A, 1, gw), lambda j: (0, 0, 0))
    return pl.pallas_call(
        body, name="swa_bwd", grid=(steps,),
        in_specs=[pl.BlockSpec((tb, HM), cur), pl.BlockSpec((BLOCK, HM), nxt),
                  pl.BlockSpec((tb, HM), cur), pl.BlockSpec((BLOCK, HM), nxt),
                  pl.BlockSpec((N_HEADS, 1, tb), cur3), pl.BlockSpec((N_HEADS, 1, BLOCK), nxt3),
                  pl.BlockSpec((tb, HM), cur), pl.BlockSpec((BLOCK, HM), nxt),
                  pl.BlockSpec((BLOCK, kvw), prev), pl.BlockSpec((tb, kvw), cur),
                  pl.BlockSpec((BLOCK, kvw), prev), pl.BlockSpec((tb, kvw), cur),
                  pl.BlockSpec((BLOCK, 1), prev), pl.BlockSpec((tb, 1), cur),
                  pl.BlockSpec((1, tb), lambda j: (0, j)),
                  pl.BlockSpec((1, BLOCK), lambda j: (0, jnp.minimum(sub * (j + 1), nb - 1))), rows, rows],
        out_specs=[pl.BlockSpec((tb, HM + 2 * kvw), cur), rows],
        out_shape=[jax.ShapeDtypeStruct((t, HM + 2 * kvw), BF16), jax.ShapeDtypeStruct((N_KV_A, 1, gw), F32)],
        compiler_params=_params(("arbitrary",)),
    )(qa, qa, d_oa, d_oa, lse, lse, out_a, out_a, ka, ka, va, va,
      pos_col, pos_col, pos_row, pos_row, slope_rows, sink_rows)


def _mesh_pos():
    return lax.axis_index("x"), lax.axis_index("y"), lax.axis_index("c")


def _flip(v, bit):
    return 1 - v if bit else v


def _direct_copies(srcs, dsts, send_sems, recv_sems, local_sems, gather, sem_base=0, only=None):
    x, y, c = _mesh_pos()
    me = 4 * x + 2 * y + c

    def among(idx, dests):
        ok = idx == dests[0]
        for d in dests[1:]:
            ok = jnp.logical_or(ok, idx == d)
        return ok

    local, remote = [], []
    for a, (src, dst) in enumerate(zip(srcs, dsts)):
        dests = None if only is None else only[a]
        recv_ok = None if dests is None else among(me, dests)
        local.append((pltpu.make_async_copy(src if gather else src.at[me], dst.at[me],
                                            local_sems.at[sem_base + a]), recv_ok))
        for r in range(1, N_DEV):
            px, py, pc = _flip(x, r & 4), _flip(y, r & 2), _flip(c, r & 1)
            peer = 4 * px + 2 * py + pc
            sem = (N_DEV - 1) * (sem_base + a) + r - 1
            copy = pltpu.make_async_remote_copy(
                src_ref=src if gather else src.at[peer], dst_ref=dst.at[me],
                send_sem=send_sems.at[sem], recv_sem=recv_sems.at[sem],
                device_id=(px, py, pc), device_id_type=pl.DeviceIdType.MESH)
            remote.append((copy, None if dests is None else among(peer, dests), recv_ok))
    return local, remote


def _when(cond, fn):
    if cond is None:
        fn()
    else:
        pl.when(cond)(fn)


def _start_copies(local, remote):
    for cp, ok in local:
        _when(ok, cp.start)
    for cp, send_ok, _ in remote:
        _when(send_ok, cp.start)


def _wait_copies(local, remote):
    for cp, _, recv_ok in remote:
        _when(recv_ok, cp.wait_recv)
    for cp, send_ok, _ in remote:
        _when(send_ok, cp.wait_send)
    for cp, ok in local:
        _when(ok, cp.wait)


def _exchange_scratch(n):
    return [pltpu.SemaphoreType.DMA((n * (N_DEV - 1),)), pltpu.SemaphoreType.DMA((n * (N_DEV - 1),)),
            pltpu.SemaphoreType.DMA((n,))]


ANY_SPEC = pl.BlockSpec(memory_space=pl.ANY)


def _mla_fwd(qt, kb, vt, late):
    t = kb.shape[0]
    tk = _attn_tile(t)
    ratio = 2 if t >= 2 * tk else 1
    tq = ratio * tk
    nq = t // tq
    hps = MLA_FWD_HEADS_PER_STEP
    w = hps * SLAB
    pairs = [(i, j) for i in range(nq) for j in range(ratio * (i + 1))]
    i_tab = jnp.asarray(np.array([p[0] for p in pairs], np.int32))
    j_tab = jnp.asarray(np.array([p[1] for p in pairs], np.int32))

    n_late = len(late)

    def body(it_ref, jt_ref, qt_ref, k_ref, vt_ref, *rest):
        late_refs, (o_ref, ot_ref, qa_ref) = rest[:n_late], rest[n_late:n_late + 3]
        gathered_refs = rest[n_late + 3:2 * n_late + 3]
        m_s, acc_s, send_sems, recv_sems, local_sems = rest[2 * n_late + 3:]
        n = pl.program_id(1)
        i, j = it_ref[n], jt_ref[n]
        first_step = jnp.logical_and(pl.program_id(0) == 0, n == 0)
        last_step = jnp.logical_and(pl.program_id(0) == N_HEADS // hps - 1, n == len(pairs) - 1)

        @pl.when(first_step)
        def _():
            _start_copies(*_direct_copies(late_refs, gathered_refs, send_sems, recv_sems, local_sems, True))

        @pl.when(j == 0)
        def _():
            m_s[...] = jnp.full_like(m_s, NEG)
            acc_s[...] = jnp.zeros_like(acc_s)

        def update(masked, q0):
            qc = slice(q0, tq)

            def scores(hh):
                sl = slice(hh * SLAB, (hh + 1) * SLAB)
                return _dot(k_ref[:, sl], qt_ref[sl, qc])

            def softmax(hh, s):
                if masked:
                    s = jnp.where(lax.broadcasted_iota(jnp.int32, s.shape, 0)
                                  <= lax.broadcasted_iota(jnp.int32, s.shape, 1), s, NEG)
                m_old = m_s[hh][:, qc]
                m_new = jnp.maximum(m_old, jnp.max(s, axis=0, keepdims=True))
                m_s[hh, :, qc] = m_new
                return jnp.exp2(s - m_new).astype(BF16), jnp.exp2(m_old - m_new)

            def accumulate(hh, p, alpha):
                sl = slice(hh * SLAB, hh * SLAB + V_DIM_B + ONES_ROWS)
                acc_s[sl, qc] = alpha * acc_s[sl, qc] + _dot(vt_ref[sl, :], p)

            s_next, pending = scores(0), None
            for hh in range(hps):
                s = s_next
                if hh + 1 < hps:
                    s_next = scores(hh + 1)
                p, alpha = softmax(hh, s)
                if pending is not None:
                    accumulate(*pending)
                pending = (hh, p, alpha)
            accumulate(*pending)

        @pl.when(j < ratio * i)
        def _():
            update(False, 0)

        for part in range(ratio):
            @pl.when(j == ratio * i + part)
            def _():
                update(True, part * tk)

        @pl.when(j == ratio * i + ratio - 1)
        def _():
            for hh in range(hps):
                sl = slice(hh * SLAB, (hh + 1) * SLAB)
                den = acc_s[hh * SLAB + V_DIM_B:hh * SLAB + V_DIM_B + 1, :]
                values = lax.broadcasted_iota(jnp.int32, (SLAB, tq), 0) < V_DIM_B
                ot = jnp.where(values, acc_s[sl, :] / den, 0.0)
                ot_ref[sl, :] = ot.astype(BF16)
                o_ref[:, sl] = ot.T.astype(BF16)
                lse = m_s[hh] + jnp.log2(den)
                qa_ref[sl, :] = _plant_rows(qt_ref[sl, :].astype(F32), Q_HEAD_B, lse).astype(BF16)

        @pl.when(last_step)
        def _():
            _wait_copies(*_direct_copies(late_refs, gathered_refs, send_sems, recv_sems, local_sems, True))

    grid_spec = pltpu.PrefetchScalarGridSpec(
        num_scalar_prefetch=2, grid=(N_HEADS // hps, len(pairs)),
        in_specs=[pl.BlockSpec((w, tq), lambda h, n, it, jt: (h, it[n])),
                  pl.BlockSpec((tk, w), lambda h, n, it, jt: (jt[n], h)),
                  pl.BlockSpec((w, tk), lambda h, n, it, jt: (h, jt[n]))] + [ANY_SPEC] * n_late,
        out_specs=[pl.BlockSpec((tq, w), lambda h, n, it, jt: (it[n], h)),
                   pl.BlockSpec((w, tq), lambda h, n, it, jt: (h, it[n])),
                   pl.BlockSpec((w, tq), lambda h, n, it, jt: (h, it[n]))] + [ANY_SPEC] * n_late,
        scratch_shapes=[pltpu.VMEM((hps, 1, tq), F32), pltpu.VMEM((w, tq), F32)] + _exchange_scratch(n_late))
    outs = pl.pallas_call(
        body, name="mla_fwd", grid_spec=grid_spec,
        out_shape=[jax.ShapeDtypeStruct((t, HM), BF16), jax.ShapeDtypeStruct((HM, t), BF16),
                   jax.ShapeDtypeStruct((HM, t), BF16)]
        + [jax.ShapeDtypeStruct((N_DEV,) + a.shape, a.dtype) for a in late],
        compiler_params=_params(("arbitrary", "arbitrary")),
    )(i_tab, j_tab, qt, kb, vt, *late)
    return outs[0], outs[1], outs[2], list(outs[3:])


def _mla_bwd(qt, kb, kt, vb, d_ob_t, grad_slices):
    t = kb.shape[0]
    tk = _attn_tile(t)
    ratio = 2 if t >= 2 * tk else 1
    tq = ratio * tk
    nk, nq = t // tk, t // tq
    hps = MLA_HEADS_PER_STEP
    w = hps * SLAB
    pairs = [(j, i) for j in range(nk) for i in range(j // ratio, nq)]
    j_tab = jnp.asarray(np.array([p[0] for p in pairs], np.int32))
    i_tab = jnp.asarray(np.array([p[1] for p in pairs], np.int32))

    n_ex = len(grad_slices)

    def body(jt_ref, it_ref, qt_ref, dot_ref, k_ref, kt_ref, v_ref, *rest):
        slice_refs, (dqt_ref, dkt_ref, dvt_ref) = rest[:n_ex], rest[n_ex:n_ex + 3]
        part_refs = rest[n_ex + 3:2 * n_ex + 3]
        dk_s, dv_s, send_sems, recv_sems, local_sems = rest[2 * n_ex + 3:]
        n = pl.program_id(1)
        j, i = jt_ref[n], it_ref[n]
        first_step = jnp.logical_and(pl.program_id(0) == 0, n == 0)
        last_step = jnp.logical_and(pl.program_id(0) == N_HEADS // hps - 1, n == len(pairs) - 1)

        @pl.when(first_step)
        def _():
            _start_copies(*_direct_copies(slice_refs, part_refs, send_sems, recv_sems, local_sems, False))

        @pl.when(n == 0)
        def _():
            dqt_ref[...] = jnp.zeros_like(dqt_ref)

        def update(diagonal, q0):
            qc = slice(q0, tq)
            cols = pl.ds(pl.multiple_of(i * tq + q0, tk), tq - q0)

            def softmax_bwd(hh, s, dp):
                if diagonal:
                    s = jnp.where(lax.broadcasted_iota(jnp.int32, s.shape, 0)
                                  <= lax.broadcasted_iota(jnp.int32, s.shape, 1), s, NEG)
                p = jnp.exp2(s)
                return p.astype(BF16), (p * dp).astype(BF16)

            def gradients(hh, p, ds):
                base = hh * SLAB
                vrows = slice(base, base + V_DIM_B)
                qrows = slice(base, base + QK_NOPE + QK_ROPE)
                dv = _dot_nt(dot_ref[vrows, qc], p)
                dk = _dot_nt(qt_ref[qrows, qc], ds)
                if diagonal:
                    dv_s[base:base + SLAB, :] = jnp.concatenate([dv, jnp.zeros((SLAB - V_DIM_B, tk), F32)], axis=0)
                    dk_s[base:base + SLAB, :] = jnp.concatenate(
                        [dk, jnp.zeros((SLAB - QK_NOPE - QK_ROPE, tk), F32)], axis=0)
                else:
                    dv_s[vrows, :] += dv
                    dk_s[qrows, :] += dk
                dqt_ref[qrows, cols] += _dot(kt_ref[qrows, :], ds)

            def scores(hh):
                sl = slice(hh * SLAB, (hh + 1) * SLAB)
                return _dot(k_ref[:, sl], qt_ref[sl, qc])

            def dprod(hh):
                sl = slice(hh * SLAB, (hh + 1) * SLAB)
                return _dot(v_ref[:, sl], dot_ref[sl, qc])

            s_next = scores(0)
            for hh in range(hps):
                s = s_next
                dp = dprod(hh)
                if hh + 1 < hps:
                    s_next = scores(hh + 1)
                gradients(hh, *softmax_bwd(hh, s, dp))

        first_tile = lax.div(j, ratio)
        for part in range(ratio):
            @pl.when(jnp.logical_and(i == first_tile, lax.rem(j, ratio) == part))
            def _():
                update(True, part * tk)

        @pl.when(i > first_tile)
        def _():
            update(False, 0)

        @pl.when(i == nq - 1)
        def _():
            dkt_ref[...] = (dk_s[...] * (1.0 / LOG2E)).astype(BF16)
            dvt_ref[...] = dv_s[...].astype(BF16)

        @pl.when(last_step)
        def _():
            _wait_copies(*_direct_copies(slice_refs, part_refs, send_sems, recv_sems, local_sems, False))

    grid_spec = pltpu.PrefetchScalarGridSpec(
        num_scalar_prefetch=2, grid=(N_HEADS // hps, len(pairs)),
        in_specs=[pl.BlockSpec((w, tq), lambda h, n, jt, it: (h, it[n])),
                  pl.BlockSpec((w, tq), lambda h, n, jt, it: (h, it[n])),
                  pl.BlockSpec((tk, w), lambda h, n, jt, it: (jt[n], h)),
                  pl.BlockSpec((w, tk), lambda h, n, jt, it: (h, jt[n])),
                  pl.BlockSpec((tk, w), lambda h, n, jt, it: (jt[n], h))] + [ANY_SPEC] * n_ex,
        out_specs=[pl.BlockSpec((w, t), lambda h, n, jt, it: (h, 0)),
                   pl.BlockSpec((w, tk), lambda h, n, jt, it: (h, jt[n])),
                   pl.BlockSpec((w, tk), lambda h, n, jt, it: (h, jt[n]))] + [ANY_SPEC] * n_ex,
        scratch_shapes=[pltpu.VMEM((w, tk), F32), pltpu.VMEM((w, tk), F32)] + _exchange_scratch(n_ex))
    outs = pl.pallas_call(
        body, name="mla_bwd", grid_spec=grid_spec,
        out_shape=[jax.ShapeDtypeStruct((HM, t), F32), jax.ShapeDtypeStruct((HM, t), BF16),
                   jax.ShapeDtypeStruct((HM, t), BF16)]
        + [jax.ShapeDtypeStruct(a.shape, a.dtype) for a in grad_slices],
        compiler_params=_params(("arbitrary", "arbitrary")),
    )(j_tab, i_tab, qt, d_ob_t, kb, kt, vb, *grad_slices)
    return outs[0], outs[1], outs[2], list(outs[3:])


def _merge_fwd(out_a, out_b, gates, x, w_oa, w_ob, w_out, g2, g3):
    t = x.shape[0]
    tm = _wide_token_tile(t)

    def body(oa_ref, ob_ref, gates_ref, x_ref, woa_ref, wob_ref, wout_ref, g2_ref, g3_ref,
             oap_ref, obp_ref, merged_ref, y_ref, x1_ref, h2_ref):
        oa_p = _dot(oa_ref[...], woa_ref[...])
        ob_p = _dot(ob_ref[...], wob_ref[...])
        oap_ref[...] = oa_p.astype(BF16)
        obp_ref[...] = ob_p.astype(BF16)
        sa = _sigmoid(gates_ref[:, 0:D_MODEL].astype(F32))
        sb = _sigmoid(gates_ref[:, D_MODEL:2 * D_MODEL].astype(F32))
        merged = (sa * oa_p + sb * ob_p).astype(BF16)
        merged_ref[...] = merged
        y = _dot(merged, wout_ref[...])
        y_ref[...] = y
        x1 = x_ref[...] + y * _rms_r(y) * g2_ref[...]
        x1_ref[...] = x1
        h2_ref[...] = (x1 * _rms_r(x1) * g3_ref[...]).astype(BF16)

    def sds(dt):
        return jax.ShapeDtypeStruct((t, D_MODEL), dt)

    row = _row_spec(tm, D_MODEL)
    return pl.pallas_call(
        body, name="merge_fwd", grid=(t // tm,),
        in_specs=[_row_spec(tm, HM), _row_spec(tm, HM), _row_spec(tm, 2 * D_MODEL), row,
                  _full_spec((HM, D_MODEL)), _full_spec((HM, D_MODEL)), _full_spec((D_MODEL, D_MODEL)),
                  _full_spec((1, D_MODEL)), _full_spec((1, D_MODEL))],
        out_specs=[row] * 6,
        out_shape=[sds(BF16), sds(BF16), sds(BF16), sds(F32), sds(F32), sds(BF16)],
        compiler_params=_params(("parallel",)),
    )(out_a, out_b, gates, x, w_oa, w_ob, w_out, g2, g3)


def _merge_bwd(dx1, y, gates, oa_p, ob_p, out_a, out_b, out_b_t, merged, w_oa, w_ob, w_out, g2):
    t = dx1.shape[0]
    tm = _wide_token_tile(t)

    def body(dx1_ref, y_ref, gates_ref, oap_ref, obp_ref, oa_ref, ob_ref, obt_ref, merged_ref,
             woa_ref, wob_ref, wout_ref, g2_ref,
             dgates_ref, doa_ref, dobt_ref, dg2_ref, dwoa_ref, dwob_ref, dwout_ref):
        @pl.when(pl.program_id(0) == 0)
        def _():
            dwoa_ref[...] = jnp.zeros_like(dwoa_ref)
            dwob_ref[...] = jnp.zeros_like(dwob_ref)
            dwout_ref[...] = jnp.zeros_like(dwout_ref)

        dx1v = dx1_ref[...]
        yv = y_ref[...]
        r2 = _rms_r(yv)
        _acc_rows(dg2_ref, dx1v * yv * r2)
        dy = _rms_bwd(yv, r2, g2_ref[...], dx1v).astype(BF16)
        dwout_ref[...] += _dot_tn(merged_ref[...], dy)
        dm = _dot_nt(dy, wout_ref[...])
        sa = _sigmoid(gates_ref[:, 0:D_MODEL].astype(F32))
        sb = _sigmoid(gates_ref[:, D_MODEL:2 * D_MODEL].astype(F32))
        d_oap = (dm * sa).astype(BF16)
        d_obp = (dm * sb).astype(BF16)
        dwoa_ref[...] += _dot_tn(oa_ref[...], d_oap)
        dwob_ref[...] += _dot_tn(ob_ref[...], d_obp)
        dgates_ref[:, 0:D_MODEL] = (dm * oap_ref[...].astype(F32) * sa * (1.0 - sa)).astype(BF16)
        dgates_ref[:, D_MODEL:2 * D_MODEL] = (dm * obp_ref[...].astype(F32) * sb * (1.0 - sb)).astype(BF16)
        doa_ref[...] = _dot_nt(d_oap, woa_ref[...]).astype(BF16)
        d_ob_t = _dot_nt(wob_ref[...], d_obp)
        for hd in range(N_HEADS):
            sl = slice(hd * SLAB, (hd + 1) * SLAB)
            delta = jnp.sum(d_ob_t[sl, :] * obt_ref[sl, :].astype(F32), axis=0, keepdims=True)
            dobt_ref[sl, :] = _plant_rows(d_ob_t[sl, :], V_DIM_B, delta).astype(BF16)

    def sds(n, dt):
        return jax.ShapeDtypeStruct((t, n), dt)

    row = _row_spec(tm, D_MODEL)
    return pl.pallas_call(
        body, name="merge_bwd", grid=(t // tm,),
        in_specs=[row, row, _row_spec(tm, 2 * D_MODEL), row, row, _row_spec(tm, HM), _row_spec(tm, HM),
                  _col_spec(HM, tm), row,
                  _full_spec((HM, D_MODEL)), _full_spec((HM, D_MODEL)), _full_spec((D_MODEL, D_MODEL)),
                  _full_spec((1, D_MODEL))],
        out_specs=[_row_spec(tm, 2 * D_MODEL), _row_spec(tm, HM), _col_spec(HM, tm), _full_spec((1, D_MODEL)),
                   _full_spec((HM, D_MODEL)), _full_spec((HM, D_MODEL)), _full_spec((D_MODEL, D_MODEL))],
        out_shape=[sds(2 * D_MODEL, BF16), sds(HM, BF16), jax.ShapeDtypeStruct((HM, t), BF16),
                   jax.ShapeDtypeStruct((1, D_MODEL), F32),
                   jax.ShapeDtypeStruct((HM, D_MODEL), F32), jax.ShapeDtypeStruct((HM, D_MODEL), F32),
                   jax.ShapeDtypeStruct((D_MODEL, D_MODEL), F32)],
        compiler_params=_params(("arbitrary",), VMEM_LIMIT_MERGE_BWD),
    )(dx1, y, gates, oa_p, ob_p, out_a, out_b, out_b_t, merged, w_oa, w_ob, w_out, g2)


def _mlp_fwd_bwd(x1, h2, target, w_up, w_down, g3, g4):
    t = x1.shape[0]
    tm = _token_tile(t)
    fs = D_FF // N_DEV

    def body(x1_ref, h2_ref, tgt_ref, wup_ref, wdown_ref, g3_ref, g4_ref,
             a_ref, du_ref, dy2_ref, dx1_ref, loss_ref, dg3_ref, dg4_ref):
        x1v = x1_ref[...]
        h2v = h2_ref[...]
        u = jnp.concatenate([_dot(h2v, wup_ref[s]) for s in range(N_DEV)], axis=1)
        ru = jnp.maximum(u, 0.0)
        a = (ru * ru).astype(BF16)
        a_ref[...] = a
        y2 = _dot(a, wdown_ref[...])
        r4 = _rms_r(y2)
        diff = x1v + y2 * r4 * g4_ref[...] - tgt_ref[...]
        _acc_rows(loss_ref, jnp.sum(diff * diff, axis=-1, keepdims=True) * (0.5 / D_MODEL)
                  * jnp.ones((1, SLAB), F32))
        dx2 = diff * (1.0 / D_MODEL)
        _acc_rows(dg4_ref, dx2 * y2 * r4)
        dy2 = _rms_bwd(y2, r4, g4_ref[...], dx2).astype(BF16)
        dy2_ref[...] = dy2
        du = (_dot_nt(dy2, wdown_ref[...]) * (2.0 * ru)).astype(BF16)
        du_ref[...] = du
        dh2 = _dot_nt(du[:, 0:fs], wup_ref[0])
        for s in range(1, N_DEV):
            dh2 += _dot_nt(du[:, s * fs:(s + 1) * fs], wup_ref[s])
        r3 = _rms_r(x1v)
        _acc_rows(dg3_ref, dh2 * x1v * r3)
        dx1_ref[...] = dx2 + _rms_bwd(x1v, r3, g3_ref[...], dh2)

    row = _row_spec(tm, D_MODEL)
    frow = _row_spec(tm, D_FF)
    vec = _full_spec((1, D_MODEL))
    return pl.pallas_call(
        body, name="mlp_fwd_bwd", grid=(t // tm,),
        in_specs=[row, row, row, _full_spec((N_DEV, D_MODEL, fs)), _full_spec((D_FF, D_MODEL)), vec, vec],
        out_specs=[frow, frow, row, row, _full_spec((1, SLAB)), vec, vec],
        out_shape=[jax.ShapeDtypeStruct((t, D_FF), BF16), jax.ShapeDtypeStruct((t, D_FF), BF16),
                   jax.ShapeDtypeStruct((t, D_MODEL), BF16), jax.ShapeDtypeStruct((t, D_MODEL), F32),
                   jax.ShapeDtypeStruct((1, SLAB), F32), jax.ShapeDtypeStruct((1, D_MODEL), F32),
                   jax.ShapeDtypeStruct((1, D_MODEL), F32)],
        compiler_params=_params(("arbitrary",)),
    )(x1, h2, target, w_up, w_down, g3, g4)


def _latent_bwd(dqb_t, dkb_t, dvb_t, cq, ckv, cqn, ckvn, rope_ct, rope_s1t, rope_s2t, g_q, g_kv, w_qb, w_kvb):
    t = cq.shape[0]
    tm = min(t, 2 * _wide_token_tile(t))

    def body(dqt_ref, dkt_ref, dvt_ref, cq_ref, ckv_ref, cqn_ref, ckvn_ref, ct_ref, s1t_ref, s2t_ref,
             gq_ref, gkv_ref, wqb_ref, wkvb_ref,
             dlate_ref, dgq_ref, dgkv_ref, dwqb_ref, dwkvb_ref, dqbrt_ref, dkvbt_ref):
        @pl.when(pl.program_id(0) == 0)
        def _():
            dwqb_ref[...] = jnp.zeros_like(dwqb_ref)
            dwkvb_ref[...] = jnp.zeros_like(dwkvb_ref)

        ct, s1t, s2t = ct_ref[...], s1t_ref[...], s2t_ref[...]
        dk_sum_t = jnp.zeros((SLAB, tm), F32)
        for hd in range(N_HEADS):
            sl = slice(hd * SLAB, (hd + 1) * SLAB)
            dqbrt_ref[sl, :] = _rope_t_bwd(dqt_ref[sl, :] * SCALE_B, ct, s1t, s2t).astype(BF16)
            dk_sum_t += dkt_ref[sl, :].astype(F32)
        dkvbt_ref[0:HM, :] = dkt_ref[...]
        dkvbt_ref[HM:2 * HM, :] = dvt_ref[...]
        dkr = _rope_t_bwd(dk_sum_t, ct, s1t, s2t).T
        dwqb_ref[...] += _dot(dqbrt_ref[...], cqn_ref[...])
        dwkvb_ref[...] += _dot(dkvbt_ref[...], ckvn_ref[...])
        dcqn = _dot(wqb_ref[...], dqbrt_ref[...]).T
        cq = cq_ref[...]
        rq = _rms_r(cq)
        _acc_rows(dgq_ref, dcqn * cq * rq)
        dcq = _rms_bwd(cq, rq, gq_ref[...], dcqn)
        dckvn = _dot(wkvb_ref[...], dkvbt_ref[...]).T
        ckv = ckv_ref[...]
        rkv = _rms_r(ckv)
        _acc_rows(dgkv_ref, dckvn * ckv * rkv)
        dckv = _rms_bwd(ckv, rkv, gkv_ref[...], dckvn)
        dlate_ref[:, 0:C_CKV - C_CQ] = dcq.astype(BF16)
        dlate_ref[:, C_CKV - C_CQ:C_KR - C_CQ] = dckv.astype(BF16)
        dlate_ref[:, C_KR - C_CQ:D_IN_PAD - C_CQ] = dkr.astype(BF16)

    hmt = _col_spec(HM, tm)
    tab = _col_spec(SLAB, tm)
    return pl.pallas_call(
        body, name="latent_bwd", grid=(t // tm,),
        in_specs=[hmt, hmt, hmt,
                  _row_spec(tm, Q_LORA), _row_spec(tm, KV_LORA), _row_spec(tm, Q_LORA), _row_spec(tm, KV_LORA),
                  tab, tab, tab, _full_spec((1, Q_LORA)), _full_spec((1, KV_LORA)),
                  _full_spec((Q_LORA, HM)), _full_spec((KV_LORA, 2 * HM))],
        out_specs=[_row_spec(tm, D_IN_PAD - C_CQ), _full_spec((1, Q_LORA)), _full_spec((1, KV_LORA)),
                   _full_spec((HM, Q_LORA)), _full_spec((2 * HM, KV_LORA))],
        out_shape=[jax.ShapeDtypeStruct((t, D_IN_PAD - C_CQ), BF16),
                   jax.ShapeDtypeStruct((1, Q_LORA), F32), jax.ShapeDtypeStruct((1, KV_LORA), F32),
                   jax.ShapeDtypeStruct((HM, Q_LORA), F32), jax.ShapeDtypeStruct((2 * HM, KV_LORA), F32)],
        scratch_shapes=[pltpu.VMEM((HM, tm), BF16), pltpu.VMEM((2 * HM, tm), BF16)],
        compiler_params=_params(("arbitrary",)),
    )(dqb_t, dkb_t, dvb_t, cq, ckv, cqn, ckvn, rope_ct, rope_s1t, rope_s2t, g_q, g_kv, w_qb, w_kvb)


def _inproj_bwd(dgates, dqkv, dlate, x, dx1, g1, w_in, grad_slices, only, small):
    t = x.shape[0]
    tm = _wide_token_tile(t)
    n_ex = len(grad_slices)
    zeroed = [a for a in range(n_ex) if only[a] is not None]

    def body(dgates_ref, dqkv_ref, dlate_ref, x_ref, dx1_ref, g1_ref, win_ref, *rest):
        slice_refs, small_ref = rest[:n_ex], rest[n_ex]
        dx_ref = rest[n_ex + 1]
        part_refs = rest[n_ex + 2:2 * n_ex + 2]
        small_dst, dg1_dst = rest[2 * n_ex + 2:2 * n_ex + 4]
        dproj_ref, dg1_ref, send_sems, recv_sems, local_sems = rest[2 * n_ex + 4:2 * n_ex + 9]
        zero_refs, zero_sem = rest[2 * n_ex + 9:-1], rest[-1]
        sems = (send_sems, recv_sems, local_sems)

        def slice_copies():
            return _direct_copies(slice_refs, part_refs, *sems, False, only=only)

        def small_copies():
            return _direct_copies([small_ref], [small_dst], *sems, True, sem_base=n_ex)

        @pl.when(pl.program_id(0) == 0)
        def _():
            _start_copies(*slice_copies())
            _start_copies(*small_copies())
            x_, y_, c_ = _mesh_pos()
            me = 4 * x_ + 2 * y_ + c_
            for a, z_ref in zip(zeroed, zero_refs):
                outside = me != only[a][0]
                for d in only[a][1:]:
                    outside = jnp.logical_and(outside, me != d)

                @pl.when(outside)
                def _():
                    z_ref[...] = jnp.zeros_like(z_ref)
                    fills = [pltpu.make_async_copy(z_ref, part_refs[a].at[k], zero_sem.at[k])
                             for k in range(N_DEV)]
                    for cp in fills:
                        cp.start()
                    for cp in fills:
                        cp.wait()

        dproj_ref[:, C_GATES:C_QA] = dgates_ref[...]
        dproj_ref[:, C_QA:C_CQ] = dqkv_ref[...]
        dproj_ref[:, C_CQ:D_IN_PAD] = dlate_ref[...]
        dh = _dot_nt(dproj_ref[...], win_ref[...])
        xv = x_ref[...]
        r1 = _rms_r(xv)
        _acc_rows(dg1_ref, dh * xv * r1)
        dx_ref[...] = dx1_ref[...] + _rms_bwd(xv, r1, g1_ref[...], dh)

        @pl.when(pl.program_id(0) == t // tm - 1)
        def _():
            gain_copies = _direct_copies([dg1_ref], [dg1_dst], *sems, True, sem_base=n_ex + 1)
            _start_copies(*gain_copies)
            _wait_copies(*slice_copies())
            _wait_copies(*small_copies())
            _wait_copies(*gain_copies)

    kvw = N_KV_A * SLAB
    row = _row_spec(tm, D_MODEL)
    outs = pl.pallas_call(
        body, name="inproj_bwd", grid=(t // tm,),
        in_specs=[_row_spec(tm, 2 * D_MODEL), _row_spec(tm, HM + 2 * kvw), _row_spec(tm, D_IN_PAD - C_CQ),
                  row, row, _full_spec((1, D_MODEL)), _full_spec((D_MODEL, D_IN_PAD))]
        + [ANY_SPEC] * (n_ex + 1),
        out_specs=[row] + [ANY_SPEC] * (n_ex + 2),
        out_shape=[jax.ShapeDtypeStruct((t, D_MODEL), F32)]
        + [jax.ShapeDtypeStruct(a.shape, a.dtype) for a in grad_slices]
        + [jax.ShapeDtypeStruct((N_DEV,) + small.shape, F32), jax.ShapeDtypeStruct((N_DEV, 1, D_MODEL), F32)],
        scratch_shapes=[pltpu.VMEM((tm, D_IN_PAD), BF16), pltpu.VMEM((1, D_MODEL), F32)]
        + _exchange_scratch(n_ex + 2)
        + [pltpu.VMEM(grad_slices[a].shape[1:], grad_slices[a].dtype) for a in zeroed]
        + [pltpu.SemaphoreType.DMA((N_DEV,))],
        compiler_params=_params(("arbitrary",)),
    )(dgates, dqkv, dlate, x, dx1, g1, w_in, *grad_slices, small)
    return outs[0], list(outs[1:n_ex + 1]), outs[n_ex + 1], outs[n_ex + 2]


def _matmul_tn(a, b, name, out_dtype=F32, n_shards=1):
    t, k = a.shape
    n = b.shape[1]
    bn = min(n, MATMUL_COLS)
    bt = min(t, MATMUL_TOKENS)
    bk = min(k, MATMUL_ACC_ELEMS // bn)
    ns = n // n_shards
    per_block = bn // ns
    steps = t // bt

    def body(a_ref, b_ref, o_ref, acc):
        s = pl.program_id(2)

        @pl.when(s == 0)
        def _():
            acc[...] = jnp.zeros_like(acc)

        acc[...] += _dot_tn(a_ref[...], b_ref[...])

        @pl.when(s == steps - 1)
        def _():
            if n_shards > 1:
                for p in range(per_block):
                    o_ref[p] = acc[:, p * ns:(p + 1) * ns].astype(out_dtype)
            else:
                o_ref[...] = acc[...].astype(out_dtype)

    if n_shards > 1:
        out_spec = pl.BlockSpec((per_block, bk, ns), lambda i, j, s: (j, i, 0))
        out_shape = jax.ShapeDtypeStruct((n_shards, k, ns), out_dtype)
    else:
        out_spec = pl.BlockSpec((bk, bn), lambda i, j, s: (i, j))
        out_shape = jax.ShapeDtypeStruct((k, n), out_dtype)
    return pl.pallas_call(
        body, name=name, grid=(k // bk, n // bn, steps),
        in_specs=[pl.BlockSpec((bt, bk), lambda i, j, s: (s, i)), pl.BlockSpec((bt, bn), lambda i, j, s: (s, j))],
        out_specs=out_spec, out_shape=out_shape, scratch_shapes=[pltpu.VMEM((bk, bn), F32)],
        compiler_params=_params(("parallel", "parallel", "arbitrary")),
    )(a, b)


def _two_level_gather(srcs, dsts, send_sems, recv_sems, local_sems):
    n = len(srcs)
    x, y, c = _mesh_pos()
    me, sibling = (x, y, c), (x, y, 1 - c)
    chips = [(1 - x, y), (x, 1 - y), (1 - x, 1 - y)]

    def slot(a, px, py, pc):
        return dsts[a].at[4 * px + 2 * py + pc]

    def copy(a, k, block, to, src=None):
        return pltpu.make_async_remote_copy(
            src_ref=slot(a, *block) if src is None else src, dst_ref=slot(a, *block),
            send_sem=send_sems.at[(N_DEV - 1) * a + k], recv_sem=recv_sems.at[(N_DEV - 1) * a + k],
            device_id=to, device_id_type=pl.DeviceIdType.MESH)

    def own_copies():
        mine = [pltpu.make_async_copy(srcs[a], slot(a, *me), local_sems.at[a]) for a in range(n)]
        first = []
        for a in range(n):
            first.append(copy(a, 0, me, sibling, src=srcs[a]))
            first += [copy(a, 1 + j, me, (*chip, c), src=srcs[a]) for j, chip in enumerate(chips)]
        return mine, first

    def start():
        mine, first = own_copies()
        for cp in mine + first:
            cp.start()

    def finish():
        mine, first = own_copies()
        passed = []
        for j, chip in enumerate(chips):
            for a in range(n):
                copy(a, 1 + j, (*chip, c), me).wait_recv()
                passed.append(copy(a, 4 + j, (*chip, c), sibling))
                passed[-1].start()
        for a in range(n):
            copy(a, 0, sibling, me).wait_recv()
        for j, chip in enumerate(chips):
            for a in range(n):
                copy(a, 4 + j, (*chip, 1 - c), me).wait_recv()
        for cp in first + passed:
            cp.wait_send()
        for cp in mine:
            cp.wait()

    return start, finish


def _adamw(parts, w, m, v, name):
    n_parts = len(parts)
    _, k, n = parts[0].shape
    bk = min(k, ADAM_ROWS)
    c1 = 1.0 - ADAM_B1 ** ADAM_STEP
    c2 = 1.0 - ADAM_B2 ** ADAM_STEP

    def body(*refs):
        p_refs, (w_ref, m_ref, v_ref, g_ref, d_ref, mo_ref, vo_ref) = refs[:n_parts], refs[n_parts:]
        g = p_refs[0][0].astype(F32)
        for p_ref in p_refs:
            for s in range(N_DEV):
                if p_ref is not p_refs[0] or s > 0:
                    g = g + p_ref[s].astype(F32)
        g_ref[0] = g
        m_new = ADAM_B1 * m_ref[0] + (1.0 - ADAM_B1) * g
        v_new = ADAM_B2 * v_ref[0] + (1.0 - ADAM_B2) * (g * g)
        mo_ref[0] = m_new
        vo_ref[0] = v_new
        m_hat = m_new / c1
        v_hat = v_new / c2
        d_ref[0] = -ADAM_LR * (m_hat / (jnp.sqrt(v_hat) + ADAM_EPS) + ADAM_WD * w_ref[0])

    blk = pl.BlockSpec((1, bk, n), lambda i: (0, i, 0))
    out = jax.ShapeDtypeStruct((1, k, n), F32)
    return pl.pallas_call(
        body, name=name, grid=(k // bk,),
        in_specs=[pl.BlockSpec((N_DEV, bk, n), lambda i: (0, i, 0))] * n_parts + [blk, blk, blk],
        out_specs=[blk] * 4, out_shape=[out] * 4,
        compiler_params=_params(("parallel",)),
    )(*parts, w, m, v)


def _adamw_small(parts, w, m, v):
    k = len(SMALL_LAYOUT)
    c1 = 1.0 - ADAM_B1 ** ADAM_STEP
    c2 = 1.0 - ADAM_B2 ** ADAM_STEP

    def body(p_ref, *refs):
        w_refs, m_refs, v_refs, outs = refs[:k], refs[k:2 * k], refs[2 * k:3 * k], refs[3 * k:]
        total = p_ref[0]
        for s in range(1, N_DEV):
            total = total + p_ref[s]
        for i, (_, row, off, width) in enumerate(SMALL_LAYOUT):
            g = total[row:row + 1, off:off + width]
            m_new = ADAM_B1 * m_refs[i][...] + (1.0 - ADAM_B1) * g
            v_new = ADAM_B2 * v_refs[i][...] + (1.0 - ADAM_B2) * (g * g)
            outs[4 * i][...] = g
            outs[4 * i + 1][...] = -ADAM_LR * ((m_new / c1) / (jnp.sqrt(v_new / c2) + ADAM_EPS)
                                               + ADAM_WD * w_refs[i][...])
            outs[4 * i + 2][...] = m_new
            outs[4 * i + 3][...] = v_new
        outs[4 * k][...] = total[SMALL_LOSS_ROW:SMALL_LOSS_ROW + 1, SMALL_LOSS_OFF:SMALL_LOSS_OFF + 1]

    names = [name for name, *_ in SMALL_LAYOUT]
    out_shape = [jax.ShapeDtypeStruct(w[name].shape, F32) for name in names for _ in range(4)]
    outs = pl.pallas_call(
        body, name="adamw_small", out_shape=out_shape + [jax.ShapeDtypeStruct((1, 1), F32)],
    )(parts, *[w[n] for n in names], *[m[n] for n in names], *[v[n] for n in names])
    return {name: tuple(outs[4 * i:4 * i + 4]) for i, name in enumerate(names)}, outs[4 * k]


def _pad_heads_cols(w, heads, width):
    k = w.shape[0]
    w = w.reshape(k, heads, width)
    return jnp.pad(w, ((0, 0), (0, 0), (0, SLAB - width))).reshape(k, heads * SLAB)


def _unpad_heads_cols(w, heads, width):
    k = w.shape[0]
    return w.reshape(k, heads, SLAB)[:, :, :width].reshape(k, heads * width)


def _pad_heads_rows(w, heads, width):
    n = w.shape[1]
    w = w.reshape(heads, width, n)
    return jnp.pad(w, ((0, 0), (0, SLAB - width), (0, 0))).reshape(heads * SLAB, n)


def _unpad_heads_rows(w, heads, width):
    n = w.shape[1]
    return w.reshape(heads, SLAB, n)[:, :width, :].reshape(heads * width, n)


def _pad_w_in(w_in):
    o = 2 * D_MODEL
    qa = _pad_heads_cols(w_in[:, o:o + 512], N_HEADS, HEAD_A)
    ka = _pad_heads_cols(w_in[:, o + 512:o + 640], N_KV_A, HEAD_A)
    va = _pad_heads_cols(w_in[:, o + 640:o + 768], N_KV_A, HEAD_A)
    kr = jnp.pad(w_in[:, o + 1152:o + 1184], ((0, 0), (QK_NOPE, SLAB - QK_NOPE - QK_ROPE)))
    return jnp.concatenate([w_in[:, :o], qa, ka, va, w_in[:, o + 768:o + 1152], kr], axis=1)


def _unpad_w_in(w):
    qa = _unpad_heads_cols(w[:, C_QA:C_KA], N_HEADS, HEAD_A)
    ka = _unpad_heads_cols(w[:, C_KA:C_VA], N_KV_A, HEAD_A)
    va = _unpad_heads_cols(w[:, C_VA:C_CQ], N_KV_A, HEAD_A)
    kr = w[:, C_KR + QK_NOPE:C_KR + QK_NOPE + QK_ROPE]
    return jnp.concatenate([w[:, :C_QA], qa, ka, va, w[:, C_CQ:C_KR], kr], axis=1)


def _pad_w_kvb(w_kvb):
    w = w_kvb.reshape(KV_LORA, N_HEADS, QK_NOPE + V_DIM_B)
    k = jnp.pad(w[:, :, :QK_NOPE], ((0, 0), (0, 0), (0, SLAB - QK_NOPE))).reshape(KV_LORA, HM)
    v = jnp.pad(w[:, :, QK_NOPE:], ((0, 0), (0, 0), (0, SLAB - V_DIM_B))).reshape(KV_LORA, HM)
    return jnp.concatenate([k, v], axis=1)


def _unpad_w_kvb(w):
    k = w[:, :HM].reshape(KV_LORA, N_HEADS, SLAB)[:, :, :QK_NOPE]
    v = w[:, HM:].reshape(KV_LORA, N_HEADS, SLAB)[:, :, :V_DIM_B]
    return jnp.concatenate([k, v], axis=2).reshape(KV_LORA, N_HEADS * (QK_NOPE + V_DIM_B))


def _col_shards(w):
    k, n = w.shape
    ns = n // N_DEV
    if ns % SLAB:
        return jnp.stack([w[:, d * ns:(d + 1) * ns] for d in range(N_DEV)])
    return w.reshape(k, N_DEV, ns).transpose(1, 0, 2)


def _from_col_shards(s):
    _, k, ns = s.shape
    if ns % SLAB:
        return jnp.concatenate([s[d] for d in range(N_DEV)], axis=1)
    return s.transpose(1, 0, 2).reshape(k, N_DEV * ns)


def _freq_row():
    freqs = ROPE_THETA ** (-jnp.arange(0, QK_ROPE, 2, dtype=F32) / QK_ROPE)
    return jnp.concatenate([jnp.zeros((QK_NOPE,), F32), freqs, freqs,
                            jnp.zeros((SLAB - QK_NOPE - QK_ROPE,), F32)]).reshape(1, SLAB)


SMALL_D_ROWS = ("pre_norm_mix", "post_norm_mix", "pre_norm_mlp", "post_norm_mlp")
SMALL_LAYOUT = tuple((name, i, 0, D_MODEL) for i, name in enumerate(SMALL_D_ROWS)) + (
    ("q_a_norm", 4, 0, Q_LORA), ("kv_a_norm", 4, 256, KV_LORA), ("sinks", 4, 384, N_HEADS))
SMALL_LOSS_ROW, SMALL_LOSS_OFF = 4, 512


def _pack_small(vals):
    row4 = jnp.concatenate([vals["q_a_norm"].reshape(-1), vals["kv_a_norm"].reshape(-1), vals["sinks"].reshape(-1),
                            jnp.zeros((SMALL_LOSS_OFF - 392,), F32), vals["loss"].reshape(-1),
                            jnp.zeros((1024 - SMALL_LOSS_OFF - 1,), F32)])
    rows = [vals[n].reshape(1024) for n in SMALL_D_ROWS] + [row4]
    return jnp.concatenate([jnp.stack(rows), jnp.zeros((SMALL_ROWS - 5, 1024), F32)], axis=0)


WEIGHT_ORDER = ("pre_norm_mix", "w_in", "q_a_norm", "w_q_b", "kv_a_norm", "w_kv_b", "sinks", "w_o_a", "w_o_b",
                "w_out", "post_norm_mix", "pre_norm_mlp", "w_up", "w_down", "post_norm_mlp")


def kernel(x, positions, pre_norm_mix, w_in, q_a_norm, w_q_b, kv_a_norm, w_kv_b, sinks, w_o_a, w_o_b, w_out, post_norm_mix, pre_norm_mlp, w_up, w_down, post_norm_mlp, loss_target, m_pre_norm_mix, m_w_in, m_q_a_norm, m_w_q_b, m_kv_a_norm, m_w_kv_b, m_sinks, m_w_o_a, m_w_o_b, m_w_out, m_post_norm_mix, m_pre_norm_mlp, m_w_up, m_w_down, m_post_norm_mlp, v_pre_norm_mix, v_w_in, v_q_a_norm, v_w_q_b, v_kv_a_norm, v_w_kv_b, v_sinks, v_w_o_a, v_w_o_b, v_w_out, v_post_norm_mix, v_pre_norm_mlp, v_w_up, v_w_down, v_post_norm_mlp):
    weights = dict(pre_norm_mix=pre_norm_mix, w_in=w_in, q_a_norm=q_a_norm, w_q_b=w_q_b, kv_a_norm=kv_a_norm,
                   w_kv_b=w_kv_b, sinks=sinks, w_o_a=w_o_a, w_o_b=w_o_b, w_out=w_out, post_norm_mix=post_norm_mix,
                   pre_norm_mlp=pre_norm_mlp, w_up=w_up, w_down=w_down, post_norm_mlp=post_norm_mlp)
    m_in = dict(pre_norm_mix=m_pre_norm_mix, w_in=m_w_in, q_a_norm=m_q_a_norm, w_q_b=m_w_q_b, kv_a_norm=m_kv_a_norm,
                w_kv_b=m_w_kv_b, sinks=m_sinks, w_o_a=m_w_o_a, w_o_b=m_w_o_b, w_out=m_w_out,
                post_norm_mix=m_post_norm_mix, pre_norm_mlp=m_pre_norm_mlp, w_up=m_w_up, w_down=m_w_down,
                post_norm_mlp=m_post_norm_mlp)
    v_in = dict(pre_norm_mix=v_pre_norm_mix, w_in=v_w_in, q_a_norm=v_q_a_norm, w_q_b=v_w_q_b, kv_a_norm=v_kv_a_norm,
                w_kv_b=v_w_kv_b, sinks=v_sinks, w_o_a=v_w_o_a, w_o_b=v_w_o_b, w_out=v_w_out,
                post_norm_mix=v_post_norm_mix, pre_norm_mlp=v_pre_norm_mlp, w_up=v_w_up, w_down=v_w_down,
                post_norm_mlp=v_post_norm_mlp)

    xs, pos, target = x[0], positions[0], loss_target[0]
    t = xs.shape[0]
    pos_col = pos.reshape(t, 1)
    pos_row = pos.reshape(1, t)
    g1, g2, g3, g4 = (weights[n] for n in SMALL_D_ROWS)
    g_q, g_kv = q_a_norm, kv_a_norm
    sink_vec = sinks.reshape(N_HEADS)
    shard = {n: weights[n][0].astype(BF16) for n in EARLY + LATE}

    tables, (e_in, e_qb, e_kvb) = _rope_tables(pos_col, _freq_row(), [shard[n] for n in EARLY])
    w_in_p = _pad_w_in(_from_col_shards(e_in))
    w_qb = _pad_heads_cols(_from_col_shards(e_qb), N_HEADS, QK_NOPE + QK_ROPE)
    w_kvb = _pad_w_kvb(_from_col_shards(e_kvb))

    (h, gates, qa, ka, va, cq, ckv, cqn, ckvn, kb, vb, qt, kt, vt) = _inproj_fwd(
        xs, g1, w_in_p, g_q, g_kv, w_kvb, w_qb.T, w_kvb[:, :HM].T, w_kvb[:, HM:].T, w_in_p[:, C_KR:].T, tables)
    out_a, lse_a = _swa_fwd(qa, ka, va, pos_col, pos_row, sink_vec)
    out_b, out_b_t, qt_lse, (l_oa, l_ob, l_out, w_up_s, l_down) = _mla_fwd(qt, kb, vt, [shard[n] for n in LATE])
    w_oa = _pad_heads_rows(_from_col_shards(l_oa), N_HEADS, HEAD_A)
    w_ob = _pad_heads_rows(_from_col_shards(l_ob), N_HEADS, V_DIM_B)
    w_out_f = l_out.reshape(D_MODEL, D_MODEL)
    w_down_f = l_down.reshape(D_FF, D_MODEL)

    oa_p, ob_p, merged, y, x1, h2 = _merge_fwd(out_a, out_b, gates, xs, w_oa, w_ob, w_out_f, g2, g3)
    a, du, dy2, dx1, loss, dg3, dg4 = _mlp_fwd_bwd(x1, h2, target, w_up_s, w_down_f, g3, g4)
    (dgates, d_oa, d_ob_t, dg2, dw_oa, dw_ob, dw_out) = _merge_bwd(
        dx1, y, gates, oa_p, ob_p, out_a, out_b, out_b_t, merged, w_oa, w_ob, w_out_f, g2)
    late_slices = [
        _col_shards(_unpad_heads_rows(dw_oa, N_HEADS, HEAD_A)).astype(BF16),
        _col_shards(_unpad_heads_rows(dw_ob, N_HEADS, V_DIM_B)).astype(BF16),
        dw_out.astype(BF16).reshape(N_DEV, D_MODEL // N_DEV, D_MODEL),
        _matmul_tn(h2, du, "dw_up", BF16, N_DEV),
        _matmul_tn(a, dy2, "dw_down", BF16).reshape(N_DEV, D_FF // N_DEV, D_MODEL),
    ]
    dqkv_a, dsink = _swa_bwd(qa, ka, va, out_a, d_oa, lse_a, pos_col, pos_row, sink_vec)
    dw_in_early = jnp.concatenate([_matmul_tn(h, dgates, "dw_in_gates", BF16),
                                   _matmul_tn(h, dqkv_a, "dw_in_mixer_a", BF16),
                                   jnp.zeros((D_MODEL, D_IN_PAD - C_CQ), BF16)], axis=1)
    late_slices.append(_col_shards(_unpad_w_in(dw_in_early)))
    dqb_t, dkb_t, dvb_t, late_parts = _mla_bwd(qt_lse, kb, kt, vb, d_ob_t, late_slices)
    w_in_early_parts = late_parts.pop()
    dproj_late, dgq, dgkv, dw_qb_t, dw_kvb_t = _latent_bwd(
        dqb_t, dkb_t, dvb_t, cq, ckv, cqn, ckvn, *tables[3:], g_q, g_kv, w_qb, w_kvb)
    dw_l = _matmul_tn(h, dproj_late, "dw_in_latents", BF16)
    late_cols = jnp.concatenate([dw_l[:, :Q_LORA + KV_LORA], dw_l[:, C_KR - C_CQ + QK_NOPE:C_KR - C_CQ + Q_HEAD_B]],
                                axis=1)
    shard_cols = w_in.shape[2]
    head = late_cols.shape[1] - shard_cols
    w_in_late = jnp.concatenate([
        jnp.zeros((N_DEV - 2, D_MODEL, shard_cols), BF16),
        jnp.pad(late_cols[:, :head], ((0, 0), (shard_cols - head, 0)))[None], late_cols[:, head:][None]])
    early_slices = [
        w_in_late,
        _col_shards(_unpad_heads_cols(dw_qb_t.T, N_HEADS, QK_NOPE + QK_ROPE)).astype(BF16),
        _col_shards(_unpad_w_kvb(dw_kvb_t.T)).astype(BF16),
    ]
    small_grads = {"pre_norm_mix": jnp.zeros((1, D_MODEL), F32), "post_norm_mix": dg2, "pre_norm_mlp": dg3,
                   "post_norm_mlp": dg4, "q_a_norm": dgq, "kv_a_norm": dgkv,
                   "sinks": dsink.reshape(N_HEADS, BLOCK).sum(axis=1), "loss": loss[0, 0:1]}
    dx, early_parts, s_parts, dg1_parts = _inproj_bwd(
        dgates, dqkv_a, dproj_late, xs, dx1, g1, w_in_p, early_slices,
        only=[(N_DEV - 2, N_DEV - 1), None, None], small=_pack_small(small_grads))
    s_parts = s_parts.at[:, SMALL_D_ROWS.index("pre_norm_mix"), :].set(dg1_parts[:, 0, :])

    updates = {}
    all_parts = [[w_in_early_parts, early_parts[0]]] + [[p] for p in early_parts[1:] + late_parts]
    for name, parts in zip(EARLY + LATE, all_parts):
        outs = _adamw(parts, weights[name], m_in[name], v_in[name], "adamw_" + name)
        for kind, arr in zip(("g", "d", "m", "v"), outs):
            updates[kind, name] = arr
    small_out, loss_sum = _adamw_small(s_parts, weights, m_in, v_in)
    for name, outs in small_out.items():
        for kind, arr in zip(("g", "d", "m", "v"), outs):
            updates[kind, name] = arr
    results = [updates[kind, name] for kind in ("g", "d", "m", "v") for name in WEIGHT_ORDER]
    return (loss_sum.reshape(()), dx[None], *results)
```

```python
import functools

import numpy as np
import jax
import jax.numpy as jnp
from jax import lax
from jax.experimental import pallas as pl
from jax.experimental.pallas import tpu as pltpu

F32 = jnp.float32
BF16 = jnp.bfloat16

D_MODEL = 1024
D_FF = 4096
N_HEADS = 8
N_KV_A = 2
GROUP_A = N_HEADS // N_KV_A
HEAD_A = 64
QK_NOPE = 64
QK_ROPE = 32
V_DIM_B = 64
Q_LORA = 256
KV_LORA = 128
BLOCK = 128
SLAB = 128
ROPE_THETA = 10000.0
EPS = 1e-6
N_DEV = 8
NEG = -1e30

SCALE_A = HEAD_A ** -0.5
SCALE_B = (QK_NOPE + QK_ROPE) ** -0.5
LOG2E = 1.4426950408889634
SCORE_B = SCALE_B * LOG2E
MLA_HEADS_PER_STEP = 4
MLA_FWD_HEADS_PER_STEP = 8
Q_HEAD_B = QK_NOPE + QK_ROPE
SWA_BLOCKS_PER_STEP = 8
ONES_ROWS = 16
SLOPES_A = tuple(2.0 ** (-8.0 * (h + 1) / N_HEADS) for h in range(N_HEADS))

ADAM_LR = 0.001
ADAM_B1 = 0.9
ADAM_B2 = 0.999
ADAM_EPS = 1e-08
ADAM_WD = 0.01
ADAM_STEP = 10

HM = N_HEADS * SLAB
C_GATES = 0
C_QA = 2 * D_MODEL
C_KA = C_QA + HM
C_VA = C_KA + N_KV_A * SLAB
C_CQ = C_VA + N_KV_A * SLAB
C_CKV = C_CQ + Q_LORA
C_KR = C_CKV + KV_LORA
D_IN_PAD = C_KR + SLAB

VMEM_LIMIT = 56 * 1024 * 1024
VMEM_LIMIT_MERGE_BWD = 60 * 1024 * 1024

EARLY = ("w_in", "w_q_b", "w_kv_b")
LATE = ("w_o_a", "w_o_b", "w_out", "w_up", "w_down")
ADAM_ROWS = 256
MATMUL_COLS = 2048
MATMUL_TOKENS = 2048
MATMUL_ACC_ELEMS = 2048 * 1024
SMALL_ROWS = 8


def _token_tile(t):
    return min(256, t)


def _wide_token_tile(t):
    return min(512, t)


def _attn_tile(t):
    return 512 if t >= 2048 else 128


def _params(sem, vmem=VMEM_LIMIT):
    return pltpu.CompilerParams(dimension_semantics=sem, vmem_limit_bytes=vmem)


def _dot(a, b):
    return jnp.dot(a, b, preferred_element_type=F32)


def _dot_nt(a, b):
    return lax.dot_general(a, b, (((1,), (1,)), ((), ())), preferred_element_type=F32)


def _dot_tn(a, b):
    return lax.dot_general(a, b, (((0,), (0,)), ((), ())), preferred_element_type=F32)


def _rms_r(x):
    return lax.rsqrt(jnp.mean(x * x, axis=-1, keepdims=True) + EPS)


def _rms_bwd(x, r, g, dy):
    t = dy * g
    return r * t - x * (r * r * r) * jnp.mean(x * t, axis=-1, keepdims=True)


def _sigmoid(x):
    return 1.0 / (1.0 + jnp.exp(-x))


def _rope(x, c, s1, s2):
    return x * c + pltpu.roll(x, SLAB - 16, 1) * s1 + pltpu.roll(x, 16, 1) * s2


def _rope_bwd(d, c, s1, s2):
    return d * c + pltpu.roll(d * s1, 16, 1) + pltpu.roll(d * s2, SLAB - 16, 1)


def _roll_rows(x, shift):
    return jnp.concatenate([x[-shift:], x[:-shift]], axis=0)


def _rope_t(x, c, s1, s2):
    return x * c + _roll_rows(x, SLAB - 16) * s1 + _roll_rows(x, 16) * s2


def _rope_t_bwd(d, c, s1, s2):
    return d * c + _roll_rows(d * s1, 16) + _roll_rows(d * s2, SLAB - 16)


def _plant_rows(slab, row, vals):
    hi = vals.astype(BF16).astype(F32)
    lo = (vals - hi).astype(BF16).astype(F32)
    idx = lax.broadcasted_iota(jnp.int32, slab.shape, 0)
    return jnp.where(idx == row, -hi, jnp.where(idx == row + 1, -lo, slab))


def _row_spec(tm, n):
    return pl.BlockSpec((tm, n), lambda i: (i, 0))


def _col_spec(n, tm):
    return pl.BlockSpec((n, tm), lambda i: (0, i))


def _full_spec(shape):
    nd = len(shape)
    return pl.BlockSpec(shape, lambda i: (0,) * nd, pipeline_mode=pl.Buffered(1))


def _acc_rows(ref, val):
    @pl.when(pl.program_id(0) == 0)
    def _():
        ref[...] = jnp.zeros_like(ref)
    ref[...] += jnp.sum(val, axis=0, keepdims=True)


def _rope_tables(pos_col, freq_row, early):
    t = pos_col.shape[0]
    tm = _token_tile(t)
    n = len(early)

    def body(pos_ref, f_ref, *rest):
        shard_refs, (c_ref, s1_ref, s2_ref, ct_ref, s1t_ref, s2t_ref) = rest[:n], rest[n:n + 6]
        start, finish = _two_level_gather(shard_refs, rest[n + 6:2 * n + 6], *rest[2 * n + 6:])
        pl.when(pl.program_id(0) == 0)(start)
        ang = pos_ref[...].astype(F32) * f_ref[...]
        lane = lax.broadcasted_iota(jnp.int32, ang.shape, 1)
        s = jnp.sin(ang)
        c = jnp.cos(ang)
        s1 = jnp.where((lane >= 64) & (lane < 80), -s, 0.0)
        s2 = jnp.where((lane >= 80) & (lane < 96), s, 0.0)
        c_ref[...], s1_ref[...], s2_ref[...] = c, s1, s2
        ct_ref[...], s1t_ref[...], s2t_ref[...] = c.T, s1.T, s2.T
        pl.when(pl.program_id(0) == t // tm - 1)(finish)

    tab = jax.ShapeDtypeStruct((t, SLAB), F32)
    tabt = jax.ShapeDtypeStruct((SLAB, t), F32)
    outs = pl.pallas_call(
        body, name="rope_tables", grid=(t // tm,),
        in_specs=[_row_spec(tm, 1), _full_spec((1, SLAB))] + [ANY_SPEC] * n,
        out_specs=[_row_spec(tm, SLAB)] * 3 + [_col_spec(SLAB, tm)] * 3 + [ANY_SPEC] * n,
        out_shape=[tab] * 3 + [tabt] * 3 + [jax.ShapeDtypeStruct((N_DEV,) + a.shape, a.dtype) for a in early],
        scratch_shapes=_exchange_scratch(n),
        compiler_params=_params(("arbitrary",)),
    )(pos_col, freq_row, *early)
    return outs[:6], outs[6:]


def _inproj_fwd(x, g1, w_in, g_q, g_kv, w_kvb, w_qb_t, w_kb_t, w_vb_t, w_kr_t, tables):
    t = x.shape[0]
    tm = _wide_token_tile(t)

    def body(x_ref, g1_ref, win_ref, gq_ref, gkv_ref, wkvb_ref, wqbt_ref, wkbt_ref, wvbt_ref, wkrt_ref,
             c_ref, s1_ref, s2_ref, ct_ref, s1t_ref, s2t_ref,
             h_ref, gates_ref, qa_ref, ka_ref, va_ref, cq_ref, ckv_ref, cqn_ref, ckvn_ref,
             kb_ref, vb_ref, qt_ref, kt_ref, vt_ref):
        xv = x_ref[...]
        h = (xv * _rms_r(xv) * g1_ref[...]).astype(BF16)
        h_ref[...] = h
        proj = _dot(h, win_ref[...])
        gates_ref[...] = proj[:, C_GATES:C_QA].astype(BF16)
        qa_ref[...] = proj[:, C_QA:C_KA].astype(BF16)
        ka_ref[...] = proj[:, C_KA:C_VA].astype(BF16)
        va_ref[...] = proj[:, C_VA:C_CQ].astype(BF16)
        cq = proj[:, C_CQ:C_CKV]
        ckv = proj[:, C_CKV:C_KR]
        kr = proj[:, C_KR:D_IN_PAD]
        cq_ref[...] = cq
        ckv_ref[...] = ckv
        cqn = (cq * _rms_r(cq) * gq_ref[...]).astype(BF16)
        ckvn = (ckv * _rms_r(ckv) * gkv_ref[...]).astype(BF16)
        cqn_ref[...] = cqn
        ckvn_ref[...] = ckvn
        c, s1, s2 = c_ref[...], s1_ref[...], s2_ref[...]
        kvb = _dot(ckvn, wkvb_ref[...])
        kr_rot = _rope(kr, c, s1, s2)
        ct, s1t, s2t = ct_ref[...], s1t_ref[...], s2t_ref[...]
        q_t = _dot_nt(wqbt_ref[...], cqn)
        k_t = _dot_nt(wkbt_ref[...], ckvn)
        kr_t = _rope_t(_dot_nt(wkrt_ref[...], h), ct, s1t, s2t)
        k_lane = lax.broadcasted_iota(jnp.int32, (1, SLAB), 1)
        k_ones = jnp.where((k_lane == Q_HEAD_B) | (k_lane == Q_HEAD_B + 1), 1.0, 0.0)
        for hd in range(N_HEADS):
            sl = slice(hd * SLAB, (hd + 1) * SLAB)
            kb_ref[:, sl] = (kvb[:, sl] + kr_rot + k_ones).astype(BF16)
            qt_ref[sl, :] = (_rope_t(q_t[sl, :], ct, s1t, s2t) * SCORE_B).astype(BF16)
            kt_ref[sl, :] = (k_t[sl, :] + kr_t).astype(BF16)
        v_lane = lax.broadcasted_iota(jnp.int32, (1, HM), 1) & (SLAB - 1)
        v_ones = jnp.where((v_lane == V_DIM_B) | (v_lane == V_DIM_B + 1), 1.0, 0.0)
        vb_ref[...] = (kvb[:, HM:2 * HM] + v_ones).astype(BF16)
        pad_row = lax.broadcasted_iota(jnp.int32, (HM, 1), 0) & (SLAB - 1)
        ones_rows = jnp.where((pad_row >= V_DIM_B) & (pad_row < V_DIM_B + ONES_ROWS), 1.0, 0.0)
        vt_ref[...] = (_dot_nt(wvbt_ref[...], ckvn) + ones_rows).astype(BF16)

    def sds(n, dt):
        return jax.ShapeDtypeStruct((t, n), dt)

    outs = [(D_MODEL, BF16), (2 * D_MODEL, BF16), (HM, BF16), (N_KV_A * SLAB, BF16), (N_KV_A * SLAB, BF16),
            (Q_LORA, F32), (KV_LORA, F32), (Q_LORA, BF16), (KV_LORA, BF16), (HM, BF16), (HM, BF16)]
    tab, tabt = _row_spec(tm, SLAB), _col_spec(SLAB, tm)
    return pl.pallas_call(
        body, name="inproj_fwd", grid=(t // tm,),
        in_specs=[_row_spec(tm, D_MODEL), _full_spec((1, D_MODEL)), _full_spec((D_MODEL, D_IN_PAD)),
                  _full_spec((1, Q_LORA)), _full_spec((1, KV_LORA)), _full_spec((KV_LORA, 2 * HM)),
                  _full_spec((HM, Q_LORA)), _full_spec((HM, KV_LORA)), _full_spec((HM, KV_LORA)),
                  _full_spec((SLAB, D_MODEL)), tab, tab, tab, tabt, tabt, tabt],
        out_specs=[_row_spec(tm, n) for n, _ in outs] + [_col_spec(HM, tm)] * 3,
        out_shape=[sds(n, dt) for n, dt in outs] + [jax.ShapeDtypeStruct((HM, t), BF16)] * 3,
        compiler_params=_params(("parallel",)),
    )(x, g1, w_in, g_q, g_kv, w_kvb, w_qb_t, w_kb_t, w_vb_t, w_kr_t, *tables)


def _tile_group(a):
    return jnp.concatenate([a] * GROUP_A, axis=1)


def _swa_masks():
    row = lax.broadcasted_iota(jnp.int32, (BLOCK, GROUP_A * BLOCK), 0)
    col = lax.broadcasted_iota(jnp.int32, (BLOCK, GROUP_A * BLOCK), 1) & (BLOCK - 1)
    return row <= col, row > col


def _heads_beside(ref, g, rows=slice(None)):
    return jnp.concatenate([ref[rows, (g * GROUP_A + hh) * SLAB:(g * GROUP_A + hh + 1) * SLAB].T
                            for hh in range(GROUP_A)], axis=1)


def _rows_beside(ref, g, cols=slice(None)):
    return jnp.concatenate([ref[g * GROUP_A + hh, :, cols] for hh in range(GROUP_A)], axis=1)


def _swa_rows(sinks):
    slopes = jnp.repeat(jnp.asarray(SLOPES_A, F32).reshape(N_KV_A, GROUP_A, 1), BLOCK, axis=2)
    sink_rows = jnp.repeat(sinks.reshape(N_KV_A, GROUP_A, 1), BLOCK, axis=2)
    return slopes.reshape(N_KV_A, 1, GROUP_A * BLOCK), sink_rows.reshape(N_KV_A, 1, GROUP_A * BLOCK)


def _swa_fwd(qa, ka, va, pos_col, pos_row, sinks):
    t = qa.shape[0]
    sub = min(SWA_BLOCKS_PER_STEP, t // BLOCK)
    tb = sub * BLOCK
    gw = GROUP_A * BLOCK
    slope_rows, sink_rows = _swa_rows(sinks)

    def body(q_ref, k_ref, kp_ref, v_ref, vp_ref, pk_ref, pkp_ref, pq_ref, slope_ref, sink_ref, o_ref, l_ref):
        i = pl.program_id(0)
        mask_c, older = _swa_masks()
        for s in range(sub):
            rows = slice(s * BLOCK, (s + 1) * BLOCK)
            before = slice((s - 1) * BLOCK, s * BLOCK)
            if s == 0:
                k_p, v_p, pk_p, mask_p = kp_ref, vp_ref, pkp_ref[...], jnp.logical_and(older, i > 0)
                before = slice(None)
            else:
                k_p, v_p, pk_p, mask_p = k_ref, v_ref, pk_ref[before, :], older
            pq = pq_ref[:, rows]
            dist_c = _tile_group(jnp.abs(pk_ref[rows, :] - pq).astype(F32))
            dist_p = _tile_group(jnp.abs(pk_p - pq).astype(F32))
            raw = []
            for g in range(N_KV_A):
                gs = slice(g * SLAB, (g + 1) * SLAB)
                x = _heads_beside(q_ref, g, rows)
                raw.append((_dot(k_ref[rows, gs], x), _dot(k_p[before, gs], x)))
            for g in range(N_KV_A):
                gs = slice(g * SLAB, (g + 1) * SLAB)
                slope, sink = slope_ref[g], sink_ref[g]
                s_c = jnp.where(mask_c, raw[g][0] * SCALE_A - slope * dist_c, NEG)
                s_p = jnp.where(mask_p, raw[g][1] * SCALE_A - slope * dist_p, NEG)
                m = jnp.maximum(jnp.maximum(jnp.max(s_c, axis=0, keepdims=True),
                                            jnp.max(s_p, axis=0, keepdims=True)), sink)
                e_c = jnp.exp(s_c - m)
                e_p = jnp.exp(s_p - m)
                den = (jnp.sum(e_c, axis=0, keepdims=True) + jnp.sum(e_p, axis=0, keepdims=True)
                       + jnp.exp(sink - m))
                inv = 1.0 / den
                ot = (_dot_tn(v_ref[rows, gs], (e_c * inv).astype(BF16))
                      + _dot_tn(v_p[before, gs], (e_p * inv).astype(BF16)))
                lse = m + jnp.log(den)
                for hh in range(GROUP_A):
                    hd = g * GROUP_A + hh
                    seg = slice(hh * BLOCK, (hh + 1) * BLOCK)
                    o_ref[rows, hd * SLAB:(hd + 1) * SLAB] = ot[:, seg].T.astype(BF16)
                    l_ref[hd, :, rows] = lse[:, seg]

    cur = lambda i: (i, 0)
    prev = lambda i: (jnp.maximum(sub * i - 1, 0), 0)
    kvw = N_KV_A * SLAB
    rows = pl.BlockSpec((N_KV_A, 1, gw), lambda i: (0, 0, 0))
    return pl.pallas_call(
        body, name="swa_fwd", grid=(t // tb,),
        in_specs=[pl.BlockSpec((tb, HM), cur),
                  pl.BlockSpec((tb, kvw), cur), pl.BlockSpec((BLOCK, kvw), prev),
                  pl.BlockSpec((tb, kvw), cur), pl.BlockSpec((BLOCK, kvw), prev),
                  pl.BlockSpec((tb, 1), cur), pl.BlockSpec((BLOCK, 1), prev),
                  pl.BlockSpec((1, tb), lambda i: (0, i)), rows, rows],
        out_specs=[pl.BlockSpec((tb, HM), cur), pl.BlockSpec((N_HEADS, 1, tb), lambda i: (0, 0, i))],
        out_shape=[jax.ShapeDtypeStruct((t, HM), BF16), jax.ShapeDtypeStruct((N_HEADS, 1, t), F32)],
        compiler_params=_params(("parallel",)),
    )(qa, ka, ka, va, va, pos_col, pos_col, pos_row, slope_rows, sink_rows)


def _swa_bwd(qa, ka, va, out_a, d_oa, lse, pos_col, pos_row, sinks):
    t = qa.shape[0]
    nb = t // BLOCK
    sub = min(SWA_BLOCKS_PER_STEP, nb)
    tb = sub * BLOCK
    steps = t // tb
    gw = GROUP_A * BLOCK
    kvw = N_KV_A * SLAB
    slope_rows, sink_rows = _swa_rows(sinks)

    def body(q_ref, qn_ref, do_ref, don_ref, l_ref, ln_ref, o_ref, on_ref, kp_ref, k_ref, vp_ref, v_ref,
             pkp_ref, pk_ref, pq_ref, pqn_ref, slope_ref, sink_ref, dqkv_ref, dsink_ref):
        j = pl.program_id(0)
        mask_cc, older = _swa_masks()

        @pl.when(j == 0)
        def _():
            dsink_ref[...] = jnp.zeros_like(dsink_ref)

        def tile(k, v, x, dox, lrow, drow, dist, mask, slope):
            s = jnp.where(mask, _dot(k, x) * SCALE_A - slope * dist, NEG)
            p = jnp.exp(s - lrow)
            ds = p * (_dot(v, dox) - drow)
            return p.astype(BF16), ds.astype(BF16)

        def queries(q, do, o, l, rows):
            out = []
            for g in range(N_KV_A):
                x, dox = _heads_beside(q, g, rows), _heads_beside(do, g, rows)
                drow = jnp.sum(dox.astype(F32) * _heads_beside(o, g, rows).astype(F32), axis=0, keepdims=True)
                out.append((x, dox, _rows_beside(l, g, rows), drow))
            return out

        cur_q = queries(q_ref, do_ref, o_ref, l_ref, slice(0, BLOCK))
        for s in range(sub):
            rows = slice(s * BLOCK, (s + 1) * BLOCK)
            if s == 0:
                k_p, v_p, before, pkp = kp_ref, vp_ref, slice(None), pkp_ref[...]
                mask_cp = jnp.logical_and(older, j > 0)
            else:
                before = slice((s - 1) * BLOCK, s * BLOCK)
                k_p, v_p, pkp, mask_cp = k_ref, v_ref, pk_ref[before, :], older
            if s == sub - 1:
                nxt_q = queries(qn_ref, don_ref, on_ref, ln_ref, slice(None))
                pqn, mask_nc = pqn_ref[...], jnp.logical_and(older, j < steps - 1)
            else:
                after = slice((s + 1) * BLOCK, (s + 2) * BLOCK)
                nxt_q = queries(q_ref, do_ref, o_ref, l_ref, after)
                pqn, mask_nc = pq_ref[:, after], older
            pkc, pq = pk_ref[rows, :], pq_ref[:, rows]
            dist_cc = _tile_group(jnp.abs(pkc - pq).astype(F32))
            dist_cp = _tile_group(jnp.abs(pkp - pq).astype(F32))
            dist_nc = _tile_group(jnp.abs(pkc - pqn).astype(F32))
            for g in range(N_KV_A):
                gs = slice(g * SLAB, (g + 1) * SLAB)
                kc, kp, vc, vp = k_ref[rows, gs], k_p[before, gs], v_ref[rows, gs], v_p[before, gs]
                slope, sink = slope_ref[g], sink_ref[g]
                x, dox, lrow, drow = cur_q[g]
                xn, doxn, lrown, drown = nxt_q[g]
                p_cc, ds_cc = tile(kc, vc, x, dox, lrow, drow, dist_cc, mask_cc, slope)
                _, ds_cp = tile(kp, vp, x, dox, lrow, drow, dist_cp, mask_cp, slope)
                p_nc, ds_nc = tile(kc, vc, xn, doxn, lrown, drown, dist_nc, mask_nc, slope)
                dqt = (_dot_tn(kc, ds_cc) + _dot_tn(kp, ds_cp)) * SCALE_A
                for hh in range(GROUP_A):
                    hd = g * GROUP_A + hh
                    dqkv_ref[rows, hd * SLAB:(hd + 1) * SLAB] = dqt[:, hh * BLOCK:(hh + 1) * BLOCK].T.astype(BF16)
                dqkv_ref[rows, HM + g * SLAB:HM + (g + 1) * SLAB] = (
                    (_dot_nt(ds_cc, x) + _dot_nt(ds_nc, xn)) * SCALE_A).astype(BF16)
                dqkv_ref[rows, HM + kvw + g * SLAB:HM + kvw + (g + 1) * SLAB] = (
                    _dot_nt(p_cc, dox) + _dot_nt(p_nc, doxn)).astype(BF16)
                dsink_ref[g] -= jnp.exp(sink - lrow) * drow
            cur_q = nxt_q

    cur = lambda j: (j, 0)
    prev = lambda j: (jnp.maximum(sub * j - 1, 0), 0)
    nxt = lambda j: (jnp.minimum(sub * (j + 1), nb - 1), 0)
    cur3 = lambda j: (0, 0, j)
    nxt3 = lambda j: (0, 0, jnp.minimum(sub * (j + 1), nb - 1))
    rows = pl.BlockSpec((N_KV_A, 1, gw), lambda j: (0, 0, 0))
    return pl.pallas_call(
        body, name="swa_bwd", grid=(steps,),
        in_specs=[pl.BlockSpec((tb, HM), cur), pl.BlockSpec((BLOCK, HM), nxt),
                  pl.BlockSpec((tb, HM), cur), pl.BlockSpec((BLOCK, HM), nxt),
                  pl.BlockSpec((N_HEADS, 1, tb), cur3), pl.BlockSpec((N_HEADS, 1, BLOCK), nxt3),
                  pl.BlockSpec((tb, HM), cur), pl.BlockSpec((BLOCK, HM), nxt),
                  pl.BlockSpec((BLOCK, kvw), prev), pl.BlockSpec((tb, kvw), cur),
                  pl.BlockSpec((BLOCK, kvw), prev), pl.BlockSpec((tb, kvw), cur),
                  pl.BlockSpec((BLOCK, 1), prev), pl.BlockSpec((tb, 1), cur),
                  pl.BlockSpec((1, tb), lambda j: (0, j)),
                  pl.BlockSpec((1, BLOCK), lambda j: (0, jnp.minimum(sub * (j + 1), nb - 1))), rows, rows],
        out_specs=[pl.BlockSpec((tb, HM + 2 * kvw), cur), rows],
        out_shape=[jax.ShapeDtypeStruct((t, HM + 2 * kvw), BF16), jax.ShapeDtypeStruct((N_KV_A, 1, gw), F32)],
        compiler_params=_params(("arbitrary",)),
    )(qa, qa, d_oa, d_oa, lse, lse, out_a, out_a, ka, ka, va, va,
      pos_col, pos_col, pos_row, pos_row, slope_rows, sink_rows)


def _mesh_pos():
    return lax.axis_index("x"), lax.axis_index("y"), lax.axis_index("c")


def _flip(v, bit):
    return 1 - v if bit else v


def _direct_copies(srcs, dsts, send_sems, recv_sems, local_sems, gather, sem_base=0, only=None):
    x, y, c = _mesh_pos()
    me = 4 * x + 2 * y + c

    def among(idx, dests):
        ok = idx == dests[0]
        for d in dests[1:]:
            ok = jnp.logical_or(ok, idx == d)
        return ok

    local, remote = [], []
    for a, (src, dst) in enumerate(zip(srcs, dsts)):
        dests = None if only is None else only[a]
        recv_ok = None if dests is None else among(me, dests)
        local.append((pltpu.make_async_copy(src if gather else src.at[me], dst.at[me],
                                            local_sems.at[sem_base + a]), recv_ok))
        for r in range(1, N_DEV):
            px, py, pc = _flip(x, r & 4), _flip(y, r & 2), _flip(c, r & 1)
            peer = 4 * px + 2 * py + pc
            sem = (N_DEV - 1) * (sem_base + a) + r - 1
            copy = pltpu.make_async_remote_copy(
                src_ref=src if gather else src.at[peer], dst_ref=dst.at[me],
                send_sem=send_sems.at[sem], recv_sem=recv_sems.at[sem],
                device_id=(px, py, pc), device_id_type=pl.DeviceIdType.MESH)
            remote.append((copy, None if dests is None else among(peer, dests), recv_ok))
    return local, remote


def _when(cond, fn):
    if cond is None:
        fn()
    else:
        pl.when(cond)(fn)


def _start_copies(local, remote):
    for cp, ok in local:
        _when(ok, cp.start)
    for cp, send_ok, _ in remote:
        _when(send_ok, cp.start)


def _wait_copies(local, remote):
    for cp, _, recv_ok in remote:
        _when(recv_ok, cp.wait_recv)
    for cp, send_ok, _ in remote:
        _when(send_ok, cp.wait_send)
    for cp, ok in local:
        _when(ok, cp.wait)


def _exchange_scratch(n):
    return [pltpu.SemaphoreType.DMA((n * (N_DEV - 1),)), pltpu.SemaphoreType.DMA((n * (N_DEV - 1),)),
            pltpu.SemaphoreType.DMA((n,))]


ANY_SPEC = pl.BlockSpec(memory_space=pl.ANY)


def _mla_fwd(qt, kb, vt, late):
    t = kb.shape[0]
    tk = _attn_tile(t)
    ratio = 2 if t >= 2 * tk else 1
    tq = ratio * tk
    nq = t // tq
    hps = MLA_FWD_HEADS_PER_STEP
    w = hps * SLAB
    pairs = [(i, j) for i in range(nq) for j in range(ratio * (i + 1))]
    i_tab = jnp.asarray(np.array([p[0] for p in pairs], np.int32))
    j_tab = jnp.asarray(np.array([p[1] for p in pairs], np.int32))

    n_late = len(late)

    def body(it_ref, jt_ref, qt_ref, k_ref, vt_ref, *rest):
        late_refs, (o_ref, ot_ref, qa_ref) = rest[:n_late], rest[n_late:n_late + 3]
        gathered_refs = rest[n_late + 3:2 * n_late + 3]
        m_s, acc_s, send_sems, recv_sems, local_sems = rest[2 * n_late + 3:]
        n = pl.program_id(1)
        i, j = it_ref[n], jt_ref[n]
        first_step = jnp.logical_and(pl.program_id(0) == 0, n == 0)
        last_step = jnp.logical_and(pl.program_id(0) == N_HEADS // hps - 1, n == len(pairs) - 1)

        @pl.when(first_step)
        def _():
            _start_copies(*_direct_copies(late_refs, gathered_refs, send_sems, recv_sems, local_sems, True))

        @pl.when(j == 0)
        def _():
            m_s[...] = jnp.full_like(m_s, NEG)
            acc_s[...] = jnp.zeros_like(acc_s)

        def update(masked, q0):
            qc = slice(q0, tq)

            def scores(hh):
                sl = slice(hh * SLAB, (hh + 1) * SLAB)
                return _dot(k_ref[:, sl], qt_ref[sl, qc])

            def softmax(hh, s):
                if masked:
                    s = jnp.where(lax.broadcasted_iota(jnp.int32, s.shape, 0)
                                  <= lax.broadcasted_iota(jnp.int32, s.shape, 1), s, NEG)
                m_old = m_s[hh][:, qc]
                m_new = jnp.maximum(m_old, jnp.max(s, axis=0, keepdims=True))
                m_s[hh, :, qc] = m_new
                return jnp.exp2(s - m_new).astype(BF16), jnp.exp2(m_old - m_new)

            def accumulate(hh, p, alpha):
                sl = slice(hh * SLAB, hh * SLAB + V_DIM_B + ONES_ROWS)
                acc_s[sl, qc] = alpha * acc_s[sl, qc] + _dot(vt_ref[sl, :], p)

            s_next, pending = scores(0), None
            for hh in range(hps):
                s = s_next
                if hh + 1 < hps:
                    s_next = scores(hh + 1)
                p, alpha = softmax(hh, s)
                if pending is not None:
                    accumulate(*pending)
                pending = (hh, p, alpha)
            accumulate(*pending)

        @pl.when(j < ratio * i)
        def _():
            update(False, 0)

        for part in range(ratio):
            @pl.when(j == ratio * i + part)
            def _():
                update(True, part * tk)

        @pl.when(j == ratio * i + ratio - 1)
        def _():
            for hh in range(hps):
                sl = slice(hh * SLAB, (hh + 1) * SLAB)
                den = acc_s[hh * SLAB + V_DIM_B:hh * SLAB + V_DIM_B + 1, :]
                values = lax.broadcasted_iota(jnp.int32, (SLAB, tq), 0) < V_DIM_B
                ot = jnp.where(values, acc_s[sl, :] / den, 0.0)
                ot_ref[sl, :] = ot.astype(BF16)
                o_ref[:, sl] = ot.T.astype(BF16)
                lse = m_s[hh] + jnp.log2(den)
                qa_ref[sl, :] = _plant_rows(qt_ref[sl, :].astype(F32), Q_HEAD_B, lse).astype(BF16)

        @pl.when(last_step)
        def _():
            _wait_copies(*_direct_copies(late_refs, gathered_refs, send_sems, recv_sems, local_sems, True))

    grid_spec = pltpu.PrefetchScalarGridSpec(
        num_scalar_prefetch=2, grid=(N_HEADS // hps, len(pairs)),
        in_specs=[pl.BlockSpec((w, tq), lambda h, n, it, jt: (h, it[n])),
                  pl.BlockSpec((tk, w), lambda h, n, it, jt: (jt[n], h)),
                  pl.BlockSpec((w, tk), lambda h, n, it, jt: (h, jt[n]))] + [ANY_SPEC] * n_late,
        out_specs=[pl.BlockSpec((tq, w), lambda h, n, it, jt: (it[n], h)),
                   pl.BlockSpec((w, tq), lambda h, n, it, jt: (h, it[n])),
                   pl.BlockSpec((w, tq), lambda h, n, it, jt: (h, it[n]))] + [ANY_SPEC] * n_late,
        scratch_shapes=[pltpu.VMEM((hps, 1, tq), F32), pltpu.VMEM((w, tq), F32)] + _exchange_scratch(n_late))
    outs = pl.pallas_call(
        body, name="mla_fwd", grid_spec=grid_spec,
        out_shape=[jax.ShapeDtypeStruct((t, HM), BF16), jax.ShapeDtypeStruct((HM, t), BF16),
                   jax.ShapeDtypeStruct((HM, t), BF16)]
        + [jax.ShapeDtypeStruct((N_DEV,) + a.shape, a.dtype) for a in late],
        compiler_params=_params(("arbitrary", "arbitrary")),
    )(i_tab, j_tab, qt, kb, vt, *late)
    return outs[0], outs[1], outs[2], list(outs[3:])


def _mla_bwd(qt, kb, kt, vb, d_ob_t, grad_slices):
    t = kb.shape[0]
    tk = _attn_tile(t)
    ratio = 2 if t >= 2 * tk else 1
    tq = ratio * tk
    nk, nq = t // tk, t // tq
    hps = MLA_HEADS_PER_STEP
    w = hps * SLAB
    pairs = [(j, i) for j in range(nk) for i in range(j // ratio, nq)]
    j_tab = jnp.asarray(np.array([p[0] for p in pairs], np.int32))
    i_tab = jnp.asarray(np.array([p[1] for p in pairs], np.int32))

    n_ex = len(grad_slices)

    def body(jt_ref, it_ref, qt_ref, dot_ref, k_ref, kt_ref, v_ref, *rest):
        slice_refs, (dqt_ref, dkt_ref, dvt_ref) = rest[:n_ex], rest[n_ex:n_ex + 3]
        part_refs = rest[n_ex + 3:2 * n_ex + 3]
        dk_s, dv_s, send_sems, recv_sems, local_sems = rest[2 * n_ex + 3:]
        n = pl.program_id(1)
        j, i = jt_ref[n], it_ref[n]
        first_step = jnp.logical_and(pl.program_id(0) == 0, n == 0)
        last_step = jnp.logical_and(pl.program_id(0) == N_HEADS // hps - 1, n == len(pairs) - 1)

        @pl.when(first_step)
        def _():
            _start_copies(*_direct_copies(slice_refs, part_refs, send_sems, recv_sems, local_sems, False))

        @pl.when(n == 0)
        def _():
            dqt_ref[...] = jnp.zeros_like(dqt_ref)

        def update(diagonal, q0):
            qc = slice(q0, tq)
            cols = pl.ds(pl.multiple_of(i * tq + q0, tk), tq - q0)

            def softmax_bwd(hh, s, dp):
                if diagonal:
                    s = jnp.where(lax.broadcasted_iota(jnp.int32, s.shape, 0)
                                  <= lax.broadcasted_iota(jnp.int32, s.shape, 1), s, NEG)
                p = jnp.exp2(s)
                return p.astype(BF16), (p * dp).astype(BF16)

            def gradients(hh, p, ds):
                base = hh * SLAB
                vrows = slice(base, base + V_DIM_B)
                qrows = slice(base, base + QK_NOPE + QK_ROPE)
                dv = _dot_nt(dot_ref[vrows, qc], p)
                dk = _dot_nt(qt_ref[qrows, qc], ds)
                if diagonal:
                    dv_s[base:base + SLAB, :] = jnp.concatenate([dv, jnp.zeros((SLAB - V_DIM_B, tk), F32)], axis=0)
                    dk_s[base:base + SLAB, :] = jnp.concatenate(
                        [dk, jnp.zeros((SLAB - QK_NOPE - QK_ROPE, tk), F32)], axis=0)
                else:
                    dv_s[vrows, :] += dv
                    dk_s[qrows, :] += dk
                dqt_ref[qrows, cols] += _dot(kt_ref[qrows, :], ds)

            def scores(hh):
                sl = slice(hh * SLAB, (hh + 1) * SLAB)
                return _dot(k_ref[:, sl], qt_ref[sl, qc])

            def dprod(hh):
                sl = slice(hh * SLAB, (hh + 1) * SLAB)
                return _dot(v_ref[:, sl], dot_ref[sl, qc])

            s_next = scores(0)
            for hh in range(hps):
                s = s_next
                dp = dprod(hh)
                if hh + 1 < hps:
                    s_next = scores(hh + 1)
                gradients(hh, *softmax_bwd(hh, s, dp))

        first_tile = lax.div(j, ratio)
        for part in range(ratio):
            @pl.when(jnp.logical_and(i == first_tile, lax.rem(j, ratio) == part))
            def _():
                update(True, part * tk)

        @pl.when(i > first_tile)
        def _():
            update(False, 0)

        @pl.when(i == nq - 1)
        def _():
            dkt_ref[...] = (dk_s[...] * (1.0 / LOG2E)).astype(BF16)
            dvt_ref[...] = dv_s[...].astype(BF16)

        @pl.when(last_step)
        def _():
            _wait_copies(*_direct_copies(slice_refs, part_refs, send_sems, recv_sems, local_sems, False))

    grid_spec = pltpu.PrefetchScalarGridSpec(
        num_scalar_prefetch=2, grid=(N_HEADS // hps, len(pairs)),
        in_specs=[pl.BlockSpec((w, tq), lambda h, n, jt, it: (h, it[n])),
                  pl.BlockSpec((w, tq), lambda h, n, jt, it: (h, it[n])),
                  pl.BlockSpec((tk, w), lambda h, n, jt, it: (jt[n], h)),
                  pl.BlockSpec((w, tk), lambda h, n, jt, it: (h, jt[n])),
                  pl.BlockSpec((tk, w), lambda h, n, jt, it: (jt[n], h))] + [ANY_SPEC] * n_ex,
        out_specs=[pl.BlockSpec((w, t), lambda h, n, jt, it: (h, 0)),
                   pl.BlockSpec((w, tk), lambda h, n, jt, it: (h, jt[n])),
                   pl.BlockSpec((w, tk), lambda h, n, jt, it: (h, jt[n]))] + [ANY_SPEC] * n_ex,
        scratch_shapes=[pltpu.VMEM((w, tk), F32), pltpu.VMEM((w, tk), F32)] + _exchange_scratch(n_ex))
    outs = pl.pallas_call(
        body, name="mla_bwd", grid_spec=grid_spec,
        out_shape=[jax.ShapeDtypeStruct((HM, t), F32), jax.ShapeDtypeStruct((HM, t), BF16),
                   jax.ShapeDtypeStruct((HM, t), BF16)]
        + [jax.ShapeDtypeStruct(a.shape, a.dtype) for a in grad_slices],
        compiler_params=_params(("arbitrary", "arbitrary")),
    )(j_tab, i_tab, qt, d_ob_t, kb, kt, vb, *grad_slices)
    return outs[0], outs[1], outs[2], list(outs[3:])


def _merge_fwd(out_a, out_b, gates, x, w_oa, w_ob, w_out, g2, g3):
    t = x.shape[0]
    tm = _wide_token_tile(t)

    def body(oa_ref, ob_ref, gates_ref, x_ref, woa_ref, wob_ref, wout_ref, g2_ref, g3_ref,
             oap_ref, obp_ref, merged_ref, y_ref, x1_ref, h2_ref):
        oa_p = _dot(oa_ref[...], woa_ref[...])
        ob_p = _dot(ob_ref[...], wob_ref[...])
        oap_ref[...] = oa_p.astype(BF16)
        obp_ref[...] = ob_p.astype(BF16)
        sa = _sigmoid(gates_ref[:, 0:D_MODEL].astype(F32))
        sb = _sigmoid(gates_ref[:, D_MODEL:2 * D_MODEL].astype(F32))
        merged = (sa * oa_p + sb * ob_p).astype(BF16)
        merged_ref[...] = merged
        y = _dot(merged, wout_ref[...])
        y_ref[...] = y
        x1 = x_ref[...] + y * _rms_r(y) * g2_ref[...]
        x1_ref[...] = x1
        h2_ref[...] = (x1 * _rms_r(x1) * g3_ref[...]).astype(BF16)

    def sds(dt):
        return jax.ShapeDtypeStruct((t, D_MODEL), dt)

    row = _row_spec(tm, D_MODEL)
    return pl.pallas_call(
        body, name="merge_fwd", grid=(t // tm,),
        in_specs=[_row_spec(tm, HM), _row_spec(tm, HM), _row_spec(tm, 2 * D_MODEL), row,
                  _full_spec((HM, D_MODEL)), _full_spec((HM, D_MODEL)), _full_spec((D_MODEL, D_MODEL)),
                  _full_spec((1, D_MODEL)), _full_spec((1, D_MODEL))],
        out_specs=[row] * 6,
        out_shape=[sds(BF16), sds(BF16), sds(BF16), sds(F32), sds(F32), sds(BF16)],
        compiler_params=_params(("parallel",)),
    )(out_a, out_b, gates, x, w_oa, w_ob, w_out, g2, g3)


def _merge_bwd(dx1, y, gates, oa_p, ob_p, out_a, out_b, out_b_t, merged, w_oa, w_ob, w_out, g2):
    t = dx1.shape[0]
    tm = _wide_token_tile(t)

    def body(dx1_ref, y_ref, gates_ref, oap_ref, obp_ref, oa_ref, ob_ref, obt_ref, merged_ref,
             woa_ref, wob_ref, wout_ref, g2_ref,
             dgates_ref, doa_ref, dobt_ref, dg2_ref, dwoa_ref, dwob_ref, dwout_ref):
        @pl.when(pl.program_id(0) == 0)
        def _():
            dwoa_ref[...] = jnp.zeros_like(dwoa_ref)
            dwob_ref[...] = jnp.zeros_like(dwob_ref)
            dwout_ref[...] = jnp.zeros_like(dwout_ref)

        dx1v = dx1_ref[...]
        yv = y_ref[...]
        r2 = _rms_r(yv)
        _acc_rows(dg2_ref, dx1v * yv * r2)
        dy = _rms_bwd(yv, r2, g2_ref[...], dx1v).astype(BF16)
        dwout_ref[...] += _dot_tn(merged_ref[...], dy)
        dm = _dot_nt(dy, wout_ref[...])
        sa = _sigmoid(gates_ref[:, 0:D_MODEL].astype(F32))
        sb = _sigmoid(gates_ref[:, D_MODEL:2 * D_MODEL].astype(F32))
        d_oap = (dm * sa).astype(BF16)
        d_obp = (dm * sb).astype(BF16)
        dwoa_ref[...] += _dot_tn(oa_ref[...], d_oap)
        dwob_ref[...] += _dot_tn(ob_ref[...], d_obp)
        dgates_ref[:, 0:D_MODEL] = (dm * oap_ref[...].astype(F32) * sa * (1.0 - sa)).astype(BF16)
        dgates_ref[:, D_MODEL:2 * D_MODEL] = (dm * obp_ref[...].astype(F32) * sb * (1.0 - sb)).astype(BF16)
        doa_ref[...] = _dot_nt(d_oap, woa_ref[...]).astype(BF16)
        d_ob_t = _dot_nt(wob_ref[...], d_obp)
        for hd in range(N_HEADS):
            sl = slice(hd * SLAB, (hd + 1) * SLAB)
            delta = jnp.sum(d_ob_t[sl, :] * obt_ref[sl, :].astype(F32), axis=0, keepdims=True)
            dobt_ref[sl, :] = _plant_rows(d_ob_t[sl, :], V_DIM_B, delta).astype(BF16)

    def sds(n, dt):
        return jax.ShapeDtypeStruct((t, n), dt)

    row = _row_spec(tm, D_MODEL)
    return pl.pallas_call(
        body, name="merge_bwd", grid=(t // tm,),
        in_specs=[row, row, _row_spec(tm, 2 * D_MODEL), row, row, _row_spec(tm, HM), _row_spec(tm, HM),
                  _col_spec(HM, tm), row,
                  _full_spec((HM, D_MODEL)), _full_spec((HM, D_MODEL)), _full_spec((D_MODEL, D_MODEL)),
                  _full_spec((1, D_MODEL))],
        out_specs=[_row_spec(tm, 2 * D_MODEL), _row_spec(tm, HM), _col_spec(HM, tm), _full_spec((1, D_MODEL)),
                   _full_spec((HM, D_MODEL)), _full_spec((HM, D_MODEL)), _full_spec((D_MODEL, D_MODEL))],
        out_shape=[sds(2 * D_MODEL, BF16), sds(HM, BF16), jax.ShapeDtypeStruct((HM, t), BF16),
                   jax.ShapeDtypeStruct((1, D_MODEL), F32),
                   jax.ShapeDtypeStruct((HM, D_MODEL), F32), jax.ShapeDtypeStruct((HM, D_MODEL), F32),
                   jax.ShapeDtypeStruct((D_MODEL, D_MODEL), F32)],
        compiler_params=_params(("arbitrary",), VMEM_LIMIT_MERGE_BWD),
    )(dx1, y, gates, oa_p, ob_p, out_a, out_b, out_b_t, merged, w_oa, w_ob, w_out, g2)


def _mlp_fwd_bwd(x1, h2, target, w_up, w_down, g3, g4):
    t = x1.shape[0]
    tm = _token_tile(t)
    fs = D_FF // N_DEV

    def body(x1_ref, h2_ref, tgt_ref, wup_ref, wdown_ref, g3_ref, g4_ref,
             a_ref, du_ref, dy2_ref, dx1_ref, loss_ref, dg3_ref, dg4_ref):
        x1v = x1_ref[...]
        h2v = h2_ref[...]
        u = jnp.concatenate([_dot(h2v, wup_ref[s]) for s in range(N_DEV)], axis=1)
        ru = jnp.maximum(u, 0.0)
        a = (ru * ru).astype(BF16)
        a_ref[...] = a
        y2 = _dot(a, wdown_ref[...])
        r4 = _rms_r(y2)
        diff = x1v + y2 * r4 * g4_ref[...] - tgt_ref[...]
        _acc_rows(loss_ref, jnp.sum(diff * diff, axis=-1, keepdims=True) * (0.5 / D_MODEL)
                  * jnp.ones((1, SLAB), F32))
        dx2 = diff * (1.0 / D_MODEL)
        _acc_rows(dg4_ref, dx2 * y2 * r4)
        dy2 = _rms_bwd(y2, r4, g4_ref[...], dx2).astype(BF16)
        dy2_ref[...] = dy2
        du = (_dot_nt(dy2, wdown_ref[...]) * (2.0 * ru)).astype(BF16)
        du_ref[...] = du
        dh2 = _dot_nt(du[:, 0:fs], wup_ref[0])
        for s in range(1, N_DEV):
            dh2 += _dot_nt(du[:, s * fs:(s + 1) * fs], wup_ref[s])
        r3 = _rms_r(x1v)
        _acc_rows(dg3_ref, dh2 * x1v * r3)
        dx1_ref[...] = dx2 + _rms_bwd(x1v, r3, g3_ref[...], dh2)

    row = _row_spec(tm, D_MODEL)
    frow = _row_spec(tm, D_FF)
    vec = _full_spec((1, D_MODEL))
    return pl.pallas_call(
        body, name="mlp_fwd_bwd", grid=(t // tm,),
        in_specs=[row, row, row, _full_spec((N_DEV, D_MODEL, fs)), _full_spec((D_FF, D_MODEL)), vec, vec],
        out_specs=[frow, frow, row, row, _full_spec((1, SLAB)), vec, vec],
        out_shape=[jax.ShapeDtypeStruct((t, D_FF), BF16), jax.ShapeDtypeStruct((t, D_FF), BF16),
                   jax.ShapeDtypeStruct((t, D_MODEL), BF16), jax.ShapeDtypeStruct((t, D_MODEL), F32),
                   jax.ShapeDtypeStruct((1, SLAB), F32), jax.ShapeDtypeStruct((1, D_MODEL), F32),
                   jax.ShapeDtypeStruct((1, D_MODEL), F32)],
        compiler_params=_params(("arbitrary",)),
    )(x1, h2, target, w_up, w_down, g3, g4)


def _latent_bwd(dqb_t, dkb_t, dvb_t, cq, ckv, cqn, ckvn, rope_ct, rope_s1t, rope_s2t, g_q, g_kv, w_qb, w_kvb):
    t = cq.shape[0]
    tm = min(t, 2 * _wide_token_tile(t))

    def body(dqt_ref, dkt_ref, dvt_ref, cq_ref, ckv_ref, cqn_ref, ckvn_ref, ct_ref, s1t_ref, s2t_ref,
             gq_ref, gkv_ref, wqb_ref, wkvb_ref,
             dlate_ref, dgq_ref, dgkv_ref, dwqb_ref, dwkvb_ref, dqbrt_ref, dkvbt_ref):
        @pl.when(pl.program_id(0) == 0)
        def _():
            dwqb_ref[...] = jnp.zeros_like(dwqb_ref)
            dwkvb_ref[...] = jnp.zeros_like(dwkvb_ref)

        ct, s1t, s2t = ct_ref[...], s1t_ref[...], s2t_ref[...]
        dk_sum_t = jnp.zeros((SLAB, tm), F32)
        for hd in range(N_HEADS):
            sl = slice(hd * SLAB, (hd + 1) * SLAB)
            dqbrt_ref[sl, :] = _rope_t_bwd(dqt_ref[sl, :] * SCALE_B, ct, s1t, s2t).astype(BF16)
            dk_sum_t += dkt_ref[sl, :].astype(F32)
        dkvbt_ref[0:HM, :] = dkt_ref[...]
        dkvbt_ref[HM:2 * HM, :] = dvt_ref[...]
        dkr = _rope_t_bwd(dk_sum_t, ct, s1t, s2t).T
        dwqb_ref[...] += _dot(dqbrt_ref[...], cqn_ref[...])
        dwkvb_ref[...] += _dot(dkvbt_ref[...], ckvn_ref[...])
        dcqn = _dot(wqb_ref[...], dqbrt_ref[...]).T
        cq = cq_ref[...]
        rq = _rms_r(cq)
        _acc_rows(dgq_ref, dcqn * cq * rq)
        dcq = _rms_bwd(cq, rq, gq_ref[...], dcqn)
        dckvn = _dot(wkvb_ref[...], dkvbt_ref[...]).T
        ckv = ckv_ref[...]
        rkv = _rms_r(ckv)
        _acc_rows(dgkv_ref, dckvn * ckv * rkv)
        dckv = _rms_bwd(ckv, rkv, gkv_ref[...], dckvn)
        dlate_ref[:, 0:C_CKV - C_CQ] = dcq.astype(BF16)
        dlate_ref[:, C_CKV - C_CQ:C_KR - C_CQ] = dckv.astype(BF16)
        dlate_ref[:, C_KR - C_CQ:D_IN_PAD - C_CQ] = dkr.astype(BF16)

    hmt = _col_spec(HM, tm)
    tab = _col_spec(SLAB, tm)
    return pl.pallas_call(
        body, name="latent_bwd", grid=(t // tm,),
        in_specs=[hmt, hmt, hmt,
                  _row_spec(tm, Q_LORA), _row_spec(tm, KV_LORA), _row_spec(tm, Q_LORA), _row_spec(tm, KV_LORA),
                  tab, tab, tab, _full_spec((1, Q_LORA)), _full_spec((1, KV_LORA)),
                  _full_spec((Q_LORA, HM)), _full_spec((KV_LORA, 2 * HM))],
        out_specs=[_row_spec(tm, D_IN_PAD - C_CQ), _full_spec((1, Q_LORA)), _full_spec((1, KV_LORA)),
                   _full_spec((HM, Q_LORA)), _full_spec((2 * HM, KV_LORA))],
        out_shape=[jax.ShapeDtypeStruct((t, D_IN_PAD - C_CQ), BF16),
                   jax.ShapeDtypeStruct((1, Q_LORA), F32), jax.ShapeDtypeStruct((1, KV_LORA), F32),
                   jax.ShapeDtypeStruct((HM, Q_LORA), F32), jax.ShapeDtypeStruct((2 * HM, KV_LORA), F32)],
        scratch_shapes=[pltpu.VMEM((HM, tm), BF16), pltpu.VMEM((2 * HM, tm), BF16)],
        compiler_params=_params(("arbitrary",)),
    )(dqb_t, dkb_t, dvb_t, cq, ckv, cqn, ckvn, rope_ct, rope_s1t, rope_s2t, g_q, g_kv, w_qb, w_kvb)


def _inproj_bwd(dgates, dqkv, dlate, x, dx1, g1, w_in, grad_slices, only, small):
    t = x.shape[0]
    tm = _wide_token_tile(t)
    n_ex = len(grad_slices)
    zeroed = [a for a in range(n_ex) if only[a] is not None]

    def body(dgates_ref, dqkv_ref, dlate_ref, x_ref, dx1_ref, g1_ref, win_ref, *rest):
        slice_refs, small_ref = rest[:n_ex], rest[n_ex]
        dx_ref = rest[n_ex + 1]
        part_refs = rest[n_ex + 2:2 * n_ex + 2]
        small_dst, dg1_dst = rest[2 * n_ex + 2:2 * n_ex + 4]
        dproj_ref, dg1_ref, send_sems, recv_sems, local_sems = rest[2 * n_ex + 4:2 * n_ex + 9]
        zero_refs, zero_sem = rest[2 * n_ex + 9:-1], rest[-1]
        sems = (send_sems, recv_sems, local_sems)

        def slice_copies():
            return _direct_copies(slice_refs, part_refs, *sems, False, only=only)

        def small_copies():
            return _direct_copies([small_ref], [small_dst], *sems, True, sem_base=n_ex)

        @pl.when(pl.program_id(0) == 0)
        def _():
            _start_copies(*slice_copies())
            _start_copies(*small_copies())
            x_, y_, c_ = _mesh_pos()
            me = 4 * x_ + 2 * y_ + c_
            for a, z_ref in zip(zeroed, zero_refs):
                outside = me != only[a][0]
                for d in only[a][1:]:
                    outside = jnp.logical_and(outside, me != d)

                @pl.when(outside)
                def _():
                    z_ref[...] = jnp.zeros_like(z_ref)
                    fills = [pltpu.make_async_copy(z_ref, part_refs[a].at[k], zero_sem.at[k])
                             for k in range(N_DEV)]
                    for cp in fills:
                        cp.start()
                    for cp in fills:
                        cp.wait()

        dproj_ref[:, C_GATES:C_QA] = dgates_ref[...]
        dproj_ref[:, C_QA:C_CQ] = dqkv_ref[...]
        dproj_ref[:, C_CQ:D_IN_PAD] = dlate_ref[...]
        dh = _dot_nt(dproj_ref[...], win_ref[...])
        xv = x_ref[...]
        r1 = _rms_r(xv)
        _acc_rows(dg1_ref, dh * xv * r1)
        dx_ref[...] = dx1_ref[...] + _rms_bwd(xv, r1, g1_ref[...], dh)

        @pl.when(pl.program_id(0) == t // tm - 1)
        def _():
            gain_copies = _direct_copies([dg1_ref], [dg1_dst], *sems, True, sem_base=n_ex + 1)
            _start_copies(*gain_copies)
            _wait_copies(*slice_copies())
            _wait_copies(*small_copies())
            _wait_copies(*gain_copies)

    kvw = N_KV_A * SLAB
    row = _row_spec(tm, D_MODEL)
    outs = pl.pallas_call(
        body, name="inproj_bwd", grid=(t // tm,),
        in_specs=[_row_spec(tm, 2 * D_MODEL), _row_spec(tm, HM + 2 * kvw), _row_spec(tm, D_IN_PAD - C_CQ),
                  row, row, _full_spec((1, D_MODEL)), _full_spec((D_MODEL, D_IN_PAD))]
        + [ANY_SPEC] * (n_ex + 1),
        out_specs=[row] + [ANY_SPEC] * (n_ex + 2),
        out_shape=[jax.ShapeDtypeStruct((t, D_MODEL), F32)]
        + [jax.ShapeDtypeStruct(a.shape, a.dtype) for a in grad_slices]
        + [jax.ShapeDtypeStruct((N_DEV,) + small.shape, F32), jax.ShapeDtypeStruct((N_DEV, 1, D_MODEL), F32)],
        scratch_shapes=[pltpu.VMEM((tm, D_IN_PAD), BF16), pltpu.VMEM((1, D_MODEL), F32)]
        + _exchange_scratch(n_ex + 2)
        + [pltpu.VMEM(grad_slices[a].shape[1:], grad_slices[a].dtype) for a in zeroed]
        + [pltpu.SemaphoreType.DMA((N_DEV,))],
        compiler_params=_params(("arbitrary",)),
    )(dgates, dqkv, dlate, x, dx1, g1, w_in, *grad_slices, small)
    return outs[0], list(outs[1:n_ex + 1]), outs[n_ex + 1], outs[n_ex + 2]


def _matmul_tn(a, b, name, out_dtype=F32, n_shards=1):
    t, k = a.shape
    n = b.shape[1]
    bn = min(n, MATMUL_COLS)
    bt = min(t, MATMUL_TOKENS)
    bk = min(k, MATMUL_ACC_ELEMS // bn)
    ns = n // n_shards
    per_block = bn // ns
    steps = t // bt

    def body(a_ref, b_ref, o_ref, acc):
        s = pl.program_id(2)

        @pl.when(s == 0)
        def _():
            acc[...] = jnp.zeros_like(acc)

        acc[...] += _dot_tn(a_ref[...], b_ref[...])

        @pl.when(s == steps - 1)
        def _():
            if n_shards > 1:
                for p in range(per_block):
                    o_ref[p] = acc[:, p * ns:(p + 1) * ns].astype(out_dtype)
            else:
                o_ref[...] = acc[...].astype(out_dtype)

    if n_shards > 1:
        out_spec = pl.BlockSpec((per_block, bk, ns), lambda i, j, s: (j, i, 0))
        out_shape = jax.ShapeDtypeStruct((n_shards, k, ns), out_dtype)
    else:
        out_spec = pl.BlockSpec((bk, bn), lambda i, j, s: (i, j))
        out_shape = jax.ShapeDtypeStruct((k, n), out_dtype)
    return pl.pallas_call(
        body, name=name, grid=(k // bk, n // bn, steps),
        in_specs=[pl.BlockSpec((bt, bk), lambda i, j, s: (s, i)), pl.BlockSpec((bt, bn), lambda i, j, s: (s, j))],
        out_specs=out_spec, out_shape=out_shape, scratch_shapes=[pltpu.VMEM((bk, bn), F32)],
        compiler_params=_params(("parallel", "parallel", "arbitrary")),
    )(a, b)


def _two_level_gather(srcs, dsts, send_sems, recv_sems, local_sems):
    n = len(srcs)
    x, y, c = _mesh_pos()
    me, sibling = (x, y, c), (x, y, 1 - c)
    chips = [(1 - x, y), (x, 1 - y), (1 - x, 1 - y)]

    def slot(a, px, py, pc):
        return dsts[a].at[4 * px + 2 * py + pc]

    def copy(a, k, block, to, src=None):
        return pltpu.make_async_remote_copy(
            src_ref=slot(a, *block) if src is None else src, dst_ref=slot(a, *block),
            send_sem=send_sems.at[(N_DEV - 1) * a + k], recv_sem=recv_sems.at[(N_DEV - 1) * a + k],
            device_id=to, device_id_type=pl.DeviceIdType.MESH)

    def own_copies():
        mine = [pltpu.make_async_copy(srcs[a], slot(a, *me), local_sems.at[a]) for a in range(n)]
        first = []
        for a in range(n):
            first.append(copy(a, 0, me, sibling, src=srcs[a]))
            first += [copy(a, 1 + j, me, (*chip, c), src=srcs[a]) for j, chip in enumerate(chips)]
        return mine, first

    def start():
        mine, first = own_copies()
        for cp in mine + first:
            cp.start()

    def finish():
        mine, first = own_copies()
        passed = []
        for j, chip in enumerate(chips):
            for a in range(n):
                copy(a, 1 + j, (*chip, c), me).wait_recv()
                passed.append(copy(a, 4 + j, (*chip, c), sibling))
                passed[-1].start()
        for a in range(n):
            copy(a, 0, sibling, me).wait_recv()
        for j, chip in enumerate(chips):
            for a in range(n):
                copy(a, 4 + j, (*chip, 1 - c), me).wait_recv()
        for cp in first + passed:
            cp.wait_send()
        for cp in mine:
            cp.wait()

    return start, finish


def _adamw(parts, w, m, v, name):
    n_parts = len(parts)
    _, k, n = parts[0].shape
    bk = min(k, ADAM_ROWS)
    c1 = 1.0 - ADAM_B1 ** ADAM_STEP
    c2 = 1.0 - ADAM_B2 ** ADAM_STEP

    def body(*refs):
        p_refs, (w_ref, m_ref, v_ref, g_ref, d_ref, mo_ref, vo_ref) = refs[:n_parts], refs[n_parts:]
        g = p_refs[0][0].astype(F32)
        for p_ref in p_refs:
            for s in range(N_DEV):
                if p_ref is not p_refs[0] or s > 0:
                    g = g + p_ref[s].astype(F32)
        g_ref[0] = g
        m_new = ADAM_B1 * m_ref[0] + (1.0 - ADAM_B1) * g
        v_new = ADAM_B2 * v_ref[0] + (1.0 - ADAM_B2) * (g * g)
        mo_ref[0] = m_new
        vo_ref[0] = v_new
        m_hat = m_new / c1
        v_hat = v_new / c2
        d_ref[0] = -ADAM_LR * (m_hat / (jnp.sqrt(v_hat) + ADAM_EPS) + ADAM_WD * w_ref[0])

    blk = pl.BlockSpec((1, bk, n), lambda i: (0, i, 0))
    out = jax.ShapeDtypeStruct((1, k, n), F32)
    return pl.pallas_call(
        body, name=name, grid=(k // bk,),
        in_specs=[pl.BlockSpec((N_DEV, bk, n), lambda i: (0, i, 0))] * n_parts + [blk, blk, blk],
        out_specs=[blk] * 4, out_shape=[out] * 4,
        compiler_params=_params(("parallel",)),
    )(*parts, w, m, v)


def _adamw_small(parts, w, m, v):
    k = len(SMALL_LAYOUT)
    c1 = 1.0 - ADAM_B1 ** ADAM_STEP
    c2 = 1.0 - ADAM_B2 ** ADAM_STEP

    def body(p_ref, *refs):
        w_refs, m_refs, v_refs, outs = refs[:k], refs[k:2 * k], refs[2 * k:3 * k], refs[3 * k:]
        total = p_ref[0]
        for s in range(1, N_DEV):
            total = total + p_ref[s]
        for i, (_, row, off, width) in enumerate(SMALL_LAYOUT):
            g = total[row:row + 1, off:off + width]
            m_new = ADAM_B1 * m_refs[i][...] + (1.0 - ADAM_B1) * g
            v_new = ADAM_B2 * v_refs[i][...] + (1.0 - ADAM_B2) * (g * g)
            outs[4 * i][...] = g
            outs[4 * i + 1][...] = -ADAM_LR * ((m_new / c1) / (jnp.sqrt(v_new / c2) + ADAM_EPS)
                                               + ADAM_WD * w_refs[i][...])
            outs[4 * i + 2][...] = m_new
            outs[4 * i + 3][...] = v_new
        outs[4 * k][...] = total[SMALL_LOSS_ROW:SMALL_LOSS_ROW + 1, SMALL_LOSS_OFF:SMALL_LOSS_OFF + 1]

    names = [name for name, *_ in SMALL_LAYOUT]
    out_shape = [jax.ShapeDtypeStruct(w[name].shape, F32) for name in names for _ in range(4)]
    outs = pl.pallas_call(
        body, name="adamw_small", out_shape=out_shape + [jax.ShapeDtypeStruct((1, 1), F32)],
    )(parts, *[w[n] for n in names], *[m[n] for n in names], *[v[n] for n in names])
    return {name: tuple(outs[4 * i:4 * i + 4]) for i, name in enumerate(names)}, outs[4 * k]


def _pad_heads_cols(w, heads, width):
    k = w.shape[0]
    w = w.reshape(k, heads, width)
    return jnp.pad(w, ((0, 0), (0, 0), (0, SLAB - width))).reshape(k, heads * SLAB)


def _unpad_heads_cols(w, heads, width):
    k = w.shape[0]
    return w.reshape(k, heads, SLAB)[:, :, :width].reshape(k, heads * width)


def _pad_heads_rows(w, heads, width):
    n = w.shape[1]
    w = w.reshape(heads, width, n)
    return jnp.pad(w, ((0, 0), (0, SLAB - width), (0, 0))).reshape(heads * SLAB, n)


def _unpad_heads_rows(w, heads, width):
    n = w.shape[1]
    return w.reshape(heads, SLAB, n)[:, :width, :].reshape(heads * width, n)


def _pad_w_in(w_in):
    o = 2 * D_MODEL
    qa = _pad_heads_cols(w_in[:, o:o + 512], N_HEADS, HEAD_A)
    ka = _pad_heads_cols(w_in[:, o + 512:o + 640], N_KV_A, HEAD_A)
    va = _pad_heads_cols(w_in[:, o + 640:o + 768], N_KV_A, HEAD_A)
    kr = jnp.pad(w_in[:, o + 1152:o + 1184], ((0, 0), (QK_NOPE, SLAB - QK_NOPE - QK_ROPE)))
    return jnp.concatenate([w_in[:, :o], qa, ka, va, w_in[:, o + 768:o + 1152], kr], axis=1)


def _unpad_w_in(w):
    qa = _unpad_heads_cols(w[:, C_QA:C_KA], N_HEADS, HEAD_A)
    ka = _unpad_heads_cols(w[:, C_KA:C_VA], N_KV_A, HEAD_A)
    va = _unpad_heads_cols(w[:, C_VA:C_CQ], N_KV_A, HEAD_A)
    kr = w[:, C_KR + QK_NOPE:C_KR + QK_NOPE + QK_ROPE]
    return jnp.concatenate([w[:, :C_QA], qa, ka, va, w[:, C_CQ:C_KR], kr], axis=1)


def _pad_w_kvb(w_kvb):
    w = w_kvb.reshape(KV_LORA, N_HEADS, QK_NOPE + V_DIM_B)
    k = jnp.pad(w[:, :, :QK_NOPE], ((0, 0), (0, 0), (0, SLAB - QK_NOPE))).reshape(KV_LORA, HM)
    v = jnp.pad(w[:, :, QK_NOPE:], ((0, 0), (0, 0), (0, SLAB - V_DIM_B))).reshape(KV_LORA, HM)
    return jnp.concatenate([k, v], axis=1)


def _unpad_w_kvb(w):
    k = w[:, :HM].reshape(KV_LORA, N_HEADS, SLAB)[:, :, :QK_NOPE]
    v = w[:, HM:].reshape(KV_LORA, N_HEADS, SLAB)[:, :, :V_DIM_B]
    return jnp.concatenate([k, v], axis=2).reshape(KV_LORA, N_HEADS * (QK_NOPE + V_DIM_B))


def _col_shards(w):
    k, n = w.shape
    ns = n // N_DEV
    if ns % SLAB:
        return jnp.stack([w[:, d * ns:(d + 1) * ns] for d in range(N_DEV)])
    return w.reshape(k, N_DEV, ns).transpose(1, 0, 2)


def _from_col_shards(s):
    _, k, ns = s.shape
    if ns % SLAB:
        return jnp.concatenate([s[d] for d in range(N_DEV)], axis=1)
    return s.transpose(1, 0, 2).reshape(k, N_DEV * ns)


def _freq_row():
    freqs = ROPE_THETA ** (-jnp.arange(0, QK_ROPE, 2, dtype=F32) / QK_ROPE)
    return jnp.concatenate([jnp.zeros((QK_NOPE,), F32), freqs, freqs,
                            jnp.zeros((SLAB - QK_NOPE - QK_ROPE,), F32)]).reshape(1, SLAB)


SMALL_D_ROWS = ("pre_norm_mix", "post_norm_mix", "pre_norm_mlp", "post_norm_mlp")
SMALL_LAYOUT = tuple((name, i, 0, D_MODEL) for i, name in enumerate(SMALL_D_ROWS)) + (
    ("q_a_norm", 4, 0, Q_LORA), ("kv_a_norm", 4, 256, KV_LORA), ("sinks", 4, 384, N_HEADS))
SMALL_LOSS_ROW, SMALL_LOSS_OFF = 4, 512


def _pack_small(vals):
    row4 = jnp.concatenate([vals["q_a_norm"].reshape(-1), vals["kv_a_norm"].reshape(-1), vals["sinks"].reshape(-1),
                            jnp.zeros((SMALL_LOSS_OFF - 392,), F32), vals["loss"].reshape(-1),
                            jnp.zeros((1024 - SMALL_LOSS_OFF - 1,), F32)])
    rows = [vals[n].reshape(1024) for n in SMALL_D_ROWS] + [row4]
    return jnp.concatenate([jnp.stack(rows), jnp.zeros((SMALL_ROWS - 5, 1024), F32)], axis=0)


WEIGHT_ORDER = ("pre_norm_mix", "w_in", "q_a_norm", "w_q_b", "kv_a_norm", "w_kv_b", "sinks", "w_o_a", "w_o_b",
                "w_out", "post_norm_mix", "pre_norm_mlp", "w_up", "w_down", "post_norm_mlp")


def kernel(x, positions, pre_norm_mix, w_in, q_a_norm, w_q_b, kv_a_norm, w_kv_b, sinks, w_o_a, w_o_b, w_out, post_norm_mix, pre_norm_mlp, w_up, w_down, post_norm_mlp, loss_target, m_pre_norm_mix, m_w_in, m_q_a_norm, m_w_q_b, m_kv_a_norm, m_w_kv_b, m_sinks, m_w_o_a, m_w_o_b, m_w_out, m_post_norm_mix, m_pre_norm_mlp, m_w_up, m_w_down, m_post_norm_mlp, v_pre_norm_mix, v_w_in, v_q_a_norm, v_w_q_b, v_kv_a_norm, v_w_kv_b, v_sinks, v_w_o_a, v_w_o_b, v_w_out, v_post_norm_mix, v_pre_norm_mlp, v_w_up, v_w_down, v_post_norm_mlp):
    weights = dict(pre_norm_mix=pre_norm_mix, w_in=w_in, q_a_norm=q_a_norm, w_q_b=w_q_b, kv_a_norm=kv_a_norm,
                   w_kv_b=w_kv_b, sinks=sinks, w_o_a=w_o_a, w_o_b=w_o_b, w_out=w_out, post_norm_mix=post_norm_mix,
                   pre_norm_mlp=pre_norm_mlp, w_up=w_up, w_down=w_down, post_norm_mlp=post_norm_mlp)
    m_in = dict(pre_norm_mix=m_pre_norm_mix, w_in=m_w_in, q_a_norm=m_q_a_norm, w_q_b=m_w_q_b, kv_a_norm=m_kv_a_norm,
                w_kv_b=m_w_kv_b, sinks=m_sinks, w_o_a=m_w_o_a, w_o_b=m_w_o_b, w_out=m_w_out,
                post_norm_mix=m_post_norm_mix, pre_norm_mlp=m_pre_norm_mlp, w_up=m_w_up, w_down=m_w_down,
                post_norm_mlp=m_post_norm_mlp)
    v_in = dict(pre_norm_mix=v_pre_norm_mix, w_in=v_w_in, q_a_norm=v_q_a_norm, w_q_b=v_w_q_b, kv_a_norm=v_kv_a_norm,
                w_kv_b=v_w_kv_b, sinks=v_sinks, w_o_a=v_w_o_a, w_o_b=v_w_o_b, w_out=v_w_out,
                post_norm_mix=v_post_norm_mix, pre_norm_mlp=v_pre_norm_mlp, w_up=v_w_up, w_down=v_w_down,
                post_norm_mlp=v_post_norm_mlp)

    xs, pos, target = x[0], positions[0], loss_target[0]
    t = xs.shape[0]
    pos_col = pos.reshape(t, 1)
    pos_row = pos.reshape(1, t)
    g1, g2, g3, g4 = (weights[n] for n in SMALL_D_ROWS)
    g_q, g_kv = q_a_norm, kv_a_norm
    sink_vec = sinks.reshape(N_HEADS)
    shard = {n: weights[n][0].astype(BF16) for n in EARLY + LATE}

    tables, (e_in, e_qb, e_kvb) = _rope_tables(pos_col, _freq_row(), [shard[n] for n in EARLY])
    w_in_p = _pad_w_in(_from_col_shards(e_in))
    w_qb = _pad_heads_cols(_from_col_shards(e_qb), N_HEADS, QK_NOPE + QK_ROPE)
    w_kvb = _pad_w_kvb(_from_col_shards(e_kvb))

    (h, gates, qa, ka, va, cq, ckv, cqn, ckvn, kb, vb, qt, kt, vt) = _inproj_fwd(
        xs, g1, w_in_p, g_q, g_kv, w_kvb, w_qb.T, w_kvb[:, :HM].T, w_kvb[:, HM:].T, w_in_p[:, C_KR:].T, tables)
    out_a, lse_a = _swa_fwd(qa, ka, va, pos_col, pos_row, sink_vec)
    out_b, out_b_t, qt_lse, (l_oa, l_ob, l_out, w_up_s, l_down) = _mla_fwd(qt, kb, vt, [shard[n] for n in LATE])
    w_oa = _pad_heads_rows(_from_col_shards(l_oa), N_HEADS, HEAD_A)
    w_ob = _pad_heads_rows(_from_col_shards(l_ob), N_HEADS, V_DIM_B)
    w_out_f = l_out.reshape(D_MODEL, D_MODEL)
    w_down_f = l_down.reshape(D_FF, D_MODEL)

    oa_p, ob_p, merged, y, x1, h2 = _merge_fwd(out_a, out_b, gates, xs, w_oa, w_ob, w_out_f, g2, g3)
    a, du, dy2, dx1, loss, dg3, dg4 = _mlp_fwd_bwd(x1, h2, target, w_up_s, w_down_f, g3, g4)
    (dgates, d_oa, d_ob_t, dg2, dw_oa, dw_ob, dw_out) = _merge_bwd(
        dx1, y, gates, oa_p, ob_p, out_a, out_b, out_b_t, merged, w_oa, w_ob, w_out_f, g2)
    late_slices = [
        _col_shards(_unpad_heads_rows(dw_oa, N_HEADS, HEAD_A)).astype(BF16),
        _col_shards(_unpad_heads_rows(dw_ob, N_HEADS, V_DIM_B)).astype(BF16),
        dw_out.astype(BF16).reshape(N_DEV, D_MODEL // N_DEV, D_MODEL),
        _matmul_tn(h2, du, "dw_up", BF16, N_DEV),
        _matmul_tn(a, dy2, "dw_down", BF16).reshape(N_DEV, D_FF // N_DEV, D_MODEL),
    ]
    dqkv_a, dsink = _swa_bwd(qa, ka, va, out_a, d_oa, lse_a, pos_col, pos_row, sink_vec)
    dw_in_early = jnp.concatenate([_matmul_tn(h, dgates, "dw_in_gates", BF16),
                                   _matmul_tn(h, dqkv_a, "dw_in_mixer_a", BF16),
                                   jnp.zeros((D_MODEL, D_IN_PAD - C_CQ), BF16)], axis=1)
    late_slices.append(_col_shards(_unpad_w_in(dw_in_early)))
    dqb_t, dkb_t, dvb_t, late_parts = _mla_bwd(qt_lse, kb, kt, vb, d_ob_t, late_slices)
    w_in_early_parts = late_parts.pop()
    dproj_late, dgq, dgkv, dw_qb_t, dw_kvb_t = _latent_bwd(
        dqb_t, dkb_t, dvb_t, cq, ckv, cqn, ckvn, *tables[3:], g_q, g_kv, w_qb, w_kvb)
    dw_l = _matmul_tn(h, dproj_late, "dw_in_latents", BF16)
    late_cols = jnp.concatenate([dw_l[:, :Q_LORA + KV_LORA], dw_l[:, C_KR - C_CQ + QK_NOPE:C_KR - C_CQ + Q_HEAD_B]],
                                axis=1)
    shard_cols = w_in.shape[2]
    head = late_cols.shape[1] - shard_cols
    w_in_late = jnp.concatenate([
        jnp.zeros((N_DEV - 2, D_MODEL, shard_cols), BF16),
        jnp.pad(late_cols[:, :head], ((0, 0), (shard_cols - head, 0)))[None], late_cols[:, head:][None]])
    early_slices = [
        w_in_late,
        _col_shards(_unpad_heads_cols(dw_qb_t.T, N_HEADS, QK_NOPE + QK_ROPE)).astype(BF16),
        _col_shards(_unpad_w_kvb(dw_kvb_t.T)).astype(BF16),
    ]
    small_grads = {"pre_norm_mix": jnp.zeros((1, D_MODEL), F32), "post_norm_mix": dg2, "pre_norm_mlp": dg3,
                   "post_norm_mlp": dg4, "q_a_norm": dgq, "kv_a_norm": dgkv,
                   "sinks": dsink.reshape(N_HEADS, BLOCK).sum(axis=1), "loss": loss[0, 0:1]}
    dx, early_parts, s_parts, dg1_parts = _inproj_bwd(
        dgates, dqkv_a, dproj_late, xs, dx1, g1, w_in_p, early_slices,
        only=[(N_DEV - 2, N_DEV - 1), None, None], small=_pack_small(small_grads))
    s_parts = s_parts.at[:, SMALL_D_ROWS.index("pre_norm_mix"), :].set(dg1_parts[:, 0, :])

    updates = {}
    all_parts = [[w_in_early_parts, early_parts[0]]] + [[p] for p in early_parts[1:] + late_parts]
    for name, parts in zip(EARLY + LATE, all_parts):
        outs = _adamw(parts, weights[name], m_in[name], v_in[name], "adamw_" + name)
        for kind, arr in zip(("g", "d", "m", "v"), outs):
            updates[kind, name] = arr
    small_out, loss_sum = _adamw_small(s_parts, weights, m_in, v_in)
    for name, outs in small_out.items():
        for kind, arr in zip(("g", "d", "m", "v"), outs):
            updates[kind, name] = arr
    results = [updates[kind, name] for kind in ("g", "d", "m", "v") for name in WEIGHT_ORDER]
    return (loss_sum.reshape(()), dx[None], *results)
```
